```python
import math
import jax, jax.numpy as jnp
from jax import lax
import numpy as np

D_MODEL = 1024
BATCH = 16
SEQ = 2048
DEPTH = 2

D_MIX = 1024
HEAD_DIM = 64
ATT_HEADS = 6
ATT_WIDTH = ATT_HEADS * HEAD_DIM
DILATED_PAIRS = ((128, 1), (512, 4), (2048, 16))
ATT_BLOCK = 128
SSD_HEADS = 6
SSD_HEADDIM = 64
SSD_WIDTH = SSD_HEADS * SSD_HEADDIM
SSD_GROUPS = 2
SSD_STATE = 128
SSD_CONV = 4
SSD_CHUNK = 128
SSD_CONV_DIM = SSD_WIDTH + 2 * SSD_GROUPS * SSD_STATE
SGU_GROUPS = 4
SGU_GROUP_DIM = 64
SGU_WIDTH = SGU_GROUPS * SGU_GROUP_DIM
SGU_CHUNK = 128
D_FF = 2816
D_IN = 3 * ATT_WIDTH + SSD_WIDTH + SSD_CONV_DIM + SSD_HEADS + 2 * SGU_WIDTH
RMS_EPS = 1e-6
LN_EPS = 1e-5

kernel_name = 'hybrid_dilated_ssd_sgu_macaron'


def _rmsnorm(x, g):
    xf = x.astype(jnp.float32)
    y = xf * lax.rsqrt(jnp.mean(xf * xf, axis=-1, keepdims=True) + RMS_EPS)
    return (y * g.astype(jnp.float32)).astype(x.dtype)


def _swiglu(x, w_gate, w_up, w_down):
    return (jax.nn.silu(x @ w_gate) * (x @ w_up)) @ w_down


def _to_blocks(t, dil, n_blocks):
    b, s, h, e = t.shape
    L = s // dil
    t = t.reshape(b, L, dil, h, e).transpose(0, 2, 1, 3, 4)
    t = jnp.pad(t, ((0, 0), (0, 0), (0, n_blocks * ATT_BLOCK - L), (0, 0), (0, 0)))
    return t.reshape(b, dil, n_blocks, ATT_BLOCK, h, e)


def _with_prev_block(t):
    prev = jnp.pad(t, ((0, 0), (0, 0), (1, 0), (0, 0), (0, 0), (0, 0)))[:, :, :-1]
    return jnp.concatenate([prev, t], axis=3)


def _dilated_branch(q, k, v, window, dil):
    b, s, h, e = q.shape
    L = s // dil
    nb = -(-L // ATT_BLOCK)
    span = window // dil
    qb = _to_blocks(q, dil, nb)
    kk = _with_prev_block(_to_blocks(k, dil, nb))
    vv = _with_prev_block(_to_blocks(v, dil, nb))
    scores = jnp.einsum('bdnqhe,bdnkhe->bdnhqk', qb, kk,
                        preferred_element_type=jnp.float32) * (e ** -0.5)
    qi = jnp.arange(ATT_BLOCK)[:, None]
    kj = jnp.arange(2 * ATT_BLOCK)[None, :]
    dist = qi + ATT_BLOCK - kj
    band = (dist >= 0) & (dist <= span)
    valid_key = (jnp.arange(nb)[:, None, None] > 0) | (kj[None] >= ATT_BLOCK)
    mask = (band[None] & valid_key)[:, None]
    scores = jnp.where(mask, scores, -jnp.inf)
    m = jnp.max(scores, axis=-1, keepdims=True)
    p = jnp.exp(scores - m)
    den = jnp.sum(p, axis=-1)
    o = jnp.einsum('bdnhqk,bdnkhe->bdnqhe', p.astype(v.dtype), vv)
    o = o / den.transpose(0, 1, 2, 4, 3)[..., None]
    lse = (m[..., 0] + jnp.log(den)).transpose(0, 1, 2, 4, 3)
    o = o.reshape(b, dil, nb * ATT_BLOCK, h, e)[:, :, :L].transpose(0, 2, 1, 3, 4).reshape(b, s, h, e)
    lse = lse.reshape(b, dil, nb * ATT_BLOCK, h)[:, :, :L].transpose(0, 2, 1, 3).reshape(b, s, h)
    return o, lse


def _dilated_attention(q, k, v):
    outs, lses = [], []
    for window, dil in DILATED_PAIRS:
        o, lse = _dilated_branch(q, k, v, window, dil)
        outs.append(o)
        lses.append(lse)
    w = jax.nn.softmax(jnp.stack(lses, axis=0), axis=0)
    out = w[0][..., None] * outs[0]
    for i in range(1, len(outs)):
        out = out + w[i][..., None] * outs[i]
    return out.astype(q.dtype)


def _ssd_mixer(z, xbc, dt_raw, conv_w, conv_b, dt_bias, a_log, d_skip, norm_g):
    b, s, _ = xbc.shape
    G, J, P, N = SSD_GROUPS, SSD_HEADS // SSD_GROUPS, SSD_HEADDIM, SSD_STATE
    c, l = s // SSD_CHUNK, SSD_CHUNK
    xbc = lax.conv_general_dilated(xbc, conv_w[:, None, :].astype(xbc.dtype), window_strides=(1,),
                                   padding=[(SSD_CONV - 1, 0)],
                                   dimension_numbers=('NWC', 'WIO', 'NWC'),
                                   feature_group_count=SSD_CONV_DIM)
    xbc = jax.nn.silu(xbc + conv_b)
    xs, bm, cm = jnp.split(xbc, [SSD_WIDTH, SSD_WIDTH + G * N], axis=-1)
    dt = jax.nn.softplus(dt_raw.astype(jnp.float32) + dt_bias.astype(jnp.float32))
    a = dt * (-jnp.exp(a_log.astype(jnp.float32)))
    xs_h = xs.astype(jnp.float32).reshape(b, s, SSD_HEADS, P)
    X = (xs_h * dt[..., None]).reshape(b, c, l, G, J, P)
    Bc = bm.astype(jnp.float32).reshape(b, c, l, G, N)
    Cc = cm.astype(jnp.float32).reshape(b, c, l, G, N)
    a_cs = jnp.cumsum(a.reshape(b, c, l, G, J), axis=2)
    causal = jnp.tril(jnp.ones((l, l), dtype=bool))[:, :, None, None]
    seg = a_cs[:, :, :, None] - a_cs[:, :, None, :]
    decay_in = jnp.exp(jnp.where(causal, seg, -jnp.inf))
    cb = jnp.einsum('bclgn,bcsgn->bclsg', Cc, Bc)
    y_diag = jnp.einsum('bclsg,bclsgj,bcsgjp->bclgjp', cb, decay_in, X)
    decay_to_end = jnp.exp(a_cs[:, :, -1:] - a_cs)
    chunk_states = jnp.einsum('bclgn,bclgj,bclgjp->bcgjpn', Bc, decay_to_end, X)
    chunk_decay = jnp.exp(a_cs[:, :, -1])

    def step(h, inp):
        st, dec = inp
        return h * dec[..., None, None] + st, h

    h0 = jnp.zeros((b, G, J, P, N), jnp.float32)
    _, prev = lax.scan(step, h0, (chunk_states.transpose(1, 0, 2, 3, 4, 5),
                                  chunk_decay.transpose(1, 0, 2, 3)))
    prev = prev.transpose(1, 0, 2, 3, 4, 5)
    y_off = jnp.einsum('bclgn,bcgjpn,bclgj->bclgjp', Cc, prev, jnp.exp(a_cs))
    y = (y_diag + y_off).reshape(b, s, SSD_HEADS, P) + d_skip.astype(jnp.float32)[:, None] * xs_h
    y = y.reshape(b, s, G, J * P) * jax.nn.silu(z.astype(jnp.float32)).reshape(b, s, G, J * P)
    y = y * lax.rsqrt(jnp.mean(y * y, axis=-1, keepdims=True) + RMS_EPS)
    y = y.reshape(b, s, SSD_WIDTH) * norm_g.astype(jnp.float32)
    return y.astype(z.dtype)


def _sgu_mixer(uv, ln_g, ln_b, w_s, b_s):
    b, s, _ = uv.shape
    uv = jax.nn.gelu(uv, approximate=False)
    u, v = jnp.split(uv, 2, axis=-1)
    vf = v.astype(jnp.float32)
    mu = jnp.mean(vf, axis=-1, keepdims=True)
    var = jnp.mean(jnp.square(vf - mu), axis=-1, keepdims=True)
    vn = (vf - mu) * lax.rsqrt(var + LN_EPS) * ln_g.astype(jnp.float32) + ln_b.astype(jnp.float32)
    vc = vn.reshape(b, s // SGU_CHUNK, SGU_CHUNK, SGU_GROUPS, SGU_GROUP_DIM)
    tri = jnp.tril(jnp.ones((SGU_CHUNK, SGU_CHUNK), dtype=bool))[None]
    w_causal = jnp.where(tri, w_s.astype(jnp.float32), 0.0)
    mixed = jnp.einsum('gts,bnsgc->bntgc', w_causal, vc) + b_s.astype(jnp.float32).T[:, :, None]
    return (u.astype(jnp.float32) * mixed.reshape(b, s, SGU_WIDTH)).astype(uv.dtype)


def _fwd_setup_inputs(seed: int = 0) -> dict:
    key = jax.random.key(seed)
    ks = jax.random.split(key, 32)
    f32 = jnp.float32
    nrm = lambda k, shape, scale: jax.random.normal(k, shape, f32) * scale
    gain = lambda k, shape: jnp.ones(shape, f32) + 0.02 * jax.random.normal(k, shape, f32)
    dt0 = jnp.exp(jax.random.uniform(ks[9], (DEPTH, SSD_HEADS), f32, math.log(1e-3), math.log(1e-1)))
    return {
        'x': jax.random.normal(ks[0], (BATCH, SEQ, D_MODEL), f32),
        'ffn1_norm': gain(ks[1], (DEPTH, D_MODEL)),
        'ffn1_w_gate': nrm(ks[2], (DEPTH, D_MODEL, D_FF), D_MODEL ** -0.5),
        'ffn1_w_up': nrm(ks[3], (DEPTH, D_MODEL, D_FF), D_MODEL ** -0.5),
        'ffn1_w_down': nrm(ks[4], (DEPTH, D_FF, D_MODEL), D_FF ** -0.5),
        'mix_norm': gain(ks[5], (DEPTH, D_MODEL)),
        'w_in': nrm(ks[6], (DEPTH, D_MODEL, D_IN), D_MODEL ** -0.5),
        'conv_w': nrm(ks[7], (DEPTH, SSD_CONV, SSD_CONV_DIM), SSD_CONV ** -0.5),
        'conv_b': nrm(ks[8], (DEPTH, SSD_CONV_DIM), 0.02),
        'dt_bias': dt0 + jnp.log(-jnp.expm1(-dt0)),
        'a_log': jnp.log(jax.random.uniform(ks[10], (DEPTH, SSD_HEADS), f32, 1.0, 16.0)),
        'd_skip': jnp.ones((DEPTH, SSD_HEADS), f32) + 0.1 * jax.random.normal(ks[11], (DEPTH, SSD_HEADS), f32),
        'ssd_norm': gain(ks[12], (DEPTH, SSD_WIDTH)),
        'sgu_ln_g': gain(ks[13], (DEPTH, SGU_WIDTH)),
        'sgu_ln_b': nrm(ks[14], (DEPTH, SGU_WIDTH), 0.02),
        'sgu_w': nrm(ks[15], (DEPTH, SGU_GROUPS, SGU_CHUNK, SGU_CHUNK), SGU_CHUNK ** -0.5),
        'sgu_b': jnp.ones((DEPTH, SGU_GROUPS, SGU_CHUNK), f32) + 0.1 * jax.random.normal(ks[16], (DEPTH, SGU_GROUPS, SGU_CHUNK), f32),
        'w_out': nrm(ks[17], (DEPTH, D_MIX, D_MODEL), D_MIX ** -0.5),
        'ffn2_norm': gain(ks[18], (DEPTH, D_MODEL)),
        'ffn2_w_gate': nrm(ks[19], (DEPTH, D_MODEL, D_FF), D_MODEL ** -0.5),
        'ffn2_w_up': nrm(ks[20], (DEPTH, D_MODEL, D_FF), D_MODEL ** -0.5),
        'ffn2_w_down': nrm(ks[21], (DEPTH, D_FF, D_MODEL), D_FF ** -0.5),
        'final_norm': gain(ks[22], (D_MODEL,)),
    }


def _fwd_reference(x, ffn1_norm, ffn1_w_gate, ffn1_w_up, ffn1_w_down, mix_norm, w_in, conv_w, conv_b,
              dt_bias, a_log, d_skip, ssd_norm, sgu_ln_g, sgu_ln_b, sgu_w, sgu_b, w_out,
              ffn2_norm, ffn2_w_gate, ffn2_w_up, ffn2_w_down, final_norm):
    b, s, _ = x.shape
    widths = [ATT_WIDTH, ATT_WIDTH, ATT_WIDTH, SSD_WIDTH, SSD_CONV_DIM, SSD_HEADS]
    offsets = []
    acc = 0
    for wdt in widths:
        acc += wdt
        offsets.append(acc)
    for i in range(DEPTH):
        x = x + 0.5 * _swiglu(_rmsnorm(x, ffn1_norm[i]), ffn1_w_gate[i], ffn1_w_up[i], ffn1_w_down[i])
        h = _rmsnorm(x, mix_norm[i])
        proj = h @ w_in[i]
        q, k, v, z, xbc, dt_raw, uv = jnp.split(proj, offsets, axis=-1)
        hs = (b, s, ATT_HEADS, HEAD_DIM)
        y_att = _dilated_attention(q.reshape(hs), k.reshape(hs), v.reshape(hs)).reshape(b, s, ATT_WIDTH)
        y_ssd = _ssd_mixer(z, xbc, dt_raw, conv_w[i], conv_b[i], dt_bias[i], a_log[i], d_skip[i], ssd_norm[i])
        y_sgu = _sgu_mixer(uv, sgu_ln_g[i], sgu_ln_b[i], sgu_w[i], sgu_b[i])
        x = x + jnp.concatenate([y_att, y_ssd, y_sgu], axis=-1) @ w_out[i]
        x = x + 0.5 * _swiglu(_rmsnorm(x, ffn2_norm[i]), ffn2_w_gate[i], ffn2_w_up[i], ffn2_w_down[i])
    return _rmsnorm(x, final_norm)


import jax as _jax
import jax.numpy as _jnp

TWIN_FORMAT = 'train_step'
FWD_PARAMS = ['x', 'ffn1_norm', 'ffn1_w_gate', 'ffn1_w_up', 'ffn1_w_down', 'mix_norm', 'w_in', 'conv_w', 'conv_b', 'dt_bias', 'a_log', 'd_skip', 'ssd_norm', 'sgu_ln_g', 'sgu_ln_b', 'sgu_w', 'sgu_b', 'w_out', 'ffn2_norm', 'ffn2_w_gate', 'ffn2_w_up', 'ffn2_w_down', 'final_norm']
TWIN_WEIGHTS = ['ffn1_norm', 'ffn1_w_gate', 'ffn1_w_up', 'ffn1_w_down', 'mix_norm', 'w_in', 'conv_w', 'conv_b', 'dt_bias', 'a_log', 'd_skip', 'ssd_norm', 'sgu_ln_g', 'sgu_ln_b', 'sgu_w', 'sgu_b', 'w_out', 'ffn2_norm', 'ffn2_w_gate', 'ffn2_w_up', 'ffn2_w_down', 'final_norm']
TWIN_DIFF_INPUT = 'x'
TWIN_INPUTS = ['x', 'ffn1_norm', 'ffn1_w_gate', 'ffn1_w_up', 'ffn1_w_down', 'mix_norm', 'w_in', 'conv_w', 'conv_b', 'dt_bias', 'a_log', 'd_skip', 'ssd_norm', 'sgu_ln_g', 'sgu_ln_b', 'sgu_w', 'sgu_b', 'w_out', 'ffn2_norm', 'ffn2_w_gate', 'ffn2_w_up', 'ffn2_w_down', 'final_norm', 'loss_target', 'm_ffn1_norm', 'm_ffn1_w_gate', 'm_ffn1_w_up', 'm_ffn1_w_down', 'm_mix_norm', 'm_w_in', 'm_conv_w', 'm_conv_b', 'm_dt_bias', 'm_a_log', 'm_d_skip', 'm_ssd_norm', 'm_sgu_ln_g', 'm_sgu_ln_b', 'm_sgu_w', 'm_sgu_b', 'm_w_out', 'm_ffn2_norm', 'm_ffn2_w_gate', 'm_ffn2_w_up', 'm_ffn2_w_down', 'm_final_norm', 'v_ffn1_norm', 'v_ffn1_w_gate', 'v_ffn1_w_up', 'v_ffn1_w_down', 'v_mix_norm', 'v_w_in', 'v_conv_w', 'v_conv_b', 'v_dt_bias', 'v_a_log', 'v_d_skip', 'v_ssd_norm', 'v_sgu_ln_g', 'v_sgu_ln_b', 'v_sgu_w', 'v_sgu_b', 'v_w_out', 'v_ffn2_norm', 'v_ffn2_w_gate', 'v_ffn2_w_up', 'v_ffn2_w_down', 'v_final_norm']
TWIN_OUTPUTS = ['loss', 'grad_x', 'grad_ffn1_norm', 'grad_ffn1_w_gate', 'grad_ffn1_w_up', 'grad_ffn1_w_down', 'grad_mix_norm', 'grad_w_in', 'grad_conv_w', 'grad_conv_b', 'grad_dt_bias', 'grad_a_log', 'grad_d_skip', 'grad_ssd_norm', 'grad_sgu_ln_g', 'grad_sgu_ln_b', 'grad_sgu_w', 'grad_sgu_b', 'grad_w_out', 'grad_ffn2_norm', 'grad_ffn2_w_gate', 'grad_ffn2_w_up', 'grad_ffn2_w_down', 'grad_final_norm', 'delta_ffn1_norm', 'delta_ffn1_w_gate', 'delta_ffn1_w_up', 'delta_ffn1_w_down', 'delta_mix_norm', 'delta_w_in', 'delta_conv_w', 'delta_conv_b', 'delta_dt_bias', 'delta_a_log', 'delta_d_skip', 'delta_ssd_norm', 'delta_sgu_ln_g', 'delta_sgu_ln_b', 'delta_sgu_w', 'delta_sgu_b', 'delta_w_out', 'delta_ffn2_norm', 'delta_ffn2_w_gate', 'delta_ffn2_w_up', 'delta_ffn2_w_down', 'delta_final_norm', 'new_m_ffn1_norm', 'new_m_ffn1_w_gate', 'new_m_ffn1_w_up', 'new_m_ffn1_w_down', 'new_m_mix_norm', 'new_m_w_in', 'new_m_conv_w', 'new_m_conv_b', 'new_m_dt_bias', 'new_m_a_log', 'new_m_d_skip', 'new_m_ssd_norm', 'new_m_sgu_ln_g', 'new_m_sgu_ln_b', 'new_m_sgu_w', 'new_m_sgu_b', 'new_m_w_out', 'new_m_ffn2_norm', 'new_m_ffn2_w_gate', 'new_m_ffn2_w_up', 'new_m_ffn2_w_down', 'new_m_final_norm', 'new_v_ffn1_norm', 'new_v_ffn1_w_gate', 'new_v_ffn1_w_up', 'new_v_ffn1_w_down', 'new_v_mix_norm', 'new_v_w_in', 'new_v_conv_w', 'new_v_conv_b', 'new_v_dt_bias', 'new_v_a_log', 'new_v_d_skip', 'new_v_ssd_norm', 'new_v_sgu_ln_g', 'new_v_sgu_ln_b', 'new_v_sgu_w', 'new_v_sgu_b', 'new_v_w_out', 'new_v_ffn2_norm', 'new_v_ffn2_w_gate', 'new_v_ffn2_w_up', 'new_v_ffn2_w_down', 'new_v_final_norm']
TWIN_LEAF_KINDS = {'loss': 'loss', 'grad_x': 'grad_x', 'grad_ffn1_norm': 'grad_w', 'grad_ffn1_w_gate': 'grad_w', 'grad_ffn1_w_up': 'grad_w', 'grad_ffn1_w_down': 'grad_w', 'grad_mix_norm': 'grad_w', 'grad_w_in': 'grad_w', 'grad_conv_w': 'grad_w', 'grad_conv_b': 'grad_w', 'grad_dt_bias': 'grad_w', 'grad_a_log': 'grad_w', 'grad_d_skip': 'grad_w', 'grad_ssd_norm': 'grad_w', 'grad_sgu_ln_g': 'grad_w', 'grad_sgu_ln_b': 'grad_w', 'grad_sgu_w': 'grad_w', 'grad_sgu_b': 'grad_w', 'grad_w_out': 'grad_w', 'grad_ffn2_norm': 'grad_w', 'grad_ffn2_w_gate': 'grad_w', 'grad_ffn2_w_up': 'grad_w', 'grad_ffn2_w_down': 'grad_w', 'grad_final_norm': 'grad_w', 'delta_ffn1_norm': 'delta_w', 'delta_ffn1_w_gate': 'delta_w', 'delta_ffn1_w_up': 'delta_w', 'delta_ffn1_w_down': 'delta_w', 'delta_mix_norm': 'delta_w', 'delta_w_in': 'delta_w', 'delta_conv_w': 'delta_w', 'delta_conv_b': 'delta_w', 'delta_dt_bias': 'delta_w', 'delta_a_log': 'delta_w', 'delta_d_skip': 'delta_w', 'delta_ssd_norm': 'delta_w', 'delta_sgu_ln_g': 'delta_w', 'delta_sgu_ln_b': 'delta_w', 'delta_sgu_w': 'delta_w', 'delta_sgu_b': 'delta_w', 'delta_w_out': 'delta_w', 'delta_ffn2_norm': 'delta_w', 'delta_ffn2_w_gate': 'delta_w', 'delta_ffn2_w_up': 'delta_w', 'delta_ffn2_w_down': 'delta_w', 'delta_final_norm': 'delta_w', 'new_m_ffn1_norm': 'new_m', 'new_m_ffn1_w_gate': 'new_m', 'new_m_ffn1_w_up': 'new_m', 'new_m_ffn1_w_down': 'new_m', 'new_m_mix_norm': 'new_m', 'new_m_w_in': 'new_m', 'new_m_conv_w': 'new_m', 'new_m_conv_b': 'new_m', 'new_m_dt_bias': 'new_m', 'new_m_a_log': 'new_m', 'new_m_d_skip': 'new_m', 'new_m_ssd_norm': 'new_m', 'new_m_sgu_ln_g': 'new_m', 'new_m_sgu_ln_b': 'new_m', 'new_m_sgu_w': 'new_m', 'new_m_sgu_b': 'new_m', 'new_m_w_out': 'new_m', 'new_m_ffn2_norm': 'new_m', 'new_m_ffn2_w_gate': 'new_m', 'new_m_ffn2_w_up': 'new_m', 'new_m_ffn2_w_down': 'new_m', 'new_m_final_norm': 'new_m', 'new_v_ffn1_norm': 'new_v', 'new_v_ffn1_w_gate': 'new_v', 'new_v_ffn1_w_up': 'new_v', 'new_v_ffn1_w_down': 'new_v', 'new_v_mix_norm': 'new_v', 'new_v_w_in': 'new_v', 'new_v_conv_w': 'new_v', 'new_v_conv_b': 'new_v', 'new_v_dt_bias': 'new_v', 'new_v_a_log': 'new_v', 'new_v_d_skip': 'new_v', 'new_v_ssd_norm': 'new_v', 'new_v_sgu_ln_g': 'new_v', 'new_v_sgu_ln_b': 'new_v', 'new_v_sgu_w': 'new_v', 'new_v_sgu_b': 'new_v', 'new_v_w_out': 'new_v', 'new_v_ffn2_norm': 'new_v', 'new_v_ffn2_w_gate': 'new_v', 'new_v_ffn2_w_up': 'new_v', 'new_v_ffn2_w_down': 'new_v', 'new_v_final_norm': 'new_v'}


def _forward(args):
    return _fwd_reference(*[args[k] for k in FWD_PARAMS])


def _output_shape():
    out = _jax.eval_shape(lambda: _forward(_fwd_setup_inputs(0)))
    return out.shape, out.dtype

N_MICROBATCH = 1
ADAM_LR = 0.001
ADAM_B1 = 0.9
ADAM_B2 = 0.999
ADAM_EPS = 1e-08
ADAM_WD = 0.01
ADAM_STEP = 10
PER_EXAMPLE_BATCH_AXIS = {'x': 0, 'loss_target': 0}
SHARED_INPUTS = []
_WEIGHT_DTYPES = {'ffn1_norm': _jnp.float32, 'ffn1_w_gate': _jnp.float32, 'ffn1_w_up': _jnp.float32, 'ffn1_w_down': _jnp.float32, 'mix_norm': _jnp.float32, 'w_in': _jnp.float32, 'conv_w': _jnp.float32, 'conv_b': _jnp.float32, 'dt_bias': _jnp.float32, 'a_log': _jnp.float32, 'd_skip': _jnp.float32, 'ssd_norm': _jnp.float32, 'sgu_ln_g': _jnp.float32, 'sgu_ln_b': _jnp.float32, 'sgu_w': _jnp.float32, 'sgu_b': _jnp.float32, 'w_out': _jnp.float32, 'ffn2_norm': _jnp.float32, 'ffn2_w_gate': _jnp.float32, 'ffn2_w_up': _jnp.float32, 'ffn2_w_down': _jnp.float32, 'final_norm': _jnp.float32}
MOMENT_SCALE = {'ffn1_norm': 8.673170e-02, 'ffn1_w_gate': 3.689775e-02, 'ffn1_w_up': 3.573422e-02, 'ffn1_w_down': 5.928289e-02, 'mix_norm': 1.515401e-01, 'w_in': 9.127920e-02, 'conv_w': 1.011174e-01, 'conv_b': 1.335760e-01, 'dt_bias': 2.706947e-01, 'a_log': 2.281585e-01, 'd_skip': 4.347749e-01, 'ssd_norm': 1.427856e-01, 'sgu_ln_g': 7.379029e-02, 'sgu_ln_b': 6.931040e-02, 'sgu_w': 4.625175e-02, 'sgu_b': 7.134524e-02, 'w_out': 1.076903e-01, 'ffn2_norm': 6.078898e-02, 'ffn2_w_gate': 2.552132e-02, 'ffn2_w_up': 2.480071e-02, 'ffn2_w_down': 4.111765e-02, 'final_norm': 3.196185e+01}


def _to_microbatches(a, axis):
    t = _jnp.moveaxis(a, axis, 0)
    t = t.reshape((N_MICROBATCH, t.shape[0] // N_MICROBATCH) + t.shape[1:])
    return _jnp.moveaxis(t, 1, axis + 1)


def setup_inputs(seed: int = 0) -> dict:
    inp = _fwd_setup_inputs(seed)
    key = _jax.random.fold_in(_jax.random.key(seed), 7919)
    shape, _ = _output_shape()
    out = dict(inp)
    out["loss_target"] = _jax.random.normal(_jax.random.fold_in(key, 0), shape, _jnp.float32)
    for i, name in enumerate(TWIN_WEIGHTS):
        w = inp[name].astype(_jnp.float32)
        if MOMENT_SCALE is None:
            s = _jnp.sqrt(_jnp.mean(_jnp.square(w)) + 1e-30)
        else:
            s = MOMENT_SCALE[name]
        km, kv = _jax.random.split(_jax.random.fold_in(key, i + 1))
        out[name] = w
        out["m_" + name] = s * _jax.random.normal(km, w.shape, _jnp.float32)
        out["v_" + name] = (s * s) * _jax.random.uniform(kv, w.shape, _jnp.float32, 0.5, 1.5)
    if N_MICROBATCH > 1:
        for name, axis in PER_EXAMPLE_BATCH_AXIS.items():
            out[name] = _to_microbatches(out[name], axis)
    return {'x': out['x'], 'ffn1_norm': out['ffn1_norm'], 'ffn1_w_gate': out['ffn1_w_gate'], 'ffn1_w_up': out['ffn1_w_up'], 'ffn1_w_down': out['ffn1_w_down'], 'mix_norm': out['mix_norm'], 'w_in': out['w_in'], 'conv_w': out['conv_w'], 'conv_b': out['conv_b'], 'dt_bias': out['dt_bias'], 'a_log': out['a_log'], 'd_skip': out['d_skip'], 'ssd_norm': out['ssd_norm'], 'sgu_ln_g': out['sgu_ln_g'], 'sgu_ln_b': out['sgu_ln_b'], 'sgu_w': out['sgu_w'], 'sgu_b': out['sgu_b'], 'w_out': out['w_out'], 'ffn2_norm': out['ffn2_norm'], 'ffn2_w_gate': out['ffn2_w_gate'], 'ffn2_w_up': out['ffn2_w_up'], 'ffn2_w_down': out['ffn2_w_down'], 'final_norm': out['final_norm'], 'loss_target': out['loss_target'], 'm_ffn1_norm': out['m_ffn1_norm'], 'm_ffn1_w_gate': out['m_ffn1_w_gate'], 'm_ffn1_w_up': out['m_ffn1_w_up'], 'm_ffn1_w_down': out['m_ffn1_w_down'], 'm_mix_norm': out['m_mix_norm'], 'm_w_in': out['m_w_in'], 'm_conv_w': out['m_conv_w'], 'm_conv_b': out['m_conv_b'], 'm_dt_bias': out['m_dt_bias'], 'm_a_log': out['m_a_log'], 'm_d_skip': out['m_d_skip'], 'm_ssd_norm': out['m_ssd_norm'], 'm_sgu_ln_g': out['m_sgu_ln_g'], 'm_sgu_ln_b': out['m_sgu_ln_b'], 'm_sgu_w': out['m_sgu_w'], 'm_sgu_b': out['m_sgu_b'], 'm_w_out': out['m_w_out'], 'm_ffn2_norm': out['m_ffn2_norm'], 'm_ffn2_w_gate': out['m_ffn2_w_gate'], 'm_ffn2_w_up': out['m_ffn2_w_up'], 'm_ffn2_w_down': out['m_ffn2_w_down'], 'm_final_norm': out['m_final_norm'], 'v_ffn1_norm': out['v_ffn1_norm'], 'v_ffn1_w_gate': out['v_ffn1_w_gate'], 'v_ffn1_w_up': out['v_ffn1_w_up'], 'v_ffn1_w_down': out['v_ffn1_w_down'], 'v_mix_norm': out['v_mix_norm'], 'v_w_in': out['v_w_in'], 'v_conv_w': out['v_conv_w'], 'v_conv_b': out['v_conv_b'], 'v_dt_bias': out['v_dt_bias'], 'v_a_log': out['v_a_log'], 'v_d_skip': out['v_d_skip'], 'v_ssd_norm': out['v_ssd_norm'], 'v_sgu_ln_g': out['v_sgu_ln_g'], 'v_sgu_ln_b': out['v_sgu_ln_b'], 'v_sgu_w': out['v_sgu_w'], 'v_sgu_b': out['v_sgu_b'], 'v_w_out': out['v_w_out'], 'v_ffn2_norm': out['v_ffn2_norm'], 'v_ffn2_w_gate': out['v_ffn2_w_gate'], 'v_ffn2_w_up': out['v_ffn2_w_up'], 'v_ffn2_w_down': out['v_ffn2_w_down'], 'v_final_norm': out['v_final_norm']}


def _loss(weights, diff, rest, loss_target):
    with _jax.named_scope("forward"):
        args = {**rest, TWIN_DIFF_INPUT: diff, **{k: w.astype(_WEIGHT_DTYPES[k]) for k, w in weights.items()}}
        y = _forward(args)
    with _jax.named_scope("loss_head"):
        err = _jnp.square(y.astype(_jnp.float32) - loss_target)
        return 0.5 * _jnp.sum(_jnp.mean(err, axis=-1)) if err.ndim else 0.5 * err


def _adamw(w, g, m, v):
    m = ADAM_B1 * m + (1.0 - ADAM_B1) * g
    v = ADAM_B2 * v + (1.0 - ADAM_B2) * _jnp.square(g)
    m_hat = m / (1.0 - ADAM_B1 ** ADAM_STEP)
    v_hat = v / (1.0 - ADAM_B2 ** ADAM_STEP)
    delta = -ADAM_LR * (m_hat / (_jnp.sqrt(v_hat) + ADAM_EPS) + ADAM_WD * w)
    return delta, m, v


def reference(x, ffn1_norm, ffn1_w_gate, ffn1_w_up, ffn1_w_down, mix_norm, w_in, conv_w, conv_b, dt_bias, a_log, d_skip, ssd_norm, sgu_ln_g, sgu_ln_b, sgu_w, sgu_b, w_out, ffn2_norm, ffn2_w_gate, ffn2_w_up, ffn2_w_down, final_norm, loss_target, m_ffn1_norm, m_ffn1_w_gate, m_ffn1_w_up, m_ffn1_w_down, m_mix_norm, m_w_in, m_conv_w, m_conv_b, m_dt_bias, m_a_log, m_d_skip, m_ssd_norm, m_sgu_ln_g, m_sgu_ln_b, m_sgu_w, m_sgu_b, m_w_out, m_ffn2_norm, m_ffn2_w_gate, m_ffn2_w_up, m_ffn2_w_down, m_final_norm, v_ffn1_norm, v_ffn1_w_gate, v_ffn1_w_up, v_ffn1_w_down, v_mix_norm, v_w_in, v_conv_w, v_conv_b, v_dt_bias, v_a_log, v_d_skip, v_ssd_norm, v_sgu_ln_g, v_sgu_ln_b, v_sgu_w, v_sgu_b, v_w_out, v_ffn2_norm, v_ffn2_w_gate, v_ffn2_w_up, v_ffn2_w_down, v_final_norm):
    given = dict(x=x, ffn1_norm=ffn1_norm, ffn1_w_gate=ffn1_w_gate, ffn1_w_up=ffn1_w_up, ffn1_w_down=ffn1_w_down, mix_norm=mix_norm, w_in=w_in, conv_w=conv_w, conv_b=conv_b, dt_bias=dt_bias, a_log=a_log, d_skip=d_skip, ssd_norm=ssd_norm, sgu_ln_g=sgu_ln_g, sgu_ln_b=sgu_ln_b, sgu_w=sgu_w, sgu_b=sgu_b, w_out=w_out, ffn2_norm=ffn2_norm, ffn2_w_gate=ffn2_w_gate, ffn2_w_up=ffn2_w_up, ffn2_w_down=ffn2_w_down, final_norm=final_norm, loss_target=loss_target, m_ffn1_norm=m_ffn1_norm, m_ffn1_w_gate=m_ffn1_w_gate, m_ffn1_w_up=m_ffn1_w_up, m_ffn1_w_down=m_ffn1_w_down, m_mix_norm=m_mix_norm, m_w_in=m_w_in, m_conv_w=m_conv_w, m_conv_b=m_conv_b, m_dt_bias=m_dt_bias, m_a_log=m_a_log, m_d_skip=m_d_skip, m_ssd_norm=m_ssd_norm, m_sgu_ln_g=m_sgu_ln_g, m_sgu_ln_b=m_sgu_ln_b, m_sgu_w=m_sgu_w, m_sgu_b=m_sgu_b, m_w_out=m_w_out, m_ffn2_norm=m_ffn2_norm, m_ffn2_w_gate=m_ffn2_w_gate, m_ffn2_w_up=m_ffn2_w_up, m_ffn2_w_down=m_ffn2_w_down, m_final_norm=m_final_norm, v_ffn1_norm=v_ffn1_norm, v_ffn1_w_gate=v_ffn1_w_gate, v_ffn1_w_up=v_ffn1_w_up, v_ffn1_w_down=v_ffn1_w_down, v_mix_norm=v_mix_norm, v_w_in=v_w_in, v_conv_w=v_conv_w, v_conv_b=v_conv_b, v_dt_bias=v_dt_bias, v_a_log=v_a_log, v_d_skip=v_d_skip, v_ssd_norm=v_ssd_norm, v_sgu_ln_g=v_sgu_ln_g, v_sgu_ln_b=v_sgu_ln_b, v_sgu_w=v_sgu_w, v_sgu_b=v_sgu_b, v_w_out=v_w_out, v_ffn2_norm=v_ffn2_norm, v_ffn2_w_gate=v_ffn2_w_gate, v_ffn2_w_up=v_ffn2_w_up, v_ffn2_w_down=v_ffn2_w_down, v_final_norm=v_final_norm)
    weights = {n: given[n] for n in TWIN_WEIGHTS}
    shared = {n: given[n] for n in SHARED_INPUTS}
    per_example = {n: given[n] for n in ['x']}
    grad_fn = _jax.value_and_grad(_loss, argnums=(0, 1))

    def one_microbatch(ex, loss_target):
        ex = dict(ex)
        diff = ex.pop(TWIN_DIFF_INPUT)
        return grad_fn(weights, diff, {**shared, **ex}, loss_target)

    if N_MICROBATCH == 1:
        loss, (grad_w, grad_x) = one_microbatch(per_example, given["loss_target"])
    else:
        def body(carry, xs):
            loss_sum, grad_sum = carry
            l_k, (gw_k, gx_k) = one_microbatch(xs[0], xs[1])
            with _jax.named_scope("update"):
                return (loss_sum + l_k, _jax.tree.map(_jnp.add, grad_sum, gw_k)), gx_k

        init = (_jnp.zeros((), _jnp.float32), _jax.tree.map(_jnp.zeros_like, weights))
        (loss, grad_w), grad_x = _jax.lax.scan(body, init, (per_example, given["loss_target"]))
    with _jax.named_scope("update"):
        delta_w, new_m, new_v = {}, {}, {}
        for n in TWIN_WEIGHTS:
            delta_w[n], new_m[n], new_v[n] = _adamw(weights[n], grad_w[n], given["m_" + n], given["v_" + n])
    return (loss, grad_x, *[grad_w[n] for n in TWIN_WEIGHTS], *[delta_w[n] for n in TWIN_WEIGHTS],
            *[new_m[n] for n in TWIN_WEIGHTS], *[new_v[n] for n in TWIN_WEIGHTS])
```

```python
import functools
import math

import jax
import jax.numpy as jnp
from jax import lax
from jax.experimental import pallas as pl
from jax.experimental.pallas import tpu as pltpu

F32 = jnp.float32
BF = jnp.bfloat16

RMS_EPS = 1e-6
LN_EPS = 1e-5
SEQ = 2048
CHUNK = 128
N_CHUNK = SEQ // CHUNK
ATT_W = 384
HEAD = 64
SSD_W = 384
SSD_CONV_DIM = 896
SSD_STATE = 128
SGU_W = 256
DILATIONS = (1, 4, 16)
W_QKV = 3 * ATT_W
W_SSD = SSD_CONV_DIM + SSD_W + SSD_W
W_UV = 2 * SGU_W
ADAM_LR = 0.001
ADAM_B1 = 0.9
ADAM_B2 = 0.999
ADAM_EPS = 1e-08
ADAM_WD = 0.01
ADAM_STEP = 10
NEG = -1e30


def _dot(a, b):
    return jnp.dot(a, b, preferred_element_type=F32)


def _dot_nt(a, b):
    return lax.dot_general(a, b, (((1,), (1,)), ((), ())), preferred_element_type=F32)


def _dot_tn(a, b):
    return lax.dot_general(a, b, (((0,), (0,)), ((), ())), preferred_element_type=F32)


def _sigmoid(x):
    return 1.0 / (1.0 + jnp.exp(-x))


def _call(body, *, name, grid, in_specs, out_specs, out_shape, scratch=(), sem=None):
    return pl.pallas_call(
        body, name=name, grid=grid, in_specs=in_specs, out_specs=out_specs, out_shape=out_shape,
        scratch_shapes=list(scratch),
        compiler_params=pltpu.CompilerParams(dimension_semantics=sem),
    )


def _tile(n, want):
    t = min(n, want)
    while n % t:
        t //= 2
    return t


def _rms_fwd(x, g):
    T, D = x.shape
    tm = _tile(T, 512)

    def body(x_ref, g_ref, h_ref):
        xf = x_ref[...]
        r = lax.rsqrt(jnp.mean(xf * xf, axis=-1, keepdims=True) + RMS_EPS)
        h_ref[...] = (xf * r * g_ref[...]).astype(BF)

    return _call(body, name="rms_fwd", grid=(T // tm,),
                 in_specs=[pl.BlockSpec((tm, D), lambda i: (i, 0)), pl.BlockSpec((1, D), lambda i: (0, 0))],
                 out_specs=pl.BlockSpec((tm, D), lambda i: (i, 0)),
                 out_shape=jax.ShapeDtypeStruct((T, D), BF), sem=("parallel",))(x, g)


def _rms_bwd(x, g, dh, dres):
    T, D = x.shape
    tm = _tile(T, 512)

    def body(x_ref, g_ref, dh_ref, dr_ref, dx_ref, dg_ref):
        @pl.when(pl.program_id(0) == 0)
        def _():
            dg_ref[...] = jnp.zeros_like(dg_ref)

        xf = x_ref[...]
        r = lax.rsqrt(jnp.mean(xf * xf, axis=-1, keepdims=True) + RMS_EPS)
        dh_ = dh_ref[...]
        u = dh_ * g_ref[...]
        mu = jnp.mean(u * xf, axis=-1, keepdims=True)
        dx_ref[...] = dr_ref[...] + r * (u - xf * (r * r * mu))
        dg_ref[...] += jnp.sum(dh_ * xf * r, axis=0, keepdims=True)

    row = pl.BlockSpec((tm, D), lambda i: (i, 0))
    vec = pl.BlockSpec((1, D), lambda i: (0, 0))
    return _call(body, name="rms_bwd", grid=(T // tm,), in_specs=[row, vec, row, row], out_specs=[row, vec],
                 out_shape=[jax.ShapeDtypeStruct((T, D), F32), jax.ShapeDtypeStruct((1, D), F32)],
                 sem=("arbitrary",))(x, g, dh, dres)


def _final_loss(x, g, tgt):
    T, D = x.shape
    tm = _tile(T, 512)

    def body(x_ref, g_ref, t_ref, l_ref, dx_ref, dg_ref):
        @pl.when(pl.program_id(0) == 0)
        def _():
            dg_ref[...] = jnp.zeros_like(dg_ref)
            l_ref[...] = jnp.zeros_like(l_ref)

        xf = x_ref[...]
        gg = g_ref[...]
        r = lax.rsqrt(jnp.mean(xf * xf, axis=-1, keepdims=True) + RMS_EPS)
        xn = xf * r
        e = xn * gg - t_ref[...]
        part = 0.5 * jnp.sum(jnp.mean(e * e, axis=-1, keepdims=True), axis=0, keepdims=True)
        l_ref[...] += jnp.broadcast_to(part, l_ref.shape)
        dy = e * (1.0 / D)
        u = dy * gg
        mu = jnp.mean(u * xf, axis=-1, keepdims=True)
        dx_ref[...] = r * (u - xf * (r * r * mu))
        dg_ref[...] += jnp.sum(dy * xn, axis=0, keepdims=True)

    row = pl.BlockSpec((tm, D), lambda i: (i, 0))
    vec = pl.BlockSpec((1, D), lambda i: (0, 0))
    lsp = pl.BlockSpec((1, 128), lambda i: (0, 0))
    return _call(body, name="final_loss", grid=(T // tm,), in_specs=[row, vec, row], out_specs=[lsp, row, vec],
                 out_shape=[jax.ShapeDtypeStruct((1, 128), F32), jax.ShapeDtypeStruct((T, D), F32),
                            jax.ShapeDtypeStruct((1, D), F32)],
                 sem=("arbitrary",))(x, g, tgt)


def _ffn_up(hb, wg, wu):
    T, D = hb.shape
    NS, _, Fs = wg.shape
    tm = _tile(T, 512)

    def body(h_ref, wg_ref, wu_ref, g_ref, u_ref, a_ref):
        h = h_ref[...]
        g = _dot(h, wg_ref[...])
        u = _dot(h, wu_ref[...])
        g_ref[...] = g.astype(BF)
        u_ref[...] = u.astype(BF)
        a_ref[...] = (g * _sigmoid(g) * u).astype(BF)

    w = pl.BlockSpec((None, D, Fs), lambda j, i: (j, 0, 0))
    o = pl.BlockSpec((None, tm, Fs), lambda j, i: (j, i, 0))
    sh = jax.ShapeDtypeStruct((NS, T, Fs), BF)
    return _call(body, name="ffn_up", grid=(NS, T // tm),
                 in_specs=[pl.BlockSpec((tm, D), lambda j, i: (i, 0)), w, w], out_specs=[o, o, o],
                 out_shape=[sh, sh, sh], sem=("parallel", "parallel"))(hb, wg, wu)


def _ffn_down(a, wd, x):
    NS, T, Fs = a.shape
    D = wd.shape[2]
    tm = _tile(T, 512)

    def body(a_ref, w_ref, x_ref, o_ref, acc):
        j = pl.program_id(1)

        @pl.when(j == 0)
        def _():
            acc[...] = jnp.zeros_like(acc)

        acc[...] += _dot(a_ref[...], w_ref[...])

        @pl.when(j == NS - 1)
        def _():
            o_ref[...] = x_ref[...] + 0.5 * acc[...]

    row = pl.BlockSpec((tm, D), lambda i, j: (i, 0))
    return _call(body, name="ffn_down", grid=(T // tm, NS),
                 in_specs=[pl.BlockSpec((None, tm, Fs), lambda i, j: (j, i, 0)),
                           pl.BlockSpec((None, Fs, D), lambda i, j: (j, 0, 0)), row],
                 out_specs=row, out_shape=jax.ShapeDtypeStruct((T, D), F32),
                 scratch=[pltpu.VMEM((tm, D), F32)], sem=("parallel", "arbitrary"))(a, wd, x)


def _ffn_bwd_act(dxo, wd, g, u):
    NS, T, Fs = g.shape
    D = dxo.shape[1]
    tm = _tile(T, 512)

    def body(dx_ref, w_ref, g_ref, u_ref, dg_ref, du_ref):
        dy = (0.5 * dx_ref[...]).astype(BF)
        da = _dot_nt(dy, w_ref[...])
        gf = g_ref[...].astype(F32)
        uf = u_ref[...].astype(F32)
        sg = _sigmoid(gf)
        dg_ref[...] = (da * uf * (sg * (1.0 + gf * (1.0 - sg)))).astype(BF)
        du_ref[...] = (da * gf * sg).astype(BF)

    o = pl.BlockSpec((None, tm, Fs), lambda j, i: (j, i, 0))
    sh = jax.ShapeDtypeStruct((NS, T, Fs), BF)
    return _call(body, name="ffn_bwd_act", grid=(NS, T // tm),
                 in_specs=[pl.BlockSpec((tm, D), lambda j, i: (i, 0)),
                           pl.BlockSpec((None, Fs, D), lambda j, i: (j, 0, 0)), o, o],
                 out_specs=[o, o], out_shape=[sh, sh], sem=("parallel", "parallel"))(dxo, wd, g, u)


def _ffn_bwd_wd(a, dxo):
    NS, T, Fs = a.shape
    D = dxo.shape[1]
    tk = _tile(T, 512)

    def body(a_ref, dx_ref, o_ref):
        @pl.when(pl.program_id(1) == 0)
        def _():
            o_ref[...] = jnp.zeros_like(o_ref)

        o_ref[...] += _dot_tn(a_ref[...], (0.5 * dx_ref[...]).astype(BF))

    return _call(body, name="ffn_bwd_wd", grid=(NS, T // tk),
                 in_specs=[pl.BlockSpec((None, tk, Fs), lambda j, k: (j, k, 0)),
                           pl.BlockSpec((tk, D), lambda j, k: (k, 0))],
                 out_specs=pl.BlockSpec((None, Fs, D), lambda j, k: (j, 0, 0)),
                 out_shape=jax.ShapeDtypeStruct((NS, Fs, D), F32), sem=("parallel", "arbitrary"))(a, dxo)


def _ffn_bwd_wgu(hb, dg, du):
    NS, T, Fs = dg.shape
    D = hb.shape[1]
    tk = _tile(T, 512)

    def body(h_ref, dg_ref, du_ref, og_ref, ou_ref):
        @pl.when(pl.program_id(1) == 0)
        def _():
            og_ref[...] = jnp.zeros_like(og_ref)
            ou_ref[...] = jnp.zeros_like(ou_ref)

        h = h_ref[...]
        og_ref[...] += _dot_tn(h, dg_ref[...])
        ou_ref[...] += _dot_tn(h, du_ref[...])

    d = pl.BlockSpec((None, tk, Fs), lambda j, k: (j, k, 0))
    o = pl.BlockSpec((None, D, Fs), lambda j, k: (j, 0, 0))
    sh = jax.ShapeDtypeStruct((NS, D, Fs), F32)
    return _call(body, name="ffn_bwd_wgu", grid=(NS, T // tk),
                 in_specs=[pl.BlockSpec((tk, D), lambda j, k: (k, 0)), d, d], out_specs=[o, o],
                 out_shape=[sh, sh], sem=("parallel", "arbitrary"))(hb, dg, du)


def _ffn_bwd_dh(dg, du, wg, wu):
    NS, T, Fs = dg.shape
    D = wg.shape[1]
    tm = _tile(T, 512)

    def body(dg_ref, du_ref, wg_ref, wu_ref, o_ref, acc):
        j = pl.program_id(1)

        @pl.when(j == 0)
        def _():
            acc[...] = jnp.zeros_like(acc)

        acc[...] += _dot_nt(dg_ref[...], wg_ref[...]) + _dot_nt(du_ref[...], wu_ref[...])

        @pl.when(j == NS - 1)
        def _():
            o_ref[...] = acc[...]

    d = pl.BlockSpec((None, tm, Fs), lambda i, j: (j, i, 0))
    w = pl.BlockSpec((None, D, Fs), lambda i, j: (j, 0, 0))
    return _call(body, name="ffn_bwd_dh", grid=(T // tm, NS), in_specs=[d, d, w, w],
                 out_specs=pl.BlockSpec((tm, D), lambda i, j: (i, 0)),
                 out_shape=jax.ShapeDtypeStruct((T, D), F32),
                 scratch=[pltpu.VMEM((tm, D), F32)], sem=("parallel", "arbitrary"))(dg, du, wg, wu)


def _mm_nn(a, b, res=None, out_dtype=F32):
    T, K = a.shape
    N = b.shape[1]
    tm = _tile(T, 512)
    tn = N if N <= 2048 else _tile(N, 1024)

    def body(*refs):
        if res is None:
            a_ref, b_ref, o_ref = refs
            o_ref[...] = _dot(a_ref[...], b_ref[...]).astype(out_dtype)
        else:
            a_ref, b_ref, r_ref, o_ref = refs
            o_ref[...] = (r_ref[...] + _dot(a_ref[...], b_ref[...])).astype(out_dtype)

    o = pl.BlockSpec((tm, tn), lambda i, j: (i, j))
    ins = [pl.BlockSpec((tm, K), lambda i, j: (i, 0)), pl.BlockSpec((K, tn), lambda i, j: (0, j))]
    args = [a, b]
    if res is not None:
        ins.append(o)
        args.append(res)
    return _call(body, name="mm_nn", grid=(T // tm, N // tn), in_specs=ins, out_specs=o,
                 out_shape=jax.ShapeDtypeStruct((T, N), out_dtype), sem=("parallel", "parallel"))(*args)


def _mm_nt(a, b, res=None):
    T, K = a.shape
    N = b.shape[0]
    tm = _tile(T, 512)

    def body(*refs):
        if res is None:
            a_ref, b_ref, o_ref = refs
            o_ref[...] = _dot_nt(a_ref[...].astype(BF), b_ref[...])
        else:
            a_ref, b_ref, r_ref, o_ref = refs
            o_ref[...] = r_ref[...] + _dot_nt(a_ref[...].astype(BF), b_ref[...])

    o = pl.BlockSpec((tm, N), lambda i: (i, 0))
    ins = [pl.BlockSpec((tm, K), lambda i: (i, 0)), pl.BlockSpec((N, K), lambda i: (0, 0))]
    args = [a, b]
    if res is not None:
        ins.append(o)
        args.append(res)
    return _call(body, name="mm_nt", grid=(T // tm,), in_specs=ins, out_specs=o,
                 out_shape=jax.ShapeDtypeStruct((T, N), F32), sem=("parallel",))(*args)


def _mm_tn(a, b):
    T, M = a.shape
    N = b.shape[1]
    tk = _tile(T, 512)
    tmm = _tile(M, 512)

    def body(a_ref, b_ref, o_ref):
        @pl.when(pl.program_id(1) == 0)
        def _():
            o_ref[...] = jnp.zeros_like(o_ref)

        o_ref[...] += _dot_tn(a_ref[...].astype(BF), b_ref[...].astype(BF))

    return _call(body, name="mm_tn", grid=(M // tmm, T // tk),
                 in_specs=[pl.BlockSpec((tk, tmm), lambda i, k: (k, i)), pl.BlockSpec((tk, N), lambda i, k: (k, 0))],
                 out_specs=pl.BlockSpec((tmm, N), lambda i, k: (i, 0)),
                 out_shape=jax.ShapeDtypeStruct((M, N), F32), sem=("parallel", "arbitrary"))(a, b)


def _lane_mask(e, width=128):
    return (lax.broadcasted_iota(jnp.int32, (1, width), 1) // HEAD) == e


def _band_mask(n):
    qi = lax.broadcasted_iota(jnp.int32, (CHUNK, 2 * CHUNK), 0)
    kj = lax.broadcasted_iota(jnp.int32, (CHUNK, 2 * CHUNK), 1)
    dist = qi + CHUNK - kj
    return (dist >= 0) & (dist <= CHUNK) & ((kj >= CHUNK) | (n > 0))


def _prev_cur(ref, n):
    cur = pl.multiple_of(n * CHUNK, CHUNK)
    prv = pl.multiple_of(jnp.maximum(n - 1, 0) * CHUNK, CHUNK)
    return jnp.concatenate([ref[pl.ds(prv, CHUNK), :], ref[pl.ds(cur, CHUNK), :]], axis=0), prv, cur


def _attn_fwd(qkv, dil):
    T = qkv.shape[0]
    B, L = T // SEQ, SEQ // dil
    nb = L // CHUNK
    scale = HEAD ** -0.5

    def body(q_ref, k_ref, v_ref, o_ref, l_ref):
        n = pl.program_id(2)
        q = q_ref[...]
        kk, _, _ = _prev_cur(k_ref, n)
        vv, _, _ = _prev_cur(v_ref, n)
        mask = _band_mask(n)
        for t in range(ATT_W // 128):
            sl = slice(128 * t, 128 * (t + 1))
            qt, kt, vt = q[:, sl], kk[:, sl], vv[:, sl]
            o_pair = jnp.zeros((CHUNK, 128), F32)
            l_pair = jnp.zeros((CHUNK, 128), F32)
            for e in range(2):
                lm = _lane_mask(e)
                s = _dot_nt(jnp.where(lm, qt, jnp.zeros_like(qt)), kt) * scale
                s = jnp.where(mask, s, NEG)
                m = jnp.max(s, axis=-1, keepdims=True)
                p = jnp.exp(s - m)
                den = jnp.sum(p, axis=-1, keepdims=True)
                o = _dot(p.astype(BF), vt) / den
                o_pair = jnp.where(lm, o, o_pair)
                l_pair = jnp.where(lm, m + jnp.log(den), l_pair)
            o_ref[:, sl] = o_pair
            l_ref[:, sl] = l_pair

    qv = qkv.reshape(B, L, dil * W_QKV)
    o = pl.BlockSpec((None, CHUNK, ATT_W), lambda b, r, n: (b, n, r))
    sh = jax.ShapeDtypeStruct((B, L, dil * ATT_W), F32)
    out, lse = _call(
        body, name=f"attn_fwd_d{dil}", grid=(B, dil, nb),
        in_specs=[pl.BlockSpec((None, CHUNK, ATT_W), lambda b, r, n: (b, n, 3 * r)),
                  pl.BlockSpec((None, L, ATT_W), lambda b, r, n: (b, 0, 3 * r + 1)),
                  pl.BlockSpec((None, L, ATT_W), lambda b, r, n: (b, 0, 3 * r + 2))],
        out_specs=[o, o], out_shape=[sh, sh], sem=("parallel", "parallel", "parallel"))(qv, qv, qv)
    return out.reshape(T, ATT_W), lse.reshape(T, ATT_W)


def _attn_combine(o1, o2, o3, l1, l2, l3):
    T = o1.shape[0]
    tm = _tile(T, 512)

    def body(o1_ref, o2_ref, o3_ref, l1_ref, l2_ref, l3_ref, y_ref, l_ref):
        a, b, c = l1_ref[...], l2_ref[...], l3_ref[...]
        m = jnp.maximum(jnp.maximum(a, b), c)
        ea, eb, ec = jnp.exp(a - m), jnp.exp(b - m), jnp.exp(c - m)
        z = ea + eb + ec
        y_ref[...] = (ea * o1_ref[...] + eb * o2_ref[...] + ec * o3_ref[...]) / z
        l_ref[...] = m + jnp.log(z)

    row = pl.BlockSpec((tm, ATT_W), lambda i: (i, 0))
    sh = jax.ShapeDtypeStruct((T, ATT_W), F32)
    return _call(body, name="attn_combine", grid=(T // tm,), in_specs=[row] * 6, out_specs=[row, row],
                 out_shape=[sh, sh], sem=("parallel",))(o1, o2, o3, l1, l2, l3)


def _attn_bwd(qkv, do, out, lse, dil):
    T = qkv.shape[0]
    B, L = T // SEQ, SEQ // dil
    nb = L // CHUNK
    scale = HEAD ** -0.5

    def body(q_ref, k_ref, v_ref, do_ref, out_ref, lse_ref, dq_ref, dk_ref, dv_ref):
        n = pl.program_id(2)

        @pl.when(n == 0)
        def _():
            dk_ref[...] = jnp.zeros_like(dk_ref)
            dv_ref[...] = jnp.zeros_like(dv_ref)

        q = q_ref[...]
        kk, prv, cur = _prev_cur(k_ref, n)
        vv, _, _ = _prev_cur(v_ref, n)
        mask = _band_mask(n)
        do_ = do_ref[...]
        dlt = do_ * out_ref[...]
        ls = lse_ref[...]
        for t in range(ATT_W // 128):
            sl = slice(128 * t, 128 * (t + 1))
            qt, kt, vt = q[:, sl], kk[:, sl], vv[:, sl]
            dq_pair = jnp.zeros((CHUNK, 128), F32)
            dk_acc = jnp.zeros((2 * CHUNK, 128), F32)
            dv_acc = jnp.zeros((2 * CHUNK, 128), F32)
            for e in range(2):
                lm = _lane_mask(e)
                qm = jnp.where(lm, qt, jnp.zeros_like(qt))
                s = _dot_nt(qm, kt) * scale
                lse_col = ls[:, 128 * t + HEAD * e:128 * t + HEAD * e + 1]
                p = jnp.exp(jnp.where(mask, s - lse_col, NEG))
                dom = jnp.where(lm, do_[:, sl], 0.0).astype(BF)
                dv_acc += _dot_tn(p.astype(BF), dom)
                dp = _dot_nt(dom, vt)
                delta = jnp.sum(jnp.where(lm, dlt[:, sl], 0.0), axis=-1, keepdims=True)
                ds = (p * (dp - delta) * scale).astype(BF)
                dq_pair += jnp.where(lm, _dot(ds, kt), 0.0)
                dk_acc += _dot_tn(ds, qm)
            dq_ref[:, sl] = dq_pair
            dk_ref[pl.ds(cur, CHUNK), sl] += dk_acc[CHUNK:]
            dk_ref[pl.ds(prv, CHUNK), sl] += dk_acc[:CHUNK]
            dv_ref[pl.ds(cur, CHUNK), sl] += dv_acc[CHUNK:]
            dv_ref[pl.ds(prv, CHUNK), sl] += dv_acc[:CHUNK]

    qv = qkv.reshape(B, L, dil * W_QKV)
    view = lambda a: a.reshape(B, L, dil * ATT_W)
    blk = pl.BlockSpec((None, CHUNK, ATT_W), lambda b, r, n: (b, n, r))
    whole = pl.BlockSpec((None, L, ATT_W), lambda b, r, n: (b, 0, r))
    sh = jax.ShapeDtypeStruct((B, L, dil * ATT_W), F32)
    dq, dk, dv = _call(
        body, name=f"attn_bwd_d{dil}", grid=(B, dil, nb),
        in_specs=[pl.BlockSpec((None, CHUNK, ATT_W), lambda b, r, n: (b, n, 3 * r)),
                  pl.BlockSpec((None, L, ATT_W), lambda b, r, n: (b, 0, 3 * r + 1)),
                  pl.BlockSpec((None, L, ATT_W), lambda b, r, n: (b, 0, 3 * r + 2)),
                  blk, blk, blk],
        out_specs=[blk, whole, whole], out_shape=[sh, sh, sh],
        sem=("parallel", "parallel", "arbitrary"))(qv, qv, qv, view(do), view(out), view(lse))
    return dq.reshape(T, ATT_W), dk.reshape(T, ATT_W), dv.reshape(T, ATT_W)


def _sum_branches(parts):
    T = parts[0][0].shape[0]
    tm = _tile(T, 512)

    def body(*refs):
        o_ref = refs[-1]
        for s in range(3):
            acc = refs[s][...] + refs[3 + s][...] + refs[6 + s][...]
            o_ref[:, ATT_W * s:ATT_W * (s + 1)] = acc.astype(BF)

    row = pl.BlockSpec((tm, ATT_W), lambda i: (i, 0))
    flat = [a for tr in parts for a in tr]
    return _call(body, name="attn_sum_branches", grid=(T // tm,), in_specs=[row] * 9,
                 out_specs=pl.BlockSpec((tm, W_QKV), lambda i: (i, 0)),
                 out_shape=jax.ShapeDtypeStruct((T, W_QKV), BF), sem=("parallel",))(*flat)


def _silu(x):
    return x * _sigmoid(x)


def _dsilu(x):
    s = _sigmoid(x)
    return s * (1.0 + x * (1.0 - s))


def _log1p(u):
    return jnp.where(u < 0.01, u * (1.0 - u * (0.5 - u * (1.0 / 3.0))), jnp.log(1.0 + u))


def _softplus(x):
    return jnp.maximum(x, 0.0) + _log1p(jnp.exp(-jnp.abs(x)))


def _cumsum_rows(x, reverse=False):
    n = x.shape[0]
    rows = lax.broadcasted_iota(jnp.int32, x.shape, 0)
    k = 1
    while k < n:
        if reverse:
            x = x + jnp.where(rows < n - k, pltpu.roll(x, n - k, 0), 0.0)
        else:
            x = x + jnp.where(rows >= k, pltpu.roll(x, k, 0), 0.0)
        k *= 2
    return x


def _tri():
    r = lax.broadcasted_iota(jnp.int32, (CHUNK, CHUNK), 0)
    c = lax.broadcasted_iota(jnp.int32, (CHUNK, CHUNK), 1)
    return r >= c


def _row_mask(e):
    return (lax.broadcasted_iota(jnp.int32, (128, 1), 0) // HEAD) == e


def _first_lane(e):
    return lax.broadcasted_iota(jnp.int32, (1, 128), 1) == HEAD * e


def _ssd_pre(x_ref, halo_ref, first, cw_ref, cb_ref, dtb_ref, al_ref, ext):
    row = x_ref[...]
    z = row[:, SSD_CONV_DIM:SSD_CONV_DIM + SSD_W]
    u = row[:, SSD_CONV_DIM + SSD_W:] + dtb_ref[...]
    ext[0:8, :] = jnp.where(first, 0.0, halo_ref[:, 0:SSD_CONV_DIM])
    ext[8:8 + CHUNK, :] = row[:, 0:SSD_CONV_DIM]
    xc = cb_ref[...]
    for j in range(4):
        xc = xc + cw_ref[j:j + 1, :] * ext[pl.ds(5 + j, CHUNK), :]
    xa = _silu(xc)
    dt = _softplus(u)
    a = dt * (-jnp.exp(al_ref[...]))
    A = _cumsum_rows(a)
    return dict(z=z, u=u, xc=xc, xs=xa[:, 0:SSD_W], Bm=xa[:, SSD_W:SSD_W + 256], Cm=xa[:, SSD_W + 256:],
                dt=dt, a=a, A=A, AT=A.T, eA=jnp.exp(A), wdec=jnp.exp(A[CHUNK - 1:CHUNK, :] - A),
                dtot=jnp.exp(A[CHUNK - 1:CHUNK, :]))


def _ssd_y(p, hp_ref, dskip):
    tri = _tri()
    X = p["xs"] * p["dt"]
    Bb = [p["Bm"][:, 128 * g:128 * (g + 1)].astype(BF) for g in range(2)]
    Cb = [p["Cm"][:, 128 * g:128 * (g + 1)].astype(BF) for g in range(2)]
    CB = [_dot_nt(Cb[g], Bb[g]) for g in range(2)]
    tiles = []
    for t in range(3):
        sl = slice(128 * t, 128 * (t + 1))
        hpb = hp_ref[sl, :].astype(BF)
        acc = jnp.zeros((CHUNK, 128), F32)
        for e in range(2):
            h = 2 * t + e
            g, col = h // 3, HEAD * h
            lm = _lane_mask(e)
            L = jnp.exp(jnp.where(tri, p["A"][:, col:col + 1] - p["AT"][col:col + 1, :], NEG))
            yd = _dot((CB[g] * L).astype(BF), jnp.where(lm, X[:, sl], 0.0).astype(BF))
            yo = _dot_nt(Cb[g], hpb) * p["eA"][:, sl]
            acc = acc + jnp.where(lm, yd + yo, 0.0)
        tiles.append(acc)
    return jnp.concatenate(tiles, axis=1) + dskip * p["xs"], X, Bb, Cb, CB


def _group_stats(v):
    g0 = lax.broadcasted_iota(jnp.int32, (1, SSD_W), 1) < SSD_W // 2
    m0 = jnp.sum(jnp.where(g0, v, 0.0), axis=-1, keepdims=True) * (2.0 / SSD_W)
    m1 = jnp.sum(jnp.where(g0, 0.0, v), axis=-1, keepdims=True) * (2.0 / SSD_W)
    return jnp.where(g0, m0, m1)


def _ssd_specs(T, rev):
    B = T // SEQ

    def chunk(b, c):
        return b * N_CHUNK + (N_CHUNK - 1 - c if rev else c)

    row = pl.BlockSpec((CHUNK, W_SSD), lambda b, c: (chunk(b, c), 0))
    halo = pl.BlockSpec((8, W_SSD), lambda b, c: (jnp.maximum(chunk(b, c) * (CHUNK // 8) - 1, 0), 0))
    hp = pl.BlockSpec((None, SSD_W, SSD_STATE), lambda b, c: (chunk(b, c), 0, 0))
    y = pl.BlockSpec((CHUNK, SSD_W), lambda b, c: (chunk(b, c), 0))
    const = lambda r, w: pl.BlockSpec((r, w), lambda b, c: (0, 0))
    params = [const(4, SSD_CONV_DIM), const(1, SSD_CONV_DIM)] + [const(1, SSD_W)] * 4
    return B, row, halo, hp, y, const, params


def _ssd_fwd(sin, conv_w, conv_b, dtb, alog, dskip, norm_g):
    T = sin.shape[0]
    B, row, halo, hp, y, const, params = _ssd_specs(T, False)

    def body(x_ref, halo_ref, cw_ref, cb_ref, dtb_ref, al_ref, dk_ref, ng_ref, y_ref, hp_ref, ext, hst):
        c = pl.program_id(1)

        @pl.when(c == 0)
        def _():
            hst[...] = jnp.zeros_like(hst)

        p = _ssd_pre(x_ref, halo_ref, c == 0, cw_ref, cb_ref, dtb_ref, al_ref, ext)
        yv, X, Bb, Cb, CB = _ssd_y(p, hst, dk_ref[...])
        hp_ref[...] = hst[...]
        for t in range(3):
            sl = slice(128 * t, 128 * (t + 1))
            old = hst[sl, :]
            new = old
            for e in range(2):
                h = 2 * t + e
                g, col = h // 3, HEAD * h
                st = _dot_tn(jnp.where(_lane_mask(e), X[:, sl] * p["wdec"][:, sl], 0.0).astype(BF), Bb[g])
                new = jnp.where(_row_mask(e), old * p["dtot"][:, col:col + 1] + st, new)
            hst[sl, :] = new
        y2 = yv * _silu(p["z"])
        r = lax.rsqrt(_group_stats(y2 * y2) + RMS_EPS)
        y_ref[...] = y2 * r * ng_ref[...]

    return _call(body, name="ssd_fwd", grid=(B, N_CHUNK), in_specs=[row, halo] + params, out_specs=[y, hp],
                 out_shape=[jax.ShapeDtypeStruct((T, SSD_W), F32),
                            jax.ShapeDtypeStruct((T // CHUNK, SSD_W, SSD_STATE), F32)],
                 scratch=[pltpu.VMEM((8 + CHUNK, SSD_CONV_DIM), F32), pltpu.VMEM((SSD_W, SSD_STATE), F32)],
                 sem=("parallel", "arbitrary"))(sin, sin, conv_w, conv_b, dtb, alog, dskip, norm_g)


def _ssd_bwd(sin, hprev, dy3, conv_w, conv_b, dtb, alog, dskip, norm_g):
    T = sin.shape[0]
    B, row, halo, hp, y, const, params = _ssd_specs(T, True)

    def body(x_ref, halo_ref, hp_ref, dy_ref, cw_ref, cb_ref, dtb_ref, al_ref, dk_ref, ng_ref,
             dx_ref, dcw_ref, dcb_ref, dvec_ref, ext, ext2, dh):
        c = pl.program_id(1)

        @pl.when((pl.program_id(0) == 0) & (c == 0))
        def _():
            dcw_ref[...] = jnp.zeros_like(dcw_ref)
            dcb_ref[...] = jnp.zeros_like(dcb_ref)
            dvec_ref[...] = jnp.zeros_like(dvec_ref)

        @pl.when(c == 0)
        def _():
            dh[...] = jnp.zeros_like(dh)
            ext2[CHUNK:CHUNK + 8, :] = jnp.zeros((8, SSD_CONV_DIM), F32)

        p = _ssd_pre(x_ref, halo_ref, c == N_CHUNK - 1, cw_ref, cb_ref, dtb_ref, al_ref, ext)
        dskip_ = dk_ref[...]
        yv, X, Bb, Cb, CB = _ssd_y(p, hp_ref, dskip_)
        xs, z, A, AT = p["xs"], p["z"], p["A"], p["AT"]

        sz = _silu(z)
        y2 = yv * sz
        r = lax.rsqrt(_group_stats(y2 * y2) + RMS_EPS)
        dy3_ = dy_ref[...]
        uu = dy3_ * ng_ref[...]
        dy2 = r * (uu - y2 * (r * r * _group_stats(uu * y2)))
        dy = dy2 * sz
        dz = dy2 * yv * _dsilu(z)

        tri = _tri()
        rows = lax.broadcasted_iota(jnp.int32, (CHUNK, 1), 0)
        dG = [jnp.zeros((CHUNK, CHUNK), F32) for _ in range(2)]
        dB = [jnp.zeros((CHUNK, SSD_STATE), F32) for _ in range(2)]
        dC = [jnp.zeros((CHUNK, SSD_STATE), F32) for _ in range(2)]
        dX_t, dA_t, ddtx_t = [], [], []
        for t in range(3):
            sl = slice(128 * t, 128 * (t + 1))
            hp_t = hp_ref[sl, :]
            hpb = hp_t.astype(BF)
            dhc = dh[sl, :]
            dh_new = jnp.zeros((128, SSD_STATE), F32)
            dX = jnp.zeros((CHUNK, 128), F32)
            dA = jnp.zeros((CHUNK, 128), F32)
            ddtx = jnp.zeros((CHUNK, 128), F32)
            for e in range(2):
                h = 2 * t + e
                g, col = h // 3, HEAD * h
                lm, rm, fl = _lane_mask(e), _row_mask(e), _first_lane(e)
                L = jnp.exp(jnp.where(tri, A[:, col:col + 1] - AT[col:col + 1, :], NEG))
                Mf = CB[g] * L
                Xm = jnp.where(lm, X[:, sl], 0.0)
                Xmb = Xm.astype(BF)
                dyh = jnp.where(lm, dy[:, sl], 0.0)
                dyb = dyh.astype(BF)
                dXh = _dot_tn(Mf.astype(BF), dyb)
                dM = jnp.where(tri, _dot_nt(dyb, Xmb), 0.0)
                Wm = dM * Mf
                dAc = jnp.sum(Wm, axis=-1, keepdims=True) - jnp.sum(Wm.T, axis=-1, keepdims=True)
                dG[g] = dG[g] + dM * L
                eAt = p["eA"][:, sl]
                yo = _dot_nt(Cb[g], hpb)
                dyo = (dyh * eAt).astype(BF)
                dC[g] = dC[g] + _dot(dyo, hpb)
                dh_new = dh_new + _dot_tn(dyo, Cb[g])
                dAc = dAc + jnp.sum(dyh * yo * eAt, axis=-1, keepdims=True)
                dHn = jnp.where(rm, dhc, 0.0)
                dHnb = dHn.astype(BF)
                dec = p["dtot"][:, col:col + 1]
                dh_new = dh_new + dec * dHn
                Z = _dot_nt(Bb[g], dHnb)
                wt = p["wdec"][:, sl]
                xi = jnp.sum(Xm * Z, axis=-1, keepdims=True) * p["wdec"][:, col:col + 1]
                dXh = dXh + wt * Z
                dB[g] = dB[g] + _dot(jnp.where(lm, X[:, sl] * wt, 0.0).astype(BF), dHnb)
                dAtot = jnp.sum(xi, axis=0, keepdims=True) + dec * jnp.sum(
                    jnp.sum(dHn * hp_t, axis=-1, keepdims=True), axis=0, keepdims=True)
                dAc = dAc - xi + jnp.where(rows == CHUNK - 1, dAtot, 0.0)
                dA = dA + jnp.where(fl, dAc, 0.0)
                dX = dX + dXh
                ddtx = ddtx + jnp.where(fl, jnp.sum(dXh * xs[:, sl], axis=-1, keepdims=True), 0.0)
            dh[sl, :] = dh_new
            dX_t.append(dX)
            dA_t.append(dA)
            ddtx_t.append(ddtx)
        for g in range(2):
            dGb = dG[g].astype(BF)
            dC[g] = dC[g] + _dot(dGb, Bb[g])
            dB[g] = dB[g] + _dot_tn(dGb, Cb[g])
        dXf = jnp.concatenate(dX_t, axis=1)
        da = _cumsum_rows(jnp.concatenate(dA_t, axis=1), reverse=True)
        ddt = da * (-jnp.exp(al_ref[...])) + jnp.concatenate(ddtx_t, axis=1)
        du = ddt * _sigmoid(p["u"])
        dxs = dXf * p["dt"] + dskip_ * dy
        dxc = jnp.concatenate([dxs, dB[0], dB[1], dC[0], dC[1]], axis=1) * _dsilu(p["xc"])
        ext2[0:CHUNK, :] = dxc
        dxbc = jnp.zeros((CHUNK, SSD_CONV_DIM), F32)
        for j in range(4):
            dxbc = dxbc + cw_ref[j:j + 1, :] * ext2[pl.ds(3 - j, CHUNK), :]
            dcw_ref[j:j + 1, :] += jnp.sum(dxc * ext[pl.ds(5 + j, CHUNK), :], axis=0, keepdims=True)
        ext2[CHUNK:CHUNK + 8, :] = dxc[0:8, :]
        dcb_ref[...] += jnp.sum(dxc, axis=0, keepdims=True)
        dvec_ref[0:1, :] += jnp.sum(du, axis=0, keepdims=True)
        dvec_ref[1:2, :] += jnp.sum(da * p["a"], axis=0, keepdims=True)
        dvec_ref[2:3, :] += jnp.sum(dy * xs, axis=0, keepdims=True)
        dvec_ref[3:4, :] += jnp.sum(dy3_ * y2 * r, axis=0, keepdims=True)
        dx_ref[...] = jnp.concatenate([dxbc, dz, du], axis=1).astype(BF)

    return _call(body, name="ssd_bwd", grid=(B, N_CHUNK), in_specs=[row, halo, hp, y] + params,
                 out_specs=[row, const(4, SSD_CONV_DIM), const(1, SSD_CONV_DIM), const(8, SSD_W)],
                 out_shape=[jax.ShapeDtypeStruct((T, W_SSD), BF), jax.ShapeDtypeStruct((4, SSD_CONV_DIM), F32),
                            jax.ShapeDtypeStruct((1, SSD_CONV_DIM), F32), jax.ShapeDtypeStruct((8, SSD_W), F32)],
                 scratch=[pltpu.VMEM((8 + CHUNK, SSD_CONV_DIM), F32), pltpu.VMEM((8 + CHUNK, SSD_CONV_DIM), F32),
                          pltpu.VMEM((SSD_W, SSD_STATE), F32)],
                 sem=("arbitrary", "arbitrary"))(sin, sin, hprev, dy3, conv_w, conv_b, dtb, alog, dskip, norm_g)


def _sgu_core(uv_ref, g_ref, b_ref, w_ref, bias_ref):
    x = uv_ref[...]
    cdf = 0.5 * (1.0 + lax.erf(x * (2.0 ** -0.5)))
    ge = x * cdf
    dge = cdf + x * jnp.exp(-0.5 * x * x) * ((2.0 * math.pi) ** -0.5)
    u, v = ge[:, 0:SGU_W], ge[:, SGU_W:]
    vc = v - jnp.mean(v, axis=-1, keepdims=True)
    rstd = lax.rsqrt(jnp.mean(vc * vc, axis=-1, keepdims=True) + LN_EPS)
    vhat = vc * rstd
    vn = vhat * g_ref[...] + b_ref[...]
    tri = _tri()
    wc = [jnp.where(tri, w_ref[gi], 0.0).astype(BF) for gi in range(4)]
    vm = [jnp.where(_lane_mask(gi % 2), vn[:, 128 * (gi // 2):128 * (gi // 2 + 1)], 0.0).astype(BF) for gi in range(4)]
    mixed = jnp.concatenate([_dot(wc[2 * t], vm[2 * t]) + _dot(wc[2 * t + 1], vm[2 * t + 1]) for t in range(2)],
                            axis=1) + bias_ref[...]
    return dict(dge=dge, u=u, rstd=rstd, vhat=vhat, wc=wc, vm=vm, mixed=mixed)


def _sgu_specs():
    vec = pl.BlockSpec((1, SGU_W), lambda i: (0, 0))
    return [pl.BlockSpec((CHUNK, W_UV), lambda i: (i, 0)), vec, vec,
            pl.BlockSpec((4, CHUNK, CHUNK), lambda i: (0, 0, 0)), pl.BlockSpec((CHUNK, SGU_W), lambda i: (0, 0))]


def _sgu_fwd(uv, ln_g, ln_b, w, bias):
    T = uv.shape[0]

    def body(uv_ref, g_ref, b_ref, w_ref, bias_ref, y_ref):
        s = _sgu_core(uv_ref, g_ref, b_ref, w_ref, bias_ref)
        y_ref[...] = s["u"] * s["mixed"]

    return _call(body, name="sgu_fwd", grid=(T // CHUNK,), in_specs=_sgu_specs(),
                 out_specs=pl.BlockSpec((CHUNK, SGU_W), lambda i: (i, 0)),
                 out_shape=jax.ShapeDtypeStruct((T, SGU_W), F32), sem=("parallel",))(uv, ln_g, ln_b, w, bias)


def _sgu_bwd(uv, dy, ln_g, ln_b, w, bias):
    T = uv.shape[0]

    def body(uv_ref, dy_ref, g_ref, b_ref, w_ref, bias_ref, dx_ref, dw_ref, dbias_ref, dln_ref):
        @pl.when(pl.program_id(0) == 0)
        def _():
            dw_ref[...] = jnp.zeros_like(dw_ref)
            dbias_ref[...] = jnp.zeros_like(dbias_ref)
            dln_ref[...] = jnp.zeros_like(dln_ref)

        s = _sgu_core(uv_ref, g_ref, b_ref, w_ref, bias_ref)
        dy_ = dy_ref[...]
        du = dy_ * s["mixed"]
        dmix = dy_ * s["u"]
        dbias_ref[...] += dmix
        tri = _tri()
        dvn_t = []
        for t in range(2):
            acc = jnp.zeros((CHUNK, 128), F32)
            for e in range(2):
                gi = 2 * t + e
                dmg = jnp.where(_lane_mask(e), dmix[:, 128 * t:128 * (t + 1)], 0.0).astype(BF)
                acc = acc + _dot_tn(s["wc"][gi], dmg)
                dw_ref[gi] += jnp.where(tri, _dot_nt(dmg, s["vm"][gi]), 0.0)
            dvn_t.append(acc)
        dvn = jnp.concatenate(dvn_t, axis=1)
        dln_ref[0:1, :] += jnp.sum(dvn * s["vhat"], axis=0, keepdims=True)
        dln_ref[1:2, :] += jnp.sum(dvn, axis=0, keepdims=True)
        dvh = dvn * g_ref[...]
        dv = s["rstd"] * (dvh - jnp.mean(dvh, axis=-1, keepdims=True)
                          - s["vhat"] * jnp.mean(dvh * s["vhat"], axis=-1, keepdims=True))
        dx_ref[...] = (jnp.concatenate([du, dv], axis=1) * s["dge"]).astype(BF)

    ins = _sgu_specs()
    return _call(body, name="sgu_bwd", grid=(T // CHUNK,),
                 in_specs=[ins[0], pl.BlockSpec((CHUNK, SGU_W), lambda i: (i, 0))] + ins[1:],
                 out_specs=[pl.BlockSpec((CHUNK, W_UV), lambda i: (i, 0)),
                            pl.BlockSpec((4, CHUNK, CHUNK), lambda i: (0, 0, 0)),
                            pl.BlockSpec((CHUNK, SGU_W), lambda i: (0, 0)), pl.BlockSpec((8, SGU_W), lambda i: (0, 0))],
                 out_shape=[jax.ShapeDtypeStruct((T, W_UV), BF), jax.ShapeDtypeStruct((4, CHUNK, CHUNK), F32),
                            jax.ShapeDtypeStruct((CHUNK, SGU_W), F32), jax.ShapeDtypeStruct((8, SGU_W), F32)],
                 sem=("arbitrary",))(uv, dy, ln_g, ln_b, w, bias)


def _adamw(w, g, m, v):
    R, C = w.shape
    tr = _tile(R, 256) if R % 8 == 0 else R

    def body(w_ref, g_ref, m_ref, v_ref, d_ref, nm_ref, nv_ref):
        g_ = g_ref[...]
        m2 = ADAM_B1 * m_ref[...] + (1.0 - ADAM_B1) * g_
        v2 = ADAM_B2 * v_ref[...] + (1.0 - ADAM_B2) * (g_ * g_)
        m_hat = m2 / (1.0 - ADAM_B1 ** ADAM_STEP)
        v_hat = v2 / (1.0 - ADAM_B2 ** ADAM_STEP)
        d_ref[...] = -ADAM_LR * (m_hat / (jnp.sqrt(v_hat) + ADAM_EPS) + ADAM_WD * w_ref[...])
        nm_ref[...] = m2
        nv_ref[...] = v2

    blk = pl.BlockSpec((tr, C), lambda i: (i, 0))
    sh = jax.ShapeDtypeStruct((R, C), F32)
    return _call(body, name="adamw", grid=(R // tr,), in_specs=[blk] * 4, out_specs=[blk] * 3,
                 out_shape=[sh] * 3, sem=("parallel",))(w, g, m, v)


def _pair_add(gbuf, rsib, c):
    NS, _, R, C = gbuf.shape
    tr = 512

    def body(c_ref, a_ref, b_ref, o_ref):
        o_ref[...] = a_ref[...] + b_ref[...]

    blk = pl.BlockSpec((None, tr, C), lambda j, i, c_ref: (j, i, 0))
    return pl.pallas_call(
        body, name="rs_pair_add",
        grid_spec=pltpu.PrefetchScalarGridSpec(
            num_scalar_prefetch=1, grid=(NS, pl.cdiv(R, tr)),
            in_specs=[pl.BlockSpec((None, None, tr, C), lambda j, i, c_ref: (j, c_ref[0], i, 0)), blk],
            out_specs=blk),
        out_shape=jax.ShapeDtypeStruct((NS, R, C), F32),
        compiler_params=pltpu.CompilerParams(dimension_semantics=("parallel", "parallel")),
    )(jnp.reshape(c, (1,)).astype(jnp.int32), gbuf, rsib)


def _chip_sum(parts):
    NS, R, C = parts.shape
    tr = 512

    def body(p_ref, o_ref):
        o_ref[...] = ((p_ref[0] + p_ref[1]) + p_ref[2]) + p_ref[3]

    return _call(body, name="rs_chip_sum", grid=(pl.cdiv(R, tr),),
                 in_specs=[pl.BlockSpec((NS, tr, C), lambda i: (0, i, 0))],
                 out_specs=pl.BlockSpec((tr, C), lambda i: (i, 0)),
                 out_shape=jax.ShapeDtypeStruct((R, C), F32), sem=("parallel",))(parts)


MESH = pl.DeviceIdType.MESH
ANY = pl.BlockSpec(memory_space=pl.ANY)


def _place():
    x, y, c = lax.axis_index("x"), lax.axis_index("y"), lax.axis_index("c")
    return x, y, c, [(1 - x, y), (x, 1 - y), (1 - x, 1 - y)]


def _gather_shards(arrs):
    n = len(arrs)

    def body(*refs):
        ins, outs = refs[:n], refs[n:2 * n]
        send, recv, loc = refs[2 * n:]
        x, y, c, chips = _place()
        me = 2 * x + y
        local, remote = [], []
        for k in range(n):
            lc = pltpu.make_async_copy(ins[k], outs[k].at[me], loc.at[k])
            lc.start()
            local.append(lc)
            for r, (px, py) in enumerate(chips):
                cp = pltpu.make_async_remote_copy(
                    src_ref=ins[k], dst_ref=outs[k].at[me], send_sem=send.at[3 * k + r], recv_sem=recv.at[3 * k + r],
                    device_id=(px, py, c), device_id_type=MESH)
                cp.start()
                remote.append(cp)
        for cp in remote:
            cp.wait()
        for lc in local:
            lc.wait()

    return pl.pallas_call(
        body, name="gather_shards", in_specs=[ANY] * n, out_specs=[ANY] * n,
        out_shape=[jax.ShapeDtypeStruct((4,) + a.shape, a.dtype) for a in arrs],
        scratch_shapes=[pltpu.SemaphoreType.DMA((3 * n,)), pltpu.SemaphoreType.DMA((3 * n,)),
                        pltpu.SemaphoreType.DMA((n,))],
    )(*arrs)


def _rs_to_sibling(gbuf):
    NS, _, R, C = gbuf.shape

    def body(g_ref, o_ref, send, recv):
        x, y, c, _ = _place()
        cps = []
        for j in range(NS):
            cp = pltpu.make_async_remote_copy(
                src_ref=g_ref.at[j, 1 - c], dst_ref=o_ref.at[j], send_sem=send.at[j], recv_sem=recv.at[j],
                device_id=(x, y, 1 - c), device_id_type=MESH)
            cp.start()
            cps.append(cp)
        for cp in cps:
            cp.wait()

    return pl.pallas_call(
        body, name="rs_to_sibling", in_specs=[ANY], out_specs=ANY,
        out_shape=jax.ShapeDtypeStruct((NS, R, C), F32),
        scratch_shapes=[pltpu.SemaphoreType.DMA((NS,)), pltpu.SemaphoreType.DMA((NS,))],
    )(gbuf)


def _rs_to_chips(pbuf):
    NS, R, C = pbuf.shape

    def body(p_ref, o_ref, send, recv, loc):
        x, y, c, chips = _place()
        me = 2 * x + y
        lc = pltpu.make_async_copy(p_ref.at[me], o_ref.at[me], loc)
        lc.start()
        cps = []
        for r, (px, py) in enumerate(chips):
            cp = pltpu.make_async_remote_copy(
                src_ref=p_ref.at[2 * px + py], dst_ref=o_ref.at[me], send_sem=send.at[r], recv_sem=recv.at[r],
                device_id=(px, py, c), device_id_type=MESH)
            cp.start()
            cps.append(cp)
        for cp in cps:
            cp.wait()
        lc.wait()

    return pl.pallas_call(
        body, name="rs_to_chips", in_specs=[ANY], out_specs=ANY,
        out_shape=jax.ShapeDtypeStruct((NS, R, C), F32),
        scratch_shapes=[pltpu.SemaphoreType.DMA((3,)), pltpu.SemaphoreType.DMA((3,)), pltpu.SemaphoreType.DMA],
    )(pbuf)


def _rs_join_halves(half):
    R, C = half.shape

    def body(h_ref, o_ref, send, recv, loc):
        x, y, c, _ = _place()
        lc = pltpu.make_async_copy(h_ref, o_ref.at[c], loc)
        lc.start()
        cp = pltpu.make_async_remote_copy(src_ref=h_ref, dst_ref=o_ref.at[c], send_sem=send, recv_sem=recv,
                                          device_id=(x, y, 1 - c), device_id_type=MESH)
        cp.start()
        cp.wait()
        lc.wait()

    return pl.pallas_call(
        body, name="rs_join_halves", in_specs=[ANY], out_specs=ANY,
        out_shape=jax.ShapeDtypeStruct((2, R, C), F32),
        scratch_shapes=[pltpu.SemaphoreType.DMA, pltpu.SemaphoreType.DMA, pltpu.SemaphoreType.DMA],
    )(half)


def _all_reduce_small(v):
    R, C = v.shape

    def body(v_ref, o_ref, g_ref, send, recv, loc):
        x, y, c, chips = _place()
        me, sibling = (x, y, c), (x, y, 1 - c)

        def rows(px, py, pc):
            return g_ref.at[4 * px + 2 * py + pc]

        def copy(k, block, to, src=None):
            return pltpu.make_async_remote_copy(
                src_ref=rows(*block) if src is None else src, dst_ref=rows(*block),
                send_sem=send.at[k], recv_sem=recv.at[k], device_id=to, device_id_type=MESH)

        mine = pltpu.make_async_copy(v_ref, rows(*me), loc)
        mine.start()
        first = [copy(0, me, sibling, src=v_ref)]
        first += [copy(1 + j, me, (*chip, c), src=v_ref) for j, chip in enumerate(chips)]
        for cp in first:
            cp.start()
        passed = [copy(4 + j, (*chip, c), sibling) for j, chip in enumerate(chips)]
        for j, chip in enumerate(chips):
            copy(1 + j, (*chip, c), me).wait_recv()
            passed[j].start()
        copy(0, sibling, me).wait_recv()
        for j, chip in enumerate(chips):
            copy(4 + j, (*chip, 1 - c), me).wait_recv()
        for cp in first + passed:
            cp.wait_send()
        mine.wait()
        acc = g_ref[0]
        for d in range(1, 8):
            acc = acc + g_ref[d]
        o_ref[...] = acc

    vm = pl.BlockSpec(memory_space=pltpu.VMEM)
    return pl.pallas_call(
        body, name="all_reduce_small", in_specs=[vm], out_specs=[vm, vm],
        out_shape=[jax.ShapeDtypeStruct((R, C), F32), jax.ShapeDtypeStruct((8, R, C), F32)],
        scratch_shapes=[pltpu.SemaphoreType.DMA((7,)), pltpu.SemaphoreType.DMA((7,)), pltpu.SemaphoreType.DMA],
    )(v)[0]


WEIGHTS = ['ffn1_norm', 'ffn1_w_gate', 'ffn1_w_up', 'ffn1_w_down', 'mix_norm', 'w_in', 'conv_w', 'conv_b', 'dt_bias',
           'a_log', 'd_skip', 'ssd_norm', 'sgu_ln_g', 'sgu_ln_b', 'sgu_w', 'sgu_b', 'w_out', 'ffn2_norm',
           'ffn2_w_gate', 'ffn2_w_up', 'ffn2_w_down', 'final_norm']
SHARDED = ['ffn1_w_gate', 'ffn1_w_up', 'ffn1_w_down', 'w_in', 'conv_w', 'w_out', 'ffn2_w_gate', 'ffn2_w_up',
           'ffn2_w_down']
SMALL = [n for n in WEIGHTS if n not in SHARDED]
RS_COLS = 1024
DEPTH = 2


def _pack_w_in(w):
    return jnp.concatenate([w[..., 0:1152], w[..., 1536:2432], w[..., 1152:1536],
                            jnp.repeat(w[..., 2432:2438], HEAD, axis=-1), w[..., 2438:2950]], axis=-1)


def _unpack_w_in(dq, ds, du):
    return jnp.concatenate([dq, ds[:, 896:1280], ds[:, 0:896], ds[:, 1280::HEAD], du], axis=-1)


def _ffn_fwd(x, g, wg, wu, wd):
    hb = _rms_fwd(x, g)
    G, U, A = _ffn_up(hb, wg, wu)
    return _ffn_down(A, wd, x), (x, hb, G, U, A)


def _ffn_bwd(dxo, saved, g, wg, wu, wd):
    x, hb, G, U, A = saved
    dG, dU = _ffn_bwd_act(dxo, wd, G, U)
    dwd = _ffn_bwd_wd(A, dxo)
    dwg, dwu = _ffn_bwd_wgu(hb, dG, dU)
    dh = _ffn_bwd_dh(dG, dU, wg, wu)
    dx, dg = _rms_bwd(x, g, dh, dxo)
    return dx, dg, dwg, dwu, dwd


def _mix_fwd(x, P):
    hb = _rms_fwd(x, P["mix_norm"])
    qkv = _mm_nn(hb, P["w_qkv"], out_dtype=BF)
    sin = _mm_nn(hb, P["w_ssd"])
    uv = _mm_nn(hb, P["w_uv"])
    o1, l1 = _attn_fwd(qkv, 1)
    o2, l2 = _attn_fwd(qkv, 4)
    o3, l3 = _attn_fwd(qkv, 16)
    y_att, lse = _attn_combine(o1, o2, o3, l1, l2, l3)
    y_ssd, hprev = _ssd_fwd(sin, *P["ssd"])
    y_sgu = _sgu_fwd(uv, *P["sgu"])
    ycat = jnp.concatenate([y_att, y_ssd, y_sgu], axis=1).astype(BF)
    return _mm_nn(ycat, P["w_out"], res=x), (x, hb, qkv, sin, uv, y_att, lse, hprev, ycat)


def _mix_bwd(dxo, saved, P):
    x, hb, qkv, sin, uv, y_att, lse, hprev, ycat = saved
    dycat = _mm_nt(dxo, P["w_out"])
    dwout = _mm_tn(ycat, dxo)
    dy_att, dy_ssd, dy_sgu = dycat[:, 0:ATT_W], dycat[:, ATT_W:ATT_W + SSD_W], dycat[:, ATT_W + SSD_W:]
    dqkv = _sum_branches([_attn_bwd(qkv, dy_att, y_att, lse, d) for d in DILATIONS])
    dsin, dcw, dcb, dvec = _ssd_bwd(sin, hprev, dy_ssd, *P["ssd"])
    duv, dsw, dsbias, dln = _sgu_bwd(uv, dy_sgu, *P["sgu"])
    dwin = _unpack_w_in(_mm_tn(hb, dqkv), _mm_tn(hb, dsin), _mm_tn(hb, duv))
    dh = _mm_nt(dqkv, P["w_qkv"])
    dh = _mm_nt(dsin, P["w_ssd"], res=dh)
    dh = _mm_nt(duv, P["w_uv"], res=dh)
    dx, dg = _rms_bwd(x, P["mix_norm"], dh, dxo)
    grads = dict(
        mix_norm=dg[0], w_in=dwin, conv_w=dcw, conv_b=dcb[0], dt_bias=dvec[0, ::HEAD], a_log=dvec[1, ::HEAD],
        d_skip=jnp.sum(dvec[2].reshape(6, HEAD), axis=-1), ssd_norm=dvec[3], sgu_ln_g=dln[0], sgu_ln_b=dln[1],
        sgu_w=dsw, sgu_b=jnp.sum(dsbias.reshape(CHUNK, 4, HEAD), axis=-1).T, w_out=dwout)
    return dx, grads


def _shard_major(name, g):
    if name == "conv_w":
        return g.reshape(4, 4, 224).transpose(1, 0, 2).reshape(4, -1)
    return g.reshape(4, -1)


def kernel(x, ffn1_norm, ffn1_w_gate, ffn1_w_up, ffn1_w_down, mix_norm, w_in, conv_w, conv_b, dt_bias, a_log, d_skip, ssd_norm, sgu_ln_g, sgu_ln_b, sgu_w, sgu_b, w_out, ffn2_norm, ffn2_w_gate, ffn2_w_up, ffn2_w_down, final_norm, loss_target, m_ffn1_norm, m_ffn1_w_gate, m_ffn1_w_up, m_ffn1_w_down, m_mix_norm, m_w_in, m_conv_w, m_conv_b, m_dt_bias, m_a_log, m_d_skip, m_ssd_norm, m_sgu_ln_g, m_sgu_ln_b, m_sgu_w, m_sgu_b, m_w_out, m_ffn2_norm, m_ffn2_w_gate, m_ffn2_w_up, m_ffn2_w_down, m_final_norm, v_ffn1_norm, v_ffn1_w_gate, v_ffn1_w_up, v_ffn1_w_down, v_mix_norm, v_w_in, v_conv_w, v_conv_b, v_dt_bias, v_a_log, v_d_skip, v_ssd_norm, v_sgu_ln_g, v_sgu_ln_b, v_sgu_w, v_sgu_b, v_w_out, v_ffn2_norm, v_ffn2_w_gate, v_ffn2_w_up, v_ffn2_w_down, v_final_norm):
    given = dict(x=x, ffn1_norm=ffn1_norm, ffn1_w_gate=ffn1_w_gate, ffn1_w_up=ffn1_w_up, ffn1_w_down=ffn1_w_down, mix_norm=mix_norm, w_in=w_in, conv_w=conv_w, conv_b=conv_b, dt_bias=dt_bias, a_log=a_log, d_skip=d_skip, ssd_norm=ssd_norm, sgu_ln_g=sgu_ln_g, sgu_ln_b=sgu_ln_b, sgu_w=sgu_w, sgu_b=sgu_b, w_out=w_out, ffn2_norm=ffn2_norm, ffn2_w_gate=ffn2_w_gate, ffn2_w_up=ffn2_w_up, ffn2_w_down=ffn2_w_down, final_norm=final_norm, loss_target=loss_target, m_ffn1_norm=m_ffn1_norm, m_ffn1_w_gate=m_ffn1_w_gate, m_ffn1_w_up=m_ffn1_w_up, m_ffn1_w_down=m_ffn1_w_down, m_mix_norm=m_mix_norm, m_w_in=m_w_in, m_conv_w=m_conv_w, m_conv_b=m_conv_b, m_dt_bias=m_dt_bias, m_a_log=m_a_log, m_d_skip=m_d_skip, m_ssd_norm=m_ssd_norm, m_sgu_ln_g=m_sgu_ln_g, m_sgu_ln_b=m_sgu_ln_b, m_sgu_w=m_sgu_w, m_sgu_b=m_sgu_b, m_w_out=m_w_out, m_ffn2_norm=m_ffn2_norm, m_ffn2_w_gate=m_ffn2_w_gate, m_ffn2_w_up=m_ffn2_w_up, m_ffn2_w_down=m_ffn2_w_down, m_final_norm=m_final_norm, v_ffn1_norm=v_ffn1_norm, v_ffn1_w_gate=v_ffn1_w_gate, v_ffn1_w_up=v_ffn1_w_up, v_ffn1_w_down=v_ffn1_w_down, v_mix_norm=v_mix_norm, v_w_in=v_w_in, v_conv_w=v_conv_w, v_conv_b=v_conv_b, v_dt_bias=v_dt_bias, v_a_log=v_a_log, v_d_skip=v_d_skip, v_ssd_norm=v_ssd_norm, v_sgu_ln_g=v_sgu_ln_g, v_sgu_ln_b=v_sgu_ln_b, v_sgu_w=v_sgu_w, v_sgu_b=v_sgu_b, v_w_out=v_w_out, v_ffn2_norm=v_ffn2_norm, v_ffn2_w_gate=v_ffn2_w_gate, v_ffn2_w_up=v_ffn2_w_up, v_ffn2_w_down=v_ffn2_w_down, v_final_norm=v_final_norm)
    T = given["x"].shape[0] * given["x"].shape[1]
    D = given["x"].shape[2]
    x0 = given["x"].reshape(T, D)
    tgt = given["loss_target"].reshape(T, D)
    c = lax.axis_index("c")

    plain = [n for n in SHARDED if n not in ("w_in", "conv_w")]
    send = [given[n].astype(BF) for n in plain] + [_pack_w_in(given["w_in"]).astype(BF), given["conv_w"]]
    full = dict(zip(plain + ["w_in", "conv_w"], _gather_shards(send)))

    def layer_params(i):
        win = full["w_in"][:, i].reshape(D, W_QKV + W_SSD + W_UV)
        rep = lambda v: jnp.repeat(v, HEAD)[None]
        ssd = (full["conv_w"][:, i].transpose(1, 0, 2).reshape(4, SSD_CONV_DIM), given["conv_b"][i][None],
               rep(given["dt_bias"][i]), rep(given["a_log"][i]), rep(given["d_skip"][i]), given["ssd_norm"][i][None])
        sgu = (given["sgu_ln_g"][i][None], given["sgu_ln_b"][i][None], given["sgu_w"][i],
               jnp.repeat(given["sgu_b"][i].T, HEAD, axis=1))
        ffn = lambda p: (given[p + "_norm"][i][None], full[p + "_w_gate"][:, i], full[p + "_w_up"][:, i],
                         full[p + "_w_down"][:, i])
        return dict(ffn1=ffn("ffn1"), ffn2=ffn("ffn2"), mix_norm=given["mix_norm"][i][None], w_qkv=win[:, 0:W_QKV],
                    w_ssd=win[:, W_QKV:W_QKV + W_SSD], w_uv=win[:, W_QKV + W_SSD:],
                    w_out=full["w_out"][:, i].reshape(-1, D), ssd=ssd, sgu=sgu)

    x = x0
    tape = []
    for i in range(DEPTH):
        P = layer_params(i)
        x, s1 = _ffn_fwd(x, *P["ffn1"])
        x, s2 = _mix_fwd(x, P)
        x, s3 = _ffn_fwd(x, *P["ffn2"])
        tape.append((P, s1, s2, s3))
    loss_part, dx, dgf = _final_loss(x, given["final_norm"][None], tgt)

    grads = [dict() for _ in range(DEPTH)]
    for i in reversed(range(DEPTH)):
        P, s1, s2, s3 = tape[i]
        g = grads[i]
        dx, dn2, g["ffn2_w_gate"], g["ffn2_w_up"], g["ffn2_w_down"] = _ffn_bwd(dx, s3, *P["ffn2"])
        dx, gm = _mix_bwd(dx, s2, P)
        g.update(gm)
        dx, dn1, g["ffn1_w_gate"], g["ffn1_w_up"], g["ffn1_w_down"] = _ffn_bwd(dx, s1, *P["ffn1"])
        g["ffn1_norm"], g["ffn2_norm"] = dn1[0], dn2[0]
    grad_x = dx.reshape(given["x"].shape)

    pieces = [_shard_major(n, grads[i][n]) for n in SHARDED for i in range(DEPTH)]
    n_shard = sum(p.shape[1] for p in pieces)
    rows_half = -(-n_shard // (2 * RS_COLS * 8)) * 8
    pad = 2 * rows_half * RS_COLS - n_shard
    gbuf = jnp.concatenate(pieces + [jnp.zeros((4, pad), F32)], axis=1).reshape(4, 2, rows_half, RS_COLS)
    pair = _pair_add(gbuf, _rs_to_sibling(gbuf), c)
    half = _chip_sum(_rs_to_chips(pair))
    gshard = _rs_join_halves(half).reshape(-1)

    order = [n for n in SMALL if n != "final_norm"] + ["final_norm"]
    small = [jnp.stack([grads[i][n] for i in range(DEPTH)]) for n in order[:-1]] + [dgf[0], loss_part[0, 0:1]]
    n_small = sum(s.size for s in small)
    rows_small = -(-n_small // (128 * 8)) * 8

    def flat(arrs):
        fill = rows_small * 128 - sum(a.size for a in arrs)
        return jnp.concatenate([a.reshape(-1) for a in arrs] + [jnp.zeros((fill,), F32)]).reshape(rows_small, 128)

    gsmall = _all_reduce_small(flat(small)).reshape(-1)

    grad_w = {}
    off = 0
    for n in SHARDED:
        size = given[n].size
        grad_w[n] = gshard[off:off + size].reshape(given[n].shape)
        off += size
    off = 0
    for n in order:
        size = given[n].size
        grad_w[n] = gsmall[off:off + size].reshape(given[n].shape)
        off += size
    loss = gsmall[off]

    delta, new_m, new_v = {}, {}, {}
    for n in SHARDED:
        shp = given[n].shape
        two_d = (shp[0] * shp[1], shp[2])
        d, m2, v2 = _adamw(*[a.reshape(two_d) for a in (given[n], grad_w[n], given["m_" + n], given["v_" + n])])
        delta[n], new_m[n], new_v[n] = d.reshape(shp), m2.reshape(shp), v2.reshape(shp)
    packed = [flat([src[pre + n] for n in order])
              for src, pre in ((given, ""), (grad_w, ""), (given, "m_"), (given, "v_"))]
    outs = [o.reshape(-1) for o in _adamw(*packed)]
    off = 0
    for n in order:
        size = given[n].size
        for dst, o in zip((delta, new_m, new_v), outs):
            dst[n] = o[off:off + size].reshape(given[n].shape)
        off += size

    return (loss, grad_x, *[grad_w[n] for n in WEIGHTS], *[delta[n] for n in WEIGHTS],
            *[new_m[n] for n in WEIGHTS], *[new_v[n] for n in WEIGHTS])
```

```python
import functools
import math

import jax
import jax.numpy as jnp
from jax import lax
from jax.experimental import pallas as pl
from jax.experimental.pallas import tpu as pltpu

F32 = jnp.float32
BF = jnp.bfloat16

RMS_EPS = 1e-6
LN_EPS = 1e-5
SEQ = 2048
CHUNK = 128
N_CHUNK = SEQ // CHUNK
ATT_W = 384
HEAD = 64
SSD_W = 384
SSD_CONV_DIM = 896
SSD_STATE = 128
SGU_W = 256
DILATIONS = (1, 4, 16)
W_QKV = 3 * ATT_W
W_SSD = SSD_CONV_DIM + SSD_W + SSD_W
W_UV = 2 * SGU_W
ADAM_LR = 0.001
ADAM_B1 = 0.9
ADAM_B2 = 0.999
ADAM_EPS = 1e-08
ADAM_WD = 0.01
ADAM_STEP = 10
NEG = -1e30


def _dot(a, b):
    return jnp.dot(a, b, preferred_element_type=F32)


def _dot_nt(a, b):
    return lax.dot_general(a, b, (((1,), (1,)), ((), ())), preferred_element_type=F32)


def _dot_tn(a, b):
    return lax.dot_general(a, b, (((0,), (0,)), ((), ())), preferred_element_type=F32)


def _sigmoid(x):
    return 1.0 / (1.0 + jnp.exp(-x))


def _call(body, *, name, grid, in_specs, out_specs, out_shape, scratch=(), sem=None):
    return pl.pallas_call(
        body, name=name, grid=grid, in_specs=in_specs, out_specs=out_specs, out_shape=out_shape,
        scratch_shapes=list(scratch),
        compiler_params=pltpu.CompilerParams(dimension_semantics=sem),
    )


def _tile(n, want):
    t = min(n, want)
    while n % t:
        t //= 2
    return t


def _rms_fwd(x, g):
    T, D = x.shape
    tm = _tile(T, 512)

    def body(x_ref, g_ref, h_ref):
        xf = x_ref[...]
        r = lax.rsqrt(jnp.mean(xf * xf, axis=-1, keepdims=True) + RMS_EPS)
        h_ref[...] = (xf * r * g_ref[...]).astype(BF)

    return _call(body, name="rms_fwd", grid=(T // tm,),
                 in_specs=[pl.BlockSpec((tm, D), lambda i: (i, 0)), pl.BlockSpec((1, D), lambda i: (0, 0))],
                 out_specs=pl.BlockSpec((tm, D), lambda i: (i, 0)),
                 out_shape=jax.ShapeDtypeStruct((T, D), BF), sem=("parallel",))(x, g)


def _rms_bwd(x, g, dh, dres):
    T, D = x.shape
    tm = _tile(T, 512)

    def body(x_ref, g_ref, dh_ref, dr_ref, dx_ref, dg_ref):
        @pl.when(pl.program_id(0) == 0)
        def _():
            dg_ref[...] = jnp.zeros_like(dg_ref)

        xf = x_ref[...]
        r = lax.rsqrt(jnp.mean(xf * xf, axis=-1, keepdims=True) + RMS_EPS)
        dh_ = dh_ref[...]
        u = dh_ * g_ref[...]
        mu = jnp.mean(u * xf, axis=-1, keepdims=True)
        dx_ref[...] = dr_ref[...] + r * (u - xf * (r * r * mu))
        dg_ref[...] += jnp.sum(dh_ * xf * r, axis=0, keepdims=True)

    row = pl.BlockSpec((tm, D), lambda i: (i, 0))
    vec = pl.BlockSpec((1, D), lambda i: (0, 0))
    return _call(body, name="rms_bwd", grid=(T // tm,), in_specs=[row, vec, row, row], out_specs=[row, vec],
                 out_shape=[jax.ShapeDtypeStruct((T, D), F32), jax.ShapeDtypeStruct((1, D), F32)],
                 sem=("arbitrary",))(x, g, dh, dres)


def _final_loss(x, g, tgt):
    T, D = x.shape
    tm = _tile(T, 512)

    def body(x_ref, g_ref, t_ref, l_ref, dx_ref, dg_ref):
        @pl.when(pl.program_id(0) == 0)
        def _():
            dg_ref[...] = jnp.zeros_like(dg_ref)
            l_ref[...] = jnp.zeros_like(l_ref)

        xf = x_ref[...]
        gg = g_ref[...]
        r = lax.rsqrt(jnp.mean(xf * xf, axis=-1, keepdims=True) + RMS_EPS)
        xn = xf * r
        e = xn * gg - t_ref[...]
        part = 0.5 * jnp.sum(jnp.mean(e * e, axis=-1, keepdims=True), axis=0, keepdims=True)
        l_ref[...] += jnp.broadcast_to(part, l_ref.shape)
        dy = e * (1.0 / D)
        u = dy * gg
        mu = jnp.mean(u * xf, axis=-1, keepdims=True)
        dx_ref[...] = r * (u - xf * (r * r * mu))
        dg_ref[...] += jnp.sum(dy * xn, axis=0, keepdims=True)

    row = pl.BlockSpec((tm, D), lambda i: (i, 0))
    vec = pl.BlockSpec((1, D), lambda i: (0, 0))
    lsp = pl.BlockSpec((1, 128), lambda i: (0, 0))
    return _call(body, name="final_loss", grid=(T // tm,), in_specs=[row, vec, row], out_specs=[lsp, row, vec],
                 out_shape=[jax.ShapeDtypeStruct((1, 128), F32), jax.ShapeDtypeStruct((T, D), F32),
                            jax.ShapeDtypeStruct((1, D), F32)],
                 sem=("arbitrary",))(x, g, tgt)


def _ffn_up(hb, wg, wu):
    T, D = hb.shape
    NS, _, Fs = wg.shape
    tm = _tile(T, 512)

    def body(h_ref, wg_ref, wu_ref, g_ref, u_ref, a_ref):
        h = h_ref[...]
        g = _dot(h, wg_ref[...])
        u = _dot(h, wu_ref[...])
        g_ref[...] = g.astype(BF)
        u_ref[...] = u.astype(BF)
        a_ref[...] = (g * _sigmoid(g) * u).astype(BF)

    w = pl.BlockSpec((None, D, Fs), lambda j, i: (j, 0, 0))
    o = pl.BlockSpec((None, tm, Fs), lambda j, i: (j, i, 0))
    sh = jax.ShapeDtypeStruct((NS, T, Fs), BF)
    return _call(body, name="ffn_up", grid=(NS, T // tm),
                 in_specs=[pl.BlockSpec((tm, D), lambda j, i: (i, 0)), w, w], out_specs=[o, o, o],
                 out_shape=[sh, sh, sh], sem=("parallel", "parallel"))(hb, wg, wu)


def _ffn_down(a, wd, x):
    NS, T, Fs = a.shape
    D = wd.shape[2]
    tm = _tile(T, 512)

    def body(a_ref, w_ref, x_ref, o_ref, acc):
        j = pl.program_id(1)

        @pl.when(j == 0)
        def _():
            acc[...] = jnp.zeros_like(acc)

        acc[...] += _dot(a_ref[...], w_ref[...])

        @pl.when(j == NS - 1)
        def _():
            o_ref[...] = x_ref[...] + 0.5 * acc[...]

    row = pl.BlockSpec((tm, D), lambda i, j: (i, 0))
    return _call(body, name="ffn_down", grid=(T // tm, NS),
                 in_specs=[pl.BlockSpec((None, tm, Fs), lambda i, j: (j, i, 0)),
                           pl.BlockSpec((None, Fs, D), lambda i, j: (j, 0, 0)), row],
                 out_specs=row, out_shape=jax.ShapeDtypeStruct((T, D), F32),
                 scratch=[pltpu.VMEM((tm, D), F32)], sem=("parallel", "arbitrary"))(a, wd, x)


def _ffn_bwd_act(dxo, wd, g, u):
    NS, T, Fs = g.shape
    D = dxo.shape[1]
    tm = _tile(T, 512)

    def body(dx_ref, w_ref, g_ref, u_ref, dg_ref, du_ref):
        dy = (0.5 * dx_ref[...]).astype(BF)
        da = _dot_nt(dy, w_ref[...])
        gf = g_ref[...].astype(F32)
        uf = u_ref[...].astype(F32)
        sg = _sigmoid(gf)
        dg_ref[...] = (da * uf * (sg * (1.0 + gf * (1.0 - sg)))).astype(BF)
        du_ref[...] = (da * gf * sg).astype(BF)

    o = pl.BlockSpec((None, tm, Fs), lambda j, i: (j, i, 0))
    sh = jax.ShapeDtypeStruct((NS, T, Fs), BF)
    return _call(body, name="ffn_bwd_act", grid=(NS, T // tm),
                 in_specs=[pl.BlockSpec((tm, D), lambda j, i: (i, 0)),
                           pl.BlockSpec((None, Fs, D), lambda j, i: (j, 0, 0)), o, o],
                 out_specs=[o, o], out_shape=[sh, sh], sem=("parallel", "parallel"))(dxo, wd, g, u)


def _ffn_bwd_wd(a, dxo):
    NS, T, Fs = a.shape
    D = dxo.shape[1]
    tk = _tile(T, 512)

    def body(a_ref, dx_ref, o_ref):
        @pl.when(pl.program_id(1) == 0)
        def _():
            o_ref[...] = jnp.zeros_like(o_ref)

        o_ref[...] += _dot_tn(a_ref[...], (0.5 * dx_ref[...]).astype(BF))

    return _call(body, name="ffn_bwd_wd", grid=(NS, T // tk),
                 in_specs=[pl.BlockSpec((None, tk, Fs), lambda j, k: (j, k, 0)),
                           pl.BlockSpec((tk, D), lambda j, k: (k, 0))],
                 out_specs=pl.BlockSpec((None, Fs, D), lambda j, k: (j, 0, 0)),
                 out_shape=jax.ShapeDtypeStruct((NS, Fs, D), F32), sem=("parallel", "arbitrary"))(a, dxo)


def _ffn_bwd_wgu(hb, dg, du):
    NS, T, Fs = dg.shape
    D = hb.shape[1]
    tk = _tile(T, 512)

    def body(h_ref, dg_ref, du_ref, og_ref, ou_ref):
        @pl.when(pl.program_id(1) == 0)
        def _():
            og_ref[...] = jnp.zeros_like(og_ref)
            ou_ref[...] = jnp.zeros_like(ou_ref)

        h = h_ref[...]
        og_ref[...] += _dot_tn(h, dg_ref[...])
        ou_ref[...] += _dot_tn(h, du_ref[...])

    d = pl.BlockSpec((None, tk, Fs), lambda j, k: (j, k, 0))
    o = pl.BlockSpec((None, D, Fs), lambda j, k: (j, 0, 0))
    sh = jax.ShapeDtypeStruct((NS, D, Fs), F32)
    return _call(body, name="ffn_bwd_wgu", grid=(NS, T // tk),
                 in_specs=[pl.BlockSpec((tk, D), lambda j, k: (k, 0)), d, d], out_specs=[o, o],
                 out_shape=[sh, sh], sem=("parallel", "arbitrary"))(hb, dg, du)


def _ffn_bwd_dh(dg, du, wg, wu):
    NS, T, Fs = dg.shape
    D = wg.shape[1]
    tm = _tile(T, 512)

    def body(dg_ref, du_ref, wg_ref, wu_ref, o_ref, acc):
        j = pl.program_id(1)

        @pl.when(j == 0)
        def _():
            acc[...] = jnp.zeros_like(acc)

        acc[...] += _dot_nt(dg_ref[...], wg_ref[...]) + _dot_nt(du_ref[...], wu_ref[...])

        @pl.when(j == NS - 1)
        def _():
            o_ref[...] = acc[...]

    d = pl.BlockSpec((None, tm, Fs), lambda i, j: (j, i, 0))
    w = pl.BlockSpec((None, D, Fs), lambda i, j: (j, 0, 0))
    return _call(body, name="ffn_bwd_dh", grid=(T // tm, NS), in_specs=[d, d, w, w],
                 out_specs=pl.BlockSpec((tm, D), lambda i, j: (i, 0)),
                 out_shape=jax.ShapeDtypeStruct((T, D), F32),
                 scratch=[pltpu.VMEM((tm, D), F32)], sem=("parallel", "arbitrary"))(dg, du, wg, wu)


def _mm_nn(a, b, res=None, out_dtype=F32):
    T, K = a.shape
    N = b.shape[1]
    tm = _tile(T, 512)
    tn = N if N <= 2048 else _tile(N, 1024)

    def body(*refs):
        if res is None:
            a_ref, b_ref, o_ref = refs
            o_ref[...] = _dot(a_ref[...], b_ref[...]).astype(out_dtype)
        else:
            a_ref, b_ref, r_ref, o_ref = refs
            o_ref[...] = (r_ref[...] + _dot(a_ref[...], b_ref[...])).astype(out_dtype)

    o = pl.BlockSpec((tm, tn), lambda i, j: (i, j))
    ins = [pl.BlockSpec((tm, K), lambda i, j: (i, 0)), pl.BlockSpec((K, tn), lambda i, j: (0, j))]
    args = [a, b]
    if res is not None:
        ins.append(o)
        args.append(res)
    return _call(body, name="mm_nn", grid=(T // tm, N // tn), in_specs=ins, out_specs=o,
                 out_shape=jax.ShapeDtypeStruct((T, N), out_dtype), sem=("parallel", "parallel"))(*args)


def _mm_nt(a, b, res=None):
    T, K = a.shape
    N = b.shape[0]
    tm = _tile(T, 512)

    def body(*refs):
        if res is None:
            a_ref, b_ref, o_ref = refs
            o_ref[...] = _dot_nt(a_ref[...].astype(BF), b_ref[...])
        else:
            a_ref, b_ref, r_ref, o_ref = refs
            o_ref[...] = r_ref[...] + _dot_nt(a_ref[...].astype(BF), b_ref[...])

    o = pl.BlockSpec((tm, N), lambda i: (i, 0))
    ins = [pl.BlockSpec((tm, K), lambda i: (i, 0)), pl.BlockSpec((N, K), lambda i: (0, 0))]
    args = [a, b]
    if res is not None:
        ins.append(o)
        args.append(res)
    return _call(body, name="mm_nt", grid=(T // tm,), in_specs=ins, out_specs=o,
                 out_shape=jax.ShapeDtypeStruct((T, N), F32), sem=("parallel",))(*args)


def _mm_tn(a, b):
    T, M = a.shape
    N = b.shape[1]
    tk = _tile(T, 512)
    tmm = _tile(M, 512)

    def body(a_ref, b_ref, o_ref):
        @pl.when(pl.program_id(1) == 0)
        def _():
            o_ref[...] = jnp.zeros_like(o_ref)

        o_ref[...] += _dot_tn(a_ref[...].astype(BF), b_ref[...].astype(BF))

    return _call(body, name="mm_tn", grid=(M // tmm, T // tk),
                 in_specs=[pl.BlockSpec((tk, tmm), lambda i, k: (k, i)), pl.BlockSpec((tk, N), lambda i, k: (k, 0))],
                 out_specs=pl.BlockSpec((tmm, N), lambda i, k: (i, 0)),
                 out_shape=jax.ShapeDtypeStruct((M, N), F32), sem=("parallel", "arbitrary"))(a, b)


def _lane_mask(e, width=128):
    return (lax.broadcasted_iota(jnp.int32, (1, width), 1) // HEAD) == e


def _band_mask(n):
    qi = lax.broadcasted_iota(jnp.int32, (CHUNK, 2 * CHUNK), 0)
    kj = lax.broadcasted_iota(jnp.int32, (CHUNK, 2 * CHUNK), 1)
    dist = qi + CHUNK - kj
    return (dist >= 0) & (dist <= CHUNK) & ((kj >= CHUNK) | (n > 0))


def _prev_cur(ref, n):
    cur = pl.multiple_of(n * CHUNK, CHUNK)
    prv = pl.multiple_of(jnp.maximum(n - 1, 0) * CHUNK, CHUNK)
    return jnp.concatenate([ref[pl.ds(prv, CHUNK), :], ref[pl.ds(cur, CHUNK), :]], axis=0), prv, cur


def _attn_fwd(qkv, dil):
    T = qkv.shape[0]
    B, L = T // SEQ, SEQ // dil
    nb = L // CHUNK
    scale = HEAD ** -0.5

    def body(q_ref, k_ref, v_ref, o_ref, l_ref):
        n = pl.program_id(2)
        q = q_ref[...]
        kk, _, _ = _prev_cur(k_ref, n)
        vv, _, _ = _prev_cur(v_ref, n)
        mask = _band_mask(n)
        for t in range(ATT_W // 128):
            sl = slice(128 * t, 128 * (t + 1))
            qt, kt, vt = q[:, sl], kk[:, sl], vv[:, sl]
            o_pair = jnp.zeros((CHUNK, 128), F32)
            l_pair = jnp.zeros((CHUNK, 128), F32)
            for e in range(2):
                lm = _lane_mask(e)
                s = _dot_nt(jnp.where(lm, qt, jnp.zeros_like(qt)), kt) * scale
                s = jnp.where(mask, s, NEG)
                m = jnp.max(s, axis=-1, keepdims=True)
                p = jnp.exp(s - m)
                den = jnp.sum(p, axis=-1, keepdims=True)
                o = _dot(p.astype(BF), vt) / den
                o_pair = jnp.where(lm, o, o_pair)
                l_pair = jnp.where(lm, m + jnp.log(den), l_pair)
            o_ref[:, sl] = o_pair
            l_ref[:, sl] = l_pair

    qv = qkv.reshape(B, L, dil * W_QKV)
    o = pl.BlockSpec((None, CHUNK, ATT_W), lambda b, r, n: (b, n, r))
    sh = jax.ShapeDtypeStruct((B, L, dil * ATT_W), F32)
    out, lse = _call(
        body, name=f"attn_fwd_d{dil}", grid=(B, dil, nb),
        in_specs=[pl.BlockSpec((None, CHUNK, ATT_W), lambda b, r, n: (b, n, 3 * r)),
                  pl.BlockSpec((None, L, ATT_W), lambda b, r, n: (b, 0, 3 * r + 1)),
                  pl.BlockSpec((None, L, ATT_W), lambda b, r, n: (b, 0, 3 * r + 2))],
        out_specs=[o, o], out_shape=[sh, sh], sem=("parallel", "parallel", "parallel"))(qv, qv, qv)
    return out.reshape(T, ATT_W), lse.reshape(T, ATT_W)


def _attn_combine(o1, o2, o3, l1, l2, l3):
    T = o1.shape[0]
    tm = _tile(T, 512)

    def body(o1_ref, o2_ref, o3_ref, l1_ref, l2_ref, l3_ref, y_ref, l_ref):
        a, b, c = l1_ref[...], l2_ref[...], l3_ref[...]
        m = jnp.maximum(jnp.maximum(a, b), c)
        ea, eb, ec = jnp.exp(a - m), jnp.exp(b - m), jnp.exp(c - m)
        z = ea + eb + ec
        y_ref[...] = (ea * o1_ref[...] + eb * o2_ref[...] + ec * o3_ref[...]) / z
        l_ref[...] = m + jnp.log(z)

    row = pl.BlockSpec((tm, ATT_W), lambda i: (i, 0))
    sh = jax.ShapeDtypeStruct((T, ATT_W), F32)
    return _call(body, name="attn_combine", grid=(T // tm,), in_specs=[row] * 6, out_specs=[row, row],
                 out_shape=[sh, sh], sem=("parallel",))(o1, o2, o3, l1, l2, l3)


def _attn_bwd(qkv, do, out, lse, dil):
    T = qkv.shape[0]
    B, L = T // SEQ, SEQ // dil
    nb = L // CHUNK
    scale = HEAD ** -0.5

    def body(q_ref, k_ref, v_ref, do_ref, out_ref, lse_ref, dq_ref, dk_ref, dv_ref):
        n = pl.program_id(2)

        @pl.when(n == 0)
        def _():
            dk_ref[...] = jnp.zeros_like(dk_ref)
            dv_ref[...] = jnp.zeros_like(dv_ref)

        q = q_ref[...]
        kk, prv, cur = _prev_cur(k_ref, n)
        vv, _, _ = _prev_cur(v_ref, n)
        mask = _band_mask(n)
        do_ = do_ref[...]
        dlt = do_ * out_ref[...]
        ls = lse_ref[...]
        for t in range(ATT_W // 128):
            sl = slice(128 * t, 128 * (t + 1))
            qt, kt, vt = q[:, sl], kk[:, sl], vv[:, sl]
            dq_pair = jnp.zeros((CHUNK, 128), F32)
            dk_acc = jnp.zeros((2 * CHUNK, 128), F32)
            dv_acc = jnp.zeros((2 * CHUNK, 128), F32)
            for e in range(2):
                lm = _lane_mask(e)
                qm = jnp.where(lm, qt, jnp.zeros_like(qt))
                s = _dot_nt(qm, kt) * scale
                lse_col = ls[:, 128 * t + HEAD * e:128 * t + HEAD * e + 1]
                p = jnp.exp(jnp.where(mask, s - lse_col, NEG))
                dom = jnp.where(lm, do_[:, sl], 0.0).astype(BF)
                dv_acc += _dot_tn(p.astype(BF), dom)
                dp = _dot_nt(dom, vt)
                delta = jnp.sum(jnp.where(lm, dlt[:, sl], 0.0), axis=-1, keepdims=True)
                ds = (p * (dp - delta) * scale).astype(BF)
                dq_pair += jnp.where(lm, _dot(ds, kt), 0.0)
                dk_acc += _dot_tn(ds, qm)
            dq_ref[:, sl] = dq_pair
            dk_ref[pl.ds(cur, CHUNK), sl] += dk_acc[CHUNK:]
            dk_ref[pl.ds(prv, CHUNK), sl] += dk_acc[:CHUNK]
            dv_ref[pl.ds(cur, CHUNK), sl] += dv_acc[CHUNK:]
            dv_ref[pl.ds(prv, CHUNK), sl] += dv_acc[:CHUNK]

    qv = qkv.reshape(B, L, dil * W_QKV)
    view = lambda a: a.reshape(B, L, dil * ATT_W)
    blk = pl.BlockSpec((None, CHUNK, ATT_W), lambda b, r, n: (b, n, r))
    whole = pl.BlockSpec((None, L, ATT_W), lambda b, r, n: (b, 0, r))
    sh = jax.ShapeDtypeStruct((B, L, dil * ATT_W), F32)
    dq, dk, dv = _call(
        body, name=f"attn_bwd_d{dil}", grid=(B, dil, nb),
        in_specs=[pl.BlockSpec((None, CHUNK, ATT_W), lambda b, r, n: (b, n, 3 * r)),
                  pl.BlockSpec((None, L, ATT_W), lambda b, r, n: (b, 0, 3 * r + 1)),
                  pl.BlockSpec((None, L, ATT_W), lambda b, r, n: (b, 0, 3 * r + 2)),
                  blk, blk, blk],
        out_specs=[blk, whole, whole], out_shape=[sh, sh, sh],
        sem=("parallel", "parallel", "arbitrary"))(qv, qv, qv, view(do), view(out), view(lse))
    return dq.reshape(T, ATT_W), dk.reshape(T, ATT_W), dv.reshape(T, ATT_W)


def _sum_branches(parts):
    T = parts[0][0].shape[0]
    tm = _tile(T, 512)

    def body(*refs):
        o_ref = refs[-1]
        for s in range(3):
            acc = refs[s][...] + refs[3 + s][...] + refs[6 + s][...]
            o_ref[:, ATT_W * s:ATT_W * (s + 1)] = acc.astype(BF)

    row = pl.BlockSpec((tm, ATT_W), lambda i: (i, 0))
    flat = [a for tr in parts for a in tr]
    return _call(body, name="attn_sum_branches", grid=(T // tm,), in_specs=[row] * 9,
                 out_specs=pl.BlockSpec((tm, W_QKV), lambda i: (i, 0)),
                 out_shape=jax.ShapeDtypeStruct((T, W_QKV), BF), sem=("parallel",))(*flat)


def _silu(x):
    return x * _sigmoid(x)


def _dsilu(x):
    s = _sigmoid(x)
    return s * (1.0 + x * (1.0 - s))


def _log1p(u):
    return jnp.where(u < 0.01, u * (1.0 - u * (0.5 - u * (1.0 / 3.0))), jnp.log(1.0 + u))


def _softplus(x):
    return jnp.maximum(x, 0.0) + _log1p(jnp.exp(-jnp.abs(x)))


def _cumsum_rows(x, reverse=False):
    n = x.shape[0]
    rows = lax.broadcasted_iota(jnp.int32, x.shape, 0)
    k = 1
    while k < n:
        if reverse:
            x = x + jnp.where(rows < n - k, pltpu.roll(x, n - k, 0), 0.0)
        else:
            x = x + jnp.where(rows >= k, pltpu.roll(x, k, 0), 0.0)
        k *= 2
    return x


def _tri():
    r = lax.broadcasted_iota(jnp.int32, (CHUNK, CHUNK), 0)
    c = lax.broadcasted_iota(jnp.int32, (CHUNK, CHUNK), 1)
    return r >= c


def _row_mask(e):
    return (lax.broadcasted_iota(jnp.int32, (128, 1), 0) // HEAD) == e


def _first_lane(e):
    return lax.broadcasted_iota(jnp.int32, (1, 128), 1) == HEAD * e


def _ssd_pre(x_ref, halo_ref, first, cw_ref, cb_ref, dtb_ref, al_ref, ext):
    row = x_ref[...]
    z = row[:, SSD_CONV_DIM:SSD_CONV_DIM + SSD_W]
    u = row[:, SSD_CONV_DIM + SSD_W:] + dtb_ref[...]
    ext[0:8, :] = jnp.where(first, 0.0, halo_ref[:, 0:SSD_CONV_DIM])
    ext[8:8 + CHUNK, :] = row[:, 0:SSD_CONV_DIM]
    xc = cb_ref[...]
    for j in range(4):
        xc = xc + cw_ref[j:j + 1, :] * ext[pl.ds(5 + j, CHUNK), :]
    xa = _silu(xc)
    dt = _softplus(u)
    a = dt * (-jnp.exp(al_ref[...]))
    A = _cumsum_rows(a)
    return dict(z=z, u=u, xc=xc, xs=xa[:, 0:SSD_W], Bm=xa[:, SSD_W:SSD_W + 256], Cm=xa[:, SSD_W + 256:],
                dt=dt, a=a, A=A, AT=A.T, eA=jnp.exp(A), wdec=jnp.exp(A[CHUNK - 1:CHUNK, :] - A),
                dtot=jnp.exp(A[CHUNK - 1:CHUNK, :]))


def _ssd_y(p, hp_ref, dskip):
    tri = _tri()
    X = p["xs"] * p["dt"]
    Bb = [p["Bm"][:, 128 * g:128 * (g + 1)].astype(BF) for g in range(2)]
    Cb = [p["Cm"][:, 128 * g:128 * (g + 1)].astype(BF) for g in range(2)]
    CB = [_dot_nt(Cb[g], Bb[g]) for g in range(2)]
    tiles = []
    for t in range(3):
        sl = slice(128 * t, 128 * (t + 1))
        hpb = hp_ref[sl, :].astype(BF)
        acc = jnp.zeros((CHUNK, 128), F32)
        for e in range(2):
            h = 2 * t + e
            g, col = h // 3, HEAD * h
            lm = _lane_mask(e)
            L = jnp.exp(jnp.where(tri, p["A"][:, col:col + 1] - p["AT"][col:col + 1, :], NEG))
            yd = _dot((CB[g] * L).astype(BF), jnp.where(lm, X[:, sl], 0.0).astype(BF))
            yo = _dot_nt(Cb[g], hpb) * p["eA"][:, sl]
            acc = acc + jnp.where(lm, yd + yo, 0.0)
        tiles.append(acc)
    return jnp.concatenate(tiles, axis=1) + dskip * p["xs"], X, Bb, Cb, CB


def _group_stats(v):
    g0 = lax.broadcasted_iota(jnp.int32, (1, SSD_W), 1) < SSD_W // 2
    m0 = jnp.sum(jnp.where(g0, v, 0.0), axis=-1, keepdims=True) * (2.0 / SSD_W)
    m1 = jnp.sum(jnp.where(g0, 0.0, v), axis=-1, keepdims=True) * (2.0 / SSD_W)
    return jnp.where(g0, m0, m1)


def _ssd_specs(T, rev):
    B = T // SEQ

    def chunk(b, c):
        return b * N_CHUNK + (N_CHUNK - 1 - c if rev else c)

    row = pl.BlockSpec((CHUNK, W_SSD), lambda b, c: (chunk(b, c), 0))
    halo = pl.BlockSpec((8, W_SSD), lambda b, c: (jnp.maximum(chunk(b, c) * (CHUNK // 8) - 1, 0), 0))
    hp = pl.BlockSpec((None, SSD_W, SSD_STATE), lambda b, c: (chunk(b, c), 0, 0))
    y = pl.BlockSpec((CHUNK, SSD_W), lambda b, c: (chunk(b, c), 0))
    const = lambda r, w: pl.BlockSpec((r, w), lambda b, c: (0, 0))
    params = [const(4, SSD_CONV_DIM), const(1, SSD_CONV_DIM)] + [const(1, SSD_W)] * 4
    return B, row, halo, hp, y, const, params


def _ssd_fwd(sin, conv_w, conv_b, dtb, alog, dskip, norm_g):
    T = sin.shape[0]
    B, row, halo, hp, y, const, params = _ssd_specs(T, False)

    def body(x_ref, halo_ref, cw_ref, cb_ref, dtb_ref, al_ref, dk_ref, ng_ref, y_ref, hp_ref, ext, hst):
        c = pl.program_id(1)

        @pl.when(c == 0)
        def _():
            hst[...] = jnp.zeros_like(hst)

        p = _ssd_pre(x_ref, halo_ref, c == 0, cw_ref, cb_ref, dtb_ref, al_ref, ext)
        yv, X, Bb, Cb, CB = _ssd_y(p, hst, dk_ref[...])
        hp_ref[...] = hst[...]
        for t in range(3):
            sl = slice(128 * t, 128 * (t + 1))
            old = hst[sl, :]
            new = old
            for e in range(2):
                h = 2 * t + e
                g, col = h // 3, HEAD * h
                st = _dot_tn(jnp.where(_lane_mask(e), X[:, sl] * p["wdec"][:, sl], 0.0).astype(BF), Bb[g])
                new = jnp.where(_row_mask(e), old * p["dtot"][:, col:col + 1] + st, new)
            hst[sl, :] = new
        y2 = yv * _silu(p["z"])
        r = lax.rsqrt(_group_stats(y2 * y2) + RMS_EPS)
        y_ref[...] = y2 * r * ng_ref[...]

    return _call(body, name="ssd_fwd", grid=(B, N_CHUNK), in_specs=[row, halo] + params, out_specs=[y, hp],
                 out_shape=[jax.ShapeDtypeStruct((T, SSD_W), F32),
                            jax.ShapeDtypeStruct((T // CHUNK, SSD_W, SSD_STATE), F32)],
                 scratch=[pltpu.VMEM((8 + CHUNK, SSD_CONV_DIM), F32), pltpu.VMEM((SSD_W, SSD_STATE), F32)],
                 sem=("parallel", "arbitrary"))(sin, sin, conv_w, conv_b, dtb, alog, dskip, norm_g)


def _ssd_bwd(sin, hprev, dy3, conv_w, conv_b, dtb, alog, dskip, norm_g):
    T = sin.shape[0]
    B, row, halo, hp, y, const, params = _ssd_specs(T, True)

    def body(x_ref, halo_ref, hp_ref, dy_ref, cw_ref, cb_ref, dtb_ref, al_ref, dk_ref, ng_ref,
             dx_ref, dcw_ref, dcb_ref, dvec_ref, ext, ext2, dh):
        c = pl.program_id(1)

        @pl.when((pl.program_id(0) == 0) & (c == 0))
        def _():
            dcw_ref[...] = jnp.zeros_like(dcw_ref)
            dcb_ref[...] = jnp.zeros_like(dcb_ref)
            dvec_ref[...] = jnp.zeros_like(dvec_ref)

        @pl.when(c == 0)
        def _():
            dh[...] = jnp.zeros_like(dh)
            ext2[CHUNK:CHUNK + 8, :] = jnp.zeros((8, SSD_CONV_DIM), F32)

        p = _ssd_pre(x_ref, halo_ref, c == N_CHUNK - 1, cw_ref, cb_ref, dtb_ref, al_ref, ext)
        dskip_ = dk_ref[...]
        yv, X, Bb, Cb, CB = _ssd_y(p, hp_ref, dskip_)
        xs, z, A, AT = p["xs"], p["z"], p["A"], p["AT"]

        sz = _silu(z)
        y2 = yv * sz
        r = lax.rsqrt(_group_stats(y2 * y2) + RMS_EPS)
        dy3_ = dy_ref[...]
        uu = dy3_ * ng_ref[...]
        dy2 = r * (uu - y2 * (r * r * _group_stats(uu * y2)))
        dy = dy2 * sz
        dz = dy2 * yv * _dsilu(z)

        tri = _tri()
        rows = lax.broadcasted_iota(jnp.int32, (CHUNK, 1), 0)
        dG = [jnp.zeros((CHUNK, CHUNK), F32) for _ in range(2)]
        dB = [jnp.zeros((CHUNK, SSD_STATE), F32) for _ in range(2)]
        dC = [jnp.zeros((CHUNK, SSD_STATE), F32) for _ in range(2)]
        dX_t, dA_t, ddtx_t = [], [], []
        for t in range(3):
            sl = slice(128 * t, 128 * (t + 1))
            hp_t = hp_ref[sl, :]
            hpb = hp_t.astype(BF)
            dhc = dh[sl, :]
            dh_new = jnp.zeros((128, SSD_STATE), F32)
            dX = jnp.zeros((CHUNK, 128), F32)
            dA = jnp.zeros((CHUNK, 128), F32)
            ddtx = jnp.zeros((CHUNK, 128), F32)
            for e in range(2):
                h = 2 * t + e
                g, col = h // 3, HEAD * h
                lm, rm, fl = _lane_mask(e), _row_mask(e), _first_lane(e)
                L = jnp.exp(jnp.where(tri, A[:, col:col + 1] - AT[col:col + 1, :], NEG))
                Mf = CB[g] * L
                Xm = jnp.where(lm, X[:, sl], 0.0)
                Xmb = Xm.astype(BF)
                dyh = jnp.where(lm, dy[:, sl], 0.0)
                dyb = dyh.astype(BF)
                dXh = _dot_tn(Mf.astype(BF), dyb)
                dM = jnp.where(tri, _dot_nt(dyb, Xmb), 0.0)
                Wm = dM * Mf
                dAc = jnp.sum(Wm, axis=-1, keepdims=True) - jnp.sum(Wm.T, axis=-1, keepdims=True)
                dG[g] = dG[g] + dM * L
                eAt = p["eA"][:, sl]
                yo = _dot_nt(Cb[g], hpb)
                dyo = (dyh * eAt).astype(BF)
                dC[g] = dC[g] + _dot(dyo, hpb)
                dh_new = dh_new + _dot_tn(dyo, Cb[g])
                dAc = dAc + jnp.sum(dyh * yo * eAt, axis=-1, keepdims=True)
                dHn = jnp.where(rm, dhc, 0.0)
                dHnb = dHn.astype(BF)
                dec = p["dtot"][:, col:col + 1]
                dh_new = dh_new + dec * dHn
                Z = _dot_nt(Bb[g], dHnb)
                wt = p["wdec"][:, sl]
                xi = jnp.sum(Xm * Z, axis=-1, keepdims=True) * p["wdec"][:, col:col + 1]
                dXh = dXh + wt * Z
                dB[g] = dB[g] + _dot(jnp.where(lm, X[:, sl] * wt, 0.0).astype(BF), dHnb)
                dAtot = jnp.sum(xi, axis=0, keepdims=True) + dec * jnp.sum(
                    jnp.sum(dHn * hp_t, axis=-1, keepdims=True), axis=0, keepdims=True)
                dAc = dAc - xi + jnp.where(rows == CHUNK - 1, dAtot, 0.0)
                dA = dA + jnp.where(fl, dAc, 0.0)
                dX = dX + dXh
                ddtx = ddtx + jnp.where(fl, jnp.sum(dXh * xs[:, sl], axis=-1, keepdims=True), 0.0)
            dh[sl, :] = dh_new
            dX_t.append(dX)
            dA_t.append(dA)
            ddtx_t.append(ddtx)
        for g in range(2):
            dGb = dG[g].astype(BF)
            dC[g] = dC[g] + _dot(dGb, Bb[g])
            dB[g] = dB[g] + _dot_tn(dGb, Cb[g])
        dXf = jnp.concatenate(dX_t, axis=1)
        da = _cumsum_rows(jnp.concatenate(dA_t, axis=1), reverse=True)
        ddt = da * (-jnp.exp(al_ref[...])) + jnp.concatenate(ddtx_t, axis=1)
        du = ddt * _sigmoid(p["u"])
        dxs = dXf * p["dt"] + dskip_ * dy
        dxc = jnp.concatenate([dxs, dB[0], dB[1], dC[0], dC[1]], axis=1) * _dsilu(p["xc"])
        ext2[0:CHUNK, :] = dxc
        dxbc = jnp.zeros((CHUNK, SSD_CONV_DIM), F32)
        for j in range(4):
            dxbc = dxbc + cw_ref[j:j + 1, :] * ext2[pl.ds(3 - j, CHUNK), :]
            dcw_ref[j:j + 1, :] += jnp.sum(dxc * ext[pl.ds(5 + j, CHUNK), :], axis=0, keepdims=True)
        ext2[CHUNK:CHUNK + 8, :] = dxc[0:8, :]
        dcb_ref[...] += jnp.sum(dxc, axis=0, keepdims=True)
        dvec_ref[0:1, :] += jnp.sum(du, axis=0, keepdims=True)
        dvec_ref[1:2, :] += jnp.sum(da * p["a"], axis=0, keepdims=True)
        dvec_ref[2:3, :] += jnp.sum(dy * xs, axis=0, keepdims=True)
        dvec_ref[3:4, :] += jnp.sum(dy3_ * y2 * r, axis=0, keepdims=True)
        dx_ref[...] = jnp.concatenate([dxbc, dz, du], axis=1).astype(BF)

    return _call(body, name="ssd_bwd", grid=(B, N_CHUNK), in_specs=[row, halo, hp, y] + params,
                 out_specs=[row, const(4, SSD_CONV_DIM), const(1, SSD_CONV_DIM), const(8, SSD_W)],
                 out_shape=[jax.ShapeDtypeStruct((T, W_SSD), BF), jax.ShapeDtypeStruct((4, SSD_CONV_DIM), F32),
                            jax.ShapeDtypeStruct((1, SSD_CONV_DIM), F32), jax.ShapeDtypeStruct((8, SSD_W), F32)],
                 scratch=[pltpu.VMEM((8 + CHUNK, SSD_CONV_DIM), F32), pltpu.VMEM((8 + CHUNK, SSD_CONV_DIM), F32),
                          pltpu.VMEM((SSD_W, SSD_STATE), F32)],
                 sem=("arbitrary", "arbitrary"))(sin, sin, hprev, dy3, conv_w, conv_b, dtb, alog, dskip, norm_g)


def _sgu_core(uv_ref, g_ref, b_ref, w_ref, bias_ref):
    x = uv_ref[...]
    cdf = 0.5 * (1.0 + lax.erf(x * (2.0 ** -0.5)))
    ge = x * cdf
    dge = cdf + x * jnp.exp(-0.5 * x * x) * ((2.0 * math.pi) ** -0.5)
    u, v = ge[:, 0:SGU_W], ge[:, SGU_W:]
    vc = v - jnp.mean(v, axis=-1, keepdims=True)
    rstd = lax.rsqrt(jnp.mean(vc * vc, axis=-1, keepdims=True) + LN_EPS)
    vhat = vc * rstd
    vn = vhat * g_ref[...] + b_ref[...]
    tri = _tri()
    wc = [jnp.where(tri, w_ref[gi], 0.0).astype(BF) for gi in range(4)]
    vm = [jnp.where(_lane_mask(gi % 2), vn[:, 128 * (gi // 2):128 * (gi // 2 + 1)], 0.0).astype(BF) for gi in range(4)]
    mixed = jnp.concatenate([_dot(wc[2 * t], vm[2 * t]) + _dot(wc[2 * t + 1], vm[2 * t + 1]) for t in range(2)],
                            axis=1) + bias_ref[...]
    return dict(dge=dge, u=u, rstd=rstd, vhat=vhat, wc=wc, vm=vm, mixed=mixed)


def _sgu_specs():
    vec = pl.BlockSpec((1, SGU_W), lambda i: (0, 0))
    return [pl.BlockSpec((CHUNK, W_UV), lambda i: (i, 0)), vec, vec,
            pl.BlockSpec((4, CHUNK, CHUNK), lambda i: (0, 0, 0)), pl.BlockSpec((CHUNK, SGU_W), lambda i: (0, 0))]


def _sgu_fwd(uv, ln_g, ln_b, w, bias):
    T = uv.shape[0]

    def body(uv_ref, g_ref, b_ref, w_ref, bias_ref, y_ref):
        s = _sgu_core(uv_ref, g_ref, b_ref, w_ref, bias_ref)
        y_ref[...] = s["u"] * s["mixed"]

    return _call(body, name="sgu_fwd", grid=(T // CHUNK,), in_specs=_sgu_specs(),
                 out_specs=pl.BlockSpec((CHUNK, SGU_W), lambda i: (i, 0)),
                 out_shape=jax.ShapeDtypeStruct((T, SGU_W), F32), sem=("parallel",))(uv, ln_g, ln_b, w, bias)


def _sgu_bwd(uv, dy, ln_g, ln_b, w, bias):
    T = uv.shape[0]

    def body(uv_ref, dy_ref, g_ref, b_ref, w_ref, bias_ref, dx_ref, dw_ref, dbias_ref, dln_ref):
        @pl.when(pl.program_id(0) == 0)
        def _():
            dw_ref[...] = jnp.zeros_like(dw_ref)
            dbias_ref[...] = jnp.zeros_like(dbias_ref)
            dln_ref[...] = jnp.zeros_like(dln_ref)

        s = _sgu_core(uv_ref, g_ref, b_ref, w_ref, bias_ref)
        dy_ = dy_ref[...]
        du = dy_ * s["mixed"]
        dmix = dy_ * s["u"]
        dbias_ref[...] += dmix
        tri = _tri()
        dvn_t = []
        for t in range(2):
            acc = jnp.zeros((CHUNK, 128), F32)
            for e in range(2):
                gi = 2 * t + e
                dmg = jnp.where(_lane_mask(e), dmix[:, 128 * t:128 * (t + 1)], 0.0).astype(BF)
                acc = acc + _dot_tn(s["wc"][gi], dmg)
                dw_ref[gi] += jnp.where(tri, _dot_nt(dmg, s["vm"][gi]), 0.0)
            dvn_t.append(acc)
        dvn = jnp.concatenate(dvn_t, axis=1)
        dln_ref[0:1, :] += jnp.sum(dvn * s["vhat"], axis=0, keepdims=True)
        dln_ref[1:2, :] += jnp.sum(dvn, axis=0, keepdims=True)
        dvh = dvn * g_ref[...]
        dv = s["rstd"] * (dvh - jnp.mean(dvh, axis=-1, keepdims=True)
                          - s["vhat"] * jnp.mean(dvh * s["vhat"], axis=-1, keepdims=True))
        dx_ref[...] = (jnp.concatenate([du, dv], axis=1) * s["dge"]).astype(BF)

    ins = _sgu_specs()
    return _call(body, name="sgu_bwd", grid=(T // CHUNK,),
                 in_specs=[ins[0], pl.BlockSpec((CHUNK, SGU_W), lambda i: (i, 0))] + ins[1:],
                 out_specs=[pl.BlockSpec((CHUNK, W_UV), lambda i: (i, 0)),
                            pl.BlockSpec((4, CHUNK, CHUNK), lambda i: (0, 0, 0)),
                            pl.BlockSpec((CHUNK, SGU_W), lambda i: (0, 0)), pl.BlockSpec((8, SGU_W), lambda i: (0, 0))],
                 out_shape=[jax.ShapeDtypeStruct((T, W_UV), BF), jax.ShapeDtypeStruct((4, CHUNK, CHUNK), F32),
                            jax.ShapeDtypeStruct((CHUNK, SGU_W), F32), jax.ShapeDtypeStruct((8, SGU_W), F32)],
                 sem=("arbitrary",))(uv, dy, ln_g, ln_b, w, bias)


def _adamw(w, g, m, v):
    R, C = w.shape
    tr = _tile(R, 256) if R % 8 == 0 else R

    def body(w_ref, g_ref, m_ref, v_ref, d_ref, nm_ref, nv_ref):
        g_ = g_ref[...]
        m2 = ADAM_B1 * m_ref[...] + (1.0 - ADAM_B1) * g_
        v2 = ADAM_B2 * v_ref[...] + (1.0 - ADAM_B2) * (g_ * g_)
        m_hat = m2 / (1.0 - ADAM_B1 ** ADAM_STEP)
        v_hat = v2 / (1.0 - ADAM_B2 ** ADAM_STEP)
        d_ref[...] = -ADAM_LR * (m_hat / (jnp.sqrt(v_hat) + ADAM_EPS) + ADAM_WD * w_ref[...])
        nm_ref[...] = m2
        nv_ref[...] = v2

    blk = pl.BlockSpec((tr, C), lambda i: (i, 0))
    sh = jax.ShapeDtypeStruct((R, C), F32)
    return _call(body, name="adamw", grid=(R // tr,), in_specs=[blk] * 4, out_specs=[blk] * 3,
                 out_shape=[sh] * 3, sem=("parallel",))(w, g, m, v)


def _pair_add(gbuf, rsib, c):
    NS, _, R, C = gbuf.shape
    tr = 512

    def body(c_ref, a_ref, b_ref, o_ref):
        o_ref[...] = (a_ref[...] + b_ref[...]).astype(BF)

    blk = pl.BlockSpec((None, tr, C), lambda j, i, c_ref: (j, i, 0))
    return pl.pallas_call(
        body, name="rs_pair_add",
        grid_spec=pltpu.PrefetchScalarGridSpec(
            num_scalar_prefetch=1, grid=(NS, pl.cdiv(R, tr)),
            in_specs=[pl.BlockSpec((None, None, tr, C), lambda j, i, c_ref: (j, c_ref[0], i, 0)), blk],
            out_specs=blk),
        out_shape=jax.ShapeDtypeStruct((NS, R, C), BF),
        compiler_params=pltpu.CompilerParams(dimension_semantics=("parallel", "parallel")),
    )(jnp.reshape(c, (1,)).astype(jnp.int32), gbuf, rsib)


def _chip_sum(parts):
    NS, R, C = parts.shape
    tr = 512

    def body(p_ref, o_ref):
        p = [p_ref[j].astype(F32) for j in range(4)]
        o_ref[...] = ((p[0] + p[1]) + p[2]) + p[3]

    return _call(body, name="rs_chip_sum", grid=(pl.cdiv(R, tr),),
                 in_specs=[pl.BlockSpec((NS, tr, C), lambda i: (0, i, 0))],
                 out_specs=pl.BlockSpec((tr, C), lambda i: (i, 0)),
                 out_shape=jax.ShapeDtypeStruct((R, C), F32), sem=("parallel",))(parts)


MESH = pl.DeviceIdType.MESH
ANY = pl.BlockSpec(memory_space=pl.ANY)


def _place():
    x, y, c = lax.axis_index("x"), lax.axis_index("y"), lax.axis_index("c")
    return x, y, c, [(1 - x, y), (x, 1 - y), (1 - x, 1 - y)]


HBM = pl.BlockSpec(memory_space=pltpu.HBM)
SEM = pl.BlockSpec(memory_space=pltpu.SEMAPHORE)
EFFECT = pltpu.SideEffectType.DATAFLOW_SIDE_EFFECTING


def _gather_copies(srcs, lands, send, recv):
    x, y, c, chips = _place()
    me = 2 * x + y
    return [pltpu.make_async_remote_copy(
        src_ref=srcs[k], dst_ref=lands[k].at[me], send_sem=send.at[3 * k + r], recv_sem=recv.at[3 * k + r],
        device_id=(px, py, c), device_id_type=MESH) for k in range(len(srcs)) for r, (px, py) in enumerate(chips)]


def _gather_start(arrs, tag):
    n = len(arrs)
    me = 2 * lax.axis_index("x") + lax.axis_index("y")
    lands = [lax.dynamic_update_index_in_dim(jnp.zeros((4,) + a.shape, a.dtype), a, me, 0) for a in arrs]

    def body(*refs):
        srcs, zones = refs[:n], refs[n:2 * n]
        send, recv = refs[2 * n], refs[2 * n + 1]
        token = refs[-1]
        for cp in _gather_copies(srcs, zones, send, recv):
            cp.start()
        token[...] = jnp.zeros_like(token)

    hbm = lambda a: pltpu.with_memory_space_constraint(a, pltpu.HBM)
    out = pl.pallas_call(
        body, name="gather_start_" + tag,
        out_shape=(pltpu.SemaphoreType.DMA((3 * n,)), pltpu.SemaphoreType.DMA((3 * n,)),
                   *[pltpu.HBM(a.shape, a.dtype) for a in arrs], *[pltpu.HBM(z.shape, z.dtype) for z in lands],
                   jax.ShapeDtypeStruct((8, 128), F32)),
        in_specs=[HBM] * (2 * n), out_specs=(SEM, SEM, *[HBM] * (2 * n), pl.BlockSpec(memory_space=pltpu.VMEM)),
        input_output_aliases={k: 2 + k for k in range(2 * n)},
        compiler_params=pltpu.CompilerParams(has_side_effects=EFFECT),
    )(*[hbm(a) for a in arrs], *[hbm(z) for z in lands])
    return out[0], out[1], list(out[2:2 + n]), list(out[2 + n:2 + 2 * n]), out[-1]


def _gather_wait(send, recv, srcs, lands, after, tag):
    n = len(srcs)

    def body(*refs):
        s_refs, zones = refs[:n], refs[n:2 * n]
        send_, recv_ = refs[2 * n], refs[2 * n + 1]
        for cp in _gather_copies(s_refs, zones, send_, recv_):
            cp.wait_send()
            cp.wait_recv()

    out = pl.pallas_call(
        body, name="gather_wait_" + tag,
        out_shape=(*[pltpu.HBM(a.shape, a.dtype) for a in srcs], *[pltpu.HBM(z.shape, z.dtype) for z in lands]),
        in_specs=[HBM] * (2 * n) + [SEM, SEM, ANY], out_specs=tuple([HBM] * (2 * n)),
        input_output_aliases={k: k for k in range(2 * n)},
        compiler_params=pltpu.CompilerParams(has_side_effects=EFFECT),
    )(*srcs, *lands, send, recv, after)
    return list(out[n:])


def _rs_to_sibling(gbuf):
    NS, _, R, C = gbuf.shape

    def body(g_ref, o_ref, send, recv):
        x, y, c, _ = _place()
        cps = []
        for j in range(NS):
            cp = pltpu.make_async_remote_copy(
                src_ref=g_ref.at[j, 1 - c], dst_ref=o_ref.at[j], send_sem=send.at[j], recv_sem=recv.at[j],
                device_id=(x, y, 1 - c), device_id_type=MESH)
            cp.start()
            cps.append(cp)
        for cp in cps:
            cp.wait()

    return pl.pallas_call(
        body, name="rs_to_sibling", in_specs=[ANY], out_specs=ANY,
        out_shape=jax.ShapeDtypeStruct((NS, R, C), F32),
        scratch_shapes=[pltpu.SemaphoreType.DMA((NS,)), pltpu.SemaphoreType.DMA((NS,))],
    )(gbuf)


def _rs_to_chips(pbuf):
    NS, R, C = pbuf.shape

    def body(p_ref, o_ref, send, recv, loc):
        x, y, c, chips = _place()
        me = 2 * x + y
        lc = pltpu.make_async_copy(p_ref.at[me], o_ref.at[me], loc)
        lc.start()
        cps = []
        for r, (px, py) in enumerate(chips):
            cp = pltpu.make_async_remote_copy(
                src_ref=p_ref.at[2 * px + py], dst_ref=o_ref.at[me], send_sem=send.at[r], recv_sem=recv.at[r],
                device_id=(px, py, c), device_id_type=MESH)
            cp.start()
            cps.append(cp)
        for cp in cps:
            cp.wait()
        lc.wait()

    return pl.pallas_call(
        body, name="rs_to_chips", in_specs=[ANY], out_specs=ANY,
        out_shape=jax.ShapeDtypeStruct((NS, R, C), pbuf.dtype),
        scratch_shapes=[pltpu.SemaphoreType.DMA((3,)), pltpu.SemaphoreType.DMA((3,)), pltpu.SemaphoreType.DMA],
    )(pbuf)


def _rs_join_halves(half):
    R, C = half.shape

    def body(h_ref, o_ref, send, recv, loc):
        x, y, c, _ = _place()
        lc = pltpu.make_async_copy(h_ref, o_ref.at[c], loc)
        lc.start()
        cp = pltpu.make_async_remote_copy(src_ref=h_ref, dst_ref=o_ref.at[c], send_sem=send, recv_sem=recv,
                                          device_id=(x, y, 1 - c), device_id_type=MESH)
        cp.start()
        cp.wait()
        lc.wait()

    return pl.pallas_call(
        body, name="rs_join_halves", in_specs=[ANY], out_specs=ANY,
        out_shape=jax.ShapeDtypeStruct((2, R, C), F32),
        scratch_shapes=[pltpu.SemaphoreType.DMA, pltpu.SemaphoreType.DMA, pltpu.SemaphoreType.DMA],
    )(half)


def _all_reduce_small(v):
    R, C = v.shape

    def body(v_ref, o_ref, g_ref, send, recv, loc):
        x, y, c, chips = _place()
        me, sibling = (x, y, c), (x, y, 1 - c)

        def rows(px, py, pc):
            return g_ref.at[4 * px + 2 * py + pc]

        def copy(k, block, to, src=None):
            return pltpu.make_async_remote_copy(
                src_ref=rows(*block) if src is None else src, dst_ref=rows(*block),
                send_sem=send.at[k], recv_sem=recv.at[k], device_id=to, device_id_type=MESH)

        mine = pltpu.make_async_copy(v_ref, rows(*me), loc)
        mine.start()
        first = [copy(0, me, sibling, src=v_ref)]
        first += [copy(1 + j, me, (*chip, c), src=v_ref) for j, chip in enumerate(chips)]
        for cp in first:
            cp.start()
        passed = [copy(4 + j, (*chip, c), sibling) for j, chip in enumerate(chips)]
        for j, chip in enumerate(chips):
            copy(1 + j, (*chip, c), me).wait_recv()
            passed[j].start()
        copy(0, sibling, me).wait_recv()
        for j, chip in enumerate(chips):
            copy(4 + j, (*chip, 1 - c), me).wait_recv()
        for cp in first + passed:
            cp.wait_send()
        mine.wait()
        acc = g_ref[0]
        for d in range(1, 8):
            acc = acc + g_ref[d]
        o_ref[...] = acc

    vm = pl.BlockSpec(memory_space=pltpu.VMEM)
    return pl.pallas_call(
        body, name="all_reduce_small", in_specs=[vm], out_specs=[vm, vm],
        out_shape=[jax.ShapeDtypeStruct((R, C), F32), jax.ShapeDtypeStruct((8, R, C), F32)],
        scratch_shapes=[pltpu.SemaphoreType.DMA((7,)), pltpu.SemaphoreType.DMA((7,)), pltpu.SemaphoreType.DMA],
    )(v)[0]


WEIGHTS = ['ffn1_norm', 'ffn1_w_gate', 'ffn1_w_up', 'ffn1_w_down', 'mix_norm', 'w_in', 'conv_w', 'conv_b', 'dt_bias',
           'a_log', 'd_skip', 'ssd_norm', 'sgu_ln_g', 'sgu_ln_b', 'sgu_w', 'sgu_b', 'w_out', 'ffn2_norm',
           'ffn2_w_gate', 'ffn2_w_up', 'ffn2_w_down', 'final_norm']
SHARDED = ['ffn1_w_gate', 'ffn1_w_up', 'ffn1_w_down', 'w_in', 'conv_w', 'w_out', 'ffn2_w_gate', 'ffn2_w_up',
           'ffn2_w_down']
SMALL = [n for n in WEIGHTS if n not in SHARDED]
RS_COLS = 1024
DEPTH = 2


def _pack_w_in(w):
    return jnp.concatenate([w[..., 0:1152], w[..., 1536:2432], w[..., 1152:1536],
                            jnp.repeat(w[..., 2432:2438], HEAD, axis=-1), w[..., 2438:2950]], axis=-1)


def _unpack_w_in(dq, ds, du):
    return jnp.concatenate([dq, ds[:, 896:1280], ds[:, 0:896], ds[:, 1280::HEAD], du], axis=-1)


def _ffn_fwd(x, g, wg, wu, wd):
    hb = _rms_fwd(x, g)
    G, U, A = _ffn_up(hb, wg, wu)
    return _ffn_down(A, wd, x), (x, hb, G, U, A)


def _ffn_bwd(dxo, saved, g, wg, wu, wd):
    x, hb, G, U, A = saved
    dG, dU = _ffn_bwd_act(dxo, wd, G, U)
    dwd = _ffn_bwd_wd(A, dxo)
    dwg, dwu = _ffn_bwd_wgu(hb, dG, dU)
    dh = _ffn_bwd_dh(dG, dU, wg, wu)
    dx, dg = _rms_bwd(x, g, dh, dxo)
    return dx, dg, dwg, dwu, dwd


def _mix_fwd(x, P):
    hb = _rms_fwd(x, P["mix_norm"])
    qkv = _mm_nn(hb, P["w_qkv"], out_dtype=BF)
    sin = _mm_nn(hb, P["w_ssd"])
    uv = _mm_nn(hb, P["w_uv"])
    o1, l1 = _attn_fwd(qkv, 1)
    o2, l2 = _attn_fwd(qkv, 4)
    o3, l3 = _attn_fwd(qkv, 16)
    y_att, lse = _attn_combine(o1, o2, o3, l1, l2, l3)
    y_ssd, hprev = _ssd_fwd(sin, *P["ssd"])
    y_sgu = _sgu_fwd(uv, *P["sgu"])
    ycat = jnp.concatenate([y_att, y_ssd, y_sgu], axis=1).astype(BF)
    return _mm_nn(ycat, P["w_out"], res=x), (x, hb, qkv, sin, uv, y_att, lse, hprev, ycat)


def _mix_bwd(dxo, saved, P):
    x, hb, qkv, sin, uv, y_att, lse, hprev, ycat = saved
    dycat = _mm_nt(dxo, P["w_out"])
    dwout = _mm_tn(ycat, dxo)
    dy_att, dy_ssd, dy_sgu = dycat[:, 0:ATT_W], dycat[:, ATT_W:ATT_W + SSD_W], dycat[:, ATT_W + SSD_W:]
    dqkv = _sum_branches([_attn_bwd(qkv, dy_att, y_att, lse, d) for d in DILATIONS])
    dsin, dcw, dcb, dvec = _ssd_bwd(sin, hprev, dy_ssd, *P["ssd"])
    duv, dsw, dsbias, dln = _sgu_bwd(uv, dy_sgu, *P["sgu"])
    dwin = _unpack_w_in(_mm_tn(hb, dqkv), _mm_tn(hb, dsin), _mm_tn(hb, duv))
    dh = _mm_nt(dqkv, P["w_qkv"])
    dh = _mm_nt(dsin, P["w_ssd"], res=dh)
    dh = _mm_nt(duv, P["w_uv"], res=dh)
    dx, dg = _rms_bwd(x, P["mix_norm"], dh, dxo)
    grads = dict(
        mix_norm=dg[0], w_in=dwin, conv_w=dcw, conv_b=dcb[0], dt_bias=dvec[0, ::HEAD], a_log=dvec[1, ::HEAD],
        d_skip=jnp.sum(dvec[2].reshape(6, HEAD), axis=-1), ssd_norm=dvec[3], sgu_ln_g=dln[0], sgu_ln_b=dln[1],
        sgu_w=dsw, sgu_b=jnp.sum(dsbias.reshape(CHUNK, 4, HEAD), axis=-1).T, w_out=dwout)
    return dx, grads


def _shard_major(name, g):
    if name == "conv_w":
        return g.reshape(4, 4, 224).transpose(1, 0, 2).reshape(4, -1)
    return g.reshape(4, -1)


def kernel(x, ffn1_norm, ffn1_w_gate, ffn1_w_up, ffn1_w_down, mix_norm, w_in, conv_w, conv_b, dt_bias, a_log, d_skip, ssd_norm, sgu_ln_g, sgu_ln_b, sgu_w, sgu_b, w_out, ffn2_norm, ffn2_w_gate, ffn2_w_up, ffn2_w_down, final_norm, loss_target, m_ffn1_norm, m_ffn1_w_gate, m_ffn1_w_up, m_ffn1_w_down, m_mix_norm, m_w_in, m_conv_w, m_conv_b, m_dt_bias, m_a_log, m_d_skip, m_ssd_norm, m_sgu_ln_g, m_sgu_ln_b, m_sgu_w, m_sgu_b, m_w_out, m_ffn2_norm, m_ffn2_w_gate, m_ffn2_w_up, m_ffn2_w_down, m_final_norm, v_ffn1_norm, v_ffn1_w_gate, v_ffn1_w_up, v_ffn1_w_down, v_mix_norm, v_w_in, v_conv_w, v_conv_b, v_dt_bias, v_a_log, v_d_skip, v_ssd_norm, v_sgu_ln_g, v_sgu_ln_b, v_sgu_w, v_sgu_b, v_w_out, v_ffn2_norm, v_ffn2_w_gate, v_ffn2_w_up, v_ffn2_w_down, v_final_norm):
    given = dict(x=x, ffn1_norm=ffn1_norm, ffn1_w_gate=ffn1_w_gate, ffn1_w_up=ffn1_w_up, ffn1_w_down=ffn1_w_down, mix_norm=mix_norm, w_in=w_in, conv_w=conv_w, conv_b=conv_b, dt_bias=dt_bias, a_log=a_log, d_skip=d_skip, ssd_norm=ssd_norm, sgu_ln_g=sgu_ln_g, sgu_ln_b=sgu_ln_b, sgu_w=sgu_w, sgu_b=sgu_b, w_out=w_out, ffn2_norm=ffn2_norm, ffn2_w_gate=ffn2_w_gate, ffn2_w_up=ffn2_w_up, ffn2_w_down=ffn2_w_down, final_norm=final_norm, loss_target=loss_target, m_ffn1_norm=m_ffn1_norm, m_ffn1_w_gate=m_ffn1_w_gate, m_ffn1_w_up=m_ffn1_w_up, m_ffn1_w_down=m_ffn1_w_down, m_mix_norm=m_mix_norm, m_w_in=m_w_in, m_conv_w=m_conv_w, m_conv_b=m_conv_b, m_dt_bias=m_dt_bias, m_a_log=m_a_log, m_d_skip=m_d_skip, m_ssd_norm=m_ssd_norm, m_sgu_ln_g=m_sgu_ln_g, m_sgu_ln_b=m_sgu_ln_b, m_sgu_w=m_sgu_w, m_sgu_b=m_sgu_b, m_w_out=m_w_out, m_ffn2_norm=m_ffn2_norm, m_ffn2_w_gate=m_ffn2_w_gate, m_ffn2_w_up=m_ffn2_w_up, m_ffn2_w_down=m_ffn2_w_down, m_final_norm=m_final_norm, v_ffn1_norm=v_ffn1_norm, v_ffn1_w_gate=v_ffn1_w_gate, v_ffn1_w_up=v_ffn1_w_up, v_ffn1_w_down=v_ffn1_w_down, v_mix_norm=v_mix_norm, v_w_in=v_w_in, v_conv_w=v_conv_w, v_conv_b=v_conv_b, v_dt_bias=v_dt_bias, v_a_log=v_a_log, v_d_skip=v_d_skip, v_ssd_norm=v_ssd_norm, v_sgu_ln_g=v_sgu_ln_g, v_sgu_ln_b=v_sgu_ln_b, v_sgu_w=v_sgu_w, v_sgu_b=v_sgu_b, v_w_out=v_w_out, v_ffn2_norm=v_ffn2_norm, v_ffn2_w_gate=v_ffn2_w_gate, v_ffn2_w_up=v_ffn2_w_up, v_ffn2_w_down=v_ffn2_w_down, v_final_norm=v_final_norm)
    T = given["x"].shape[0] * given["x"].shape[1]
    D = given["x"].shape[2]
    x0 = given["x"].reshape(T, D)
    tgt = given["loss_target"].reshape(T, D)
    c = lax.axis_index("c")

    bf = {n: given[n].astype(BF) for n in SHARDED if n not in ("w_in", "conv_w")}
    bf["w_in"] = _pack_w_in(given["w_in"]).astype(BF)
    groups = []
    for i in range(DEPTH):
        groups.append((f"l{i}_ffn1", [bf["ffn1_w_gate"][i], bf["ffn1_w_up"][i], bf["ffn1_w_down"][i]]))
        groups.append((f"l{i}_mix", [bf["w_in"][i], given["conv_w"][i], bf["w_out"][i]]))
        groups.append((f"l{i}_ffn2", [bf["ffn2_w_gate"][i], bf["ffn2_w_up"][i], bf["ffn2_w_down"][i]]))
    started = [_gather_start(arrs, tag) for tag, arrs in groups]
    token = functools.reduce(lambda a, b: a + b, [s[4][0, 0] for s in started])

    def gathered(k, after):
        send, recv, srcs, lands, _ = started[k]
        return _gather_wait(send, recv, srcs, lands, after, groups[k][0])

    def mix_params(i, got):
        win = got[0].reshape(D, W_QKV + W_SSD + W_UV)
        rep = lambda v: jnp.repeat(v, HEAD)[None]
        ssd = (got[1].transpose(1, 0, 2).reshape(4, SSD_CONV_DIM), given["conv_b"][i][None],
               rep(given["dt_bias"][i]), rep(given["a_log"][i]), rep(given["d_skip"][i]), given["ssd_norm"][i][None])
        sgu = (given["sgu_ln_g"][i][None], given["sgu_ln_b"][i][None], given["sgu_w"][i],
               jnp.repeat(given["sgu_b"][i].T, HEAD, axis=1))
        return dict(mix_norm=given["mix_norm"][i][None], w_qkv=win[:, 0:W_QKV], w_ssd=win[:, W_QKV:W_QKV + W_SSD],
                    w_uv=win[:, W_QKV + W_SSD:], w_out=got[2].reshape(-1, D), ssd=ssd, sgu=sgu)

    x = x0
    tape = []
    for i in range(DEPTH):
        P = dict(ffn1=(given["ffn1_norm"][i][None] + (token if i == 0 else 0.0), *gathered(3 * i, x)))
        x, s1 = _ffn_fwd(x, *P["ffn1"])
        P.update(mix_params(i, gathered(3 * i + 1, x)))
        x, s2 = _mix_fwd(x, P)
        P["ffn2"] = (given["ffn2_norm"][i][None], *gathered(3 * i + 2, x))
        x, s3 = _ffn_fwd(x, *P["ffn2"])
        tape.append((P, s1, s2, s3))
    loss_part, dx, dgf = _final_loss(x, given["final_norm"][None], tgt)

    grads = [dict() for _ in range(DEPTH)]
    for i in reversed(range(DEPTH)):
        P, s1, s2, s3 = tape[i]
        g = grads[i]
        dx, dn2, g["ffn2_w_gate"], g["ffn2_w_up"], g["ffn2_w_down"] = _ffn_bwd(dx, s3, *P["ffn2"])
        dx, gm = _mix_bwd(dx, s2, P)
        g.update(gm)
        dx, dn1, g["ffn1_w_gate"], g["ffn1_w_up"], g["ffn1_w_down"] = _ffn_bwd(dx, s1, *P["ffn1"])
        g["ffn1_norm"], g["ffn2_norm"] = dn1[0], dn2[0]
    grad_x = dx.reshape(given["x"].shape)

    pieces = [_shard_major(n, grads[i][n]) for n in SHARDED for i in range(DEPTH)]
    n_shard = sum(p.shape[1] for p in pieces)
    rows_half = -(-n_shard // (2 * RS_COLS * 16)) * 16
    pad = 2 * rows_half * RS_COLS - n_shard
    gbuf = jnp.concatenate(pieces + [jnp.zeros((4, pad), F32)], axis=1).reshape(4, 2, rows_half, RS_COLS)
    pair = _pair_add(gbuf, _rs_to_sibling(gbuf), c)
    half = _chip_sum(_rs_to_chips(pair))
    gshard = _rs_join_halves(half).reshape(-1)

    order = [n for n in SMALL if n != "final_norm"] + ["final_norm"]
    small = [jnp.stack([grads[i][n] for i in range(DEPTH)]) for n in order[:-1]] + [dgf[0], loss_part[0, 0:1]]
    n_small = sum(s.size for s in small)
    rows_small = -(-n_small // (128 * 8)) * 8

    def flat(arrs):
        fill = rows_small * 128 - sum(a.size for a in arrs)
        return jnp.concatenate([a.reshape(-1) for a in arrs] + [jnp.zeros((fill,), F32)]).reshape(rows_small, 128)

    gsmall = _all_reduce_small(flat(small)).reshape(-1)

    grad_w = {}
    off = 0
    for n in SHARDED:
        size = given[n].size
        grad_w[n] = gshard[off:off + size].reshape(given[n].shape)
        off += size
    off = 0
    for n in order:
        size = given[n].size
        grad_w[n] = gsmall[off:off + size].reshape(given[n].shape)
        off += size
    loss = gsmall[off]

    delta, new_m, new_v = {}, {}, {}
    for n in SHARDED:
        shp = given[n].shape
        two_d = (shp[0] * shp[1], shp[2])
        d, m2, v2 = _adamw(*[a.reshape(two_d) for a in (given[n], grad_w[n], given["m_" + n], given["v_" + n])])
        delta[n], new_m[n], new_v[n] = d.reshape(shp), m2.reshape(shp), v2.reshape(shp)
    packed = [flat([src[pre + n] for n in order])
              for src, pre in ((given, ""), (grad_w, ""), (given, "m_"), (given, "v_"))]
    outs = [o.reshape(-1) for o in _adamw(*packed)]
    off = 0
    for n in order:
        size = given[n].size
        for dst, o in zip((delta, new_m, new_v), outs):
            dst[n] = o[off:off + size].reshape(given[n].shape)
        off += size

    return (loss, grad_x, *[grad_w[n] for n in WEIGHTS], *[delta[n] for n in WEIGHTS],
            *[new_m[n] for n in WEIGHTS], *[new_v[n] for n in WEIGHTS])
```

```python
import functools
import math

import jax
import jax.numpy as jnp
from jax import lax
from jax.experimental import pallas as pl
from jax.experimental.pallas import tpu as pltpu

F32 = jnp.float32
BF = jnp.bfloat16

RMS_EPS = 1e-6
LN_EPS = 1e-5
SEQ = 2048
CHUNK = 128
N_CHUNK = SEQ // CHUNK
ATT_W = 384
HEAD = 64
SSD_W = 384
SSD_CONV_DIM = 896
SSD_STATE = 128
SGU_W = 256
DILATIONS = (1, 4, 16)
W_QKV = 3 * ATT_W
W_SSD = SSD_CONV_DIM + SSD_W + SSD_W
W_UV = 2 * SGU_W
ADAM_LR = 0.001
ADAM_B1 = 0.9
ADAM_B2 = 0.999
ADAM_EPS = 1e-08
ADAM_WD = 0.01
ADAM_STEP = 10
NEG = -1e30


def _dot(a, b):
    return jnp.dot(a, b, preferred_element_type=F32)


def _dot_nt(a, b):
    return lax.dot_general(a, b, (((1,), (1,)), ((), ())), preferred_element_type=F32)


def _dot_tn(a, b):
    return lax.dot_general(a, b, (((0,), (0,)), ((), ())), preferred_element_type=F32)


def _sigmoid(x):
    return 1.0 / (1.0 + jnp.exp(-x))


def _call(body, *, name, grid, in_specs, out_specs, out_shape, scratch=(), sem=None):
    return pl.pallas_call(
        body, name=name, grid=grid, in_specs=in_specs, out_specs=out_specs, out_shape=out_shape,
        scratch_shapes=list(scratch),
        compiler_params=pltpu.CompilerParams(dimension_semantics=sem),
    )


def _tile(n, want):
    t = min(n, want)
    while n % t:
        t //= 2
    return t


def _rms_fwd(x, g):
    T, D = x.shape
    tm = _tile(T, 512)

    def body(x_ref, g_ref, h_ref):
        xf = x_ref[...]
        r = lax.rsqrt(jnp.mean(xf * xf, axis=-1, keepdims=True) + RMS_EPS)
        h_ref[...] = (xf * r * g_ref[...]).astype(BF)

    return _call(body, name="rms_fwd", grid=(T // tm,),
                 in_specs=[pl.BlockSpec((tm, D), lambda i: (i, 0)), pl.BlockSpec((1, D), lambda i: (0, 0))],
                 out_specs=pl.BlockSpec((tm, D), lambda i: (i, 0)),
                 out_shape=jax.ShapeDtypeStruct((T, D), BF), sem=("parallel",))(x, g)


def _rms_bwd(x, g, dh, dres):
    T, D = x.shape
    tm = _tile(T, 512)

    def body(x_ref, g_ref, dh_ref, dr_ref, dx_ref, dg_ref):
        @pl.when(pl.program_id(0) == 0)
        def _():
            dg_ref[...] = jnp.zeros_like(dg_ref)

        xf = x_ref[...]
        r = lax.rsqrt(jnp.mean(xf * xf, axis=-1, keepdims=True) + RMS_EPS)
        dh_ = dh_ref[...]
        u = dh_ * g_ref[...]
        mu = jnp.mean(u * xf, axis=-1, keepdims=True)
        dx_ref[...] = dr_ref[...] + r * (u - xf * (r * r * mu))
        dg_ref[...] += jnp.sum(dh_ * xf * r, axis=0, keepdims=True)

    row = pl.BlockSpec((tm, D), lambda i: (i, 0))
    vec = pl.BlockSpec((1, D), lambda i: (0, 0))
    return _call(body, name="rms_bwd", grid=(T // tm,), in_specs=[row, vec, row, row], out_specs=[row, vec],
                 out_shape=[jax.ShapeDtypeStruct((T, D), F32), jax.ShapeDtypeStruct((1, D), F32)],
                 sem=("arbitrary",))(x, g, dh, dres)


def _final_loss(x, g, tgt):
    T, D = x.shape
    tm = _tile(T, 512)

    def body(x_ref, g_ref, t_ref, l_ref, dx_ref, dg_ref):
        @pl.when(pl.program_id(0) == 0)
        def _():
            dg_ref[...] = jnp.zeros_like(dg_ref)
            l_ref[...] = jnp.zeros_like(l_ref)

        xf = x_ref[...]
        gg = g_ref[...]
        r = lax.rsqrt(jnp.mean(xf * xf, axis=-1, keepdims=True) + RMS_EPS)
        xn = xf * r
        e = xn * gg - t_ref[...]
        part = 0.5 * jnp.sum(jnp.mean(e * e, axis=-1, keepdims=True), axis=0, keepdims=True)
        l_ref[...] += jnp.broadcast_to(part, l_ref.shape)
        dy = e * (1.0 / D)
        u = dy * gg
        mu = jnp.mean(u * xf, axis=-1, keepdims=True)
        dx_ref[...] = r * (u - xf * (r * r * mu))
        dg_ref[...] += jnp.sum(dy * xn, axis=0, keepdims=True)

    row = pl.BlockSpec((tm, D), lambda i: (i, 0))
    vec = pl.BlockSpec((1, D), lambda i: (0, 0))
    lsp = pl.BlockSpec((1, 128), lambda i: (0, 0))
    return _call(body, name="final_loss", grid=(T // tm,), in_specs=[row, vec, row], out_specs=[lsp, row, vec],
                 out_shape=[jax.ShapeDtypeStruct((1, 128), F32), jax.ShapeDtypeStruct((T, D), F32),
                            jax.ShapeDtypeStruct((1, D), F32)],
                 sem=("arbitrary",))(x, g, tgt)


def _ffn_up(hb, wg, wu):
    T, D = hb.shape
    NS, _, Fs = wg.shape
    tm = _tile(T, 512)

    def body(h_ref, wg_ref, wu_ref, g_ref, u_ref, a_ref):
        h = h_ref[...]
        g = _dot(h, wg_ref[...])
        u = _dot(h, wu_ref[...])
        g_ref[...] = g.astype(BF)
        u_ref[...] = u.astype(BF)
        a_ref[...] = (g * _sigmoid(g) * u).astype(BF)

    w = pl.BlockSpec((None, D, Fs), lambda j, i: (j, 0, 0))
    o = pl.BlockSpec((None, tm, Fs), lambda j, i: (j, i, 0))
    sh = jax.ShapeDtypeStruct((NS, T, Fs), BF)
    return _call(body, name="ffn_up", grid=(NS, T // tm),
                 in_specs=[pl.BlockSpec((tm, D), lambda j, i: (i, 0)), w, w], out_specs=[o, o, o],
                 out_shape=[sh, sh, sh], sem=("parallel", "parallel"))(hb, wg, wu)


def _ffn_down(a, wd, x):
    NS, T, Fs = a.shape
    D = wd.shape[2]
    tm = _tile(T, 512)

    def body(a_ref, w_ref, x_ref, o_ref, acc):
        j = pl.program_id(1)

        @pl.when(j == 0)
        def _():
            acc[...] = jnp.zeros_like(acc)

        acc[...] += _dot(a_ref[...], w_ref[...])

        @pl.when(j == NS - 1)
        def _():
            o_ref[...] = x_ref[...] + 0.5 * acc[...]

    row = pl.BlockSpec((tm, D), lambda i, j: (i, 0))
    return _call(body, name="ffn_down", grid=(T // tm, NS),
                 in_specs=[pl.BlockSpec((None, tm, Fs), lambda i, j: (j, i, 0)),
                           pl.BlockSpec((None, Fs, D), lambda i, j: (j, 0, 0)), row],
                 out_specs=row, out_shape=jax.ShapeDtypeStruct((T, D), F32),
                 scratch=[pltpu.VMEM((tm, D), F32)], sem=("parallel", "arbitrary"))(a, wd, x)


def _ffn_bwd_act(dxo, wd, g, u):
    NS, T, Fs = g.shape
    D = dxo.shape[1]
    tm = _tile(T, 512)

    def body(dx_ref, w_ref, g_ref, u_ref, dg_ref, du_ref):
        dy = (0.5 * dx_ref[...]).astype(BF)
        da = _dot_nt(dy, w_ref[...])
        gf = g_ref[...].astype(F32)
        uf = u_ref[...].astype(F32)
        sg = _sigmoid(gf)
        dg_ref[...] = (da * uf * (sg * (1.0 + gf * (1.0 - sg)))).astype(BF)
        du_ref[...] = (da * gf * sg).astype(BF)

    o = pl.BlockSpec((None, tm, Fs), lambda j, i: (j, i, 0))
    sh = jax.ShapeDtypeStruct((NS, T, Fs), BF)
    return _call(body, name="ffn_bwd_act", grid=(NS, T // tm),
                 in_specs=[pl.BlockSpec((tm, D), lambda j, i: (i, 0)),
                           pl.BlockSpec((None, Fs, D), lambda j, i: (j, 0, 0)), o, o],
                 out_specs=[o, o], out_shape=[sh, sh], sem=("parallel", "parallel"))(dxo, wd, g, u)


def _ffn_bwd_wd(a, dxo):
    NS, T, Fs = a.shape
    D = dxo.shape[1]
    tk = _tile(T, 512)

    def body(a_ref, dx_ref, o_ref):
        @pl.when(pl.program_id(1) == 0)
        def _():
            o_ref[...] = jnp.zeros_like(o_ref)

        o_ref[...] += _dot_tn(a_ref[...], (0.5 * dx_ref[...]).astype(BF))

    return _call(body, name="ffn_bwd_wd", grid=(NS, T // tk),
                 in_specs=[pl.BlockSpec((None, tk, Fs), lambda j, k: (j, k, 0)),
                           pl.BlockSpec((tk, D), lambda j, k: (k, 0))],
                 out_specs=pl.BlockSpec((None, Fs, D), lambda j, k: (j, 0, 0)),
                 out_shape=jax.ShapeDtypeStruct((NS, Fs, D), F32), sem=("parallel", "arbitrary"))(a, dxo)


def _ffn_bwd_wgu(hb, dg, du):
    NS, T, Fs = dg.shape
    D = hb.shape[1]
    tk = _tile(T, 512)

    def body(h_ref, dg_ref, du_ref, og_ref, ou_ref):
        @pl.when(pl.program_id(1) == 0)
        def _():
            og_ref[...] = jnp.zeros_like(og_ref)
            ou_ref[...] = jnp.zeros_like(ou_ref)

        h = h_ref[...]
        og_ref[...] += _dot_tn(h, dg_ref[...])
        ou_ref[...] += _dot_tn(h, du_ref[...])

    d = pl.BlockSpec((None, tk, Fs), lambda j, k: (j, k, 0))
    o = pl.BlockSpec((None, D, Fs), lambda j, k: (j, 0, 0))
    sh = jax.ShapeDtypeStruct((NS, D, Fs), F32)
    return _call(body, name="ffn_bwd_wgu", grid=(NS, T // tk),
                 in_specs=[pl.BlockSpec((tk, D), lambda j, k: (k, 0)), d, d], out_specs=[o, o],
                 out_shape=[sh, sh], sem=("parallel", "arbitrary"))(hb, dg, du)


def _ffn_bwd_dh(dg, du, wg, wu):
    NS, T, Fs = dg.shape
    D = wg.shape[1]
    tm = _tile(T, 512)

    def body(dg_ref, du_ref, wg_ref, wu_ref, o_ref, acc):
        j = pl.program_id(1)

        @pl.when(j == 0)
        def _():
            acc[...] = jnp.zeros_like(acc)

        acc[...] += _dot_nt(dg_ref[...], wg_ref[...]) + _dot_nt(du_ref[...], wu_ref[...])

        @pl.when(j == NS - 1)
        def _():
            o_ref[...] = acc[...]

    d = pl.BlockSpec((None, tm, Fs), lambda i, j: (j, i, 0))
    w = pl.BlockSpec((None, D, Fs), lambda i, j: (j, 0, 0))
    return _call(body, name="ffn_bwd_dh", grid=(T // tm, NS), in_specs=[d, d, w, w],
                 out_specs=pl.BlockSpec((tm, D), lambda i, j: (i, 0)),
                 out_shape=jax.ShapeDtypeStruct((T, D), F32),
                 scratch=[pltpu.VMEM((tm, D), F32)], sem=("parallel", "arbitrary"))(dg, du, wg, wu)


def _mm_nn(a, b, res=None, out_dtype=F32):
    T, K = a.shape
    N = b.shape[1]
    tm = _tile(T, 512)
    tn = N if N <= 2048 else _tile(N, 1024)

    def body(*refs):
        if res is None:
            a_ref, b_ref, o_ref = refs
            o_ref[...] = _dot(a_ref[...], b_ref[...]).astype(out_dtype)
        else:
            a_ref, b_ref, r_ref, o_ref = refs
            o_ref[...] = (r_ref[...] + _dot(a_ref[...], b_ref[...])).astype(out_dtype)

    o = pl.BlockSpec((tm, tn), lambda i, j: (i, j))
    ins = [pl.BlockSpec((tm, K), lambda i, j: (i, 0)), pl.BlockSpec((K, tn), lambda i, j: (0, j))]
    args = [a, b]
    if res is not None:
        ins.append(o)
        args.append(res)
    return _call(body, name="mm_nn", grid=(T // tm, N // tn), in_specs=ins, out_specs=o,
                 out_shape=jax.ShapeDtypeStruct((T, N), out_dtype), sem=("parallel", "parallel"))(*args)


def _mm_nt(a, b, res=None):
    T, K = a.shape
    N = b.shape[0]
    tm = _tile(T, 512)

    def body(*refs):
        if res is None:
            a_ref, b_ref, o_ref = refs
            o_ref[...] = _dot_nt(a_ref[...].astype(BF), b_ref[...])
        else:
            a_ref, b_ref, r_ref, o_ref = refs
            o_ref[...] = r_ref[...] + _dot_nt(a_ref[...].astype(BF), b_ref[...])

    o = pl.BlockSpec((tm, N), lambda i: (i, 0))
    ins = [pl.BlockSpec((tm, K), lambda i: (i, 0)), pl.BlockSpec((N, K), lambda i: (0, 0))]
    args = [a, b]
    if res is not None:
        ins.append(o)
        args.append(res)
    return _call(body, name="mm_nt", grid=(T // tm,), in_specs=ins, out_specs=o,
                 out_shape=jax.ShapeDtypeStruct((T, N), F32), sem=("parallel",))(*args)


def _mm_tn(a, b):
    T, M = a.shape
    N = b.shape[1]
    tk = _tile(T, 512)
    tmm = _tile(M, 512)

    def body(a_ref, b_ref, o_ref):
        @pl.when(pl.program_id(1) == 0)
        def _():
            o_ref[...] = jnp.zeros_like(o_ref)

        o_ref[...] += _dot_tn(a_ref[...].astype(BF), b_ref[...].astype(BF))

    return _call(body, name="mm_tn", grid=(M // tmm, T // tk),
                 in_specs=[pl.BlockSpec((tk, tmm), lambda i, k: (k, i)), pl.BlockSpec((tk, N), lambda i, k: (k, 0))],
                 out_specs=pl.BlockSpec((tmm, N), lambda i, k: (i, 0)),
                 out_shape=jax.ShapeDtypeStruct((M, N), F32), sem=("parallel", "arbitrary"))(a, b)


def _lane_mask(e, width=128):
    return (lax.broadcasted_iota(jnp.int32, (1, width), 1) // HEAD) == e


def _band_mask(n):
    qi = lax.broadcasted_iota(jnp.int32, (CHUNK, 2 * CHUNK), 0)
    kj = lax.broadcasted_iota(jnp.int32, (CHUNK, 2 * CHUNK), 1)
    dist = qi + CHUNK - kj
    return (dist >= 0) & (dist <= CHUNK) & ((kj >= CHUNK) | (n > 0))


def _prev_cur(ref, n):
    cur = pl.multiple_of(n * CHUNK, CHUNK)
    prv = pl.multiple_of(jnp.maximum(n - 1, 0) * CHUNK, CHUNK)
    return jnp.concatenate([ref[pl.ds(prv, CHUNK), :], ref[pl.ds(cur, CHUNK), :]], axis=0), prv, cur


def _attn_fwd(qkv, dil):
    T = qkv.shape[0]
    B, L = T // SEQ, SEQ // dil
    nb = L // CHUNK
    scale = HEAD ** -0.5

    def body(q_ref, k_ref, v_ref, o_ref, l_ref):
        n = pl.program_id(2)
        q = q_ref[...]
        kk, _, _ = _prev_cur(k_ref, n)
        vv, _, _ = _prev_cur(v_ref, n)
        mask = _band_mask(n)
        for t in range(ATT_W // 128):
            sl = slice(128 * t, 128 * (t + 1))
            qt, kt, vt = q[:, sl], kk[:, sl], vv[:, sl]
            o_pair = jnp.zeros((CHUNK, 128), F32)
            l_pair = jnp.zeros((CHUNK, 128), F32)
            for e in range(2):
                lm = _lane_mask(e)
                s = _dot_nt(jnp.where(lm, qt, jnp.zeros_like(qt)), kt) * scale
                s = jnp.where(mask, s, NEG)
                m = jnp.max(s, axis=-1, keepdims=True)
                p = jnp.exp(s - m)
                den = jnp.sum(p, axis=-1, keepdims=True)
                o = _dot(p.astype(BF), vt) / den
                o_pair = jnp.where(lm, o, o_pair)
                l_pair = jnp.where(lm, m + jnp.log(den), l_pair)
            o_ref[:, sl] = o_pair
            l_ref[:, sl] = l_pair

    qv = qkv.reshape(B, L, dil * W_QKV)
    o = pl.BlockSpec((None, CHUNK, ATT_W), lambda b, r, n: (b, n, r))
    sh = jax.ShapeDtypeStruct((B, L, dil * ATT_W), F32)
    out, lse = _call(
        body, name=f"attn_fwd_d{dil}", grid=(B, dil, nb),
        in_specs=[pl.BlockSpec((None, CHUNK, ATT_W), lambda b, r, n: (b, n, 3 * r)),
                  pl.BlockSpec((None, L, ATT_W), lambda b, r, n: (b, 0, 3 * r + 1)),
                  pl.BlockSpec((None, L, ATT_W), lambda b, r, n: (b, 0, 3 * r + 2))],
        out_specs=[o, o], out_shape=[sh, sh], sem=("parallel", "parallel", "parallel"))(qv, qv, qv)
    return out.reshape(T, ATT_W), lse.reshape(T, ATT_W)


def _attn_combine(o1, o2, o3, l1, l2, l3):
    T = o1.shape[0]
    tm = _tile(T, 512)

    def body(o1_ref, o2_ref, o3_ref, l1_ref, l2_ref, l3_ref, y_ref, l_ref):
        a, b, c = l1_ref[...], l2_ref[...], l3_ref[...]
        m = jnp.maximum(jnp.maximum(a, b), c)
        ea, eb, ec = jnp.exp(a - m), jnp.exp(b - m), jnp.exp(c - m)
        z = ea + eb + ec
        y_ref[...] = (ea * o1_ref[...] + eb * o2_ref[...] + ec * o3_ref[...]) / z
        l_ref[...] = m + jnp.log(z)

    row = pl.BlockSpec((tm, ATT_W), lambda i: (i, 0))
    sh = jax.ShapeDtypeStruct((T, ATT_W), F32)
    return _call(body, name="attn_combine", grid=(T // tm,), in_specs=[row] * 6, out_specs=[row, row],
                 out_shape=[sh, sh], sem=("parallel",))(o1, o2, o3, l1, l2, l3)


def _attn_bwd(qkv, do, out, lse, dil):
    T = qkv.shape[0]
    B, L = T // SEQ, SEQ // dil
    nb = L // CHUNK
    scale = HEAD ** -0.5

    def body(q_ref, k_ref, v_ref, do_ref, out_ref, lse_ref, dq_ref, dk_ref, dv_ref):
        n = pl.program_id(2)

        @pl.when(n == 0)
        def _():
            dk_ref[...] = jnp.zeros_like(dk_ref)
            dv_ref[...] = jnp.zeros_like(dv_ref)

        q = q_ref[...]
        kk, prv, cur = _prev_cur(k_ref, n)
        vv, _, _ = _prev_cur(v_ref, n)
        mask = _band_mask(n)
        do_ = do_ref[...]
        dlt = do_ * out_ref[...]
        ls = lse_ref[...]
        for t in range(ATT_W // 128):
            sl = slice(128 * t, 128 * (t + 1))
            qt, kt, vt = q[:, sl], kk[:, sl], vv[:, sl]
            dq_pair = jnp.zeros((CHUNK, 128), F32)
            dk_acc = jnp.zeros((2 * CHUNK, 128), F32)
            dv_acc = jnp.zeros((2 * CHUNK, 128), F32)
            for e in range(2):
                lm = _lane_mask(e)
                qm = jnp.where(lm, qt, jnp.zeros_like(qt))
                s = _dot_nt(qm, kt) * scale
                lse_col = ls[:, 128 * t + HEAD * e:128 * t + HEAD * e + 1]
                p = jnp.exp(jnp.where(mask, s - lse_col, NEG))
                dom = jnp.where(lm, do_[:, sl], 0.0).astype(BF)
                dv_acc += _dot_tn(p.astype(BF), dom)
                dp = _dot_nt(dom, vt)
                delta = jnp.sum(jnp.where(lm, dlt[:, sl], 0.0), axis=-1, keepdims=True)
                ds = (p * (dp - delta) * scale).astype(BF)
                dq_pair += jnp.where(lm, _dot(ds, kt), 0.0)
                dk_acc += _dot_tn(ds, qm)
            dq_ref[:, sl] = dq_pair
            dk_ref[pl.ds(cur, CHUNK), sl] += dk_acc[CHUNK:]
            dk_ref[pl.ds(prv, CHUNK), sl] += dk_acc[:CHUNK]
            dv_ref[pl.ds(cur, CHUNK), sl] += dv_acc[CHUNK:]
            dv_ref[pl.ds(prv, CHUNK), sl] += dv_acc[:CHUNK]

    qv = qkv.reshape(B, L, dil * W_QKV)
    view = lambda a: a.reshape(B, L, dil * ATT_W)
    blk = pl.BlockSpec((None, CHUNK, ATT_W), lambda b, r, n: (b, n, r))
    whole = pl.BlockSpec((None, L, ATT_W), lambda b, r, n: (b, 0, r))
    sh = jax.ShapeDtypeStruct((B, L, dil * ATT_W), F32)
    dq, dk, dv = _call(
        body, name=f"attn_bwd_d{dil}", grid=(B, dil, nb),
        in_specs=[pl.BlockSpec((None, CHUNK, ATT_W), lambda b, r, n: (b, n, 3 * r)),
                  pl.BlockSpec((None, L, ATT_W), lambda b, r, n: (b, 0, 3 * r + 1)),
                  pl.BlockSpec((None, L, ATT_W), lambda b, r, n: (b, 0, 3 * r + 2)),
                  blk, blk, blk],
        out_specs=[blk, whole, whole], out_shape=[sh, sh, sh],
        sem=("parallel", "parallel", "arbitrary"))(qv, qv, qv, view(do), view(out), view(lse))
    return dq.reshape(T, ATT_W), dk.reshape(T, ATT_W), dv.reshape(T, ATT_W)


def _sum_branches(parts):
    T = parts[0][0].shape[0]
    tm = _tile(T, 512)

    def body(*refs):
        o_ref = refs[-1]
        for s in range(3):
            acc = refs[s][...] + refs[3 + s][...] + refs[6 + s][...]
            o_ref[:, ATT_W * s:ATT_W * (s + 1)] = acc.astype(BF)

    row = pl.BlockSpec((tm, ATT_W), lambda i: (i, 0))
    flat = [a for tr in parts for a in tr]
    return _call(body, name="attn_sum_branches", grid=(T // tm,), in_specs=[row] * 9,
                 out_specs=pl.BlockSpec((tm, W_QKV), lambda i: (i, 0)),
                 out_shape=jax.ShapeDtypeStruct((T, W_QKV), BF), sem=("parallel",))(*flat)


def _silu(x):
    return x * _sigmoid(x)


def _dsilu(x):
    s = _sigmoid(x)
    return s * (1.0 + x * (1.0 - s))


def _log1p(u):
    return jnp.where(u < 0.01, u * (1.0 - u * (0.5 - u * (1.0 / 3.0))), jnp.log(1.0 + u))


def _softplus(x):
    return jnp.maximum(x, 0.0) + _log1p(jnp.exp(-jnp.abs(x)))


def _cumsum_rows(x, reverse=False):
    n = x.shape[0]
    rows = lax.broadcasted_iota(jnp.int32, x.shape, 0)
    k = 1
    while k < n:
        if reverse:
            x = x + jnp.where(rows < n - k, pltpu.roll(x, n - k, 0), 0.0)
        else:
            x = x + jnp.where(rows >= k, pltpu.roll(x, k, 0), 0.0)
        k *= 2
    return x


def _tri():
    r = lax.broadcasted_iota(jnp.int32, (CHUNK, CHUNK), 0)
    c = lax.broadcasted_iota(jnp.int32, (CHUNK, CHUNK), 1)
    return r >= c


def _row_mask(e):
    return (lax.broadcasted_iota(jnp.int32, (128, 1), 0) // HEAD) == e


def _first_lane(e):
    return lax.broadcasted_iota(jnp.int32, (1, 128), 1) == HEAD * e


def _ssd_pre(x_ref, halo_ref, first, cw_ref, cb_ref, dtb_ref, al_ref, ext):
    row = x_ref[...]
    z = row[:, SSD_CONV_DIM:SSD_CONV_DIM + SSD_W]
    u = row[:, SSD_CONV_DIM + SSD_W:] + dtb_ref[...]
    ext[0:8, :] = jnp.where(first, 0.0, halo_ref[:, 0:SSD_CONV_DIM])
    ext[8:8 + CHUNK, :] = row[:, 0:SSD_CONV_DIM]
    xc = cb_ref[...]
    for j in range(4):
        xc = xc + cw_ref[j:j + 1, :] * ext[pl.ds(5 + j, CHUNK), :]
    xa = _silu(xc)
    dt = _softplus(u)
    a = dt * (-jnp.exp(al_ref[...]))
    A = _cumsum_rows(a)
    return dict(z=z, u=u, xc=xc, xs=xa[:, 0:SSD_W], Bm=xa[:, SSD_W:SSD_W + 256], Cm=xa[:, SSD_W + 256:],
                dt=dt, a=a, A=A, AT=A.T, eA=jnp.exp(A), wdec=jnp.exp(A[CHUNK - 1:CHUNK, :] - A),
                dtot=jnp.exp(A[CHUNK - 1:CHUNK, :]))


def _ssd_y(p, hp_ref, dskip):
    tri = _tri()
    X = p["xs"] * p["dt"]
    Bb = [p["Bm"][:, 128 * g:128 * (g + 1)].astype(BF) for g in range(2)]
    Cb = [p["Cm"][:, 128 * g:128 * (g + 1)].astype(BF) for g in range(2)]
    CB = [_dot_nt(Cb[g], Bb[g]) for g in range(2)]
    tiles = []
    for t in range(3):
        sl = slice(128 * t, 128 * (t + 1))
        hpb = hp_ref[sl, :].astype(BF)
        acc = jnp.zeros((CHUNK, 128), F32)
        for e in range(2):
            h = 2 * t + e
            g, col = h // 3, HEAD * h
            lm = _lane_mask(e)
            L = jnp.exp(jnp.where(tri, p["A"][:, col:col + 1] - p["AT"][col:col + 1, :], NEG))
            yd = _dot((CB[g] * L).astype(BF), jnp.where(lm, X[:, sl], 0.0).astype(BF))
            yo = _dot_nt(Cb[g], hpb) * p["eA"][:, sl]
            acc = acc + jnp.where(lm, yd + yo, 0.0)
        tiles.append(acc)
    return jnp.concatenate(tiles, axis=1) + dskip * p["xs"], X, Bb, Cb, CB


def _group_stats(v):
    g0 = lax.broadcasted_iota(jnp.int32, (1, SSD_W), 1) < SSD_W // 2
    m0 = jnp.sum(jnp.where(g0, v, 0.0), axis=-1, keepdims=True) * (2.0 / SSD_W)
    m1 = jnp.sum(jnp.where(g0, 0.0, v), axis=-1, keepdims=True) * (2.0 / SSD_W)
    return jnp.where(g0, m0, m1)


def _ssd_specs(T, rev):
    B = T // SEQ

    def chunk(b, c):
        return b * N_CHUNK + (N_CHUNK - 1 - c if rev else c)

    row = pl.BlockSpec((CHUNK, W_SSD), lambda b, c: (chunk(b, c), 0))
    halo = pl.BlockSpec((8, W_SSD), lambda b, c: (jnp.maximum(chunk(b, c) * (CHUNK // 8) - 1, 0), 0))
    hp = pl.BlockSpec((None, SSD_W, SSD_STATE), lambda b, c: (chunk(b, c), 0, 0))
    y = pl.BlockSpec((CHUNK, SSD_W), lambda b, c: (chunk(b, c), 0))
    const = lambda r, w: pl.BlockSpec((r, w), lambda b, c: (0, 0))
    params = [const(4, SSD_CONV_DIM), const(1, SSD_CONV_DIM)] + [const(1, SSD_W)] * 4
    return B, row, halo, hp, y, const, params


def _ssd_fwd(sin, conv_w, conv_b, dtb, alog, dskip, norm_g):
    T = sin.shape[0]
    B, row, halo, hp, y, const, params = _ssd_specs(T, False)

    def body(x_ref, halo_ref, cw_ref, cb_ref, dtb_ref, al_ref, dk_ref, ng_ref, y_ref, hp_ref, ext, hst):
        c = pl.program_id(1)

        @pl.when(c == 0)
        def _():
            hst[...] = jnp.zeros_like(hst)

        p = _ssd_pre(x_ref, halo_ref, c == 0, cw_ref, cb_ref, dtb_ref, al_ref, ext)
        yv, X, Bb, Cb, CB = _ssd_y(p, hst, dk_ref[...])
        hp_ref[...] = hst[...]
        for t in range(3):
            sl = slice(128 * t, 128 * (t + 1))
            old = hst[sl, :]
            new = old
            for e in range(2):
                h = 2 * t + e
                g, col = h // 3, HEAD * h
                st = _dot_tn(jnp.where(_lane_mask(e), X[:, sl] * p["wdec"][:, sl], 0.0).astype(BF), Bb[g])
                new = jnp.where(_row_mask(e), old * p["dtot"][:, col:col + 1] + st, new)
            hst[sl, :] = new
        y2 = yv * _silu(p["z"])
        r = lax.rsqrt(_group_stats(y2 * y2) + RMS_EPS)
        y_ref[...] = y2 * r * ng_ref[...]

    return _call(body, name="ssd_fwd", grid=(B, N_CHUNK), in_specs=[row, halo] + params, out_specs=[y, hp],
                 out_shape=[jax.ShapeDtypeStruct((T, SSD_W), F32),
                            jax.ShapeDtypeStruct((T // CHUNK, SSD_W, SSD_STATE), F32)],
                 scratch=[pltpu.VMEM((8 + CHUNK, SSD_CONV_DIM), F32), pltpu.VMEM((SSD_W, SSD_STATE), F32)],
                 sem=("parallel", "arbitrary"))(sin, sin, conv_w, conv_b, dtb, alog, dskip, norm_g)


def _ssd_bwd(sin, hprev, dy3, conv_w, conv_b, dtb, alog, dskip, norm_g):
    T = sin.shape[0]
    B, row, halo, hp, y, const, params = _ssd_specs(T, True)

    def body(x_ref, halo_ref, hp_ref, dy_ref, cw_ref, cb_ref, dtb_ref, al_ref, dk_ref, ng_ref,
             dx_ref, dcw_ref, dcb_ref, dvec_ref, ext, ext2, dh):
        c = pl.program_id(1)

        @pl.when((pl.program_id(0) == 0) & (c == 0))
        def _():
            dcw_ref[...] = jnp.zeros_like(dcw_ref)
            dcb_ref[...] = jnp.zeros_like(dcb_ref)
            dvec_ref[...] = jnp.zeros_like(dvec_ref)

        @pl.when(c == 0)
        def _():
            dh[...] = jnp.zeros_like(dh)
            ext2[CHUNK:CHUNK + 8, :] = jnp.zeros((8, SSD_CONV_DIM), F32)

        p = _ssd_pre(x_ref, halo_ref, c == N_CHUNK - 1, cw_ref, cb_ref, dtb_ref, al_ref, ext)
        dskip_ = dk_ref[...]
        yv, X, Bb, Cb, CB = _ssd_y(p, hp_ref, dskip_)
        xs, z, A, AT = p["xs"], p["z"], p["A"], p["AT"]

        sz = _silu(z)
        y2 = yv * sz
        r = lax.rsqrt(_group_stats(y2 * y2) + RMS_EPS)
        dy3_ = dy_ref[...]
        uu = dy3_ * ng_ref[...]
        dy2 = r * (uu - y2 * (r * r * _group_stats(uu * y2)))
        dy = dy2 * sz
        dz = dy2 * yv * _dsilu(z)

        tri = _tri()
        rows = lax.broadcasted_iota(jnp.int32, (CHUNK, 1), 0)
        dG = [jnp.zeros((CHUNK, CHUNK), F32) for _ in range(2)]
        dB = [jnp.zeros((CHUNK, SSD_STATE), F32) for _ in range(2)]
        dC = [jnp.zeros((CHUNK, SSD_STATE), F32) for _ in range(2)]
        dX_t, dA_t, ddtx_t = [], [], []
        for t in range(3):
            sl = slice(128 * t, 128 * (t + 1))
            hp_t = hp_ref[sl, :]
            hpb = hp_t.astype(BF)
            dhc = dh[sl, :]
            dh_new = jnp.zeros((128, SSD_STATE), F32)
            dX = jnp.zeros((CHUNK, 128), F32)
            dA = jnp.zeros((CHUNK, 128), F32)
            ddtx = jnp.zeros((CHUNK, 128), F32)
            for e in range(2):
                h = 2 * t + e
                g, col = h // 3, HEAD * h
                lm, rm, fl = _lane_mask(e), _row_mask(e), _first_lane(e)
                L = jnp.exp(jnp.where(tri, A[:, col:col + 1] - AT[col:col + 1, :], NEG))
                Mf = CB[g] * L
                Xm = jnp.where(lm, X[:, sl], 0.0)
                Xmb = Xm.astype(BF)
                dyh = jnp.where(lm, dy[:, sl], 0.0)
                dyb = dyh.astype(BF)
                dXh = _dot_tn(Mf.astype(BF), dyb)
                dM = jnp.where(tri, _dot_nt(dyb, Xmb), 0.0)
                Wm = dM * Mf
                dAc = jnp.sum(Wm, axis=-1, keepdims=True) - jnp.sum(Wm.T, axis=-1, keepdims=True)
                dG[g] = dG[g] + dM * L
                eAt = p["eA"][:, sl]
                yo = _dot_nt(Cb[g], hpb)
                dyo = (dyh * eAt).astype(BF)
                dC[g] = dC[g] + _dot(dyo, hpb)
                dh_new = dh_new + _dot_tn(dyo, Cb[g])
                dAc = dAc + jnp.sum(dyh * yo * eAt, axis=-1, keepdims=True)
                dHn = jnp.where(rm, dhc, 0.0)
                dHnb = dHn.astype(BF)
                dec = p["dtot"][:, col:col + 1]
                dh_new = dh_new + dec * dHn
                Z = _dot_nt(Bb[g], dHnb)
                wt = p["wdec"][:, sl]
                xi = jnp.sum(Xm * Z, axis=-1, keepdims=True) * p["wdec"][:, col:col + 1]
                dXh = dXh + wt * Z
                dB[g] = dB[g] + _dot(jnp.where(lm, X[:, sl] * wt, 0.0).astype(BF), dHnb)
                dAtot = jnp.sum(xi, axis=0, keepdims=True) + dec * jnp.sum(
                    jnp.sum(dHn * hp_t, axis=-1, keepdims=True), axis=0, keepdims=True)
                dAc = dAc - xi + jnp.where(rows == CHUNK - 1, dAtot, 0.0)
                dA = dA + jnp.where(fl, dAc, 0.0)
                dX = dX + dXh
                ddtx = ddtx + jnp.where(fl, jnp.sum(dXh * xs[:, sl], axis=-1, keepdims=True), 0.0)
            dh[sl, :] = dh_new
            dX_t.append(dX)
            dA_t.append(dA)
            ddtx_t.append(ddtx)
        for g in range(2):
            dGb = dG[g].astype(BF)
            dC[g] = dC[g] + _dot(dGb, Bb[g])
            dB[g] = dB[g] + _dot_tn(dGb, Cb[g])
        dXf = jnp.concatenate(dX_t, axis=1)
        da = _cumsum_rows(jnp.concatenate(dA_t, axis=1), reverse=True)
        ddt = da * (-jnp.exp(al_ref[...])) + jnp.concatenate(ddtx_t, axis=1)
        du = ddt * _sigmoid(p["u"])
        dxs = dXf * p["dt"] + dskip_ * dy
        dxc = jnp.concatenate([dxs, dB[0], dB[1], dC[0], dC[1]], axis=1) * _dsilu(p["xc"])
        ext2[0:CHUNK, :] = dxc
        dxbc = jnp.zeros((CHUNK, SSD_CONV_DIM), F32)
        for j in range(4):
            dxbc = dxbc + cw_ref[j:j + 1, :] * ext2[pl.ds(3 - j, CHUNK), :]
            dcw_ref[j:j + 1, :] += jnp.sum(dxc * ext[pl.ds(5 + j, CHUNK), :], axis=0, keepdims=True)
        ext2[CHUNK:CHUNK + 8, :] = dxc[0:8, :]
        dcb_ref[...] += jnp.sum(dxc, axis=0, keepdims=True)
        dvec_ref[0:1, :] += jnp.sum(du, axis=0, keepdims=True)
        dvec_ref[1:2, :] += jnp.sum(da * p["a"], axis=0, keepdims=True)
        dvec_ref[2:3, :] += jnp.sum(dy * xs, axis=0, keepdims=True)
        dvec_ref[3:4, :] += jnp.sum(dy3_ * y2 * r, axis=0, keepdims=True)
        dx_ref[...] = jnp.concatenate([dxbc, dz, du], axis=1).astype(BF)

    return _call(body, name="ssd_bwd", grid=(B, N_CHUNK), in_specs=[row, halo, hp, y] + params,
                 out_specs=[row, const(4, SSD_CONV_DIM), const(1, SSD_CONV_DIM), const(8, SSD_W)],
                 out_shape=[jax.ShapeDtypeStruct((T, W_SSD), BF), jax.ShapeDtypeStruct((4, SSD_CONV_DIM), F32),
                            jax.ShapeDtypeStruct((1, SSD_CONV_DIM), F32), jax.ShapeDtypeStruct((8, SSD_W), F32)],
                 scratch=[pltpu.VMEM((8 + CHUNK, SSD_CONV_DIM), F32), pltpu.VMEM((8 + CHUNK, SSD_CONV_DIM), F32),
                          pltpu.VMEM((SSD_W, SSD_STATE), F32)],
                 sem=("arbitrary", "arbitrary"))(sin, sin, hprev, dy3, conv_w, conv_b, dtb, alog, dskip, norm_g)


def _sgu_core(uv_ref, g_ref, b_ref, w_ref, bias_ref):
    x = uv_ref[...]
    cdf = 0.5 * (1.0 + lax.erf(x * (2.0 ** -0.5)))
    ge = x * cdf
    dge = cdf + x * jnp.exp(-0.5 * x * x) * ((2.0 * math.pi) ** -0.5)
    u, v = ge[:, 0:SGU_W], ge[:, SGU_W:]
    vc = v - jnp.mean(v, axis=-1, keepdims=True)
    rstd = lax.rsqrt(jnp.mean(vc * vc, axis=-1, keepdims=True) + LN_EPS)
    vhat = vc * rstd
    vn = vhat * g_ref[...] + b_ref[...]
    tri = _tri()
    wc = [jnp.where(tri, w_ref[gi], 0.0).astype(BF) for gi in range(4)]
    vm = [jnp.where(_lane_mask(gi % 2), vn[:, 128 * (gi // 2):128 * (gi // 2 + 1)], 0.0).astype(BF) for gi in range(4)]
    mixed = jnp.concatenate([_dot(wc[2 * t], vm[2 * t]) + _dot(wc[2 * t + 1], vm[2 * t + 1]) for t in range(2)],
                            axis=1) + bias_ref[...]
    return dict(dge=dge, u=u, rstd=rstd, vhat=vhat, wc=wc, vm=vm, mixed=mixed)


def _sgu_specs():
    vec = pl.BlockSpec((1, SGU_W), lambda i: (0, 0))
    return [pl.BlockSpec((CHUNK, W_UV), lambda i: (i, 0)), vec, vec,
            pl.BlockSpec((4, CHUNK, CHUNK), lambda i: (0, 0, 0)), pl.BlockSpec((CHUNK, SGU_W), lambda i: (0, 0))]


def _sgu_fwd(uv, ln_g, ln_b, w, bias):
    T = uv.shape[0]

    def body(uv_ref, g_ref, b_ref, w_ref, bias_ref, y_ref):
        s = _sgu_core(uv_ref, g_ref, b_ref, w_ref, bias_ref)
        y_ref[...] = s["u"] * s["mixed"]

    return _call(body, name="sgu_fwd", grid=(T // CHUNK,), in_specs=_sgu_specs(),
                 out_specs=pl.BlockSpec((CHUNK, SGU_W), lambda i: (i, 0)),
                 out_shape=jax.ShapeDtypeStruct((T, SGU_W), F32), sem=("parallel",))(uv, ln_g, ln_b, w, bias)


def _sgu_bwd(uv, dy, ln_g, ln_b, w, bias):
    T = uv.shape[0]

    def body(uv_ref, dy_ref, g_ref, b_ref, w_ref, bias_ref, dx_ref, dw_ref, dbias_ref, dln_ref):
        @pl.when(pl.program_id(0) == 0)
        def _():
            dw_ref[...] = jnp.zeros_like(dw_ref)
            dbias_ref[...] = jnp.zeros_like(dbias_ref)
            dln_ref[...] = jnp.zeros_like(dln_ref)

        s = _sgu_core(uv_ref, g_ref, b_ref, w_ref, bias_ref)
        dy_ = dy_ref[...]
        du = dy_ * s["mixed"]
        dmix = dy_ * s["u"]
        dbias_ref[...] += dmix
        tri = _tri()
        dvn_t = []
        for t in range(2):
            acc = jnp.zeros((CHUNK, 128), F32)
            for e in range(2):
                gi = 2 * t + e
                dmg = jnp.where(_lane_mask(e), dmix[:, 128 * t:128 * (t + 1)], 0.0).astype(BF)
                acc = acc + _dot_tn(s["wc"][gi], dmg)
                dw_ref[gi] += jnp.where(tri, _dot_nt(dmg, s["vm"][gi]), 0.0)
            dvn_t.append(acc)
        dvn = jnp.concatenate(dvn_t, axis=1)
        dln_ref[0:1, :] += jnp.sum(dvn * s["vhat"], axis=0, keepdims=True)
        dln_ref[1:2, :] += jnp.sum(dvn, axis=0, keepdims=True)
        dvh = dvn * g_ref[...]
        dv = s["rstd"] * (dvh - jnp.mean(dvh, axis=-1, keepdims=True)
                          - s["vhat"] * jnp.mean(dvh * s["vhat"], axis=-1, keepdims=True))
        dx_ref[...] = (jnp.concatenate([du, dv], axis=1) * s["dge"]).astype(BF)

    ins = _sgu_specs()
    return _call(body, name="sgu_bwd", grid=(T // CHUNK,),
                 in_specs=[ins[0], pl.BlockSpec((CHUNK, SGU_W), lambda i: (i, 0))] + ins[1:],
                 out_specs=[pl.BlockSpec((CHUNK, W_UV), lambda i: (i, 0)),
                            pl.BlockSpec((4, CHUNK, CHUNK), lambda i: (0, 0, 0)),
                            pl.BlockSpec((CHUNK, SGU_W), lambda i: (0, 0)), pl.BlockSpec((8, SGU_W), lambda i: (0, 0))],
                 out_shape=[jax.ShapeDtypeStruct((T, W_UV), BF), jax.ShapeDtypeStruct((4, CHUNK, CHUNK), F32),
                            jax.ShapeDtypeStruct((CHUNK, SGU_W), F32), jax.ShapeDtypeStruct((8, SGU_W), F32)],
                 sem=("arbitrary",))(uv, dy, ln_g, ln_b, w, bias)


def _adamw(w, g, m, v):
    R, C = w.shape
    tr = _tile(R, 256) if R % 8 == 0 else R

    def body(w_ref, g_ref, m_ref, v_ref, d_ref, nm_ref, nv_ref):
        g_ = g_ref[...]
        m2 = ADAM_B1 * m_ref[...] + (1.0 - ADAM_B1) * g_
        v2 = ADAM_B2 * v_ref[...] + (1.0 - ADAM_B2) * (g_ * g_)
        m_hat = m2 / (1.0 - ADAM_B1 ** ADAM_STEP)
        v_hat = v2 / (1.0 - ADAM_B2 ** ADAM_STEP)
        d_ref[...] = -ADAM_LR * (m_hat / (jnp.sqrt(v_hat) + ADAM_EPS) + ADAM_WD * w_ref[...])
        nm_ref[...] = m2
        nv_ref[...] = v2

    blk = pl.BlockSpec((tr, C), lambda i: (i, 0))
    sh = jax.ShapeDtypeStruct((R, C), F32)
    return _call(body, name="adamw", grid=(R // tr,), in_specs=[blk] * 4, out_specs=[blk] * 3,
                 out_shape=[sh] * 3, sem=("parallel",))(w, g, m, v)


def _pair_add(gbuf, rsib, c):
    NS, _, R, C = gbuf.shape
    tr = 512

    def body(c_ref, a_ref, b_ref, o_ref):
        o_ref[...] = (a_ref[...] + b_ref[...]).astype(BF)

    blk = pl.BlockSpec((None, tr, C), lambda j, i, c_ref: (j, i, 0))
    return pl.pallas_call(
        body, name="rs_pair_add",
        grid_spec=pltpu.PrefetchScalarGridSpec(
            num_scalar_prefetch=1, grid=(NS, pl.cdiv(R, tr)),
            in_specs=[pl.BlockSpec((None, None, tr, C), lambda j, i, c_ref: (j, c_ref[0], i, 0)), blk],
            out_specs=blk),
        out_shape=jax.ShapeDtypeStruct((NS, R, C), BF),
        compiler_params=pltpu.CompilerParams(dimension_semantics=("parallel", "parallel")),
    )(jnp.reshape(c, (1,)).astype(jnp.int32), gbuf, rsib)


def _chip_sum(pair, recv, me, c):
    NS, R, C = pair.shape
    tr = 512

    def body(s_ref, own_ref, p_ref, o_ref):
        p = [jnp.where(s_ref[0] == j, own_ref[...], p_ref[j]).astype(F32) for j in range(4)]
        o_ref[...] = ((p[0] + p[1]) + p[2]) + p[3]

    return pl.pallas_call(
        body, name="rs_chip_sum",
        grid_spec=pltpu.PrefetchScalarGridSpec(
            num_scalar_prefetch=1, grid=(pl.cdiv(R, tr),),
            in_specs=[pl.BlockSpec((None, tr, C), lambda i, s: (s[0], i, 0)),
                      pl.BlockSpec((NS, tr, C), lambda i, s: (0, i, 0))],
            out_specs=pl.BlockSpec((None, tr, C), lambda i, s: (s[1], i, 0))),
        out_shape=jax.ShapeDtypeStruct((2, R, C), F32),
        compiler_params=pltpu.CompilerParams(dimension_semantics=("parallel",)),
    )(jnp.stack([me, c]).astype(jnp.int32), pair, recv)


MESH = pl.DeviceIdType.MESH
ANY = pl.BlockSpec(memory_space=pl.ANY)


def _place():
    x, y, c = lax.axis_index("x"), lax.axis_index("y"), lax.axis_index("c")
    return x, y, c, [(1 - x, y), (x, 1 - y), (1 - x, 1 - y)]


HBM = pl.BlockSpec(memory_space=pltpu.HBM)
SEM = pl.BlockSpec(memory_space=pltpu.SEMAPHORE)
EFFECT = pltpu.SideEffectType.DATAFLOW_SIDE_EFFECTING


class _Split:
    def __init__(self, tag, arrays, copies, n_copies):
        self.tag, self.copies, k = tag, copies, len(arrays)

        def body(*refs):
            for cp in copies(refs[:k], refs[k], refs[k + 1]):
                cp.start()
            refs[-1][...] = jnp.zeros_like(refs[-1])

        out = pl.pallas_call(
            body, name=tag + "_start",
            out_shape=(pltpu.SemaphoreType.DMA((n_copies,)), pltpu.SemaphoreType.DMA((n_copies,)),
                       *[pltpu.HBM(a.shape, a.dtype) for a in arrays], jax.ShapeDtypeStruct((8, 128), F32)),
            in_specs=[HBM] * k, out_specs=(SEM, SEM, *[HBM] * k, pl.BlockSpec(memory_space=pltpu.VMEM)),
            input_output_aliases={i: 2 + i for i in range(k)},
            compiler_params=pltpu.CompilerParams(has_side_effects=EFFECT),
        )(*[pltpu.with_memory_space_constraint(a, pltpu.HBM) for a in arrays])
        self.send, self.recv, self.arrays, self.token = out[0], out[1], list(out[2:2 + k]), out[-1][0, 0]

    def wait(self, after):
        k, copies = len(self.arrays), self.copies

        def body(*refs):
            for cp in copies(refs[:k], refs[k], refs[k + 1]):
                cp.wait_send()
                cp.wait_recv()

        return list(pl.pallas_call(
            body, name=self.tag + "_wait", out_shape=tuple(pltpu.HBM(a.shape, a.dtype) for a in self.arrays),
            in_specs=[HBM] * k + [SEM, SEM, ANY], out_specs=tuple([HBM] * k),
            input_output_aliases={i: i for i in range(k)},
            compiler_params=pltpu.CompilerParams(has_side_effects=EFFECT),
        )(*self.arrays, self.send, self.recv, after))


def _gather_start(arrs, tag):
    n = len(arrs)
    me = 2 * lax.axis_index("x") + lax.axis_index("y")
    lands = [lax.dynamic_update_index_in_dim(jnp.zeros((4,) + a.shape, a.dtype), a, me, 0) for a in arrs]

    def copies(refs, send, recv):
        x, y, c, chips = _place()
        return [pltpu.make_async_remote_copy(
            src_ref=refs[k], dst_ref=refs[n + k].at[2 * x + y], send_sem=send.at[3 * k + r],
            recv_sem=recv.at[3 * k + r], device_id=(px, py, c), device_id_type=MESH)
            for k in range(n) for r, (px, py) in enumerate(chips)]

    return _Split("gather_" + tag, list(arrs) + lands, copies, 3 * n)


def _to_sibling_start(gbuf, tag):
    NS, _, R, C = gbuf.shape

    def copies(refs, send, recv):
        x, y, c, _ = _place()
        return [pltpu.make_async_remote_copy(
            src_ref=refs[0].at[j, 1 - c], dst_ref=refs[1].at[j], send_sem=send.at[j], recv_sem=recv.at[j],
            device_id=(x, y, 1 - c), device_id_type=MESH) for j in range(NS)]

    return _Split("rs_sibling_" + tag, [gbuf, lax.empty((NS, R, C), gbuf.dtype)], copies, NS)


def _to_chips_start(pbuf, tag):
    def copies(refs, send, recv):
        x, y, c, chips = _place()
        return [pltpu.make_async_remote_copy(
            src_ref=refs[0].at[2 * px + py], dst_ref=refs[1].at[2 * x + y], send_sem=send.at[r], recv_sem=recv.at[r],
            device_id=(px, py, c), device_id_type=MESH) for r, (px, py) in enumerate(chips)]

    return _Split("rs_chips_" + tag, [pbuf, lax.empty(pbuf.shape, pbuf.dtype)], copies, 3)


def _join_start(full, tag):
    def copies(refs, send, recv):
        x, y, c, _ = _place()
        return [pltpu.make_async_remote_copy(
            src_ref=refs[0].at[c], dst_ref=refs[0].at[c], send_sem=send.at[0], recv_sem=recv.at[0],
            device_id=(x, y, 1 - c), device_id_type=MESH)]

    return _Split("rs_join_" + tag, [full], copies, 1)


def _all_reduce_small(v):
    R, C = v.shape

    def body(v_ref, o_ref, g_ref, send, recv, loc):
        x, y, c, chips = _place()
        me, sibling = (x, y, c), (x, y, 1 - c)

        def rows(px, py, pc):
            return g_ref.at[4 * px + 2 * py + pc]

        def copy(k, block, to, src=None):
            return pltpu.make_async_remote_copy(
                src_ref=rows(*block) if src is None else src, dst_ref=rows(*block),
                send_sem=send.at[k], recv_sem=recv.at[k], device_id=to, device_id_type=MESH)

        mine = pltpu.make_async_copy(v_ref, rows(*me), loc)
        mine.start()
        first = [copy(0, me, sibling, src=v_ref)]
        first += [copy(1 + j, me, (*chip, c), src=v_ref) for j, chip in enumerate(chips)]
        for cp in first:
            cp.start()
        passed = [copy(4 + j, (*chip, c), sibling) for j, chip in enumerate(chips)]
        for j, chip in enumerate(chips):
            copy(1 + j, (*chip, c), me).wait_recv()
            passed[j].start()
        copy(0, sibling, me).wait_recv()
        for j, chip in enumerate(chips):
            copy(4 + j, (*chip, 1 - c), me).wait_recv()
        for cp in first + passed:
            cp.wait_send()
        mine.wait()
        acc = g_ref[0]
        for d in range(1, 8):
            acc = acc + g_ref[d]
        o_ref[...] = acc

    vm = pl.BlockSpec(memory_space=pltpu.VMEM)
    return pl.pallas_call(
        body, name="all_reduce_small", in_specs=[vm], out_specs=[vm, vm],
        out_shape=[jax.ShapeDtypeStruct((R, C), F32), jax.ShapeDtypeStruct((8, R, C), F32)],
        scratch_shapes=[pltpu.SemaphoreType.DMA((7,)), pltpu.SemaphoreType.DMA((7,)), pltpu.SemaphoreType.DMA],
    )(v)[0]


WEIGHTS = ['ffn1_norm', 'ffn1_w_gate', 'ffn1_w_up', 'ffn1_w_down', 'mix_norm', 'w_in', 'conv_w', 'conv_b', 'dt_bias',
           'a_log', 'd_skip', 'ssd_norm', 'sgu_ln_g', 'sgu_ln_b', 'sgu_w', 'sgu_b', 'w_out', 'ffn2_norm',
           'ffn2_w_gate', 'ffn2_w_up', 'ffn2_w_down', 'final_norm']
SHARDED = ['ffn1_w_gate', 'ffn1_w_up', 'ffn1_w_down', 'w_in', 'conv_w', 'w_out', 'ffn2_w_gate', 'ffn2_w_up',
           'ffn2_w_down']
SMALL = [n for n in WEIGHTS if n not in SHARDED]
GROUPS = [("ffn1", ["ffn1_w_gate", "ffn1_w_up", "ffn1_w_down"]), ("mix", ["w_in", "conv_w", "w_out"]),
          ("ffn2", ["ffn2_w_gate", "ffn2_w_up", "ffn2_w_down"])]
RS_COLS = 1024
DEPTH = 2


def _pack_w_in(w):
    return jnp.concatenate([w[..., 0:1152], w[..., 1536:2432], w[..., 1152:1536],
                            jnp.repeat(w[..., 2432:2438], HEAD, axis=-1), w[..., 2438:2950]], axis=-1)


def _unpack_w_in(dq, ds, du):
    return jnp.concatenate([dq, ds[:, 896:1280], ds[:, 0:896], ds[:, 1280::HEAD], du], axis=-1)


def _ffn_fwd(x, g, wg, wu, wd):
    hb = _rms_fwd(x, g)
    G, U, A = _ffn_up(hb, wg, wu)
    return _ffn_down(A, wd, x), (x, hb, G, U, A)


def _ffn_bwd(dxo, saved, g, wg, wu, wd):
    x, hb, G, U, A = saved
    dG, dU = _ffn_bwd_act(dxo, wd, G, U)
    dwd = _ffn_bwd_wd(A, dxo)
    dwg, dwu = _ffn_bwd_wgu(hb, dG, dU)
    dh = _ffn_bwd_dh(dG, dU, wg, wu)
    dx, dg = _rms_bwd(x, g, dh, dxo)
    return dx, dg, dwg, dwu, dwd


def _mix_fwd(x, P):
    hb = _rms_fwd(x, P["mix_norm"])
    qkv = _mm_nn(hb, P["w_qkv"], out_dtype=BF)
    sin = _mm_nn(hb, P["w_ssd"])
    uv = _mm_nn(hb, P["w_uv"])
    o1, l1 = _attn_fwd(qkv, 1)
    o2, l2 = _attn_fwd(qkv, 4)
    o3, l3 = _attn_fwd(qkv, 16)
    y_att, lse = _attn_combine(o1, o2, o3, l1, l2, l3)
    y_ssd, hprev = _ssd_fwd(sin, *P["ssd"])
    y_sgu = _sgu_fwd(uv, *P["sgu"])
    ycat = jnp.concatenate([y_att, y_ssd, y_sgu], axis=1).astype(BF)
    return _mm_nn(ycat, P["w_out"], res=x), (x, hb, qkv, sin, uv, y_att, lse, hprev, ycat)


def _mix_bwd(dxo, saved, P):
    x, hb, qkv, sin, uv, y_att, lse, hprev, ycat = saved
    dycat = _mm_nt(dxo, P["w_out"])
    dwout = _mm_tn(ycat, dxo)
    dy_att, dy_ssd, dy_sgu = dycat[:, 0:ATT_W], dycat[:, ATT_W:ATT_W + SSD_W], dycat[:, ATT_W + SSD_W:]
    dqkv = _sum_branches([_attn_bwd(qkv, dy_att, y_att, lse, d) for d in DILATIONS])
    dsin, dcw, dcb, dvec = _ssd_bwd(sin, hprev, dy_ssd, *P["ssd"])
    duv, dsw, dsbias, dln = _sgu_bwd(uv, dy_sgu, *P["sgu"])
    dwin = _unpack_w_in(_mm_tn(hb, dqkv), _mm_tn(hb, dsin), _mm_tn(hb, duv))
    dh = _mm_nt(dqkv, P["w_qkv"])
    dh = _mm_nt(dsin, P["w_ssd"], res=dh)
    dh = _mm_nt(duv, P["w_uv"], res=dh)
    dx, dg = _rms_bwd(x, P["mix_norm"], dh, dxo)
    grads = dict(
        mix_norm=dg[0], w_in=dwin, conv_w=dcw, conv_b=dcb[0], dt_bias=dvec[0, ::HEAD], a_log=dvec[1, ::HEAD],
        d_skip=jnp.sum(dvec[2].reshape(6, HEAD), axis=-1), ssd_norm=dvec[3], sgu_ln_g=dln[0], sgu_ln_b=dln[1],
        sgu_w=dsw, sgu_b=jnp.sum(dsbias.reshape(CHUNK, 4, HEAD), axis=-1).T, w_out=dwout)
    return dx, grads


def _shard_major(name, g):
    if name == "conv_w":
        return g.reshape(4, 4, 224).transpose(1, 0, 2).reshape(4, -1)
    return g.reshape(4, -1)


def kernel(x, ffn1_norm, ffn1_w_gate, ffn1_w_up, ffn1_w_down, mix_norm, w_in, conv_w, conv_b, dt_bias, a_log, d_skip, ssd_norm, sgu_ln_g, sgu_ln_b, sgu_w, sgu_b, w_out, ffn2_norm, ffn2_w_gate, ffn2_w_up, ffn2_w_down, final_norm, loss_target, m_ffn1_norm, m_ffn1_w_gate, m_ffn1_w_up, m_ffn1_w_down, m_mix_norm, m_w_in, m_conv_w, m_conv_b, m_dt_bias, m_a_log, m_d_skip, m_ssd_norm, m_sgu_ln_g, m_sgu_ln_b, m_sgu_w, m_sgu_b, m_w_out, m_ffn2_norm, m_ffn2_w_gate, m_ffn2_w_up, m_ffn2_w_down, m_final_norm, v_ffn1_norm, v_ffn1_w_gate, v_ffn1_w_up, v_ffn1_w_down, v_mix_norm, v_w_in, v_conv_w, v_conv_b, v_dt_bias, v_a_log, v_d_skip, v_ssd_norm, v_sgu_ln_g, v_sgu_ln_b, v_sgu_w, v_sgu_b, v_w_out, v_ffn2_norm, v_ffn2_w_gate, v_ffn2_w_up, v_ffn2_w_down, v_final_norm):
    given = dict(x=x, ffn1_norm=ffn1_norm, ffn1_w_gate=ffn1_w_gate, ffn1_w_up=ffn1_w_up, ffn1_w_down=ffn1_w_down, mix_norm=mix_norm, w_in=w_in, conv_w=conv_w, conv_b=conv_b, dt_bias=dt_bias, a_log=a_log, d_skip=d_skip, ssd_norm=ssd_norm, sgu_ln_g=sgu_ln_g, sgu_ln_b=sgu_ln_b, sgu_w=sgu_w, sgu_b=sgu_b, w_out=w_out, ffn2_norm=ffn2_norm, ffn2_w_gate=ffn2_w_gate, ffn2_w_up=ffn2_w_up, ffn2_w_down=ffn2_w_down, final_norm=final_norm, loss_target=loss_target, m_ffn1_norm=m_ffn1_norm, m_ffn1_w_gate=m_ffn1_w_gate, m_ffn1_w_up=m_ffn1_w_up, m_ffn1_w_down=m_ffn1_w_down, m_mix_norm=m_mix_norm, m_w_in=m_w_in, m_conv_w=m_conv_w, m_conv_b=m_conv_b, m_dt_bias=m_dt_bias, m_a_log=m_a_log, m_d_skip=m_d_skip, m_ssd_norm=m_ssd_norm, m_sgu_ln_g=m_sgu_ln_g, m_sgu_ln_b=m_sgu_ln_b, m_sgu_w=m_sgu_w, m_sgu_b=m_sgu_b, m_w_out=m_w_out, m_ffn2_norm=m_ffn2_norm, m_ffn2_w_gate=m_ffn2_w_gate, m_ffn2_w_up=m_ffn2_w_up, m_ffn2_w_down=m_ffn2_w_down, m_final_norm=m_final_norm, v_ffn1_norm=v_ffn1_norm, v_ffn1_w_gate=v_ffn1_w_gate, v_ffn1_w_up=v_ffn1_w_up, v_ffn1_w_down=v_ffn1_w_down, v_mix_norm=v_mix_norm, v_w_in=v_w_in, v_conv_w=v_conv_w, v_conv_b=v_conv_b, v_dt_bias=v_dt_bias, v_a_log=v_a_log, v_d_skip=v_d_skip, v_ssd_norm=v_ssd_norm, v_sgu_ln_g=v_sgu_ln_g, v_sgu_ln_b=v_sgu_ln_b, v_sgu_w=v_sgu_w, v_sgu_b=v_sgu_b, v_w_out=v_w_out, v_ffn2_norm=v_ffn2_norm, v_ffn2_w_gate=v_ffn2_w_gate, v_ffn2_w_up=v_ffn2_w_up, v_ffn2_w_down=v_ffn2_w_down, v_final_norm=v_final_norm)
    T = given["x"].shape[0] * given["x"].shape[1]
    D = given["x"].shape[2]
    x0 = given["x"].reshape(T, D)
    tgt = given["loss_target"].reshape(T, D)
    c = lax.axis_index("c")

    bf = {n: given[n].astype(BF) for n in SHARDED if n not in ("w_in", "conv_w")}
    bf["w_in"] = _pack_w_in(given["w_in"]).astype(BF)
    bf["conv_w"] = given["conv_w"]
    gathers = {(i, gname): _gather_start([bf[n][i] for n in names], f"l{i}_{gname}")
               for i in range(DEPTH) for gname, names in GROUPS}
    token = functools.reduce(lambda a, b: a + b, [g.token for g in gathers.values()])

    def gathered(i, gname, after):
        return gathers[(i, gname)].wait(after)[3:]

    def mix_params(i, got):
        win = got[0].reshape(D, W_QKV + W_SSD + W_UV)
        rep = lambda v: jnp.repeat(v, HEAD)[None]
        ssd = (got[1].transpose(1, 0, 2).reshape(4, SSD_CONV_DIM), given["conv_b"][i][None],
               rep(given["dt_bias"][i]), rep(given["a_log"][i]), rep(given["d_skip"][i]), given["ssd_norm"][i][None])
        sgu = (given["sgu_ln_g"][i][None], given["sgu_ln_b"][i][None], given["sgu_w"][i],
               jnp.repeat(given["sgu_b"][i].T, HEAD, axis=1))
        return dict(mix_norm=given["mix_norm"][i][None], w_qkv=win[:, 0:W_QKV], w_ssd=win[:, W_QKV:W_QKV + W_SSD],
                    w_uv=win[:, W_QKV + W_SSD:], w_out=got[2].reshape(-1, D), ssd=ssd, sgu=sgu)

    x = x0
    tape = []
    for i in range(DEPTH):
        P = dict(ffn1=(given["ffn1_norm"][i][None] + (token if i == 0 else 0.0), *gathered(i, "ffn1", x)))
        x, s1 = _ffn_fwd(x, *P["ffn1"])
        P.update(mix_params(i, gathered(i, "mix", x)))
        x, s2 = _mix_fwd(x, P)
        P["ffn2"] = (given["ffn2_norm"][i][None], *gathered(i, "ffn2", x))
        x, s3 = _ffn_fwd(x, *P["ffn2"])
        tape.append((P, s1, s2, s3))
    loss_part, dx, dgf = _final_loss(x, given["final_norm"][None], tgt)

    me = 2 * lax.axis_index("x") + lax.axis_index("y")
    jobs = []

    def rs_begin(i, gname, gd):
        pieces = [_shard_major(n, gd[n]) for n in dict(GROUPS)[gname]]
        n_el = sum(p.shape[1] for p in pieces)
        rows_half = -(-n_el // (2 * RS_COLS * 16)) * 16
        fill = jnp.zeros((4, 2 * rows_half * RS_COLS - n_el), F32)
        gbuf = jnp.concatenate(pieces + [fill], axis=1).reshape(4, 2, rows_half, RS_COLS)
        jobs.append(dict(key=(i, gname), tag=f"l{i}_{gname}", stage=1, op=_to_sibling_start(gbuf, f"l{i}_{gname}")))

    def rs_advance(job, after):
        if job["stage"] == 1:
            gbuf, landed = job["op"].wait(after)
            job.update(stage=2, op=_to_chips_start(_pair_add(gbuf, landed, c), job["tag"]))
        elif job["stage"] == 2:
            pair, landed = job["op"].wait(after)
            job.update(stage=3, op=_join_start(_chip_sum(pair, landed, me, c), job["tag"]))
        elif job["stage"] == 3:
            job.update(stage=4, out=job["op"].wait(after)[0].reshape(-1))

    def tick(after, begin=None):
        for job in jobs:
            rs_advance(job, after)
        if begin is not None:
            rs_begin(*begin)
        return functools.reduce(lambda a, b: a + b, [j["op"].token for j in jobs if j["stage"] < 4], 0.0)

    grads = [dict() for _ in range(DEPTH)]
    tok = 0.0
    for i in reversed(range(DEPTH)):
        P, s1, s2, s3 = tape[i]
        g = grads[i]
        norm, wg, wu, wd = P["ffn2"]
        dx, dn2, g["ffn2_w_gate"], g["ffn2_w_up"], g["ffn2_w_down"] = _ffn_bwd(dx, s3, norm + tok, wg, wu, wd)
        tok = tick(dx, (i, "ffn2", g))
        dx, gm = _mix_bwd(dx, s2, {**P, "mix_norm": P["mix_norm"] + tok})
        g.update(gm)
        tok = tick(dx, (i, "mix", g))
        norm, wg, wu, wd = P["ffn1"]
        dx, dn1, g["ffn1_w_gate"], g["ffn1_w_up"], g["ffn1_w_down"] = _ffn_bwd(dx, s1, norm + tok, wg, wu, wd)
        tok = tick(dx, (i, "ffn1", g))
        g["ffn1_norm"], g["ffn2_norm"] = dn1[0], dn2[0]
    while any(j["stage"] < 4 for j in jobs):
        tick(dx)
    grad_x = dx.reshape(given["x"].shape)
    reduced = {j["key"]: j["out"] for j in jobs}

    order = [n for n in SMALL if n != "final_norm"] + ["final_norm"]
    small = [jnp.stack([grads[i][n] for i in range(DEPTH)]) for n in order[:-1]] + [dgf[0], loss_part[0, 0:1]]
    n_small = sum(s.size for s in small)
    rows_small = -(-n_small // (128 * 8)) * 8

    def flat(arrs):
        fill = rows_small * 128 - sum(a.size for a in arrs)
        return jnp.concatenate([a.reshape(-1) for a in arrs] + [jnp.zeros((fill,), F32)]).reshape(rows_small, 128)

    gsmall = _all_reduce_small(flat(small)).reshape(-1)

    grad_w = {}
    for gname, names in GROUPS:
        off = 0
        for n in names:
            size = given[n][0].size
            grad_w[n] = jnp.stack([reduced[(i, gname)][off:off + size].reshape(given[n].shape[1:])
                                   for i in range(DEPTH)])
            off += size
    off = 0
    for n in order:
        size = given[n].size
        grad_w[n] = gsmall[off:off + size].reshape(given[n].shape)
        off += size
    loss = gsmall[off]

    delta, new_m, new_v = {}, {}, {}
    for n in SHARDED:
        shp = given[n].shape
        two_d = (shp[0] * shp[1], shp[2])
        d, m2, v2 = _adamw(*[a.reshape(two_d) for a in (given[n], grad_w[n], given["m_" + n], given["v_" + n])])
        delta[n], new_m[n], new_v[n] = d.reshape(shp), m2.reshape(shp), v2.reshape(shp)
    packed = [flat([src[pre + n] for n in order])
              for src, pre in ((given, ""), (grad_w, ""), (given, "m_"), (given, "v_"))]
    outs = [o.reshape(-1) for o in _adamw(*packed)]
    off = 0
    for n in order:
        size = given[n].size
        for dst, o in zip((delta, new_m, new_v), outs):
            dst[n] = o[off:off + size].reshape(given[n].shape)
        off += size

    return (loss, grad_x, *[grad_w[n] for n in WEIGHTS], *[delta[n] for n in WEIGHTS],
            *[new_m[n] for n in WEIGHTS], *[new_v[n] for n in WEIGHTS])
```

```python
import functools
import math

import jax
import jax.numpy as jnp
from jax import lax
from jax.experimental import pallas as pl
from jax.experimental.pallas import tpu as pltpu

F32 = jnp.float32
BF = jnp.bfloat16

RMS_EPS = 1e-6
LN_EPS = 1e-5
SEQ = 2048
CHUNK = 128
N_CHUNK = SEQ // CHUNK
ATT_W = 384
HEAD = 64
SSD_W = 384
SSD_CONV_DIM = 896
SSD_STATE = 128
SGU_W = 256
DILATIONS = (1, 4, 16)
W_QKV = 3 * ATT_W
W_SSD = SSD_CONV_DIM + SSD_W + SSD_W
W_UV = 2 * SGU_W
ADAM_LR = 0.001
ADAM_B1 = 0.9
ADAM_B2 = 0.999
ADAM_EPS = 1e-08
ADAM_WD = 0.01
ADAM_STEP = 10
NEG = -1e30


def _dot(a, b):
    return jnp.dot(a, b, preferred_element_type=F32)


def _dot_nt(a, b):
    return lax.dot_general(a, b, (((1,), (1,)), ((), ())), preferred_element_type=F32)


def _dot_tn(a, b):
    return lax.dot_general(a, b, (((0,), (0,)), ((), ())), preferred_element_type=F32)


def _sigmoid(x):
    return 1.0 / (1.0 + jnp.exp(-x))


def _call(body, *, name, grid, in_specs, out_specs, out_shape, scratch=(), sem=None):
    return pl.pallas_call(
        body, name=name, grid=grid, in_specs=in_specs, out_specs=out_specs, out_shape=out_shape,
        scratch_shapes=list(scratch),
        compiler_params=pltpu.CompilerParams(dimension_semantics=sem),
    )


def _tile(n, want):
    t = min(n, want)
    while n % t:
        t //= 2
    return t


def _rms_fwd(x, g):
    T, D = x.shape
    tm = _tile(T, 512)

    def body(x_ref, g_ref, h_ref):
        xf = x_ref[...]
        r = lax.rsqrt(jnp.mean(xf * xf, axis=-1, keepdims=True) + RMS_EPS)
        h_ref[...] = (xf * r * g_ref[...]).astype(BF)

    return _call(body, name="rms_fwd", grid=(T // tm,),
                 in_specs=[pl.BlockSpec((tm, D), lambda i: (i, 0)), pl.BlockSpec((1, D), lambda i: (0, 0))],
                 out_specs=pl.BlockSpec((tm, D), lambda i: (i, 0)),
                 out_shape=jax.ShapeDtypeStruct((T, D), BF), sem=("parallel",))(x, g)


def _rms_bwd(x, g, dh, dres):
    T, D = x.shape
    tm = _tile(T, 512)

    def body(x_ref, g_ref, dh_ref, dr_ref, dx_ref, dg_ref):
        @pl.when(pl.program_id(0) == 0)
        def _():
            dg_ref[...] = jnp.zeros_like(dg_ref)

        xf = x_ref[...]
        r = lax.rsqrt(jnp.mean(xf * xf, axis=-1, keepdims=True) + RMS_EPS)
        dh_ = dh_ref[...]
        u = dh_ * g_ref[...]
        mu = jnp.mean(u * xf, axis=-1, keepdims=True)
        dx_ref[...] = dr_ref[...] + r * (u - xf * (r * r * mu))
        dg_ref[...] += jnp.sum(dh_ * xf * r, axis=0, keepdims=True)

    row = pl.BlockSpec((tm, D), lambda i: (i, 0))
    vec = pl.BlockSpec((1, D), lambda i: (0, 0))
    return _call(body, name="rms_bwd", grid=(T // tm,), in_specs=[row, vec, row, row], out_specs=[row, vec],
                 out_shape=[jax.ShapeDtypeStruct((T, D), F32), jax.ShapeDtypeStruct((1, D), F32)],
                 sem=("arbitrary",))(x, g, dh, dres)


def _final_loss(x, g, tgt):
    T, D = x.shape
    tm = _tile(T, 512)

    def body(x_ref, g_ref, t_ref, l_ref, dx_ref, dg_ref):
        @pl.when(pl.program_id(0) == 0)
        def _():
            dg_ref[...] = jnp.zeros_like(dg_ref)
            l_ref[...] = jnp.zeros_like(l_ref)

        xf = x_ref[...]
        gg = g_ref[...]
        r = lax.rsqrt(jnp.mean(xf * xf, axis=-1, keepdims=True) + RMS_EPS)
        xn = xf * r
        e = xn * gg - t_ref[...]
        part = 0.5 * jnp.sum(jnp.mean(e * e, axis=-1, keepdims=True), axis=0, keepdims=True)
        l_ref[...] += jnp.broadcast_to(part, l_ref.shape)
        dy = e * (1.0 / D)
        u = dy * gg
        mu = jnp.mean(u * xf, axis=-1, keepdims=True)
        dx_ref[...] = r * (u - xf * (r * r * mu))
        dg_ref[...] += jnp.sum(dy * xn, axis=0, keepdims=True)

    row = pl.BlockSpec((tm, D), lambda i: (i, 0))
    vec = pl.BlockSpec((1, D), lambda i: (0, 0))
    lsp = pl.BlockSpec((1, 128), lambda i: (0, 0))
    return _call(body, name="final_loss", grid=(T // tm,), in_specs=[row, vec, row], out_specs=[lsp, row, vec],
                 out_shape=[jax.ShapeDtypeStruct((1, 128), F32), jax.ShapeDtypeStruct((T, D), F32),
                            jax.ShapeDtypeStruct((1, D), F32)],
                 sem=("arbitrary",))(x, g, tgt)


def _resident(shape):
    return pl.BlockSpec(shape, lambda *_: (0,) * len(shape), pipeline_mode=pl.Buffered(1))


def _ffn_fwd_k(x, gn, wg, wu, wd):
    T, D = x.shape
    NS, _, Fs = wg.shape
    tm = _tile(T, 512)

    def body(x_ref, gn_ref, wg_ref, wu_ref, wd_ref, o_ref, h_ref, g_ref, u_ref, hs, acc):
        j = pl.program_id(1)

        @pl.when(j == 0)
        def _():
            xf = x_ref[...]
            r = lax.rsqrt(jnp.mean(xf * xf, axis=-1, keepdims=True) + RMS_EPS)
            hs[...] = (xf * r * gn_ref[...]).astype(BF)
            h_ref[...] = hs[...]
            acc[...] = jnp.zeros_like(acc)

        h = hs[...]
        g = _dot(h, wg_ref[j])
        u = _dot(h, wu_ref[j])
        g_ref[...] = g.astype(BF)
        u_ref[...] = u.astype(BF)
        acc[...] += _dot((g * _sigmoid(g) * u).astype(BF), wd_ref[j])

        @pl.when(j == NS - 1)
        def _():
            o_ref[...] = x_ref[...] + 0.5 * acc[...]

    row = pl.BlockSpec((tm, D), lambda i, j: (i, 0))
    act = pl.BlockSpec((None, tm, Fs), lambda i, j: (j, i, 0))
    sh = jax.ShapeDtypeStruct((NS, T, Fs), BF)
    return _call(body, name="ffn_fwd", grid=(T // tm, NS),
                 in_specs=[row, pl.BlockSpec((1, D), lambda i, j: (0, 0)), _resident(wg.shape), _resident(wu.shape),
                           _resident(wd.shape)],
                 out_specs=[row, row, act, act],
                 out_shape=[jax.ShapeDtypeStruct((T, D), F32), jax.ShapeDtypeStruct((T, D), BF), sh, sh],
                 scratch=[pltpu.VMEM((tm, D), BF), pltpu.VMEM((tm, D), F32)],
                 sem=("parallel", "arbitrary"))(x, gn, wg, wu, wd)


def _ffn_bwd_k1(dxo, x, gn, g, u, wg, wu, wd):
    NS, T, Fs = g.shape
    D = x.shape[1]
    tm = _tile(T, 512)

    def body(dxo_ref, x_ref, gn_ref, g_ref, u_ref, wg_ref, wu_ref, wd_ref,
             dx_ref, dgn_ref, dg_ref, du_ref, dy_ref, dys, acc):
        i, j = pl.program_id(0), pl.program_id(1)

        @pl.when((i == 0) & (j == 0))
        def _():
            dgn_ref[...] = jnp.zeros_like(dgn_ref)

        @pl.when(j == 0)
        def _():
            dys[...] = (0.5 * dxo_ref[...]).astype(BF)
            dy_ref[...] = dys[...]
            acc[...] = jnp.zeros_like(acc)

        da = _dot_nt(dys[...], wd_ref[j])
        gf = g_ref[...].astype(F32)
        uf = u_ref[...].astype(F32)
        sg = _sigmoid(gf)
        dg = (da * uf * (sg * (1.0 + gf * (1.0 - sg)))).astype(BF)
        du = (da * gf * sg).astype(BF)
        dg_ref[...] = dg
        du_ref[...] = du
        acc[...] += _dot_nt(dg, wg_ref[j]) + _dot_nt(du, wu_ref[j])

        @pl.when(j == NS - 1)
        def _():
            xf = x_ref[...]
            r = lax.rsqrt(jnp.mean(xf * xf, axis=-1, keepdims=True) + RMS_EPS)
            dh = acc[...]
            uu = dh * gn_ref[...]
            mu = jnp.mean(uu * xf, axis=-1, keepdims=True)
            dx_ref[...] = dxo_ref[...] + r * (uu - xf * (r * r * mu))
            dgn_ref[...] += jnp.sum(dh * xf * r, axis=0, keepdims=True)

    row = pl.BlockSpec((tm, D), lambda i, j: (i, 0))
    vec = pl.BlockSpec((1, D), lambda i, j: (0, 0))
    act = pl.BlockSpec((None, tm, Fs), lambda i, j: (j, i, 0))
    sh = jax.ShapeDtypeStruct((NS, T, Fs), BF)
    return _call(body, name="ffn_bwd_x", grid=(T // tm, NS),
                 in_specs=[row, row, vec, act, act, _resident(wg.shape), _resident(wu.shape), _resident(wd.shape)],
                 out_specs=[row, vec, act, act, row],
                 out_shape=[jax.ShapeDtypeStruct((T, D), F32), jax.ShapeDtypeStruct((1, D), F32), sh, sh,
                            jax.ShapeDtypeStruct((T, D), BF)],
                 scratch=[pltpu.VMEM((tm, D), BF), pltpu.VMEM((tm, D), F32)],
                 sem=("arbitrary", "arbitrary"))(dxo, x, gn, g, u, wg, wu, wd)


def _ffn_bwd_k2(hb, dyb, g, u, dg, du):
    NS, T, Fs = g.shape
    D = hb.shape[1]
    tk = _tile(T, 512)

    def body(h_ref, dy_ref, g_ref, u_ref, dg_ref, du_ref, og_ref, ou_ref, od_ref):
        @pl.when(pl.program_id(1) == 0)
        def _():
            og_ref[...] = jnp.zeros_like(og_ref)
            ou_ref[...] = jnp.zeros_like(ou_ref)
            od_ref[...] = jnp.zeros_like(od_ref)

        h = h_ref[...]
        gf = g_ref[...].astype(F32)
        a = (gf * _sigmoid(gf) * u_ref[...].astype(F32)).astype(BF)
        og_ref[...] += _dot_tn(h, dg_ref[...])
        ou_ref[...] += _dot_tn(h, du_ref[...])
        od_ref[...] += _dot_tn(a, dy_ref[...])

    row = pl.BlockSpec((tk, D), lambda j, k: (k, 0))
    act = pl.BlockSpec((None, tk, Fs), lambda j, k: (j, k, 0))
    return _call(body, name="ffn_bwd_w", grid=(NS, T // tk), in_specs=[row, row, act, act, act, act],
                 out_specs=[pl.BlockSpec((None, D, Fs), lambda j, k: (j, 0, 0))] * 2
                 + [pl.BlockSpec((None, Fs, D), lambda j, k: (j, 0, 0))],
                 out_shape=[jax.ShapeDtypeStruct((NS, D, Fs), F32)] * 2 + [jax.ShapeDtypeStruct((NS, Fs, D), F32)],
                 sem=("parallel", "arbitrary"))(hb, dyb, g, u, dg, du)


def _mm_nn(a, b, res=None, out_dtype=F32):
    T, K = a.shape
    N = b.shape[1]
    tm = _tile(T, 512)
    tn = N if N <= 2048 else _tile(N, 1024)

    def body(*refs):
        if res is None:
            a_ref, b_ref, o_ref = refs
            o_ref[...] = _dot(a_ref[...], b_ref[...]).astype(out_dtype)
        else:
            a_ref, b_ref, r_ref, o_ref = refs
            o_ref[...] = (r_ref[...] + _dot(a_ref[...], b_ref[...])).astype(out_dtype)

    o = pl.BlockSpec((tm, tn), lambda i, j: (i, j))
    ins = [pl.BlockSpec((tm, K), lambda i, j: (i, 0)), pl.BlockSpec((K, tn), lambda i, j: (0, j))]
    args = [a, b]
    if res is not None:
        ins.append(o)
        args.append(res)
    return _call(body, name="mm_nn", grid=(T // tm, N // tn), in_specs=ins, out_specs=o,
                 out_shape=jax.ShapeDtypeStruct((T, N), out_dtype), sem=("parallel", "parallel"))(*args)


def _mm_nt(a, b, res=None):
    T, K = a.shape
    N = b.shape[0]
    tm = _tile(T, 512)

    def body(*refs):
        if res is None:
            a_ref, b_ref, o_ref = refs
            o_ref[...] = _dot_nt(a_ref[...].astype(BF), b_ref[...])
        else:
            a_ref, b_ref, r_ref, o_ref = refs
            o_ref[...] = r_ref[...] + _dot_nt(a_ref[...].astype(BF), b_ref[...])

    o = pl.BlockSpec((tm, N), lambda i: (i, 0))
    ins = [pl.BlockSpec((tm, K), lambda i: (i, 0)), pl.BlockSpec((N, K), lambda i: (0, 0))]
    args = [a, b]
    if res is not None:
        ins.append(o)
        args.append(res)
    return _call(body, name="mm_nt", grid=(T // tm,), in_specs=ins, out_specs=o,
                 out_shape=jax.ShapeDtypeStruct((T, N), F32), sem=("parallel",))(*args)


def _mm_tn(a, b):
    T, M = a.shape
    N = b.shape[1]
    tk = _tile(T, 512)
    tmm = _tile(M, 512)

    def body(a_ref, b_ref, o_ref):
        @pl.when(pl.program_id(1) == 0)
        def _():
            o_ref[...] = jnp.zeros_like(o_ref)

        o_ref[...] += _dot_tn(a_ref[...].astype(BF), b_ref[...].astype(BF))

    return _call(body, name="mm_tn", grid=(M // tmm, T // tk),
                 in_specs=[pl.BlockSpec((tk, tmm), lambda i, k: (k, i)), pl.BlockSpec((tk, N), lambda i, k: (k, 0))],
                 out_specs=pl.BlockSpec((tmm, N), lambda i, k: (i, 0)),
                 out_shape=jax.ShapeDtypeStruct((M, N), F32), sem=("parallel", "arbitrary"))(a, b)


def _lane_mask(e, width=128):
    return (lax.broadcasted_iota(jnp.int32, (1, width), 1) // HEAD) == e


def _band_mask(n):
    qi = lax.broadcasted_iota(jnp.int32, (CHUNK, 2 * CHUNK), 0)
    kj = lax.broadcasted_iota(jnp.int32, (CHUNK, 2 * CHUNK), 1)
    dist = qi + CHUNK - kj
    return (dist >= 0) & (dist <= CHUNK) & ((kj >= CHUNK) | (n > 0))


def _prev_cur(ref, n):
    cur = pl.multiple_of(n * CHUNK, CHUNK)
    prv = pl.multiple_of(jnp.maximum(n - 1, 0) * CHUNK, CHUNK)
    return jnp.concatenate([ref[pl.ds(prv, CHUNK), :], ref[pl.ds(cur, CHUNK), :]], axis=0), prv, cur


def _attn_fwd(qkv, dil):
    T = qkv.shape[0]
    B, L = T // SEQ, SEQ // dil
    nb = L // CHUNK
    scale = HEAD ** -0.5

    def body(q_ref, k_ref, v_ref, o_ref, l_ref):
        n = pl.program_id(2)
        q = q_ref[...]
        kk, _, _ = _prev_cur(k_ref, n)
        vv, _, _ = _prev_cur(v_ref, n)
        mask = _band_mask(n)
        for t in range(ATT_W // 128):
            sl = slice(128 * t, 128 * (t + 1))
            qt, kt, vt = q[:, sl], kk[:, sl], vv[:, sl]
            o_pair = jnp.zeros((CHUNK, 128), F32)
            l_pair = jnp.zeros((CHUNK, 128), F32)
            for e in range(2):
                lm = _lane_mask(e)
                s = _dot_nt(jnp.where(lm, qt, jnp.zeros_like(qt)), kt) * scale
                s = jnp.where(mask, s, NEG)
                m = jnp.max(s, axis=-1, keepdims=True)
                p = jnp.exp(s - m)
                den = jnp.sum(p, axis=-1, keepdims=True)
                o = _dot(p.astype(BF), vt) / den
                o_pair = jnp.where(lm, o, o_pair)
                l_pair = jnp.where(lm, m + jnp.log(den), l_pair)
            o_ref[:, sl] = o_pair
            l_ref[:, sl] = l_pair

    qv = qkv.reshape(B, L, dil * W_QKV)
    o = pl.BlockSpec((None, CHUNK, ATT_W), lambda b, r, n: (b, n, r))
    sh = jax.ShapeDtypeStruct((B, L, dil * ATT_W), F32)
    out, lse = _call(
        body, name=f"attn_fwd_d{dil}", grid=(B, dil, nb),
        in_specs=[pl.BlockSpec((None, CHUNK, ATT_W), lambda b, r, n: (b, n, 3 * r)),
                  pl.BlockSpec((None, L, ATT_W), lambda b, r, n: (b, 0, 3 * r + 1)),
                  pl.BlockSpec((None, L, ATT_W), lambda b, r, n: (b, 0, 3 * r + 2))],
        out_specs=[o, o], out_shape=[sh, sh], sem=("parallel", "parallel", "parallel"))(qv, qv, qv)
    return out.reshape(T, ATT_W), lse.reshape(T, ATT_W)


def _attn_combine(o1, o2, o3, l1, l2, l3):
    T = o1.shape[0]
    tm = _tile(T, 512)

    def body(o1_ref, o2_ref, o3_ref, l1_ref, l2_ref, l3_ref, y_ref, l_ref):
        a, b, c = l1_ref[...], l2_ref[...], l3_ref[...]
        m = jnp.maximum(jnp.maximum(a, b), c)
        ea, eb, ec = jnp.exp(a - m), jnp.exp(b - m), jnp.exp(c - m)
        z = ea + eb + ec
        y_ref[...] = (ea * o1_ref[...] + eb * o2_ref[...] + ec * o3_ref[...]) / z
        l_ref[...] = m + jnp.log(z)

    row = pl.BlockSpec((tm, ATT_W), lambda i: (i, 0))
    sh = jax.ShapeDtypeStruct((T, ATT_W), F32)
    return _call(body, name="attn_combine", grid=(T // tm,), in_specs=[row] * 6, out_specs=[row, row],
                 out_shape=[sh, sh], sem=("parallel",))(o1, o2, o3, l1, l2, l3)


def _attn_bwd(qkv, do, out, lse, dil):
    T = qkv.shape[0]
    B, L = T // SEQ, SEQ // dil
    nb = L // CHUNK
    scale = HEAD ** -0.5

    def body(q_ref, k_ref, v_ref, do_ref, out_ref, lse_ref, dq_ref, dk_ref, dv_ref):
        n = pl.program_id(2)

        @pl.when(n == 0)
        def _():
            dk_ref[...] = jnp.zeros_like(dk_ref)
            dv_ref[...] = jnp.zeros_like(dv_ref)

        q = q_ref[...]
        kk, prv, cur = _prev_cur(k_ref, n)
        vv, _, _ = _prev_cur(v_ref, n)
        mask = _band_mask(n)
        do_ = do_ref[...]
        dlt = do_ * out_ref[...]
        ls = lse_ref[...]
        for t in range(ATT_W // 128):
            sl = slice(128 * t, 128 * (t + 1))
            qt, kt, vt = q[:, sl], kk[:, sl], vv[:, sl]
            dq_pair = jnp.zeros((CHUNK, 128), F32)
            dk_acc = jnp.zeros((2 * CHUNK, 128), F32)
            dv_acc = jnp.zeros((2 * CHUNK, 128), F32)
            for e in range(2):
                lm = _lane_mask(e)
                qm = jnp.where(lm, qt, jnp.zeros_like(qt))
                s = _dot_nt(qm, kt) * scale
                lse_col = ls[:, 128 * t + HEAD * e:128 * t + HEAD * e + 1]
                p = jnp.exp(jnp.where(mask, s - lse_col, NEG))
                dom = jnp.where(lm, do_[:, sl], 0.0).astype(BF)
                dv_acc += _dot_tn(p.astype(BF), dom)
                dp = _dot_nt(dom, vt)
                delta = jnp.sum(jnp.where(lm, dlt[:, sl], 0.0), axis=-1, keepdims=True)
                ds = (p * (dp - delta) * scale).astype(BF)
                dq_pair += jnp.where(lm, _dot(ds, kt), 0.0)
                dk_acc += _dot_tn(ds, qm)
            dq_ref[:, sl] = dq_pair
            dk_ref[pl.ds(cur, CHUNK), sl] += dk_acc[CHUNK:]
            dk_ref[pl.ds(prv, CHUNK), sl] += dk_acc[:CHUNK]
            dv_ref[pl.ds(cur, CHUNK), sl] += dv_acc[CHUNK:]
            dv_ref[pl.ds(prv, CHUNK), sl] += dv_acc[:CHUNK]

    qv = qkv.reshape(B, L, dil * W_QKV)
    view = lambda a: a.reshape(B, L, dil * ATT_W)
    blk = pl.BlockSpec((None, CHUNK, ATT_W), lambda b, r, n: (b, n, r))
    whole = pl.BlockSpec((None, L, ATT_W), lambda b, r, n: (b, 0, r))
    sh = jax.ShapeDtypeStruct((B, L, dil * ATT_W), F32)
    dq, dk, dv = _call(
        body, name=f"attn_bwd_d{dil}", grid=(B, dil, nb),
        in_specs=[pl.BlockSpec((None, CHUNK, ATT_W), lambda b, r, n: (b, n, 3 * r)),
                  pl.BlockSpec((None, L, ATT_W), lambda b, r, n: (b, 0, 3 * r + 1)),
                  pl.BlockSpec((None, L, ATT_W), lambda b, r, n: (b, 0, 3 * r + 2)),
                  blk, blk, blk],
        out_specs=[blk, whole, whole], out_shape=[sh, sh, sh],
        sem=("parallel", "parallel", "arbitrary"))(qv, qv, qv, view(do), view(out), view(lse))
    return dq.reshape(T, ATT_W), dk.reshape(T, ATT_W), dv.reshape(T, ATT_W)


def _sum_branches(parts):
    T = parts[0][0].shape[0]
    tm = _tile(T, 512)

    def body(*refs):
        o_ref = refs[-1]
        for s in range(3):
            acc = refs[s][...] + refs[3 + s][...] + refs[6 + s][...]
            o_ref[:, ATT_W * s:ATT_W * (s + 1)] = acc.astype(BF)

    row = pl.BlockSpec((tm, ATT_W), lambda i: (i, 0))
    flat = [a for tr in parts for a in tr]
    return _call(body, name="attn_sum_branches", grid=(T // tm,), in_specs=[row] * 9,
                 out_specs=pl.BlockSpec((tm, W_QKV), lambda i: (i, 0)),
                 out_shape=jax.ShapeDtypeStruct((T, W_QKV), BF), sem=("parallel",))(*flat)


def _silu(x):
    return x * _sigmoid(x)


def _dsilu(x):
    s = _sigmoid(x)
    return s * (1.0 + x * (1.0 - s))


def _log1p(u):
    return jnp.where(u < 0.01, u * (1.0 - u * (0.5 - u * (1.0 / 3.0))), jnp.log(1.0 + u))


def _softplus(x):
    return jnp.maximum(x, 0.0) + _log1p(jnp.exp(-jnp.abs(x)))


def _cumsum_rows(x, reverse=False):
    n = x.shape[0]
    rows = lax.broadcasted_iota(jnp.int32, x.shape, 0)
    k = 1
    while k < n:
        if reverse:
            x = x + jnp.where(rows < n - k, pltpu.roll(x, n - k, 0), 0.0)
        else:
            x = x + jnp.where(rows >= k, pltpu.roll(x, k, 0), 0.0)
        k *= 2
    return x


def _tri():
    r = lax.broadcasted_iota(jnp.int32, (CHUNK, CHUNK), 0)
    c = lax.broadcasted_iota(jnp.int32, (CHUNK, CHUNK), 1)
    return r >= c


def _row_mask(e):
    return (lax.broadcasted_iota(jnp.int32, (128, 1), 0) // HEAD) == e


def _first_lane(e):
    return lax.broadcasted_iota(jnp.int32, (1, 128), 1) == HEAD * e


def _ssd_pre(x_ref, halo_ref, first, cw_ref, cb_ref, dtb_ref, al_ref, ext):
    row = x_ref[...]
    z = row[:, SSD_CONV_DIM:SSD_CONV_DIM + SSD_W]
    u = row[:, SSD_CONV_DIM + SSD_W:] + dtb_ref[...]
    ext[0:8, :] = jnp.where(first, 0.0, halo_ref[:, 0:SSD_CONV_DIM])
    ext[8:8 + CHUNK, :] = row[:, 0:SSD_CONV_DIM]
    xc = cb_ref[...]
    for j in range(4):
        xc = xc + cw_ref[j:j + 1, :] * ext[pl.ds(5 + j, CHUNK), :]
    xa = _silu(xc)
    dt = _softplus(u)
    a = dt * (-jnp.exp(al_ref[...]))
    A = _cumsum_rows(a)
    return dict(z=z, u=u, xc=xc, xs=xa[:, 0:SSD_W], Bm=xa[:, SSD_W:SSD_W + 256], Cm=xa[:, SSD_W + 256:],
                dt=dt, a=a, A=A, AT=A.T, eA=jnp.exp(A), wdec=jnp.exp(A[CHUNK - 1:CHUNK, :] - A),
                dtot=jnp.exp(A[CHUNK - 1:CHUNK, :]))


def _ssd_y(p, hp_ref, dskip):
    tri = _tri()
    X = p["xs"] * p["dt"]
    Bb = [p["Bm"][:, 128 * g:128 * (g + 1)].astype(BF) for g in range(2)]
    Cb = [p["Cm"][:, 128 * g:128 * (g + 1)].astype(BF) for g in range(2)]
    CB = [_dot_nt(Cb[g], Bb[g]) for g in range(2)]
    tiles = []
    for t in range(3):
        sl = slice(128 * t, 128 * (t + 1))
        hpb = hp_ref[sl, :].astype(BF)
        acc = jnp.zeros((CHUNK, 128), F32)
        for e in range(2):
            h = 2 * t + e
            g, col = h // 3, HEAD * h
            lm = _lane_mask(e)
            L = jnp.exp(jnp.where(tri, p["A"][:, col:col + 1] - p["AT"][col:col + 1, :], NEG))
            yd = _dot((CB[g] * L).astype(BF), jnp.where(lm, X[:, sl], 0.0).astype(BF))
            yo = _dot_nt(Cb[g], hpb) * p["eA"][:, sl]
            acc = acc + jnp.where(lm, yd + yo, 0.0)
        tiles.append(acc)
    return jnp.concatenate(tiles, axis=1) + dskip * p["xs"], X, Bb, Cb, CB


def _group_stats(v):
    g0 = lax.broadcasted_iota(jnp.int32, (1, SSD_W), 1) < SSD_W // 2
    m0 = jnp.sum(jnp.where(g0, v, 0.0), axis=-1, keepdims=True) * (2.0 / SSD_W)
    m1 = jnp.sum(jnp.where(g0, 0.0, v), axis=-1, keepdims=True) * (2.0 / SSD_W)
    return jnp.where(g0, m0, m1)


def _ssd_specs(T, rev):
    B = T // SEQ

    def chunk(b, c):
        return b * N_CHUNK + (N_CHUNK - 1 - c if rev else c)

    row = pl.BlockSpec((CHUNK, W_SSD), lambda b, c: (chunk(b, c), 0))
    halo = pl.BlockSpec((8, W_SSD), lambda b, c: (jnp.maximum(chunk(b, c) * (CHUNK // 8) - 1, 0), 0))
    hp = pl.BlockSpec((None, SSD_W, SSD_STATE), lambda b, c: (chunk(b, c), 0, 0))
    y = pl.BlockSpec((CHUNK, SSD_W), lambda b, c: (chunk(b, c), 0))
    const = lambda r, w: pl.BlockSpec((r, w), lambda b, c: (0, 0))
    params = [const(4, SSD_CONV_DIM), const(1, SSD_CONV_DIM)] + [const(1, SSD_W)] * 4
    return B, row, halo, hp, y, const, params


def _ssd_fwd(sin, conv_w, conv_b, dtb, alog, dskip, norm_g):
    T = sin.shape[0]
    B, row, halo, hp, y, const, params = _ssd_specs(T, False)

    def body(x_ref, halo_ref, cw_ref, cb_ref, dtb_ref, al_ref, dk_ref, ng_ref, y_ref, hp_ref, ext, hst):
        c = pl.program_id(1)

        @pl.when(c == 0)
        def _():
            hst[...] = jnp.zeros_like(hst)

        p = _ssd_pre(x_ref, halo_ref, c == 0, cw_ref, cb_ref, dtb_ref, al_ref, ext)
        yv, X, Bb, Cb, CB = _ssd_y(p, hst, dk_ref[...])
        hp_ref[...] = hst[...]
        for t in range(3):
            sl = slice(128 * t, 128 * (t + 1))
            old = hst[sl, :]
            new = old
            for e in range(2):
                h = 2 * t + e
                g, col = h // 3, HEAD * h
                st = _dot_tn(jnp.where(_lane_mask(e), X[:, sl] * p["wdec"][:, sl], 0.0).astype(BF), Bb[g])
                new = jnp.where(_row_mask(e), old * p["dtot"][:, col:col + 1] + st, new)
            hst[sl, :] = new
        y2 = yv * _silu(p["z"])
        r = lax.rsqrt(_group_stats(y2 * y2) + RMS_EPS)
        y_ref[...] = y2 * r * ng_ref[...]

    return _call(body, name="ssd_fwd", grid=(B, N_CHUNK), in_specs=[row, halo] + params, out_specs=[y, hp],
                 out_shape=[jax.ShapeDtypeStruct((T, SSD_W), F32),
                            jax.ShapeDtypeStruct((T // CHUNK, SSD_W, SSD_STATE), F32)],
                 scratch=[pltpu.VMEM((8 + CHUNK, SSD_CONV_DIM), F32), pltpu.VMEM((SSD_W, SSD_STATE), F32)],
                 sem=("parallel", "arbitrary"))(sin, sin, conv_w, conv_b, dtb, alog, dskip, norm_g)


def _ssd_bwd(sin, hprev, dy3, conv_w, conv_b, dtb, alog, dskip, norm_g):
    T = sin.shape[0]
    B, row, halo, hp, y, const, params = _ssd_specs(T, True)

    def body(x_ref, halo_ref, hp_ref, dy_ref, cw_ref, cb_ref, dtb_ref, al_ref, dk_ref, ng_ref,
             dx_ref, dcw_ref, dcb_ref, dvec_ref, ext, ext2, dh):
        c = pl.program_id(1)

        @pl.when((pl.program_id(0) == 0) & (c == 0))
        def _():
            dcw_ref[...] = jnp.zeros_like(dcw_ref)
            dcb_ref[...] = jnp.zeros_like(dcb_ref)
            dvec_ref[...] = jnp.zeros_like(dvec_ref)

        @pl.when(c == 0)
        def _():
            dh[...] = jnp.zeros_like(dh)
            ext2[CHUNK:CHUNK + 8, :] = jnp.zeros((8, SSD_CONV_DIM), F32)

        p = _ssd_pre(x_ref, halo_ref, c == N_CHUNK - 1, cw_ref, cb_ref, dtb_ref, al_ref, ext)
        dskip_ = dk_ref[...]
        yv, X, Bb, Cb, CB = _ssd_y(p, hp_ref, dskip_)
        xs, z, A, AT = p["xs"], p["z"], p["A"], p["AT"]

        sz = _silu(z)
        y2 = yv * sz
        r = lax.rsqrt(_group_stats(y2 * y2) + RMS_EPS)
        dy3_ = dy_ref[...]
        uu = dy3_ * ng_ref[...]
        dy2 = r * (uu - y2 * (r * r * _group_stats(uu * y2)))
        dy = dy2 * sz
        dz = dy2 * yv * _dsilu(z)

        tri = _tri()
        rows = lax.broadcasted_iota(jnp.int32, (CHUNK, 1), 0)
        dG = [jnp.zeros((CHUNK, CHUNK), F32) for _ in range(2)]
        dB = [jnp.zeros((CHUNK, SSD_STATE), F32) for _ in range(2)]
        dC = [jnp.zeros((CHUNK, SSD_STATE), F32) for _ in range(2)]
        dX_t, dA_t, ddtx_t = [], [], []
        for t in range(3):
            sl = slice(128 * t, 128 * (t + 1))
            hp_t = hp_ref[sl, :]
            hpb = hp_t.astype(BF)
            dhc = dh[sl, :]
            dh_new = jnp.zeros((128, SSD_STATE), F32)
            dX = jnp.zeros((CHUNK, 128), F32)
            dA = jnp.zeros((CHUNK, 128), F32)
            ddtx = jnp.zeros((CHUNK, 128), F32)
            for e in range(2):
                h = 2 * t + e
                g, col = h // 3, HEAD * h
                lm, rm, fl = _lane_mask(e), _row_mask(e), _first_lane(e)
                L = jnp.exp(jnp.where(tri, A[:, col:col + 1] - AT[col:col + 1, :], NEG))
                Mf = CB[g] * L
                Xm = jnp.where(lm, X[:, sl], 0.0)
                Xmb = Xm.astype(BF)
                dyh = jnp.where(lm, dy[:, sl], 0.0)
                dyb = dyh.astype(BF)
                dXh = _dot_tn(Mf.astype(BF), dyb)
                dM = jnp.where(tri, _dot_nt(dyb, Xmb), 0.0)
                Wm = dM * Mf
                dAc = jnp.sum(Wm, axis=-1, keepdims=True) - jnp.sum(Wm.T, axis=-1, keepdims=True)
                dG[g] = dG[g] + dM * L
                eAt = p["eA"][:, sl]
                yo = _dot_nt(Cb[g], hpb)
                dyo = (dyh * eAt).astype(BF)
                dC[g] = dC[g] + _dot(dyo, hpb)
                dh_new = dh_new + _dot_tn(dyo, Cb[g])
                dAc = dAc + jnp.sum(dyh * yo * eAt, axis=-1, keepdims=True)
                dHn = jnp.where(rm, dhc, 0.0)
                dHnb = dHn.astype(BF)
                dec = p["dtot"][:, col:col + 1]
                dh_new = dh_new + dec * dHn
                Z = _dot_nt(Bb[g], dHnb)
                wt = p["wdec"][:, sl]
                xi = jnp.sum(Xm * Z, axis=-1, keepdims=True) * p["wdec"][:, col:col + 1]
                dXh = dXh + wt * Z
                dB[g] = dB[g] + _dot(jnp.where(lm, X[:, sl] * wt, 0.0).astype(BF), dHnb)
                dAtot = jnp.sum(xi, axis=0, keepdims=True) + dec * jnp.sum(
                    jnp.sum(dHn * hp_t, axis=-1, keepdims=True), axis=0, keepdims=True)
                dAc = dAc - xi + jnp.where(rows == CHUNK - 1, dAtot, 0.0)
                dA = dA + jnp.where(fl, dAc, 0.0)
                dX = dX + dXh
                ddtx = ddtx + jnp.where(fl, jnp.sum(dXh * xs[:, sl], axis=-1, keepdims=True), 0.0)
            dh[sl, :] = dh_new
            dX_t.append(dX)
            dA_t.append(dA)
            ddtx_t.append(ddtx)
        for g in range(2):
            dGb = dG[g].astype(BF)
            dC[g] = dC[g] + _dot(dGb, Bb[g])
            dB[g] = dB[g] + _dot_tn(dGb, Cb[g])
        dXf = jnp.concatenate(dX_t, axis=1)
        da = _cumsum_rows(jnp.concatenate(dA_t, axis=1), reverse=True)
        ddt = da * (-jnp.exp(al_ref[...])) + jnp.concatenate(ddtx_t, axis=1)
        du = ddt * _sigmoid(p["u"])
        dxs = dXf * p["dt"] + dskip_ * dy
        dxc = jnp.concatenate([dxs, dB[0], dB[1], dC[0], dC[1]], axis=1) * _dsilu(p["xc"])
        ext2[0:CHUNK, :] = dxc
        dxbc = jnp.zeros((CHUNK, SSD_CONV_DIM), F32)
        for j in range(4):
            dxbc = dxbc + cw_ref[j:j + 1, :] * ext2[pl.ds(3 - j, CHUNK), :]
            dcw_ref[j:j + 1, :] += jnp.sum(dxc * ext[pl.ds(5 + j, CHUNK), :], axis=0, keepdims=True)
        ext2[CHUNK:CHUNK + 8, :] = dxc[0:8, :]
        dcb_ref[...] += jnp.sum(dxc, axis=0, keepdims=True)
        dvec_ref[0:1, :] += jnp.sum(du, axis=0, keepdims=True)
        dvec_ref[1:2, :] += jnp.sum(da * p["a"], axis=0, keepdims=True)
        dvec_ref[2:3, :] += jnp.sum(dy * xs, axis=0, keepdims=True)
        dvec_ref[3:4, :] += jnp.sum(dy3_ * y2 * r, axis=0, keepdims=True)
        dx_ref[...] = jnp.concatenate([dxbc, dz, du], axis=1).astype(BF)

    return _call(body, name="ssd_bwd", grid=(B, N_CHUNK), in_specs=[row, halo, hp, y] + params,
                 out_specs=[row, const(4, SSD_CONV_DIM), const(1, SSD_CONV_DIM), const(8, SSD_W)],
                 out_shape=[jax.ShapeDtypeStruct((T, W_SSD), BF), jax.ShapeDtypeStruct((4, SSD_CONV_DIM), F32),
                            jax.ShapeDtypeStruct((1, SSD_CONV_DIM), F32), jax.ShapeDtypeStruct((8, SSD_W), F32)],
                 scratch=[pltpu.VMEM((8 + CHUNK, SSD_CONV_DIM), F32), pltpu.VMEM((8 + CHUNK, SSD_CONV_DIM), F32),
                          pltpu.VMEM((SSD_W, SSD_STATE), F32)],
                 sem=("arbitrary", "arbitrary"))(sin, sin, hprev, dy3, conv_w, conv_b, dtb, alog, dskip, norm_g)


def _sgu_core(uv_ref, g_ref, b_ref, w_ref, bias_ref):
    x = uv_ref[...]
    cdf = 0.5 * (1.0 + lax.erf(x * (2.0 ** -0.5)))
    ge = x * cdf
    dge = cdf + x * jnp.exp(-0.5 * x * x) * ((2.0 * math.pi) ** -0.5)
    u, v = ge[:, 0:SGU_W], ge[:, SGU_W:]
    vc = v - jnp.mean(v, axis=-1, keepdims=True)
    rstd = lax.rsqrt(jnp.mean(vc * vc, axis=-1, keepdims=True) + LN_EPS)
    vhat = vc * rstd
    vn = vhat * g_ref[...] + b_ref[...]
    tri = _tri()
    wc = [jnp.where(tri, w_ref[gi], 0.0).astype(BF) for gi in range(4)]
    vm = [jnp.where(_lane_mask(gi % 2), vn[:, 128 * (gi // 2):128 * (gi // 2 + 1)], 0.0).astype(BF) for gi in range(4)]
    mixed = jnp.concatenate([_dot(wc[2 * t], vm[2 * t]) + _dot(wc[2 * t + 1], vm[2 * t + 1]) for t in range(2)],
                            axis=1) + bias_ref[...]
    return dict(dge=dge, u=u, rstd=rstd, vhat=vhat, wc=wc, vm=vm, mixed=mixed)


def _sgu_specs():
    vec = pl.BlockSpec((1, SGU_W), lambda i: (0, 0))
    return [pl.BlockSpec((CHUNK, W_UV), lambda i: (i, 0)), vec, vec,
            pl.BlockSpec((4, CHUNK, CHUNK), lambda i: (0, 0, 0)), pl.BlockSpec((CHUNK, SGU_W), lambda i: (0, 0))]


def _sgu_fwd(uv, ln_g, ln_b, w, bias):
    T = uv.shape[0]

    def body(uv_ref, g_ref, b_ref, w_ref, bias_ref, y_ref):
        s = _sgu_core(uv_ref, g_ref, b_ref, w_ref, bias_ref)
        y_ref[...] = s["u"] * s["mixed"]

    return _call(body, name="sgu_fwd", grid=(T // CHUNK,), in_specs=_sgu_specs(),
                 out_specs=pl.BlockSpec((CHUNK, SGU_W), lambda i: (i, 0)),
                 out_shape=jax.ShapeDtypeStruct((T, SGU_W), F32), sem=("parallel",))(uv, ln_g, ln_b, w, bias)


def _sgu_bwd(uv, dy, ln_g, ln_b, w, bias):
    T = uv.shape[0]

    def body(uv_ref, dy_ref, g_ref, b_ref, w_ref, bias_ref, dx_ref, dw_ref, dbias_ref, dln_ref):
        @pl.when(pl.program_id(0) == 0)
        def _():
            dw_ref[...] = jnp.zeros_like(dw_ref)
            dbias_ref[...] = jnp.zeros_like(dbias_ref)
            dln_ref[...] = jnp.zeros_like(dln_ref)

        s = _sgu_core(uv_ref, g_ref, b_ref, w_ref, bias_ref)
        dy_ = dy_ref[...]
        du = dy_ * s["mixed"]
        dmix = dy_ * s["u"]
        dbias_ref[...] += dmix
        tri = _tri()
        dvn_t = []
        for t in range(2):
            acc = jnp.zeros((CHUNK, 128), F32)
            for e in range(2):
                gi = 2 * t + e
                dmg = jnp.where(_lane_mask(e), dmix[:, 128 * t:128 * (t + 1)], 0.0).astype(BF)
                acc = acc + _dot_tn(s["wc"][gi], dmg)
                dw_ref[gi] += jnp.where(tri, _dot_nt(dmg, s["vm"][gi]), 0.0)
            dvn_t.append(acc)
        dvn = jnp.concatenate(dvn_t, axis=1)
        dln_ref[0:1, :] += jnp.sum(dvn * s["vhat"], axis=0, keepdims=True)
        dln_ref[1:2, :] += jnp.sum(dvn, axis=0, keepdims=True)
        dvh = dvn * g_ref[...]
        dv = s["rstd"] * (dvh - jnp.mean(dvh, axis=-1, keepdims=True)
                          - s["vhat"] * jnp.mean(dvh * s["vhat"], axis=-1, keepdims=True))
        dx_ref[...] = (jnp.concatenate([du, dv], axis=1) * s["dge"]).astype(BF)

    ins = _sgu_specs()
    return _call(body, name="sgu_bwd", grid=(T // CHUNK,),
                 in_specs=[ins[0], pl.BlockSpec((CHUNK, SGU_W), lambda i: (i, 0))] + ins[1:],
                 out_specs=[pl.BlockSpec((CHUNK, W_UV), lambda i: (i, 0)),
                            pl.BlockSpec((4, CHUNK, CHUNK), lambda i: (0, 0, 0)),
                            pl.BlockSpec((CHUNK, SGU_W), lambda i: (0, 0)), pl.BlockSpec((8, SGU_W), lambda i: (0, 0))],
                 out_shape=[jax.ShapeDtypeStruct((T, W_UV), BF), jax.ShapeDtypeStruct((4, CHUNK, CHUNK), F32),
                            jax.ShapeDtypeStruct((CHUNK, SGU_W), F32), jax.ShapeDtypeStruct((8, SGU_W), F32)],
                 sem=("arbitrary",))(uv, dy, ln_g, ln_b, w, bias)


def _adamw(w, g, m, v):
    R, C = w.shape
    tr = _tile(R, 256) if R % 8 == 0 else R

    def body(w_ref, g_ref, m_ref, v_ref, d_ref, nm_ref, nv_ref):
        g_ = g_ref[...]
        m2 = ADAM_B1 * m_ref[...] + (1.0 - ADAM_B1) * g_
        v2 = ADAM_B2 * v_ref[...] + (1.0 - ADAM_B2) * (g_ * g_)
        m_hat = m2 / (1.0 - ADAM_B1 ** ADAM_STEP)
        v_hat = v2 / (1.0 - ADAM_B2 ** ADAM_STEP)
        d_ref[...] = -ADAM_LR * (m_hat / (jnp.sqrt(v_hat) + ADAM_EPS) + ADAM_WD * w_ref[...])
        nm_ref[...] = m2
        nv_ref[...] = v2

    blk = pl.BlockSpec((tr, C), lambda i: (i, 0))
    sh = jax.ShapeDtypeStruct((R, C), F32)
    return _call(body, name="adamw", grid=(R // tr,), in_specs=[blk] * 4, out_specs=[blk] * 3,
                 out_shape=[sh] * 3, sem=("parallel",))(w, g, m, v)


def _pair_add(gbuf, rsib, c):
    NS, _, R, C = gbuf.shape
    tr = 512

    def body(c_ref, a_ref, b_ref, o_ref):
        o_ref[...] = (a_ref[...] + b_ref[...]).astype(BF)

    blk = pl.BlockSpec((None, tr, C), lambda j, i, c_ref: (j, i, 0))
    return pl.pallas_call(
        body, name="rs_pair_add",
        grid_spec=pltpu.PrefetchScalarGridSpec(
            num_scalar_prefetch=1, grid=(NS, pl.cdiv(R, tr)),
            in_specs=[pl.BlockSpec((None, None, tr, C), lambda j, i, c_ref: (j, c_ref[0], i, 0)), blk],
            out_specs=blk),
        out_shape=jax.ShapeDtypeStruct((NS, R, C), BF),
        compiler_params=pltpu.CompilerParams(dimension_semantics=("parallel", "parallel")),
    )(jnp.reshape(c, (1,)).astype(jnp.int32), gbuf, rsib)


def _chip_sum(pair, recv, me, c):
    NS, R, C = pair.shape
    tr = 512

    def body(s_ref, own_ref, p_ref, o_ref):
        p = [jnp.where(s_ref[0] == j, own_ref[...], p_ref[j]).astype(F32) for j in range(4)]
        o_ref[...] = ((p[0] + p[1]) + p[2]) + p[3]

    return pl.pallas_call(
        body, name="rs_chip_sum",
        grid_spec=pltpu.PrefetchScalarGridSpec(
            num_scalar_prefetch=1, grid=(pl.cdiv(R, tr),),
            in_specs=[pl.BlockSpec((None, tr, C), lambda i, s: (s[0], i, 0)),
                      pl.BlockSpec((NS, tr, C), lambda i, s: (0, i, 0))],
            out_specs=pl.BlockSpec((None, tr, C), lambda i, s: (s[1], i, 0))),
        out_shape=jax.ShapeDtypeStruct((2, R, C), F32),
        compiler_params=pltpu.CompilerParams(dimension_semantics=("parallel",)),
    )(jnp.stack([me, c]).astype(jnp.int32), pair, recv)


MESH = pl.DeviceIdType.MESH
ANY = pl.BlockSpec(memory_space=pl.ANY)


def _place():
    x, y, c = lax.axis_index("x"), lax.axis_index("y"), lax.axis_index("c")
    return x, y, c, [(1 - x, y), (x, 1 - y), (1 - x, 1 - y)]


HBM = pl.BlockSpec(memory_space=pltpu.HBM)
SEM = pl.BlockSpec(memory_space=pltpu.SEMAPHORE)
EFFECT = pltpu.SideEffectType.DATAFLOW_SIDE_EFFECTING


class _Split:
    def __init__(self, tag, arrays, copies, n_copies):
        self.tag, self.copies, k = tag, copies, len(arrays)

        def body(*refs):
            for cp in copies(refs[:k], refs[k], refs[k + 1]):
                cp.start()
            refs[-1][...] = jnp.zeros_like(refs[-1])

        out = pl.pallas_call(
            body, name=tag + "_start",
            out_shape=(pltpu.SemaphoreType.DMA((n_copies,)), pltpu.SemaphoreType.DMA((n_copies,)),
                       *[pltpu.HBM(a.shape, a.dtype) for a in arrays], jax.ShapeDtypeStruct((8, 128), F32)),
            in_specs=[HBM] * k, out_specs=(SEM, SEM, *[HBM] * k, pl.BlockSpec(memory_space=pltpu.VMEM)),
            input_output_aliases={i: 2 + i for i in range(k)},
            compiler_params=pltpu.CompilerParams(has_side_effects=EFFECT),
        )(*[pltpu.with_memory_space_constraint(a, pltpu.HBM) for a in arrays])
        self.send, self.recv, self.arrays, self.token = out[0], out[1], list(out[2:2 + k]), out[-1][0, 0]

    def wait(self, after):
        k, copies = len(self.arrays), self.copies

        def body(*refs):
            for cp in copies(refs[:k], refs[k], refs[k + 1]):
                cp.wait_send()
                cp.wait_recv()

        return list(pl.pallas_call(
            body, name=self.tag + "_wait", out_shape=tuple(pltpu.HBM(a.shape, a.dtype) for a in self.arrays),
            in_specs=[HBM] * k + [SEM, SEM, ANY], out_specs=tuple([HBM] * k),
            input_output_aliases={i: i for i in range(k)},
            compiler_params=pltpu.CompilerParams(has_side_effects=EFFECT),
        )(*self.arrays, self.send, self.recv, after))


def _gather_start(arrs, tag):
    n = len(arrs)
    me = 2 * lax.axis_index("x") + lax.axis_index("y")
    lands = [lax.dynamic_update_index_in_dim(jnp.zeros((4,) + a.shape, a.dtype), a, me, 0) for a in arrs]

    def copies(refs, send, recv):
        x, y, c, chips = _place()
        return [pltpu.make_async_remote_copy(
            src_ref=refs[k], dst_ref=refs[n + k].at[2 * x + y], send_sem=send.at[3 * k + r],
            recv_sem=recv.at[3 * k + r], device_id=(px, py, c), device_id_type=MESH)
            for k in range(n) for r, (px, py) in enumerate(chips)]

    return _Split("gather_" + tag, list(arrs) + lands, copies, 3 * n)


def _to_sibling_start(gbuf, tag):
    NS, _, R, C = gbuf.shape

    def copies(refs, send, recv):
        x, y, c, _ = _place()
        return [pltpu.make_async_remote_copy(
            src_ref=refs[0].at[j, 1 - c], dst_ref=refs[1].at[j], send_sem=send.at[j], recv_sem=recv.at[j],
            device_id=(x, y, 1 - c), device_id_type=MESH) for j in range(NS)]

    return _Split("rs_sibling_" + tag, [gbuf, lax.empty((NS, R, C), gbuf.dtype)], copies, NS)


def _to_chips_start(pbuf, tag):
    def copies(refs, send, recv):
        x, y, c, chips = _place()
        return [pltpu.make_async_remote_copy(
            src_ref=refs[0].at[2 * px + py], dst_ref=refs[1].at[2 * x + y], send_sem=send.at[r], recv_sem=recv.at[r],
            device_id=(px, py, c), device_id_type=MESH) for r, (px, py) in enumerate(chips)]

    return _Split("rs_chips_" + tag, [pbuf, lax.empty(pbuf.shape, pbuf.dtype)], copies, 3)


def _join_start(full, tag):
    def copies(refs, send, recv):
        x, y, c, _ = _place()
        return [pltpu.make_async_remote_copy(
            src_ref=refs[0].at[c], dst_ref=refs[0].at[c], send_sem=send.at[0], recv_sem=recv.at[0],
            device_id=(x, y, 1 - c), device_id_type=MESH)]

    return _Split("rs_join_" + tag, [full], copies, 1)


def _all_reduce_small(v):
    R, C = v.shape

    def body(v_ref, o_ref, g_ref, send, recv, loc):
        x, y, c, chips = _place()
        me, sibling = (x, y, c), (x, y, 1 - c)

        def rows(px, py, pc):
            return g_ref.at[4 * px + 2 * py + pc]

        def copy(k, block, to, src=None):
            return pltpu.make_async_remote_copy(
                src_ref=rows(*block) if src is None else src, dst_ref=rows(*block),
                send_sem=send.at[k], recv_sem=recv.at[k], device_id=to, device_id_type=MESH)

        mine = pltpu.make_async_copy(v_ref, rows(*me), loc)
        mine.start()
        first = [copy(0, me, sibling, src=v_ref)]
        first += [copy(1 + j, me, (*chip, c), src=v_ref) for j, chip in enumerate(chips)]
        for cp in first:
            cp.start()
        passed = [copy(4 + j, (*chip, c), sibling) for j, chip in enumerate(chips)]
        for j, chip in enumerate(chips):
            copy(1 + j, (*chip, c), me).wait_recv()
            passed[j].start()
        copy(0, sibling, me).wait_recv()
        for j, chip in enumerate(chips):
            copy(4 + j, (*chip, 1 - c), me).wait_recv()
        for cp in first + passed:
            cp.wait_send()
        mine.wait()
        acc = g_ref[0]
        for d in range(1, 8):
            acc = acc + g_ref[d]
        o_ref[...] = acc

    vm = pl.BlockSpec(memory_space=pltpu.VMEM)
    return pl.pallas_call(
        body, name="all_reduce_small", in_specs=[vm], out_specs=[vm, vm],
        out_shape=[jax.ShapeDtypeStruct((R, C), F32), jax.ShapeDtypeStruct((8, R, C), F32)],
        scratch_shapes=[pltpu.SemaphoreType.DMA((7,)), pltpu.SemaphoreType.DMA((7,)), pltpu.SemaphoreType.DMA],
    )(v)[0]


WEIGHTS = ['ffn1_norm', 'ffn1_w_gate', 'ffn1_w_up', 'ffn1_w_down', 'mix_norm', 'w_in', 'conv_w', 'conv_b', 'dt_bias',
           'a_log', 'd_skip', 'ssd_norm', 'sgu_ln_g', 'sgu_ln_b', 'sgu_w', 'sgu_b', 'w_out', 'ffn2_norm',
           'ffn2_w_gate', 'ffn2_w_up', 'ffn2_w_down', 'final_norm']
SHARDED = ['ffn1_w_gate', 'ffn1_w_up', 'ffn1_w_down', 'w_in', 'conv_w', 'w_out', 'ffn2_w_gate', 'ffn2_w_up',
           'ffn2_w_down']
SMALL = [n for n in WEIGHTS if n not in SHARDED]
GROUPS = [("ffn1", ["ffn1_w_gate", "ffn1_w_up", "ffn1_w_down"]), ("mix", ["w_in", "conv_w", "w_out"]),
          ("ffn2", ["ffn2_w_gate", "ffn2_w_up", "ffn2_w_down"])]
RS_COLS = 1024
DEPTH = 2


def _pack_w_in(w):
    return jnp.concatenate([w[..., 0:1152], w[..., 1536:2432], w[..., 1152:1536],
                            jnp.repeat(w[..., 2432:2438], HEAD, axis=-1), w[..., 2438:2950]], axis=-1)


def _unpack_w_in(dq, ds, du):
    return jnp.concatenate([dq, ds[:, 896:1280], ds[:, 0:896], ds[:, 1280::HEAD], du], axis=-1)


def _ffn_fwd(x, g, wg, wu, wd):
    xo, hb, G, U = _ffn_fwd_k(x, g, wg, wu, wd)
    return xo, (x, hb, G, U)


def _ffn_bwd(dxo, saved, g, wg, wu, wd):
    x, hb, G, U = saved
    dx, dg, dG, dU, dyb = _ffn_bwd_k1(dxo, x, g, G, U, wg, wu, wd)
    dwg, dwu, dwd = _ffn_bwd_k2(hb, dyb, G, U, dG, dU)
    return dx, dg, dwg, dwu, dwd


def _mix_fwd(x, P):
    hb = _rms_fwd(x, P["mix_norm"])
    qkv = _mm_nn(hb, P["w_qkv"], out_dtype=BF)
    sin = _mm_nn(hb, P["w_ssd"])
    uv = _mm_nn(hb, P["w_uv"])
    o1, l1 = _attn_fwd(qkv, 1)
    o2, l2 = _attn_fwd(qkv, 4)
    o3, l3 = _attn_fwd(qkv, 16)
    y_att, lse = _attn_combine(o1, o2, o3, l1, l2, l3)
    y_ssd, hprev = _ssd_fwd(sin, *P["ssd"])
    y_sgu = _sgu_fwd(uv, *P["sgu"])
    ycat = jnp.concatenate([y_att, y_ssd, y_sgu], axis=1).astype(BF)
    return _mm_nn(ycat, P["w_out"], res=x), (x, hb, qkv, sin, uv, y_att, lse, hprev, ycat)


def _mix_bwd(dxo, saved, P):
    x, hb, qkv, sin, uv, y_att, lse, hprev, ycat = saved
    dycat = _mm_nt(dxo, P["w_out"])
    dwout = _mm_tn(ycat, dxo)
    dy_att, dy_ssd, dy_sgu = dycat[:, 0:ATT_W], dycat[:, ATT_W:ATT_W + SSD_W], dycat[:, ATT_W + SSD_W:]
    dqkv = _sum_branches([_attn_bwd(qkv, dy_att, y_att, lse, d) for d in DILATIONS])
    dsin, dcw, dcb, dvec = _ssd_bwd(sin, hprev, dy_ssd, *P["ssd"])
    duv, dsw, dsbias, dln = _sgu_bwd(uv, dy_sgu, *P["sgu"])
    dwin = _unpack_w_in(_mm_tn(hb, dqkv), _mm_tn(hb, dsin), _mm_tn(hb, duv))
    dh = _mm_nt(dqkv, P["w_qkv"])
    dh = _mm_nt(dsin, P["w_ssd"], res=dh)
    dh = _mm_nt(duv, P["w_uv"], res=dh)
    dx, dg = _rms_bwd(x, P["mix_norm"], dh, dxo)
    grads = dict(
        mix_norm=dg[0], w_in=dwin, conv_w=dcw, conv_b=dcb[0], dt_bias=dvec[0, ::HEAD], a_log=dvec[1, ::HEAD],
        d_skip=jnp.sum(dvec[2].reshape(6, HEAD), axis=-1), ssd_norm=dvec[3], sgu_ln_g=dln[0], sgu_ln_b=dln[1],
        sgu_w=dsw, sgu_b=jnp.sum(dsbias.reshape(CHUNK, 4, HEAD), axis=-1).T, w_out=dwout)
    return dx, grads


def _shard_major(name, g):
    if name == "conv_w":
        return g.reshape(4, 4, 224).transpose(1, 0, 2).reshape(4, -1)
    return g.reshape(4, -1)


def kernel(x, ffn1_norm, ffn1_w_gate, ffn1_w_up, ffn1_w_down, mix_norm, w_in, conv_w, conv_b, dt_bias, a_log, d_skip, ssd_norm, sgu_ln_g, sgu_ln_b, sgu_w, sgu_b, w_out, ffn2_norm, ffn2_w_gate, ffn2_w_up, ffn2_w_down, final_norm, loss_target, m_ffn1_norm, m_ffn1_w_gate, m_ffn1_w_up, m_ffn1_w_down, m_mix_norm, m_w_in, m_conv_w, m_conv_b, m_dt_bias, m_a_log, m_d_skip, m_ssd_norm, m_sgu_ln_g, m_sgu_ln_b, m_sgu_w, m_sgu_b, m_w_out, m_ffn2_norm, m_ffn2_w_gate, m_ffn2_w_up, m_ffn2_w_down, m_final_norm, v_ffn1_norm, v_ffn1_w_gate, v_ffn1_w_up, v_ffn1_w_down, v_mix_norm, v_w_in, v_conv_w, v_conv_b, v_dt_bias, v_a_log, v_d_skip, v_ssd_norm, v_sgu_ln_g, v_sgu_ln_b, v_sgu_w, v_sgu_b, v_w_out, v_ffn2_norm, v_ffn2_w_gate, v_ffn2_w_up, v_ffn2_w_down, v_final_norm):
    given = dict(x=x, ffn1_norm=ffn1_norm, ffn1_w_gate=ffn1_w_gate, ffn1_w_up=ffn1_w_up, ffn1_w_down=ffn1_w_down, mix_norm=mix_norm, w_in=w_in, conv_w=conv_w, conv_b=conv_b, dt_bias=dt_bias, a_log=a_log, d_skip=d_skip, ssd_norm=ssd_norm, sgu_ln_g=sgu_ln_g, sgu_ln_b=sgu_ln_b, sgu_w=sgu_w, sgu_b=sgu_b, w_out=w_out, ffn2_norm=ffn2_norm, ffn2_w_gate=ffn2_w_gate, ffn2_w_up=ffn2_w_up, ffn2_w_down=ffn2_w_down, final_norm=final_norm, loss_target=loss_target, m_ffn1_norm=m_ffn1_norm, m_ffn1_w_gate=m_ffn1_w_gate, m_ffn1_w_up=m_ffn1_w_up, m_ffn1_w_down=m_ffn1_w_down, m_mix_norm=m_mix_norm, m_w_in=m_w_in, m_conv_w=m_conv_w, m_conv_b=m_conv_b, m_dt_bias=m_dt_bias, m_a_log=m_a_log, m_d_skip=m_d_skip, m_ssd_norm=m_ssd_norm, m_sgu_ln_g=m_sgu_ln_g, m_sgu_ln_b=m_sgu_ln_b, m_sgu_w=m_sgu_w, m_sgu_b=m_sgu_b, m_w_out=m_w_out, m_ffn2_norm=m_ffn2_norm, m_ffn2_w_gate=m_ffn2_w_gate, m_ffn2_w_up=m_ffn2_w_up, m_ffn2_w_down=m_ffn2_w_down, m_final_norm=m_final_norm, v_ffn1_norm=v_ffn1_norm, v_ffn1_w_gate=v_ffn1_w_gate, v_ffn1_w_up=v_ffn1_w_up, v_ffn1_w_down=v_ffn1_w_down, v_mix_norm=v_mix_norm, v_w_in=v_w_in, v_conv_w=v_conv_w, v_conv_b=v_conv_b, v_dt_bias=v_dt_bias, v_a_log=v_a_log, v_d_skip=v_d_skip, v_ssd_norm=v_ssd_norm, v_sgu_ln_g=v_sgu_ln_g, v_sgu_ln_b=v_sgu_ln_b, v_sgu_w=v_sgu_w, v_sgu_b=v_sgu_b, v_w_out=v_w_out, v_ffn2_norm=v_ffn2_norm, v_ffn2_w_gate=v_ffn2_w_gate, v_ffn2_w_up=v_ffn2_w_up, v_ffn2_w_down=v_ffn2_w_down, v_final_norm=v_final_norm)
    T = given["x"].shape[0] * given["x"].shape[1]
    D = given["x"].shape[2]
    x0 = given["x"].reshape(T, D)
    tgt = given["loss_target"].reshape(T, D)
    c = lax.axis_index("c")

    bf = {n: given[n].astype(BF) for n in SHARDED if n not in ("w_in", "conv_w")}
    bf["w_in"] = _pack_w_in(given["w_in"]).astype(BF)
    bf["conv_w"] = given["conv_w"]
    gathers = {(i, gname): _gather_start([bf[n][i] for n in names], f"l{i}_{gname}")
               for i in range(DEPTH) for gname, names in GROUPS}
    token = functools.reduce(lambda a, b: a + b, [g.token for g in gathers.values()])

    def gathered(i, gname, after):
        return gathers[(i, gname)].wait(after)[3:]

    def mix_params(i, got):
        win = got[0].reshape(D, W_QKV + W_SSD + W_UV)
        rep = lambda v: jnp.repeat(v, HEAD)[None]
        ssd = (got[1].transpose(1, 0, 2).reshape(4, SSD_CONV_DIM), given["conv_b"][i][None],
               rep(given["dt_bias"][i]), rep(given["a_log"][i]), rep(given["d_skip"][i]), given["ssd_norm"][i][None])
        sgu = (given["sgu_ln_g"][i][None], given["sgu_ln_b"][i][None], given["sgu_w"][i],
               jnp.repeat(given["sgu_b"][i].T, HEAD, axis=1))
        return dict(mix_norm=given["mix_norm"][i][None], w_qkv=win[:, 0:W_QKV], w_ssd=win[:, W_QKV:W_QKV + W_SSD],
                    w_uv=win[:, W_QKV + W_SSD:], w_out=got[2].reshape(-1, D), ssd=ssd, sgu=sgu)

    x = x0
    tape = []
    for i in range(DEPTH):
        P = dict(ffn1=(given["ffn1_norm"][i][None] + (token if i == 0 else 0.0), *gathered(i, "ffn1", x)))
        x, s1 = _ffn_fwd(x, *P["ffn1"])
        P.update(mix_params(i, gathered(i, "mix", x)))
        x, s2 = _mix_fwd(x, P)
        P["ffn2"] = (given["ffn2_norm"][i][None], *gathered(i, "ffn2", x))
        x, s3 = _ffn_fwd(x, *P["ffn2"])
        tape.append((P, s1, s2, s3))
    loss_part, dx, dgf = _final_loss(x, given["final_norm"][None], tgt)

    me = 2 * lax.axis_index("x") + lax.axis_index("y")
    jobs = []

    def rs_begin(i, gname, gd):
        pieces = [_shard_major(n, gd[n]) for n in dict(GROUPS)[gname]]
        n_el = sum(p.shape[1] for p in pieces)
        rows_half = -(-n_el // (2 * RS_COLS * 16)) * 16
        fill = jnp.zeros((4, 2 * rows_half * RS_COLS - n_el), F32)
        gbuf = jnp.concatenate(pieces + [fill], axis=1).reshape(4, 2, rows_half, RS_COLS)
        jobs.append(dict(key=(i, gname), tag=f"l{i}_{gname}", stage=1, op=_to_sibling_start(gbuf, f"l{i}_{gname}")))

    def rs_advance(job, after):
        if job["stage"] == 1:
            gbuf, landed = job["op"].wait(after)
            job.update(stage=2, op=_to_chips_start(_pair_add(gbuf, landed, c), job["tag"]))
        elif job["stage"] == 2:
            pair, landed = job["op"].wait(after)
            job.update(stage=3, op=_join_start(_chip_sum(pair, landed, me, c), job["tag"]))
        elif job["stage"] == 3:
            job.update(stage=4, out=job["op"].wait(after)[0].reshape(-1))

    def tick(after, begin=None):
        for job in jobs:
            rs_advance(job, after)
        if begin is not None:
            rs_begin(*begin)
        return functools.reduce(lambda a, b: a + b, [j["op"].token for j in jobs if j["stage"] < 4], 0.0)

    grads = [dict() for _ in range(DEPTH)]
    tok = 0.0
    for i in reversed(range(DEPTH)):
        P, s1, s2, s3 = tape[i]
        g = grads[i]
        norm, wg, wu, wd = P["ffn2"]
        dx, dn2, g["ffn2_w_gate"], g["ffn2_w_up"], g["ffn2_w_down"] = _ffn_bwd(dx, s3, norm + tok, wg, wu, wd)
        tok = tick(dx, (i, "ffn2", g))
        dx, gm = _mix_bwd(dx, s2, {**P, "mix_norm": P["mix_norm"] + tok})
        g.update(gm)
        tok = tick(dx, (i, "mix", g))
        norm, wg, wu, wd = P["ffn1"]
        dx, dn1, g["ffn1_w_gate"], g["ffn1_w_up"], g["ffn1_w_down"] = _ffn_bwd(dx, s1, norm + tok, wg, wu, wd)
        tok = tick(dx, (i, "ffn1", g))
        g["ffn1_norm"], g["ffn2_norm"] = dn1[0], dn2[0]
    while any(j["stage"] < 4 for j in jobs):
        tick(dx)
    grad_x = dx.reshape(given["x"].shape)
    reduced = {j["key"]: j["out"] for j in jobs}

    order = [n for n in SMALL if n != "final_norm"] + ["final_norm"]
    small = [jnp.stack([grads[i][n] for i in range(DEPTH)]) for n in order[:-1]] + [dgf[0], loss_part[0, 0:1]]
    n_small = sum(s.size for s in small)
    rows_small = -(-n_small // (128 * 8)) * 8

    def flat(arrs):
        fill = rows_small * 128 - sum(a.size for a in arrs)
        return jnp.concatenate([a.reshape(-1) for a in arrs] + [jnp.zeros((fill,), F32)]).reshape(rows_small, 128)

    gsmall = _all_reduce_small(flat(small)).reshape(-1)

    grad_w = {}
    for gname, names in GROUPS:
        off = 0
        for n in names:
            size = given[n][0].size
            grad_w[n] = jnp.stack([reduced[(i, gname)][off:off + size].reshape(given[n].shape[1:])
                                   for i in range(DEPTH)])
            off += size
    off = 0
    for n in order:
        size = given[n].size
        grad_w[n] = gsmall[off:off + size].reshape(given[n].shape)
        off += size
    loss = gsmall[off]

    delta, new_m, new_v = {}, {}, {}
    for n in SHARDED:
        shp = given[n].shape
        two_d = (shp[0] * shp[1], shp[2])
        d, m2, v2 = _adamw(*[a.reshape(two_d) for a in (given[n], grad_w[n], given["m_" + n], given["v_" + n])])
        delta[n], new_m[n], new_v[n] = d.reshape(shp), m2.reshape(shp), v2.reshape(shp)
    packed = [flat([src[pre + n] for n in order])
              for src, pre in ((given, ""), (grad_w, ""), (given, "m_"), (given, "v_"))]
    outs = [o.reshape(-1) for o in _adamw(*packed)]
    off = 0
    for n in order:
        size = given[n].size
        for dst, o in zip((delta, new_m, new_v), outs):
            dst[n] = o[off:off + size].reshape(given[n].shape)
        off += size

    return (loss, grad_x, *[grad_w[n] for n in WEIGHTS], *[delta[n] for n in WEIGHTS],
            *[new_m[n] for n in WEIGHTS], *[new_v[n] for n in WEIGHTS])
```

```python
import functools
import math

import jax
import jax.numpy as jnp
from jax import lax
from jax.experimental import pallas as pl
from jax.experimental.pallas import tpu as pltpu

F32 = jnp.float32
BF = jnp.bfloat16

RMS_EPS = 1e-6
LN_EPS = 1e-5
SEQ = 2048
CHUNK = 128
N_CHUNK = SEQ // CHUNK
ATT_W = 384
HEAD = 64
SSD_W = 384
SSD_CONV_DIM = 896
SSD_STATE = 128
SGU_W = 256
DILATIONS = (1, 4, 16)
W_QKV = 3 * ATT_W
W_SSD = SSD_CONV_DIM + SSD_W + SSD_W
W_UV = 2 * SGU_W
ADAM_LR = 0.001
ADAM_B1 = 0.9
ADAM_B2 = 0.999
ADAM_EPS = 1e-08
ADAM_WD = 0.01
ADAM_STEP = 10
NEG = -1e30


def _dot(a, b):
    return jnp.dot(a, b, preferred_element_type=F32)


def _dot_nt(a, b):
    return lax.dot_general(a, b, (((1,), (1,)), ((), ())), preferred_element_type=F32)


def _dot_tn(a, b):
    return lax.dot_general(a, b, (((0,), (0,)), ((), ())), preferred_element_type=F32)


def _sigmoid(x):
    return 1.0 / (1.0 + jnp.exp(-x))


def _call(body, *, name, grid, in_specs, out_specs, out_shape, scratch=(), sem=None):
    return pl.pallas_call(
        body, name=name, grid=grid, in_specs=in_specs, out_specs=out_specs, out_shape=out_shape,
        scratch_shapes=list(scratch),
        compiler_params=pltpu.CompilerParams(dimension_semantics=sem),
    )


def _tile(n, want):
    t = min(n, want)
    while n % t:
        t //= 2
    return t


def _rms_fwd(x, g):
    T, D = x.shape
    tm = _tile(T, 512)

    def body(x_ref, g_ref, h_ref):
        xf = x_ref[...]
        r = lax.rsqrt(jnp.mean(xf * xf, axis=-1, keepdims=True) + RMS_EPS)
        h_ref[...] = (xf * r * g_ref[...]).astype(BF)

    return _call(body, name="rms_fwd", grid=(T // tm,),
                 in_specs=[pl.BlockSpec((tm, D), lambda i: (i, 0)), pl.BlockSpec((1, D), lambda i: (0, 0))],
                 out_specs=pl.BlockSpec((tm, D), lambda i: (i, 0)),
                 out_shape=jax.ShapeDtypeStruct((T, D), BF), sem=("parallel",))(x, g)


def _rms_bwd(x, g, dh, dres):
    T, D = x.shape
    tm = _tile(T, 512)

    def body(x_ref, g_ref, dh_ref, dr_ref, dx_ref, dg_ref):
        @pl.when(pl.program_id(0) == 0)
        def _():
            dg_ref[...] = jnp.zeros_like(dg_ref)

        xf = x_ref[...]
        r = lax.rsqrt(jnp.mean(xf * xf, axis=-1, keepdims=True) + RMS_EPS)
        dh_ = dh_ref[...]
        u = dh_ * g_ref[...]
        mu = jnp.mean(u * xf, axis=-1, keepdims=True)
        dx_ref[...] = dr_ref[...] + r * (u - xf * (r * r * mu))
        dg_ref[...] += jnp.sum(dh_ * xf * r, axis=0, keepdims=True)

    row = pl.BlockSpec((tm, D), lambda i: (i, 0))
    vec = pl.BlockSpec((1, D), lambda i: (0, 0))
    return _call(body, name="rms_bwd", grid=(T // tm,), in_specs=[row, vec, row, row], out_specs=[row, vec],
                 out_shape=[jax.ShapeDtypeStruct((T, D), F32), jax.ShapeDtypeStruct((1, D), F32)],
                 sem=("arbitrary",))(x, g, dh, dres)


def _final_loss(x, g, tgt):
    T, D = x.shape
    tm = _tile(T, 512)

    def body(x_ref, g_ref, t_ref, l_ref, dx_ref, dg_ref):
        @pl.when(pl.program_id(0) == 0)
        def _():
            dg_ref[...] = jnp.zeros_like(dg_ref)
            l_ref[...] = jnp.zeros_like(l_ref)

        xf = x_ref[...]
        gg = g_ref[...]
        r = lax.rsqrt(jnp.mean(xf * xf, axis=-1, keepdims=True) + RMS_EPS)
        xn = xf * r
        e = xn * gg - t_ref[...]
        part = 0.5 * jnp.sum(jnp.mean(e * e, axis=-1, keepdims=True), axis=0, keepdims=True)
        l_ref[...] += jnp.broadcast_to(part, l_ref.shape)
        dy = e * (1.0 / D)
        u = dy * gg
        mu = jnp.mean(u * xf, axis=-1, keepdims=True)
        dx_ref[...] = r * (u - xf * (r * r * mu))
        dg_ref[...] += jnp.sum(dy * xn, axis=0, keepdims=True)

    row = pl.BlockSpec((tm, D), lambda i: (i, 0))
    vec = pl.BlockSpec((1, D), lambda i: (0, 0))
    lsp = pl.BlockSpec((1, 128), lambda i: (0, 0))
    return _call(body, name="final_loss", grid=(T // tm,), in_specs=[row, vec, row], out_specs=[lsp, row, vec],
                 out_shape=[jax.ShapeDtypeStruct((1, 128), F32), jax.ShapeDtypeStruct((T, D), F32),
                            jax.ShapeDtypeStruct((1, D), F32)],
                 sem=("arbitrary",))(x, g, tgt)


def _resident(shape):
    return pl.BlockSpec(shape, lambda *_: (0,) * len(shape), pipeline_mode=pl.Buffered(1))


def _ffn_fwd_k(x, gn, wg, wu, wd):
    T, D = x.shape
    NS, _, Fs = wg.shape
    tm = _tile(T, 512)

    def body(x_ref, gn_ref, wg_ref, wu_ref, wd_ref, o_ref, h_ref, g_ref, u_ref, hs, acc):
        j = pl.program_id(1)

        @pl.when(j == 0)
        def _():
            xf = x_ref[...]
            r = lax.rsqrt(jnp.mean(xf * xf, axis=-1, keepdims=True) + RMS_EPS)
            hs[...] = (xf * r * gn_ref[...]).astype(BF)
            h_ref[...] = hs[...]
            acc[...] = jnp.zeros_like(acc)

        h = hs[...]
        g = _dot(h, wg_ref[j])
        u = _dot(h, wu_ref[j])
        g_ref[...] = g.astype(BF)
        u_ref[...] = u.astype(BF)
        acc[...] += _dot((g * _sigmoid(g) * u).astype(BF), wd_ref[j])

        @pl.when(j == NS - 1)
        def _():
            o_ref[...] = x_ref[...] + 0.5 * acc[...]

    row = pl.BlockSpec((tm, D), lambda i, j: (i, 0))
    act = pl.BlockSpec((None, tm, Fs), lambda i, j: (j, i, 0))
    sh = jax.ShapeDtypeStruct((NS, T, Fs), BF)
    return _call(body, name="ffn_fwd", grid=(T // tm, NS),
                 in_specs=[row, pl.BlockSpec((1, D), lambda i, j: (0, 0)), _resident(wg.shape), _resident(wu.shape),
                           _resident(wd.shape)],
                 out_specs=[row, row, act, act],
                 out_shape=[jax.ShapeDtypeStruct((T, D), F32), jax.ShapeDtypeStruct((T, D), BF), sh, sh],
                 scratch=[pltpu.VMEM((tm, D), BF), pltpu.VMEM((tm, D), F32)],
                 sem=("parallel", "arbitrary"))(x, gn, wg, wu, wd)


def _ffn_bwd_k1(dxo, x, gn, g, u, wg, wu, wd):
    NS, T, Fs = g.shape
    D = x.shape[1]
    tm = _tile(T, 512)

    def body(dxo_ref, x_ref, gn_ref, g_ref, u_ref, wg_ref, wu_ref, wd_ref,
             dx_ref, dgn_ref, dg_ref, du_ref, dy_ref, dys, acc):
        i, j = pl.program_id(0), pl.program_id(1)

        @pl.when((i == 0) & (j == 0))
        def _():
            dgn_ref[...] = jnp.zeros_like(dgn_ref)

        @pl.when(j == 0)
        def _():
            dys[...] = (0.5 * dxo_ref[...]).astype(BF)
            dy_ref[...] = dys[...]
            acc[...] = jnp.zeros_like(acc)

        da = _dot_nt(dys[...], wd_ref[j])
        gf = g_ref[...].astype(F32)
        uf = u_ref[...].astype(F32)
        sg = _sigmoid(gf)
        dg = (da * uf * (sg * (1.0 + gf * (1.0 - sg)))).astype(BF)
        du = (da * gf * sg).astype(BF)
        dg_ref[...] = dg
        du_ref[...] = du
        acc[...] += _dot_nt(dg, wg_ref[j]) + _dot_nt(du, wu_ref[j])

        @pl.when(j == NS - 1)
        def _():
            xf = x_ref[...]
            r = lax.rsqrt(jnp.mean(xf * xf, axis=-1, keepdims=True) + RMS_EPS)
            dh = acc[...]
            uu = dh * gn_ref[...]
            mu = jnp.mean(uu * xf, axis=-1, keepdims=True)
            dx_ref[...] = dxo_ref[...] + r * (uu - xf * (r * r * mu))
            dgn_ref[...] += jnp.sum(dh * xf * r, axis=0, keepdims=True)

    row = pl.BlockSpec((tm, D), lambda i, j: (i, 0))
    vec = pl.BlockSpec((1, D), lambda i, j: (0, 0))
    act = pl.BlockSpec((None, tm, Fs), lambda i, j: (j, i, 0))
    sh = jax.ShapeDtypeStruct((NS, T, Fs), BF)
    return _call(body, name="ffn_bwd_x", grid=(T // tm, NS),
                 in_specs=[row, row, vec, act, act, _resident(wg.shape), _resident(wu.shape), _resident(wd.shape)],
                 out_specs=[row, vec, act, act, row],
                 out_shape=[jax.ShapeDtypeStruct((T, D), F32), jax.ShapeDtypeStruct((1, D), F32), sh, sh,
                            jax.ShapeDtypeStruct((T, D), BF)],
                 scratch=[pltpu.VMEM((tm, D), BF), pltpu.VMEM((tm, D), F32)],
                 sem=("arbitrary", "arbitrary"))(dxo, x, gn, g, u, wg, wu, wd)


def _ffn_bwd_k2(hb, dyb, g, u, dg, du):
    NS, T, Fs = g.shape
    D = hb.shape[1]
    tk = _tile(T, 512)

    def body(h_ref, dy_ref, g_ref, u_ref, dg_ref, du_ref, og_ref, ou_ref, od_ref):
        @pl.when(pl.program_id(1) == 0)
        def _():
            og_ref[...] = jnp.zeros_like(og_ref)
            ou_ref[...] = jnp.zeros_like(ou_ref)
            od_ref[...] = jnp.zeros_like(od_ref)

        h = h_ref[...]
        gf = g_ref[...].astype(F32)
        a = (gf * _sigmoid(gf) * u_ref[...].astype(F32)).astype(BF)
        og_ref[...] += _dot_tn(h, dg_ref[...])
        ou_ref[...] += _dot_tn(h, du_ref[...])
        od_ref[...] += _dot_tn(a, dy_ref[...])

    row = pl.BlockSpec((tk, D), lambda j, k: (k, 0))
    act = pl.BlockSpec((None, tk, Fs), lambda j, k: (j, k, 0))
    return _call(body, name="ffn_bwd_w", grid=(NS, T // tk), in_specs=[row, row, act, act, act, act],
                 out_specs=[pl.BlockSpec((None, D, Fs), lambda j, k: (j, 0, 0))] * 2
                 + [pl.BlockSpec((None, Fs, D), lambda j, k: (j, 0, 0))],
                 out_shape=[jax.ShapeDtypeStruct((NS, D, Fs), F32)] * 2 + [jax.ShapeDtypeStruct((NS, Fs, D), F32)],
                 sem=("parallel", "arbitrary"))(hb, dyb, g, u, dg, du)


def _mm_nn(a, b, res=None, out_dtype=F32):
    T, K = a.shape
    N = b.shape[1]
    tm = _tile(T, 512)
    tn = N if N <= 2048 else _tile(N, 1024)

    def body(*refs):
        if res is None:
            a_ref, b_ref, o_ref = refs
            o_ref[...] = _dot(a_ref[...], b_ref[...]).astype(out_dtype)
        else:
            a_ref, b_ref, r_ref, o_ref = refs
            o_ref[...] = (r_ref[...] + _dot(a_ref[...], b_ref[...])).astype(out_dtype)

    o = pl.BlockSpec((tm, tn), lambda i, j: (i, j))
    ins = [pl.BlockSpec((tm, K), lambda i, j: (i, 0)), pl.BlockSpec((K, tn), lambda i, j: (0, j))]
    args = [a, b]
    if res is not None:
        ins.append(o)
        args.append(res)
    return _call(body, name="mm_nn", grid=(T // tm, N // tn), in_specs=ins, out_specs=o,
                 out_shape=jax.ShapeDtypeStruct((T, N), out_dtype), sem=("parallel", "parallel"))(*args)


def _mm_nt(a, b, res=None):
    T, K = a.shape
    N = b.shape[0]
    tm = _tile(T, 512)

    def body(*refs):
        if res is None:
            a_ref, b_ref, o_ref = refs
            o_ref[...] = _dot_nt(a_ref[...].astype(BF), b_ref[...])
        else:
            a_ref, b_ref, r_ref, o_ref = refs
            o_ref[...] = r_ref[...] + _dot_nt(a_ref[...].astype(BF), b_ref[...])

    o = pl.BlockSpec((tm, N), lambda i: (i, 0))
    ins = [pl.BlockSpec((tm, K), lambda i: (i, 0)), pl.BlockSpec((N, K), lambda i: (0, 0))]
    args = [a, b]
    if res is not None:
        ins.append(o)
        args.append(res)
    return _call(body, name="mm_nt", grid=(T // tm,), in_specs=ins, out_specs=o,
                 out_shape=jax.ShapeDtypeStruct((T, N), F32), sem=("parallel",))(*args)


def _mm_tn(a, b):
    T, M = a.shape
    N = b.shape[1]
    tk = _tile(T, 512)
    tmm = _tile(M, 512)

    def body(a_ref, b_ref, o_ref):
        @pl.when(pl.program_id(1) == 0)
        def _():
            o_ref[...] = jnp.zeros_like(o_ref)

        o_ref[...] += _dot_tn(a_ref[...].astype(BF), b_ref[...].astype(BF))

    return _call(body, name="mm_tn", grid=(M // tmm, T // tk),
                 in_specs=[pl.BlockSpec((tk, tmm), lambda i, k: (k, i)), pl.BlockSpec((tk, N), lambda i, k: (k, 0))],
                 out_specs=pl.BlockSpec((tmm, N), lambda i, k: (i, 0)),
                 out_shape=jax.ShapeDtypeStruct((M, N), F32), sem=("parallel", "arbitrary"))(a, b)


def _lane_mask(e, width=128):
    return (lax.broadcasted_iota(jnp.int32, (1, width), 1) // HEAD) == e


def _band_mask(n):
    qi = lax.broadcasted_iota(jnp.int32, (CHUNK, 2 * CHUNK), 0)
    kj = lax.broadcasted_iota(jnp.int32, (CHUNK, 2 * CHUNK), 1)
    dist = qi + CHUNK - kj
    return (dist >= 0) & (dist <= CHUNK) & ((kj >= CHUNK) | (n > 0))


def _prev_cur(ref, n):
    cur = pl.multiple_of(n * CHUNK, CHUNK)
    prv = pl.multiple_of(jnp.maximum(n - 1, 0) * CHUNK, CHUNK)
    return jnp.concatenate([ref[pl.ds(prv, CHUNK), :], ref[pl.ds(cur, CHUNK), :]], axis=0), prv, cur


def _attn_fwd(qkv, dil):
    T = qkv.shape[0]
    B, L = T // SEQ, SEQ // dil
    nb = L // CHUNK
    scale = HEAD ** -0.5

    def body(q_ref, k_ref, v_ref, o_ref, l_ref):
        n = pl.program_id(2)
        q = q_ref[...]
        kk, _, _ = _prev_cur(k_ref, n)
        vv, _, _ = _prev_cur(v_ref, n)
        mask = _band_mask(n)
        for t in range(ATT_W // 128):
            sl = slice(128 * t, 128 * (t + 1))
            qt, kt, vt = q[:, sl], kk[:, sl], vv[:, sl]
            o_pair = jnp.zeros((CHUNK, 128), F32)
            l_pair = jnp.zeros((CHUNK, 128), F32)
            for e in range(2):
                lm = _lane_mask(e)
                s = _dot_nt(jnp.where(lm, qt, jnp.zeros_like(qt)), kt) * scale
                s = jnp.where(mask, s, NEG)
                m = jnp.max(s, axis=-1, keepdims=True)
                p = jnp.exp(s - m)
                den = jnp.sum(p, axis=-1, keepdims=True)
                o = _dot(p.astype(BF), vt) / den
                o_pair = jnp.where(lm, o, o_pair)
                l_pair = jnp.where(lm, m + jnp.log(den), l_pair)
            o_ref[:, sl] = o_pair
            l_ref[:, sl] = l_pair

    qv = qkv.reshape(B, L, dil * W_QKV)
    o = pl.BlockSpec((None, CHUNK, ATT_W), lambda b, r, n: (b, n, r))
    sh = jax.ShapeDtypeStruct((B, L, dil * ATT_W), F32)
    out, lse = _call(
        body, name=f"attn_fwd_d{dil}", grid=(B, dil, nb),
        in_specs=[pl.BlockSpec((None, CHUNK, ATT_W), lambda b, r, n: (b, n, 3 * r)),
                  pl.BlockSpec((None, L, ATT_W), lambda b, r, n: (b, 0, 3 * r + 1)),
                  pl.BlockSpec((None, L, ATT_W), lambda b, r, n: (b, 0, 3 * r + 2))],
        out_specs=[o, o], out_shape=[sh, sh], sem=("parallel", "parallel", "parallel"))(qv, qv, qv)
    return out.reshape(T, ATT_W), lse.reshape(T, ATT_W)


def _attn_combine(o1, o2, o3, l1, l2, l3):
    T = o1.shape[0]
    tm = _tile(T, 512)

    def body(o1_ref, o2_ref, o3_ref, l1_ref, l2_ref, l3_ref, y_ref, l_ref):
        a, b, c = l1_ref[...], l2_ref[...], l3_ref[...]
        m = jnp.maximum(jnp.maximum(a, b), c)
        ea, eb, ec = jnp.exp(a - m), jnp.exp(b - m), jnp.exp(c - m)
        z = ea + eb + ec
        y_ref[...] = (ea * o1_ref[...] + eb * o2_ref[...] + ec * o3_ref[...]) / z
        l_ref[...] = m + jnp.log(z)

    row = pl.BlockSpec((tm, ATT_W), lambda i: (i, 0))
    sh = jax.ShapeDtypeStruct((T, ATT_W), F32)
    return _call(body, name="attn_combine", grid=(T // tm,), in_specs=[row] * 6, out_specs=[row, row],
                 out_shape=[sh, sh], sem=("parallel",))(o1, o2, o3, l1, l2, l3)


def _attn_bwd(qkv, do, out, lse, dil):
    T = qkv.shape[0]
    B, L = T // SEQ, SEQ // dil
    nb = L // CHUNK
    scale = HEAD ** -0.5

    def body(q_ref, k_ref, v_ref, do_ref, out_ref, lse_ref, dq_ref, dk_ref, dv_ref):
        n = pl.program_id(2)

        @pl.when(n == 0)
        def _():
            dk_ref[...] = jnp.zeros_like(dk_ref)
            dv_ref[...] = jnp.zeros_like(dv_ref)

        q = q_ref[...]
        kk, prv, cur = _prev_cur(k_ref, n)
        vv, _, _ = _prev_cur(v_ref, n)
        mask = _band_mask(n)
        do_ = do_ref[...]
        dlt = do_ * out_ref[...]
        ls = lse_ref[...]
        for t in range(ATT_W // 128):
            sl = slice(128 * t, 128 * (t + 1))
            qt, kt, vt = q[:, sl], kk[:, sl], vv[:, sl]
            dq_pair = jnp.zeros((CHUNK, 128), F32)
            dk_acc = jnp.zeros((2 * CHUNK, 128), F32)
            dv_acc = jnp.zeros((2 * CHUNK, 128), F32)
            for e in range(2):
                lm = _lane_mask(e)
                qm = jnp.where(lm, qt, jnp.zeros_like(qt))
                s = _dot_nt(qm, kt) * scale
                lse_col = ls[:, 128 * t + HEAD * e:128 * t + HEAD * e + 1]
                p = jnp.exp(jnp.where(mask, s - lse_col, NEG))
                dom = jnp.where(lm, do_[:, sl], 0.0).astype(BF)
                dv_acc += _dot_tn(p.astype(BF), dom)
                dp = _dot_nt(dom, vt)
                delta = jnp.sum(jnp.where(lm, dlt[:, sl], 0.0), axis=-1, keepdims=True)
                ds = (p * (dp - delta) * scale).astype(BF)
                dq_pair += jnp.where(lm, _dot(ds, kt), 0.0)
                dk_acc += _dot_tn(ds, qm)
            dq_ref[:, sl] = dq_pair
            dk_ref[pl.ds(cur, CHUNK), sl] += dk_acc[CHUNK:]
            dk_ref[pl.ds(prv, CHUNK), sl] += dk_acc[:CHUNK]
            dv_ref[pl.ds(cur, CHUNK), sl] += dv_acc[CHUNK:]
            dv_ref[pl.ds(prv, CHUNK), sl] += dv_acc[:CHUNK]

    qv = qkv.reshape(B, L, dil * W_QKV)
    view = lambda a: a.reshape(B, L, dil * ATT_W)
    blk = pl.BlockSpec((None, CHUNK, ATT_W), lambda b, r, n: (b, n, r))
    whole = pl.BlockSpec((None, L, ATT_W), lambda b, r, n: (b, 0, r))
    sh = jax.ShapeDtypeStruct((B, L, dil * ATT_W), F32)
    dq, dk, dv = _call(
        body, name=f"attn_bwd_d{dil}", grid=(B, dil, nb),
        in_specs=[pl.BlockSpec((None, CHUNK, ATT_W), lambda b, r, n: (b, n, 3 * r)),
                  pl.BlockSpec((None, L, ATT_W), lambda b, r, n: (b, 0, 3 * r + 1)),
                  pl.BlockSpec((None, L, ATT_W), lambda b, r, n: (b, 0, 3 * r + 2)),
                  blk, blk, blk],
        out_specs=[blk, whole, whole], out_shape=[sh, sh, sh],
        sem=("parallel", "parallel", "arbitrary"))(qv, qv, qv, view(do), view(out), view(lse))
    return dq.reshape(T, ATT_W), dk.reshape(T, ATT_W), dv.reshape(T, ATT_W)


def _sum_branches(parts):
    T = parts[0][0].shape[0]
    tm = _tile(T, 512)

    def body(*refs):
        o_ref = refs[-1]
        for s in range(3):
            acc = refs[s][...] + refs[3 + s][...] + refs[6 + s][...]
            o_ref[:, ATT_W * s:ATT_W * (s + 1)] = acc.astype(BF)

    row = pl.BlockSpec((tm, ATT_W), lambda i: (i, 0))
    flat = [a for tr in parts for a in tr]
    return _call(body, name="attn_sum_branches", grid=(T // tm,), in_specs=[row] * 9,
                 out_specs=pl.BlockSpec((tm, W_QKV), lambda i: (i, 0)),
                 out_shape=jax.ShapeDtypeStruct((T, W_QKV), BF), sem=("parallel",))(*flat)


def _silu(x):
    return x * _sigmoid(x)


def _dsilu(x):
    s = _sigmoid(x)
    return s * (1.0 + x * (1.0 - s))


def _log1p(u):
    return jnp.where(u < 0.01, u * (1.0 - u * (0.5 - u * (1.0 / 3.0))), jnp.log(1.0 + u))


def _softplus(x):
    return jnp.maximum(x, 0.0) + _log1p(jnp.exp(-jnp.abs(x)))


def _cumsum_rows(x, reverse=False):
    n = x.shape[0]
    rows = lax.broadcasted_iota(jnp.int32, x.shape, 0)
    k = 1
    while k < n:
        if reverse:
            x = x + jnp.where(rows < n - k, pltpu.roll(x, n - k, 0), 0.0)
        else:
            x = x + jnp.where(rows >= k, pltpu.roll(x, k, 0), 0.0)
        k *= 2
    return x


def _tri():
    r = lax.broadcasted_iota(jnp.int32, (CHUNK, CHUNK), 0)
    c = lax.broadcasted_iota(jnp.int32, (CHUNK, CHUNK), 1)
    return r >= c


def _row_mask(e):
    return (lax.broadcasted_iota(jnp.int32, (128, 1), 0) // HEAD) == e


def _first_lane(e):
    return lax.broadcasted_iota(jnp.int32, (1, 128), 1) == HEAD * e


def _ssd_pre(x_ref, halo_ref, first, cw_ref, cb_ref, dtb_ref, al_ref, ext):
    row = x_ref[...]
    z = row[:, SSD_CONV_DIM:SSD_CONV_DIM + SSD_W]
    u = row[:, SSD_CONV_DIM + SSD_W:] + dtb_ref[...]
    ext[0:8, :] = jnp.where(first, 0.0, halo_ref[:, 0:SSD_CONV_DIM])
    ext[8:8 + CHUNK, :] = row[:, 0:SSD_CONV_DIM]
    xc = cb_ref[...]
    for j in range(4):
        xc = xc + cw_ref[j:j + 1, :] * ext[pl.ds(5 + j, CHUNK), :]
    xa = _silu(xc)
    dt = _softplus(u)
    a = dt * (-jnp.exp(al_ref[...]))
    A = _cumsum_rows(a)
    return dict(z=z, u=u, xc=xc, xs=xa[:, 0:SSD_W], Bm=xa[:, SSD_W:SSD_W + 256], Cm=xa[:, SSD_W + 256:],
                dt=dt, a=a, A=A, AT=A.T, eA=jnp.exp(A), wdec=jnp.exp(A[CHUNK - 1:CHUNK, :] - A),
                dtot=jnp.exp(A[CHUNK - 1:CHUNK, :]))


def _ssd_y(p, hp_ref, dskip):
    tri = _tri()
    X = p["xs"] * p["dt"]
    Bb = [p["Bm"][:, 128 * g:128 * (g + 1)].astype(BF) for g in range(2)]
    Cb = [p["Cm"][:, 128 * g:128 * (g + 1)].astype(BF) for g in range(2)]
    CB = [_dot_nt(Cb[g], Bb[g]) for g in range(2)]
    tiles = []
    for t in range(3):
        sl = slice(128 * t, 128 * (t + 1))
        hpb = hp_ref[sl, :].astype(BF)
        acc = jnp.zeros((CHUNK, 128), F32)
        for e in range(2):
            h = 2 * t + e
            g, col = h // 3, HEAD * h
            lm = _lane_mask(e)
            L = jnp.exp(jnp.where(tri, p["A"][:, col:col + 1] - p["AT"][col:col + 1, :], NEG))
            yd = _dot((CB[g] * L).astype(BF), jnp.where(lm, X[:, sl], 0.0).astype(BF))
            yo = _dot_nt(Cb[g], hpb) * p["eA"][:, sl]
            acc = acc + jnp.where(lm, yd + yo, 0.0)
        tiles.append(acc)
    return jnp.concatenate(tiles, axis=1) + dskip * p["xs"], X, Bb, Cb, CB


def _group_stats(v):
    g0 = lax.broadcasted_iota(jnp.int32, (1, SSD_W), 1) < SSD_W // 2
    m0 = jnp.sum(jnp.where(g0, v, 0.0), axis=-1, keepdims=True) * (2.0 / SSD_W)
    m1 = jnp.sum(jnp.where(g0, 0.0, v), axis=-1, keepdims=True) * (2.0 / SSD_W)
    return jnp.where(g0, m0, m1)


def _ssd_specs(T, rev):
    B = T // SEQ

    def chunk(b, c):
        return b * N_CHUNK + (N_CHUNK - 1 - c if rev else c)

    row = pl.BlockSpec((CHUNK, W_SSD), lambda b, c: (chunk(b, c), 0))
    halo = pl.BlockSpec((8, W_SSD), lambda b, c: (jnp.maximum(chunk(b, c) * (CHUNK // 8) - 1, 0), 0))
    hp = pl.BlockSpec((None, SSD_W, SSD_STATE), lambda b, c: (chunk(b, c), 0, 0))
    y = pl.BlockSpec((CHUNK, SSD_W), lambda b, c: (chunk(b, c), 0))
    const = lambda r, w: pl.BlockSpec((r, w), lambda b, c: (0, 0))
    params = [const(4, SSD_CONV_DIM), const(1, SSD_CONV_DIM)] + [const(1, SSD_W)] * 4
    return B, row, halo, hp, y, const, params


def _ssd_fwd(sin, conv_w, conv_b, dtb, alog, dskip, norm_g):
    T = sin.shape[0]
    B, row, halo, hp, y, const, params = _ssd_specs(T, False)

    def body(x_ref, halo_ref, cw_ref, cb_ref, dtb_ref, al_ref, dk_ref, ng_ref, y_ref, hp_ref, ext, hst):
        c = pl.program_id(1)

        @pl.when(c == 0)
        def _():
            hst[...] = jnp.zeros_like(hst)

        p = _ssd_pre(x_ref, halo_ref, c == 0, cw_ref, cb_ref, dtb_ref, al_ref, ext)
        yv, X, Bb, Cb, CB = _ssd_y(p, hst, dk_ref[...])
        hp_ref[...] = hst[...]
        for t in range(3):
            sl = slice(128 * t, 128 * (t + 1))
            old = hst[sl, :]
            new = old
            for e in range(2):
                h = 2 * t + e
                g, col = h // 3, HEAD * h
                st = _dot_tn(jnp.where(_lane_mask(e), X[:, sl] * p["wdec"][:, sl], 0.0).astype(BF), Bb[g])
                new = jnp.where(_row_mask(e), old * p["dtot"][:, col:col + 1] + st, new)
            hst[sl, :] = new
        y2 = yv * _silu(p["z"])
        r = lax.rsqrt(_group_stats(y2 * y2) + RMS_EPS)
        y_ref[...] = y2 * r * ng_ref[...]

    return _call(body, name="ssd_fwd", grid=(B, N_CHUNK), in_specs=[row, halo] + params, out_specs=[y, hp],
                 out_shape=[jax.ShapeDtypeStruct((T, SSD_W), F32),
                            jax.ShapeDtypeStruct((T // CHUNK, SSD_W, SSD_STATE), F32)],
                 scratch=[pltpu.VMEM((8 + CHUNK, SSD_CONV_DIM), F32), pltpu.VMEM((SSD_W, SSD_STATE), F32)],
                 sem=("parallel", "arbitrary"))(sin, sin, conv_w, conv_b, dtb, alog, dskip, norm_g)


def _ssd_bwd(sin, hprev, dy3, conv_w, conv_b, dtb, alog, dskip, norm_g):
    T = sin.shape[0]
    B, row, halo, hp, y, const, params = _ssd_specs(T, True)

    def body(x_ref, halo_ref, hp_ref, dy_ref, cw_ref, cb_ref, dtb_ref, al_ref, dk_ref, ng_ref,
             dx_ref, dcw_ref, dcb_ref, dvec_ref, ext, ext2, dh):
        c = pl.program_id(1)

        @pl.when((pl.program_id(0) == 0) & (c == 0))
        def _():
            dcw_ref[...] = jnp.zeros_like(dcw_ref)
            dcb_ref[...] = jnp.zeros_like(dcb_ref)
            dvec_ref[...] = jnp.zeros_like(dvec_ref)

        @pl.when(c == 0)
        def _():
            dh[...] = jnp.zeros_like(dh)
            ext2[CHUNK:CHUNK + 8, :] = jnp.zeros((8, SSD_CONV_DIM), F32)

        p = _ssd_pre(x_ref, halo_ref, c == N_CHUNK - 1, cw_ref, cb_ref, dtb_ref, al_ref, ext)
        dskip_ = dk_ref[...]
        yv, X, Bb, Cb, CB = _ssd_y(p, hp_ref, dskip_)
        xs, z, A, AT = p["xs"], p["z"], p["A"], p["AT"]

        sz = _silu(z)
        y2 = yv * sz
        r = lax.rsqrt(_group_stats(y2 * y2) + RMS_EPS)
        dy3_ = dy_ref[...]
        uu = dy3_ * ng_ref[...]
        dy2 = r * (uu - y2 * (r * r * _group_stats(uu * y2)))
        dy = dy2 * sz
        dz = dy2 * yv * _dsilu(z)

        tri = _tri()
        rows = lax.broadcasted_iota(jnp.int32, (CHUNK, 1), 0)
        dG = [jnp.zeros((CHUNK, CHUNK), F32) for _ in range(2)]
        dB = [jnp.zeros((CHUNK, SSD_STATE), F32) for _ in range(2)]
        dC = [jnp.zeros((CHUNK, SSD_STATE), F32) for _ in range(2)]
        dX_t, dA_t, ddtx_t = [], [], []
        for t in range(3):
            sl = slice(128 * t, 128 * (t + 1))
            hp_t = hp_ref[sl, :]
            hpb = hp_t.astype(BF)
            dhc = dh[sl, :]
            dh_new = jnp.zeros((128, SSD_STATE), F32)
            dX = jnp.zeros((CHUNK, 128), F32)
            dA = jnp.zeros((CHUNK, 128), F32)
            ddtx = jnp.zeros((CHUNK, 128), F32)
            for e in range(2):
                h = 2 * t + e
                g, col = h // 3, HEAD * h
                lm, rm, fl = _lane_mask(e), _row_mask(e), _first_lane(e)
                L = jnp.exp(jnp.where(tri, A[:, col:col + 1] - AT[col:col + 1, :], NEG))
                Mf = CB[g] * L
                Xm = jnp.where(lm, X[:, sl], 0.0)
                Xmb = Xm.astype(BF)
                dyh = jnp.where(lm, dy[:, sl], 0.0)
                dyb = dyh.astype(BF)
                dXh = _dot_tn(Mf.astype(BF), dyb)
                dM = jnp.where(tri, _dot_nt(dyb, Xmb), 0.0)
                Wm = dM * Mf
                dAc = jnp.sum(Wm, axis=-1, keepdims=True) - jnp.sum(Wm.T, axis=-1, keepdims=True)
                dG[g] = dG[g] + dM * L
                eAt = p["eA"][:, sl]
                yo = _dot_nt(Cb[g], hpb)
                dyo = (dyh * eAt).astype(BF)
                dC[g] = dC[g] + _dot(dyo, hpb)
                dh_new = dh_new + _dot_tn(dyo, Cb[g])
                dAc = dAc + jnp.sum(dyh * yo * eAt, axis=-1, keepdims=True)
                dHn = jnp.where(rm, dhc, 0.0)
                dHnb = dHn.astype(BF)
                dec = p["dtot"][:, col:col + 1]
                dh_new = dh_new + dec * dHn
                Z = _dot_nt(Bb[g], dHnb)
                wt = p["wdec"][:, sl]
                xi = jnp.sum(Xm * Z, axis=-1, keepdims=True) * p["wdec"][:, col:col + 1]
                dXh = dXh + wt * Z
                dB[g] = dB[g] + _dot(jnp.where(lm, X[:, sl] * wt, 0.0).astype(BF), dHnb)
                dAtot = jnp.sum(xi, axis=0, keepdims=True) + dec * jnp.sum(
                    jnp.sum(dHn * hp_t, axis=-1, keepdims=True), axis=0, keepdims=True)
                dAc = dAc - xi + jnp.where(rows == CHUNK - 1, dAtot, 0.0)
                dA = dA + jnp.where(fl, dAc, 0.0)
                dX = dX + dXh
                ddtx = ddtx + jnp.where(fl, jnp.sum(dXh * xs[:, sl], axis=-1, keepdims=True), 0.0)
            dh[sl, :] = dh_new
            dX_t.append(dX)
            dA_t.append(dA)
            ddtx_t.append(ddtx)
        for g in range(2):
            dGb = dG[g].astype(BF)
            dC[g] = dC[g] + _dot(dGb, Bb[g])
            dB[g] = dB[g] + _dot_tn(dGb, Cb[g])
        dXf = jnp.concatenate(dX_t, axis=1)
        da = _cumsum_rows(jnp.concatenate(dA_t, axis=1), reverse=True)
        ddt = da * (-jnp.exp(al_ref[...])) + jnp.concatenate(ddtx_t, axis=1)
        du = ddt * _sigmoid(p["u"])
        dxs = dXf * p["dt"] + dskip_ * dy
        dxc = jnp.concatenate([dxs, dB[0], dB[1], dC[0], dC[1]], axis=1) * _dsilu(p["xc"])
        ext2[0:CHUNK, :] = dxc
        dxbc = jnp.zeros((CHUNK, SSD_CONV_DIM), F32)
        for j in range(4):
            dxbc = dxbc + cw_ref[j:j + 1, :] * ext2[pl.ds(3 - j, CHUNK), :]
            dcw_ref[j:j + 1, :] += jnp.sum(dxc * ext[pl.ds(5 + j, CHUNK), :], axis=0, keepdims=True)
        ext2[CHUNK:CHUNK + 8, :] = dxc[0:8, :]
        dcb_ref[...] += jnp.sum(dxc, axis=0, keepdims=True)
        dvec_ref[0:1, :] += jnp.sum(du, axis=0, keepdims=True)
        dvec_ref[1:2, :] += jnp.sum(da * p["a"], axis=0, keepdims=True)
        dvec_ref[2:3, :] += jnp.sum(dy * xs, axis=0, keepdims=True)
        dvec_ref[3:4, :] += jnp.sum(dy3_ * y2 * r, axis=0, keepdims=True)
        dx_ref[...] = jnp.concatenate([dxbc, dz, du], axis=1).astype(BF)

    return _call(body, name="ssd_bwd", grid=(B, N_CHUNK), in_specs=[row, halo, hp, y] + params,
                 out_specs=[row, const(4, SSD_CONV_DIM), const(1, SSD_CONV_DIM), const(8, SSD_W)],
                 out_shape=[jax.ShapeDtypeStruct((T, W_SSD), BF), jax.ShapeDtypeStruct((4, SSD_CONV_DIM), F32),
                            jax.ShapeDtypeStruct((1, SSD_CONV_DIM), F32), jax.ShapeDtypeStruct((8, SSD_W), F32)],
                 scratch=[pltpu.VMEM((8 + CHUNK, SSD_CONV_DIM), F32), pltpu.VMEM((8 + CHUNK, SSD_CONV_DIM), F32),
                          pltpu.VMEM((SSD_W, SSD_STATE), F32)],
                 sem=("arbitrary", "arbitrary"))(sin, sin, hprev, dy3, conv_w, conv_b, dtb, alog, dskip, norm_g)


def _sgu_core(uv_ref, g_ref, b_ref, w_ref, bias_ref):
    x = uv_ref[...]
    cdf = 0.5 * (1.0 + lax.erf(x * (2.0 ** -0.5)))
    ge = x * cdf
    dge = cdf + x * jnp.exp(-0.5 * x * x) * ((2.0 * math.pi) ** -0.5)
    u, v = ge[:, 0:SGU_W], ge[:, SGU_W:]
    vc = v - jnp.mean(v, axis=-1, keepdims=True)
    rstd = lax.rsqrt(jnp.mean(vc * vc, axis=-1, keepdims=True) + LN_EPS)
    vhat = vc * rstd
    vn = vhat * g_ref[...] + b_ref[...]
    tri = _tri()
    wc = [jnp.where(tri, w_ref[gi], 0.0).astype(BF) for gi in range(4)]
    vm = [jnp.where(_lane_mask(gi % 2), vn[:, 128 * (gi // 2):128 * (gi // 2 + 1)], 0.0).astype(BF) for gi in range(4)]
    mixed = jnp.concatenate([_dot(wc[2 * t], vm[2 * t]) + _dot(wc[2 * t + 1], vm[2 * t + 1]) for t in range(2)],
                            axis=1) + bias_ref[...]
    return dict(dge=dge, u=u, rstd=rstd, vhat=vhat, wc=wc, vm=vm, mixed=mixed)


def _sgu_specs():
    vec = pl.BlockSpec((1, SGU_W), lambda i: (0, 0))
    return [pl.BlockSpec((CHUNK, W_UV), lambda i: (i, 0)), vec, vec,
            pl.BlockSpec((4, CHUNK, CHUNK), lambda i: (0, 0, 0)), pl.BlockSpec((CHUNK, SGU_W), lambda i: (0, 0))]


def _sgu_fwd(uv, ln_g, ln_b, w, bias):
    T = uv.shape[0]

    def body(uv_ref, g_ref, b_ref, w_ref, bias_ref, y_ref):
        s = _sgu_core(uv_ref, g_ref, b_ref, w_ref, bias_ref)
        y_ref[...] = s["u"] * s["mixed"]

    return _call(body, name="sgu_fwd", grid=(T // CHUNK,), in_specs=_sgu_specs(),
                 out_specs=pl.BlockSpec((CHUNK, SGU_W), lambda i: (i, 0)),
                 out_shape=jax.ShapeDtypeStruct((T, SGU_W), F32), sem=("parallel",))(uv, ln_g, ln_b, w, bias)


def _sgu_bwd(uv, dy, ln_g, ln_b, w, bias):
    T = uv.shape[0]

    def body(uv_ref, dy_ref, g_ref, b_ref, w_ref, bias_ref, dx_ref, dw_ref, dbias_ref, dln_ref):
        @pl.when(pl.program_id(0) == 0)
        def _():
            dw_ref[...] = jnp.zeros_like(dw_ref)
            dbias_ref[...] = jnp.zeros_like(dbias_ref)
            dln_ref[...] = jnp.zeros_like(dln_ref)

        s = _sgu_core(uv_ref, g_ref, b_ref, w_ref, bias_ref)
        dy_ = dy_ref[...]
        du = dy_ * s["mixed"]
        dmix = dy_ * s["u"]
        dbias_ref[...] += dmix
        tri = _tri()
        dvn_t = []
        for t in range(2):
            acc = jnp.zeros((CHUNK, 128), F32)
            for e in range(2):
                gi = 2 * t + e
                dmg = jnp.where(_lane_mask(e), dmix[:, 128 * t:128 * (t + 1)], 0.0).astype(BF)
                acc = acc + _dot_tn(s["wc"][gi], dmg)
                dw_ref[gi] += jnp.where(tri, _dot_nt(dmg, s["vm"][gi]), 0.0)
            dvn_t.append(acc)
        dvn = jnp.concatenate(dvn_t, axis=1)
        dln_ref[0:1, :] += jnp.sum(dvn * s["vhat"], axis=0, keepdims=True)
        dln_ref[1:2, :] += jnp.sum(dvn, axis=0, keepdims=True)
        dvh = dvn * g_ref[...]
        dv = s["rstd"] * (dvh - jnp.mean(dvh, axis=-1, keepdims=True)
                          - s["vhat"] * jnp.mean(dvh * s["vhat"], axis=-1, keepdims=True))
        dx_ref[...] = (jnp.concatenate([du, dv], axis=1) * s["dge"]).astype(BF)

    ins = _sgu_specs()
    return _call(body, name="sgu_bwd", grid=(T // CHUNK,),
                 in_specs=[ins[0], pl.BlockSpec((CHUNK, SGU_W), lambda i: (i, 0))] + ins[1:],
                 out_specs=[pl.BlockSpec((CHUNK, W_UV), lambda i: (i, 0)),
                            pl.BlockSpec((4, CHUNK, CHUNK), lambda i: (0, 0, 0)),
                            pl.BlockSpec((CHUNK, SGU_W), lambda i: (0, 0)), pl.BlockSpec((8, SGU_W), lambda i: (0, 0))],
                 out_shape=[jax.ShapeDtypeStruct((T, W_UV), BF), jax.ShapeDtypeStruct((4, CHUNK, CHUNK), F32),
                            jax.ShapeDtypeStruct((CHUNK, SGU_W), F32), jax.ShapeDtypeStruct((8, SGU_W), F32)],
                 sem=("arbitrary",))(uv, dy, ln_g, ln_b, w, bias)


def _adamw(w, g, m, v):
    R, C = w.shape
    tr = _tile(R, 256) if R % 8 == 0 else R

    def body(w_ref, g_ref, m_ref, v_ref, d_ref, nm_ref, nv_ref):
        g_ = g_ref[...]
        m2 = ADAM_B1 * m_ref[...] + (1.0 - ADAM_B1) * g_
        v2 = ADAM_B2 * v_ref[...] + (1.0 - ADAM_B2) * (g_ * g_)
        m_hat = m2 / (1.0 - ADAM_B1 ** ADAM_STEP)
        v_hat = v2 / (1.0 - ADAM_B2 ** ADAM_STEP)
        d_ref[...] = -ADAM_LR * (m_hat / (jnp.sqrt(v_hat) + ADAM_EPS) + ADAM_WD * w_ref[...])
        nm_ref[...] = m2
        nv_ref[...] = v2

    blk = pl.BlockSpec((tr, C), lambda i: (i, 0))
    sh = jax.ShapeDtypeStruct((R, C), F32)
    return _call(body, name="adamw", grid=(R // tr,), in_specs=[blk] * 4, out_specs=[blk] * 3,
                 out_shape=[sh] * 3, sem=("parallel",))(w, g, m, v)


def _row_steps(rows):
    return 2 if rows % 32 == 0 else 1


def _pair_add(gbuf, rsib, c):
    NS, _, R, C = gbuf.shape
    n = _row_steps(R)
    tr = R // n

    def body(c_ref, a_ref, b_ref, o_ref):
        o_ref[...] = (a_ref[...] + b_ref[...]).astype(BF)

    blk = pl.BlockSpec((None, tr, C), lambda j, i, c_ref: (j, i, 0))
    return pl.pallas_call(
        body, name="rs_pair_add",
        grid_spec=pltpu.PrefetchScalarGridSpec(
            num_scalar_prefetch=1, grid=(NS, n),
            in_specs=[pl.BlockSpec((None, None, tr, C), lambda j, i, c_ref: (j, c_ref[0], i, 0)), blk],
            out_specs=blk),
        out_shape=jax.ShapeDtypeStruct((NS, R, C), BF),
        compiler_params=pltpu.CompilerParams(dimension_semantics=("parallel", "parallel")),
    )(jnp.reshape(c, (1,)).astype(jnp.int32), gbuf, rsib)


def _chip_sum(pair, recv, me, c):
    NS, R, C = pair.shape
    n = _row_steps(R)
    tr = R // n

    def body(s_ref, own_ref, p_ref, o_ref):
        p = [jnp.where(s_ref[0] == j, own_ref[...], p_ref[j]).astype(F32) for j in range(4)]
        o_ref[...] = ((p[0] + p[1]) + p[2]) + p[3]

    return pl.pallas_call(
        body, name="rs_chip_sum",
        grid_spec=pltpu.PrefetchScalarGridSpec(
            num_scalar_prefetch=1, grid=(n,),
            in_specs=[pl.BlockSpec((None, tr, C), lambda i, s: (s[0], i, 0)),
                      pl.BlockSpec((NS, tr, C), lambda i, s: (0, i, 0))],
            out_specs=pl.BlockSpec((None, tr, C), lambda i, s: (s[1], i, 0))),
        out_shape=jax.ShapeDtypeStruct((2, R, C), F32),
        compiler_params=pltpu.CompilerParams(dimension_semantics=("parallel",)),
    )(jnp.stack([me, c]).astype(jnp.int32), pair, recv)


MESH = pl.DeviceIdType.MESH
ANY = pl.BlockSpec(memory_space=pl.ANY)


def _place():
    x, y, c = lax.axis_index("x"), lax.axis_index("y"), lax.axis_index("c")
    return x, y, c, [(1 - x, y), (x, 1 - y), (1 - x, 1 - y)]


HBM = pl.BlockSpec(memory_space=pltpu.HBM)
SEM = pl.BlockSpec(memory_space=pltpu.SEMAPHORE)
EFFECT = pltpu.SideEffectType.DATAFLOW_SIDE_EFFECTING


class _Split:
    def __init__(self, tag, arrays, copies, n_copies):
        self.tag, self.copies, k = tag, copies, len(arrays)

        def body(*refs):
            for cp in copies(refs[:k], refs[k], refs[k + 1]):
                cp.start()
            refs[-1][...] = jnp.zeros_like(refs[-1])

        out = pl.pallas_call(
            body, name=tag + "_start",
            out_shape=(pltpu.SemaphoreType.DMA((n_copies,)), pltpu.SemaphoreType.DMA((n_copies,)),
                       *[pltpu.HBM(a.shape, a.dtype) for a in arrays], jax.ShapeDtypeStruct((8, 128), F32)),
            in_specs=[HBM] * k, out_specs=(SEM, SEM, *[HBM] * k, pl.BlockSpec(memory_space=pltpu.VMEM)),
            input_output_aliases={i: 2 + i for i in range(k)},
            compiler_params=pltpu.CompilerParams(has_side_effects=EFFECT),
        )(*[pltpu.with_memory_space_constraint(a, pltpu.HBM) for a in arrays])
        self.send, self.recv, self.arrays, self.token = out[0], out[1], list(out[2:2 + k]), out[-1][0, 0]

    def wait(self, after):
        k, copies = len(self.arrays), self.copies

        def body(*refs):
            for cp in copies(refs[:k], refs[k], refs[k + 1]):
                cp.wait_send()
                cp.wait_recv()

        return list(pl.pallas_call(
            body, name=self.tag + "_wait", out_shape=tuple(pltpu.HBM(a.shape, a.dtype) for a in self.arrays),
            in_specs=[HBM] * k + [SEM, SEM, ANY], out_specs=tuple([HBM] * k),
            input_output_aliases={i: i for i in range(k)},
            compiler_params=pltpu.CompilerParams(has_side_effects=EFFECT),
        )(*self.arrays, self.send, self.recv, after))


def _gather_start(arrs, tag):
    n = len(arrs)
    me = 2 * lax.axis_index("x") + lax.axis_index("y")
    lands = [lax.dynamic_update_index_in_dim(lax.empty((4,) + a.shape, a.dtype), a, me, 0) for a in arrs]

    def copies(refs, send, recv):
        x, y, c, chips = _place()
        return [pltpu.make_async_remote_copy(
            src_ref=refs[k], dst_ref=refs[n + k].at[2 * x + y], send_sem=send.at[3 * k + r],
            recv_sem=recv.at[3 * k + r], device_id=(px, py, c), device_id_type=MESH)
            for k in range(n) for r, (px, py) in enumerate(chips)]

    return _Split("gather_" + tag, list(arrs) + lands, copies, 3 * n)


def _to_sibling_start(gbufs, tag):
    n = len(gbufs)

    def copies(refs, send, recv):
        x, y, c, _ = _place()
        return [pltpu.make_async_remote_copy(
            src_ref=refs[k].at[j, 1 - c], dst_ref=refs[n + k].at[j], send_sem=send.at[4 * k + j],
            recv_sem=recv.at[4 * k + j], device_id=(x, y, 1 - c), device_id_type=MESH)
            for k in range(n) for j in range(4)]

    lands = [lax.empty((4,) + g.shape[2:], g.dtype) for g in gbufs]
    return _Split("rs_sibling_" + tag, list(gbufs) + lands, copies, 4 * n)


def _to_chips_start(pbufs, tag):
    n = len(pbufs)

    def copies(refs, send, recv):
        x, y, c, chips = _place()
        return [pltpu.make_async_remote_copy(
            src_ref=refs[k].at[2 * px + py], dst_ref=refs[n + k].at[2 * x + y], send_sem=send.at[3 * k + r],
            recv_sem=recv.at[3 * k + r], device_id=(px, py, c), device_id_type=MESH)
            for k in range(n) for r, (px, py) in enumerate(chips)]

    return _Split("rs_chips_" + tag, list(pbufs) + [lax.empty(p.shape, p.dtype) for p in pbufs], copies, 3 * n)


def _join_start(fulls, tag):
    def copies(refs, send, recv):
        x, y, c, _ = _place()
        return [pltpu.make_async_remote_copy(
            src_ref=refs[k].at[c], dst_ref=refs[k].at[c], send_sem=send.at[k], recv_sem=recv.at[k],
            device_id=(x, y, 1 - c), device_id_type=MESH) for k in range(len(fulls))]

    return _Split("rs_join_" + tag, list(fulls), copies, len(fulls))


def _all_reduce_small(v):
    R, C = v.shape

    def body(v_ref, o_ref, g_ref, send, recv, loc):
        x, y, c, chips = _place()
        me, sibling = (x, y, c), (x, y, 1 - c)

        def rows(px, py, pc):
            return g_ref.at[4 * px + 2 * py + pc]

        def copy(k, block, to, src=None):
            return pltpu.make_async_remote_copy(
                src_ref=rows(*block) if src is None else src, dst_ref=rows(*block),
                send_sem=send.at[k], recv_sem=recv.at[k], device_id=to, device_id_type=MESH)

        mine = pltpu.make_async_copy(v_ref, rows(*me), loc)
        mine.start()
        first = [copy(0, me, sibling, src=v_ref)]
        first += [copy(1 + j, me, (*chip, c), src=v_ref) for j, chip in enumerate(chips)]
        for cp in first:
            cp.start()
        passed = [copy(4 + j, (*chip, c), sibling) for j, chip in enumerate(chips)]
        for j, chip in enumerate(chips):
            copy(1 + j, (*chip, c), me).wait_recv()
            passed[j].start()
        copy(0, sibling, me).wait_recv()
        for j, chip in enumerate(chips):
            copy(4 + j, (*chip, 1 - c), me).wait_recv()
        for cp in first + passed:
            cp.wait_send()
        mine.wait()
        acc = g_ref[0]
        for d in range(1, 8):
            acc = acc + g_ref[d]
        o_ref[...] = acc

    vm = pl.BlockSpec(memory_space=pltpu.VMEM)
    return pl.pallas_call(
        body, name="all_reduce_small", in_specs=[vm], out_specs=[vm, vm],
        out_shape=[jax.ShapeDtypeStruct((R, C), F32), jax.ShapeDtypeStruct((8, R, C), F32)],
        scratch_shapes=[pltpu.SemaphoreType.DMA((7,)), pltpu.SemaphoreType.DMA((7,)), pltpu.SemaphoreType.DMA],
    )(v)[0]


WEIGHTS = ['ffn1_norm', 'ffn1_w_gate', 'ffn1_w_up', 'ffn1_w_down', 'mix_norm', 'w_in', 'conv_w', 'conv_b', 'dt_bias',
           'a_log', 'd_skip', 'ssd_norm', 'sgu_ln_g', 'sgu_ln_b', 'sgu_w', 'sgu_b', 'w_out', 'ffn2_norm',
           'ffn2_w_gate', 'ffn2_w_up', 'ffn2_w_down', 'final_norm']
SHARDED = ['ffn1_w_gate', 'ffn1_w_up', 'ffn1_w_down', 'w_in', 'conv_w', 'w_out', 'ffn2_w_gate', 'ffn2_w_up',
           'ffn2_w_down']
SMALL = [n for n in WEIGHTS if n not in SHARDED]
GROUPS = [("ffn1", ["ffn1_w_gate", "ffn1_w_up", "ffn1_w_down"]), ("mix", ["w_in", "conv_w", "w_out"]),
          ("ffn2", ["ffn2_w_gate", "ffn2_w_up", "ffn2_w_down"])]
DEPTH = 2


def _pack_w_in(w):
    return jnp.concatenate([w[..., 0:1152], w[..., 1536:2432], w[..., 1152:1536],
                            jnp.repeat(w[..., 2432:2438], HEAD, axis=-1), w[..., 2438:2950]], axis=-1)


def _unpack_w_in(dq, ds, du):
    return jnp.concatenate([dq, ds[:, 896:1280], ds[:, 0:896], ds[:, 1280::HEAD], du], axis=-1)


def _ffn_fwd(x, g, wg, wu, wd):
    xo, hb, G, U = _ffn_fwd_k(x, g, wg, wu, wd)
    return xo, (x, hb, G, U)


def _ffn_bwd(dxo, saved, g, wg, wu, wd):
    x, hb, G, U = saved
    dx, dg, dG, dU, dyb = _ffn_bwd_k1(dxo, x, g, G, U, wg, wu, wd)
    dwg, dwu, dwd = _ffn_bwd_k2(hb, dyb, G, U, dG, dU)
    return dx, dg, dwg, dwu, dwd


def _mix_fwd(x, P):
    hb = _rms_fwd(x, P["mix_norm"])
    qkv = _mm_nn(hb, P["w_qkv"], out_dtype=BF)
    sin = _mm_nn(hb, P["w_ssd"])
    uv = _mm_nn(hb, P["w_uv"])
    o1, l1 = _attn_fwd(qkv, 1)
    o2, l2 = _attn_fwd(qkv, 4)
    o3, l3 = _attn_fwd(qkv, 16)
    y_att, lse = _attn_combine(o1, o2, o3, l1, l2, l3)
    y_ssd, hprev = _ssd_fwd(sin, *P["ssd"])
    y_sgu = _sgu_fwd(uv, *P["sgu"])
    ycat = jnp.concatenate([y_att, y_ssd, y_sgu], axis=1).astype(BF)
    return _mm_nn(ycat, P["w_out"], res=x), (x, hb, qkv, sin, uv, y_att, lse, hprev, ycat)


def _mix_bwd(dxo, saved, P):
    x, hb, qkv, sin, uv, y_att, lse, hprev, ycat = saved
    dycat = _mm_nt(dxo, P["w_out"])
    dwout = _mm_tn(ycat, dxo)
    dy_att, dy_ssd, dy_sgu = dycat[:, 0:ATT_W], dycat[:, ATT_W:ATT_W + SSD_W], dycat[:, ATT_W + SSD_W:]
    dqkv = _sum_branches([_attn_bwd(qkv, dy_att, y_att, lse, d) for d in DILATIONS])
    dsin, dcw, dcb, dvec = _ssd_bwd(sin, hprev, dy_ssd, *P["ssd"])
    duv, dsw, dsbias, dln = _sgu_bwd(uv, dy_sgu, *P["sgu"])
    dwin = _unpack_w_in(_mm_tn(hb, dqkv), _mm_tn(hb, dsin), _mm_tn(hb, duv))
    dh = _mm_nt(dqkv, P["w_qkv"])
    dh = _mm_nt(dsin, P["w_ssd"], res=dh)
    dh = _mm_nt(duv, P["w_uv"], res=dh)
    dx, dg = _rms_bwd(x, P["mix_norm"], dh, dxo)
    grads = dict(
        mix_norm=dg[0], w_in=dwin, conv_w=dcw, conv_b=dcb[0], dt_bias=dvec[0, ::HEAD], a_log=dvec[1, ::HEAD],
        d_skip=jnp.sum(dvec[2].reshape(6, HEAD), axis=-1), ssd_norm=dvec[3], sgu_ln_g=dln[0], sgu_ln_b=dln[1],
        sgu_w=dsw, sgu_b=jnp.sum(dsbias.reshape(CHUNK, 4, HEAD), axis=-1).T, w_out=dwout)
    return dx, grads


def _halved(g):
    rows = g.size // g.shape[-1]
    return g.reshape(4, 2, rows // 8, g.shape[-1])


def kernel(x, ffn1_norm, ffn1_w_gate, ffn1_w_up, ffn1_w_down, mix_norm, w_in, conv_w, conv_b, dt_bias, a_log, d_skip, ssd_norm, sgu_ln_g, sgu_ln_b, sgu_w, sgu_b, w_out, ffn2_norm, ffn2_w_gate, ffn2_w_up, ffn2_w_down, final_norm, loss_target, m_ffn1_norm, m_ffn1_w_gate, m_ffn1_w_up, m_ffn1_w_down, m_mix_norm, m_w_in, m_conv_w, m_conv_b, m_dt_bias, m_a_log, m_d_skip, m_ssd_norm, m_sgu_ln_g, m_sgu_ln_b, m_sgu_w, m_sgu_b, m_w_out, m_ffn2_norm, m_ffn2_w_gate, m_ffn2_w_up, m_ffn2_w_down, m_final_norm, v_ffn1_norm, v_ffn1_w_gate, v_ffn1_w_up, v_ffn1_w_down, v_mix_norm, v_w_in, v_conv_w, v_conv_b, v_dt_bias, v_a_log, v_d_skip, v_ssd_norm, v_sgu_ln_g, v_sgu_ln_b, v_sgu_w, v_sgu_b, v_w_out, v_ffn2_norm, v_ffn2_w_gate, v_ffn2_w_up, v_ffn2_w_down, v_final_norm):
    given = dict(x=x, ffn1_norm=ffn1_norm, ffn1_w_gate=ffn1_w_gate, ffn1_w_up=ffn1_w_up, ffn1_w_down=ffn1_w_down, mix_norm=mix_norm, w_in=w_in, conv_w=conv_w, conv_b=conv_b, dt_bias=dt_bias, a_log=a_log, d_skip=d_skip, ssd_norm=ssd_norm, sgu_ln_g=sgu_ln_g, sgu_ln_b=sgu_ln_b, sgu_w=sgu_w, sgu_b=sgu_b, w_out=w_out, ffn2_norm=ffn2_norm, ffn2_w_gate=ffn2_w_gate, ffn2_w_up=ffn2_w_up, ffn2_w_down=ffn2_w_down, final_norm=final_norm, loss_target=loss_target, m_ffn1_norm=m_ffn1_norm, m_ffn1_w_gate=m_ffn1_w_gate, m_ffn1_w_up=m_ffn1_w_up, m_ffn1_w_down=m_ffn1_w_down, m_mix_norm=m_mix_norm, m_w_in=m_w_in, m_conv_w=m_conv_w, m_conv_b=m_conv_b, m_dt_bias=m_dt_bias, m_a_log=m_a_log, m_d_skip=m_d_skip, m_ssd_norm=m_ssd_norm, m_sgu_ln_g=m_sgu_ln_g, m_sgu_ln_b=m_sgu_ln_b, m_sgu_w=m_sgu_w, m_sgu_b=m_sgu_b, m_w_out=m_w_out, m_ffn2_norm=m_ffn2_norm, m_ffn2_w_gate=m_ffn2_w_gate, m_ffn2_w_up=m_ffn2_w_up, m_ffn2_w_down=m_ffn2_w_down, m_final_norm=m_final_norm, v_ffn1_norm=v_ffn1_norm, v_ffn1_w_gate=v_ffn1_w_gate, v_ffn1_w_up=v_ffn1_w_up, v_ffn1_w_down=v_ffn1_w_down, v_mix_norm=v_mix_norm, v_w_in=v_w_in, v_conv_w=v_conv_w, v_conv_b=v_conv_b, v_dt_bias=v_dt_bias, v_a_log=v_a_log, v_d_skip=v_d_skip, v_ssd_norm=v_ssd_norm, v_sgu_ln_g=v_sgu_ln_g, v_sgu_ln_b=v_sgu_ln_b, v_sgu_w=v_sgu_w, v_sgu_b=v_sgu_b, v_w_out=v_w_out, v_ffn2_norm=v_ffn2_norm, v_ffn2_w_gate=v_ffn2_w_gate, v_ffn2_w_up=v_ffn2_w_up, v_ffn2_w_down=v_ffn2_w_down, v_final_norm=v_final_norm)
    T = given["x"].shape[0] * given["x"].shape[1]
    D = given["x"].shape[2]
    x0 = given["x"].reshape(T, D)
    tgt = given["loss_target"].reshape(T, D)
    c = lax.axis_index("c")

    bf = {n: given[n].astype(BF) for n in SHARDED if n not in ("w_in", "conv_w")}
    bf["w_in"] = _pack_w_in(given["w_in"]).astype(BF)
    bf["conv_w"] = given["conv_w"]
    gathers = {(i, gname): _gather_start([bf[n][i] for n in names], f"l{i}_{gname}")
               for i in range(DEPTH) for gname, names in GROUPS}
    token = functools.reduce(lambda a, b: a + b, [g.token for g in gathers.values()])

    def gathered(i, gname, after):
        return gathers[(i, gname)].wait(after)[3:]

    def mix_params(i, got):
        win = got[0].reshape(D, W_QKV + W_SSD + W_UV)
        rep = lambda v: jnp.repeat(v, HEAD)[None]
        ssd = (got[1].transpose(1, 0, 2).reshape(4, SSD_CONV_DIM), given["conv_b"][i][None],
               rep(given["dt_bias"][i]), rep(given["a_log"][i]), rep(given["d_skip"][i]), given["ssd_norm"][i][None])
        sgu = (given["sgu_ln_g"][i][None], given["sgu_ln_b"][i][None], given["sgu_w"][i],
               jnp.repeat(given["sgu_b"][i].T, HEAD, axis=1))
        return dict(mix_norm=given["mix_norm"][i][None], w_qkv=win[:, 0:W_QKV], w_ssd=win[:, W_QKV:W_QKV + W_SSD],
                    w_uv=win[:, W_QKV + W_SSD:], w_out=got[2].reshape(-1, D), ssd=ssd, sgu=sgu)

    x = x0
    tape = []
    for i in range(DEPTH):
        P = dict(ffn1=(given["ffn1_norm"][i][None] + (token if i == 0 else 0.0), *gathered(i, "ffn1", x)))
        x, s1 = _ffn_fwd(x, *P["ffn1"])
        P.update(mix_params(i, gathered(i, "mix", x)))
        x, s2 = _mix_fwd(x, P)
        P["ffn2"] = (given["ffn2_norm"][i][None], *gathered(i, "ffn2", x))
        x, s3 = _ffn_fwd(x, *P["ffn2"])
        tape.append((P, s1, s2, s3))
    loss_part, dx, dgf = _final_loss(x, given["final_norm"][None], tgt)

    me = 2 * lax.axis_index("x") + lax.axis_index("y")
    jobs = []

    def rs_begin(i, gname, gd):
        tag = f"l{i}_{gname}"
        names = [n for n in dict(GROUPS)[gname] if n != "conv_w"]
        jobs.append(dict(key=(i, gname), names=names, tag=tag, stage=1,
                         op=_to_sibling_start([_halved(gd[n]) for n in names], tag)))

    def rs_advance(job, after):
        k = len(job["names"])
        if job["stage"] == 1:
            got = job["op"].wait(after)
            job.update(stage=2, op=_to_chips_start([_pair_add(g, l, c) for g, l in zip(got[:k], got[k:])], job["tag"]))
        elif job["stage"] == 2:
            got = job["op"].wait(after)
            job.update(stage=3, op=_join_start([_chip_sum(p, l, me, c) for p, l in zip(got[:k], got[k:])], job["tag"]))
        elif job["stage"] == 3:
            job.update(stage=4, out=dict(zip(job["names"], job["op"].wait(after))))

    def tick(after, begin=None):
        for job in jobs:
            rs_advance(job, after)
        if begin is not None:
            rs_begin(*begin)
        return functools.reduce(lambda a, b: a + b, [j["op"].token for j in jobs if j["stage"] < 4], 0.0)

    grads = [dict() for _ in range(DEPTH)]
    tok = 0.0
    for i in reversed(range(DEPTH)):
        P, s1, s2, s3 = tape[i]
        g = grads[i]
        norm, wg, wu, wd = P["ffn2"]
        dx, dn2, g["ffn2_w_gate"], g["ffn2_w_up"], g["ffn2_w_down"] = _ffn_bwd(dx, s3, norm + tok, wg, wu, wd)
        tok = tick(dx, (i, "ffn2", g))
        dx, gm = _mix_bwd(dx, s2, {**P, "mix_norm": P["mix_norm"] + tok})
        g.update(gm)
        tok = tick(dx, (i, "mix", g))
        norm, wg, wu, wd = P["ffn1"]
        dx, dn1, g["ffn1_w_gate"], g["ffn1_w_up"], g["ffn1_w_down"] = _ffn_bwd(dx, s1, norm + tok, wg, wu, wd)
        tok = tick(dx, (i, "ffn1", g))
        g["ffn1_norm"], g["ffn2_norm"] = dn1[0], dn2[0]
    while any(j["stage"] < 4 for j in jobs):
        tick(dx)
    grad_x = dx.reshape(given["x"].shape)

    order = [n for n in SMALL if n != "final_norm"] + ["final_norm"]
    small = [jnp.stack([grads[i][n] for i in range(DEPTH)]) for n in order[:-1] + ["conv_w"]]
    small = small[:-1] + [dgf[0], small[-1], loss_part[0, 0:1]]
    n_small = sum(s.size for s in small)
    rows_small = -(-n_small // (128 * 8)) * 8

    def flat(arrs):
        fill = rows_small * 128 - sum(a.size for a in arrs)
        return jnp.concatenate([a.reshape(-1) for a in arrs] + [jnp.zeros((fill,), F32)]).reshape(rows_small, 128)

    gsmall = _all_reduce_small(flat(small)).reshape(-1)

    grad_w = {}
    for job in jobs:
        for n, full in job["out"].items():
            grad_w.setdefault(n, [None] * DEPTH)[job["key"][0]] = full.reshape(given[n].shape[1:])
    grad_w = {n: jnp.stack(v) for n, v in grad_w.items()}
    off = 0
    for n in order:
        size = given[n].size
        grad_w[n] = gsmall[off:off + size].reshape(given[n].shape)
        off += size
    cw = gsmall[off:off + 2 * 4 * SSD_CONV_DIM].reshape(DEPTH, 4, SSD_CONV_DIM)
    grad_w["conv_w"] = lax.dynamic_slice_in_dim(cw, me * (SSD_CONV_DIM // 4), SSD_CONV_DIM // 4, axis=2)
    loss = gsmall[off + 2 * 4 * SSD_CONV_DIM]

    delta, new_m, new_v = {}, {}, {}
    for n in SHARDED:
        shp = given[n].shape
        two_d = (shp[0] * shp[1], shp[2])
        d, m2, v2 = _adamw(*[a.reshape(two_d) for a in (given[n], grad_w[n], given["m_" + n], given["v_" + n])])
        delta[n], new_m[n], new_v[n] = d.reshape(shp), m2.reshape(shp), v2.reshape(shp)
    packed = [flat([src[pre + n] for n in order])
              for src, pre in ((given, ""), (grad_w, ""), (given, "m_"), (given, "v_"))]
    outs = [o.reshape(-1) for o in _adamw(*packed)]
    off = 0
    for n in order:
        size = given[n].size
        for dst, o in zip((delta, new_m, new_v), outs):
            dst[n] = o[off:off + size].reshape(given[n].shape)
        off += size

    return (loss, grad_x, *[grad_w[n] for n in WEIGHTS], *[delta[n] for n in WEIGHTS],
            *[new_m[n] for n in WEIGHTS], *[new_v[n] for n in WEIGHTS])
```

```python
import functools
import math

import jax
import jax.numpy as jnp
from jax import lax
from jax.experimental import pallas as pl
from jax.experimental.pallas import tpu as pltpu

F32 = jnp.float32
BF = jnp.bfloat16

RMS_EPS = 1e-6
LN_EPS = 1e-5
SEQ = 2048
CHUNK = 128
N_CHUNK = SEQ // CHUNK
ATT_W = 384
HEAD = 64
SSD_W = 384
SSD_CONV_DIM = 896
SSD_STATE = 128
SGU_W = 256
DILATIONS = (1, 4, 16)
W_QKV = 3 * ATT_W
W_SSD = SSD_CONV_DIM + SSD_W + SSD_W
W_UV = 2 * SGU_W
ADAM_LR = 0.001
ADAM_B1 = 0.9
ADAM_B2 = 0.999
ADAM_EPS = 1e-08
ADAM_WD = 0.01
ADAM_STEP = 10
NEG = -1e30


def _dot(a, b):
    return jnp.dot(a, b, preferred_element_type=F32)


def _dot_nt(a, b):
    return lax.dot_general(a, b, (((1,), (1,)), ((), ())), preferred_element_type=F32)


def _dot_tn(a, b):
    return lax.dot_general(a, b, (((0,), (0,)), ((), ())), preferred_element_type=F32)


def _sigmoid(x):
    return 1.0 / (1.0 + jnp.exp(-x))


def _call(body, *, name, grid, in_specs, out_specs, out_shape, scratch=(), sem=None):
    return pl.pallas_call(
        body, name=name, grid=grid, in_specs=in_specs, out_specs=out_specs, out_shape=out_shape,
        scratch_shapes=list(scratch),
        compiler_params=pltpu.CompilerParams(dimension_semantics=sem),
    )


def _tile(n, want):
    t = min(n, want)
    while n % t:
        t //= 2
    return t


def _rms_fwd(x, g):
    T, D = x.shape
    tm = _tile(T, 512)

    def body(x_ref, g_ref, h_ref):
        xf = x_ref[...]
        r = lax.rsqrt(jnp.mean(xf * xf, axis=-1, keepdims=True) + RMS_EPS)
        h_ref[...] = (xf * r * g_ref[...]).astype(BF)

    return _call(body, name="rms_fwd", grid=(T // tm,),
                 in_specs=[pl.BlockSpec((tm, D), lambda i: (i, 0)), pl.BlockSpec((1, D), lambda i: (0, 0))],
                 out_specs=pl.BlockSpec((tm, D), lambda i: (i, 0)),
                 out_shape=jax.ShapeDtypeStruct((T, D), BF), sem=("parallel",))(x, g)


def _rms_bwd(x, g, dh, dres):
    T, D = x.shape
    tm = _tile(T, 512)

    def body(x_ref, g_ref, dh_ref, dr_ref, dx_ref, dg_ref):
        @pl.when(pl.program_id(0) == 0)
        def _():
            dg_ref[...] = jnp.zeros_like(dg_ref)

        xf = x_ref[...]
        r = lax.rsqrt(jnp.mean(xf * xf, axis=-1, keepdims=True) + RMS_EPS)
        dh_ = dh_ref[...]
        u = dh_ * g_ref[...]
        mu = jnp.mean(u * xf, axis=-1, keepdims=True)
        dx_ref[...] = dr_ref[...] + r * (u - xf * (r * r * mu))
        dg_ref[...] += jnp.sum(dh_ * xf * r, axis=0, keepdims=True)

    row = pl.BlockSpec((tm, D), lambda i: (i, 0))
    vec = pl.BlockSpec((1, D), lambda i: (0, 0))
    return _call(body, name="rms_bwd", grid=(T // tm,), in_specs=[row, vec, row, row], out_specs=[row, vec],
                 out_shape=[jax.ShapeDtypeStruct((T, D), F32), jax.ShapeDtypeStruct((1, D), F32)],
                 sem=("arbitrary",))(x, g, dh, dres)


def _final_loss(x, g, tgt):
    T, D = x.shape
    tm = _tile(T, 512)

    def body(x_ref, g_ref, t_ref, l_ref, dx_ref, dg_ref):
        @pl.when(pl.program_id(0) == 0)
        def _():
            dg_ref[...] = jnp.zeros_like(dg_ref)
            l_ref[...] = jnp.zeros_like(l_ref)

        xf = x_ref[...]
        gg = g_ref[...]
        r = lax.rsqrt(jnp.mean(xf * xf, axis=-1, keepdims=True) + RMS_EPS)
        xn = xf * r
        e = xn * gg - t_ref[...]
        part = 0.5 * jnp.sum(jnp.mean(e * e, axis=-1, keepdims=True), axis=0, keepdims=True)
        l_ref[...] += jnp.broadcast_to(part, l_ref.shape)
        dy = e * (1.0 / D)
        u = dy * gg
        mu = jnp.mean(u * xf, axis=-1, keepdims=True)
        dx_ref[...] = r * (u - xf * (r * r * mu))
        dg_ref[...] += jnp.sum(dy * xn, axis=0, keepdims=True)

    row = pl.BlockSpec((tm, D), lambda i: (i, 0))
    vec = pl.BlockSpec((1, D), lambda i: (0, 0))
    lsp = pl.BlockSpec((1, 128), lambda i: (0, 0))
    return _call(body, name="final_loss", grid=(T // tm,), in_specs=[row, vec, row], out_specs=[lsp, row, vec],
                 out_shape=[jax.ShapeDtypeStruct((1, 128), F32), jax.ShapeDtypeStruct((T, D), F32),
                            jax.ShapeDtypeStruct((1, D), F32)],
                 sem=("arbitrary",))(x, g, tgt)


def _slabs(tm, n=2):
    return [slice(k * tm // n, (k + 1) * tm // n) for k in range(n)] if tm % (16 * n) == 0 else [slice(0, tm)]


def _resident(shape):
    return pl.BlockSpec(shape, lambda *_: (0,) * len(shape), pipeline_mode=pl.Buffered(1))


def _ffn_fwd_k(x, gn, wg, wu, wd):
    T, D = x.shape
    NS, _, Fs = wg.shape
    tm = _tile(T, 512)

    def body(x_ref, gn_ref, wg_ref, wu_ref, wd_ref, o_ref, h_ref, s1_ref, s2_ref, a_ref, hs, acc):
        j = pl.program_id(1)

        @pl.when(j == 0)
        def _():
            xf = x_ref[...]
            r = lax.rsqrt(jnp.mean(xf * xf, axis=-1, keepdims=True) + RMS_EPS)
            hs[...] = (xf * r * gn_ref[...]).astype(BF)
            h_ref[...] = hs[...]
            acc[...] = jnp.zeros_like(acc)

        h = hs[...]
        g = _dot(h, wg_ref[j])
        u = _dot(h, wu_ref[j])
        sg = _sigmoid(g)
        s1 = g * sg
        a = (s1 * u).astype(BF)
        s1_ref[...] = s1.astype(BF)
        s2_ref[...] = (u * (sg * (1.0 + g * (1.0 - sg)))).astype(BF)
        a_ref[...] = a
        acc[...] += _dot(a, wd_ref[j])

        @pl.when(j == NS - 1)
        def _():
            o_ref[...] = x_ref[...] + 0.5 * acc[...]

    row = pl.BlockSpec((tm, D), lambda i, j: (i, 0))
    act = pl.BlockSpec((None, tm, Fs), lambda i, j: (j, i, 0))
    sh = jax.ShapeDtypeStruct((NS, T, Fs), BF)
    return _call(body, name="ffn_fwd", grid=(T // tm, NS),
                 in_specs=[row, pl.BlockSpec((1, D), lambda i, j: (0, 0)), _resident(wg.shape), _resident(wu.shape),
                           _resident(wd.shape)],
                 out_specs=[row, row, act, act, act],
                 out_shape=[jax.ShapeDtypeStruct((T, D), F32), jax.ShapeDtypeStruct((T, D), BF), sh, sh, sh],
                 scratch=[pltpu.VMEM((tm, D), BF), pltpu.VMEM((tm, D), F32)],
                 sem=("parallel", "arbitrary"))(x, gn, wg, wu, wd)


def _ffn_bwd_k1(dxo, x, gn, s1, s2, wg, wu, wd):
    NS, T, Fs = s1.shape
    D = x.shape[1]
    tm = _tile(T, 512)

    def body(dxo_ref, x_ref, gn_ref, s1_ref, s2_ref, wg_ref, wu_ref, wd_ref,
             dx_ref, dgn_ref, dg_ref, du_ref, dy_ref, dys, acc):
        i, j = pl.program_id(0), pl.program_id(1)

        @pl.when((i == 0) & (j == 0))
        def _():
            dgn_ref[...] = jnp.zeros_like(dgn_ref)

        @pl.when(j == 0)
        def _():
            dys[...] = (0.5 * dxo_ref[...]).astype(BF)
            dy_ref[...] = dys[...]
            acc[...] = jnp.zeros_like(acc)

        for rows in _slabs(tm):
            da = _dot_nt(dys[rows, :], wd_ref[j])
            dg = (da * s2_ref[rows, :].astype(F32)).astype(BF)
            du = (da * s1_ref[rows, :].astype(F32)).astype(BF)
            dg_ref[rows, :] = dg
            du_ref[rows, :] = du
            acc[rows, :] += _dot_nt(dg, wg_ref[j]) + _dot_nt(du, wu_ref[j])

        @pl.when(j == NS - 1)
        def _():
            xf = x_ref[...]
            r = lax.rsqrt(jnp.mean(xf * xf, axis=-1, keepdims=True) + RMS_EPS)
            dh = acc[...]
            uu = dh * gn_ref[...]
            mu = jnp.mean(uu * xf, axis=-1, keepdims=True)
            dx_ref[...] = dxo_ref[...] + r * (uu - xf * (r * r * mu))
            dgn_ref[...] += jnp.sum(dh * xf * r, axis=0, keepdims=True)

    row = pl.BlockSpec((tm, D), lambda i, j: (i, 0))
    vec = pl.BlockSpec((1, D), lambda i, j: (0, 0))
    act = pl.BlockSpec((None, tm, Fs), lambda i, j: (j, i, 0))
    sh = jax.ShapeDtypeStruct((NS, T, Fs), BF)
    return _call(body, name="ffn_bwd_x", grid=(T // tm, NS),
                 in_specs=[row, row, vec, act, act, _resident(wg.shape), _resident(wu.shape), _resident(wd.shape)],
                 out_specs=[row, vec, act, act, row],
                 out_shape=[jax.ShapeDtypeStruct((T, D), F32), jax.ShapeDtypeStruct((1, D), F32), sh, sh,
                            jax.ShapeDtypeStruct((T, D), BF)],
                 scratch=[pltpu.VMEM((tm, D), BF), pltpu.VMEM((tm, D), F32)],
                 sem=("arbitrary", "arbitrary"))(dxo, x, gn, s1, s2, wg, wu, wd)


def _ffn_bwd_k2(hb, dyb, a, dg, du):
    NS, T, Fs = a.shape
    D = hb.shape[1]
    tk = _tile(T, 512)

    def body(h_ref, dy_ref, a_ref, dg_ref, du_ref, og_ref, ou_ref, od_ref):
        @pl.when(pl.program_id(1) == 0)
        def _():
            og_ref[...] = jnp.zeros_like(og_ref)
            ou_ref[...] = jnp.zeros_like(ou_ref)
            od_ref[...] = jnp.zeros_like(od_ref)

        h = h_ref[...]
        og_ref[...] += _dot_tn(h, dg_ref[...])
        ou_ref[...] += _dot_tn(h, du_ref[...])
        od_ref[...] += _dot_tn(a_ref[...], dy_ref[...])

    row = pl.BlockSpec((tk, D), lambda j, k: (k, 0))
    act = pl.BlockSpec((None, tk, Fs), lambda j, k: (j, k, 0))
    return _call(body, name="ffn_bwd_w", grid=(NS, T // tk), in_specs=[row, row, act, act, act],
                 out_specs=[pl.BlockSpec((None, D, Fs), lambda j, k: (j, 0, 0))] * 2
                 + [pl.BlockSpec((None, Fs, D), lambda j, k: (j, 0, 0))],
                 out_shape=[jax.ShapeDtypeStruct((NS, D, Fs), F32)] * 2 + [jax.ShapeDtypeStruct((NS, Fs, D), F32)],
                 sem=("parallel", "arbitrary"))(hb, dyb, a, dg, du)


def _mm_nn(a, b, res=None, out_dtype=F32):
    T, K = a.shape
    N = b.shape[1]
    tm = _tile(T, 512)
    tn = N if N <= 2048 else _tile(N, 1024)

    def body(*refs):
        if res is None:
            a_ref, b_ref, o_ref = refs
            o_ref[...] = _dot(a_ref[...], b_ref[...]).astype(out_dtype)
        else:
            a_ref, b_ref, r_ref, o_ref = refs
            o_ref[...] = (r_ref[...] + _dot(a_ref[...], b_ref[...])).astype(out_dtype)

    o = pl.BlockSpec((tm, tn), lambda i, j: (i, j))
    ins = [pl.BlockSpec((tm, K), lambda i, j: (i, 0)), pl.BlockSpec((K, tn), lambda i, j: (0, j))]
    args = [a, b]
    if res is not None:
        ins.append(o)
        args.append(res)
    return _call(body, name="mm_nn", grid=(T // tm, N // tn), in_specs=ins, out_specs=o,
                 out_shape=jax.ShapeDtypeStruct((T, N), out_dtype), sem=("parallel", "parallel"))(*args)


def _mm_nt(a, b, res=None):
    T, K = a.shape
    N = b.shape[0]
    tm = _tile(T, 512)

    def body(*refs):
        if res is None:
            a_ref, b_ref, o_ref = refs
            o_ref[...] = _dot_nt(a_ref[...].astype(BF), b_ref[...])
        else:
            a_ref, b_ref, r_ref, o_ref = refs
            o_ref[...] = r_ref[...] + _dot_nt(a_ref[...].astype(BF), b_ref[...])

    o = pl.BlockSpec((tm, N), lambda i: (i, 0))
    ins = [pl.BlockSpec((tm, K), lambda i: (i, 0)), pl.BlockSpec((N, K), lambda i: (0, 0))]
    args = [a, b]
    if res is not None:
        ins.append(o)
        args.append(res)
    return _call(body, name="mm_nt", grid=(T // tm,), in_specs=ins, out_specs=o,
                 out_shape=jax.ShapeDtypeStruct((T, N), F32), sem=("parallel",))(*args)


def _mm_tn(a, b):
    T, M = a.shape
    N = b.shape[1]
    tk = _tile(T, 512)
    tmm = _tile(M, 512)

    def body(a_ref, b_ref, o_ref):
        @pl.when(pl.program_id(1) == 0)
        def _():
            o_ref[...] = jnp.zeros_like(o_ref)

        o_ref[...] += _dot_tn(a_ref[...].astype(BF), b_ref[...].astype(BF))

    return _call(body, name="mm_tn", grid=(M // tmm, T // tk),
                 in_specs=[pl.BlockSpec((tk, tmm), lambda i, k: (k, i)), pl.BlockSpec((tk, N), lambda i, k: (k, 0))],
                 out_specs=pl.BlockSpec((tmm, N), lambda i, k: (i, 0)),
                 out_shape=jax.ShapeDtypeStruct((M, N), F32), sem=("parallel", "arbitrary"))(a, b)


def _lane_mask(e, width=128):
    return (lax.broadcasted_iota(jnp.int32, (1, width), 1) // HEAD) == e


def _band_mask(n):
    qi = lax.broadcasted_iota(jnp.int32, (CHUNK, 2 * CHUNK), 0)
    kj = lax.broadcasted_iota(jnp.int32, (CHUNK, 2 * CHUNK), 1)
    dist = qi + CHUNK - kj
    return (dist >= 0) & (dist <= CHUNK) & ((kj >= CHUNK) | (n > 0))


def _prev_cur(ref, n):
    cur = pl.multiple_of(n * CHUNK, CHUNK)
    prv = pl.multiple_of(jnp.maximum(n - 1, 0) * CHUNK, CHUNK)
    return jnp.concatenate([ref[pl.ds(prv, CHUNK), :], ref[pl.ds(cur, CHUNK), :]], axis=0), prv, cur


def _attn_fwd(qkv, dil):
    T = qkv.shape[0]
    B, L = T // SEQ, SEQ // dil
    nb = L // CHUNK
    scale = HEAD ** -0.5

    def body(q_ref, k_ref, v_ref, o_ref, l_ref):
        n = pl.program_id(2)
        q = q_ref[...]
        kk, _, _ = _prev_cur(k_ref, n)
        vv, _, _ = _prev_cur(v_ref, n)
        mask = _band_mask(n)
        for t in range(ATT_W // 128):
            sl = slice(128 * t, 128 * (t + 1))
            qt, kt, vt = q[:, sl], kk[:, sl], vv[:, sl]
            o_pair = jnp.zeros((CHUNK, 128), F32)
            l_pair = jnp.zeros((CHUNK, 128), F32)
            for e in range(2):
                lm = _lane_mask(e)
                s = _dot_nt(jnp.where(lm, qt, jnp.zeros_like(qt)), kt) * scale
                s = jnp.where(mask, s, NEG)
                m = jnp.max(s, axis=-1, keepdims=True)
                p = jnp.exp(s - m)
                den = jnp.sum(p, axis=-1, keepdims=True)
                o = _dot(p.astype(BF), vt) / den
                o_pair = jnp.where(lm, o, o_pair)
                l_pair = jnp.where(lm, m + jnp.log(den), l_pair)
            o_ref[:, sl] = o_pair
            l_ref[:, sl] = l_pair

    qv = qkv.reshape(B, L, dil * W_QKV)
    o = pl.BlockSpec((None, CHUNK, ATT_W), lambda b, r, n: (b, n, r))
    sh = jax.ShapeDtypeStruct((B, L, dil * ATT_W), F32)
    out, lse = _call(
        body, name=f"attn_fwd_d{dil}", grid=(B, dil, nb),
        in_specs=[pl.BlockSpec((None, CHUNK, ATT_W), lambda b, r, n: (b, n, 3 * r)),
                  pl.BlockSpec((None, L, ATT_W), lambda b, r, n: (b, 0, 3 * r + 1)),
                  pl.BlockSpec((None, L, ATT_W), lambda b, r, n: (b, 0, 3 * r + 2))],
        out_specs=[o, o], out_shape=[sh, sh], sem=("parallel", "parallel", "parallel"))(qv, qv, qv)
    return out.reshape(T, ATT_W), lse.reshape(T, ATT_W)


def _attn_combine(o1, o2, o3, l1, l2, l3):
    T = o1.shape[0]
    tm = _tile(T, 512)

    def body(o1_ref, o2_ref, o3_ref, l1_ref, l2_ref, l3_ref, y_ref, l_ref):
        a, b, c = l1_ref[...], l2_ref[...], l3_ref[...]
        m = jnp.maximum(jnp.maximum(a, b), c)
        ea, eb, ec = jnp.exp(a - m), jnp.exp(b - m), jnp.exp(c - m)
        z = ea + eb + ec
        y_ref[...] = (ea * o1_ref[...] + eb * o2_ref[...] + ec * o3_ref[...]) / z
        l_ref[...] = m + jnp.log(z)

    row = pl.BlockSpec((tm, ATT_W), lambda i: (i, 0))
    sh = jax.ShapeDtypeStruct((T, ATT_W), F32)
    return _call(body, name="attn_combine", grid=(T // tm,), in_specs=[row] * 6, out_specs=[row, row],
                 out_shape=[sh, sh], sem=("parallel",))(o1, o2, o3, l1, l2, l3)


def _attn_bwd(qkv, do, out, lse, dil):
    T = qkv.shape[0]
    B, L = T // SEQ, SEQ // dil
    nb = L // CHUNK
    scale = HEAD ** -0.5

    def body(q_ref, k_ref, v_ref, do_ref, out_ref, lse_ref, dq_ref, dk_ref, dv_ref):
        n = pl.program_id(2)

        @pl.when(n == 0)
        def _():
            dk_ref[...] = jnp.zeros_like(dk_ref)
            dv_ref[...] = jnp.zeros_like(dv_ref)

        q = q_ref[...]
        kk, prv, cur = _prev_cur(k_ref, n)
        vv, _, _ = _prev_cur(v_ref, n)
        mask = _band_mask(n)
        do_ = do_ref[...]
        dlt = do_ * out_ref[...]
        ls = lse_ref[...]
        for t in range(ATT_W // 128):
            sl = slice(128 * t, 128 * (t + 1))
            qt, kt, vt = q[:, sl], kk[:, sl], vv[:, sl]
            dq_pair = jnp.zeros((CHUNK, 128), F32)
            dk_acc = jnp.zeros((2 * CHUNK, 128), F32)
            dv_acc = jnp.zeros((2 * CHUNK, 128), F32)
            for e in range(2):
                lm = _lane_mask(e)
                qm = jnp.where(lm, qt, jnp.zeros_like(qt))
                s = _dot_nt(qm, kt) * scale
                lse_col = ls[:, 128 * t + HEAD * e:128 * t + HEAD * e + 1]
                p = jnp.exp(jnp.where(mask, s - lse_col, NEG))
                dom = jnp.where(lm, do_[:, sl], 0.0).astype(BF)
                dv_acc += _dot_tn(p.astype(BF), dom)
                dp = _dot_nt(dom, vt)
                delta = jnp.sum(jnp.where(lm, dlt[:, sl], 0.0), axis=-1, keepdims=True)
                ds = (p * (dp - delta) * scale).astype(BF)
                dq_pair += jnp.where(lm, _dot(ds, kt), 0.0)
                dk_acc += _dot_tn(ds, qm)
            dq_ref[:, sl] = dq_pair
            dk_ref[pl.ds(cur, CHUNK), sl] += dk_acc[CHUNK:]
            dk_ref[pl.ds(prv, CHUNK), sl] += dk_acc[:CHUNK]
            dv_ref[pl.ds(cur, CHUNK), sl] += dv_acc[CHUNK:]
            dv_ref[pl.ds(prv, CHUNK), sl] += dv_acc[:CHUNK]

    qv = qkv.reshape(B, L, dil * W_QKV)
    view = lambda a: a.reshape(B, L, dil * ATT_W)
    blk = pl.BlockSpec((None, CHUNK, ATT_W), lambda b, r, n: (b, n, r))
    whole = pl.BlockSpec((None, L, ATT_W), lambda b, r, n: (b, 0, r))
    sh = jax.ShapeDtypeStruct((B, L, dil * ATT_W), F32)
    dq, dk, dv = _call(
        body, name=f"attn_bwd_d{dil}", grid=(B, dil, nb),
        in_specs=[pl.BlockSpec((None, CHUNK, ATT_W), lambda b, r, n: (b, n, 3 * r)),
                  pl.BlockSpec((None, L, ATT_W), lambda b, r, n: (b, 0, 3 * r + 1)),
                  pl.BlockSpec((None, L, ATT_W), lambda b, r, n: (b, 0, 3 * r + 2)),
                  blk, blk, blk],
        out_specs=[blk, whole, whole], out_shape=[sh, sh, sh],
        sem=("parallel", "parallel", "arbitrary"))(qv, qv, qv, view(do), view(out), view(lse))
    return dq.reshape(T, ATT_W), dk.reshape(T, ATT_W), dv.reshape(T, ATT_W)


def _sum_branches(parts):
    T = parts[0][0].shape[0]
    tm = _tile(T, 512)

    def body(*refs):
        o_ref = refs[-1]
        for s in range(3):
            acc = refs[s][...] + refs[3 + s][...] + refs[6 + s][...]
            o_ref[:, ATT_W * s:ATT_W * (s + 1)] = acc.astype(BF)

    row = pl.BlockSpec((tm, ATT_W), lambda i: (i, 0))
    flat = [a for tr in parts for a in tr]
    return _call(body, name="attn_sum_branches", grid=(T // tm,), in_specs=[row] * 9,
                 out_specs=pl.BlockSpec((tm, W_QKV), lambda i: (i, 0)),
                 out_shape=jax.ShapeDtypeStruct((T, W_QKV), BF), sem=("parallel",))(*flat)


def _silu(x):
    return x * _sigmoid(x)


def _dsilu(x):
    s = _sigmoid(x)
    return s * (1.0 + x * (1.0 - s))


def _log1p(u):
    return jnp.where(u < 0.01, u * (1.0 - u * (0.5 - u * (1.0 / 3.0))), jnp.log(1.0 + u))


def _softplus(x):
    return jnp.maximum(x, 0.0) + _log1p(jnp.exp(-jnp.abs(x)))


def _cumsum_rows(x, reverse=False):
    n = x.shape[0]
    rows = lax.broadcasted_iota(jnp.int32, x.shape, 0)
    k = 1
    while k < n:
        if reverse:
            x = x + jnp.where(rows < n - k, pltpu.roll(x, n - k, 0), 0.0)
        else:
            x = x + jnp.where(rows >= k, pltpu.roll(x, k, 0), 0.0)
        k *= 2
    return x


def _tri():
    r = lax.broadcasted_iota(jnp.int32, (CHUNK, CHUNK), 0)
    c = lax.broadcasted_iota(jnp.int32, (CHUNK, CHUNK), 1)
    return r >= c


def _row_mask(e):
    return (lax.broadcasted_iota(jnp.int32, (128, 1), 0) // HEAD) == e


def _first_lane(e):
    return lax.broadcasted_iota(jnp.int32, (1, 128), 1) == HEAD * e


def _ssd_pre(x_ref, halo_ref, first, cw_ref, cb_ref, dtb_ref, al_ref, ext):
    row = x_ref[...]
    z = row[:, SSD_CONV_DIM:SSD_CONV_DIM + SSD_W]
    u = row[:, SSD_CONV_DIM + SSD_W:] + dtb_ref[...]
    ext[0:8, :] = jnp.where(first, 0.0, halo_ref[:, 0:SSD_CONV_DIM])
    ext[8:8 + CHUNK, :] = row[:, 0:SSD_CONV_DIM]
    xc = cb_ref[...]
    for j in range(4):
        xc = xc + cw_ref[j:j + 1, :] * ext[pl.ds(5 + j, CHUNK), :]
    xa = _silu(xc)
    dt = _softplus(u)
    a = dt * (-jnp.exp(al_ref[...]))
    A = _cumsum_rows(a)
    return dict(z=z, u=u, xc=xc, xs=xa[:, 0:SSD_W], Bm=xa[:, SSD_W:SSD_W + 256], Cm=xa[:, SSD_W + 256:],
                dt=dt, a=a, A=A, AT=A.T, eA=jnp.exp(A), wdec=jnp.exp(A[CHUNK - 1:CHUNK, :] - A),
                dtot=jnp.exp(A[CHUNK - 1:CHUNK, :]))


def _ssd_y(p, hp_ref, dskip):
    tri = _tri()
    X = p["xs"] * p["dt"]
    Bb = [p["Bm"][:, 128 * g:128 * (g + 1)].astype(BF) for g in range(2)]
    Cb = [p["Cm"][:, 128 * g:128 * (g + 1)].astype(BF) for g in range(2)]
    CB = [_dot_nt(Cb[g], Bb[g]) for g in range(2)]
    tiles = []
    for t in range(3):
        sl = slice(128 * t, 128 * (t + 1))
        hpb = hp_ref[sl, :].astype(BF)
        acc = jnp.zeros((CHUNK, 128), F32)
        for e in range(2):
            h = 2 * t + e
            g, col = h // 3, HEAD * h
            lm = _lane_mask(e)
            L = jnp.exp(jnp.where(tri, p["A"][:, col:col + 1] - p["AT"][col:col + 1, :], NEG))
            yd = _dot((CB[g] * L).astype(BF), jnp.where(lm, X[:, sl], 0.0).astype(BF))
            yo = _dot_nt(Cb[g], hpb) * p["eA"][:, sl]
            acc = acc + jnp.where(lm, yd + yo, 0.0)
        tiles.append(acc)
    return jnp.concatenate(tiles, axis=1) + dskip * p["xs"], X, Bb, Cb, CB


def _group_stats(v):
    g0 = lax.broadcasted_iota(jnp.int32, (1, SSD_W), 1) < SSD_W // 2
    m0 = jnp.sum(jnp.where(g0, v, 0.0), axis=-1, keepdims=True) * (2.0 / SSD_W)
    m1 = jnp.sum(jnp.where(g0, 0.0, v), axis=-1, keepdims=True) * (2.0 / SSD_W)
    return jnp.where(g0, m0, m1)


def _ssd_specs(T, rev):
    B = T // SEQ

    def chunk(b, c):
        return b * N_CHUNK + (N_CHUNK - 1 - c if rev else c)

    row = pl.BlockSpec((CHUNK, W_SSD), lambda b, c: (chunk(b, c), 0))
    halo = pl.BlockSpec((8, W_SSD), lambda b, c: (jnp.maximum(chunk(b, c) * (CHUNK // 8) - 1, 0), 0))
    hp = pl.BlockSpec((None, SSD_W, SSD_STATE), lambda b, c: (chunk(b, c), 0, 0))
    y = pl.BlockSpec((CHUNK, SSD_W), lambda b, c: (chunk(b, c), 0))
    const = lambda r, w: pl.BlockSpec((r, w), lambda b, c: (0, 0))
    params = [const(4, SSD_CONV_DIM), const(1, SSD_CONV_DIM)] + [const(1, SSD_W)] * 4
    return B, row, halo, hp, y, const, params


def _ssd_fwd(sin, conv_w, conv_b, dtb, alog, dskip, norm_g):
    T = sin.shape[0]
    B, row, halo, hp, y, const, params = _ssd_specs(T, False)

    def body(x_ref, halo_ref, cw_ref, cb_ref, dtb_ref, al_ref, dk_ref, ng_ref, y_ref, hp_ref, ext, hst):
        c = pl.program_id(1)

        @pl.when(c == 0)
        def _():
            hst[...] = jnp.zeros_like(hst)

        p = _ssd_pre(x_ref, halo_ref, c == 0, cw_ref, cb_ref, dtb_ref, al_ref, ext)
        yv, X, Bb, Cb, CB = _ssd_y(p, hst, dk_ref[...])
        hp_ref[...] = hst[...]
        for t in range(3):
            sl = slice(128 * t, 128 * (t + 1))
            old = hst[sl, :]
            new = old
            for e in range(2):
                h = 2 * t + e
                g, col = h // 3, HEAD * h
                st = _dot_tn(jnp.where(_lane_mask(e), X[:, sl] * p["wdec"][:, sl], 0.0).astype(BF), Bb[g])
                new = jnp.where(_row_mask(e), old * p["dtot"][:, col:col + 1] + st, new)
            hst[sl, :] = new
        y2 = yv * _silu(p["z"])
        r = lax.rsqrt(_group_stats(y2 * y2) + RMS_EPS)
        y_ref[...] = y2 * r * ng_ref[...]

    return _call(body, name="ssd_fwd", grid=(B, N_CHUNK), in_specs=[row, halo] + params, out_specs=[y, hp],
                 out_shape=[jax.ShapeDtypeStruct((T, SSD_W), F32),
                            jax.ShapeDtypeStruct((T // CHUNK, SSD_W, SSD_STATE), F32)],
                 scratch=[pltpu.VMEM((8 + CHUNK, SSD_CONV_DIM), F32), pltpu.VMEM((SSD_W, SSD_STATE), F32)],
                 sem=("parallel", "arbitrary"))(sin, sin, conv_w, conv_b, dtb, alog, dskip, norm_g)


def _ssd_bwd(sin, hprev, dy3, conv_w, conv_b, dtb, alog, dskip, norm_g):
    T = sin.shape[0]
    B, row, halo, hp, y, const, params = _ssd_specs(T, True)

    def body(x_ref, halo_ref, hp_ref, dy_ref, cw_ref, cb_ref, dtb_ref, al_ref, dk_ref, ng_ref,
             dx_ref, dcw_ref, dcb_ref, dvec_ref, ext, ext2, dh):
        c = pl.program_id(1)

        @pl.when((pl.program_id(0) == 0) & (c == 0))
        def _():
            dcw_ref[...] = jnp.zeros_like(dcw_ref)
            dcb_ref[...] = jnp.zeros_like(dcb_ref)
            dvec_ref[...] = jnp.zeros_like(dvec_ref)

        @pl.when(c == 0)
        def _():
            dh[...] = jnp.zeros_like(dh)
            ext2[CHUNK:CHUNK + 8, :] = jnp.zeros((8, SSD_CONV_DIM), F32)

        p = _ssd_pre(x_ref, halo_ref, c == N_CHUNK - 1, cw_ref, cb_ref, dtb_ref, al_ref, ext)
        dskip_ = dk_ref[...]
        yv, X, Bb, Cb, CB = _ssd_y(p, hp_ref, dskip_)
        xs, z, A, AT = p["xs"], p["z"], p["A"], p["AT"]

        sz = _silu(z)
        y2 = yv * sz
        r = lax.rsqrt(_group_stats(y2 * y2) + RMS_EPS)
        dy3_ = dy_ref[...]
        uu = dy3_ * ng_ref[...]
        dy2 = r * (uu - y2 * (r * r * _group_stats(uu * y2)))
        dy = dy2 * sz
        dz = dy2 * yv * _dsilu(z)

        tri = _tri()
        rows = lax.broadcasted_iota(jnp.int32, (CHUNK, 1), 0)
        dG = [jnp.zeros((CHUNK, CHUNK), F32) for _ in range(2)]
        dB = [jnp.zeros((CHUNK, SSD_STATE), F32) for _ in range(2)]
        dC = [jnp.zeros((CHUNK, SSD_STATE), F32) for _ in range(2)]
        dX_t, dA_t, ddtx_t = [], [], []
        for t in range(3):
            sl = slice(128 * t, 128 * (t + 1))
            hp_t = hp_ref[sl, :]
            hpb = hp_t.astype(BF)
            dhc = dh[sl, :]
            dh_new = jnp.zeros((128, SSD_STATE), F32)
            dX = jnp.zeros((CHUNK, 128), F32)
            dA = jnp.zeros((CHUNK, 128), F32)
            ddtx = jnp.zeros((CHUNK, 128), F32)
            for e in range(2):
                h = 2 * t + e
                g, col = h // 3, HEAD * h
                lm, rm, fl = _lane_mask(e), _row_mask(e), _first_lane(e)
                L = jnp.exp(jnp.where(tri, A[:, col:col + 1] - AT[col:col + 1, :], NEG))
                Mf = CB[g] * L
                Xm = jnp.where(lm, X[:, sl], 0.0)
                Xmb = Xm.astype(BF)
                dyh = jnp.where(lm, dy[:, sl], 0.0)
                dyb = dyh.astype(BF)
                dXh = _dot_tn(Mf.astype(BF), dyb)
                dM = jnp.where(tri, _dot_nt(dyb, Xmb), 0.0)
                Wm = dM * Mf
                dAc = jnp.sum(Wm, axis=-1, keepdims=True) - jnp.sum(Wm.T, axis=-1, keepdims=True)
                dG[g] = dG[g] + dM * L
                eAt = p["eA"][:, sl]
                yo = _dot_nt(Cb[g], hpb)
                dyo = (dyh * eAt).astype(BF)
                dC[g] = dC[g] + _dot(dyo, hpb)
                dh_new = dh_new + _dot_tn(dyo, Cb[g])
                dAc = dAc + jnp.sum(dyh * yo * eAt, axis=-1, keepdims=True)
                dHn = jnp.where(rm, dhc, 0.0)
                dHnb = dHn.astype(BF)
                dec = p["dtot"][:, col:col + 1]
                dh_new = dh_new + dec * dHn
                Z = _dot_nt(Bb[g], dHnb)
                wt = p["wdec"][:, sl]
                xi = jnp.sum(Xm * Z, axis=-1, keepdims=True) * p["wdec"][:, col:col + 1]
                dXh = dXh + wt * Z
                dB[g] = dB[g] + _dot(jnp.where(lm, X[:, sl] * wt, 0.0).astype(BF), dHnb)
                dAtot = jnp.sum(xi, axis=0, keepdims=True) + dec * jnp.sum(
                    jnp.sum(dHn * hp_t, axis=-1, keepdims=True), axis=0, keepdims=True)
                dAc = dAc - xi + jnp.where(rows == CHUNK - 1, dAtot, 0.0)
                dA = dA + jnp.where(fl, dAc, 0.0)
                dX = dX + dXh
                ddtx = ddtx + jnp.where(fl, jnp.sum(dXh * xs[:, sl], axis=-1, keepdims=True), 0.0)
            dh[sl, :] = dh_new
            dX_t.append(dX)
            dA_t.append(dA)
            ddtx_t.append(ddtx)
        for g in range(2):
            dGb = dG[g].astype(BF)
            dC[g] = dC[g] + _dot(dGb, Bb[g])
            dB[g] = dB[g] + _dot_tn(dGb, Cb[g])
        dXf = jnp.concatenate(dX_t, axis=1)
        da = _cumsum_rows(jnp.concatenate(dA_t, axis=1), reverse=True)
        ddt = da * (-jnp.exp(al_ref[...])) + jnp.concatenate(ddtx_t, axis=1)
        du = ddt * _sigmoid(p["u"])
        dxs = dXf * p["dt"] + dskip_ * dy
        dxc = jnp.concatenate([dxs, dB[0], dB[1], dC[0], dC[1]], axis=1) * _dsilu(p["xc"])
        ext2[0:CHUNK, :] = dxc
        dxbc = jnp.zeros((CHUNK, SSD_CONV_DIM), F32)
        for j in range(4):
            dxbc = dxbc + cw_ref[j:j + 1, :] * ext2[pl.ds(3 - j, CHUNK), :]
            dcw_ref[j:j + 1, :] += jnp.sum(dxc * ext[pl.ds(5 + j, CHUNK), :], axis=0, keepdims=True)
        ext2[CHUNK:CHUNK + 8, :] = dxc[0:8, :]
        dcb_ref[...] += jnp.sum(dxc, axis=0, keepdims=True)
        dvec_ref[0:1, :] += jnp.sum(du, axis=0, keepdims=True)
        dvec_ref[1:2, :] += jnp.sum(da * p["a"], axis=0, keepdims=True)
        dvec_ref[2:3, :] += jnp.sum(dy * xs, axis=0, keepdims=True)
        dvec_ref[3:4, :] += jnp.sum(dy3_ * y2 * r, axis=0, keepdims=True)
        dx_ref[...] = jnp.concatenate([dxbc, dz, du], axis=1).astype(BF)

    return _call(body, name="ssd_bwd", grid=(B, N_CHUNK), in_specs=[row, halo, hp, y] + params,
                 out_specs=[row, const(4, SSD_CONV_DIM), const(1, SSD_CONV_DIM), const(8, SSD_W)],
                 out_shape=[jax.ShapeDtypeStruct((T, W_SSD), BF), jax.ShapeDtypeStruct((4, SSD_CONV_DIM), F32),
                            jax.ShapeDtypeStruct((1, SSD_CONV_DIM), F32), jax.ShapeDtypeStruct((8, SSD_W), F32)],
                 scratch=[pltpu.VMEM((8 + CHUNK, SSD_CONV_DIM), F32), pltpu.VMEM((8 + CHUNK, SSD_CONV_DIM), F32),
                          pltpu.VMEM((SSD_W, SSD_STATE), F32)],
                 sem=("arbitrary", "arbitrary"))(sin, sin, hprev, dy3, conv_w, conv_b, dtb, alog, dskip, norm_g)


def _sgu_core(uv_ref, g_ref, b_ref, w_ref, bias_ref):
    x = uv_ref[...]
    cdf = 0.5 * (1.0 + lax.erf(x * (2.0 ** -0.5)))
    ge = x * cdf
    dge = cdf + x * jnp.exp(-0.5 * x * x) * ((2.0 * math.pi) ** -0.5)
    u, v = ge[:, 0:SGU_W], ge[:, SGU_W:]
    vc = v - jnp.mean(v, axis=-1, keepdims=True)
    rstd = lax.rsqrt(jnp.mean(vc * vc, axis=-1, keepdims=True) + LN_EPS)
    vhat = vc * rstd
    vn = vhat * g_ref[...] + b_ref[...]
    tri = _tri()
    wc = [jnp.where(tri, w_ref[gi], 0.0).astype(BF) for gi in range(4)]
    vm = [jnp.where(_lane_mask(gi % 2), vn[:, 128 * (gi // 2):128 * (gi // 2 + 1)], 0.0).astype(BF) for gi in range(4)]
    mixed = jnp.concatenate([_dot(wc[2 * t], vm[2 * t]) + _dot(wc[2 * t + 1], vm[2 * t + 1]) for t in range(2)],
                            axis=1) + bias_ref[...]
    return dict(dge=dge, u=u, rstd=rstd, vhat=vhat, wc=wc, vm=vm, mixed=mixed)


def _sgu_specs():
    vec = pl.BlockSpec((1, SGU_W), lambda i: (0, 0))
    return [pl.BlockSpec((CHUNK, W_UV), lambda i: (i, 0)), vec, vec,
            pl.BlockSpec((4, CHUNK, CHUNK), lambda i: (0, 0, 0)), pl.BlockSpec((CHUNK, SGU_W), lambda i: (0, 0))]


def _sgu_fwd(uv, ln_g, ln_b, w, bias):
    T = uv.shape[0]

    def body(uv_ref, g_ref, b_ref, w_ref, bias_ref, y_ref):
        s = _sgu_core(uv_ref, g_ref, b_ref, w_ref, bias_ref)
        y_ref[...] = s["u"] * s["mixed"]

    return _call(body, name="sgu_fwd", grid=(T // CHUNK,), in_specs=_sgu_specs(),
                 out_specs=pl.BlockSpec((CHUNK, SGU_W), lambda i: (i, 0)),
                 out_shape=jax.ShapeDtypeStruct((T, SGU_W), F32), sem=("parallel",))(uv, ln_g, ln_b, w, bias)


def _sgu_bwd(uv, dy, ln_g, ln_b, w, bias):
    T = uv.shape[0]

    def body(uv_ref, dy_ref, g_ref, b_ref, w_ref, bias_ref, dx_ref, dw_ref, dbias_ref, dln_ref):
        @pl.when(pl.program_id(0) == 0)
        def _():
            dw_ref[...] = jnp.zeros_like(dw_ref)
            dbias_ref[...] = jnp.zeros_like(dbias_ref)
            dln_ref[...] = jnp.zeros_like(dln_ref)

        s = _sgu_core(uv_ref, g_ref, b_ref, w_ref, bias_ref)
        dy_ = dy_ref[...]
        du = dy_ * s["mixed"]
        dmix = dy_ * s["u"]
        dbias_ref[...] += dmix
        tri = _tri()
        dvn_t = []
        for t in range(2):
            acc = jnp.zeros((CHUNK, 128), F32)
            for e in range(2):
                gi = 2 * t + e
                dmg = jnp.where(_lane_mask(e), dmix[:, 128 * t:128 * (t + 1)], 0.0).astype(BF)
                acc = acc + _dot_tn(s["wc"][gi], dmg)
                dw_ref[gi] += jnp.where(tri, _dot_nt(dmg, s["vm"][gi]), 0.0)
            dvn_t.append(acc)
        dvn = jnp.concatenate(dvn_t, axis=1)
        dln_ref[0:1, :] += jnp.sum(dvn * s["vhat"], axis=0, keepdims=True)
        dln_ref[1:2, :] += jnp.sum(dvn, axis=0, keepdims=True)
        dvh = dvn * g_ref[...]
        dv = s["rstd"] * (dvh - jnp.mean(dvh, axis=-1, keepdims=True)
                          - s["vhat"] * jnp.mean(dvh * s["vhat"], axis=-1, keepdims=True))
        dx_ref[...] = (jnp.concatenate([du, dv], axis=1) * s["dge"]).astype(BF)

    ins = _sgu_specs()
    return _call(body, name="sgu_bwd", grid=(T // CHUNK,),
                 in_specs=[ins[0], pl.BlockSpec((CHUNK, SGU_W), lambda i: (i, 0))] + ins[1:],
                 out_specs=[pl.BlockSpec((CHUNK, W_UV), lambda i: (i, 0)),
                            pl.BlockSpec((4, CHUNK, CHUNK), lambda i: (0, 0, 0)),
                            pl.BlockSpec((CHUNK, SGU_W), lambda i: (0, 0)), pl.BlockSpec((8, SGU_W), lambda i: (0, 0))],
                 out_shape=[jax.ShapeDtypeStruct((T, W_UV), BF), jax.ShapeDtypeStruct((4, CHUNK, CHUNK), F32),
                            jax.ShapeDtypeStruct((CHUNK, SGU_W), F32), jax.ShapeDtypeStruct((8, SGU_W), F32)],
                 sem=("arbitrary",))(uv, dy, ln_g, ln_b, w, bias)


def _adamw(w, g, m, v):
    R, C = w.shape
    tr = _tile(R, 256) if R % 8 == 0 else R

    def body(w_ref, g_ref, m_ref, v_ref, d_ref, nm_ref, nv_ref):
        g_ = g_ref[...]
        m2 = ADAM_B1 * m_ref[...] + (1.0 - ADAM_B1) * g_
        v2 = ADAM_B2 * v_ref[...] + (1.0 - ADAM_B2) * (g_ * g_)
        m_hat = m2 / (1.0 - ADAM_B1 ** ADAM_STEP)
        v_hat = v2 / (1.0 - ADAM_B2 ** ADAM_STEP)
        d_ref[...] = -ADAM_LR * (m_hat / (jnp.sqrt(v_hat) + ADAM_EPS) + ADAM_WD * w_ref[...])
        nm_ref[...] = m2
        nv_ref[...] = v2

    blk = pl.BlockSpec((tr, C), lambda i: (i, 0))
    sh = jax.ShapeDtypeStruct((R, C), F32)
    return _call(body, name="adamw", grid=(R // tr,), in_specs=[blk] * 4, out_specs=[blk] * 3,
                 out_shape=[sh] * 3, sem=("parallel",))(w, g, m, v)


def _row_steps(rows):
    return 2 if rows % 32 == 0 else 1


def _pair_add(gbuf, rsib, c):
    NS, _, R, C = gbuf.shape
    n = _row_steps(R)
    tr = R // n

    def body(c_ref, a_ref, b_ref, o_ref):
        o_ref[...] = (a_ref[...] + b_ref[...]).astype(BF)

    blk = pl.BlockSpec((None, tr, C), lambda j, i, c_ref: (j, i, 0))
    return pl.pallas_call(
        body, name="rs_pair_add",
        grid_spec=pltpu.PrefetchScalarGridSpec(
            num_scalar_prefetch=1, grid=(NS, n),
            in_specs=[pl.BlockSpec((None, None, tr, C), lambda j, i, c_ref: (j, c_ref[0], i, 0)), blk],
            out_specs=blk),
        out_shape=jax.ShapeDtypeStruct((NS, R, C), BF),
        compiler_params=pltpu.CompilerParams(dimension_semantics=("parallel", "parallel")),
    )(jnp.reshape(c, (1,)).astype(jnp.int32), gbuf, rsib)


def _chip_sum(pair, recv, me, c):
    NS, R, C = pair.shape
    n = _row_steps(R)
    tr = R // n

    def body(s_ref, own_ref, p_ref, o_ref):
        p = [jnp.where(s_ref[0] == j, own_ref[...], p_ref[j]).astype(F32) for j in range(4)]
        o_ref[...] = ((p[0] + p[1]) + p[2]) + p[3]

    return pl.pallas_call(
        body, name="rs_chip_sum",
        grid_spec=pltpu.PrefetchScalarGridSpec(
            num_scalar_prefetch=1, grid=(n,),
            in_specs=[pl.BlockSpec((None, tr, C), lambda i, s: (s[0], i, 0)),
                      pl.BlockSpec((NS, tr, C), lambda i, s: (0, i, 0))],
            out_specs=pl.BlockSpec((None, tr, C), lambda i, s: (s[1], i, 0))),
        out_shape=jax.ShapeDtypeStruct((2, R, C), F32),
        compiler_params=pltpu.CompilerParams(dimension_semantics=("parallel",)),
    )(jnp.stack([me, c]).astype(jnp.int32), pair, recv)


MESH = pl.DeviceIdType.MESH
ANY = pl.BlockSpec(memory_space=pl.ANY)


def _place():
    x, y, c = lax.axis_index("x"), lax.axis_index("y"), lax.axis_index("c")
    return x, y, c, [(1 - x, y), (x, 1 - y), (1 - x, 1 - y)]


HBM = pl.BlockSpec(memory_space=pltpu.HBM)
SEM = pl.BlockSpec(memory_space=pltpu.SEMAPHORE)
EFFECT = pltpu.SideEffectType.DATAFLOW_SIDE_EFFECTING


class _Split:
    def __init__(self, tag, arrays, copies, n_copies):
        self.tag, self.copies, k = tag, copies, len(arrays)

        def body(*refs):
            for cp in copies(refs[:k], refs[k], refs[k + 1]):
                cp.start()
            refs[-1][...] = jnp.zeros_like(refs[-1])

        out = pl.pallas_call(
            body, name=tag + "_start",
            out_shape=(pltpu.SemaphoreType.DMA((n_copies,)), pltpu.SemaphoreType.DMA((n_copies,)),
                       *[pltpu.HBM(a.shape, a.dtype) for a in arrays], jax.ShapeDtypeStruct((8, 128), F32)),
            in_specs=[HBM] * k, out_specs=(SEM, SEM, *[HBM] * k, pl.BlockSpec(memory_space=pltpu.VMEM)),
            input_output_aliases={i: 2 + i for i in range(k)},
            compiler_params=pltpu.CompilerParams(has_side_effects=EFFECT),
        )(*[pltpu.with_memory_space_constraint(a, pltpu.HBM) for a in arrays])
        self.send, self.recv, self.arrays, self.token = out[0], out[1], list(out[2:2 + k]), out[-1][0, 0]

    def wait(self, after):
        k, copies = len(self.arrays), self.copies

        def body(*refs):
            for cp in copies(refs[:k], refs[k], refs[k + 1]):
                cp.wait_send()
                cp.wait_recv()

        return list(pl.pallas_call(
            body, name=self.tag + "_wait", out_shape=tuple(pltpu.HBM(a.shape, a.dtype) for a in self.arrays),
            in_specs=[HBM] * k + [SEM, SEM, ANY], out_specs=tuple([HBM] * k),
            input_output_aliases={i: i for i in range(k)},
            compiler_params=pltpu.CompilerParams(has_side_effects=EFFECT),
        )(*self.arrays, self.send, self.recv, after))


def _gather_start(arrs, tag):
    n = len(arrs)
    me = 2 * lax.axis_index("x") + lax.axis_index("y")
    lands = [lax.dynamic_update_index_in_dim(lax.empty((4,) + a.shape, a.dtype), a, me, 0) for a in arrs]

    def copies(refs, send, recv):
        x, y, c, chips = _place()
        return [pltpu.make_async_remote_copy(
            src_ref=refs[k], dst_ref=refs[n + k].at[2 * x + y], send_sem=send.at[3 * k + r],
            recv_sem=recv.at[3 * k + r], device_id=(px, py, c), device_id_type=MESH)
            for k in range(n) for r, (px, py) in enumerate(chips)]

    return _Split("gather_" + tag, list(arrs) + lands, copies, 3 * n)


def _to_sibling_start(gbufs, tag):
    n = len(gbufs)

    def copies(refs, send, recv):
        x, y, c, _ = _place()
        return [pltpu.make_async_remote_copy(
            src_ref=refs[k].at[j, 1 - c], dst_ref=refs[n + k].at[j], send_sem=send.at[4 * k + j],
            recv_sem=recv.at[4 * k + j], device_id=(x, y, 1 - c), device_id_type=MESH)
            for k in range(n) for j in range(4)]

    lands = [lax.empty((4,) + g.shape[2:], g.dtype) for g in gbufs]
    return _Split("rs_sibling_" + tag, list(gbufs) + lands, copies, 4 * n)


def _to_chips_start(pbufs, tag):
    n = len(pbufs)

    def copies(refs, send, recv):
        x, y, c, chips = _place()
        return [pltpu.make_async_remote_copy(
            src_ref=refs[k].at[2 * px + py], dst_ref=refs[n + k].at[2 * x + y], send_sem=send.at[3 * k + r],
            recv_sem=recv.at[3 * k + r], device_id=(px, py, c), device_id_type=MESH)
            for k in range(n) for r, (px, py) in enumerate(chips)]

    return _Split("rs_chips_" + tag, list(pbufs) + [lax.empty(p.shape, p.dtype) for p in pbufs], copies, 3 * n)


def _join_start(fulls, tag):
    def copies(refs, send, recv):
        x, y, c, _ = _place()
        return [pltpu.make_async_remote_copy(
            src_ref=refs[k].at[c], dst_ref=refs[k].at[c], send_sem=send.at[k], recv_sem=recv.at[k],
            device_id=(x, y, 1 - c), device_id_type=MESH) for k in range(len(fulls))]

    return _Split("rs_join_" + tag, list(fulls), copies, len(fulls))


def _all_reduce_small(v):
    R, C = v.shape

    def body(v_ref, o_ref, g_ref, send, recv, loc):
        x, y, c, chips = _place()
        me, sibling = (x, y, c), (x, y, 1 - c)

        def rows(px, py, pc):
            return g_ref.at[4 * px + 2 * py + pc]

        def copy(k, block, to, src=None):
            return pltpu.make_async_remote_copy(
                src_ref=rows(*block) if src is None else src, dst_ref=rows(*block),
                send_sem=send.at[k], recv_sem=recv.at[k], device_id=to, device_id_type=MESH)

        mine = pltpu.make_async_copy(v_ref, rows(*me), loc)
        mine.start()
        first = [copy(0, me, sibling, src=v_ref)]
        first += [copy(1 + j, me, (*chip, c), src=v_ref) for j, chip in enumerate(chips)]
        for cp in first:
            cp.start()
        passed = [copy(4 + j, (*chip, c), sibling) for j, chip in enumerate(chips)]
        for j, chip in enumerate(chips):
            copy(1 + j, (*chip, c), me).wait_recv()
            passed[j].start()
        copy(0, sibling, me).wait_recv()
        for j, chip in enumerate(chips):
            copy(4 + j, (*chip, 1 - c), me).wait_recv()
        for cp in first + passed:
            cp.wait_send()
        mine.wait()
        acc = g_ref[0]
        for d in range(1, 8):
            acc = acc + g_ref[d]
        o_ref[...] = acc

    vm = pl.BlockSpec(memory_space=pltpu.VMEM)
    return pl.pallas_call(
        body, name="all_reduce_small", in_specs=[vm], out_specs=[vm, vm],
        out_shape=[jax.ShapeDtypeStruct((R, C), F32), jax.ShapeDtypeStruct((8, R, C), F32)],
        scratch_shapes=[pltpu.SemaphoreType.DMA((7,)), pltpu.SemaphoreType.DMA((7,)), pltpu.SemaphoreType.DMA],
    )(v)[0]


WEIGHTS = ['ffn1_norm', 'ffn1_w_gate', 'ffn1_w_up', 'ffn1_w_down', 'mix_norm', 'w_in', 'conv_w', 'conv_b', 'dt_bias',
           'a_log', 'd_skip', 'ssd_norm', 'sgu_ln_g', 'sgu_ln_b', 'sgu_w', 'sgu_b', 'w_out', 'ffn2_norm',
           'ffn2_w_gate', 'ffn2_w_up', 'ffn2_w_down', 'final_norm']
SHARDED = ['ffn1_w_gate', 'ffn1_w_up', 'ffn1_w_down', 'w_in', 'conv_w', 'w_out', 'ffn2_w_gate', 'ffn2_w_up',
           'ffn2_w_down']
SMALL = [n for n in WEIGHTS if n not in SHARDED]
GROUPS = [("ffn1", ["ffn1_w_gate", "ffn1_w_up", "ffn1_w_down"]), ("mix", ["w_in", "conv_w", "w_out"]),
          ("ffn2", ["ffn2_w_gate", "ffn2_w_up", "ffn2_w_down"])]
DEPTH = 2


def _pack_w_in(w):
    return jnp.concatenate([w[..., 0:1152], w[..., 1536:2432], w[..., 1152:1536],
                            jnp.repeat(w[..., 2432:2438], HEAD, axis=-1), w[..., 2438:2950]], axis=-1)


def _unpack_w_in(dq, ds, du):
    return jnp.concatenate([dq, ds[:, 896:1280], ds[:, 0:896], ds[:, 1280::HEAD], du], axis=-1)


def _ffn_fwd(x, g, wg, wu, wd):
    xo, hb, S1, S2, A = _ffn_fwd_k(x, g, wg, wu, wd)
    return xo, (x, hb, S1, S2, A)


def _ffn_bwd(dxo, saved, g, wg, wu, wd):
    x, hb, S1, S2, A = saved
    dx, dg, dG, dU, dyb = _ffn_bwd_k1(dxo, x, g, S1, S2, wg, wu, wd)
    dwg, dwu, dwd = _ffn_bwd_k2(hb, dyb, A, dG, dU)
    return dx, dg, dwg, dwu, dwd


def _mix_fwd(x, P):
    hb = _rms_fwd(x, P["mix_norm"])
    qkv = _mm_nn(hb, P["w_qkv"], out_dtype=BF)
    sin = _mm_nn(hb, P["w_ssd"])
    uv = _mm_nn(hb, P["w_uv"])
    o1, l1 = _attn_fwd(qkv, 1)
    o2, l2 = _attn_fwd(qkv, 4)
    o3, l3 = _attn_fwd(qkv, 16)
    y_att, lse = _attn_combine(o1, o2, o3, l1, l2, l3)
    y_ssd, hprev = _ssd_fwd(sin, *P["ssd"])
    y_sgu = _sgu_fwd(uv, *P["sgu"])
    ycat = jnp.concatenate([y_att, y_ssd, y_sgu], axis=1).astype(BF)
    return _mm_nn(ycat, P["w_out"], res=x), (x, hb, qkv, sin, uv, y_att, lse, hprev, ycat)


def _mix_bwd(dxo, saved, P):
    x, hb, qkv, sin, uv, y_att, lse, hprev, ycat = saved
    dycat = _mm_nt(dxo, P["w_out"])
    dwout = _mm_tn(ycat, dxo)
    dy_att, dy_ssd, dy_sgu = dycat[:, 0:ATT_W], dycat[:, ATT_W:ATT_W + SSD_W], dycat[:, ATT_W + SSD_W:]
    dqkv = _sum_branches([_attn_bwd(qkv, dy_att, y_att, lse, d) for d in DILATIONS])
    dsin, dcw, dcb, dvec = _ssd_bwd(sin, hprev, dy_ssd, *P["ssd"])
    duv, dsw, dsbias, dln = _sgu_bwd(uv, dy_sgu, *P["sgu"])
    dwin = _unpack_w_in(_mm_tn(hb, dqkv), _mm_tn(hb, dsin), _mm_tn(hb, duv))
    dh = _mm_nt(dqkv, P["w_qkv"])
    dh = _mm_nt(dsin, P["w_ssd"], res=dh)
    dh = _mm_nt(duv, P["w_uv"], res=dh)
    dx, dg = _rms_bwd(x, P["mix_norm"], dh, dxo)
    grads = dict(
        mix_norm=dg[0], w_in=dwin, conv_w=dcw, conv_b=dcb[0], dt_bias=dvec[0, ::HEAD], a_log=dvec[1, ::HEAD],
        d_skip=jnp.sum(dvec[2].reshape(6, HEAD), axis=-1), ssd_norm=dvec[3], sgu_ln_g=dln[0], sgu_ln_b=dln[1],
        sgu_w=dsw, sgu_b=jnp.sum(dsbias.reshape(CHUNK, 4, HEAD), axis=-1).T, w_out=dwout)
    return dx, grads


def _halved(g):
    rows = g.size // g.shape[-1]
    return g.reshape(4, 2, rows // 8, g.shape[-1])


def kernel(x, ffn1_norm, ffn1_w_gate, ffn1_w_up, ffn1_w_down, mix_norm, w_in, conv_w, conv_b, dt_bias, a_log, d_skip, ssd_norm, sgu_ln_g, sgu_ln_b, sgu_w, sgu_b, w_out, ffn2_norm, ffn2_w_gate, ffn2_w_up, ffn2_w_down, final_norm, loss_target, m_ffn1_norm, m_ffn1_w_gate, m_ffn1_w_up, m_ffn1_w_down, m_mix_norm, m_w_in, m_conv_w, m_conv_b, m_dt_bias, m_a_log, m_d_skip, m_ssd_norm, m_sgu_ln_g, m_sgu_ln_b, m_sgu_w, m_sgu_b, m_w_out, m_ffn2_norm, m_ffn2_w_gate, m_ffn2_w_up, m_ffn2_w_down, m_final_norm, v_ffn1_norm, v_ffn1_w_gate, v_ffn1_w_up, v_ffn1_w_down, v_mix_norm, v_w_in, v_conv_w, v_conv_b, v_dt_bias, v_a_log, v_d_skip, v_ssd_norm, v_sgu_ln_g, v_sgu_ln_b, v_sgu_w, v_sgu_b, v_w_out, v_ffn2_norm, v_ffn2_w_gate, v_ffn2_w_up, v_ffn2_w_down, v_final_norm):
    given = dict(x=x, ffn1_norm=ffn1_norm, ffn1_w_gate=ffn1_w_gate, ffn1_w_up=ffn1_w_up, ffn1_w_down=ffn1_w_down, mix_norm=mix_norm, w_in=w_in, conv_w=conv_w, conv_b=conv_b, dt_bias=dt_bias, a_log=a_log, d_skip=d_skip, ssd_norm=ssd_norm, sgu_ln_g=sgu_ln_g, sgu_ln_b=sgu_ln_b, sgu_w=sgu_w, sgu_b=sgu_b, w_out=w_out, ffn2_norm=ffn2_norm, ffn2_w_gate=ffn2_w_gate, ffn2_w_up=ffn2_w_up, ffn2_w_down=ffn2_w_down, final_norm=final_norm, loss_target=loss_target, m_ffn1_norm=m_ffn1_norm, m_ffn1_w_gate=m_ffn1_w_gate, m_ffn1_w_up=m_ffn1_w_up, m_ffn1_w_down=m_ffn1_w_down, m_mix_norm=m_mix_norm, m_w_in=m_w_in, m_conv_w=m_conv_w, m_conv_b=m_conv_b, m_dt_bias=m_dt_bias, m_a_log=m_a_log, m_d_skip=m_d_skip, m_ssd_norm=m_ssd_norm, m_sgu_ln_g=m_sgu_ln_g, m_sgu_ln_b=m_sgu_ln_b, m_sgu_w=m_sgu_w, m_sgu_b=m_sgu_b, m_w_out=m_w_out, m_ffn2_norm=m_ffn2_norm, m_ffn2_w_gate=m_ffn2_w_gate, m_ffn2_w_up=m_ffn2_w_up, m_ffn2_w_down=m_ffn2_w_down, m_final_norm=m_final_norm, v_ffn1_norm=v_ffn1_norm, v_ffn1_w_gate=v_ffn1_w_gate, v_ffn1_w_up=v_ffn1_w_up, v_ffn1_w_down=v_ffn1_w_down, v_mix_norm=v_mix_norm, v_w_in=v_w_in, v_conv_w=v_conv_w, v_conv_b=v_conv_b, v_dt_bias=v_dt_bias, v_a_log=v_a_log, v_d_skip=v_d_skip, v_ssd_norm=v_ssd_norm, v_sgu_ln_g=v_sgu_ln_g, v_sgu_ln_b=v_sgu_ln_b, v_sgu_w=v_sgu_w, v_sgu_b=v_sgu_b, v_w_out=v_w_out, v_ffn2_norm=v_ffn2_norm, v_ffn2_w_gate=v_ffn2_w_gate, v_ffn2_w_up=v_ffn2_w_up, v_ffn2_w_down=v_ffn2_w_down, v_final_norm=v_final_norm)
    T = given["x"].shape[0] * given["x"].shape[1]
    D = given["x"].shape[2]
    x0 = given["x"].reshape(T, D)
    tgt = given["loss_target"].reshape(T, D)
    c = lax.axis_index("c")

    bf = {n: given[n].astype(BF) for n in SHARDED if n not in ("w_in", "conv_w")}
    bf["w_in"] = _pack_w_in(given["w_in"]).astype(BF)
    bf["conv_w"] = given["conv_w"]
    gathers = {(i, gname): _gather_start([bf[n][i] for n in names], f"l{i}_{gname}")
               for i in range(DEPTH) for gname, names in GROUPS}
    token = functools.reduce(lambda a, b: a + b, [g.token for g in gathers.values()])

    def gathered(i, gname, after):
        return gathers[(i, gname)].wait(after)[3:]

    def mix_params(i, got):
        win = got[0].reshape(D, W_QKV + W_SSD + W_UV)
        rep = lambda v: jnp.repeat(v, HEAD)[None]
        ssd = (got[1].transpose(1, 0, 2).reshape(4, SSD_CONV_DIM), given["conv_b"][i][None],
               rep(given["dt_bias"][i]), rep(given["a_log"][i]), rep(given["d_skip"][i]), given["ssd_norm"][i][None])
        sgu = (given["sgu_ln_g"][i][None], given["sgu_ln_b"][i][None], given["sgu_w"][i],
               jnp.repeat(given["sgu_b"][i].T, HEAD, axis=1))
        return dict(mix_norm=given["mix_norm"][i][None], w_qkv=win[:, 0:W_QKV], w_ssd=win[:, W_QKV:W_QKV + W_SSD],
                    w_uv=win[:, W_QKV + W_SSD:], w_out=got[2].reshape(-1, D), ssd=ssd, sgu=sgu)

    x = x0
    tape = []
    for i in range(DEPTH):
        P = dict(ffn1=(given["ffn1_norm"][i][None] + (token if i == 0 else 0.0), *gathered(i, "ffn1", x)))
        x, s1 = _ffn_fwd(x, *P["ffn1"])
        P.update(mix_params(i, gathered(i, "mix", x)))
        x, s2 = _mix_fwd(x, P)
        P["ffn2"] = (given["ffn2_norm"][i][None], *gathered(i, "ffn2", x))
        x, s3 = _ffn_fwd(x, *P["ffn2"])
        tape.append((P, s1, s2, s3))
    loss_part, dx, dgf = _final_loss(x, given["final_norm"][None], tgt)

    me = 2 * lax.axis_index("x") + lax.axis_index("y")
    jobs = []

    def rs_begin(i, gname, gd):
        tag = f"l{i}_{gname}"
        names = [n for n in dict(GROUPS)[gname] if n != "conv_w"]
        jobs.append(dict(key=(i, gname), names=names, tag=tag, stage=1,
                         op=_to_sibling_start([_halved(gd[n]) for n in names], tag)))

    def rs_advance(job, after):
        k = len(job["names"])
        if job["stage"] == 1:
            got = job["op"].wait(after)
            job.update(stage=2, op=_to_chips_start([_pair_add(g, l, c) for g, l in zip(got[:k], got[k:])], job["tag"]))
        elif job["stage"] == 2:
            got = job["op"].wait(after)
            job.update(stage=3, op=_join_start([_chip_sum(p, l, me, c) for p, l in zip(got[:k], got[k:])], job["tag"]))
        elif job["stage"] == 3:
            job.update(stage=4, out=dict(zip(job["names"], job["op"].wait(after))))

    def tick(after, begin=None):
        for job in jobs:
            rs_advance(job, after)
        if begin is not None:
            rs_begin(*begin)
        return functools.reduce(lambda a, b: a + b, [j["op"].token for j in jobs if j["stage"] < 4], 0.0)

    grads = [dict() for _ in range(DEPTH)]
    tok = 0.0
    for i in reversed(range(DEPTH)):
        P, s1, s2, s3 = tape[i]
        g = grads[i]
        norm, wg, wu, wd = P["ffn2"]
        dx, dn2, g["ffn2_w_gate"], g["ffn2_w_up"], g["ffn2_w_down"] = _ffn_bwd(dx, s3, norm + tok, wg, wu, wd)
        tok = tick(dx, (i, "ffn2", g))
        dx, gm = _mix_bwd(dx, s2, {**P, "mix_norm": P["mix_norm"] + tok})
        g.update(gm)
        tok = tick(dx, (i, "mix", g))
        norm, wg, wu, wd = P["ffn1"]
        dx, dn1, g["ffn1_w_gate"], g["ffn1_w_up"], g["ffn1_w_down"] = _ffn_bwd(dx, s1, norm + tok, wg, wu, wd)
        tok = tick(dx, (i, "ffn1", g))
        g["ffn1_norm"], g["ffn2_norm"] = dn1[0], dn2[0]
    while any(j["stage"] < 4 for j in jobs):
        tick(dx)
    grad_x = dx.reshape(given["x"].shape)

    order = [n for n in SMALL if n != "final_norm"] + ["final_norm"]
    small = [jnp.stack([grads[i][n] for i in range(DEPTH)]) for n in order[:-1] + ["conv_w"]]
    small = small[:-1] + [dgf[0], small[-1], loss_part[0, 0:1]]
    n_small = sum(s.size for s in small)
    rows_small = -(-n_small // (128 * 8)) * 8

    def flat(arrs):
        fill = rows_small * 128 - sum(a.size for a in arrs)
        return jnp.concatenate([a.reshape(-1) for a in arrs] + [jnp.zeros((fill,), F32)]).reshape(rows_small, 128)

    gsmall = _all_reduce_small(flat(small)).reshape(-1)

    grad_w = {}
    for job in jobs:
        for n, full in job["out"].items():
            grad_w.setdefault(n, [None] * DEPTH)[job["key"][0]] = full.reshape(given[n].shape[1:])
    grad_w = {n: jnp.stack(v) for n, v in grad_w.items()}
    off = 0
    for n in order:
        size = given[n].size
        grad_w[n] = gsmall[off:off + size].reshape(given[n].shape)
        off += size
    cw = gsmall[off:off + 2 * 4 * SSD_CONV_DIM].reshape(DEPTH, 4, SSD_CONV_DIM)
    grad_w["conv_w"] = lax.dynamic_slice_in_dim(cw, me * (SSD_CONV_DIM // 4), SSD_CONV_DIM // 4, axis=2)
    loss = gsmall[off + 2 * 4 * SSD_CONV_DIM]

    delta, new_m, new_v = {}, {}, {}
    for n in SHARDED:
        shp = given[n].shape
        two_d = (shp[0] * shp[1], shp[2])
        d, m2, v2 = _adamw(*[a.reshape(two_d) for a in (given[n], grad_w[n], given["m_" + n], given["v_" + n])])
        delta[n], new_m[n], new_v[n] = d.reshape(shp), m2.reshape(shp), v2.reshape(shp)
    packed = [flat([given[pre + n] for n in order]) for pre in ("", "m_", "v_")]
    outs = [o.reshape(-1) for o in _adamw(packed[0], gsmall.reshape(rows_small, 128), packed[1], packed[2])]
    off = 0
    for n in order:
        size = given[n].size
        for dst, o in zip((delta, new_m, new_v), outs):
            dst[n] = o[off:off + size].reshape(given[n].shape)
        off += size

    return (loss, grad_x, *[grad_w[n] for n in WEIGHTS], *[delta[n] for n in WEIGHTS],
            *[new_m[n] for n in WEIGHTS], *[new_v[n] for n in WEIGHTS])
```

```python
import functools
import math

import jax
import jax.numpy as jnp
from jax import lax
from jax.experimental import pallas as pl
from jax.experimental.pallas import tpu as pltpu

F32 = jnp.float32
BF = jnp.bfloat16

RMS_EPS = 1e-6
LN_EPS = 1e-5
SEQ = 2048
CHUNK = 128
N_CHUNK = SEQ // CHUNK
ATT_W = 384
HEAD = 64
SSD_W = 384
SSD_CONV_DIM = 896
SSD_STATE = 128
SGU_W = 256
DILATIONS = (1, 4, 16)
W_QKV = 3 * ATT_W
W_SSD = SSD_CONV_DIM + SSD_W + SSD_W
W_UV = 2 * SGU_W
ADAM_LR = 0.001
ADAM_B1 = 0.9
ADAM_B2 = 0.999
ADAM_EPS = 1e-08
ADAM_WD = 0.01
ADAM_STEP = 10
NEG = -1e30


def _dot(a, b):
    return jnp.dot(a, b, preferred_element_type=F32)


def _dot_nt(a, b):
    return lax.dot_general(a, b, (((1,), (1,)), ((), ())), preferred_element_type=F32)


def _dot_tn(a, b):
    return lax.dot_general(a, b, (((0,), (0,)), ((), ())), preferred_element_type=F32)


def _sigmoid(x):
    return 1.0 / (1.0 + jnp.exp(-x))


def _call(body, *, name, grid, in_specs, out_specs, out_shape, scratch=(), sem=None):
    return pl.pallas_call(
        body, name=name, grid=grid, in_specs=in_specs, out_specs=out_specs, out_shape=out_shape,
        scratch_shapes=list(scratch),
        compiler_params=pltpu.CompilerParams(dimension_semantics=sem),
    )


def _tile(n, want):
    t = min(n, want)
    while n % t:
        t //= 2
    return t


def _rms_fwd(x, g):
    T, D = x.shape
    tm = _tile(T, 512)

    def body(x_ref, g_ref, h_ref):
        xf = x_ref[...]
        r = lax.rsqrt(jnp.mean(xf * xf, axis=-1, keepdims=True) + RMS_EPS)
        h_ref[...] = (xf * r * g_ref[...]).astype(BF)

    return _call(body, name="rms_fwd", grid=(T // tm,),
                 in_specs=[pl.BlockSpec((tm, D), lambda i: (i, 0)), pl.BlockSpec((1, D), lambda i: (0, 0))],
                 out_specs=pl.BlockSpec((tm, D), lambda i: (i, 0)),
                 out_shape=jax.ShapeDtypeStruct((T, D), BF), sem=("parallel",))(x, g)


def _rms_bwd(x, g, dh, dres):
    T, D = x.shape
    tm = _tile(T, 512)

    def body(x_ref, g_ref, dh_ref, dr_ref, dx_ref, dg_ref):
        @pl.when(pl.program_id(0) == 0)
        def _():
            dg_ref[...] = jnp.zeros_like(dg_ref)

        xf = x_ref[...]
        r = lax.rsqrt(jnp.mean(xf * xf, axis=-1, keepdims=True) + RMS_EPS)
        dh_ = dh_ref[...]
        u = dh_ * g_ref[...]
        mu = jnp.mean(u * xf, axis=-1, keepdims=True)
        dx_ref[...] = dr_ref[...] + r * (u - xf * (r * r * mu))
        dg_ref[...] += jnp.sum(dh_ * xf * r, axis=0, keepdims=True)

    row = pl.BlockSpec((tm, D), lambda i: (i, 0))
    vec = pl.BlockSpec((1, D), lambda i: (0, 0))
    return _call(body, name="rms_bwd", grid=(T // tm,), in_specs=[row, vec, row, row], out_specs=[row, vec],
                 out_shape=[jax.ShapeDtypeStruct((T, D), F32), jax.ShapeDtypeStruct((1, D), F32)],
                 sem=("arbitrary",))(x, g, dh, dres)


def _final_loss(x, g, tgt):
    T, D = x.shape
    tm = _tile(T, 512)

    def body(x_ref, g_ref, t_ref, l_ref, dx_ref, dg_ref):
        @pl.when(pl.program_id(0) == 0)
        def _():
            dg_ref[...] = jnp.zeros_like(dg_ref)
            l_ref[...] = jnp.zeros_like(l_ref)

        xf = x_ref[...]
        gg = g_ref[...]
        r = lax.rsqrt(jnp.mean(xf * xf, axis=-1, keepdims=True) + RMS_EPS)
        xn = xf * r
        e = xn * gg - t_ref[...]
        part = 0.5 * jnp.sum(jnp.mean(e * e, axis=-1, keepdims=True), axis=0, keepdims=True)
        l_ref[...] += jnp.broadcast_to(part, l_ref.shape)
        dy = e * (1.0 / D)
        u = dy * gg
        mu = jnp.mean(u * xf, axis=-1, keepdims=True)
        dx_ref[...] = r * (u - xf * (r * r * mu))
        dg_ref[...] += jnp.sum(dy * xn, axis=0, keepdims=True)

    row = pl.BlockSpec((tm, D), lambda i: (i, 0))
    vec = pl.BlockSpec((1, D), lambda i: (0, 0))
    lsp = pl.BlockSpec((1, 128), lambda i: (0, 0))
    return _call(body, name="final_loss", grid=(T // tm,), in_specs=[row, vec, row], out_specs=[lsp, row, vec],
                 out_shape=[jax.ShapeDtypeStruct((1, 128), F32), jax.ShapeDtypeStruct((T, D), F32),
                            jax.ShapeDtypeStruct((1, D), F32)],
                 sem=("arbitrary",))(x, g, tgt)


def _slabs(tm, n=2):
    return [slice(k * tm // n, (k + 1) * tm // n) for k in range(n)] if tm % (16 * n) == 0 else [slice(0, tm)]


def _resident(shape):
    return pl.BlockSpec(shape, lambda *_: (0,) * len(shape), pipeline_mode=pl.Buffered(1))


def _ffn_fwd_k(x, gn, wg, wu, wd):
    T, D = x.shape
    NS, _, Fs = wg.shape
    tm = _tile(T, 512)

    def body(x_ref, gn_ref, wg_ref, wu_ref, wd_ref, o_ref, h_ref, s1_ref, s2_ref, a_ref, hs, acc):
        j = pl.program_id(1)

        @pl.when(j == 0)
        def _():
            xf = x_ref[...]
            r = lax.rsqrt(jnp.mean(xf * xf, axis=-1, keepdims=True) + RMS_EPS)
            hs[...] = (xf * r * gn_ref[...]).astype(BF)
            h_ref[...] = hs[...]
            acc[...] = jnp.zeros_like(acc)

        h = hs[...]
        g = _dot(h, wg_ref[j])
        u = _dot(h, wu_ref[j])
        sg = _sigmoid(g)
        s1 = g * sg
        a = (s1 * u).astype(BF)
        s1_ref[...] = s1.astype(BF)
        s2_ref[...] = (u * (sg * (1.0 + g * (1.0 - sg)))).astype(BF)
        a_ref[...] = a
        acc[...] += _dot(a, wd_ref[j])

        @pl.when(j == NS - 1)
        def _():
            o_ref[...] = x_ref[...] + 0.5 * acc[...]

    row = pl.BlockSpec((tm, D), lambda i, j: (i, 0))
    act = pl.BlockSpec((None, tm, Fs), lambda i, j: (j, i, 0))
    sh = jax.ShapeDtypeStruct((NS, T, Fs), BF)
    return _call(body, name="ffn_fwd", grid=(T // tm, NS),
                 in_specs=[row, pl.BlockSpec((1, D), lambda i, j: (0, 0)), _resident(wg.shape), _resident(wu.shape),
                           _resident(wd.shape)],
                 out_specs=[row, row, act, act, act],
                 out_shape=[jax.ShapeDtypeStruct((T, D), F32), jax.ShapeDtypeStruct((T, D), BF), sh, sh, sh],
                 scratch=[pltpu.VMEM((tm, D), BF), pltpu.VMEM((tm, D), F32)],
                 sem=("parallel", "arbitrary"))(x, gn, wg, wu, wd)


def _ffn_bwd_k1(dxo, x, gn, s1, s2, wg, wu, wd):
    NS, T, Fs = s1.shape
    D = x.shape[1]
    tm = _tile(T, 512)

    def body(dxo_ref, x_ref, gn_ref, s1_ref, s2_ref, wg_ref, wu_ref, wd_ref,
             dx_ref, dgn_ref, dg_ref, du_ref, dy_ref, dys, acc):
        i, j = pl.program_id(0), pl.program_id(1)

        @pl.when((i == 0) & (j == 0))
        def _():
            dgn_ref[...] = jnp.zeros_like(dgn_ref)

        @pl.when(j == 0)
        def _():
            dys[...] = (0.5 * dxo_ref[...]).astype(BF)
            dy_ref[...] = dys[...]
            acc[...] = jnp.zeros_like(acc)

        for rows in _slabs(tm):
            da = _dot_nt(dys[rows, :], wd_ref[j])
            dg = (da * s2_ref[rows, :].astype(F32)).astype(BF)
            du = (da * s1_ref[rows, :].astype(F32)).astype(BF)
            dg_ref[rows, :] = dg
            du_ref[rows, :] = du
            acc[rows, :] += _dot_nt(dg, wg_ref[j]) + _dot_nt(du, wu_ref[j])

        @pl.when(j == NS - 1)
        def _():
            xf = x_ref[...]
            r = lax.rsqrt(jnp.mean(xf * xf, axis=-1, keepdims=True) + RMS_EPS)
            dh = acc[...]
            uu = dh * gn_ref[...]
            mu = jnp.mean(uu * xf, axis=-1, keepdims=True)
            dx_ref[...] = dxo_ref[...] + r * (uu - xf * (r * r * mu))
            dgn_ref[...] += jnp.sum(dh * xf * r, axis=0, keepdims=True)

    row = pl.BlockSpec((tm, D), lambda i, j: (i, 0))
    vec = pl.BlockSpec((1, D), lambda i, j: (0, 0))
    act = pl.BlockSpec((None, tm, Fs), lambda i, j: (j, i, 0))
    sh = jax.ShapeDtypeStruct((NS, T, Fs), BF)
    return _call(body, name="ffn_bwd_x", grid=(T // tm, NS),
                 in_specs=[row, row, vec, act, act, _resident(wg.shape), _resident(wu.shape), _resident(wd.shape)],
                 out_specs=[row, vec, act, act, row],
                 out_shape=[jax.ShapeDtypeStruct((T, D), F32), jax.ShapeDtypeStruct((1, D), F32), sh, sh,
                            jax.ShapeDtypeStruct((T, D), BF)],
                 scratch=[pltpu.VMEM((tm, D), BF), pltpu.VMEM((tm, D), F32)],
                 sem=("arbitrary", "arbitrary"))(dxo, x, gn, s1, s2, wg, wu, wd)


def _ffn_bwd_k2(hb, dyb, a, dg, du):
    NS, T, Fs = a.shape
    D = hb.shape[1]
    tk = _tile(T, 512)

    def body(h_ref, dy_ref, a_ref, dg_ref, du_ref, og_ref, ou_ref, od_ref):
        @pl.when(pl.program_id(1) == 0)
        def _():
            og_ref[...] = jnp.zeros_like(og_ref)
            ou_ref[...] = jnp.zeros_like(ou_ref)
            od_ref[...] = jnp.zeros_like(od_ref)

        h = h_ref[...]
        og_ref[...] += _dot_tn(h, dg_ref[...])
        ou_ref[...] += _dot_tn(h, du_ref[...])
        od_ref[...] += _dot_tn(a_ref[...], dy_ref[...])

    row = pl.BlockSpec((tk, D), lambda j, k: (k, 0))
    act = pl.BlockSpec((None, tk, Fs), lambda j, k: (j, k, 0))
    return _call(body, name="ffn_bwd_w", grid=(NS, T // tk), in_specs=[row, row, act, act, act],
                 out_specs=[pl.BlockSpec((None, D, Fs), lambda j, k: (j, 0, 0))] * 2
                 + [pl.BlockSpec((None, Fs, D), lambda j, k: (j, 0, 0))],
                 out_shape=[jax.ShapeDtypeStruct((NS, D, Fs), F32)] * 2 + [jax.ShapeDtypeStruct((NS, Fs, D), F32)],
                 sem=("parallel", "arbitrary"))(hb, dyb, a, dg, du)


def _mm_nn(a, b, res=None, out_dtype=F32):
    T, K = a.shape
    N = b.shape[1]
    tm = _tile(T, 512)
    tn = N if N <= 2048 else _tile(N, 1024)

    def body(*refs):
        if res is None:
            a_ref, b_ref, o_ref = refs
            o_ref[...] = _dot(a_ref[...], b_ref[...]).astype(out_dtype)
        else:
            a_ref, b_ref, r_ref, o_ref = refs
            o_ref[...] = (r_ref[...] + _dot(a_ref[...], b_ref[...])).astype(out_dtype)

    o = pl.BlockSpec((tm, tn), lambda i, j: (i, j))
    ins = [pl.BlockSpec((tm, K), lambda i, j: (i, 0)), pl.BlockSpec((K, tn), lambda i, j: (0, j))]
    args = [a, b]
    if res is not None:
        ins.append(o)
        args.append(res)
    return _call(body, name="mm_nn", grid=(T // tm, N // tn), in_specs=ins, out_specs=o,
                 out_shape=jax.ShapeDtypeStruct((T, N), out_dtype), sem=("parallel", "parallel"))(*args)


def _mm_nt(a, b, res=None):
    T, K = a.shape
    N = b.shape[0]
    tm = _tile(T, 512)

    def body(*refs):
        if res is None:
            a_ref, b_ref, o_ref = refs
            o_ref[...] = _dot_nt(a_ref[...].astype(BF), b_ref[...])
        else:
            a_ref, b_ref, r_ref, o_ref = refs
            o_ref[...] = r_ref[...] + _dot_nt(a_ref[...].astype(BF), b_ref[...])

    o = pl.BlockSpec((tm, N), lambda i: (i, 0))
    ins = [pl.BlockSpec((tm, K), lambda i: (i, 0)), pl.BlockSpec((N, K), lambda i: (0, 0))]
    args = [a, b]
    if res is not None:
        ins.append(o)
        args.append(res)
    return _call(body, name="mm_nt", grid=(T // tm,), in_specs=ins, out_specs=o,
                 out_shape=jax.ShapeDtypeStruct((T, N), F32), sem=("parallel",))(*args)


def _mm_tn(a, b):
    T, M = a.shape
    N = b.shape[1]
    tk = _tile(T, 512)
    tmm = _tile(M, 512)

    def body(a_ref, b_ref, o_ref):
        @pl.when(pl.program_id(1) == 0)
        def _():
            o_ref[...] = jnp.zeros_like(o_ref)

        o_ref[...] += _dot_tn(a_ref[...].astype(BF), b_ref[...].astype(BF))

    return _call(body, name="mm_tn", grid=(M // tmm, T // tk),
                 in_specs=[pl.BlockSpec((tk, tmm), lambda i, k: (k, i)), pl.BlockSpec((tk, N), lambda i, k: (k, 0))],
                 out_specs=pl.BlockSpec((tmm, N), lambda i, k: (i, 0)),
                 out_shape=jax.ShapeDtypeStruct((M, N), F32), sem=("parallel", "arbitrary"))(a, b)


def _lane_mask(e, width=128):
    return (lax.broadcasted_iota(jnp.int32, (1, width), 1) // HEAD) == e


def _band_mask(n):
    qi = lax.broadcasted_iota(jnp.int32, (CHUNK, 2 * CHUNK), 0)
    kj = lax.broadcasted_iota(jnp.int32, (CHUNK, 2 * CHUNK), 1)
    dist = qi + CHUNK - kj
    return (dist >= 0) & (dist <= CHUNK) & ((kj >= CHUNK) | (n > 0))


def _prev_cur(ref, n):
    cur = pl.multiple_of(n * CHUNK, CHUNK)
    prv = pl.multiple_of(jnp.maximum(n - 1, 0) * CHUNK, CHUNK)
    return jnp.concatenate([ref[pl.ds(prv, CHUNK), :], ref[pl.ds(cur, CHUNK), :]], axis=0), prv, cur


def _attn_fwd(qkv, dil):
    T = qkv.shape[0]
    B, L = T // SEQ, SEQ // dil
    nb = L // CHUNK
    scale = HEAD ** -0.5

    def body(q_ref, k_ref, v_ref, o_ref, l_ref):
        n = pl.program_id(2)
        q = q_ref[...]
        kk, _, _ = _prev_cur(k_ref, n)
        vv, _, _ = _prev_cur(v_ref, n)
        mask = _band_mask(n)
        for t in range(ATT_W // 128):
            sl = slice(128 * t, 128 * (t + 1))
            qt, kt, vt = q[:, sl], kk[:, sl], vv[:, sl]
            o_pair = jnp.zeros((CHUNK, 128), F32)
            l_pair = jnp.zeros((CHUNK, 128), F32)
            for e in range(2):
                lm = _lane_mask(e)
                s = _dot_nt(jnp.where(lm, qt, jnp.zeros_like(qt)), kt) * scale
                s = jnp.where(mask, s, NEG)
                m = jnp.max(s, axis=-1, keepdims=True)
                p = jnp.exp(s - m)
                den = jnp.sum(p, axis=-1, keepdims=True)
                o = _dot(p.astype(BF), vt) / den
                o_pair = jnp.where(lm, o, o_pair)
                l_pair = jnp.where(lm, m + jnp.log(den), l_pair)
            o_ref[:, sl] = o_pair
            l_ref[:, sl] = l_pair

    qv = qkv.reshape(B, L, dil * W_QKV)
    o = pl.BlockSpec((None, CHUNK, ATT_W), lambda b, r, n: (b, n, r))
    sh = jax.ShapeDtypeStruct((B, L, dil * ATT_W), F32)
    out, lse = _call(
        body, name=f"attn_fwd_d{dil}", grid=(B, dil, nb),
        in_specs=[pl.BlockSpec((None, CHUNK, ATT_W), lambda b, r, n: (b, n, 3 * r)),
                  pl.BlockSpec((None, L, ATT_W), lambda b, r, n: (b, 0, 3 * r + 1)),
                  pl.BlockSpec((None, L, ATT_W), lambda b, r, n: (b, 0, 3 * r + 2))],
        out_specs=[o, o], out_shape=[sh, sh], sem=("parallel", "parallel", "parallel"))(qv, qv, qv)
    return out.reshape(T, ATT_W), lse.reshape(T, ATT_W)


def _attn_combine(o1, o2, o3, l1, l2, l3):
    T = o1.shape[0]
    tm = _tile(T, 512)

    def body(o1_ref, o2_ref, o3_ref, l1_ref, l2_ref, l3_ref, y_ref, l_ref):
        a, b, c = l1_ref[...], l2_ref[...], l3_ref[...]
        m = jnp.maximum(jnp.maximum(a, b), c)
        ea, eb, ec = jnp.exp(a - m), jnp.exp(b - m), jnp.exp(c - m)
        z = ea + eb + ec
        y_ref[...] = (ea * o1_ref[...] + eb * o2_ref[...] + ec * o3_ref[...]) / z
        l_ref[...] = m + jnp.log(z)

    row = pl.BlockSpec((tm, ATT_W), lambda i: (i, 0))
    sh = jax.ShapeDtypeStruct((T, ATT_W), F32)
    return _call(body, name="attn_combine", grid=(T // tm,), in_specs=[row] * 6, out_specs=[row, row],
                 out_shape=[sh, sh], sem=("parallel",))(o1, o2, o3, l1, l2, l3)


def _attn_bwd(qkv, do, out, lse, dil):
    T = qkv.shape[0]
    B, L = T // SEQ, SEQ // dil
    nb = L // CHUNK
    scale = HEAD ** -0.5

    def body(q_ref, k_ref, v_ref, do_ref, out_ref, lse_ref, dq_ref, dk_ref, dv_ref):
        n = pl.program_id(2)

        @pl.when(n == 0)
        def _():
            dk_ref[...] = jnp.zeros_like(dk_ref)
            dv_ref[...] = jnp.zeros_like(dv_ref)

        q = q_ref[...]
        kk, prv, cur = _prev_cur(k_ref, n)
        vv, _, _ = _prev_cur(v_ref, n)
        mask = _band_mask(n)
        do_ = do_ref[...]
        dlt = do_ * out_ref[...]
        ls = lse_ref[...]
        for t in range(ATT_W // 128):
            sl = slice(128 * t, 128 * (t + 1))
            qt, kt, vt = q[:, sl], kk[:, sl], vv[:, sl]
            dq_pair = jnp.zeros((CHUNK, 128), F32)
            dk_acc = jnp.zeros((2 * CHUNK, 128), F32)
            dv_acc = jnp.zeros((2 * CHUNK, 128), F32)
            for e in range(2):
                lm = _lane_mask(e)
                qm = jnp.where(lm, qt, jnp.zeros_like(qt))
                s = _dot_nt(qm, kt) * scale
                lse_col = ls[:, 128 * t + HEAD * e:128 * t + HEAD * e + 1]
                p = jnp.exp(jnp.where(mask, s - lse_col, NEG))
                dom = jnp.where(lm, do_[:, sl], 0.0).astype(BF)
                dv_acc += _dot_tn(p.astype(BF), dom)
                dp = _dot_nt(dom, vt)
                delta = jnp.sum(jnp.where(lm, dlt[:, sl], 0.0), axis=-1, keepdims=True)
                ds = (p * (dp - delta) * scale).astype(BF)
                dq_pair += jnp.where(lm, _dot(ds, kt), 0.0)
                dk_acc += _dot_tn(ds, qm)
            dq_ref[:, sl] = dq_pair
            dk_ref[pl.ds(cur, CHUNK), sl] += dk_acc[CHUNK:]
            dk_ref[pl.ds(prv, CHUNK), sl] += dk_acc[:CHUNK]
            dv_ref[pl.ds(cur, CHUNK), sl] += dv_acc[CHUNK:]
            dv_ref[pl.ds(prv, CHUNK), sl] += dv_acc[:CHUNK]

    qv = qkv.reshape(B, L, dil * W_QKV)
    view = lambda a: a.reshape(B, L, dil * ATT_W)
    blk = pl.BlockSpec((None, CHUNK, ATT_W), lambda b, r, n: (b, n, r))
    whole = pl.BlockSpec((None, L, ATT_W), lambda b, r, n: (b, 0, r))
    sh = jax.ShapeDtypeStruct((B, L, dil * ATT_W), F32)
    dq, dk, dv = _call(
        body, name=f"attn_bwd_d{dil}", grid=(B, dil, nb),
        in_specs=[pl.BlockSpec((None, CHUNK, ATT_W), lambda b, r, n: (b, n, 3 * r)),
                  pl.BlockSpec((None, L, ATT_W), lambda b, r, n: (b, 0, 3 * r + 1)),
                  pl.BlockSpec((None, L, ATT_W), lambda b, r, n: (b, 0, 3 * r + 2)),
                  blk, blk, blk],
        out_specs=[blk, whole, whole], out_shape=[sh, sh, sh],
        sem=("parallel", "parallel", "arbitrary"))(qv, qv, qv, view(do), view(out), view(lse))
    return dq.reshape(T, ATT_W), dk.reshape(T, ATT_W), dv.reshape(T, ATT_W)


def _sum_branches(parts):
    T = parts[0][0].shape[0]
    tm = _tile(T, 512)

    def body(*refs):
        o_ref = refs[-1]
        for s in range(3):
            acc = refs[s][...] + refs[3 + s][...] + refs[6 + s][...]
            o_ref[:, ATT_W * s:ATT_W * (s + 1)] = acc.astype(BF)

    row = pl.BlockSpec((tm, ATT_W), lambda i: (i, 0))
    flat = [a for tr in parts for a in tr]
    return _call(body, name="attn_sum_branches", grid=(T // tm,), in_specs=[row] * 9,
                 out_specs=pl.BlockSpec((tm, W_QKV), lambda i: (i, 0)),
                 out_shape=jax.ShapeDtypeStruct((T, W_QKV), BF), sem=("parallel",))(*flat)


def _silu(x):
    return x * _sigmoid(x)


def _dsilu(x):
    s = _sigmoid(x)
    return s * (1.0 + x * (1.0 - s))


def _log1p(u):
    return jnp.where(u < 0.01, u * (1.0 - u * (0.5 - u * (1.0 / 3.0))), jnp.log(1.0 + u))


def _softplus(x):
    return jnp.maximum(x, 0.0) + _log1p(jnp.exp(-jnp.abs(x)))


def _cumsum_rows(x, reverse=False):
    n = x.shape[0]
    rows = lax.broadcasted_iota(jnp.int32, x.shape, 0)
    k = 1
    while k < n:
        if reverse:
            x = x + jnp.where(rows < n - k, pltpu.roll(x, n - k, 0), 0.0)
        else:
            x = x + jnp.where(rows >= k, pltpu.roll(x, k, 0), 0.0)
        k *= 2
    return x


def _tri():
    r = lax.broadcasted_iota(jnp.int32, (CHUNK, CHUNK), 0)
    c = lax.broadcasted_iota(jnp.int32, (CHUNK, CHUNK), 1)
    return r >= c


def _row_mask(e):
    return (lax.broadcasted_iota(jnp.int32, (128, 1), 0) // HEAD) == e


def _first_lane(e):
    return lax.broadcasted_iota(jnp.int32, (1, 128), 1) == HEAD * e


def _ssd_pre(x_ref, halo_ref, first, cw_ref, cb_ref, dtb_ref, al_ref, ext):
    row = x_ref[...]
    z = row[:, SSD_CONV_DIM:SSD_CONV_DIM + SSD_W]
    u = row[:, SSD_CONV_DIM + SSD_W:] + dtb_ref[...]
    ext[0:8, :] = jnp.where(first, 0.0, halo_ref[:, 0:SSD_CONV_DIM])
    ext[8:8 + CHUNK, :] = row[:, 0:SSD_CONV_DIM]
    xc = cb_ref[...]
    for j in range(4):
        xc = xc + cw_ref[j:j + 1, :] * ext[pl.ds(5 + j, CHUNK), :]
    xa = _silu(xc)
    dt = _softplus(u)
    a = dt * (-jnp.exp(al_ref[...]))
    A = _cumsum_rows(a)
    return dict(z=z, u=u, xc=xc, xs=xa[:, 0:SSD_W], Bm=xa[:, SSD_W:SSD_W + 256], Cm=xa[:, SSD_W + 256:],
                dt=dt, a=a, A=A, AT=A.T, eA=jnp.exp(A), wdec=jnp.exp(A[CHUNK - 1:CHUNK, :] - A),
                dtot=jnp.exp(A[CHUNK - 1:CHUNK, :]))


def _ssd_y(p, hp_ref, dskip):
    tri = _tri()
    X = p["xs"] * p["dt"]
    Bb = [p["Bm"][:, 128 * g:128 * (g + 1)].astype(BF) for g in range(2)]
    Cb = [p["Cm"][:, 128 * g:128 * (g + 1)].astype(BF) for g in range(2)]
    CB = [_dot_nt(Cb[g], Bb[g]) for g in range(2)]
    tiles = []
    for t in range(3):
        sl = slice(128 * t, 128 * (t + 1))
        hpb = hp_ref[sl, :].astype(BF)
        acc = jnp.zeros((CHUNK, 128), F32)
        for e in range(2):
            h = 2 * t + e
            g, col = h // 3, HEAD * h
            lm = _lane_mask(e)
            L = jnp.exp(jnp.where(tri, p["A"][:, col:col + 1] - p["AT"][col:col + 1, :], NEG))
            yd = _dot((CB[g] * L).astype(BF), jnp.where(lm, X[:, sl], 0.0).astype(BF))
            yo = _dot_nt(Cb[g], hpb) * p["eA"][:, sl]
            acc = acc + jnp.where(lm, yd + yo, 0.0)
        tiles.append(acc)
    return jnp.concatenate(tiles, axis=1) + dskip * p["xs"], X, Bb, Cb, CB


def _group_stats(v):
    g0 = lax.broadcasted_iota(jnp.int32, (1, SSD_W), 1) < SSD_W // 2
    m0 = jnp.sum(jnp.where(g0, v, 0.0), axis=-1, keepdims=True) * (2.0 / SSD_W)
    m1 = jnp.sum(jnp.where(g0, 0.0, v), axis=-1, keepdims=True) * (2.0 / SSD_W)
    return jnp.where(g0, m0, m1)


def _ssd_specs(T, rev):
    B = T // SEQ

    def chunk(b, c):
        return b * N_CHUNK + (N_CHUNK - 1 - c if rev else c)

    row = pl.BlockSpec((CHUNK, W_SSD), lambda b, c: (chunk(b, c), 0))
    halo = pl.BlockSpec((8, W_SSD), lambda b, c: (jnp.maximum(chunk(b, c) * (CHUNK // 8) - 1, 0), 0))
    hp = pl.BlockSpec((None, SSD_W, SSD_STATE), lambda b, c: (chunk(b, c), 0, 0))
    y = pl.BlockSpec((CHUNK, SSD_W), lambda b, c: (chunk(b, c), 0))
    const = lambda r, w: pl.BlockSpec((r, w), lambda b, c: (0, 0))
    params = [const(4, SSD_CONV_DIM), const(1, SSD_CONV_DIM)] + [const(1, SSD_W)] * 4
    return B, row, halo, hp, y, const, params


def _ssd_fwd(sin, conv_w, conv_b, dtb, alog, dskip, norm_g):
    T = sin.shape[0]
    B, row, halo, hp, y, const, params = _ssd_specs(T, False)

    def body(x_ref, halo_ref, cw_ref, cb_ref, dtb_ref, al_ref, dk_ref, ng_ref, y_ref, hp_ref, ext, hst):
        c = pl.program_id(1)

        @pl.when(c == 0)
        def _():
            hst[...] = jnp.zeros_like(hst)

        p = _ssd_pre(x_ref, halo_ref, c == 0, cw_ref, cb_ref, dtb_ref, al_ref, ext)
        yv, X, Bb, Cb, CB = _ssd_y(p, hst, dk_ref[...])
        hp_ref[...] = hst[...]
        for t in range(3):
            sl = slice(128 * t, 128 * (t + 1))
            old = hst[sl, :]
            new = old
            for e in range(2):
                h = 2 * t + e
                g, col = h // 3, HEAD * h
                st = _dot_tn(jnp.where(_lane_mask(e), X[:, sl] * p["wdec"][:, sl], 0.0).astype(BF), Bb[g])
                new = jnp.where(_row_mask(e), old * p["dtot"][:, col:col + 1] + st, new)
            hst[sl, :] = new
        y2 = yv * _silu(p["z"])
        r = lax.rsqrt(_group_stats(y2 * y2) + RMS_EPS)
        y_ref[...] = y2 * r * ng_ref[...]

    return _call(body, name="ssd_fwd", grid=(B, N_CHUNK), in_specs=[row, halo] + params, out_specs=[y, hp],
                 out_shape=[jax.ShapeDtypeStruct((T, SSD_W), F32),
                            jax.ShapeDtypeStruct((T // CHUNK, SSD_W, SSD_STATE), F32)],
                 scratch=[pltpu.VMEM((8 + CHUNK, SSD_CONV_DIM), F32), pltpu.VMEM((SSD_W, SSD_STATE), F32)],
                 sem=("parallel", "arbitrary"))(sin, sin, conv_w, conv_b, dtb, alog, dskip, norm_g)


def _ssd_bwd(sin, hprev, dy3, conv_w, conv_b, dtb, alog, dskip, norm_g):
    T = sin.shape[0]
    B, row, halo, hp, y, const, params = _ssd_specs(T, True)

    def body(x_ref, halo_ref, hp_ref, dy_ref, cw_ref, cb_ref, dtb_ref, al_ref, dk_ref, ng_ref,
             dx_ref, dcw_ref, dcb_ref, dvec_ref, ext, ext2, dh):
        c = pl.program_id(1)

        @pl.when((pl.program_id(0) == 0) & (c == 0))
        def _():
            dcw_ref[...] = jnp.zeros_like(dcw_ref)
            dcb_ref[...] = jnp.zeros_like(dcb_ref)
            dvec_ref[...] = jnp.zeros_like(dvec_ref)

        @pl.when(c == 0)
        def _():
            dh[...] = jnp.zeros_like(dh)
            ext2[CHUNK:CHUNK + 8, :] = jnp.zeros((8, SSD_CONV_DIM), F32)

        p = _ssd_pre(x_ref, halo_ref, c == N_CHUNK - 1, cw_ref, cb_ref, dtb_ref, al_ref, ext)
        dskip_ = dk_ref[...]
        yv, X, Bb, Cb, CB = _ssd_y(p, hp_ref, dskip_)
        xs, z, A, AT = p["xs"], p["z"], p["A"], p["AT"]

        sz = _silu(z)
        y2 = yv * sz
        r = lax.rsqrt(_group_stats(y2 * y2) + RMS_EPS)
        dy3_ = dy_ref[...]
        uu = dy3_ * ng_ref[...]
        dy2 = r * (uu - y2 * (r * r * _group_stats(uu * y2)))
        dy = dy2 * sz
        dz = dy2 * yv * _dsilu(z)

        tri = _tri()
        rows = lax.broadcasted_iota(jnp.int32, (CHUNK, 1), 0)
        dG = [jnp.zeros((CHUNK, CHUNK), F32) for _ in range(2)]
        dB = [jnp.zeros((CHUNK, SSD_STATE), F32) for _ in range(2)]
        dC = [jnp.zeros((CHUNK, SSD_STATE), F32) for _ in range(2)]
        dX_t, dA_t, ddtx_t = [], [], []
        for t in range(3):
            sl = slice(128 * t, 128 * (t + 1))
            hp_t = hp_ref[sl, :]
            hpb = hp_t.astype(BF)
            dhc = dh[sl, :]
            dh_new = jnp.zeros((128, SSD_STATE), F32)
            dX = jnp.zeros((CHUNK, 128), F32)
            dA = jnp.zeros((CHUNK, 128), F32)
            ddtx = jnp.zeros((CHUNK, 128), F32)
            for e in range(2):
                h = 2 * t + e
                g, col = h // 3, HEAD * h
                lm, rm, fl = _lane_mask(e), _row_mask(e), _first_lane(e)
                L = jnp.exp(jnp.where(tri, A[:, col:col + 1] - AT[col:col + 1, :], NEG))
                Mf = CB[g] * L
                Xm = jnp.where(lm, X[:, sl], 0.0)
                Xmb = Xm.astype(BF)
                dyh = jnp.where(lm, dy[:, sl], 0.0)
                dyb = dyh.astype(BF)
                dXh = _dot_tn(Mf.astype(BF), dyb)
                dM = jnp.where(tri, _dot_nt(dyb, Xmb), 0.0)
                Wm = dM * Mf
                dAc = jnp.sum(Wm, axis=-1, keepdims=True) - jnp.sum(Wm.T, axis=-1, keepdims=True)
                dG[g] = dG[g] + dM * L
                eAt = p["eA"][:, sl]
                yo = _dot_nt(Cb[g], hpb)
                dyo = (dyh * eAt).astype(BF)
                dC[g] = dC[g] + _dot(dyo, hpb)
                dh_new = dh_new + _dot_tn(dyo, Cb[g])
                dAc = dAc + jnp.sum(dyh * yo * eAt, axis=-1, keepdims=True)
                dHn = jnp.where(rm, dhc, 0.0)
                dHnb = dHn.astype(BF)
                dec = p["dtot"][:, col:col + 1]
                dh_new = dh_new + dec * dHn
                Z = _dot_nt(Bb[g], dHnb)
                wt = p["wdec"][:, sl]
                xi = jnp.sum(Xm * Z, axis=-1, keepdims=True) * p["wdec"][:, col:col + 1]
                dXh = dXh + wt * Z
                dB[g] = dB[g] + _dot(jnp.where(lm, X[:, sl] * wt, 0.0).astype(BF), dHnb)
                dAtot = jnp.sum(xi, axis=0, keepdims=True) + dec * jnp.sum(
                    jnp.sum(dHn * hp_t, axis=-1, keepdims=True), axis=0, keepdims=True)
                dAc = dAc - xi + jnp.where(rows == CHUNK - 1, dAtot, 0.0)
                dA = dA + jnp.where(fl, dAc, 0.0)
                dX = dX + dXh
                ddtx = ddtx + jnp.where(fl, jnp.sum(dXh * xs[:, sl], axis=-1, keepdims=True), 0.0)
            dh[sl, :] = dh_new
            dX_t.append(dX)
            dA_t.append(dA)
            ddtx_t.append(ddtx)
        for g in range(2):
            dGb = dG[g].astype(BF)
            dC[g] = dC[g] + _dot(dGb, Bb[g])
            dB[g] = dB[g] + _dot_tn(dGb, Cb[g])
        dXf = jnp.concatenate(dX_t, axis=1)
        da = _cumsum_rows(jnp.concatenate(dA_t, axis=1), reverse=True)
        ddt = da * (-jnp.exp(al_ref[...])) + jnp.concatenate(ddtx_t, axis=1)
        du = ddt * _sigmoid(p["u"])
        dxs = dXf * p["dt"] + dskip_ * dy
        dxc = jnp.concatenate([dxs, dB[0], dB[1], dC[0], dC[1]], axis=1) * _dsilu(p["xc"])
        ext2[0:CHUNK, :] = dxc
        dxbc = jnp.zeros((CHUNK, SSD_CONV_DIM), F32)
        for j in range(4):
            dxbc = dxbc + cw_ref[j:j + 1, :] * ext2[pl.ds(3 - j, CHUNK), :]
            dcw_ref[j:j + 1, :] += jnp.sum(dxc * ext[pl.ds(5 + j, CHUNK), :], axis=0, keepdims=True)
        ext2[CHUNK:CHUNK + 8, :] = dxc[0:8, :]
        dcb_ref[...] += jnp.sum(dxc, axis=0, keepdims=True)
        dvec_ref[0:1, :] += jnp.sum(du, axis=0, keepdims=True)
        dvec_ref[1:2, :] += jnp.sum(da * p["a"], axis=0, keepdims=True)
        dvec_ref[2:3, :] += jnp.sum(dy * xs, axis=0, keepdims=True)
        dvec_ref[3:4, :] += jnp.sum(dy3_ * y2 * r, axis=0, keepdims=True)
        dx_ref[...] = jnp.concatenate([dxbc, dz, du], axis=1).astype(BF)

    return _call(body, name="ssd_bwd", grid=(B, N_CHUNK), in_specs=[row, halo, hp, y] + params,
                 out_specs=[row, const(4, SSD_CONV_DIM), const(1, SSD_CONV_DIM), const(8, SSD_W)],
                 out_shape=[jax.ShapeDtypeStruct((T, W_SSD), BF), jax.ShapeDtypeStruct((4, SSD_CONV_DIM), F32),
                            jax.ShapeDtypeStruct((1, SSD_CONV_DIM), F32), jax.ShapeDtypeStruct((8, SSD_W), F32)],
                 scratch=[pltpu.VMEM((8 + CHUNK, SSD_CONV_DIM), F32), pltpu.VMEM((8 + CHUNK, SSD_CONV_DIM), F32),
                          pltpu.VMEM((SSD_W, SSD_STATE), F32)],
                 sem=("arbitrary", "arbitrary"))(sin, sin, hprev, dy3, conv_w, conv_b, dtb, alog, dskip, norm_g)


def _sgu_core(uv_ref, g_ref, b_ref, w_ref, bias_ref):
    x = uv_ref[...]
    cdf = 0.5 * (1.0 + lax.erf(x * (2.0 ** -0.5)))
    ge = x * cdf
    dge = cdf + x * jnp.exp(-0.5 * x * x) * ((2.0 * math.pi) ** -0.5)
    u, v = ge[:, 0:SGU_W], ge[:, SGU_W:]
    vc = v - jnp.mean(v, axis=-1, keepdims=True)
    rstd = lax.rsqrt(jnp.mean(vc * vc, axis=-1, keepdims=True) + LN_EPS)
    vhat = vc * rstd
    vn = vhat * g_ref[...] + b_ref[...]
    tri = _tri()
    wc = [jnp.where(tri, w_ref[gi], 0.0).astype(BF) for gi in range(4)]
    vm = [jnp.where(_lane_mask(gi % 2), vn[:, 128 * (gi // 2):128 * (gi // 2 + 1)], 0.0).astype(BF) for gi in range(4)]
    mixed = jnp.concatenate([_dot(wc[2 * t], vm[2 * t]) + _dot(wc[2 * t + 1], vm[2 * t + 1]) for t in range(2)],
                            axis=1) + bias_ref[...]
    return dict(dge=dge, u=u, rstd=rstd, vhat=vhat, wc=wc, vm=vm, mixed=mixed)


def _sgu_specs():
    vec = pl.BlockSpec((1, SGU_W), lambda i: (0, 0))
    return [pl.BlockSpec((CHUNK, W_UV), lambda i: (i, 0)), vec, vec,
            pl.BlockSpec((4, CHUNK, CHUNK), lambda i: (0, 0, 0)), pl.BlockSpec((CHUNK, SGU_W), lambda i: (0, 0))]


def _sgu_fwd(uv, ln_g, ln_b, w, bias):
    T = uv.shape[0]

    def body(uv_ref, g_ref, b_ref, w_ref, bias_ref, y_ref):
        s = _sgu_core(uv_ref, g_ref, b_ref, w_ref, bias_ref)
        y_ref[...] = s["u"] * s["mixed"]

    return _call(body, name="sgu_fwd", grid=(T // CHUNK,), in_specs=_sgu_specs(),
                 out_specs=pl.BlockSpec((CHUNK, SGU_W), lambda i: (i, 0)),
                 out_shape=jax.ShapeDtypeStruct((T, SGU_W), F32), sem=("parallel",))(uv, ln_g, ln_b, w, bias)


def _sgu_bwd(uv, dy, ln_g, ln_b, w, bias):
    T = uv.shape[0]

    def body(uv_ref, dy_ref, g_ref, b_ref, w_ref, bias_ref, dx_ref, dw_ref, dbias_ref, dln_ref):
        @pl.when(pl.program_id(0) == 0)
        def _():
            dw_ref[...] = jnp.zeros_like(dw_ref)
            dbias_ref[...] = jnp.zeros_like(dbias_ref)
            dln_ref[...] = jnp.zeros_like(dln_ref)

        s = _sgu_core(uv_ref, g_ref, b_ref, w_ref, bias_ref)
        dy_ = dy_ref[...]
        du = dy_ * s["mixed"]
        dmix = dy_ * s["u"]
        dbias_ref[...] += dmix
        tri = _tri()
        dvn_t = []
        for t in range(2):
            acc = jnp.zeros((CHUNK, 128), F32)
            for e in range(2):
                gi = 2 * t + e
                dmg = jnp.where(_lane_mask(e), dmix[:, 128 * t:128 * (t + 1)], 0.0).astype(BF)
                acc = acc + _dot_tn(s["wc"][gi], dmg)
                dw_ref[gi] += jnp.where(tri, _dot_nt(dmg, s["vm"][gi]), 0.0)
            dvn_t.append(acc)
        dvn = jnp.concatenate(dvn_t, axis=1)
        dln_ref[0:1, :] += jnp.sum(dvn * s["vhat"], axis=0, keepdims=True)
        dln_ref[1:2, :] += jnp.sum(dvn, axis=0, keepdims=True)
        dvh = dvn * g_ref[...]
        dv = s["rstd"] * (dvh - jnp.mean(dvh, axis=-1, keepdims=True)
                          - s["vhat"] * jnp.mean(dvh * s["vhat"], axis=-1, keepdims=True))
        dx_ref[...] = (jnp.concatenate([du, dv], axis=1) * s["dge"]).astype(BF)

    ins = _sgu_specs()
    return _call(body, name="sgu_bwd", grid=(T // CHUNK,),
                 in_specs=[ins[0], pl.BlockSpec((CHUNK, SGU_W), lambda i: (i, 0))] + ins[1:],
                 out_specs=[pl.BlockSpec((CHUNK, W_UV), lambda i: (i, 0)),
                            pl.BlockSpec((4, CHUNK, CHUNK), lambda i: (0, 0, 0)),
                            pl.BlockSpec((CHUNK, SGU_W), lambda i: (0, 0)), pl.BlockSpec((8, SGU_W), lambda i: (0, 0))],
                 out_shape=[jax.ShapeDtypeStruct((T, W_UV), BF), jax.ShapeDtypeStruct((4, CHUNK, CHUNK), F32),
                            jax.ShapeDtypeStruct((CHUNK, SGU_W), F32), jax.ShapeDtypeStruct((8, SGU_W), F32)],
                 sem=("arbitrary",))(uv, dy, ln_g, ln_b, w, bias)


def _adamw(w, g, m, v):
    R, C = w.shape
    tr = _tile(R, 256) if R % 8 == 0 else R

    def body(w_ref, g_ref, m_ref, v_ref, d_ref, nm_ref, nv_ref):
        g_ = g_ref[...]
        m2 = ADAM_B1 * m_ref[...] + (1.0 - ADAM_B1) * g_
        v2 = ADAM_B2 * v_ref[...] + (1.0 - ADAM_B2) * (g_ * g_)
        m_hat = m2 / (1.0 - ADAM_B1 ** ADAM_STEP)
        v_hat = v2 / (1.0 - ADAM_B2 ** ADAM_STEP)
        d_ref[...] = -ADAM_LR * (m_hat / (jnp.sqrt(v_hat) + ADAM_EPS) + ADAM_WD * w_ref[...])
        nm_ref[...] = m2
        nv_ref[...] = v2

    blk = pl.BlockSpec((tr, C), lambda i: (i, 0))
    sh = jax.ShapeDtypeStruct((R, C), F32)
    return _call(body, name="adamw", grid=(R // tr,), in_specs=[blk] * 4, out_specs=[blk] * 3,
                 out_shape=[sh] * 3, sem=("parallel",))(w, g, m, v)


def _adamw_layer(w, g, m, v, layer, dep, carry=None):
    L, R, C = w.shape
    tr = _tile(R, 256) if R % 8 == 0 else R

    def body(*refs):
        w_ref, g_ref, m_ref, v_ref = refs[:4]
        d_ref, nm_ref, nv_ref, og_ref = refs[-4:]
        g_ = g_ref[...]
        m2 = ADAM_B1 * m_ref[...] + (1.0 - ADAM_B1) * g_
        v2 = ADAM_B2 * v_ref[...] + (1.0 - ADAM_B2) * (g_ * g_)
        m_hat = m2 / (1.0 - ADAM_B1 ** ADAM_STEP)
        v_hat = v2 / (1.0 - ADAM_B2 ** ADAM_STEP)
        d_ref[...] = -ADAM_LR * (m_hat / (jnp.sqrt(v_hat) + ADAM_EPS) + ADAM_WD * w_ref[...])
        nm_ref[...] = m2
        nv_ref[...] = v2
        og_ref[...] = g_

    lay = pl.BlockSpec((None, tr, C), lambda i: (layer, i, 0))
    ins = [lay, pl.BlockSpec((tr, C), lambda i: (i, 0)), lay, lay, pl.BlockSpec((8, 128), lambda i: (0, 0))]
    args = [w, g, m, v, dep]
    if carry is not None:
        ins += [ANY] * 4
        args += list(carry)
    return pl.pallas_call(
        body, name="adamw_layer", grid=(R // tr,), in_specs=ins, out_specs=[lay] * 4,
        out_shape=[jax.ShapeDtypeStruct((L, R, C), F32)] * 4,
        input_output_aliases={5 + k: k for k in range(4)} if carry is not None else {},
        compiler_params=pltpu.CompilerParams(dimension_semantics=("parallel",)),
    )(*args)


def _row_steps(rows):
    return 2 if rows % 32 == 0 else 1


def _pair_add(gbuf, rsib, c):
    NS, _, R, C = gbuf.shape
    n = _row_steps(R)
    tr = R // n

    def body(c_ref, a_ref, b_ref, o_ref):
        o_ref[...] = (a_ref[...] + b_ref[...]).astype(BF)

    blk = pl.BlockSpec((None, tr, C), lambda j, i, c_ref: (j, i, 0))
    return pl.pallas_call(
        body, name="rs_pair_add",
        grid_spec=pltpu.PrefetchScalarGridSpec(
            num_scalar_prefetch=1, grid=(NS, n),
            in_specs=[pl.BlockSpec((None, None, tr, C), lambda j, i, c_ref: (j, c_ref[0], i, 0)), blk],
            out_specs=blk),
        out_shape=jax.ShapeDtypeStruct((NS, R, C), BF),
        compiler_params=pltpu.CompilerParams(dimension_semantics=("parallel", "parallel")),
    )(jnp.reshape(c, (1,)).astype(jnp.int32), gbuf, rsib)


def _chip_sum(pair, recv, me, c):
    NS, R, C = pair.shape
    n = _row_steps(R)
    tr = R // n

    def body(s_ref, own_ref, p_ref, o_ref):
        p = [jnp.where(s_ref[0] == j, own_ref[...], p_ref[j]).astype(F32) for j in range(4)]
        o_ref[...] = ((p[0] + p[1]) + p[2]) + p[3]

    return pl.pallas_call(
        body, name="rs_chip_sum",
        grid_spec=pltpu.PrefetchScalarGridSpec(
            num_scalar_prefetch=1, grid=(n,),
            in_specs=[pl.BlockSpec((None, tr, C), lambda i, s: (s[0], i, 0)),
                      pl.BlockSpec((NS, tr, C), lambda i, s: (0, i, 0))],
            out_specs=pl.BlockSpec((None, tr, C), lambda i, s: (s[1], i, 0))),
        out_shape=jax.ShapeDtypeStruct((2, R, C), F32),
        compiler_params=pltpu.CompilerParams(dimension_semantics=("parallel",)),
    )(jnp.stack([me, c]).astype(jnp.int32), pair, recv)


MESH = pl.DeviceIdType.MESH
ANY = pl.BlockSpec(memory_space=pl.ANY)


def _place():
    x, y, c = lax.axis_index("x"), lax.axis_index("y"), lax.axis_index("c")
    return x, y, c, [(1 - x, y), (x, 1 - y), (1 - x, 1 - y)]


HBM = pl.BlockSpec(memory_space=pltpu.HBM)
SEM = pl.BlockSpec(memory_space=pltpu.SEMAPHORE)
EFFECT = pltpu.SideEffectType.DATAFLOW_SIDE_EFFECTING


class _Split:
    def __init__(self, tag, arrays, copies, n_copies):
        self.tag, self.copies, k = tag, copies, len(arrays)

        def body(*refs):
            for cp in copies(refs[:k], refs[k], refs[k + 1]):
                cp.start()
            refs[-1][...] = jnp.zeros_like(refs[-1])

        out = pl.pallas_call(
            body, name=tag + "_start",
            out_shape=(pltpu.SemaphoreType.DMA((n_copies,)), pltpu.SemaphoreType.DMA((n_copies,)),
                       *[pltpu.HBM(a.shape, a.dtype) for a in arrays], jax.ShapeDtypeStruct((8, 128), F32)),
            in_specs=[HBM] * k, out_specs=(SEM, SEM, *[HBM] * k, pl.BlockSpec(memory_space=pltpu.VMEM)),
            input_output_aliases={i: 2 + i for i in range(k)},
            compiler_params=pltpu.CompilerParams(has_side_effects=EFFECT),
        )(*[pltpu.with_memory_space_constraint(a, pltpu.HBM) for a in arrays])
        self.send, self.recv, self.arrays, self.token = out[0], out[1], list(out[2:2 + k]), out[-1][0, 0]

    def wait(self, after):
        k, copies = len(self.arrays), self.copies

        def body(*refs):
            for cp in copies(refs[:k], refs[k], refs[k + 1]):
                cp.wait_send()
                cp.wait_recv()

        return list(pl.pallas_call(
            body, name=self.tag + "_wait", out_shape=tuple(pltpu.HBM(a.shape, a.dtype) for a in self.arrays),
            in_specs=[HBM] * k + [SEM, SEM, ANY], out_specs=tuple([HBM] * k),
            input_output_aliases={i: i for i in range(k)},
            compiler_params=pltpu.CompilerParams(has_side_effects=EFFECT),
        )(*self.arrays, self.send, self.recv, after))


def _gather_start(arrs, tag):
    n = len(arrs)
    me = 2 * lax.axis_index("x") + lax.axis_index("y")
    lands = [lax.dynamic_update_index_in_dim(lax.empty((4,) + a.shape, a.dtype), a, me, 0) for a in arrs]

    def copies(refs, send, recv):
        x, y, c, chips = _place()
        return [pltpu.make_async_remote_copy(
            src_ref=refs[k], dst_ref=refs[n + k].at[2 * x + y], send_sem=send.at[3 * k + r],
            recv_sem=recv.at[3 * k + r], device_id=(px, py, c), device_id_type=MESH)
            for k in range(n) for r, (px, py) in enumerate(chips)]

    return _Split("gather_" + tag, list(arrs) + lands, copies, 3 * n)


def _to_sibling_start(gbufs, tag):
    n = len(gbufs)

    def copies(refs, send, recv):
        x, y, c, _ = _place()
        return [pltpu.make_async_remote_copy(
            src_ref=refs[k].at[j, 1 - c], dst_ref=refs[n + k].at[j], send_sem=send.at[4 * k + j],
            recv_sem=recv.at[4 * k + j], device_id=(x, y, 1 - c), device_id_type=MESH)
            for k in range(n) for j in range(4)]

    lands = [lax.empty((4,) + g.shape[2:], g.dtype) for g in gbufs]
    return _Split("rs_sibling_" + tag, list(gbufs) + lands, copies, 4 * n)


def _to_chips_start(pbufs, tag):
    n = len(pbufs)

    def copies(refs, send, recv):
        x, y, c, chips = _place()
        return [pltpu.make_async_remote_copy(
            src_ref=refs[k].at[2 * px + py], dst_ref=refs[n + k].at[2 * x + y], send_sem=send.at[3 * k + r],
            recv_sem=recv.at[3 * k + r], device_id=(px, py, c), device_id_type=MESH)
            for k in range(n) for r, (px, py) in enumerate(chips)]

    return _Split("rs_chips_" + tag, list(pbufs) + [lax.empty(p.shape, p.dtype) for p in pbufs], copies, 3 * n)


def _join_start(fulls, tag):
    def copies(refs, send, recv):
        x, y, c, _ = _place()
        return [pltpu.make_async_remote_copy(
            src_ref=refs[k].at[c], dst_ref=refs[k].at[c], send_sem=send.at[k], recv_sem=recv.at[k],
            device_id=(x, y, 1 - c), device_id_type=MESH) for k in range(len(fulls))]

    return _Split("rs_join_" + tag, list(fulls), copies, len(fulls))


def _all_reduce_small(v):
    R, C = v.shape

    def body(v_ref, o_ref, g_ref, send, recv, loc):
        x, y, c, chips = _place()
        me, sibling = (x, y, c), (x, y, 1 - c)

        def rows(px, py, pc):
            return g_ref.at[4 * px + 2 * py + pc]

        def copy(k, block, to, src=None):
            return pltpu.make_async_remote_copy(
                src_ref=rows(*block) if src is None else src, dst_ref=rows(*block),
                send_sem=send.at[k], recv_sem=recv.at[k], device_id=to, device_id_type=MESH)

        mine = pltpu.make_async_copy(v_ref, rows(*me), loc)
        mine.start()
        first = [copy(0, me, sibling, src=v_ref)]
        first += [copy(1 + j, me, (*chip, c), src=v_ref) for j, chip in enumerate(chips)]
        for cp in first:
            cp.start()
        passed = [copy(4 + j, (*chip, c), sibling) for j, chip in enumerate(chips)]
        for j, chip in enumerate(chips):
            copy(1 + j, (*chip, c), me).wait_recv()
            passed[j].start()
        copy(0, sibling, me).wait_recv()
        for j, chip in enumerate(chips):
            copy(4 + j, (*chip, 1 - c), me).wait_recv()
        for cp in first + passed:
            cp.wait_send()
        mine.wait()
        acc = g_ref[0]
        for d in range(1, 8):
            acc = acc + g_ref[d]
        o_ref[...] = acc

    vm = pl.BlockSpec(memory_space=pltpu.VMEM)
    return pl.pallas_call(
        body, name="all_reduce_small", in_specs=[vm], out_specs=[vm, vm],
        out_shape=[jax.ShapeDtypeStruct((R, C), F32), jax.ShapeDtypeStruct((8, R, C), F32)],
        scratch_shapes=[pltpu.SemaphoreType.DMA((7,)), pltpu.SemaphoreType.DMA((7,)), pltpu.SemaphoreType.DMA],
    )(v)[0]


WEIGHTS = ['ffn1_norm', 'ffn1_w_gate', 'ffn1_w_up', 'ffn1_w_down', 'mix_norm', 'w_in', 'conv_w', 'conv_b', 'dt_bias',
           'a_log', 'd_skip', 'ssd_norm', 'sgu_ln_g', 'sgu_ln_b', 'sgu_w', 'sgu_b', 'w_out', 'ffn2_norm',
           'ffn2_w_gate', 'ffn2_w_up', 'ffn2_w_down', 'final_norm']
SHARDED = ['ffn1_w_gate', 'ffn1_w_up', 'ffn1_w_down', 'w_in', 'conv_w', 'w_out', 'ffn2_w_gate', 'ffn2_w_up',
           'ffn2_w_down']
SMALL = [n for n in WEIGHTS if n not in SHARDED]
GROUPS = [("ffn1", ["ffn1_w_gate", "ffn1_w_up", "ffn1_w_down"]), ("mix", ["w_in", "conv_w", "w_out"]),
          ("ffn2", ["ffn2_w_gate", "ffn2_w_up", "ffn2_w_down"])]
DEPTH = 2


def _pack_w_in(w):
    return jnp.concatenate([w[..., 0:1152], w[..., 1536:2432], w[..., 1152:1536],
                            jnp.repeat(w[..., 2432:2438], HEAD, axis=-1), w[..., 2438:2950]], axis=-1)


def _unpack_w_in(dq, ds, du):
    return jnp.concatenate([dq, ds[:, 896:1280], ds[:, 0:896], ds[:, 1280::HEAD], du], axis=-1)


def _ffn_fwd(x, g, wg, wu, wd):
    xo, hb, S1, S2, A = _ffn_fwd_k(x, g, wg, wu, wd)
    return xo, (x, hb, S1, S2, A)


def _ffn_bwd(dxo, saved, g, wg, wu, wd):
    x, hb, S1, S2, A = saved
    dx, dg, dG, dU, dyb = _ffn_bwd_k1(dxo, x, g, S1, S2, wg, wu, wd)
    dwg, dwu, dwd = _ffn_bwd_k2(hb, dyb, A, dG, dU)
    return dx, dg, dwg, dwu, dwd


def _mix_fwd(x, P):
    hb = _rms_fwd(x, P["mix_norm"])
    qkv = _mm_nn(hb, P["w_qkv"], out_dtype=BF)
    sin = _mm_nn(hb, P["w_ssd"])
    uv = _mm_nn(hb, P["w_uv"])
    o1, l1 = _attn_fwd(qkv, 1)
    o2, l2 = _attn_fwd(qkv, 4)
    o3, l3 = _attn_fwd(qkv, 16)
    y_att, lse = _attn_combine(o1, o2, o3, l1, l2, l3)
    y_ssd, hprev = _ssd_fwd(sin, *P["ssd"])
    y_sgu = _sgu_fwd(uv, *P["sgu"])
    ycat = jnp.concatenate([y_att, y_ssd, y_sgu], axis=1).astype(BF)
    return _mm_nn(ycat, P["w_out"], res=x), (x, hb, qkv, sin, uv, y_att, lse, hprev, ycat)


def _mix_bwd(dxo, saved, P):
    x, hb, qkv, sin, uv, y_att, lse, hprev, ycat = saved
    dycat = _mm_nt(dxo, P["w_out"])
    dwout = _mm_tn(ycat, dxo)
    dy_att, dy_ssd, dy_sgu = dycat[:, 0:ATT_W], dycat[:, ATT_W:ATT_W + SSD_W], dycat[:, ATT_W + SSD_W:]
    dqkv = _sum_branches([_attn_bwd(qkv, dy_att, y_att, lse, d) for d in DILATIONS])
    dsin, dcw, dcb, dvec = _ssd_bwd(sin, hprev, dy_ssd, *P["ssd"])
    duv, dsw, dsbias, dln = _sgu_bwd(uv, dy_sgu, *P["sgu"])
    dwin = _unpack_w_in(_mm_tn(hb, dqkv), _mm_tn(hb, dsin), _mm_tn(hb, duv))
    dh = _mm_nt(dqkv, P["w_qkv"])
    dh = _mm_nt(dsin, P["w_ssd"], res=dh)
    dh = _mm_nt(duv, P["w_uv"], res=dh)
    dx, dg = _rms_bwd(x, P["mix_norm"], dh, dxo)
    grads = dict(
        mix_norm=dg[0], w_in=dwin, conv_w=dcw, conv_b=dcb[0], dt_bias=dvec[0, ::HEAD], a_log=dvec[1, ::HEAD],
        d_skip=jnp.sum(dvec[2].reshape(6, HEAD), axis=-1), ssd_norm=dvec[3], sgu_ln_g=dln[0], sgu_ln_b=dln[1],
        sgu_w=dsw, sgu_b=jnp.sum(dsbias.reshape(CHUNK, 4, HEAD), axis=-1).T, w_out=dwout)
    return dx, grads


def _halved(g):
    rows = g.size // g.shape[-1]
    return g.reshape(4, 2, rows // 8, g.shape[-1])


def kernel(x, ffn1_norm, ffn1_w_gate, ffn1_w_up, ffn1_w_down, mix_norm, w_in, conv_w, conv_b, dt_bias, a_log, d_skip, ssd_norm, sgu_ln_g, sgu_ln_b, sgu_w, sgu_b, w_out, ffn2_norm, ffn2_w_gate, ffn2_w_up, ffn2_w_down, final_norm, loss_target, m_ffn1_norm, m_ffn1_w_gate, m_ffn1_w_up, m_ffn1_w_down, m_mix_norm, m_w_in, m_conv_w, m_conv_b, m_dt_bias, m_a_log, m_d_skip, m_ssd_norm, m_sgu_ln_g, m_sgu_ln_b, m_sgu_w, m_sgu_b, m_w_out, m_ffn2_norm, m_ffn2_w_gate, m_ffn2_w_up, m_ffn2_w_down, m_final_norm, v_ffn1_norm, v_ffn1_w_gate, v_ffn1_w_up, v_ffn1_w_down, v_mix_norm, v_w_in, v_conv_w, v_conv_b, v_dt_bias, v_a_log, v_d_skip, v_ssd_norm, v_sgu_ln_g, v_sgu_ln_b, v_sgu_w, v_sgu_b, v_w_out, v_ffn2_norm, v_ffn2_w_gate, v_ffn2_w_up, v_ffn2_w_down, v_final_norm):
    given = dict(x=x, ffn1_norm=ffn1_norm, ffn1_w_gate=ffn1_w_gate, ffn1_w_up=ffn1_w_up, ffn1_w_down=ffn1_w_down, mix_norm=mix_norm, w_in=w_in, conv_w=conv_w, conv_b=conv_b, dt_bias=dt_bias, a_log=a_log, d_skip=d_skip, ssd_norm=ssd_norm, sgu_ln_g=sgu_ln_g, sgu_ln_b=sgu_ln_b, sgu_w=sgu_w, sgu_b=sgu_b, w_out=w_out, ffn2_norm=ffn2_norm, ffn2_w_gate=ffn2_w_gate, ffn2_w_up=ffn2_w_up, ffn2_w_down=ffn2_w_down, final_norm=final_norm, loss_target=loss_target, m_ffn1_norm=m_ffn1_norm, m_ffn1_w_gate=m_ffn1_w_gate, m_ffn1_w_up=m_ffn1_w_up, m_ffn1_w_down=m_ffn1_w_down, m_mix_norm=m_mix_norm, m_w_in=m_w_in, m_conv_w=m_conv_w, m_conv_b=m_conv_b, m_dt_bias=m_dt_bias, m_a_log=m_a_log, m_d_skip=m_d_skip, m_ssd_norm=m_ssd_norm, m_sgu_ln_g=m_sgu_ln_g, m_sgu_ln_b=m_sgu_ln_b, m_sgu_w=m_sgu_w, m_sgu_b=m_sgu_b, m_w_out=m_w_out, m_ffn2_norm=m_ffn2_norm, m_ffn2_w_gate=m_ffn2_w_gate, m_ffn2_w_up=m_ffn2_w_up, m_ffn2_w_down=m_ffn2_w_down, m_final_norm=m_final_norm, v_ffn1_norm=v_ffn1_norm, v_ffn1_w_gate=v_ffn1_w_gate, v_ffn1_w_up=v_ffn1_w_up, v_ffn1_w_down=v_ffn1_w_down, v_mix_norm=v_mix_norm, v_w_in=v_w_in, v_conv_w=v_conv_w, v_conv_b=v_conv_b, v_dt_bias=v_dt_bias, v_a_log=v_a_log, v_d_skip=v_d_skip, v_ssd_norm=v_ssd_norm, v_sgu_ln_g=v_sgu_ln_g, v_sgu_ln_b=v_sgu_ln_b, v_sgu_w=v_sgu_w, v_sgu_b=v_sgu_b, v_w_out=v_w_out, v_ffn2_norm=v_ffn2_norm, v_ffn2_w_gate=v_ffn2_w_gate, v_ffn2_w_up=v_ffn2_w_up, v_ffn2_w_down=v_ffn2_w_down, v_final_norm=v_final_norm)
    T = given["x"].shape[0] * given["x"].shape[1]
    D = given["x"].shape[2]
    x0 = given["x"].reshape(T, D)
    tgt = given["loss_target"].reshape(T, D)
    c = lax.axis_index("c")

    bf = {n: given[n].astype(BF) for n in SHARDED if n not in ("w_in", "conv_w")}
    bf["w_in"] = _pack_w_in(given["w_in"]).astype(BF)
    bf["conv_w"] = given["conv_w"]
    gathers = {(i, gname): _gather_start([bf[n][i] for n in names], f"l{i}_{gname}")
               for i in range(DEPTH) for gname, names in GROUPS}
    token = functools.reduce(lambda a, b: a + b, [g.token for g in gathers.values()])

    def gathered(i, gname, after):
        return gathers[(i, gname)].wait(after)[3:]

    def mix_params(i, got):
        win = got[0].reshape(D, W_QKV + W_SSD + W_UV)
        rep = lambda v: jnp.repeat(v, HEAD)[None]
        ssd = (got[1].transpose(1, 0, 2).reshape(4, SSD_CONV_DIM), given["conv_b"][i][None],
               rep(given["dt_bias"][i]), rep(given["a_log"][i]), rep(given["d_skip"][i]), given["ssd_norm"][i][None])
        sgu = (given["sgu_ln_g"][i][None], given["sgu_ln_b"][i][None], given["sgu_w"][i],
               jnp.repeat(given["sgu_b"][i].T, HEAD, axis=1))
        return dict(mix_norm=given["mix_norm"][i][None], w_qkv=win[:, 0:W_QKV], w_ssd=win[:, W_QKV:W_QKV + W_SSD],
                    w_uv=win[:, W_QKV + W_SSD:], w_out=got[2].reshape(-1, D), ssd=ssd, sgu=sgu)

    x = x0
    tape = []
    for i in range(DEPTH):
        P = dict(ffn1=(given["ffn1_norm"][i][None] + (token if i == 0 else 0.0), *gathered(i, "ffn1", x)))
        x, s1 = _ffn_fwd(x, *P["ffn1"])
        P.update(mix_params(i, gathered(i, "mix", x)))
        x, s2 = _mix_fwd(x, P)
        P["ffn2"] = (given["ffn2_norm"][i][None], *gathered(i, "ffn2", x))
        x, s3 = _ffn_fwd(x, *P["ffn2"])
        tape.append((P, s1, s2, s3))
    loss_part, dx, dgf = _final_loss(x, given["final_norm"][None], tgt)

    me = 2 * lax.axis_index("x") + lax.axis_index("y")
    jobs = []

    def rs_begin(i, gname, gd):
        tag = f"l{i}_{gname}"
        names = [n for n in dict(GROUPS)[gname] if n != "conv_w"]
        jobs.append(dict(key=(i, gname), names=names, tag=tag, stage=1,
                         op=_to_sibling_start([_halved(gd[n]) for n in names], tag)))

    def rs_advance(job, after):
        k = len(job["names"])
        if job["stage"] == 1:
            got = job["op"].wait(after)
            job.update(stage=2, op=_to_chips_start([_pair_add(g, l, c) for g, l in zip(got[:k], got[k:])], job["tag"]))
        elif job["stage"] == 2:
            got = job["op"].wait(after)
            job.update(stage=3, op=_join_start([_chip_sum(p, l, me, c) for p, l in zip(got[:k], got[k:])], job["tag"]))
        elif job["stage"] == 3:
            job.update(stage=4, out=dict(zip(job["names"], job["op"].wait(after))))

    def tick(after, begin=None):
        for job in jobs:
            rs_advance(job, after)
        if begin is not None:
            rs_begin(*begin)
        return functools.reduce(lambda a, b: a + b, [j["op"].token for j in jobs if j["stage"] < 4], 0.0)

    grads = [dict() for _ in range(DEPTH)]
    tok = 0.0
    for i in reversed(range(DEPTH)):
        P, s1, s2, s3 = tape[i]
        g = grads[i]
        norm, wg, wu, wd = P["ffn2"]
        dx, dn2, g["ffn2_w_gate"], g["ffn2_w_up"], g["ffn2_w_down"] = _ffn_bwd(dx, s3, norm + tok, wg, wu, wd)
        tok = tick(dx, (i, "ffn2", g))
        dx, gm = _mix_bwd(dx, s2, {**P, "mix_norm": P["mix_norm"] + tok})
        g.update(gm)
        tok = tick(dx, (i, "mix", g))
        norm, wg, wu, wd = P["ffn1"]
        dx, dn1, g["ffn1_w_gate"], g["ffn1_w_up"], g["ffn1_w_down"] = _ffn_bwd(dx, s1, norm + tok, wg, wu, wd)
        tok = tick(dx, (i, "ffn1", g))
        g["ffn1_norm"], g["ffn2_norm"] = dn1[0], dn2[0]
    grad_x = dx.reshape(given["x"].shape)

    order = [n for n in SMALL if n != "final_norm"] + ["final_norm"]
    small = [jnp.stack([grads[i][n] for i in range(DEPTH)]) for n in order[:-1] + ["conv_w"]]
    small = small[:-1] + [dgf[0], small[-1], loss_part[0, 0:1]]
    n_small = sum(s.size for s in small)
    rows_small = -(-n_small // (128 * 8)) * 8

    def flat(arrs):
        fill = rows_small * 128 - sum(a.size for a in arrs)
        return jnp.concatenate([a.reshape(-1) for a in arrs] + [jnp.zeros((fill,), F32)]).reshape(rows_small, 128)

    gsmall = _all_reduce_small(flat(small)).reshape(-1)

    stepped = {}

    def update_arrived(dep):
        out = None
        for job in jobs:
            if job["stage"] == 4 and not job.get("stepped"):
                for n, full in job["out"].items():
                    stepped[n] = _adamw_layer(given[n], full.reshape(given[n].shape[1:]), given["m_" + n],
                                              given["v_" + n], job["key"][0], dep, stepped.get(n))
                    out = stepped[n][0]
                job["stepped"] = True
        return out

    after = dx
    while any(j["stage"] < 4 for j in jobs):
        done = update_arrived(jnp.zeros((8, 128), F32) + tok)
        after = after if done is None else done
        tok = tick(after)
    update_arrived(jnp.zeros((8, 128), F32) + tok)

    grad_w = {n: stepped[n][3] for n in stepped}
    off = 0
    for n in order:
        size = given[n].size
        grad_w[n] = gsmall[off:off + size].reshape(given[n].shape)
        off += size
    cw = gsmall[off:off + 2 * 4 * SSD_CONV_DIM].reshape(DEPTH, 4, SSD_CONV_DIM)
    grad_w["conv_w"] = lax.dynamic_slice_in_dim(cw, me * (SSD_CONV_DIM // 4), SSD_CONV_DIM // 4, axis=2)
    loss = gsmall[off + 2 * 4 * SSD_CONV_DIM]

    delta = {n: stepped[n][0] for n in stepped}
    new_m = {n: stepped[n][1] for n in stepped}
    new_v = {n: stepped[n][2] for n in stepped}
    shp = given["conv_w"].shape
    d, m2, v2 = _adamw(*[a.reshape(shp[0] * shp[1], shp[2])
                         for a in (given["conv_w"], grad_w["conv_w"], given["m_conv_w"], given["v_conv_w"])])
    delta["conv_w"], new_m["conv_w"], new_v["conv_w"] = d.reshape(shp), m2.reshape(shp), v2.reshape(shp)
    packed = [flat([given[pre + n] for n in order]) for pre in ("", "m_", "v_")]
    outs = [o.reshape(-1) for o in _adamw(packed[0], gsmall.reshape(rows_small, 128), packed[1], packed[2])]
    off = 0
    for n in order:
        size = given[n].size
        for dst, o in zip((delta, new_m, new_v), outs):
            dst[n] = o[off:off + size].reshape(given[n].shape)
        off += size

    return (loss, grad_x, *[grad_w[n] for n in WEIGHTS], *[delta[n] for n in WEIGHTS],
            *[new_m[n] for n in WEIGHTS], *[new_v[n] for n in WEIGHTS])
```

```python
import functools
import math

import jax
import jax.numpy as jnp
from jax import lax
from jax.experimental import pallas as pl
from jax.experimental.pallas import tpu as pltpu

F32 = jnp.float32
BF = jnp.bfloat16

RMS_EPS = 1e-6
LN_EPS = 1e-5
SEQ = 2048
CHUNK = 128
N_CHUNK = SEQ // CHUNK
ATT_W = 384
HEAD = 64
SSD_W = 384
SSD_CONV_DIM = 896
SSD_STATE = 128
SGU_W = 256
DILATIONS = (1, 4, 16)
W_QKV = 3 * ATT_W
W_SSD = SSD_CONV_DIM + SSD_W + SSD_W
W_UV = 2 * SGU_W
ADAM_LR = 0.001
ADAM_B1 = 0.9
ADAM_B2 = 0.999
ADAM_EPS = 1e-08
ADAM_WD = 0.01
ADAM_STEP = 10
NEG = -1e30


def _dot(a, b):
    return jnp.dot(a, b, preferred_element_type=F32)


def _dot_nt(a, b):
    return lax.dot_general(a, b, (((1,), (1,)), ((), ())), preferred_element_type=F32)


def _dot_tn(a, b):
    return lax.dot_general(a, b, (((0,), (0,)), ((), ())), preferred_element_type=F32)


def _sigmoid(x):
    return 1.0 / (1.0 + jnp.exp(-x))


def _call(body, *, name, grid, in_specs, out_specs, out_shape, scratch=(), sem=None):
    return pl.pallas_call(
        body, name=name, grid=grid, in_specs=in_specs, out_specs=out_specs, out_shape=out_shape,
        scratch_shapes=list(scratch),
        compiler_params=pltpu.CompilerParams(dimension_semantics=sem),
    )


def _tile(n, want):
    t = min(n, want)
    while n % t:
        t //= 2
    return t


def _rms_fwd(x, g):
    T, D = x.shape
    tm = _tile(T, 512)

    def body(x_ref, g_ref, h_ref):
        xf = x_ref[...]
        r = lax.rsqrt(jnp.mean(xf * xf, axis=-1, keepdims=True) + RMS_EPS)
        h_ref[...] = (xf * r * g_ref[...]).astype(BF)

    return _call(body, name="rms_fwd", grid=(T // tm,),
                 in_specs=[pl.BlockSpec((tm, D), lambda i: (i, 0)), pl.BlockSpec((1, D), lambda i: (0, 0))],
                 out_specs=pl.BlockSpec((tm, D), lambda i: (i, 0)),
                 out_shape=jax.ShapeDtypeStruct((T, D), BF), sem=("parallel",))(x, g)


def _rms_bwd(x, g, dh, dres):
    T, D = x.shape
    tm = _tile(T, 512)

    def body(x_ref, g_ref, dh_ref, dr_ref, dx_ref, dg_ref):
        @pl.when(pl.program_id(0) == 0)
        def _():
            dg_ref[...] = jnp.zeros_like(dg_ref)

        xf = x_ref[...]
        r = lax.rsqrt(jnp.mean(xf * xf, axis=-1, keepdims=True) + RMS_EPS)
        dh_ = dh_ref[...]
        u = dh_ * g_ref[...]
        mu = jnp.mean(u * xf, axis=-1, keepdims=True)
        dx_ref[...] = dr_ref[...] + r * (u - xf * (r * r * mu))
        dg_ref[...] += jnp.sum(dh_ * xf * r, axis=0, keepdims=True)

    row = pl.BlockSpec((tm, D), lambda i: (i, 0))
    vec = pl.BlockSpec((1, D), lambda i: (0, 0))
    return _call(body, name="rms_bwd", grid=(T // tm,), in_specs=[row, vec, row, row], out_specs=[row, vec],
                 out_shape=[jax.ShapeDtypeStruct((T, D), F32), jax.ShapeDtypeStruct((1, D), F32)],
                 sem=("arbitrary",))(x, g, dh, dres)


def _final_loss(x, g, tgt):
    T, D = x.shape
    tm = _tile(T, 512)

    def body(x_ref, g_ref, t_ref, l_ref, dx_ref, dg_ref):
        @pl.when(pl.program_id(0) == 0)
        def _():
            dg_ref[...] = jnp.zeros_like(dg_ref)
            l_ref[...] = jnp.zeros_like(l_ref)

        xf = x_ref[...]
        gg = g_ref[...]
        r = lax.rsqrt(jnp.mean(xf * xf, axis=-1, keepdims=True) + RMS_EPS)
        xn = xf * r
        e = xn * gg - t_ref[...]
        part = 0.5 * jnp.sum(jnp.mean(e * e, axis=-1, keepdims=True), axis=0, keepdims=True)
        l_ref[...] += jnp.broadcast_to(part, l_ref.shape)
        dy = e * (1.0 / D)
        u = dy * gg
        mu = jnp.mean(u * xf, axis=-1, keepdims=True)
        dx_ref[...] = r * (u - xf * (r * r * mu))
        dg_ref[...] += jnp.sum(dy * xn, axis=0, keepdims=True)

    row = pl.BlockSpec((tm, D), lambda i: (i, 0))
    vec = pl.BlockSpec((1, D), lambda i: (0, 0))
    lsp = pl.BlockSpec((1, 128), lambda i: (0, 0))
    return _call(body, name="final_loss", grid=(T // tm,), in_specs=[row, vec, row], out_specs=[lsp, row, vec],
                 out_shape=[jax.ShapeDtypeStruct((1, 128), F32), jax.ShapeDtypeStruct((T, D), F32),
                            jax.ShapeDtypeStruct((1, D), F32)],
                 sem=("arbitrary",))(x, g, tgt)


def _slabs(tm, n=2):
    return [slice(k * tm // n, (k + 1) * tm // n) for k in range(n)] if tm % (16 * n) == 0 else [slice(0, tm)]


def _resident(shape):
    return pl.BlockSpec(shape, lambda *_: (0,) * len(shape), pipeline_mode=pl.Buffered(1))


def _ffn_fwd_k(x, gn, wg, wu, wd):
    T, D = x.shape
    NS, _, Fs = wg.shape
    tm = _tile(T, 512)

    def body(x_ref, gn_ref, wg_ref, wu_ref, wd_ref, o_ref, h_ref, s1_ref, s2_ref, a_ref, hs, acc):
        j = pl.program_id(1)

        @pl.when(j == 0)
        def _():
            xf = x_ref[...]
            r = lax.rsqrt(jnp.mean(xf * xf, axis=-1, keepdims=True) + RMS_EPS)
            hs[...] = (xf * r * gn_ref[...]).astype(BF)
            h_ref[...] = hs[...]
            acc[...] = jnp.zeros_like(acc)

        h = hs[...]
        g = _dot(h, wg_ref[j])
        u = _dot(h, wu_ref[j])
        sg = _sigmoid(g)
        s1 = g * sg
        a = (s1 * u).astype(BF)
        s1_ref[...] = s1.astype(BF)
        s2_ref[...] = (u * (sg * (1.0 + g * (1.0 - sg)))).astype(BF)
        a_ref[...] = a
        acc[...] += _dot(a, wd_ref[j])

        @pl.when(j == NS - 1)
        def _():
            o_ref[...] = x_ref[...] + 0.5 * acc[...]

    row = pl.BlockSpec((tm, D), lambda i, j: (i, 0))
    act = pl.BlockSpec((None, tm, Fs), lambda i, j: (j, i, 0))
    sh = jax.ShapeDtypeStruct((NS, T, Fs), BF)
    return _call(body, name="ffn_fwd", grid=(T // tm, NS),
                 in_specs=[row, pl.BlockSpec((1, D), lambda i, j: (0, 0)), _resident(wg.shape), _resident(wu.shape),
                           _resident(wd.shape)],
                 out_specs=[row, row, act, act, act],
                 out_shape=[jax.ShapeDtypeStruct((T, D), F32), jax.ShapeDtypeStruct((T, D), BF), sh, sh, sh],
                 scratch=[pltpu.VMEM((tm, D), BF), pltpu.VMEM((tm, D), F32)],
                 sem=("parallel", "arbitrary"))(x, gn, wg, wu, wd)


def _ffn_bwd_k1(dxo, x, gn, s1, s2, wg, wu, wd):
    NS, T, Fs = s1.shape
    D = x.shape[1]
    tm = _tile(T, 512)

    def body(dxo_ref, x_ref, gn_ref, s1_ref, s2_ref, wg_ref, wu_ref, wd_ref,
             dx_ref, dgn_ref, dg_ref, du_ref, dy_ref, dys, acc):
        i, j = pl.program_id(0), pl.program_id(1)

        @pl.when((i == 0) & (j == 0))
        def _():
            dgn_ref[...] = jnp.zeros_like(dgn_ref)

        @pl.when(j == 0)
        def _():
            dys[...] = (0.5 * dxo_ref[...]).astype(BF)
            dy_ref[...] = dys[...]
            acc[...] = jnp.zeros_like(acc)

        for rows in _slabs(tm):
            da = _dot_nt(dys[rows, :], wd_ref[j])
            dg = (da * s2_ref[rows, :].astype(F32)).astype(BF)
            du = (da * s1_ref[rows, :].astype(F32)).astype(BF)
            dg_ref[rows, :] = dg
            du_ref[rows, :] = du
            acc[rows, :] += _dot_nt(dg, wg_ref[j]) + _dot_nt(du, wu_ref[j])

        @pl.when(j == NS - 1)
        def _():
            xf = x_ref[...]
            r = lax.rsqrt(jnp.mean(xf * xf, axis=-1, keepdims=True) + RMS_EPS)
            dh = acc[...]
            uu = dh * gn_ref[...]
            mu = jnp.mean(uu * xf, axis=-1, keepdims=True)
            dx_ref[...] = dxo_ref[...] + r * (uu - xf * (r * r * mu))
            dgn_ref[...] += jnp.sum(dh * xf * r, axis=0, keepdims=True)

    row = pl.BlockSpec((tm, D), lambda i, j: (i, 0))
    vec = pl.BlockSpec((1, D), lambda i, j: (0, 0))
    act = pl.BlockSpec((None, tm, Fs), lambda i, j: (j, i, 0))
    sh = jax.ShapeDtypeStruct((NS, T, Fs), BF)
    return _call(body, name="ffn_bwd_x", grid=(T // tm, NS),
                 in_specs=[row, row, vec, act, act, _resident(wg.shape), _resident(wu.shape), _resident(wd.shape)],
                 out_specs=[row, vec, act, act, row],
                 out_shape=[jax.ShapeDtypeStruct((T, D), F32), jax.ShapeDtypeStruct((1, D), F32), sh, sh,
                            jax.ShapeDtypeStruct((T, D), BF)],
                 scratch=[pltpu.VMEM((tm, D), BF), pltpu.VMEM((tm, D), F32)],
                 sem=("arbitrary", "arbitrary"))(dxo, x, gn, s1, s2, wg, wu, wd)


def _ffn_bwd_k2(hb, dyb, a, dg, du):
    NS, T, Fs = a.shape
    D = hb.shape[1]
    tk = _tile(T, 512)

    def body(h_ref, dy_ref, a_ref, dg_ref, du_ref, og_ref, ou_ref, od_ref):
        @pl.when(pl.program_id(1) == 0)
        def _():
            og_ref[...] = jnp.zeros_like(og_ref)
            ou_ref[...] = jnp.zeros_like(ou_ref)
            od_ref[...] = jnp.zeros_like(od_ref)

        h = h_ref[...]
        og_ref[...] += _dot_tn(h, dg_ref[...])
        ou_ref[...] += _dot_tn(h, du_ref[...])
        od_ref[...] += _dot_tn(a_ref[...], dy_ref[...])

    row = pl.BlockSpec((tk, D), lambda j, k: (k, 0))
    act = pl.BlockSpec((None, tk, Fs), lambda j, k: (j, k, 0))
    return _call(body, name="ffn_bwd_w", grid=(NS, T // tk), in_specs=[row, row, act, act, act],
                 out_specs=[pl.BlockSpec((None, D, Fs), lambda j, k: (j, 0, 0))] * 2
                 + [pl.BlockSpec((None, Fs, D), lambda j, k: (j, 0, 0))],
                 out_shape=[jax.ShapeDtypeStruct((NS, D, Fs), F32)] * 2 + [jax.ShapeDtypeStruct((NS, Fs, D), F32)],
                 sem=("parallel", "arbitrary"))(hb, dyb, a, dg, du)


def _mm_nn(a, b, res=None, out_dtype=F32):
    T, K = a.shape
    N = b.shape[1]
    tm = _tile(T, 512)
    tn = N if N <= 2048 else _tile(N, 1024)

    def body(*refs):
        if res is None:
            a_ref, b_ref, o_ref = refs
            o_ref[...] = _dot(a_ref[...], b_ref[...]).astype(out_dtype)
        else:
            a_ref, b_ref, r_ref, o_ref = refs
            o_ref[...] = (r_ref[...] + _dot(a_ref[...], b_ref[...])).astype(out_dtype)

    o = pl.BlockSpec((tm, tn), lambda i, j: (i, j))
    ins = [pl.BlockSpec((tm, K), lambda i, j: (i, 0)), pl.BlockSpec((K, tn), lambda i, j: (0, j))]
    args = [a, b]
    if res is not None:
        ins.append(o)
        args.append(res)
    return _call(body, name="mm_nn", grid=(T // tm, N // tn), in_specs=ins, out_specs=o,
                 out_shape=jax.ShapeDtypeStruct((T, N), out_dtype), sem=("parallel", "parallel"))(*args)


def _mm_nt(a, b, res=None):
    T, K = a.shape
    N = b.shape[0]
    tm = _tile(T, 512)

    def body(*refs):
        if res is None:
            a_ref, b_ref, o_ref = refs
            o_ref[...] = _dot_nt(a_ref[...].astype(BF), b_ref[...])
        else:
            a_ref, b_ref, r_ref, o_ref = refs
            o_ref[...] = r_ref[...] + _dot_nt(a_ref[...].astype(BF), b_ref[...])

    o = pl.BlockSpec((tm, N), lambda i: (i, 0))
    ins = [pl.BlockSpec((tm, K), lambda i: (i, 0)), pl.BlockSpec((N, K), lambda i: (0, 0))]
    args = [a, b]
    if res is not None:
        ins.append(o)
        args.append(res)
    return _call(body, name="mm_nt", grid=(T // tm,), in_specs=ins, out_specs=o,
                 out_shape=jax.ShapeDtypeStruct((T, N), F32), sem=("parallel",))(*args)


def _mm_tn(a, b):
    T, M = a.shape
    N = b.shape[1]
    tk = _tile(T, 512)
    tmm = _tile(M, 512)

    def body(a_ref, b_ref, o_ref):
        @pl.when(pl.program_id(1) == 0)
        def _():
            o_ref[...] = jnp.zeros_like(o_ref)

        o_ref[...] += _dot_tn(a_ref[...].astype(BF), b_ref[...].astype(BF))

    return _call(body, name="mm_tn", grid=(M // tmm, T // tk),
                 in_specs=[pl.BlockSpec((tk, tmm), lambda i, k: (k, i)), pl.BlockSpec((tk, N), lambda i, k: (k, 0))],
                 out_specs=pl.BlockSpec((tmm, N), lambda i, k: (i, 0)),
                 out_shape=jax.ShapeDtypeStruct((M, N), F32), sem=("parallel", "arbitrary"))(a, b)


def _lane_mask(e, width=128):
    return (lax.broadcasted_iota(jnp.int32, (1, width), 1) // HEAD) == e


def _band_mask(n):
    qi = lax.broadcasted_iota(jnp.int32, (CHUNK, 2 * CHUNK), 0)
    kj = lax.broadcasted_iota(jnp.int32, (CHUNK, 2 * CHUNK), 1)
    dist = qi + CHUNK - kj
    return (dist >= 0) & (dist <= CHUNK) & ((kj >= CHUNK) | (n > 0))


def _prev_cur(ref, n):
    cur = pl.multiple_of(n * CHUNK, CHUNK)
    prv = pl.multiple_of(jnp.maximum(n - 1, 0) * CHUNK, CHUNK)
    return jnp.concatenate([ref[pl.ds(prv, CHUNK), :], ref[pl.ds(cur, CHUNK), :]], axis=0), prv, cur


def _attn_fwd(qkv, dil):
    T = qkv.shape[0]
    B, L = T // SEQ, SEQ // dil
    nb = L // CHUNK
    scale = HEAD ** -0.5

    def body(q_ref, k_ref, v_ref, o_ref, l_ref):
        n = pl.program_id(2)
        q = q_ref[...]
        kk, _, _ = _prev_cur(k_ref, n)
        vv, _, _ = _prev_cur(v_ref, n)
        mask = _band_mask(n)
        for t in range(ATT_W // 128):
            sl = slice(128 * t, 128 * (t + 1))
            qt, kt, vt = q[:, sl], kk[:, sl], vv[:, sl]
            o_pair = jnp.zeros((CHUNK, 128), F32)
            l_pair = jnp.zeros((CHUNK, 128), F32)
            for e in range(2):
                lm = _lane_mask(e)
                s = _dot_nt(jnp.where(lm, qt, jnp.zeros_like(qt)), kt) * scale
                s = jnp.where(mask, s, NEG)
                m = jnp.max(s, axis=-1, keepdims=True)
                p = jnp.exp(s - m)
                den = jnp.sum(p, axis=-1, keepdims=True)
                o = _dot(p.astype(BF), vt) / den
                o_pair = jnp.where(lm, o, o_pair)
                l_pair = jnp.where(lm, m + jnp.log(den), l_pair)
            o_ref[:, sl] = o_pair
            l_ref[:, sl] = l_pair

    qv = qkv.reshape(B, L, dil * W_QKV)
    o = pl.BlockSpec((None, CHUNK, ATT_W), lambda b, r, n: (b, n, r))
    sh = jax.ShapeDtypeStruct((B, L, dil * ATT_W), F32)
    out, lse = _call(
        body, name=f"attn_fwd_d{dil}", grid=(B, dil, nb),
        in_specs=[pl.BlockSpec((None, CHUNK, ATT_W), lambda b, r, n: (b, n, 3 * r)),
                  pl.BlockSpec((None, L, ATT_W), lambda b, r, n: (b, 0, 3 * r + 1)),
                  pl.BlockSpec((None, L, ATT_W), lambda b, r, n: (b, 0, 3 * r + 2))],
        out_specs=[o, o], out_shape=[sh, sh], sem=("parallel", "parallel", "parallel"))(qv, qv, qv)
    return out.reshape(T, ATT_W), lse.reshape(T, ATT_W)


def _attn_combine(o1, o2, o3, l1, l2, l3):
    T = o1.shape[0]
    tm = _tile(T, 512)

    def body(o1_ref, o2_ref, o3_ref, l1_ref, l2_ref, l3_ref, y_ref, l_ref):
        a, b, c = l1_ref[...], l2_ref[...], l3_ref[...]
        m = jnp.maximum(jnp.maximum(a, b), c)
        ea, eb, ec = jnp.exp(a - m), jnp.exp(b - m), jnp.exp(c - m)
        z = ea + eb + ec
        y_ref[...] = (ea * o1_ref[...] + eb * o2_ref[...] + ec * o3_ref[...]) / z
        l_ref[...] = m + jnp.log(z)

    row = pl.BlockSpec((tm, ATT_W), lambda i: (i, 0))
    sh = jax.ShapeDtypeStruct((T, ATT_W), F32)
    return _call(body, name="attn_combine", grid=(T // tm,), in_specs=[row] * 6, out_specs=[row, row],
                 out_shape=[sh, sh], sem=("parallel",))(o1, o2, o3, l1, l2, l3)


def _attn_bwd(qkv, do, out, lse, dil):
    T = qkv.shape[0]
    B, L = T // SEQ, SEQ // dil
    nb = L // CHUNK
    scale = HEAD ** -0.5

    def body(q_ref, k_ref, v_ref, do_ref, out_ref, lse_ref, dq_ref, dk_ref, dv_ref):
        n = pl.program_id(2)

        @pl.when(n == 0)
        def _():
            dk_ref[...] = jnp.zeros_like(dk_ref)
            dv_ref[...] = jnp.zeros_like(dv_ref)

        q = q_ref[...]
        kk, prv, cur = _prev_cur(k_ref, n)
        vv, _, _ = _prev_cur(v_ref, n)
        mask = _band_mask(n)
        do_ = do_ref[...]
        dlt = do_ * out_ref[...]
        ls = lse_ref[...]
        for t in range(ATT_W // 128):
            sl = slice(128 * t, 128 * (t + 1))
            qt, kt, vt = q[:, sl], kk[:, sl], vv[:, sl]
            dq_pair = jnp.zeros((CHUNK, 128), F32)
            dk_acc = jnp.zeros((2 * CHUNK, 128), F32)
            dv_acc = jnp.zeros((2 * CHUNK, 128), F32)
            for e in range(2):
                lm = _lane_mask(e)
                qm = jnp.where(lm, qt, jnp.zeros_like(qt))
                s = _dot_nt(qm, kt) * scale
                lse_col = ls[:, 128 * t + HEAD * e:128 * t + HEAD * e + 1]
                p = jnp.exp(jnp.where(mask, s - lse_col, NEG))
                dom = jnp.where(lm, do_[:, sl], 0.0).astype(BF)
                dv_acc += _dot_tn(p.astype(BF), dom)
                dp = _dot_nt(dom, vt)
                delta = jnp.sum(jnp.where(lm, dlt[:, sl], 0.0), axis=-1, keepdims=True)
                ds = (p * (dp - delta) * scale).astype(BF)
                dq_pair += jnp.where(lm, _dot(ds, kt), 0.0)
                dk_acc += _dot_tn(ds, qm)
            dq_ref[:, sl] = dq_pair
            dk_ref[pl.ds(cur, CHUNK), sl] += dk_acc[CHUNK:]
            dk_ref[pl.ds(prv, CHUNK), sl] += dk_acc[:CHUNK]
            dv_ref[pl.ds(cur, CHUNK), sl] += dv_acc[CHUNK:]
            dv_ref[pl.ds(prv, CHUNK), sl] += dv_acc[:CHUNK]

    qv = qkv.reshape(B, L, dil * W_QKV)
    view = lambda a: a.reshape(B, L, dil * ATT_W)
    blk = pl.BlockSpec((None, CHUNK, ATT_W), lambda b, r, n: (b, n, r))
    whole = pl.BlockSpec((None, L, ATT_W), lambda b, r, n: (b, 0, r))
    sh = jax.ShapeDtypeStruct((B, L, dil * ATT_W), F32)
    dq, dk, dv = _call(
        body, name=f"attn_bwd_d{dil}", grid=(B, dil, nb),
        in_specs=[pl.BlockSpec((None, CHUNK, ATT_W), lambda b, r, n: (b, n, 3 * r)),
                  pl.BlockSpec((None, L, ATT_W), lambda b, r, n: (b, 0, 3 * r + 1)),
                  pl.BlockSpec((None, L, ATT_W), lambda b, r, n: (b, 0, 3 * r + 2)),
                  blk, blk, blk],
        out_specs=[blk, whole, whole], out_shape=[sh, sh, sh],
        sem=("parallel", "parallel", "arbitrary"))(qv, qv, qv, view(do), view(out), view(lse))
    return dq.reshape(T, ATT_W), dk.reshape(T, ATT_W), dv.reshape(T, ATT_W)


def _sum_branches(parts):
    T = parts[0][0].shape[0]
    tm = _tile(T, 512)

    def body(*refs):
        o_ref = refs[-1]
        for s in range(3):
            acc = refs[s][...] + refs[3 + s][...] + refs[6 + s][...]
            o_ref[:, ATT_W * s:ATT_W * (s + 1)] = acc.astype(BF)

    row = pl.BlockSpec((tm, ATT_W), lambda i: (i, 0))
    flat = [a for tr in parts for a in tr]
    return _call(body, name="attn_sum_branches", grid=(T // tm,), in_specs=[row] * 9,
                 out_specs=pl.BlockSpec((tm, W_QKV), lambda i: (i, 0)),
                 out_shape=jax.ShapeDtypeStruct((T, W_QKV), BF), sem=("parallel",))(*flat)


def _silu(x):
    return x * _sigmoid(x)


def _dsilu(x):
    s = _sigmoid(x)
    return s * (1.0 + x * (1.0 - s))


def _log1p(u):
    return jnp.where(u < 0.01, u * (1.0 - u * (0.5 - u * (1.0 / 3.0))), jnp.log(1.0 + u))


def _softplus(x):
    return jnp.maximum(x, 0.0) + _log1p(jnp.exp(-jnp.abs(x)))


def _cumsum_rows(x, reverse=False):
    n = x.shape[0]
    rows = lax.broadcasted_iota(jnp.int32, x.shape, 0)
    k = 1
    while k < n:
        if reverse:
            x = x + jnp.where(rows < n - k, pltpu.roll(x, n - k, 0), 0.0)
        else:
            x = x + jnp.where(rows >= k, pltpu.roll(x, k, 0), 0.0)
        k *= 2
    return x


def _tri():
    r = lax.broadcasted_iota(jnp.int32, (CHUNK, CHUNK), 0)
    c = lax.broadcasted_iota(jnp.int32, (CHUNK, CHUNK), 1)
    return r >= c


def _row_mask(e):
    return (lax.broadcasted_iota(jnp.int32, (128, 1), 0) // HEAD) == e


def _first_lane(e):
    return lax.broadcasted_iota(jnp.int32, (1, 128), 1) == HEAD * e


def _ssd_pre(x_ref, halo_ref, first, cw_ref, cb_ref, dtb_ref, al_ref, ext):
    row = x_ref[...]
    z = row[:, SSD_CONV_DIM:SSD_CONV_DIM + SSD_W]
    u = row[:, SSD_CONV_DIM + SSD_W:] + dtb_ref[...]
    ext[0:8, :] = jnp.where(first, 0.0, halo_ref[:, 0:SSD_CONV_DIM])
    ext[8:8 + CHUNK, :] = row[:, 0:SSD_CONV_DIM]
    xc = cb_ref[...]
    for j in range(4):
        xc = xc + cw_ref[j:j + 1, :] * ext[pl.ds(5 + j, CHUNK), :]
    xa = _silu(xc)
    dt = _softplus(u)
    a = dt * (-jnp.exp(al_ref[...]))
    A = _cumsum_rows(a)
    return dict(z=z, u=u, xc=xc, xs=xa[:, 0:SSD_W], Bm=xa[:, SSD_W:SSD_W + 256], Cm=xa[:, SSD_W + 256:],
                dt=dt, a=a, A=A, AT=A.T, eA=jnp.exp(A), wdec=jnp.exp(A[CHUNK - 1:CHUNK, :] - A),
                dtot=jnp.exp(A[CHUNK - 1:CHUNK, :]))


def _ssd_y(p, hp_ref, dskip):
    tri = _tri()
    X = p["xs"] * p["dt"]
    Bb = [p["Bm"][:, 128 * g:128 * (g + 1)].astype(BF) for g in range(2)]
    Cb = [p["Cm"][:, 128 * g:128 * (g + 1)].astype(BF) for g in range(2)]
    CB = [_dot_nt(Cb[g], Bb[g]) for g in range(2)]
    tiles = []
    for t in range(3):
        sl = slice(128 * t, 128 * (t + 1))
        hpb = hp_ref[sl, :].astype(BF)
        acc = jnp.zeros((CHUNK, 128), F32)
        for e in range(2):
            h = 2 * t + e
            g, col = h // 3, HEAD * h
            lm = _lane_mask(e)
            L = jnp.exp(jnp.where(tri, p["A"][:, col:col + 1] - p["AT"][col:col + 1, :], NEG))
            yd = _dot((CB[g] * L).astype(BF), jnp.where(lm, X[:, sl], 0.0).astype(BF))
            yo = _dot_nt(Cb[g], hpb) * p["eA"][:, sl]
            acc = acc + jnp.where(lm, yd + yo, 0.0)
        tiles.append(acc)
    return jnp.concatenate(tiles, axis=1) + dskip * p["xs"], X, Bb, Cb, CB


def _group_stats(v):
    g0 = lax.broadcasted_iota(jnp.int32, (1, SSD_W), 1) < SSD_W // 2
    m0 = jnp.sum(jnp.where(g0, v, 0.0), axis=-1, keepdims=True) * (2.0 / SSD_W)
    m1 = jnp.sum(jnp.where(g0, 0.0, v), axis=-1, keepdims=True) * (2.0 / SSD_W)
    return jnp.where(g0, m0, m1)


def _ssd_specs(T, rev):
    B = T // SEQ

    def chunk(b, c):
        return b * N_CHUNK + (N_CHUNK - 1 - c if rev else c)

    row = pl.BlockSpec((CHUNK, W_SSD), lambda b, c: (chunk(b, c), 0))
    halo = pl.BlockSpec((8, W_SSD), lambda b, c: (jnp.maximum(chunk(b, c) * (CHUNK // 8) - 1, 0), 0))
    hp = pl.BlockSpec((None, SSD_W, SSD_STATE), lambda b, c: (chunk(b, c), 0, 0))
    y = pl.BlockSpec((CHUNK, SSD_W), lambda b, c: (chunk(b, c), 0))
    const = lambda r, w: pl.BlockSpec((r, w), lambda b, c: (0, 0))
    params = [const(4, SSD_CONV_DIM), const(1, SSD_CONV_DIM)] + [const(1, SSD_W)] * 4
    return B, row, halo, hp, y, const, params


def _ssd_fwd(sin, conv_w, conv_b, dtb, alog, dskip, norm_g):
    T = sin.shape[0]
    B, row, halo, hp, y, const, params = _ssd_specs(T, False)

    def body(x_ref, halo_ref, cw_ref, cb_ref, dtb_ref, al_ref, dk_ref, ng_ref, y_ref, hp_ref, ext, hst):
        c = pl.program_id(1)

        @pl.when(c == 0)
        def _():
            hst[...] = jnp.zeros_like(hst)

        p = _ssd_pre(x_ref, halo_ref, c == 0, cw_ref, cb_ref, dtb_ref, al_ref, ext)
        yv, X, Bb, Cb, CB = _ssd_y(p, hst, dk_ref[...])
        hp_ref[...] = hst[...]
        for t in range(3):
            sl = slice(128 * t, 128 * (t + 1))
            old = hst[sl, :]
            new = old
            for e in range(2):
                h = 2 * t + e
                g, col = h // 3, HEAD * h
                st = _dot_tn(jnp.where(_lane_mask(e), X[:, sl] * p["wdec"][:, sl], 0.0).astype(BF), Bb[g])
                new = jnp.where(_row_mask(e), old * p["dtot"][:, col:col + 1] + st, new)
            hst[sl, :] = new
        y2 = yv * _silu(p["z"])
        r = lax.rsqrt(_group_stats(y2 * y2) + RMS_EPS)
        y_ref[...] = y2 * r * ng_ref[...]

    return _call(body, name="ssd_fwd", grid=(B, N_CHUNK), in_specs=[row, halo] + params, out_specs=[y, hp],
                 out_shape=[jax.ShapeDtypeStruct((T, SSD_W), F32),
                            jax.ShapeDtypeStruct((T // CHUNK, SSD_W, SSD_STATE), F32)],
                 scratch=[pltpu.VMEM((8 + CHUNK, SSD_CONV_DIM), F32), pltpu.VMEM((SSD_W, SSD_STATE), F32)],
                 sem=("parallel", "arbitrary"))(sin, sin, conv_w, conv_b, dtb, alog, dskip, norm_g)


def _ssd_bwd(sin, hprev, dy3, conv_w, conv_b, dtb, alog, dskip, norm_g):
    T = sin.shape[0]
    B, row, halo, hp, y, const, params = _ssd_specs(T, True)

    def body(x_ref, halo_ref, hp_ref, dy_ref, cw_ref, cb_ref, dtb_ref, al_ref, dk_ref, ng_ref,
             dx_ref, dcw_ref, dcb_ref, dvec_ref, ext, ext2, dh):
        c = pl.program_id(1)

        @pl.when((pl.program_id(0) == 0) & (c == 0))
        def _():
            dcw_ref[...] = jnp.zeros_like(dcw_ref)
            dcb_ref[...] = jnp.zeros_like(dcb_ref)
            dvec_ref[...] = jnp.zeros_like(dvec_ref)

        @pl.when(c == 0)
        def _():
            dh[...] = jnp.zeros_like(dh)
            ext2[CHUNK:CHUNK + 8, :] = jnp.zeros((8, SSD_CONV_DIM), F32)

        p = _ssd_pre(x_ref, halo_ref, c == N_CHUNK - 1, cw_ref, cb_ref, dtb_ref, al_ref, ext)
        dskip_ = dk_ref[...]
        yv, X, Bb, Cb, CB = _ssd_y(p, hp_ref, dskip_)
        xs, z, A, AT = p["xs"], p["z"], p["A"], p["AT"]

        sz = _silu(z)
        y2 = yv * sz
        r = lax.rsqrt(_group_stats(y2 * y2) + RMS_EPS)
        dy3_ = dy_ref[...]
        uu = dy3_ * ng_ref[...]
        dy2 = r * (uu - y2 * (r * r * _group_stats(uu * y2)))
        dy = dy2 * sz
        dz = dy2 * yv * _dsilu(z)

        tri = _tri()
        rows = lax.broadcasted_iota(jnp.int32, (CHUNK, 1), 0)
        dG = [jnp.zeros((CHUNK, CHUNK), F32) for _ in range(2)]
        dB = [jnp.zeros((CHUNK, SSD_STATE), F32) for _ in range(2)]
        dC = [jnp.zeros((CHUNK, SSD_STATE), F32) for _ in range(2)]
        dX_t, dA_t, ddtx_t = [], [], []
        for t in range(3):
            sl = slice(128 * t, 128 * (t + 1))
            hp_t = hp_ref[sl, :]
            hpb = hp_t.astype(BF)
            dhc = dh[sl, :]
            dh_new = jnp.zeros((128, SSD_STATE), F32)
            dX = jnp.zeros((CHUNK, 128), F32)
            dA = jnp.zeros((CHUNK, 128), F32)
            ddtx = jnp.zeros((CHUNK, 128), F32)
            for e in range(2):
                h = 2 * t + e
                g, col = h // 3, HEAD * h
                lm, rm, fl = _lane_mask(e), _row_mask(e), _first_lane(e)
                L = jnp.exp(jnp.where(tri, A[:, col:col + 1] - AT[col:col + 1, :], NEG))
                Mf = CB[g] * L
                Xm = jnp.where(lm, X[:, sl], 0.0)
                Xmb = Xm.astype(BF)
                dyh = jnp.where(lm, dy[:, sl], 0.0)
                dyb = dyh.astype(BF)
                dXh = _dot_tn(Mf.astype(BF), dyb)
                dM = jnp.where(tri, _dot_nt(dyb, Xmb), 0.0)
                Wm = dM * Mf
                dAc = jnp.sum(Wm, axis=-1, keepdims=True) - jnp.sum(Wm.T, axis=-1, keepdims=True)
                dG[g] = dG[g] + dM * L
                eAt = p["eA"][:, sl]
                yo = _dot_nt(Cb[g], hpb)
                dyo = (dyh * eAt).astype(BF)
                dC[g] = dC[g] + _dot(dyo, hpb)
                dh_new = dh_new + _dot_tn(dyo, Cb[g])
                dAc = dAc + jnp.sum(dyh * yo * eAt, axis=-1, keepdims=True)
                dHn = jnp.where(rm, dhc, 0.0)
                dHnb = dHn.astype(BF)
                dec = p["dtot"][:, col:col + 1]
                dh_new = dh_new + dec * dHn
                Z = _dot_nt(Bb[g], dHnb)
                wt = p["wdec"][:, sl]
                xi = jnp.sum(Xm * Z, axis=-1, keepdims=True) * p["wdec"][:, col:col + 1]
                dXh = dXh + wt * Z
                dB[g] = dB[g] + _dot(jnp.where(lm, X[:, sl] * wt, 0.0).astype(BF), dHnb)
                dAtot = jnp.sum(xi, axis=0, keepdims=True) + dec * jnp.sum(
                    jnp.sum(dHn * hp_t, axis=-1, keepdims=True), axis=0, keepdims=True)
                dAc = dAc - xi + jnp.where(rows == CHUNK - 1, dAtot, 0.0)
                dA = dA + jnp.where(fl, dAc, 0.0)
                dX = dX + dXh
                ddtx = ddtx + jnp.where(fl, jnp.sum(dXh * xs[:, sl], axis=-1, keepdims=True), 0.0)
            dh[sl, :] = dh_new
            dX_t.append(dX)
            dA_t.append(dA)
            ddtx_t.append(ddtx)
        for g in range(2):
            dGb = dG[g].astype(BF)
            dC[g] = dC[g] + _dot(dGb, Bb[g])
            dB[g] = dB[g] + _dot_tn(dGb, Cb[g])
        dXf = jnp.concatenate(dX_t, axis=1)
        da = _cumsum_rows(jnp.concatenate(dA_t, axis=1), reverse=True)
        ddt = da * (-jnp.exp(al_ref[...])) + jnp.concatenate(ddtx_t, axis=1)
        du = ddt * _sigmoid(p["u"])
        dxs = dXf * p["dt"] + dskip_ * dy
        dxc = jnp.concatenate([dxs, dB[0], dB[1], dC[0], dC[1]], axis=1) * _dsilu(p["xc"])
        ext2[0:CHUNK, :] = dxc
        dxbc = jnp.zeros((CHUNK, SSD_CONV_DIM), F32)
        for j in range(4):
            dxbc = dxbc + cw_ref[j:j + 1, :] * ext2[pl.ds(3 - j, CHUNK), :]
            dcw_ref[j:j + 1, :] += jnp.sum(dxc * ext[pl.ds(5 + j, CHUNK), :], axis=0, keepdims=True)
        ext2[CHUNK:CHUNK + 8, :] = dxc[0:8, :]
        dcb_ref[...] += jnp.sum(dxc, axis=0, keepdims=True)
        dvec_ref[0:1, :] += jnp.sum(du, axis=0, keepdims=True)
        dvec_ref[1:2, :] += jnp.sum(da * p["a"], axis=0, keepdims=True)
        dvec_ref[2:3, :] += jnp.sum(dy * xs, axis=0, keepdims=True)
        dvec_ref[3:4, :] += jnp.sum(dy3_ * y2 * r, axis=0, keepdims=True)
        dx_ref[...] = jnp.concatenate([dxbc, dz, du], axis=1).astype(BF)

    return _call(body, name="ssd_bwd", grid=(B, N_CHUNK), in_specs=[row, halo, hp, y] + params,
                 out_specs=[row, const(4, SSD_CONV_DIM), const(1, SSD_CONV_DIM), const(8, SSD_W)],
                 out_shape=[jax.ShapeDtypeStruct((T, W_SSD), BF), jax.ShapeDtypeStruct((4, SSD_CONV_DIM), F32),
                            jax.ShapeDtypeStruct((1, SSD_CONV_DIM), F32), jax.ShapeDtypeStruct((8, SSD_W), F32)],
                 scratch=[pltpu.VMEM((8 + CHUNK, SSD_CONV_DIM), F32), pltpu.VMEM((8 + CHUNK, SSD_CONV_DIM), F32),
                          pltpu.VMEM((SSD_W, SSD_STATE), F32)],
                 sem=("arbitrary", "arbitrary"))(sin, sin, hprev, dy3, conv_w, conv_b, dtb, alog, dskip, norm_g)


def _sgu_core(uv_ref, g_ref, b_ref, w_ref, bias_ref):
    x = uv_ref[...]
    cdf = 0.5 * (1.0 + lax.erf(x * (2.0 ** -0.5)))
    ge = x * cdf
    dge = cdf + x * jnp.exp(-0.5 * x * x) * ((2.0 * math.pi) ** -0.5)
    u, v = ge[:, 0:SGU_W], ge[:, SGU_W:]
    vc = v - jnp.mean(v, axis=-1, keepdims=True)
    rstd = lax.rsqrt(jnp.mean(vc * vc, axis=-1, keepdims=True) + LN_EPS)
    vhat = vc * rstd
    vn = vhat * g_ref[...] + b_ref[...]
    tri = _tri()
    wc = [jnp.where(tri, w_ref[gi], 0.0).astype(BF) for gi in range(4)]
    vm = [jnp.where(_lane_mask(gi % 2), vn[:, 128 * (gi // 2):128 * (gi // 2 + 1)], 0.0).astype(BF) for gi in range(4)]
    mixed = jnp.concatenate([_dot(wc[2 * t], vm[2 * t]) + _dot(wc[2 * t + 1], vm[2 * t + 1]) for t in range(2)],
                            axis=1) + bias_ref[...]
    return dict(dge=dge, u=u, rstd=rstd, vhat=vhat, wc=wc, vm=vm, mixed=mixed)


def _sgu_specs():
    vec = pl.BlockSpec((1, SGU_W), lambda i: (0, 0))
    return [pl.BlockSpec((CHUNK, W_UV), lambda i: (i, 0)), vec, vec,
            pl.BlockSpec((4, CHUNK, CHUNK), lambda i: (0, 0, 0)), pl.BlockSpec((CHUNK, SGU_W), lambda i: (0, 0))]


def _sgu_fwd(uv, ln_g, ln_b, w, bias):
    T = uv.shape[0]

    def body(uv_ref, g_ref, b_ref, w_ref, bias_ref, y_ref):
        s = _sgu_core(uv_ref, g_ref, b_ref, w_ref, bias_ref)
        y_ref[...] = s["u"] * s["mixed"]

    return _call(body, name="sgu_fwd", grid=(T // CHUNK,), in_specs=_sgu_specs(),
                 out_specs=pl.BlockSpec((CHUNK, SGU_W), lambda i: (i, 0)),
                 out_shape=jax.ShapeDtypeStruct((T, SGU_W), F32), sem=("parallel",))(uv, ln_g, ln_b, w, bias)


def _sgu_bwd(uv, dy, ln_g, ln_b, w, bias):
    T = uv.shape[0]

    def body(uv_ref, dy_ref, g_ref, b_ref, w_ref, bias_ref, dx_ref, dw_ref, dbias_ref, dln_ref):
        @pl.when(pl.program_id(0) == 0)
        def _():
            dw_ref[...] = jnp.zeros_like(dw_ref)
            dbias_ref[...] = jnp.zeros_like(dbias_ref)
            dln_ref[...] = jnp.zeros_like(dln_ref)

        s = _sgu_core(uv_ref, g_ref, b_ref, w_ref, bias_ref)
        dy_ = dy_ref[...]
        du = dy_ * s["mixed"]
        dmix = dy_ * s["u"]
        dbias_ref[...] += dmix
        tri = _tri()
        dvn_t = []
        for t in range(2):
            acc = jnp.zeros((CHUNK, 128), F32)
            for e in range(2):
                gi = 2 * t + e
                dmg = jnp.where(_lane_mask(e), dmix[:, 128 * t:128 * (t + 1)], 0.0).astype(BF)
                acc = acc + _dot_tn(s["wc"][gi], dmg)
                dw_ref[gi] += jnp.where(tri, _dot_nt(dmg, s["vm"][gi]), 0.0)
            dvn_t.append(acc)
        dvn = jnp.concatenate(dvn_t, axis=1)
        dln_ref[0:1, :] += jnp.sum(dvn * s["vhat"], axis=0, keepdims=True)
        dln_ref[1:2, :] += jnp.sum(dvn, axis=0, keepdims=True)
        dvh = dvn * g_ref[...]
        dv = s["rstd"] * (dvh - jnp.mean(dvh, axis=-1, keepdims=True)
                          - s["vhat"] * jnp.mean(dvh * s["vhat"], axis=-1, keepdims=True))
        dx_ref[...] = (jnp.concatenate([du, dv], axis=1) * s["dge"]).astype(BF)

    ins = _sgu_specs()
    return _call(body, name="sgu_bwd", grid=(T // CHUNK,),
                 in_specs=[ins[0], pl.BlockSpec((CHUNK, SGU_W), lambda i: (i, 0))] + ins[1:],
                 out_specs=[pl.BlockSpec((CHUNK, W_UV), lambda i: (i, 0)),
                            pl.BlockSpec((4, CHUNK, CHUNK), lambda i: (0, 0, 0)),
                            pl.BlockSpec((CHUNK, SGU_W), lambda i: (0, 0)), pl.BlockSpec((8, SGU_W), lambda i: (0, 0))],
                 out_shape=[jax.ShapeDtypeStruct((T, W_UV), BF), jax.ShapeDtypeStruct((4, CHUNK, CHUNK), F32),
                            jax.ShapeDtypeStruct((CHUNK, SGU_W), F32), jax.ShapeDtypeStruct((8, SGU_W), F32)],
                 sem=("arbitrary",))(uv, dy, ln_g, ln_b, w, bias)


def _adamw(w, g, m, v):
    R, C = w.shape
    tr = _tile(R, 256) if R % 8 == 0 else R

    def body(w_ref, g_ref, m_ref, v_ref, d_ref, nm_ref, nv_ref):
        g_ = g_ref[...]
        m2 = ADAM_B1 * m_ref[...] + (1.0 - ADAM_B1) * g_
        v2 = ADAM_B2 * v_ref[...] + (1.0 - ADAM_B2) * (g_ * g_)
        m_hat = m2 / (1.0 - ADAM_B1 ** ADAM_STEP)
        v_hat = v2 / (1.0 - ADAM_B2 ** ADAM_STEP)
        d_ref[...] = -ADAM_LR * (m_hat / (jnp.sqrt(v_hat) + ADAM_EPS) + ADAM_WD * w_ref[...])
        nm_ref[...] = m2
        nv_ref[...] = v2

    blk = pl.BlockSpec((tr, C), lambda i: (i, 0))
    sh = jax.ShapeDtypeStruct((R, C), F32)
    return _call(body, name="adamw", grid=(R // tr,), in_specs=[blk] * 4, out_specs=[blk] * 3,
                 out_shape=[sh] * 3, sem=("parallel",))(w, g, m, v)


def _adamw_pair(w, g0, g1, m, v, dep):
    L, R, C = w.shape
    tr = _tile(R, 256 if C <= 1024 else 64)

    def body(w_ref, g0_ref, g1_ref, m_ref, v_ref, dep_ref, d_ref, nm_ref, nv_ref, og_ref):
        g_ = jnp.where(pl.program_id(0) == 0, g0_ref[...], g1_ref[...])
        m2 = ADAM_B1 * m_ref[...] + (1.0 - ADAM_B1) * g_
        v2 = ADAM_B2 * v_ref[...] + (1.0 - ADAM_B2) * (g_ * g_)
        m_hat = m2 / (1.0 - ADAM_B1 ** ADAM_STEP)
        v_hat = v2 / (1.0 - ADAM_B2 ** ADAM_STEP)
        d_ref[...] = -ADAM_LR * (m_hat / (jnp.sqrt(v_hat) + ADAM_EPS) + ADAM_WD * w_ref[...])
        nm_ref[...] = m2
        nv_ref[...] = v2
        og_ref[...] = g_

    lay = pl.BlockSpec((None, tr, C), lambda l, i: (l, i, 0))
    one = pl.BlockSpec((tr, C), lambda l, i: (i, 0))
    return _call(body, name="adamw_pair", grid=(L, R // tr),
                 in_specs=[lay, one, one, lay, lay, pl.BlockSpec((8, 128), lambda l, i: (0, 0))], out_specs=[lay] * 4,
                 out_shape=[jax.ShapeDtypeStruct((L, R, C), F32)] * 4,
                 sem=("parallel", "parallel"))(w, g0, g1, m, v, dep)


def _row_steps(rows):
    return 2 if rows % 32 == 0 else 1


def _pair_add(gbuf, rsib, c):
    NS, _, R, C = gbuf.shape
    n = _row_steps(R)
    tr = R // n

    def body(c_ref, a_ref, b_ref, o_ref):
        o_ref[...] = (a_ref[...] + b_ref[...]).astype(BF)

    blk = pl.BlockSpec((None, tr, C), lambda j, i, c_ref: (j, i, 0))
    return pl.pallas_call(
        body, name="rs_pair_add",
        grid_spec=pltpu.PrefetchScalarGridSpec(
            num_scalar_prefetch=1, grid=(NS, n),
            in_specs=[pl.BlockSpec((None, None, tr, C), lambda j, i, c_ref: (j, c_ref[0], i, 0)), blk],
            out_specs=blk),
        out_shape=jax.ShapeDtypeStruct((NS, R, C), BF),
        compiler_params=pltpu.CompilerParams(dimension_semantics=("parallel", "parallel")),
    )(jnp.reshape(c, (1,)).astype(jnp.int32), gbuf, rsib)


def _chip_sum(pair, recv, me, c):
    NS, R, C = pair.shape
    n = _row_steps(R)
    tr = R // n

    def body(s_ref, own_ref, p_ref, o_ref):
        p = [jnp.where(s_ref[0] == j, own_ref[...], p_ref[j]).astype(F32) for j in range(4)]
        o_ref[...] = ((p[0] + p[1]) + p[2]) + p[3]

    return pl.pallas_call(
        body, name="rs_chip_sum",
        grid_spec=pltpu.PrefetchScalarGridSpec(
            num_scalar_prefetch=1, grid=(n,),
            in_specs=[pl.BlockSpec((None, tr, C), lambda i, s: (s[0], i, 0)),
                      pl.BlockSpec((NS, tr, C), lambda i, s: (0, i, 0))],
            out_specs=pl.BlockSpec((None, tr, C), lambda i, s: (s[1], i, 0))),
        out_shape=jax.ShapeDtypeStruct((2, R, C), F32),
        compiler_params=pltpu.CompilerParams(dimension_semantics=("parallel",)),
    )(jnp.stack([me, c]).astype(jnp.int32), pair, recv)


MESH = pl.DeviceIdType.MESH
ANY = pl.BlockSpec(memory_space=pl.ANY)


def _place():
    x, y, c = lax.axis_index("x"), lax.axis_index("y"), lax.axis_index("c")
    return x, y, c, [(1 - x, y), (x, 1 - y), (1 - x, 1 - y)]


HBM = pl.BlockSpec(memory_space=pltpu.HBM)
SEM = pl.BlockSpec(memory_space=pltpu.SEMAPHORE)
EFFECT = pltpu.SideEffectType.DATAFLOW_SIDE_EFFECTING


class _Split:
    def __init__(self, tag, arrays, copies, n_copies):
        self.tag, self.copies, k = tag, copies, len(arrays)

        def body(*refs):
            for cp in copies(refs[:k], refs[k], refs[k + 1]):
                cp.start()
            refs[-1][...] = jnp.zeros_like(refs[-1])

        out = pl.pallas_call(
            body, name=tag + "_start",
            out_shape=(pltpu.SemaphoreType.DMA((n_copies,)), pltpu.SemaphoreType.DMA((n_copies,)),
                       *[pltpu.HBM(a.shape, a.dtype) for a in arrays], jax.ShapeDtypeStruct((8, 128), F32)),
            in_specs=[HBM] * k, out_specs=(SEM, SEM, *[HBM] * k, pl.BlockSpec(memory_space=pltpu.VMEM)),
            input_output_aliases={i: 2 + i for i in range(k)},
            compiler_params=pltpu.CompilerParams(has_side_effects=EFFECT),
        )(*[pltpu.with_memory_space_constraint(a, pltpu.HBM) for a in arrays])
        self.send, self.recv, self.arrays, self.token = out[0], out[1], list(out[2:2 + k]), out[-1][0, 0]

    def wait(self, after):
        k, copies = len(self.arrays), self.copies

        def body(*refs):
            for cp in copies(refs[:k], refs[k], refs[k + 1]):
                cp.wait_send()
                cp.wait_recv()

        return list(pl.pallas_call(
            body, name=self.tag + "_wait", out_shape=tuple(pltpu.HBM(a.shape, a.dtype) for a in self.arrays),
            in_specs=[HBM] * k + [SEM, SEM, ANY], out_specs=tuple([HBM] * k),
            input_output_aliases={i: i for i in range(k)},
            compiler_params=pltpu.CompilerParams(has_side_effects=EFFECT),
        )(*self.arrays, self.send, self.recv, after))


def _gather_start(arrs, tag):
    n = len(arrs)
    me = 2 * lax.axis_index("x") + lax.axis_index("y")
    lands = [lax.dynamic_update_index_in_dim(lax.empty((4,) + a.shape, a.dtype), a, me, 0) for a in arrs]

    def copies(refs, send, recv):
        x, y, c, chips = _place()
        return [pltpu.make_async_remote_copy(
            src_ref=refs[k], dst_ref=refs[n + k].at[2 * x + y], send_sem=send.at[3 * k + r],
            recv_sem=recv.at[3 * k + r], device_id=(px, py, c), device_id_type=MESH)
            for k in range(n) for r, (px, py) in enumerate(chips)]

    return _Split("gather_" + tag, list(arrs) + lands, copies, 3 * n)


def _to_sibling_start(gbufs, tag):
    n = len(gbufs)

    def copies(refs, send, recv):
        x, y, c, _ = _place()
        return [pltpu.make_async_remote_copy(
            src_ref=refs[k].at[j, 1 - c], dst_ref=refs[n + k].at[j], send_sem=send.at[4 * k + j],
            recv_sem=recv.at[4 * k + j], device_id=(x, y, 1 - c), device_id_type=MESH)
            for k in range(n) for j in range(4)]

    lands = [lax.empty((4,) + g.shape[2:], g.dtype) for g in gbufs]
    return _Split("rs_sibling_" + tag, list(gbufs) + lands, copies, 4 * n)


def _to_chips_start(pbufs, tag):
    n = len(pbufs)

    def copies(refs, send, recv):
        x, y, c, chips = _place()
        return [pltpu.make_async_remote_copy(
            src_ref=refs[k].at[2 * px + py], dst_ref=refs[n + k].at[2 * x + y], send_sem=send.at[3 * k + r],
            recv_sem=recv.at[3 * k + r], device_id=(px, py, c), device_id_type=MESH)
            for k in range(n) for r, (px, py) in enumerate(chips)]

    return _Split("rs_chips_" + tag, list(pbufs) + [lax.empty(p.shape, p.dtype) for p in pbufs], copies, 3 * n)


def _join_start(fulls, tag):
    def copies(refs, send, recv):
        x, y, c, _ = _place()
        return [pltpu.make_async_remote_copy(
            src_ref=refs[k].at[c], dst_ref=refs[k].at[c], send_sem=send.at[k], recv_sem=recv.at[k],
            device_id=(x, y, 1 - c), device_id_type=MESH) for k in range(len(fulls))]

    return _Split("rs_join_" + tag, list(fulls), copies, len(fulls))


def _all_reduce_small(v):
    R, C = v.shape

    def body(v_ref, o_ref, g_ref, send, recv, loc):
        x, y, c, chips = _place()
        me, sibling = (x, y, c), (x, y, 1 - c)

        def rows(px, py, pc):
            return g_ref.at[4 * px + 2 * py + pc]

        def copy(k, block, to, src=None):
            return pltpu.make_async_remote_copy(
                src_ref=rows(*block) if src is None else src, dst_ref=rows(*block),
                send_sem=send.at[k], recv_sem=recv.at[k], device_id=to, device_id_type=MESH)

        mine = pltpu.make_async_copy(v_ref, rows(*me), loc)
        mine.start()
        first = [copy(0, me, sibling, src=v_ref)]
        first += [copy(1 + j, me, (*chip, c), src=v_ref) for j, chip in enumerate(chips)]
        for cp in first:
            cp.start()
        passed = [copy(4 + j, (*chip, c), sibling) for j, chip in enumerate(chips)]
        for j, chip in enumerate(chips):
            copy(1 + j, (*chip, c), me).wait_recv()
            passed[j].start()
        copy(0, sibling, me).wait_recv()
        for j, chip in enumerate(chips):
            copy(4 + j, (*chip, 1 - c), me).wait_recv()
        for cp in first + passed:
            cp.wait_send()
        mine.wait()
        acc = g_ref[0]
        for d in range(1, 8):
            acc = acc + g_ref[d]
        o_ref[...] = acc

    vm = pl.BlockSpec(memory_space=pltpu.VMEM)
    return pl.pallas_call(
        body, name="all_reduce_small", in_specs=[vm], out_specs=[vm, vm],
        out_shape=[jax.ShapeDtypeStruct((R, C), F32), jax.ShapeDtypeStruct((8, R, C), F32)],
        scratch_shapes=[pltpu.SemaphoreType.DMA((7,)), pltpu.SemaphoreType.DMA((7,)), pltpu.SemaphoreType.DMA],
    )(v)[0]


WEIGHTS = ['ffn1_norm', 'ffn1_w_gate', 'ffn1_w_up', 'ffn1_w_down', 'mix_norm', 'w_in', 'conv_w', 'conv_b', 'dt_bias',
           'a_log', 'd_skip', 'ssd_norm', 'sgu_ln_g', 'sgu_ln_b', 'sgu_w', 'sgu_b', 'w_out', 'ffn2_norm',
           'ffn2_w_gate', 'ffn2_w_up', 'ffn2_w_down', 'final_norm']
SHARDED = ['ffn1_w_gate', 'ffn1_w_up', 'ffn1_w_down', 'w_in', 'conv_w', 'w_out', 'ffn2_w_gate', 'ffn2_w_up',
           'ffn2_w_down']
SMALL = [n for n in WEIGHTS if n not in SHARDED]
GROUPS = [("ffn1", ["ffn1_w_gate", "ffn1_w_up", "ffn1_w_down"]), ("mix", ["w_in", "conv_w", "w_out"]),
          ("ffn2", ["ffn2_w_gate", "ffn2_w_up", "ffn2_w_down"])]
DEPTH = 2


def _pack_w_in(w):
    return jnp.concatenate([w[..., 0:1152], w[..., 1536:2432], w[..., 1152:1536],
                            jnp.repeat(w[..., 2432:2438], HEAD, axis=-1), w[..., 2438:2950]], axis=-1)


def _unpack_w_in(dq, ds, du):
    return jnp.concatenate([dq, ds[:, 896:1280], ds[:, 0:896], ds[:, 1280::HEAD], du], axis=-1)


def _ffn_fwd(x, g, wg, wu, wd):
    xo, hb, S1, S2, A = _ffn_fwd_k(x, g, wg, wu, wd)
    return xo, (x, hb, S1, S2, A)


def _ffn_bwd(dxo, saved, g, wg, wu, wd):
    x, hb, S1, S2, A = saved
    dx, dg, dG, dU, dyb = _ffn_bwd_k1(dxo, x, g, S1, S2, wg, wu, wd)
    dwg, dwu, dwd = _ffn_bwd_k2(hb, dyb, A, dG, dU)
    return dx, dg, dwg, dwu, dwd


def _mix_fwd(x, P):
    hb = _rms_fwd(x, P["mix_norm"])
    qkv = _mm_nn(hb, P["w_qkv"], out_dtype=BF)
    sin = _mm_nn(hb, P["w_ssd"])
    uv = _mm_nn(hb, P["w_uv"])
    o1, l1 = _attn_fwd(qkv, 1)
    o2, l2 = _attn_fwd(qkv, 4)
    o3, l3 = _attn_fwd(qkv, 16)
    y_att, lse = _attn_combine(o1, o2, o3, l1, l2, l3)
    y_ssd, hprev = _ssd_fwd(sin, *P["ssd"])
    y_sgu = _sgu_fwd(uv, *P["sgu"])
    ycat = jnp.concatenate([y_att, y_ssd, y_sgu], axis=1).astype(BF)
    return _mm_nn(ycat, P["w_out"], res=x), (x, hb, qkv, sin, uv, y_att, lse, hprev, ycat)


def _mix_bwd(dxo, saved, P):
    x, hb, qkv, sin, uv, y_att, lse, hprev, ycat = saved
    dycat = _mm_nt(dxo, P["w_out"])
    dwout = _mm_tn(ycat, dxo)
    dy_att, dy_ssd, dy_sgu = dycat[:, 0:ATT_W], dycat[:, ATT_W:ATT_W + SSD_W], dycat[:, ATT_W + SSD_W:]
    dqkv = _sum_branches([_attn_bwd(qkv, dy_att, y_att, lse, d) for d in DILATIONS])
    dsin, dcw, dcb, dvec = _ssd_bwd(sin, hprev, dy_ssd, *P["ssd"])
    duv, dsw, dsbias, dln = _sgu_bwd(uv, dy_sgu, *P["sgu"])
    dwin = _unpack_w_in(_mm_tn(hb, dqkv), _mm_tn(hb, dsin), _mm_tn(hb, duv))
    dh = _mm_nt(dqkv, P["w_qkv"])
    dh = _mm_nt(dsin, P["w_ssd"], res=dh)
    dh = _mm_nt(duv, P["w_uv"], res=dh)
    dx, dg = _rms_bwd(x, P["mix_norm"], dh, dxo)
    grads = dict(
        mix_norm=dg[0], w_in=dwin, conv_w=dcw, conv_b=dcb[0], dt_bias=dvec[0, ::HEAD], a_log=dvec[1, ::HEAD],
        d_skip=jnp.sum(dvec[2].reshape(6, HEAD), axis=-1), ssd_norm=dvec[3], sgu_ln_g=dln[0], sgu_ln_b=dln[1],
        sgu_w=dsw, sgu_b=jnp.sum(dsbias.reshape(CHUNK, 4, HEAD), axis=-1).T, w_out=dwout)
    return dx, grads


def _halved(g):
    rows = g.size // g.shape[-1]
    return g.reshape(4, 2, rows // 8, g.shape[-1])


def kernel(x, ffn1_norm, ffn1_w_gate, ffn1_w_up, ffn1_w_down, mix_norm, w_in, conv_w, conv_b, dt_bias, a_log, d_skip, ssd_norm, sgu_ln_g, sgu_ln_b, sgu_w, sgu_b, w_out, ffn2_norm, ffn2_w_gate, ffn2_w_up, ffn2_w_down, final_norm, loss_target, m_ffn1_norm, m_ffn1_w_gate, m_ffn1_w_up, m_ffn1_w_down, m_mix_norm, m_w_in, m_conv_w, m_conv_b, m_dt_bias, m_a_log, m_d_skip, m_ssd_norm, m_sgu_ln_g, m_sgu_ln_b, m_sgu_w, m_sgu_b, m_w_out, m_ffn2_norm, m_ffn2_w_gate, m_ffn2_w_up, m_ffn2_w_down, m_final_norm, v_ffn1_norm, v_ffn1_w_gate, v_ffn1_w_up, v_ffn1_w_down, v_mix_norm, v_w_in, v_conv_w, v_conv_b, v_dt_bias, v_a_log, v_d_skip, v_ssd_norm, v_sgu_ln_g, v_sgu_ln_b, v_sgu_w, v_sgu_b, v_w_out, v_ffn2_norm, v_ffn2_w_gate, v_ffn2_w_up, v_ffn2_w_down, v_final_norm):
    given = dict(x=x, ffn1_norm=ffn1_norm, ffn1_w_gate=ffn1_w_gate, ffn1_w_up=ffn1_w_up, ffn1_w_down=ffn1_w_down, mix_norm=mix_norm, w_in=w_in, conv_w=conv_w, conv_b=conv_b, dt_bias=dt_bias, a_log=a_log, d_skip=d_skip, ssd_norm=ssd_norm, sgu_ln_g=sgu_ln_g, sgu_ln_b=sgu_ln_b, sgu_w=sgu_w, sgu_b=sgu_b, w_out=w_out, ffn2_norm=ffn2_norm, ffn2_w_gate=ffn2_w_gate, ffn2_w_up=ffn2_w_up, ffn2_w_down=ffn2_w_down, final_norm=final_norm, loss_target=loss_target, m_ffn1_norm=m_ffn1_norm, m_ffn1_w_gate=m_ffn1_w_gate, m_ffn1_w_up=m_ffn1_w_up, m_ffn1_w_down=m_ffn1_w_down, m_mix_norm=m_mix_norm, m_w_in=m_w_in, m_conv_w=m_conv_w, m_conv_b=m_conv_b, m_dt_bias=m_dt_bias, m_a_log=m_a_log, m_d_skip=m_d_skip, m_ssd_norm=m_ssd_norm, m_sgu_ln_g=m_sgu_ln_g, m_sgu_ln_b=m_sgu_ln_b, m_sgu_w=m_sgu_w, m_sgu_b=m_sgu_b, m_w_out=m_w_out, m_ffn2_norm=m_ffn2_norm, m_ffn2_w_gate=m_ffn2_w_gate, m_ffn2_w_up=m_ffn2_w_up, m_ffn2_w_down=m_ffn2_w_down, m_final_norm=m_final_norm, v_ffn1_norm=v_ffn1_norm, v_ffn1_w_gate=v_ffn1_w_gate, v_ffn1_w_up=v_ffn1_w_up, v_ffn1_w_down=v_ffn1_w_down, v_mix_norm=v_mix_norm, v_w_in=v_w_in, v_conv_w=v_conv_w, v_conv_b=v_conv_b, v_dt_bias=v_dt_bias, v_a_log=v_a_log, v_d_skip=v_d_skip, v_ssd_norm=v_ssd_norm, v_sgu_ln_g=v_sgu_ln_g, v_sgu_ln_b=v_sgu_ln_b, v_sgu_w=v_sgu_w, v_sgu_b=v_sgu_b, v_w_out=v_w_out, v_ffn2_norm=v_ffn2_norm, v_ffn2_w_gate=v_ffn2_w_gate, v_ffn2_w_up=v_ffn2_w_up, v_ffn2_w_down=v_ffn2_w_down, v_final_norm=v_final_norm)
    T = given["x"].shape[0] * given["x"].shape[1]
    D = given["x"].shape[2]
    x0 = given["x"].reshape(T, D)
    tgt = given["loss_target"].reshape(T, D)
    c = lax.axis_index("c")

    bf = {n: given[n].astype(BF) for n in SHARDED if n not in ("w_in", "conv_w")}
    bf["w_in"] = _pack_w_in(given["w_in"]).astype(BF)
    bf["conv_w"] = given["conv_w"]
    gathers = {(i, gname): _gather_start([bf[n][i] for n in names], f"l{i}_{gname}")
               for i in range(DEPTH) for gname, names in GROUPS}
    token = functools.reduce(lambda a, b: a + b, [g.token for g in gathers.values()])

    def gathered(i, gname, after):
        return gathers[(i, gname)].wait(after)[3:]

    def mix_params(i, got):
        win = got[0].reshape(D, W_QKV + W_SSD + W_UV)
        rep = lambda v: jnp.repeat(v, HEAD)[None]
        ssd = (got[1].transpose(1, 0, 2).reshape(4, SSD_CONV_DIM), given["conv_b"][i][None],
               rep(given["dt_bias"][i]), rep(given["a_log"][i]), rep(given["d_skip"][i]), given["ssd_norm"][i][None])
        sgu = (given["sgu_ln_g"][i][None], given["sgu_ln_b"][i][None], given["sgu_w"][i],
               jnp.repeat(given["sgu_b"][i].T, HEAD, axis=1))
        return dict(mix_norm=given["mix_norm"][i][None], w_qkv=win[:, 0:W_QKV], w_ssd=win[:, W_QKV:W_QKV + W_SSD],
                    w_uv=win[:, W_QKV + W_SSD:], w_out=got[2].reshape(-1, D), ssd=ssd, sgu=sgu)

    x = x0
    tape = []
    for i in range(DEPTH):
        P = dict(ffn1=(given["ffn1_norm"][i][None] + (token if i == 0 else 0.0), *gathered(i, "ffn1", x)))
        x, s1 = _ffn_fwd(x, *P["ffn1"])
        P.update(mix_params(i, gathered(i, "mix", x)))
        x, s2 = _mix_fwd(x, P)
        P["ffn2"] = (given["ffn2_norm"][i][None], *gathered(i, "ffn2", x))
        x, s3 = _ffn_fwd(x, *P["ffn2"])
        tape.append((P, s1, s2, s3))
    loss_part, dx, dgf = _final_loss(x, given["final_norm"][None], tgt)

    me = 2 * lax.axis_index("x") + lax.axis_index("y")
    jobs = []

    def rs_begin(i, gname, gd):
        tag = f"l{i}_{gname}"
        names = [n for n in dict(GROUPS)[gname] if n != "conv_w"]
        jobs.append(dict(key=(i, gname), names=names, tag=tag, stage=1,
                         op=_to_sibling_start([_halved(gd[n]) for n in names], tag)))

    def rs_advance(job, after):
        k = len(job["names"])
        if job["stage"] == 1:
            got = job["op"].wait(after)
            job.update(stage=2, op=_to_chips_start([_pair_add(g, l, c) for g, l in zip(got[:k], got[k:])], job["tag"]))
        elif job["stage"] == 2:
            got = job["op"].wait(after)
            job.update(stage=3, op=_join_start([_chip_sum(p, l, me, c) for p, l in zip(got[:k], got[k:])], job["tag"]))
        elif job["stage"] == 3:
            job.update(stage=4, out=dict(zip(job["names"], job["op"].wait(after))))

    def tick(after, begin=None):
        for job in jobs:
            rs_advance(job, after)
        if begin is not None:
            rs_begin(*begin)
        return functools.reduce(lambda a, b: a + b, [j["op"].token for j in jobs if j["stage"] < 4], 0.0)

    grads = [dict() for _ in range(DEPTH)]
    tok = 0.0
    for i in reversed(range(DEPTH)):
        P, s1, s2, s3 = tape[i]
        g = grads[i]
        norm, wg, wu, wd = P["ffn2"]
        dx, dn2, g["ffn2_w_gate"], g["ffn2_w_up"], g["ffn2_w_down"] = _ffn_bwd(dx, s3, norm + tok, wg, wu, wd)
        tok = tick(dx, (i, "ffn2", g))
        dx, gm = _mix_bwd(dx, s2, {**P, "mix_norm": P["mix_norm"] + tok})
        g.update(gm)
        tok = tick(dx, (i, "mix", g))
        norm, wg, wu, wd = P["ffn1"]
        dx, dn1, g["ffn1_w_gate"], g["ffn1_w_up"], g["ffn1_w_down"] = _ffn_bwd(dx, s1, norm + tok, wg, wu, wd)
        tok = tick(dx, (i, "ffn1", g))
        g["ffn1_norm"], g["ffn2_norm"] = dn1[0], dn2[0]
    grad_x = dx.reshape(given["x"].shape)

    order = [n for n in SMALL if n != "final_norm"] + ["final_norm"]
    small = [jnp.stack([grads[i][n] for i in range(DEPTH)]) for n in order[:-1] + ["conv_w"]]
    small = small[:-1] + [dgf[0], small[-1], loss_part[0, 0:1]]
    n_small = sum(s.size for s in small)
    rows_small = -(-n_small // (128 * 8)) * 8

    def flat(arrs):
        fill = rows_small * 128 - sum(a.size for a in arrs)
        return jnp.concatenate([a.reshape(-1) for a in arrs] + [jnp.zeros((fill,), F32)]).reshape(rows_small, 128)

    gsmall = _all_reduce_small(flat(small)).reshape(-1)

    grad_w = {}
    off = 0
    for n in order:
        size = given[n].size
        grad_w[n] = gsmall[off:off + size].reshape(given[n].shape)
        off += size
    cw = gsmall[off:off + 2 * 4 * SSD_CONV_DIM].reshape(DEPTH, 4, SSD_CONV_DIM)
    grad_w["conv_w"] = lax.dynamic_slice_in_dim(cw, me * (SSD_CONV_DIM // 4), SSD_CONV_DIM // 4, axis=2)
    loss = gsmall[off + 2 * 4 * SSD_CONV_DIM]

    delta, new_m, new_v = {}, {}, {}
    shp = given["conv_w"].shape
    d, m2, v2 = _adamw(*[a.reshape(shp[0] * shp[1], shp[2])
                         for a in (given["conv_w"], grad_w["conv_w"], given["m_conv_w"], given["v_conv_w"])])
    delta["conv_w"], new_m["conv_w"], new_v["conv_w"] = d.reshape(shp), m2.reshape(shp), v2.reshape(shp)
    packed = [flat([given[pre + n] for n in order]) for pre in ("", "m_", "v_")]
    small_out = _adamw(packed[0], gsmall.reshape(rows_small, 128), packed[1], packed[2])
    outs = [o.reshape(-1) for o in small_out]
    off = 0
    for n in order:
        size = given[n].size
        for dst, o in zip((delta, new_m, new_v), outs):
            dst[n] = o[off:off + size].reshape(given[n].shape)
        off += size

    stepped, arrived = {}, {}

    def update_arrived(dep):
        out = None
        for job in jobs:
            if job["stage"] == 4 and not job.get("seen"):
                job["seen"] = True
                for n, full in job["out"].items():
                    arrived.setdefault(n, {})[job["key"][0]] = full.reshape(given[n].shape[1:])
                    if len(arrived[n]) == DEPTH:
                        stepped[n] = _adamw_pair(given[n], arrived[n][0], arrived[n][1], given["m_" + n],
                                                 given["v_" + n], dep)
                        out = stepped[n][0]
        return out

    after = small_out[0]
    while any(j["stage"] < 4 for j in jobs):
        done = update_arrived(jnp.zeros((8, 128), F32) + tok)
        after = after if done is None else done
        tok = tick(after)
    update_arrived(jnp.zeros((8, 128), F32) + tok)
    for n, (d, m2, v2, g) in stepped.items():
        delta[n], new_m[n], new_v[n], grad_w[n] = d, m2, v2, g

    return (loss, grad_x, *[grad_w[n] for n in WEIGHTS], *[delta[n] for n in WEIGHTS],
            *[new_m[n] for n in WEIGHTS], *[new_v[n] for n in WEIGHTS])
```

```python
import functools
import math

import jax
import jax.numpy as jnp
from jax import lax
from jax.experimental import pallas as pl
from jax.experimental.pallas import tpu as pltpu

F32 = jnp.float32
BF = jnp.bfloat16

RMS_EPS = 1e-6
LN_EPS = 1e-5
SEQ = 2048
CHUNK = 128
N_CHUNK = SEQ // CHUNK
ATT_W = 384
HEAD = 64
SSD_W = 384
SSD_CONV_DIM = 896
SSD_STATE = 128
SGU_W = 256
DILATIONS = (1, 4, 16)
W_QKV = 3 * ATT_W
W_SSD = SSD_CONV_DIM + SSD_W + SSD_W
W_UV = 2 * SGU_W
ADAM_LR = 0.001
ADAM_B1 = 0.9
ADAM_B2 = 0.999
ADAM_EPS = 1e-08
ADAM_WD = 0.01
ADAM_STEP = 10
NEG = -1e30


def _dot(a, b):
    return jnp.dot(a, b, preferred_element_type=F32)


def _dot_nt(a, b):
    return lax.dot_general(a, b, (((1,), (1,)), ((), ())), preferred_element_type=F32)


def _dot_tn(a, b):
    return lax.dot_general(a, b, (((0,), (0,)), ((), ())), preferred_element_type=F32)


def _sigmoid(x):
    return 1.0 / (1.0 + jnp.exp(-x))


def _call(body, *, name, grid, in_specs, out_specs, out_shape, scratch=(), sem=None):
    return pl.pallas_call(
        body, name=name, grid=grid, in_specs=in_specs, out_specs=out_specs, out_shape=out_shape,
        scratch_shapes=list(scratch),
        compiler_params=pltpu.CompilerParams(dimension_semantics=sem),
    )


def _tile(n, want):
    t = min(n, want)
    while n % t:
        t //= 2
    return t


def _rms_fwd(x, g):
    T, D = x.shape
    tm = _tile(T, 512)

    def body(x_ref, g_ref, h_ref):
        xf = x_ref[...]
        r = lax.rsqrt(jnp.mean(xf * xf, axis=-1, keepdims=True) + RMS_EPS)
        h_ref[...] = (xf * r * g_ref[...]).astype(BF)

    return _call(body, name="rms_fwd", grid=(T // tm,),
                 in_specs=[pl.BlockSpec((tm, D), lambda i: (i, 0)), pl.BlockSpec((1, D), lambda i: (0, 0))],
                 out_specs=pl.BlockSpec((tm, D), lambda i: (i, 0)),
                 out_shape=jax.ShapeDtypeStruct((T, D), BF), sem=("parallel",))(x, g)


def _rms_bwd(x, g, dh, dres):
    T, D = x.shape
    tm = _tile(T, 512)

    def body(x_ref, g_ref, dh_ref, dr_ref, dx_ref, dg_ref):
        @pl.when(pl.program_id(0) == 0)
        def _():
            dg_ref[...] = jnp.zeros_like(dg_ref)

        xf = x_ref[...]
        r = lax.rsqrt(jnp.mean(xf * xf, axis=-1, keepdims=True) + RMS_EPS)
        dh_ = dh_ref[...]
        u = dh_ * g_ref[...]
        mu = jnp.mean(u * xf, axis=-1, keepdims=True)
        dx_ref[...] = dr_ref[...] + r * (u - xf * (r * r * mu))
        dg_ref[...] += jnp.sum(dh_ * xf * r, axis=0, keepdims=True)

    row = pl.BlockSpec((tm, D), lambda i: (i, 0))
    vec = pl.BlockSpec((1, D), lambda i: (0, 0))
    return _call(body, name="rms_bwd", grid=(T // tm,), in_specs=[row, vec, row, row], out_specs=[row, vec],
                 out_shape=[jax.ShapeDtypeStruct((T, D), F32), jax.ShapeDtypeStruct((1, D), F32)],
                 sem=("arbitrary",))(x, g, dh, dres)


def _final_loss(x, g, tgt):
    T, D = x.shape
    tm = _tile(T, 512)

    def body(x_ref, g_ref, t_ref, l_ref, dx_ref, dg_ref):
        @pl.when(pl.program_id(0) == 0)
        def _():
            dg_ref[...] = jnp.zeros_like(dg_ref)
            l_ref[...] = jnp.zeros_like(l_ref)

        xf = x_ref[...]
        gg = g_ref[...]
        r = lax.rsqrt(jnp.mean(xf * xf, axis=-1, keepdims=True) + RMS_EPS)
        xn = xf * r
        e = xn * gg - t_ref[...]
        part = 0.5 * jnp.sum(jnp.mean(e * e, axis=-1, keepdims=True), axis=0, keepdims=True)
        l_ref[...] += jnp.broadcast_to(part, l_ref.shape)
        dy = e * (1.0 / D)
        u = dy * gg
        mu = jnp.mean(u * xf, axis=-1, keepdims=True)
        dx_ref[...] = r * (u - xf * (r * r * mu))
        dg_ref[...] += jnp.sum(dy * xn, axis=0, keepdims=True)

    row = pl.BlockSpec((tm, D), lambda i: (i, 0))
    vec = pl.BlockSpec((1, D), lambda i: (0, 0))
    lsp = pl.BlockSpec((1, 128), lambda i: (0, 0))
    return _call(body, name="final_loss", grid=(T // tm,), in_specs=[row, vec, row], out_specs=[lsp, row, vec],
                 out_shape=[jax.ShapeDtypeStruct((1, 128), F32), jax.ShapeDtypeStruct((T, D), F32),
                            jax.ShapeDtypeStruct((1, D), F32)],
                 sem=("arbitrary",))(x, g, tgt)


def _slabs(tm, n=2):
    return [slice(k * tm // n, (k + 1) * tm // n) for k in range(n)] if tm % (16 * n) == 0 else [slice(0, tm)]


def _resident(shape):
    return pl.BlockSpec(shape, lambda *_: (0,) * len(shape), pipeline_mode=pl.Buffered(1))


def _ffn_fwd_k(x, gn, wg, wu, wd):
    T, D = x.shape
    NS, _, Fs = wg.shape
    tm = _tile(T, 512)

    def body(x_ref, gn_ref, wg_ref, wu_ref, wd_ref, o_ref, h_ref, s1_ref, s2_ref, a_ref, hs, acc):
        j = pl.program_id(1)

        @pl.when(j == 0)
        def _():
            xf = x_ref[...]
            r = lax.rsqrt(jnp.mean(xf * xf, axis=-1, keepdims=True) + RMS_EPS)
            hs[...] = (xf * r * gn_ref[...]).astype(BF)
            h_ref[...] = hs[...]
            acc[...] = jnp.zeros_like(acc)

        h = hs[...]
        g = _dot(h, wg_ref[j])
        u = _dot(h, wu_ref[j])
        sg = _sigmoid(g)
        s1 = g * sg
        a = (s1 * u).astype(BF)
        s1_ref[...] = s1.astype(BF)
        s2_ref[...] = (u * (sg * (1.0 + g * (1.0 - sg)))).astype(BF)
        a_ref[...] = a
        acc[...] += _dot(a, wd_ref[j])

        @pl.when(j == NS - 1)
        def _():
            o_ref[...] = x_ref[...] + 0.5 * acc[...]

    row = pl.BlockSpec((tm, D), lambda i, j: (i, 0))
    act = pl.BlockSpec((None, tm, Fs), lambda i, j: (j, i, 0))
    sh = jax.ShapeDtypeStruct((NS, T, Fs), BF)
    return _call(body, name="ffn_fwd", grid=(T // tm, NS),
                 in_specs=[row, pl.BlockSpec((1, D), lambda i, j: (0, 0)), _resident(wg.shape), _resident(wu.shape),
                           _resident(wd.shape)],
                 out_specs=[row, row, act, act, act],
                 out_shape=[jax.ShapeDtypeStruct((T, D), F32), jax.ShapeDtypeStruct((T, D), BF), sh, sh, sh],
                 scratch=[pltpu.VMEM((tm, D), BF), pltpu.VMEM((tm, D), F32)],
                 sem=("parallel", "arbitrary"))(x, gn, wg, wu, wd)


def _ffn_bwd_k1(dxo, x, gn, s1, s2, wg, wu, wd):
    NS, T, Fs = s1.shape
    D = x.shape[1]
    tm = _tile(T, 512)

    def body(dxo_ref, x_ref, gn_ref, s1_ref, s2_ref, wg_ref, wu_ref, wd_ref,
             dx_ref, dgn_ref, dg_ref, du_ref, dy_ref, dys, acc):
        i, j = pl.program_id(0), pl.program_id(1)

        @pl.when((i == 0) & (j == 0))
        def _():
            dgn_ref[...] = jnp.zeros_like(dgn_ref)

        @pl.when(j == 0)
        def _():
            dys[...] = (0.5 * dxo_ref[...]).astype(BF)
            dy_ref[...] = dys[...]
            acc[...] = jnp.zeros_like(acc)

        for rows in _slabs(tm):
            da = _dot_nt(dys[rows, :], wd_ref[j])
            dg = (da * s2_ref[rows, :].astype(F32)).astype(BF)
            du = (da * s1_ref[rows, :].astype(F32)).astype(BF)
            dg_ref[rows, :] = dg
            du_ref[rows, :] = du
            acc[rows, :] += _dot_nt(dg, wg_ref[j]) + _dot_nt(du, wu_ref[j])

        @pl.when(j == NS - 1)
        def _():
            xf = x_ref[...]
            r = lax.rsqrt(jnp.mean(xf * xf, axis=-1, keepdims=True) + RMS_EPS)
            dh = acc[...]
            uu = dh * gn_ref[...]
            mu = jnp.mean(uu * xf, axis=-1, keepdims=True)
            dx_ref[...] = dxo_ref[...] + r * (uu - xf * (r * r * mu))
            dgn_ref[...] += jnp.sum(dh * xf * r, axis=0, keepdims=True)

    row = pl.BlockSpec((tm, D), lambda i, j: (i, 0))
    vec = pl.BlockSpec((1, D), lambda i, j: (0, 0))
    act = pl.BlockSpec((None, tm, Fs), lambda i, j: (j, i, 0))
    sh = jax.ShapeDtypeStruct((NS, T, Fs), BF)
    return _call(body, name="ffn_bwd_x", grid=(T // tm, NS),
                 in_specs=[row, row, vec, act, act, _resident(wg.shape), _resident(wu.shape), _resident(wd.shape)],
                 out_specs=[row, vec, act, act, row],
                 out_shape=[jax.ShapeDtypeStruct((T, D), F32), jax.ShapeDtypeStruct((1, D), F32), sh, sh,
                            jax.ShapeDtypeStruct((T, D), BF)],
                 scratch=[pltpu.VMEM((tm, D), BF), pltpu.VMEM((tm, D), F32)],
                 sem=("arbitrary", "arbitrary"))(dxo, x, gn, s1, s2, wg, wu, wd)


def _ffn_bwd_k2(hb, dyb, a, dg, du):
    NS, T, Fs = a.shape
    D = hb.shape[1]
    tk = _tile(T, 512)

    def body(h_ref, dy_ref, a_ref, dg_ref, du_ref, og_ref, ou_ref, od_ref):
        @pl.when(pl.program_id(1) == 0)
        def _():
            og_ref[...] = jnp.zeros_like(og_ref)
            ou_ref[...] = jnp.zeros_like(ou_ref)
            od_ref[...] = jnp.zeros_like(od_ref)

        h = h_ref[...]
        og_ref[...] += _dot_tn(dg_ref[...], h)
        ou_ref[...] += _dot_tn(du_ref[...], h)
        od_ref[...] += _dot_tn(a_ref[...], dy_ref[...])

    row = pl.BlockSpec((tk, D), lambda j, k: (k, 0))
    act = pl.BlockSpec((None, tk, Fs), lambda j, k: (j, k, 0))
    return _call(body, name="ffn_bwd_w", grid=(NS, T // tk), in_specs=[row, row, act, act, act],
                 out_specs=[pl.BlockSpec((None, Fs, D), lambda j, k: (j, 0, 0))] * 3,
                 out_shape=[jax.ShapeDtypeStruct((NS, Fs, D), F32)] * 3,
                 sem=("parallel", "arbitrary"))(hb, dyb, a, dg, du)


def _mm_nn(a, b, res=None, out_dtype=F32):
    T, K = a.shape
    N = b.shape[1]
    tm = _tile(T, 512)
    tn = N if N <= 2048 else _tile(N, 1024)

    def body(*refs):
        if res is None:
            a_ref, b_ref, o_ref = refs
            o_ref[...] = _dot(a_ref[...], b_ref[...]).astype(out_dtype)
        else:
            a_ref, b_ref, r_ref, o_ref = refs
            o_ref[...] = (r_ref[...] + _dot(a_ref[...], b_ref[...])).astype(out_dtype)

    o = pl.BlockSpec((tm, tn), lambda i, j: (i, j))
    ins = [pl.BlockSpec((tm, K), lambda i, j: (i, 0)), pl.BlockSpec((K, tn), lambda i, j: (0, j))]
    args = [a, b]
    if res is not None:
        ins.append(o)
        args.append(res)
    return _call(body, name="mm_nn", grid=(T // tm, N // tn), in_specs=ins, out_specs=o,
                 out_shape=jax.ShapeDtypeStruct((T, N), out_dtype), sem=("parallel", "parallel"))(*args)


def _mm_nt(a, b, res=None):
    T, K = a.shape
    N = b.shape[0]
    tm = _tile(T, 512)

    def body(*refs):
        if res is None:
            a_ref, b_ref, o_ref = refs
            o_ref[...] = _dot_nt(a_ref[...].astype(BF), b_ref[...])
        else:
            a_ref, b_ref, r_ref, o_ref = refs
            o_ref[...] = r_ref[...] + _dot_nt(a_ref[...].astype(BF), b_ref[...])

    o = pl.BlockSpec((tm, N), lambda i: (i, 0))
    ins = [pl.BlockSpec((tm, K), lambda i: (i, 0)), pl.BlockSpec((N, K), lambda i: (0, 0))]
    args = [a, b]
    if res is not None:
        ins.append(o)
        args.append(res)
    return _call(body, name="mm_nt", grid=(T // tm,), in_specs=ins, out_specs=o,
                 out_shape=jax.ShapeDtypeStruct((T, N), F32), sem=("parallel",))(*args)


def _mm_tn(a, b):
    T, M = a.shape
    N = b.shape[1]
    tk = _tile(T, 512)
    tmm = _tile(M, 512)

    def body(a_ref, b_ref, o_ref):
        @pl.when(pl.program_id(1) == 0)
        def _():
            o_ref[...] = jnp.zeros_like(o_ref)

        o_ref[...] += _dot_tn(a_ref[...].astype(BF), b_ref[...].astype(BF))

    return _call(body, name="mm_tn", grid=(M // tmm, T // tk),
                 in_specs=[pl.BlockSpec((tk, tmm), lambda i, k: (k, i)), pl.BlockSpec((tk, N), lambda i, k: (k, 0))],
                 out_specs=pl.BlockSpec((tmm, N), lambda i, k: (i, 0)),
                 out_shape=jax.ShapeDtypeStruct((M, N), F32), sem=("parallel", "arbitrary"))(a, b)


def _lane_mask(e, width=128):
    return (lax.broadcasted_iota(jnp.int32, (1, width), 1) // HEAD) == e


def _band_mask(n):
    qi = lax.broadcasted_iota(jnp.int32, (CHUNK, 2 * CHUNK), 0)
    kj = lax.broadcasted_iota(jnp.int32, (CHUNK, 2 * CHUNK), 1)
    dist = qi + CHUNK - kj
    return (dist >= 0) & (dist <= CHUNK) & ((kj >= CHUNK) | (n > 0))


def _prev_cur(ref, n):
    cur = pl.multiple_of(n * CHUNK, CHUNK)
    prv = pl.multiple_of(jnp.maximum(n - 1, 0) * CHUNK, CHUNK)
    return jnp.concatenate([ref[pl.ds(prv, CHUNK), :], ref[pl.ds(cur, CHUNK), :]], axis=0), prv, cur


def _attn_fwd(qkv, dil):
    T = qkv.shape[0]
    B, L = T // SEQ, SEQ // dil
    nb = L // CHUNK
    scale = HEAD ** -0.5

    def body(q_ref, k_ref, v_ref, o_ref, l_ref):
        n = pl.program_id(2)
        q = q_ref[...]
        kk, _, _ = _prev_cur(k_ref, n)
        vv, _, _ = _prev_cur(v_ref, n)
        mask = _band_mask(n)
        for t in range(ATT_W // 128):
            sl = slice(128 * t, 128 * (t + 1))
            qt, kt, vt = q[:, sl], kk[:, sl], vv[:, sl]
            o_pair = jnp.zeros((CHUNK, 128), F32)
            l_pair = jnp.zeros((CHUNK, 128), F32)
            for e in range(2):
                lm = _lane_mask(e)
                s = _dot_nt(jnp.where(lm, qt, jnp.zeros_like(qt)), kt) * scale
                s = jnp.where(mask, s, NEG)
                m = jnp.max(s, axis=-1, keepdims=True)
                p = jnp.exp(s - m)
                den = jnp.sum(p, axis=-1, keepdims=True)
                o = _dot(p.astype(BF), vt) / den
                o_pair = jnp.where(lm, o, o_pair)
                l_pair = jnp.where(lm, m + jnp.log(den), l_pair)
            o_ref[:, sl] = o_pair
            l_ref[:, sl] = l_pair

    qv = qkv.reshape(B, L, dil * W_QKV)
    o = pl.BlockSpec((None, CHUNK, ATT_W), lambda b, r, n: (b, n, r))
    sh = jax.ShapeDtypeStruct((B, L, dil * ATT_W), F32)
    out, lse = _call(
        body, name=f"attn_fwd_d{dil}", grid=(B, dil, nb),
        in_specs=[pl.BlockSpec((None, CHUNK, ATT_W), lambda b, r, n: (b, n, 3 * r)),
                  pl.BlockSpec((None, L, ATT_W), lambda b, r, n: (b, 0, 3 * r + 1)),
                  pl.BlockSpec((None, L, ATT_W), lambda b, r, n: (b, 0, 3 * r + 2))],
        out_specs=[o, o], out_shape=[sh, sh], sem=("parallel", "parallel", "parallel"))(qv, qv, qv)
    return out.reshape(T, ATT_W), lse.reshape(T, ATT_W)


def _attn_combine(o1, o2, o3, l1, l2, l3):
    T = o1.shape[0]
    tm = _tile(T, 512)

    def body(o1_ref, o2_ref, o3_ref, l1_ref, l2_ref, l3_ref, y_ref, l_ref):
        a, b, c = l1_ref[...], l2_ref[...], l3_ref[...]
        m = jnp.maximum(jnp.maximum(a, b), c)
        ea, eb, ec = jnp.exp(a - m), jnp.exp(b - m), jnp.exp(c - m)
        z = ea + eb + ec
        y_ref[...] = (ea * o1_ref[...] + eb * o2_ref[...] + ec * o3_ref[...]) / z
        l_ref[...] = m + jnp.log(z)

    row = pl.BlockSpec((tm, ATT_W), lambda i: (i, 0))
    sh = jax.ShapeDtypeStruct((T, ATT_W), F32)
    return _call(body, name="attn_combine", grid=(T // tm,), in_specs=[row] * 6, out_specs=[row, row],
                 out_shape=[sh, sh], sem=("parallel",))(o1, o2, o3, l1, l2, l3)


def _attn_bwd(qkv, do, out, lse, dil):
    T = qkv.shape[0]
    B, L = T // SEQ, SEQ // dil
    nb = L // CHUNK
    scale = HEAD ** -0.5

    def body(q_ref, k_ref, v_ref, do_ref, out_ref, lse_ref, dq_ref, dk_ref, dv_ref):
        n = pl.program_id(2)

        @pl.when(n == 0)
        def _():
            dk_ref[...] = jnp.zeros_like(dk_ref)
            dv_ref[...] = jnp.zeros_like(dv_ref)

        q = q_ref[...]
        kk, prv, cur = _prev_cur(k_ref, n)
        vv, _, _ = _prev_cur(v_ref, n)
        mask = _band_mask(n)
        do_ = do_ref[...]
        dlt = do_ * out_ref[...]
        ls = lse_ref[...]
        for t in range(ATT_W // 128):
            sl = slice(128 * t, 128 * (t + 1))
            qt, kt, vt = q[:, sl], kk[:, sl], vv[:, sl]
            dq_pair = jnp.zeros((CHUNK, 128), F32)
            dk_acc = jnp.zeros((2 * CHUNK, 128), F32)
            dv_acc = jnp.zeros((2 * CHUNK, 128), F32)
            for e in range(2):
                lm = _lane_mask(e)
                qm = jnp.where(lm, qt, jnp.zeros_like(qt))
                s = _dot_nt(qm, kt) * scale
                lse_col = ls[:, 128 * t + HEAD * e:128 * t + HEAD * e + 1]
                p = jnp.exp(jnp.where(mask, s - lse_col, NEG))
                dom = jnp.where(lm, do_[:, sl], 0.0).astype(BF)
                dv_acc += _dot_tn(p.astype(BF), dom)
                dp = _dot_nt(dom, vt)
                delta = jnp.sum(jnp.where(lm, dlt[:, sl], 0.0), axis=-1, keepdims=True)
                ds = (p * (dp - delta) * scale).astype(BF)
                dq_pair += jnp.where(lm, _dot(ds, kt), 0.0)
                dk_acc += _dot_tn(ds, qm)
            dq_ref[:, sl] = dq_pair
            dk_ref[pl.ds(cur, CHUNK), sl] += dk_acc[CHUNK:]
            dk_ref[pl.ds(prv, CHUNK), sl] += dk_acc[:CHUNK]
            dv_ref[pl.ds(cur, CHUNK), sl] += dv_acc[CHUNK:]
            dv_ref[pl.ds(prv, CHUNK), sl] += dv_acc[:CHUNK]

    qv = qkv.reshape(B, L, dil * W_QKV)
    view = lambda a: a.reshape(B, L, dil * ATT_W)
    blk = pl.BlockSpec((None, CHUNK, ATT_W), lambda b, r, n: (b, n, r))
    whole = pl.BlockSpec((None, L, ATT_W), lambda b, r, n: (b, 0, r))
    sh = jax.ShapeDtypeStruct((B, L, dil * ATT_W), F32)
    dq, dk, dv = _call(
        body, name=f"attn_bwd_d{dil}", grid=(B, dil, nb),
        in_specs=[pl.BlockSpec((None, CHUNK, ATT_W), lambda b, r, n: (b, n, 3 * r)),
                  pl.BlockSpec((None, L, ATT_W), lambda b, r, n: (b, 0, 3 * r + 1)),
                  pl.BlockSpec((None, L, ATT_W), lambda b, r, n: (b, 0, 3 * r + 2)),
                  blk, blk, blk],
        out_specs=[blk, whole, whole], out_shape=[sh, sh, sh],
        sem=("parallel", "parallel", "arbitrary"))(qv, qv, qv, view(do), view(out), view(lse))
    return dq.reshape(T, ATT_W), dk.reshape(T, ATT_W), dv.reshape(T, ATT_W)


def _sum_branches(parts):
    T = parts[0][0].shape[0]
    tm = _tile(T, 512)

    def body(*refs):
        o_ref = refs[-1]
        for s in range(3):
            acc = refs[s][...] + refs[3 + s][...] + refs[6 + s][...]
            o_ref[:, ATT_W * s:ATT_W * (s + 1)] = acc.astype(BF)

    row = pl.BlockSpec((tm, ATT_W), lambda i: (i, 0))
    flat = [a for tr in parts for a in tr]
    return _call(body, name="attn_sum_branches", grid=(T // tm,), in_specs=[row] * 9,
                 out_specs=pl.BlockSpec((tm, W_QKV), lambda i: (i, 0)),
                 out_shape=jax.ShapeDtypeStruct((T, W_QKV), BF), sem=("parallel",))(*flat)


def _silu(x):
    return x * _sigmoid(x)


def _dsilu(x):
    s = _sigmoid(x)
    return s * (1.0 + x * (1.0 - s))


def _log1p(u):
    return jnp.where(u < 0.01, u * (1.0 - u * (0.5 - u * (1.0 / 3.0))), jnp.log(1.0 + u))


def _softplus(x):
    return jnp.maximum(x, 0.0) + _log1p(jnp.exp(-jnp.abs(x)))


def _cumsum_rows(x, reverse=False):
    n = x.shape[0]
    rows = lax.broadcasted_iota(jnp.int32, x.shape, 0)
    k = 1
    while k < n:
        if reverse:
            x = x + jnp.where(rows < n - k, pltpu.roll(x, n - k, 0), 0.0)
        else:
            x = x + jnp.where(rows >= k, pltpu.roll(x, k, 0), 0.0)
        k *= 2
    return x


def _tri():
    r = lax.broadcasted_iota(jnp.int32, (CHUNK, CHUNK), 0)
    c = lax.broadcasted_iota(jnp.int32, (CHUNK, CHUNK), 1)
    return r >= c


def _row_mask(e):
    return (lax.broadcasted_iota(jnp.int32, (128, 1), 0) // HEAD) == e


def _first_lane(e):
    return lax.broadcasted_iota(jnp.int32, (1, 128), 1) == HEAD * e


def _ssd_pre(x_ref, halo_ref, first, cw_ref, cb_ref, dtb_ref, al_ref, ext):
    row = x_ref[...]
    z = row[:, SSD_CONV_DIM:SSD_CONV_DIM + SSD_W]
    u = row[:, SSD_CONV_DIM + SSD_W:] + dtb_ref[...]
    ext[0:8, :] = jnp.where(first, 0.0, halo_ref[:, 0:SSD_CONV_DIM])
    ext[8:8 + CHUNK, :] = row[:, 0:SSD_CONV_DIM]
    xc = cb_ref[...]
    for j in range(4):
        xc = xc + cw_ref[j:j + 1, :] * ext[pl.ds(5 + j, CHUNK), :]
    xa = _silu(xc)
    dt = _softplus(u)
    a = dt * (-jnp.exp(al_ref[...]))
    A = _cumsum_rows(a)
    return dict(z=z, u=u, xc=xc, xs=xa[:, 0:SSD_W], Bm=xa[:, SSD_W:SSD_W + 256], Cm=xa[:, SSD_W + 256:],
                dt=dt, a=a, A=A, AT=A.T, eA=jnp.exp(A), wdec=jnp.exp(A[CHUNK - 1:CHUNK, :] - A),
                dtot=jnp.exp(A[CHUNK - 1:CHUNK, :]))


def _ssd_y(p, hp_ref, dskip):
    tri = _tri()
    X = p["xs"] * p["dt"]
    Bb = [p["Bm"][:, 128 * g:128 * (g + 1)].astype(BF) for g in range(2)]
    Cb = [p["Cm"][:, 128 * g:128 * (g + 1)].astype(BF) for g in range(2)]
    CB = [_dot_nt(Cb[g], Bb[g]) for g in range(2)]
    tiles = []
    for t in range(3):
        sl = slice(128 * t, 128 * (t + 1))
        hpb = hp_ref[sl, :].astype(BF)
        acc = jnp.zeros((CHUNK, 128), F32)
        for e in range(2):
            h = 2 * t + e
            g, col = h // 3, HEAD * h
            lm = _lane_mask(e)
            L = jnp.exp(jnp.where(tri, p["A"][:, col:col + 1] - p["AT"][col:col + 1, :], NEG))
            yd = _dot((CB[g] * L).astype(BF), jnp.where(lm, X[:, sl], 0.0).astype(BF))
            yo = _dot_nt(Cb[g], hpb) * p["eA"][:, sl]
            acc = acc + jnp.where(lm, yd + yo, 0.0)
        tiles.append(acc)
    return jnp.concatenate(tiles, axis=1) + dskip * p["xs"], X, Bb, Cb, CB


def _group_stats(v):
    g0 = lax.broadcasted_iota(jnp.int32, (1, SSD_W), 1) < SSD_W // 2
    m0 = jnp.sum(jnp.where(g0, v, 0.0), axis=-1, keepdims=True) * (2.0 / SSD_W)
    m1 = jnp.sum(jnp.where(g0, 0.0, v), axis=-1, keepdims=True) * (2.0 / SSD_W)
    return jnp.where(g0, m0, m1)


def _ssd_specs(T, rev):
    B = T // SEQ

    def chunk(b, c):
        return b * N_CHUNK + (N_CHUNK - 1 - c if rev else c)

    row = pl.BlockSpec((CHUNK, W_SSD), lambda b, c: (chunk(b, c), 0))
    halo = pl.BlockSpec((8, W_SSD), lambda b, c: (jnp.maximum(chunk(b, c) * (CHUNK // 8) - 1, 0), 0))
    hp = pl.BlockSpec((None, SSD_W, SSD_STATE), lambda b, c: (chunk(b, c), 0, 0))
    y = pl.BlockSpec((CHUNK, SSD_W), lambda b, c: (chunk(b, c), 0))
    const = lambda r, w: pl.BlockSpec((r, w), lambda b, c: (0, 0))
    params = [const(4, SSD_CONV_DIM), const(1, SSD_CONV_DIM)] + [const(1, SSD_W)] * 4
    return B, row, halo, hp, y, const, params


def _ssd_fwd(sin, conv_w, conv_b, dtb, alog, dskip, norm_g):
    T = sin.shape[0]
    B, row, halo, hp, y, const, params = _ssd_specs(T, False)

    def body(x_ref, halo_ref, cw_ref, cb_ref, dtb_ref, al_ref, dk_ref, ng_ref, y_ref, hp_ref, ext, hst):
        c = pl.program_id(1)

        @pl.when(c == 0)
        def _():
            hst[...] = jnp.zeros_like(hst)

        p = _ssd_pre(x_ref, halo_ref, c == 0, cw_ref, cb_ref, dtb_ref, al_ref, ext)
        yv, X, Bb, Cb, CB = _ssd_y(p, hst, dk_ref[...])
        hp_ref[...] = hst[...]
        for t in range(3):
            sl = slice(128 * t, 128 * (t + 1))
            old = hst[sl, :]
            new = old
            for e in range(2):
                h = 2 * t + e
                g, col = h // 3, HEAD * h
                st = _dot_tn(jnp.where(_lane_mask(e), X[:, sl] * p["wdec"][:, sl], 0.0).astype(BF), Bb[g])
                new = jnp.where(_row_mask(e), old * p["dtot"][:, col:col + 1] + st, new)
            hst[sl, :] = new
        y2 = yv * _silu(p["z"])
        r = lax.rsqrt(_group_stats(y2 * y2) + RMS_EPS)
        y_ref[...] = y2 * r * ng_ref[...]

    return _call(body, name="ssd_fwd", grid=(B, N_CHUNK), in_specs=[row, halo] + params, out_specs=[y, hp],
                 out_shape=[jax.ShapeDtypeStruct((T, SSD_W), F32),
                            jax.ShapeDtypeStruct((T // CHUNK, SSD_W, SSD_STATE), F32)],
                 scratch=[pltpu.VMEM((8 + CHUNK, SSD_CONV_DIM), F32), pltpu.VMEM((SSD_W, SSD_STATE), F32)],
                 sem=("parallel", "arbitrary"))(sin, sin, conv_w, conv_b, dtb, alog, dskip, norm_g)


def _ssd_bwd(sin, hprev, dy3, conv_w, conv_b, dtb, alog, dskip, norm_g):
    T = sin.shape[0]
    B, row, halo, hp, y, const, params = _ssd_specs(T, True)

    def body(x_ref, halo_ref, hp_ref, dy_ref, cw_ref, cb_ref, dtb_ref, al_ref, dk_ref, ng_ref,
             dx_ref, dcw_ref, dcb_ref, dvec_ref, ext, ext2, dh):
        c = pl.program_id(1)

        @pl.when((pl.program_id(0) == 0) & (c == 0))
        def _():
            dcw_ref[...] = jnp.zeros_like(dcw_ref)
            dcb_ref[...] = jnp.zeros_like(dcb_ref)
            dvec_ref[...] = jnp.zeros_like(dvec_ref)

        @pl.when(c == 0)
        def _():
            dh[...] = jnp.zeros_like(dh)
            ext2[CHUNK:CHUNK + 8, :] = jnp.zeros((8, SSD_CONV_DIM), F32)

        p = _ssd_pre(x_ref, halo_ref, c == N_CHUNK - 1, cw_ref, cb_ref, dtb_ref, al_ref, ext)
        dskip_ = dk_ref[...]
        yv, X, Bb, Cb, CB = _ssd_y(p, hp_ref, dskip_)
        xs, z, A, AT = p["xs"], p["z"], p["A"], p["AT"]

        sz = _silu(z)
        y2 = yv * sz
        r = lax.rsqrt(_group_stats(y2 * y2) + RMS_EPS)
        dy3_ = dy_ref[...]
        uu = dy3_ * ng_ref[...]
        dy2 = r * (uu - y2 * (r * r * _group_stats(uu * y2)))
        dy = dy2 * sz
        dz = dy2 * yv * _dsilu(z)

        tri = _tri()
        rows = lax.broadcasted_iota(jnp.int32, (CHUNK, 1), 0)
        dG = [jnp.zeros((CHUNK, CHUNK), F32) for _ in range(2)]
        dB = [jnp.zeros((CHUNK, SSD_STATE), F32) for _ in range(2)]
        dC = [jnp.zeros((CHUNK, SSD_STATE), F32) for _ in range(2)]
        dX_t, dA_t, ddtx_t = [], [], []
        for t in range(3):
            sl = slice(128 * t, 128 * (t + 1))
            hp_t = hp_ref[sl, :]
            hpb = hp_t.astype(BF)
            dhc = dh[sl, :]
            dh_new = jnp.zeros((128, SSD_STATE), F32)
            dX = jnp.zeros((CHUNK, 128), F32)
            dA = jnp.zeros((CHUNK, 128), F32)
            ddtx = jnp.zeros((CHUNK, 128), F32)
            for e in range(2):
                h = 2 * t + e
                g, col = h // 3, HEAD * h
                lm, rm, fl = _lane_mask(e), _row_mask(e), _first_lane(e)
                L = jnp.exp(jnp.where(tri, A[:, col:col + 1] - AT[col:col + 1, :], NEG))
                Mf = CB[g] * L
                Xm = jnp.where(lm, X[:, sl], 0.0)
                Xmb = Xm.astype(BF)
                dyh = jnp.where(lm, dy[:, sl], 0.0)
                dyb = dyh.astype(BF)
                dXh = _dot_tn(Mf.astype(BF), dyb)
                dM = jnp.where(tri, _dot_nt(dyb, Xmb), 0.0)
                Wm = dM * Mf
                dAc = jnp.sum(Wm, axis=-1, keepdims=True) - jnp.sum(Wm.T, axis=-1, keepdims=True)
                dG[g] = dG[g] + dM * L
                eAt = p["eA"][:, sl]
                yo = _dot_nt(Cb[g], hpb)
                dyo = (dyh * eAt).astype(BF)
                dC[g] = dC[g] + _dot(dyo, hpb)
                dh_new = dh_new + _dot_tn(dyo, Cb[g])
                dAc = dAc + jnp.sum(dyh * yo * eAt, axis=-1, keepdims=True)
                dHn = jnp.where(rm, dhc, 0.0)
                dHnb = dHn.astype(BF)
                dec = p["dtot"][:, col:col + 1]
                dh_new = dh_new + dec * dHn
                Z = _dot_nt(Bb[g], dHnb)
                wt = p["wdec"][:, sl]
                xi = jnp.sum(Xm * Z, axis=-1, keepdims=True) * p["wdec"][:, col:col + 1]
                dXh = dXh + wt * Z
                dB[g] = dB[g] + _dot(jnp.where(lm, X[:, sl] * wt, 0.0).astype(BF), dHnb)
                dAtot = jnp.sum(xi, axis=0, keepdims=True) + dec * jnp.sum(
                    jnp.sum(dHn * hp_t, axis=-1, keepdims=True), axis=0, keepdims=True)
                dAc = dAc - xi + jnp.where(rows == CHUNK - 1, dAtot, 0.0)
                dA = dA + jnp.where(fl, dAc, 0.0)
                dX = dX + dXh
                ddtx = ddtx + jnp.where(fl, jnp.sum(dXh * xs[:, sl], axis=-1, keepdims=True), 0.0)
            dh[sl, :] = dh_new
            dX_t.append(dX)
            dA_t.append(dA)
            ddtx_t.append(ddtx)
        for g in range(2):
            dGb = dG[g].astype(BF)
            dC[g] = dC[g] + _dot(dGb, Bb[g])
            dB[g] = dB[g] + _dot_tn(dGb, Cb[g])
        dXf = jnp.concatenate(dX_t, axis=1)
        da = _cumsum_rows(jnp.concatenate(dA_t, axis=1), reverse=True)
        ddt = da * (-jnp.exp(al_ref[...])) + jnp.concatenate(ddtx_t, axis=1)
        du = ddt * _sigmoid(p["u"])
        dxs = dXf * p["dt"] + dskip_ * dy
        dxc = jnp.concatenate([dxs, dB[0], dB[1], dC[0], dC[1]], axis=1) * _dsilu(p["xc"])
        ext2[0:CHUNK, :] = dxc
        dxbc = jnp.zeros((CHUNK, SSD_CONV_DIM), F32)
        for j in range(4):
            dxbc = dxbc + cw_ref[j:j + 1, :] * ext2[pl.ds(3 - j, CHUNK), :]
            dcw_ref[j:j + 1, :] += jnp.sum(dxc * ext[pl.ds(5 + j, CHUNK), :], axis=0, keepdims=True)
        ext2[CHUNK:CHUNK + 8, :] = dxc[0:8, :]
        dcb_ref[...] += jnp.sum(dxc, axis=0, keepdims=True)
        dvec_ref[0:1, :] += jnp.sum(du, axis=0, keepdims=True)
        dvec_ref[1:2, :] += jnp.sum(da * p["a"], axis=0, keepdims=True)
        dvec_ref[2:3, :] += jnp.sum(dy * xs, axis=0, keepdims=True)
        dvec_ref[3:4, :] += jnp.sum(dy3_ * y2 * r, axis=0, keepdims=True)
        dx_ref[...] = jnp.concatenate([dxbc, dz, du], axis=1).astype(BF)

    return _call(body, name="ssd_bwd", grid=(B, N_CHUNK), in_specs=[row, halo, hp, y] + params,
                 out_specs=[row, const(4, SSD_CONV_DIM), const(1, SSD_CONV_DIM), const(8, SSD_W)],
                 out_shape=[jax.ShapeDtypeStruct((T, W_SSD), BF), jax.ShapeDtypeStruct((4, SSD_CONV_DIM), F32),
                            jax.ShapeDtypeStruct((1, SSD_CONV_DIM), F32), jax.ShapeDtypeStruct((8, SSD_W), F32)],
                 scratch=[pltpu.VMEM((8 + CHUNK, SSD_CONV_DIM), F32), pltpu.VMEM((8 + CHUNK, SSD_CONV_DIM), F32),
                          pltpu.VMEM((SSD_W, SSD_STATE), F32)],
                 sem=("arbitrary", "arbitrary"))(sin, sin, hprev, dy3, conv_w, conv_b, dtb, alog, dskip, norm_g)


def _sgu_core(uv_ref, g_ref, b_ref, w_ref, bias_ref):
    x = uv_ref[...]
    cdf = 0.5 * (1.0 + lax.erf(x * (2.0 ** -0.5)))
    ge = x * cdf
    dge = cdf + x * jnp.exp(-0.5 * x * x) * ((2.0 * math.pi) ** -0.5)
    u, v = ge[:, 0:SGU_W], ge[:, SGU_W:]
    vc = v - jnp.mean(v, axis=-1, keepdims=True)
    rstd = lax.rsqrt(jnp.mean(vc * vc, axis=-1, keepdims=True) + LN_EPS)
    vhat = vc * rstd
    vn = vhat * g_ref[...] + b_ref[...]
    tri = _tri()
    wc = [jnp.where(tri, w_ref[gi], 0.0).astype(BF) for gi in range(4)]
    vm = [jnp.where(_lane_mask(gi % 2), vn[:, 128 * (gi // 2):128 * (gi // 2 + 1)], 0.0).astype(BF) for gi in range(4)]
    mixed = jnp.concatenate([_dot(wc[2 * t], vm[2 * t]) + _dot(wc[2 * t + 1], vm[2 * t + 1]) for t in range(2)],
                            axis=1) + bias_ref[...]
    return dict(dge=dge, u=u, rstd=rstd, vhat=vhat, wc=wc, vm=vm, mixed=mixed)


def _sgu_specs():
    vec = pl.BlockSpec((1, SGU_W), lambda i: (0, 0))
    return [pl.BlockSpec((CHUNK, W_UV), lambda i: (i, 0)), vec, vec,
            pl.BlockSpec((4, CHUNK, CHUNK), lambda i: (0, 0, 0)), pl.BlockSpec((CHUNK, SGU_W), lambda i: (0, 0))]


def _sgu_fwd(uv, ln_g, ln_b, w, bias):
    T = uv.shape[0]

    def body(uv_ref, g_ref, b_ref, w_ref, bias_ref, y_ref):
        s = _sgu_core(uv_ref, g_ref, b_ref, w_ref, bias_ref)
        y_ref[...] = s["u"] * s["mixed"]

    return _call(body, name="sgu_fwd", grid=(T // CHUNK,), in_specs=_sgu_specs(),
                 out_specs=pl.BlockSpec((CHUNK, SGU_W), lambda i: (i, 0)),
                 out_shape=jax.ShapeDtypeStruct((T, SGU_W), F32), sem=("parallel",))(uv, ln_g, ln_b, w, bias)


def _sgu_bwd(uv, dy, ln_g, ln_b, w, bias):
    T = uv.shape[0]

    def body(uv_ref, dy_ref, g_ref, b_ref, w_ref, bias_ref, dx_ref, dw_ref, dbias_ref, dln_ref):
        @pl.when(pl.program_id(0) == 0)
        def _():
            dw_ref[...] = jnp.zeros_like(dw_ref)
            dbias_ref[...] = jnp.zeros_like(dbias_ref)
            dln_ref[...] = jnp.zeros_like(dln_ref)

        s = _sgu_core(uv_ref, g_ref, b_ref, w_ref, bias_ref)
        dy_ = dy_ref[...]
        du = dy_ * s["mixed"]
        dmix = dy_ * s["u"]
        dbias_ref[...] += dmix
        tri = _tri()
        dvn_t = []
        for t in range(2):
            acc = jnp.zeros((CHUNK, 128), F32)
            for e in range(2):
                gi = 2 * t + e
                dmg = jnp.where(_lane_mask(e), dmix[:, 128 * t:128 * (t + 1)], 0.0).astype(BF)
                acc = acc + _dot_tn(s["wc"][gi], dmg)
                dw_ref[gi] += jnp.where(tri, _dot_nt(dmg, s["vm"][gi]), 0.0)
            dvn_t.append(acc)
        dvn = jnp.concatenate(dvn_t, axis=1)
        dln_ref[0:1, :] += jnp.sum(dvn * s["vhat"], axis=0, keepdims=True)
        dln_ref[1:2, :] += jnp.sum(dvn, axis=0, keepdims=True)
        dvh = dvn * g_ref[...]
        dv = s["rstd"] * (dvh - jnp.mean(dvh, axis=-1, keepdims=True)
                          - s["vhat"] * jnp.mean(dvh * s["vhat"], axis=-1, keepdims=True))
        dx_ref[...] = (jnp.concatenate([du, dv], axis=1) * s["dge"]).astype(BF)

    ins = _sgu_specs()
    return _call(body, name="sgu_bwd", grid=(T // CHUNK,),
                 in_specs=[ins[0], pl.BlockSpec((CHUNK, SGU_W), lambda i: (i, 0))] + ins[1:],
                 out_specs=[pl.BlockSpec((CHUNK, W_UV), lambda i: (i, 0)),
                            pl.BlockSpec((4, CHUNK, CHUNK), lambda i: (0, 0, 0)),
                            pl.BlockSpec((CHUNK, SGU_W), lambda i: (0, 0)), pl.BlockSpec((8, SGU_W), lambda i: (0, 0))],
                 out_shape=[jax.ShapeDtypeStruct((T, W_UV), BF), jax.ShapeDtypeStruct((4, CHUNK, CHUNK), F32),
                            jax.ShapeDtypeStruct((CHUNK, SGU_W), F32), jax.ShapeDtypeStruct((8, SGU_W), F32)],
                 sem=("arbitrary",))(uv, dy, ln_g, ln_b, w, bias)


def _adamw(w, g, m, v):
    R, C = w.shape
    tr = _tile(R, 256) if R % 8 == 0 else R

    def body(w_ref, g_ref, m_ref, v_ref, d_ref, nm_ref, nv_ref):
        g_ = g_ref[...]
        m2 = ADAM_B1 * m_ref[...] + (1.0 - ADAM_B1) * g_
        v2 = ADAM_B2 * v_ref[...] + (1.0 - ADAM_B2) * (g_ * g_)
        m_hat = m2 / (1.0 - ADAM_B1 ** ADAM_STEP)
        v_hat = v2 / (1.0 - ADAM_B2 ** ADAM_STEP)
        d_ref[...] = -ADAM_LR * (m_hat / (jnp.sqrt(v_hat) + ADAM_EPS) + ADAM_WD * w_ref[...])
        nm_ref[...] = m2
        nv_ref[...] = v2

    blk = pl.BlockSpec((tr, C), lambda i: (i, 0))
    sh = jax.ShapeDtypeStruct((R, C), F32)
    return _call(body, name="adamw", grid=(R // tr,), in_specs=[blk] * 4, out_specs=[blk] * 3,
                 out_shape=[sh] * 3, sem=("parallel",))(w, g, m, v)


def _adamw_pair(w, g0, g1, m, v, dep):
    L, R, C = w.shape
    tr = _tile(R, 256 if C <= 1024 else 64)

    def body(w_ref, g0_ref, g1_ref, m_ref, v_ref, dep_ref, d_ref, nm_ref, nv_ref, og_ref):
        g_ = jnp.where(pl.program_id(0) == 0, g0_ref[...], g1_ref[...])
        m2 = ADAM_B1 * m_ref[...] + (1.0 - ADAM_B1) * g_
        v2 = ADAM_B2 * v_ref[...] + (1.0 - ADAM_B2) * (g_ * g_)
        m_hat = m2 / (1.0 - ADAM_B1 ** ADAM_STEP)
        v_hat = v2 / (1.0 - ADAM_B2 ** ADAM_STEP)
        d_ref[...] = -ADAM_LR * (m_hat / (jnp.sqrt(v_hat) + ADAM_EPS) + ADAM_WD * w_ref[...])
        nm_ref[...] = m2
        nv_ref[...] = v2
        og_ref[...] = g_

    lay = pl.BlockSpec((None, tr, C), lambda l, i: (l, i, 0))
    one = pl.BlockSpec((tr, C), lambda l, i: (i, 0))
    return _call(body, name="adamw_pair", grid=(L, R // tr),
                 in_specs=[lay, one, one, lay, lay, pl.BlockSpec((8, 128), lambda l, i: (0, 0))], out_specs=[lay] * 4,
                 out_shape=[jax.ShapeDtypeStruct((L, R, C), F32)] * 4,
                 sem=("parallel", "parallel"))(w, g0, g1, m, v, dep)


def _row_steps(rows):
    return 2 if rows % 32 == 0 else 1


def _pair_add(gbuf, rsib, c):
    NS, _, R, C = gbuf.shape
    n = _row_steps(R)
    tr = R // n

    def body(c_ref, a_ref, b_ref, o_ref):
        o_ref[...] = (a_ref[...] + b_ref[...]).astype(BF)

    blk = pl.BlockSpec((None, tr, C), lambda j, i, c_ref: (j, i, 0))
    return pl.pallas_call(
        body, name="rs_pair_add",
        grid_spec=pltpu.PrefetchScalarGridSpec(
            num_scalar_prefetch=1, grid=(NS, n),
            in_specs=[pl.BlockSpec((None, None, tr, C), lambda j, i, c_ref: (j, c_ref[0], i, 0)), blk],
            out_specs=blk),
        out_shape=jax.ShapeDtypeStruct((NS, R, C), BF),
        compiler_params=pltpu.CompilerParams(dimension_semantics=("parallel", "parallel")),
    )(jnp.reshape(c, (1,)).astype(jnp.int32), gbuf, rsib)


def _chip_sum(pair, recv, me, c):
    NS, R, C = pair.shape
    n = _row_steps(R)
    tr = R // n

    def body(s_ref, own_ref, p_ref, o_ref):
        p = [jnp.where(s_ref[0] == j, own_ref[...], p_ref[j]).astype(F32) for j in range(4)]
        o_ref[...] = ((p[0] + p[1]) + p[2]) + p[3]

    return pl.pallas_call(
        body, name="rs_chip_sum",
        grid_spec=pltpu.PrefetchScalarGridSpec(
            num_scalar_prefetch=1, grid=(n,),
            in_specs=[pl.BlockSpec((None, tr, C), lambda i, s: (s[0], i, 0)),
                      pl.BlockSpec((NS, tr, C), lambda i, s: (0, i, 0))],
            out_specs=pl.BlockSpec((None, tr, C), lambda i, s: (s[1], i, 0))),
        out_shape=jax.ShapeDtypeStruct((2, R, C), F32),
        compiler_params=pltpu.CompilerParams(dimension_semantics=("parallel",)),
    )(jnp.stack([me, c]).astype(jnp.int32), pair, recv)


MESH = pl.DeviceIdType.MESH
ANY = pl.BlockSpec(memory_space=pl.ANY)


def _place():
    x, y, c = lax.axis_index("x"), lax.axis_index("y"), lax.axis_index("c")
    return x, y, c, [(1 - x, y), (x, 1 - y), (1 - x, 1 - y)]


HBM = pl.BlockSpec(memory_space=pltpu.HBM)
SEM = pl.BlockSpec(memory_space=pltpu.SEMAPHORE)
EFFECT = pltpu.SideEffectType.DATAFLOW_SIDE_EFFECTING


class _Split:
    def __init__(self, tag, arrays, copies, n_copies):
        self.tag, self.copies, k = tag, copies, len(arrays)

        def body(*refs):
            for cp in copies(refs[:k], refs[k], refs[k + 1]):
                cp.start()
            refs[-1][...] = jnp.zeros_like(refs[-1])

        out = pl.pallas_call(
            body, name=tag + "_start",
            out_shape=(pltpu.SemaphoreType.DMA((n_copies,)), pltpu.SemaphoreType.DMA((n_copies,)),
                       *[pltpu.HBM(a.shape, a.dtype) for a in arrays], jax.ShapeDtypeStruct((8, 128), F32)),
            in_specs=[HBM] * k, out_specs=(SEM, SEM, *[HBM] * k, pl.BlockSpec(memory_space=pltpu.VMEM)),
            input_output_aliases={i: 2 + i for i in range(k)},
            compiler_params=pltpu.CompilerParams(has_side_effects=EFFECT),
        )(*[pltpu.with_memory_space_constraint(a, pltpu.HBM) for a in arrays])
        self.send, self.recv, self.arrays, self.token = out[0], out[1], list(out[2:2 + k]), out[-1][0, 0]

    def wait(self, after):
        k, copies = len(self.arrays), self.copies

        def body(*refs):
            for cp in copies(refs[:k], refs[k], refs[k + 1]):
                cp.wait_send()
                cp.wait_recv()

        return list(pl.pallas_call(
            body, name=self.tag + "_wait", out_shape=tuple(pltpu.HBM(a.shape, a.dtype) for a in self.arrays),
            in_specs=[HBM] * k + [SEM, SEM, ANY], out_specs=tuple([HBM] * k),
            input_output_aliases={i: i for i in range(k)},
            compiler_params=pltpu.CompilerParams(has_side_effects=EFFECT),
        )(*self.arrays, self.send, self.recv, after))


def _gather_start(arrs, tag):
    n = len(arrs)
    me = 2 * lax.axis_index("x") + lax.axis_index("y")
    lands = [lax.dynamic_update_index_in_dim(lax.empty((4,) + a.shape, a.dtype), a, me, 0) for a in arrs]

    def copies(refs, send, recv):
        x, y, c, chips = _place()
        return [pltpu.make_async_remote_copy(
            src_ref=refs[k], dst_ref=refs[n + k].at[2 * x + y], send_sem=send.at[3 * k + r],
            recv_sem=recv.at[3 * k + r], device_id=(px, py, c), device_id_type=MESH)
            for k in range(n) for r, (px, py) in enumerate(chips)]

    return _Split("gather_" + tag, list(arrs) + lands, copies, 3 * n)


def _to_sibling_start(gbufs, tag):
    n = len(gbufs)

    def copies(refs, send, recv):
        x, y, c, _ = _place()
        return [pltpu.make_async_remote_copy(
            src_ref=refs[k].at[j, 1 - c], dst_ref=refs[n + k].at[j], send_sem=send.at[4 * k + j],
            recv_sem=recv.at[4 * k + j], device_id=(x, y, 1 - c), device_id_type=MESH)
            for k in range(n) for j in range(4)]

    lands = [lax.empty((4,) + g.shape[2:], g.dtype) for g in gbufs]
    return _Split("rs_sibling_" + tag, list(gbufs) + lands, copies, 4 * n)


def _to_chips_start(pbufs, tag):
    n = len(pbufs)

    def copies(refs, send, recv):
        x, y, c, chips = _place()
        return [pltpu.make_async_remote_copy(
            src_ref=refs[k].at[2 * px + py], dst_ref=refs[n + k].at[2 * x + y], send_sem=send.at[3 * k + r],
            recv_sem=recv.at[3 * k + r], device_id=(px, py, c), device_id_type=MESH)
            for k in range(n) for r, (px, py) in enumerate(chips)]

    return _Split("rs_chips_" + tag, list(pbufs) + [lax.empty(p.shape, p.dtype) for p in pbufs], copies, 3 * n)


def _join_start(fulls, tag):
    def copies(refs, send, recv):
        x, y, c, _ = _place()
        return [pltpu.make_async_remote_copy(
            src_ref=refs[k].at[c], dst_ref=refs[k].at[c], send_sem=send.at[k], recv_sem=recv.at[k],
            device_id=(x, y, 1 - c), device_id_type=MESH) for k in range(len(fulls))]

    return _Split("rs_join_" + tag, list(fulls), copies, len(fulls))


def _all_reduce_small(v):
    R, C = v.shape

    def body(v_ref, o_ref, g_ref, send, recv, loc):
        x, y, c, chips = _place()
        me, sibling = (x, y, c), (x, y, 1 - c)

        def rows(px, py, pc):
            return g_ref.at[4 * px + 2 * py + pc]

        def copy(k, block, to, src=None):
            return pltpu.make_async_remote_copy(
                src_ref=rows(*block) if src is None else src, dst_ref=rows(*block),
                send_sem=send.at[k], recv_sem=recv.at[k], device_id=to, device_id_type=MESH)

        mine = pltpu.make_async_copy(v_ref, rows(*me), loc)
        mine.start()
        first = [copy(0, me, sibling, src=v_ref)]
        first += [copy(1 + j, me, (*chip, c), src=v_ref) for j, chip in enumerate(chips)]
        for cp in first:
            cp.start()
        passed = [copy(4 + j, (*chip, c), sibling) for j, chip in enumerate(chips)]
        for j, chip in enumerate(chips):
            copy(1 + j, (*chip, c), me).wait_recv()
            passed[j].start()
        copy(0, sibling, me).wait_recv()
        for j, chip in enumerate(chips):
            copy(4 + j, (*chip, 1 - c), me).wait_recv()
        for cp in first + passed:
            cp.wait_send()
        mine.wait()
        acc = g_ref[0]
        for d in range(1, 8):
            acc = acc + g_ref[d]
        o_ref[...] = acc

    vm = pl.BlockSpec(memory_space=pltpu.VMEM)
    return pl.pallas_call(
        body, name="all_reduce_small", in_specs=[vm], out_specs=[vm, vm],
        out_shape=[jax.ShapeDtypeStruct((R, C), F32), jax.ShapeDtypeStruct((8, R, C), F32)],
        scratch_shapes=[pltpu.SemaphoreType.DMA((7,)), pltpu.SemaphoreType.DMA((7,)), pltpu.SemaphoreType.DMA],
    )(v)[0]


WEIGHTS = ['ffn1_norm', 'ffn1_w_gate', 'ffn1_w_up', 'ffn1_w_down', 'mix_norm', 'w_in', 'conv_w', 'conv_b', 'dt_bias',
           'a_log', 'd_skip', 'ssd_norm', 'sgu_ln_g', 'sgu_ln_b', 'sgu_w', 'sgu_b', 'w_out', 'ffn2_norm',
           'ffn2_w_gate', 'ffn2_w_up', 'ffn2_w_down', 'final_norm']
SHARDED = ['ffn1_w_gate', 'ffn1_w_up', 'ffn1_w_down', 'w_in', 'conv_w', 'w_out', 'ffn2_w_gate', 'ffn2_w_up',
           'ffn2_w_down']
SMALL = [n for n in WEIGHTS if n not in SHARDED]
GROUPS = [("ffn1", ["ffn1_w_gate", "ffn1_w_up", "ffn1_w_down"]), ("mix", ["w_in", "conv_w", "w_out"]),
          ("ffn2", ["ffn2_w_gate", "ffn2_w_up", "ffn2_w_down"])]
TRANSPOSED = ("ffn1_w_gate", "ffn1_w_up", "ffn2_w_gate", "ffn2_w_up")
DEPTH = 2


def _pack_w_in(w):
    return jnp.concatenate([w[..., 0:1152], w[..., 1536:2432], w[..., 1152:1536],
                            jnp.repeat(w[..., 2432:2438], HEAD, axis=-1), w[..., 2438:2950]], axis=-1)


def _unpack_w_in(dq, ds, du):
    return jnp.concatenate([dq, ds[:, 896:1280], ds[:, 0:896], ds[:, 1280::HEAD], du], axis=-1)


def _ffn_fwd(x, g, wg, wu, wd):
    xo, hb, S1, S2, A = _ffn_fwd_k(x, g, wg, wu, wd)
    return xo, (x, hb, S1, S2, A)


def _ffn_bwd(dxo, saved, g, wg, wu, wd):
    x, hb, S1, S2, A = saved
    dx, dg, dG, dU, dyb = _ffn_bwd_k1(dxo, x, g, S1, S2, wg, wu, wd)
    dwg, dwu, dwd = _ffn_bwd_k2(hb, dyb, A, dG, dU)
    return dx, dg, dwg, dwu, dwd


def _mix_fwd(x, P):
    hb = _rms_fwd(x, P["mix_norm"])
    qkv = _mm_nn(hb, P["w_qkv"], out_dtype=BF)
    sin = _mm_nn(hb, P["w_ssd"])
    uv = _mm_nn(hb, P["w_uv"])
    o1, l1 = _attn_fwd(qkv, 1)
    o2, l2 = _attn_fwd(qkv, 4)
    o3, l3 = _attn_fwd(qkv, 16)
    y_att, lse = _attn_combine(o1, o2, o3, l1, l2, l3)
    y_ssd, hprev = _ssd_fwd(sin, *P["ssd"])
    y_sgu = _sgu_fwd(uv, *P["sgu"])
    ycat = jnp.concatenate([y_att, y_ssd, y_sgu], axis=1).astype(BF)
    return _mm_nn(ycat, P["w_out"], res=x), (x, hb, qkv, sin, uv, y_att, lse, hprev, ycat)


def _mix_bwd(dxo, saved, P):
    x, hb, qkv, sin, uv, y_att, lse, hprev, ycat = saved
    dycat = _mm_nt(dxo, P["w_out"])
    dwout = _mm_tn(ycat, dxo)
    dy_att, dy_ssd, dy_sgu = dycat[:, 0:ATT_W], dycat[:, ATT_W:ATT_W + SSD_W], dycat[:, ATT_W + SSD_W:]
    dqkv = _sum_branches([_attn_bwd(qkv, dy_att, y_att, lse, d) for d in DILATIONS])
    dsin, dcw, dcb, dvec = _ssd_bwd(sin, hprev, dy_ssd, *P["ssd"])
    duv, dsw, dsbias, dln = _sgu_bwd(uv, dy_sgu, *P["sgu"])
    dwin = _unpack_w_in(_mm_tn(hb, dqkv), _mm_tn(hb, dsin), _mm_tn(hb, duv))
    dh = _mm_nt(dqkv, P["w_qkv"])
    dh = _mm_nt(dsin, P["w_ssd"], res=dh)
    dh = _mm_nt(duv, P["w_uv"], res=dh)
    dx, dg = _rms_bwd(x, P["mix_norm"], dh, dxo)
    grads = dict(
        mix_norm=dg[0], w_in=dwin, conv_w=dcw, conv_b=dcb[0], dt_bias=dvec[0, ::HEAD], a_log=dvec[1, ::HEAD],
        d_skip=jnp.sum(dvec[2].reshape(6, HEAD), axis=-1), ssd_norm=dvec[3], sgu_ln_g=dln[0], sgu_ln_b=dln[1],
        sgu_w=dsw, sgu_b=jnp.sum(dsbias.reshape(CHUNK, 4, HEAD), axis=-1).T, w_out=dwout)
    return dx, grads


def _halved(g):
    rows = g.size // g.shape[-1]
    return g.reshape(4, 2, rows // 8, g.shape[-1])


def kernel(x, ffn1_norm, ffn1_w_gate, ffn1_w_up, ffn1_w_down, mix_norm, w_in, conv_w, conv_b, dt_bias, a_log, d_skip, ssd_norm, sgu_ln_g, sgu_ln_b, sgu_w, sgu_b, w_out, ffn2_norm, ffn2_w_gate, ffn2_w_up, ffn2_w_down, final_norm, loss_target, m_ffn1_norm, m_ffn1_w_gate, m_ffn1_w_up, m_ffn1_w_down, m_mix_norm, m_w_in, m_conv_w, m_conv_b, m_dt_bias, m_a_log, m_d_skip, m_ssd_norm, m_sgu_ln_g, m_sgu_ln_b, m_sgu_w, m_sgu_b, m_w_out, m_ffn2_norm, m_ffn2_w_gate, m_ffn2_w_up, m_ffn2_w_down, m_final_norm, v_ffn1_norm, v_ffn1_w_gate, v_ffn1_w_up, v_ffn1_w_down, v_mix_norm, v_w_in, v_conv_w, v_conv_b, v_dt_bias, v_a_log, v_d_skip, v_ssd_norm, v_sgu_ln_g, v_sgu_ln_b, v_sgu_w, v_sgu_b, v_w_out, v_ffn2_norm, v_ffn2_w_gate, v_ffn2_w_up, v_ffn2_w_down, v_final_norm):
    given = dict(x=x, ffn1_norm=ffn1_norm, ffn1_w_gate=ffn1_w_gate, ffn1_w_up=ffn1_w_up, ffn1_w_down=ffn1_w_down, mix_norm=mix_norm, w_in=w_in, conv_w=conv_w, conv_b=conv_b, dt_bias=dt_bias, a_log=a_log, d_skip=d_skip, ssd_norm=ssd_norm, sgu_ln_g=sgu_ln_g, sgu_ln_b=sgu_ln_b, sgu_w=sgu_w, sgu_b=sgu_b, w_out=w_out, ffn2_norm=ffn2_norm, ffn2_w_gate=ffn2_w_gate, ffn2_w_up=ffn2_w_up, ffn2_w_down=ffn2_w_down, final_norm=final_norm, loss_target=loss_target, m_ffn1_norm=m_ffn1_norm, m_ffn1_w_gate=m_ffn1_w_gate, m_ffn1_w_up=m_ffn1_w_up, m_ffn1_w_down=m_ffn1_w_down, m_mix_norm=m_mix_norm, m_w_in=m_w_in, m_conv_w=m_conv_w, m_conv_b=m_conv_b, m_dt_bias=m_dt_bias, m_a_log=m_a_log, m_d_skip=m_d_skip, m_ssd_norm=m_ssd_norm, m_sgu_ln_g=m_sgu_ln_g, m_sgu_ln_b=m_sgu_ln_b, m_sgu_w=m_sgu_w, m_sgu_b=m_sgu_b, m_w_out=m_w_out, m_ffn2_norm=m_ffn2_norm, m_ffn2_w_gate=m_ffn2_w_gate, m_ffn2_w_up=m_ffn2_w_up, m_ffn2_w_down=m_ffn2_w_down, m_final_norm=m_final_norm, v_ffn1_norm=v_ffn1_norm, v_ffn1_w_gate=v_ffn1_w_gate, v_ffn1_w_up=v_ffn1_w_up, v_ffn1_w_down=v_ffn1_w_down, v_mix_norm=v_mix_norm, v_w_in=v_w_in, v_conv_w=v_conv_w, v_conv_b=v_conv_b, v_dt_bias=v_dt_bias, v_a_log=v_a_log, v_d_skip=v_d_skip, v_ssd_norm=v_ssd_norm, v_sgu_ln_g=v_sgu_ln_g, v_sgu_ln_b=v_sgu_ln_b, v_sgu_w=v_sgu_w, v_sgu_b=v_sgu_b, v_w_out=v_w_out, v_ffn2_norm=v_ffn2_norm, v_ffn2_w_gate=v_ffn2_w_gate, v_ffn2_w_up=v_ffn2_w_up, v_ffn2_w_down=v_ffn2_w_down, v_final_norm=v_final_norm)
    T = given["x"].shape[0] * given["x"].shape[1]
    D = given["x"].shape[2]
    x0 = given["x"].reshape(T, D)
    tgt = given["loss_target"].reshape(T, D)
    c = lax.axis_index("c")

    bf = {n: given[n].astype(BF) for n in SHARDED if n not in ("w_in", "conv_w")}
    bf["w_in"] = _pack_w_in(given["w_in"]).astype(BF)
    bf["conv_w"] = given["conv_w"]
    gathers = {(i, gname): _gather_start([bf[n][i] for n in names], f"l{i}_{gname}")
               for i in range(DEPTH) for gname, names in GROUPS}
    token = functools.reduce(lambda a, b: a + b, [g.token for g in gathers.values()])

    def gathered(i, gname, after):
        return gathers[(i, gname)].wait(after)[3:]

    def mix_params(i, got):
        win = got[0].reshape(D, W_QKV + W_SSD + W_UV)
        rep = lambda v: jnp.repeat(v, HEAD)[None]
        ssd = (got[1].transpose(1, 0, 2).reshape(4, SSD_CONV_DIM), given["conv_b"][i][None],
               rep(given["dt_bias"][i]), rep(given["a_log"][i]), rep(given["d_skip"][i]), given["ssd_norm"][i][None])
        sgu = (given["sgu_ln_g"][i][None], given["sgu_ln_b"][i][None], given["sgu_w"][i],
               jnp.repeat(given["sgu_b"][i].T, HEAD, axis=1))
        return dict(mix_norm=given["mix_norm"][i][None], w_qkv=win[:, 0:W_QKV], w_ssd=win[:, W_QKV:W_QKV + W_SSD],
                    w_uv=win[:, W_QKV + W_SSD:], w_out=got[2].reshape(-1, D), ssd=ssd, sgu=sgu)

    x = x0
    tape = []
    for i in range(DEPTH):
        P = dict(ffn1=(given["ffn1_norm"][i][None] + (token if i == 0 else 0.0), *gathered(i, "ffn1", x)))
        x, s1 = _ffn_fwd(x, *P["ffn1"])
        P.update(mix_params(i, gathered(i, "mix", x)))
        x, s2 = _mix_fwd(x, P)
        P["ffn2"] = (given["ffn2_norm"][i][None], *gathered(i, "ffn2", x))
        x, s3 = _ffn_fwd(x, *P["ffn2"])
        tape.append((P, s1, s2, s3))
    loss_part, dx, dgf = _final_loss(x, given["final_norm"][None], tgt)

    me = 2 * lax.axis_index("x") + lax.axis_index("y")
    jobs = []

    def rs_begin(i, gname, gd):
        tag = f"l{i}_{gname}"
        names = [n for n in dict(GROUPS)[gname] if n != "conv_w"]
        jobs.append(dict(key=(i, gname), names=names, tag=tag, stage=1,
                         op=_to_sibling_start([_halved(gd[n]) for n in names], tag)))

    def rs_advance(job, after):
        k = len(job["names"])
        if job["stage"] == 1:
            got = job["op"].wait(after)
            job.update(stage=2, op=_to_chips_start([_pair_add(g, l, c) for g, l in zip(got[:k], got[k:])], job["tag"]))
        elif job["stage"] == 2:
            got = job["op"].wait(after)
            job.update(stage=3, op=_join_start([_chip_sum(p, l, me, c) for p, l in zip(got[:k], got[k:])], job["tag"]))
        elif job["stage"] == 3:
            job.update(stage=4, out=dict(zip(job["names"], job["op"].wait(after))))

    def tick(after, begin=None):
        for job in jobs:
            rs_advance(job, after)
        if begin is not None:
            rs_begin(*begin)
        return functools.reduce(lambda a, b: a + b, [j["op"].token for j in jobs if j["stage"] < 4], 0.0)

    grads = [dict() for _ in range(DEPTH)]
    tok = 0.0
    for i in reversed(range(DEPTH)):
        P, s1, s2, s3 = tape[i]
        g = grads[i]
        norm, wg, wu, wd = P["ffn2"]
        dx, dn2, g["ffn2_w_gate"], g["ffn2_w_up"], g["ffn2_w_down"] = _ffn_bwd(dx, s3, norm + tok, wg, wu, wd)
        tok = tick(dx, (i, "ffn2", g))
        dx, gm = _mix_bwd(dx, s2, {**P, "mix_norm": P["mix_norm"] + tok})
        g.update(gm)
        tok = tick(dx, (i, "mix", g))
        norm, wg, wu, wd = P["ffn1"]
        dx, dn1, g["ffn1_w_gate"], g["ffn1_w_up"], g["ffn1_w_down"] = _ffn_bwd(dx, s1, norm + tok, wg, wu, wd)
        tok = tick(dx, (i, "ffn1", g))
        g["ffn1_norm"], g["ffn2_norm"] = dn1[0], dn2[0]
    grad_x = dx.reshape(given["x"].shape)

    order = [n for n in SMALL if n != "final_norm"] + ["final_norm"]
    small = [jnp.stack([grads[i][n] for i in range(DEPTH)]) for n in order[:-1] + ["conv_w"]]
    small = small[:-1] + [dgf[0], small[-1], loss_part[0, 0:1]]
    n_small = sum(s.size for s in small)
    rows_small = -(-n_small // (128 * 8)) * 8

    def flat(arrs):
        fill = rows_small * 128 - sum(a.size for a in arrs)
        return jnp.concatenate([a.reshape(-1) for a in arrs] + [jnp.zeros((fill,), F32)]).reshape(rows_small, 128)

    gsmall = _all_reduce_small(flat(small)).reshape(-1)

    grad_w = {}
    off = 0
    for n in order:
        size = given[n].size
        grad_w[n] = gsmall[off:off + size].reshape(given[n].shape)
        off += size
    cw = gsmall[off:off + 2 * 4 * SSD_CONV_DIM].reshape(DEPTH, 4, SSD_CONV_DIM)
    grad_w["conv_w"] = lax.dynamic_slice_in_dim(cw, me * (SSD_CONV_DIM // 4), SSD_CONV_DIM // 4, axis=2)
    loss = gsmall[off + 2 * 4 * SSD_CONV_DIM]

    delta, new_m, new_v = {}, {}, {}
    shp = given["conv_w"].shape
    d, m2, v2 = _adamw(*[a.reshape(shp[0] * shp[1], shp[2])
                         for a in (given["conv_w"], grad_w["conv_w"], given["m_conv_w"], given["v_conv_w"])])
    delta["conv_w"], new_m["conv_w"], new_v["conv_w"] = d.reshape(shp), m2.reshape(shp), v2.reshape(shp)
    packed = [flat([given[pre + n] for n in order]) for pre in ("", "m_", "v_")]
    small_out = _adamw(packed[0], gsmall.reshape(rows_small, 128), packed[1], packed[2])
    outs = [o.reshape(-1) for o in small_out]
    off = 0
    for n in order:
        size = given[n].size
        for dst, o in zip((delta, new_m, new_v), outs):
            dst[n] = o[off:off + size].reshape(given[n].shape)
        off += size

    stepped, arrived = {}, {}

    def update_arrived(dep):
        out = None
        for job in jobs:
            if job["stage"] == 4 and not job.get("seen"):
                job["seen"] = True
                for n, full in job["out"].items():
                    view = (lambda a: jnp.swapaxes(a, 1, 2)) if n in TRANSPOSED else (lambda a: a)
                    arrived.setdefault(n, {})[job["key"][0]] = full.reshape(view(given[n]).shape[1:])
                    if len(arrived[n]) == DEPTH:
                        res = _adamw_pair(view(given[n]), arrived[n][0], arrived[n][1], view(given["m_" + n]),
                                          view(given["v_" + n]), dep)
                        stepped[n] = [view(r) for r in res]
                        out = res[0]
        return out

    after = small_out[0]
    while any(j["stage"] < 4 for j in jobs):
        done = update_arrived(jnp.zeros((8, 128), F32) + tok)
        after = after if done is None else done
        tok = tick(after)
    update_arrived(jnp.zeros((8, 128), F32) + tok)
    for n, (d, m2, v2, g) in stepped.items():
        delta[n], new_m[n], new_v[n], grad_w[n] = d, m2, v2, g

    return (loss, grad_x, *[grad_w[n] for n in WEIGHTS], *[delta[n] for n in WEIGHTS],
            *[new_m[n] for n in WEIGHTS], *[new_v[n] for n in WEIGHTS])
```

```python
import functools
import math

import jax
import jax.numpy as jnp
from jax import lax
from jax.experimental import pallas as pl
from jax.experimental.pallas import tpu as pltpu

F32 = jnp.float32
BF = jnp.bfloat16

RMS_EPS = 1e-6
LN_EPS = 1e-5
SEQ = 2048
CHUNK = 128
N_CHUNK = SEQ // CHUNK
ATT_W = 384
HEAD = 64
SSD_W = 384
SSD_CONV_DIM = 896
SSD_STATE = 128
SGU_W = 256
DILATIONS = (1, 4, 16)
W_QKV = 3 * ATT_W
W_SSD = SSD_CONV_DIM + SSD_W + SSD_W
W_UV = 2 * SGU_W
ADAM_LR = 0.001
ADAM_B1 = 0.9
ADAM_B2 = 0.999
ADAM_EPS = 1e-08
ADAM_WD = 0.01
ADAM_STEP = 10
NEG = -1e30
ATTN_BWD_VMEM = 48 * 2 ** 20


def _dot(a, b):
    return jnp.dot(a, b, preferred_element_type=F32)


def _dot_nt(a, b):
    return lax.dot_general(a, b, (((1,), (1,)), ((), ())), preferred_element_type=F32)


def _dot_tn(a, b):
    return lax.dot_general(a, b, (((0,), (0,)), ((), ())), preferred_element_type=F32)


def _sigmoid(x):
    return 1.0 / (1.0 + jnp.exp(-x))


def _call(body, *, name, grid, in_specs, out_specs, out_shape, scratch=(), sem=None, vmem=None):
    return pl.pallas_call(
        body, name=name, grid=grid, in_specs=in_specs, out_specs=out_specs, out_shape=out_shape,
        scratch_shapes=list(scratch),
        compiler_params=pltpu.CompilerParams(dimension_semantics=sem, vmem_limit_bytes=vmem),
    )


def _tile(n, want):
    t = min(n, want)
    while n % t:
        t //= 2
    return t


def _rms_fwd(x, g):
    T, D = x.shape
    tm = _tile(T, 512)

    def body(x_ref, g_ref, h_ref):
        xf = x_ref[...]
        r = lax.rsqrt(jnp.mean(xf * xf, axis=-1, keepdims=True) + RMS_EPS)
        h_ref[...] = (xf * r * g_ref[...]).astype(BF)

    return _call(body, name="rms_fwd", grid=(T // tm,),
                 in_specs=[pl.BlockSpec((tm, D), lambda i: (i, 0)), pl.BlockSpec((1, D), lambda i: (0, 0))],
                 out_specs=pl.BlockSpec((tm, D), lambda i: (i, 0)),
                 out_shape=jax.ShapeDtypeStruct((T, D), BF), sem=("parallel",))(x, g)


def _rms_bwd(x, g, dh, dres):
    T, D = x.shape
    tm = _tile(T, 512)

    def body(x_ref, g_ref, dh_ref, dr_ref, dx_ref, dg_ref):
        @pl.when(pl.program_id(0) == 0)
        def _():
            dg_ref[...] = jnp.zeros_like(dg_ref)

        xf = x_ref[...]
        r = lax.rsqrt(jnp.mean(xf * xf, axis=-1, keepdims=True) + RMS_EPS)
        dh_ = dh_ref[...]
        u = dh_ * g_ref[...]
        mu = jnp.mean(u * xf, axis=-1, keepdims=True)
        dx_ref[...] = dr_ref[...] + r * (u - xf * (r * r * mu))
        dg_ref[...] += jnp.sum(dh_ * xf * r, axis=0, keepdims=True)

    row = pl.BlockSpec((tm, D), lambda i: (i, 0))
    vec = pl.BlockSpec((1, D), lambda i: (0, 0))
    return _call(body, name="rms_bwd", grid=(T // tm,), in_specs=[row, vec, row, row], out_specs=[row, vec],
                 out_shape=[jax.ShapeDtypeStruct((T, D), F32), jax.ShapeDtypeStruct((1, D), F32)],
                 sem=("arbitrary",))(x, g, dh, dres)


def _final_loss(x, g, tgt):
    T, D = x.shape
    tm = _tile(T, 512)

    def body(x_ref, g_ref, t_ref, l_ref, dx_ref, dg_ref):
        @pl.when(pl.program_id(0) == 0)
        def _():
            dg_ref[...] = jnp.zeros_like(dg_ref)
            l_ref[...] = jnp.zeros_like(l_ref)

        xf = x_ref[...]
        gg = g_ref[...]
        r = lax.rsqrt(jnp.mean(xf * xf, axis=-1, keepdims=True) + RMS_EPS)
        xn = xf * r
        e = xn * gg - t_ref[...]
        part = 0.5 * jnp.sum(jnp.mean(e * e, axis=-1, keepdims=True), axis=0, keepdims=True)
        l_ref[...] += jnp.broadcast_to(part, l_ref.shape)
        dy = e * (1.0 / D)
        u = dy * gg
        mu = jnp.mean(u * xf, axis=-1, keepdims=True)
        dx_ref[...] = r * (u - xf * (r * r * mu))
        dg_ref[...] += jnp.sum(dy * xn, axis=0, keepdims=True)

    row = pl.BlockSpec((tm, D), lambda i: (i, 0))
    vec = pl.BlockSpec((1, D), lambda i: (0, 0))
    lsp = pl.BlockSpec((1, 128), lambda i: (0, 0))
    return _call(body, name="final_loss", grid=(T // tm,), in_specs=[row, vec, row], out_specs=[lsp, row, vec],
                 out_shape=[jax.ShapeDtypeStruct((1, 128), F32), jax.ShapeDtypeStruct((T, D), F32),
                            jax.ShapeDtypeStruct((1, D), F32)],
                 sem=("arbitrary",))(x, g, tgt)


def _slabs(tm, n=2):
    return [slice(k * tm // n, (k + 1) * tm // n) for k in range(n)] if tm % (16 * n) == 0 else [slice(0, tm)]


def _resident(shape):
    return pl.BlockSpec(shape, lambda *_: (0,) * len(shape), pipeline_mode=pl.Buffered(1))


def _ffn_fwd_k(x, gn, wg, wu, wd):
    T, D = x.shape
    NS, _, Fs = wg.shape
    tm = _tile(T, 512)

    def body(x_ref, gn_ref, wg_ref, wu_ref, wd_ref, o_ref, h_ref, s1_ref, s2_ref, a_ref, hs, acc):
        j = pl.program_id(1)

        @pl.when(j == 0)
        def _():
            xf = x_ref[...]
            r = lax.rsqrt(jnp.mean(xf * xf, axis=-1, keepdims=True) + RMS_EPS)
            hs[...] = (xf * r * gn_ref[...]).astype(BF)
            h_ref[...] = hs[...]
            acc[...] = jnp.zeros_like(acc)

        h = hs[...]
        g = _dot(h, wg_ref[j])
        u = _dot(h, wu_ref[j])
        sg = _sigmoid(g)
        s1 = g * sg
        a = (s1 * u).astype(BF)
        s1_ref[...] = s1.astype(BF)
        s2_ref[...] = (u * (sg * (1.0 + g * (1.0 - sg)))).astype(BF)
        a_ref[...] = a
        acc[...] += _dot(a, wd_ref[j])

        @pl.when(j == NS - 1)
        def _():
            o_ref[...] = x_ref[...] + 0.5 * acc[...]

    row = pl.BlockSpec((tm, D), lambda i, j: (i, 0))
    act = pl.BlockSpec((None, tm, Fs), lambda i, j: (j, i, 0))
    sh = jax.ShapeDtypeStruct((NS, T, Fs), BF)
    return _call(body, name="ffn_fwd", grid=(T // tm, NS),
                 in_specs=[row, pl.BlockSpec((1, D), lambda i, j: (0, 0)), _resident(wg.shape), _resident(wu.shape),
                           _resident(wd.shape)],
                 out_specs=[row, row, act, act, act],
                 out_shape=[jax.ShapeDtypeStruct((T, D), F32), jax.ShapeDtypeStruct((T, D), BF), sh, sh, sh],
                 scratch=[pltpu.VMEM((tm, D), BF), pltpu.VMEM((tm, D), F32)],
                 sem=("parallel", "arbitrary"))(x, gn, wg, wu, wd)


def _ffn_bwd_k1(dxo, x, gn, s1, s2, wg, wu, wd):
    NS, T, Fs = s1.shape
    D = x.shape[1]
    tm = _tile(T, 512)

    def body(dxo_ref, x_ref, gn_ref, s1_ref, s2_ref, wg_ref, wu_ref, wd_ref,
             dx_ref, dgn_ref, dg_ref, du_ref, dy_ref, dys, acc):
        i, j = pl.program_id(0), pl.program_id(1)

        @pl.when((i == 0) & (j == 0))
        def _():
            dgn_ref[...] = jnp.zeros_like(dgn_ref)

        @pl.when(j == 0)
        def _():
            dys[...] = (0.5 * dxo_ref[...]).astype(BF)
            dy_ref[...] = dys[...]
            acc[...] = jnp.zeros_like(acc)

        for rows in _slabs(tm):
            da = _dot_nt(dys[rows, :], wd_ref[j])
            dg = (da * s2_ref[rows, :].astype(F32)).astype(BF)
            du = (da * s1_ref[rows, :].astype(F32)).astype(BF)
            dg_ref[rows, :] = dg
            du_ref[rows, :] = du
            acc[rows, :] += _dot_nt(dg, wg_ref[j]) + _dot_nt(du, wu_ref[j])

        @pl.when(j == NS - 1)
        def _():
            xf = x_ref[...]
            r = lax.rsqrt(jnp.mean(xf * xf, axis=-1, keepdims=True) + RMS_EPS)
            dh = acc[...]
            uu = dh * gn_ref[...]
            mu = jnp.mean(uu * xf, axis=-1, keepdims=True)
            dx_ref[...] = dxo_ref[...] + r * (uu - xf * (r * r * mu))
            dgn_ref[...] += jnp.sum(dh * xf * r, axis=0, keepdims=True)

    row = pl.BlockSpec((tm, D), lambda i, j: (i, 0))
    vec = pl.BlockSpec((1, D), lambda i, j: (0, 0))
    act = pl.BlockSpec((None, tm, Fs), lambda i, j: (j, i, 0))
    sh = jax.ShapeDtypeStruct((NS, T, Fs), BF)
    return _call(body, name="ffn_bwd_x", grid=(T // tm, NS),
                 in_specs=[row, row, vec, act, act, _resident(wg.shape), _resident(wu.shape), _resident(wd.shape)],
                 out_specs=[row, vec, act, act, row],
                 out_shape=[jax.ShapeDtypeStruct((T, D), F32), jax.ShapeDtypeStruct((1, D), F32), sh, sh,
                            jax.ShapeDtypeStruct((T, D), BF)],
                 scratch=[pltpu.VMEM((tm, D), BF), pltpu.VMEM((tm, D), F32)],
                 sem=("arbitrary", "arbitrary"))(dxo, x, gn, s1, s2, wg, wu, wd)


def _ffn_bwd_k2(hb, dyb, a, dg, du):
    NS, T, Fs = a.shape
    D = hb.shape[1]
    tk = _tile(T, 512)

    def body(h_ref, dy_ref, a_ref, dg_ref, du_ref, og_ref, ou_ref, od_ref):
        @pl.when(pl.program_id(1) == 0)
        def _():
            og_ref[...] = jnp.zeros_like(og_ref)
            ou_ref[...] = jnp.zeros_like(ou_ref)
            od_ref[...] = jnp.zeros_like(od_ref)

        h = h_ref[...]
        og_ref[...] += _dot_tn(dg_ref[...], h)
        ou_ref[...] += _dot_tn(du_ref[...], h)
        od_ref[...] += _dot_tn(a_ref[...], dy_ref[...])

    row = pl.BlockSpec((tk, D), lambda j, k: (k, 0))
    act = pl.BlockSpec((None, tk, Fs), lambda j, k: (j, k, 0))
    return _call(body, name="ffn_bwd_w", grid=(NS, T // tk), in_specs=[row, row, act, act, act],
                 out_specs=[pl.BlockSpec((None, Fs, D), lambda j, k: (j, 0, 0))] * 3,
                 out_shape=[jax.ShapeDtypeStruct((NS, Fs, D), F32)] * 3,
                 sem=("parallel", "arbitrary"))(hb, dyb, a, dg, du)


def _mm_nn(a, b, res=None, out_dtype=F32):
    T, K = a.shape
    N = b.shape[1]
    tm = _tile(T, 512)
    tn = N if N <= 2048 else _tile(N, 1024)

    def body(*refs):
        if res is None:
            a_ref, b_ref, o_ref = refs
            o_ref[...] = _dot(a_ref[...], b_ref[...]).astype(out_dtype)
        else:
            a_ref, b_ref, r_ref, o_ref = refs
            o_ref[...] = (r_ref[...] + _dot(a_ref[...], b_ref[...])).astype(out_dtype)

    o = pl.BlockSpec((tm, tn), lambda i, j: (i, j))
    ins = [pl.BlockSpec((tm, K), lambda i, j: (i, 0)), pl.BlockSpec((K, tn), lambda i, j: (0, j))]
    args = [a, b]
    if res is not None:
        ins.append(o)
        args.append(res)
    return _call(body, name="mm_nn", grid=(T // tm, N // tn), in_specs=ins, out_specs=o,
                 out_shape=jax.ShapeDtypeStruct((T, N), out_dtype), sem=("parallel", "parallel"))(*args)


def _mm_nt(a, b, res=None):
    T, K = a.shape
    N = b.shape[0]
    tm = _tile(T, 512)

    def body(*refs):
        if res is None:
            a_ref, b_ref, o_ref = refs
            o_ref[...] = _dot_nt(a_ref[...].astype(BF), b_ref[...])
        else:
            a_ref, b_ref, r_ref, o_ref = refs
            o_ref[...] = r_ref[...] + _dot_nt(a_ref[...].astype(BF), b_ref[...])

    o = pl.BlockSpec((tm, N), lambda i: (i, 0))
    ins = [pl.BlockSpec((tm, K), lambda i: (i, 0)), pl.BlockSpec((N, K), lambda i: (0, 0))]
    args = [a, b]
    if res is not None:
        ins.append(o)
        args.append(res)
    return _call(body, name="mm_nt", grid=(T // tm,), in_specs=ins, out_specs=o,
                 out_shape=jax.ShapeDtypeStruct((T, N), F32), sem=("parallel",))(*args)


def _mm_tn(a, b):
    T, M = a.shape
    N = b.shape[1]
    tk = _tile(T, 512)
    tmm = _tile(M, 512)

    def body(a_ref, b_ref, o_ref):
        @pl.when(pl.program_id(1) == 0)
        def _():
            o_ref[...] = jnp.zeros_like(o_ref)

        o_ref[...] += _dot_tn(a_ref[...].astype(BF), b_ref[...].astype(BF))

    return _call(body, name="mm_tn", grid=(M // tmm, T // tk),
                 in_specs=[pl.BlockSpec((tk, tmm), lambda i, k: (k, i)), pl.BlockSpec((tk, N), lambda i, k: (k, 0))],
                 out_specs=pl.BlockSpec((tmm, N), lambda i, k: (i, 0)),
                 out_shape=jax.ShapeDtypeStruct((M, N), F32), sem=("parallel", "arbitrary"))(a, b)


def _lane_mask(e, width=128):
    return (lax.broadcasted_iota(jnp.int32, (1, width), 1) // HEAD) == e


def _band_mask(n):
    qi = lax.broadcasted_iota(jnp.int32, (CHUNK, 2 * CHUNK), 0)
    kj = lax.broadcasted_iota(jnp.int32, (CHUNK, 2 * CHUNK), 1)
    dist = qi + CHUNK - kj
    return (dist >= 0) & (dist <= CHUNK) & ((kj >= CHUNK) | (n > 0))


def _sub_rows(r, block, dil):
    if dil == 1:
        return pl.ds(pl.multiple_of(block * CHUNK, CHUNK), CHUNK)
    return pl.ds(r + dil * CHUNK * block, CHUNK, stride=dil)


def _attn_specs(T, dil):
    B, nb = T // SEQ, SEQ // (CHUNK * dil)
    once = dict(pipeline_mode=pl.Buffered(1))
    q_like = lambda col: pl.BlockSpec((CHUNK * dil, 128), lambda b, n, r: (b * nb + n, col), **(once if nb == 1 else {}))
    k_like = lambda col: pl.BlockSpec((SEQ, 128), lambda b, n, r: (b, col), **once)
    return B, nb, q_like, k_like


def _attn_fwd(qkv, dil):
    T = qkv.shape[0]
    B, nb, q_like, k_like = _attn_specs(T, dil)
    scale = HEAD ** -0.5

    def body(*refs):
        q_t, k_t, v_t, o_t, l_t = refs[0:3], refs[3:6], refs[6:9], refs[9:12], refs[12:15]
        n, r = pl.program_id(1), pl.program_id(2)
        mine = _sub_rows(r, 0, dil)
        cur, prv = _sub_rows(r, n, dil), _sub_rows(r, jnp.maximum(n - 1, 0), dil)
        mask = _band_mask(n)
        for t in range(3):
            qt = q_t[t][mine, :].astype(BF)
            kt = jnp.concatenate([k_t[t][prv, :], k_t[t][cur, :]], axis=0).astype(BF)
            vt = jnp.concatenate([v_t[t][prv, :], v_t[t][cur, :]], axis=0).astype(BF)
            o_pair = jnp.zeros((CHUNK, 128), F32)
            l_pair = jnp.zeros((CHUNK, 128), F32)
            for e in range(2):
                lm = _lane_mask(e)
                s = _dot_nt(jnp.where(lm, qt, jnp.zeros_like(qt)), kt) * scale
                s = jnp.where(mask, s, NEG)
                m = jnp.max(s, axis=-1, keepdims=True)
                p = jnp.exp(s - m)
                den = jnp.sum(p, axis=-1, keepdims=True)
                o = _dot(p.astype(BF), vt) / den
                o_pair = jnp.where(lm, o, o_pair)
                l_pair = jnp.where(lm, m + jnp.log(den), l_pair)
            o_t[t][mine, :] = o_pair
            l_t[t][mine, :] = l_pair

    out_spec = pl.BlockSpec((CHUNK * dil, 128), lambda b, n, r: (b * nb + n, 0))
    sh = jax.ShapeDtypeStruct((T, 128), F32)
    outs = _call(
        body, name=f"attn_fwd_d{dil}", grid=(B, nb, dil),
        in_specs=[q_like(t) for t in range(3)] + [k_like(3 + t) for t in range(3)] + [k_like(6 + t) for t in range(3)],
        out_specs=[out_spec] * 6, out_shape=[sh] * 6, sem=("parallel", "arbitrary", "arbitrary"))(*([qkv] * 9))
    return list(outs[0:3]), list(outs[3:6])


def _attn_combine(branches):
    T = branches[0][0][0].shape[0]
    tm = _tile(T, 512)

    def body(*refs):
        y_ref, l_ref = refs[-2:]
        for t in range(3):
            o = [refs[6 * i + t][...] for i in range(3)]
            a, b, c = [refs[6 * i + 3 + t][...] for i in range(3)]
            m = jnp.maximum(jnp.maximum(a, b), c)
            ea, eb, ec = jnp.exp(a - m), jnp.exp(b - m), jnp.exp(c - m)
            z = ea + eb + ec
            y_ref[:, 128 * t:128 * (t + 1)] = (ea * o[0] + eb * o[1] + ec * o[2]) / z
            l_ref[:, 128 * t:128 * (t + 1)] = m + jnp.log(z)

    tile = pl.BlockSpec((tm, 128), lambda i: (i, 0))
    row = pl.BlockSpec((tm, ATT_W), lambda i: (i, 0))
    sh = jax.ShapeDtypeStruct((T, ATT_W), F32)
    flat = [a for o_t, l_t in branches for a in (*o_t, *l_t)]
    return _call(body, name="attn_combine", grid=(T // tm,), in_specs=[tile] * 18, out_specs=[row, row],
                 out_shape=[sh, sh], sem=("parallel",))(*flat)


def _attn_bwd(qkv, do, out, lse, dil):
    T = qkv.shape[0]
    B, nb, q_like, k_like = _attn_specs(T, dil)
    scale = HEAD ** -0.5

    def body(*refs):
        q_t, k_t, v_t = refs[0:3], refs[3:6], refs[6:9]
        do_t, out_t, lse_t = refs[9:12], refs[12:15], refs[15:18]
        dq_t, dk_t, dv_t = refs[18:21], refs[21:24], refs[24:27]
        n, r = pl.program_id(1), pl.program_id(2)

        @pl.when((n == 0) & (r == 0))
        def _():
            for t in range(3):
                dk_t[t][...] = jnp.zeros_like(dk_t[t])
                dv_t[t][...] = jnp.zeros_like(dv_t[t])

        mine = _sub_rows(r, 0, dil)
        cur, prv = _sub_rows(r, n, dil), _sub_rows(r, jnp.maximum(n - 1, 0), dil)
        mask = _band_mask(n)
        for t in range(3):
            qt = q_t[t][mine, :].astype(BF)
            kt = jnp.concatenate([k_t[t][prv, :], k_t[t][cur, :]], axis=0).astype(BF)
            vt = jnp.concatenate([v_t[t][prv, :], v_t[t][cur, :]], axis=0).astype(BF)
            do_ = do_t[t][mine, :]
            dlt = do_ * out_t[t][mine, :]
            ls = lse_t[t][mine, :]
            dq_pair = jnp.zeros((CHUNK, 128), F32)
            dk_acc = jnp.zeros((2 * CHUNK, 128), F32)
            dv_acc = jnp.zeros((2 * CHUNK, 128), F32)
            for e in range(2):
                lm = _lane_mask(e)
                qm = jnp.where(lm, qt, jnp.zeros_like(qt))
                s = _dot_nt(qm, kt) * scale
                p = jnp.exp(jnp.where(mask, s - ls[:, HEAD * e:HEAD * e + 1], NEG))
                dom = jnp.where(lm, do_, 0.0).astype(BF)
                dv_acc += _dot_tn(p.astype(BF), dom)
                dp = _dot_nt(dom, vt)
                delta = jnp.sum(jnp.where(lm, dlt, 0.0), axis=-1, keepdims=True)
                ds = (p * (dp - delta) * scale).astype(BF)
                dq_pair += jnp.where(lm, _dot(ds, kt), 0.0)
                dk_acc += _dot_tn(ds, qm)
            dq_t[t][mine, :] = dq_pair
            dk_t[t][cur, :] = dk_t[t][cur, :] + dk_acc[CHUNK:]
            dk_t[t][prv, :] = dk_t[t][prv, :] + dk_acc[:CHUNK]
            dv_t[t][cur, :] = dv_t[t][cur, :] + dv_acc[CHUNK:]
            dv_t[t][prv, :] = dv_t[t][prv, :] + dv_acc[:CHUNK]

    q_out = pl.BlockSpec((CHUNK * dil, 128), lambda b, n, r: (b * nb + n, 0))
    k_out = pl.BlockSpec((SEQ, 128), lambda b, n, r: (b, 0))
    sh = jax.ShapeDtypeStruct((T, 128), F32)
    tiles = lambda: [q_like(t) for t in range(3)]
    return list(_call(
        body, name=f"attn_bwd_d{dil}", grid=(B, nb, dil),
        in_specs=tiles() + [k_like(3 + t) for t in range(3)] + [k_like(6 + t) for t in range(3)]
        + tiles() + tiles() + tiles(),
        out_specs=[q_out] * 3 + [k_out] * 6, out_shape=[sh] * 9,
        sem=("parallel", "arbitrary", "arbitrary"), vmem=ATTN_BWD_VMEM)(*([qkv] * 9 + [do] * 3 + [out] * 3 + [lse] * 3)))


def _sum_branches(parts):
    T = parts[0][0].shape[0]
    tm = _tile(T, 512)

    def body(*refs):
        o_ref = refs[-1]
        for c in range(9):
            acc = refs[c][...] + refs[9 + c][...] + refs[18 + c][...]
            o_ref[:, 128 * c:128 * (c + 1)] = acc.astype(BF)

    tile = pl.BlockSpec((tm, 128), lambda i: (i, 0))
    flat = [a for br in parts for a in br]
    return _call(body, name="attn_sum_branches", grid=(T // tm,), in_specs=[tile] * 27,
                 out_specs=pl.BlockSpec((tm, W_QKV), lambda i: (i, 0)),
                 out_shape=jax.ShapeDtypeStruct((T, W_QKV), BF), sem=("parallel",))(*flat)


def _silu(x):
    return x * _sigmoid(x)


def _dsilu(x):
    s = _sigmoid(x)
    return s * (1.0 + x * (1.0 - s))


def _log1p(u):
    return jnp.where(u < 0.01, u * (1.0 - u * (0.5 - u * (1.0 / 3.0))), jnp.log(1.0 + u))


def _softplus(x):
    return jnp.maximum(x, 0.0) + _log1p(jnp.exp(-jnp.abs(x)))


def _cumsum_rows(x, reverse=False):
    n = x.shape[0]
    rows = lax.broadcasted_iota(jnp.int32, x.shape, 0)
    k = 1
    while k < n:
        if reverse:
            x = x + jnp.where(rows < n - k, pltpu.roll(x, n - k, 0), 0.0)
        else:
            x = x + jnp.where(rows >= k, pltpu.roll(x, k, 0), 0.0)
        k *= 2
    return x


def _tri():
    r = lax.broadcasted_iota(jnp.int32, (CHUNK, CHUNK), 0)
    c = lax.broadcasted_iota(jnp.int32, (CHUNK, CHUNK), 1)
    return r >= c


def _row_mask(e):
    return (lax.broadcasted_iota(jnp.int32, (128, 1), 0) // HEAD) == e


def _first_lane(e):
    return lax.broadcasted_iota(jnp.int32, (1, 128), 1) == HEAD * e


def _ssd_pre(x_ref, halo_ref, first, cw_ref, cb_ref, dtb_ref, al_ref, ext):
    row = x_ref[...]
    z = row[:, SSD_CONV_DIM:SSD_CONV_DIM + SSD_W]
    u = row[:, SSD_CONV_DIM + SSD_W:] + dtb_ref[...]
    ext[0:8, :] = jnp.where(first, 0.0, halo_ref[:, 0:SSD_CONV_DIM])
    ext[8:8 + CHUNK, :] = row[:, 0:SSD_CONV_DIM]
    xc = cb_ref[...]
    for j in range(4):
        xc = xc + cw_ref[j:j + 1, :] * ext[pl.ds(5 + j, CHUNK), :]
    xa = _silu(xc)
    dt = _softplus(u)
    a = dt * (-jnp.exp(al_ref[...]))
    A = _cumsum_rows(a)
    return dict(z=z, u=u, xc=xc, xs=xa[:, 0:SSD_W], Bm=xa[:, SSD_W:SSD_W + 256], Cm=xa[:, SSD_W + 256:],
                dt=dt, a=a, A=A, AT=A.T, eA=jnp.exp(A), wdec=jnp.exp(A[CHUNK - 1:CHUNK, :] - A),
                dtot=jnp.exp(A[CHUNK - 1:CHUNK, :]))


def _ssd_y(p, hp_ref, dskip):
    tri = _tri()
    X = p["xs"] * p["dt"]
    Bb = [p["Bm"][:, 128 * g:128 * (g + 1)].astype(BF) for g in range(2)]
    Cb = [p["Cm"][:, 128 * g:128 * (g + 1)].astype(BF) for g in range(2)]
    CB = [_dot_nt(Cb[g], Bb[g]) for g in range(2)]
    tiles = []
    for t in range(3):
        sl = slice(128 * t, 128 * (t + 1))
        hpb = hp_ref[sl, :].astype(BF)
        acc = jnp.zeros((CHUNK, 128), F32)
        for e in range(2):
            h = 2 * t + e
            g, col = h // 3, HEAD * h
            lm = _lane_mask(e)
            L = jnp.exp(jnp.where(tri, p["A"][:, col:col + 1] - p["AT"][col:col + 1, :], NEG))
            yd = _dot((CB[g] * L).astype(BF), jnp.where(lm, X[:, sl], 0.0).astype(BF))
            yo = _dot_nt(Cb[g], hpb) * p["eA"][:, sl]
            acc = acc + jnp.where(lm, yd + yo, 0.0)
        tiles.append(acc)
    return jnp.concatenate(tiles, axis=1) + dskip * p["xs"], X, Bb, Cb, CB


def _group_stats(v):
    g0 = lax.broadcasted_iota(jnp.int32, (1, SSD_W), 1) < SSD_W // 2
    m0 = jnp.sum(jnp.where(g0, v, 0.0), axis=-1, keepdims=True) * (2.0 / SSD_W)
    m1 = jnp.sum(jnp.where(g0, 0.0, v), axis=-1, keepdims=True) * (2.0 / SSD_W)
    return jnp.where(g0, m0, m1)


def _ssd_specs(T, rev):
    B = T // SEQ

    def chunk(b, c):
        return b * N_CHUNK + (N_CHUNK - 1 - c if rev else c)

    row = pl.BlockSpec((CHUNK, W_SSD), lambda b, c: (chunk(b, c), 0))
    halo = pl.BlockSpec((8, W_SSD), lambda b, c: (jnp.maximum(chunk(b, c) * (CHUNK // 8) - 1, 0), 0))
    hp = pl.BlockSpec((None, SSD_W, SSD_STATE), lambda b, c: (chunk(b, c), 0, 0))
    y = pl.BlockSpec((CHUNK, SSD_W), lambda b, c: (chunk(b, c), 0))
    const = lambda r, w: pl.BlockSpec((r, w), lambda b, c: (0, 0))
    params = [const(4, SSD_CONV_DIM), const(1, SSD_CONV_DIM)] + [const(1, SSD_W)] * 4
    return B, row, halo, hp, y, const, params


def _ssd_fwd(sin, conv_w, conv_b, dtb, alog, dskip, norm_g):
    T = sin.shape[0]
    B, row, halo, hp, y, const, params = _ssd_specs(T, False)

    def body(x_ref, halo_ref, cw_ref, cb_ref, dtb_ref, al_ref, dk_ref, ng_ref, y_ref, hp_ref, ext, hst):
        c = pl.program_id(1)

        @pl.when(c == 0)
        def _():
            hst[...] = jnp.zeros_like(hst)

        p = _ssd_pre(x_ref, halo_ref, c == 0, cw_ref, cb_ref, dtb_ref, al_ref, ext)
        yv, X, Bb, Cb, CB = _ssd_y(p, hst, dk_ref[...])
        hp_ref[...] = hst[...]
        for t in range(3):
            sl = slice(128 * t, 128 * (t + 1))
            old = hst[sl, :]
            new = old
            for e in range(2):
                h = 2 * t + e
                g, col = h // 3, HEAD * h
                st = _dot_tn(jnp.where(_lane_mask(e), X[:, sl] * p["wdec"][:, sl], 0.0).astype(BF), Bb[g])
                new = jnp.where(_row_mask(e), old * p["dtot"][:, col:col + 1] + st, new)
            hst[sl, :] = new
        y2 = yv * _silu(p["z"])
        r = lax.rsqrt(_group_stats(y2 * y2) + RMS_EPS)
        y_ref[...] = y2 * r * ng_ref[...]

    return _call(body, name="ssd_fwd", grid=(B, N_CHUNK), in_specs=[row, halo] + params, out_specs=[y, hp],
                 out_shape=[jax.ShapeDtypeStruct((T, SSD_W), F32),
                            jax.ShapeDtypeStruct((T // CHUNK, SSD_W, SSD_STATE), F32)],
                 scratch=[pltpu.VMEM((8 + CHUNK, SSD_CONV_DIM), F32), pltpu.VMEM((SSD_W, SSD_STATE), F32)],
                 sem=("parallel", "arbitrary"))(sin, sin, conv_w, conv_b, dtb, alog, dskip, norm_g)


def _ssd_bwd(sin, hprev, dy3, conv_w, conv_b, dtb, alog, dskip, norm_g):
    T = sin.shape[0]
    B, row, halo, hp, y, const, params = _ssd_specs(T, True)

    def body(x_ref, halo_ref, hp_ref, dy_ref, cw_ref, cb_ref, dtb_ref, al_ref, dk_ref, ng_ref,
             dx_ref, dcw_ref, dcb_ref, dvec_ref, ext, ext2, dh):
        c = pl.program_id(1)

        @pl.when((pl.program_id(0) == 0) & (c == 0))
        def _():
            dcw_ref[...] = jnp.zeros_like(dcw_ref)
            dcb_ref[...] = jnp.zeros_like(dcb_ref)
            dvec_ref[...] = jnp.zeros_like(dvec_ref)

        @pl.when(c == 0)
        def _():
            dh[...] = jnp.zeros_like(dh)
            ext2[CHUNK:CHUNK + 8, :] = jnp.zeros((8, SSD_CONV_DIM), F32)

        p = _ssd_pre(x_ref, halo_ref, c == N_CHUNK - 1, cw_ref, cb_ref, dtb_ref, al_ref, ext)
        dskip_ = dk_ref[...]
        yv, X, Bb, Cb, CB = _ssd_y(p, hp_ref, dskip_)
        xs, z, A, AT = p["xs"], p["z"], p["A"], p["AT"]

        sz = _silu(z)
        y2 = yv * sz
        r = lax.rsqrt(_group_stats(y2 * y2) + RMS_EPS)
        dy3_ = dy_ref[...]
        uu = dy3_ * ng_ref[...]
        dy2 = r * (uu - y2 * (r * r * _group_stats(uu * y2)))
        dy = dy2 * sz
        dz = dy2 * yv * _dsilu(z)

        tri = _tri()
        rows = lax.broadcasted_iota(jnp.int32, (CHUNK, 1), 0)
        dG = [jnp.zeros((CHUNK, CHUNK), F32) for _ in range(2)]
        dB = [jnp.zeros((CHUNK, SSD_STATE), F32) for _ in range(2)]
        dC = [jnp.zeros((CHUNK, SSD_STATE), F32) for _ in range(2)]
        dX_t, dA_t, ddtx_t = [], [], []
        for t in range(3):
            sl = slice(128 * t, 128 * (t + 1))
            hp_t = hp_ref[sl, :]
            hpb = hp_t.astype(BF)
            dhc = dh[sl, :]
            dh_new = jnp.zeros((128, SSD_STATE), F32)
            dX = jnp.zeros((CHUNK, 128), F32)
            dA = jnp.zeros((CHUNK, 128), F32)
            ddtx = jnp.zeros((CHUNK, 128), F32)
            for e in range(2):
                h = 2 * t + e
                g, col = h // 3, HEAD * h
                lm, rm, fl = _lane_mask(e), _row_mask(e), _first_lane(e)
                L = jnp.exp(jnp.where(tri, A[:, col:col + 1] - AT[col:col + 1, :], NEG))
                Mf = CB[g] * L
                Xm = jnp.where(lm, X[:, sl], 0.0)
                Xmb = Xm.astype(BF)
                dyh = jnp.where(lm, dy[:, sl], 0.0)
                dyb = dyh.astype(BF)
                dXh = _dot_tn(Mf.astype(BF), dyb)
                dM = jnp.where(tri, _dot_nt(dyb, Xmb), 0.0)
                Wm = dM * Mf
                dAc = jnp.sum(Wm, axis=-1, keepdims=True) - jnp.sum(Wm.T, axis=-1, keepdims=True)
                dG[g] = dG[g] + dM * L
                eAt = p["eA"][:, sl]
                yo = _dot_nt(Cb[g], hpb)
                dyo = (dyh * eAt).astype(BF)
                dC[g] = dC[g] + _dot(dyo, hpb)
                dh_new = dh_new + _dot_tn(dyo, Cb[g])
                dAc = dAc + jnp.sum(dyh * yo * eAt, axis=-1, keepdims=True)
                dHn = jnp.where(rm, dhc, 0.0)
                dHnb = dHn.astype(BF)
                dec = p["dtot"][:, col:col + 1]
                dh_new = dh_new + dec * dHn
                Z = _dot_nt(Bb[g], dHnb)
                wt = p["wdec"][:, sl]
                xi = jnp.sum(Xm * Z, axis=-1, keepdims=True) * p["wdec"][:, col:col + 1]
                dXh = dXh + wt * Z
                dB[g] = dB[g] + _dot(jnp.where(lm, X[:, sl] * wt, 0.0).astype(BF), dHnb)
                dAtot = jnp.sum(xi, axis=0, keepdims=True) + dec * jnp.sum(
                    jnp.sum(dHn * hp_t, axis=-1, keepdims=True), axis=0, keepdims=True)
                dAc = dAc - xi + jnp.where(rows == CHUNK - 1, dAtot, 0.0)
                dA = dA + jnp.where(fl, dAc, 0.0)
                dX = dX + dXh
                ddtx = ddtx + jnp.where(fl, jnp.sum(dXh * xs[:, sl], axis=-1, keepdims=True), 0.0)
            dh[sl, :] = dh_new
            dX_t.append(dX)
            dA_t.append(dA)
            ddtx_t.append(ddtx)
        for g in range(2):
            dGb = dG[g].astype(BF)
            dC[g] = dC[g] + _dot(dGb, Bb[g])
            dB[g] = dB[g] + _dot_tn(dGb, Cb[g])
        dXf = jnp.concatenate(dX_t, axis=1)
        da = _cumsum_rows(jnp.concatenate(dA_t, axis=1), reverse=True)
        ddt = da * (-jnp.exp(al_ref[...])) + jnp.concatenate(ddtx_t, axis=1)
        du = ddt * _sigmoid(p["u"])
        dxs = dXf * p["dt"] + dskip_ * dy
        dxc = jnp.concatenate([dxs, dB[0], dB[1], dC[0], dC[1]], axis=1) * _dsilu(p["xc"])
        ext2[0:CHUNK, :] = dxc
        dxbc = jnp.zeros((CHUNK, SSD_CONV_DIM), F32)
        for j in range(4):
            dxbc = dxbc + cw_ref[j:j + 1, :] * ext2[pl.ds(3 - j, CHUNK), :]
            dcw_ref[j:j + 1, :] += jnp.sum(dxc * ext[pl.ds(5 + j, CHUNK), :], axis=0, keepdims=True)
        ext2[CHUNK:CHUNK + 8, :] = dxc[0:8, :]
        dcb_ref[...] += jnp.sum(dxc, axis=0, keepdims=True)
        dvec_ref[0:1, :] += jnp.sum(du, axis=0, keepdims=True)
        dvec_ref[1:2, :] += jnp.sum(da * p["a"], axis=0, keepdims=True)
        dvec_ref[2:3, :] += jnp.sum(dy * xs, axis=0, keepdims=True)
        dvec_ref[3:4, :] += jnp.sum(dy3_ * y2 * r, axis=0, keepdims=True)
        dx_ref[...] = jnp.concatenate([dxbc, dz, du], axis=1).astype(BF)

    return _call(body, name="ssd_bwd", grid=(B, N_CHUNK), in_specs=[row, halo, hp, y] + params,
                 out_specs=[row, const(4, SSD_CONV_DIM), const(1, SSD_CONV_DIM), const(8, SSD_W)],
                 out_shape=[jax.ShapeDtypeStruct((T, W_SSD), BF), jax.ShapeDtypeStruct((4, SSD_CONV_DIM), F32),
                            jax.ShapeDtypeStruct((1, SSD_CONV_DIM), F32), jax.ShapeDtypeStruct((8, SSD_W), F32)],
                 scratch=[pltpu.VMEM((8 + CHUNK, SSD_CONV_DIM), F32), pltpu.VMEM((8 + CHUNK, SSD_CONV_DIM), F32),
                          pltpu.VMEM((SSD_W, SSD_STATE), F32)],
                 sem=("arbitrary", "arbitrary"))(sin, sin, hprev, dy3, conv_w, conv_b, dtb, alog, dskip, norm_g)


def _sgu_core(uv_ref, g_ref, b_ref, w_ref, bias_ref):
    x = uv_ref[...]
    cdf = 0.5 * (1.0 + lax.erf(x * (2.0 ** -0.5)))
    ge = x * cdf
    dge = cdf + x * jnp.exp(-0.5 * x * x) * ((2.0 * math.pi) ** -0.5)
    u, v = ge[:, 0:SGU_W], ge[:, SGU_W:]
    vc = v - jnp.mean(v, axis=-1, keepdims=True)
    rstd = lax.rsqrt(jnp.mean(vc * vc, axis=-1, keepdims=True) + LN_EPS)
    vhat = vc * rstd
    vn = vhat * g_ref[...] + b_ref[...]
    tri = _tri()
    wc = [jnp.where(tri, w_ref[gi], 0.0).astype(BF) for gi in range(4)]
    vm = [jnp.where(_lane_mask(gi % 2), vn[:, 128 * (gi // 2):128 * (gi // 2 + 1)], 0.0).astype(BF) for gi in range(4)]
    mixed = jnp.concatenate([_dot(wc[2 * t], vm[2 * t]) + _dot(wc[2 * t + 1], vm[2 * t + 1]) for t in range(2)],
                            axis=1) + bias_ref[...]
    return dict(dge=dge, u=u, rstd=rstd, vhat=vhat, wc=wc, vm=vm, mixed=mixed)


def _sgu_specs():
    vec = pl.BlockSpec((1, SGU_W), lambda i: (0, 0))
    return [pl.BlockSpec((CHUNK, W_UV), lambda i: (i, 0)), vec, vec,
            pl.BlockSpec((4, CHUNK, CHUNK), lambda i: (0, 0, 0)), pl.BlockSpec((CHUNK, SGU_W), lambda i: (0, 0))]


def _sgu_fwd(uv, ln_g, ln_b, w, bias):
    T = uv.shape[0]

    def body(uv_ref, g_ref, b_ref, w_ref, bias_ref, y_ref):
        s = _sgu_core(uv_ref, g_ref, b_ref, w_ref, bias_ref)
        y_ref[...] = s["u"] * s["mixed"]

    return _call(body, name="sgu_fwd", grid=(T // CHUNK,), in_specs=_sgu_specs(),
                 out_specs=pl.BlockSpec((CHUNK, SGU_W), lambda i: (i, 0)),
                 out_shape=jax.ShapeDtypeStruct((T, SGU_W), F32), sem=("parallel",))(uv, ln_g, ln_b, w, bias)


def _sgu_bwd(uv, dy, ln_g, ln_b, w, bias):
    T = uv.shape[0]

    def body(uv_ref, dy_ref, g_ref, b_ref, w_ref, bias_ref, dx_ref, dw_ref, dbias_ref, dln_ref):
        @pl.when(pl.program_id(0) == 0)
        def _():
            dw_ref[...] = jnp.zeros_like(dw_ref)
            dbias_ref[...] = jnp.zeros_like(dbias_ref)
            dln_ref[...] = jnp.zeros_like(dln_ref)

        s = _sgu_core(uv_ref, g_ref, b_ref, w_ref, bias_ref)
        dy_ = dy_ref[...]
        du = dy_ * s["mixed"]
        dmix = dy_ * s["u"]
        dbias_ref[...] += dmix
        tri = _tri()
        dvn_t = []
        for t in range(2):
            acc = jnp.zeros((CHUNK, 128), F32)
            for e in range(2):
                gi = 2 * t + e
                dmg = jnp.where(_lane_mask(e), dmix[:, 128 * t:128 * (t + 1)], 0.0).astype(BF)
                acc = acc + _dot_tn(s["wc"][gi], dmg)
                dw_ref[gi] += jnp.where(tri, _dot_nt(dmg, s["vm"][gi]), 0.0)
            dvn_t.append(acc)
        dvn = jnp.concatenate(dvn_t, axis=1)
        dln_ref[0:1, :] += jnp.sum(dvn * s["vhat"], axis=0, keepdims=True)
        dln_ref[1:2, :] += jnp.sum(dvn, axis=0, keepdims=True)
        dvh = dvn * g_ref[...]
        dv = s["rstd"] * (dvh - jnp.mean(dvh, axis=-1, keepdims=True)
                          - s["vhat"] * jnp.mean(dvh * s["vhat"], axis=-1, keepdims=True))
        dx_ref[...] = (jnp.concatenate([du, dv], axis=1) * s["dge"]).astype(BF)

    ins = _sgu_specs()
    return _call(body, name="sgu_bwd", grid=(T // CHUNK,),
                 in_specs=[ins[0], pl.BlockSpec((CHUNK, SGU_W), lambda i: (i, 0))] + ins[1:],
                 out_specs=[pl.BlockSpec((CHUNK, W_UV), lambda i: (i, 0)),
                            pl.BlockSpec((4, CHUNK, CHUNK), lambda i: (0, 0, 0)),
                            pl.BlockSpec((CHUNK, SGU_W), lambda i: (0, 0)), pl.BlockSpec((8, SGU_W), lambda i: (0, 0))],
                 out_shape=[jax.ShapeDtypeStruct((T, W_UV), BF), jax.ShapeDtypeStruct((4, CHUNK, CHUNK), F32),
                            jax.ShapeDtypeStruct((CHUNK, SGU_W), F32), jax.ShapeDtypeStruct((8, SGU_W), F32)],
                 sem=("arbitrary",))(uv, dy, ln_g, ln_b, w, bias)


def _adamw(w, g, m, v):
    R, C = w.shape
    tr = _tile(R, 256) if R % 8 == 0 else R

    def body(w_ref, g_ref, m_ref, v_ref, d_ref, nm_ref, nv_ref):
        g_ = g_ref[...]
        m2 = ADAM_B1 * m_ref[...] + (1.0 - ADAM_B1) * g_
        v2 = ADAM_B2 * v_ref[...] + (1.0 - ADAM_B2) * (g_ * g_)
        m_hat = m2 / (1.0 - ADAM_B1 ** ADAM_STEP)
        v_hat = v2 / (1.0 - ADAM_B2 ** ADAM_STEP)
        d_ref[...] = -ADAM_LR * (m_hat / (jnp.sqrt(v_hat) + ADAM_EPS) + ADAM_WD * w_ref[...])
        nm_ref[...] = m2
        nv_ref[...] = v2

    blk = pl.BlockSpec((tr, C), lambda i: (i, 0))
    sh = jax.ShapeDtypeStruct((R, C), F32)
    return _call(body, name="adamw", grid=(R // tr,), in_specs=[blk] * 4, out_specs=[blk] * 3,
                 out_shape=[sh] * 3, sem=("parallel",))(w, g, m, v)


def _adamw_pair(w, g0, g1, m, v, dep):
    L, R, C = w.shape
    tr = _tile(R, 256 if C <= 1024 else 64)

    def body(w_ref, g0_ref, g1_ref, m_ref, v_ref, dep_ref, d_ref, nm_ref, nv_ref, og_ref):
        g_ = jnp.where(pl.program_id(0) == 0, g0_ref[...], g1_ref[...])
        m2 = ADAM_B1 * m_ref[...] + (1.0 - ADAM_B1) * g_
        v2 = ADAM_B2 * v_ref[...] + (1.0 - ADAM_B2) * (g_ * g_)
        m_hat = m2 / (1.0 - ADAM_B1 ** ADAM_STEP)
        v_hat = v2 / (1.0 - ADAM_B2 ** ADAM_STEP)
        d_ref[...] = -ADAM_LR * (m_hat / (jnp.sqrt(v_hat) + ADAM_EPS) + ADAM_WD * w_ref[...])
        nm_ref[...] = m2
        nv_ref[...] = v2
        og_ref[...] = g_

    lay = pl.BlockSpec((None, tr, C), lambda l, i: (l, i, 0))
    one = pl.BlockSpec((tr, C), lambda l, i: (i, 0))
    return _call(body, name="adamw_pair", grid=(L, R // tr),
                 in_specs=[lay, one, one, lay, lay, pl.BlockSpec((8, 128), lambda l, i: (0, 0))], out_specs=[lay] * 4,
                 out_shape=[jax.ShapeDtypeStruct((L, R, C), F32)] * 4,
                 sem=("parallel", "parallel"))(w, g0, g1, m, v, dep)


def _row_steps(rows):
    return 2 if rows % 32 == 0 else 1


def _pair_add(gbuf, rsib, c):
    NS, _, R, C = gbuf.shape
    n = _row_steps(R)
    tr = R // n

    def body(c_ref, a_ref, b_ref, o_ref):
        o_ref[...] = (a_ref[...] + b_ref[...]).astype(BF)

    blk = pl.BlockSpec((None, tr, C), lambda j, i, c_ref: (j, i, 0))
    return pl.pallas_call(
        body, name="rs_pair_add",
        grid_spec=pltpu.PrefetchScalarGridSpec(
            num_scalar_prefetch=1, grid=(NS, n),
            in_specs=[pl.BlockSpec((None, None, tr, C), lambda j, i, c_ref: (j, c_ref[0], i, 0)), blk],
            out_specs=blk),
        out_shape=jax.ShapeDtypeStruct((NS, R, C), BF),
        compiler_params=pltpu.CompilerParams(dimension_semantics=("parallel", "parallel")),
    )(jnp.reshape(c, (1,)).astype(jnp.int32), gbuf, rsib)


def _chip_sum(pair, recv, me, c):
    NS, R, C = pair.shape
    n = _row_steps(R)
    tr = R // n

    def body(s_ref, own_ref, p_ref, o_ref):
        p = [jnp.where(s_ref[0] == j, own_ref[...], p_ref[j]).astype(F32) for j in range(4)]
        o_ref[...] = ((p[0] + p[1]) + p[2]) + p[3]

    return pl.pallas_call(
        body, name="rs_chip_sum",
        grid_spec=pltpu.PrefetchScalarGridSpec(
            num_scalar_prefetch=1, grid=(n,),
            in_specs=[pl.BlockSpec((None, tr, C), lambda i, s: (s[0], i, 0)),
                      pl.BlockSpec((NS, tr, C), lambda i, s: (0, i, 0))],
            out_specs=pl.BlockSpec((None, tr, C), lambda i, s: (s[1], i, 0))),
        out_shape=jax.ShapeDtypeStruct((2, R, C), F32),
        compiler_params=pltpu.CompilerParams(dimension_semantics=("parallel",)),
    )(jnp.stack([me, c]).astype(jnp.int32), pair, recv)


MESH = pl.DeviceIdType.MESH
ANY = pl.BlockSpec(memory_space=pl.ANY)


def _place():
    x, y, c = lax.axis_index("x"), lax.axis_index("y"), lax.axis_index("c")
    return x, y, c, [(1 - x, y), (x, 1 - y), (1 - x, 1 - y)]


HBM = pl.BlockSpec(memory_space=pltpu.HBM)
SEM = pl.BlockSpec(memory_space=pltpu.SEMAPHORE)
EFFECT = pltpu.SideEffectType.DATAFLOW_SIDE_EFFECTING


class _Split:
    def __init__(self, tag, arrays, copies, n_copies):
        self.tag, self.copies, k = tag, copies, len(arrays)

        def body(*refs):
            for cp in copies(refs[:k], refs[k], refs[k + 1]):
                cp.start()
            refs[-1][...] = jnp.zeros_like(refs[-1])

        out = pl.pallas_call(
            body, name=tag + "_start",
            out_shape=(pltpu.SemaphoreType.DMA((n_copies,)), pltpu.SemaphoreType.DMA((n_copies,)),
                       *[pltpu.HBM(a.shape, a.dtype) for a in arrays], jax.ShapeDtypeStruct((8, 128), F32)),
            in_specs=[HBM] * k, out_specs=(SEM, SEM, *[HBM] * k, pl.BlockSpec(memory_space=pltpu.VMEM)),
            input_output_aliases={i: 2 + i for i in range(k)},
            compiler_params=pltpu.CompilerParams(has_side_effects=EFFECT),
        )(*[pltpu.with_memory_space_constraint(a, pltpu.HBM) for a in arrays])
        self.send, self.recv, self.arrays, self.token = out[0], out[1], list(out[2:2 + k]), out[-1][0, 0]

    def wait(self, after):
        k, copies = len(self.arrays), self.copies

        def body(*refs):
            for cp in copies(refs[:k], refs[k], refs[k + 1]):
                cp.wait_send()
                cp.wait_recv()

        return list(pl.pallas_call(
            body, name=self.tag + "_wait", out_shape=tuple(pltpu.HBM(a.shape, a.dtype) for a in self.arrays),
            in_specs=[HBM] * k + [SEM, SEM, ANY], out_specs=tuple([HBM] * k),
            input_output_aliases={i: i for i in range(k)},
            compiler_params=pltpu.CompilerParams(has_side_effects=EFFECT),
        )(*self.arrays, self.send, self.recv, after))


def _gather_start(arrs, tag):
    n = len(arrs)
    me = 2 * lax.axis_index("x") + lax.axis_index("y")
    lands = [lax.dynamic_update_index_in_dim(lax.empty((4,) + a.shape, a.dtype), a, me, 0) for a in arrs]

    def copies(refs, send, recv):
        x, y, c, chips = _place()
        return [pltpu.make_async_remote_copy(
            src_ref=refs[k], dst_ref=refs[n + k].at[2 * x + y], send_sem=send.at[3 * k + r],
            recv_sem=recv.at[3 * k + r], device_id=(px, py, c), device_id_type=MESH)
            for k in range(n) for r, (px, py) in enumerate(chips)]

    return _Split("gather_" + tag, list(arrs) + lands, copies, 3 * n)


def _to_sibling_start(gbufs, tag):
    n = len(gbufs)

    def copies(refs, send, recv):
        x, y, c, _ = _place()
        return [pltpu.make_async_remote_copy(
            src_ref=refs[k].at[j, 1 - c], dst_ref=refs[n + k].at[j], send_sem=send.at[4 * k + j],
            recv_sem=recv.at[4 * k + j], device_id=(x, y, 1 - c), device_id_type=MESH)
            for k in range(n) for j in range(4)]

    lands = [lax.empty((4,) + g.shape[2:], g.dtype) for g in gbufs]
    return _Split("rs_sibling_" + tag, list(gbufs) + lands, copies, 4 * n)


def _to_chips_start(pbufs, tag):
    n = len(pbufs)

    def copies(refs, send, recv):
        x, y, c, chips = _place()
        return [pltpu.make_async_remote_copy(
            src_ref=refs[k].at[2 * px + py], dst_ref=refs[n + k].at[2 * x + y], send_sem=send.at[3 * k + r],
            recv_sem=recv.at[3 * k + r], device_id=(px, py, c), device_id_type=MESH)
            for k in range(n) for r, (px, py) in enumerate(chips)]

    return _Split("rs_chips_" + tag, list(pbufs) + [lax.empty(p.shape, p.dtype) for p in pbufs], copies, 3 * n)


def _join_start(fulls, tag):
    def copies(refs, send, recv):
        x, y, c, _ = _place()
        return [pltpu.make_async_remote_copy(
            src_ref=refs[k].at[c], dst_ref=refs[k].at[c], send_sem=send.at[k], recv_sem=recv.at[k],
            device_id=(x, y, 1 - c), device_id_type=MESH) for k in range(len(fulls))]

    return _Split("rs_join_" + tag, list(fulls), copies, len(fulls))


def _all_reduce_small(v):
    R, C = v.shape

    def body(v_ref, o_ref, g_ref, send, recv, loc):
        x, y, c, chips = _place()
        me, sibling = (x, y, c), (x, y, 1 - c)

        def rows(px, py, pc):
            return g_ref.at[4 * px + 2 * py + pc]

        def copy(k, block, to, src=None):
            return pltpu.make_async_remote_copy(
                src_ref=rows(*block) if src is None else src, dst_ref=rows(*block),
                send_sem=send.at[k], recv_sem=recv.at[k], device_id=to, device_id_type=MESH)

        mine = pltpu.make_async_copy(v_ref, rows(*me), loc)
        mine.start()
        first = [copy(0, me, sibling, src=v_ref)]
        first += [copy(1 + j, me, (*chip, c), src=v_ref) for j, chip in enumerate(chips)]
        for cp in first:
            cp.start()
        passed = [copy(4 + j, (*chip, c), sibling) for j, chip in enumerate(chips)]
        for j, chip in enumerate(chips):
            copy(1 + j, (*chip, c), me).wait_recv()
            passed[j].start()
        copy(0, sibling, me).wait_recv()
        for j, chip in enumerate(chips):
            copy(4 + j, (*chip, 1 - c), me).wait_recv()
        for cp in first + passed:
            cp.wait_send()
        mine.wait()
        acc = g_ref[0]
        for d in range(1, 8):
            acc = acc + g_ref[d]
        o_ref[...] = acc

    vm = pl.BlockSpec(memory_space=pltpu.VMEM)
    return pl.pallas_call(
        body, name="all_reduce_small", in_specs=[vm], out_specs=[vm, vm],
        out_shape=[jax.ShapeDtypeStruct((R, C), F32), jax.ShapeDtypeStruct((8, R, C), F32)],
        scratch_shapes=[pltpu.SemaphoreType.DMA((7,)), pltpu.SemaphoreType.DMA((7,)), pltpu.SemaphoreType.DMA],
    )(v)[0]


WEIGHTS = ['ffn1_norm', 'ffn1_w_gate', 'ffn1_w_up', 'ffn1_w_down', 'mix_norm', 'w_in', 'conv_w', 'conv_b', 'dt_bias',
           'a_log', 'd_skip', 'ssd_norm', 'sgu_ln_g', 'sgu_ln_b', 'sgu_w', 'sgu_b', 'w_out', 'ffn2_norm',
           'ffn2_w_gate', 'ffn2_w_up', 'ffn2_w_down', 'final_norm']
SHARDED = ['ffn1_w_gate', 'ffn1_w_up', 'ffn1_w_down', 'w_in', 'conv_w', 'w_out', 'ffn2_w_gate', 'ffn2_w_up',
           'ffn2_w_down']
SMALL = [n for n in WEIGHTS if n not in SHARDED]
GROUPS = [("ffn1", ["ffn1_w_gate", "ffn1_w_up", "ffn1_w_down"]), ("mix", ["w_in", "conv_w", "w_out"]),
          ("ffn2", ["ffn2_w_gate", "ffn2_w_up", "ffn2_w_down"])]
TRANSPOSED = ("ffn1_w_gate", "ffn1_w_up", "ffn2_w_gate", "ffn2_w_up")
DEPTH = 2


def _pack_w_in(w):
    return jnp.concatenate([w[..., 0:1152], w[..., 1536:2432], w[..., 1152:1536],
                            jnp.repeat(w[..., 2432:2438], HEAD, axis=-1), w[..., 2438:2950]], axis=-1)


def _unpack_w_in(dq, ds, du):
    return jnp.concatenate([dq, ds[:, 896:1280], ds[:, 0:896], ds[:, 1280::HEAD], du], axis=-1)


def _ffn_fwd(x, g, wg, wu, wd):
    xo, hb, S1, S2, A = _ffn_fwd_k(x, g, wg, wu, wd)
    return xo, (x, hb, S1, S2, A)


def _ffn_bwd(dxo, saved, g, wg, wu, wd):
    x, hb, S1, S2, A = saved
    dx, dg, dG, dU, dyb = _ffn_bwd_k1(dxo, x, g, S1, S2, wg, wu, wd)
    dwg, dwu, dwd = _ffn_bwd_k2(hb, dyb, A, dG, dU)
    return dx, dg, dwg, dwu, dwd


def _mix_fwd(x, P):
    hb = _rms_fwd(x, P["mix_norm"])
    qkv = _mm_nn(hb, P["w_qkv"])
    sin = _mm_nn(hb, P["w_ssd"])
    uv = _mm_nn(hb, P["w_uv"])
    y_att, lse = _attn_combine([_attn_fwd(qkv, d) for d in DILATIONS])
    y_ssd, hprev = _ssd_fwd(sin, *P["ssd"])
    y_sgu = _sgu_fwd(uv, *P["sgu"])
    ycat = jnp.concatenate([y_att, y_ssd, y_sgu], axis=1).astype(BF)
    return _mm_nn(ycat, P["w_out"], res=x), (x, hb, qkv, sin, uv, y_att, lse, hprev, ycat)


def _mix_bwd(dxo, saved, P):
    x, hb, qkv, sin, uv, y_att, lse, hprev, ycat = saved
    dycat = _mm_nt(dxo, P["w_out"])
    dwout = _mm_tn(ycat, dxo)
    dy_att, dy_ssd, dy_sgu = dycat[:, 0:ATT_W], dycat[:, ATT_W:ATT_W + SSD_W], dycat[:, ATT_W + SSD_W:]
    dqkv = _sum_branches([_attn_bwd(qkv, dy_att, y_att, lse, d) for d in DILATIONS])
    dsin, dcw, dcb, dvec = _ssd_bwd(sin, hprev, dy_ssd, *P["ssd"])
    duv, dsw, dsbias, dln = _sgu_bwd(uv, dy_sgu, *P["sgu"])
    dwin = _unpack_w_in(_mm_tn(hb, dqkv), _mm_tn(hb, dsin), _mm_tn(hb, duv))
    dh = _mm_nt(dqkv, P["w_qkv"])
    dh = _mm_nt(dsin, P["w_ssd"], res=dh)
    dh = _mm_nt(duv, P["w_uv"], res=dh)
    dx, dg = _rms_bwd(x, P["mix_norm"], dh, dxo)
    grads = dict(
        mix_norm=dg[0], w_in=dwin, conv_w=dcw, conv_b=dcb[0], dt_bias=dvec[0, ::HEAD], a_log=dvec[1, ::HEAD],
        d_skip=jnp.sum(dvec[2].reshape(6, HEAD), axis=-1), ssd_norm=dvec[3], sgu_ln_g=dln[0], sgu_ln_b=dln[1],
        sgu_w=dsw, sgu_b=jnp.sum(dsbias.reshape(CHUNK, 4, HEAD), axis=-1).T, w_out=dwout)
    return dx, grads


def _halved(g):
    rows = g.size // g.shape[-1]
    return g.reshape(4, 2, rows // 8, g.shape[-1])


def kernel(x, ffn1_norm, ffn1_w_gate, ffn1_w_up, ffn1_w_down, mix_norm, w_in, conv_w, conv_b, dt_bias, a_log, d_skip, ssd_norm, sgu_ln_g, sgu_ln_b, sgu_w, sgu_b, w_out, ffn2_norm, ffn2_w_gate, ffn2_w_up, ffn2_w_down, final_norm, loss_target, m_ffn1_norm, m_ffn1_w_gate, m_ffn1_w_up, m_ffn1_w_down, m_mix_norm, m_w_in, m_conv_w, m_conv_b, m_dt_bias, m_a_log, m_d_skip, m_ssd_norm, m_sgu_ln_g, m_sgu_ln_b, m_sgu_w, m_sgu_b, m_w_out, m_ffn2_norm, m_ffn2_w_gate, m_ffn2_w_up, m_ffn2_w_down, m_final_norm, v_ffn1_norm, v_ffn1_w_gate, v_ffn1_w_up, v_ffn1_w_down, v_mix_norm, v_w_in, v_conv_w, v_conv_b, v_dt_bias, v_a_log, v_d_skip, v_ssd_norm, v_sgu_ln_g, v_sgu_ln_b, v_sgu_w, v_sgu_b, v_w_out, v_ffn2_norm, v_ffn2_w_gate, v_ffn2_w_up, v_ffn2_w_down, v_final_norm):
    given = dict(x=x, ffn1_norm=ffn1_norm, ffn1_w_gate=ffn1_w_gate, ffn1_w_up=ffn1_w_up, ffn1_w_down=ffn1_w_down, mix_norm=mix_norm, w_in=w_in, conv_w=conv_w, conv_b=conv_b, dt_bias=dt_bias, a_log=a_log, d_skip=d_skip, ssd_norm=ssd_norm, sgu_ln_g=sgu_ln_g, sgu_ln_b=sgu_ln_b, sgu_w=sgu_w, sgu_b=sgu_b, w_out=w_out, ffn2_norm=ffn2_norm, ffn2_w_gate=ffn2_w_gate, ffn2_w_up=ffn2_w_up, ffn2_w_down=ffn2_w_down, final_norm=final_norm, loss_target=loss_target, m_ffn1_norm=m_ffn1_norm, m_ffn1_w_gate=m_ffn1_w_gate, m_ffn1_w_up=m_ffn1_w_up, m_ffn1_w_down=m_ffn1_w_down, m_mix_norm=m_mix_norm, m_w_in=m_w_in, m_conv_w=m_conv_w, m_conv_b=m_conv_b, m_dt_bias=m_dt_bias, m_a_log=m_a_log, m_d_skip=m_d_skip, m_ssd_norm=m_ssd_norm, m_sgu_ln_g=m_sgu_ln_g, m_sgu_ln_b=m_sgu_ln_b, m_sgu_w=m_sgu_w, m_sgu_b=m_sgu_b, m_w_out=m_w_out, m_ffn2_norm=m_ffn2_norm, m_ffn2_w_gate=m_ffn2_w_gate, m_ffn2_w_up=m_ffn2_w_up, m_ffn2_w_down=m_ffn2_w_down, m_final_norm=m_final_norm, v_ffn1_norm=v_ffn1_norm, v_ffn1_w_gate=v_ffn1_w_gate, v_ffn1_w_up=v_ffn1_w_up, v_ffn1_w_down=v_ffn1_w_down, v_mix_norm=v_mix_norm, v_w_in=v_w_in, v_conv_w=v_conv_w, v_conv_b=v_conv_b, v_dt_bias=v_dt_bias, v_a_log=v_a_log, v_d_skip=v_d_skip, v_ssd_norm=v_ssd_norm, v_sgu_ln_g=v_sgu_ln_g, v_sgu_ln_b=v_sgu_ln_b, v_sgu_w=v_sgu_w, v_sgu_b=v_sgu_b, v_w_out=v_w_out, v_ffn2_norm=v_ffn2_norm, v_ffn2_w_gate=v_ffn2_w_gate, v_ffn2_w_up=v_ffn2_w_up, v_ffn2_w_down=v_ffn2_w_down, v_final_norm=v_final_norm)
    T = given["x"].shape[0] * given["x"].shape[1]
    D = given["x"].shape[2]
    x0 = given["x"].reshape(T, D)
    tgt = given["loss_target"].reshape(T, D)
    c = lax.axis_index("c")

    bf = {n: given[n].astype(BF) for n in SHARDED if n not in ("w_in", "conv_w")}
    bf["w_in"] = _pack_w_in(given["w_in"]).astype(BF)
    bf["conv_w"] = given["conv_w"]
    gathers = {(i, gname): _gather_start([bf[n][i] for n in names], f"l{i}_{gname}")
               for i in range(DEPTH) for gname, names in GROUPS}
    token = functools.reduce(lambda a, b: a + b, [g.token for g in gathers.values()])

    def gathered(i, gname, after):
        return gathers[(i, gname)].wait(after)[3:]

    def mix_params(i, got):
        win = got[0].reshape(D, W_QKV + W_SSD + W_UV)
        rep = lambda v: jnp.repeat(v, HEAD)[None]
        ssd = (got[1].transpose(1, 0, 2).reshape(4, SSD_CONV_DIM), given["conv_b"][i][None],
               rep(given["dt_bias"][i]), rep(given["a_log"][i]), rep(given["d_skip"][i]), given["ssd_norm"][i][None])
        sgu = (given["sgu_ln_g"][i][None], given["sgu_ln_b"][i][None], given["sgu_w"][i],
               jnp.repeat(given["sgu_b"][i].T, HEAD, axis=1))
        return dict(mix_norm=given["mix_norm"][i][None], w_qkv=win[:, 0:W_QKV], w_ssd=win[:, W_QKV:W_QKV + W_SSD],
                    w_uv=win[:, W_QKV + W_SSD:], w_out=got[2].reshape(-1, D), ssd=ssd, sgu=sgu)

    x = x0
    tape = []
    for i in range(DEPTH):
        P = dict(ffn1=(given["ffn1_norm"][i][None] + (token if i == 0 else 0.0), *gathered(i, "ffn1", x)))
        x, s1 = _ffn_fwd(x, *P["ffn1"])
        P.update(mix_params(i, gathered(i, "mix", x)))
        x, s2 = _mix_fwd(x, P)
        P["ffn2"] = (given["ffn2_norm"][i][None], *gathered(i, "ffn2", x))
        x, s3 = _ffn_fwd(x, *P["ffn2"])
        tape.append((P, s1, s2, s3))
    loss_part, dx, dgf = _final_loss(x, given["final_norm"][None], tgt)

    me = 2 * lax.axis_index("x") + lax.axis_index("y")
    jobs = []

    def rs_begin(i, gname, gd):
        tag = f"l{i}_{gname}"
        names = [n for n in dict(GROUPS)[gname] if n != "conv_w"]
        jobs.append(dict(key=(i, gname), names=names, tag=tag, stage=1,
                         op=_to_sibling_start([_halved(gd[n]) for n in names], tag)))

    def rs_advance(job, after):
        k = len(job["names"])
        if job["stage"] == 1:
            got = job["op"].wait(after)
            job.update(stage=2, op=_to_chips_start([_pair_add(g, l, c) for g, l in zip(got[:k], got[k:])], job["tag"]))
        elif job["stage"] == 2:
            got = job["op"].wait(after)
            job.update(stage=3, op=_join_start([_chip_sum(p, l, me, c) for p, l in zip(got[:k], got[k:])], job["tag"]))
        elif job["stage"] == 3:
            job.update(stage=4, out=dict(zip(job["names"], job["op"].wait(after))))

    def tick(after, begin=None):
        for job in jobs:
            rs_advance(job, after)
        if begin is not None:
            rs_begin(*begin)
        return functools.reduce(lambda a, b: a + b, [j["op"].token for j in jobs if j["stage"] < 4], 0.0)

    grads = [dict() for _ in range(DEPTH)]
    tok = 0.0
    for i in reversed(range(DEPTH)):
        P, s1, s2, s3 = tape[i]
        g = grads[i]
        norm, wg, wu, wd = P["ffn2"]
        dx, dn2, g["ffn2_w_gate"], g["ffn2_w_up"], g["ffn2_w_down"] = _ffn_bwd(dx, s3, norm + tok, wg, wu, wd)
        tok = tick(dx, (i, "ffn2", g))
        dx, gm = _mix_bwd(dx, s2, {**P, "mix_norm": P["mix_norm"] + tok})
        g.update(gm)
        tok = tick(dx, (i, "mix", g))
        norm, wg, wu, wd = P["ffn1"]
        dx, dn1, g["ffn1_w_gate"], g["ffn1_w_up"], g["ffn1_w_down"] = _ffn_bwd(dx, s1, norm + tok, wg, wu, wd)
        tok = tick(dx, (i, "ffn1", g))
        g["ffn1_norm"], g["ffn2_norm"] = dn1[0], dn2[0]
    grad_x = dx.reshape(given["x"].shape)

    order = [n for n in SMALL if n != "final_norm"] + ["final_norm"]
    small = [jnp.stack([grads[i][n] for i in range(DEPTH)]) for n in order[:-1] + ["conv_w"]]
    small = small[:-1] + [dgf[0], small[-1], loss_part[0, 0:1]]
    n_small = sum(s.size for s in small)
    rows_small = -(-n_small // (128 * 8)) * 8

    def flat(arrs):
        fill = rows_small * 128 - sum(a.size for a in arrs)
        return jnp.concatenate([a.reshape(-1) for a in arrs] + [jnp.zeros((fill,), F32)]).reshape(rows_small, 128)

    gsmall = _all_reduce_small(flat(small)).reshape(-1)

    grad_w = {}
    off = 0
    for n in order:
        size = given[n].size
        grad_w[n] = gsmall[off:off + size].reshape(given[n].shape)
        off += size
    cw = gsmall[off:off + 2 * 4 * SSD_CONV_DIM].reshape(DEPTH, 4, SSD_CONV_DIM)
    grad_w["conv_w"] = lax.dynamic_slice_in_dim(cw, me * (SSD_CONV_DIM // 4), SSD_CONV_DIM // 4, axis=2)
    loss = gsmall[off + 2 * 4 * SSD_CONV_DIM]

    delta, new_m, new_v = {}, {}, {}
    shp = given["conv_w"].shape
    d, m2, v2 = _adamw(*[a.reshape(shp[0] * shp[1], shp[2])
                         for a in (given["conv_w"], grad_w["conv_w"], given["m_conv_w"], given["v_conv_w"])])
    delta["conv_w"], new_m["conv_w"], new_v["conv_w"] = d.reshape(shp), m2.reshape(shp), v2.reshape(shp)
    packed = [flat([given[pre + n] for n in order]) for pre in ("", "m_", "v_")]
    small_out = _adamw(packed[0], gsmall.reshape(rows_small, 128), packed[1], packed[2])
    outs = [o.reshape(-1) for o in small_out]
    off = 0
    for n in order:
        size = given[n].size
        for dst, o in zip((delta, new_m, new_v), outs):
            dst[n] = o[off:off + size].reshape(given[n].shape)
        off += size

    stepped, arrived = {}, {}

    def update_arrived(dep):
        out = None
        for job in jobs:
            if job["stage"] == 4 and not job.get("seen"):
                job["seen"] = True
                for n, full in job["out"].items():
                    view = (lambda a: jnp.swapaxes(a, 1, 2)) if n in TRANSPOSED else (lambda a: a)
                    arrived.setdefault(n, {})[job["key"][0]] = full.reshape(view(given[n]).shape[1:])
                    if len(arrived[n]) == DEPTH:
                        res = _adamw_pair(view(given[n]), arrived[n][0], arrived[n][1], view(given["m_" + n]),
                                          view(given["v_" + n]), dep)
                        stepped[n] = [view(r) for r in res]
                        out = res[0]
        return out

    after = small_out[0]
    while any(j["stage"] < 4 for j in jobs):
        done = update_arrived(jnp.zeros((8, 128), F32) + tok)
        after = after if done is None else done
        tok = tick(after)
    update_arrived(jnp.zeros((8, 128), F32) + tok)
    for n, (d, m2, v2, g) in stepped.items():
        delta[n], new_m[n], new_v[n], grad_w[n] = d, m2, v2, g

    return (loss, grad_x, *[grad_w[n] for n in WEIGHTS], *[delta[n] for n in WEIGHTS],
            *[new_m[n] for n in WEIGHTS], *[new_v[n] for n in WEIGHTS])
```

```python
import functools
import math

import jax
import jax.numpy as jnp
from jax import lax
from jax.experimental import pallas as pl
from jax.experimental.pallas import tpu as pltpu

F32 = jnp.float32
BF = jnp.bfloat16

RMS_EPS = 1e-6
LN_EPS = 1e-5
SEQ = 2048
CHUNK = 128
N_CHUNK = SEQ // CHUNK
ATT_W = 384
HEAD = 64
SSD_W = 384
SSD_CONV_DIM = 896
SSD_STATE = 128
SGU_W = 256
DILATIONS = (1, 4, 16)
W_QKV = 3 * ATT_W
W_SSD = SSD_CONV_DIM + SSD_W + SSD_W
W_UV = 2 * SGU_W
ADAM_LR = 0.001
ADAM_B1 = 0.9
ADAM_B2 = 0.999
ADAM_EPS = 1e-08
ADAM_WD = 0.01
ADAM_STEP = 10
NEG = -1e30
ATTN_BWD_VMEM = 48 * 2 ** 20


def _dot(a, b):
    return jnp.dot(a, b, preferred_element_type=F32)


def _dot_nt(a, b):
    return lax.dot_general(a, b, (((1,), (1,)), ((), ())), preferred_element_type=F32)


def _dot_tn(a, b):
    return lax.dot_general(a, b, (((0,), (0,)), ((), ())), preferred_element_type=F32)


def _sigmoid(x):
    return 1.0 / (1.0 + jnp.exp(-x))


def _call(body, *, name, grid, in_specs, out_specs, out_shape, scratch=(), sem=None, vmem=None):
    return pl.pallas_call(
        body, name=name, grid=grid, in_specs=in_specs, out_specs=out_specs, out_shape=out_shape,
        scratch_shapes=list(scratch),
        compiler_params=pltpu.CompilerParams(dimension_semantics=sem, vmem_limit_bytes=vmem),
    )


def _tile(n, want):
    t = min(n, want)
    while n % t:
        t //= 2
    return t


def _rms_fwd(x, g):
    T, D = x.shape
    tm = _tile(T, 512)

    def body(x_ref, g_ref, h_ref):
        xf = x_ref[...]
        r = lax.rsqrt(jnp.mean(xf * xf, axis=-1, keepdims=True) + RMS_EPS)
        h_ref[...] = (xf * r * g_ref[...]).astype(BF)

    return _call(body, name="rms_fwd", grid=(T // tm,),
                 in_specs=[pl.BlockSpec((tm, D), lambda i: (i, 0)), pl.BlockSpec((1, D), lambda i: (0, 0))],
                 out_specs=pl.BlockSpec((tm, D), lambda i: (i, 0)),
                 out_shape=jax.ShapeDtypeStruct((T, D), BF), sem=("parallel",))(x, g)


def _rms_bwd(x, g, dh, dres):
    T, D = x.shape
    tm = _tile(T, 512)

    def body(x_ref, g_ref, dh_ref, dr_ref, dx_ref, dg_ref):
        @pl.when(pl.program_id(0) == 0)
        def _():
            dg_ref[...] = jnp.zeros_like(dg_ref)

        xf = x_ref[...]
        r = lax.rsqrt(jnp.mean(xf * xf, axis=-1, keepdims=True) + RMS_EPS)
        dh_ = dh_ref[...]
        u = dh_ * g_ref[...]
        mu = jnp.mean(u * xf, axis=-1, keepdims=True)
        dx_ref[...] = dr_ref[...] + r * (u - xf * (r * r * mu))
        dg_ref[...] += jnp.sum(dh_ * xf * r, axis=0, keepdims=True)

    row = pl.BlockSpec((tm, D), lambda i: (i, 0))
    vec = pl.BlockSpec((1, D), lambda i: (0, 0))
    return _call(body, name="rms_bwd", grid=(T // tm,), in_specs=[row, vec, row, row], out_specs=[row, vec],
                 out_shape=[jax.ShapeDtypeStruct((T, D), F32), jax.ShapeDtypeStruct((1, D), F32)],
                 sem=("arbitrary",))(x, g, dh, dres)


def _final_loss(x, g, tgt):
    T, D = x.shape
    tm = _tile(T, 512)

    def body(x_ref, g_ref, t_ref, l_ref, dx_ref, dg_ref):
        @pl.when(pl.program_id(0) == 0)
        def _():
            dg_ref[...] = jnp.zeros_like(dg_ref)
            l_ref[...] = jnp.zeros_like(l_ref)

        xf = x_ref[...]
        gg = g_ref[...]
        r = lax.rsqrt(jnp.mean(xf * xf, axis=-1, keepdims=True) + RMS_EPS)
        xn = xf * r
        e = xn * gg - t_ref[...]
        part = 0.5 * jnp.sum(jnp.mean(e * e, axis=-1, keepdims=True), axis=0, keepdims=True)
        l_ref[...] += jnp.broadcast_to(part, l_ref.shape)
        dy = e * (1.0 / D)
        u = dy * gg
        mu = jnp.mean(u * xf, axis=-1, keepdims=True)
        dx_ref[...] = r * (u - xf * (r * r * mu))
        dg_ref[...] += jnp.sum(dy * xn, axis=0, keepdims=True)

    row = pl.BlockSpec((tm, D), lambda i: (i, 0))
    vec = pl.BlockSpec((1, D), lambda i: (0, 0))
    lsp = pl.BlockSpec((1, 128), lambda i: (0, 0))
    return _call(body, name="final_loss", grid=(T // tm,), in_specs=[row, vec, row], out_specs=[lsp, row, vec],
                 out_shape=[jax.ShapeDtypeStruct((1, 128), F32), jax.ShapeDtypeStruct((T, D), F32),
                            jax.ShapeDtypeStruct((1, D), F32)],
                 sem=("arbitrary",))(x, g, tgt)


def _slabs(tm, n=2):
    return [slice(k * tm // n, (k + 1) * tm // n) for k in range(n)] if tm % (16 * n) == 0 else [slice(0, tm)]


def _resident(shape):
    return pl.BlockSpec(shape, lambda *_: (0,) * len(shape), pipeline_mode=pl.Buffered(1))


def _ffn_fwd_k(x, gn, wg, wu, wd):
    T, D = x.shape
    NS, _, Fs = wg.shape
    tm = _tile(T, 512)

    def body(x_ref, gn_ref, wg_ref, wu_ref, wd_ref, o_ref, h_ref, s1_ref, s2_ref, a_ref, hs, acc):
        j = pl.program_id(1)

        @pl.when(j == 0)
        def _():
            xf = x_ref[...]
            r = lax.rsqrt(jnp.mean(xf * xf, axis=-1, keepdims=True) + RMS_EPS)
            hs[...] = (xf * r * gn_ref[...]).astype(BF)
            h_ref[...] = hs[...]
            acc[...] = jnp.zeros_like(acc)

        h = hs[...]
        g = _dot(h, wg_ref[j])
        u = _dot(h, wu_ref[j])
        sg = _sigmoid(g)
        s1 = g * sg
        a = (s1 * u).astype(BF)
        s1_ref[...] = s1.astype(BF)
        s2_ref[...] = (u * (sg * (1.0 + g * (1.0 - sg)))).astype(BF)
        a_ref[...] = a
        acc[...] += _dot(a, wd_ref[j])

        @pl.when(j == NS - 1)
        def _():
            o_ref[...] = x_ref[...] + 0.5 * acc[...]

    row = pl.BlockSpec((tm, D), lambda i, j: (i, 0))
    act = pl.BlockSpec((None, tm, Fs), lambda i, j: (j, i, 0))
    sh = jax.ShapeDtypeStruct((NS, T, Fs), BF)
    return _call(body, name="ffn_fwd", grid=(T // tm, NS),
                 in_specs=[row, pl.BlockSpec((1, D), lambda i, j: (0, 0)), _resident(wg.shape), _resident(wu.shape),
                           _resident(wd.shape)],
                 out_specs=[row, row, act, act, act],
                 out_shape=[jax.ShapeDtypeStruct((T, D), F32), jax.ShapeDtypeStruct((T, D), BF), sh, sh, sh],
                 scratch=[pltpu.VMEM((tm, D), BF), pltpu.VMEM((tm, D), F32)],
                 sem=("parallel", "arbitrary"))(x, gn, wg, wu, wd)


def _ffn_bwd_k1(dxo, x, gn, s1, s2, wg, wu, wd):
    NS, T, Fs = s1.shape
    D = x.shape[1]
    tm = _tile(T, 512)

    def body(dxo_ref, x_ref, gn_ref, s1_ref, s2_ref, wg_ref, wu_ref, wd_ref,
             dx_ref, dgn_ref, dg_ref, du_ref, dy_ref, dys, acc):
        i, j = pl.program_id(0), pl.program_id(1)

        @pl.when((i == 0) & (j == 0))
        def _():
            dgn_ref[...] = jnp.zeros_like(dgn_ref)

        @pl.when(j == 0)
        def _():
            dys[...] = (0.5 * dxo_ref[...]).astype(BF)
            dy_ref[...] = dys[...]
            acc[...] = jnp.zeros_like(acc)

        for rows in _slabs(tm):
            da = _dot_nt(dys[rows, :], wd_ref[j])
            dg = (da * s2_ref[rows, :].astype(F32)).astype(BF)
            du = (da * s1_ref[rows, :].astype(F32)).astype(BF)
            dg_ref[rows, :] = dg
            du_ref[rows, :] = du
            acc[rows, :] += _dot_nt(dg, wg_ref[j]) + _dot_nt(du, wu_ref[j])

        @pl.when(j == NS - 1)
        def _():
            xf = x_ref[...]
            r = lax.rsqrt(jnp.mean(xf * xf, axis=-1, keepdims=True) + RMS_EPS)
            dh = acc[...]
            uu = dh * gn_ref[...]
            mu = jnp.mean(uu * xf, axis=-1, keepdims=True)
            dx_ref[...] = dxo_ref[...] + r * (uu - xf * (r * r * mu))
            dgn_ref[...] += jnp.sum(dh * xf * r, axis=0, keepdims=True)

    row = pl.BlockSpec((tm, D), lambda i, j: (i, 0))
    vec = pl.BlockSpec((1, D), lambda i, j: (0, 0))
    act = pl.BlockSpec((None, tm, Fs), lambda i, j: (j, i, 0))
    sh = jax.ShapeDtypeStruct((NS, T, Fs), BF)
    return _call(body, name="ffn_bwd_x", grid=(T // tm, NS),
                 in_specs=[row, row, vec, act, act, _resident(wg.shape), _resident(wu.shape), _resident(wd.shape)],
                 out_specs=[row, vec, act, act, row],
                 out_shape=[jax.ShapeDtypeStruct((T, D), F32), jax.ShapeDtypeStruct((1, D), F32), sh, sh,
                            jax.ShapeDtypeStruct((T, D), BF)],
                 scratch=[pltpu.VMEM((tm, D), BF), pltpu.VMEM((tm, D), F32)],
                 sem=("arbitrary", "arbitrary"))(dxo, x, gn, s1, s2, wg, wu, wd)


def _ffn_bwd_k2(hb, dyb, a, dg, du):
    NS, T, Fs = a.shape
    D = hb.shape[1]
    tk = _tile(T, 512)

    def body(h_ref, dy_ref, a_ref, dg_ref, du_ref, og_ref, ou_ref, od_ref):
        @pl.when(pl.program_id(1) == 0)
        def _():
            og_ref[...] = jnp.zeros_like(og_ref)
            ou_ref[...] = jnp.zeros_like(ou_ref)
            od_ref[...] = jnp.zeros_like(od_ref)

        h = h_ref[...]
        og_ref[...] += _dot_tn(dg_ref[...], h)
        ou_ref[...] += _dot_tn(du_ref[...], h)
        od_ref[...] += _dot_tn(a_ref[...], dy_ref[...])

    row = pl.BlockSpec((tk, D), lambda j, k: (k, 0))
    act = pl.BlockSpec((None, tk, Fs), lambda j, k: (j, k, 0))
    return _call(body, name="ffn_bwd_w", grid=(NS, T // tk), in_specs=[row, row, act, act, act],
                 out_specs=[pl.BlockSpec((None, Fs, D), lambda j, k: (j, 0, 0))] * 3,
                 out_shape=[jax.ShapeDtypeStruct((NS, Fs, D), F32)] * 3,
                 sem=("parallel", "arbitrary"))(hb, dyb, a, dg, du)


def _mm_nn(a, b, res=None, out_dtype=F32):
    T, K = a.shape
    N = b.shape[1]
    tm = _tile(T, 512)
    tn = N if N <= 2048 else _tile(N, 1024)

    def body(*refs):
        if res is None:
            a_ref, b_ref, o_ref = refs
            o_ref[...] = _dot(a_ref[...], b_ref[...]).astype(out_dtype)
        else:
            a_ref, b_ref, r_ref, o_ref = refs
            o_ref[...] = (r_ref[...] + _dot(a_ref[...], b_ref[...])).astype(out_dtype)

    o = pl.BlockSpec((tm, tn), lambda i, j: (i, j))
    ins = [pl.BlockSpec((tm, K), lambda i, j: (i, 0)), pl.BlockSpec((K, tn), lambda i, j: (0, j))]
    args = [a, b]
    if res is not None:
        ins.append(o)
        args.append(res)
    return _call(body, name="mm_nn", grid=(T // tm, N // tn), in_specs=ins, out_specs=o,
                 out_shape=jax.ShapeDtypeStruct((T, N), out_dtype), sem=("parallel", "parallel"))(*args)


def _mm_nt(a, b, res=None):
    T, K = a.shape
    N = b.shape[0]
    tm = _tile(T, 512)

    def body(*refs):
        if res is None:
            a_ref, b_ref, o_ref = refs
            o_ref[...] = _dot_nt(a_ref[...].astype(BF), b_ref[...])
        else:
            a_ref, b_ref, r_ref, o_ref = refs
            o_ref[...] = r_ref[...] + _dot_nt(a_ref[...].astype(BF), b_ref[...])

    o = pl.BlockSpec((tm, N), lambda i: (i, 0))
    ins = [pl.BlockSpec((tm, K), lambda i: (i, 0)), pl.BlockSpec((N, K), lambda i: (0, 0))]
    args = [a, b]
    if res is not None:
        ins.append(o)
        args.append(res)
    return _call(body, name="mm_nt", grid=(T // tm,), in_specs=ins, out_specs=o,
                 out_shape=jax.ShapeDtypeStruct((T, N), F32), sem=("parallel",))(*args)


def _mm_tn(a, b):
    T, M = a.shape
    N = b.shape[1]
    tk = _tile(T, 512)
    tmm = _tile(M, 512)

    def body(a_ref, b_ref, o_ref):
        @pl.when(pl.program_id(1) == 0)
        def _():
            o_ref[...] = jnp.zeros_like(o_ref)

        o_ref[...] += _dot_tn(a_ref[...].astype(BF), b_ref[...].astype(BF))

    return _call(body, name="mm_tn", grid=(M // tmm, T // tk),
                 in_specs=[pl.BlockSpec((tk, tmm), lambda i, k: (k, i)), pl.BlockSpec((tk, N), lambda i, k: (k, 0))],
                 out_specs=pl.BlockSpec((tmm, N), lambda i, k: (i, 0)),
                 out_shape=jax.ShapeDtypeStruct((M, N), F32), sem=("parallel", "arbitrary"))(a, b)


def _lane_mask(e, width=128):
    return (lax.broadcasted_iota(jnp.int32, (1, width), 1) // HEAD) == e


def _band_mask(n):
    qi = lax.broadcasted_iota(jnp.int32, (CHUNK, 2 * CHUNK), 0)
    kj = lax.broadcasted_iota(jnp.int32, (CHUNK, 2 * CHUNK), 1)
    dist = qi + CHUNK - kj
    return (dist >= 0) & (dist <= CHUNK) & ((kj >= CHUNK) | (n > 0))


def _sub_rows(r, block, dil):
    if dil == 1:
        return pl.ds(pl.multiple_of(block * CHUNK, CHUNK), CHUNK)
    return pl.ds(r + dil * CHUNK * block, CHUNK, stride=dil)


def _attn_specs(T, dil):
    B, nb = T // SEQ, SEQ // (CHUNK * dil)
    once = dict(pipeline_mode=pl.Buffered(1))
    q_like = lambda col: pl.BlockSpec((CHUNK * dil, 128), lambda b, n, r: (b * nb + n, col), **(once if nb == 1 else {}))
    k_like = lambda col: pl.BlockSpec((SEQ, 128), lambda b, n, r: (b, col), **once)
    return B, nb, q_like, k_like


def _attn_fwd(qkv, dil):
    T = qkv.shape[0]
    B, nb, q_like, k_like = _attn_specs(T, dil)
    scale = HEAD ** -0.5

    def body(*refs):
        q_t, k_t, v_t, o_t, l_t = refs[0:3], refs[3:6], refs[6:9], refs[9:12], refs[12:15]
        n, r = pl.program_id(1), pl.program_id(2)
        mine = _sub_rows(r, 0, dil)
        cur, prv = _sub_rows(r, n, dil), _sub_rows(r, jnp.maximum(n - 1, 0), dil)
        mask = _band_mask(n)
        for t in range(3):
            qt = q_t[t][mine, :].astype(BF)
            kt = jnp.concatenate([k_t[t][prv, :], k_t[t][cur, :]], axis=0).astype(BF)
            vt = jnp.concatenate([v_t[t][prv, :], v_t[t][cur, :]], axis=0).astype(BF)
            o_pair = jnp.zeros((CHUNK, 128), F32)
            l_pair = jnp.zeros((CHUNK, 128), F32)
            for e in range(2):
                lm = _lane_mask(e)
                s = _dot_nt(jnp.where(lm, qt, jnp.zeros_like(qt)), kt) * scale
                s = jnp.where(mask, s, NEG)
                m = jnp.max(s, axis=-1, keepdims=True)
                p = jnp.exp(s - m)
                den = jnp.sum(p, axis=-1, keepdims=True)
                o = _dot(p.astype(BF), vt) / den
                o_pair = jnp.where(lm, o, o_pair)
                l_pair = jnp.where(lm, m + jnp.log(den), l_pair)
            o_t[t][mine, :] = o_pair
            l_t[t][mine, :] = l_pair

    out_spec = pl.BlockSpec((CHUNK * dil, 128), lambda b, n, r: (b * nb + n, 0))
    sh = jax.ShapeDtypeStruct((T, 128), F32)
    outs = _call(
        body, name=f"attn_fwd_d{dil}", grid=(B, nb, dil),
        in_specs=[q_like(t) for t in range(3)] + [k_like(3 + t) for t in range(3)] + [k_like(6 + t) for t in range(3)],
        out_specs=[out_spec] * 6, out_shape=[sh] * 6, sem=("parallel", "arbitrary", "arbitrary"))(*([qkv] * 9))
    return list(outs[0:3]), list(outs[3:6])


def _attn_combine(branches):
    T = branches[0][0][0].shape[0]
    tm = _tile(T, 512)

    def body(*refs):
        y_ref, l_ref = refs[-2:]
        for t in range(3):
            o = [refs[6 * i + t][...] for i in range(3)]
            a, b, c = [refs[6 * i + 3 + t][...] for i in range(3)]
            m = jnp.maximum(jnp.maximum(a, b), c)
            ea, eb, ec = jnp.exp(a - m), jnp.exp(b - m), jnp.exp(c - m)
            z = ea + eb + ec
            y_ref[:, 128 * t:128 * (t + 1)] = (ea * o[0] + eb * o[1] + ec * o[2]) / z
            l_ref[:, 128 * t:128 * (t + 1)] = m + jnp.log(z)

    tile = pl.BlockSpec((tm, 128), lambda i: (i, 0))
    row = pl.BlockSpec((tm, ATT_W), lambda i: (i, 0))
    sh = jax.ShapeDtypeStruct((T, ATT_W), F32)
    flat = [a for o_t, l_t in branches for a in (*o_t, *l_t)]
    return _call(body, name="attn_combine", grid=(T // tm,), in_specs=[tile] * 18, out_specs=[row, row],
                 out_shape=[sh, sh], sem=("parallel",))(*flat)


def _attn_bwd(qkv, do, out, lse, dil):
    T = qkv.shape[0]
    B, nb, q_like, k_like = _attn_specs(T, dil)
    scale = HEAD ** -0.5

    def body(*refs):
        q_t, k_t, v_t = refs[0:3], refs[3:6], refs[6:9]
        do_t, out_t, lse_t = refs[9:12], refs[12:15], refs[15:18]
        dq_t, dk_t, dv_t = refs[18:21], refs[21:24], refs[24:27]
        n, r = pl.program_id(1), pl.program_id(2)

        @pl.when((n == 0) & (r == 0))
        def _():
            for t in range(3):
                dk_t[t][...] = jnp.zeros_like(dk_t[t])
                dv_t[t][...] = jnp.zeros_like(dv_t[t])

        mine = _sub_rows(r, 0, dil)
        cur, prv = _sub_rows(r, n, dil), _sub_rows(r, jnp.maximum(n - 1, 0), dil)
        mask = _band_mask(n)
        for t in range(3):
            qt = q_t[t][mine, :].astype(BF)
            kt = jnp.concatenate([k_t[t][prv, :], k_t[t][cur, :]], axis=0).astype(BF)
            vt = jnp.concatenate([v_t[t][prv, :], v_t[t][cur, :]], axis=0).astype(BF)
            do_ = do_t[t][mine, :]
            dlt = do_ * out_t[t][mine, :]
            ls = lse_t[t][mine, :]
            dq_pair = jnp.zeros((CHUNK, 128), F32)
            dk_acc = jnp.zeros((2 * CHUNK, 128), F32)
            dv_acc = jnp.zeros((2 * CHUNK, 128), F32)
            for e in range(2):
                lm = _lane_mask(e)
                qm = jnp.where(lm, qt, jnp.zeros_like(qt))
                s = _dot_nt(qm, kt) * scale
                p = jnp.exp(jnp.where(mask, s - ls[:, HEAD * e:HEAD * e + 1], NEG))
                dom = jnp.where(lm, do_, 0.0).astype(BF)
                dv_acc += _dot_tn(p.astype(BF), dom)
                dp = _dot_nt(dom, vt)
                delta = jnp.sum(jnp.where(lm, dlt, 0.0), axis=-1, keepdims=True)
                ds = (p * (dp - delta) * scale).astype(BF)
                dq_pair += jnp.where(lm, _dot(ds, kt), 0.0)
                dk_acc += _dot_tn(ds, qm)
            dq_t[t][mine, :] = dq_pair
            dk_t[t][cur, :] = dk_t[t][cur, :] + dk_acc[CHUNK:]
            dk_t[t][prv, :] = dk_t[t][prv, :] + dk_acc[:CHUNK]
            dv_t[t][cur, :] = dv_t[t][cur, :] + dv_acc[CHUNK:]
            dv_t[t][prv, :] = dv_t[t][prv, :] + dv_acc[:CHUNK]

    q_out = pl.BlockSpec((CHUNK * dil, 128), lambda b, n, r: (b * nb + n, 0))
    k_out = pl.BlockSpec((SEQ, 128), lambda b, n, r: (b, 0))
    sh = jax.ShapeDtypeStruct((T, 128), F32)
    tiles = lambda: [q_like(t) for t in range(3)]
    return list(_call(
        body, name=f"attn_bwd_d{dil}", grid=(B, nb, dil),
        in_specs=tiles() + [k_like(3 + t) for t in range(3)] + [k_like(6 + t) for t in range(3)]
        + tiles() + tiles() + tiles(),
        out_specs=[q_out] * 3 + [k_out] * 6, out_shape=[sh] * 9,
        sem=("parallel", "arbitrary", "arbitrary"), vmem=ATTN_BWD_VMEM)(*([qkv] * 9 + [do] * 3 + [out] * 3 + [lse] * 3)))


def _sum_branches(parts):
    T = parts[0][0].shape[0]
    tm = _tile(T, 512)

    def body(*refs):
        o_ref = refs[-1]
        for c in range(9):
            acc = refs[c][...] + refs[9 + c][...] + refs[18 + c][...]
            o_ref[:, 128 * c:128 * (c + 1)] = acc.astype(BF)

    tile = pl.BlockSpec((tm, 128), lambda i: (i, 0))
    flat = [a for br in parts for a in br]
    return _call(body, name="attn_sum_branches", grid=(T // tm,), in_specs=[tile] * 27,
                 out_specs=pl.BlockSpec((tm, W_QKV), lambda i: (i, 0)),
                 out_shape=jax.ShapeDtypeStruct((T, W_QKV), BF), sem=("parallel",))(*flat)


def _silu(x):
    return x * _sigmoid(x)


def _dsilu(x):
    s = _sigmoid(x)
    return s * (1.0 + x * (1.0 - s))


def _log1p(u):
    return jnp.where(u < 0.01, u * (1.0 - u * (0.5 - u * (1.0 / 3.0))), jnp.log(1.0 + u))


def _softplus(x):
    return jnp.maximum(x, 0.0) + _log1p(jnp.exp(-jnp.abs(x)))


def _cumsum_rows(x, reverse=False):
    n = x.shape[0]
    rows = lax.broadcasted_iota(jnp.int32, x.shape, 0)
    k = 1
    while k < n:
        if reverse:
            x = x + jnp.where(rows < n - k, pltpu.roll(x, n - k, 0), 0.0)
        else:
            x = x + jnp.where(rows >= k, pltpu.roll(x, k, 0), 0.0)
        k *= 2
    return x


def _tri():
    r = lax.broadcasted_iota(jnp.int32, (CHUNK, CHUNK), 0)
    c = lax.broadcasted_iota(jnp.int32, (CHUNK, CHUNK), 1)
    return r >= c


def _row_mask(e):
    return (lax.broadcasted_iota(jnp.int32, (128, 1), 0) // HEAD) == e


def _first_lane(e):
    return lax.broadcasted_iota(jnp.int32, (1, 128), 1) == HEAD * e


def _ssd_pre(x_ref, halo_ref, first, cw_ref, cb_ref, dtb_ref, al_ref, ext):
    row = x_ref[...]
    z = row[:, SSD_CONV_DIM:SSD_CONV_DIM + SSD_W]
    u = row[:, SSD_CONV_DIM + SSD_W:] + dtb_ref[...]
    ext[0:8, :] = jnp.where(first, 0.0, halo_ref[:, 0:SSD_CONV_DIM])
    ext[8:8 + CHUNK, :] = row[:, 0:SSD_CONV_DIM]
    xc = cb_ref[...]
    for j in range(4):
        xc = xc + cw_ref[j:j + 1, :] * ext[pl.ds(5 + j, CHUNK), :]
    xa = _silu(xc)
    dt = _softplus(u)
    a = dt * (-jnp.exp(al_ref[...]))
    A = _cumsum_rows(a)
    return dict(z=z, u=u, xc=xc, xs=xa[:, 0:SSD_W], Bm=xa[:, SSD_W:SSD_W + 256], Cm=xa[:, SSD_W + 256:],
                dt=dt, a=a, A=A, AT=A.T, eA=jnp.exp(A), wdec=jnp.exp(A[CHUNK - 1:CHUNK, :] - A),
                dtot=jnp.exp(A[CHUNK - 1:CHUNK, :]))


def _ssd_y(p, hp_ref, dskip):
    tri = _tri()
    X = p["xs"] * p["dt"]
    Bb = [p["Bm"][:, 128 * g:128 * (g + 1)].astype(BF) for g in range(2)]
    Cb = [p["Cm"][:, 128 * g:128 * (g + 1)].astype(BF) for g in range(2)]
    CB = [_dot_nt(Cb[g], Bb[g]) for g in range(2)]
    tiles = []
    for t in range(3):
        sl = slice(128 * t, 128 * (t + 1))
        hpb = hp_ref[sl, :].astype(BF)
        acc = jnp.zeros((CHUNK, 128), F32)
        for e in range(2):
            h = 2 * t + e
            g, col = h // 3, HEAD * h
            lm = _lane_mask(e)
            L = jnp.exp(jnp.where(tri, p["A"][:, col:col + 1] - p["AT"][col:col + 1, :], NEG))
            yd = _dot((CB[g] * L).astype(BF), jnp.where(lm, X[:, sl], 0.0).astype(BF))
            yo = _dot_nt(Cb[g], hpb) * p["eA"][:, sl]
            acc = acc + jnp.where(lm, yd + yo, 0.0)
        tiles.append(acc)
    return jnp.concatenate(tiles, axis=1) + dskip * p["xs"], X, Bb, Cb, CB


def _group_stats(v):
    g0 = lax.broadcasted_iota(jnp.int32, (1, SSD_W), 1) < SSD_W // 2
    m0 = jnp.sum(jnp.where(g0, v, 0.0), axis=-1, keepdims=True) * (2.0 / SSD_W)
    m1 = jnp.sum(jnp.where(g0, 0.0, v), axis=-1, keepdims=True) * (2.0 / SSD_W)
    return jnp.where(g0, m0, m1)


def _ssd_specs(T, rev):
    B = T // SEQ

    def chunk(b, c):
        return b * N_CHUNK + (N_CHUNK - 1 - c if rev else c)

    row = pl.BlockSpec((CHUNK, W_SSD), lambda b, c: (chunk(b, c), 0))
    halo = pl.BlockSpec((8, W_SSD), lambda b, c: (jnp.maximum(chunk(b, c) * (CHUNK // 8) - 1, 0), 0))
    hp = pl.BlockSpec((None, SSD_W, SSD_STATE), lambda b, c: (chunk(b, c), 0, 0))
    y = pl.BlockSpec((CHUNK, SSD_W), lambda b, c: (chunk(b, c), 0))
    const = lambda r, w: pl.BlockSpec((r, w), lambda b, c: (0, 0))
    params = [const(4, SSD_CONV_DIM), const(1, SSD_CONV_DIM)] + [const(1, SSD_W)] * 4
    return B, row, halo, hp, y, const, params


def _ssd_fwd(sin, conv_w, conv_b, dtb, alog, dskip, norm_g):
    T = sin.shape[0]
    B, row, halo, hp, y, const, params = _ssd_specs(T, False)

    def body(x_ref, halo_ref, cw_ref, cb_ref, dtb_ref, al_ref, dk_ref, ng_ref, y_ref, hp_ref, ext, hst):
        c = pl.program_id(1)

        @pl.when(c == 0)
        def _():
            hst[...] = jnp.zeros_like(hst)

        p = _ssd_pre(x_ref, halo_ref, c == 0, cw_ref, cb_ref, dtb_ref, al_ref, ext)
        yv, X, Bb, Cb, CB = _ssd_y(p, hst, dk_ref[...])
        hp_ref[...] = hst[...]
        for t in range(3):
            sl = slice(128 * t, 128 * (t + 1))
            old = hst[sl, :]
            new = old
            for e in range(2):
                h = 2 * t + e
                g, col = h // 3, HEAD * h
                st = _dot_tn(jnp.where(_lane_mask(e), X[:, sl] * p["wdec"][:, sl], 0.0).astype(BF), Bb[g])
                new = jnp.where(_row_mask(e), old * p["dtot"][:, col:col + 1] + st, new)
            hst[sl, :] = new
        y2 = yv * _silu(p["z"])
        r = lax.rsqrt(_group_stats(y2 * y2) + RMS_EPS)
        y_ref[...] = y2 * r * ng_ref[...]

    return _call(body, name="ssd_fwd", grid=(B, N_CHUNK), in_specs=[row, halo] + params, out_specs=[y, hp],
                 out_shape=[jax.ShapeDtypeStruct((T, SSD_W), F32),
                            jax.ShapeDtypeStruct((T // CHUNK, SSD_W, SSD_STATE), F32)],
                 scratch=[pltpu.VMEM((8 + CHUNK, SSD_CONV_DIM), F32), pltpu.VMEM((SSD_W, SSD_STATE), F32)],
                 sem=("parallel", "arbitrary"))(sin, sin, conv_w, conv_b, dtb, alog, dskip, norm_g)


def _ssd_bwd(sin, hprev, dy3, conv_w, conv_b, dtb, alog, dskip, norm_g):
    T = sin.shape[0]
    B, row, halo, hp, y, const, params = _ssd_specs(T, True)

    def body(x_ref, halo_ref, hp_ref, dy_ref, cw_ref, cb_ref, dtb_ref, al_ref, dk_ref, ng_ref,
             dx_ref, dcw_ref, dcb_ref, dvec_ref, ext, ext2, dh):
        c = pl.program_id(1)

        @pl.when((pl.program_id(0) == 0) & (c == 0))
        def _():
            dcw_ref[...] = jnp.zeros_like(dcw_ref)
            dcb_ref[...] = jnp.zeros_like(dcb_ref)
            dvec_ref[...] = jnp.zeros_like(dvec_ref)

        @pl.when(c == 0)
        def _():
            dh[...] = jnp.zeros_like(dh)
            ext2[CHUNK:CHUNK + 8, :] = jnp.zeros((8, SSD_CONV_DIM), F32)

        p = _ssd_pre(x_ref, halo_ref, c == N_CHUNK - 1, cw_ref, cb_ref, dtb_ref, al_ref, ext)
        dskip_ = dk_ref[...]
        yv, X, Bb, Cb, CB = _ssd_y(p, hp_ref, dskip_)
        xs, z, A, AT = p["xs"], p["z"], p["A"], p["AT"]

        sz = _silu(z)
        y2 = yv * sz
        r = lax.rsqrt(_group_stats(y2 * y2) + RMS_EPS)
        dy3_ = dy_ref[...]
        uu = dy3_ * ng_ref[...]
        dy2 = r * (uu - y2 * (r * r * _group_stats(uu * y2)))
        dy = dy2 * sz
        dz = dy2 * yv * _dsilu(z)

        tri = _tri()
        rows = lax.broadcasted_iota(jnp.int32, (CHUNK, 1), 0)
        dG = [jnp.zeros((CHUNK, CHUNK), F32) for _ in range(2)]
        dB = [jnp.zeros((CHUNK, SSD_STATE), F32) for _ in range(2)]
        dC = [jnp.zeros((CHUNK, SSD_STATE), F32) for _ in range(2)]
        dX_t, dA_t, ddtx_t = [], [], []
        for t in range(3):
            sl = slice(128 * t, 128 * (t + 1))
            hp_t = hp_ref[sl, :]
            hpb = hp_t.astype(BF)
            dhc = dh[sl, :]
            dh_new = jnp.zeros((128, SSD_STATE), F32)
            dX = jnp.zeros((CHUNK, 128), F32)
            dA = jnp.zeros((CHUNK, 128), F32)
            ddtx = jnp.zeros((CHUNK, 128), F32)
            for e in range(2):
                h = 2 * t + e
                g, col = h // 3, HEAD * h
                lm, rm, fl = _lane_mask(e), _row_mask(e), _first_lane(e)
                L = jnp.exp(jnp.where(tri, A[:, col:col + 1] - AT[col:col + 1, :], NEG))
                Mf = CB[g] * L
                Xm = jnp.where(lm, X[:, sl], 0.0)
                Xmb = Xm.astype(BF)
                dyh = jnp.where(lm, dy[:, sl], 0.0)
                dyb = dyh.astype(BF)
                dXh = _dot_tn(Mf.astype(BF), dyb)
                dM = jnp.where(tri, _dot_nt(dyb, Xmb), 0.0)
                Wm = dM * Mf
                dAc = jnp.sum(Wm, axis=-1, keepdims=True) - jnp.sum(Wm.T, axis=-1, keepdims=True)
                dG[g] = dG[g] + dM * L
                eAt = p["eA"][:, sl]
                yo = _dot_nt(Cb[g], hpb)
                dyo = (dyh * eAt).astype(BF)
                dC[g] = dC[g] + _dot(dyo, hpb)
                dh_new = dh_new + _dot_tn(dyo, Cb[g])
                dAc = dAc + jnp.sum(dyh * yo * eAt, axis=-1, keepdims=True)
                dHn = jnp.where(rm, dhc, 0.0)
                dHnb = dHn.astype(BF)
                dec = p["dtot"][:, col:col + 1]
                dh_new = dh_new + dec * dHn
                Z = _dot_nt(Bb[g], dHnb)
                wt = p["wdec"][:, sl]
                xi = jnp.sum(Xm * Z, axis=-1, keepdims=True) * p["wdec"][:, col:col + 1]
                dXh = dXh + wt * Z
                dB[g] = dB[g] + _dot(jnp.where(lm, X[:, sl] * wt, 0.0).astype(BF), dHnb)
                dAtot = jnp.sum(xi, axis=0, keepdims=True) + dec * jnp.sum(
                    jnp.sum(dHn * hp_t, axis=-1, keepdims=True), axis=0, keepdims=True)
                dAc = dAc - xi + jnp.where(rows == CHUNK - 1, dAtot, 0.0)
                dA = dA + jnp.where(fl, dAc, 0.0)
                dX = dX + dXh
                ddtx = ddtx + jnp.where(fl, jnp.sum(dXh * xs[:, sl], axis=-1, keepdims=True), 0.0)
            dh[sl, :] = dh_new
            dX_t.append(dX)
            dA_t.append(dA)
            ddtx_t.append(ddtx)
        for g in range(2):
            dGb = dG[g].astype(BF)
            dC[g] = dC[g] + _dot(dGb, Bb[g])
            dB[g] = dB[g] + _dot_tn(dGb, Cb[g])
        dXf = jnp.concatenate(dX_t, axis=1)
        da = _cumsum_rows(jnp.concatenate(dA_t, axis=1), reverse=True)
        ddt = da * (-jnp.exp(al_ref[...])) + jnp.concatenate(ddtx_t, axis=1)
        du = ddt * _sigmoid(p["u"])
        dxs = dXf * p["dt"] + dskip_ * dy
        dxc = jnp.concatenate([dxs, dB[0], dB[1], dC[0], dC[1]], axis=1) * _dsilu(p["xc"])
        ext2[0:CHUNK, :] = dxc
        dxbc = jnp.zeros((CHUNK, SSD_CONV_DIM), F32)
        for j in range(4):
            dxbc = dxbc + cw_ref[j:j + 1, :] * ext2[pl.ds(3 - j, CHUNK), :]
            dcw_ref[j:j + 1, :] += jnp.sum(dxc * ext[pl.ds(5 + j, CHUNK), :], axis=0, keepdims=True)
        ext2[CHUNK:CHUNK + 8, :] = dxc[0:8, :]
        dcb_ref[...] += jnp.sum(dxc, axis=0, keepdims=True)
        dvec_ref[0:1, :] += jnp.sum(du, axis=0, keepdims=True)
        dvec_ref[1:2, :] += jnp.sum(da * p["a"], axis=0, keepdims=True)
        dvec_ref[2:3, :] += jnp.sum(dy * xs, axis=0, keepdims=True)
        dvec_ref[3:4, :] += jnp.sum(dy3_ * y2 * r, axis=0, keepdims=True)
        dx_ref[...] = jnp.concatenate([dxbc, dz, du], axis=1).astype(BF)

    return _call(body, name="ssd_bwd", grid=(B, N_CHUNK), in_specs=[row, halo, hp, y] + params,
                 out_specs=[row, const(4, SSD_CONV_DIM), const(1, SSD_CONV_DIM), const(8, SSD_W)],
                 out_shape=[jax.ShapeDtypeStruct((T, W_SSD), BF), jax.ShapeDtypeStruct((4, SSD_CONV_DIM), F32),
                            jax.ShapeDtypeStruct((1, SSD_CONV_DIM), F32), jax.ShapeDtypeStruct((8, SSD_W), F32)],
                 scratch=[pltpu.VMEM((8 + CHUNK, SSD_CONV_DIM), F32), pltpu.VMEM((8 + CHUNK, SSD_CONV_DIM), F32),
                          pltpu.VMEM((SSD_W, SSD_STATE), F32)],
                 sem=("arbitrary", "arbitrary"))(sin, sin, hprev, dy3, conv_w, conv_b, dtb, alog, dskip, norm_g)


def _sgu_core(uv_ref, g_ref, b_ref, w_ref, bias_ref):
    x = uv_ref[...]
    cdf = 0.5 * (1.0 + lax.erf(x * (2.0 ** -0.5)))
    ge = x * cdf
    dge = cdf + x * jnp.exp(-0.5 * x * x) * ((2.0 * math.pi) ** -0.5)
    u, v = ge[:, 0:SGU_W], ge[:, SGU_W:]
    vc = v - jnp.mean(v, axis=-1, keepdims=True)
    rstd = lax.rsqrt(jnp.mean(vc * vc, axis=-1, keepdims=True) + LN_EPS)
    vhat = vc * rstd
    vn = vhat * g_ref[...] + b_ref[...]
    tri = _tri()
    wc = [jnp.where(tri, w_ref[gi], 0.0).astype(BF) for gi in range(4)]
    vm = [jnp.where(_lane_mask(gi % 2), vn[:, 128 * (gi // 2):128 * (gi // 2 + 1)], 0.0).astype(BF) for gi in range(4)]
    mixed = jnp.concatenate([_dot(wc[2 * t], vm[2 * t]) + _dot(wc[2 * t + 1], vm[2 * t + 1]) for t in range(2)],
                            axis=1) + bias_ref[...]
    return dict(dge=dge, u=u, rstd=rstd, vhat=vhat, wc=wc, vm=vm, mixed=mixed)


def _sgu_specs():
    vec = pl.BlockSpec((1, SGU_W), lambda i: (0, 0))
    return [pl.BlockSpec((CHUNK, W_UV), lambda i: (i, 0)), vec, vec,
            pl.BlockSpec((4, CHUNK, CHUNK), lambda i: (0, 0, 0)), pl.BlockSpec((CHUNK, SGU_W), lambda i: (0, 0))]


def _sgu_fwd(uv, ln_g, ln_b, w, bias):
    T = uv.shape[0]

    def body(uv_ref, g_ref, b_ref, w_ref, bias_ref, y_ref):
        s = _sgu_core(uv_ref, g_ref, b_ref, w_ref, bias_ref)
        y_ref[...] = s["u"] * s["mixed"]

    return _call(body, name="sgu_fwd", grid=(T // CHUNK,), in_specs=_sgu_specs(),
                 out_specs=pl.BlockSpec((CHUNK, SGU_W), lambda i: (i, 0)),
                 out_shape=jax.ShapeDtypeStruct((T, SGU_W), F32), sem=("parallel",))(uv, ln_g, ln_b, w, bias)


def _sgu_bwd(uv, dy, ln_g, ln_b, w, bias):
    T = uv.shape[0]

    def body(uv_ref, dy_ref, g_ref, b_ref, w_ref, bias_ref, dx_ref, dw_ref, dbias_ref, dln_ref):
        @pl.when(pl.program_id(0) == 0)
        def _():
            dw_ref[...] = jnp.zeros_like(dw_ref)
            dbias_ref[...] = jnp.zeros_like(dbias_ref)
            dln_ref[...] = jnp.zeros_like(dln_ref)

        s = _sgu_core(uv_ref, g_ref, b_ref, w_ref, bias_ref)
        dy_ = dy_ref[...]
        du = dy_ * s["mixed"]
        dmix = dy_ * s["u"]
        dbias_ref[...] += dmix
        tri = _tri()
        dvn_t = []
        for t in range(2):
            acc = jnp.zeros((CHUNK, 128), F32)
            for e in range(2):
                gi = 2 * t + e
                dmg = jnp.where(_lane_mask(e), dmix[:, 128 * t:128 * (t + 1)], 0.0).astype(BF)
                acc = acc + _dot_tn(s["wc"][gi], dmg)
                dw_ref[gi] += jnp.where(tri, _dot_nt(dmg, s["vm"][gi]), 0.0)
            dvn_t.append(acc)
        dvn = jnp.concatenate(dvn_t, axis=1)
        dln_ref[0:1, :] += jnp.sum(dvn * s["vhat"], axis=0, keepdims=True)
        dln_ref[1:2, :] += jnp.sum(dvn, axis=0, keepdims=True)
        dvh = dvn * g_ref[...]
        dv = s["rstd"] * (dvh - jnp.mean(dvh, axis=-1, keepdims=True)
                          - s["vhat"] * jnp.mean(dvh * s["vhat"], axis=-1, keepdims=True))
        dx_ref[...] = (jnp.concatenate([du, dv], axis=1) * s["dge"]).astype(BF)

    ins = _sgu_specs()
    return _call(body, name="sgu_bwd", grid=(T // CHUNK,),
                 in_specs=[ins[0], pl.BlockSpec((CHUNK, SGU_W), lambda i: (i, 0))] + ins[1:],
                 out_specs=[pl.BlockSpec((CHUNK, W_UV), lambda i: (i, 0)),
                            pl.BlockSpec((4, CHUNK, CHUNK), lambda i: (0, 0, 0)),
                            pl.BlockSpec((CHUNK, SGU_W), lambda i: (0, 0)), pl.BlockSpec((8, SGU_W), lambda i: (0, 0))],
                 out_shape=[jax.ShapeDtypeStruct((T, W_UV), BF), jax.ShapeDtypeStruct((4, CHUNK, CHUNK), F32),
                            jax.ShapeDtypeStruct((CHUNK, SGU_W), F32), jax.ShapeDtypeStruct((8, SGU_W), F32)],
                 sem=("arbitrary",))(uv, dy, ln_g, ln_b, w, bias)


def _adamw(w, g, m, v):
    R, C = w.shape
    tr = R

    def body(w_ref, g_ref, m_ref, v_ref, d_ref, nm_ref, nv_ref):
        g_ = g_ref[...]
        m2 = ADAM_B1 * m_ref[...] + (1.0 - ADAM_B1) * g_
        v2 = ADAM_B2 * v_ref[...] + (1.0 - ADAM_B2) * (g_ * g_)
        m_hat = m2 / (1.0 - ADAM_B1 ** ADAM_STEP)
        v_hat = v2 / (1.0 - ADAM_B2 ** ADAM_STEP)
        d_ref[...] = -ADAM_LR * (m_hat / (jnp.sqrt(v_hat) + ADAM_EPS) + ADAM_WD * w_ref[...])
        nm_ref[...] = m2
        nv_ref[...] = v2

    blk = pl.BlockSpec((tr, C), lambda i: (i, 0))
    sh = jax.ShapeDtypeStruct((R, C), F32)
    return _call(body, name="adamw", grid=(R // tr,), in_specs=[blk] * 4, out_specs=[blk] * 3,
                 out_shape=[sh] * 3, sem=("parallel",))(w, g, m, v)


def _adamw_pair(w, g0, g1, m, v, dep):
    L, R, C = w.shape
    tr = _tile(R, 256 if C <= 1024 else 64)

    def body(w_ref, g0_ref, g1_ref, m_ref, v_ref, dep_ref, d_ref, nm_ref, nv_ref, og_ref):
        g_ = jnp.where(pl.program_id(0) == 0, g0_ref[...], g1_ref[...])
        m2 = ADAM_B1 * m_ref[...] + (1.0 - ADAM_B1) * g_
        v2 = ADAM_B2 * v_ref[...] + (1.0 - ADAM_B2) * (g_ * g_)
        m_hat = m2 / (1.0 - ADAM_B1 ** ADAM_STEP)
        v_hat = v2 / (1.0 - ADAM_B2 ** ADAM_STEP)
        d_ref[...] = -ADAM_LR * (m_hat / (jnp.sqrt(v_hat) + ADAM_EPS) + ADAM_WD * w_ref[...])
        nm_ref[...] = m2
        nv_ref[...] = v2
        og_ref[...] = g_

    lay = pl.BlockSpec((None, tr, C), lambda l, i: (l, i, 0))
    one = lambda k: pl.BlockSpec((tr, C), lambda l, i: (jnp.where(l == k, i, 0), 0))
    return _call(body, name="adamw_pair", grid=(L, R // tr),
                 in_specs=[lay, one(0), one(1), lay, lay, pl.BlockSpec((8, 128), lambda l, i: (0, 0))],
                 out_specs=[lay] * 4,
                 out_shape=[jax.ShapeDtypeStruct((L, R, C), F32)] * 4,
                 sem=("parallel", "parallel"))(w, g0, g1, m, v, dep)


def _row_steps(rows):
    return 2 if rows % 32 == 0 else 1


def _pair_add(gbuf, rsib, c):
    NS, _, R, C = gbuf.shape
    n = _row_steps(R)
    tr = R // n

    def body(c_ref, a_ref, b_ref, o_ref):
        o_ref[...] = (a_ref[...] + b_ref[...]).astype(BF)

    blk = pl.BlockSpec((None, tr, C), lambda j, i, c_ref: (j, i, 0))
    return pl.pallas_call(
        body, name="rs_pair_add",
        grid_spec=pltpu.PrefetchScalarGridSpec(
            num_scalar_prefetch=1, grid=(NS, n),
            in_specs=[pl.BlockSpec((None, None, tr, C), lambda j, i, c_ref: (j, c_ref[0], i, 0)), blk],
            out_specs=blk),
        out_shape=jax.ShapeDtypeStruct((NS, R, C), BF),
        compiler_params=pltpu.CompilerParams(dimension_semantics=("parallel", "parallel")),
    )(jnp.reshape(c, (1,)).astype(jnp.int32), gbuf, rsib)


def _chip_sum(pair, recv, me, c):
    NS, R, C = pair.shape
    n = _row_steps(R)
    tr = R // n

    def body(s_ref, own_ref, p_ref, o_ref):
        p = [jnp.where(s_ref[0] == j, own_ref[...], p_ref[j]).astype(F32) for j in range(4)]
        o_ref[...] = ((p[0] + p[1]) + p[2]) + p[3]

    return pl.pallas_call(
        body, name="rs_chip_sum",
        grid_spec=pltpu.PrefetchScalarGridSpec(
            num_scalar_prefetch=1, grid=(n,),
            in_specs=[pl.BlockSpec((None, tr, C), lambda i, s: (s[0], i, 0)),
                      pl.BlockSpec((NS, tr, C), lambda i, s: (0, i, 0))],
            out_specs=pl.BlockSpec((None, tr, C), lambda i, s: (s[1], i, 0))),
        out_shape=jax.ShapeDtypeStruct((2, R, C), F32),
        compiler_params=pltpu.CompilerParams(dimension_semantics=("parallel",)),
    )(jnp.stack([me, c]).astype(jnp.int32), pair, recv)


MESH = pl.DeviceIdType.MESH
ANY = pl.BlockSpec(memory_space=pl.ANY)


def _place():
    x, y, c = lax.axis_index("x"), lax.axis_index("y"), lax.axis_index("c")
    return x, y, c, [(1 - x, y), (x, 1 - y), (1 - x, 1 - y)]


HBM = pl.BlockSpec(memory_space=pltpu.HBM)
SEM = pl.BlockSpec(memory_space=pltpu.SEMAPHORE)
EFFECT = pltpu.SideEffectType.DATAFLOW_SIDE_EFFECTING


class _Split:
    def __init__(self, tag, arrays, copies, n_copies):
        self.tag, self.copies, k = tag, copies, len(arrays)

        def body(*refs):
            for cp in copies(refs[:k], refs[k], refs[k + 1]):
                cp.start()
            refs[-1][...] = jnp.zeros_like(refs[-1])

        out = pl.pallas_call(
            body, name=tag + "_start",
            out_shape=(pltpu.SemaphoreType.DMA((n_copies,)), pltpu.SemaphoreType.DMA((n_copies,)),
                       *[pltpu.HBM(a.shape, a.dtype) for a in arrays], jax.ShapeDtypeStruct((8, 128), F32)),
            in_specs=[HBM] * k, out_specs=(SEM, SEM, *[HBM] * k, pl.BlockSpec(memory_space=pltpu.VMEM)),
            input_output_aliases={i: 2 + i for i in range(k)},
            compiler_params=pltpu.CompilerParams(has_side_effects=EFFECT),
        )(*[pltpu.with_memory_space_constraint(a, pltpu.HBM) for a in arrays])
        self.send, self.recv, self.arrays, self.token = out[0], out[1], list(out[2:2 + k]), out[-1][0, 0]

    def wait(self, after):
        k, copies = len(self.arrays), self.copies

        def body(*refs):
            for cp in copies(refs[:k], refs[k], refs[k + 1]):
                cp.wait_send()
                cp.wait_recv()

        return list(pl.pallas_call(
            body, name=self.tag + "_wait", out_shape=tuple(pltpu.HBM(a.shape, a.dtype) for a in self.arrays),
            in_specs=[HBM] * k + [SEM, SEM, ANY], out_specs=tuple([HBM] * k),
            input_output_aliases={i: i for i in range(k)},
            compiler_params=pltpu.CompilerParams(has_side_effects=EFFECT),
        )(*self.arrays, self.send, self.recv, after))


def _gather_start(arrs, tag):
    n = len(arrs)
    me = 2 * lax.axis_index("x") + lax.axis_index("y")
    lands = [lax.dynamic_update_index_in_dim(lax.empty((4,) + a.shape, a.dtype), a, me, 0) for a in arrs]

    def copies(refs, send, recv):
        x, y, c, chips = _place()
        return [pltpu.make_async_remote_copy(
            src_ref=refs[k], dst_ref=refs[n + k].at[2 * x + y], send_sem=send.at[3 * k + r],
            recv_sem=recv.at[3 * k + r], device_id=(px, py, c), device_id_type=MESH)
            for k in range(n) for r, (px, py) in enumerate(chips)]

    return _Split("gather_" + tag, list(arrs) + lands, copies, 3 * n)


def _gather_halves_start(arrs, tag):
    n = len(arrs)
    me = 2 * lax.axis_index("x") + lax.axis_index("y")
    lands = [lax.dynamic_update_index_in_dim(lax.empty((4,) + a.shape, a.dtype), a, me, 0) for a in arrs]

    def copies(refs, send, recv):
        x, y, c, chips = _place()
        return [pltpu.make_async_remote_copy(
            src_ref=refs[k].at[c], dst_ref=refs[n + k].at[2 * x + y, c], send_sem=send.at[3 * k + r],
            recv_sem=recv.at[3 * k + r], device_id=(px, py, c), device_id_type=MESH)
            for k in range(n) for r, (px, py) in enumerate(chips)]

    return _Split("gather_" + tag, list(arrs) + lands, copies, 3 * n)


def _gather_halves_finish(lands, tag):
    n = len(lands)

    def copies(refs, send, recv):
        x, y, c, chips = _place()
        return [pltpu.make_async_remote_copy(
            src_ref=refs[k].at[2 * px + py, c], dst_ref=refs[k].at[2 * px + py, c], send_sem=send.at[3 * k + r],
            recv_sem=recv.at[3 * k + r], device_id=(x, y, 1 - c), device_id_type=MESH)
            for k in range(n) for r, (px, py) in enumerate(chips)]

    return _Split("gather_pass_" + tag, list(lands), copies, 3 * n)


def _to_sibling_start(gbufs, tag):
    n = len(gbufs)

    def copies(refs, send, recv):
        x, y, c, _ = _place()
        return [pltpu.make_async_remote_copy(
            src_ref=refs[k].at[j, 1 - c], dst_ref=refs[n + k].at[j], send_sem=send.at[4 * k + j],
            recv_sem=recv.at[4 * k + j], device_id=(x, y, 1 - c), device_id_type=MESH)
            for k in range(n) for j in range(4)]

    lands = [lax.empty((4,) + g.shape[2:], g.dtype) for g in gbufs]
    return _Split("rs_sibling_" + tag, list(gbufs) + lands, copies, 4 * n)


def _to_chips_start(pbufs, tag):
    n = len(pbufs)

    def copies(refs, send, recv):
        x, y, c, chips = _place()
        return [pltpu.make_async_remote_copy(
            src_ref=refs[k].at[2 * px + py], dst_ref=refs[n + k].at[2 * x + y], send_sem=send.at[3 * k + r],
            recv_sem=recv.at[3 * k + r], device_id=(px, py, c), device_id_type=MESH)
            for k in range(n) for r, (px, py) in enumerate(chips)]

    return _Split("rs_chips_" + tag, list(pbufs) + [lax.empty(p.shape, p.dtype) for p in pbufs], copies, 3 * n)


def _join_start(fulls, tag):
    def copies(refs, send, recv):
        x, y, c, _ = _place()
        return [pltpu.make_async_remote_copy(
            src_ref=refs[k].at[c], dst_ref=refs[k].at[c], send_sem=send.at[k], recv_sem=recv.at[k],
            device_id=(x, y, 1 - c), device_id_type=MESH) for k in range(len(fulls))]

    return _Split("rs_join_" + tag, list(fulls), copies, len(fulls))


def _all_reduce_small(v):
    R, C = v.shape

    def body(v_ref, o_ref, g_ref, send, recv, loc):
        x, y, c, chips = _place()
        me, sibling = (x, y, c), (x, y, 1 - c)

        def rows(px, py, pc):
            return g_ref.at[4 * px + 2 * py + pc]

        def copy(k, block, to, src=None):
            return pltpu.make_async_remote_copy(
                src_ref=rows(*block) if src is None else src, dst_ref=rows(*block),
                send_sem=send.at[k], recv_sem=recv.at[k], device_id=to, device_id_type=MESH)

        mine = pltpu.make_async_copy(v_ref, rows(*me), loc)
        mine.start()
        first = [copy(0, me, sibling, src=v_ref)]
        first += [copy(1 + j, me, (*chip, c), src=v_ref) for j, chip in enumerate(chips)]
        for cp in first:
            cp.start()
        passed = [copy(4 + j, (*chip, c), sibling) for j, chip in enumerate(chips)]
        for j, chip in enumerate(chips):
            copy(1 + j, (*chip, c), me).wait_recv()
            passed[j].start()
        copy(0, sibling, me).wait_recv()
        for j, chip in enumerate(chips):
            copy(4 + j, (*chip, 1 - c), me).wait_recv()
        for cp in first + passed:
            cp.wait_send()
        mine.wait()
        acc = g_ref[0]
        for d in range(1, 8):
            acc = acc + g_ref[d]
        o_ref[...] = acc

    vm = pl.BlockSpec(memory_space=pltpu.VMEM)
    return pl.pallas_call(
        body, name="all_reduce_small", in_specs=[vm], out_specs=[vm, vm],
        out_shape=[jax.ShapeDtypeStruct((R, C), F32), jax.ShapeDtypeStruct((8, R, C), F32)],
        scratch_shapes=[pltpu.SemaphoreType.DMA((7,)), pltpu.SemaphoreType.DMA((7,)), pltpu.SemaphoreType.DMA],
    )(v)[0]


WEIGHTS = ['ffn1_norm', 'ffn1_w_gate', 'ffn1_w_up', 'ffn1_w_down', 'mix_norm', 'w_in', 'conv_w', 'conv_b', 'dt_bias',
           'a_log', 'd_skip', 'ssd_norm', 'sgu_ln_g', 'sgu_ln_b', 'sgu_w', 'sgu_b', 'w_out', 'ffn2_norm',
           'ffn2_w_gate', 'ffn2_w_up', 'ffn2_w_down', 'final_norm']
SHARDED = ['ffn1_w_gate', 'ffn1_w_up', 'ffn1_w_down', 'w_in', 'conv_w', 'w_out', 'ffn2_w_gate', 'ffn2_w_up',
           'ffn2_w_down']
SMALL = [n for n in WEIGHTS if n not in SHARDED]
GROUPS = [("ffn1", ["ffn1_w_gate", "ffn1_w_up", "ffn1_w_down"]), ("mix", ["w_in", "conv_w", "w_out"]),
          ("ffn2", ["ffn2_w_gate", "ffn2_w_up", "ffn2_w_down"])]
TRANSPOSED = ("ffn1_w_gate", "ffn1_w_up", "ffn2_w_gate", "ffn2_w_up")
DEPTH = 2


def _pack_w_in(w):
    return jnp.concatenate([w[..., 0:1152], w[..., 1536:2432], w[..., 1152:1536],
                            jnp.repeat(w[..., 2432:2438], HEAD, axis=-1), w[..., 2438:2950]], axis=-1)


def _unpack_w_in(dq, ds, du):
    return jnp.concatenate([dq, ds[:, 896:1280], ds[:, 0:896], ds[:, 1280::HEAD], du], axis=-1)


def _ffn_fwd(x, g, wg, wu, wd):
    xo, hb, S1, S2, A = _ffn_fwd_k(x, g, wg, wu, wd)
    return xo, (x, hb, S1, S2, A)


def _ffn_bwd(dxo, saved, g, wg, wu, wd):
    x, hb, S1, S2, A = saved
    dx, dg, dG, dU, dyb = _ffn_bwd_k1(dxo, x, g, S1, S2, wg, wu, wd)
    dwg, dwu, dwd = _ffn_bwd_k2(hb, dyb, A, dG, dU)
    return dx, dg, dwg, dwu, dwd


def _mix_fwd(x, P):
    hb = _rms_fwd(x, P["mix_norm"])
    qkv = _mm_nn(hb, P["w_qkv"])
    sin = _mm_nn(hb, P["w_ssd"])
    uv = _mm_nn(hb, P["w_uv"])
    y_att, lse = _attn_combine([_attn_fwd(qkv, d) for d in DILATIONS])
    y_ssd, hprev = _ssd_fwd(sin, *P["ssd"])
    y_sgu = _sgu_fwd(uv, *P["sgu"])
    ycat = jnp.concatenate([y_att, y_ssd, y_sgu], axis=1).astype(BF)
    return _mm_nn(ycat, P["w_out"], res=x), (x, hb, qkv, sin, uv, y_att, lse, hprev, ycat)


def _mix_bwd(dxo, saved, P):
    x, hb, qkv, sin, uv, y_att, lse, hprev, ycat = saved
    dycat = _mm_nt(dxo, P["w_out"])
    dwout = _mm_tn(ycat, dxo)
    dy_att, dy_ssd, dy_sgu = dycat[:, 0:ATT_W], dycat[:, ATT_W:ATT_W + SSD_W], dycat[:, ATT_W + SSD_W:]
    dqkv = _sum_branches([_attn_bwd(qkv, dy_att, y_att, lse, d) for d in DILATIONS])
    dsin, dcw, dcb, dvec = _ssd_bwd(sin, hprev, dy_ssd, *P["ssd"])
    duv, dsw, dsbias, dln = _sgu_bwd(uv, dy_sgu, *P["sgu"])
    dwin = _unpack_w_in(_mm_tn(hb, dqkv), _mm_tn(hb, dsin), _mm_tn(hb, duv))
    dh = _mm_nt(dqkv, P["w_qkv"])
    dh = _mm_nt(dsin, P["w_ssd"], res=dh)
    dh = _mm_nt(duv, P["w_uv"], res=dh)
    dx, dg = _rms_bwd(x, P["mix_norm"], dh, dxo)
    grads = dict(
        mix_norm=dg[0], w_in=dwin, conv_w=dcw, conv_b=dcb[0], dt_bias=dvec[0, ::HEAD], a_log=dvec[1, ::HEAD],
        d_skip=jnp.sum(dvec[2].reshape(6, HEAD), axis=-1), ssd_norm=dvec[3], sgu_ln_g=dln[0], sgu_ln_b=dln[1],
        sgu_w=dsw, sgu_b=jnp.sum(dsbias.reshape(CHUNK, 4, HEAD), axis=-1).T, w_out=dwout)
    return dx, grads


def _halved(g):
    rows = g.size // g.shape[-1]
    return g.reshape(4, 2, rows // 8, g.shape[-1])


def kernel(x, ffn1_norm, ffn1_w_gate, ffn1_w_up, ffn1_w_down, mix_norm, w_in, conv_w, conv_b, dt_bias, a_log, d_skip, ssd_norm, sgu_ln_g, sgu_ln_b, sgu_w, sgu_b, w_out, ffn2_norm, ffn2_w_gate, ffn2_w_up, ffn2_w_down, final_norm, loss_target, m_ffn1_norm, m_ffn1_w_gate, m_ffn1_w_up, m_ffn1_w_down, m_mix_norm, m_w_in, m_conv_w, m_conv_b, m_dt_bias, m_a_log, m_d_skip, m_ssd_norm, m_sgu_ln_g, m_sgu_ln_b, m_sgu_w, m_sgu_b, m_w_out, m_ffn2_norm, m_ffn2_w_gate, m_ffn2_w_up, m_ffn2_w_down, m_final_norm, v_ffn1_norm, v_ffn1_w_gate, v_ffn1_w_up, v_ffn1_w_down, v_mix_norm, v_w_in, v_conv_w, v_conv_b, v_dt_bias, v_a_log, v_d_skip, v_ssd_norm, v_sgu_ln_g, v_sgu_ln_b, v_sgu_w, v_sgu_b, v_w_out, v_ffn2_norm, v_ffn2_w_gate, v_ffn2_w_up, v_ffn2_w_down, v_final_norm):
    given = dict(x=x, ffn1_norm=ffn1_norm, ffn1_w_gate=ffn1_w_gate, ffn1_w_up=ffn1_w_up, ffn1_w_down=ffn1_w_down, mix_norm=mix_norm, w_in=w_in, conv_w=conv_w, conv_b=conv_b, dt_bias=dt_bias, a_log=a_log, d_skip=d_skip, ssd_norm=ssd_norm, sgu_ln_g=sgu_ln_g, sgu_ln_b=sgu_ln_b, sgu_w=sgu_w, sgu_b=sgu_b, w_out=w_out, ffn2_norm=ffn2_norm, ffn2_w_gate=ffn2_w_gate, ffn2_w_up=ffn2_w_up, ffn2_w_down=ffn2_w_down, final_norm=final_norm, loss_target=loss_target, m_ffn1_norm=m_ffn1_norm, m_ffn1_w_gate=m_ffn1_w_gate, m_ffn1_w_up=m_ffn1_w_up, m_ffn1_w_down=m_ffn1_w_down, m_mix_norm=m_mix_norm, m_w_in=m_w_in, m_conv_w=m_conv_w, m_conv_b=m_conv_b, m_dt_bias=m_dt_bias, m_a_log=m_a_log, m_d_skip=m_d_skip, m_ssd_norm=m_ssd_norm, m_sgu_ln_g=m_sgu_ln_g, m_sgu_ln_b=m_sgu_ln_b, m_sgu_w=m_sgu_w, m_sgu_b=m_sgu_b, m_w_out=m_w_out, m_ffn2_norm=m_ffn2_norm, m_ffn2_w_gate=m_ffn2_w_gate, m_ffn2_w_up=m_ffn2_w_up, m_ffn2_w_down=m_ffn2_w_down, m_final_norm=m_final_norm, v_ffn1_norm=v_ffn1_norm, v_ffn1_w_gate=v_ffn1_w_gate, v_ffn1_w_up=v_ffn1_w_up, v_ffn1_w_down=v_ffn1_w_down, v_mix_norm=v_mix_norm, v_w_in=v_w_in, v_conv_w=v_conv_w, v_conv_b=v_conv_b, v_dt_bias=v_dt_bias, v_a_log=v_a_log, v_d_skip=v_d_skip, v_ssd_norm=v_ssd_norm, v_sgu_ln_g=v_sgu_ln_g, v_sgu_ln_b=v_sgu_ln_b, v_sgu_w=v_sgu_w, v_sgu_b=v_sgu_b, v_w_out=v_w_out, v_ffn2_norm=v_ffn2_norm, v_ffn2_w_gate=v_ffn2_w_gate, v_ffn2_w_up=v_ffn2_w_up, v_ffn2_w_down=v_ffn2_w_down, v_final_norm=v_final_norm)
    T = given["x"].shape[0] * given["x"].shape[1]
    D = given["x"].shape[2]
    x0 = given["x"].reshape(T, D)
    tgt = given["loss_target"].reshape(T, D)
    c = lax.axis_index("c")

    bf = {n: given[n].astype(BF) for n in SHARDED if n not in ("w_in", "conv_w")}
    bf["w_in"] = _pack_w_in(given["w_in"]).astype(BF)
    bf["conv_w"] = given["conv_w"]
    first = [bf[n][0].reshape((2, bf[n].shape[1] // 2) + bf[n].shape[2:]) for n in GROUPS[0][1]]
    gathers = {(0, GROUPS[0][0]): _gather_halves_start(first, "l0_" + GROUPS[0][0])}
    gathers.update({(i, gname): _gather_start([bf[n][i] for n in names], f"l{i}_{gname}")
                    for i in range(DEPTH) for gname, names in GROUPS if (i, gname) not in gathers})
    token = functools.reduce(lambda a, b: a + b, [g.token for g in gathers.values()])

    def gathered(i, gname, after):
        got = gathers[(i, gname)].wait(after)[3:]
        if (i, gname) == (0, GROUPS[0][0]):
            got = _gather_halves_finish(got, "l0_" + gname).wait(after)
            got = [z.reshape((4, 2 * z.shape[2]) + z.shape[3:]) for z in got]
        return got

    def mix_params(i, got):
        win = got[0].reshape(D, W_QKV + W_SSD + W_UV)
        rep = lambda v: jnp.repeat(v, HEAD)[None]
        ssd = (got[1].transpose(1, 0, 2).reshape(4, SSD_CONV_DIM), given["conv_b"][i][None],
               rep(given["dt_bias"][i]), rep(given["a_log"][i]), rep(given["d_skip"][i]), given["ssd_norm"][i][None])
        sgu = (given["sgu_ln_g"][i][None], given["sgu_ln_b"][i][None], given["sgu_w"][i],
               jnp.repeat(given["sgu_b"][i].T, HEAD, axis=1))
        return dict(mix_norm=given["mix_norm"][i][None], w_qkv=win[:, 0:W_QKV], w_ssd=win[:, W_QKV:W_QKV + W_SSD],
                    w_uv=win[:, W_QKV + W_SSD:], w_out=got[2].reshape(-1, D), ssd=ssd, sgu=sgu)

    x = x0
    tape = []
    for i in range(DEPTH):
        P = dict(ffn1=(given["ffn1_norm"][i][None] + (token if i == 0 else 0.0), *gathered(i, "ffn1", x)))
        x, s1 = _ffn_fwd(x, *P["ffn1"])
        P.update(mix_params(i, gathered(i, "mix", x)))
        x, s2 = _mix_fwd(x, P)
        P["ffn2"] = (given["ffn2_norm"][i][None], *gathered(i, "ffn2", x))
        x, s3 = _ffn_fwd(x, *P["ffn2"])
        tape.append((P, s1, s2, s3))
    loss_part, dx, dgf = _final_loss(x, given["final_norm"][None], tgt)

    me = 2 * lax.axis_index("x") + lax.axis_index("y")
    jobs = []

    def rs_begin(i, gname, gd):
        tag = f"l{i}_{gname}"
        names = [n for n in dict(GROUPS)[gname] if n != "conv_w"]
        jobs.append(dict(key=(i, gname), names=names, tag=tag, stage=1,
                         op=_to_sibling_start([_halved(gd[n]) for n in names], tag)))

    def rs_advance(job, after):
        k = len(job["names"])
        if job["stage"] == 1:
            got = job["op"].wait(after)
            job.update(stage=2, op=_to_chips_start([_pair_add(g, l, c) for g, l in zip(got[:k], got[k:])], job["tag"]))
        elif job["stage"] == 2:
            got = job["op"].wait(after)
            job.update(stage=3, op=_join_start([_chip_sum(p, l, me, c) for p, l in zip(got[:k], got[k:])], job["tag"]))
        elif job["stage"] == 3:
            job.update(stage=4, out=dict(zip(job["names"], job["op"].wait(after))))

    def tick(after, begin=None):
        for job in jobs:
            rs_advance(job, after)
        if begin is not None:
            rs_begin(*begin)
        return functools.reduce(lambda a, b: a + b, [j["op"].token for j in jobs if j["stage"] < 4], 0.0)

    grads = [dict() for _ in range(DEPTH)]
    tok = 0.0
    for i in reversed(range(DEPTH)):
        P, s1, s2, s3 = tape[i]
        g = grads[i]
        norm, wg, wu, wd = P["ffn2"]
        dx, dn2, g["ffn2_w_gate"], g["ffn2_w_up"], g["ffn2_w_down"] = _ffn_bwd(dx, s3, norm + tok, wg, wu, wd)
        tok = tick(dx, (i, "ffn2", g))
        dx, gm = _mix_bwd(dx, s2, {**P, "mix_norm": P["mix_norm"] + tok})
        g.update(gm)
        tok = tick(dx, (i, "mix", g))
        norm, wg, wu, wd = P["ffn1"]
        dx, dn1, g["ffn1_w_gate"], g["ffn1_w_up"], g["ffn1_w_down"] = _ffn_bwd(dx, s1, norm + tok, wg, wu, wd)
        tok = tick(dx, (i, "ffn1", g))
        g["ffn1_norm"], g["ffn2_norm"] = dn1[0], dn2[0]
    grad_x = dx.reshape(given["x"].shape)

    order = [n for n in SMALL if n != "final_norm"] + ["final_norm"]
    small = [jnp.stack([grads[i][n] for i in range(DEPTH)]) for n in order[:-1] + ["conv_w"]]
    small = small[:-1] + [dgf[0], small[-1], loss_part[0, 0:1]]
    n_small = sum(s.size for s in small)
    rows_small = -(-n_small // (128 * 8)) * 8

    def flat(arrs):
        fill = rows_small * 128 - sum(a.size for a in arrs)
        return jnp.concatenate([a.reshape(-1) for a in arrs] + [jnp.zeros((fill,), F32)]).reshape(rows_small, 128)

    gsmall = _all_reduce_small(flat(small)).reshape(-1)

    grad_w = {}
    off = 0
    for n in order:
        size = given[n].size
        grad_w[n] = gsmall[off:off + size].reshape(given[n].shape)
        off += size
    cw = gsmall[off:off + 2 * 4 * SSD_CONV_DIM].reshape(DEPTH, 4, SSD_CONV_DIM)
    grad_w["conv_w"] = lax.dynamic_slice_in_dim(cw, me * (SSD_CONV_DIM // 4), SSD_CONV_DIM // 4, axis=2)
    loss = gsmall[off + 2 * 4 * SSD_CONV_DIM]

    delta, new_m, new_v = {}, {}, {}
    shp = given["conv_w"].shape
    d, m2, v2 = _adamw(*[a.reshape(shp[0] * shp[1], shp[2])
                         for a in (given["conv_w"], grad_w["conv_w"], given["m_conv_w"], given["v_conv_w"])])
    delta["conv_w"], new_m["conv_w"], new_v["conv_w"] = d.reshape(shp), m2.reshape(shp), v2.reshape(shp)
    packed = [flat([given[pre + n] for n in order]) for pre in ("", "m_", "v_")]
    small_out = _adamw(packed[0], gsmall.reshape(rows_small, 128), packed[1], packed[2])
    outs = [o.reshape(-1) for o in small_out]
    off = 0
    for n in order:
        size = given[n].size
        for dst, o in zip((delta, new_m, new_v), outs):
            dst[n] = o[off:off + size].reshape(given[n].shape)
        off += size

    stepped, arrived = {}, {}

    def update_arrived(dep):
        out = None
        for job in jobs:
            if job["stage"] == 4 and not job.get("seen"):
                job["seen"] = True
                for n, full in job["out"].items():
                    view = (lambda a: jnp.swapaxes(a, 1, 2)) if n in TRANSPOSED else (lambda a: a)
                    arrived.setdefault(n, {})[job["key"][0]] = full.reshape(view(given[n]).shape[1:])
                    if len(arrived[n]) == DEPTH:
                        res = _adamw_pair(view(given[n]), arrived[n][0], arrived[n][1], view(given["m_" + n]),
                                          view(given["v_" + n]), dep)
                        stepped[n] = [view(r) for r in res]
                        out = res[0]
        return out

    after = small_out[0]
    while any(j["stage"] < 4 for j in jobs):
        done = update_arrived(jnp.zeros((8, 128), F32) + tok)
        after = after if done is None else done
        tok = tick(after)
    update_arrived(jnp.zeros((8, 128), F32) + tok)
    for n, (d, m2, v2, g) in stepped.items():
        delta[n], new_m[n], new_v[n], grad_w[n] = d, m2, v2, g

    return (loss, grad_x, *[grad_w[n] for n in WEIGHTS], *[delta[n] for n in WEIGHTS],
            *[new_m[n] for n in WEIGHTS], *[new_v[n] for n in WEIGHTS])
```

```python
import functools
import math

import jax
import jax.numpy as jnp
from jax import lax
from jax.experimental import pallas as pl
from jax.experimental.pallas import tpu as pltpu

F32 = jnp.float32
BF = jnp.bfloat16

RMS_EPS = 1e-6
LN_EPS = 1e-5
SEQ = 2048
CHUNK = 128
N_CHUNK = SEQ // CHUNK
ATT_W = 384
HEAD = 64
SSD_W = 384
SSD_CONV_DIM = 896
SSD_STATE = 128
SGU_W = 256
DILATIONS = (1, 4, 16)
W_QKV = 3 * ATT_W
W_SSD = SSD_CONV_DIM + SSD_W + SSD_W
W_UV = 2 * SGU_W
ADAM_LR = 0.001
ADAM_B1 = 0.9
ADAM_B2 = 0.999
ADAM_EPS = 1e-08
ADAM_WD = 0.01
ADAM_STEP = 10
NEG = -1e30
ATTN_BWD_VMEM = 48 * 2 ** 20


def _dot(a, b):
    return jnp.dot(a, b, preferred_element_type=F32)


def _dot_nt(a, b):
    return lax.dot_general(a, b, (((1,), (1,)), ((), ())), preferred_element_type=F32)


def _dot_tn(a, b):
    return lax.dot_general(a, b, (((0,), (0,)), ((), ())), preferred_element_type=F32)


def _sigmoid(x):
    return 1.0 / (1.0 + jnp.exp(-x))


def _call(body, *, name, grid, in_specs, out_specs, out_shape, scratch=(), sem=None, vmem=None):
    return pl.pallas_call(
        body, name=name, grid=grid, in_specs=in_specs, out_specs=out_specs, out_shape=out_shape,
        scratch_shapes=list(scratch),
        compiler_params=pltpu.CompilerParams(dimension_semantics=sem, vmem_limit_bytes=vmem),
    )


def _tile(n, want):
    t = min(n, want)
    while n % t:
        t //= 2
    return t


def _final_loss(x, g, tgt):
    T, D = x.shape
    tm = _tile(T, 512)

    def body(x_ref, g_ref, t_ref, l_ref, dx_ref, dg_ref):
        @pl.when(pl.program_id(0) == 0)
        def _():
            dg_ref[...] = jnp.zeros_like(dg_ref)
            l_ref[...] = jnp.zeros_like(l_ref)

        xf = x_ref[...]
        gg = g_ref[...]
        r = lax.rsqrt(jnp.mean(xf * xf, axis=-1, keepdims=True) + RMS_EPS)
        xn = xf * r
        e = xn * gg - t_ref[...]
        part = 0.5 * jnp.sum(jnp.mean(e * e, axis=-1, keepdims=True), axis=0, keepdims=True)
        l_ref[...] += jnp.broadcast_to(part, l_ref.shape)
        dy = e * (1.0 / D)
        u = dy * gg
        mu = jnp.mean(u * xf, axis=-1, keepdims=True)
        dx_ref[...] = r * (u - xf * (r * r * mu))
        dg_ref[...] += jnp.sum(dy * xn, axis=0, keepdims=True)

    row = pl.BlockSpec((tm, D), lambda i: (i, 0))
    vec = pl.BlockSpec((1, D), lambda i: (0, 0))
    lsp = pl.BlockSpec((1, 128), lambda i: (0, 0))
    return _call(body, name="final_loss", grid=(T // tm,), in_specs=[row, vec, row], out_specs=[lsp, row, vec],
                 out_shape=[jax.ShapeDtypeStruct((1, 128), F32), jax.ShapeDtypeStruct((T, D), F32),
                            jax.ShapeDtypeStruct((1, D), F32)],
                 sem=("arbitrary",))(x, g, tgt)


def _slabs(tm, n=2):
    return [slice(k * tm // n, (k + 1) * tm // n) for k in range(n)] if tm % (16 * n) == 0 else [slice(0, tm)]


def _resident(shape):
    return pl.BlockSpec(shape, lambda *_: (0,) * len(shape), pipeline_mode=pl.Buffered(1))


def _ffn_fwd_k(x, gn, wg, wu, wd):
    T, D = x.shape
    NS, _, Fs = wg.shape
    tm = _tile(T, 512)

    def body(x_ref, gn_ref, wg_ref, wu_ref, wd_ref, o_ref, h_ref, s1_ref, s2_ref, a_ref, hs, acc):
        j = pl.program_id(1)

        @pl.when(j == 0)
        def _():
            xf = x_ref[...]
            r = lax.rsqrt(jnp.mean(xf * xf, axis=-1, keepdims=True) + RMS_EPS)
            hs[...] = (xf * r * gn_ref[...]).astype(BF)
            h_ref[...] = hs[...]
            acc[...] = jnp.zeros_like(acc)

        h = hs[...]
        g = _dot(h, wg_ref[j])
        u = _dot(h, wu_ref[j])
        sg = _sigmoid(g)
        s1 = g * sg
        a = (s1 * u).astype(BF)
        s1_ref[...] = s1.astype(BF)
        s2_ref[...] = (u * (sg * (1.0 + g * (1.0 - sg)))).astype(BF)
        a_ref[...] = a
        acc[...] += _dot(a, wd_ref[j])

        @pl.when(j == NS - 1)
        def _():
            o_ref[...] = x_ref[...] + 0.5 * acc[...]

    row = pl.BlockSpec((tm, D), lambda i, j: (i, 0))
    act = pl.BlockSpec((None, tm, Fs), lambda i, j: (j, i, 0))
    sh = jax.ShapeDtypeStruct((NS, T, Fs), BF)
    return _call(body, name="ffn_fwd", grid=(T // tm, NS),
                 in_specs=[row, pl.BlockSpec((1, D), lambda i, j: (0, 0)), _resident(wg.shape), _resident(wu.shape),
                           _resident(wd.shape)],
                 out_specs=[row, row, act, act, act],
                 out_shape=[jax.ShapeDtypeStruct((T, D), F32), jax.ShapeDtypeStruct((T, D), BF), sh, sh, sh],
                 scratch=[pltpu.VMEM((tm, D), BF), pltpu.VMEM((tm, D), F32)],
                 sem=("parallel", "arbitrary"))(x, gn, wg, wu, wd)


def _ffn_bwd_k1(dxo, x, gn, s1, s2, wg, wu, wd):
    NS, T, Fs = s1.shape
    D = x.shape[1]
    tm = _tile(T, 512)

    def body(dxo_ref, x_ref, gn_ref, s1_ref, s2_ref, wg_ref, wu_ref, wd_ref,
             dx_ref, dgn_ref, dg_ref, du_ref, dy_ref, dys, acc):
        i, j = pl.program_id(0), pl.program_id(1)

        @pl.when((i == 0) & (j == 0))
        def _():
            dgn_ref[...] = jnp.zeros_like(dgn_ref)

        @pl.when(j == 0)
        def _():
            dys[...] = (0.5 * dxo_ref[...]).astype(BF)
            dy_ref[...] = dys[...]
            acc[...] = jnp.zeros_like(acc)

        for rows in _slabs(tm):
            da = _dot_nt(dys[rows, :], wd_ref[j])
            dg = (da * s2_ref[rows, :].astype(F32)).astype(BF)
            du = (da * s1_ref[rows, :].astype(F32)).astype(BF)
            dg_ref[rows, :] = dg
            du_ref[rows, :] = du
            acc[rows, :] += _dot_nt(dg, wg_ref[j]) + _dot_nt(du, wu_ref[j])

        @pl.when(j == NS - 1)
        def _():
            xf = x_ref[...]
            r = lax.rsqrt(jnp.mean(xf * xf, axis=-1, keepdims=True) + RMS_EPS)
            dh = acc[...]
            uu = dh * gn_ref[...]
            mu = jnp.mean(uu * xf, axis=-1, keepdims=True)
            dx_ref[...] = dxo_ref[...] + r * (uu - xf * (r * r * mu))
            dgn_ref[...] += jnp.sum(dh * xf * r, axis=0, keepdims=True)

    row = pl.BlockSpec((tm, D), lambda i, j: (i, 0))
    vec = pl.BlockSpec((1, D), lambda i, j: (0, 0))
    act = pl.BlockSpec((None, tm, Fs), lambda i, j: (j, i, 0))
    sh = jax.ShapeDtypeStruct((NS, T, Fs), BF)
    return _call(body, name="ffn_bwd_x", grid=(T // tm, NS),
                 in_specs=[row, row, vec, act, act, _resident(wg.shape), _resident(wu.shape), _resident(wd.shape)],
                 out_specs=[row, vec, act, act, row],
                 out_shape=[jax.ShapeDtypeStruct((T, D), F32), jax.ShapeDtypeStruct((1, D), F32), sh, sh,
                            jax.ShapeDtypeStruct((T, D), BF)],
                 scratch=[pltpu.VMEM((tm, D), BF), pltpu.VMEM((tm, D), F32)],
                 sem=("arbitrary", "arbitrary"))(dxo, x, gn, s1, s2, wg, wu, wd)


def _ffn_bwd_k2(hb, dyb, a, dg, du):
    NS, T, Fs = a.shape
    D = hb.shape[1]
    tk = _tile(T, 1024)

    def body(h_ref, dy_ref, a_ref, dg_ref, du_ref, og_ref, ou_ref, od_ref):
        @pl.when(pl.program_id(1) == 0)
        def _():
            og_ref[...] = jnp.zeros_like(og_ref)
            ou_ref[...] = jnp.zeros_like(ou_ref)
            od_ref[...] = jnp.zeros_like(od_ref)

        h = h_ref[...]
        og_ref[...] += _dot_tn(dg_ref[...], h)
        ou_ref[...] += _dot_tn(du_ref[...], h)
        od_ref[...] += _dot_tn(a_ref[...], dy_ref[...])

    row = pl.BlockSpec((tk, D), lambda j, k: (k, 0))
    act = pl.BlockSpec((None, tk, Fs), lambda j, k: (j, k, 0))
    return _call(body, name="ffn_bwd_w", grid=(NS, T // tk), in_specs=[row, row, act, act, act],
                 out_specs=[pl.BlockSpec((None, Fs, D), lambda j, k: (j, 0, 0))] * 3,
                 out_shape=[jax.ShapeDtypeStruct((NS, Fs, D), F32)] * 3,
                 sem=("parallel", "arbitrary"))(hb, dyb, a, dg, du)


def _mm_nn(a, b, res=None, out_dtype=F32):
    T, K = a.shape
    N = b.shape[1]
    tm = _tile(T, 512)
    tn = N if N <= 2048 else _tile(N, 1024)

    def body(*refs):
        if res is None:
            a_ref, b_ref, o_ref = refs
            o_ref[...] = _dot(a_ref[...], b_ref[...]).astype(out_dtype)
        else:
            a_ref, b_ref, r_ref, o_ref = refs
            o_ref[...] = (r_ref[...] + _dot(a_ref[...], b_ref[...])).astype(out_dtype)

    o = pl.BlockSpec((tm, tn), lambda i, j: (i, j))
    ins = [pl.BlockSpec((tm, K), lambda i, j: (i, 0)), pl.BlockSpec((K, tn), lambda i, j: (0, j))]
    args = [a, b]
    if res is not None:
        ins.append(o)
        args.append(res)
    return _call(body, name="mm_nn", grid=(T // tm, N // tn), in_specs=ins, out_specs=o,
                 out_shape=jax.ShapeDtypeStruct((T, N), out_dtype), sem=("parallel", "parallel"))(*args)


def _mm_nt(a, b, res=None):
    T, K = a.shape
    N = b.shape[0]
    tm = _tile(T, 512)

    def body(*refs):
        if res is None:
            a_ref, b_ref, o_ref = refs
            o_ref[...] = _dot_nt(a_ref[...].astype(BF), b_ref[...])
        else:
            a_ref, b_ref, r_ref, o_ref = refs
            o_ref[...] = r_ref[...] + _dot_nt(a_ref[...].astype(BF), b_ref[...])

    o = pl.BlockSpec((tm, N), lambda i: (i, 0))
    ins = [pl.BlockSpec((tm, K), lambda i: (i, 0)), pl.BlockSpec((N, K), lambda i: (0, 0))]
    args = [a, b]
    if res is not None:
        ins.append(o)
        args.append(res)
    return _call(body, name="mm_nt", grid=(T // tm,), in_specs=ins, out_specs=o,
                 out_shape=jax.ShapeDtypeStruct((T, N), F32), sem=("parallel",))(*args)


def _mm_tn(a, b):
    T, M = a.shape
    N = b.shape[1]
    tk = _tile(T, 1024)
    tmm = _tile(M, 512)

    def body(a_ref, b_ref, o_ref):
        @pl.when(pl.program_id(1) == 0)
        def _():
            o_ref[...] = jnp.zeros_like(o_ref)

        o_ref[...] += _dot_tn(a_ref[...].astype(BF), b_ref[...].astype(BF))

    return _call(body, name="mm_tn", grid=(M // tmm, T // tk),
                 in_specs=[pl.BlockSpec((tk, tmm), lambda i, k: (k, i)), pl.BlockSpec((tk, N), lambda i, k: (k, 0))],
                 out_specs=pl.BlockSpec((tmm, N), lambda i, k: (i, 0)),
                 out_shape=jax.ShapeDtypeStruct((M, N), F32), sem=("parallel", "arbitrary"))(a, b)


def _mix_proj(x, gn, win):
    T, D = x.shape
    tm = _tile(T, 512)
    cuts = (0, W_QKV, W_QKV + W_SSD, W_QKV + W_SSD + W_UV)

    def body(x_ref, gn_ref, w_ref, h_ref, q_ref, s_ref, u_ref):
        xf = x_ref[...]
        r = lax.rsqrt(jnp.mean(xf * xf, axis=-1, keepdims=True) + RMS_EPS)
        h = (xf * r * gn_ref[...]).astype(BF)
        h_ref[...] = h
        for o_ref, lo, hi in zip((q_ref, s_ref, u_ref), cuts[:-1], cuts[1:]):
            o_ref[...] = _dot(h, w_ref[:, lo:hi])

    row = lambda w: pl.BlockSpec((tm, w), lambda i: (i, 0))
    return _call(body, name="mix_proj", grid=(T // tm,),
                 in_specs=[row(D), pl.BlockSpec((1, D), lambda i: (0, 0)), _resident(win.shape)],
                 out_specs=[row(D), row(W_QKV), row(W_SSD), row(W_UV)],
                 out_shape=[jax.ShapeDtypeStruct((T, D), BF), jax.ShapeDtypeStruct((T, W_QKV), F32),
                            jax.ShapeDtypeStruct((T, W_SSD), F32), jax.ShapeDtypeStruct((T, W_UV), F32)],
                 sem=("parallel",))(x, gn, win)


def _mix_bwd_dx(dqkv, dsin, duv, win, x, gn, dxo):
    T, D = x.shape
    tm = _tile(T, 512)
    cuts = (0, W_QKV, W_QKV + W_SSD, W_QKV + W_SSD + W_UV)

    def body(dq_ref, ds_ref, du_ref, w_ref, x_ref, gn_ref, dxo_ref, dx_ref, dgn_ref):
        @pl.when(pl.program_id(0) == 0)
        def _():
            dgn_ref[...] = jnp.zeros_like(dgn_ref)

        dh = (_dot_nt(dq_ref[...], w_ref[:, cuts[0]:cuts[1]]) + _dot_nt(ds_ref[...], w_ref[:, cuts[1]:cuts[2]])
              + _dot_nt(du_ref[...], w_ref[:, cuts[2]:cuts[3]]))
        xf = x_ref[...]
        r = lax.rsqrt(jnp.mean(xf * xf, axis=-1, keepdims=True) + RMS_EPS)
        uu = dh * gn_ref[...]
        mu = jnp.mean(uu * xf, axis=-1, keepdims=True)
        dx_ref[...] = dxo_ref[...] + r * (uu - xf * (r * r * mu))
        dgn_ref[...] += jnp.sum(dh * xf * r, axis=0, keepdims=True)

    row = lambda w: pl.BlockSpec((tm, w), lambda i: (i, 0))
    vec = pl.BlockSpec((1, D), lambda i: (0, 0))
    return _call(body, name="mix_bwd_dx", grid=(T // tm,),
                 in_specs=[row(W_QKV), row(W_SSD), row(W_UV), _resident(win.shape), row(D), vec, row(D)],
                 out_specs=[row(D), vec],
                 out_shape=[jax.ShapeDtypeStruct((T, D), F32), jax.ShapeDtypeStruct((1, D), F32)],
                 sem=("arbitrary",))(dqkv, dsin, duv, win, x, gn, dxo)


def _lane_mask(e, width=128):
    return (lax.broadcasted_iota(jnp.int32, (1, width), 1) // HEAD) == e


def _band_mask(n):
    qi = lax.broadcasted_iota(jnp.int32, (CHUNK, 2 * CHUNK), 0)
    kj = lax.broadcasted_iota(jnp.int32, (CHUNK, 2 * CHUNK), 1)
    dist = qi + CHUNK - kj
    return (dist >= 0) & (dist <= CHUNK) & ((kj >= CHUNK) | (n > 0))


def _sub_rows(r, block, dil):
    if dil == 1:
        return pl.ds(pl.multiple_of(block * CHUNK, CHUNK), CHUNK)
    return pl.ds(r + dil * CHUNK * block, CHUNK, stride=dil)


def _attn_specs(T, dil):
    B, nb = T // SEQ, SEQ // (CHUNK * dil)
    once = dict(pipeline_mode=pl.Buffered(1))
    q_like = lambda col: pl.BlockSpec((CHUNK * dil, 128), lambda b, n, r: (b * nb + n, col), **(once if nb == 1 else {}))
    k_like = lambda col: pl.BlockSpec((SEQ, 128), lambda b, n, r: (b, col), **once)
    return B, nb, q_like, k_like


def _attn_fwd(qkv, dil):
    T = qkv.shape[0]
    B, nb, q_like, k_like = _attn_specs(T, dil)
    scale = HEAD ** -0.5

    def body(*refs):
        q_t, k_t, v_t, o_t, l_t = refs[0:3], refs[3:6], refs[6:9], refs[9:12], refs[12:15]
        n, r = pl.program_id(1), pl.program_id(2)
        mine = _sub_rows(r, 0, dil)
        cur, prv = _sub_rows(r, n, dil), _sub_rows(r, jnp.maximum(n - 1, 0), dil)
        mask = _band_mask(n)
        for t in range(3):
            qt = q_t[t][mine, :].astype(BF)
            kt = jnp.concatenate([k_t[t][prv, :], k_t[t][cur, :]], axis=0).astype(BF)
            vt = jnp.concatenate([v_t[t][prv, :], v_t[t][cur, :]], axis=0).astype(BF)
            o_pair = jnp.zeros((CHUNK, 128), F32)
            l_pair = jnp.zeros((CHUNK, 128), F32)
            for e in range(2):
                lm = _lane_mask(e)
                s = _dot_nt(jnp.where(lm, qt, jnp.zeros_like(qt)), kt) * scale
                s = jnp.where(mask, s, NEG)
                m = jnp.max(s, axis=-1, keepdims=True)
                p = jnp.exp(s - m)
                den = jnp.sum(p, axis=-1, keepdims=True)
                o = _dot(p.astype(BF), vt) / den
                o_pair = jnp.where(lm, o, o_pair)
                l_pair = jnp.where(lm, m + jnp.log(den), l_pair)
            o_t[t][mine, :] = o_pair
            l_t[t][mine, :] = l_pair

    out_spec = pl.BlockSpec((CHUNK * dil, 128), lambda b, n, r: (b * nb + n, 0))
    sh = jax.ShapeDtypeStruct((T, 128), F32)
    outs = _call(
        body, name=f"attn_fwd_d{dil}", grid=(B, nb, dil),
        in_specs=[q_like(t) for t in range(3)] + [k_like(3 + t) for t in range(3)] + [k_like(6 + t) for t in range(3)],
        out_specs=[out_spec] * 6, out_shape=[sh] * 6, sem=("parallel", "arbitrary", "arbitrary"))(*([qkv] * 9))
    return list(outs[0:3]), list(outs[3:6])


def _attn_combine(branches):
    T = branches[0][0][0].shape[0]
    tm = _tile(T, 512)

    def body(*refs):
        y_ref, l_ref = refs[-2:]
        for t in range(3):
            o = [refs[6 * i + t][...] for i in range(3)]
            a, b, c = [refs[6 * i + 3 + t][...] for i in range(3)]
            m = jnp.maximum(jnp.maximum(a, b), c)
            ea, eb, ec = jnp.exp(a - m), jnp.exp(b - m), jnp.exp(c - m)
            z = ea + eb + ec
            y_ref[:, 128 * t:128 * (t + 1)] = (ea * o[0] + eb * o[1] + ec * o[2]) / z
            l_ref[:, 128 * t:128 * (t + 1)] = m + jnp.log(z)

    tile = pl.BlockSpec((tm, 128), lambda i: (i, 0))
    row = pl.BlockSpec((tm, ATT_W), lambda i: (i, 0))
    sh = jax.ShapeDtypeStruct((T, ATT_W), F32)
    flat = [a for o_t, l_t in branches for a in (*o_t, *l_t)]
    return _call(body, name="attn_combine", grid=(T // tm,), in_specs=[tile] * 18, out_specs=[row, row],
                 out_shape=[sh, sh], sem=("parallel",))(*flat)


def _attn_bwd(qkv, do, out, lse, dil):
    T = qkv.shape[0]
    B, nb, q_like, k_like = _attn_specs(T, dil)
    scale = HEAD ** -0.5

    def body(*refs):
        q_t, k_t, v_t = refs[0:3], refs[3:6], refs[6:9]
        do_t, out_t, lse_t = refs[9:12], refs[12:15], refs[15:18]
        dq_t, dk_t, dv_t = refs[18:21], refs[21:24], refs[24:27]
        n, r = pl.program_id(1), pl.program_id(2)

        @pl.when((n == 0) & (r == 0))
        def _():
            for t in range(3):
                dk_t[t][...] = jnp.zeros_like(dk_t[t])
                dv_t[t][...] = jnp.zeros_like(dv_t[t])

        mine = _sub_rows(r, 0, dil)
        cur, prv = _sub_rows(r, n, dil), _sub_rows(r, jnp.maximum(n - 1, 0), dil)
        mask = _band_mask(n)
        for t in range(3):
            qt = q_t[t][mine, :].astype(BF)
            kt = jnp.concatenate([k_t[t][prv, :], k_t[t][cur, :]], axis=0).astype(BF)
            vt = jnp.concatenate([v_t[t][prv, :], v_t[t][cur, :]], axis=0).astype(BF)
            do_ = do_t[t][mine, :]
            dlt = do_ * out_t[t][mine, :]
            ls = lse_t[t][mine, :]
            dq_pair = jnp.zeros((CHUNK, 128), F32)
            dk_acc = jnp.zeros((2 * CHUNK, 128), F32)
            dv_acc = jnp.zeros((2 * CHUNK, 128), F32)
            for e in range(2):
                lm = _lane_mask(e)
                qm = jnp.where(lm, qt, jnp.zeros_like(qt))
                s = _dot_nt(qm, kt) * scale
                p = jnp.exp(jnp.where(mask, s - ls[:, HEAD * e:HEAD * e + 1], NEG))
                dom = jnp.where(lm, do_, 0.0).astype(BF)
                dv_acc += _dot_tn(p.astype(BF), dom)
                dp = _dot_nt(dom, vt)
                delta = jnp.sum(jnp.where(lm, dlt, 0.0), axis=-1, keepdims=True)
                ds = (p * (dp - delta) * scale).astype(BF)
                dq_pair += jnp.where(lm, _dot(ds, kt), 0.0)
                dk_acc += _dot_tn(ds, qm)
            dq_t[t][mine, :] = dq_pair
            dk_t[t][cur, :] = dk_t[t][cur, :] + dk_acc[CHUNK:]
            dk_t[t][prv, :] = dk_t[t][prv, :] + dk_acc[:CHUNK]
            dv_t[t][cur, :] = dv_t[t][cur, :] + dv_acc[CHUNK:]
            dv_t[t][prv, :] = dv_t[t][prv, :] + dv_acc[:CHUNK]

    q_out = pl.BlockSpec((CHUNK * dil, 128), lambda b, n, r: (b * nb + n, 0))
    k_out = pl.BlockSpec((SEQ, 128), lambda b, n, r: (b, 0))
    sh = jax.ShapeDtypeStruct((T, 128), F32)
    tiles = lambda: [q_like(t) for t in range(3)]
    return list(_call(
        body, name=f"attn_bwd_d{dil}", grid=(B, nb, dil),
        in_specs=tiles() + [k_like(3 + t) for t in range(3)] + [k_like(6 + t) for t in range(3)]
        + tiles() + tiles() + tiles(),
        out_specs=[q_out] * 3 + [k_out] * 6, out_shape=[sh] * 9,
        sem=("parallel", "arbitrary", "arbitrary"), vmem=ATTN_BWD_VMEM)(*([qkv] * 9 + [do] * 3 + [out] * 3 + [lse] * 3)))


def _sum_branches(parts):
    T = parts[0][0].shape[0]
    tm = _tile(T, 512)

    def body(*refs):
        o_ref = refs[-1]
        for c in range(9):
            acc = refs[c][...] + refs[9 + c][...] + refs[18 + c][...]
            o_ref[:, 128 * c:128 * (c + 1)] = acc.astype(BF)

    tile = pl.BlockSpec((tm, 128), lambda i: (i, 0))
    flat = [a for br in parts for a in br]
    return _call(body, name="attn_sum_branches", grid=(T // tm,), in_specs=[tile] * 27,
                 out_specs=pl.BlockSpec((tm, W_QKV), lambda i: (i, 0)),
                 out_shape=jax.ShapeDtypeStruct((T, W_QKV), BF), sem=("parallel",))(*flat)


def _silu(x):
    return x * _sigmoid(x)


def _dsilu(x):
    s = _sigmoid(x)
    return s * (1.0 + x * (1.0 - s))


def _log1p(u):
    return jnp.where(u < 0.01, u * (1.0 - u * (0.5 - u * (1.0 / 3.0))), jnp.log(1.0 + u))


def _softplus(x):
    return jnp.maximum(x, 0.0) + _log1p(jnp.exp(-jnp.abs(x)))


def _cumsum_rows(x, reverse=False):
    n = x.shape[0]
    rows = lax.broadcasted_iota(jnp.int32, x.shape, 0)
    k = 1
    while k < n:
        if reverse:
            x = x + jnp.where(rows < n - k, pltpu.roll(x, n - k, 0), 0.0)
        else:
            x = x + jnp.where(rows >= k, pltpu.roll(x, k, 0), 0.0)
        k *= 2
    return x


def _tri():
    r = lax.broadcasted_iota(jnp.int32, (CHUNK, CHUNK), 0)
    c = lax.broadcasted_iota(jnp.int32, (CHUNK, CHUNK), 1)
    return r >= c


def _row_mask(e):
    return (lax.broadcasted_iota(jnp.int32, (128, 1), 0) // HEAD) == e


def _first_lane(e):
    return lax.broadcasted_iota(jnp.int32, (1, 128), 1) == HEAD * e


def _ssd_pre(x_ref, halo_ref, first, cw_ref, cb_ref, dtb_ref, al_ref, ext):
    row = x_ref[...]
    z = row[:, SSD_CONV_DIM:SSD_CONV_DIM + SSD_W]
    u = row[:, SSD_CONV_DIM + SSD_W:] + dtb_ref[...]
    ext[0:8, :] = jnp.where(first, 0.0, halo_ref[:, 0:SSD_CONV_DIM])
    ext[8:8 + CHUNK, :] = row[:, 0:SSD_CONV_DIM]
    xc = cb_ref[...]
    for j in range(4):
        xc = xc + cw_ref[j:j + 1, :] * ext[pl.ds(5 + j, CHUNK), :]
    xa = _silu(xc)
    dt = _softplus(u)
    a = dt * (-jnp.exp(al_ref[...]))
    A = _cumsum_rows(a)
    return dict(z=z, u=u, xc=xc, xs=xa[:, 0:SSD_W], Bm=xa[:, SSD_W:SSD_W + 256], Cm=xa[:, SSD_W + 256:],
                dt=dt, a=a, A=A, AT=A.T, eA=jnp.exp(A), wdec=jnp.exp(A[CHUNK - 1:CHUNK, :] - A),
                dtot=jnp.exp(A[CHUNK - 1:CHUNK, :]))


def _ssd_y(p, hp_ref, dskip):
    tri = _tri()
    X = p["xs"] * p["dt"]
    Bb = [p["Bm"][:, 128 * g:128 * (g + 1)].astype(BF) for g in range(2)]
    Cb = [p["Cm"][:, 128 * g:128 * (g + 1)].astype(BF) for g in range(2)]
    CB = [_dot_nt(Cb[g], Bb[g]) for g in range(2)]
    tiles = []
    for t in range(3):
        sl = slice(128 * t, 128 * (t + 1))
        hpb = hp_ref[sl, :].astype(BF)
        acc = jnp.zeros((CHUNK, 128), F32)
        for e in range(2):
            h = 2 * t + e
            g, col = h // 3, HEAD * h
            lm = _lane_mask(e)
            L = jnp.exp(jnp.where(tri, p["A"][:, col:col + 1] - p["AT"][col:col + 1, :], NEG))
            yd = _dot((CB[g] * L).astype(BF), jnp.where(lm, X[:, sl], 0.0).astype(BF))
            yo = _dot_nt(Cb[g], hpb) * p["eA"][:, sl]
            acc = acc + jnp.where(lm, yd + yo, 0.0)
        tiles.append(acc)
    return jnp.concatenate(tiles, axis=1) + dskip * p["xs"], X, Bb, Cb, CB


def _group_stats(v):
    g0 = lax.broadcasted_iota(jnp.int32, (1, SSD_W), 1) < SSD_W // 2
    m0 = jnp.sum(jnp.where(g0, v, 0.0), axis=-1, keepdims=True) * (2.0 / SSD_W)
    m1 = jnp.sum(jnp.where(g0, 0.0, v), axis=-1, keepdims=True) * (2.0 / SSD_W)
    return jnp.where(g0, m0, m1)


def _ssd_specs(T, rev):
    B = T // SEQ

    def chunk(b, c):
        return b * N_CHUNK + (N_CHUNK - 1 - c if rev else c)

    row = pl.BlockSpec((CHUNK, W_SSD), lambda b, c: (chunk(b, c), 0))
    halo = pl.BlockSpec((8, W_SSD), lambda b, c: (jnp.maximum(chunk(b, c) * (CHUNK // 8) - 1, 0), 0))
    hp = pl.BlockSpec((None, SSD_W, SSD_STATE), lambda b, c: (chunk(b, c), 0, 0))
    y = pl.BlockSpec((CHUNK, SSD_W), lambda b, c: (chunk(b, c), 0))
    const = lambda r, w: pl.BlockSpec((r, w), lambda b, c: (0, 0))
    params = [const(4, SSD_CONV_DIM), const(1, SSD_CONV_DIM)] + [const(1, SSD_W)] * 4
    return B, row, halo, hp, y, const, params


def _ssd_fwd(sin, conv_w, conv_b, dtb, alog, dskip, norm_g):
    T = sin.shape[0]
    B, row, halo, hp, y, const, params = _ssd_specs(T, False)

    def body(x_ref, halo_ref, cw_ref, cb_ref, dtb_ref, al_ref, dk_ref, ng_ref, y_ref, hp_ref, ext, hst):
        c = pl.program_id(1)

        @pl.when(c == 0)
        def _():
            hst[...] = jnp.zeros_like(hst)

        p = _ssd_pre(x_ref, halo_ref, c == 0, cw_ref, cb_ref, dtb_ref, al_ref, ext)
        yv, X, Bb, Cb, CB = _ssd_y(p, hst, dk_ref[...])
        hp_ref[...] = hst[...]
        for t in range(3):
            sl = slice(128 * t, 128 * (t + 1))
            old = hst[sl, :]
            new = old
            for e in range(2):
                h = 2 * t + e
                g, col = h // 3, HEAD * h
                st = _dot_tn(jnp.where(_lane_mask(e), X[:, sl] * p["wdec"][:, sl], 0.0).astype(BF), Bb[g])
                new = jnp.where(_row_mask(e), old * p["dtot"][:, col:col + 1] + st, new)
            hst[sl, :] = new
        y2 = yv * _silu(p["z"])
        r = lax.rsqrt(_group_stats(y2 * y2) + RMS_EPS)
        y_ref[...] = y2 * r * ng_ref[...]

    return _call(body, name="ssd_fwd", grid=(B, N_CHUNK), in_specs=[row, halo] + params, out_specs=[y, hp],
                 out_shape=[jax.ShapeDtypeStruct((T, SSD_W), F32),
                            jax.ShapeDtypeStruct((T // CHUNK, SSD_W, SSD_STATE), F32)],
                 scratch=[pltpu.VMEM((8 + CHUNK, SSD_CONV_DIM), F32), pltpu.VMEM((SSD_W, SSD_STATE), F32)],
                 sem=("parallel", "arbitrary"))(sin, sin, conv_w, conv_b, dtb, alog, dskip, norm_g)


def _ssd_bwd(sin, hprev, dy3, conv_w, conv_b, dtb, alog, dskip, norm_g):
    T = sin.shape[0]
    B, row, halo, hp, y, const, params = _ssd_specs(T, True)

    def body(x_ref, halo_ref, hp_ref, dy_ref, cw_ref, cb_ref, dtb_ref, al_ref, dk_ref, ng_ref,
             dx_ref, dcw_ref, dcb_ref, dvec_ref, ext, ext2, dh):
        c = pl.program_id(1)

        @pl.when((pl.program_id(0) == 0) & (c == 0))
        def _():
            dcw_ref[...] = jnp.zeros_like(dcw_ref)
            dcb_ref[...] = jnp.zeros_like(dcb_ref)
            dvec_ref[...] = jnp.zeros_like(dvec_ref)

        @pl.when(c == 0)
        def _():
            dh[...] = jnp.zeros_like(dh)
            ext2[CHUNK:CHUNK + 8, :] = jnp.zeros((8, SSD_CONV_DIM), F32)

        p = _ssd_pre(x_ref, halo_ref, c == N_CHUNK - 1, cw_ref, cb_ref, dtb_ref, al_ref, ext)
        dskip_ = dk_ref[...]
        yv, X, Bb, Cb, CB = _ssd_y(p, hp_ref, dskip_)
        xs, z, A, AT = p["xs"], p["z"], p["A"], p["AT"]

        sz = _silu(z)
        y2 = yv * sz
        r = lax.rsqrt(_group_stats(y2 * y2) + RMS_EPS)
        dy3_ = dy_ref[...]
        uu = dy3_ * ng_ref[...]
        dy2 = r * (uu - y2 * (r * r * _group_stats(uu * y2)))
        dy = dy2 * sz
        dz = dy2 * yv * _dsilu(z)

        tri = _tri()
        rows = lax.broadcasted_iota(jnp.int32, (CHUNK, 1), 0)
        dG = [jnp.zeros((CHUNK, CHUNK), F32) for _ in range(2)]
        dB = [jnp.zeros((CHUNK, SSD_STATE), F32) for _ in range(2)]
        dC = [jnp.zeros((CHUNK, SSD_STATE), F32) for _ in range(2)]
        dX_t, dA_t, ddtx_t = [], [], []
        for t in range(3):
            sl = slice(128 * t, 128 * (t + 1))
            hp_t = hp_ref[sl, :]
            hpb = hp_t.astype(BF)
            dhc = dh[sl, :]
            dh_new = jnp.zeros((128, SSD_STATE), F32)
            dX = jnp.zeros((CHUNK, 128), F32)
            dA = jnp.zeros((CHUNK, 128), F32)
            ddtx = jnp.zeros((CHUNK, 128), F32)
            for e in range(2):
                h = 2 * t + e
                g, col = h // 3, HEAD * h
                lm, rm, fl = _lane_mask(e), _row_mask(e), _first_lane(e)
                L = jnp.exp(jnp.where(tri, A[:, col:col + 1] - AT[col:col + 1, :], NEG))
                Mf = CB[g] * L
                Xm = jnp.where(lm, X[:, sl], 0.0)
                Xmb = Xm.astype(BF)
                dyh = jnp.where(lm, dy[:, sl], 0.0)
                dyb = dyh.astype(BF)
                dXh = _dot_tn(Mf.astype(BF), dyb)
                dM = jnp.where(tri, _dot_nt(dyb, Xmb), 0.0)
                Wm = dM * Mf
                dAc = jnp.sum(Wm, axis=-1, keepdims=True) - jnp.sum(Wm.T, axis=-1, keepdims=True)
                dG[g] = dG[g] + dM * L
                eAt = p["eA"][:, sl]
                yo = _dot_nt(Cb[g], hpb)
                dyo = (dyh * eAt).astype(BF)
                dC[g] = dC[g] + _dot(dyo, hpb)
                dh_new = dh_new + _dot_tn(dyo, Cb[g])
                dAc = dAc + jnp.sum(dyh * yo * eAt, axis=-1, keepdims=True)
                dHn = jnp.where(rm, dhc, 0.0)
                dHnb = dHn.astype(BF)
                dec = p["dtot"][:, col:col + 1]
                dh_new = dh_new + dec * dHn
                Z = _dot_nt(Bb[g], dHnb)
                wt = p["wdec"][:, sl]
                xi = jnp.sum(Xm * Z, axis=-1, keepdims=True) * p["wdec"][:, col:col + 1]
                dXh = dXh + wt * Z
                dB[g] = dB[g] + _dot(jnp.where(lm, X[:, sl] * wt, 0.0).astype(BF), dHnb)
                dAtot = jnp.sum(xi, axis=0, keepdims=True) + dec * jnp.sum(
                    jnp.sum(dHn * hp_t, axis=-1, keepdims=True), axis=0, keepdims=True)
                dAc = dAc - xi + jnp.where(rows == CHUNK - 1, dAtot, 0.0)
                dA = dA + jnp.where(fl, dAc, 0.0)
                dX = dX + dXh
                ddtx = ddtx + jnp.where(fl, jnp.sum(dXh * xs[:, sl], axis=-1, keepdims=True), 0.0)
            dh[sl, :] = dh_new
            dX_t.append(dX)
            dA_t.append(dA)
            ddtx_t.append(ddtx)
        for g in range(2):
            dGb = dG[g].astype(BF)
            dC[g] = dC[g] + _dot(dGb, Bb[g])
            dB[g] = dB[g] + _dot_tn(dGb, Cb[g])
        dXf = jnp.concatenate(dX_t, axis=1)
        da = _cumsum_rows(jnp.concatenate(dA_t, axis=1), reverse=True)
        ddt = da * (-jnp.exp(al_ref[...])) + jnp.concatenate(ddtx_t, axis=1)
        du = ddt * _sigmoid(p["u"])
        dxs = dXf * p["dt"] + dskip_ * dy
        dxc = jnp.concatenate([dxs, dB[0], dB[1], dC[0], dC[1]], axis=1) * _dsilu(p["xc"])
        ext2[0:CHUNK, :] = dxc
        dxbc = jnp.zeros((CHUNK, SSD_CONV_DIM), F32)
        for j in range(4):
            dxbc = dxbc + cw_ref[j:j + 1, :] * ext2[pl.ds(3 - j, CHUNK), :]
            dcw_ref[j:j + 1, :] += jnp.sum(dxc * ext[pl.ds(5 + j, CHUNK), :], axis=0, keepdims=True)
        ext2[CHUNK:CHUNK + 8, :] = dxc[0:8, :]
        dcb_ref[...] += jnp.sum(dxc, axis=0, keepdims=True)
        dvec_ref[0:1, :] += jnp.sum(du, axis=0, keepdims=True)
        dvec_ref[1:2, :] += jnp.sum(da * p["a"], axis=0, keepdims=True)
        dvec_ref[2:3, :] += jnp.sum(dy * xs, axis=0, keepdims=True)
        dvec_ref[3:4, :] += jnp.sum(dy3_ * y2 * r, axis=0, keepdims=True)
        dx_ref[...] = jnp.concatenate([dxbc, dz, du], axis=1).astype(BF)

    return _call(body, name="ssd_bwd", grid=(B, N_CHUNK), in_specs=[row, halo, hp, y] + params,
                 out_specs=[row, const(4, SSD_CONV_DIM), const(1, SSD_CONV_DIM), const(8, SSD_W)],
                 out_shape=[jax.ShapeDtypeStruct((T, W_SSD), BF), jax.ShapeDtypeStruct((4, SSD_CONV_DIM), F32),
                            jax.ShapeDtypeStruct((1, SSD_CONV_DIM), F32), jax.ShapeDtypeStruct((8, SSD_W), F32)],
                 scratch=[pltpu.VMEM((8 + CHUNK, SSD_CONV_DIM), F32), pltpu.VMEM((8 + CHUNK, SSD_CONV_DIM), F32),
                          pltpu.VMEM((SSD_W, SSD_STATE), F32)],
                 sem=("arbitrary", "arbitrary"))(sin, sin, hprev, dy3, conv_w, conv_b, dtb, alog, dskip, norm_g)


def _sgu_core(uv_ref, g_ref, b_ref, w_ref, bias_ref):
    x = uv_ref[...]
    cdf = 0.5 * (1.0 + lax.erf(x * (2.0 ** -0.5)))
    ge = x * cdf
    dge = cdf + x * jnp.exp(-0.5 * x * x) * ((2.0 * math.pi) ** -0.5)
    u, v = ge[:, 0:SGU_W], ge[:, SGU_W:]
    vc = v - jnp.mean(v, axis=-1, keepdims=True)
    rstd = lax.rsqrt(jnp.mean(vc * vc, axis=-1, keepdims=True) + LN_EPS)
    vhat = vc * rstd
    vn = vhat * g_ref[...] + b_ref[...]
    tri = _tri()
    wc = [jnp.where(tri, w_ref[gi], 0.0).astype(BF) for gi in range(4)]
    vm = [jnp.where(_lane_mask(gi % 2), vn[:, 128 * (gi // 2):128 * (gi // 2 + 1)], 0.0).astype(BF) for gi in range(4)]
    mixed = jnp.concatenate([_dot(wc[2 * t], vm[2 * t]) + _dot(wc[2 * t + 1], vm[2 * t + 1]) for t in range(2)],
                            axis=1) + bias_ref[...]
    return dict(dge=dge, u=u, rstd=rstd, vhat=vhat, wc=wc, vm=vm, mixed=mixed)


def _sgu_specs():
    vec = pl.BlockSpec((1, SGU_W), lambda i: (0, 0))
    return [pl.BlockSpec((CHUNK, W_UV), lambda i: (i, 0)), vec, vec,
            pl.BlockSpec((4, CHUNK, CHUNK), lambda i: (0, 0, 0)), pl.BlockSpec((CHUNK, SGU_W), lambda i: (0, 0))]


def _sgu_fwd(uv, ln_g, ln_b, w, bias):
    T = uv.shape[0]

    def body(uv_ref, g_ref, b_ref, w_ref, bias_ref, y_ref):
        s = _sgu_core(uv_ref, g_ref, b_ref, w_ref, bias_ref)
        y_ref[...] = s["u"] * s["mixed"]

    return _call(body, name="sgu_fwd", grid=(T // CHUNK,), in_specs=_sgu_specs(),
                 out_specs=pl.BlockSpec((CHUNK, SGU_W), lambda i: (i, 0)),
                 out_shape=jax.ShapeDtypeStruct((T, SGU_W), F32), sem=("parallel",))(uv, ln_g, ln_b, w, bias)


def _sgu_bwd(uv, dy, ln_g, ln_b, w, bias):
    T = uv.shape[0]

    def body(uv_ref, dy_ref, g_ref, b_ref, w_ref, bias_ref, dx_ref, dw_ref, dbias_ref, dln_ref):
        @pl.when(pl.program_id(0) == 0)
        def _():
            dw_ref[...] = jnp.zeros_like(dw_ref)
            dbias_ref[...] = jnp.zeros_like(dbias_ref)
            dln_ref[...] = jnp.zeros_like(dln_ref)

        s = _sgu_core(uv_ref, g_ref, b_ref, w_ref, bias_ref)
        dy_ = dy_ref[...]
        du = dy_ * s["mixed"]
        dmix = dy_ * s["u"]
        dbias_ref[...] += dmix
        tri = _tri()
        dvn_t = []
        for t in range(2):
            acc = jnp.zeros((CHUNK, 128), F32)
            for e in range(2):
                gi = 2 * t + e
                dmg = jnp.where(_lane_mask(e), dmix[:, 128 * t:128 * (t + 1)], 0.0).astype(BF)
                acc = acc + _dot_tn(s["wc"][gi], dmg)
                dw_ref[gi] += jnp.where(tri, _dot_nt(dmg, s["vm"][gi]), 0.0)
            dvn_t.append(acc)
        dvn = jnp.concatenate(dvn_t, axis=1)
        dln_ref[0:1, :] += jnp.sum(dvn * s["vhat"], axis=0, keepdims=True)
        dln_ref[1:2, :] += jnp.sum(dvn, axis=0, keepdims=True)
        dvh = dvn * g_ref[...]
        dv = s["rstd"] * (dvh - jnp.mean(dvh, axis=-1, keepdims=True)
                          - s["vhat"] * jnp.mean(dvh * s["vhat"], axis=-1, keepdims=True))
        dx_ref[...] = (jnp.concatenate([du, dv], axis=1) * s["dge"]).astype(BF)

    ins = _sgu_specs()
    return _call(body, name="sgu_bwd", grid=(T // CHUNK,),
                 in_specs=[ins[0], pl.BlockSpec((CHUNK, SGU_W), lambda i: (i, 0))] + ins[1:],
                 out_specs=[pl.BlockSpec((CHUNK, W_UV), lambda i: (i, 0)),
                            pl.BlockSpec((4, CHUNK, CHUNK), lambda i: (0, 0, 0)),
                            pl.BlockSpec((CHUNK, SGU_W), lambda i: (0, 0)), pl.BlockSpec((8, SGU_W), lambda i: (0, 0))],
                 out_shape=[jax.ShapeDtypeStruct((T, W_UV), BF), jax.ShapeDtypeStruct((4, CHUNK, CHUNK), F32),
                            jax.ShapeDtypeStruct((CHUNK, SGU_W), F32), jax.ShapeDtypeStruct((8, SGU_W), F32)],
                 sem=("arbitrary",))(uv, dy, ln_g, ln_b, w, bias)


def _adamw(w, g, m, v):
    R, C = w.shape
    tr = R

    def body(w_ref, g_ref, m_ref, v_ref, d_ref, nm_ref, nv_ref):
        g_ = g_ref[...]
        m2 = ADAM_B1 * m_ref[...] + (1.0 - ADAM_B1) * g_
        v2 = ADAM_B2 * v_ref[...] + (1.0 - ADAM_B2) * (g_ * g_)
        m_hat = m2 / (1.0 - ADAM_B1 ** ADAM_STEP)
        v_hat = v2 / (1.0 - ADAM_B2 ** ADAM_STEP)
        d_ref[...] = -ADAM_LR * (m_hat / (jnp.sqrt(v_hat) + ADAM_EPS) + ADAM_WD * w_ref[...])
        nm_ref[...] = m2
        nv_ref[...] = v2

    blk = pl.BlockSpec((tr, C), lambda i: (i, 0))
    sh = jax.ShapeDtypeStruct((R, C), F32)
    return _call(body, name="adamw", grid=(R // tr,), in_specs=[blk] * 4, out_specs=[blk] * 3,
                 out_shape=[sh] * 3, sem=("parallel",))(w, g, m, v)


def _adamw_pair(w, g0, g1, m, v, dep):
    L, R, C = w.shape
    tr = _tile(R, 256 if C <= 1024 else 64)

    def body(w_ref, g0_ref, g1_ref, m_ref, v_ref, dep_ref, d_ref, nm_ref, nv_ref, og_ref):
        g_ = jnp.where(pl.program_id(0) == 0, g0_ref[...], g1_ref[...])
        m2 = ADAM_B1 * m_ref[...] + (1.0 - ADAM_B1) * g_
        v2 = ADAM_B2 * v_ref[...] + (1.0 - ADAM_B2) * (g_ * g_)
        m_hat = m2 / (1.0 - ADAM_B1 ** ADAM_STEP)
        v_hat = v2 / (1.0 - ADAM_B2 ** ADAM_STEP)
        d_ref[...] = -ADAM_LR * (m_hat / (jnp.sqrt(v_hat) + ADAM_EPS) + ADAM_WD * w_ref[...])
        nm_ref[...] = m2
        nv_ref[...] = v2
        og_ref[...] = g_

    lay = pl.BlockSpec((None, tr, C), lambda l, i: (l, i, 0))
    one = lambda k: pl.BlockSpec((tr, C), lambda l, i: (jnp.where(l == k, i, 0), 0))
    return _call(body, name="adamw_pair", grid=(L, R // tr),
                 in_specs=[lay, one(0), one(1), lay, lay, pl.BlockSpec((8, 128), lambda l, i: (0, 0))],
                 out_specs=[lay] * 4,
                 out_shape=[jax.ShapeDtypeStruct((L, R, C), F32)] * 4,
                 sem=("parallel", "parallel"))(w, g0, g1, m, v, dep)


def _row_steps(rows):
    return 2 if rows % 32 == 0 else 1


def _pair_add(gbuf, rsib, c):
    NS, _, R, C = gbuf.shape
    n = _row_steps(R)
    tr = R // n

    def body(c_ref, a_ref, b_ref, o_ref):
        o_ref[...] = (a_ref[...] + b_ref[...]).astype(BF)

    blk = pl.BlockSpec((None, tr, C), lambda j, i, c_ref: (j, i, 0))
    return pl.pallas_call(
        body, name="rs_pair_add",
        grid_spec=pltpu.PrefetchScalarGridSpec(
            num_scalar_prefetch=1, grid=(NS, n),
            in_specs=[pl.BlockSpec((None, None, tr, C), lambda j, i, c_ref: (j, c_ref[0], i, 0)), blk],
            out_specs=blk),
        out_shape=jax.ShapeDtypeStruct((NS, R, C), BF),
        compiler_params=pltpu.CompilerParams(dimension_semantics=("parallel", "parallel")),
    )(jnp.reshape(c, (1,)).astype(jnp.int32), gbuf, rsib)


def _chip_sum(pair, recv, me, c):
    NS, R, C = pair.shape
    n = _row_steps(R)
    tr = R // n

    def body(s_ref, own_ref, p_ref, o_ref):
        p = [jnp.where(s_ref[0] == j, own_ref[...], p_ref[j]).astype(F32) for j in range(4)]
        o_ref[...] = ((p[0] + p[1]) + p[2]) + p[3]

    return pl.pallas_call(
        body, name="rs_chip_sum",
        grid_spec=pltpu.PrefetchScalarGridSpec(
            num_scalar_prefetch=1, grid=(n,),
            in_specs=[pl.BlockSpec((None, tr, C), lambda i, s: (s[0], i, 0)),
                      pl.BlockSpec((NS, tr, C), lambda i, s: (0, i, 0))],
            out_specs=pl.BlockSpec((None, tr, C), lambda i, s: (s[1], i, 0))),
        out_shape=jax.ShapeDtypeStruct((2, R, C), F32),
        compiler_params=pltpu.CompilerParams(dimension_semantics=("parallel",)),
    )(jnp.stack([me, c]).astype(jnp.int32), pair, recv)


MESH = pl.DeviceIdType.MESH
ANY = pl.BlockSpec(memory_space=pl.ANY)


def _place():
    x, y, c = lax.axis_index("x"), lax.axis_index("y"), lax.axis_index("c")
    return x, y, c, [(1 - x, y), (x, 1 - y), (1 - x, 1 - y)]


HBM = pl.BlockSpec(memory_space=pltpu.HBM)
SEM = pl.BlockSpec(memory_space=pltpu.SEMAPHORE)
EFFECT = pltpu.SideEffectType.DATAFLOW_SIDE_EFFECTING


class _Split:
    def __init__(self, tag, arrays, copies, n_copies):
        self.tag, self.copies, k = tag, copies, len(arrays)

        def body(*refs):
            for cp in copies(refs[:k], refs[k], refs[k + 1]):
                cp.start()
            refs[-1][...] = jnp.zeros_like(refs[-1])

        out = pl.pallas_call(
            body, name=tag + "_start",
            out_shape=(pltpu.SemaphoreType.DMA((n_copies,)), pltpu.SemaphoreType.DMA((n_copies,)),
                       *[pltpu.HBM(a.shape, a.dtype) for a in arrays], jax.ShapeDtypeStruct((8, 128), F32)),
            in_specs=[HBM] * k, out_specs=(SEM, SEM, *[HBM] * k, pl.BlockSpec(memory_space=pltpu.VMEM)),
            input_output_aliases={i: 2 + i for i in range(k)},
            compiler_params=pltpu.CompilerParams(has_side_effects=EFFECT),
        )(*[pltpu.with_memory_space_constraint(a, pltpu.HBM) for a in arrays])
        self.send, self.recv, self.arrays, self.token = out[0], out[1], list(out[2:2 + k]), out[-1][0, 0]

    def wait(self, after):
        k, copies = len(self.arrays), self.copies

        def body(*refs):
            for cp in copies(refs[:k], refs[k], refs[k + 1]):
                cp.wait_send()
                cp.wait_recv()

        return list(pl.pallas_call(
            body, name=self.tag + "_wait", out_shape=tuple(pltpu.HBM(a.shape, a.dtype) for a in self.arrays),
            in_specs=[HBM] * k + [SEM, SEM, ANY], out_specs=tuple([HBM] * k),
            input_output_aliases={i: i for i in range(k)},
            compiler_params=pltpu.CompilerParams(has_side_effects=EFFECT),
        )(*self.arrays, self.send, self.recv, after))


def _gather_start(arrs, tag):
    n = len(arrs)
    me = 2 * lax.axis_index("x") + lax.axis_index("y")
    lands = [lax.dynamic_update_index_in_dim(lax.empty((4,) + a.shape, a.dtype), a, me, 0) for a in arrs]

    def copies(refs, send, recv):
        x, y, c, chips = _place()
        return [pltpu.make_async_remote_copy(
            src_ref=refs[k], dst_ref=refs[n + k].at[2 * x + y], send_sem=send.at[3 * k + r],
            recv_sem=recv.at[3 * k + r], device_id=(px, py, c), device_id_type=MESH)
            for k in range(n) for r, (px, py) in enumerate(chips)]

    return _Split("gather_" + tag, list(arrs) + lands, copies, 3 * n)


def _gather_halves_start(arrs, tag):
    n = len(arrs)
    me = 2 * lax.axis_index("x") + lax.axis_index("y")
    lands = [lax.dynamic_update_index_in_dim(lax.empty((4,) + a.shape, a.dtype), a, me, 0) for a in arrs]

    def copies(refs, send, recv):
        x, y, c, chips = _place()
        return [pltpu.make_async_remote_copy(
            src_ref=refs[k].at[c], dst_ref=refs[n + k].at[2 * x + y, c], send_sem=send.at[3 * k + r],
            recv_sem=recv.at[3 * k + r], device_id=(px, py, c), device_id_type=MESH)
            for k in range(n) for r, (px, py) in enumerate(chips)]

    return _Split("gather_" + tag, list(arrs) + lands, copies, 3 * n)


def _gather_halves_finish(lands, tag):
    n = len(lands)

    def copies(refs, send, recv):
        x, y, c, chips = _place()
        return [pltpu.make_async_remote_copy(
            src_ref=refs[k].at[2 * px + py, c], dst_ref=refs[k].at[2 * px + py, c], send_sem=send.at[3 * k + r],
            recv_sem=recv.at[3 * k + r], device_id=(x, y, 1 - c), device_id_type=MESH)
            for k in range(n) for r, (px, py) in enumerate(chips)]

    return _Split("gather_pass_" + tag, list(lands), copies, 3 * n)


def _to_sibling_start(gbufs, tag):
    n = len(gbufs)

    def copies(refs, send, recv):
        x, y, c, _ = _place()
        return [pltpu.make_async_remote_copy(
            src_ref=refs[k].at[j, 1 - c], dst_ref=refs[n + k].at[j], send_sem=send.at[4 * k + j],
            recv_sem=recv.at[4 * k + j], device_id=(x, y, 1 - c), device_id_type=MESH)
            for k in range(n) for j in range(4)]

    lands = [lax.empty((4,) + g.shape[2:], g.dtype) for g in gbufs]
    return _Split("rs_sibling_" + tag, list(gbufs) + lands, copies, 4 * n)


def _to_chips_start(pbufs, tag):
    n = len(pbufs)

    def copies(refs, send, recv):
        x, y, c, chips = _place()
        return [pltpu.make_async_remote_copy(
            src_ref=refs[k].at[2 * px + py], dst_ref=refs[n + k].at[2 * x + y], send_sem=send.at[3 * k + r],
            recv_sem=recv.at[3 * k + r], device_id=(px, py, c), device_id_type=MESH)
            for k in range(n) for r, (px, py) in enumerate(chips)]

    return _Split("rs_chips_" + tag, list(pbufs) + [lax.empty(p.shape, p.dtype) for p in pbufs], copies, 3 * n)


def _join_start(fulls, tag):
    def copies(refs, send, recv):
        x, y, c, _ = _place()
        return [pltpu.make_async_remote_copy(
            src_ref=refs[k].at[c], dst_ref=refs[k].at[c], send_sem=send.at[k], recv_sem=recv.at[k],
            device_id=(x, y, 1 - c), device_id_type=MESH) for k in range(len(fulls))]

    return _Split("rs_join_" + tag, list(fulls), copies, len(fulls))


def _all_reduce_small(v):
    R, C = v.shape

    def body(v_ref, o_ref, g_ref, send, recv, loc):
        x, y, c, chips = _place()
        me, sibling = (x, y, c), (x, y, 1 - c)

        def rows(px, py, pc):
            return g_ref.at[4 * px + 2 * py + pc]

        def copy(k, block, to, src=None):
            return pltpu.make_async_remote_copy(
                src_ref=rows(*block) if src is None else src, dst_ref=rows(*block),
                send_sem=send.at[k], recv_sem=recv.at[k], device_id=to, device_id_type=MESH)

        mine = pltpu.make_async_copy(v_ref, rows(*me), loc)
        mine.start()
        first = [copy(0, me, sibling, src=v_ref)]
        first += [copy(1 + j, me, (*chip, c), src=v_ref) for j, chip in enumerate(chips)]
        for cp in first:
            cp.start()
        passed = [copy(4 + j, (*chip, c), sibling) for j, chip in enumerate(chips)]
        for j, chip in enumerate(chips):
            copy(1 + j, (*chip, c), me).wait_recv()
            passed[j].start()
        copy(0, sibling, me).wait_recv()
        for j, chip in enumerate(chips):
            copy(4 + j, (*chip, 1 - c), me).wait_recv()
        for cp in first + passed:
            cp.wait_send()
        mine.wait()
        acc = g_ref[0]
        for d in range(1, 8):
            acc = acc + g_ref[d]
        o_ref[...] = acc

    vm = pl.BlockSpec(memory_space=pltpu.VMEM)
    return pl.pallas_call(
        body, name="all_reduce_small", in_specs=[vm], out_specs=[vm, vm],
        out_shape=[jax.ShapeDtypeStruct((R, C), F32), jax.ShapeDtypeStruct((8, R, C), F32)],
        scratch_shapes=[pltpu.SemaphoreType.DMA((7,)), pltpu.SemaphoreType.DMA((7,)), pltpu.SemaphoreType.DMA],
    )(v)[0]


WEIGHTS = ['ffn1_norm', 'ffn1_w_gate', 'ffn1_w_up', 'ffn1_w_down', 'mix_norm', 'w_in', 'conv_w', 'conv_b', 'dt_bias',
           'a_log', 'd_skip', 'ssd_norm', 'sgu_ln_g', 'sgu_ln_b', 'sgu_w', 'sgu_b', 'w_out', 'ffn2_norm',
           'ffn2_w_gate', 'ffn2_w_up', 'ffn2_w_down', 'final_norm']
SHARDED = ['ffn1_w_gate', 'ffn1_w_up', 'ffn1_w_down', 'w_in', 'conv_w', 'w_out', 'ffn2_w_gate', 'ffn2_w_up',
           'ffn2_w_down']
SMALL = [n for n in WEIGHTS if n not in SHARDED]
GROUPS = [("ffn1", ["ffn1_w_gate", "ffn1_w_up", "ffn1_w_down"]), ("mix", ["w_in", "conv_w", "w_out"]),
          ("ffn2", ["ffn2_w_gate", "ffn2_w_up", "ffn2_w_down"])]
TRANSPOSED = ("ffn1_w_gate", "ffn1_w_up", "ffn2_w_gate", "ffn2_w_up")
DEPTH = 2


def _pack_w_in(w):
    return jnp.concatenate([w[..., 0:1152], w[..., 1536:2432], w[..., 1152:1536],
                            jnp.repeat(w[..., 2432:2438], HEAD, axis=-1), w[..., 2438:2950]], axis=-1)


def _unpack_w_in(dq, ds, du):
    return jnp.concatenate([dq, ds[:, 896:1280], ds[:, 0:896], ds[:, 1280::HEAD], du], axis=-1)


def _ffn_fwd(x, g, wg, wu, wd):
    xo, hb, S1, S2, A = _ffn_fwd_k(x, g, wg, wu, wd)
    return xo, (x, hb, S1, S2, A)


def _ffn_bwd(dxo, saved, g, wg, wu, wd):
    x, hb, S1, S2, A = saved
    dx, dg, dG, dU, dyb = _ffn_bwd_k1(dxo, x, g, S1, S2, wg, wu, wd)
    dwg, dwu, dwd = _ffn_bwd_k2(hb, dyb, A, dG, dU)
    return dx, dg, dwg, dwu, dwd


def _mix_fwd(x, P):
    hb, qkv, sin, uv = _mix_proj(x, P["mix_norm"], P["w_in"])
    y_att, lse = _attn_combine([_attn_fwd(qkv, d) for d in DILATIONS])
    y_ssd, hprev = _ssd_fwd(sin, *P["ssd"])
    y_sgu = _sgu_fwd(uv, *P["sgu"])
    ycat = jnp.concatenate([y_att, y_ssd, y_sgu], axis=1).astype(BF)
    return _mm_nn(ycat, P["w_out"], res=x), (x, hb, qkv, sin, uv, y_att, lse, hprev, ycat)


def _mix_bwd(dxo, saved, P):
    x, hb, qkv, sin, uv, y_att, lse, hprev, ycat = saved
    dycat = _mm_nt(dxo, P["w_out"])
    dwout = _mm_tn(ycat, dxo)
    dy_att, dy_ssd, dy_sgu = dycat[:, 0:ATT_W], dycat[:, ATT_W:ATT_W + SSD_W], dycat[:, ATT_W + SSD_W:]
    dqkv = _sum_branches([_attn_bwd(qkv, dy_att, y_att, lse, d) for d in DILATIONS])
    dsin, dcw, dcb, dvec = _ssd_bwd(sin, hprev, dy_ssd, *P["ssd"])
    duv, dsw, dsbias, dln = _sgu_bwd(uv, dy_sgu, *P["sgu"])
    dwin = _unpack_w_in(_mm_tn(hb, dqkv), _mm_tn(hb, dsin), _mm_tn(hb, duv))
    dx, dg = _mix_bwd_dx(dqkv, dsin, duv, P["w_in"], x, P["mix_norm"], dxo)
    grads = dict(
        mix_norm=dg[0], w_in=dwin, conv_w=dcw, conv_b=dcb[0], dt_bias=dvec[0, ::HEAD], a_log=dvec[1, ::HEAD],
        d_skip=jnp.sum(dvec[2].reshape(6, HEAD), axis=-1), ssd_norm=dvec[3], sgu_ln_g=dln[0], sgu_ln_b=dln[1],
        sgu_w=dsw, sgu_b=jnp.sum(dsbias.reshape(CHUNK, 4, HEAD), axis=-1).T, w_out=dwout)
    return dx, grads


def _halved(g):
    rows = g.size // g.shape[-1]
    return g.reshape(4, 2, rows // 8, g.shape[-1])


def kernel(x, ffn1_norm, ffn1_w_gate, ffn1_w_up, ffn1_w_down, mix_norm, w_in, conv_w, conv_b, dt_bias, a_log, d_skip, ssd_norm, sgu_ln_g, sgu_ln_b, sgu_w, sgu_b, w_out, ffn2_norm, ffn2_w_gate, ffn2_w_up, ffn2_w_down, final_norm, loss_target, m_ffn1_norm, m_ffn1_w_gate, m_ffn1_w_up, m_ffn1_w_down, m_mix_norm, m_w_in, m_conv_w, m_conv_b, m_dt_bias, m_a_log, m_d_skip, m_ssd_norm, m_sgu_ln_g, m_sgu_ln_b, m_sgu_w, m_sgu_b, m_w_out, m_ffn2_norm, m_ffn2_w_gate, m_ffn2_w_up, m_ffn2_w_down, m_final_norm, v_ffn1_norm, v_ffn1_w_gate, v_ffn1_w_up, v_ffn1_w_down, v_mix_norm, v_w_in, v_conv_w, v_conv_b, v_dt_bias, v_a_log, v_d_skip, v_ssd_norm, v_sgu_ln_g, v_sgu_ln_b, v_sgu_w, v_sgu_b, v_w_out, v_ffn2_norm, v_ffn2_w_gate, v_ffn2_w_up, v_ffn2_w_down, v_final_norm):
    given = dict(x=x, ffn1_norm=ffn1_norm, ffn1_w_gate=ffn1_w_gate, ffn1_w_up=ffn1_w_up, ffn1_w_down=ffn1_w_down, mix_norm=mix_norm, w_in=w_in, conv_w=conv_w, conv_b=conv_b, dt_bias=dt_bias, a_log=a_log, d_skip=d_skip, ssd_norm=ssd_norm, sgu_ln_g=sgu_ln_g, sgu_ln_b=sgu_ln_b, sgu_w=sgu_w, sgu_b=sgu_b, w_out=w_out, ffn2_norm=ffn2_norm, ffn2_w_gate=ffn2_w_gate, ffn2_w_up=ffn2_w_up, ffn2_w_down=ffn2_w_down, final_norm=final_norm, loss_target=loss_target, m_ffn1_norm=m_ffn1_norm, m_ffn1_w_gate=m_ffn1_w_gate, m_ffn1_w_up=m_ffn1_w_up, m_ffn1_w_down=m_ffn1_w_down, m_mix_norm=m_mix_norm, m_w_in=m_w_in, m_conv_w=m_conv_w, m_conv_b=m_conv_b, m_dt_bias=m_dt_bias, m_a_log=m_a_log, m_d_skip=m_d_skip, m_ssd_norm=m_ssd_norm, m_sgu_ln_g=m_sgu_ln_g, m_sgu_ln_b=m_sgu_ln_b, m_sgu_w=m_sgu_w, m_sgu_b=m_sgu_b, m_w_out=m_w_out, m_ffn2_norm=m_ffn2_norm, m_ffn2_w_gate=m_ffn2_w_gate, m_ffn2_w_up=m_ffn2_w_up, m_ffn2_w_down=m_ffn2_w_down, m_final_norm=m_final_norm, v_ffn1_norm=v_ffn1_norm, v_ffn1_w_gate=v_ffn1_w_gate, v_ffn1_w_up=v_ffn1_w_up, v_ffn1_w_down=v_ffn1_w_down, v_mix_norm=v_mix_norm, v_w_in=v_w_in, v_conv_w=v_conv_w, v_conv_b=v_conv_b, v_dt_bias=v_dt_bias, v_a_log=v_a_log, v_d_skip=v_d_skip, v_ssd_norm=v_ssd_norm, v_sgu_ln_g=v_sgu_ln_g, v_sgu_ln_b=v_sgu_ln_b, v_sgu_w=v_sgu_w, v_sgu_b=v_sgu_b, v_w_out=v_w_out, v_ffn2_norm=v_ffn2_norm, v_ffn2_w_gate=v_ffn2_w_gate, v_ffn2_w_up=v_ffn2_w_up, v_ffn2_w_down=v_ffn2_w_down, v_final_norm=v_final_norm)
    T = given["x"].shape[0] * given["x"].shape[1]
    D = given["x"].shape[2]
    x0 = given["x"].reshape(T, D)
    tgt = given["loss_target"].reshape(T, D)
    c = lax.axis_index("c")

    bf = {n: given[n].astype(BF) for n in SHARDED if n not in ("w_in", "conv_w")}
    bf["w_in"] = _pack_w_in(given["w_in"]).astype(BF)
    bf["conv_w"] = given["conv_w"]
    first = [bf[n][0].reshape((2, bf[n].shape[1] // 2) + bf[n].shape[2:]) for n in GROUPS[0][1]]
    gathers = {(0, GROUPS[0][0]): _gather_halves_start(first, "l0_" + GROUPS[0][0])}
    gathers.update({(i, gname): _gather_start([bf[n][i] for n in names], f"l{i}_{gname}")
                    for i in range(DEPTH) for gname, names in GROUPS if (i, gname) not in gathers})
    token = functools.reduce(lambda a, b: a + b, [g.token for g in gathers.values()])

    def gathered(i, gname, after):
        got = gathers[(i, gname)].wait(after)[3:]
        if (i, gname) == (0, GROUPS[0][0]):
            got = _gather_halves_finish(got, "l0_" + gname).wait(after)
            got = [z.reshape((4, 2 * z.shape[2]) + z.shape[3:]) for z in got]
        return got

    def mix_params(i, got):
        win = got[0].reshape(D, W_QKV + W_SSD + W_UV)
        rep = lambda v: jnp.repeat(v, HEAD)[None]
        ssd = (got[1].transpose(1, 0, 2).reshape(4, SSD_CONV_DIM), given["conv_b"][i][None],
               rep(given["dt_bias"][i]), rep(given["a_log"][i]), rep(given["d_skip"][i]), given["ssd_norm"][i][None])
        sgu = (given["sgu_ln_g"][i][None], given["sgu_ln_b"][i][None], given["sgu_w"][i],
               jnp.repeat(given["sgu_b"][i].T, HEAD, axis=1))
        return dict(mix_norm=given["mix_norm"][i][None], w_in=win, w_out=got[2].reshape(-1, D), ssd=ssd, sgu=sgu)

    x = x0
    tape = []
    for i in range(DEPTH):
        P = dict(ffn1=(given["ffn1_norm"][i][None] + (token if i == 0 else 0.0), *gathered(i, "ffn1", x)))
        x, s1 = _ffn_fwd(x, *P["ffn1"])
        P.update(mix_params(i, gathered(i, "mix", x)))
        x, s2 = _mix_fwd(x, P)
        P["ffn2"] = (given["ffn2_norm"][i][None], *gathered(i, "ffn2", x))
        x, s3 = _ffn_fwd(x, *P["ffn2"])
        tape.append((P, s1, s2, s3))
    loss_part, dx, dgf = _final_loss(x, given["final_norm"][None], tgt)

    me = 2 * lax.axis_index("x") + lax.axis_index("y")
    jobs = []

    def rs_begin(i, gname, gd):
        tag = f"l{i}_{gname}"
        names = [n for n in dict(GROUPS)[gname] if n != "conv_w"]
        jobs.append(dict(key=(i, gname), names=names, tag=tag, stage=1,
                         op=_to_sibling_start([_halved(gd[n]) for n in names], tag)))

    def rs_advance(job, after):
        k = len(job["names"])
        if job["stage"] == 1:
            got = job["op"].wait(after)
            job.update(stage=2, op=_to_chips_start([_pair_add(g, l, c) for g, l in zip(got[:k], got[k:])], job["tag"]))
        elif job["stage"] == 2:
            got = job["op"].wait(after)
            job.update(stage=3, op=_join_start([_chip_sum(p, l, me, c) for p, l in zip(got[:k], got[k:])], job["tag"]))
        elif job["stage"] == 3:
            job.update(stage=4, out=dict(zip(job["names"], job["op"].wait(after))))

    def tick(after, begin=None):
        for job in jobs:
            rs_advance(job, after)
        if begin is not None:
            rs_begin(*begin)
        return functools.reduce(lambda a, b: a + b, [j["op"].token for j in jobs if j["stage"] < 4], 0.0)

    grads = [dict() for _ in range(DEPTH)]
    tok = 0.0
    for i in reversed(range(DEPTH)):
        P, s1, s2, s3 = tape[i]
        g = grads[i]
        norm, wg, wu, wd = P["ffn2"]
        dx, dn2, g["ffn2_w_gate"], g["ffn2_w_up"], g["ffn2_w_down"] = _ffn_bwd(dx, s3, norm + tok, wg, wu, wd)
        tok = tick(dx, (i, "ffn2", g))
        dx, gm = _mix_bwd(dx, s2, {**P, "mix_norm": P["mix_norm"] + tok})
        g.update(gm)
        tok = tick(dx, (i, "mix", g))
        norm, wg, wu, wd = P["ffn1"]
        dx, dn1, g["ffn1_w_gate"], g["ffn1_w_up"], g["ffn1_w_down"] = _ffn_bwd(dx, s1, norm + tok, wg, wu, wd)
        tok = tick(dx, (i, "ffn1", g))
        g["ffn1_norm"], g["ffn2_norm"] = dn1[0], dn2[0]
    grad_x = dx.reshape(given["x"].shape)

    order = [n for n in SMALL if n != "final_norm"] + ["final_norm"]
    small = [jnp.stack([grads[i][n] for i in range(DEPTH)]) for n in order[:-1] + ["conv_w"]]
    small = small[:-1] + [dgf[0], small[-1], loss_part[0, 0:1]]
    n_small = sum(s.size for s in small)
    rows_small = -(-n_small // (128 * 8)) * 8

    def flat(arrs):
        fill = rows_small * 128 - sum(a.size for a in arrs)
        return jnp.concatenate([a.reshape(-1) for a in arrs] + [jnp.zeros((fill,), F32)]).reshape(rows_small, 128)

    gsmall = _all_reduce_small(flat(small)).reshape(-1)

    grad_w = {}
    off = 0
    for n in order:
        size = given[n].size
        grad_w[n] = gsmall[off:off + size].reshape(given[n].shape)
        off += size
    cw = gsmall[off:off + 2 * 4 * SSD_CONV_DIM].reshape(DEPTH, 4, SSD_CONV_DIM)
    grad_w["conv_w"] = lax.dynamic_slice_in_dim(cw, me * (SSD_CONV_DIM // 4), SSD_CONV_DIM // 4, axis=2)
    loss = gsmall[off + 2 * 4 * SSD_CONV_DIM]

    delta, new_m, new_v = {}, {}, {}
    shp = given["conv_w"].shape
    d, m2, v2 = _adamw(*[a.reshape(shp[0] * shp[1], shp[2])
                         for a in (given["conv_w"], grad_w["conv_w"], given["m_conv_w"], given["v_conv_w"])])
    delta["conv_w"], new_m["conv_w"], new_v["conv_w"] = d.reshape(shp), m2.reshape(shp), v2.reshape(shp)
    packed = [flat([given[pre + n] for n in order]) for pre in ("", "m_", "v_")]
    small_out = _adamw(packed[0], gsmall.reshape(rows_small, 128), packed[1], packed[2])
    outs = [o.reshape(-1) for o in small_out]
    off = 0
    for n in order:
        size = given[n].size
        for dst, o in zip((delta, new_m, new_v), outs):
            dst[n] = o[off:off + size].reshape(given[n].shape)
        off += size

    stepped, arrived = {}, {}

    def update_arrived(dep):
        out = None
        for job in jobs:
            if job["stage"] == 4 and not job.get("seen"):
                job["seen"] = True
                for n, full in job["out"].items():
                    view = (lambda a: jnp.swapaxes(a, 1, 2)) if n in TRANSPOSED else (lambda a: a)
                    arrived.setdefault(n, {})[job["key"][0]] = full.reshape(view(given[n]).shape[1:])
                    if len(arrived[n]) == DEPTH:
                        res = _adamw_pair(view(given[n]), arrived[n][0], arrived[n][1], view(given["m_" + n]),
                                          view(given["v_" + n]), dep)
                        stepped[n] = [view(r) for r in res]
                        out = res[0]
        return out

    after = small_out[0]
    while any(j["stage"] < 4 for j in jobs):
        done = update_arrived(jnp.zeros((8, 128), F32) + tok)
        after = after if done is None else done
        tok = tick(after)
    update_arrived(jnp.zeros((8, 128), F32) + tok)
    for n, (d, m2, v2, g) in stepped.items():
        delta[n], new_m[n], new_v[n], grad_w[n] = d, m2, v2, g

    return (loss, grad_x, *[grad_w[n] for n in WEIGHTS], *[delta[n] for n in WEIGHTS],
            *[new_m[n] for n in WEIGHTS], *[new_v[n] for n in WEIGHTS])
```

```python
import functools
import math

import jax
import jax.numpy as jnp
from jax import lax
from jax.experimental import pallas as pl
from jax.experimental.pallas import tpu as pltpu

F32 = jnp.float32
BF = jnp.bfloat16

RMS_EPS = 1e-6
LN_EPS = 1e-5
SEQ = 2048
CHUNK = 128
N_CHUNK = SEQ // CHUNK
ATT_W = 384
HEAD = 64
SSD_W = 384
SSD_CONV_DIM = 896
SSD_STATE = 128
SGU_W = 256
DILATIONS = (1, 4, 16)
W_QKV = 3 * ATT_W
W_SSD = SSD_CONV_DIM + SSD_W + SSD_W
W_UV = 2 * SGU_W
ADAM_LR = 0.001
ADAM_B1 = 0.9
ADAM_B2 = 0.999
ADAM_EPS = 1e-08
ADAM_WD = 0.01
ADAM_STEP = 10
NEG = -1e30
ATTN_BWD_VMEM = 48 * 2 ** 20
FFN_VMEM = 60 * 2 ** 20


def _dot(a, b):
    return jnp.dot(a, b, preferred_element_type=F32)


def _dot_nt(a, b):
    return lax.dot_general(a, b, (((1,), (1,)), ((), ())), preferred_element_type=F32)


def _dot_tn(a, b):
    return lax.dot_general(a, b, (((0,), (0,)), ((), ())), preferred_element_type=F32)


def _sigmoid(x):
    return 1.0 / (1.0 + jnp.exp(-x))


def _call(body, *, name, grid, in_specs, out_specs, out_shape, scratch=(), sem=None, vmem=None):
    return pl.pallas_call(
        body, name=name, grid=grid, in_specs=in_specs, out_specs=out_specs, out_shape=out_shape,
        scratch_shapes=list(scratch),
        compiler_params=pltpu.CompilerParams(dimension_semantics=sem, vmem_limit_bytes=vmem),
    )


def _tile(n, want):
    t = min(n, want)
    while n % t:
        t //= 2
    return t


def _final_loss(x, g, tgt):
    T, D = x.shape
    tm = _tile(T, 512)

    def body(x_ref, g_ref, t_ref, l_ref, dx_ref, dg_ref):
        @pl.when(pl.program_id(0) == 0)
        def _():
            dg_ref[...] = jnp.zeros_like(dg_ref)
            l_ref[...] = jnp.zeros_like(l_ref)

        xf = x_ref[...]
        gg = g_ref[...]
        r = lax.rsqrt(jnp.mean(xf * xf, axis=-1, keepdims=True) + RMS_EPS)
        xn = xf * r
        e = xn * gg - t_ref[...]
        part = 0.5 * jnp.sum(jnp.mean(e * e, axis=-1, keepdims=True), axis=0, keepdims=True)
        l_ref[...] += jnp.broadcast_to(part, l_ref.shape)
        dy = e * (1.0 / D)
        u = dy * gg
        mu = jnp.mean(u * xf, axis=-1, keepdims=True)
        dx_ref[...] = r * (u - xf * (r * r * mu))
        dg_ref[...] += jnp.sum(dy * xn, axis=0, keepdims=True)

    row = pl.BlockSpec((tm, D), lambda i: (i, 0))
    vec = pl.BlockSpec((1, D), lambda i: (0, 0))
    lsp = pl.BlockSpec((1, 128), lambda i: (0, 0))
    return _call(body, name="final_loss", grid=(T // tm,), in_specs=[row, vec, row], out_specs=[lsp, row, vec],
                 out_shape=[jax.ShapeDtypeStruct((1, 128), F32), jax.ShapeDtypeStruct((T, D), F32),
                            jax.ShapeDtypeStruct((1, D), F32)],
                 sem=("arbitrary",))(x, g, tgt)


def _slabs(tm, n=2):
    return [slice(k * tm // n, (k + 1) * tm // n) for k in range(n)] if tm % (16 * n) == 0 else [slice(0, tm)]


def _resident(shape):
    return pl.BlockSpec(shape, lambda *_: (0,) * len(shape), pipeline_mode=pl.Buffered(1))


def _ffn_fwd_k(x, gn, wg, wu, wd):
    T, D = x.shape
    NS, _, Fs = wg.shape
    tm = _tile(T, 1024)

    def body(x_ref, gn_ref, wg_ref, wu_ref, wd_ref, o_ref, h_ref, s1_ref, s2_ref, a_ref, hs, acc):
        j = pl.program_id(1)

        @pl.when(j == 0)
        def _():
            xf = x_ref[...]
            r = lax.rsqrt(jnp.mean(xf * xf, axis=-1, keepdims=True) + RMS_EPS)
            hs[...] = (xf * r * gn_ref[...]).astype(BF)
            h_ref[...] = hs[...]
            acc[...] = jnp.zeros_like(acc)

        h = hs[...]
        g = _dot(h, wg_ref[...])
        u = _dot(h, wu_ref[...])
        sg = _sigmoid(g)
        s1 = g * sg
        a = (s1 * u).astype(BF)
        s1_ref[...] = s1.astype(BF)
        s2_ref[...] = (u * (sg * (1.0 + g * (1.0 - sg)))).astype(BF)
        a_ref[...] = a
        acc[...] += _dot(a, wd_ref[...])

        @pl.when(j == NS - 1)
        def _():
            o_ref[...] = x_ref[...] + 0.5 * acc[...]

    row = pl.BlockSpec((tm, D), lambda i, j: (i, 0))
    act = pl.BlockSpec((None, tm, Fs), lambda i, j: (j, i, 0))
    sh = jax.ShapeDtypeStruct((NS, T, Fs), BF)
    wspec = lambda w: pl.BlockSpec((None,) + w.shape[1:], lambda i, j: (j, 0, 0))
    return _call(body, name="ffn_fwd", grid=(T // tm, NS),
                 in_specs=[row, pl.BlockSpec((1, D), lambda i, j: (0, 0)), wspec(wg), wspec(wu), wspec(wd)],
                 out_specs=[row, row, act, act, act],
                 out_shape=[jax.ShapeDtypeStruct((T, D), F32), jax.ShapeDtypeStruct((T, D), BF), sh, sh, sh],
                 scratch=[pltpu.VMEM((tm, D), BF), pltpu.VMEM((tm, D), F32)],
                 sem=("parallel", "arbitrary"), vmem=FFN_VMEM)(x, gn, wg, wu, wd)


def _ffn_bwd_k1(dxo, x, gn, s1, s2, wg, wu, wd):
    NS, T, Fs = s1.shape
    D = x.shape[1]
    tm = _tile(T, 1024)

    def body(dxo_ref, x_ref, gn_ref, s1_ref, s2_ref, wg_ref, wu_ref, wd_ref,
             dx_ref, dgn_ref, dg_ref, du_ref, dy_ref, dys, acc):
        i, j = pl.program_id(0), pl.program_id(1)

        @pl.when((i == 0) & (j == 0))
        def _():
            dgn_ref[...] = jnp.zeros_like(dgn_ref)

        @pl.when(j == 0)
        def _():
            dys[...] = (0.5 * dxo_ref[...]).astype(BF)
            dy_ref[...] = dys[...]
            acc[...] = jnp.zeros_like(acc)

        for rows in _slabs(tm):
            da = _dot_nt(dys[rows, :], wd_ref[...])
            dg = (da * s2_ref[rows, :].astype(F32)).astype(BF)
            du = (da * s1_ref[rows, :].astype(F32)).astype(BF)
            dg_ref[rows, :] = dg
            du_ref[rows, :] = du
            acc[rows, :] += _dot_nt(dg, wg_ref[...]) + _dot_nt(du, wu_ref[...])

        @pl.when(j == NS - 1)
        def _():
            xf = x_ref[...]
            r = lax.rsqrt(jnp.mean(xf * xf, axis=-1, keepdims=True) + RMS_EPS)
            dh = acc[...]
            uu = dh * gn_ref[...]
            mu = jnp.mean(uu * xf, axis=-1, keepdims=True)
            dx_ref[...] = dxo_ref[...] + r * (uu - xf * (r * r * mu))
            dgn_ref[...] += jnp.sum(dh * xf * r, axis=0, keepdims=True)

    row = pl.BlockSpec((tm, D), lambda i, j: (i, 0))
    vec = pl.BlockSpec((1, D), lambda i, j: (0, 0))
    act = pl.BlockSpec((None, tm, Fs), lambda i, j: (j, i, 0))
    sh = jax.ShapeDtypeStruct((NS, T, Fs), BF)
    wspec = lambda w: pl.BlockSpec((None,) + w.shape[1:], lambda i, j: (j, 0, 0))
    once = pl.BlockSpec((tm, D), lambda i, j: (i, 0), pipeline_mode=pl.Buffered(1))
    return _call(body, name="ffn_bwd_x", grid=(T // tm, NS),
                 in_specs=[once, once, vec, act, act, wspec(wg), wspec(wu), wspec(wd)],
                 out_specs=[row, vec, act, act, row],
                 out_shape=[jax.ShapeDtypeStruct((T, D), F32), jax.ShapeDtypeStruct((1, D), F32), sh, sh,
                            jax.ShapeDtypeStruct((T, D), BF)],
                 scratch=[pltpu.VMEM((tm, D), BF), pltpu.VMEM((tm, D), F32)],
                 sem=("arbitrary", "arbitrary"), vmem=FFN_VMEM)(dxo, x, gn, s1, s2, wg, wu, wd)


def _ffn_bwd_k2(hb, dyb, a, dg, du):
    NS, T, Fs = a.shape
    D = hb.shape[1]
    tk = _tile(T, 1024)

    def body(h_ref, dy_ref, a_ref, dg_ref, du_ref, og_ref, ou_ref, od_ref):
        @pl.when(pl.program_id(1) == 0)
        def _():
            og_ref[...] = jnp.zeros_like(og_ref)
            ou_ref[...] = jnp.zeros_like(ou_ref)
            od_ref[...] = jnp.zeros_like(od_ref)

        h = h_ref[...]
        og_ref[...] += _dot_tn(dg_ref[...], h)
        ou_ref[...] += _dot_tn(du_ref[...], h)
        od_ref[...] += _dot_tn(a_ref[...], dy_ref[...])

    row = pl.BlockSpec((tk, D), lambda j, k: (k, 0))
    act = pl.BlockSpec((None, tk, Fs), lambda j, k: (j, k, 0))
    return _call(body, name="ffn_bwd_w", grid=(NS, T // tk), in_specs=[row, row, act, act, act],
                 out_specs=[pl.BlockSpec((None, Fs, D), lambda j, k: (j, 0, 0))] * 3,
                 out_shape=[jax.ShapeDtypeStruct((NS, Fs, D), F32)] * 3,
                 sem=("parallel", "arbitrary"))(hb, dyb, a, dg, du)


def _mm_nn(a, b, res=None, out_dtype=F32):
    T, K = a.shape
    N = b.shape[1]
    tm = _tile(T, 512)
    tn = N if N <= 2048 else _tile(N, 1024)

    def body(*refs):
        if res is None:
            a_ref, b_ref, o_ref = refs
            o_ref[...] = _dot(a_ref[...], b_ref[...]).astype(out_dtype)
        else:
            a_ref, b_ref, r_ref, o_ref = refs
            o_ref[...] = (r_ref[...] + _dot(a_ref[...], b_ref[...])).astype(out_dtype)

    o = pl.BlockSpec((tm, tn), lambda i, j: (i, j))
    ins = [pl.BlockSpec((tm, K), lambda i, j: (i, 0)), pl.BlockSpec((K, tn), lambda i, j: (0, j))]
    args = [a, b]
    if res is not None:
        ins.append(o)
        args.append(res)
    return _call(body, name="mm_nn", grid=(T // tm, N // tn), in_specs=ins, out_specs=o,
                 out_shape=jax.ShapeDtypeStruct((T, N), out_dtype), sem=("parallel", "parallel"))(*args)


def _mm_nt(a, b, res=None):
    T, K = a.shape
    N = b.shape[0]
    tm = _tile(T, 512)

    def body(*refs):
        if res is None:
            a_ref, b_ref, o_ref = refs
            o_ref[...] = _dot_nt(a_ref[...].astype(BF), b_ref[...])
        else:
            a_ref, b_ref, r_ref, o_ref = refs
            o_ref[...] = r_ref[...] + _dot_nt(a_ref[...].astype(BF), b_ref[...])

    o = pl.BlockSpec((tm, N), lambda i: (i, 0))
    ins = [pl.BlockSpec((tm, K), lambda i: (i, 0)), pl.BlockSpec((N, K), lambda i: (0, 0))]
    args = [a, b]
    if res is not None:
        ins.append(o)
        args.append(res)
    return _call(body, name="mm_nt", grid=(T // tm,), in_specs=ins, out_specs=o,
                 out_shape=jax.ShapeDtypeStruct((T, N), F32), sem=("parallel",))(*args)


def _mm_tn(a, b):
    T, M = a.shape
    N = b.shape[1]
    tk = _tile(T, 1024)
    tmm = _tile(M, 512)

    def body(a_ref, b_ref, o_ref):
        @pl.when(pl.program_id(1) == 0)
        def _():
            o_ref[...] = jnp.zeros_like(o_ref)

        o_ref[...] += _dot_tn(a_ref[...].astype(BF), b_ref[...].astype(BF))

    return _call(body, name="mm_tn", grid=(M // tmm, T // tk),
                 in_specs=[pl.BlockSpec((tk, tmm), lambda i, k: (k, i)), pl.BlockSpec((tk, N), lambda i, k: (k, 0))],
                 out_specs=pl.BlockSpec((tmm, N), lambda i, k: (i, 0)),
                 out_shape=jax.ShapeDtypeStruct((M, N), F32), sem=("parallel", "arbitrary"))(a, b)


def _mix_proj(x, gn, win):
    T, D = x.shape
    tm = _tile(T, 512)
    cuts = (0, W_QKV, W_QKV + W_SSD, W_QKV + W_SSD + W_UV)

    def body(x_ref, gn_ref, w_ref, h_ref, q_ref, s_ref, u_ref):
        xf = x_ref[...]
        r = lax.rsqrt(jnp.mean(xf * xf, axis=-1, keepdims=True) + RMS_EPS)
        h = (xf * r * gn_ref[...]).astype(BF)
        h_ref[...] = h
        for o_ref, lo, hi in zip((q_ref, s_ref, u_ref), cuts[:-1], cuts[1:]):
            o_ref[...] = _dot(h, w_ref[:, lo:hi])

    row = lambda w: pl.BlockSpec((tm, w), lambda i: (i, 0))
    return _call(body, name="mix_proj", grid=(T // tm,),
                 in_specs=[row(D), pl.BlockSpec((1, D), lambda i: (0, 0)), _resident(win.shape)],
                 out_specs=[row(D), row(W_QKV), row(W_SSD), row(W_UV)],
                 out_shape=[jax.ShapeDtypeStruct((T, D), BF), jax.ShapeDtypeStruct((T, W_QKV), F32),
                            jax.ShapeDtypeStruct((T, W_SSD), F32), jax.ShapeDtypeStruct((T, W_UV), F32)],
                 sem=("parallel",))(x, gn, win)


def _mix_bwd_dx(dqkv, dsin, duv, win, x, gn, dxo):
    T, D = x.shape
    tm = _tile(T, 512)
    cuts = (0, W_QKV, W_QKV + W_SSD, W_QKV + W_SSD + W_UV)

    def body(dq_ref, ds_ref, du_ref, w_ref, x_ref, gn_ref, dxo_ref, dx_ref, dgn_ref):
        @pl.when(pl.program_id(0) == 0)
        def _():
            dgn_ref[...] = jnp.zeros_like(dgn_ref)

        dh = (_dot_nt(dq_ref[...], w_ref[:, cuts[0]:cuts[1]]) + _dot_nt(ds_ref[...], w_ref[:, cuts[1]:cuts[2]])
              + _dot_nt(du_ref[...], w_ref[:, cuts[2]:cuts[3]]))
        xf = x_ref[...]
        r = lax.rsqrt(jnp.mean(xf * xf, axis=-1, keepdims=True) + RMS_EPS)
        uu = dh * gn_ref[...]
        mu = jnp.mean(uu * xf, axis=-1, keepdims=True)
        dx_ref[...] = dxo_ref[...] + r * (uu - xf * (r * r * mu))
        dgn_ref[...] += jnp.sum(dh * xf * r, axis=0, keepdims=True)

    row = lambda w: pl.BlockSpec((tm, w), lambda i: (i, 0))
    vec = pl.BlockSpec((1, D), lambda i: (0, 0))
    return _call(body, name="mix_bwd_dx", grid=(T // tm,),
                 in_specs=[row(W_QKV), row(W_SSD), row(W_UV), _resident(win.shape), row(D), vec, row(D)],
                 out_specs=[row(D), vec],
                 out_shape=[jax.ShapeDtypeStruct((T, D), F32), jax.ShapeDtypeStruct((1, D), F32)],
                 sem=("arbitrary",))(dqkv, dsin, duv, win, x, gn, dxo)


def _lane_mask(e, width=128):
    return (lax.broadcasted_iota(jnp.int32, (1, width), 1) // HEAD) == e


def _band_mask(n):
    qi = lax.broadcasted_iota(jnp.int32, (CHUNK, 2 * CHUNK), 0)
    kj = lax.broadcasted_iota(jnp.int32, (CHUNK, 2 * CHUNK), 1)
    dist = qi + CHUNK - kj
    return (dist >= 0) & (dist <= CHUNK) & ((kj >= CHUNK) | (n > 0))


def _sub_rows(r, block, dil):
    if dil == 1:
        return pl.ds(pl.multiple_of(block * CHUNK, CHUNK), CHUNK)
    return pl.ds(r + dil * CHUNK * block, CHUNK, stride=dil)


def _attn_specs(T, dil):
    B, nb = T // SEQ, SEQ // (CHUNK * dil)
    once = dict(pipeline_mode=pl.Buffered(1))
    q_like = lambda col: pl.BlockSpec((CHUNK * dil, 128), lambda b, n, r: (b * nb + n, col), **(once if nb == 1 else {}))
    k_like = lambda col: pl.BlockSpec((SEQ, 128), lambda b, n, r: (b, col), **once)
    return B, nb, q_like, k_like


def _attn_fwd(qkv, dil):
    T = qkv.shape[0]
    B, nb, q_like, k_like = _attn_specs(T, dil)
    scale = HEAD ** -0.5

    def body(*refs):
        q_t, k_t, v_t, o_t, l_t = refs[0:3], refs[3:6], refs[6:9], refs[9:12], refs[12:15]
        n, r = pl.program_id(1), pl.program_id(2)
        mine = _sub_rows(r, 0, dil)
        cur, prv = _sub_rows(r, n, dil), _sub_rows(r, jnp.maximum(n - 1, 0), dil)
        mask = _band_mask(n)
        for t in range(3):
            qt = q_t[t][mine, :].astype(BF)
            kt = jnp.concatenate([k_t[t][prv, :], k_t[t][cur, :]], axis=0).astype(BF)
            vt = jnp.concatenate([v_t[t][prv, :], v_t[t][cur, :]], axis=0).astype(BF)
            o_pair = jnp.zeros((CHUNK, 128), F32)
            l_pair = jnp.zeros((CHUNK, 128), F32)
            for e in range(2):
                lm = _lane_mask(e)
                s = _dot_nt(jnp.where(lm, qt, jnp.zeros_like(qt)), kt) * scale
                s = jnp.where(mask, s, NEG)
                m = jnp.max(s, axis=-1, keepdims=True)
                p = jnp.exp(s - m)
                den = jnp.sum(p, axis=-1, keepdims=True)
                o = _dot(p.astype(BF), vt) / den
                o_pair = jnp.where(lm, o, o_pair)
                l_pair = jnp.where(lm, m + jnp.log(den), l_pair)
            o_t[t][mine, :] = o_pair
            l_t[t][mine, :] = l_pair

    out_spec = pl.BlockSpec((CHUNK * dil, 128), lambda b, n, r: (b * nb + n, 0))
    sh = jax.ShapeDtypeStruct((T, 128), F32)
    outs = _call(
        body, name=f"attn_fwd_d{dil}", grid=(B, nb, dil),
        in_specs=[q_like(t) for t in range(3)] + [k_like(3 + t) for t in range(3)] + [k_like(6 + t) for t in range(3)],
        out_specs=[out_spec] * 6, out_shape=[sh] * 6, sem=("parallel", "arbitrary", "arbitrary"))(*([qkv] * 9))
    return list(outs[0:3]), list(outs[3:6])


def _attn_combine(branches):
    T = branches[0][0][0].shape[0]
    tm = _tile(T, 512)

    def body(*refs):
        y_ref, l_ref = refs[-2:]
        for t in range(3):
            o = [refs[6 * i + t][...] for i in range(3)]
            a, b, c = [refs[6 * i + 3 + t][...] for i in range(3)]
            m = jnp.maximum(jnp.maximum(a, b), c)
            ea, eb, ec = jnp.exp(a - m), jnp.exp(b - m), jnp.exp(c - m)
            z = ea + eb + ec
            y_ref[:, 128 * t:128 * (t + 1)] = (ea * o[0] + eb * o[1] + ec * o[2]) / z
            l_ref[:, 128 * t:128 * (t + 1)] = m + jnp.log(z)

    tile = pl.BlockSpec((tm, 128), lambda i: (i, 0))
    row = pl.BlockSpec((tm, ATT_W), lambda i: (i, 0))
    sh = jax.ShapeDtypeStruct((T, ATT_W), F32)
    flat = [a for o_t, l_t in branches for a in (*o_t, *l_t)]
    return _call(body, name="attn_combine", grid=(T // tm,), in_specs=[tile] * 18, out_specs=[row, row],
                 out_shape=[sh, sh], sem=("parallel",))(*flat)


def _attn_bwd(qkv, do, out, lse, dil):
    T = qkv.shape[0]
    B, nb, q_like, k_like = _attn_specs(T, dil)
    scale = HEAD ** -0.5

    def body(*refs):
        q_t, k_t, v_t = refs[0:3], refs[3:6], refs[6:9]
        do_t, out_t, lse_t = refs[9:12], refs[12:15], refs[15:18]
        dq_t, dk_t, dv_t = refs[18:21], refs[21:24], refs[24:27]
        n, r = pl.program_id(1), pl.program_id(2)

        @pl.when((n == 0) & (r == 0))
        def _():
            for t in range(3):
                dk_t[t][...] = jnp.zeros_like(dk_t[t])
                dv_t[t][...] = jnp.zeros_like(dv_t[t])

        mine = _sub_rows(r, 0, dil)
        cur, prv = _sub_rows(r, n, dil), _sub_rows(r, jnp.maximum(n - 1, 0), dil)
        mask = _band_mask(n)
        for t in range(3):
            qt = q_t[t][mine, :].astype(BF)
            kt = jnp.concatenate([k_t[t][prv, :], k_t[t][cur, :]], axis=0).astype(BF)
            vt = jnp.concatenate([v_t[t][prv, :], v_t[t][cur, :]], axis=0).astype(BF)
            do_ = do_t[t][mine, :]
            dlt = do_ * out_t[t][mine, :]
            ls = lse_t[t][mine, :]
            dq_pair = jnp.zeros((CHUNK, 128), F32)
            dk_acc = jnp.zeros((2 * CHUNK, 128), F32)
            dv_acc = jnp.zeros((2 * CHUNK, 128), F32)
            for e in range(2):
                lm = _lane_mask(e)
                qm = jnp.where(lm, qt, jnp.zeros_like(qt))
                s = _dot_nt(qm, kt) * scale
                p = jnp.exp(jnp.where(mask, s - ls[:, HEAD * e:HEAD * e + 1], NEG))
                dom = jnp.where(lm, do_, 0.0).astype(BF)
                dv_acc += _dot_tn(p.astype(BF), dom)
                dp = _dot_nt(dom, vt)
                delta = jnp.sum(jnp.where(lm, dlt, 0.0), axis=-1, keepdims=True)
                ds = (p * (dp - delta) * scale).astype(BF)
                dq_pair += jnp.where(lm, _dot(ds, kt), 0.0)
                dk_acc += _dot_tn(ds, qm)
            dq_t[t][mine, :] = dq_pair
            dk_t[t][cur, :] = dk_t[t][cur, :] + dk_acc[CHUNK:]
            dk_t[t][prv, :] = dk_t[t][prv, :] + dk_acc[:CHUNK]
            dv_t[t][cur, :] = dv_t[t][cur, :] + dv_acc[CHUNK:]
            dv_t[t][prv, :] = dv_t[t][prv, :] + dv_acc[:CHUNK]

    q_out = pl.BlockSpec((CHUNK * dil, 128), lambda b, n, r: (b * nb + n, 0))
    k_out = pl.BlockSpec((SEQ, 128), lambda b, n, r: (b, 0))
    sh = jax.ShapeDtypeStruct((T, 128), F32)
    tiles = lambda: [q_like(t) for t in range(3)]
    return list(_call(
        body, name=f"attn_bwd_d{dil}", grid=(B, nb, dil),
        in_specs=tiles() + [k_like(3 + t) for t in range(3)] + [k_like(6 + t) for t in range(3)]
        + tiles() + tiles() + tiles(),
        out_specs=[q_out] * 3 + [k_out] * 6, out_shape=[sh] * 9,
        sem=("parallel", "arbitrary", "arbitrary"), vmem=ATTN_BWD_VMEM)(*([qkv] * 9 + [do] * 3 + [out] * 3 + [lse] * 3)))


def _sum_branches(parts):
    T = parts[0][0].shape[0]
    tm = _tile(T, 512)

    def body(*refs):
        o_ref = refs[-1]
        for c in range(9):
            acc = refs[c][...] + refs[9 + c][...] + refs[18 + c][...]
            o_ref[:, 128 * c:128 * (c + 1)] = acc.astype(BF)

    tile = pl.BlockSpec((tm, 128), lambda i: (i, 0))
    flat = [a for br in parts for a in br]
    return _call(body, name="attn_sum_branches", grid=(T // tm,), in_specs=[tile] * 27,
                 out_specs=pl.BlockSpec((tm, W_QKV), lambda i: (i, 0)),
                 out_shape=jax.ShapeDtypeStruct((T, W_QKV), BF), sem=("parallel",))(*flat)


def _silu(x):
    return x * _sigmoid(x)


def _dsilu(x):
    s = _sigmoid(x)
    return s * (1.0 + x * (1.0 - s))


def _log1p(u):
    return jnp.where(u < 0.01, u * (1.0 - u * (0.5 - u * (1.0 / 3.0))), jnp.log(1.0 + u))


def _softplus(x):
    return jnp.maximum(x, 0.0) + _log1p(jnp.exp(-jnp.abs(x)))


def _cumsum_rows(x, reverse=False):
    n = x.shape[0]
    rows = lax.broadcasted_iota(jnp.int32, x.shape, 0)
    k = 1
    while k < n:
        if reverse:
            x = x + jnp.where(rows < n - k, pltpu.roll(x, n - k, 0), 0.0)
        else:
            x = x + jnp.where(rows >= k, pltpu.roll(x, k, 0), 0.0)
        k *= 2
    return x


def _tri():
    r = lax.broadcasted_iota(jnp.int32, (CHUNK, CHUNK), 0)
    c = lax.broadcasted_iota(jnp.int32, (CHUNK, CHUNK), 1)
    return r >= c


def _row_mask(e):
    return (lax.broadcasted_iota(jnp.int32, (128, 1), 0) // HEAD) == e


def _first_lane(e):
    return lax.broadcasted_iota(jnp.int32, (1, 128), 1) == HEAD * e


def _ssd_pre(x_ref, halo_ref, first, cw_ref, cb_ref, dtb_ref, al_ref, ext):
    row = x_ref[...]
    z = row[:, SSD_CONV_DIM:SSD_CONV_DIM + SSD_W]
    u = row[:, SSD_CONV_DIM + SSD_W:] + dtb_ref[...]
    ext[0:8, :] = jnp.where(first, 0.0, halo_ref[:, 0:SSD_CONV_DIM])
    ext[8:8 + CHUNK, :] = row[:, 0:SSD_CONV_DIM]
    xc = cb_ref[...]
    for j in range(4):
        xc = xc + cw_ref[j:j + 1, :] * ext[pl.ds(5 + j, CHUNK), :]
    xa = _silu(xc)
    dt = _softplus(u)
    a = dt * (-jnp.exp(al_ref[...]))
    A = _cumsum_rows(a)
    return dict(z=z, u=u, xc=xc, xs=xa[:, 0:SSD_W], Bm=xa[:, SSD_W:SSD_W + 256], Cm=xa[:, SSD_W + 256:],
                dt=dt, a=a, A=A, AT=A.T, eA=jnp.exp(A), wdec=jnp.exp(A[CHUNK - 1:CHUNK, :] - A),
                dtot=jnp.exp(A[CHUNK - 1:CHUNK, :]))


def _ssd_y(p, hp_ref, dskip):
    tri = _tri()
    X = p["xs"] * p["dt"]
    Bb = [p["Bm"][:, 128 * g:128 * (g + 1)].astype(BF) for g in range(2)]
    Cb = [p["Cm"][:, 128 * g:128 * (g + 1)].astype(BF) for g in range(2)]
    CB = [_dot_nt(Cb[g], Bb[g]) for g in range(2)]
    tiles = []
    for t in range(3):
        sl = slice(128 * t, 128 * (t + 1))
        hpb = hp_ref[sl, :].astype(BF)
        acc = jnp.zeros((CHUNK, 128), F32)
        for e in range(2):
            h = 2 * t + e
            g, col = h // 3, HEAD * h
            lm = _lane_mask(e)
            L = jnp.exp(jnp.where(tri, p["A"][:, col:col + 1] - p["AT"][col:col + 1, :], NEG))
            yd = _dot((CB[g] * L).astype(BF), jnp.where(lm, X[:, sl], 0.0).astype(BF))
            yo = _dot_nt(Cb[g], hpb) * p["eA"][:, sl]
            acc = acc + jnp.where(lm, yd + yo, 0.0)
        tiles.append(acc)
    return jnp.concatenate(tiles, axis=1) + dskip * p["xs"], X, Bb, Cb, CB


def _group_stats(v):
    g0 = lax.broadcasted_iota(jnp.int32, (1, SSD_W), 1) < SSD_W // 2
    m0 = jnp.sum(jnp.where(g0, v, 0.0), axis=-1, keepdims=True) * (2.0 / SSD_W)
    m1 = jnp.sum(jnp.where(g0, 0.0, v), axis=-1, keepdims=True) * (2.0 / SSD_W)
    return jnp.where(g0, m0, m1)


def _ssd_specs(T, rev):
    B = T // SEQ

    def chunk(b, c):
        return b * N_CHUNK + (N_CHUNK - 1 - c if rev else c)

    row = pl.BlockSpec((CHUNK, W_SSD), lambda b, c: (chunk(b, c), 0))
    halo = pl.BlockSpec((8, W_SSD), lambda b, c: (jnp.maximum(chunk(b, c) * (CHUNK // 8) - 1, 0), 0))
    hp = pl.BlockSpec((None, SSD_W, SSD_STATE), lambda b, c: (chunk(b, c), 0, 0))
    y = pl.BlockSpec((CHUNK, SSD_W), lambda b, c: (chunk(b, c), 0))
    const = lambda r, w: pl.BlockSpec((r, w), lambda b, c: (0, 0))
    params = [const(4, SSD_CONV_DIM), const(1, SSD_CONV_DIM)] + [const(1, SSD_W)] * 4
    return B, row, halo, hp, y, const, params


def _ssd_fwd(sin, conv_w, conv_b, dtb, alog, dskip, norm_g):
    T = sin.shape[0]
    B, row, halo, hp, y, const, params = _ssd_specs(T, False)

    def body(x_ref, halo_ref, cw_ref, cb_ref, dtb_ref, al_ref, dk_ref, ng_ref, y_ref, hp_ref, ext, hst):
        c = pl.program_id(1)

        @pl.when(c == 0)
        def _():
            hst[...] = jnp.zeros_like(hst)

        p = _ssd_pre(x_ref, halo_ref, c == 0, cw_ref, cb_ref, dtb_ref, al_ref, ext)
        yv, X, Bb, Cb, CB = _ssd_y(p, hst, dk_ref[...])
        hp_ref[...] = hst[...]
        for t in range(3):
            sl = slice(128 * t, 128 * (t + 1))
            old = hst[sl, :]
            new = old
            for e in range(2):
                h = 2 * t + e
                g, col = h // 3, HEAD * h
                st = _dot_tn(jnp.where(_lane_mask(e), X[:, sl] * p["wdec"][:, sl], 0.0).astype(BF), Bb[g])
                new = jnp.where(_row_mask(e), old * p["dtot"][:, col:col + 1] + st, new)
            hst[sl, :] = new
        y2 = yv * _silu(p["z"])
        r = lax.rsqrt(_group_stats(y2 * y2) + RMS_EPS)
        y_ref[...] = y2 * r * ng_ref[...]

    return _call(body, name="ssd_fwd", grid=(B, N_CHUNK), in_specs=[row, halo] + params, out_specs=[y, hp],
                 out_shape=[jax.ShapeDtypeStruct((T, SSD_W), F32),
                            jax.ShapeDtypeStruct((T // CHUNK, SSD_W, SSD_STATE), F32)],
                 scratch=[pltpu.VMEM((8 + CHUNK, SSD_CONV_DIM), F32), pltpu.VMEM((SSD_W, SSD_STATE), F32)],
                 sem=("parallel", "arbitrary"))(sin, sin, conv_w, conv_b, dtb, alog, dskip, norm_g)


def _ssd_bwd(sin, hprev, dy3, conv_w, conv_b, dtb, alog, dskip, norm_g):
    T = sin.shape[0]
    B, row, halo, hp, y, const, params = _ssd_specs(T, True)

    def body(x_ref, halo_ref, hp_ref, dy_ref, cw_ref, cb_ref, dtb_ref, al_ref, dk_ref, ng_ref,
             dx_ref, dcw_ref, dcb_ref, dvec_ref, ext, ext2, dh):
        c = pl.program_id(1)

        @pl.when((pl.program_id(0) == 0) & (c == 0))
        def _():
            dcw_ref[...] = jnp.zeros_like(dcw_ref)
            dcb_ref[...] = jnp.zeros_like(dcb_ref)
            dvec_ref[...] = jnp.zeros_like(dvec_ref)

        @pl.when(c == 0)
        def _():
            dh[...] = jnp.zeros_like(dh)
            ext2[CHUNK:CHUNK + 8, :] = jnp.zeros((8, SSD_CONV_DIM), F32)

        p = _ssd_pre(x_ref, halo_ref, c == N_CHUNK - 1, cw_ref, cb_ref, dtb_ref, al_ref, ext)
        dskip_ = dk_ref[...]
        yv, X, Bb, Cb, CB = _ssd_y(p, hp_ref, dskip_)
        xs, z, A, AT = p["xs"], p["z"], p["A"], p["AT"]

        sz = _silu(z)
        y2 = yv * sz
        r = lax.rsqrt(_group_stats(y2 * y2) + RMS_EPS)
        dy3_ = dy_ref[...]
        uu = dy3_ * ng_ref[...]
        dy2 = r * (uu - y2 * (r * r * _group_stats(uu * y2)))
        dy = dy2 * sz
        dz = dy2 * yv * _dsilu(z)

        tri = _tri()
        rows = lax.broadcasted_iota(jnp.int32, (CHUNK, 1), 0)
        dG = [jnp.zeros((CHUNK, CHUNK), F32) for _ in range(2)]
        dB = [jnp.zeros((CHUNK, SSD_STATE), F32) for _ in range(2)]
        dC = [jnp.zeros((CHUNK, SSD_STATE), F32) for _ in range(2)]
        dX_t, dA_t, ddtx_t = [], [], []
        for t in range(3):
            sl = slice(128 * t, 128 * (t + 1))
            hp_t = hp_ref[sl, :]
            hpb = hp_t.astype(BF)
            dhc = dh[sl, :]
            dh_new = jnp.zeros((128, SSD_STATE), F32)
            dX = jnp.zeros((CHUNK, 128), F32)
            dA = jnp.zeros((CHUNK, 128), F32)
            ddtx = jnp.zeros((CHUNK, 128), F32)
            for e in range(2):
                h = 2 * t + e
                g, col = h // 3, HEAD * h
                lm, rm, fl = _lane_mask(e), _row_mask(e), _first_lane(e)
                L = jnp.exp(jnp.where(tri, A[:, col:col + 1] - AT[col:col + 1, :], NEG))
                Mf = CB[g] * L
                Xm = jnp.where(lm, X[:, sl], 0.0)
                Xmb = Xm.astype(BF)
                dyh = jnp.where(lm, dy[:, sl], 0.0)
                dyb = dyh.astype(BF)
                dXh = _dot_tn(Mf.astype(BF), dyb)
                dM = jnp.where(tri, _dot_nt(dyb, Xmb), 0.0)
                Wm = dM * Mf
                dAc = jnp.sum(Wm, axis=-1, keepdims=True) - jnp.sum(Wm.T, axis=-1, keepdims=True)
                dG[g] = dG[g] + dM * L
                eAt = p["eA"][:, sl]
                yo = _dot_nt(Cb[g], hpb)
                dyo = (dyh * eAt).astype(BF)
                dC[g] = dC[g] + _dot(dyo, hpb)
                dh_new = dh_new + _dot_tn(dyo, Cb[g])
                dAc = dAc + jnp.sum(dyh * yo * eAt, axis=-1, keepdims=True)
                dHn = jnp.where(rm, dhc, 0.0)
                dHnb = dHn.astype(BF)
                dec = p["dtot"][:, col:col + 1]
                dh_new = dh_new + dec * dHn
                Z = _dot_nt(Bb[g], dHnb)
                wt = p["wdec"][:, sl]
                xi = jnp.sum(Xm * Z, axis=-1, keepdims=True) * p["wdec"][:, col:col + 1]
                dXh = dXh + wt * Z
                dB[g] = dB[g] + _dot(jnp.where(lm, X[:, sl] * wt, 0.0).astype(BF), dHnb)
                dAtot = jnp.sum(xi, axis=0, keepdims=True) + dec * jnp.sum(
                    jnp.sum(dHn * hp_t, axis=-1, keepdims=True), axis=0, keepdims=True)
                dAc = dAc - xi + jnp.where(rows == CHUNK - 1, dAtot, 0.0)
                dA = dA + jnp.where(fl, dAc, 0.0)
                dX = dX + dXh
                ddtx = ddtx + jnp.where(fl, jnp.sum(dXh * xs[:, sl], axis=-1, keepdims=True), 0.0)
            dh[sl, :] = dh_new
            dX_t.append(dX)
            dA_t.append(dA)
            ddtx_t.append(ddtx)
        for g in range(2):
            dGb = dG[g].astype(BF)
            dC[g] = dC[g] + _dot(dGb, Bb[g])
            dB[g] = dB[g] + _dot_tn(dGb, Cb[g])
        dXf = jnp.concatenate(dX_t, axis=1)
        da = _cumsum_rows(jnp.concatenate(dA_t, axis=1), reverse=True)
        ddt = da * (-jnp.exp(al_ref[...])) + jnp.concatenate(ddtx_t, axis=1)
        du = ddt * _sigmoid(p["u"])
        dxs = dXf * p["dt"] + dskip_ * dy
        dxc = jnp.concatenate([dxs, dB[0], dB[1], dC[0], dC[1]], axis=1) * _dsilu(p["xc"])
        ext2[0:CHUNK, :] = dxc
        dxbc = jnp.zeros((CHUNK, SSD_CONV_DIM), F32)
        for j in range(4):
            dxbc = dxbc + cw_ref[j:j + 1, :] * ext2[pl.ds(3 - j, CHUNK), :]
            dcw_ref[j:j + 1, :] += jnp.sum(dxc * ext[pl.ds(5 + j, CHUNK), :], axis=0, keepdims=True)
        ext2[CHUNK:CHUNK + 8, :] = dxc[0:8, :]
        dcb_ref[...] += jnp.sum(dxc, axis=0, keepdims=True)
        dvec_ref[0:1, :] += jnp.sum(du, axis=0, keepdims=True)
        dvec_ref[1:2, :] += jnp.sum(da * p["a"], axis=0, keepdims=True)
        dvec_ref[2:3, :] += jnp.sum(dy * xs, axis=0, keepdims=True)
        dvec_ref[3:4, :] += jnp.sum(dy3_ * y2 * r, axis=0, keepdims=True)
        dx_ref[...] = jnp.concatenate([dxbc, dz, du], axis=1).astype(BF)

    return _call(body, name="ssd_bwd", grid=(B, N_CHUNK), in_specs=[row, halo, hp, y] + params,
                 out_specs=[row, const(4, SSD_CONV_DIM), const(1, SSD_CONV_DIM), const(8, SSD_W)],
                 out_shape=[jax.ShapeDtypeStruct((T, W_SSD), BF), jax.ShapeDtypeStruct((4, SSD_CONV_DIM), F32),
                            jax.ShapeDtypeStruct((1, SSD_CONV_DIM), F32), jax.ShapeDtypeStruct((8, SSD_W), F32)],
                 scratch=[pltpu.VMEM((8 + CHUNK, SSD_CONV_DIM), F32), pltpu.VMEM((8 + CHUNK, SSD_CONV_DIM), F32),
                          pltpu.VMEM((SSD_W, SSD_STATE), F32)],
                 sem=("arbitrary", "arbitrary"))(sin, sin, hprev, dy3, conv_w, conv_b, dtb, alog, dskip, norm_g)


def _sgu_core(uv_ref, g_ref, b_ref, w_ref, bias_ref):
    x = uv_ref[...]
    cdf = 0.5 * (1.0 + lax.erf(x * (2.0 ** -0.5)))
    ge = x * cdf
    dge = cdf + x * jnp.exp(-0.5 * x * x) * ((2.0 * math.pi) ** -0.5)
    u, v = ge[:, 0:SGU_W], ge[:, SGU_W:]
    vc = v - jnp.mean(v, axis=-1, keepdims=True)
    rstd = lax.rsqrt(jnp.mean(vc * vc, axis=-1, keepdims=True) + LN_EPS)
    vhat = vc * rstd
    vn = vhat * g_ref[...] + b_ref[...]
    tri = _tri()
    wc = [jnp.where(tri, w_ref[gi], 0.0).astype(BF) for gi in range(4)]
    vm = [jnp.where(_lane_mask(gi % 2), vn[:, 128 * (gi // 2):128 * (gi // 2 + 1)], 0.0).astype(BF) for gi in range(4)]
    mixed = jnp.concatenate([_dot(wc[2 * t], vm[2 * t]) + _dot(wc[2 * t + 1], vm[2 * t + 1]) for t in range(2)],
                            axis=1) + bias_ref[...]
    return dict(dge=dge, u=u, rstd=rstd, vhat=vhat, wc=wc, vm=vm, mixed=mixed)


def _sgu_specs():
    vec = pl.BlockSpec((1, SGU_W), lambda i: (0, 0))
    return [pl.BlockSpec((CHUNK, W_UV), lambda i: (i, 0)), vec, vec,
            pl.BlockSpec((4, CHUNK, CHUNK), lambda i: (0, 0, 0)), pl.BlockSpec((CHUNK, SGU_W), lambda i: (0, 0))]


def _sgu_fwd(uv, ln_g, ln_b, w, bias):
    T = uv.shape[0]

    def body(uv_ref, g_ref, b_ref, w_ref, bias_ref, y_ref):
        s = _sgu_core(uv_ref, g_ref, b_ref, w_ref, bias_ref)
        y_ref[...] = s["u"] * s["mixed"]

    return _call(body, name="sgu_fwd", grid=(T // CHUNK,), in_specs=_sgu_specs(),
                 out_specs=pl.BlockSpec((CHUNK, SGU_W), lambda i: (i, 0)),
                 out_shape=jax.ShapeDtypeStruct((T, SGU_W), F32), sem=("parallel",))(uv, ln_g, ln_b, w, bias)


def _sgu_bwd(uv, dy, ln_g, ln_b, w, bias):
    T = uv.shape[0]

    def body(uv_ref, dy_ref, g_ref, b_ref, w_ref, bias_ref, dx_ref, dw_ref, dbias_ref, dln_ref):
        @pl.when(pl.program_id(0) == 0)
        def _():
            dw_ref[...] = jnp.zeros_like(dw_ref)
            dbias_ref[...] = jnp.zeros_like(dbias_ref)
            dln_ref[...] = jnp.zeros_like(dln_ref)

        s = _sgu_core(uv_ref, g_ref, b_ref, w_ref, bias_ref)
        dy_ = dy_ref[...]
        du = dy_ * s["mixed"]
        dmix = dy_ * s["u"]
        dbias_ref[...] += dmix
        tri = _tri()
        dvn_t = []
        for t in range(2):
            acc = jnp.zeros((CHUNK, 128), F32)
            for e in range(2):
                gi = 2 * t + e
                dmg = jnp.where(_lane_mask(e), dmix[:, 128 * t:128 * (t + 1)], 0.0).astype(BF)
                acc = acc + _dot_tn(s["wc"][gi], dmg)
                dw_ref[gi] += jnp.where(tri, _dot_nt(dmg, s["vm"][gi]), 0.0)
            dvn_t.append(acc)
        dvn = jnp.concatenate(dvn_t, axis=1)
        dln_ref[0:1, :] += jnp.sum(dvn * s["vhat"], axis=0, keepdims=True)
        dln_ref[1:2, :] += jnp.sum(dvn, axis=0, keepdims=True)
        dvh = dvn * g_ref[...]
        dv = s["rstd"] * (dvh - jnp.mean(dvh, axis=-1, keepdims=True)
                          - s["vhat"] * jnp.mean(dvh * s["vhat"], axis=-1, keepdims=True))
        dx_ref[...] = (jnp.concatenate([du, dv], axis=1) * s["dge"]).astype(BF)

    ins = _sgu_specs()
    return _call(body, name="sgu_bwd", grid=(T // CHUNK,),
                 in_specs=[ins[0], pl.BlockSpec((CHUNK, SGU_W), lambda i: (i, 0))] + ins[1:],
                 out_specs=[pl.BlockSpec((CHUNK, W_UV), lambda i: (i, 0)),
                            pl.BlockSpec((4, CHUNK, CHUNK), lambda i: (0, 0, 0)),
                            pl.BlockSpec((CHUNK, SGU_W), lambda i: (0, 0)), pl.BlockSpec((8, SGU_W), lambda i: (0, 0))],
                 out_shape=[jax.ShapeDtypeStruct((T, W_UV), BF), jax.ShapeDtypeStruct((4, CHUNK, CHUNK), F32),
                            jax.ShapeDtypeStruct((CHUNK, SGU_W), F32), jax.ShapeDtypeStruct((8, SGU_W), F32)],
                 sem=("arbitrary",))(uv, dy, ln_g, ln_b, w, bias)


def _adamw(w, g, m, v):
    R, C = w.shape
    tr = R

    def body(w_ref, g_ref, m_ref, v_ref, d_ref, nm_ref, nv_ref):
        g_ = g_ref[...]
        m2 = ADAM_B1 * m_ref[...] + (1.0 - ADAM_B1) * g_
        v2 = ADAM_B2 * v_ref[...] + (1.0 - ADAM_B2) * (g_ * g_)
        m_hat = m2 / (1.0 - ADAM_B1 ** ADAM_STEP)
        v_hat = v2 / (1.0 - ADAM_B2 ** ADAM_STEP)
        d_ref[...] = -ADAM_LR * (m_hat / (jnp.sqrt(v_hat) + ADAM_EPS) + ADAM_WD * w_ref[...])
        nm_ref[...] = m2
        nv_ref[...] = v2

    blk = pl.BlockSpec((tr, C), lambda i: (i, 0))
    sh = jax.ShapeDtypeStruct((R, C), F32)
    return _call(body, name="adamw", grid=(R // tr,), in_specs=[blk] * 4, out_specs=[blk] * 3,
                 out_shape=[sh] * 3, sem=("parallel",))(w, g, m, v)


def _adamw_pair(w, g0, g1, m, v, dep):
    L, R, C = w.shape
    tr = _tile(R, 256 if C <= 1024 else 64)

    def body(w_ref, g0_ref, g1_ref, m_ref, v_ref, dep_ref, d_ref, nm_ref, nv_ref, og_ref):
        g_ = jnp.where(pl.program_id(0) == 0, g0_ref[...], g1_ref[...])
        m2 = ADAM_B1 * m_ref[...] + (1.0 - ADAM_B1) * g_
        v2 = ADAM_B2 * v_ref[...] + (1.0 - ADAM_B2) * (g_ * g_)
        m_hat = m2 / (1.0 - ADAM_B1 ** ADAM_STEP)
        v_hat = v2 / (1.0 - ADAM_B2 ** ADAM_STEP)
        d_ref[...] = -ADAM_LR * (m_hat / (jnp.sqrt(v_hat) + ADAM_EPS) + ADAM_WD * w_ref[...])
        nm_ref[...] = m2
        nv_ref[...] = v2
        og_ref[...] = g_

    lay = pl.BlockSpec((None, tr, C), lambda l, i: (l, i, 0))
    one = lambda k: pl.BlockSpec((tr, C), lambda l, i: (jnp.where(l == k, i, 0), 0))
    return _call(body, name="adamw_pair", grid=(L, R // tr),
                 in_specs=[lay, one(0), one(1), lay, lay, pl.BlockSpec((8, 128), lambda l, i: (0, 0))],
                 out_specs=[lay] * 4,
                 out_shape=[jax.ShapeDtypeStruct((L, R, C), F32)] * 4,
                 sem=("parallel", "parallel"))(w, g0, g1, m, v, dep)


def _row_steps(rows):
    return 2 if rows % 32 == 0 else 1


def _pair_add(gbuf, rsib, c):
    NS, _, R, C = gbuf.shape
    n = _row_steps(R)
    tr = R // n

    def body(c_ref, a_ref, b_ref, o_ref):
        o_ref[...] = (a_ref[...] + b_ref[...]).astype(BF)

    blk = pl.BlockSpec((None, tr, C), lambda j, i, c_ref: (j, i, 0))
    return pl.pallas_call(
        body, name="rs_pair_add",
        grid_spec=pltpu.PrefetchScalarGridSpec(
            num_scalar_prefetch=1, grid=(NS, n),
            in_specs=[pl.BlockSpec((None, None, tr, C), lambda j, i, c_ref: (j, c_ref[0], i, 0)), blk],
            out_specs=blk),
        out_shape=jax.ShapeDtypeStruct((NS, R, C), BF),
        compiler_params=pltpu.CompilerParams(dimension_semantics=("parallel", "parallel")),
    )(jnp.reshape(c, (1,)).astype(jnp.int32), gbuf, rsib)


def _chip_sum(pair, recv, me, c):
    NS, R, C = pair.shape
    n = _row_steps(R)
    tr = R // n

    def body(s_ref, own_ref, p_ref, o_ref):
        p = [jnp.where(s_ref[0] == j, own_ref[...], p_ref[j]).astype(F32) for j in range(4)]
        o_ref[...] = ((p[0] + p[1]) + p[2]) + p[3]

    return pl.pallas_call(
        body, name="rs_chip_sum",
        grid_spec=pltpu.PrefetchScalarGridSpec(
            num_scalar_prefetch=1, grid=(n,),
            in_specs=[pl.BlockSpec((None, tr, C), lambda i, s: (s[0], i, 0)),
                      pl.BlockSpec((NS, tr, C), lambda i, s: (0, i, 0))],
            out_specs=pl.BlockSpec((None, tr, C), lambda i, s: (s[1], i, 0))),
        out_shape=jax.ShapeDtypeStruct((2, R, C), F32),
        compiler_params=pltpu.CompilerParams(dimension_semantics=("parallel",)),
    )(jnp.stack([me, c]).astype(jnp.int32), pair, recv)


MESH = pl.DeviceIdType.MESH
ANY = pl.BlockSpec(memory_space=pl.ANY)


def _place():
    x, y, c = lax.axis_index("x"), lax.axis_index("y"), lax.axis_index("c")
    return x, y, c, [(1 - x, y), (x, 1 - y), (1 - x, 1 - y)]


HBM = pl.BlockSpec(memory_space=pltpu.HBM)
SEM = pl.BlockSpec(memory_space=pltpu.SEMAPHORE)
EFFECT = pltpu.SideEffectType.DATAFLOW_SIDE_EFFECTING


class _Split:
    def __init__(self, tag, arrays, copies, n_copies, after=()):
        self.tag, self.copies, k = tag, copies, len(arrays)

        def body(*refs):
            sems = k + len(after)
            for cp in copies(refs[:k], refs[sems], refs[sems + 1]):
                cp.start()
            refs[-1][...] = jnp.zeros_like(refs[-1])

        out = pl.pallas_call(
            body, name=tag + "_start",
            out_shape=(pltpu.SemaphoreType.DMA((n_copies,)), pltpu.SemaphoreType.DMA((n_copies,)),
                       *[pltpu.HBM(a.shape, a.dtype) for a in arrays], jax.ShapeDtypeStruct((8, 128), F32)),
            in_specs=[HBM] * k + [ANY] * len(after),
            out_specs=(SEM, SEM, *[HBM] * k, pl.BlockSpec(memory_space=pltpu.VMEM)),
            input_output_aliases={i: 2 + i for i in range(k)},
            compiler_params=pltpu.CompilerParams(has_side_effects=EFFECT),
        )(*[pltpu.with_memory_space_constraint(a, pltpu.HBM) for a in arrays], *after)
        self.send, self.recv, self.arrays, self.token = out[0], out[1], list(out[2:2 + k]), out[-1][0, 0]

    def wait(self, after):
        k, copies = len(self.arrays), self.copies
        after = list(after) if isinstance(after, (list, tuple)) else [after]

        def body(*refs):
            for cp in copies(refs[:k], refs[k], refs[k + 1]):
                cp.wait_send()
                cp.wait_recv()

        return list(pl.pallas_call(
            body, name=self.tag + "_wait", out_shape=tuple(pltpu.HBM(a.shape, a.dtype) for a in self.arrays),
            in_specs=[HBM] * k + [SEM, SEM] + [ANY] * len(after), out_specs=tuple([HBM] * k),
            input_output_aliases={i: i for i in range(k)},
            compiler_params=pltpu.CompilerParams(has_side_effects=EFFECT),
        )(*self.arrays, self.send, self.recv, *after))


def _landing_zones(arrs):
    me = 2 * lax.axis_index("x") + lax.axis_index("y")
    return [lax.dynamic_update_index_in_dim(lax.empty((4,) + a.shape, a.dtype), a, me, 0) for a in arrs]


def _gather_start(arrs, lands, tag, after=()):
    n = len(arrs)

    def copies(refs, send, recv):
        x, y, c, chips = _place()
        return [pltpu.make_async_remote_copy(
            src_ref=refs[k], dst_ref=refs[n + k].at[2 * x + y], send_sem=send.at[3 * k + r],
            recv_sem=recv.at[3 * k + r], device_id=(px, py, c), device_id_type=MESH)
            for k in range(n) for r, (px, py) in enumerate(chips)]

    return _Split("gather_" + tag, list(arrs) + lands, copies, 3 * n, after)


def _gather_halves_start(arrs, tag):
    n = len(arrs)
    lands = _landing_zones(arrs)

    def copies(refs, send, recv):
        x, y, c, chips = _place()
        return [pltpu.make_async_remote_copy(
            src_ref=refs[k].at[c], dst_ref=refs[n + k].at[2 * x + y, c], send_sem=send.at[3 * k + r],
            recv_sem=recv.at[3 * k + r], device_id=(px, py, c), device_id_type=MESH)
            for k in range(n) for r, (px, py) in enumerate(chips)]

    return _Split("gather_" + tag, list(arrs) + lands, copies, 3 * n)


def _gather_halves_finish(lands, tag):
    n = len(lands)

    def copies(refs, send, recv):
        x, y, c, chips = _place()
        return [pltpu.make_async_remote_copy(
            src_ref=refs[k].at[2 * px + py, c], dst_ref=refs[k].at[2 * px + py, c], send_sem=send.at[3 * k + r],
            recv_sem=recv.at[3 * k + r], device_id=(x, y, 1 - c), device_id_type=MESH)
            for k in range(n) for r, (px, py) in enumerate(chips)]

    return _Split("gather_pass_" + tag, list(lands), copies, 3 * n)


def _to_sibling_start(gbufs, tag):
    n = len(gbufs)

    def copies(refs, send, recv):
        x, y, c, _ = _place()
        return [pltpu.make_async_remote_copy(
            src_ref=refs[k].at[j, 1 - c], dst_ref=refs[n + k].at[j], send_sem=send.at[4 * k + j],
            recv_sem=recv.at[4 * k + j], device_id=(x, y, 1 - c), device_id_type=MESH)
            for k in range(n) for j in range(4)]

    lands = [lax.empty((4,) + g.shape[2:], g.dtype) for g in gbufs]
    return _Split("rs_sibling_" + tag, list(gbufs) + lands, copies, 4 * n)


def _to_chips_start(pbufs, tag):
    n = len(pbufs)

    def copies(refs, send, recv):
        x, y, c, chips = _place()
        return [pltpu.make_async_remote_copy(
            src_ref=refs[k].at[2 * px + py], dst_ref=refs[n + k].at[2 * x + y], send_sem=send.at[3 * k + r],
            recv_sem=recv.at[3 * k + r], device_id=(px, py, c), device_id_type=MESH)
            for k in range(n) for r, (px, py) in enumerate(chips)]

    return _Split("rs_chips_" + tag, list(pbufs) + [lax.empty(p.shape, p.dtype) for p in pbufs], copies, 3 * n)


def _join_start(fulls, tag):
    def copies(refs, send, recv):
        x, y, c, _ = _place()
        return [pltpu.make_async_remote_copy(
            src_ref=refs[k].at[c], dst_ref=refs[k].at[c], send_sem=send.at[k], recv_sem=recv.at[k],
            device_id=(x, y, 1 - c), device_id_type=MESH) for k in range(len(fulls))]

    return _Split("rs_join_" + tag, list(fulls), copies, len(fulls))


def _all_reduce_small(v):
    R, C = v.shape

    def body(v_ref, o_ref, g_ref, send, recv, loc):
        x, y, c, chips = _place()
        me, sibling = (x, y, c), (x, y, 1 - c)

        def rows(px, py, pc):
            return g_ref.at[4 * px + 2 * py + pc]

        def copy(k, block, to, src=None):
            return pltpu.make_async_remote_copy(
                src_ref=rows(*block) if src is None else src, dst_ref=rows(*block),
                send_sem=send.at[k], recv_sem=recv.at[k], device_id=to, device_id_type=MESH)

        mine = pltpu.make_async_copy(v_ref, rows(*me), loc)
        mine.start()
        first = [copy(0, me, sibling, src=v_ref)]
        first += [copy(1 + j, me, (*chip, c), src=v_ref) for j, chip in enumerate(chips)]
        for cp in first:
            cp.start()
        passed = [copy(4 + j, (*chip, c), sibling) for j, chip in enumerate(chips)]
        for j, chip in enumerate(chips):
            copy(1 + j, (*chip, c), me).wait_recv()
            passed[j].start()
        copy(0, sibling, me).wait_recv()
        for j, chip in enumerate(chips):
            copy(4 + j, (*chip, 1 - c), me).wait_recv()
        for cp in first + passed:
            cp.wait_send()
        mine.wait()
        acc = g_ref[0]
        for d in range(1, 8):
            acc = acc + g_ref[d]
        o_ref[...] = acc

    vm = pl.BlockSpec(memory_space=pltpu.VMEM)
    return pl.pallas_call(
        body, name="all_reduce_small", in_specs=[vm], out_specs=[vm, vm],
        out_shape=[jax.ShapeDtypeStruct((R, C), F32), jax.ShapeDtypeStruct((8, R, C), F32)],
        scratch_shapes=[pltpu.SemaphoreType.DMA((7,)), pltpu.SemaphoreType.DMA((7,)), pltpu.SemaphoreType.DMA],
    )(v)[0]


WEIGHTS = ['ffn1_norm', 'ffn1_w_gate', 'ffn1_w_up', 'ffn1_w_down', 'mix_norm', 'w_in', 'conv_w', 'conv_b', 'dt_bias',
           'a_log', 'd_skip', 'ssd_norm', 'sgu_ln_g', 'sgu_ln_b', 'sgu_w', 'sgu_b', 'w_out', 'ffn2_norm',
           'ffn2_w_gate', 'ffn2_w_up', 'ffn2_w_down', 'final_norm']
SHARDED = ['ffn1_w_gate', 'ffn1_w_up', 'ffn1_w_down', 'w_in', 'conv_w', 'w_out', 'ffn2_w_gate', 'ffn2_w_up',
           'ffn2_w_down']
SMALL = [n for n in WEIGHTS if n not in SHARDED]
GROUPS = [("ffn1", ["ffn1_w_gate", "ffn1_w_up", "ffn1_w_down"]), ("mix", ["w_in", "conv_w", "w_out"]),
          ("ffn2", ["ffn2_w_gate", "ffn2_w_up", "ffn2_w_down"])]
TRANSPOSED = ("ffn1_w_gate", "ffn1_w_up", "ffn2_w_gate", "ffn2_w_up")
DEPTH = 2


def _pack_w_in(w):
    return jnp.concatenate([w[..., 0:1152], w[..., 1536:2432], w[..., 1152:1536],
                            jnp.repeat(w[..., 2432:2438], HEAD, axis=-1), w[..., 2438:2950]], axis=-1)


def _unpack_w_in(dq, ds, du):
    return jnp.concatenate([dq, ds[:, 896:1280], ds[:, 0:896], ds[:, 1280::HEAD], du], axis=-1)


def _ffn_fwd(x, g, wg, wu, wd):
    xo, hb, S1, S2, A = _ffn_fwd_k(x, g, wg, wu, wd)
    return xo, (x, hb, S1, S2, A)


def _ffn_bwd(dxo, saved, g, wg, wu, wd):
    x, hb, S1, S2, A = saved
    dx, dg, dG, dU, dyb = _ffn_bwd_k1(dxo, x, g, S1, S2, wg, wu, wd)
    dwg, dwu, dwd = _ffn_bwd_k2(hb, dyb, A, dG, dU)
    return dx, dg, dwg, dwu, dwd


def _mix_fwd(x, P):
    hb, qkv, sin, uv = _mix_proj(x, P["mix_norm"], P["w_in"])
    y_att, lse = _attn_combine([_attn_fwd(qkv, d) for d in DILATIONS])
    y_ssd, hprev = _ssd_fwd(sin, *P["ssd"])
    y_sgu = _sgu_fwd(uv, *P["sgu"])
    ycat = jnp.concatenate([y_att, y_ssd, y_sgu], axis=1).astype(BF)
    return _mm_nn(ycat, P["w_out"], res=x), (x, hb, qkv, sin, uv, y_att, lse, hprev, ycat)


def _mix_bwd(dxo, saved, P):
    x, hb, qkv, sin, uv, y_att, lse, hprev, ycat = saved
    dycat = _mm_nt(dxo, P["w_out"])
    dwout = _mm_tn(ycat, dxo)
    dy_att, dy_ssd, dy_sgu = dycat[:, 0:ATT_W], dycat[:, ATT_W:ATT_W + SSD_W], dycat[:, ATT_W + SSD_W:]
    dqkv = _sum_branches([_attn_bwd(qkv, dy_att, y_att, lse, d) for d in DILATIONS])
    dsin, dcw, dcb, dvec = _ssd_bwd(sin, hprev, dy_ssd, *P["ssd"])
    duv, dsw, dsbias, dln = _sgu_bwd(uv, dy_sgu, *P["sgu"])
    dwin = _unpack_w_in(_mm_tn(hb, dqkv), _mm_tn(hb, dsin), _mm_tn(hb, duv))
    dx, dg = _mix_bwd_dx(dqkv, dsin, duv, P["w_in"], x, P["mix_norm"], dxo)
    grads = dict(
        mix_norm=dg[0], w_in=dwin, conv_w=dcw, conv_b=dcb[0], dt_bias=dvec[0, ::HEAD], a_log=dvec[1, ::HEAD],
        d_skip=jnp.sum(dvec[2].reshape(6, HEAD), axis=-1), ssd_norm=dvec[3], sgu_ln_g=dln[0], sgu_ln_b=dln[1],
        sgu_w=dsw, sgu_b=jnp.sum(dsbias.reshape(CHUNK, 4, HEAD), axis=-1).T, w_out=dwout)
    return dx, grads


def _halved(g):
    rows = g.size // g.shape[-1]
    return g.reshape(4, 2, rows // 8, g.shape[-1])


def kernel(x, ffn1_norm, ffn1_w_gate, ffn1_w_up, ffn1_w_down, mix_norm, w_in, conv_w, conv_b, dt_bias, a_log, d_skip, ssd_norm, sgu_ln_g, sgu_ln_b, sgu_w, sgu_b, w_out, ffn2_norm, ffn2_w_gate, ffn2_w_up, ffn2_w_down, final_norm, loss_target, m_ffn1_norm, m_ffn1_w_gate, m_ffn1_w_up, m_ffn1_w_down, m_mix_norm, m_w_in, m_conv_w, m_conv_b, m_dt_bias, m_a_log, m_d_skip, m_ssd_norm, m_sgu_ln_g, m_sgu_ln_b, m_sgu_w, m_sgu_b, m_w_out, m_ffn2_norm, m_ffn2_w_gate, m_ffn2_w_up, m_ffn2_w_down, m_final_norm, v_ffn1_norm, v_ffn1_w_gate, v_ffn1_w_up, v_ffn1_w_down, v_mix_norm, v_w_in, v_conv_w, v_conv_b, v_dt_bias, v_a_log, v_d_skip, v_ssd_norm, v_sgu_ln_g, v_sgu_ln_b, v_sgu_w, v_sgu_b, v_w_out, v_ffn2_norm, v_ffn2_w_gate, v_ffn2_w_up, v_ffn2_w_down, v_final_norm):
    given = dict(x=x, ffn1_norm=ffn1_norm, ffn1_w_gate=ffn1_w_gate, ffn1_w_up=ffn1_w_up, ffn1_w_down=ffn1_w_down, mix_norm=mix_norm, w_in=w_in, conv_w=conv_w, conv_b=conv_b, dt_bias=dt_bias, a_log=a_log, d_skip=d_skip, ssd_norm=ssd_norm, sgu_ln_g=sgu_ln_g, sgu_ln_b=sgu_ln_b, sgu_w=sgu_w, sgu_b=sgu_b, w_out=w_out, ffn2_norm=ffn2_norm, ffn2_w_gate=ffn2_w_gate, ffn2_w_up=ffn2_w_up, ffn2_w_down=ffn2_w_down, final_norm=final_norm, loss_target=loss_target, m_ffn1_norm=m_ffn1_norm, m_ffn1_w_gate=m_ffn1_w_gate, m_ffn1_w_up=m_ffn1_w_up, m_ffn1_w_down=m_ffn1_w_down, m_mix_norm=m_mix_norm, m_w_in=m_w_in, m_conv_w=m_conv_w, m_conv_b=m_conv_b, m_dt_bias=m_dt_bias, m_a_log=m_a_log, m_d_skip=m_d_skip, m_ssd_norm=m_ssd_norm, m_sgu_ln_g=m_sgu_ln_g, m_sgu_ln_b=m_sgu_ln_b, m_sgu_w=m_sgu_w, m_sgu_b=m_sgu_b, m_w_out=m_w_out, m_ffn2_norm=m_ffn2_norm, m_ffn2_w_gate=m_ffn2_w_gate, m_ffn2_w_up=m_ffn2_w_up, m_ffn2_w_down=m_ffn2_w_down, m_final_norm=m_final_norm, v_ffn1_norm=v_ffn1_norm, v_ffn1_w_gate=v_ffn1_w_gate, v_ffn1_w_up=v_ffn1_w_up, v_ffn1_w_down=v_ffn1_w_down, v_mix_norm=v_mix_norm, v_w_in=v_w_in, v_conv_w=v_conv_w, v_conv_b=v_conv_b, v_dt_bias=v_dt_bias, v_a_log=v_a_log, v_d_skip=v_d_skip, v_ssd_norm=v_ssd_norm, v_sgu_ln_g=v_sgu_ln_g, v_sgu_ln_b=v_sgu_ln_b, v_sgu_w=v_sgu_w, v_sgu_b=v_sgu_b, v_w_out=v_w_out, v_ffn2_norm=v_ffn2_norm, v_ffn2_w_gate=v_ffn2_w_gate, v_ffn2_w_up=v_ffn2_w_up, v_ffn2_w_down=v_ffn2_w_down, v_final_norm=v_final_norm)
    T = given["x"].shape[0] * given["x"].shape[1]
    D = given["x"].shape[2]
    x0 = given["x"].reshape(T, D)
    tgt = given["loss_target"].reshape(T, D)
    c = lax.axis_index("c")

    bf = {n: given[n].astype(BF) for n in SHARDED if n not in ("w_in", "conv_w")}
    bf["w_in"] = _pack_w_in(given["w_in"]).astype(BF)
    bf["conv_w"] = given["conv_w"]
    first_key = (0, GROUPS[0][0])
    first = [bf[n][0].reshape((2, bf[n].shape[1] // 2) + bf[n].shape[2:]) for n in GROUPS[0][1]]
    gathers = {first_key: _gather_halves_start(first, "l0_" + GROUPS[0][0])}
    later = {(i, gname): [bf[n][i] for n in names] for i in range(DEPTH) for gname, names in GROUPS
             if (i, gname) != first_key}
    zones = {key: _landing_zones(arrs) for key, arrs in later.items()}

    def gathered(i, gname, after):
        if (i, gname) != first_key:
            return gathers[(i, gname)].wait(after)[3:]
        got = gathers[first_key].wait([after] + [z for zs in zones.values() for z in zs])[3:]
        got = _gather_halves_finish(got, "l0_" + gname).wait(after)
        for key, arrs in later.items():
            gathers[key] = _gather_start(arrs, zones[key], f"l{key[0]}_{key[1]}", after=[got[0]])
        return [z.reshape((4, 2 * z.shape[2]) + z.shape[3:]) for z in got]

    def mix_params(i, got):
        win = got[0].reshape(D, W_QKV + W_SSD + W_UV)
        rep = lambda v: jnp.repeat(v, HEAD)[None]
        ssd = (got[1].transpose(1, 0, 2).reshape(4, SSD_CONV_DIM), given["conv_b"][i][None],
               rep(given["dt_bias"][i]), rep(given["a_log"][i]), rep(given["d_skip"][i]), given["ssd_norm"][i][None])
        sgu = (given["sgu_ln_g"][i][None], given["sgu_ln_b"][i][None], given["sgu_w"][i],
               jnp.repeat(given["sgu_b"][i].T, HEAD, axis=1))
        return dict(mix_norm=given["mix_norm"][i][None], w_in=win, w_out=got[2].reshape(-1, D), ssd=ssd, sgu=sgu)

    x = x0
    tape = []
    for i in range(DEPTH):
        got = gathered(i, "ffn1", x)
        token = functools.reduce(lambda a, b: a + b, [g.token for g in gathers.values()]) if i == 0 else 0.0
        P = dict(ffn1=(given["ffn1_norm"][i][None] + token, *got))
        x, s1 = _ffn_fwd(x, *P["ffn1"])
        P.update(mix_params(i, gathered(i, "mix", x)))
        x, s2 = _mix_fwd(x, P)
        P["ffn2"] = (given["ffn2_norm"][i][None], *gathered(i, "ffn2", x))
        x, s3 = _ffn_fwd(x, *P["ffn2"])
        tape.append((P, s1, s2, s3))
    loss_part, dx, dgf = _final_loss(x, given["final_norm"][None], tgt)

    me = 2 * lax.axis_index("x") + lax.axis_index("y")
    jobs = []

    def rs_begin(i, gname, gd):
        tag = f"l{i}_{gname}"
        names = [n for n in dict(GROUPS)[gname] if n != "conv_w"]
        jobs.append(dict(key=(i, gname), names=names, tag=tag, stage=1,
                         op=_to_sibling_start([_halved(gd[n]) for n in names], tag)))

    def rs_advance(job, after):
        k = len(job["names"])
        if job["stage"] == 1:
            got = job["op"].wait(after)
            job.update(stage=2, op=_to_chips_start([_pair_add(g, l, c) for g, l in zip(got[:k], got[k:])], job["tag"]))
        elif job["stage"] == 2:
            got = job["op"].wait(after)
            job.update(stage=3, op=_join_start([_chip_sum(p, l, me, c) for p, l in zip(got[:k], got[k:])], job["tag"]))
        elif job["stage"] == 3:
            job.update(stage=4, out=dict(zip(job["names"], job["op"].wait(after))))

    def tick(after, begin=None):
        for job in jobs:
            rs_advance(job, after)
        if begin is not None:
            rs_begin(*begin)
        return functools.reduce(lambda a, b: a + b, [j["op"].token for j in jobs if j["stage"] < 4], 0.0)

    grads = [dict() for _ in range(DEPTH)]
    tok = 0.0
    for i in reversed(range(DEPTH)):
        P, s1, s2, s3 = tape[i]
        g = grads[i]
        norm, wg, wu, wd = P["ffn2"]
        dx, dn2, g["ffn2_w_gate"], g["ffn2_w_up"], g["ffn2_w_down"] = _ffn_bwd(dx, s3, norm + tok, wg, wu, wd)
        tok = tick(dx, (i, "ffn2", g))
        dx, gm = _mix_bwd(dx, s2, {**P, "mix_norm": P["mix_norm"] + tok})
        g.update(gm)
        tok = tick(dx, (i, "mix", g))
        norm, wg, wu, wd = P["ffn1"]
        dx, dn1, g["ffn1_w_gate"], g["ffn1_w_up"], g["ffn1_w_down"] = _ffn_bwd(dx, s1, norm + tok, wg, wu, wd)
        tok = tick(dx, (i, "ffn1", g))
        g["ffn1_norm"], g["ffn2_norm"] = dn1[0], dn2[0]
    grad_x = dx.reshape(given["x"].shape)

    order = [n for n in SMALL if n != "final_norm"] + ["final_norm"]
    small = [jnp.stack([grads[i][n] for i in range(DEPTH)]) for n in order[:-1] + ["conv_w"]]
    small = small[:-1] + [dgf[0], small[-1], loss_part[0, 0:1]]
    n_small = sum(s.size for s in small)
    rows_small = -(-n_small // (128 * 8)) * 8

    def flat(arrs):
        fill = rows_small * 128 - sum(a.size for a in arrs)
        return jnp.concatenate([a.reshape(-1) for a in arrs] + [jnp.zeros((fill,), F32)]).reshape(rows_small, 128)

    gsmall = _all_reduce_small(flat(small)).reshape(-1)

    grad_w = {}
    off = 0
    for n in order:
        size = given[n].size
        grad_w[n] = gsmall[off:off + size].reshape(given[n].shape)
        off += size
    cw = gsmall[off:off + 2 * 4 * SSD_CONV_DIM].reshape(DEPTH, 4, SSD_CONV_DIM)
    grad_w["conv_w"] = lax.dynamic_slice_in_dim(cw, me * (SSD_CONV_DIM // 4), SSD_CONV_DIM // 4, axis=2)
    loss = gsmall[off + 2 * 4 * SSD_CONV_DIM]

    delta, new_m, new_v = {}, {}, {}
    shp = given["conv_w"].shape
    d, m2, v2 = _adamw(*[a.reshape(shp[0] * shp[1], shp[2])
                         for a in (given["conv_w"], grad_w["conv_w"], given["m_conv_w"], given["v_conv_w"])])
    delta["conv_w"], new_m["conv_w"], new_v["conv_w"] = d.reshape(shp), m2.reshape(shp), v2.reshape(shp)
    packed = [flat([given[pre + n] for n in order]) for pre in ("", "m_", "v_")]
    small_out = _adamw(packed[0], gsmall.reshape(rows_small, 128), packed[1], packed[2])
    outs = [o.reshape(-1) for o in small_out]
    off = 0
    for n in order:
        size = given[n].size
        for dst, o in zip((delta, new_m, new_v), outs):
            dst[n] = o[off:off + size].reshape(given[n].shape)
        off += size

    stepped, arrived = {}, {}

    def update_arrived(dep):
        out = None
        for job in jobs:
            if job["stage"] == 4 and not job.get("seen"):
                job["seen"] = True
                for n, full in job["out"].items():
                    view = (lambda a: jnp.swapaxes(a, 1, 2)) if n in TRANSPOSED else (lambda a: a)
                    arrived.setdefault(n, {})[job["key"][0]] = full.reshape(view(given[n]).shape[1:])
                    if len(arrived[n]) == DEPTH:
                        res = _adamw_pair(view(given[n]), arrived[n][0], arrived[n][1], view(given["m_" + n]),
                                          view(given["v_" + n]), dep)
                        stepped[n] = [view(r) for r in res]
                        out = res[0]
        return out

    after = small_out[0]
    while any(j["stage"] < 4 for j in jobs):
        done = update_arrived(jnp.zeros((8, 128), F32) + tok)
        after = after if done is None else done
        tok = tick(after)
    update_arrived(jnp.zeros((8, 128), F32) + tok)
    for n, (d, m2, v2, g) in stepped.items():
        delta[n], new_m[n], new_v[n], grad_w[n] = d, m2, v2, g

    return (loss, grad_x, *[grad_w[n] for n in WEIGHTS], *[delta[n] for n in WEIGHTS],
            *[new_m[n] for n in WEIGHTS], *[new_v[n] for n in WEIGHTS])
```

```python
import functools
import math

import jax
import jax.numpy as jnp
from jax import lax
from jax.experimental import pallas as pl
from jax.experimental.pallas import tpu as pltpu

F32 = jnp.float32
BF = jnp.bfloat16

RMS_EPS = 1e-6
LN_EPS = 1e-5
SEQ = 2048
CHUNK = 128
N_CHUNK = SEQ // CHUNK
ATT_W = 384
HEAD = 64
SSD_W = 384
SSD_CONV_DIM = 896
SSD_STATE = 128
SGU_W = 256
DILATIONS = (1, 4, 16)
W_QKV = 3 * ATT_W
W_SSD = SSD_CONV_DIM + SSD_W + SSD_W
W_UV = 2 * SGU_W
ADAM_LR = 0.001
ADAM_B1 = 0.9
ADAM_B2 = 0.999
ADAM_EPS = 1e-08
ADAM_WD = 0.01
ADAM_STEP = 10
NEG = -1e30
ATTN_BWD_VMEM = 48 * 2 ** 20
FFN_VMEM = 60 * 2 ** 20


def _dot(a, b):
    return jnp.dot(a, b, preferred_element_type=F32)


def _dot_nt(a, b):
    return lax.dot_general(a, b, (((1,), (1,)), ((), ())), preferred_element_type=F32)


def _dot_tn(a, b):
    return lax.dot_general(a, b, (((0,), (0,)), ((), ())), preferred_element_type=F32)


def _sigmoid(x):
    return 1.0 / (1.0 + jnp.exp(-x))


def _call(body, *, name, grid, in_specs, out_specs, out_shape, scratch=(), sem=None, vmem=None):
    return pl.pallas_call(
        body, name=name, grid=grid, in_specs=in_specs, out_specs=out_specs, out_shape=out_shape,
        scratch_shapes=list(scratch),
        compiler_params=pltpu.CompilerParams(dimension_semantics=sem, vmem_limit_bytes=vmem),
    )


def _tile(n, want):
    t = min(n, want)
    while n % t:
        t //= 2
    return t


def _final_loss(x, g, tgt):
    T, D = x.shape
    tm = _tile(T, 512)

    def body(x_ref, g_ref, t_ref, l_ref, dx_ref, dg_ref):
        @pl.when(pl.program_id(0) == 0)
        def _():
            dg_ref[...] = jnp.zeros_like(dg_ref)
            l_ref[...] = jnp.zeros_like(l_ref)

        xf = x_ref[...]
        gg = g_ref[...]
        r = lax.rsqrt(jnp.mean(xf * xf, axis=-1, keepdims=True) + RMS_EPS)
        xn = xf * r
        e = xn * gg - t_ref[...]
        part = 0.5 * jnp.sum(jnp.mean(e * e, axis=-1, keepdims=True), axis=0, keepdims=True)
        l_ref[...] += jnp.broadcast_to(part, l_ref.shape)
        dy = e * (1.0 / D)
        u = dy * gg
        mu = jnp.mean(u * xf, axis=-1, keepdims=True)
        dx_ref[...] = r * (u - xf * (r * r * mu))
        dg_ref[...] += jnp.sum(dy * xn, axis=0, keepdims=True)

    row = pl.BlockSpec((tm, D), lambda i: (i, 0))
    vec = pl.BlockSpec((1, D), lambda i: (0, 0))
    lsp = pl.BlockSpec((1, 128), lambda i: (0, 0))
    return _call(body, name="final_loss", grid=(T // tm,), in_specs=[row, vec, row], out_specs=[lsp, row, vec],
                 out_shape=[jax.ShapeDtypeStruct((1, 128), F32), jax.ShapeDtypeStruct((T, D), F32),
                            jax.ShapeDtypeStruct((1, D), F32)],
                 sem=("arbitrary",))(x, g, tgt)


def _slabs(tm, n=2):
    return [slice(k * tm // n, (k + 1) * tm // n) for k in range(n)] if tm % (16 * n) == 0 else [slice(0, tm)]


def _resident(shape):
    return pl.BlockSpec(shape, lambda *_: (0,) * len(shape), pipeline_mode=pl.Buffered(1))


def _ffn_fwd_k(x, gn, wg, wu, wd):
    T, D = x.shape
    NS, _, Fs = wg.shape
    tm = _tile(T, 1024)

    def body(x_ref, gn_ref, wg_ref, wu_ref, wd_ref, o_ref, h_ref, s1_ref, s2_ref, a_ref, hs, acc):
        j = pl.program_id(1)

        @pl.when(j == 0)
        def _():
            xf = x_ref[...]
            r = lax.rsqrt(jnp.mean(xf * xf, axis=-1, keepdims=True) + RMS_EPS)
            hs[...] = (xf * r * gn_ref[...]).astype(BF)
            h_ref[...] = hs[...]
            acc[...] = jnp.zeros_like(acc)

        h = hs[...]
        g = _dot(h, wg_ref[...])
        u = _dot(h, wu_ref[...])
        sg = _sigmoid(g)
        s1 = g * sg
        a = (s1 * u).astype(BF)
        s1_ref[...] = s1.astype(BF)
        s2_ref[...] = (u * (sg * (1.0 + g * (1.0 - sg)))).astype(BF)
        a_ref[...] = a
        acc[...] += _dot(a, wd_ref[...])

        @pl.when(j == NS - 1)
        def _():
            o_ref[...] = x_ref[...] + 0.5 * acc[...]

    row = pl.BlockSpec((tm, D), lambda i, j: (i, 0))
    act = pl.BlockSpec((None, tm, Fs), lambda i, j: (j, i, 0))
    sh = jax.ShapeDtypeStruct((NS, T, Fs), BF)
    wspec = lambda w: pl.BlockSpec((None,) + w.shape[1:], lambda i, j: (j, 0, 0))
    return _call(body, name="ffn_fwd", grid=(T // tm, NS),
                 in_specs=[row, pl.BlockSpec((1, D), lambda i, j: (0, 0)), wspec(wg), wspec(wu), wspec(wd)],
                 out_specs=[row, row, act, act, act],
                 out_shape=[jax.ShapeDtypeStruct((T, D), F32), jax.ShapeDtypeStruct((T, D), BF), sh, sh, sh],
                 scratch=[pltpu.VMEM((tm, D), BF), pltpu.VMEM((tm, D), F32)],
                 sem=("parallel", "arbitrary"), vmem=FFN_VMEM)(x, gn, wg, wu, wd)


def _ffn_bwd_k1(dxo, x, gn, s1, s2, wg, wu, wd):
    NS, T, Fs = s1.shape
    D = x.shape[1]
    tm = _tile(T, 512)

    def body(dxo_ref, x_ref, gn_ref, s1_ref, s2_ref, wg_ref, wu_ref, wd_ref,
             dx_ref, dgn_ref, dg_ref, du_ref, dy_ref, dys, acc):
        i, j = pl.program_id(0), pl.program_id(1)

        @pl.when((i == 0) & (j == 0))
        def _():
            dgn_ref[...] = jnp.zeros_like(dgn_ref)

        @pl.when(j == 0)
        def _():
            dys[...] = (0.5 * dxo_ref[...]).astype(BF)
            dy_ref[...] = dys[...]
            acc[...] = jnp.zeros_like(acc)

        for rows in _slabs(tm):
            da = _dot_nt(dys[rows, :], wd_ref[j])
            dg = (da * s2_ref[rows, :].astype(F32)).astype(BF)
            du = (da * s1_ref[rows, :].astype(F32)).astype(BF)
            dg_ref[rows, :] = dg
            du_ref[rows, :] = du
            acc[rows, :] += _dot_nt(dg, wg_ref[j]) + _dot_nt(du, wu_ref[j])

        @pl.when(j == NS - 1)
        def _():
            xf = x_ref[...]
            r = lax.rsqrt(jnp.mean(xf * xf, axis=-1, keepdims=True) + RMS_EPS)
            dh = acc[...]
            uu = dh * gn_ref[...]
            mu = jnp.mean(uu * xf, axis=-1, keepdims=True)
            dx_ref[...] = dxo_ref[...] + r * (uu - xf * (r * r * mu))
            dgn_ref[...] += jnp.sum(dh * xf * r, axis=0, keepdims=True)

    row = pl.BlockSpec((tm, D), lambda i, j: (i, 0))
    vec = pl.BlockSpec((1, D), lambda i, j: (0, 0))
    act = pl.BlockSpec((None, tm, Fs), lambda i, j: (j, i, 0))
    sh = jax.ShapeDtypeStruct((NS, T, Fs), BF)
    return _call(body, name="ffn_bwd_x", grid=(T // tm, NS),
                 in_specs=[row, row, vec, act, act, _resident(wg.shape), _resident(wu.shape), _resident(wd.shape)],
                 out_specs=[row, vec, act, act, row],
                 out_shape=[jax.ShapeDtypeStruct((T, D), F32), jax.ShapeDtypeStruct((1, D), F32), sh, sh,
                            jax.ShapeDtypeStruct((T, D), BF)],
                 scratch=[pltpu.VMEM((tm, D), BF), pltpu.VMEM((tm, D), F32)],
                 sem=("arbitrary", "arbitrary"))(dxo, x, gn, s1, s2, wg, wu, wd)


def _ffn_bwd_k2(hb, dyb, a, dg, du):
    NS, T, Fs = a.shape
    D = hb.shape[1]
    tk = _tile(T, 1024)

    def body(h_ref, dy_ref, a_ref, dg_ref, du_ref, og_ref, ou_ref, od_ref):
        @pl.when(pl.program_id(1) == 0)
        def _():
            og_ref[...] = jnp.zeros_like(og_ref)
            ou_ref[...] = jnp.zeros_like(ou_ref)
            od_ref[...] = jnp.zeros_like(od_ref)

        h = h_ref[...]
        og_ref[...] += _dot_tn(dg_ref[...], h)
        ou_ref[...] += _dot_tn(du_ref[...], h)
        od_ref[...] += _dot_tn(a_ref[...], dy_ref[...])

    row = pl.BlockSpec((tk, D), lambda j, k: (k, 0))
    act = pl.BlockSpec((None, tk, Fs), lambda j, k: (j, k, 0))
    return _call(body, name="ffn_bwd_w", grid=(NS, T // tk), in_specs=[row, row, act, act, act],
                 out_specs=[pl.BlockSpec((None, Fs, D), lambda j, k: (j, 0, 0))] * 3,
                 out_shape=[jax.ShapeDtypeStruct((NS, Fs, D), F32)] * 3,
                 sem=("parallel", "arbitrary"))(hb, dyb, a, dg, du)


def _mm_nn(a, b, res=None, out_dtype=F32):
    T, K = a.shape
    N = b.shape[1]
    tm = _tile(T, 512)
    tn = N if N <= 2048 else _tile(N, 1024)

    def body(*refs):
        if res is None:
            a_ref, b_ref, o_ref = refs
            o_ref[...] = _dot(a_ref[...], b_ref[...]).astype(out_dtype)
        else:
            a_ref, b_ref, r_ref, o_ref = refs
            o_ref[...] = (r_ref[...] + _dot(a_ref[...], b_ref[...])).astype(out_dtype)

    o = pl.BlockSpec((tm, tn), lambda i, j: (i, j))
    ins = [pl.BlockSpec((tm, K), lambda i, j: (i, 0)), pl.BlockSpec((K, tn), lambda i, j: (0, j))]
    args = [a, b]
    if res is not None:
        ins.append(o)
        args.append(res)
    return _call(body, name="mm_nn", grid=(T // tm, N // tn), in_specs=ins, out_specs=o,
                 out_shape=jax.ShapeDtypeStruct((T, N), out_dtype), sem=("parallel", "parallel"))(*args)


def _mm_nt(a, b, res=None):
    T, K = a.shape
    N = b.shape[0]
    tm = _tile(T, 512)

    def body(*refs):
        if res is None:
            a_ref, b_ref, o_ref = refs
            o_ref[...] = _dot_nt(a_ref[...].astype(BF), b_ref[...])
        else:
            a_ref, b_ref, r_ref, o_ref = refs
            o_ref[...] = r_ref[...] + _dot_nt(a_ref[...].astype(BF), b_ref[...])

    o = pl.BlockSpec((tm, N), lambda i: (i, 0))
    ins = [pl.BlockSpec((tm, K), lambda i: (i, 0)), pl.BlockSpec((N, K), lambda i: (0, 0))]
    args = [a, b]
    if res is not None:
        ins.append(o)
        args.append(res)
    return _call(body, name="mm_nt", grid=(T // tm,), in_specs=ins, out_specs=o,
                 out_shape=jax.ShapeDtypeStruct((T, N), F32), sem=("parallel",))(*args)


def _mm_tn(a, b):
    T, M = a.shape
    N = b.shape[1]
    tk = _tile(T, 1024)
    tmm = _tile(M, 512)

    def body(a_ref, b_ref, o_ref):
        @pl.when(pl.program_id(1) == 0)
        def _():
            o_ref[...] = jnp.zeros_like(o_ref)

        o_ref[...] += _dot_tn(a_ref[...].astype(BF), b_ref[...].astype(BF))

    return _call(body, name="mm_tn", grid=(M // tmm, T // tk),
                 in_specs=[pl.BlockSpec((tk, tmm), lambda i, k: (k, i)), pl.BlockSpec((tk, N), lambda i, k: (k, 0))],
                 out_specs=pl.BlockSpec((tmm, N), lambda i, k: (i, 0)),
                 out_shape=jax.ShapeDtypeStruct((M, N), F32), sem=("parallel", "arbitrary"))(a, b)


def _mix_proj(x, gn, win):
    T, D = x.shape
    tm = _tile(T, 512)
    cuts = (0, W_QKV, W_QKV + W_SSD, W_QKV + W_SSD + W_UV)

    def body(x_ref, gn_ref, w_ref, h_ref, q_ref, s_ref, u_ref):
        xf = x_ref[...]
        r = lax.rsqrt(jnp.mean(xf * xf, axis=-1, keepdims=True) + RMS_EPS)
        h = (xf * r * gn_ref[...]).astype(BF)
        h_ref[...] = h
        for o_ref, lo, hi in zip((q_ref, s_ref, u_ref), cuts[:-1], cuts[1:]):
            o_ref[...] = _dot(h, w_ref[:, lo:hi])

    row = lambda w: pl.BlockSpec((tm, w), lambda i: (i, 0))
    return _call(body, name="mix_proj", grid=(T // tm,),
                 in_specs=[row(D), pl.BlockSpec((1, D), lambda i: (0, 0)), _resident(win.shape)],
                 out_specs=[row(D), row(W_QKV), row(W_SSD), row(W_UV)],
                 out_shape=[jax.ShapeDtypeStruct((T, D), BF), jax.ShapeDtypeStruct((T, W_QKV), F32),
                            jax.ShapeDtypeStruct((T, W_SSD), F32), jax.ShapeDtypeStruct((T, W_UV), F32)],
                 sem=("parallel",))(x, gn, win)


def _mix_bwd_dx(dqkv, dsin, duv, win, x, gn, dxo):
    T, D = x.shape
    tm = _tile(T, 512)
    cuts = (0, W_QKV, W_QKV + W_SSD, W_QKV + W_SSD + W_UV)

    def body(dq_ref, ds_ref, du_ref, w_ref, x_ref, gn_ref, dxo_ref, dx_ref, dgn_ref):
        @pl.when(pl.program_id(0) == 0)
        def _():
            dgn_ref[...] = jnp.zeros_like(dgn_ref)

        dh = (_dot_nt(dq_ref[...], w_ref[:, cuts[0]:cuts[1]]) + _dot_nt(ds_ref[...], w_ref[:, cuts[1]:cuts[2]])
              + _dot_nt(du_ref[...], w_ref[:, cuts[2]:cuts[3]]))
        xf = x_ref[...]
        r = lax.rsqrt(jnp.mean(xf * xf, axis=-1, keepdims=True) + RMS_EPS)
        uu = dh * gn_ref[...]
        mu = jnp.mean(uu * xf, axis=-1, keepdims=True)
        dx_ref[...] = dxo_ref[...] + r * (uu - xf * (r * r * mu))
        dgn_ref[...] += jnp.sum(dh * xf * r, axis=0, keepdims=True)

    row = lambda w: pl.BlockSpec((tm, w), lambda i: (i, 0))
    vec = pl.BlockSpec((1, D), lambda i: (0, 0))
    return _call(body, name="mix_bwd_dx", grid=(T // tm,),
                 in_specs=[row(W_QKV), row(W_SSD), row(W_UV), _resident(win.shape), row(D), vec, row(D)],
                 out_specs=[row(D), vec],
                 out_shape=[jax.ShapeDtypeStruct((T, D), F32), jax.ShapeDtypeStruct((1, D), F32)],
                 sem=("arbitrary",))(dqkv, dsin, duv, win, x, gn, dxo)


def _lane_mask(e, width=128):
    return (lax.broadcasted_iota(jnp.int32, (1, width), 1) // HEAD) == e


def _band_mask(n):
    qi = lax.broadcasted_iota(jnp.int32, (CHUNK, 2 * CHUNK), 0)
    kj = lax.broadcasted_iota(jnp.int32, (CHUNK, 2 * CHUNK), 1)
    dist = qi + CHUNK - kj
    return (dist >= 0) & (dist <= CHUNK) & ((kj >= CHUNK) | (n > 0))


def _sub_rows(r, block, dil):
    if dil == 1:
        return pl.ds(pl.multiple_of(block * CHUNK, CHUNK), CHUNK)
    return pl.ds(r + dil * CHUNK * block, CHUNK, stride=dil)


def _attn_specs(T, dil):
    B, nb = T // SEQ, SEQ // (CHUNK * dil)
    once = dict(pipeline_mode=pl.Buffered(1))
    q_like = lambda col: pl.BlockSpec((CHUNK * dil, 128), lambda b, n, r: (b * nb + n, col), **(once if nb == 1 else {}))
    k_like = lambda col: pl.BlockSpec((SEQ, 128), lambda b, n, r: (b, col), **once)
    return B, nb, q_like, k_like


def _attn_fwd(qkv, dil):
    T = qkv.shape[0]
    B, nb, q_like, k_like = _attn_specs(T, dil)
    scale = HEAD ** -0.5

    def body(*refs):
        q_t, k_t, v_t, o_t, l_t = refs[0:3], refs[3:6], refs[6:9], refs[9:12], refs[12:15]
        n, r = pl.program_id(1), pl.program_id(2)
        mine = _sub_rows(r, 0, dil)
        cur, prv = _sub_rows(r, n, dil), _sub_rows(r, jnp.maximum(n - 1, 0), dil)
        mask = _band_mask(n)
        for t in range(3):
            qt = q_t[t][mine, :].astype(BF)
            kt = jnp.concatenate([k_t[t][prv, :], k_t[t][cur, :]], axis=0).astype(BF)
            vt = jnp.concatenate([v_t[t][prv, :], v_t[t][cur, :]], axis=0).astype(BF)
            o_pair = jnp.zeros((CHUNK, 128), F32)
            l_pair = jnp.zeros((CHUNK, 128), F32)
            for e in range(2):
                lm = _lane_mask(e)
                s = _dot_nt(jnp.where(lm, qt, jnp.zeros_like(qt)), kt) * scale
                s = jnp.where(mask, s, NEG)
                m = jnp.max(s, axis=-1, keepdims=True)
                p = jnp.exp(s - m)
                den = jnp.sum(p, axis=-1, keepdims=True)
                o = _dot(p.astype(BF), vt) / den
                o_pair = jnp.where(lm, o, o_pair)
                l_pair = jnp.where(lm, m + jnp.log(den), l_pair)
            o_t[t][mine, :] = o_pair
            l_t[t][mine, :] = l_pair

    out_spec = pl.BlockSpec((CHUNK * dil, 128), lambda b, n, r: (b * nb + n, 0))
    sh = jax.ShapeDtypeStruct((T, 128), F32)
    outs = _call(
        body, name=f"attn_fwd_d{dil}", grid=(B, nb, dil),
        in_specs=[q_like(t) for t in range(3)] + [k_like(3 + t) for t in range(3)] + [k_like(6 + t) for t in range(3)],
        out_specs=[out_spec] * 6, out_shape=[sh] * 6, sem=("parallel", "arbitrary", "arbitrary"))(*([qkv] * 9))
    return list(outs[0:3]), list(outs[3:6])


def _attn_combine(branches):
    T = branches[0][0][0].shape[0]
    tm = _tile(T, 512)

    def body(*refs):
        y_ref, l_ref = refs[-2:]
        for t in range(3):
            o = [refs[6 * i + t][...] for i in range(3)]
            a, b, c = [refs[6 * i + 3 + t][...] for i in range(3)]
            m = jnp.maximum(jnp.maximum(a, b), c)
            ea, eb, ec = jnp.exp(a - m), jnp.exp(b - m), jnp.exp(c - m)
            z = ea + eb + ec
            y_ref[:, 128 * t:128 * (t + 1)] = (ea * o[0] + eb * o[1] + ec * o[2]) / z
            l_ref[:, 128 * t:128 * (t + 1)] = m + jnp.log(z)

    tile = pl.BlockSpec((tm, 128), lambda i: (i, 0))
    row = pl.BlockSpec((tm, ATT_W), lambda i: (i, 0))
    sh = jax.ShapeDtypeStruct((T, ATT_W), F32)
    flat = [a for o_t, l_t in branches for a in (*o_t, *l_t)]
    return _call(body, name="attn_combine", grid=(T // tm,), in_specs=[tile] * 18, out_specs=[row, row],
                 out_shape=[sh, sh], sem=("parallel",))(*flat)


def _attn_bwd(qkv, do, out, lse, dil):
    T = qkv.shape[0]
    B, nb, q_like, k_like = _attn_specs(T, dil)
    scale = HEAD ** -0.5

    def body(*refs):
        q_t, k_t, v_t = refs[0:3], refs[3:6], refs[6:9]
        do_t, out_t, lse_t = refs[9:12], refs[12:15], refs[15:18]
        dq_t, dk_t, dv_t = refs[18:21], refs[21:24], refs[24:27]
        n, r = pl.program_id(1), pl.program_id(2)

        @pl.when((n == 0) & (r == 0))
        def _():
            for t in range(3):
                dk_t[t][...] = jnp.zeros_like(dk_t[t])
                dv_t[t][...] = jnp.zeros_like(dv_t[t])

        mine = _sub_rows(r, 0, dil)
        cur, prv = _sub_rows(r, n, dil), _sub_rows(r, jnp.maximum(n - 1, 0), dil)
        mask = _band_mask(n)
        for t in range(3):
            qt = q_t[t][mine, :].astype(BF)
            kt = jnp.concatenate([k_t[t][prv, :], k_t[t][cur, :]], axis=0).astype(BF)
            vt = jnp.concatenate([v_t[t][prv, :], v_t[t][cur, :]], axis=0).astype(BF)
            do_ = do_t[t][mine, :]
            dlt = do_ * out_t[t][mine, :]
            ls = lse_t[t][mine, :]
            dq_pair = jnp.zeros((CHUNK, 128), F32)
            dk_acc = jnp.zeros((2 * CHUNK, 128), F32)
            dv_acc = jnp.zeros((2 * CHUNK, 128), F32)
            for e in range(2):
                lm = _lane_mask(e)
                qm = jnp.where(lm, qt, jnp.zeros_like(qt))
                s = _dot_nt(qm, kt) * scale
                p = jnp.exp(jnp.where(mask, s - ls[:, HEAD * e:HEAD * e + 1], NEG))
                dom = jnp.where(lm, do_, 0.0).astype(BF)
                dv_acc += _dot_tn(p.astype(BF), dom)
                dp = _dot_nt(dom, vt)
                delta = jnp.sum(jnp.where(lm, dlt, 0.0), axis=-1, keepdims=True)
                ds = (p * (dp - delta) * scale).astype(BF)
                dq_pair += jnp.where(lm, _dot(ds, kt), 0.0)
                dk_acc += _dot_tn(ds, qm)
            dq_t[t][mine, :] = dq_pair
            dk_t[t][cur, :] = dk_t[t][cur, :] + dk_acc[CHUNK:]
            dk_t[t][prv, :] = dk_t[t][prv, :] + dk_acc[:CHUNK]
            dv_t[t][cur, :] = dv_t[t][cur, :] + dv_acc[CHUNK:]
            dv_t[t][prv, :] = dv_t[t][prv, :] + dv_acc[:CHUNK]

    q_out = pl.BlockSpec((CHUNK * dil, 128), lambda b, n, r: (b * nb + n, 0))
    k_out = pl.BlockSpec((SEQ, 128), lambda b, n, r: (b, 0))
    sh = jax.ShapeDtypeStruct((T, 128), F32)
    tiles = lambda: [q_like(t) for t in range(3)]
    return list(_call(
        body, name=f"attn_bwd_d{dil}", grid=(B, nb, dil),
        in_specs=tiles() + [k_like(3 + t) for t in range(3)] + [k_like(6 + t) for t in range(3)]
        + tiles() + tiles() + tiles(),
        out_specs=[q_out] * 3 + [k_out] * 6, out_shape=[sh] * 9,
        sem=("parallel", "arbitrary", "arbitrary"), vmem=ATTN_BWD_VMEM)(*([qkv] * 9 + [do] * 3 + [out] * 3 + [lse] * 3)))


def _sum_branches(parts):
    T = parts[0][0].shape[0]
    tm = _tile(T, 512)

    def body(*refs):
        o_ref = refs[-1]
        for c in range(9):
            acc = refs[c][...] + refs[9 + c][...] + refs[18 + c][...]
            o_ref[:, 128 * c:128 * (c + 1)] = acc.astype(BF)

    tile = pl.BlockSpec((tm, 128), lambda i: (i, 0))
    flat = [a for br in parts for a in br]
    return _call(body, name="attn_sum_branches", grid=(T // tm,), in_specs=[tile] * 27,
                 out_specs=pl.BlockSpec((tm, W_QKV), lambda i: (i, 0)),
                 out_shape=jax.ShapeDtypeStruct((T, W_QKV), BF), sem=("parallel",))(*flat)


def _silu(x):
    return x * _sigmoid(x)


def _dsilu(x):
    s = _sigmoid(x)
    return s * (1.0 + x * (1.0 - s))


def _log1p(u):
    return jnp.where(u < 0.01, u * (1.0 - u * (0.5 - u * (1.0 / 3.0))), jnp.log(1.0 + u))


def _softplus(x):
    return jnp.maximum(x, 0.0) + _log1p(jnp.exp(-jnp.abs(x)))


def _cumsum_rows(x, reverse=False):
    n = x.shape[0]
    rows = lax.broadcasted_iota(jnp.int32, x.shape, 0)
    k = 1
    while k < n:
        if reverse:
            x = x + jnp.where(rows < n - k, pltpu.roll(x, n - k, 0), 0.0)
        else:
            x = x + jnp.where(rows >= k, pltpu.roll(x, k, 0), 0.0)
        k *= 2
    return x


def _tri():
    r = lax.broadcasted_iota(jnp.int32, (CHUNK, CHUNK), 0)
    c = lax.broadcasted_iota(jnp.int32, (CHUNK, CHUNK), 1)
    return r >= c


def _row_mask(e):
    return (lax.broadcasted_iota(jnp.int32, (128, 1), 0) // HEAD) == e


def _first_lane(e):
    return lax.broadcasted_iota(jnp.int32, (1, 128), 1) == HEAD * e


def _ssd_pre(x_ref, halo_ref, first, cw_ref, cb_ref, dtb_ref, al_ref, ext):
    row = x_ref[...]
    z = row[:, SSD_CONV_DIM:SSD_CONV_DIM + SSD_W]
    u = row[:, SSD_CONV_DIM + SSD_W:] + dtb_ref[...]
    ext[0:8, :] = jnp.where(first, 0.0, halo_ref[:, 0:SSD_CONV_DIM])
    ext[8:8 + CHUNK, :] = row[:, 0:SSD_CONV_DIM]
    xc = cb_ref[...]
    for j in range(4):
        xc = xc + cw_ref[j:j + 1, :] * ext[pl.ds(5 + j, CHUNK), :]
    xa = _silu(xc)
    dt = _softplus(u)
    a = dt * (-jnp.exp(al_ref[...]))
    A = _cumsum_rows(a)
    return dict(z=z, u=u, xc=xc, xs=xa[:, 0:SSD_W], Bm=xa[:, SSD_W:SSD_W + 256], Cm=xa[:, SSD_W + 256:],
                dt=dt, a=a, A=A, AT=A.T, eA=jnp.exp(A), wdec=jnp.exp(A[CHUNK - 1:CHUNK, :] - A),
                dtot=jnp.exp(A[CHUNK - 1:CHUNK, :]))


def _ssd_y(p, hp_ref, dskip):
    tri = _tri()
    X = p["xs"] * p["dt"]
    Bb = [p["Bm"][:, 128 * g:128 * (g + 1)].astype(BF) for g in range(2)]
    Cb = [p["Cm"][:, 128 * g:128 * (g + 1)].astype(BF) for g in range(2)]
    CB = [_dot_nt(Cb[g], Bb[g]) for g in range(2)]
    tiles = []
    for t in range(3):
        sl = slice(128 * t, 128 * (t + 1))
        hpb = hp_ref[sl, :].astype(BF)
        acc = jnp.zeros((CHUNK, 128), F32)
        for e in range(2):
            h = 2 * t + e
            g, col = h // 3, HEAD * h
            lm = _lane_mask(e)
            L = jnp.exp(jnp.where(tri, p["A"][:, col:col + 1] - p["AT"][col:col + 1, :], NEG))
            yd = _dot((CB[g] * L).astype(BF), jnp.where(lm, X[:, sl], 0.0).astype(BF))
            yo = _dot_nt(Cb[g], hpb) * p["eA"][:, sl]
            acc = acc + jnp.where(lm, yd + yo, 0.0)
        tiles.append(acc)
    return jnp.concatenate(tiles, axis=1) + dskip * p["xs"], X, Bb, Cb, CB


def _group_stats(v):
    g0 = lax.broadcasted_iota(jnp.int32, (1, SSD_W), 1) < SSD_W // 2
    m0 = jnp.sum(jnp.where(g0, v, 0.0), axis=-1, keepdims=True) * (2.0 / SSD_W)
    m1 = jnp.sum(jnp.where(g0, 0.0, v), axis=-1, keepdims=True) * (2.0 / SSD_W)
    return jnp.where(g0, m0, m1)


def _ssd_specs(T, rev):
    B = T // SEQ

    def chunk(b, c):
        return b * N_CHUNK + (N_CHUNK - 1 - c if rev else c)

    row = pl.BlockSpec((CHUNK, W_SSD), lambda b, c: (chunk(b, c), 0))
    halo = pl.BlockSpec((8, W_SSD), lambda b, c: (jnp.maximum(chunk(b, c) * (CHUNK // 8) - 1, 0), 0))
    hp = pl.BlockSpec((None, SSD_W, SSD_STATE), lambda b, c: (chunk(b, c), 0, 0))
    y = pl.BlockSpec((CHUNK, SSD_W), lambda b, c: (chunk(b, c), 0))
    const = lambda r, w: pl.BlockSpec((r, w), lambda b, c: (0, 0))
    params = [const(4, SSD_CONV_DIM), const(1, SSD_CONV_DIM)] + [const(1, SSD_W)] * 4
    return B, row, halo, hp, y, const, params


def _ssd_fwd(sin, conv_w, conv_b, dtb, alog, dskip, norm_g):
    T = sin.shape[0]
    B, row, halo, hp, y, const, params = _ssd_specs(T, False)

    def body(x_ref, halo_ref, cw_ref, cb_ref, dtb_ref, al_ref, dk_ref, ng_ref, y_ref, hp_ref, ext, hst):
        c = pl.program_id(1)

        @pl.when(c == 0)
        def _():
            hst[...] = jnp.zeros_like(hst)

        p = _ssd_pre(x_ref, halo_ref, c == 0, cw_ref, cb_ref, dtb_ref, al_ref, ext)
        yv, X, Bb, Cb, CB = _ssd_y(p, hst, dk_ref[...])
        hp_ref[...] = hst[...]
        for t in range(3):
            sl = slice(128 * t, 128 * (t + 1))
            old = hst[sl, :]
            new = old
            for e in range(2):
                h = 2 * t + e
                g, col = h // 3, HEAD * h
                st = _dot_tn(jnp.where(_lane_mask(e), X[:, sl] * p["wdec"][:, sl], 0.0).astype(BF), Bb[g])
                new = jnp.where(_row_mask(e), old * p["dtot"][:, col:col + 1] + st, new)
            hst[sl, :] = new
        y2 = yv * _silu(p["z"])
        r = lax.rsqrt(_group_stats(y2 * y2) + RMS_EPS)
        y_ref[...] = y2 * r * ng_ref[...]

    return _call(body, name="ssd_fwd", grid=(B, N_CHUNK), in_specs=[row, halo] + params, out_specs=[y, hp],
                 out_shape=[jax.ShapeDtypeStruct((T, SSD_W), F32),
                            jax.ShapeDtypeStruct((T // CHUNK, SSD_W, SSD_STATE), F32)],
                 scratch=[pltpu.VMEM((8 + CHUNK, SSD_CONV_DIM), F32), pltpu.VMEM((SSD_W, SSD_STATE), F32)],
                 sem=("parallel", "arbitrary"))(sin, sin, conv_w, conv_b, dtb, alog, dskip, norm_g)


def _ssd_bwd(sin, hprev, dy3, conv_w, conv_b, dtb, alog, dskip, norm_g):
    T = sin.shape[0]
    B, row, halo, hp, y, const, params = _ssd_specs(T, True)

    def body(x_ref, halo_ref, hp_ref, dy_ref, cw_ref, cb_ref, dtb_ref, al_ref, dk_ref, ng_ref,
             dx_ref, dcw_ref, dcb_ref, dvec_ref, ext, ext2, dh):
        c = pl.program_id(1)

        @pl.when((pl.program_id(0) == 0) & (c == 0))
        def _():
            dcw_ref[...] = jnp.zeros_like(dcw_ref)
            dcb_ref[...] = jnp.zeros_like(dcb_ref)
            dvec_ref[...] = jnp.zeros_like(dvec_ref)

        @pl.when(c == 0)
        def _():
            dh[...] = jnp.zeros_like(dh)
            ext2[CHUNK:CHUNK + 8, :] = jnp.zeros((8, SSD_CONV_DIM), F32)

        p = _ssd_pre(x_ref, halo_ref, c == N_CHUNK - 1, cw_ref, cb_ref, dtb_ref, al_ref, ext)
        dskip_ = dk_ref[...]
        yv, X, Bb, Cb, CB = _ssd_y(p, hp_ref, dskip_)
        xs, z, A, AT = p["xs"], p["z"], p["A"], p["AT"]

        sz = _silu(z)
        y2 = yv * sz
        r = lax.rsqrt(_group_stats(y2 * y2) + RMS_EPS)
        dy3_ = dy_ref[...]
        uu = dy3_ * ng_ref[...]
        dy2 = r * (uu - y2 * (r * r * _group_stats(uu * y2)))
        dy = dy2 * sz
        dz = dy2 * yv * _dsilu(z)

        tri = _tri()
        rows = lax.broadcasted_iota(jnp.int32, (CHUNK, 1), 0)
        dG = [jnp.zeros((CHUNK, CHUNK), F32) for _ in range(2)]
        dB = [jnp.zeros((CHUNK, SSD_STATE), F32) for _ in range(2)]
        dC = [jnp.zeros((CHUNK, SSD_STATE), F32) for _ in range(2)]
        dX_t, dA_t, ddtx_t = [], [], []
        for t in range(3):
            sl = slice(128 * t, 128 * (t + 1))
            hp_t = hp_ref[sl, :]
            hpb = hp_t.astype(BF)
            dhc = dh[sl, :]
            dh_new = jnp.zeros((128, SSD_STATE), F32)
            dX = jnp.zeros((CHUNK, 128), F32)
            dA = jnp.zeros((CHUNK, 128), F32)
            ddtx = jnp.zeros((CHUNK, 128), F32)
            for e in range(2):
                h = 2 * t + e
                g, col = h // 3, HEAD * h
                lm, rm, fl = _lane_mask(e), _row_mask(e), _first_lane(e)
                L = jnp.exp(jnp.where(tri, A[:, col:col + 1] - AT[col:col + 1, :], NEG))
                Mf = CB[g] * L
                Xm = jnp.where(lm, X[:, sl], 0.0)
                Xmb = Xm.astype(BF)
                dyh = jnp.where(lm, dy[:, sl], 0.0)
                dyb = dyh.astype(BF)
                dXh = _dot_tn(Mf.astype(BF), dyb)
                dM = jnp.where(tri, _dot_nt(dyb, Xmb), 0.0)
                Wm = dM * Mf
                dAc = jnp.sum(Wm, axis=-1, keepdims=True) - jnp.sum(Wm.T, axis=-1, keepdims=True)
                dG[g] = dG[g] + dM * L
                eAt = p["eA"][:, sl]
                yo = _dot_nt(Cb[g], hpb)
                dyo = (dyh * eAt).astype(BF)
                dC[g] = dC[g] + _dot(dyo, hpb)
                dh_new = dh_new + _dot_tn(dyo, Cb[g])
                dAc = dAc + jnp.sum(dyh * yo * eAt, axis=-1, keepdims=True)
                dHn = jnp.where(rm, dhc, 0.0)
                dHnb = dHn.astype(BF)
                dec = p["dtot"][:, col:col + 1]
                dh_new = dh_new + dec * dHn
                Z = _dot_nt(Bb[g], dHnb)
                wt = p["wdec"][:, sl]
                xi = jnp.sum(Xm * Z, axis=-1, keepdims=True) * p["wdec"][:, col:col + 1]
                dXh = dXh + wt * Z
                dB[g] = dB[g] + _dot(jnp.where(lm, X[:, sl] * wt, 0.0).astype(BF), dHnb)
                dAtot = jnp.sum(xi, axis=0, keepdims=True) + dec * jnp.sum(
                    jnp.sum(dHn * hp_t, axis=-1, keepdims=True), axis=0, keepdims=True)
                dAc = dAc - xi + jnp.where(rows == CHUNK - 1, dAtot, 0.0)
                dA = dA + jnp.where(fl, dAc, 0.0)
                dX = dX + dXh
                ddtx = ddtx + jnp.where(fl, jnp.sum(dXh * xs[:, sl], axis=-1, keepdims=True), 0.0)
            dh[sl, :] = dh_new
            dX_t.append(dX)
            dA_t.append(dA)
            ddtx_t.append(ddtx)
        for g in range(2):
            dGb = dG[g].astype(BF)
            dC[g] = dC[g] + _dot(dGb, Bb[g])
            dB[g] = dB[g] + _dot_tn(dGb, Cb[g])
        dXf = jnp.concatenate(dX_t, axis=1)
        da = _cumsum_rows(jnp.concatenate(dA_t, axis=1), reverse=True)
        ddt = da * (-jnp.exp(al_ref[...])) + jnp.concatenate(ddtx_t, axis=1)
        du = ddt * _sigmoid(p["u"])
        dxs = dXf * p["dt"] + dskip_ * dy
        dxc = jnp.concatenate([dxs, dB[0], dB[1], dC[0], dC[1]], axis=1) * _dsilu(p["xc"])
        ext2[0:CHUNK, :] = dxc
        dxbc = jnp.zeros((CHUNK, SSD_CONV_DIM), F32)
        for j in range(4):
            dxbc = dxbc + cw_ref[j:j + 1, :] * ext2[pl.ds(3 - j, CHUNK), :]
            dcw_ref[j:j + 1, :] += jnp.sum(dxc * ext[pl.ds(5 + j, CHUNK), :], axis=0, keepdims=True)
        ext2[CHUNK:CHUNK + 8, :] = dxc[0:8, :]
        dcb_ref[...] += jnp.sum(dxc, axis=0, keepdims=True)
        dvec_ref[0:1, :] += jnp.sum(du, axis=0, keepdims=True)
        dvec_ref[1:2, :] += jnp.sum(da * p["a"], axis=0, keepdims=True)
        dvec_ref[2:3, :] += jnp.sum(dy * xs, axis=0, keepdims=True)
        dvec_ref[3:4, :] += jnp.sum(dy3_ * y2 * r, axis=0, keepdims=True)
        dx_ref[...] = jnp.concatenate([dxbc, dz, du], axis=1).astype(BF)

    return _call(body, name="ssd_bwd", grid=(B, N_CHUNK), in_specs=[row, halo, hp, y] + params,
                 out_specs=[row, const(4, SSD_CONV_DIM), const(1, SSD_CONV_DIM), const(8, SSD_W)],
                 out_shape=[jax.ShapeDtypeStruct((T, W_SSD), BF), jax.ShapeDtypeStruct((4, SSD_CONV_DIM), F32),
                            jax.ShapeDtypeStruct((1, SSD_CONV_DIM), F32), jax.ShapeDtypeStruct((8, SSD_W), F32)],
                 scratch=[pltpu.VMEM((8 + CHUNK, SSD_CONV_DIM), F32), pltpu.VMEM((8 + CHUNK, SSD_CONV_DIM), F32),
                          pltpu.VMEM((SSD_W, SSD_STATE), F32)],
                 sem=("arbitrary", "arbitrary"))(sin, sin, hprev, dy3, conv_w, conv_b, dtb, alog, dskip, norm_g)


def _sgu_core(uv_ref, g_ref, b_ref, w_ref, bias_ref):
    x = uv_ref[...]
    cdf = 0.5 * (1.0 + lax.erf(x * (2.0 ** -0.5)))
    ge = x * cdf
    dge = cdf + x * jnp.exp(-0.5 * x * x) * ((2.0 * math.pi) ** -0.5)
    u, v = ge[:, 0:SGU_W], ge[:, SGU_W:]
    vc = v - jnp.mean(v, axis=-1, keepdims=True)
    rstd = lax.rsqrt(jnp.mean(vc * vc, axis=-1, keepdims=True) + LN_EPS)
    vhat = vc * rstd
    vn = vhat * g_ref[...] + b_ref[...]
    tri = _tri()
    wc = [jnp.where(tri, w_ref[gi], 0.0).astype(BF) for gi in range(4)]
    vm = [jnp.where(_lane_mask(gi % 2), vn[:, 128 * (gi // 2):128 * (gi // 2 + 1)], 0.0).astype(BF) for gi in range(4)]
    mixed = jnp.concatenate([_dot(wc[2 * t], vm[2 * t]) + _dot(wc[2 * t + 1], vm[2 * t + 1]) for t in range(2)],
                            axis=1) + bias_ref[...]
    return dict(dge=dge, u=u, rstd=rstd, vhat=vhat, wc=wc, vm=vm, mixed=mixed)


def _sgu_specs():
    vec = pl.BlockSpec((1, SGU_W), lambda i: (0, 0))
    return [pl.BlockSpec((CHUNK, W_UV), lambda i: (i, 0)), vec, vec,
            pl.BlockSpec((4, CHUNK, CHUNK), lambda i: (0, 0, 0)), pl.BlockSpec((CHUNK, SGU_W), lambda i: (0, 0))]


def _sgu_fwd(uv, ln_g, ln_b, w, bias):
    T = uv.shape[0]

    def body(uv_ref, g_ref, b_ref, w_ref, bias_ref, y_ref):
        s = _sgu_core(uv_ref, g_ref, b_ref, w_ref, bias_ref)
        y_ref[...] = s["u"] * s["mixed"]

    return _call(body, name="sgu_fwd", grid=(T // CHUNK,), in_specs=_sgu_specs(),
                 out_specs=pl.BlockSpec((CHUNK, SGU_W), lambda i: (i, 0)),
                 out_shape=jax.ShapeDtypeStruct((T, SGU_W), F32), sem=("parallel",))(uv, ln_g, ln_b, w, bias)


def _sgu_bwd(uv, dy, ln_g, ln_b, w, bias):
    T = uv.shape[0]

    def body(uv_ref, dy_ref, g_ref, b_ref, w_ref, bias_ref, dx_ref, dw_ref, dbias_ref, dln_ref):
        @pl.when(pl.program_id(0) == 0)
        def _():
            dw_ref[...] = jnp.zeros_like(dw_ref)
            dbias_ref[...] = jnp.zeros_like(dbias_ref)
            dln_ref[...] = jnp.zeros_like(dln_ref)

        s = _sgu_core(uv_ref, g_ref, b_ref, w_ref, bias_ref)
        dy_ = dy_ref[...]
        du = dy_ * s["mixed"]
        dmix = dy_ * s["u"]
        dbias_ref[...] += dmix
        tri = _tri()
        dvn_t = []
        for t in range(2):
            acc = jnp.zeros((CHUNK, 128), F32)
            for e in range(2):
                gi = 2 * t + e
                dmg = jnp.where(_lane_mask(e), dmix[:, 128 * t:128 * (t + 1)], 0.0).astype(BF)
                acc = acc + _dot_tn(s["wc"][gi], dmg)
                dw_ref[gi] += jnp.where(tri, _dot_nt(dmg, s["vm"][gi]), 0.0)
            dvn_t.append(acc)
        dvn = jnp.concatenate(dvn_t, axis=1)
        dln_ref[0:1, :] += jnp.sum(dvn * s["vhat"], axis=0, keepdims=True)
        dln_ref[1:2, :] += jnp.sum(dvn, axis=0, keepdims=True)
        dvh = dvn * g_ref[...]
        dv = s["rstd"] * (dvh - jnp.mean(dvh, axis=-1, keepdims=True)
                          - s["vhat"] * jnp.mean(dvh * s["vhat"], axis=-1, keepdims=True))
        dx_ref[...] = (jnp.concatenate([du, dv], axis=1) * s["dge"]).astype(BF)

    ins = _sgu_specs()
    return _call(body, name="sgu_bwd", grid=(T // CHUNK,),
                 in_specs=[ins[0], pl.BlockSpec((CHUNK, SGU_W), lambda i: (i, 0))] + ins[1:],
                 out_specs=[pl.BlockSpec((CHUNK, W_UV), lambda i: (i, 0)),
                            pl.BlockSpec((4, CHUNK, CHUNK), lambda i: (0, 0, 0)),
                            pl.BlockSpec((CHUNK, SGU_W), lambda i: (0, 0)), pl.BlockSpec((8, SGU_W), lambda i: (0, 0))],
                 out_shape=[jax.ShapeDtypeStruct((T, W_UV), BF), jax.ShapeDtypeStruct((4, CHUNK, CHUNK), F32),
                            jax.ShapeDtypeStruct((CHUNK, SGU_W), F32), jax.ShapeDtypeStruct((8, SGU_W), F32)],
                 sem=("arbitrary",))(uv, dy, ln_g, ln_b, w, bias)


def _adamw(w, g, m, v):
    R, C = w.shape
    tr = R

    def body(w_ref, g_ref, m_ref, v_ref, d_ref, nm_ref, nv_ref):
        g_ = g_ref[...]
        m2 = ADAM_B1 * m_ref[...] + (1.0 - ADAM_B1) * g_
        v2 = ADAM_B2 * v_ref[...] + (1.0 - ADAM_B2) * (g_ * g_)
        m_hat = m2 / (1.0 - ADAM_B1 ** ADAM_STEP)
        v_hat = v2 / (1.0 - ADAM_B2 ** ADAM_STEP)
        d_ref[...] = -ADAM_LR * (m_hat / (jnp.sqrt(v_hat) + ADAM_EPS) + ADAM_WD * w_ref[...])
        nm_ref[...] = m2
        nv_ref[...] = v2

    blk = pl.BlockSpec((tr, C), lambda i: (i, 0))
    sh = jax.ShapeDtypeStruct((R, C), F32)
    return _call(body, name="adamw", grid=(R // tr,), in_specs=[blk] * 4, out_specs=[blk] * 3,
                 out_shape=[sh] * 3, sem=("parallel",))(w, g, m, v)


def _adamw_pair(w, g0, g1, m, v, dep):
    L, R, C = w.shape
    tr = _tile(R, 256 if C <= 1024 else 64)

    def body(w_ref, g0_ref, g1_ref, m_ref, v_ref, dep_ref, d_ref, nm_ref, nv_ref, og_ref):
        g_ = jnp.where(pl.program_id(0) == 0, g0_ref[...], g1_ref[...])
        m2 = ADAM_B1 * m_ref[...] + (1.0 - ADAM_B1) * g_
        v2 = ADAM_B2 * v_ref[...] + (1.0 - ADAM_B2) * (g_ * g_)
        m_hat = m2 / (1.0 - ADAM_B1 ** ADAM_STEP)
        v_hat = v2 / (1.0 - ADAM_B2 ** ADAM_STEP)
        d_ref[...] = -ADAM_LR * (m_hat / (jnp.sqrt(v_hat) + ADAM_EPS) + ADAM_WD * w_ref[...])
        nm_ref[...] = m2
        nv_ref[...] = v2
        og_ref[...] = g_

    lay = pl.BlockSpec((None, tr, C), lambda l, i: (l, i, 0))
    one = lambda k: pl.BlockSpec((tr, C), lambda l, i: (jnp.where(l == k, i, 0), 0))
    return _call(body, name="adamw_pair", grid=(L, R // tr),
                 in_specs=[lay, one(0), one(1), lay, lay, pl.BlockSpec((8, 128), lambda l, i: (0, 0))],
                 out_specs=[lay] * 4,
                 out_shape=[jax.ShapeDtypeStruct((L, R, C), F32)] * 4,
                 sem=("parallel", "parallel"))(w, g0, g1, m, v, dep)


def _row_steps(rows):
    return 2 if rows % 32 == 0 else 1


def _pair_add(gbuf, rsib, c):
    NS, _, R, C = gbuf.shape
    n = _row_steps(R)
    tr = R // n

    def body(c_ref, a_ref, b_ref, o_ref):
        o_ref[...] = (a_ref[...] + b_ref[...]).astype(BF)

    blk = pl.BlockSpec((None, tr, C), lambda j, i, c_ref: (j, i, 0))
    return pl.pallas_call(
        body, name="rs_pair_add",
        grid_spec=pltpu.PrefetchScalarGridSpec(
            num_scalar_prefetch=1, grid=(NS, n),
            in_specs=[pl.BlockSpec((None, None, tr, C), lambda j, i, c_ref: (j, c_ref[0], i, 0)), blk],
            out_specs=blk),
        out_shape=jax.ShapeDtypeStruct((NS, R, C), BF),
        compiler_params=pltpu.CompilerParams(dimension_semantics=("parallel", "parallel")),
    )(jnp.reshape(c, (1,)).astype(jnp.int32), gbuf, rsib)


def _chip_sum(pair, recv, me, c):
    NS, R, C = pair.shape
    n = _row_steps(R)
    tr = R // n

    def body(s_ref, own_ref, p_ref, o_ref):
        p = [jnp.where(s_ref[0] == j, own_ref[...], p_ref[j]).astype(F32) for j in range(4)]
        o_ref[...] = ((p[0] + p[1]) + p[2]) + p[3]

    return pl.pallas_call(
        body, name="rs_chip_sum",
        grid_spec=pltpu.PrefetchScalarGridSpec(
            num_scalar_prefetch=1, grid=(n,),
            in_specs=[pl.BlockSpec((None, tr, C), lambda i, s: (s[0], i, 0)),
                      pl.BlockSpec((NS, tr, C), lambda i, s: (0, i, 0))],
            out_specs=pl.BlockSpec((None, tr, C), lambda i, s: (s[1], i, 0))),
        out_shape=jax.ShapeDtypeStruct((2, R, C), F32),
        compiler_params=pltpu.CompilerParams(dimension_semantics=("parallel",)),
    )(jnp.stack([me, c]).astype(jnp.int32), pair, recv)


MESH = pl.DeviceIdType.MESH
ANY = pl.BlockSpec(memory_space=pl.ANY)


def _place():
    x, y, c = lax.axis_index("x"), lax.axis_index("y"), lax.axis_index("c")
    return x, y, c, [(1 - x, y), (x, 1 - y), (1 - x, 1 - y)]


HBM = pl.BlockSpec(memory_space=pltpu.HBM)
SEM = pl.BlockSpec(memory_space=pltpu.SEMAPHORE)
EFFECT = pltpu.SideEffectType.DATAFLOW_SIDE_EFFECTING


class _Split:
    def __init__(self, tag, arrays, copies, n_copies, after=()):
        self.tag, self.copies, k = tag, copies, len(arrays)

        def body(*refs):
            sems = k + len(after)
            for cp in copies(refs[:k], refs[sems], refs[sems + 1]):
                cp.start()
            refs[-1][...] = jnp.zeros_like(refs[-1])

        out = pl.pallas_call(
            body, name=tag + "_start",
            out_shape=(pltpu.SemaphoreType.DMA((n_copies,)), pltpu.SemaphoreType.DMA((n_copies,)),
                       *[pltpu.HBM(a.shape, a.dtype) for a in arrays], jax.ShapeDtypeStruct((8, 128), F32)),
            in_specs=[HBM] * k + [ANY] * len(after),
            out_specs=(SEM, SEM, *[HBM] * k, pl.BlockSpec(memory_space=pltpu.VMEM)),
            input_output_aliases={i: 2 + i for i in range(k)},
            compiler_params=pltpu.CompilerParams(has_side_effects=EFFECT),
        )(*[pltpu.with_memory_space_constraint(a, pltpu.HBM) for a in arrays], *after)
        self.send, self.recv, self.arrays, self.token_array = out[0], out[1], list(out[2:2 + k]), out[-1]
        self.token = self.token_array[0, 0]

    def wait(self, after):
        k, copies = len(self.arrays), self.copies
        after = list(after) if isinstance(after, (list, tuple)) else [after]

        def body(*refs):
            for cp in copies(refs[:k], refs[k], refs[k + 1]):
                cp.wait_send()
                cp.wait_recv()

        return list(pl.pallas_call(
            body, name=self.tag + "_wait", out_shape=tuple(pltpu.HBM(a.shape, a.dtype) for a in self.arrays),
            in_specs=[HBM] * k + [SEM, SEM] + [ANY] * len(after), out_specs=tuple([HBM] * k),
            input_output_aliases={i: i for i in range(k)},
            compiler_params=pltpu.CompilerParams(has_side_effects=EFFECT),
        )(*self.arrays, self.send, self.recv, *after))


def _landing_zones(arrs):
    me = 2 * lax.axis_index("x") + lax.axis_index("y")
    return [lax.dynamic_update_index_in_dim(lax.empty((4,) + a.shape, a.dtype), a, me, 0) for a in arrs]


def _gather_start(arrs, lands, tag, after=()):
    n = len(arrs)

    def copies(refs, send, recv):
        x, y, c, chips = _place()
        return [pltpu.make_async_remote_copy(
            src_ref=refs[k], dst_ref=refs[n + k].at[2 * x + y], send_sem=send.at[3 * k + r],
            recv_sem=recv.at[3 * k + r], device_id=(px, py, c), device_id_type=MESH)
            for k in range(n) for r, (px, py) in enumerate(chips)]

    return _Split("gather_" + tag, list(arrs) + lands, copies, 3 * n, after)


def _gather_halves_start(arrs, tag):
    n = len(arrs)
    lands = _landing_zones(arrs)

    def copies(refs, send, recv):
        x, y, c, chips = _place()
        return [pltpu.make_async_remote_copy(
            src_ref=refs[k].at[c], dst_ref=refs[n + k].at[2 * x + y, c], send_sem=send.at[3 * k + r],
            recv_sem=recv.at[3 * k + r], device_id=(px, py, c), device_id_type=MESH)
            for k in range(n) for r, (px, py) in enumerate(chips)]

    return _Split("gather_" + tag, list(arrs) + lands, copies, 3 * n)


def _gather_halves_finish(lands, tag):
    n = len(lands)

    def copies(refs, send, recv):
        x, y, c, chips = _place()
        return [pltpu.make_async_remote_copy(
            src_ref=refs[k].at[2 * px + py, c], dst_ref=refs[k].at[2 * px + py, c], send_sem=send.at[3 * k + r],
            recv_sem=recv.at[3 * k + r], device_id=(x, y, 1 - c), device_id_type=MESH)
            for k in range(n) for r, (px, py) in enumerate(chips)]

    return _Split("gather_pass_" + tag, list(lands), copies, 3 * n)


def _to_sibling_start(gbufs, tag):
    n = len(gbufs)

    def copies(refs, send, recv):
        x, y, c, _ = _place()
        return [pltpu.make_async_remote_copy(
            src_ref=refs[k].at[j, 1 - c], dst_ref=refs[n + k].at[j], send_sem=send.at[4 * k + j],
            recv_sem=recv.at[4 * k + j], device_id=(x, y, 1 - c), device_id_type=MESH)
            for k in range(n) for j in range(4)]

    lands = [lax.empty((4,) + g.shape[2:], g.dtype) for g in gbufs]
    return _Split("rs_sibling_" + tag, list(gbufs) + lands, copies, 4 * n)


def _to_chips_start(pbufs, tag):
    n = len(pbufs)

    def copies(refs, send, recv):
        x, y, c, chips = _place()
        return [pltpu.make_async_remote_copy(
            src_ref=refs[k].at[2 * px + py], dst_ref=refs[n + k].at[2 * x + y], send_sem=send.at[3 * k + r],
            recv_sem=recv.at[3 * k + r], device_id=(px, py, c), device_id_type=MESH)
            for k in range(n) for r, (px, py) in enumerate(chips)]

    return _Split("rs_chips_" + tag, list(pbufs) + [lax.empty(p.shape, p.dtype) for p in pbufs], copies, 3 * n)


def _join_start(fulls, tag):
    def copies(refs, send, recv):
        x, y, c, _ = _place()
        return [pltpu.make_async_remote_copy(
            src_ref=refs[k].at[c], dst_ref=refs[k].at[c], send_sem=send.at[k], recv_sem=recv.at[k],
            device_id=(x, y, 1 - c), device_id_type=MESH) for k in range(len(fulls))]

    return _Split("rs_join_" + tag, list(fulls), copies, len(fulls))


def _all_reduce_small(v):
    R, C = v.shape

    def body(v_ref, o_ref, g_ref, send, recv, loc):
        x, y, c, chips = _place()
        me, sibling = (x, y, c), (x, y, 1 - c)

        def rows(px, py, pc):
            return g_ref.at[4 * px + 2 * py + pc]

        def copy(k, block, to, src=None):
            return pltpu.make_async_remote_copy(
                src_ref=rows(*block) if src is None else src, dst_ref=rows(*block),
                send_sem=send.at[k], recv_sem=recv.at[k], device_id=to, device_id_type=MESH)

        mine = pltpu.make_async_copy(v_ref, rows(*me), loc)
        mine.start()
        first = [copy(0, me, sibling, src=v_ref)]
        first += [copy(1 + j, me, (*chip, c), src=v_ref) for j, chip in enumerate(chips)]
        for cp in first:
            cp.start()
        passed = [copy(4 + j, (*chip, c), sibling) for j, chip in enumerate(chips)]
        for j, chip in enumerate(chips):
            copy(1 + j, (*chip, c), me).wait_recv()
            passed[j].start()
        copy(0, sibling, me).wait_recv()
        for j, chip in enumerate(chips):
            copy(4 + j, (*chip, 1 - c), me).wait_recv()
        for cp in first + passed:
            cp.wait_send()
        mine.wait()
        acc = g_ref[0]
        for d in range(1, 8):
            acc = acc + g_ref[d]
        o_ref[...] = acc

    vm = pl.BlockSpec(memory_space=pltpu.VMEM)
    return pl.pallas_call(
        body, name="all_reduce_small", in_specs=[vm], out_specs=[vm, vm],
        out_shape=[jax.ShapeDtypeStruct((R, C), F32), jax.ShapeDtypeStruct((8, R, C), F32)],
        scratch_shapes=[pltpu.SemaphoreType.DMA((7,)), pltpu.SemaphoreType.DMA((7,)), pltpu.SemaphoreType.DMA],
    )(v)[0]


WEIGHTS = ['ffn1_norm', 'ffn1_w_gate', 'ffn1_w_up', 'ffn1_w_down', 'mix_norm', 'w_in', 'conv_w', 'conv_b', 'dt_bias',
           'a_log', 'd_skip', 'ssd_norm', 'sgu_ln_g', 'sgu_ln_b', 'sgu_w', 'sgu_b', 'w_out', 'ffn2_norm',
           'ffn2_w_gate', 'ffn2_w_up', 'ffn2_w_down', 'final_norm']
SHARDED = ['ffn1_w_gate', 'ffn1_w_up', 'ffn1_w_down', 'w_in', 'conv_w', 'w_out', 'ffn2_w_gate', 'ffn2_w_up',
           'ffn2_w_down']
SMALL = [n for n in WEIGHTS if n not in SHARDED]
GROUPS = [("ffn1", ["ffn1_w_gate", "ffn1_w_up", "ffn1_w_down"]), ("mix", ["w_in", "conv_w", "w_out"]),
          ("ffn2", ["ffn2_w_gate", "ffn2_w_up", "ffn2_w_down"])]
TRANSPOSED = ("ffn1_w_gate", "ffn1_w_up", "ffn2_w_gate", "ffn2_w_up")
DEPTH = 2


def _pack_w_in(w):
    return jnp.concatenate([w[..., 0:1152], w[..., 1536:2432], w[..., 1152:1536],
                            jnp.repeat(w[..., 2432:2438], HEAD, axis=-1), w[..., 2438:2950]], axis=-1)


def _unpack_w_in(dq, ds, du):
    return jnp.concatenate([dq, ds[:, 896:1280], ds[:, 0:896], ds[:, 1280::HEAD], du], axis=-1)


def _ffn_fwd(x, g, wg, wu, wd):
    xo, hb, S1, S2, A = _ffn_fwd_k(x, g, wg, wu, wd)
    return xo, (x, hb, S1, S2, A)


def _ffn_bwd(dxo, saved, g, wg, wu, wd):
    x, hb, S1, S2, A = saved
    dx, dg, dG, dU, dyb = _ffn_bwd_k1(dxo, x, g, S1, S2, wg, wu, wd)
    dwg, dwu, dwd = _ffn_bwd_k2(hb, dyb, A, dG, dU)
    return dx, dg, dwg, dwu, dwd


def _mix_fwd(x, P):
    hb, qkv, sin, uv = _mix_proj(x, P["mix_norm"], P["w_in"])
    y_att, lse = _attn_combine([_attn_fwd(qkv, d) for d in DILATIONS])
    y_ssd, hprev = _ssd_fwd(sin, *P["ssd"])
    y_sgu = _sgu_fwd(uv, *P["sgu"])
    ycat = jnp.concatenate([y_att, y_ssd, y_sgu], axis=1).astype(BF)
    return _mm_nn(ycat, P["w_out"], res=x), (x, hb, qkv, sin, uv, y_att, lse, hprev, ycat)


def _mix_bwd(dxo, saved, P):
    x, hb, qkv, sin, uv, y_att, lse, hprev, ycat = saved
    dycat = _mm_nt(dxo, P["w_out"])
    dwout = _mm_tn(ycat, dxo)
    dy_att, dy_ssd, dy_sgu = dycat[:, 0:ATT_W], dycat[:, ATT_W:ATT_W + SSD_W], dycat[:, ATT_W + SSD_W:]
    dqkv = _sum_branches([_attn_bwd(qkv, dy_att, y_att, lse, d) for d in DILATIONS])
    dsin, dcw, dcb, dvec = _ssd_bwd(sin, hprev, dy_ssd, *P["ssd"])
    duv, dsw, dsbias, dln = _sgu_bwd(uv, dy_sgu, *P["sgu"])
    dwin = _unpack_w_in(_mm_tn(hb, dqkv), _mm_tn(hb, dsin), _mm_tn(hb, duv))
    dx, dg = _mix_bwd_dx(dqkv, dsin, duv, P["w_in"], x, P["mix_norm"], dxo)
    grads = dict(
        mix_norm=dg[0], w_in=dwin, conv_w=dcw, conv_b=dcb[0], dt_bias=dvec[0, ::HEAD], a_log=dvec[1, ::HEAD],
        d_skip=jnp.sum(dvec[2].reshape(6, HEAD), axis=-1), ssd_norm=dvec[3], sgu_ln_g=dln[0], sgu_ln_b=dln[1],
        sgu_w=dsw, sgu_b=jnp.sum(dsbias.reshape(CHUNK, 4, HEAD), axis=-1).T, w_out=dwout)
    return dx, grads


def _halved(g):
    rows = g.size // g.shape[-1]
    return g.reshape(4, 2, rows // 8, g.shape[-1])


def kernel(x, ffn1_norm, ffn1_w_gate, ffn1_w_up, ffn1_w_down, mix_norm, w_in, conv_w, conv_b, dt_bias, a_log, d_skip, ssd_norm, sgu_ln_g, sgu_ln_b, sgu_w, sgu_b, w_out, ffn2_norm, ffn2_w_gate, ffn2_w_up, ffn2_w_down, final_norm, loss_target, m_ffn1_norm, m_ffn1_w_gate, m_ffn1_w_up, m_ffn1_w_down, m_mix_norm, m_w_in, m_conv_w, m_conv_b, m_dt_bias, m_a_log, m_d_skip, m_ssd_norm, m_sgu_ln_g, m_sgu_ln_b, m_sgu_w, m_sgu_b, m_w_out, m_ffn2_norm, m_ffn2_w_gate, m_ffn2_w_up, m_ffn2_w_down, m_final_norm, v_ffn1_norm, v_ffn1_w_gate, v_ffn1_w_up, v_ffn1_w_down, v_mix_norm, v_w_in, v_conv_w, v_conv_b, v_dt_bias, v_a_log, v_d_skip, v_ssd_norm, v_sgu_ln_g, v_sgu_ln_b, v_sgu_w, v_sgu_b, v_w_out, v_ffn2_norm, v_ffn2_w_gate, v_ffn2_w_up, v_ffn2_w_down, v_final_norm):
    given = dict(x=x, ffn1_norm=ffn1_norm, ffn1_w_gate=ffn1_w_gate, ffn1_w_up=ffn1_w_up, ffn1_w_down=ffn1_w_down, mix_norm=mix_norm, w_in=w_in, conv_w=conv_w, conv_b=conv_b, dt_bias=dt_bias, a_log=a_log, d_skip=d_skip, ssd_norm=ssd_norm, sgu_ln_g=sgu_ln_g, sgu_ln_b=sgu_ln_b, sgu_w=sgu_w, sgu_b=sgu_b, w_out=w_out, ffn2_norm=ffn2_norm, ffn2_w_gate=ffn2_w_gate, ffn2_w_up=ffn2_w_up, ffn2_w_down=ffn2_w_down, final_norm=final_norm, loss_target=loss_target, m_ffn1_norm=m_ffn1_norm, m_ffn1_w_gate=m_ffn1_w_gate, m_ffn1_w_up=m_ffn1_w_up, m_ffn1_w_down=m_ffn1_w_down, m_mix_norm=m_mix_norm, m_w_in=m_w_in, m_conv_w=m_conv_w, m_conv_b=m_conv_b, m_dt_bias=m_dt_bias, m_a_log=m_a_log, m_d_skip=m_d_skip, m_ssd_norm=m_ssd_norm, m_sgu_ln_g=m_sgu_ln_g, m_sgu_ln_b=m_sgu_ln_b, m_sgu_w=m_sgu_w, m_sgu_b=m_sgu_b, m_w_out=m_w_out, m_ffn2_norm=m_ffn2_norm, m_ffn2_w_gate=m_ffn2_w_gate, m_ffn2_w_up=m_ffn2_w_up, m_ffn2_w_down=m_ffn2_w_down, m_final_norm=m_final_norm, v_ffn1_norm=v_ffn1_norm, v_ffn1_w_gate=v_ffn1_w_gate, v_ffn1_w_up=v_ffn1_w_up, v_ffn1_w_down=v_ffn1_w_down, v_mix_norm=v_mix_norm, v_w_in=v_w_in, v_conv_w=v_conv_w, v_conv_b=v_conv_b, v_dt_bias=v_dt_bias, v_a_log=v_a_log, v_d_skip=v_d_skip, v_ssd_norm=v_ssd_norm, v_sgu_ln_g=v_sgu_ln_g, v_sgu_ln_b=v_sgu_ln_b, v_sgu_w=v_sgu_w, v_sgu_b=v_sgu_b, v_w_out=v_w_out, v_ffn2_norm=v_ffn2_norm, v_ffn2_w_gate=v_ffn2_w_gate, v_ffn2_w_up=v_ffn2_w_up, v_ffn2_w_down=v_ffn2_w_down, v_final_norm=v_final_norm)
    T = given["x"].shape[0] * given["x"].shape[1]
    D = given["x"].shape[2]
    x0 = given["x"].reshape(T, D)
    tgt = given["loss_target"].reshape(T, D)
    c = lax.axis_index("c")

    bf = {n: given[n].astype(BF) for n in SHARDED if n not in ("w_in", "conv_w")}
    bf["w_in"] = _pack_w_in(given["w_in"]).astype(BF)
    bf["conv_w"] = given["conv_w"]
    first_key = (0, GROUPS[0][0])
    first = [bf[n][0].reshape((2, bf[n].shape[1] // 2) + bf[n].shape[2:]) for n in GROUPS[0][1]]
    gathers = {first_key: _gather_halves_start(first, "l0_" + GROUPS[0][0])}
    later = {(i, gname): [bf[n][i] for n in names] for i in range(DEPTH) for gname, names in GROUPS
             if (i, gname) != first_key}
    zones = {key: _landing_zones(arrs) for key, arrs in later.items()}

    def gathered(i, gname, after):
        if (i, gname) != first_key:
            return gathers[(i, gname)].wait(after)[3:]
        got = gathers[first_key].wait([after] + [z for zs in zones.values() for z in zs])[3:]
        got = _gather_halves_finish(got, "l0_" + gname).wait(after)
        prev = got[0]
        for key, arrs in later.items():
            gathers[key] = _gather_start(arrs, zones[key], f"l{key[0]}_{key[1]}", after=[prev])
            prev = gathers[key].token_array
        return [z.reshape((4, 2 * z.shape[2]) + z.shape[3:]) for z in got]

    def mix_params(i, got):
        win = got[0].reshape(D, W_QKV + W_SSD + W_UV)
        rep = lambda v: jnp.repeat(v, HEAD)[None]
        ssd = (got[1].transpose(1, 0, 2).reshape(4, SSD_CONV_DIM), given["conv_b"][i][None],
               rep(given["dt_bias"][i]), rep(given["a_log"][i]), rep(given["d_skip"][i]), given["ssd_norm"][i][None])
        sgu = (given["sgu_ln_g"][i][None], given["sgu_ln_b"][i][None], given["sgu_w"][i],
               jnp.repeat(given["sgu_b"][i].T, HEAD, axis=1))
        return dict(mix_norm=given["mix_norm"][i][None], w_in=win, w_out=got[2].reshape(-1, D), ssd=ssd, sgu=sgu)

    x = x0
    tape = []
    for i in range(DEPTH):
        got = gathered(i, "ffn1", x)
        token = functools.reduce(lambda a, b: a + b, [g.token for g in gathers.values()]) if i == 0 else 0.0
        P = dict(ffn1=(given["ffn1_norm"][i][None] + token, *got))
        x, s1 = _ffn_fwd(x, *P["ffn1"])
        P.update(mix_params(i, gathered(i, "mix", x)))
        x, s2 = _mix_fwd(x, P)
        P["ffn2"] = (given["ffn2_norm"][i][None], *gathered(i, "ffn2", x))
        x, s3 = _ffn_fwd(x, *P["ffn2"])
        tape.append((P, s1, s2, s3))
    loss_part, dx, dgf = _final_loss(x, given["final_norm"][None], tgt)

    me = 2 * lax.axis_index("x") + lax.axis_index("y")
    jobs = []

    def rs_begin(i, gname, gd):
        tag = f"l{i}_{gname}"
        names = [n for n in dict(GROUPS)[gname] if n != "conv_w"]
        jobs.append(dict(key=(i, gname), names=names, tag=tag, stage=1,
                         op=_to_sibling_start([_halved(gd[n]) for n in names], tag)))

    def rs_advance(job, after):
        k = len(job["names"])
        if job["stage"] == 1:
            got = job["op"].wait(after)
            job.update(stage=2, op=_to_chips_start([_pair_add(g, l, c) for g, l in zip(got[:k], got[k:])], job["tag"]))
        elif job["stage"] == 2:
            got = job["op"].wait(after)
            job.update(stage=3, op=_join_start([_chip_sum(p, l, me, c) for p, l in zip(got[:k], got[k:])], job["tag"]))
        elif job["stage"] == 3:
            job.update(stage=4, out=dict(zip(job["names"], job["op"].wait(after))))

    def tick(after, begin=None):
        for job in jobs:
            rs_advance(job, after)
        if begin is not None:
            rs_begin(*begin)
        return functools.reduce(lambda a, b: a + b, [j["op"].token for j in jobs if j["stage"] < 4], 0.0)

    grads = [dict() for _ in range(DEPTH)]
    tok = 0.0
    for i in reversed(range(DEPTH)):
        P, s1, s2, s3 = tape[i]
        g = grads[i]
        norm, wg, wu, wd = P["ffn2"]
        dx, dn2, g["ffn2_w_gate"], g["ffn2_w_up"], g["ffn2_w_down"] = _ffn_bwd(dx, s3, norm + tok, wg, wu, wd)
        tok = tick(dx, (i, "ffn2", g))
        dx, gm = _mix_bwd(dx, s2, {**P, "mix_norm": P["mix_norm"] + tok})
        g.update(gm)
        tok = tick(dx, (i, "mix", g))
        norm, wg, wu, wd = P["ffn1"]
        dx, dn1, g["ffn1_w_gate"], g["ffn1_w_up"], g["ffn1_w_down"] = _ffn_bwd(dx, s1, norm + tok, wg, wu, wd)
        tok = tick(dx, (i, "ffn1", g))
        g["ffn1_norm"], g["ffn2_norm"] = dn1[0], dn2[0]
    grad_x = dx.reshape(given["x"].shape)

    order = [n for n in SMALL if n != "final_norm"] + ["final_norm"]
    small = [jnp.stack([grads[i][n] for i in range(DEPTH)]) for n in order[:-1] + ["conv_w"]]
    small = small[:-1] + [dgf[0], small[-1], loss_part[0, 0:1]]
    n_small = sum(s.size for s in small)
    rows_small = -(-n_small // (128 * 8)) * 8

    def flat(arrs):
        fill = rows_small * 128 - sum(a.size for a in arrs)
        return jnp.concatenate([a.reshape(-1) for a in arrs] + [jnp.zeros((fill,), F32)]).reshape(rows_small, 128)

    gsmall = _all_reduce_small(flat(small)).reshape(-1)

    grad_w = {}
    off = 0
    for n in order:
        size = given[n].size
        grad_w[n] = gsmall[off:off + size].reshape(given[n].shape)
        off += size
    cw = gsmall[off:off + 2 * 4 * SSD_CONV_DIM].reshape(DEPTH, 4, SSD_CONV_DIM)
    grad_w["conv_w"] = lax.dynamic_slice_in_dim(cw, me * (SSD_CONV_DIM // 4), SSD_CONV_DIM // 4, axis=2)
    loss = gsmall[off + 2 * 4 * SSD_CONV_DIM]

    delta, new_m, new_v = {}, {}, {}
    shp = given["conv_w"].shape
    d, m2, v2 = _adamw(*[a.reshape(shp[0] * shp[1], shp[2])
                         for a in (given["conv_w"], grad_w["conv_w"], given["m_conv_w"], given["v_conv_w"])])
    delta["conv_w"], new_m["conv_w"], new_v["conv_w"] = d.reshape(shp), m2.reshape(shp), v2.reshape(shp)
    packed = [flat([given[pre + n] for n in order]) for pre in ("", "m_", "v_")]
    small_out = _adamw(packed[0], gsmall.reshape(rows_small, 128), packed[1], packed[2])
    outs = [o.reshape(-1) for o in small_out]
    off = 0
    for n in order:
        size = given[n].size
        for dst, o in zip((delta, new_m, new_v), outs):
            dst[n] = o[off:off + size].reshape(given[n].shape)
        off += size

    stepped, arrived = {}, {}

    def update_arrived(dep):
        out = None
        for job in jobs:
            if job["stage"] == 4 and not job.get("seen"):
                job["seen"] = True
                for n, full in job["out"].items():
                    view = (lambda a: jnp.swapaxes(a, 1, 2)) if n in TRANSPOSED else (lambda a: a)
                    arrived.setdefault(n, {})[job["key"][0]] = full.reshape(view(given[n]).shape[1:])
                    if len(arrived[n]) == DEPTH:
                        res = _adamw_pair(view(given[n]), arrived[n][0], arrived[n][1], view(given["m_" + n]),
                                          view(given["v_" + n]), dep)
                        stepped[n] = [view(r) for r in res]
                        out = res[0]
        return out

    after = small_out[0]
    while any(j["stage"] < 4 for j in jobs):
        done = update_arrived(jnp.zeros((8, 128), F32) + tok)
        after = after if done is None else done
        tok = tick(after)
    update_arrived(jnp.zeros((8, 128), F32) + tok)
    for n, (d, m2, v2, g) in stepped.items():
        delta[n], new_m[n], new_v[n], grad_w[n] = d, m2, v2, g

    return (loss, grad_x, *[grad_w[n] for n in WEIGHTS], *[delta[n] for n in WEIGHTS],
            *[new_m[n] for n in WEIGHTS], *[new_v[n] for n in WEIGHTS])
```

```python
import functools
import math

import jax
import jax.numpy as jnp
from jax import lax
from jax.experimental import pallas as pl
from jax.experimental.pallas import tpu as pltpu

F32 = jnp.float32
BF = jnp.bfloat16

RMS_EPS = 1e-6
LN_EPS = 1e-5
SEQ = 2048
CHUNK = 128
N_CHUNK = SEQ // CHUNK
ATT_W = 384
HEAD = 64
SSD_W = 384
SSD_CONV_DIM = 896
SSD_STATE = 128
SGU_W = 256
DILATIONS = (1, 4, 16)
W_QKV = 3 * ATT_W
W_SSD = SSD_CONV_DIM + SSD_W + SSD_W
W_UV = 2 * SGU_W
ADAM_LR = 0.001
ADAM_B1 = 0.9
ADAM_B2 = 0.999
ADAM_EPS = 1e-08
ADAM_WD = 0.01
ADAM_STEP = 10
NEG = -1e30
ATTN_BWD_VMEM = 48 * 2 ** 20
FFN_VMEM = 60 * 2 ** 20


def _dot(a, b):
    return jnp.dot(a, b, preferred_element_type=F32)


def _dot_nt(a, b):
    return lax.dot_general(a, b, (((1,), (1,)), ((), ())), preferred_element_type=F32)


def _dot_tn(a, b):
    return lax.dot_general(a, b, (((0,), (0,)), ((), ())), preferred_element_type=F32)


def _sigmoid(x):
    return 1.0 / (1.0 + jnp.exp(-x))


def _call(body, *, name, grid, in_specs, out_specs, out_shape, scratch=(), sem=None, vmem=None):
    return pl.pallas_call(
        body, name=name, grid=grid, in_specs=in_specs, out_specs=out_specs, out_shape=out_shape,
        scratch_shapes=list(scratch),
        compiler_params=pltpu.CompilerParams(dimension_semantics=sem, vmem_limit_bytes=vmem),
    )


def _tile(n, want):
    t = min(n, want)
    while n % t:
        t //= 2
    return t


def _final_loss(x, g, tgt):
    T, D = x.shape
    tm = _tile(T, 512)

    def body(x_ref, g_ref, t_ref, l_ref, dx_ref, dg_ref):
        @pl.when(pl.program_id(0) == 0)
        def _():
            dg_ref[...] = jnp.zeros_like(dg_ref)
            l_ref[...] = jnp.zeros_like(l_ref)

        xf = x_ref[...]
        gg = g_ref[...]
        r = lax.rsqrt(jnp.mean(xf * xf, axis=-1, keepdims=True) + RMS_EPS)
        xn = xf * r
        e = xn * gg - t_ref[...]
        part = 0.5 * jnp.sum(jnp.mean(e * e, axis=-1, keepdims=True), axis=0, keepdims=True)
        l_ref[...] += jnp.broadcast_to(part, l_ref.shape)
        dy = e * (1.0 / D)
        u = dy * gg
        mu = jnp.mean(u * xf, axis=-1, keepdims=True)
        dx_ref[...] = r * (u - xf * (r * r * mu))
        dg_ref[...] += jnp.sum(dy * xn, axis=0, keepdims=True)

    row = pl.BlockSpec((tm, D), lambda i: (i, 0))
    vec = pl.BlockSpec((1, D), lambda i: (0, 0))
    lsp = pl.BlockSpec((1, 128), lambda i: (0, 0))
    return _call(body, name="final_loss", grid=(T // tm,), in_specs=[row, vec, row], out_specs=[lsp, row, vec],
                 out_shape=[jax.ShapeDtypeStruct((1, 128), F32), jax.ShapeDtypeStruct((T, D), F32),
                            jax.ShapeDtypeStruct((1, D), F32)],
                 sem=("arbitrary",))(x, g, tgt)


def _slabs(tm, n=2):
    return [slice(k * tm // n, (k + 1) * tm // n) for k in range(n)] if tm % (16 * n) == 0 else [slice(0, tm)]


def _resident(shape):
    return pl.BlockSpec(shape, lambda *_: (0,) * len(shape), pipeline_mode=pl.Buffered(1))


def _ffn_fwd_k(x, gn, wg, wu, wd):
    T, D = x.shape
    NS, _, Fs = wg.shape
    tm = _tile(T, 1024)

    def body(x_ref, gn_ref, wg_ref, wu_ref, wd_ref, o_ref, h_ref, s1_ref, s2_ref, a_ref, hs, acc):
        j = pl.program_id(1)

        @pl.when(j == 0)
        def _():
            xf = x_ref[...]
            r = lax.rsqrt(jnp.mean(xf * xf, axis=-1, keepdims=True) + RMS_EPS)
            hs[...] = (xf * r * gn_ref[...]).astype(BF)
            h_ref[...] = hs[...]
            acc[...] = jnp.zeros_like(acc)

        h = hs[...]
        g = _dot(h, wg_ref[...])
        u = _dot(h, wu_ref[...])
        sg = _sigmoid(g)
        s1 = g * sg
        a = (s1 * u).astype(BF)
        s1_ref[...] = s1.astype(BF)
        s2_ref[...] = (u * (sg * (1.0 + g * (1.0 - sg)))).astype(BF)
        a_ref[...] = a
        acc[...] += _dot(a, wd_ref[...])

        @pl.when(j == NS - 1)
        def _():
            o_ref[...] = x_ref[...] + 0.5 * acc[...]

    row = pl.BlockSpec((tm, D), lambda i, j: (i, 0))
    act = pl.BlockSpec((None, tm, Fs), lambda i, j: (j, i, 0))
    sh = jax.ShapeDtypeStruct((NS, T, Fs), BF)
    wspec = lambda w: pl.BlockSpec((None,) + w.shape[1:], lambda i, j: (j, 0, 0))
    return _call(body, name="ffn_fwd", grid=(T // tm, NS),
                 in_specs=[row, pl.BlockSpec((1, D), lambda i, j: (0, 0)), wspec(wg), wspec(wu), wspec(wd)],
                 out_specs=[row, row, act, act, act],
                 out_shape=[jax.ShapeDtypeStruct((T, D), F32), jax.ShapeDtypeStruct((T, D), BF), sh, sh, sh],
                 scratch=[pltpu.VMEM((tm, D), BF), pltpu.VMEM((tm, D), F32)],
                 sem=("parallel", "arbitrary"), vmem=FFN_VMEM)(x, gn, wg, wu, wd)


def _ffn_bwd_k1(dxo, x, gn, s1, s2, wg, wu, wd):
    NS, T, Fs = s1.shape
    D = x.shape[1]
    tm = _tile(T, 512)

    def body(dxo_ref, x_ref, gn_ref, s1_ref, s2_ref, wg_ref, wu_ref, wd_ref,
             dx_ref, dgn_ref, dg_ref, du_ref, dy_ref, dys, acc):
        i, j = pl.program_id(0), pl.program_id(1)

        @pl.when((i == 0) & (j == 0))
        def _():
            dgn_ref[...] = jnp.zeros_like(dgn_ref)

        @pl.when(j == 0)
        def _():
            dys[...] = (0.5 * dxo_ref[...]).astype(BF)
            dy_ref[...] = dys[...]
            acc[...] = jnp.zeros_like(acc)

        for rows in _slabs(tm):
            da = _dot_nt(dys[rows, :], wd_ref[j])
            dg = (da * s2_ref[rows, :].astype(F32)).astype(BF)
            du = (da * s1_ref[rows, :].astype(F32)).astype(BF)
            dg_ref[rows, :] = dg
            du_ref[rows, :] = du
            acc[rows, :] += _dot_nt(dg, wg_ref[j]) + _dot_nt(du, wu_ref[j])

        @pl.when(j == NS - 1)
        def _():
            xf = x_ref[...]
            r = lax.rsqrt(jnp.mean(xf * xf, axis=-1, keepdims=True) + RMS_EPS)
            dh = acc[...]
            uu = dh * gn_ref[...]
            mu = jnp.mean(uu * xf, axis=-1, keepdims=True)
            dx_ref[...] = dxo_ref[...] + r * (uu - xf * (r * r * mu))
            dgn_ref[...] += jnp.sum(dh * xf * r, axis=0, keepdims=True)

    row = pl.BlockSpec((tm, D), lambda i, j: (i, 0))
    vec = pl.BlockSpec((1, D), lambda i, j: (0, 0))
    act = pl.BlockSpec((None, tm, Fs), lambda i, j: (j, i, 0))
    sh = jax.ShapeDtypeStruct((NS, T, Fs), BF)
    return _call(body, name="ffn_bwd_x", grid=(T // tm, NS),
                 in_specs=[row, row, vec, act, act, _resident(wg.shape), _resident(wu.shape), _resident(wd.shape)],
                 out_specs=[row, vec, act, act, row],
                 out_shape=[jax.ShapeDtypeStruct((T, D), F32), jax.ShapeDtypeStruct((1, D), F32), sh, sh,
                            jax.ShapeDtypeStruct((T, D), BF)],
                 scratch=[pltpu.VMEM((tm, D), BF), pltpu.VMEM((tm, D), F32)],
                 sem=("arbitrary", "arbitrary"))(dxo, x, gn, s1, s2, wg, wu, wd)


def _ffn_bwd_k2(hb, dyb, a, dg, du):
    NS, T, Fs = a.shape
    D = hb.shape[1]
    tk = _tile(T, 1024)

    def body(h_ref, dy_ref, a_ref, dg_ref, du_ref, og_ref, ou_ref, od_ref, bg_ref, bu_ref, bd_ref):
        @pl.when(pl.program_id(1) == 0)
        def _():
            og_ref[...] = jnp.zeros_like(og_ref)
            ou_ref[...] = jnp.zeros_like(ou_ref)
            od_ref[...] = jnp.zeros_like(od_ref)

        h = h_ref[...]
        og_ref[...] += _dot_tn(dg_ref[...], h)
        ou_ref[...] += _dot_tn(du_ref[...], h)
        od_ref[...] += _dot_tn(a_ref[...], dy_ref[...])

        @pl.when(pl.program_id(1) == pl.num_programs(1) - 1)
        def _():
            bg_ref[...] = og_ref[...].astype(BF)
            bu_ref[...] = ou_ref[...].astype(BF)
            bd_ref[...] = od_ref[...].astype(BF)

    row = pl.BlockSpec((tk, D), lambda j, k: (k, 0))
    act = pl.BlockSpec((None, tk, Fs), lambda j, k: (j, k, 0))
    return _call(body, name="ffn_bwd_w", grid=(NS, T // tk), in_specs=[row, row, act, act, act],
                 out_specs=[pl.BlockSpec((None, Fs, D), lambda j, k: (j, 0, 0))] * 6,
                 out_shape=[jax.ShapeDtypeStruct((NS, Fs, D), F32)] * 3 + [jax.ShapeDtypeStruct((NS, Fs, D), BF)] * 3,
                 sem=("parallel", "arbitrary"))(hb, dyb, a, dg, du)


def _mm_nn(a, b, res=None, out_dtype=F32):
    T, K = a.shape
    N = b.shape[1]
    tm = _tile(T, 512)
    tn = N if N <= 2048 else _tile(N, 1024)

    def body(*refs):
        if res is None:
            a_ref, b_ref, o_ref = refs
            o_ref[...] = _dot(a_ref[...], b_ref[...]).astype(out_dtype)
        else:
            a_ref, b_ref, r_ref, o_ref = refs
            o_ref[...] = (r_ref[...] + _dot(a_ref[...], b_ref[...])).astype(out_dtype)

    o = pl.BlockSpec((tm, tn), lambda i, j: (i, j))
    ins = [pl.BlockSpec((tm, K), lambda i, j: (i, 0)), pl.BlockSpec((K, tn), lambda i, j: (0, j))]
    args = [a, b]
    if res is not None:
        ins.append(o)
        args.append(res)
    return _call(body, name="mm_nn", grid=(T // tm, N // tn), in_specs=ins, out_specs=o,
                 out_shape=jax.ShapeDtypeStruct((T, N), out_dtype), sem=("parallel", "parallel"))(*args)


def _mm_nt(a, b, res=None):
    T, K = a.shape
    N = b.shape[0]
    tm = _tile(T, 512)

    def body(*refs):
        if res is None:
            a_ref, b_ref, o_ref = refs
            o_ref[...] = _dot_nt(a_ref[...].astype(BF), b_ref[...])
        else:
            a_ref, b_ref, r_ref, o_ref = refs
            o_ref[...] = r_ref[...] + _dot_nt(a_ref[...].astype(BF), b_ref[...])

    o = pl.BlockSpec((tm, N), lambda i: (i, 0))
    ins = [pl.BlockSpec((tm, K), lambda i: (i, 0)), pl.BlockSpec((N, K), lambda i: (0, 0))]
    args = [a, b]
    if res is not None:
        ins.append(o)
        args.append(res)
    return _call(body, name="mm_nt", grid=(T // tm,), in_specs=ins, out_specs=o,
                 out_shape=jax.ShapeDtypeStruct((T, N), F32), sem=("parallel",))(*args)


def _mm_tn(a, b):
    T, M = a.shape
    N = b.shape[1]
    tk = _tile(T, 1024)
    tmm = _tile(M, 512)

    def body(a_ref, b_ref, o_ref):
        @pl.when(pl.program_id(1) == 0)
        def _():
            o_ref[...] = jnp.zeros_like(o_ref)

        o_ref[...] += _dot_tn(a_ref[...].astype(BF), b_ref[...].astype(BF))

    return _call(body, name="mm_tn", grid=(M // tmm, T // tk),
                 in_specs=[pl.BlockSpec((tk, tmm), lambda i, k: (k, i)), pl.BlockSpec((tk, N), lambda i, k: (k, 0))],
                 out_specs=pl.BlockSpec((tmm, N), lambda i, k: (i, 0)),
                 out_shape=jax.ShapeDtypeStruct((M, N), F32), sem=("parallel", "arbitrary"))(a, b)


def _mix_proj(x, gn, win):
    T, D = x.shape
    tm = _tile(T, 512)
    cuts = (0, W_QKV, W_QKV + W_SSD, W_QKV + W_SSD + W_UV)

    def body(x_ref, gn_ref, w_ref, h_ref, q_ref, s_ref, u_ref):
        xf = x_ref[...]
        r = lax.rsqrt(jnp.mean(xf * xf, axis=-1, keepdims=True) + RMS_EPS)
        h = (xf * r * gn_ref[...]).astype(BF)
        h_ref[...] = h
        for o_ref, lo, hi in zip((q_ref, s_ref, u_ref), cuts[:-1], cuts[1:]):
            o_ref[...] = _dot(h, w_ref[:, lo:hi])

    row = lambda w: pl.BlockSpec((tm, w), lambda i: (i, 0))
    return _call(body, name="mix_proj", grid=(T // tm,),
                 in_specs=[row(D), pl.BlockSpec((1, D), lambda i: (0, 0)), _resident(win.shape)],
                 out_specs=[row(D), row(W_QKV), row(W_SSD), row(W_UV)],
                 out_shape=[jax.ShapeDtypeStruct((T, D), BF), jax.ShapeDtypeStruct((T, W_QKV), F32),
                            jax.ShapeDtypeStruct((T, W_SSD), F32), jax.ShapeDtypeStruct((T, W_UV), F32)],
                 sem=("parallel",))(x, gn, win)


def _mix_bwd_dx(dqkv, dsin, duv, win, x, gn, dxo):
    T, D = x.shape
    tm = _tile(T, 512)
    cuts = (0, W_QKV, W_QKV + W_SSD, W_QKV + W_SSD + W_UV)

    def body(dq_ref, ds_ref, du_ref, w_ref, x_ref, gn_ref, dxo_ref, dx_ref, dgn_ref):
        @pl.when(pl.program_id(0) == 0)
        def _():
            dgn_ref[...] = jnp.zeros_like(dgn_ref)

        dh = (_dot_nt(dq_ref[...], w_ref[:, cuts[0]:cuts[1]]) + _dot_nt(ds_ref[...], w_ref[:, cuts[1]:cuts[2]])
              + _dot_nt(du_ref[...], w_ref[:, cuts[2]:cuts[3]]))
        xf = x_ref[...]
        r = lax.rsqrt(jnp.mean(xf * xf, axis=-1, keepdims=True) + RMS_EPS)
        uu = dh * gn_ref[...]
        mu = jnp.mean(uu * xf, axis=-1, keepdims=True)
        dx_ref[...] = dxo_ref[...] + r * (uu - xf * (r * r * mu))
        dgn_ref[...] += jnp.sum(dh * xf * r, axis=0, keepdims=True)

    row = lambda w: pl.BlockSpec((tm, w), lambda i: (i, 0))
    vec = pl.BlockSpec((1, D), lambda i: (0, 0))
    return _call(body, name="mix_bwd_dx", grid=(T // tm,),
                 in_specs=[row(W_QKV), row(W_SSD), row(W_UV), _resident(win.shape), row(D), vec, row(D)],
                 out_specs=[row(D), vec],
                 out_shape=[jax.ShapeDtypeStruct((T, D), F32), jax.ShapeDtypeStruct((1, D), F32)],
                 sem=("arbitrary",))(dqkv, dsin, duv, win, x, gn, dxo)


def _lane_mask(e, width=128):
    return (lax.broadcasted_iota(jnp.int32, (1, width), 1) // HEAD) == e


def _band_mask(n):
    qi = lax.broadcasted_iota(jnp.int32, (CHUNK, 2 * CHUNK), 0)
    kj = lax.broadcasted_iota(jnp.int32, (CHUNK, 2 * CHUNK), 1)
    dist = qi + CHUNK - kj
    return (dist >= 0) & (dist <= CHUNK) & ((kj >= CHUNK) | (n > 0))


def _sub_rows(r, block, dil):
    if dil == 1:
        return pl.ds(pl.multiple_of(block * CHUNK, CHUNK), CHUNK)
    return pl.ds(r + dil * CHUNK * block, CHUNK, stride=dil)


def _attn_specs(T, dil):
    B, nb = T // SEQ, SEQ // (CHUNK * dil)
    once = dict(pipeline_mode=pl.Buffered(1))
    q_like = lambda col: pl.BlockSpec((CHUNK * dil, 128), lambda b, n, r: (b * nb + n, col), **(once if nb == 1 else {}))
    k_like = lambda col: pl.BlockSpec((SEQ, 128), lambda b, n, r: (b, col), **once)
    return B, nb, q_like, k_like


def _attn_fwd(qkv, dil):
    T = qkv.shape[0]
    B, nb, q_like, k_like = _attn_specs(T, dil)
    scale = HEAD ** -0.5

    def body(*refs):
        q_t, k_t, v_t, o_t, l_t = refs[0:3], refs[3:6], refs[6:9], refs[9:12], refs[12:15]
        n, r = pl.program_id(1), pl.program_id(2)
        mine = _sub_rows(r, 0, dil)
        cur, prv = _sub_rows(r, n, dil), _sub_rows(r, jnp.maximum(n - 1, 0), dil)
        mask = _band_mask(n)
        for t in range(3):
            qt = q_t[t][mine, :].astype(BF)
            kt = jnp.concatenate([k_t[t][prv, :], k_t[t][cur, :]], axis=0).astype(BF)
            vt = jnp.concatenate([v_t[t][prv, :], v_t[t][cur, :]], axis=0).astype(BF)
            o_pair = jnp.zeros((CHUNK, 128), F32)
            l_pair = jnp.zeros((CHUNK, 128), F32)
            for e in range(2):
                lm = _lane_mask(e)
                s = _dot_nt(jnp.where(lm, qt, jnp.zeros_like(qt)), kt) * scale
                s = jnp.where(mask, s, NEG)
                m = jnp.max(s, axis=-1, keepdims=True)
                p = jnp.exp(s - m)
                den = jnp.sum(p, axis=-1, keepdims=True)
                o = _dot(p.astype(BF), vt) / den
                o_pair = jnp.where(lm, o, o_pair)
                l_pair = jnp.where(lm, m + jnp.log(den), l_pair)
            o_t[t][mine, :] = o_pair
            l_t[t][mine, :] = l_pair

    out_spec = pl.BlockSpec((CHUNK * dil, 128), lambda b, n, r: (b * nb + n, 0))
    sh = jax.ShapeDtypeStruct((T, 128), F32)
    outs = _call(
        body, name=f"attn_fwd_d{dil}", grid=(B, nb, dil),
        in_specs=[q_like(t) for t in range(3)] + [k_like(3 + t) for t in range(3)] + [k_like(6 + t) for t in range(3)],
        out_specs=[out_spec] * 6, out_shape=[sh] * 6, sem=("parallel", "arbitrary", "arbitrary"))(*([qkv] * 9))
    return list(outs[0:3]), list(outs[3:6])


def _attn_combine(branches):
    T = branches[0][0][0].shape[0]
    tm = _tile(T, 512)

    def body(*refs):
        y_ref, l_ref = refs[-2:]
        for t in range(3):
            o = [refs[6 * i + t][...] for i in range(3)]
            a, b, c = [refs[6 * i + 3 + t][...] for i in range(3)]
            m = jnp.maximum(jnp.maximum(a, b), c)
            ea, eb, ec = jnp.exp(a - m), jnp.exp(b - m), jnp.exp(c - m)
            z = ea + eb + ec
            y_ref[:, 128 * t:128 * (t + 1)] = (ea * o[0] + eb * o[1] + ec * o[2]) / z
            l_ref[:, 128 * t:128 * (t + 1)] = m + jnp.log(z)

    tile = pl.BlockSpec((tm, 128), lambda i: (i, 0))
    row = pl.BlockSpec((tm, ATT_W), lambda i: (i, 0))
    sh = jax.ShapeDtypeStruct((T, ATT_W), F32)
    flat = [a for o_t, l_t in branches for a in (*o_t, *l_t)]
    return _call(body, name="attn_combine", grid=(T // tm,), in_specs=[tile] * 18, out_specs=[row, row],
                 out_shape=[sh, sh], sem=("parallel",))(*flat)


def _attn_bwd(qkv, do, out, lse, dil):
    T = qkv.shape[0]
    B, nb, q_like, k_like = _attn_specs(T, dil)
    scale = HEAD ** -0.5

    def body(*refs):
        q_t, k_t, v_t = refs[0:3], refs[3:6], refs[6:9]
        do_t, out_t, lse_t = refs[9:12], refs[12:15], refs[15:18]
        dq_t, dk_t, dv_t = refs[18:21], refs[21:24], refs[24:27]
        n, r = pl.program_id(1), pl.program_id(2)

        @pl.when((n == 0) & (r == 0))
        def _():
            for t in range(3):
                dk_t[t][...] = jnp.zeros_like(dk_t[t])
                dv_t[t][...] = jnp.zeros_like(dv_t[t])

        mine = _sub_rows(r, 0, dil)
        cur, prv = _sub_rows(r, n, dil), _sub_rows(r, jnp.maximum(n - 1, 0), dil)
        mask = _band_mask(n)
        for t in range(3):
            qt = q_t[t][mine, :].astype(BF)
            kt = jnp.concatenate([k_t[t][prv, :], k_t[t][cur, :]], axis=0).astype(BF)
            vt = jnp.concatenate([v_t[t][prv, :], v_t[t][cur, :]], axis=0).astype(BF)
            do_ = do_t[t][mine, :]
            dlt = do_ * out_t[t][mine, :]
            ls = lse_t[t][mine, :]
            dq_pair = jnp.zeros((CHUNK, 128), F32)
            dk_acc = jnp.zeros((2 * CHUNK, 128), F32)
            dv_acc = jnp.zeros((2 * CHUNK, 128), F32)
            for e in range(2):
                lm = _lane_mask(e)
                qm = jnp.where(lm, qt, jnp.zeros_like(qt))
                s = _dot_nt(qm, kt) * scale
                p = jnp.exp(jnp.where(mask, s - ls[:, HEAD * e:HEAD * e + 1], NEG))
                dom = jnp.where(lm, do_, 0.0).astype(BF)
                dv_acc += _dot_tn(p.astype(BF), dom)
                dp = _dot_nt(dom, vt)
                delta = jnp.sum(jnp.where(lm, dlt, 0.0), axis=-1, keepdims=True)
                ds = (p * (dp - delta) * scale).astype(BF)
                dq_pair += jnp.where(lm, _dot(ds, kt), 0.0)
                dk_acc += _dot_tn(ds, qm)
            dq_t[t][mine, :] = dq_pair
            dk_t[t][cur, :] = dk_t[t][cur, :] + dk_acc[CHUNK:]
            dk_t[t][prv, :] = dk_t[t][prv, :] + dk_acc[:CHUNK]
            dv_t[t][cur, :] = dv_t[t][cur, :] + dv_acc[CHUNK:]
            dv_t[t][prv, :] = dv_t[t][prv, :] + dv_acc[:CHUNK]

    q_out = pl.BlockSpec((CHUNK * dil, 128), lambda b, n, r: (b * nb + n, 0))
    k_out = pl.BlockSpec((SEQ, 128), lambda b, n, r: (b, 0))
    sh = jax.ShapeDtypeStruct((T, 128), F32)
    tiles = lambda: [q_like(t) for t in range(3)]
    return list(_call(
        body, name=f"attn_bwd_d{dil}", grid=(B, nb, dil),
        in_specs=tiles() + [k_like(3 + t) for t in range(3)] + [k_like(6 + t) for t in range(3)]
        + tiles() + tiles() + tiles(),
        out_specs=[q_out] * 3 + [k_out] * 6, out_shape=[sh] * 9,
        sem=("parallel", "arbitrary", "arbitrary"), vmem=ATTN_BWD_VMEM)(*([qkv] * 9 + [do] * 3 + [out] * 3 + [lse] * 3)))


def _sum_branches(parts):
    T = parts[0][0].shape[0]
    tm = _tile(T, 512)

    def body(*refs):
        o_ref = refs[-1]
        for c in range(9):
            acc = refs[c][...] + refs[9 + c][...] + refs[18 + c][...]
            o_ref[:, 128 * c:128 * (c + 1)] = acc.astype(BF)

    tile = pl.BlockSpec((tm, 128), lambda i: (i, 0))
    flat = [a for br in parts for a in br]
    return _call(body, name="attn_sum_branches", grid=(T // tm,), in_specs=[tile] * 27,
                 out_specs=pl.BlockSpec((tm, W_QKV), lambda i: (i, 0)),
                 out_shape=jax.ShapeDtypeStruct((T, W_QKV), BF), sem=("parallel",))(*flat)


def _silu(x):
    return x * _sigmoid(x)


def _dsilu(x):
    s = _sigmoid(x)
    return s * (1.0 + x * (1.0 - s))


def _log1p(u):
    return jnp.where(u < 0.01, u * (1.0 - u * (0.5 - u * (1.0 / 3.0))), jnp.log(1.0 + u))


def _softplus(x):
    return jnp.maximum(x, 0.0) + _log1p(jnp.exp(-jnp.abs(x)))


def _cumsum_rows(x, reverse=False):
    n = x.shape[0]
    rows = lax.broadcasted_iota(jnp.int32, x.shape, 0)
    k = 1
    while k < n:
        if reverse:
            x = x + jnp.where(rows < n - k, pltpu.roll(x, n - k, 0), 0.0)
        else:
            x = x + jnp.where(rows >= k, pltpu.roll(x, k, 0), 0.0)
        k *= 2
    return x


def _tri():
    r = lax.broadcasted_iota(jnp.int32, (CHUNK, CHUNK), 0)
    c = lax.broadcasted_iota(jnp.int32, (CHUNK, CHUNK), 1)
    return r >= c


def _row_mask(e):
    return (lax.broadcasted_iota(jnp.int32, (128, 1), 0) // HEAD) == e


def _first_lane(e):
    return lax.broadcasted_iota(jnp.int32, (1, 128), 1) == HEAD * e


def _ssd_pre(x_ref, halo_ref, first, cw_ref, cb_ref, dtb_ref, al_ref, ext):
    row = x_ref[...]
    z = row[:, SSD_CONV_DIM:SSD_CONV_DIM + SSD_W]
    u = row[:, SSD_CONV_DIM + SSD_W:] + dtb_ref[...]
    ext[0:8, :] = jnp.where(first, 0.0, halo_ref[:, 0:SSD_CONV_DIM])
    ext[8:8 + CHUNK, :] = row[:, 0:SSD_CONV_DIM]
    xc = cb_ref[...]
    for j in range(4):
        xc = xc + cw_ref[j:j + 1, :] * ext[pl.ds(5 + j, CHUNK), :]
    xa = _silu(xc)
    dt = _softplus(u)
    a = dt * (-jnp.exp(al_ref[...]))
    A = _cumsum_rows(a)
    return dict(z=z, u=u, xc=xc, xs=xa[:, 0:SSD_W], Bm=xa[:, SSD_W:SSD_W + 256], Cm=xa[:, SSD_W + 256:],
                dt=dt, a=a, A=A, AT=A.T, eA=jnp.exp(A), wdec=jnp.exp(A[CHUNK - 1:CHUNK, :] - A),
                dtot=jnp.exp(A[CHUNK - 1:CHUNK, :]))


def _ssd_y(p, hp_ref, dskip):
    tri = _tri()
    X = p["xs"] * p["dt"]
    Bb = [p["Bm"][:, 128 * g:128 * (g + 1)].astype(BF) for g in range(2)]
    Cb = [p["Cm"][:, 128 * g:128 * (g + 1)].astype(BF) for g in range(2)]
    CB = [_dot_nt(Cb[g], Bb[g]) for g in range(2)]
    tiles = []
    for t in range(3):
        sl = slice(128 * t, 128 * (t + 1))
        hpb = hp_ref[sl, :].astype(BF)
        acc = jnp.zeros((CHUNK, 128), F32)
        for e in range(2):
            h = 2 * t + e
            g, col = h // 3, HEAD * h
            lm = _lane_mask(e)
            L = jnp.exp(jnp.where(tri, p["A"][:, col:col + 1] - p["AT"][col:col + 1, :], NEG))
            yd = _dot((CB[g] * L).astype(BF), jnp.where(lm, X[:, sl], 0.0).astype(BF))
            yo = _dot_nt(Cb[g], hpb) * p["eA"][:, sl]
            acc = acc + jnp.where(lm, yd + yo, 0.0)
        tiles.append(acc)
    return jnp.concatenate(tiles, axis=1) + dskip * p["xs"], X, Bb, Cb, CB


def _group_stats(v):
    g0 = lax.broadcasted_iota(jnp.int32, (1, SSD_W), 1) < SSD_W // 2
    m0 = jnp.sum(jnp.where(g0, v, 0.0), axis=-1, keepdims=True) * (2.0 / SSD_W)
    m1 = jnp.sum(jnp.where(g0, 0.0, v), axis=-1, keepdims=True) * (2.0 / SSD_W)
    return jnp.where(g0, m0, m1)


def _ssd_specs(T, rev):
    B = T // SEQ

    def chunk(b, c):
        return b * N_CHUNK + (N_CHUNK - 1 - c if rev else c)

    row = pl.BlockSpec((CHUNK, W_SSD), lambda b, c: (chunk(b, c), 0))
    halo = pl.BlockSpec((8, W_SSD), lambda b, c: (jnp.maximum(chunk(b, c) * (CHUNK // 8) - 1, 0), 0))
    hp = pl.BlockSpec((None, SSD_W, SSD_STATE), lambda b, c: (chunk(b, c), 0, 0))
    y = pl.BlockSpec((CHUNK, SSD_W), lambda b, c: (chunk(b, c), 0))
    const = lambda r, w: pl.BlockSpec((r, w), lambda b, c: (0, 0))
    params = [const(4, SSD_CONV_DIM), const(1, SSD_CONV_DIM)] + [const(1, SSD_W)] * 4
    return B, row, halo, hp, y, const, params


def _ssd_fwd(sin, conv_w, conv_b, dtb, alog, dskip, norm_g):
    T = sin.shape[0]
    B, row, halo, hp, y, const, params = _ssd_specs(T, False)

    def body(x_ref, halo_ref, cw_ref, cb_ref, dtb_ref, al_ref, dk_ref, ng_ref, y_ref, hp_ref, ext, hst):
        c = pl.program_id(1)

        @pl.when(c == 0)
        def _():
            hst[...] = jnp.zeros_like(hst)

        p = _ssd_pre(x_ref, halo_ref, c == 0, cw_ref, cb_ref, dtb_ref, al_ref, ext)
        yv, X, Bb, Cb, CB = _ssd_y(p, hst, dk_ref[...])
        hp_ref[...] = hst[...]
        for t in range(3):
            sl = slice(128 * t, 128 * (t + 1))
            old = hst[sl, :]
            new = old
            for e in range(2):
                h = 2 * t + e
                g, col = h // 3, HEAD * h
                st = _dot_tn(jnp.where(_lane_mask(e), X[:, sl] * p["wdec"][:, sl], 0.0).astype(BF), Bb[g])
                new = jnp.where(_row_mask(e), old * p["dtot"][:, col:col + 1] + st, new)
            hst[sl, :] = new
        y2 = yv * _silu(p["z"])
        r = lax.rsqrt(_group_stats(y2 * y2) + RMS_EPS)
        y_ref[...] = y2 * r * ng_ref[...]

    return _call(body, name="ssd_fwd", grid=(B, N_CHUNK), in_specs=[row, halo] + params, out_specs=[y, hp],
                 out_shape=[jax.ShapeDtypeStruct((T, SSD_W), F32),
                            jax.ShapeDtypeStruct((T // CHUNK, SSD_W, SSD_STATE), F32)],
                 scratch=[pltpu.VMEM((8 + CHUNK, SSD_CONV_DIM), F32), pltpu.VMEM((SSD_W, SSD_STATE), F32)],
                 sem=("parallel", "arbitrary"))(sin, sin, conv_w, conv_b, dtb, alog, dskip, norm_g)


def _ssd_bwd(sin, hprev, dy3, conv_w, conv_b, dtb, alog, dskip, norm_g):
    T = sin.shape[0]
    B, row, halo, hp, y, const, params = _ssd_specs(T, True)

    def body(x_ref, halo_ref, hp_ref, dy_ref, cw_ref, cb_ref, dtb_ref, al_ref, dk_ref, ng_ref,
             dx_ref, dcw_ref, dcb_ref, dvec_ref, ext, ext2, dh):
        c = pl.program_id(1)

        @pl.when((pl.program_id(0) == 0) & (c == 0))
        def _():
            dcw_ref[...] = jnp.zeros_like(dcw_ref)
            dcb_ref[...] = jnp.zeros_like(dcb_ref)
            dvec_ref[...] = jnp.zeros_like(dvec_ref)

        @pl.when(c == 0)
        def _():
            dh[...] = jnp.zeros_like(dh)
            ext2[CHUNK:CHUNK + 8, :] = jnp.zeros((8, SSD_CONV_DIM), F32)

        p = _ssd_pre(x_ref, halo_ref, c == N_CHUNK - 1, cw_ref, cb_ref, dtb_ref, al_ref, ext)
        dskip_ = dk_ref[...]
        yv, X, Bb, Cb, CB = _ssd_y(p, hp_ref, dskip_)
        xs, z, A, AT = p["xs"], p["z"], p["A"], p["AT"]

        sz = _silu(z)
        y2 = yv * sz
        r = lax.rsqrt(_group_stats(y2 * y2) + RMS_EPS)
        dy3_ = dy_ref[...]
        uu = dy3_ * ng_ref[...]
        dy2 = r * (uu - y2 * (r * r * _group_stats(uu * y2)))
        dy = dy2 * sz
        dz = dy2 * yv * _dsilu(z)

        tri = _tri()
        rows = lax.broadcasted_iota(jnp.int32, (CHUNK, 1), 0)
        dG = [jnp.zeros((CHUNK, CHUNK), F32) for _ in range(2)]
        dB = [jnp.zeros((CHUNK, SSD_STATE), F32) for _ in range(2)]
        dC = [jnp.zeros((CHUNK, SSD_STATE), F32) for _ in range(2)]
        dX_t, dA_t, ddtx_t = [], [], []
        for t in range(3):
            sl = slice(128 * t, 128 * (t + 1))
            hp_t = hp_ref[sl, :]
            hpb = hp_t.astype(BF)
            dhc = dh[sl, :]
            dh_new = jnp.zeros((128, SSD_STATE), F32)
            dX = jnp.zeros((CHUNK, 128), F32)
            dA = jnp.zeros((CHUNK, 128), F32)
            ddtx = jnp.zeros((CHUNK, 128), F32)
            for e in range(2):
                h = 2 * t + e
                g, col = h // 3, HEAD * h
                lm, rm, fl = _lane_mask(e), _row_mask(e), _first_lane(e)
                L = jnp.exp(jnp.where(tri, A[:, col:col + 1] - AT[col:col + 1, :], NEG))
                Mf = CB[g] * L
                Xm = jnp.where(lm, X[:, sl], 0.0)
                Xmb = Xm.astype(BF)
                dyh = jnp.where(lm, dy[:, sl], 0.0)
                dyb = dyh.astype(BF)
                dXh = _dot_tn(Mf.astype(BF), dyb)
                dM = jnp.where(tri, _dot_nt(dyb, Xmb), 0.0)
                Wm = dM * Mf
                dAc = jnp.sum(Wm, axis=-1, keepdims=True) - jnp.sum(Wm.T, axis=-1, keepdims=True)
                dG[g] = dG[g] + dM * L
                eAt = p["eA"][:, sl]
                yo = _dot_nt(Cb[g], hpb)
                dyo = (dyh * eAt).astype(BF)
                dC[g] = dC[g] + _dot(dyo, hpb)
                dh_new = dh_new + _dot_tn(dyo, Cb[g])
                dAc = dAc + jnp.sum(dyh * yo * eAt, axis=-1, keepdims=True)
                dHn = jnp.where(rm, dhc, 0.0)
                dHnb = dHn.astype(BF)
                dec = p["dtot"][:, col:col + 1]
                dh_new = dh_new + dec * dHn
                Z = _dot_nt(Bb[g], dHnb)
                wt = p["wdec"][:, sl]
                xi = jnp.sum(Xm * Z, axis=-1, keepdims=True) * p["wdec"][:, col:col + 1]
                dXh = dXh + wt * Z
                dB[g] = dB[g] + _dot(jnp.where(lm, X[:, sl] * wt, 0.0).astype(BF), dHnb)
                dAtot = jnp.sum(xi, axis=0, keepdims=True) + dec * jnp.sum(
                    jnp.sum(dHn * hp_t, axis=-1, keepdims=True), axis=0, keepdims=True)
                dAc = dAc - xi + jnp.where(rows == CHUNK - 1, dAtot, 0.0)
                dA = dA + jnp.where(fl, dAc, 0.0)
                dX = dX + dXh
                ddtx = ddtx + jnp.where(fl, jnp.sum(dXh * xs[:, sl], axis=-1, keepdims=True), 0.0)
            dh[sl, :] = dh_new
            dX_t.append(dX)
            dA_t.append(dA)
            ddtx_t.append(ddtx)
        for g in range(2):
            dGb = dG[g].astype(BF)
            dC[g] = dC[g] + _dot(dGb, Bb[g])
            dB[g] = dB[g] + _dot_tn(dGb, Cb[g])
        dXf = jnp.concatenate(dX_t, axis=1)
        da = _cumsum_rows(jnp.concatenate(dA_t, axis=1), reverse=True)
        ddt = da * (-jnp.exp(al_ref[...])) + jnp.concatenate(ddtx_t, axis=1)
        du = ddt * _sigmoid(p["u"])
        dxs = dXf * p["dt"] + dskip_ * dy
        dxc = jnp.concatenate([dxs, dB[0], dB[1], dC[0], dC[1]], axis=1) * _dsilu(p["xc"])
        ext2[0:CHUNK, :] = dxc
        dxbc = jnp.zeros((CHUNK, SSD_CONV_DIM), F32)
        for j in range(4):
            dxbc = dxbc + cw_ref[j:j + 1, :] * ext2[pl.ds(3 - j, CHUNK), :]
            dcw_ref[j:j + 1, :] += jnp.sum(dxc * ext[pl.ds(5 + j, CHUNK), :], axis=0, keepdims=True)
        ext2[CHUNK:CHUNK + 8, :] = dxc[0:8, :]
        dcb_ref[...] += jnp.sum(dxc, axis=0, keepdims=True)
        dvec_ref[0:1, :] += jnp.sum(du, axis=0, keepdims=True)
        dvec_ref[1:2, :] += jnp.sum(da * p["a"], axis=0, keepdims=True)
        dvec_ref[2:3, :] += jnp.sum(dy * xs, axis=0, keepdims=True)
        dvec_ref[3:4, :] += jnp.sum(dy3_ * y2 * r, axis=0, keepdims=True)
        dx_ref[...] = jnp.concatenate([dxbc, dz, du], axis=1).astype(BF)

    return _call(body, name="ssd_bwd", grid=(B, N_CHUNK), in_specs=[row, halo, hp, y] + params,
                 out_specs=[row, const(4, SSD_CONV_DIM), const(1, SSD_CONV_DIM), const(8, SSD_W)],
                 out_shape=[jax.ShapeDtypeStruct((T, W_SSD), BF), jax.ShapeDtypeStruct((4, SSD_CONV_DIM), F32),
                            jax.ShapeDtypeStruct((1, SSD_CONV_DIM), F32), jax.ShapeDtypeStruct((8, SSD_W), F32)],
                 scratch=[pltpu.VMEM((8 + CHUNK, SSD_CONV_DIM), F32), pltpu.VMEM((8 + CHUNK, SSD_CONV_DIM), F32),
                          pltpu.VMEM((SSD_W, SSD_STATE), F32)],
                 sem=("arbitrary", "arbitrary"))(sin, sin, hprev, dy3, conv_w, conv_b, dtb, alog, dskip, norm_g)


def _sgu_core(uv_ref, g_ref, b_ref, w_ref, bias_ref):
    x = uv_ref[...]
    cdf = 0.5 * (1.0 + lax.erf(x * (2.0 ** -0.5)))
    ge = x * cdf
    dge = cdf + x * jnp.exp(-0.5 * x * x) * ((2.0 * math.pi) ** -0.5)
    u, v = ge[:, 0:SGU_W], ge[:, SGU_W:]
    vc = v - jnp.mean(v, axis=-1, keepdims=True)
    rstd = lax.rsqrt(jnp.mean(vc * vc, axis=-1, keepdims=True) + LN_EPS)
    vhat = vc * rstd
    vn = vhat * g_ref[...] + b_ref[...]
    tri = _tri()
    wc = [jnp.where(tri, w_ref[gi], 0.0).astype(BF) for gi in range(4)]
    vm = [jnp.where(_lane_mask(gi % 2), vn[:, 128 * (gi // 2):128 * (gi // 2 + 1)], 0.0).astype(BF) for gi in range(4)]
    mixed = jnp.concatenate([_dot(wc[2 * t], vm[2 * t]) + _dot(wc[2 * t + 1], vm[2 * t + 1]) for t in range(2)],
                            axis=1) + bias_ref[...]
    return dict(dge=dge, u=u, rstd=rstd, vhat=vhat, wc=wc, vm=vm, mixed=mixed)


def _sgu_specs():
    vec = pl.BlockSpec((1, SGU_W), lambda i: (0, 0))
    return [pl.BlockSpec((CHUNK, W_UV), lambda i: (i, 0)), vec, vec,
            pl.BlockSpec((4, CHUNK, CHUNK), lambda i: (0, 0, 0)), pl.BlockSpec((CHUNK, SGU_W), lambda i: (0, 0))]


def _sgu_fwd(uv, ln_g, ln_b, w, bias):
    T = uv.shape[0]

    def body(uv_ref, g_ref, b_ref, w_ref, bias_ref, y_ref):
        s = _sgu_core(uv_ref, g_ref, b_ref, w_ref, bias_ref)
        y_ref[...] = s["u"] * s["mixed"]

    return _call(body, name="sgu_fwd", grid=(T // CHUNK,), in_specs=_sgu_specs(),
                 out_specs=pl.BlockSpec((CHUNK, SGU_W), lambda i: (i, 0)),
                 out_shape=jax.ShapeDtypeStruct((T, SGU_W), F32), sem=("parallel",))(uv, ln_g, ln_b, w, bias)


def _sgu_bwd(uv, dy, ln_g, ln_b, w, bias):
    T = uv.shape[0]

    def body(uv_ref, dy_ref, g_ref, b_ref, w_ref, bias_ref, dx_ref, dw_ref, dbias_ref, dln_ref):
        @pl.when(pl.program_id(0) == 0)
        def _():
            dw_ref[...] = jnp.zeros_like(dw_ref)
            dbias_ref[...] = jnp.zeros_like(dbias_ref)
            dln_ref[...] = jnp.zeros_like(dln_ref)

        s = _sgu_core(uv_ref, g_ref, b_ref, w_ref, bias_ref)
        dy_ = dy_ref[...]
        du = dy_ * s["mixed"]
        dmix = dy_ * s["u"]
        dbias_ref[...] += dmix
        tri = _tri()
        dvn_t = []
        for t in range(2):
            acc = jnp.zeros((CHUNK, 128), F32)
            for e in range(2):
                gi = 2 * t + e
                dmg = jnp.where(_lane_mask(e), dmix[:, 128 * t:128 * (t + 1)], 0.0).astype(BF)
                acc = acc + _dot_tn(s["wc"][gi], dmg)
                dw_ref[gi] += jnp.where(tri, _dot_nt(dmg, s["vm"][gi]), 0.0)
            dvn_t.append(acc)
        dvn = jnp.concatenate(dvn_t, axis=1)
        dln_ref[0:1, :] += jnp.sum(dvn * s["vhat"], axis=0, keepdims=True)
        dln_ref[1:2, :] += jnp.sum(dvn, axis=0, keepdims=True)
        dvh = dvn * g_ref[...]
        dv = s["rstd"] * (dvh - jnp.mean(dvh, axis=-1, keepdims=True)
                          - s["vhat"] * jnp.mean(dvh * s["vhat"], axis=-1, keepdims=True))
        dx_ref[...] = (jnp.concatenate([du, dv], axis=1) * s["dge"]).astype(BF)

    ins = _sgu_specs()
    return _call(body, name="sgu_bwd", grid=(T // CHUNK,),
                 in_specs=[ins[0], pl.BlockSpec((CHUNK, SGU_W), lambda i: (i, 0))] + ins[1:],
                 out_specs=[pl.BlockSpec((CHUNK, W_UV), lambda i: (i, 0)),
                            pl.BlockSpec((4, CHUNK, CHUNK), lambda i: (0, 0, 0)),
                            pl.BlockSpec((CHUNK, SGU_W), lambda i: (0, 0)), pl.BlockSpec((8, SGU_W), lambda i: (0, 0))],
                 out_shape=[jax.ShapeDtypeStruct((T, W_UV), BF), jax.ShapeDtypeStruct((4, CHUNK, CHUNK), F32),
                            jax.ShapeDtypeStruct((CHUNK, SGU_W), F32), jax.ShapeDtypeStruct((8, SGU_W), F32)],
                 sem=("arbitrary",))(uv, dy, ln_g, ln_b, w, bias)


def _adamw(w, g, m, v):
    R, C = w.shape
    tr = R

    def body(w_ref, g_ref, m_ref, v_ref, d_ref, nm_ref, nv_ref):
        g_ = g_ref[...]
        m2 = ADAM_B1 * m_ref[...] + (1.0 - ADAM_B1) * g_
        v2 = ADAM_B2 * v_ref[...] + (1.0 - ADAM_B2) * (g_ * g_)
        m_hat = m2 / (1.0 - ADAM_B1 ** ADAM_STEP)
        v_hat = v2 / (1.0 - ADAM_B2 ** ADAM_STEP)
        d_ref[...] = -ADAM_LR * (m_hat / (jnp.sqrt(v_hat) + ADAM_EPS) + ADAM_WD * w_ref[...])
        nm_ref[...] = m2
        nv_ref[...] = v2

    blk = pl.BlockSpec((tr, C), lambda i: (i, 0))
    sh = jax.ShapeDtypeStruct((R, C), F32)
    return _call(body, name="adamw", grid=(R // tr,), in_specs=[blk] * 4, out_specs=[blk] * 3,
                 out_shape=[sh] * 3, sem=("parallel",))(w, g, m, v)


def _adamw_pair(w, g0, g1, m, v, dep):
    L, R, C = w.shape
    tr = _tile(R, 256 if C <= 1024 else 64)

    def body(w_ref, g0_ref, g1_ref, m_ref, v_ref, dep_ref, d_ref, nm_ref, nv_ref, og_ref):
        g_ = jnp.where(pl.program_id(0) == 0, g0_ref[...], g1_ref[...])
        m2 = ADAM_B1 * m_ref[...] + (1.0 - ADAM_B1) * g_
        v2 = ADAM_B2 * v_ref[...] + (1.0 - ADAM_B2) * (g_ * g_)
        m_hat = m2 / (1.0 - ADAM_B1 ** ADAM_STEP)
        v_hat = v2 / (1.0 - ADAM_B2 ** ADAM_STEP)
        d_ref[...] = -ADAM_LR * (m_hat / (jnp.sqrt(v_hat) + ADAM_EPS) + ADAM_WD * w_ref[...])
        nm_ref[...] = m2
        nv_ref[...] = v2
        og_ref[...] = g_

    lay = pl.BlockSpec((None, tr, C), lambda l, i: (l, i, 0))
    one = lambda k: pl.BlockSpec((tr, C), lambda l, i: (jnp.where(l == k, i, 0), 0))
    return _call(body, name="adamw_pair", grid=(L, R // tr),
                 in_specs=[lay, one(0), one(1), lay, lay, pl.BlockSpec((8, 128), lambda l, i: (0, 0))],
                 out_specs=[lay] * 4,
                 out_shape=[jax.ShapeDtypeStruct((L, R, C), F32)] * 4,
                 sem=("parallel", "parallel"))(w, g0, g1, m, v, dep)


def _row_steps(rows):
    return 2 if rows % 32 == 0 else 1


def _dev_sum(g, land, me, c):
    _, _, R, C = g.shape
    n = _row_steps(R)
    tr = R // n

    def body(s_ref, own_ref, l_ref, o_ref):
        acc = None
        for i in range(4):
            for k in range(2):
                mine = (s_ref[0] == i) & (s_ref[1] == k)
                term = jnp.where(mine, own_ref[...], l_ref[i, k].astype(F32))
                acc = term if acc is None else acc + term
        o_ref[...] = acc

    return pl.pallas_call(
        body, name="rs_dev_sum",
        grid_spec=pltpu.PrefetchScalarGridSpec(
            num_scalar_prefetch=1, grid=(n,),
            in_specs=[pl.BlockSpec((None, None, tr, C), lambda i, s: (s[0], s[1], i, 0)),
                      pl.BlockSpec((4, 2, tr, C), lambda i, s: (0, 0, i, 0))],
            out_specs=pl.BlockSpec((None, tr, C), lambda i, s: (s[1], i, 0))),
        out_shape=jax.ShapeDtypeStruct((2, R, C), F32),
        compiler_params=pltpu.CompilerParams(dimension_semantics=("parallel",)),
    )(jnp.stack([me, c]).astype(jnp.int32), g, land)


MESH = pl.DeviceIdType.MESH
ANY = pl.BlockSpec(memory_space=pl.ANY)


def _place():
    x, y, c = lax.axis_index("x"), lax.axis_index("y"), lax.axis_index("c")
    return x, y, c, [(1 - x, y), (x, 1 - y), (1 - x, 1 - y)]


HBM = pl.BlockSpec(memory_space=pltpu.HBM)
SEM = pl.BlockSpec(memory_space=pltpu.SEMAPHORE)
EFFECT = pltpu.SideEffectType.DATAFLOW_SIDE_EFFECTING


class _Split:
    def __init__(self, tag, arrays, copies, n_copies, after=()):
        self.tag, self.copies, k = tag, copies, len(arrays)

        def body(*refs):
            sems = k + len(after)
            for cp in copies(refs[:k], refs[sems], refs[sems + 1]):
                cp.start()
            refs[-1][...] = jnp.zeros_like(refs[-1])

        out = pl.pallas_call(
            body, name=tag + "_start",
            out_shape=(pltpu.SemaphoreType.DMA((n_copies,)), pltpu.SemaphoreType.DMA((n_copies,)),
                       *[pltpu.HBM(a.shape, a.dtype) for a in arrays], jax.ShapeDtypeStruct((8, 128), F32)),
            in_specs=[HBM] * k + [ANY] * len(after),
            out_specs=(SEM, SEM, *[HBM] * k, pl.BlockSpec(memory_space=pltpu.VMEM)),
            input_output_aliases={i: 2 + i for i in range(k)},
            compiler_params=pltpu.CompilerParams(has_side_effects=EFFECT),
        )(*[pltpu.with_memory_space_constraint(a, pltpu.HBM) for a in arrays], *after)
        self.send, self.recv, self.arrays, self.token_array = out[0], out[1], list(out[2:2 + k]), out[-1]
        self.token = self.token_array[0, 0]

    def wait(self, after):
        k, copies = len(self.arrays), self.copies
        after = list(after) if isinstance(after, (list, tuple)) else [after]

        def body(*refs):
            for cp in copies(refs[:k], refs[k], refs[k + 1]):
                cp.wait_send()
                cp.wait_recv()

        return list(pl.pallas_call(
            body, name=self.tag + "_wait", out_shape=tuple(pltpu.HBM(a.shape, a.dtype) for a in self.arrays),
            in_specs=[HBM] * k + [SEM, SEM] + [ANY] * len(after), out_specs=tuple([HBM] * k),
            input_output_aliases={i: i for i in range(k)},
            compiler_params=pltpu.CompilerParams(has_side_effects=EFFECT),
        )(*self.arrays, self.send, self.recv, *after))


def _landing_zones(arrs):
    me = 2 * lax.axis_index("x") + lax.axis_index("y")
    return [lax.dynamic_update_index_in_dim(lax.empty((4,) + a.shape, a.dtype), a, me, 0) for a in arrs]


def _gather_start(arrs, lands, tag, after=()):
    n = len(arrs)

    def copies(refs, send, recv):
        x, y, c, chips = _place()
        return [pltpu.make_async_remote_copy(
            src_ref=refs[k], dst_ref=refs[n + k].at[2 * x + y], send_sem=send.at[3 * k + r],
            recv_sem=recv.at[3 * k + r], device_id=(px, py, c), device_id_type=MESH)
            for k in range(n) for r, (px, py) in enumerate(chips)]

    return _Split("gather_" + tag, list(arrs) + lands, copies, 3 * n, after)


def _gather_halves_start(arrs, tag):
    n = len(arrs)
    lands = _landing_zones(arrs)

    def copies(refs, send, recv):
        x, y, c, chips = _place()
        return [pltpu.make_async_remote_copy(
            src_ref=refs[k].at[c], dst_ref=refs[n + k].at[2 * x + y, c], send_sem=send.at[3 * k + r],
            recv_sem=recv.at[3 * k + r], device_id=(px, py, c), device_id_type=MESH)
            for k in range(n) for r, (px, py) in enumerate(chips)]

    return _Split("gather_" + tag, list(arrs) + lands, copies, 3 * n)


def _gather_halves_finish(lands, tag):
    n = len(lands)

    def copies(refs, send, recv):
        x, y, c, chips = _place()
        return [pltpu.make_async_remote_copy(
            src_ref=refs[k].at[2 * px + py, c], dst_ref=refs[k].at[2 * px + py, c], send_sem=send.at[3 * k + r],
            recv_sem=recv.at[3 * k + r], device_id=(x, y, 1 - c), device_id_type=MESH)
            for k in range(n) for r, (px, py) in enumerate(chips)]

    return _Split("gather_pass_" + tag, list(lands), copies, 3 * n)


RELATIONS = [(fx, fy, fc) for fx in (0, 1) for fy in (0, 1) for fc in (0, 1)][1:]


def _scatter_start(gbs, tag):
    n = len(gbs)

    def copies(refs, send, recv):
        x, y, c, _ = _place()
        out = []
        for k in range(n):
            for r, (fx, fy, fc) in enumerate(RELATIONS):
                px, py, pc = (1 - x if fx else x), (1 - y if fy else y), (1 - c if fc else c)
                out.append(pltpu.make_async_remote_copy(
                    src_ref=refs[k].at[2 * px + py, pc], dst_ref=refs[n + k].at[2 * x + y, c],
                    send_sem=send.at[7 * k + r], recv_sem=recv.at[7 * k + r],
                    device_id=(px, py, pc), device_id_type=MESH))
        return out

    return _Split("rs_scatter_" + tag, list(gbs) + [lax.empty(g.shape, g.dtype) for g in gbs], copies, 7 * n)


def _join_start(fulls, tag):
    def copies(refs, send, recv):
        x, y, c, _ = _place()
        return [pltpu.make_async_remote_copy(
            src_ref=refs[k].at[c], dst_ref=refs[k].at[c], send_sem=send.at[k], recv_sem=recv.at[k],
            device_id=(x, y, 1 - c), device_id_type=MESH) for k in range(len(fulls))]

    return _Split("rs_join_" + tag, list(fulls), copies, len(fulls))


def _all_reduce_small(v):
    R, C = v.shape

    def body(v_ref, o_ref, g_ref, send, recv, loc):
        x, y, c, chips = _place()
        me, sibling = (x, y, c), (x, y, 1 - c)

        def rows(px, py, pc):
            return g_ref.at[4 * px + 2 * py + pc]

        def copy(k, block, to, src=None):
            return pltpu.make_async_remote_copy(
                src_ref=rows(*block) if src is None else src, dst_ref=rows(*block),
                send_sem=send.at[k], recv_sem=recv.at[k], device_id=to, device_id_type=MESH)

        mine = pltpu.make_async_copy(v_ref, rows(*me), loc)
        mine.start()
        first = [copy(0, me, sibling, src=v_ref)]
        first += [copy(1 + j, me, (*chip, c), src=v_ref) for j, chip in enumerate(chips)]
        for cp in first:
            cp.start()
        passed = [copy(4 + j, (*chip, c), sibling) for j, chip in enumerate(chips)]
        for j, chip in enumerate(chips):
            copy(1 + j, (*chip, c), me).wait_recv()
            passed[j].start()
        copy(0, sibling, me).wait_recv()
        for j, chip in enumerate(chips):
            copy(4 + j, (*chip, 1 - c), me).wait_recv()
        for cp in first + passed:
            cp.wait_send()
        mine.wait()
        acc = g_ref[0]
        for d in range(1, 8):
            acc = acc + g_ref[d]
        o_ref[...] = acc

    vm = pl.BlockSpec(memory_space=pltpu.VMEM)
    return pl.pallas_call(
        body, name="all_reduce_small", in_specs=[vm], out_specs=[vm, vm],
        out_shape=[jax.ShapeDtypeStruct((R, C), F32), jax.ShapeDtypeStruct((8, R, C), F32)],
        scratch_shapes=[pltpu.SemaphoreType.DMA((7,)), pltpu.SemaphoreType.DMA((7,)), pltpu.SemaphoreType.DMA],
    )(v)[0]


WEIGHTS = ['ffn1_norm', 'ffn1_w_gate', 'ffn1_w_up', 'ffn1_w_down', 'mix_norm', 'w_in', 'conv_w', 'conv_b', 'dt_bias',
           'a_log', 'd_skip', 'ssd_norm', 'sgu_ln_g', 'sgu_ln_b', 'sgu_w', 'sgu_b', 'w_out', 'ffn2_norm',
           'ffn2_w_gate', 'ffn2_w_up', 'ffn2_w_down', 'final_norm']
SHARDED = ['ffn1_w_gate', 'ffn1_w_up', 'ffn1_w_down', 'w_in', 'conv_w', 'w_out', 'ffn2_w_gate', 'ffn2_w_up',
           'ffn2_w_down']
SMALL = [n for n in WEIGHTS if n not in SHARDED]
GROUPS = [("ffn1", ["ffn1_w_gate", "ffn1_w_up", "ffn1_w_down"]), ("mix", ["w_in", "conv_w", "w_out"]),
          ("ffn2", ["ffn2_w_gate", "ffn2_w_up", "ffn2_w_down"])]
TRANSPOSED = ("ffn1_w_gate", "ffn1_w_up", "ffn2_w_gate", "ffn2_w_up")
DEPTH = 2


def _pack_w_in(w):
    return jnp.concatenate([w[..., 0:1152], w[..., 1536:2432], w[..., 1152:1536],
                            jnp.repeat(w[..., 2432:2438], HEAD, axis=-1), w[..., 2438:2950]], axis=-1)


def _unpack_w_in(dq, ds, du):
    return jnp.concatenate([dq, ds[:, 896:1280], ds[:, 0:896], ds[:, 1280::HEAD], du], axis=-1)


def _ffn_fwd(x, g, wg, wu, wd):
    xo, hb, S1, S2, A = _ffn_fwd_k(x, g, wg, wu, wd)
    return xo, (x, hb, S1, S2, A)


def _ffn_bwd(dxo, saved, g, wg, wu, wd):
    x, hb, S1, S2, A = saved
    dx, dg, dG, dU, dyb = _ffn_bwd_k1(dxo, x, g, S1, S2, wg, wu, wd)
    dwg, dwu, dwd, bwg, bwu, bwd = _ffn_bwd_k2(hb, dyb, A, dG, dU)
    return dx, dg, (dwg, bwg), (dwu, bwu), (dwd, bwd)


def _mix_fwd(x, P):
    hb, qkv, sin, uv = _mix_proj(x, P["mix_norm"], P["w_in"])
    y_att, lse = _attn_combine([_attn_fwd(qkv, d) for d in DILATIONS])
    y_ssd, hprev = _ssd_fwd(sin, *P["ssd"])
    y_sgu = _sgu_fwd(uv, *P["sgu"])
    ycat = jnp.concatenate([y_att, y_ssd, y_sgu], axis=1).astype(BF)
    return _mm_nn(ycat, P["w_out"], res=x), (x, hb, qkv, sin, uv, y_att, lse, hprev, ycat)


def _mix_bwd(dxo, saved, P):
    x, hb, qkv, sin, uv, y_att, lse, hprev, ycat = saved
    dycat = _mm_nt(dxo, P["w_out"])
    dwout = _mm_tn(ycat, dxo)
    dy_att, dy_ssd, dy_sgu = dycat[:, 0:ATT_W], dycat[:, ATT_W:ATT_W + SSD_W], dycat[:, ATT_W + SSD_W:]
    dqkv = _sum_branches([_attn_bwd(qkv, dy_att, y_att, lse, d) for d in DILATIONS])
    dsin, dcw, dcb, dvec = _ssd_bwd(sin, hprev, dy_ssd, *P["ssd"])
    duv, dsw, dsbias, dln = _sgu_bwd(uv, dy_sgu, *P["sgu"])
    dwin = _unpack_w_in(_mm_tn(hb, dqkv), _mm_tn(hb, dsin), _mm_tn(hb, duv))
    dx, dg = _mix_bwd_dx(dqkv, dsin, duv, P["w_in"], x, P["mix_norm"], dxo)
    grads = dict(
        mix_norm=dg[0], w_in=dwin, conv_w=dcw, conv_b=dcb[0], dt_bias=dvec[0, ::HEAD], a_log=dvec[1, ::HEAD],
        d_skip=jnp.sum(dvec[2].reshape(6, HEAD), axis=-1), ssd_norm=dvec[3], sgu_ln_g=dln[0], sgu_ln_b=dln[1],
        sgu_w=dsw, sgu_b=jnp.sum(dsbias.reshape(CHUNK, 4, HEAD), axis=-1).T, w_out=dwout)
    return dx, grads


def _halved(g):
    rows = g.size // g.shape[-1]
    return g.reshape(4, 2, rows // 8, g.shape[-1])


def kernel(x, ffn1_norm, ffn1_w_gate, ffn1_w_up, ffn1_w_down, mix_norm, w_in, conv_w, conv_b, dt_bias, a_log, d_skip, ssd_norm, sgu_ln_g, sgu_ln_b, sgu_w, sgu_b, w_out, ffn2_norm, ffn2_w_gate, ffn2_w_up, ffn2_w_down, final_norm, loss_target, m_ffn1_norm, m_ffn1_w_gate, m_ffn1_w_up, m_ffn1_w_down, m_mix_norm, m_w_in, m_conv_w, m_conv_b, m_dt_bias, m_a_log, m_d_skip, m_ssd_norm, m_sgu_ln_g, m_sgu_ln_b, m_sgu_w, m_sgu_b, m_w_out, m_ffn2_norm, m_ffn2_w_gate, m_ffn2_w_up, m_ffn2_w_down, m_final_norm, v_ffn1_norm, v_ffn1_w_gate, v_ffn1_w_up, v_ffn1_w_down, v_mix_norm, v_w_in, v_conv_w, v_conv_b, v_dt_bias, v_a_log, v_d_skip, v_ssd_norm, v_sgu_ln_g, v_sgu_ln_b, v_sgu_w, v_sgu_b, v_w_out, v_ffn2_norm, v_ffn2_w_gate, v_ffn2_w_up, v_ffn2_w_down, v_final_norm):
    given = dict(x=x, ffn1_norm=ffn1_norm, ffn1_w_gate=ffn1_w_gate, ffn1_w_up=ffn1_w_up, ffn1_w_down=ffn1_w_down, mix_norm=mix_norm, w_in=w_in, conv_w=conv_w, conv_b=conv_b, dt_bias=dt_bias, a_log=a_log, d_skip=d_skip, ssd_norm=ssd_norm, sgu_ln_g=sgu_ln_g, sgu_ln_b=sgu_ln_b, sgu_w=sgu_w, sgu_b=sgu_b, w_out=w_out, ffn2_norm=ffn2_norm, ffn2_w_gate=ffn2_w_gate, ffn2_w_up=ffn2_w_up, ffn2_w_down=ffn2_w_down, final_norm=final_norm, loss_target=loss_target, m_ffn1_norm=m_ffn1_norm, m_ffn1_w_gate=m_ffn1_w_gate, m_ffn1_w_up=m_ffn1_w_up, m_ffn1_w_down=m_ffn1_w_down, m_mix_norm=m_mix_norm, m_w_in=m_w_in, m_conv_w=m_conv_w, m_conv_b=m_conv_b, m_dt_bias=m_dt_bias, m_a_log=m_a_log, m_d_skip=m_d_skip, m_ssd_norm=m_ssd_norm, m_sgu_ln_g=m_sgu_ln_g, m_sgu_ln_b=m_sgu_ln_b, m_sgu_w=m_sgu_w, m_sgu_b=m_sgu_b, m_w_out=m_w_out, m_ffn2_norm=m_ffn2_norm, m_ffn2_w_gate=m_ffn2_w_gate, m_ffn2_w_up=m_ffn2_w_up, m_ffn2_w_down=m_ffn2_w_down, m_final_norm=m_final_norm, v_ffn1_norm=v_ffn1_norm, v_ffn1_w_gate=v_ffn1_w_gate, v_ffn1_w_up=v_ffn1_w_up, v_ffn1_w_down=v_ffn1_w_down, v_mix_norm=v_mix_norm, v_w_in=v_w_in, v_conv_w=v_conv_w, v_conv_b=v_conv_b, v_dt_bias=v_dt_bias, v_a_log=v_a_log, v_d_skip=v_d_skip, v_ssd_norm=v_ssd_norm, v_sgu_ln_g=v_sgu_ln_g, v_sgu_ln_b=v_sgu_ln_b, v_sgu_w=v_sgu_w, v_sgu_b=v_sgu_b, v_w_out=v_w_out, v_ffn2_norm=v_ffn2_norm, v_ffn2_w_gate=v_ffn2_w_gate, v_ffn2_w_up=v_ffn2_w_up, v_ffn2_w_down=v_ffn2_w_down, v_final_norm=v_final_norm)
    T = given["x"].shape[0] * given["x"].shape[1]
    D = given["x"].shape[2]
    x0 = given["x"].reshape(T, D)
    tgt = given["loss_target"].reshape(T, D)
    c = lax.axis_index("c")

    bf = {n: given[n].astype(BF) for n in SHARDED if n not in ("w_in", "conv_w")}
    bf["w_in"] = _pack_w_in(given["w_in"]).astype(BF)
    bf["conv_w"] = given["conv_w"]
    first_key = (0, GROUPS[0][0])
    first = [bf[n][0].reshape((2, bf[n].shape[1] // 2) + bf[n].shape[2:]) for n in GROUPS[0][1]]
    gathers = {first_key: _gather_halves_start(first, "l0_" + GROUPS[0][0])}
    later = {(i, gname): [bf[n][i] for n in names] for i in range(DEPTH) for gname, names in GROUPS
             if (i, gname) != first_key}
    zones = {key: _landing_zones(arrs) for key, arrs in later.items()}

    def gathered(i, gname, after):
        if (i, gname) != first_key:
            return gathers[(i, gname)].wait(after)[3:]
        got = gathers[first_key].wait([after] + [z for zs in zones.values() for z in zs])[3:]
        got = _gather_halves_finish(got, "l0_" + gname).wait(after)
        prev = got[0]
        for key, arrs in later.items():
            gathers[key] = _gather_start(arrs, zones[key], f"l{key[0]}_{key[1]}", after=[prev])
            prev = gathers[key].token_array
        return [z.reshape((4, 2 * z.shape[2]) + z.shape[3:]) for z in got]

    def mix_params(i, got):
        win = got[0].reshape(D, W_QKV + W_SSD + W_UV)
        rep = lambda v: jnp.repeat(v, HEAD)[None]
        ssd = (got[1].transpose(1, 0, 2).reshape(4, SSD_CONV_DIM), given["conv_b"][i][None],
               rep(given["dt_bias"][i]), rep(given["a_log"][i]), rep(given["d_skip"][i]), given["ssd_norm"][i][None])
        sgu = (given["sgu_ln_g"][i][None], given["sgu_ln_b"][i][None], given["sgu_w"][i],
               jnp.repeat(given["sgu_b"][i].T, HEAD, axis=1))
        return dict(mix_norm=given["mix_norm"][i][None], w_in=win, w_out=got[2].reshape(-1, D), ssd=ssd, sgu=sgu)

    x = x0
    tape = []
    for i in range(DEPTH):
        got = gathered(i, "ffn1", x)
        token = functools.reduce(lambda a, b: a + b, [g.token for g in gathers.values()]) if i == 0 else 0.0
        P = dict(ffn1=(given["ffn1_norm"][i][None] + token, *got))
        x, s1 = _ffn_fwd(x, *P["ffn1"])
        P.update(mix_params(i, gathered(i, "mix", x)))
        x, s2 = _mix_fwd(x, P)
        P["ffn2"] = (given["ffn2_norm"][i][None], *gathered(i, "ffn2", x))
        x, s3 = _ffn_fwd(x, *P["ffn2"])
        tape.append((P, s1, s2, s3))
    loss_part, dx, dgf = _final_loss(x, given["final_norm"][None], tgt)

    me = 2 * lax.axis_index("x") + lax.axis_index("y")
    jobs = []

    def rs_begin(i, gname, gd):
        tag = f"l{i}_{gname}"
        names = [n for n in dict(GROUPS)[gname] if n != "conv_w"]
        pairs = [gd[n] if isinstance(gd[n], tuple) else (gd[n], gd[n].astype(BF)) for n in names]
        jobs.append(dict(key=(i, gname), names=names, tag=tag, stage=1, own=[_halved(f) for f, _ in pairs],
                         op=_scatter_start([_halved(b) for _, b in pairs], tag)))

    def rs_advance(job, after):
        k = len(job["names"])
        if job["stage"] == 1:
            landed = job["op"].wait(after)[k:]
            job.update(stage=2, op=_join_start([_dev_sum(g, l, me, c) for g, l in zip(job["own"], landed)], job["tag"]))
        elif job["stage"] == 2:
            job.update(stage=4, out=dict(zip(job["names"], job["op"].wait(after))))

    def tick(after, begin=None):
        for job in jobs:
            rs_advance(job, after)
        if begin is not None:
            rs_begin(*begin)
        return functools.reduce(lambda a, b: a + b, [j["op"].token for j in jobs if j["stage"] < 4], 0.0)

    grads = [dict() for _ in range(DEPTH)]
    tok = 0.0
    for i in reversed(range(DEPTH)):
        P, s1, s2, s3 = tape[i]
        g = grads[i]
        norm, wg, wu, wd = P["ffn2"]
        dx, dn2, g["ffn2_w_gate"], g["ffn2_w_up"], g["ffn2_w_down"] = _ffn_bwd(dx, s3, norm + tok, wg, wu, wd)
        tok = tick(dx, (i, "ffn2", g))
        dx, gm = _mix_bwd(dx, s2, {**P, "mix_norm": P["mix_norm"] + tok})
        g.update(gm)
        tok = tick(dx, (i, "mix", g))
        norm, wg, wu, wd = P["ffn1"]
        dx, dn1, g["ffn1_w_gate"], g["ffn1_w_up"], g["ffn1_w_down"] = _ffn_bwd(dx, s1, norm + tok, wg, wu, wd)
        tok = tick(dx, (i, "ffn1", g))
        g["ffn1_norm"], g["ffn2_norm"] = dn1[0], dn2[0]
    grad_x = dx.reshape(given["x"].shape)

    order = [n for n in SMALL if n != "final_norm"] + ["final_norm"]
    small = [jnp.stack([grads[i][n] for i in range(DEPTH)]) for n in order[:-1] + ["conv_w"]]
    small = small[:-1] + [dgf[0], small[-1], loss_part[0, 0:1]]
    n_small = sum(s.size for s in small)
    rows_small = -(-n_small // (128 * 8)) * 8

    def flat(arrs):
        fill = rows_small * 128 - sum(a.size for a in arrs)
        return jnp.concatenate([a.reshape(-1) for a in arrs] + [jnp.zeros((fill,), F32)]).reshape(rows_small, 128)

    gsmall = _all_reduce_small(flat(small)).reshape(-1)

    grad_w = {}
    off = 0
    for n in order:
        size = given[n].size
        grad_w[n] = gsmall[off:off + size].reshape(given[n].shape)
        off += size
    cw = gsmall[off:off + 2 * 4 * SSD_CONV_DIM].reshape(DEPTH, 4, SSD_CONV_DIM)
    grad_w["conv_w"] = lax.dynamic_slice_in_dim(cw, me * (SSD_CONV_DIM // 4), SSD_CONV_DIM // 4, axis=2)
    loss = gsmall[off + 2 * 4 * SSD_CONV_DIM]

    delta, new_m, new_v = {}, {}, {}
    shp = given["conv_w"].shape
    d, m2, v2 = _adamw(*[a.reshape(shp[0] * shp[1], shp[2])
                         for a in (given["conv_w"], grad_w["conv_w"], given["m_conv_w"], given["v_conv_w"])])
    delta["conv_w"], new_m["conv_w"], new_v["conv_w"] = d.reshape(shp), m2.reshape(shp), v2.reshape(shp)
    packed = [flat([given[pre + n] for n in order]) for pre in ("", "m_", "v_")]
    small_out = _adamw(packed[0], gsmall.reshape(rows_small, 128), packed[1], packed[2])
    outs = [o.reshape(-1) for o in small_out]
    off = 0
    for n in order:
        size = given[n].size
        for dst, o in zip((delta, new_m, new_v), outs):
            dst[n] = o[off:off + size].reshape(given[n].shape)
        off += size

    stepped, arrived = {}, {}

    def update_arrived(dep):
        out = None
        for job in jobs:
            if job["stage"] == 4 and not job.get("seen"):
                job["seen"] = True
                for n, full in job["out"].items():
                    view = (lambda a: jnp.swapaxes(a, 1, 2)) if n in TRANSPOSED else (lambda a: a)
                    arrived.setdefault(n, {})[job["key"][0]] = full.reshape(view(given[n]).shape[1:])
                    if len(arrived[n]) == DEPTH:
                        res = _adamw_pair(view(given[n]), arrived[n][0], arrived[n][1], view(given["m_" + n]),
                                          view(given["v_" + n]), dep)
                        stepped[n] = [view(r) for r in res]
                        out = res[0]
        return out

    after = small_out[0]
    while any(j["stage"] < 4 for j in jobs):
        done = update_arrived(jnp.zeros((8, 128), F32) + tok)
        after = after if done is None else done
        tok = tick(after)
    update_arrived(jnp.zeros((8, 128), F32) + tok)
    for n, (d, m2, v2, g) in stepped.items():
        delta[n], new_m[n], new_v[n], grad_w[n] = d, m2, v2, g

    return (loss, grad_x, *[grad_w[n] for n in WEIGHTS], *[delta[n] for n in WEIGHTS],
            *[new_m[n] for n in WEIGHTS], *[new_v[n] for n in WEIGHTS])
```

```python
import functools
import math

import jax
import jax.numpy as jnp
from jax import lax
from jax.experimental import pallas as pl
from jax.experimental.pallas import tpu as pltpu

F32 = jnp.float32
BF = jnp.bfloat16

RMS_EPS = 1e-6
LN_EPS = 1e-5
SEQ = 2048
CHUNK = 128
N_CHUNK = SEQ // CHUNK
ATT_W = 384
HEAD = 64
SSD_W = 384
SSD_CONV_DIM = 896
SSD_STATE = 128
SGU_W = 256
DILATIONS = (1, 4, 16)
W_QKV = 3 * ATT_W
W_SSD = SSD_CONV_DIM + SSD_W + SSD_W
W_UV = 2 * SGU_W
ADAM_LR = 0.001
ADAM_B1 = 0.9
ADAM_B2 = 0.999
ADAM_EPS = 1e-08
ADAM_WD = 0.01
ADAM_STEP = 10
NEG = -1e30
ATTN_BWD_VMEM = 48 * 2 ** 20
FFN_VMEM = 60 * 2 ** 20


def _dot(a, b):
    return jnp.dot(a, b, preferred_element_type=F32)


def _dot_nt(a, b):
    return lax.dot_general(a, b, (((1,), (1,)), ((), ())), preferred_element_type=F32)


def _dot_tn(a, b):
    return lax.dot_general(a, b, (((0,), (0,)), ((), ())), preferred_element_type=F32)


def _sigmoid(x):
    return 1.0 / (1.0 + jnp.exp(-x))


def _call(body, *, name, grid, in_specs, out_specs, out_shape, scratch=(), sem=None, vmem=None):
    return pl.pallas_call(
        body, name=name, grid=grid, in_specs=in_specs, out_specs=out_specs, out_shape=out_shape,
        scratch_shapes=list(scratch),
        compiler_params=pltpu.CompilerParams(dimension_semantics=sem, vmem_limit_bytes=vmem),
    )


def _tile(n, want):
    t = min(n, want)
    while n % t:
        t //= 2
    return t


def _final_loss(x, g, tgt):
    T, D = x.shape
    tm = _tile(T, 512)

    def body(x_ref, g_ref, t_ref, l_ref, dx_ref, dg_ref):
        @pl.when(pl.program_id(0) == 0)
        def _():
            dg_ref[...] = jnp.zeros_like(dg_ref)
            l_ref[...] = jnp.zeros_like(l_ref)

        xf = x_ref[...]
        gg = g_ref[...]
        r = lax.rsqrt(jnp.mean(xf * xf, axis=-1, keepdims=True) + RMS_EPS)
        xn = xf * r
        e = xn * gg - t_ref[...]
        part = 0.5 * jnp.sum(jnp.mean(e * e, axis=-1, keepdims=True), axis=0, keepdims=True)
        l_ref[...] += jnp.broadcast_to(part, l_ref.shape)
        dy = e * (1.0 / D)
        u = dy * gg
        mu = jnp.mean(u * xf, axis=-1, keepdims=True)
        dx_ref[...] = r * (u - xf * (r * r * mu))
        dg_ref[...] += jnp.sum(dy * xn, axis=0, keepdims=True)

    row = pl.BlockSpec((tm, D), lambda i: (i, 0))
    vec = pl.BlockSpec((1, D), lambda i: (0, 0))
    lsp = pl.BlockSpec((1, 128), lambda i: (0, 0))
    return _call(body, name="final_loss", grid=(T // tm,), in_specs=[row, vec, row], out_specs=[lsp, row, vec],
                 out_shape=[jax.ShapeDtypeStruct((1, 128), F32), jax.ShapeDtypeStruct((T, D), F32),
                            jax.ShapeDtypeStruct((1, D), F32)],
                 sem=("arbitrary",))(x, g, tgt)


def _slabs(tm, n=2):
    return [slice(k * tm // n, (k + 1) * tm // n) for k in range(n)] if tm % (16 * n) == 0 else [slice(0, tm)]


def _resident(shape):
    return pl.BlockSpec(shape, lambda *_: (0,) * len(shape), pipeline_mode=pl.Buffered(1))


def _ffn_fwd_k(x, gn, wg, wu, wd):
    T, D = x.shape
    NS, _, Fs = wg.shape
    tm = _tile(T, 1024)

    def body(x_ref, gn_ref, wg_ref, wu_ref, wd_ref, o_ref, h_ref, s1_ref, s2_ref, a_ref, hs, acc):
        j = pl.program_id(1)

        @pl.when(j == 0)
        def _():
            xf = x_ref[...]
            r = lax.rsqrt(jnp.mean(xf * xf, axis=-1, keepdims=True) + RMS_EPS)
            hs[...] = (xf * r * gn_ref[...]).astype(BF)
            h_ref[...] = hs[...]
            acc[...] = jnp.zeros_like(acc)

        h = hs[...]
        g = _dot(h, wg_ref[...])
        u = _dot(h, wu_ref[...])
        sg = _sigmoid(g)
        s1 = g * sg
        a = (s1 * u).astype(BF)
        s1_ref[...] = s1.astype(BF)
        s2_ref[...] = (u * (sg * (1.0 + g * (1.0 - sg)))).astype(BF)
        a_ref[...] = a
        acc[...] += _dot(a, wd_ref[...])

        @pl.when(j == NS - 1)
        def _():
            o_ref[...] = x_ref[...] + 0.5 * acc[...]

    row = pl.BlockSpec((tm, D), lambda i, j: (i, 0))
    act = pl.BlockSpec((None, tm, Fs), lambda i, j: (j, i, 0))
    sh = jax.ShapeDtypeStruct((NS, T, Fs), BF)
    wspec = lambda w: pl.BlockSpec((None,) + w.shape[1:], lambda i, j: (j, 0, 0))
    return _call(body, name="ffn_fwd", grid=(T // tm, NS),
                 in_specs=[row, pl.BlockSpec((1, D), lambda i, j: (0, 0)), wspec(wg), wspec(wu), wspec(wd)],
                 out_specs=[row, row, act, act, act],
                 out_shape=[jax.ShapeDtypeStruct((T, D), F32), jax.ShapeDtypeStruct((T, D), BF), sh, sh, sh],
                 scratch=[pltpu.VMEM((tm, D), BF), pltpu.VMEM((tm, D), F32)],
                 sem=("parallel", "arbitrary"), vmem=FFN_VMEM)(x, gn, wg, wu, wd)


def _ffn_bwd_k1(dxo, x, gn, s1, s2, wg, wu, wd):
    NS, T, Fs = s1.shape
    D = x.shape[1]
    tm = _tile(T, 512)

    def body(dxo_ref, x_ref, gn_ref, s1_ref, s2_ref, wg_ref, wu_ref, wd_ref,
             dx_ref, dgn_ref, dg_ref, du_ref, dy_ref, dys, acc):
        i, j = pl.program_id(0), pl.program_id(1)

        @pl.when((i == 0) & (j == 0))
        def _():
            dgn_ref[...] = jnp.zeros_like(dgn_ref)

        @pl.when(j == 0)
        def _():
            dys[...] = (0.5 * dxo_ref[...]).astype(BF)
            dy_ref[...] = dys[...]
            acc[...] = jnp.zeros_like(acc)

        for rows in _slabs(tm):
            da = _dot_nt(dys[rows, :], wd_ref[j])
            dg = (da * s2_ref[rows, :].astype(F32)).astype(BF)
            du = (da * s1_ref[rows, :].astype(F32)).astype(BF)
            dg_ref[rows, :] = dg
            du_ref[rows, :] = du
            acc[rows, :] += _dot_nt(dg, wg_ref[j]) + _dot_nt(du, wu_ref[j])

        @pl.when(j == NS - 1)
        def _():
            xf = x_ref[...]
            r = lax.rsqrt(jnp.mean(xf * xf, axis=-1, keepdims=True) + RMS_EPS)
            dh = acc[...]
            uu = dh * gn_ref[...]
            mu = jnp.mean(uu * xf, axis=-1, keepdims=True)
            dx_ref[...] = dxo_ref[...] + r * (uu - xf * (r * r * mu))
            dgn_ref[...] += jnp.sum(dh * xf * r, axis=0, keepdims=True)

    row = pl.BlockSpec((tm, D), lambda i, j: (i, 0))
    vec = pl.BlockSpec((1, D), lambda i, j: (0, 0))
    act = pl.BlockSpec((None, tm, Fs), lambda i, j: (j, i, 0))
    sh = jax.ShapeDtypeStruct((NS, T, Fs), BF)
    return _call(body, name="ffn_bwd_x", grid=(T // tm, NS),
                 in_specs=[row, row, vec, act, act, _resident(wg.shape), _resident(wu.shape), _resident(wd.shape)],
                 out_specs=[row, vec, act, act, row],
                 out_shape=[jax.ShapeDtypeStruct((T, D), F32), jax.ShapeDtypeStruct((1, D), F32), sh, sh,
                            jax.ShapeDtypeStruct((T, D), BF)],
                 scratch=[pltpu.VMEM((tm, D), BF), pltpu.VMEM((tm, D), F32)],
                 sem=("arbitrary", "arbitrary"))(dxo, x, gn, s1, s2, wg, wu, wd)


def _ffn_bwd_k2(hb, dyb, a, dg, du):
    NS, T, Fs = a.shape
    D = hb.shape[1]
    tk = _tile(T, 1024)

    def body(h_ref, dy_ref, a_ref, dg_ref, du_ref, og_ref, ou_ref, od_ref):
        @pl.when(pl.program_id(1) == 0)
        def _():
            og_ref[...] = jnp.zeros_like(og_ref)
            ou_ref[...] = jnp.zeros_like(ou_ref)
            od_ref[...] = jnp.zeros_like(od_ref)

        h = h_ref[...]
        og_ref[...] += _dot_tn(dg_ref[...], h)
        ou_ref[...] += _dot_tn(du_ref[...], h)
        od_ref[...] += _dot_tn(a_ref[...], dy_ref[...])

    row = pl.BlockSpec((tk, D), lambda j, k: (k, 0))
    act = pl.BlockSpec((None, tk, Fs), lambda j, k: (j, k, 0))
    return _call(body, name="ffn_bwd_w", grid=(NS, T // tk), in_specs=[row, row, act, act, act],
                 out_specs=[pl.BlockSpec((None, Fs, D), lambda j, k: (j, 0, 0))] * 3,
                 out_shape=[jax.ShapeDtypeStruct((NS, Fs, D), F32)] * 3,
                 sem=("parallel", "arbitrary"))(hb, dyb, a, dg, du)


def _mm_nn(a, b, res=None, out_dtype=F32):
    T, K = a.shape
    N = b.shape[1]
    tm = _tile(T, 512)
    tn = N if N <= 2048 else _tile(N, 1024)

    def body(*refs):
        if res is None:
            a_ref, b_ref, o_ref = refs
            o_ref[...] = _dot(a_ref[...], b_ref[...]).astype(out_dtype)
        else:
            a_ref, b_ref, r_ref, o_ref = refs
            o_ref[...] = (r_ref[...] + _dot(a_ref[...], b_ref[...])).astype(out_dtype)

    o = pl.BlockSpec((tm, tn), lambda i, j: (i, j))
    ins = [pl.BlockSpec((tm, K), lambda i, j: (i, 0)), pl.BlockSpec((K, tn), lambda i, j: (0, j))]
    args = [a, b]
    if res is not None:
        ins.append(o)
        args.append(res)
    return _call(body, name="mm_nn", grid=(T // tm, N // tn), in_specs=ins, out_specs=o,
                 out_shape=jax.ShapeDtypeStruct((T, N), out_dtype), sem=("parallel", "parallel"))(*args)


def _mix_bwd_dy(dxo, w_out):
    T, D = dxo.shape
    tm = _tile(T, 512)
    cuts = (0, ATT_W, ATT_W + SSD_W, ATT_W + SSD_W + SGU_W)

    def body(dx_ref, w_ref, a_ref, s_ref, g_ref):
        d = _dot_nt(dx_ref[...].astype(BF), w_ref[...])
        for o_ref, lo, hi in zip((a_ref, s_ref, g_ref), cuts[:-1], cuts[1:]):
            o_ref[...] = d[:, lo:hi]

    row = lambda w: pl.BlockSpec((tm, w), lambda i: (i, 0))
    return _call(body, name="mix_bwd_dy", grid=(T // tm,), in_specs=[row(D), _resident(w_out.shape)],
                 out_specs=[row(ATT_W), row(SSD_W), row(SGU_W)],
                 out_shape=[jax.ShapeDtypeStruct((T, w), F32) for w in (ATT_W, SSD_W, SGU_W)],
                 sem=("parallel",))(dxo, w_out)


def _mm_tn(a, b):
    T, M = a.shape
    N = b.shape[1]
    tk = _tile(T, 1024)
    tmm = _tile(M, 512)

    def body(a_ref, b_ref, o_ref):
        @pl.when(pl.program_id(1) == 0)
        def _():
            o_ref[...] = jnp.zeros_like(o_ref)

        o_ref[...] += _dot_tn(a_ref[...].astype(BF), b_ref[...].astype(BF))

    return _call(body, name="mm_tn", grid=(M // tmm, T // tk),
                 in_specs=[pl.BlockSpec((tk, tmm), lambda i, k: (k, i)), pl.BlockSpec((tk, N), lambda i, k: (k, 0))],
                 out_specs=pl.BlockSpec((tmm, N), lambda i, k: (i, 0)),
                 out_shape=jax.ShapeDtypeStruct((M, N), F32), sem=("parallel", "arbitrary"))(a, b)


def _mix_proj(x, gn, win):
    T, D = x.shape
    tm = _tile(T, 512)
    cuts = (0, W_QKV, W_QKV + W_SSD, W_QKV + W_SSD + W_UV)

    def body(x_ref, gn_ref, w_ref, h_ref, q_ref, s_ref, u_ref):
        xf = x_ref[...]
        r = lax.rsqrt(jnp.mean(xf * xf, axis=-1, keepdims=True) + RMS_EPS)
        h = (xf * r * gn_ref[...]).astype(BF)
        h_ref[...] = h
        for o_ref, lo, hi in zip((q_ref, s_ref, u_ref), cuts[:-1], cuts[1:]):
            o_ref[...] = _dot(h, w_ref[:, lo:hi])

    row = lambda w: pl.BlockSpec((tm, w), lambda i: (i, 0))
    return _call(body, name="mix_proj", grid=(T // tm,),
                 in_specs=[row(D), pl.BlockSpec((1, D), lambda i: (0, 0)), _resident(win.shape)],
                 out_specs=[row(D), row(W_QKV), row(W_SSD), row(W_UV)],
                 out_shape=[jax.ShapeDtypeStruct((T, D), BF), jax.ShapeDtypeStruct((T, W_QKV), F32),
                            jax.ShapeDtypeStruct((T, W_SSD), F32), jax.ShapeDtypeStruct((T, W_UV), F32)],
                 sem=("parallel",))(x, gn, win)


def _mix_bwd_dx(dqkv, dsin, duv, win, x, gn, dxo):
    T, D = x.shape
    tm = _tile(T, 512)
    cuts = (0, W_QKV, W_QKV + W_SSD, W_QKV + W_SSD + W_UV)

    def body(dq_ref, ds_ref, du_ref, w_ref, x_ref, gn_ref, dxo_ref, dx_ref, dgn_ref):
        @pl.when(pl.program_id(0) == 0)
        def _():
            dgn_ref[...] = jnp.zeros_like(dgn_ref)

        dh = (_dot_nt(dq_ref[...], w_ref[:, cuts[0]:cuts[1]]) + _dot_nt(ds_ref[...], w_ref[:, cuts[1]:cuts[2]])
              + _dot_nt(du_ref[...], w_ref[:, cuts[2]:cuts[3]]))
        xf = x_ref[...]
        r = lax.rsqrt(jnp.mean(xf * xf, axis=-1, keepdims=True) + RMS_EPS)
        uu = dh * gn_ref[...]
        mu = jnp.mean(uu * xf, axis=-1, keepdims=True)
        dx_ref[...] = dxo_ref[...] + r * (uu - xf * (r * r * mu))
        dgn_ref[...] += jnp.sum(dh * xf * r, axis=0, keepdims=True)

    row = lambda w: pl.BlockSpec((tm, w), lambda i: (i, 0))
    vec = pl.BlockSpec((1, D), lambda i: (0, 0))
    return _call(body, name="mix_bwd_dx", grid=(T // tm,),
                 in_specs=[row(W_QKV), row(W_SSD), row(W_UV), _resident(win.shape), row(D), vec, row(D)],
                 out_specs=[row(D), vec],
                 out_shape=[jax.ShapeDtypeStruct((T, D), F32), jax.ShapeDtypeStruct((1, D), F32)],
                 sem=("arbitrary",))(dqkv, dsin, duv, win, x, gn, dxo)


def _lane_mask(e, width=128):
    return (lax.broadcasted_iota(jnp.int32, (1, width), 1) // HEAD) == e


def _band_mask(n):
    qi = lax.broadcasted_iota(jnp.int32, (CHUNK, 2 * CHUNK), 0)
    kj = lax.broadcasted_iota(jnp.int32, (CHUNK, 2 * CHUNK), 1)
    dist = qi + CHUNK - kj
    return (dist >= 0) & (dist <= CHUNK) & ((kj >= CHUNK) | (n > 0))


def _sub_rows(r, block, dil):
    if dil == 1:
        return pl.ds(pl.multiple_of(block * CHUNK, CHUNK), CHUNK)
    return pl.ds(r + dil * CHUNK * block, CHUNK, stride=dil)


def _attn_specs(T, dil):
    B, nb = T // SEQ, SEQ // (CHUNK * dil)
    once = dict(pipeline_mode=pl.Buffered(1))
    q_like = lambda col: pl.BlockSpec((CHUNK * dil, 128), lambda b, n, r: (b * nb + n, col), **(once if nb == 1 else {}))
    k_like = lambda col: pl.BlockSpec((SEQ, 128), lambda b, n, r: (b, col), **once)
    return B, nb, q_like, k_like


def _attn_fwd(qkv, dil):
    T = qkv.shape[0]
    B, nb, q_like, k_like = _attn_specs(T, dil)
    scale = HEAD ** -0.5

    def body(*refs):
        q_t, k_t, v_t, o_t, l_t = refs[0:3], refs[3:6], refs[6:9], refs[9:12], refs[12:15]
        n, r = pl.program_id(1), pl.program_id(2)
        mine = _sub_rows(r, 0, dil)
        cur, prv = _sub_rows(r, n, dil), _sub_rows(r, jnp.maximum(n - 1, 0), dil)
        mask = _band_mask(n)
        for t in range(3):
            qt = q_t[t][mine, :].astype(BF)
            kt = jnp.concatenate([k_t[t][prv, :], k_t[t][cur, :]], axis=0).astype(BF)
            vt = jnp.concatenate([v_t[t][prv, :], v_t[t][cur, :]], axis=0).astype(BF)
            o_pair = jnp.zeros((CHUNK, 128), F32)
            l_pair = jnp.zeros((CHUNK, 128), F32)
            for e in range(2):
                lm = _lane_mask(e)
                s = _dot_nt(jnp.where(lm, qt, jnp.zeros_like(qt)), kt) * scale
                s = jnp.where(mask, s, NEG)
                m = jnp.max(s, axis=-1, keepdims=True)
                p = jnp.exp(s - m)
                den = jnp.sum(p, axis=-1, keepdims=True)
                o = _dot(p.astype(BF), vt) / den
                o_pair = jnp.where(lm, o, o_pair)
                l_pair = jnp.where(lm, m + jnp.log(den), l_pair)
            o_t[t][mine, :] = o_pair
            l_t[t][mine, :] = l_pair

    out_spec = pl.BlockSpec((CHUNK * dil, 128), lambda b, n, r: (b * nb + n, 0))
    sh = jax.ShapeDtypeStruct((T, 128), F32)
    outs = _call(
        body, name=f"attn_fwd_d{dil}", grid=(B, nb, dil),
        in_specs=[q_like(t) for t in range(3)] + [k_like(3 + t) for t in range(3)] + [k_like(6 + t) for t in range(3)],
        out_specs=[out_spec] * 6, out_shape=[sh] * 6, sem=("parallel", "arbitrary", "arbitrary"))(*([qkv] * 9))
    return list(outs[0:3]), list(outs[3:6])


def _attn_combine(branches):
    T = branches[0][0][0].shape[0]
    tm = _tile(T, 512)

    def body(*refs):
        y_ref, l_ref = refs[-2:]
        for t in range(3):
            o = [refs[6 * i + t][...] for i in range(3)]
            a, b, c = [refs[6 * i + 3 + t][...] for i in range(3)]
            m = jnp.maximum(jnp.maximum(a, b), c)
            ea, eb, ec = jnp.exp(a - m), jnp.exp(b - m), jnp.exp(c - m)
            z = ea + eb + ec
            y_ref[:, 128 * t:128 * (t + 1)] = (ea * o[0] + eb * o[1] + ec * o[2]) / z
            l_ref[:, 128 * t:128 * (t + 1)] = m + jnp.log(z)

    tile = pl.BlockSpec((tm, 128), lambda i: (i, 0))
    row = pl.BlockSpec((tm, ATT_W), lambda i: (i, 0))
    sh = jax.ShapeDtypeStruct((T, ATT_W), F32)
    flat = [a for o_t, l_t in branches for a in (*o_t, *l_t)]
    return _call(body, name="attn_combine", grid=(T // tm,), in_specs=[tile] * 18, out_specs=[row, row],
                 out_shape=[sh, sh], sem=("parallel",))(*flat)


def _attn_bwd(qkv, do, out, lse, dil):
    T = qkv.shape[0]
    B, nb, q_like, k_like = _attn_specs(T, dil)
    scale = HEAD ** -0.5

    def body(*refs):
        q_t, k_t, v_t = refs[0:3], refs[3:6], refs[6:9]
        do_t, out_t, lse_t = refs[9:12], refs[12:15], refs[15:18]
        dq_t, dk_t, dv_t = refs[18:21], refs[21:24], refs[24:27]
        n, r = pl.program_id(1), pl.program_id(2)

        @pl.when((n == 0) & (r == 0))
        def _():
            for t in range(3):
                dk_t[t][...] = jnp.zeros_like(dk_t[t])
                dv_t[t][...] = jnp.zeros_like(dv_t[t])

        mine = _sub_rows(r, 0, dil)
        cur, prv = _sub_rows(r, n, dil), _sub_rows(r, jnp.maximum(n - 1, 0), dil)
        mask = _band_mask(n)
        for t in range(3):
            qt = q_t[t][mine, :].astype(BF)
            kt = jnp.concatenate([k_t[t][prv, :], k_t[t][cur, :]], axis=0).astype(BF)
            vt = jnp.concatenate([v_t[t][prv, :], v_t[t][cur, :]], axis=0).astype(BF)
            do_ = do_t[t][mine, :]
            dlt = do_ * out_t[t][mine, :]
            ls = lse_t[t][mine, :]
            dq_pair = jnp.zeros((CHUNK, 128), F32)
            dk_acc = jnp.zeros((2 * CHUNK, 128), F32)
            dv_acc = jnp.zeros((2 * CHUNK, 128), F32)
            for e in range(2):
                lm = _lane_mask(e)
                qm = jnp.where(lm, qt, jnp.zeros_like(qt))
                s = _dot_nt(qm, kt) * scale
                p = jnp.exp(jnp.where(mask, s - ls[:, HEAD * e:HEAD * e + 1], NEG))
                dom = jnp.where(lm, do_, 0.0).astype(BF)
                dv_acc += _dot_tn(p.astype(BF), dom)
                dp = _dot_nt(dom, vt)
                delta = jnp.sum(jnp.where(lm, dlt, 0.0), axis=-1, keepdims=True)
                ds = (p * (dp - delta) * scale).astype(BF)
                dq_pair += jnp.where(lm, _dot(ds, kt), 0.0)
                dk_acc += _dot_tn(ds, qm)
            dq_t[t][mine, :] = dq_pair
            dk_t[t][cur, :] = dk_t[t][cur, :] + dk_acc[CHUNK:]
            dk_t[t][prv, :] = dk_t[t][prv, :] + dk_acc[:CHUNK]
            dv_t[t][cur, :] = dv_t[t][cur, :] + dv_acc[CHUNK:]
            dv_t[t][prv, :] = dv_t[t][prv, :] + dv_acc[:CHUNK]

    q_out = pl.BlockSpec((CHUNK * dil, 128), lambda b, n, r: (b * nb + n, 0))
    k_out = pl.BlockSpec((SEQ, 128), lambda b, n, r: (b, 0))
    sh = jax.ShapeDtypeStruct((T, 128), F32)
    tiles = lambda: [q_like(t) for t in range(3)]
    return list(_call(
        body, name=f"attn_bwd_d{dil}", grid=(B, nb, dil),
        in_specs=tiles() + [k_like(3 + t) for t in range(3)] + [k_like(6 + t) for t in range(3)]
        + tiles() + tiles() + tiles(),
        out_specs=[q_out] * 3 + [k_out] * 6, out_shape=[sh] * 9,
        sem=("parallel", "arbitrary", "arbitrary"), vmem=ATTN_BWD_VMEM)(*([qkv] * 9 + [do] * 3 + [out] * 3 + [lse] * 3)))


def _sum_branches(parts):
    T = parts[0][0].shape[0]
    tm = _tile(T, 512)

    def body(*refs):
        o_ref = refs[-1]
        for c in range(9):
            acc = refs[c][...] + refs[9 + c][...] + refs[18 + c][...]
            o_ref[:, 128 * c:128 * (c + 1)] = acc.astype(BF)

    tile = pl.BlockSpec((tm, 128), lambda i: (i, 0))
    flat = [a for br in parts for a in br]
    return _call(body, name="attn_sum_branches", grid=(T // tm,), in_specs=[tile] * 27,
                 out_specs=pl.BlockSpec((tm, W_QKV), lambda i: (i, 0)),
                 out_shape=jax.ShapeDtypeStruct((T, W_QKV), BF), sem=("parallel",))(*flat)


def _silu(x):
    return x * _sigmoid(x)


def _dsilu(x):
    s = _sigmoid(x)
    return s * (1.0 + x * (1.0 - s))


def _log1p(u):
    return jnp.where(u < 0.01, u * (1.0 - u * (0.5 - u * (1.0 / 3.0))), jnp.log(1.0 + u))


def _softplus(x):
    return jnp.maximum(x, 0.0) + _log1p(jnp.exp(-jnp.abs(x)))


def _cumsum_rows(x, reverse=False):
    n = x.shape[0]
    rows = lax.broadcasted_iota(jnp.int32, x.shape, 0)
    k = 1
    while k < n:
        if reverse:
            x = x + jnp.where(rows < n - k, pltpu.roll(x, n - k, 0), 0.0)
        else:
            x = x + jnp.where(rows >= k, pltpu.roll(x, k, 0), 0.0)
        k *= 2
    return x


def _tri():
    r = lax.broadcasted_iota(jnp.int32, (CHUNK, CHUNK), 0)
    c = lax.broadcasted_iota(jnp.int32, (CHUNK, CHUNK), 1)
    return r >= c


def _row_mask(e):
    return (lax.broadcasted_iota(jnp.int32, (128, 1), 0) // HEAD) == e


def _first_lane(e):
    return lax.broadcasted_iota(jnp.int32, (1, 128), 1) == HEAD * e


def _ssd_pre(x_ref, halo_ref, first, cw_ref, cb_ref, dtb_ref, al_ref, ext):
    row = x_ref[...]
    z = row[:, SSD_CONV_DIM:SSD_CONV_DIM + SSD_W]
    u = row[:, SSD_CONV_DIM + SSD_W:] + dtb_ref[...]
    ext[0:8, :] = jnp.where(first, 0.0, halo_ref[:, 0:SSD_CONV_DIM])
    ext[8:8 + CHUNK, :] = row[:, 0:SSD_CONV_DIM]
    xc = cb_ref[...]
    for j in range(4):
        xc = xc + cw_ref[j:j + 1, :] * ext[pl.ds(5 + j, CHUNK), :]
    xa = _silu(xc)
    dt = _softplus(u)
    a = dt * (-jnp.exp(al_ref[...]))
    A = _cumsum_rows(a)
    return dict(z=z, u=u, xc=xc, xs=xa[:, 0:SSD_W], Bm=xa[:, SSD_W:SSD_W + 256], Cm=xa[:, SSD_W + 256:],
                dt=dt, a=a, A=A, AT=A.T, eA=jnp.exp(A), wdec=jnp.exp(A[CHUNK - 1:CHUNK, :] - A),
                dtot=jnp.exp(A[CHUNK - 1:CHUNK, :]))


def _ssd_y(p, hp_ref, dskip):
    tri = _tri()
    X = p["xs"] * p["dt"]
    Bb = [p["Bm"][:, 128 * g:128 * (g + 1)].astype(BF) for g in range(2)]
    Cb = [p["Cm"][:, 128 * g:128 * (g + 1)].astype(BF) for g in range(2)]
    CB = [_dot_nt(Cb[g], Bb[g]) for g in range(2)]
    tiles = []
    for t in range(3):
        sl = slice(128 * t, 128 * (t + 1))
        hpb = hp_ref[sl, :].astype(BF)
        acc = jnp.zeros((CHUNK, 128), F32)
        for e in range(2):
            h = 2 * t + e
            g, col = h // 3, HEAD * h
            lm = _lane_mask(e)
            L = jnp.exp(jnp.where(tri, p["A"][:, col:col + 1] - p["AT"][col:col + 1, :], NEG))
            yd = _dot((CB[g] * L).astype(BF), jnp.where(lm, X[:, sl], 0.0).astype(BF))
            yo = _dot_nt(Cb[g], hpb) * p["eA"][:, sl]
            acc = acc + jnp.where(lm, yd + yo, 0.0)
        tiles.append(acc)
    return jnp.concatenate(tiles, axis=1) + dskip * p["xs"], X, Bb, Cb, CB


def _group_stats(v):
    g0 = lax.broadcasted_iota(jnp.int32, (1, SSD_W), 1) < SSD_W // 2
    m0 = jnp.sum(jnp.where(g0, v, 0.0), axis=-1, keepdims=True) * (2.0 / SSD_W)
    m1 = jnp.sum(jnp.where(g0, 0.0, v), axis=-1, keepdims=True) * (2.0 / SSD_W)
    return jnp.where(g0, m0, m1)


def _ssd_specs(T, rev):
    B = T // SEQ

    def chunk(b, c):
        return b * N_CHUNK + (N_CHUNK - 1 - c if rev else c)

    row = pl.BlockSpec((CHUNK, W_SSD), lambda b, c: (chunk(b, c), 0))
    halo = pl.BlockSpec((8, W_SSD), lambda b, c: (jnp.maximum(chunk(b, c) * (CHUNK // 8) - 1, 0), 0))
    hp = pl.BlockSpec((None, SSD_W, SSD_STATE), lambda b, c: (chunk(b, c), 0, 0))
    y = pl.BlockSpec((CHUNK, SSD_W), lambda b, c: (chunk(b, c), 0))
    const = lambda r, w: pl.BlockSpec((r, w), lambda b, c: (0, 0))
    params = [const(4, SSD_CONV_DIM), const(1, SSD_CONV_DIM)] + [const(1, SSD_W)] * 4
    return B, row, halo, hp, y, const, params


def _ssd_fwd(sin, conv_w, conv_b, dtb, alog, dskip, norm_g):
    T = sin.shape[0]
    B, row, halo, hp, y, const, params = _ssd_specs(T, False)

    def body(x_ref, halo_ref, cw_ref, cb_ref, dtb_ref, al_ref, dk_ref, ng_ref, y_ref, hp_ref, ext, hst):
        c = pl.program_id(1)

        @pl.when(c == 0)
        def _():
            hst[...] = jnp.zeros_like(hst)

        p = _ssd_pre(x_ref, halo_ref, c == 0, cw_ref, cb_ref, dtb_ref, al_ref, ext)
        yv, X, Bb, Cb, CB = _ssd_y(p, hst, dk_ref[...])
        hp_ref[...] = hst[...]
        for t in range(3):
            sl = slice(128 * t, 128 * (t + 1))
            old = hst[sl, :]
            new = old
            for e in range(2):
                h = 2 * t + e
                g, col = h // 3, HEAD * h
                st = _dot_tn(jnp.where(_lane_mask(e), X[:, sl] * p["wdec"][:, sl], 0.0).astype(BF), Bb[g])
                new = jnp.where(_row_mask(e), old * p["dtot"][:, col:col + 1] + st, new)
            hst[sl, :] = new
        y2 = yv * _silu(p["z"])
        r = lax.rsqrt(_group_stats(y2 * y2) + RMS_EPS)
        y_ref[...] = y2 * r * ng_ref[...]

    return _call(body, name="ssd_fwd", grid=(B, N_CHUNK), in_specs=[row, halo] + params, out_specs=[y, hp],
                 out_shape=[jax.ShapeDtypeStruct((T, SSD_W), F32),
                            jax.ShapeDtypeStruct((T // CHUNK, SSD_W, SSD_STATE), F32)],
                 scratch=[pltpu.VMEM((8 + CHUNK, SSD_CONV_DIM), F32), pltpu.VMEM((SSD_W, SSD_STATE), F32)],
                 sem=("parallel", "arbitrary"))(sin, sin, conv_w, conv_b, dtb, alog, dskip, norm_g)


def _ssd_bwd(sin, hprev, dy3, conv_w, conv_b, dtb, alog, dskip, norm_g):
    T = sin.shape[0]
    B, row, halo, hp, y, const, params = _ssd_specs(T, True)

    def body(x_ref, halo_ref, hp_ref, dy_ref, cw_ref, cb_ref, dtb_ref, al_ref, dk_ref, ng_ref,
             dx_ref, dcw_ref, dcb_ref, dvec_ref, ext, ext2, dh):
        c = pl.program_id(1)

        @pl.when((pl.program_id(0) == 0) & (c == 0))
        def _():
            dcw_ref[...] = jnp.zeros_like(dcw_ref)
            dcb_ref[...] = jnp.zeros_like(dcb_ref)
            dvec_ref[...] = jnp.zeros_like(dvec_ref)

        @pl.when(c == 0)
        def _():
            dh[...] = jnp.zeros_like(dh)
            ext2[CHUNK:CHUNK + 8, :] = jnp.zeros((8, SSD_CONV_DIM), F32)

        p = _ssd_pre(x_ref, halo_ref, c == N_CHUNK - 1, cw_ref, cb_ref, dtb_ref, al_ref, ext)
        dskip_ = dk_ref[...]
        yv, X, Bb, Cb, CB = _ssd_y(p, hp_ref, dskip_)
        xs, z, A, AT = p["xs"], p["z"], p["A"], p["AT"]

        sz = _silu(z)
        y2 = yv * sz
        r = lax.rsqrt(_group_stats(y2 * y2) + RMS_EPS)
        dy3_ = dy_ref[...]
        uu = dy3_ * ng_ref[...]
        dy2 = r * (uu - y2 * (r * r * _group_stats(uu * y2)))
        dy = dy2 * sz
        dz = dy2 * yv * _dsilu(z)

        tri = _tri()
        rows = lax.broadcasted_iota(jnp.int32, (CHUNK, 1), 0)
        dG = [jnp.zeros((CHUNK, CHUNK), F32) for _ in range(2)]
        dB = [jnp.zeros((CHUNK, SSD_STATE), F32) for _ in range(2)]
        dC = [jnp.zeros((CHUNK, SSD_STATE), F32) for _ in range(2)]
        dX_t, dA_t, ddtx_t = [], [], []
        for t in range(3):
            sl = slice(128 * t, 128 * (t + 1))
            hp_t = hp_ref[sl, :]
            hpb = hp_t.astype(BF)
            dhc = dh[sl, :]
            dh_new = jnp.zeros((128, SSD_STATE), F32)
            dX = jnp.zeros((CHUNK, 128), F32)
            dA = jnp.zeros((CHUNK, 128), F32)
            ddtx = jnp.zeros((CHUNK, 128), F32)
            for e in range(2):
                h = 2 * t + e
                g, col = h // 3, HEAD * h
                lm, rm, fl = _lane_mask(e), _row_mask(e), _first_lane(e)
                L = jnp.exp(jnp.where(tri, A[:, col:col + 1] - AT[col:col + 1, :], NEG))
                Mf = CB[g] * L
                Xm = jnp.where(lm, X[:, sl], 0.0)
                Xmb = Xm.astype(BF)
                dyh = jnp.where(lm, dy[:, sl], 0.0)
                dyb = dyh.astype(BF)
                dXh = _dot_tn(Mf.astype(BF), dyb)
                dM = jnp.where(tri, _dot_nt(dyb, Xmb), 0.0)
                Wm = dM * Mf
                dAc = jnp.sum(Wm, axis=-1, keepdims=True) - jnp.sum(Wm.T, axis=-1, keepdims=True)
                dG[g] = dG[g] + dM * L
                eAt = p["eA"][:, sl]
                yo = _dot_nt(Cb[g], hpb)
                dyo = (dyh * eAt).astype(BF)
                dC[g] = dC[g] + _dot(dyo, hpb)
                dh_new = dh_new + _dot_tn(dyo, Cb[g])
                dAc = dAc + jnp.sum(dyh * yo * eAt, axis=-1, keepdims=True)
                dHn = jnp.where(rm, dhc, 0.0)
                dHnb = dHn.astype(BF)
                dec = p["dtot"][:, col:col + 1]
                dh_new = dh_new + dec * dHn
                Z = _dot_nt(Bb[g], dHnb)
                wt = p["wdec"][:, sl]
                xi = jnp.sum(Xm * Z, axis=-1, keepdims=True) * p["wdec"][:, col:col + 1]
                dXh = dXh + wt * Z
                dB[g] = dB[g] + _dot(jnp.where(lm, X[:, sl] * wt, 0.0).astype(BF), dHnb)
                dAtot = jnp.sum(xi, axis=0, keepdims=True) + dec * jnp.sum(
                    jnp.sum(dHn * hp_t, axis=-1, keepdims=True), axis=0, keepdims=True)
                dAc = dAc - xi + jnp.where(rows == CHUNK - 1, dAtot, 0.0)
                dA = dA + jnp.where(fl, dAc, 0.0)
                dX = dX + dXh
                ddtx = ddtx + jnp.where(fl, jnp.sum(dXh * xs[:, sl], axis=-1, keepdims=True), 0.0)
            dh[sl, :] = dh_new
            dX_t.append(dX)
            dA_t.append(dA)
            ddtx_t.append(ddtx)
        for g in range(2):
            dGb = dG[g].astype(BF)
            dC[g] = dC[g] + _dot(dGb, Bb[g])
            dB[g] = dB[g] + _dot_tn(dGb, Cb[g])
        dXf = jnp.concatenate(dX_t, axis=1)
        da = _cumsum_rows(jnp.concatenate(dA_t, axis=1), reverse=True)
        ddt = da * (-jnp.exp(al_ref[...])) + jnp.concatenate(ddtx_t, axis=1)
        du = ddt * _sigmoid(p["u"])
        dxs = dXf * p["dt"] + dskip_ * dy
        dxc = jnp.concatenate([dxs, dB[0], dB[1], dC[0], dC[1]], axis=1) * _dsilu(p["xc"])
        ext2[0:CHUNK, :] = dxc
        dxbc = jnp.zeros((CHUNK, SSD_CONV_DIM), F32)
        for j in range(4):
            dxbc = dxbc + cw_ref[j:j + 1, :] * ext2[pl.ds(3 - j, CHUNK), :]
            dcw_ref[j:j + 1, :] += jnp.sum(dxc * ext[pl.ds(5 + j, CHUNK), :], axis=0, keepdims=True)
        ext2[CHUNK:CHUNK + 8, :] = dxc[0:8, :]
        dcb_ref[...] += jnp.sum(dxc, axis=0, keepdims=True)
        dvec_ref[0:1, :] += jnp.sum(du, axis=0, keepdims=True)
        dvec_ref[1:2, :] += jnp.sum(da * p["a"], axis=0, keepdims=True)
        dvec_ref[2:3, :] += jnp.sum(dy * xs, axis=0, keepdims=True)
        dvec_ref[3:4, :] += jnp.sum(dy3_ * y2 * r, axis=0, keepdims=True)
        dx_ref[...] = jnp.concatenate([dxbc, dz, du], axis=1).astype(BF)

    return _call(body, name="ssd_bwd", grid=(B, N_CHUNK), in_specs=[row, halo, hp, y] + params,
                 out_specs=[row, const(4, SSD_CONV_DIM), const(1, SSD_CONV_DIM), const(8, SSD_W)],
                 out_shape=[jax.ShapeDtypeStruct((T, W_SSD), BF), jax.ShapeDtypeStruct((4, SSD_CONV_DIM), F32),
                            jax.ShapeDtypeStruct((1, SSD_CONV_DIM), F32), jax.ShapeDtypeStruct((8, SSD_W), F32)],
                 scratch=[pltpu.VMEM((8 + CHUNK, SSD_CONV_DIM), F32), pltpu.VMEM((8 + CHUNK, SSD_CONV_DIM), F32),
                          pltpu.VMEM((SSD_W, SSD_STATE), F32)],
                 sem=("arbitrary", "arbitrary"))(sin, sin, hprev, dy3, conv_w, conv_b, dtb, alog, dskip, norm_g)


def _sgu_core(uv_ref, g_ref, b_ref, w_ref, bias_ref):
    x = uv_ref[...]
    cdf = 0.5 * (1.0 + lax.erf(x * (2.0 ** -0.5)))
    ge = x * cdf
    dge = cdf + x * jnp.exp(-0.5 * x * x) * ((2.0 * math.pi) ** -0.5)
    u, v = ge[:, 0:SGU_W], ge[:, SGU_W:]
    vc = v - jnp.mean(v, axis=-1, keepdims=True)
    rstd = lax.rsqrt(jnp.mean(vc * vc, axis=-1, keepdims=True) + LN_EPS)
    vhat = vc * rstd
    vn = vhat * g_ref[...] + b_ref[...]
    tri = _tri()
    wc = [jnp.where(tri, w_ref[gi], 0.0).astype(BF) for gi in range(4)]
    vm = [jnp.where(_lane_mask(gi % 2), vn[:, 128 * (gi // 2):128 * (gi // 2 + 1)], 0.0).astype(BF) for gi in range(4)]
    mixed = jnp.concatenate([_dot(wc[2 * t], vm[2 * t]) + _dot(wc[2 * t + 1], vm[2 * t + 1]) for t in range(2)],
                            axis=1) + bias_ref[...]
    return dict(dge=dge, u=u, rstd=rstd, vhat=vhat, wc=wc, vm=vm, mixed=mixed)


def _sgu_specs():
    vec = pl.BlockSpec((1, SGU_W), lambda i: (0, 0))
    return [pl.BlockSpec((CHUNK, W_UV), lambda i: (i, 0)), vec, vec,
            pl.BlockSpec((4, CHUNK, CHUNK), lambda i: (0, 0, 0)), pl.BlockSpec((CHUNK, SGU_W), lambda i: (0, 0))]


def _sgu_fwd(uv, ln_g, ln_b, w, bias):
    T = uv.shape[0]

    def body(uv_ref, g_ref, b_ref, w_ref, bias_ref, y_ref):
        s = _sgu_core(uv_ref, g_ref, b_ref, w_ref, bias_ref)
        y_ref[...] = s["u"] * s["mixed"]

    return _call(body, name="sgu_fwd", grid=(T // CHUNK,), in_specs=_sgu_specs(),
                 out_specs=pl.BlockSpec((CHUNK, SGU_W), lambda i: (i, 0)),
                 out_shape=jax.ShapeDtypeStruct((T, SGU_W), F32), sem=("parallel",))(uv, ln_g, ln_b, w, bias)


def _sgu_bwd(uv, dy, ln_g, ln_b, w, bias):
    T = uv.shape[0]

    def body(uv_ref, dy_ref, g_ref, b_ref, w_ref, bias_ref, dx_ref, dw_ref, dbias_ref, dln_ref):
        @pl.when(pl.program_id(0) == 0)
        def _():
            dw_ref[...] = jnp.zeros_like(dw_ref)
            dbias_ref[...] = jnp.zeros_like(dbias_ref)
            dln_ref[...] = jnp.zeros_like(dln_ref)

        s = _sgu_core(uv_ref, g_ref, b_ref, w_ref, bias_ref)
        dy_ = dy_ref[...]
        du = dy_ * s["mixed"]
        dmix = dy_ * s["u"]
        dbias_ref[...] += dmix
        tri = _tri()
        dvn_t = []
        for t in range(2):
            acc = jnp.zeros((CHUNK, 128), F32)
            for e in range(2):
                gi = 2 * t + e
                dmg = jnp.where(_lane_mask(e), dmix[:, 128 * t:128 * (t + 1)], 0.0).astype(BF)
                acc = acc + _dot_tn(s["wc"][gi], dmg)
                dw_ref[gi] += jnp.where(tri, _dot_nt(dmg, s["vm"][gi]), 0.0)
            dvn_t.append(acc)
        dvn = jnp.concatenate(dvn_t, axis=1)
        dln_ref[0:1, :] += jnp.sum(dvn * s["vhat"], axis=0, keepdims=True)
        dln_ref[1:2, :] += jnp.sum(dvn, axis=0, keepdims=True)
        dvh = dvn * g_ref[...]
        dv = s["rstd"] * (dvh - jnp.mean(dvh, axis=-1, keepdims=True)
                          - s["vhat"] * jnp.mean(dvh * s["vhat"], axis=-1, keepdims=True))
        dx_ref[...] = (jnp.concatenate([du, dv], axis=1) * s["dge"]).astype(BF)

    ins = _sgu_specs()
    return _call(body, name="sgu_bwd", grid=(T // CHUNK,),
                 in_specs=[ins[0], pl.BlockSpec((CHUNK, SGU_W), lambda i: (i, 0))] + ins[1:],
                 out_specs=[pl.BlockSpec((CHUNK, W_UV), lambda i: (i, 0)),
                            pl.BlockSpec((4, CHUNK, CHUNK), lambda i: (0, 0, 0)),
                            pl.BlockSpec((CHUNK, SGU_W), lambda i: (0, 0)), pl.BlockSpec((8, SGU_W), lambda i: (0, 0))],
                 out_shape=[jax.ShapeDtypeStruct((T, W_UV), BF), jax.ShapeDtypeStruct((4, CHUNK, CHUNK), F32),
                            jax.ShapeDtypeStruct((CHUNK, SGU_W), F32), jax.ShapeDtypeStruct((8, SGU_W), F32)],
                 sem=("arbitrary",))(uv, dy, ln_g, ln_b, w, bias)


def _adamw(w, g, m, v):
    R, C = w.shape
    tr = R

    def body(w_ref, g_ref, m_ref, v_ref, d_ref, nm_ref, nv_ref):
        g_ = g_ref[...]
        m2 = ADAM_B1 * m_ref[...] + (1.0 - ADAM_B1) * g_
        v2 = ADAM_B2 * v_ref[...] + (1.0 - ADAM_B2) * (g_ * g_)
        m_hat = m2 / (1.0 - ADAM_B1 ** ADAM_STEP)
        v_hat = v2 / (1.0 - ADAM_B2 ** ADAM_STEP)
        d_ref[...] = -ADAM_LR * (m_hat / (jnp.sqrt(v_hat) + ADAM_EPS) + ADAM_WD * w_ref[...])
        nm_ref[...] = m2
        nv_ref[...] = v2

    blk = pl.BlockSpec((tr, C), lambda i: (i, 0))
    sh = jax.ShapeDtypeStruct((R, C), F32)
    return _call(body, name="adamw", grid=(R // tr,), in_specs=[blk] * 4, out_specs=[blk] * 3,
                 out_shape=[sh] * 3, sem=("parallel",))(w, g, m, v)


def _adamw_pair(w, g0, g1, m, v, dep):
    L, R, C = w.shape
    tr = _tile(R, 256 if C <= 1024 else 64)

    def body(w_ref, g0_ref, g1_ref, m_ref, v_ref, dep_ref, d_ref, nm_ref, nv_ref, og_ref):
        g_ = jnp.where(pl.program_id(0) == 0, g0_ref[...], g1_ref[...])
        m2 = ADAM_B1 * m_ref[...] + (1.0 - ADAM_B1) * g_
        v2 = ADAM_B2 * v_ref[...] + (1.0 - ADAM_B2) * (g_ * g_)
        m_hat = m2 / (1.0 - ADAM_B1 ** ADAM_STEP)
        v_hat = v2 / (1.0 - ADAM_B2 ** ADAM_STEP)
        d_ref[...] = -ADAM_LR * (m_hat / (jnp.sqrt(v_hat) + ADAM_EPS) + ADAM_WD * w_ref[...])
        nm_ref[...] = m2
        nv_ref[...] = v2
        og_ref[...] = g_

    lay = pl.BlockSpec((None, tr, C), lambda l, i: (l, i, 0))
    one = lambda k: pl.BlockSpec((tr, C), lambda l, i: (jnp.where(l == k, i, 0), 0))
    return _call(body, name="adamw_pair", grid=(L, R // tr),
                 in_specs=[lay, one(0), one(1), lay, lay, pl.BlockSpec((8, 128), lambda l, i: (0, 0))],
                 out_specs=[lay] * 4,
                 out_shape=[jax.ShapeDtypeStruct((L, R, C), F32)] * 4,
                 sem=("parallel", "parallel"))(w, g0, g1, m, v, dep)


def _row_steps(rows):
    return 2 if rows % 32 == 0 else 1


def _pair_add(gbuf, rsib, c):
    NS, _, R, C = gbuf.shape
    n = _row_steps(R)
    tr = R // n

    def body(c_ref, a_ref, b_ref, o_ref):
        o_ref[...] = (a_ref[...] + b_ref[...]).astype(BF)

    blk = pl.BlockSpec((None, tr, C), lambda j, i, c_ref: (j, i, 0))
    return pl.pallas_call(
        body, name="rs_pair_add",
        grid_spec=pltpu.PrefetchScalarGridSpec(
            num_scalar_prefetch=1, grid=(NS, n),
            in_specs=[pl.BlockSpec((None, None, tr, C), lambda j, i, c_ref: (j, c_ref[0], i, 0)), blk],
            out_specs=blk),
        out_shape=jax.ShapeDtypeStruct((NS, R, C), BF),
        compiler_params=pltpu.CompilerParams(dimension_semantics=("parallel", "parallel")),
    )(jnp.reshape(c, (1,)).astype(jnp.int32), gbuf, rsib)


def _chip_sum(pair, recv, me, c):
    NS, R, C = pair.shape
    n = _row_steps(R)
    tr = R // n

    def body(s_ref, own_ref, p_ref, o_ref):
        p = [jnp.where(s_ref[0] == j, own_ref[...], p_ref[j]).astype(F32) for j in range(4)]
        o_ref[...] = ((p[0] + p[1]) + p[2]) + p[3]

    return pl.pallas_call(
        body, name="rs_chip_sum",
        grid_spec=pltpu.PrefetchScalarGridSpec(
            num_scalar_prefetch=1, grid=(n,),
            in_specs=[pl.BlockSpec((None, tr, C), lambda i, s: (s[0], i, 0)),
                      pl.BlockSpec((NS, tr, C), lambda i, s: (0, i, 0))],
            out_specs=pl.BlockSpec((None, tr, C), lambda i, s: (s[1], i, 0))),
        out_shape=jax.ShapeDtypeStruct((2, R, C), F32),
        compiler_params=pltpu.CompilerParams(dimension_semantics=("parallel",)),
    )(jnp.stack([me, c]).astype(jnp.int32), pair, recv)


MESH = pl.DeviceIdType.MESH
ANY = pl.BlockSpec(memory_space=pl.ANY)


def _place():
    x, y, c = lax.axis_index("x"), lax.axis_index("y"), lax.axis_index("c")
    return x, y, c, [(1 - x, y), (x, 1 - y), (1 - x, 1 - y)]


HBM = pl.BlockSpec(memory_space=pltpu.HBM)
SEM = pl.BlockSpec(memory_space=pltpu.SEMAPHORE)
EFFECT = pltpu.SideEffectType.DATAFLOW_SIDE_EFFECTING


class _Split:
    def __init__(self, tag, arrays, copies, n_copies, after=()):
        self.tag, self.copies, k = tag, copies, len(arrays)

        def body(*refs):
            sems = k + len(after)
            for cp in copies(refs[:k], refs[sems], refs[sems + 1]):
                cp.start()
            refs[-1][...] = jnp.zeros_like(refs[-1])

        out = pl.pallas_call(
            body, name=tag + "_start",
            out_shape=(pltpu.SemaphoreType.DMA((n_copies,)), pltpu.SemaphoreType.DMA((n_copies,)),
                       *[pltpu.HBM(a.shape, a.dtype) for a in arrays], jax.ShapeDtypeStruct((8, 128), F32)),
            in_specs=[HBM] * k + [ANY] * len(after),
            out_specs=(SEM, SEM, *[HBM] * k, pl.BlockSpec(memory_space=pltpu.VMEM)),
            input_output_aliases={i: 2 + i for i in range(k)},
            compiler_params=pltpu.CompilerParams(has_side_effects=EFFECT),
        )(*[pltpu.with_memory_space_constraint(a, pltpu.HBM) for a in arrays], *after)
        self.send, self.recv, self.arrays, self.token_array = out[0], out[1], list(out[2:2 + k]), out[-1]
        self.token = self.token_array[0, 0]

    def wait(self, after):
        k, copies = len(self.arrays), self.copies
        after = list(after) if isinstance(after, (list, tuple)) else [after]

        def body(*refs):
            for cp in copies(refs[:k], refs[k], refs[k + 1]):
                cp.wait_send()
                cp.wait_recv()

        return list(pl.pallas_call(
            body, name=self.tag + "_wait", out_shape=tuple(pltpu.HBM(a.shape, a.dtype) for a in self.arrays),
            in_specs=[HBM] * k + [SEM, SEM] + [ANY] * len(after), out_specs=tuple([HBM] * k),
            input_output_aliases={i: i for i in range(k)},
            compiler_params=pltpu.CompilerParams(has_side_effects=EFFECT),
        )(*self.arrays, self.send, self.recv, *after))


def _landing_zones(arrs):
    me = 2 * lax.axis_index("x") + lax.axis_index("y")
    return [lax.dynamic_update_index_in_dim(lax.empty((4,) + a.shape, a.dtype), a, me, 0) for a in arrs]


def _gather_start(arrs, lands, tag, after=()):
    n = len(arrs)

    def copies(refs, send, recv):
        x, y, c, chips = _place()
        return [pltpu.make_async_remote_copy(
            src_ref=refs[k], dst_ref=refs[n + k].at[2 * x + y], send_sem=send.at[3 * k + r],
            recv_sem=recv.at[3 * k + r], device_id=(px, py, c), device_id_type=MESH)
            for k in range(n) for r, (px, py) in enumerate(chips)]

    return _Split("gather_" + tag, list(arrs) + lands, copies, 3 * n, after)


def _gather_halves_start(arrs, tag):
    n = len(arrs)
    lands = _landing_zones(arrs)

    def copies(refs, send, recv):
        x, y, c, chips = _place()
        return [pltpu.make_async_remote_copy(
            src_ref=refs[k].at[c], dst_ref=refs[n + k].at[2 * x + y, c], send_sem=send.at[3 * k + r],
            recv_sem=recv.at[3 * k + r], device_id=(px, py, c), device_id_type=MESH)
            for k in range(n) for r, (px, py) in enumerate(chips)]

    return _Split("gather_" + tag, list(arrs) + lands, copies, 3 * n)


def _gather_halves_finish(lands, tag):
    n = len(lands)

    def copies(refs, send, recv):
        x, y, c, chips = _place()
        return [pltpu.make_async_remote_copy(
            src_ref=refs[k].at[2 * px + py, c], dst_ref=refs[k].at[2 * px + py, c], send_sem=send.at[3 * k + r],
            recv_sem=recv.at[3 * k + r], device_id=(x, y, 1 - c), device_id_type=MESH)
            for k in range(n) for r, (px, py) in enumerate(chips)]

    return _Split("gather_pass_" + tag, list(lands), copies, 3 * n)


def _part_sibling(gbufs):
    n = len(gbufs)

    def copies(refs, send, recv, off):
        x, y, c, _ = _place()
        return [pltpu.make_async_remote_copy(
            src_ref=refs[k].at[j, 1 - c], dst_ref=refs[n + k].at[j], send_sem=send.at[off + 4 * k + j],
            recv_sem=recv.at[off + 4 * k + j], device_id=(x, y, 1 - c), device_id_type=MESH)
            for k in range(n) for j in range(4)]

    return list(gbufs) + [lax.empty((4,) + g.shape[2:], g.dtype) for g in gbufs], 4 * n, copies


def _part_chips(pbufs):
    n = len(pbufs)

    def copies(refs, send, recv, off):
        x, y, c, chips = _place()
        return [pltpu.make_async_remote_copy(
            src_ref=refs[k].at[2 * px + py], dst_ref=refs[n + k].at[2 * x + y], send_sem=send.at[off + 3 * k + r],
            recv_sem=recv.at[off + 3 * k + r], device_id=(px, py, c), device_id_type=MESH)
            for k in range(n) for r, (px, py) in enumerate(chips)]

    return list(pbufs) + [lax.empty(p.shape, p.dtype) for p in pbufs], 3 * n, copies


def _part_join(fulls):
    def copies(refs, send, recv, off):
        x, y, c, _ = _place()
        return [pltpu.make_async_remote_copy(
            src_ref=refs[k].at[c], dst_ref=refs[k].at[c], send_sem=send.at[off + k], recv_sem=recv.at[off + k],
            device_id=(x, y, 1 - c), device_id_type=MESH) for k in range(len(fulls))]

    return list(fulls), len(fulls), copies


def _start_parts(parts, tag):
    arrays, spans, total = [], [], 0
    for arrs, n_copies, fn in parts:
        spans.append((len(arrays), len(arrs), total, fn))
        arrays += arrs
        total += n_copies

    def copies(refs, send, recv):
        return [cp for a0, na, off, fn in spans for cp in fn(refs[a0:a0 + na], send, recv, off)]

    op = _Split(tag, arrays, copies, total)
    op.spans = [(a0, na) for a0, na, _, _ in spans]
    return op


def _all_reduce_small(v):
    R, C = v.shape

    def body(v_ref, o_ref, g_ref, send, recv, loc):
        x, y, c, chips = _place()
        me, sibling = (x, y, c), (x, y, 1 - c)

        def rows(px, py, pc):
            return g_ref.at[4 * px + 2 * py + pc]

        def copy(k, block, to, src=None):
            return pltpu.make_async_remote_copy(
                src_ref=rows(*block) if src is None else src, dst_ref=rows(*block),
                send_sem=send.at[k], recv_sem=recv.at[k], device_id=to, device_id_type=MESH)

        mine = pltpu.make_async_copy(v_ref, rows(*me), loc)
        mine.start()
        first = [copy(0, me, sibling, src=v_ref)]
        first += [copy(1 + j, me, (*chip, c), src=v_ref) for j, chip in enumerate(chips)]
        for cp in first:
            cp.start()
        passed = [copy(4 + j, (*chip, c), sibling) for j, chip in enumerate(chips)]
        for j, chip in enumerate(chips):
            copy(1 + j, (*chip, c), me).wait_recv()
            passed[j].start()
        copy(0, sibling, me).wait_recv()
        for j, chip in enumerate(chips):
            copy(4 + j, (*chip, 1 - c), me).wait_recv()
        for cp in first + passed:
            cp.wait_send()
        mine.wait()
        acc = g_ref[0]
        for d in range(1, 8):
            acc = acc + g_ref[d]
        o_ref[...] = acc

    vm = pl.BlockSpec(memory_space=pltpu.VMEM)
    return pl.pallas_call(
        body, name="all_reduce_small", in_specs=[vm], out_specs=[vm, vm],
        out_shape=[jax.ShapeDtypeStruct((R, C), F32), jax.ShapeDtypeStruct((8, R, C), F32)],
        scratch_shapes=[pltpu.SemaphoreType.DMA((7,)), pltpu.SemaphoreType.DMA((7,)), pltpu.SemaphoreType.DMA],
    )(v)[0]


WEIGHTS = ['ffn1_norm', 'ffn1_w_gate', 'ffn1_w_up', 'ffn1_w_down', 'mix_norm', 'w_in', 'conv_w', 'conv_b', 'dt_bias',
           'a_log', 'd_skip', 'ssd_norm', 'sgu_ln_g', 'sgu_ln_b', 'sgu_w', 'sgu_b', 'w_out', 'ffn2_norm',
           'ffn2_w_gate', 'ffn2_w_up', 'ffn2_w_down', 'final_norm']
SHARDED = ['ffn1_w_gate', 'ffn1_w_up', 'ffn1_w_down', 'w_in', 'conv_w', 'w_out', 'ffn2_w_gate', 'ffn2_w_up',
           'ffn2_w_down']
SMALL = [n for n in WEIGHTS if n not in SHARDED]
GROUPS = [("ffn1", ["ffn1_w_gate", "ffn1_w_up", "ffn1_w_down"]), ("mix", ["w_in", "conv_w", "w_out"]),
          ("ffn2", ["ffn2_w_gate", "ffn2_w_up", "ffn2_w_down"])]
TRANSPOSED = ("ffn1_w_gate", "ffn1_w_up", "ffn2_w_gate", "ffn2_w_up")
DEPTH = 2


def _pack_w_in(w):
    return jnp.concatenate([w[..., 0:1152], w[..., 1536:2432], w[..., 1152:1536],
                            jnp.repeat(w[..., 2432:2438], HEAD, axis=-1), w[..., 2438:2950]], axis=-1)


def _unpack_w_in(dq, ds, du):
    return jnp.concatenate([dq, ds[:, 896:1280], ds[:, 0:896], ds[:, 1280::HEAD], du], axis=-1)


def _ffn_fwd(x, g, wg, wu, wd):
    xo, hb, S1, S2, A = _ffn_fwd_k(x, g, wg, wu, wd)
    return xo, (x, hb, S1, S2, A)


def _ffn_bwd(dxo, saved, g, wg, wu, wd):
    x, hb, S1, S2, A = saved
    dx, dg, dG, dU, dyb = _ffn_bwd_k1(dxo, x, g, S1, S2, wg, wu, wd)
    dwg, dwu, dwd = _ffn_bwd_k2(hb, dyb, A, dG, dU)
    return dx, dg, dwg, dwu, dwd


def _mix_fwd(x, P):
    hb, qkv, sin, uv = _mix_proj(x, P["mix_norm"], P["w_in"])
    y_att, lse = _attn_combine([_attn_fwd(qkv, d) for d in DILATIONS])
    y_ssd, hprev = _ssd_fwd(sin, *P["ssd"])
    y_sgu = _sgu_fwd(uv, *P["sgu"])
    ycat = jnp.concatenate([y_att, y_ssd, y_sgu], axis=1).astype(BF)
    return _mm_nn(ycat, P["w_out"], res=x), (x, hb, qkv, sin, uv, y_att, lse, hprev, ycat)


def _mix_bwd(dxo, saved, P):
    x, hb, qkv, sin, uv, y_att, lse, hprev, ycat = saved
    dy_att, dy_ssd, dy_sgu = _mix_bwd_dy(dxo, P["w_out"])
    dwout = _mm_tn(ycat, dxo)
    dqkv = _sum_branches([_attn_bwd(qkv, dy_att, y_att, lse, d) for d in DILATIONS])
    dsin, dcw, dcb, dvec = _ssd_bwd(sin, hprev, dy_ssd, *P["ssd"])
    duv, dsw, dsbias, dln = _sgu_bwd(uv, dy_sgu, *P["sgu"])
    dwin = _unpack_w_in(_mm_tn(hb, dqkv), _mm_tn(hb, dsin), _mm_tn(hb, duv))
    dx, dg = _mix_bwd_dx(dqkv, dsin, duv, P["w_in"], x, P["mix_norm"], dxo)
    grads = dict(
        mix_norm=dg[0], w_in=dwin, conv_w=dcw, conv_b=dcb[0], dt_bias=dvec[0, ::HEAD], a_log=dvec[1, ::HEAD],
        d_skip=jnp.sum(dvec[2].reshape(6, HEAD), axis=-1), ssd_norm=dvec[3], sgu_ln_g=dln[0], sgu_ln_b=dln[1],
        sgu_w=dsw, sgu_b=jnp.sum(dsbias.reshape(CHUNK, 4, HEAD), axis=-1).T, w_out=dwout)
    return dx, grads


def _halved(g):
    rows = g.size // g.shape[-1]
    return g.reshape(4, 2, rows // 8, g.shape[-1])


def kernel(x, ffn1_norm, ffn1_w_gate, ffn1_w_up, ffn1_w_down, mix_norm, w_in, conv_w, conv_b, dt_bias, a_log, d_skip, ssd_norm, sgu_ln_g, sgu_ln_b, sgu_w, sgu_b, w_out, ffn2_norm, ffn2_w_gate, ffn2_w_up, ffn2_w_down, final_norm, loss_target, m_ffn1_norm, m_ffn1_w_gate, m_ffn1_w_up, m_ffn1_w_down, m_mix_norm, m_w_in, m_conv_w, m_conv_b, m_dt_bias, m_a_log, m_d_skip, m_ssd_norm, m_sgu_ln_g, m_sgu_ln_b, m_sgu_w, m_sgu_b, m_w_out, m_ffn2_norm, m_ffn2_w_gate, m_ffn2_w_up, m_ffn2_w_down, m_final_norm, v_ffn1_norm, v_ffn1_w_gate, v_ffn1_w_up, v_ffn1_w_down, v_mix_norm, v_w_in, v_conv_w, v_conv_b, v_dt_bias, v_a_log, v_d_skip, v_ssd_norm, v_sgu_ln_g, v_sgu_ln_b, v_sgu_w, v_sgu_b, v_w_out, v_ffn2_norm, v_ffn2_w_gate, v_ffn2_w_up, v_ffn2_w_down, v_final_norm):
    given = dict(x=x, ffn1_norm=ffn1_norm, ffn1_w_gate=ffn1_w_gate, ffn1_w_up=ffn1_w_up, ffn1_w_down=ffn1_w_down, mix_norm=mix_norm, w_in=w_in, conv_w=conv_w, conv_b=conv_b, dt_bias=dt_bias, a_log=a_log, d_skip=d_skip, ssd_norm=ssd_norm, sgu_ln_g=sgu_ln_g, sgu_ln_b=sgu_ln_b, sgu_w=sgu_w, sgu_b=sgu_b, w_out=w_out, ffn2_norm=ffn2_norm, ffn2_w_gate=ffn2_w_gate, ffn2_w_up=ffn2_w_up, ffn2_w_down=ffn2_w_down, final_norm=final_norm, loss_target=loss_target, m_ffn1_norm=m_ffn1_norm, m_ffn1_w_gate=m_ffn1_w_gate, m_ffn1_w_up=m_ffn1_w_up, m_ffn1_w_down=m_ffn1_w_down, m_mix_norm=m_mix_norm, m_w_in=m_w_in, m_conv_w=m_conv_w, m_conv_b=m_conv_b, m_dt_bias=m_dt_bias, m_a_log=m_a_log, m_d_skip=m_d_skip, m_ssd_norm=m_ssd_norm, m_sgu_ln_g=m_sgu_ln_g, m_sgu_ln_b=m_sgu_ln_b, m_sgu_w=m_sgu_w, m_sgu_b=m_sgu_b, m_w_out=m_w_out, m_ffn2_norm=m_ffn2_norm, m_ffn2_w_gate=m_ffn2_w_gate, m_ffn2_w_up=m_ffn2_w_up, m_ffn2_w_down=m_ffn2_w_down, m_final_norm=m_final_norm, v_ffn1_norm=v_ffn1_norm, v_ffn1_w_gate=v_ffn1_w_gate, v_ffn1_w_up=v_ffn1_w_up, v_ffn1_w_down=v_ffn1_w_down, v_mix_norm=v_mix_norm, v_w_in=v_w_in, v_conv_w=v_conv_w, v_conv_b=v_conv_b, v_dt_bias=v_dt_bias, v_a_log=v_a_log, v_d_skip=v_d_skip, v_ssd_norm=v_ssd_norm, v_sgu_ln_g=v_sgu_ln_g, v_sgu_ln_b=v_sgu_ln_b, v_sgu_w=v_sgu_w, v_sgu_b=v_sgu_b, v_w_out=v_w_out, v_ffn2_norm=v_ffn2_norm, v_ffn2_w_gate=v_ffn2_w_gate, v_ffn2_w_up=v_ffn2_w_up, v_ffn2_w_down=v_ffn2_w_down, v_final_norm=v_final_norm)
    T = given["x"].shape[0] * given["x"].shape[1]
    D = given["x"].shape[2]
    x0 = given["x"].reshape(T, D)
    tgt = given["loss_target"].reshape(T, D)
    c = lax.axis_index("c")

    bf = {n: given[n].astype(BF) for n in SHARDED if n not in ("w_in", "conv_w")}
    bf["w_in"] = _pack_w_in(given["w_in"]).astype(BF)
    bf["conv_w"] = given["conv_w"]
    first_key = (0, GROUPS[0][0])
    first = [bf[n][0].reshape((2, bf[n].shape[1] // 2) + bf[n].shape[2:]) for n in GROUPS[0][1]]
    gathers = {first_key: _gather_halves_start(first, "l0_" + GROUPS[0][0])}
    later = {(i, gname): [bf[n][i] for n in names] for i in range(DEPTH) for gname, names in GROUPS
             if (i, gname) != first_key}
    zones = {key: _landing_zones(arrs) for key, arrs in later.items()}

    def gathered(i, gname, after):
        if (i, gname) != first_key:
            return gathers[(i, gname)].wait(after)[3:]
        got = gathers[first_key].wait([after] + [z for zs in zones.values() for z in zs])[3:]
        got = _gather_halves_finish(got, "l0_" + gname).wait(after)
        prev = got[0]
        for key, arrs in later.items():
            gathers[key] = _gather_start(arrs, zones[key], f"l{key[0]}_{key[1]}", after=[prev])
            prev = gathers[key].token_array
        return [z.reshape((4, 2 * z.shape[2]) + z.shape[3:]) for z in got]

    def mix_params(i, got):
        win = got[0].reshape(D, W_QKV + W_SSD + W_UV)
        rep = lambda v: jnp.repeat(v, HEAD)[None]
        ssd = (got[1].transpose(1, 0, 2).reshape(4, SSD_CONV_DIM), given["conv_b"][i][None],
               rep(given["dt_bias"][i]), rep(given["a_log"][i]), rep(given["d_skip"][i]), given["ssd_norm"][i][None])
        sgu = (given["sgu_ln_g"][i][None], given["sgu_ln_b"][i][None], given["sgu_w"][i],
               jnp.repeat(given["sgu_b"][i].T, HEAD, axis=1))
        return dict(mix_norm=given["mix_norm"][i][None], w_in=win, w_out=got[2].reshape(-1, D), ssd=ssd, sgu=sgu)

    x = x0
    tape = []
    for i in range(DEPTH):
        got = gathered(i, "ffn1", x)
        token = functools.reduce(lambda a, b: a + b, [g.token for g in gathers.values()]) if i == 0 else 0.0
        P = dict(ffn1=(given["ffn1_norm"][i][None] + token, *got))
        x, s1 = _ffn_fwd(x, *P["ffn1"])
        P.update(mix_params(i, gathered(i, "mix", x)))
        x, s2 = _mix_fwd(x, P)
        P["ffn2"] = (given["ffn2_norm"][i][None], *gathered(i, "ffn2", x))
        x, s3 = _ffn_fwd(x, *P["ffn2"])
        tape.append((P, s1, s2, s3))
    loss_part, dx, dgf = _final_loss(x, given["final_norm"][None], tgt)

    me = 2 * lax.axis_index("x") + lax.axis_index("y")
    jobs = []

    flight = dict(op=None, owners=[], ticks=0)

    def tick(after, begin=None):
        parts, owners = [], []
        if flight["op"] is not None:
            got = flight["op"].wait(after)
            for job, (a0, na) in zip(flight["owners"], flight["op"].spans):
                mine, k = got[a0:a0 + na], len(job["names"])
                if job["stage"] == 1:
                    parts.append(_part_chips([_pair_add(g, l, c) for g, l in zip(mine[:k], mine[k:])]))
                elif job["stage"] == 2:
                    parts.append(_part_join([_chip_sum(p, l, me, c) for p, l in zip(mine[:k], mine[k:])]))
                else:
                    job.update(stage=4, out=dict(zip(job["names"], mine)))
                    continue
                job["stage"] += 1
                owners.append(job)
        if begin is not None:
            i, gname, gd = begin
            names = [n for n in dict(GROUPS)[gname] if n != "conv_w"]
            jobs.append(dict(key=(i, gname), names=names, stage=1))
            parts.append(_part_sibling([_halved(gd[n]) for n in names]))
            owners.append(jobs[-1])
        flight.update(op=_start_parts(parts, f"rs_tick{flight['ticks']}") if parts else None, owners=owners,
                      ticks=flight["ticks"] + 1)
        return flight["op"].token if parts else 0.0

    grads = [dict() for _ in range(DEPTH)]
    tok = 0.0
    for i in reversed(range(DEPTH)):
        P, s1, s2, s3 = tape[i]
        g = grads[i]
        norm, wg, wu, wd = P["ffn2"]
        dx, dn2, g["ffn2_w_gate"], g["ffn2_w_up"], g["ffn2_w_down"] = _ffn_bwd(dx, s3, norm + tok, wg, wu, wd)
        tok = tick(dx, (i, "ffn2", g))
        dx, gm = _mix_bwd(dx, s2, {**P, "mix_norm": P["mix_norm"] + tok})
        g.update(gm)
        tok = tick(dx, (i, "mix", g))
        norm, wg, wu, wd = P["ffn1"]
        dx, dn1, g["ffn1_w_gate"], g["ffn1_w_up"], g["ffn1_w_down"] = _ffn_bwd(dx, s1, norm + tok, wg, wu, wd)
        tok = tick(dx, (i, "ffn1", g))
        g["ffn1_norm"], g["ffn2_norm"] = dn1[0], dn2[0]
    grad_x = dx.reshape(given["x"].shape)

    order = [n for n in SMALL if n != "final_norm"] + ["final_norm"]
    small = [jnp.stack([grads[i][n] for i in range(DEPTH)]) for n in order[:-1] + ["conv_w"]]
    small = small[:-1] + [dgf[0], small[-1], loss_part[0, 0:1]]
    n_small = sum(s.size for s in small)
    rows_small = -(-n_small // (128 * 8)) * 8

    def flat(arrs):
        fill = rows_small * 128 - sum(a.size for a in arrs)
        return jnp.concatenate([a.reshape(-1) for a in arrs] + [jnp.zeros((fill,), F32)]).reshape(rows_small, 128)

    gsmall = _all_reduce_small(flat(small)).reshape(-1)

    grad_w = {}
    off = 0
    for n in order:
        size = given[n].size
        grad_w[n] = gsmall[off:off + size].reshape(given[n].shape)
        off += size
    cw = gsmall[off:off + 2 * 4 * SSD_CONV_DIM].reshape(DEPTH, 4, SSD_CONV_DIM)
    grad_w["conv_w"] = lax.dynamic_slice_in_dim(cw, me * (SSD_CONV_DIM // 4), SSD_CONV_DIM // 4, axis=2)
    loss = gsmall[off + 2 * 4 * SSD_CONV_DIM]

    delta, new_m, new_v = {}, {}, {}
    shp = given["conv_w"].shape
    d, m2, v2 = _adamw(*[a.reshape(shp[0] * shp[1], shp[2])
                         for a in (given["conv_w"], grad_w["conv_w"], given["m_conv_w"], given["v_conv_w"])])
    delta["conv_w"], new_m["conv_w"], new_v["conv_w"] = d.reshape(shp), m2.reshape(shp), v2.reshape(shp)
    packed = [flat([given[pre + n] for n in order]) for pre in ("", "m_", "v_")]
    small_out = _adamw(packed[0], gsmall.reshape(rows_small, 128), packed[1], packed[2])
    outs = [o.reshape(-1) for o in small_out]
    off = 0
    for n in order:
        size = given[n].size
        for dst, o in zip((delta, new_m, new_v), outs):
            dst[n] = o[off:off + size].reshape(given[n].shape)
        off += size

    stepped, arrived = {}, {}

    def update_arrived(dep):
        out = None
        for job in jobs:
            if job["stage"] == 4 and not job.get("seen"):
                job["seen"] = True
                for n, full in job["out"].items():
                    view = (lambda a: jnp.swapaxes(a, 1, 2)) if n in TRANSPOSED else (lambda a: a)
                    arrived.setdefault(n, {})[job["key"][0]] = full.reshape(view(given[n]).shape[1:])
                    if len(arrived[n]) == DEPTH:
                        res = _adamw_pair(view(given[n]), arrived[n][0], arrived[n][1], view(given["m_" + n]),
                                          view(given["v_" + n]), dep)
                        stepped[n] = [view(r) for r in res]
                        out = res[0]
        return out

    after = small_out[0]
    while any(j["stage"] < 4 for j in jobs):
        done = update_arrived(jnp.zeros((8, 128), F32) + tok)
        after = after if done is None else done
        tok = tick(after)
    update_arrived(jnp.zeros((8, 128), F32) + tok)
    for n, (d, m2, v2, g) in stepped.items():
        delta[n], new_m[n], new_v[n], grad_w[n] = d, m2, v2, g

    return (loss, grad_x, *[grad_w[n] for n in WEIGHTS], *[delta[n] for n in WEIGHTS],
            *[new_m[n] for n in WEIGHTS], *[new_v[n] for n in WEIGHTS])
```

```python
import functools
import math

import jax
import jax.numpy as jnp
from jax import lax
from jax.experimental import pallas as pl
from jax.experimental.pallas import tpu as pltpu

F32 = jnp.float32
BF = jnp.bfloat16

RMS_EPS = 1e-6
LN_EPS = 1e-5
SEQ = 2048
CHUNK = 128
N_CHUNK = SEQ // CHUNK
ATT_W = 384
HEAD = 64
SSD_W = 384
SSD_CONV_DIM = 896
SSD_STATE = 128
SGU_W = 256
DILATIONS = (1, 4, 16)
W_QKV = 3 * ATT_W
W_SSD = SSD_CONV_DIM + SSD_W + SSD_W
W_UV = 2 * SGU_W
ADAM_LR = 0.001
ADAM_B1 = 0.9
ADAM_B2 = 0.999
ADAM_EPS = 1e-08
ADAM_WD = 0.01
ADAM_STEP = 10
NEG = -1e30
ATTN_BWD_VMEM = 48 * 2 ** 20
FFN_VMEM = 60 * 2 ** 20


def _dot(a, b):
    return jnp.dot(a, b, preferred_element_type=F32)


def _dot_nt(a, b):
    return lax.dot_general(a, b, (((1,), (1,)), ((), ())), preferred_element_type=F32)


def _dot_tn(a, b):
    return lax.dot_general(a, b, (((0,), (0,)), ((), ())), preferred_element_type=F32)


def _sigmoid(x):
    return 1.0 / (1.0 + jnp.exp(-x))


def _call(body, *, name, grid, in_specs, out_specs, out_shape, scratch=(), sem=None, vmem=None):
    return pl.pallas_call(
        body, name=name, grid=grid, in_specs=in_specs, out_specs=out_specs, out_shape=out_shape,
        scratch_shapes=list(scratch),
        compiler_params=pltpu.CompilerParams(dimension_semantics=sem, vmem_limit_bytes=vmem),
    )


def _tile(n, want):
    t = min(n, want)
    while n % t:
        t //= 2
    return t


def _final_loss(x, g, tgt):
    T, D = x.shape
    tm = _tile(T, 512)

    def body(x_ref, g_ref, t_ref, l_ref, dx_ref, dg_ref):
        @pl.when(pl.program_id(0) == 0)
        def _():
            dg_ref[...] = jnp.zeros_like(dg_ref)
            l_ref[...] = jnp.zeros_like(l_ref)

        xf = x_ref[...]
        gg = g_ref[...]
        r = lax.rsqrt(jnp.mean(xf * xf, axis=-1, keepdims=True) + RMS_EPS)
        xn = xf * r
        e = xn * gg - t_ref[...]
        part = 0.5 * jnp.sum(jnp.mean(e * e, axis=-1, keepdims=True), axis=0, keepdims=True)
        l_ref[...] += jnp.broadcast_to(part, l_ref.shape)
        dy = e * (1.0 / D)
        u = dy * gg
        mu = jnp.mean(u * xf, axis=-1, keepdims=True)
        dx_ref[...] = r * (u - xf * (r * r * mu))
        dg_ref[...] += jnp.sum(dy * xn, axis=0, keepdims=True)

    row = pl.BlockSpec((tm, D), lambda i: (i, 0))
    vec = pl.BlockSpec((1, D), lambda i: (0, 0))
    lsp = pl.BlockSpec((1, 128), lambda i: (0, 0))
    return _call(body, name="final_loss", grid=(T // tm,), in_specs=[row, vec, row], out_specs=[lsp, row, vec],
                 out_shape=[jax.ShapeDtypeStruct((1, 128), F32), jax.ShapeDtypeStruct((T, D), F32),
                            jax.ShapeDtypeStruct((1, D), F32)],
                 sem=("arbitrary",))(x, g, tgt)


def _resident(shape):
    return pl.BlockSpec(shape, lambda *_: (0,) * len(shape), pipeline_mode=pl.Buffered(1))


def _ffn_fwd_k(x, gn, wg, wu, wd):
    T, D = x.shape
    NS, _, Fs = wg.shape
    tm = _tile(T, 1024)

    def body(x_ref, gn_ref, wg_ref, wu_ref, wd_ref, o_ref, h_ref, s1_ref, s2_ref, a_ref, hs, acc):
        j = pl.program_id(1)

        @pl.when(j == 0)
        def _():
            xf = x_ref[...]
            r = lax.rsqrt(jnp.mean(xf * xf, axis=-1, keepdims=True) + RMS_EPS)
            hs[...] = (xf * r * gn_ref[...]).astype(BF)
            h_ref[...] = hs[...]
            acc[...] = jnp.zeros_like(acc)

        h = hs[...]
        g = _dot(h, wg_ref[...])
        u = _dot(h, wu_ref[...])
        sg = _sigmoid(g)
        s1 = g * sg
        a = (s1 * u).astype(BF)
        s1_ref[...] = s1.astype(BF)
        s2_ref[...] = (u * (sg * (1.0 + g * (1.0 - sg)))).astype(BF)
        a_ref[...] = a
        acc[...] += _dot(a, wd_ref[...])

        @pl.when(j == NS - 1)
        def _():
            o_ref[...] = x_ref[...] + 0.5 * acc[...]

    row = pl.BlockSpec((tm, D), lambda i, j: (i, 0))
    act = pl.BlockSpec((None, tm, Fs), lambda i, j: (j, i, 0))
    sh = jax.ShapeDtypeStruct((NS, T, Fs), BF)
    wspec = lambda w: pl.BlockSpec((None,) + w.shape[1:], lambda i, j: (j, 0, 0))
    return _call(body, name="ffn_fwd", grid=(T // tm, NS),
                 in_specs=[row, pl.BlockSpec((1, D), lambda i, j: (0, 0)), wspec(wg), wspec(wu), wspec(wd)],
                 out_specs=[row, row, act, act, act],
                 out_shape=[jax.ShapeDtypeStruct((T, D), F32), jax.ShapeDtypeStruct((T, D), BF), sh, sh, sh],
                 scratch=[pltpu.VMEM((tm, D), BF), pltpu.VMEM((tm, D), F32)],
                 sem=("parallel", "arbitrary"), vmem=FFN_VMEM)(x, gn, wg, wu, wd)


def _ffn_bwd_act(dxo, s1, s2, wd):
    NS, T, Fs = s1.shape
    D = dxo.shape[1]
    tm = _tile(T, 1024)

    def body(dxo_ref, s1_ref, s2_ref, wd_ref, dg_ref, du_ref, dy_ref, dys):
        j = pl.program_id(1)

        @pl.when(j == 0)
        def _():
            dys[...] = (0.5 * dxo_ref[...]).astype(BF)
            dy_ref[...] = dys[...]

        da = _dot_nt(dys[...], wd_ref[j])
        dg_ref[...] = (da * s2_ref[...].astype(F32)).astype(BF)
        du_ref[...] = (da * s1_ref[...].astype(F32)).astype(BF)

    row = pl.BlockSpec((tm, D), lambda i, j: (i, 0))
    act = pl.BlockSpec((None, tm, Fs), lambda i, j: (j, i, 0))
    sh = jax.ShapeDtypeStruct((NS, T, Fs), BF)
    return _call(body, name="ffn_bwd_act", grid=(T // tm, NS), in_specs=[row, act, act, _resident(wd.shape)],
                 out_specs=[act, act, row], out_shape=[sh, sh, jax.ShapeDtypeStruct((T, D), BF)],
                 scratch=[pltpu.VMEM((tm, D), BF)], sem=("parallel", "arbitrary"))(dxo, s1, s2, wd)


def _ffn_bwd_dx(dg, du, wg, wu, x, gn, dxo):
    NS, T, Fs = dg.shape
    D = x.shape[1]
    tm = _tile(T, 1024)

    def body(dg_ref, du_ref, wg_ref, wu_ref, x_ref, gn_ref, dxo_ref, dx_ref, dgn_ref, acc):
        i, j = pl.program_id(0), pl.program_id(1)

        @pl.when((i == 0) & (j == 0))
        def _():
            dgn_ref[...] = jnp.zeros_like(dgn_ref)

        @pl.when(j == 0)
        def _():
            acc[...] = jnp.zeros_like(acc)

        acc[...] += _dot_nt(dg_ref[...], wg_ref[j]) + _dot_nt(du_ref[...], wu_ref[j])

        @pl.when(j == NS - 1)
        def _():
            xf = x_ref[...]
            r = lax.rsqrt(jnp.mean(xf * xf, axis=-1, keepdims=True) + RMS_EPS)
            dh = acc[...]
            uu = dh * gn_ref[...]
            mu = jnp.mean(uu * xf, axis=-1, keepdims=True)
            dx_ref[...] = dxo_ref[...] + r * (uu - xf * (r * r * mu))
            dgn_ref[...] += jnp.sum(dh * xf * r, axis=0, keepdims=True)

    row = pl.BlockSpec((tm, D), lambda i, j: (i, 0))
    vec = pl.BlockSpec((1, D), lambda i, j: (0, 0))
    act = pl.BlockSpec((None, tm, Fs), lambda i, j: (j, i, 0))
    return _call(body, name="ffn_bwd_dx", grid=(T // tm, NS),
                 in_specs=[act, act, _resident(wg.shape), _resident(wu.shape), row, vec, row], out_specs=[row, vec],
                 out_shape=[jax.ShapeDtypeStruct((T, D), F32), jax.ShapeDtypeStruct((1, D), F32)],
                 scratch=[pltpu.VMEM((tm, D), F32)], sem=("arbitrary", "arbitrary"), vmem=FFN_VMEM)(
        dg, du, wg, wu, x, gn, dxo)


def _ffn_bwd_k2(hb, dyb, a, dg, du):
    NS, T, Fs = a.shape
    D = hb.shape[1]
    tk = _tile(T, 1024)

    def body(h_ref, dy_ref, a_ref, dg_ref, du_ref, og_ref, ou_ref, od_ref):
        @pl.when(pl.program_id(1) == 0)
        def _():
            og_ref[...] = jnp.zeros_like(og_ref)
            ou_ref[...] = jnp.zeros_like(ou_ref)
            od_ref[...] = jnp.zeros_like(od_ref)

        h = h_ref[...]
        og_ref[...] += _dot_tn(dg_ref[...], h)
        ou_ref[...] += _dot_tn(du_ref[...], h)
        od_ref[...] += _dot_tn(a_ref[...], dy_ref[...])

    row = pl.BlockSpec((tk, D), lambda j, k: (k, 0))
    act = pl.BlockSpec((None, tk, Fs), lambda j, k: (j, k, 0))
    return _call(body, name="ffn_bwd_w", grid=(NS, T // tk), in_specs=[row, row, act, act, act],
                 out_specs=[pl.BlockSpec((None, Fs, D), lambda j, k: (j, 0, 0))] * 3,
                 out_shape=[jax.ShapeDtypeStruct((NS, Fs, D), F32)] * 3,
                 sem=("parallel", "arbitrary"))(hb, dyb, a, dg, du)


def _mm_nn(a, b, res=None, out_dtype=F32):
    T, K = a.shape
    N = b.shape[1]
    tm = _tile(T, 512)
    tn = N if N <= 2048 else _tile(N, 1024)

    def body(*refs):
        if res is None:
            a_ref, b_ref, o_ref = refs
            o_ref[...] = _dot(a_ref[...], b_ref[...]).astype(out_dtype)
        else:
            a_ref, b_ref, r_ref, o_ref = refs
            o_ref[...] = (r_ref[...] + _dot(a_ref[...], b_ref[...])).astype(out_dtype)

    o = pl.BlockSpec((tm, tn), lambda i, j: (i, j))
    ins = [pl.BlockSpec((tm, K), lambda i, j: (i, 0)), pl.BlockSpec((K, tn), lambda i, j: (0, j))]
    args = [a, b]
    if res is not None:
        ins.append(o)
        args.append(res)
    return _call(body, name="mm_nn", grid=(T // tm, N // tn), in_specs=ins, out_specs=o,
                 out_shape=jax.ShapeDtypeStruct((T, N), out_dtype), sem=("parallel", "parallel"))(*args)


def _mix_bwd_dy(dxo, w_out):
    T, D = dxo.shape
    tm = _tile(T, 512)
    cuts = (0, ATT_W, ATT_W + SSD_W, ATT_W + SSD_W + SGU_W)

    def body(dx_ref, w_ref, a_ref, s_ref, g_ref):
        d = _dot_nt(dx_ref[...].astype(BF), w_ref[...])
        for o_ref, lo, hi in zip((a_ref, s_ref, g_ref), cuts[:-1], cuts[1:]):
            o_ref[...] = d[:, lo:hi]

    row = lambda w: pl.BlockSpec((tm, w), lambda i: (i, 0))
    return _call(body, name="mix_bwd_dy", grid=(T // tm,), in_specs=[row(D), _resident(w_out.shape)],
                 out_specs=[row(ATT_W), row(SSD_W), row(SGU_W)],
                 out_shape=[jax.ShapeDtypeStruct((T, w), F32) for w in (ATT_W, SSD_W, SGU_W)],
                 sem=("parallel",))(dxo, w_out)


def _mm_tn(a, b):
    T, M = a.shape
    N = b.shape[1]
    tk = _tile(T, 1024)
    tmm = _tile(M, 512)

    def body(a_ref, b_ref, o_ref):
        @pl.when(pl.program_id(1) == 0)
        def _():
            o_ref[...] = jnp.zeros_like(o_ref)

        o_ref[...] += _dot_tn(a_ref[...].astype(BF), b_ref[...].astype(BF))

    return _call(body, name="mm_tn", grid=(M // tmm, T // tk),
                 in_specs=[pl.BlockSpec((tk, tmm), lambda i, k: (k, i)), pl.BlockSpec((tk, N), lambda i, k: (k, 0))],
                 out_specs=pl.BlockSpec((tmm, N), lambda i, k: (i, 0)),
                 out_shape=jax.ShapeDtypeStruct((M, N), F32), sem=("parallel", "arbitrary"))(a, b)


def _mix_proj(x, gn, win):
    T, D = x.shape
    tm = _tile(T, 512)
    cuts = (0, W_QKV, W_QKV + W_SSD, W_QKV + W_SSD + W_UV)

    def body(x_ref, gn_ref, w_ref, h_ref, q_ref, s_ref, u_ref):
        xf = x_ref[...]
        r = lax.rsqrt(jnp.mean(xf * xf, axis=-1, keepdims=True) + RMS_EPS)
        h = (xf * r * gn_ref[...]).astype(BF)
        h_ref[...] = h
        for o_ref, lo, hi in zip((q_ref, s_ref, u_ref), cuts[:-1], cuts[1:]):
            o_ref[...] = _dot(h, w_ref[:, lo:hi])

    row = lambda w: pl.BlockSpec((tm, w), lambda i: (i, 0))
    return _call(body, name="mix_proj", grid=(T // tm,),
                 in_specs=[row(D), pl.BlockSpec((1, D), lambda i: (0, 0)), _resident(win.shape)],
                 out_specs=[row(D), row(W_QKV), row(W_SSD), row(W_UV)],
                 out_shape=[jax.ShapeDtypeStruct((T, D), BF), jax.ShapeDtypeStruct((T, W_QKV), F32),
                            jax.ShapeDtypeStruct((T, W_SSD), F32), jax.ShapeDtypeStruct((T, W_UV), F32)],
                 sem=("parallel",))(x, gn, win)


def _mix_bwd_dx(dqkv, dsin, duv, win, x, gn, dxo):
    T, D = x.shape
    tm = _tile(T, 512)
    cuts = (0, W_QKV, W_QKV + W_SSD, W_QKV + W_SSD + W_UV)

    def body(dq_ref, ds_ref, du_ref, w_ref, x_ref, gn_ref, dxo_ref, dx_ref, dgn_ref):
        @pl.when(pl.program_id(0) == 0)
        def _():
            dgn_ref[...] = jnp.zeros_like(dgn_ref)

        dh = (_dot_nt(dq_ref[...], w_ref[:, cuts[0]:cuts[1]]) + _dot_nt(ds_ref[...], w_ref[:, cuts[1]:cuts[2]])
              + _dot_nt(du_ref[...], w_ref[:, cuts[2]:cuts[3]]))
        xf = x_ref[...]
        r = lax.rsqrt(jnp.mean(xf * xf, axis=-1, keepdims=True) + RMS_EPS)
        uu = dh * gn_ref[...]
        mu = jnp.mean(uu * xf, axis=-1, keepdims=True)
        dx_ref[...] = dxo_ref[...] + r * (uu - xf * (r * r * mu))
        dgn_ref[...] += jnp.sum(dh * xf * r, axis=0, keepdims=True)

    row = lambda w: pl.BlockSpec((tm, w), lambda i: (i, 0))
    vec = pl.BlockSpec((1, D), lambda i: (0, 0))
    return _call(body, name="mix_bwd_dx", grid=(T // tm,),
                 in_specs=[row(W_QKV), row(W_SSD), row(W_UV), _resident(win.shape), row(D), vec, row(D)],
                 out_specs=[row(D), vec],
                 out_shape=[jax.ShapeDtypeStruct((T, D), F32), jax.ShapeDtypeStruct((1, D), F32)],
                 sem=("arbitrary",))(dqkv, dsin, duv, win, x, gn, dxo)


def _lane_mask(e, width=128):
    return (lax.broadcasted_iota(jnp.int32, (1, width), 1) // HEAD) == e


def _band_mask(n):
    qi = lax.broadcasted_iota(jnp.int32, (CHUNK, 2 * CHUNK), 0)
    kj = lax.broadcasted_iota(jnp.int32, (CHUNK, 2 * CHUNK), 1)
    dist = qi + CHUNK - kj
    return (dist >= 0) & (dist <= CHUNK) & ((kj >= CHUNK) | (n > 0))


def _sub_rows(r, block, dil):
    if dil == 1:
        return pl.ds(pl.multiple_of(block * CHUNK, CHUNK), CHUNK)
    return pl.ds(r + dil * CHUNK * block, CHUNK, stride=dil)


def _attn_specs(T, dil):
    B, nb = T // SEQ, SEQ // (CHUNK * dil)
    once = dict(pipeline_mode=pl.Buffered(1))
    q_like = lambda col: pl.BlockSpec((CHUNK * dil, 128), lambda b, n, r: (b * nb + n, col), **(once if nb == 1 else {}))
    k_like = lambda col: pl.BlockSpec((SEQ, 128), lambda b, n, r: (b, col), **once)
    return B, nb, q_like, k_like


def _attn_fwd(qkv, dil):
    T = qkv.shape[0]
    B, nb, q_like, k_like = _attn_specs(T, dil)
    scale = HEAD ** -0.5

    def body(*refs):
        q_t, k_t, v_t, o_t, l_t = refs[0:3], refs[3:6], refs[6:9], refs[9:12], refs[12:15]
        n, r = pl.program_id(1), pl.program_id(2)
        mine = _sub_rows(r, 0, dil)
        cur, prv = _sub_rows(r, n, dil), _sub_rows(r, jnp.maximum(n - 1, 0), dil)
        mask = _band_mask(n)
        for t in range(3):
            qt = q_t[t][mine, :].astype(BF)
            kt = jnp.concatenate([k_t[t][prv, :], k_t[t][cur, :]], axis=0).astype(BF)
            vt = jnp.concatenate([v_t[t][prv, :], v_t[t][cur, :]], axis=0).astype(BF)
            o_pair = jnp.zeros((CHUNK, 128), F32)
            l_pair = jnp.zeros((CHUNK, 128), F32)
            for e in range(2):
                lm = _lane_mask(e)
                s = _dot_nt(jnp.where(lm, qt, jnp.zeros_like(qt)), kt) * scale
                s = jnp.where(mask, s, NEG)
                m = jnp.max(s, axis=-1, keepdims=True)
                p = jnp.exp(s - m)
                den = jnp.sum(p, axis=-1, keepdims=True)
                o = _dot(p.astype(BF), vt) / den
                o_pair = jnp.where(lm, o, o_pair)
                l_pair = jnp.where(lm, m + jnp.log(den), l_pair)
            o_t[t][mine, :] = o_pair
            l_t[t][mine, :] = l_pair

    out_spec = pl.BlockSpec((CHUNK * dil, 128), lambda b, n, r: (b * nb + n, 0))
    sh = jax.ShapeDtypeStruct((T, 128), F32)
    outs = _call(
        body, name=f"attn_fwd_d{dil}", grid=(B, nb, dil),
        in_specs=[q_like(t) for t in range(3)] + [k_like(3 + t) for t in range(3)] + [k_like(6 + t) for t in range(3)],
        out_specs=[out_spec] * 6, out_shape=[sh] * 6, sem=("parallel", "arbitrary", "arbitrary"))(*([qkv] * 9))
    return list(outs[0:3]), list(outs[3:6])


def _attn_combine(branches):
    T = branches[0][0][0].shape[0]
    tm = _tile(T, 512)

    def body(*refs):
        y_ref, l_ref = refs[-2:]
        for t in range(3):
            o = [refs[6 * i + t][...] for i in range(3)]
            a, b, c = [refs[6 * i + 3 + t][...] for i in range(3)]
            m = jnp.maximum(jnp.maximum(a, b), c)
            ea, eb, ec = jnp.exp(a - m), jnp.exp(b - m), jnp.exp(c - m)
            z = ea + eb + ec
            y_ref[:, 128 * t:128 * (t + 1)] = (ea * o[0] + eb * o[1] + ec * o[2]) / z
            l_ref[:, 128 * t:128 * (t + 1)] = m + jnp.log(z)

    tile = pl.BlockSpec((tm, 128), lambda i: (i, 0))
    row = pl.BlockSpec((tm, ATT_W), lambda i: (i, 0))
    sh = jax.ShapeDtypeStruct((T, ATT_W), F32)
    flat = [a for o_t, l_t in branches for a in (*o_t, *l_t)]
    return _call(body, name="attn_combine", grid=(T // tm,), in_specs=[tile] * 18, out_specs=[row, row],
                 out_shape=[sh, sh], sem=("parallel",))(*flat)


def _attn_bwd(qkv, do, out, lse, dil):
    T = qkv.shape[0]
    B, nb, q_like, k_like = _attn_specs(T, dil)
    scale = HEAD ** -0.5

    def body(*refs):
        q_t, k_t, v_t = refs[0:3], refs[3:6], refs[6:9]
        do_t, out_t, lse_t = refs[9:12], refs[12:15], refs[15:18]
        dq_t, dk_t, dv_t = refs[18:21], refs[21:24], refs[24:27]
        n, r = pl.program_id(1), pl.program_id(2)

        @pl.when((n == 0) & (r == 0))
        def _():
            for t in range(3):
                dk_t[t][...] = jnp.zeros_like(dk_t[t])
                dv_t[t][...] = jnp.zeros_like(dv_t[t])

        mine = _sub_rows(r, 0, dil)
        cur, prv = _sub_rows(r, n, dil), _sub_rows(r, jnp.maximum(n - 1, 0), dil)
        mask = _band_mask(n)
        for t in range(3):
            qt = q_t[t][mine, :].astype(BF)
            kt = jnp.concatenate([k_t[t][prv, :], k_t[t][cur, :]], axis=0).astype(BF)
            vt = jnp.concatenate([v_t[t][prv, :], v_t[t][cur, :]], axis=0).astype(BF)
            do_ = do_t[t][mine, :]
            dlt = do_ * out_t[t][mine, :]
            ls = lse_t[t][mine, :]
            dq_pair = jnp.zeros((CHUNK, 128), F32)
            dk_acc = jnp.zeros((2 * CHUNK, 128), F32)
            dv_acc = jnp.zeros((2 * CHUNK, 128), F32)
            for e in range(2):
                lm = _lane_mask(e)
                qm = jnp.where(lm, qt, jnp.zeros_like(qt))
                s = _dot_nt(qm, kt) * scale
                p = jnp.exp(jnp.where(mask, s - ls[:, HEAD * e:HEAD * e + 1], NEG))
                dom = jnp.where(lm, do_, 0.0).astype(BF)
                dv_acc += _dot_tn(p.astype(BF), dom)
                dp = _dot_nt(dom, vt)
                delta = jnp.sum(jnp.where(lm, dlt, 0.0), axis=-1, keepdims=True)
                ds = (p * (dp - delta) * scale).astype(BF)
                dq_pair += jnp.where(lm, _dot(ds, kt), 0.0)
                dk_acc += _dot_tn(ds, qm)
            dq_t[t][mine, :] = dq_pair
            dk_t[t][cur, :] = dk_t[t][cur, :] + dk_acc[CHUNK:]
            dk_t[t][prv, :] = dk_t[t][prv, :] + dk_acc[:CHUNK]
            dv_t[t][cur, :] = dv_t[t][cur, :] + dv_acc[CHUNK:]
            dv_t[t][prv, :] = dv_t[t][prv, :] + dv_acc[:CHUNK]

    q_out = pl.BlockSpec((CHUNK * dil, 128), lambda b, n, r: (b * nb + n, 0))
    k_out = pl.BlockSpec((SEQ, 128), lambda b, n, r: (b, 0))
    sh = jax.ShapeDtypeStruct((T, 128), F32)
    tiles = lambda: [q_like(t) for t in range(3)]
    return list(_call(
        body, name=f"attn_bwd_d{dil}", grid=(B, nb, dil),
        in_specs=tiles() + [k_like(3 + t) for t in range(3)] + [k_like(6 + t) for t in range(3)]
        + tiles() + tiles() + tiles(),
        out_specs=[q_out] * 3 + [k_out] * 6, out_shape=[sh] * 9,
        sem=("parallel", "arbitrary", "arbitrary"), vmem=ATTN_BWD_VMEM)(*([qkv] * 9 + [do] * 3 + [out] * 3 + [lse] * 3)))


def _sum_branches(parts):
    T = parts[0][0].shape[0]
    tm = _tile(T, 512)

    def body(*refs):
        o_ref = refs[-1]
        for c in range(9):
            acc = refs[c][...] + refs[9 + c][...] + refs[18 + c][...]
            o_ref[:, 128 * c:128 * (c + 1)] = acc.astype(BF)

    tile = pl.BlockSpec((tm, 128), lambda i: (i, 0))
    flat = [a for br in parts for a in br]
    return _call(body, name="attn_sum_branches", grid=(T // tm,), in_specs=[tile] * 27,
                 out_specs=pl.BlockSpec((tm, W_QKV), lambda i: (i, 0)),
                 out_shape=jax.ShapeDtypeStruct((T, W_QKV), BF), sem=("parallel",))(*flat)


def _silu(x):
    return x * _sigmoid(x)


def _dsilu(x):
    s = _sigmoid(x)
    return s * (1.0 + x * (1.0 - s))


def _log1p(u):
    return jnp.where(u < 0.01, u * (1.0 - u * (0.5 - u * (1.0 / 3.0))), jnp.log(1.0 + u))


def _softplus(x):
    return jnp.maximum(x, 0.0) + _log1p(jnp.exp(-jnp.abs(x)))


def _cumsum_rows(x, reverse=False):
    n = x.shape[0]
    rows = lax.broadcasted_iota(jnp.int32, x.shape, 0)
    k = 1
    while k < n:
        if reverse:
            x = x + jnp.where(rows < n - k, pltpu.roll(x, n - k, 0), 0.0)
        else:
            x = x + jnp.where(rows >= k, pltpu.roll(x, k, 0), 0.0)
        k *= 2
    return x


def _tri():
    r = lax.broadcasted_iota(jnp.int32, (CHUNK, CHUNK), 0)
    c = lax.broadcasted_iota(jnp.int32, (CHUNK, CHUNK), 1)
    return r >= c


def _row_mask(e):
    return (lax.broadcasted_iota(jnp.int32, (128, 1), 0) // HEAD) == e


def _first_lane(e):
    return lax.broadcasted_iota(jnp.int32, (1, 128), 1) == HEAD * e


def _ssd_pre(x_ref, halo_ref, first, cw_ref, cb_ref, dtb_ref, al_ref, ext):
    row = x_ref[...]
    z = row[:, SSD_CONV_DIM:SSD_CONV_DIM + SSD_W]
    u = row[:, SSD_CONV_DIM + SSD_W:] + dtb_ref[...]
    ext[0:8, :] = jnp.where(first, 0.0, halo_ref[:, 0:SSD_CONV_DIM])
    ext[8:8 + CHUNK, :] = row[:, 0:SSD_CONV_DIM]
    xc = cb_ref[...]
    for j in range(4):
        xc = xc + cw_ref[j:j + 1, :] * ext[pl.ds(5 + j, CHUNK), :]
    xa = _silu(xc)
    dt = _softplus(u)
    a = dt * (-jnp.exp(al_ref[...]))
    A = _cumsum_rows(a)
    return dict(z=z, u=u, xc=xc, xs=xa[:, 0:SSD_W], Bm=xa[:, SSD_W:SSD_W + 256], Cm=xa[:, SSD_W + 256:],
                dt=dt, a=a, A=A, AT=A.T, eA=jnp.exp(A), wdec=jnp.exp(A[CHUNK - 1:CHUNK, :] - A),
                dtot=jnp.exp(A[CHUNK - 1:CHUNK, :]))


def _ssd_y(p, hp_ref, dskip):
    tri = _tri()
    X = p["xs"] * p["dt"]
    Bb = [p["Bm"][:, 128 * g:128 * (g + 1)].astype(BF) for g in range(2)]
    Cb = [p["Cm"][:, 128 * g:128 * (g + 1)].astype(BF) for g in range(2)]
    CB = [_dot_nt(Cb[g], Bb[g]) for g in range(2)]
    tiles = []
    for t in range(3):
        sl = slice(128 * t, 128 * (t + 1))
        hpb = hp_ref[sl, :].astype(BF)
        acc = jnp.zeros((CHUNK, 128), F32)
        for e in range(2):
            h = 2 * t + e
            g, col = h // 3, HEAD * h
            lm = _lane_mask(e)
            L = jnp.exp(jnp.where(tri, p["A"][:, col:col + 1] - p["AT"][col:col + 1, :], NEG))
            yd = _dot((CB[g] * L).astype(BF), jnp.where(lm, X[:, sl], 0.0).astype(BF))
            yo = _dot_nt(Cb[g], hpb) * p["eA"][:, sl]
            acc = acc + jnp.where(lm, yd + yo, 0.0)
        tiles.append(acc)
    return jnp.concatenate(tiles, axis=1) + dskip * p["xs"], X, Bb, Cb, CB


def _group_stats(v):
    g0 = lax.broadcasted_iota(jnp.int32, (1, SSD_W), 1) < SSD_W // 2
    m0 = jnp.sum(jnp.where(g0, v, 0.0), axis=-1, keepdims=True) * (2.0 / SSD_W)
    m1 = jnp.sum(jnp.where(g0, 0.0, v), axis=-1, keepdims=True) * (2.0 / SSD_W)
    return jnp.where(g0, m0, m1)


def _ssd_specs(T, rev):
    B = T // SEQ

    def chunk(b, c):
        return b * N_CHUNK + (N_CHUNK - 1 - c if rev else c)

    row = pl.BlockSpec((CHUNK, W_SSD), lambda b, c: (chunk(b, c), 0))
    halo = pl.BlockSpec((8, W_SSD), lambda b, c: (jnp.maximum(chunk(b, c) * (CHUNK // 8) - 1, 0), 0))
    hp = pl.BlockSpec((None, SSD_W, SSD_STATE), lambda b, c: (chunk(b, c), 0, 0))
    y = pl.BlockSpec((CHUNK, SSD_W), lambda b, c: (chunk(b, c), 0))
    const = lambda r, w: pl.BlockSpec((r, w), lambda b, c: (0, 0))
    params = [const(4, SSD_CONV_DIM), const(1, SSD_CONV_DIM)] + [const(1, SSD_W)] * 4
    return B, row, halo, hp, y, const, params


def _ssd_fwd(sin, conv_w, conv_b, dtb, alog, dskip, norm_g):
    T = sin.shape[0]
    B, row, halo, hp, y, const, params = _ssd_specs(T, False)

    def body(x_ref, halo_ref, cw_ref, cb_ref, dtb_ref, al_ref, dk_ref, ng_ref, y_ref, hp_ref, ext, hst):
        c = pl.program_id(1)

        @pl.when(c == 0)
        def _():
            hst[...] = jnp.zeros_like(hst)

        p = _ssd_pre(x_ref, halo_ref, c == 0, cw_ref, cb_ref, dtb_ref, al_ref, ext)
        yv, X, Bb, Cb, CB = _ssd_y(p, hst, dk_ref[...])
        hp_ref[...] = hst[...]
        for t in range(3):
            sl = slice(128 * t, 128 * (t + 1))
            old = hst[sl, :]
            new = old
            for e in range(2):
                h = 2 * t + e
                g, col = h // 3, HEAD * h
                st = _dot_tn(jnp.where(_lane_mask(e), X[:, sl] * p["wdec"][:, sl], 0.0).astype(BF), Bb[g])
                new = jnp.where(_row_mask(e), old * p["dtot"][:, col:col + 1] + st, new)
            hst[sl, :] = new
        y2 = yv * _silu(p["z"])
        r = lax.rsqrt(_group_stats(y2 * y2) + RMS_EPS)
        y_ref[...] = y2 * r * ng_ref[...]

    return _call(body, name="ssd_fwd", grid=(B, N_CHUNK), in_specs=[row, halo] + params, out_specs=[y, hp],
                 out_shape=[jax.ShapeDtypeStruct((T, SSD_W), F32),
                            jax.ShapeDtypeStruct((T // CHUNK, SSD_W, SSD_STATE), F32)],
                 scratch=[pltpu.VMEM((8 + CHUNK, SSD_CONV_DIM), F32), pltpu.VMEM((SSD_W, SSD_STATE), F32)],
                 sem=("parallel", "arbitrary"))(sin, sin, conv_w, conv_b, dtb, alog, dskip, norm_g)


def _ssd_bwd(sin, hprev, dy3, conv_w, conv_b, dtb, alog, dskip, norm_g):
    T = sin.shape[0]
    B, row, halo, hp, y, const, params = _ssd_specs(T, True)

    def body(x_ref, halo_ref, hp_ref, dy_ref, cw_ref, cb_ref, dtb_ref, al_ref, dk_ref, ng_ref,
             dx_ref, dcw_ref, dcb_ref, dvec_ref, ext, ext2, dh):
        c = pl.program_id(1)

        @pl.when((pl.program_id(0) == 0) & (c == 0))
        def _():
            dcw_ref[...] = jnp.zeros_like(dcw_ref)
            dcb_ref[...] = jnp.zeros_like(dcb_ref)
            dvec_ref[...] = jnp.zeros_like(dvec_ref)

        @pl.when(c == 0)
        def _():
            dh[...] = jnp.zeros_like(dh)
            ext2[CHUNK:CHUNK + 8, :] = jnp.zeros((8, SSD_CONV_DIM), F32)

        p = _ssd_pre(x_ref, halo_ref, c == N_CHUNK - 1, cw_ref, cb_ref, dtb_ref, al_ref, ext)
        dskip_ = dk_ref[...]
        yv, X, Bb, Cb, CB = _ssd_y(p, hp_ref, dskip_)
        xs, z, A, AT = p["xs"], p["z"], p["A"], p["AT"]

        sz = _silu(z)
        y2 = yv * sz
        r = lax.rsqrt(_group_stats(y2 * y2) + RMS_EPS)
        dy3_ = dy_ref[...]
        uu = dy3_ * ng_ref[...]
        dy2 = r * (uu - y2 * (r * r * _group_stats(uu * y2)))
        dy = dy2 * sz
        dz = dy2 * yv * _dsilu(z)

        tri = _tri()
        rows = lax.broadcasted_iota(jnp.int32, (CHUNK, 1), 0)
        dG = [jnp.zeros((CHUNK, CHUNK), F32) for _ in range(2)]
        dB = [jnp.zeros((CHUNK, SSD_STATE), F32) for _ in range(2)]
        dC = [jnp.zeros((CHUNK, SSD_STATE), F32) for _ in range(2)]
        dX_t, dA_t, ddtx_t = [], [], []
        for t in range(3):
            sl = slice(128 * t, 128 * (t + 1))
            hp_t = hp_ref[sl, :]
            hpb = hp_t.astype(BF)
            dhc = dh[sl, :]
            dh_new = jnp.zeros((128, SSD_STATE), F32)
            dX = jnp.zeros((CHUNK, 128), F32)
            dA = jnp.zeros((CHUNK, 128), F32)
            ddtx = jnp.zeros((CHUNK, 128), F32)
            for e in range(2):
                h = 2 * t + e
                g, col = h // 3, HEAD * h
                lm, rm, fl = _lane_mask(e), _row_mask(e), _first_lane(e)
                L = jnp.exp(jnp.where(tri, A[:, col:col + 1] - AT[col:col + 1, :], NEG))
                Mf = CB[g] * L
                Xm = jnp.where(lm, X[:, sl], 0.0)
                Xmb = Xm.astype(BF)
                dyh = jnp.where(lm, dy[:, sl], 0.0)
                dyb = dyh.astype(BF)
                dXh = _dot_tn(Mf.astype(BF), dyb)
                dM = jnp.where(tri, _dot_nt(dyb, Xmb), 0.0)
                Wm = dM * Mf
                dAc = jnp.sum(Wm, axis=-1, keepdims=True) - jnp.sum(Wm.T, axis=-1, keepdims=True)
                dG[g] = dG[g] + dM * L
                eAt = p["eA"][:, sl]
                yo = _dot_nt(Cb[g], hpb)
                dyo = (dyh * eAt).astype(BF)
                dC[g] = dC[g] + _dot(dyo, hpb)
                dh_new = dh_new + _dot_tn(dyo, Cb[g])
                dAc = dAc + jnp.sum(dyh * yo * eAt, axis=-1, keepdims=True)
                dHn = jnp.where(rm, dhc, 0.0)
                dHnb = dHn.astype(BF)
                dec = p["dtot"][:, col:col + 1]
                dh_new = dh_new + dec * dHn
                Z = _dot_nt(Bb[g], dHnb)
                wt = p["wdec"][:, sl]
                xi = jnp.sum(Xm * Z, axis=-1, keepdims=True) * p["wdec"][:, col:col + 1]
                dXh = dXh + wt * Z
                dB[g] = dB[g] + _dot(jnp.where(lm, X[:, sl] * wt, 0.0).astype(BF), dHnb)
                dAtot = jnp.sum(xi, axis=0, keepdims=True) + dec * jnp.sum(
                    jnp.sum(dHn * hp_t, axis=-1, keepdims=True), axis=0, keepdims=True)
                dAc = dAc - xi + jnp.where(rows == CHUNK - 1, dAtot, 0.0)
                dA = dA + jnp.where(fl, dAc, 0.0)
                dX = dX + dXh
                ddtx = ddtx + jnp.where(fl, jnp.sum(dXh * xs[:, sl], axis=-1, keepdims=True), 0.0)
            dh[sl, :] = dh_new
            dX_t.append(dX)
            dA_t.append(dA)
            ddtx_t.append(ddtx)
        for g in range(2):
            dGb = dG[g].astype(BF)
            dC[g] = dC[g] + _dot(dGb, Bb[g])
            dB[g] = dB[g] + _dot_tn(dGb, Cb[g])
        dXf = jnp.concatenate(dX_t, axis=1)
        da = _cumsum_rows(jnp.concatenate(dA_t, axis=1), reverse=True)
        ddt = da * (-jnp.exp(al_ref[...])) + jnp.concatenate(ddtx_t, axis=1)
        du = ddt * _sigmoid(p["u"])
        dxs = dXf * p["dt"] + dskip_ * dy
        dxc = jnp.concatenate([dxs, dB[0], dB[1], dC[0], dC[1]], axis=1) * _dsilu(p["xc"])
        ext2[0:CHUNK, :] = dxc
        dxbc = jnp.zeros((CHUNK, SSD_CONV_DIM), F32)
        for j in range(4):
            dxbc = dxbc + cw_ref[j:j + 1, :] * ext2[pl.ds(3 - j, CHUNK), :]
            dcw_ref[j:j + 1, :] += jnp.sum(dxc * ext[pl.ds(5 + j, CHUNK), :], axis=0, keepdims=True)
        ext2[CHUNK:CHUNK + 8, :] = dxc[0:8, :]
        dcb_ref[...] += jnp.sum(dxc, axis=0, keepdims=True)
        dvec_ref[0:1, :] += jnp.sum(du, axis=0, keepdims=True)
        dvec_ref[1:2, :] += jnp.sum(da * p["a"], axis=0, keepdims=True)
        dvec_ref[2:3, :] += jnp.sum(dy * xs, axis=0, keepdims=True)
        dvec_ref[3:4, :] += jnp.sum(dy3_ * y2 * r, axis=0, keepdims=True)
        dx_ref[...] = jnp.concatenate([dxbc, dz, du], axis=1).astype(BF)

    return _call(body, name="ssd_bwd", grid=(B, N_CHUNK), in_specs=[row, halo, hp, y] + params,
                 out_specs=[row, const(4, SSD_CONV_DIM), const(1, SSD_CONV_DIM), const(8, SSD_W)],
                 out_shape=[jax.ShapeDtypeStruct((T, W_SSD), BF), jax.ShapeDtypeStruct((4, SSD_CONV_DIM), F32),
                            jax.ShapeDtypeStruct((1, SSD_CONV_DIM), F32), jax.ShapeDtypeStruct((8, SSD_W), F32)],
                 scratch=[pltpu.VMEM((8 + CHUNK, SSD_CONV_DIM), F32), pltpu.VMEM((8 + CHUNK, SSD_CONV_DIM), F32),
                          pltpu.VMEM((SSD_W, SSD_STATE), F32)],
                 sem=("arbitrary", "arbitrary"))(sin, sin, hprev, dy3, conv_w, conv_b, dtb, alog, dskip, norm_g)


def _sgu_core(uv_ref, g_ref, b_ref, w_ref, bias_ref):
    x = uv_ref[...]
    cdf = 0.5 * (1.0 + lax.erf(x * (2.0 ** -0.5)))
    ge = x * cdf
    dge = cdf + x * jnp.exp(-0.5 * x * x) * ((2.0 * math.pi) ** -0.5)
    u, v = ge[:, 0:SGU_W], ge[:, SGU_W:]
    vc = v - jnp.mean(v, axis=-1, keepdims=True)
    rstd = lax.rsqrt(jnp.mean(vc * vc, axis=-1, keepdims=True) + LN_EPS)
    vhat = vc * rstd
    vn = vhat * g_ref[...] + b_ref[...]
    tri = _tri()
    wc = [jnp.where(tri, w_ref[gi], 0.0).astype(BF) for gi in range(4)]
    vm = [jnp.where(_lane_mask(gi % 2), vn[:, 128 * (gi // 2):128 * (gi // 2 + 1)], 0.0).astype(BF) for gi in range(4)]
    mixed = jnp.concatenate([_dot(wc[2 * t], vm[2 * t]) + _dot(wc[2 * t + 1], vm[2 * t + 1]) for t in range(2)],
                            axis=1) + bias_ref[...]
    return dict(dge=dge, u=u, rstd=rstd, vhat=vhat, wc=wc, vm=vm, mixed=mixed)


def _sgu_specs():
    vec = pl.BlockSpec((1, SGU_W), lambda i: (0, 0))
    return [pl.BlockSpec((CHUNK, W_UV), lambda i: (i, 0)), vec, vec,
            pl.BlockSpec((4, CHUNK, CHUNK), lambda i: (0, 0, 0)), pl.BlockSpec((CHUNK, SGU_W), lambda i: (0, 0))]


def _sgu_fwd(uv, ln_g, ln_b, w, bias):
    T = uv.shape[0]

    def body(uv_ref, g_ref, b_ref, w_ref, bias_ref, y_ref):
        s = _sgu_core(uv_ref, g_ref, b_ref, w_ref, bias_ref)
        y_ref[...] = s["u"] * s["mixed"]

    return _call(body, name="sgu_fwd", grid=(T // CHUNK,), in_specs=_sgu_specs(),
                 out_specs=pl.BlockSpec((CHUNK, SGU_W), lambda i: (i, 0)),
                 out_shape=jax.ShapeDtypeStruct((T, SGU_W), F32), sem=("parallel",))(uv, ln_g, ln_b, w, bias)


def _sgu_bwd(uv, dy, ln_g, ln_b, w, bias):
    T = uv.shape[0]

    def body(uv_ref, dy_ref, g_ref, b_ref, w_ref, bias_ref, dx_ref, dw_ref, dbias_ref, dln_ref):
        @pl.when(pl.program_id(0) == 0)
        def _():
            dw_ref[...] = jnp.zeros_like(dw_ref)
            dbias_ref[...] = jnp.zeros_like(dbias_ref)
            dln_ref[...] = jnp.zeros_like(dln_ref)

        s = _sgu_core(uv_ref, g_ref, b_ref, w_ref, bias_ref)
        dy_ = dy_ref[...]
        du = dy_ * s["mixed"]
        dmix = dy_ * s["u"]
        dbias_ref[...] += dmix
        tri = _tri()
        dvn_t = []
        for t in range(2):
            acc = jnp.zeros((CHUNK, 128), F32)
            for e in range(2):
                gi = 2 * t + e
                dmg = jnp.where(_lane_mask(e), dmix[:, 128 * t:128 * (t + 1)], 0.0).astype(BF)
                acc = acc + _dot_tn(s["wc"][gi], dmg)
                dw_ref[gi] += jnp.where(tri, _dot_nt(dmg, s["vm"][gi]), 0.0)
            dvn_t.append(acc)
        dvn = jnp.concatenate(dvn_t, axis=1)
        dln_ref[0:1, :] += jnp.sum(dvn * s["vhat"], axis=0, keepdims=True)
        dln_ref[1:2, :] += jnp.sum(dvn, axis=0, keepdims=True)
        dvh = dvn * g_ref[...]
        dv = s["rstd"] * (dvh - jnp.mean(dvh, axis=-1, keepdims=True)
                          - s["vhat"] * jnp.mean(dvh * s["vhat"], axis=-1, keepdims=True))
        dx_ref[...] = (jnp.concatenate([du, dv], axis=1) * s["dge"]).astype(BF)

    ins = _sgu_specs()
    return _call(body, name="sgu_bwd", grid=(T // CHUNK,),
                 in_specs=[ins[0], pl.BlockSpec((CHUNK, SGU_W), lambda i: (i, 0))] + ins[1:],
                 out_specs=[pl.BlockSpec((CHUNK, W_UV), lambda i: (i, 0)),
                            pl.BlockSpec((4, CHUNK, CHUNK), lambda i: (0, 0, 0)),
                            pl.BlockSpec((CHUNK, SGU_W), lambda i: (0, 0)), pl.BlockSpec((8, SGU_W), lambda i: (0, 0))],
                 out_shape=[jax.ShapeDtypeStruct((T, W_UV), BF), jax.ShapeDtypeStruct((4, CHUNK, CHUNK), F32),
                            jax.ShapeDtypeStruct((CHUNK, SGU_W), F32), jax.ShapeDtypeStruct((8, SGU_W), F32)],
                 sem=("arbitrary",))(uv, dy, ln_g, ln_b, w, bias)


def _adamw(w, g, m, v):
    R, C = w.shape
    tr = R

    def body(w_ref, g_ref, m_ref, v_ref, d_ref, nm_ref, nv_ref):
        g_ = g_ref[...]
        m2 = ADAM_B1 * m_ref[...] + (1.0 - ADAM_B1) * g_
        v2 = ADAM_B2 * v_ref[...] + (1.0 - ADAM_B2) * (g_ * g_)
        m_hat = m2 / (1.0 - ADAM_B1 ** ADAM_STEP)
        v_hat = v2 / (1.0 - ADAM_B2 ** ADAM_STEP)
        d_ref[...] = -ADAM_LR * (m_hat / (jnp.sqrt(v_hat) + ADAM_EPS) + ADAM_WD * w_ref[...])
        nm_ref[...] = m2
        nv_ref[...] = v2

    blk = pl.BlockSpec((tr, C), lambda i: (i, 0))
    sh = jax.ShapeDtypeStruct((R, C), F32)
    return _call(body, name="adamw", grid=(R // tr,), in_specs=[blk] * 4, out_specs=[blk] * 3,
                 out_shape=[sh] * 3, sem=("parallel",))(w, g, m, v)


def _adamw_pair(w, g0, g1, m, v, dep):
    L, R, C = w.shape
    tr = _tile(R, 256 if C <= 1024 else 64)

    def body(w_ref, g0_ref, g1_ref, m_ref, v_ref, dep_ref, d_ref, nm_ref, nv_ref, og_ref):
        g_ = jnp.where(pl.program_id(0) == 0, g0_ref[...], g1_ref[...])
        m2 = ADAM_B1 * m_ref[...] + (1.0 - ADAM_B1) * g_
        v2 = ADAM_B2 * v_ref[...] + (1.0 - ADAM_B2) * (g_ * g_)
        m_hat = m2 / (1.0 - ADAM_B1 ** ADAM_STEP)
        v_hat = v2 / (1.0 - ADAM_B2 ** ADAM_STEP)
        d_ref[...] = -ADAM_LR * (m_hat / (jnp.sqrt(v_hat) + ADAM_EPS) + ADAM_WD * w_ref[...])
        nm_ref[...] = m2
        nv_ref[...] = v2
        og_ref[...] = g_

    lay = pl.BlockSpec((None, tr, C), lambda l, i: (l, i, 0))
    one = lambda k: pl.BlockSpec((tr, C), lambda l, i: (jnp.where(l == k, i, 0), 0))
    return _call(body, name="adamw_pair", grid=(L, R // tr),
                 in_specs=[lay, one(0), one(1), lay, lay, pl.BlockSpec((8, 128), lambda l, i: (0, 0))],
                 out_specs=[lay] * 4,
                 out_shape=[jax.ShapeDtypeStruct((L, R, C), F32)] * 4,
                 sem=("parallel", "parallel"))(w, g0, g1, m, v, dep)


def _row_steps(rows):
    return 2 if rows % 32 == 0 else 1


def _pair_add(gbuf, rsib, c):
    NS, _, R, C = gbuf.shape
    n = _row_steps(R)
    tr = R // n

    def body(c_ref, a_ref, b_ref, o_ref):
        o_ref[...] = (a_ref[...] + b_ref[...]).astype(BF)

    blk = pl.BlockSpec((None, tr, C), lambda j, i, c_ref: (j, i, 0))
    return pl.pallas_call(
        body, name="rs_pair_add",
        grid_spec=pltpu.PrefetchScalarGridSpec(
            num_scalar_prefetch=1, grid=(NS, n),
            in_specs=[pl.BlockSpec((None, None, tr, C), lambda j, i, c_ref: (j, c_ref[0], i, 0)), blk],
            out_specs=blk),
        out_shape=jax.ShapeDtypeStruct((NS, R, C), BF),
        compiler_params=pltpu.CompilerParams(dimension_semantics=("parallel", "parallel")),
    )(jnp.reshape(c, (1,)).astype(jnp.int32), gbuf, rsib)


def _chip_sum(pair, recv, me, c):
    NS, R, C = pair.shape
    n = _row_steps(R)
    tr = R // n

    def body(s_ref, own_ref, p_ref, o_ref):
        p = [jnp.where(s_ref[0] == j, own_ref[...], p_ref[j]).astype(F32) for j in range(4)]
        o_ref[...] = ((p[0] + p[1]) + p[2]) + p[3]

    return pl.pallas_call(
        body, name="rs_chip_sum",
        grid_spec=pltpu.PrefetchScalarGridSpec(
            num_scalar_prefetch=1, grid=(n,),
            in_specs=[pl.BlockSpec((None, tr, C), lambda i, s: (s[0], i, 0)),
                      pl.BlockSpec((NS, tr, C), lambda i, s: (0, i, 0))],
            out_specs=pl.BlockSpec((None, tr, C), lambda i, s: (s[1], i, 0))),
        out_shape=jax.ShapeDtypeStruct((2, R, C), F32),
        compiler_params=pltpu.CompilerParams(dimension_semantics=("parallel",)),
    )(jnp.stack([me, c]).astype(jnp.int32), pair, recv)


MESH = pl.DeviceIdType.MESH
ANY = pl.BlockSpec(memory_space=pl.ANY)


def _place():
    x, y, c = lax.axis_index("x"), lax.axis_index("y"), lax.axis_index("c")
    return x, y, c, [(1 - x, y), (x, 1 - y), (1 - x, 1 - y)]


HBM = pl.BlockSpec(memory_space=pltpu.HBM)
SEM = pl.BlockSpec(memory_space=pltpu.SEMAPHORE)
EFFECT = pltpu.SideEffectType.DATAFLOW_SIDE_EFFECTING


class _Split:
    def __init__(self, tag, arrays, copies, n_copies, after=()):
        self.tag, self.copies, k = tag, copies, len(arrays)

        def body(*refs):
            sems = k + len(after)
            for cp in copies(refs[:k], refs[sems], refs[sems + 1]):
                cp.start()
            refs[-1][...] = jnp.zeros_like(refs[-1])

        out = pl.pallas_call(
            body, name=tag + "_start",
            out_shape=(pltpu.SemaphoreType.DMA((n_copies,)), pltpu.SemaphoreType.DMA((n_copies,)),
                       *[pltpu.HBM(a.shape, a.dtype) for a in arrays], jax.ShapeDtypeStruct((8, 128), F32)),
            in_specs=[HBM] * k + [ANY] * len(after),
            out_specs=(SEM, SEM, *[HBM] * k, pl.BlockSpec(memory_space=pltpu.VMEM)),
            input_output_aliases={i: 2 + i for i in range(k)},
            compiler_params=pltpu.CompilerParams(has_side_effects=EFFECT),
        )(*[pltpu.with_memory_space_constraint(a, pltpu.HBM) for a in arrays], *after)
        self.send, self.recv, self.arrays, self.token_array = out[0], out[1], list(out[2:2 + k]), out[-1]
        self.token = self.token_array[0, 0]

    def wait(self, after):
        k, copies = len(self.arrays), self.copies
        after = list(after) if isinstance(after, (list, tuple)) else [after]

        def body(*refs):
            for cp in copies(refs[:k], refs[k], refs[k + 1]):
                cp.wait_send()
                cp.wait_recv()

        return list(pl.pallas_call(
            body, name=self.tag + "_wait", out_shape=tuple(pltpu.HBM(a.shape, a.dtype) for a in self.arrays),
            in_specs=[HBM] * k + [SEM, SEM] + [ANY] * len(after), out_specs=tuple([HBM] * k),
            input_output_aliases={i: i for i in range(k)},
            compiler_params=pltpu.CompilerParams(has_side_effects=EFFECT),
        )(*self.arrays, self.send, self.recv, *after))


def _landing_zones(arrs):
    me = 2 * lax.axis_index("x") + lax.axis_index("y")
    return [lax.dynamic_update_index_in_dim(lax.empty((4,) + a.shape, a.dtype), a, me, 0) for a in arrs]


def _gather_start(arrs, lands, tag, after=()):
    n = len(arrs)

    def copies(refs, send, recv):
        x, y, c, chips = _place()
        return [pltpu.make_async_remote_copy(
            src_ref=refs[k], dst_ref=refs[n + k].at[2 * x + y], send_sem=send.at[3 * k + r],
            recv_sem=recv.at[3 * k + r], device_id=(px, py, c), device_id_type=MESH)
            for k in range(n) for r, (px, py) in enumerate(chips)]

    return _Split("gather_" + tag, list(arrs) + lands, copies, 3 * n, after)


def _gather_halves_start(arrs, tag):
    n = len(arrs)
    lands = _landing_zones(arrs)

    def copies(refs, send, recv):
        x, y, c, chips = _place()
        return [pltpu.make_async_remote_copy(
            src_ref=refs[k].at[c], dst_ref=refs[n + k].at[2 * x + y, c], send_sem=send.at[3 * k + r],
            recv_sem=recv.at[3 * k + r], device_id=(px, py, c), device_id_type=MESH)
            for k in range(n) for r, (px, py) in enumerate(chips)]

    return _Split("gather_" + tag, list(arrs) + lands, copies, 3 * n)


def _gather_halves_finish(lands, tag):
    n = len(lands)

    def copies(refs, send, recv):
        x, y, c, chips = _place()
        return [pltpu.make_async_remote_copy(
            src_ref=refs[k].at[2 * px + py, c], dst_ref=refs[k].at[2 * px + py, c], send_sem=send.at[3 * k + r],
            recv_sem=recv.at[3 * k + r], device_id=(x, y, 1 - c), device_id_type=MESH)
            for k in range(n) for r, (px, py) in enumerate(chips)]

    return _Split("gather_pass_" + tag, list(lands), copies, 3 * n)


def _part_sibling(gbufs):
    n = len(gbufs)

    def copies(refs, send, recv, off):
        x, y, c, _ = _place()
        return [pltpu.make_async_remote_copy(
            src_ref=refs[k].at[j, 1 - c], dst_ref=refs[n + k].at[j], send_sem=send.at[off + 4 * k + j],
            recv_sem=recv.at[off + 4 * k + j], device_id=(x, y, 1 - c), device_id_type=MESH)
            for k in range(n) for j in range(4)]

    return list(gbufs) + [lax.empty((4,) + g.shape[2:], g.dtype) for g in gbufs], 4 * n, copies


def _part_chips(pbufs):
    n = len(pbufs)

    def copies(refs, send, recv, off):
        x, y, c, chips = _place()
        return [pltpu.make_async_remote_copy(
            src_ref=refs[k].at[2 * px + py], dst_ref=refs[n + k].at[2 * x + y], send_sem=send.at[off + 3 * k + r],
            recv_sem=recv.at[off + 3 * k + r], device_id=(px, py, c), device_id_type=MESH)
            for k in range(n) for r, (px, py) in enumerate(chips)]

    return list(pbufs) + [lax.empty(p.shape, p.dtype) for p in pbufs], 3 * n, copies


def _part_join(fulls):
    def copies(refs, send, recv, off):
        x, y, c, _ = _place()
        return [pltpu.make_async_remote_copy(
            src_ref=refs[k].at[c], dst_ref=refs[k].at[c], send_sem=send.at[off + k], recv_sem=recv.at[off + k],
            device_id=(x, y, 1 - c), device_id_type=MESH) for k in range(len(fulls))]

    return list(fulls), len(fulls), copies


def _start_parts(parts, tag):
    arrays, spans, total = [], [], 0
    for arrs, n_copies, fn in parts:
        spans.append((len(arrays), len(arrs), total, fn))
        arrays += arrs
        total += n_copies

    def copies(refs, send, recv):
        return [cp for a0, na, off, fn in spans for cp in fn(refs[a0:a0 + na], send, recv, off)]

    op = _Split(tag, arrays, copies, total)
    op.spans = [(a0, na) for a0, na, _, _ in spans]
    return op


def _all_reduce_small(v):
    R, C = v.shape

    def body(v_ref, o_ref, g_ref, send, recv, loc):
        x, y, c, chips = _place()
        me, sibling = (x, y, c), (x, y, 1 - c)

        def rows(px, py, pc):
            return g_ref.at[4 * px + 2 * py + pc]

        def copy(k, block, to, src=None):
            return pltpu.make_async_remote_copy(
                src_ref=rows(*block) if src is None else src, dst_ref=rows(*block),
                send_sem=send.at[k], recv_sem=recv.at[k], device_id=to, device_id_type=MESH)

        mine = pltpu.make_async_copy(v_ref, rows(*me), loc)
        mine.start()
        first = [copy(0, me, sibling, src=v_ref)]
        first += [copy(1 + j, me, (*chip, c), src=v_ref) for j, chip in enumerate(chips)]
        for cp in first:
            cp.start()
        passed = [copy(4 + j, (*chip, c), sibling) for j, chip in enumerate(chips)]
        for j, chip in enumerate(chips):
            copy(1 + j, (*chip, c), me).wait_recv()
            passed[j].start()
        copy(0, sibling, me).wait_recv()
        for j, chip in enumerate(chips):
            copy(4 + j, (*chip, 1 - c), me).wait_recv()
        for cp in first + passed:
            cp.wait_send()
        mine.wait()
        acc = g_ref[0]
        for d in range(1, 8):
            acc = acc + g_ref[d]
        o_ref[...] = acc

    vm = pl.BlockSpec(memory_space=pltpu.VMEM)
    return pl.pallas_call(
        body, name="all_reduce_small", in_specs=[vm], out_specs=[vm, vm],
        out_shape=[jax.ShapeDtypeStruct((R, C), F32), jax.ShapeDtypeStruct((8, R, C), F32)],
        scratch_shapes=[pltpu.SemaphoreType.DMA((7,)), pltpu.SemaphoreType.DMA((7,)), pltpu.SemaphoreType.DMA],
    )(v)[0]


WEIGHTS = ['ffn1_norm', 'ffn1_w_gate', 'ffn1_w_up', 'ffn1_w_down', 'mix_norm', 'w_in', 'conv_w', 'conv_b', 'dt_bias',
           'a_log', 'd_skip', 'ssd_norm', 'sgu_ln_g', 'sgu_ln_b', 'sgu_w', 'sgu_b', 'w_out', 'ffn2_norm',
           'ffn2_w_gate', 'ffn2_w_up', 'ffn2_w_down', 'final_norm']
SHARDED = ['ffn1_w_gate', 'ffn1_w_up', 'ffn1_w_down', 'w_in', 'conv_w', 'w_out', 'ffn2_w_gate', 'ffn2_w_up',
           'ffn2_w_down']
SMALL = [n for n in WEIGHTS if n not in SHARDED]
GROUPS = [("ffn1", ["ffn1_w_gate", "ffn1_w_up", "ffn1_w_down"]), ("mix", ["w_in", "conv_w", "w_out"]),
          ("ffn2", ["ffn2_w_gate", "ffn2_w_up", "ffn2_w_down"])]
TRANSPOSED = ("ffn1_w_gate", "ffn1_w_up", "ffn2_w_gate", "ffn2_w_up")
DEPTH = 2


def _pack_w_in(w):
    return jnp.concatenate([w[..., 0:1152], w[..., 1536:2432], w[..., 1152:1536],
                            jnp.repeat(w[..., 2432:2438], HEAD, axis=-1), w[..., 2438:2950]], axis=-1)


def _unpack_w_in(dq, ds, du):
    return jnp.concatenate([dq, ds[:, 896:1280], ds[:, 0:896], ds[:, 1280::HEAD], du], axis=-1)


def _ffn_fwd(x, g, wg, wu, wd):
    xo, hb, S1, S2, A = _ffn_fwd_k(x, g, wg, wu, wd)
    return xo, (x, hb, S1, S2, A)


def _ffn_bwd(dxo, saved, g, wg, wu, wd):
    x, hb, S1, S2, A = saved
    dG, dU, dyb = _ffn_bwd_act(dxo, S1, S2, wd)
    dx, dg = _ffn_bwd_dx(dG, dU, wg, wu, x, g, dxo)
    dwg, dwu, dwd = _ffn_bwd_k2(hb, dyb, A, dG, dU)
    return dx, dg, dwg, dwu, dwd


def _mix_fwd(x, P):
    hb, qkv, sin, uv = _mix_proj(x, P["mix_norm"], P["w_in"])
    y_att, lse = _attn_combine([_attn_fwd(qkv, d) for d in DILATIONS])
    y_ssd, hprev = _ssd_fwd(sin, *P["ssd"])
    y_sgu = _sgu_fwd(uv, *P["sgu"])
    ycat = jnp.concatenate([y_att, y_ssd, y_sgu], axis=1).astype(BF)
    return _mm_nn(ycat, P["w_out"], res=x), (x, hb, qkv, sin, uv, y_att, lse, hprev, ycat)


def _mix_bwd(dxo, saved, P):
    x, hb, qkv, sin, uv, y_att, lse, hprev, ycat = saved
    dy_att, dy_ssd, dy_sgu = _mix_bwd_dy(dxo, P["w_out"])
    dwout = _mm_tn(ycat, dxo)
    dqkv = _sum_branches([_attn_bwd(qkv, dy_att, y_att, lse, d) for d in DILATIONS])
    dsin, dcw, dcb, dvec = _ssd_bwd(sin, hprev, dy_ssd, *P["ssd"])
    duv, dsw, dsbias, dln = _sgu_bwd(uv, dy_sgu, *P["sgu"])
    dwin = _unpack_w_in(_mm_tn(hb, dqkv), _mm_tn(hb, dsin), _mm_tn(hb, duv))
    dx, dg = _mix_bwd_dx(dqkv, dsin, duv, P["w_in"], x, P["mix_norm"], dxo)
    grads = dict(
        mix_norm=dg[0], w_in=dwin, conv_w=dcw, conv_b=dcb[0], dt_bias=dvec[0, ::HEAD], a_log=dvec[1, ::HEAD],
        d_skip=jnp.sum(dvec[2].reshape(6, HEAD), axis=-1), ssd_norm=dvec[3], sgu_ln_g=dln[0], sgu_ln_b=dln[1],
        sgu_w=dsw, sgu_b=jnp.sum(dsbias.reshape(CHUNK, 4, HEAD), axis=-1).T, w_out=dwout)
    return dx, grads


def _halved(g):
    rows = g.size // g.shape[-1]
    return g.reshape(4, 2, rows // 8, g.shape[-1])


def kernel(x, ffn1_norm, ffn1_w_gate, ffn1_w_up, ffn1_w_down, mix_norm, w_in, conv_w, conv_b, dt_bias, a_log, d_skip, ssd_norm, sgu_ln_g, sgu_ln_b, sgu_w, sgu_b, w_out, ffn2_norm, ffn2_w_gate, ffn2_w_up, ffn2_w_down, final_norm, loss_target, m_ffn1_norm, m_ffn1_w_gate, m_ffn1_w_up, m_ffn1_w_down, m_mix_norm, m_w_in, m_conv_w, m_conv_b, m_dt_bias, m_a_log, m_d_skip, m_ssd_norm, m_sgu_ln_g, m_sgu_ln_b, m_sgu_w, m_sgu_b, m_w_out, m_ffn2_norm, m_ffn2_w_gate, m_ffn2_w_up, m_ffn2_w_down, m_final_norm, v_ffn1_norm, v_ffn1_w_gate, v_ffn1_w_up, v_ffn1_w_down, v_mix_norm, v_w_in, v_conv_w, v_conv_b, v_dt_bias, v_a_log, v_d_skip, v_ssd_norm, v_sgu_ln_g, v_sgu_ln_b, v_sgu_w, v_sgu_b, v_w_out, v_ffn2_norm, v_ffn2_w_gate, v_ffn2_w_up, v_ffn2_w_down, v_final_norm):
    given = dict(x=x, ffn1_norm=ffn1_norm, ffn1_w_gate=ffn1_w_gate, ffn1_w_up=ffn1_w_up, ffn1_w_down=ffn1_w_down, mix_norm=mix_norm, w_in=w_in, conv_w=conv_w, conv_b=conv_b, dt_bias=dt_bias, a_log=a_log, d_skip=d_skip, ssd_norm=ssd_norm, sgu_ln_g=sgu_ln_g, sgu_ln_b=sgu_ln_b, sgu_w=sgu_w, sgu_b=sgu_b, w_out=w_out, ffn2_norm=ffn2_norm, ffn2_w_gate=ffn2_w_gate, ffn2_w_up=ffn2_w_up, ffn2_w_down=ffn2_w_down, final_norm=final_norm, loss_target=loss_target, m_ffn1_norm=m_ffn1_norm, m_ffn1_w_gate=m_ffn1_w_gate, m_ffn1_w_up=m_ffn1_w_up, m_ffn1_w_down=m_ffn1_w_down, m_mix_norm=m_mix_norm, m_w_in=m_w_in, m_conv_w=m_conv_w, m_conv_b=m_conv_b, m_dt_bias=m_dt_bias, m_a_log=m_a_log, m_d_skip=m_d_skip, m_ssd_norm=m_ssd_norm, m_sgu_ln_g=m_sgu_ln_g, m_sgu_ln_b=m_sgu_ln_b, m_sgu_w=m_sgu_w, m_sgu_b=m_sgu_b, m_w_out=m_w_out, m_ffn2_norm=m_ffn2_norm, m_ffn2_w_gate=m_ffn2_w_gate, m_ffn2_w_up=m_ffn2_w_up, m_ffn2_w_down=m_ffn2_w_down, m_final_norm=m_final_norm, v_ffn1_norm=v_ffn1_norm, v_ffn1_w_gate=v_ffn1_w_gate, v_ffn1_w_up=v_ffn1_w_up, v_ffn1_w_down=v_ffn1_w_down, v_mix_norm=v_mix_norm, v_w_in=v_w_in, v_conv_w=v_conv_w, v_conv_b=v_conv_b, v_dt_bias=v_dt_bias, v_a_log=v_a_log, v_d_skip=v_d_skip, v_ssd_norm=v_ssd_norm, v_sgu_ln_g=v_sgu_ln_g, v_sgu_ln_b=v_sgu_ln_b, v_sgu_w=v_sgu_w, v_sgu_b=v_sgu_b, v_w_out=v_w_out, v_ffn2_norm=v_ffn2_norm, v_ffn2_w_gate=v_ffn2_w_gate, v_ffn2_w_up=v_ffn2_w_up, v_ffn2_w_down=v_ffn2_w_down, v_final_norm=v_final_norm)
    T = given["x"].shape[0] * given["x"].shape[1]
    D = given["x"].shape[2]
    x0 = given["x"].reshape(T, D)
    tgt = given["loss_target"].reshape(T, D)
    c = lax.axis_index("c")

    bf = {n: given[n].astype(BF) for n in SHARDED if n not in ("w_in", "conv_w")}
    bf["w_in"] = _pack_w_in(given["w_in"]).astype(BF)
    bf["conv_w"] = given["conv_w"]
    first_key = (0, GROUPS[0][0])
    first = [bf[n][0].reshape((2, bf[n].shape[1] // 2) + bf[n].shape[2:]) for n in GROUPS[0][1]]
    gathers = {first_key: _gather_halves_start(first, "l0_" + GROUPS[0][0])}
    later = {(i, gname): [bf[n][i] for n in names] for i in range(DEPTH) for gname, names in GROUPS
             if (i, gname) != first_key}
    zones = {key: _landing_zones(arrs) for key, arrs in later.items()}

    def gathered(i, gname, after):
        if (i, gname) != first_key:
            return gathers[(i, gname)].wait(after)[3:]
        got = gathers[first_key].wait([after] + [z for zs in zones.values() for z in zs])[3:]
        got = _gather_halves_finish(got, "l0_" + gname).wait(after)
        prev = got[0]
        for key, arrs in later.items():
            gathers[key] = _gather_start(arrs, zones[key], f"l{key[0]}_{key[1]}", after=[prev])
            prev = gathers[key].token_array
        return [z.reshape((4, 2 * z.shape[2]) + z.shape[3:]) for z in got]

    def mix_params(i, got):
        win = got[0].reshape(D, W_QKV + W_SSD + W_UV)
        rep = lambda v: jnp.repeat(v, HEAD)[None]
        ssd = (got[1].transpose(1, 0, 2).reshape(4, SSD_CONV_DIM), given["conv_b"][i][None],
               rep(given["dt_bias"][i]), rep(given["a_log"][i]), rep(given["d_skip"][i]), given["ssd_norm"][i][None])
        sgu = (given["sgu_ln_g"][i][None], given["sgu_ln_b"][i][None], given["sgu_w"][i],
               jnp.repeat(given["sgu_b"][i].T, HEAD, axis=1))
        return dict(mix_norm=given["mix_norm"][i][None], w_in=win, w_out=got[2].reshape(-1, D), ssd=ssd, sgu=sgu)

    x = x0
    tape = []
    for i in range(DEPTH):
        got = gathered(i, "ffn1", x)
        token = functools.reduce(lambda a, b: a + b, [g.token for g in gathers.values()]) if i == 0 else 0.0
        P = dict(ffn1=(given["ffn1_norm"][i][None] + token, *got))
        x, s1 = _ffn_fwd(x, *P["ffn1"])
        P.update(mix_params(i, gathered(i, "mix", x)))
        x, s2 = _mix_fwd(x, P)
        P["ffn2"] = (given["ffn2_norm"][i][None], *gathered(i, "ffn2", x))
        x, s3 = _ffn_fwd(x, *P["ffn2"])
        tape.append((P, s1, s2, s3))
    loss_part, dx, dgf = _final_loss(x, given["final_norm"][None], tgt)

    me = 2 * lax.axis_index("x") + lax.axis_index("y")
    jobs = []

    flight = dict(op=None, owners=[], ticks=0)

    def tick(after, begin=None):
        parts, owners = [], []
        if flight["op"] is not None:
            got = flight["op"].wait(after)
            for job, (a0, na) in zip(flight["owners"], flight["op"].spans):
                mine, k = got[a0:a0 + na], len(job["names"])
                if job["stage"] == 1:
                    parts.append(_part_chips([_pair_add(g, l, c) for g, l in zip(mine[:k], mine[k:])]))
                elif job["stage"] == 2:
                    parts.append(_part_join([_chip_sum(p, l, me, c) for p, l in zip(mine[:k], mine[k:])]))
                else:
                    job.update(stage=4, out=dict(zip(job["names"], mine)))
                    continue
                job["stage"] += 1
                owners.append(job)
        if begin is not None:
            i, gname, gd = begin
            names = [n for n in dict(GROUPS)[gname] if n != "conv_w"]
            jobs.append(dict(key=(i, gname), names=names, stage=1))
            parts.append(_part_sibling([_halved(gd[n]) for n in names]))
            owners.append(jobs[-1])
        flight.update(op=_start_parts(parts, f"rs_tick{flight['ticks']}") if parts else None, owners=owners,
                      ticks=flight["ticks"] + 1)
        return flight["op"].token if parts else 0.0

    grads = [dict() for _ in range(DEPTH)]
    tok = 0.0
    for i in reversed(range(DEPTH)):
        P, s1, s2, s3 = tape[i]
        g = grads[i]
        norm, wg, wu, wd = P["ffn2"]
        dx, dn2, g["ffn2_w_gate"], g["ffn2_w_up"], g["ffn2_w_down"] = _ffn_bwd(dx, s3, norm + tok, wg, wu, wd)
        tok = tick(dx, (i, "ffn2", g))
        dx, gm = _mix_bwd(dx, s2, {**P, "mix_norm": P["mix_norm"] + tok})
        g.update(gm)
        tok = tick(dx, (i, "mix", g))
        norm, wg, wu, wd = P["ffn1"]
        dx, dn1, g["ffn1_w_gate"], g["ffn1_w_up"], g["ffn1_w_down"] = _ffn_bwd(dx, s1, norm + tok, wg, wu, wd)
        tok = tick(dx, (i, "ffn1", g))
        g["ffn1_norm"], g["ffn2_norm"] = dn1[0], dn2[0]
    grad_x = dx.reshape(given["x"].shape)

    order = [n for n in SMALL if n != "final_norm"] + ["final_norm"]
    small = [jnp.stack([grads[i][n] for i in range(DEPTH)]) for n in order[:-1] + ["conv_w"]]
    small = small[:-1] + [dgf[0], small[-1], loss_part[0, 0:1]]
    n_small = sum(s.size for s in small)
    rows_small = -(-n_small // (128 * 8)) * 8

    def flat(arrs):
        fill = rows_small * 128 - sum(a.size for a in arrs)
        return jnp.concatenate([a.reshape(-1) for a in arrs] + [jnp.zeros((fill,), F32)]).reshape(rows_small, 128)

    gsmall = _all_reduce_small(flat(small)).reshape(-1)

    grad_w = {}
    off = 0
    for n in order:
        size = given[n].size
        grad_w[n] = gsmall[off:off + size].reshape(given[n].shape)
        off += size
    cw = gsmall[off:off + 2 * 4 * SSD_CONV_DIM].reshape(DEPTH, 4, SSD_CONV_DIM)
    grad_w["conv_w"] = lax.dynamic_slice_in_dim(cw, me * (SSD_CONV_DIM // 4), SSD_CONV_DIM // 4, axis=2)
    loss = gsmall[off + 2 * 4 * SSD_CONV_DIM]

    delta, new_m, new_v = {}, {}, {}
    shp = given["conv_w"].shape
    d, m2, v2 = _adamw(*[a.reshape(shp[0] * shp[1], shp[2])
                         for a in (given["conv_w"], grad_w["conv_w"], given["m_conv_w"], given["v_conv_w"])])
    delta["conv_w"], new_m["conv_w"], new_v["conv_w"] = d.reshape(shp), m2.reshape(shp), v2.reshape(shp)
    packed = [flat([given[pre + n] for n in order]) for pre in ("", "m_", "v_")]
    small_out = _adamw(packed[0], gsmall.reshape(rows_small, 128), packed[1], packed[2])
    outs = [o.reshape(-1) for o in small_out]
    off = 0
    for n in order:
        size = given[n].size
        for dst, o in zip((delta, new_m, new_v), outs):
            dst[n] = o[off:off + size].reshape(given[n].shape)
        off += size

    stepped, arrived = {}, {}

    def update_arrived(dep):
        out = None
        for job in jobs:
            if job["stage"] == 4 and not job.get("seen"):
                job["seen"] = True
                for n, full in job["out"].items():
                    view = (lambda a: jnp.swapaxes(a, 1, 2)) if n in TRANSPOSED else (lambda a: a)
                    arrived.setdefault(n, {})[job["key"][0]] = full.reshape(view(given[n]).shape[1:])
                    if len(arrived[n]) == DEPTH:
                        res = _adamw_pair(view(given[n]), arrived[n][0], arrived[n][1], view(given["m_" + n]),
                                          view(given["v_" + n]), dep)
                        stepped[n] = [view(r) for r in res]
                        out = res[0]
        return out

    after = small_out[0]
    while any(j["stage"] < 4 for j in jobs):
        done = update_arrived(jnp.zeros((8, 128), F32) + tok)
        after = after if done is None else done
        tok = tick(after)
    update_arrived(jnp.zeros((8, 128), F32) + tok)
    for n, (d, m2, v2, g) in stepped.items():
        delta[n], new_m[n], new_v[n], grad_w[n] = d, m2, v2, g

    return (loss, grad_x, *[grad_w[n] for n in WEIGHTS], *[delta[n] for n in WEIGHTS],
            *[new_m[n] for n in WEIGHTS], *[new_v[n] for n in WEIGHTS])
```

```python
import functools
import math

import jax
import jax.numpy as jnp
from jax import lax
from jax.experimental import pallas as pl
from jax.experimental.pallas import tpu as pltpu

F32 = jnp.float32
BF = jnp.bfloat16

RMS_EPS = 1e-6
LN_EPS = 1e-5
SEQ = 2048
CHUNK = 128
N_CHUNK = SEQ // CHUNK
ATT_W = 384
HEAD = 64
SSD_W = 384
SSD_CONV_DIM = 896
SSD_STATE = 128
SGU_W = 256
DILATIONS = (1, 4, 16)
W_QKV = 3 * ATT_W
W_SSD = SSD_CONV_DIM + SSD_W + SSD_W
W_UV = 2 * SGU_W
ADAM_LR = 0.001
ADAM_B1 = 0.9
ADAM_B2 = 0.999
ADAM_EPS = 1e-08
ADAM_WD = 0.01
ADAM_STEP = 10
NEG = -1e30
ATTN_BWD_VMEM = 48 * 2 ** 20
FFN_VMEM = 60 * 2 ** 20


def _dot(a, b):
    return jnp.dot(a, b, preferred_element_type=F32)


def _dot_nt(a, b):
    return lax.dot_general(a, b, (((1,), (1,)), ((), ())), preferred_element_type=F32)


def _dot_tn(a, b):
    return lax.dot_general(a, b, (((0,), (0,)), ((), ())), preferred_element_type=F32)


def _sigmoid(x):
    return 1.0 / (1.0 + jnp.exp(-x))


def _call(body, *, name, grid, in_specs, out_specs, out_shape, scratch=(), sem=None, vmem=None):
    return pl.pallas_call(
        body, name=name, grid=grid, in_specs=in_specs, out_specs=out_specs, out_shape=out_shape,
        scratch_shapes=list(scratch),
        compiler_params=pltpu.CompilerParams(dimension_semantics=sem, vmem_limit_bytes=vmem),
    )


def _tile(n, want):
    t = min(n, want)
    while n % t:
        t //= 2
    return t


def _final_loss(x, g, tgt):
    T, D = x.shape
    tm = _tile(T, 512)

    def body(x_ref, g_ref, t_ref, l_ref, dx_ref, dg_ref):
        @pl.when(pl.program_id(0) == 0)
        def _():
            dg_ref[...] = jnp.zeros_like(dg_ref)
            l_ref[...] = jnp.zeros_like(l_ref)

        xf = x_ref[...]
        gg = g_ref[...]
        r = lax.rsqrt(jnp.mean(xf * xf, axis=-1, keepdims=True) + RMS_EPS)
        xn = xf * r
        e = xn * gg - t_ref[...]
        part = 0.5 * jnp.sum(jnp.mean(e * e, axis=-1, keepdims=True), axis=0, keepdims=True)
        l_ref[...] += jnp.broadcast_to(part, l_ref.shape)
        dy = e * (1.0 / D)
        u = dy * gg
        mu = jnp.mean(u * xf, axis=-1, keepdims=True)
        dx_ref[...] = r * (u - xf * (r * r * mu))
        dg_ref[...] += jnp.sum(dy * xn, axis=0, keepdims=True)

    row = pl.BlockSpec((tm, D), lambda i: (i, 0))
    vec = pl.BlockSpec((1, D), lambda i: (0, 0))
    lsp = pl.BlockSpec((1, 128), lambda i: (0, 0))
    return _call(body, name="final_loss", grid=(T // tm,), in_specs=[row, vec, row], out_specs=[lsp, row, vec],
                 out_shape=[jax.ShapeDtypeStruct((1, 128), F32), jax.ShapeDtypeStruct((T, D), F32),
                            jax.ShapeDtypeStruct((1, D), F32)],
                 sem=("arbitrary",))(x, g, tgt)


def _resident(shape):
    return pl.BlockSpec(shape, lambda *_: (0,) * len(shape), pipeline_mode=pl.Buffered(1))


def _ffn_fwd_k(x, gn, wg, wu, wd):
    T, D = x.shape
    NS, _, Fs = wg.shape
    tm = _tile(T, 1024)

    def body(x_ref, gn_ref, wg_ref, wu_ref, wd_ref, o_ref, h_ref, s1_ref, s2_ref, a_ref, hs, acc):
        j = pl.program_id(1)

        @pl.when(j == 0)
        def _():
            xf = x_ref[...]
            r = lax.rsqrt(jnp.mean(xf * xf, axis=-1, keepdims=True) + RMS_EPS)
            hs[...] = (xf * r * gn_ref[...]).astype(BF)
            h_ref[...] = hs[...]
            acc[...] = jnp.zeros_like(acc)

        h = hs[...]
        g = _dot(h, wg_ref[...])
        u = _dot(h, wu_ref[...])
        sg = _sigmoid(g)
        s1 = g * sg
        a = (s1 * u).astype(BF)
        s1_ref[...] = s1.astype(BF)
        s2_ref[...] = (u * (sg * (1.0 + g * (1.0 - sg)))).astype(BF)
        a_ref[...] = a
        acc[...] += _dot(a, wd_ref[...])

        @pl.when(j == NS - 1)
        def _():
            o_ref[...] = x_ref[...] + 0.5 * acc[...]

    row = pl.BlockSpec((tm, D), lambda i, j: (i, 0))
    act = pl.BlockSpec((None, tm, Fs), lambda i, j: (j, i, 0))
    sh = jax.ShapeDtypeStruct((NS, T, Fs), BF)
    wspec = lambda w: pl.BlockSpec((None,) + w.shape[1:], lambda i, j: (j, 0, 0))
    return _call(body, name="ffn_fwd", grid=(T // tm, NS),
                 in_specs=[row, pl.BlockSpec((1, D), lambda i, j: (0, 0)), wspec(wg), wspec(wu), wspec(wd)],
                 out_specs=[row, row, act, act, act],
                 out_shape=[jax.ShapeDtypeStruct((T, D), F32), jax.ShapeDtypeStruct((T, D), BF), sh, sh, sh],
                 scratch=[pltpu.VMEM((tm, D), BF), pltpu.VMEM((tm, D), F32)],
                 sem=("parallel", "arbitrary"), vmem=FFN_VMEM)(x, gn, wg, wu, wd)


def _ffn_bwd_act(dxo, s1, s2, wd):
    NS, T, Fs = s1.shape
    D = dxo.shape[1]
    tm = _tile(T, 1024)

    def body(dxo_ref, s1_ref, s2_ref, wd_ref, dg_ref, du_ref, dy_ref, dys):
        j = pl.program_id(1)

        @pl.when(j == 0)
        def _():
            dys[...] = (0.5 * dxo_ref[...]).astype(BF)
            dy_ref[...] = dys[...]

        da = _dot_nt(dys[...], wd_ref[j])
        dg_ref[...] = (da * s2_ref[...].astype(F32)).astype(BF)
        du_ref[...] = (da * s1_ref[...].astype(F32)).astype(BF)

    row = pl.BlockSpec((tm, D), lambda i, j: (i, 0))
    act = pl.BlockSpec((None, tm, Fs), lambda i, j: (j, i, 0))
    sh = jax.ShapeDtypeStruct((NS, T, Fs), BF)
    return _call(body, name="ffn_bwd_act", grid=(T // tm, NS), in_specs=[row, act, act, _resident(wd.shape)],
                 out_specs=[act, act, row], out_shape=[sh, sh, jax.ShapeDtypeStruct((T, D), BF)],
                 scratch=[pltpu.VMEM((tm, D), BF)], sem=("parallel", "arbitrary"))(dxo, s1, s2, wd)


def _ffn_bwd_dx(dg, du, wg, wu, x, gn, dxo):
    NS, T, Fs = dg.shape
    D = x.shape[1]
    tm = _tile(T, 1024)

    def body(dg_ref, du_ref, wg_ref, wu_ref, x_ref, gn_ref, dxo_ref, dx_ref, dgn_ref, acc):
        i, j = pl.program_id(0), pl.program_id(1)

        @pl.when((i == 0) & (j == 0))
        def _():
            dgn_ref[...] = jnp.zeros_like(dgn_ref)

        @pl.when(j == 0)
        def _():
            acc[...] = jnp.zeros_like(acc)

        acc[...] += _dot_nt(dg_ref[...], wg_ref[j]) + _dot_nt(du_ref[...], wu_ref[j])

        @pl.when(j == NS - 1)
        def _():
            xf = x_ref[...]
            r = lax.rsqrt(jnp.mean(xf * xf, axis=-1, keepdims=True) + RMS_EPS)
            dh = acc[...]
            uu = dh * gn_ref[...]
            mu = jnp.mean(uu * xf, axis=-1, keepdims=True)
            dx_ref[...] = dxo_ref[...] + r * (uu - xf * (r * r * mu))
            dgn_ref[...] += jnp.sum(dh * xf * r, axis=0, keepdims=True)

    row = pl.BlockSpec((tm, D), lambda i, j: (i, 0))
    vec = pl.BlockSpec((1, D), lambda i, j: (0, 0))
    act = pl.BlockSpec((None, tm, Fs), lambda i, j: (j, i, 0))
    return _call(body, name="ffn_bwd_dx", grid=(T // tm, NS),
                 in_specs=[act, act, _resident(wg.shape), _resident(wu.shape), row, vec, row], out_specs=[row, vec],
                 out_shape=[jax.ShapeDtypeStruct((T, D), F32), jax.ShapeDtypeStruct((1, D), F32)],
                 scratch=[pltpu.VMEM((tm, D), F32)], sem=("arbitrary", "arbitrary"), vmem=FFN_VMEM)(
        dg, du, wg, wu, x, gn, dxo)


def _ffn_bwd_k2(hb, dyb, a, dg, du):
    NS, T, Fs = a.shape
    D = hb.shape[1]
    tk = _tile(T, 1024)

    def body(h_ref, dy_ref, a_ref, dg_ref, du_ref, og_ref, ou_ref, od_ref):
        @pl.when(pl.program_id(1) == 0)
        def _():
            og_ref[...] = jnp.zeros_like(og_ref)
            ou_ref[...] = jnp.zeros_like(ou_ref)
            od_ref[...] = jnp.zeros_like(od_ref)

        h = h_ref[...]
        og_ref[...] += _dot_tn(dg_ref[...], h)
        ou_ref[...] += _dot_tn(du_ref[...], h)
        od_ref[...] += _dot_tn(a_ref[...], dy_ref[...])

    row = pl.BlockSpec((tk, D), lambda j, k: (k, 0))
    act = pl.BlockSpec((None, tk, Fs), lambda j, k: (j, k, 0))
    return _call(body, name="ffn_bwd_w", grid=(NS, T // tk), in_specs=[row, row, act, act, act],
                 out_specs=[pl.BlockSpec((None, Fs, D), lambda j, k: (j, 0, 0))] * 3,
                 out_shape=[jax.ShapeDtypeStruct((NS, Fs, D), F32)] * 3,
                 sem=("parallel", "arbitrary"))(hb, dyb, a, dg, du)


def _mm_nn(a, b, res=None, out_dtype=F32):
    T, K = a.shape
    N = b.shape[1]
    tm = _tile(T, 512)
    tn = N if N <= 2048 else _tile(N, 1024)

    def body(*refs):
        if res is None:
            a_ref, b_ref, o_ref = refs
            o_ref[...] = _dot(a_ref[...], b_ref[...]).astype(out_dtype)
        else:
            a_ref, b_ref, r_ref, o_ref = refs
            o_ref[...] = (r_ref[...] + _dot(a_ref[...], b_ref[...])).astype(out_dtype)

    o = pl.BlockSpec((tm, tn), lambda i, j: (i, j))
    ins = [pl.BlockSpec((tm, K), lambda i, j: (i, 0)), pl.BlockSpec((K, tn), lambda i, j: (0, j))]
    args = [a, b]
    if res is not None:
        ins.append(o)
        args.append(res)
    return _call(body, name="mm_nn", grid=(T // tm, N // tn), in_specs=ins, out_specs=o,
                 out_shape=jax.ShapeDtypeStruct((T, N), out_dtype), sem=("parallel", "parallel"))(*args)


def _mix_bwd_dy(dxo, w_out):
    T, D = dxo.shape
    tm = _tile(T, 512)
    cuts = (0, ATT_W, ATT_W + SSD_W, ATT_W + SSD_W + SGU_W)

    def body(dx_ref, w_ref, a_ref, s_ref, g_ref):
        d = _dot_nt(dx_ref[...].astype(BF), w_ref[...])
        for o_ref, lo, hi in zip((a_ref, s_ref, g_ref), cuts[:-1], cuts[1:]):
            o_ref[...] = d[:, lo:hi]

    row = lambda w: pl.BlockSpec((tm, w), lambda i: (i, 0))
    return _call(body, name="mix_bwd_dy", grid=(T // tm,), in_specs=[row(D), _resident(w_out.shape)],
                 out_specs=[row(ATT_W), row(SSD_W), row(SGU_W)],
                 out_shape=[jax.ShapeDtypeStruct((T, w), F32) for w in (ATT_W, SSD_W, SGU_W)],
                 sem=("parallel",))(dxo, w_out)


def _mm_tn(a, b):
    T, M = a.shape
    N = b.shape[1]
    tk = _tile(T, 1024)
    tmm = _tile(M, 512)

    def body(a_ref, b_ref, o_ref):
        @pl.when(pl.program_id(1) == 0)
        def _():
            o_ref[...] = jnp.zeros_like(o_ref)

        o_ref[...] += _dot_tn(a_ref[...].astype(BF), b_ref[...].astype(BF))

    return _call(body, name="mm_tn", grid=(M // tmm, T // tk),
                 in_specs=[pl.BlockSpec((tk, tmm), lambda i, k: (k, i)), pl.BlockSpec((tk, N), lambda i, k: (k, 0))],
                 out_specs=pl.BlockSpec((tmm, N), lambda i, k: (i, 0)),
                 out_shape=jax.ShapeDtypeStruct((M, N), F32), sem=("parallel", "arbitrary"))(a, b)


def _mix_proj(x, gn, win):
    T, D = x.shape
    tm = _tile(T, 512)
    cuts = (0, W_QKV, W_QKV + W_SSD, W_QKV + W_SSD + W_UV)

    def body(x_ref, gn_ref, w_ref, h_ref, q_ref, s_ref, u_ref):
        xf = x_ref[...]
        r = lax.rsqrt(jnp.mean(xf * xf, axis=-1, keepdims=True) + RMS_EPS)
        h = (xf * r * gn_ref[...]).astype(BF)
        h_ref[...] = h
        for o_ref, lo, hi in zip((q_ref, s_ref, u_ref), cuts[:-1], cuts[1:]):
            o_ref[...] = _dot(h, w_ref[:, lo:hi])

    row = lambda w: pl.BlockSpec((tm, w), lambda i: (i, 0))
    return _call(body, name="mix_proj", grid=(T // tm,),
                 in_specs=[row(D), pl.BlockSpec((1, D), lambda i: (0, 0)), _resident(win.shape)],
                 out_specs=[row(D), row(W_QKV), row(W_SSD), row(W_UV)],
                 out_shape=[jax.ShapeDtypeStruct((T, D), BF), jax.ShapeDtypeStruct((T, W_QKV), F32),
                            jax.ShapeDtypeStruct((T, W_SSD), F32), jax.ShapeDtypeStruct((T, W_UV), F32)],
                 sem=("parallel",))(x, gn, win)


def _mix_bwd_dx(dqkv, dsin, duv, win, x, gn, dxo):
    T, D = x.shape
    tm = _tile(T, 512)
    cuts = (0, W_QKV, W_QKV + W_SSD, W_QKV + W_SSD + W_UV)

    def body(dq_ref, ds_ref, du_ref, w_ref, x_ref, gn_ref, dxo_ref, dx_ref, dgn_ref):
        @pl.when(pl.program_id(0) == 0)
        def _():
            dgn_ref[...] = jnp.zeros_like(dgn_ref)

        dh = (_dot_nt(dq_ref[...], w_ref[:, cuts[0]:cuts[1]]) + _dot_nt(ds_ref[...], w_ref[:, cuts[1]:cuts[2]])
              + _dot_nt(du_ref[...], w_ref[:, cuts[2]:cuts[3]]))
        xf = x_ref[...]
        r = lax.rsqrt(jnp.mean(xf * xf, axis=-1, keepdims=True) + RMS_EPS)
        uu = dh * gn_ref[...]
        mu = jnp.mean(uu * xf, axis=-1, keepdims=True)
        dx_ref[...] = dxo_ref[...] + r * (uu - xf * (r * r * mu))
        dgn_ref[...] += jnp.sum(dh * xf * r, axis=0, keepdims=True)

    row = lambda w: pl.BlockSpec((tm, w), lambda i: (i, 0))
    vec = pl.BlockSpec((1, D), lambda i: (0, 0))
    return _call(body, name="mix_bwd_dx", grid=(T // tm,),
                 in_specs=[row(W_QKV), row(W_SSD), row(W_UV), _resident(win.shape), row(D), vec, row(D)],
                 out_specs=[row(D), vec],
                 out_shape=[jax.ShapeDtypeStruct((T, D), F32), jax.ShapeDtypeStruct((1, D), F32)],
                 sem=("arbitrary",))(dqkv, dsin, duv, win, x, gn, dxo)


def _lane_mask(e, width=128):
    return (lax.broadcasted_iota(jnp.int32, (1, width), 1) // HEAD) == e


def _band_mask(n):
    qi = lax.broadcasted_iota(jnp.int32, (CHUNK, 2 * CHUNK), 0)
    kj = lax.broadcasted_iota(jnp.int32, (CHUNK, 2 * CHUNK), 1)
    dist = qi + CHUNK - kj
    return (dist >= 0) & (dist <= CHUNK) & ((kj >= CHUNK) | (n > 0))


def _sub_rows(r, block, dil):
    if dil == 1:
        return pl.ds(pl.multiple_of(block * CHUNK, CHUNK), CHUNK)
    return pl.ds(r + dil * CHUNK * block, CHUNK, stride=dil)


def _attn_specs(T, dil):
    B, nb = T // SEQ, SEQ // (CHUNK * dil)
    once = dict(pipeline_mode=pl.Buffered(1))
    q_like = lambda col: pl.BlockSpec((CHUNK * dil, 128), lambda b, n, r: (b * nb + n, col), **(once if nb == 1 else {}))
    k_like = lambda col: pl.BlockSpec((SEQ, 128), lambda b, n, r: (b, col), **once)
    return B, nb, q_like, k_like


def _attn_fwd(qkv, dil):
    T = qkv.shape[0]
    B, nb, q_like, k_like = _attn_specs(T, dil)
    scale = HEAD ** -0.5

    def body(*refs):
        q_t, k_t, v_t, o_t, l_t = refs[0:3], refs[3:6], refs[6:9], refs[9:12], refs[12:15]
        n, r = pl.program_id(1), pl.program_id(2)
        mine = _sub_rows(r, 0, dil)
        cur, prv = _sub_rows(r, n, dil), _sub_rows(r, jnp.maximum(n - 1, 0), dil)
        mask = _band_mask(n)
        for t in range(3):
            qt = q_t[t][mine, :].astype(BF)
            kt = jnp.concatenate([k_t[t][prv, :], k_t[t][cur, :]], axis=0).astype(BF)
            vt = jnp.concatenate([v_t[t][prv, :], v_t[t][cur, :]], axis=0).astype(BF)
            o_pair = jnp.zeros((CHUNK, 128), F32)
            l_pair = jnp.zeros((CHUNK, 128), F32)
            for e in range(2):
                lm = _lane_mask(e)
                s = _dot_nt(jnp.where(lm, qt, jnp.zeros_like(qt)), kt) * scale
                s = jnp.where(mask, s, NEG)
                m = jnp.max(s, axis=-1, keepdims=True)
                p = jnp.exp(s - m)
                den = jnp.sum(p, axis=-1, keepdims=True)
                o = _dot(p.astype(BF), vt) / den
                o_pair = jnp.where(lm, o, o_pair)
                l_pair = jnp.where(lm, m + jnp.log(den), l_pair)
            o_t[t][mine, :] = o_pair
            l_t[t][mine, :] = l_pair

    out_spec = pl.BlockSpec((CHUNK * dil, 128), lambda b, n, r: (b * nb + n, 0))
    sh = jax.ShapeDtypeStruct((T, 128), F32)
    outs = _call(
        body, name=f"attn_fwd_d{dil}", grid=(B, nb, dil),
        in_specs=[q_like(t) for t in range(3)] + [k_like(3 + t) for t in range(3)] + [k_like(6 + t) for t in range(3)],
        out_specs=[out_spec] * 6, out_shape=[sh] * 6, sem=("parallel", "arbitrary", "arbitrary"))(*([qkv] * 9))
    return list(outs[0:3]), list(outs[3:6])


def _attn_combine(branches):
    T = branches[0][0][0].shape[0]
    tm = _tile(T, 512)

    def body(*refs):
        y_ref, l_ref = refs[-2:]
        for t in range(3):
            o = [refs[6 * i + t][...] for i in range(3)]
            a, b, c = [refs[6 * i + 3 + t][...] for i in range(3)]
            m = jnp.maximum(jnp.maximum(a, b), c)
            ea, eb, ec = jnp.exp(a - m), jnp.exp(b - m), jnp.exp(c - m)
            z = ea + eb + ec
            y_ref[:, 128 * t:128 * (t + 1)] = (ea * o[0] + eb * o[1] + ec * o[2]) / z
            l_ref[:, 128 * t:128 * (t + 1)] = m + jnp.log(z)

    tile = pl.BlockSpec((tm, 128), lambda i: (i, 0))
    row = pl.BlockSpec((tm, ATT_W), lambda i: (i, 0))
    sh = jax.ShapeDtypeStruct((T, ATT_W), F32)
    flat = [a for o_t, l_t in branches for a in (*o_t, *l_t)]
    return _call(body, name="attn_combine", grid=(T // tm,), in_specs=[tile] * 18, out_specs=[row, row],
                 out_shape=[sh, sh], sem=("parallel",))(*flat)


def _attn_bwd(qkv, do, out, lse, dil):
    T = qkv.shape[0]
    B, nb, q_like, k_like = _attn_specs(T, dil)
    scale = HEAD ** -0.5

    def body(*refs):
        q_t, k_t, v_t = refs[0:3], refs[3:6], refs[6:9]
        do_t, out_t, lse_t = refs[9:12], refs[12:15], refs[15:18]
        dq_t, dk_t, dv_t = refs[18:21], refs[21:24], refs[24:27]
        n, r = pl.program_id(1), pl.program_id(2)

        @pl.when((n == 0) & (r == 0))
        def _():
            for t in range(3):
                dk_t[t][...] = jnp.zeros_like(dk_t[t])
                dv_t[t][...] = jnp.zeros_like(dv_t[t])

        mine = _sub_rows(r, 0, dil)
        cur, prv = _sub_rows(r, n, dil), _sub_rows(r, jnp.maximum(n - 1, 0), dil)
        mask = _band_mask(n)
        for t in range(3):
            qt = q_t[t][mine, :].astype(BF)
            kt = jnp.concatenate([k_t[t][prv, :], k_t[t][cur, :]], axis=0).astype(BF)
            vt = jnp.concatenate([v_t[t][prv, :], v_t[t][cur, :]], axis=0).astype(BF)
            do_ = do_t[t][mine, :]
            dlt = do_ * out_t[t][mine, :]
            ls = lse_t[t][mine, :]
            dq_pair = jnp.zeros((CHUNK, 128), F32)
            dk_acc = jnp.zeros((2 * CHUNK, 128), F32)
            dv_acc = jnp.zeros((2 * CHUNK, 128), F32)
            for e in range(2):
                lm = _lane_mask(e)
                qm = jnp.where(lm, qt, jnp.zeros_like(qt))
                s = _dot_nt(qm, kt) * scale
                p = jnp.exp(jnp.where(mask, s - ls[:, HEAD * e:HEAD * e + 1], NEG))
                dom = jnp.where(lm, do_, 0.0).astype(BF)
                dv_acc += _dot_tn(p.astype(BF), dom)
                dp = _dot_nt(dom, vt)
                delta = jnp.sum(jnp.where(lm, dlt, 0.0), axis=-1, keepdims=True)
                ds = (p * (dp - delta) * scale).astype(BF)
                dq_pair += jnp.where(lm, _dot(ds, kt), 0.0)
                dk_acc += _dot_tn(ds, qm)
            dq_t[t][mine, :] = dq_pair
            dk_t[t][cur, :] = dk_t[t][cur, :] + dk_acc[CHUNK:]
            dk_t[t][prv, :] = dk_t[t][prv, :] + dk_acc[:CHUNK]
            dv_t[t][cur, :] = dv_t[t][cur, :] + dv_acc[CHUNK:]
            dv_t[t][prv, :] = dv_t[t][prv, :] + dv_acc[:CHUNK]

    q_out = pl.BlockSpec((CHUNK * dil, 128), lambda b, n, r: (b * nb + n, 0))
    k_out = pl.BlockSpec((SEQ, 128), lambda b, n, r: (b, 0))
    sh = jax.ShapeDtypeStruct((T, 128), F32)
    tiles = lambda: [q_like(t) for t in range(3)]
    return list(_call(
        body, name=f"attn_bwd_d{dil}", grid=(B, nb, dil),
        in_specs=tiles() + [k_like(3 + t) for t in range(3)] + [k_like(6 + t) for t in range(3)]
        + tiles() + tiles() + tiles(),
        out_specs=[q_out] * 3 + [k_out] * 6, out_shape=[sh] * 9,
        sem=("parallel", "arbitrary", "arbitrary"), vmem=ATTN_BWD_VMEM)(*([qkv] * 9 + [do] * 3 + [out] * 3 + [lse] * 3)))


def _sum_branches(parts):
    T = parts[0][0].shape[0]
    tm = _tile(T, 512)

    def body(*refs):
        o_ref = refs[-1]
        for c in range(9):
            acc = refs[c][...] + refs[9 + c][...] + refs[18 + c][...]
            o_ref[:, 128 * c:128 * (c + 1)] = acc.astype(BF)

    tile = pl.BlockSpec((tm, 128), lambda i: (i, 0))
    flat = [a for br in parts for a in br]
    return _call(body, name="attn_sum_branches", grid=(T // tm,), in_specs=[tile] * 27,
                 out_specs=pl.BlockSpec((tm, W_QKV), lambda i: (i, 0)),
                 out_shape=jax.ShapeDtypeStruct((T, W_QKV), BF), sem=("parallel",))(*flat)


def _silu(x):
    return x * _sigmoid(x)


def _dsilu(x):
    s = _sigmoid(x)
    return s * (1.0 + x * (1.0 - s))


def _log1p(u):
    return jnp.where(u < 0.01, u * (1.0 - u * (0.5 - u * (1.0 / 3.0))), jnp.log(1.0 + u))


def _softplus(x):
    return jnp.maximum(x, 0.0) + _log1p(jnp.exp(-jnp.abs(x)))


def _cumsum_rows(x, reverse=False):
    n = x.shape[0]
    rows = lax.broadcasted_iota(jnp.int32, x.shape, 0)
    k = 1
    while k < n:
        if reverse:
            x = x + jnp.where(rows < n - k, pltpu.roll(x, n - k, 0), 0.0)
        else:
            x = x + jnp.where(rows >= k, pltpu.roll(x, k, 0), 0.0)
        k *= 2
    return x


def _tri():
    r = lax.broadcasted_iota(jnp.int32, (CHUNK, CHUNK), 0)
    c = lax.broadcasted_iota(jnp.int32, (CHUNK, CHUNK), 1)
    return r >= c


def _row_mask(e):
    return (lax.broadcasted_iota(jnp.int32, (128, 1), 0) // HEAD) == e


def _first_lane(e):
    return lax.broadcasted_iota(jnp.int32, (1, 128), 1) == HEAD * e


def _ssd_pre(x_ref, halo_ref, first, cw_ref, cb_ref, dtb_ref, al_ref, ext):
    row = x_ref[...]
    z = row[:, SSD_CONV_DIM:SSD_CONV_DIM + SSD_W]
    u = row[:, SSD_CONV_DIM + SSD_W:] + dtb_ref[...]
    ext[0:8, :] = jnp.where(first, 0.0, halo_ref[:, 0:SSD_CONV_DIM])
    ext[8:8 + CHUNK, :] = row[:, 0:SSD_CONV_DIM]
    xc = cb_ref[...]
    for j in range(4):
        xc = xc + cw_ref[j:j + 1, :] * ext[pl.ds(5 + j, CHUNK), :]
    xa = _silu(xc)
    dt = _softplus(u)
    a = dt * (-jnp.exp(al_ref[...]))
    A = _cumsum_rows(a)
    return dict(z=z, u=u, xc=xc, xs=xa[:, 0:SSD_W], Bm=xa[:, SSD_W:SSD_W + 256], Cm=xa[:, SSD_W + 256:],
                dt=dt, a=a, A=A, AT=A.T, eA=jnp.exp(A), wdec=jnp.exp(A[CHUNK - 1:CHUNK, :] - A),
                dtot=jnp.exp(A[CHUNK - 1:CHUNK, :]))


def _ssd_y(p, hp_ref, dskip):
    tri = _tri()
    X = p["xs"] * p["dt"]
    Bb = [p["Bm"][:, 128 * g:128 * (g + 1)].astype(BF) for g in range(2)]
    Cb = [p["Cm"][:, 128 * g:128 * (g + 1)].astype(BF) for g in range(2)]
    CB = [_dot_nt(Cb[g], Bb[g]) for g in range(2)]
    tiles = []
    for t in range(3):
        sl = slice(128 * t, 128 * (t + 1))
        hpb = hp_ref[sl, :].astype(BF)
        acc = jnp.zeros((CHUNK, 128), F32)
        for e in range(2):
            h = 2 * t + e
            g, col = h // 3, HEAD * h
            lm = _lane_mask(e)
            L = jnp.exp(jnp.where(tri, p["A"][:, col:col + 1] - p["AT"][col:col + 1, :], NEG))
            yd = _dot((CB[g] * L).astype(BF), jnp.where(lm, X[:, sl], 0.0).astype(BF))
            yo = _dot_nt(Cb[g], hpb) * p["eA"][:, sl]
            acc = acc + jnp.where(lm, yd + yo, 0.0)
        tiles.append(acc)
    return jnp.concatenate(tiles, axis=1) + dskip * p["xs"], X, Bb, Cb, CB


def _group_stats(v):
    g0 = lax.broadcasted_iota(jnp.int32, (1, SSD_W), 1) < SSD_W // 2
    m0 = jnp.sum(jnp.where(g0, v, 0.0), axis=-1, keepdims=True) * (2.0 / SSD_W)
    m1 = jnp.sum(jnp.where(g0, 0.0, v), axis=-1, keepdims=True) * (2.0 / SSD_W)
    return jnp.where(g0, m0, m1)


def _ssd_specs(T, rev):
    B = T // SEQ

    def chunk(b, c):
        return b * N_CHUNK + (N_CHUNK - 1 - c if rev else c)

    row = pl.BlockSpec((CHUNK, W_SSD), lambda b, c: (chunk(b, c), 0))
    halo = pl.BlockSpec((8, W_SSD), lambda b, c: (jnp.maximum(chunk(b, c) * (CHUNK // 8) - 1, 0), 0))
    hp = pl.BlockSpec((None, SSD_W, SSD_STATE), lambda b, c: (chunk(b, c), 0, 0))
    y = pl.BlockSpec((CHUNK, SSD_W), lambda b, c: (chunk(b, c), 0))
    const = lambda r, w: pl.BlockSpec((r, w), lambda b, c: (0, 0))
    params = [const(4, SSD_CONV_DIM), const(1, SSD_CONV_DIM)] + [const(1, SSD_W)] * 4
    return B, row, halo, hp, y, const, params


def _ssd_fwd(sin, conv_w, conv_b, dtb, alog, dskip, norm_g):
    T = sin.shape[0]
    B, row, halo, hp, y, const, params = _ssd_specs(T, False)

    def body(x_ref, halo_ref, cw_ref, cb_ref, dtb_ref, al_ref, dk_ref, ng_ref, y_ref, hp_ref, ext, hst):
        c = pl.program_id(1)

        @pl.when(c == 0)
        def _():
            hst[...] = jnp.zeros_like(hst)

        p = _ssd_pre(x_ref, halo_ref, c == 0, cw_ref, cb_ref, dtb_ref, al_ref, ext)
        yv, X, Bb, Cb, CB = _ssd_y(p, hst, dk_ref[...])
        hp_ref[...] = hst[...]
        for t in range(3):
            sl = slice(128 * t, 128 * (t + 1))
            old = hst[sl, :]
            new = old
            for e in range(2):
                h = 2 * t + e
                g, col = h // 3, HEAD * h
                st = _dot_tn(jnp.where(_lane_mask(e), X[:, sl] * p["wdec"][:, sl], 0.0).astype(BF), Bb[g])
                new = jnp.where(_row_mask(e), old * p["dtot"][:, col:col + 1] + st, new)
            hst[sl, :] = new
        y2 = yv * _silu(p["z"])
        r = lax.rsqrt(_group_stats(y2 * y2) + RMS_EPS)
        y_ref[...] = y2 * r * ng_ref[...]

    return _call(body, name="ssd_fwd", grid=(B, N_CHUNK), in_specs=[row, halo] + params, out_specs=[y, hp],
                 out_shape=[jax.ShapeDtypeStruct((T, SSD_W), F32),
                            jax.ShapeDtypeStruct((T // CHUNK, SSD_W, SSD_STATE), F32)],
                 scratch=[pltpu.VMEM((8 + CHUNK, SSD_CONV_DIM), F32), pltpu.VMEM((SSD_W, SSD_STATE), F32)],
                 sem=("parallel", "arbitrary"))(sin, sin, conv_w, conv_b, dtb, alog, dskip, norm_g)


def _ssd_bwd(sin, hprev, dy3, conv_w, conv_b, dtb, alog, dskip, norm_g):
    T = sin.shape[0]
    B, row, halo, hp, y, const, params = _ssd_specs(T, True)

    def body(x_ref, halo_ref, hp_ref, dy_ref, cw_ref, cb_ref, dtb_ref, al_ref, dk_ref, ng_ref,
             dx_ref, dcw_ref, dcb_ref, dvec_ref, ext, ext2, dh):
        c = pl.program_id(1)

        @pl.when((pl.program_id(0) == 0) & (c == 0))
        def _():
            dcw_ref[...] = jnp.zeros_like(dcw_ref)
            dcb_ref[...] = jnp.zeros_like(dcb_ref)
            dvec_ref[...] = jnp.zeros_like(dvec_ref)

        @pl.when(c == 0)
        def _():
            dh[...] = jnp.zeros_like(dh)
            ext2[CHUNK:CHUNK + 8, :] = jnp.zeros((8, SSD_CONV_DIM), F32)

        p = _ssd_pre(x_ref, halo_ref, c == N_CHUNK - 1, cw_ref, cb_ref, dtb_ref, al_ref, ext)
        dskip_ = dk_ref[...]
        yv, X, Bb, Cb, CB = _ssd_y(p, hp_ref, dskip_)
        xs, z, A, AT = p["xs"], p["z"], p["A"], p["AT"]

        sz = _silu(z)
        y2 = yv * sz
        r = lax.rsqrt(_group_stats(y2 * y2) + RMS_EPS)
        dy3_ = dy_ref[...]
        uu = dy3_ * ng_ref[...]
        dy2 = r * (uu - y2 * (r * r * _group_stats(uu * y2)))
        dy = dy2 * sz
        dz = dy2 * yv * _dsilu(z)

        tri = _tri()
        rows = lax.broadcasted_iota(jnp.int32, (CHUNK, 1), 0)
        dG = [jnp.zeros((CHUNK, CHUNK), F32) for _ in range(2)]
        dB = [jnp.zeros((CHUNK, SSD_STATE), F32) for _ in range(2)]
        dC = [jnp.zeros((CHUNK, SSD_STATE), F32) for _ in range(2)]
        dX_t, dA_t, ddtx_t = [], [], []
        for t in range(3):
            sl = slice(128 * t, 128 * (t + 1))
            hp_t = hp_ref[sl, :]
            hpb = hp_t.astype(BF)
            dhc = dh[sl, :]
            dh_new = jnp.zeros((128, SSD_STATE), F32)
            dX = jnp.zeros((CHUNK, 128), F32)
            dA = jnp.zeros((CHUNK, 128), F32)
            ddtx = jnp.zeros((CHUNK, 128), F32)
            for e in range(2):
                h = 2 * t + e
                g, col = h // 3, HEAD * h
                lm, rm, fl = _lane_mask(e), _row_mask(e), _first_lane(e)
                L = jnp.exp(jnp.where(tri, A[:, col:col + 1] - AT[col:col + 1, :], NEG))
                Mf = CB[g] * L
                Xm = jnp.where(lm, X[:, sl], 0.0)
                Xmb = Xm.astype(BF)
                dyh = jnp.where(lm, dy[:, sl], 0.0)
                dyb = dyh.astype(BF)
                dXh = _dot_tn(Mf.astype(BF), dyb)
                dM = jnp.where(tri, _dot_nt(dyb, Xmb), 0.0)
                Wm = dM * Mf
                dAc = jnp.sum(Wm, axis=-1, keepdims=True) - jnp.sum(Wm.T, axis=-1, keepdims=True)
                dG[g] = dG[g] + dM * L
                eAt = p["eA"][:, sl]
                yo = _dot_nt(Cb[g], hpb)
                dyo = (dyh * eAt).astype(BF)
                dC[g] = dC[g] + _dot(dyo, hpb)
                dh_new = dh_new + _dot_tn(dyo, Cb[g])
                dAc = dAc + jnp.sum(dyh * yo * eAt, axis=-1, keepdims=True)
                dHn = jnp.where(rm, dhc, 0.0)
                dHnb = dHn.astype(BF)
                dec = p["dtot"][:, col:col + 1]
                dh_new = dh_new + dec * dHn
                Z = _dot_nt(Bb[g], dHnb)
                wt = p["wdec"][:, sl]
                xi = jnp.sum(Xm * Z, axis=-1, keepdims=True) * p["wdec"][:, col:col + 1]
                dXh = dXh + wt * Z
                dB[g] = dB[g] + _dot(jnp.where(lm, X[:, sl] * wt, 0.0).astype(BF), dHnb)
                dAtot = jnp.sum(xi, axis=0, keepdims=True) + dec * jnp.sum(
                    jnp.sum(dHn * hp_t, axis=-1, keepdims=True), axis=0, keepdims=True)
                dAc = dAc - xi + jnp.where(rows == CHUNK - 1, dAtot, 0.0)
                dA = dA + jnp.where(fl, dAc, 0.0)
                dX = dX + dXh
                ddtx = ddtx + jnp.where(fl, jnp.sum(dXh * xs[:, sl], axis=-1, keepdims=True), 0.0)
            dh[sl, :] = dh_new
            dX_t.append(dX)
            dA_t.append(dA)
            ddtx_t.append(ddtx)
        for g in range(2):
            dGb = dG[g].astype(BF)
            dC[g] = dC[g] + _dot(dGb, Bb[g])
            dB[g] = dB[g] + _dot_tn(dGb, Cb[g])
        dXf = jnp.concatenate(dX_t, axis=1)
        da = _cumsum_rows(jnp.concatenate(dA_t, axis=1), reverse=True)
        ddt = da * (-jnp.exp(al_ref[...])) + jnp.concatenate(ddtx_t, axis=1)
        du = ddt * _sigmoid(p["u"])
        dxs = dXf * p["dt"] + dskip_ * dy
        dxc = jnp.concatenate([dxs, dB[0], dB[1], dC[0], dC[1]], axis=1) * _dsilu(p["xc"])
        ext2[0:CHUNK, :] = dxc
        dxbc = jnp.zeros((CHUNK, SSD_CONV_DIM), F32)
        for j in range(4):
            dxbc = dxbc + cw_ref[j:j + 1, :] * ext2[pl.ds(3 - j, CHUNK), :]
            dcw_ref[j:j + 1, :] += jnp.sum(dxc * ext[pl.ds(5 + j, CHUNK), :], axis=0, keepdims=True)
        ext2[CHUNK:CHUNK + 8, :] = dxc[0:8, :]
        dcb_ref[...] += jnp.sum(dxc, axis=0, keepdims=True)
        dvec_ref[0:1, :] += jnp.sum(du, axis=0, keepdims=True)
        dvec_ref[1:2, :] += jnp.sum(da * p["a"], axis=0, keepdims=True)
        dvec_ref[2:3, :] += jnp.sum(dy * xs, axis=0, keepdims=True)
        dvec_ref[3:4, :] += jnp.sum(dy3_ * y2 * r, axis=0, keepdims=True)
        dx_ref[...] = jnp.concatenate([dxbc, dz, du], axis=1).astype(BF)

    return _call(body, name="ssd_bwd", grid=(B, N_CHUNK), in_specs=[row, halo, hp, y] + params,
                 out_specs=[row, const(4, SSD_CONV_DIM), const(1, SSD_CONV_DIM), const(8, SSD_W)],
                 out_shape=[jax.ShapeDtypeStruct((T, W_SSD), BF), jax.ShapeDtypeStruct((4, SSD_CONV_DIM), F32),
                            jax.ShapeDtypeStruct((1, SSD_CONV_DIM), F32), jax.ShapeDtypeStruct((8, SSD_W), F32)],
                 scratch=[pltpu.VMEM((8 + CHUNK, SSD_CONV_DIM), F32), pltpu.VMEM((8 + CHUNK, SSD_CONV_DIM), F32),
                          pltpu.VMEM((SSD_W, SSD_STATE), F32)],
                 sem=("arbitrary", "arbitrary"))(sin, sin, hprev, dy3, conv_w, conv_b, dtb, alog, dskip, norm_g)


def _sgu_core(uv_ref, g_ref, b_ref, w_ref, bias_ref):
    x = uv_ref[...]
    cdf = 0.5 * (1.0 + lax.erf(x * (2.0 ** -0.5)))
    ge = x * cdf
    dge = cdf + x * jnp.exp(-0.5 * x * x) * ((2.0 * math.pi) ** -0.5)
    u, v = ge[:, 0:SGU_W], ge[:, SGU_W:]
    vc = v - jnp.mean(v, axis=-1, keepdims=True)
    rstd = lax.rsqrt(jnp.mean(vc * vc, axis=-1, keepdims=True) + LN_EPS)
    vhat = vc * rstd
    vn = vhat * g_ref[...] + b_ref[...]
    tri = _tri()
    wc = [jnp.where(tri, w_ref[gi], 0.0).astype(BF) for gi in range(4)]
    vm = [jnp.where(_lane_mask(gi % 2), vn[:, 128 * (gi // 2):128 * (gi // 2 + 1)], 0.0).astype(BF) for gi in range(4)]
    mixed = jnp.concatenate([_dot(wc[2 * t], vm[2 * t]) + _dot(wc[2 * t + 1], vm[2 * t + 1]) for t in range(2)],
                            axis=1) + bias_ref[...]
    return dict(dge=dge, u=u, rstd=rstd, vhat=vhat, wc=wc, vm=vm, mixed=mixed)


def _sgu_specs():
    vec = pl.BlockSpec((1, SGU_W), lambda i: (0, 0))
    return [pl.BlockSpec((CHUNK, W_UV), lambda i: (i, 0)), vec, vec,
            pl.BlockSpec((4, CHUNK, CHUNK), lambda i: (0, 0, 0)), pl.BlockSpec((CHUNK, SGU_W), lambda i: (0, 0))]


def _sgu_fwd(uv, ln_g, ln_b, w, bias):
    T = uv.shape[0]

    def body(uv_ref, g_ref, b_ref, w_ref, bias_ref, y_ref):
        s = _sgu_core(uv_ref, g_ref, b_ref, w_ref, bias_ref)
        y_ref[...] = s["u"] * s["mixed"]

    return _call(body, name="sgu_fwd", grid=(T // CHUNK,), in_specs=_sgu_specs(),
                 out_specs=pl.BlockSpec((CHUNK, SGU_W), lambda i: (i, 0)),
                 out_shape=jax.ShapeDtypeStruct((T, SGU_W), F32), sem=("parallel",))(uv, ln_g, ln_b, w, bias)


def _sgu_bwd(uv, dy, ln_g, ln_b, w, bias):
    T = uv.shape[0]

    def body(uv_ref, dy_ref, g_ref, b_ref, w_ref, bias_ref, dx_ref, dw_ref, dbias_ref, dln_ref):
        @pl.when(pl.program_id(0) == 0)
        def _():
            dw_ref[...] = jnp.zeros_like(dw_ref)
            dbias_ref[...] = jnp.zeros_like(dbias_ref)
            dln_ref[...] = jnp.zeros_like(dln_ref)

        s = _sgu_core(uv_ref, g_ref, b_ref, w_ref, bias_ref)
        dy_ = dy_ref[...]
        du = dy_ * s["mixed"]
        dmix = dy_ * s["u"]
        dbias_ref[...] += dmix
        tri = _tri()
        dvn_t = []
        for t in range(2):
            acc = jnp.zeros((CHUNK, 128), F32)
            for e in range(2):
                gi = 2 * t + e
                dmg = jnp.where(_lane_mask(e), dmix[:, 128 * t:128 * (t + 1)], 0.0).astype(BF)
                acc = acc + _dot_tn(s["wc"][gi], dmg)
                dw_ref[gi] += jnp.where(tri, _dot_nt(dmg, s["vm"][gi]), 0.0)
            dvn_t.append(acc)
        dvn = jnp.concatenate(dvn_t, axis=1)
        dln_ref[0:1, :] += jnp.sum(dvn * s["vhat"], axis=0, keepdims=True)
        dln_ref[1:2, :] += jnp.sum(dvn, axis=0, keepdims=True)
        dvh = dvn * g_ref[...]
        dv = s["rstd"] * (dvh - jnp.mean(dvh, axis=-1, keepdims=True)
                          - s["vhat"] * jnp.mean(dvh * s["vhat"], axis=-1, keepdims=True))
        dx_ref[...] = (jnp.concatenate([du, dv], axis=1) * s["dge"]).astype(BF)

    ins = _sgu_specs()
    return _call(body, name="sgu_bwd", grid=(T // CHUNK,),
                 in_specs=[ins[0], pl.BlockSpec((CHUNK, SGU_W), lambda i: (i, 0))] + ins[1:],
                 out_specs=[pl.BlockSpec((CHUNK, W_UV), lambda i: (i, 0)),
                            pl.BlockSpec((4, CHUNK, CHUNK), lambda i: (0, 0, 0)),
                            pl.BlockSpec((CHUNK, SGU_W), lambda i: (0, 0)), pl.BlockSpec((8, SGU_W), lambda i: (0, 0))],
                 out_shape=[jax.ShapeDtypeStruct((T, W_UV), BF), jax.ShapeDtypeStruct((4, CHUNK, CHUNK), F32),
                            jax.ShapeDtypeStruct((CHUNK, SGU_W), F32), jax.ShapeDtypeStruct((8, SGU_W), F32)],
                 sem=("arbitrary",))(uv, dy, ln_g, ln_b, w, bias)


def _adamw(w, g, m, v):
    R, C = w.shape
    tr = R

    def body(w_ref, g_ref, m_ref, v_ref, d_ref, nm_ref, nv_ref):
        g_ = g_ref[...]
        m2 = ADAM_B1 * m_ref[...] + (1.0 - ADAM_B1) * g_
        v2 = ADAM_B2 * v_ref[...] + (1.0 - ADAM_B2) * (g_ * g_)
        m_hat = m2 / (1.0 - ADAM_B1 ** ADAM_STEP)
        v_hat = v2 / (1.0 - ADAM_B2 ** ADAM_STEP)
        d_ref[...] = -ADAM_LR * (m_hat / (jnp.sqrt(v_hat) + ADAM_EPS) + ADAM_WD * w_ref[...])
        nm_ref[...] = m2
        nv_ref[...] = v2

    blk = pl.BlockSpec((tr, C), lambda i: (i, 0))
    sh = jax.ShapeDtypeStruct((R, C), F32)
    return _call(body, name="adamw", grid=(R // tr,), in_specs=[blk] * 4, out_specs=[blk] * 3,
                 out_shape=[sh] * 3, sem=("parallel",))(w, g, m, v)


def _adamw_pair(w, g0, g1, m, v, dep):
    L, R, C = w.shape
    tr = _tile(R, 256 if C <= 1024 else 64)

    def body(w_ref, g0_ref, g1_ref, m_ref, v_ref, dep_ref, d_ref, nm_ref, nv_ref, og_ref):
        g_ = jnp.where(pl.program_id(0) == 0, g0_ref[...], g1_ref[...])
        m2 = ADAM_B1 * m_ref[...] + (1.0 - ADAM_B1) * g_
        v2 = ADAM_B2 * v_ref[...] + (1.0 - ADAM_B2) * (g_ * g_)
        m_hat = m2 / (1.0 - ADAM_B1 ** ADAM_STEP)
        v_hat = v2 / (1.0 - ADAM_B2 ** ADAM_STEP)
        d_ref[...] = -ADAM_LR * (m_hat / (jnp.sqrt(v_hat) + ADAM_EPS) + ADAM_WD * w_ref[...])
        nm_ref[...] = m2
        nv_ref[...] = v2
        og_ref[...] = g_

    lay = pl.BlockSpec((None, tr, C), lambda l, i: (l, i, 0))
    one = lambda k: pl.BlockSpec((tr, C), lambda l, i: (jnp.where(l == k, i, 0), 0))
    return _call(body, name="adamw_pair", grid=(L, R // tr),
                 in_specs=[lay, one(0), one(1), lay, lay, pl.BlockSpec((8, 128), lambda l, i: (0, 0))],
                 out_specs=[lay] * 4,
                 out_shape=[jax.ShapeDtypeStruct((L, R, C), F32)] * 4,
                 sem=("parallel", "parallel"))(w, g0, g1, m, v, dep)


def _row_steps(rows):
    return 2 if rows % 32 == 0 else 1


def _pair_add(gbuf, rsib, c):
    NS, _, R, C = gbuf.shape
    n = _row_steps(R)
    tr = R // n

    def body(c_ref, a_ref, b_ref, o_ref):
        o_ref[...] = (a_ref[...] + b_ref[...]).astype(BF)

    blk = pl.BlockSpec((None, tr, C), lambda j, i, c_ref: (j, i, 0))
    return pl.pallas_call(
        body, name="rs_pair_add",
        grid_spec=pltpu.PrefetchScalarGridSpec(
            num_scalar_prefetch=1, grid=(NS, n),
            in_specs=[pl.BlockSpec((None, None, tr, C), lambda j, i, c_ref: (j, c_ref[0], i, 0)), blk],
            out_specs=blk),
        out_shape=jax.ShapeDtypeStruct((NS, R, C), BF),
        compiler_params=pltpu.CompilerParams(dimension_semantics=("parallel", "parallel")),
    )(jnp.reshape(c, (1,)).astype(jnp.int32), gbuf, rsib)


def _chip_sum(pair, recv, me, c):
    NS, R, C = pair.shape
    n = _row_steps(R)
    tr = R // n

    def body(s_ref, own_ref, p_ref, o_ref):
        p = [jnp.where(s_ref[0] == j, own_ref[...], p_ref[j]).astype(F32) for j in range(4)]
        o_ref[...] = ((p[0] + p[1]) + p[2]) + p[3]

    return pl.pallas_call(
        body, name="rs_chip_sum",
        grid_spec=pltpu.PrefetchScalarGridSpec(
            num_scalar_prefetch=1, grid=(n,),
            in_specs=[pl.BlockSpec((None, tr, C), lambda i, s: (s[0], i, 0)),
                      pl.BlockSpec((NS, tr, C), lambda i, s: (0, i, 0))],
            out_specs=pl.BlockSpec((None, tr, C), lambda i, s: (s[1], i, 0))),
        out_shape=jax.ShapeDtypeStruct((2, R, C), F32),
        compiler_params=pltpu.CompilerParams(dimension_semantics=("parallel",)),
    )(jnp.stack([me, c]).astype(jnp.int32), pair, recv)


MESH = pl.DeviceIdType.MESH
ANY = pl.BlockSpec(memory_space=pl.ANY)


def _place():
    x, y, c = lax.axis_index("x"), lax.axis_index("y"), lax.axis_index("c")
    return x, y, c, [(1 - x, y), (x, 1 - y), (1 - x, 1 - y)]


HBM = pl.BlockSpec(memory_space=pltpu.HBM)
SEM = pl.BlockSpec(memory_space=pltpu.SEMAPHORE)
EFFECT = pltpu.SideEffectType.DATAFLOW_SIDE_EFFECTING


class _Split:
    def __init__(self, tag, arrays, copies, n_copies, after=()):
        self.tag, self.copies, k = tag, copies, len(arrays)

        def body(*refs):
            sems = k + len(after)
            for cp in copies(refs[:k], refs[sems], refs[sems + 1]):
                cp.start()
            refs[-1][...] = jnp.zeros_like(refs[-1])

        out = pl.pallas_call(
            body, name=tag + "_start",
            out_shape=(pltpu.SemaphoreType.DMA((n_copies,)), pltpu.SemaphoreType.DMA((n_copies,)),
                       *[pltpu.HBM(a.shape, a.dtype) for a in arrays], jax.ShapeDtypeStruct((8, 128), F32)),
            in_specs=[HBM] * k + [ANY] * len(after),
            out_specs=(SEM, SEM, *[HBM] * k, pl.BlockSpec(memory_space=pltpu.VMEM)),
            input_output_aliases={i: 2 + i for i in range(k)},
            compiler_params=pltpu.CompilerParams(has_side_effects=EFFECT),
        )(*[pltpu.with_memory_space_constraint(a, pltpu.HBM) for a in arrays], *after)
        self.send, self.recv, self.arrays, self.token_array = out[0], out[1], list(out[2:2 + k]), out[-1]
        self.token = self.token_array[0, 0]

    def wait(self, after):
        k, copies = len(self.arrays), self.copies
        after = list(after) if isinstance(after, (list, tuple)) else [after]

        def body(*refs):
            for cp in copies(refs[:k], refs[k], refs[k + 1]):
                cp.wait_send()
                cp.wait_recv()

        return list(pl.pallas_call(
            body, name=self.tag + "_wait", out_shape=tuple(pltpu.HBM(a.shape, a.dtype) for a in self.arrays),
            in_specs=[HBM] * k + [SEM, SEM] + [ANY] * len(after), out_specs=tuple([HBM] * k),
            input_output_aliases={i: i for i in range(k)},
            compiler_params=pltpu.CompilerParams(has_side_effects=EFFECT),
        )(*self.arrays, self.send, self.recv, *after))


def _landing_zones(arrs):
    me = 2 * lax.axis_index("x") + lax.axis_index("y")
    return [lax.dynamic_update_index_in_dim(lax.empty((4,) + a.shape, a.dtype), a, me, 0) for a in arrs]


def _gather_start(arrs, lands, tag, after=()):
    n = len(arrs)

    def copies(refs, send, recv):
        x, y, c, chips = _place()
        return [pltpu.make_async_remote_copy(
            src_ref=refs[k], dst_ref=refs[n + k].at[2 * x + y], send_sem=send.at[3 * k + r],
            recv_sem=recv.at[3 * k + r], device_id=(px, py, c), device_id_type=MESH)
            for k in range(n) for r, (px, py) in enumerate(chips)]

    return _Split("gather_" + tag, list(arrs) + lands, copies, 3 * n, after)


def _gather_halves_start(arrs, tag):
    n = len(arrs)
    lands = _landing_zones(arrs)

    def copies(refs, send, recv):
        x, y, c, chips = _place()
        return [pltpu.make_async_remote_copy(
            src_ref=refs[k].at[c], dst_ref=refs[n + k].at[2 * x + y, c], send_sem=send.at[3 * k + r],
            recv_sem=recv.at[3 * k + r], device_id=(px, py, c), device_id_type=MESH)
            for k in range(n) for r, (px, py) in enumerate(chips)]

    return _Split("gather_" + tag, list(arrs) + lands, copies, 3 * n)


def _gather_halves_finish(lands, tag):
    n = len(lands)

    def copies(refs, send, recv):
        x, y, c, chips = _place()
        return [pltpu.make_async_remote_copy(
            src_ref=refs[k].at[2 * px + py, c], dst_ref=refs[k].at[2 * px + py, c], send_sem=send.at[3 * k + r],
            recv_sem=recv.at[3 * k + r], device_id=(x, y, 1 - c), device_id_type=MESH)
            for k in range(n) for r, (px, py) in enumerate(chips)]

    return _Split("gather_pass_" + tag, list(lands), copies, 3 * n)


def _part_sibling(gbufs):
    n = len(gbufs)

    def copies(refs, send, recv, off):
        x, y, c, _ = _place()
        return [pltpu.make_async_remote_copy(
            src_ref=refs[k].at[j, 1 - c], dst_ref=refs[n + k].at[j], send_sem=send.at[off + 4 * k + j],
            recv_sem=recv.at[off + 4 * k + j], device_id=(x, y, 1 - c), device_id_type=MESH)
            for k in range(n) for j in range(4)]

    return list(gbufs) + [lax.empty((4,) + g.shape[2:], g.dtype) for g in gbufs], 4 * n, copies


def _part_chips(pbufs):
    n = len(pbufs)

    def copies(refs, send, recv, off):
        x, y, c, chips = _place()
        return [pltpu.make_async_remote_copy(
            src_ref=refs[k].at[2 * px + py], dst_ref=refs[n + k].at[2 * x + y], send_sem=send.at[off + 3 * k + r],
            recv_sem=recv.at[off + 3 * k + r], device_id=(px, py, c), device_id_type=MESH)
            for k in range(n) for r, (px, py) in enumerate(chips)]

    return list(pbufs) + [lax.empty(p.shape, p.dtype) for p in pbufs], 3 * n, copies


def _part_join(fulls):
    def copies(refs, send, recv, off):
        x, y, c, _ = _place()
        return [pltpu.make_async_remote_copy(
            src_ref=refs[k].at[c], dst_ref=refs[k].at[c], send_sem=send.at[off + k], recv_sem=recv.at[off + k],
            device_id=(x, y, 1 - c), device_id_type=MESH) for k in range(len(fulls))]

    return list(fulls), len(fulls), copies


def _start_parts(parts, tag):
    arrays, spans, total = [], [], 0
    for arrs, n_copies, fn in parts:
        spans.append((len(arrays), len(arrs), total, fn))
        arrays += arrs
        total += n_copies

    def copies(refs, send, recv):
        return [cp for a0, na, off, fn in spans for cp in fn(refs[a0:a0 + na], send, recv, off)]

    op = _Split(tag, arrays, copies, total)
    op.spans = [(a0, na) for a0, na, _, _ in spans]
    return op


def _all_reduce_small(v):
    R, C = v.shape

    def body(v_ref, o_ref, g_ref, send, recv, loc):
        x, y, c, chips = _place()
        me, sibling = (x, y, c), (x, y, 1 - c)

        def rows(px, py, pc):
            return g_ref.at[4 * px + 2 * py + pc]

        def copy(k, block, to, src=None):
            return pltpu.make_async_remote_copy(
                src_ref=rows(*block) if src is None else src, dst_ref=rows(*block),
                send_sem=send.at[k], recv_sem=recv.at[k], device_id=to, device_id_type=MESH)

        mine = pltpu.make_async_copy(v_ref, rows(*me), loc)
        mine.start()
        first = [copy(0, me, sibling, src=v_ref)]
        first += [copy(1 + j, me, (*chip, c), src=v_ref) for j, chip in enumerate(chips)]
        for cp in first:
            cp.start()
        passed = [copy(4 + j, (*chip, c), sibling) for j, chip in enumerate(chips)]
        for j, chip in enumerate(chips):
            copy(1 + j, (*chip, c), me).wait_recv()
            passed[j].start()
        copy(0, sibling, me).wait_recv()
        for j, chip in enumerate(chips):
            copy(4 + j, (*chip, 1 - c), me).wait_recv()
        for cp in first + passed:
            cp.wait_send()
        mine.wait()
        acc = g_ref[0]
        for d in range(1, 8):
            acc = acc + g_ref[d]
        o_ref[...] = acc

    vm = pl.BlockSpec(memory_space=pltpu.VMEM)
    return pl.pallas_call(
        body, name="all_reduce_small", in_specs=[vm], out_specs=[vm, vm],
        out_shape=[jax.ShapeDtypeStruct((R, C), F32), jax.ShapeDtypeStruct((8, R, C), F32)],
        scratch_shapes=[pltpu.SemaphoreType.DMA((7,)), pltpu.SemaphoreType.DMA((7,)), pltpu.SemaphoreType.DMA],
    )(v)[0]


WEIGHTS = ['ffn1_norm', 'ffn1_w_gate', 'ffn1_w_up', 'ffn1_w_down', 'mix_norm', 'w_in', 'conv_w', 'conv_b', 'dt_bias',
           'a_log', 'd_skip', 'ssd_norm', 'sgu_ln_g', 'sgu_ln_b', 'sgu_w', 'sgu_b', 'w_out', 'ffn2_norm',
           'ffn2_w_gate', 'ffn2_w_up', 'ffn2_w_down', 'final_norm']
SHARDED = ['ffn1_w_gate', 'ffn1_w_up', 'ffn1_w_down', 'w_in', 'conv_w', 'w_out', 'ffn2_w_gate', 'ffn2_w_up',
           'ffn2_w_down']
SMALL = [n for n in WEIGHTS if n not in SHARDED]
GROUPS = [("ffn1", ["ffn1_w_gate", "ffn1_w_up", "ffn1_w_down"]), ("mix", ["w_in", "conv_w", "w_out"]),
          ("ffn2", ["ffn2_w_gate", "ffn2_w_up", "ffn2_w_down"])]
TRANSPOSED = ("ffn1_w_gate", "ffn1_w_up", "ffn2_w_gate", "ffn2_w_up")
DEPTH = 2


def _pack_w_in(w):
    return jnp.concatenate([w[..., 0:1152], w[..., 1536:2432], w[..., 1152:1536],
                            jnp.repeat(w[..., 2432:2438], HEAD, axis=-1), w[..., 2438:2950]], axis=-1)


def _unpack_w_in(dq, ds, du):
    return jnp.concatenate([dq, ds[:, 896:1280], ds[:, 0:896], ds[:, 1280::HEAD], du], axis=-1)


def _ffn_fwd(x, g, wg, wu, wd):
    xo, hb, S1, S2, A = _ffn_fwd_k(x, g, wg, wu, wd)
    return xo, (x, hb, S1, S2, A)


def _ffn_bwd_weights(dxo, saved, wd):
    x, hb, S1, S2, A = saved
    dG, dU, dyb = _ffn_bwd_act(dxo, S1, S2, wd)
    return (dG, dU), _ffn_bwd_k2(hb, dyb, A, dG, dU)


def _ffn_bwd_input(dxo, saved, mids, g, wg, wu):
    return _ffn_bwd_dx(mids[0], mids[1], wg, wu, saved[0], g, dxo)


def _mix_fwd(x, P):
    hb, qkv, sin, uv = _mix_proj(x, P["mix_norm"], P["w_in"])
    y_att, lse = _attn_combine([_attn_fwd(qkv, d) for d in DILATIONS])
    y_ssd, hprev = _ssd_fwd(sin, *P["ssd"])
    y_sgu = _sgu_fwd(uv, *P["sgu"])
    ycat = jnp.concatenate([y_att, y_ssd, y_sgu], axis=1).astype(BF)
    return _mm_nn(ycat, P["w_out"], res=x), (x, hb, qkv, sin, uv, y_att, lse, hprev, ycat)


def _mix_bwd_weights(dxo, saved, P):
    x, hb, qkv, sin, uv, y_att, lse, hprev, ycat = saved
    dy_att, dy_ssd, dy_sgu = _mix_bwd_dy(dxo, P["w_out"])
    dwout = _mm_tn(ycat, dxo)
    dqkv = _sum_branches([_attn_bwd(qkv, dy_att, y_att, lse, d) for d in DILATIONS])
    dsin, dcw, dcb, dvec = _ssd_bwd(sin, hprev, dy_ssd, *P["ssd"])
    duv, dsw, dsbias, dln = _sgu_bwd(uv, dy_sgu, *P["sgu"])
    dwin = _unpack_w_in(_mm_tn(hb, dqkv), _mm_tn(hb, dsin), _mm_tn(hb, duv))
    grads = dict(
        w_in=dwin, conv_w=dcw, conv_b=dcb[0], dt_bias=dvec[0, ::HEAD], a_log=dvec[1, ::HEAD],
        d_skip=jnp.sum(dvec[2].reshape(6, HEAD), axis=-1), ssd_norm=dvec[3], sgu_ln_g=dln[0], sgu_ln_b=dln[1],
        sgu_w=dsw, sgu_b=jnp.sum(dsbias.reshape(CHUNK, 4, HEAD), axis=-1).T, w_out=dwout)
    return (dqkv, dsin, duv), grads


def _mix_bwd_input(dxo, saved, mids, P):
    return _mix_bwd_dx(*mids, P["w_in"], saved[0], P["mix_norm"], dxo)


def _halved(g):
    rows = g.size // g.shape[-1]
    return g.reshape(4, 2, rows // 8, g.shape[-1])


def kernel(x, ffn1_norm, ffn1_w_gate, ffn1_w_up, ffn1_w_down, mix_norm, w_in, conv_w, conv_b, dt_bias, a_log, d_skip, ssd_norm, sgu_ln_g, sgu_ln_b, sgu_w, sgu_b, w_out, ffn2_norm, ffn2_w_gate, ffn2_w_up, ffn2_w_down, final_norm, loss_target, m_ffn1_norm, m_ffn1_w_gate, m_ffn1_w_up, m_ffn1_w_down, m_mix_norm, m_w_in, m_conv_w, m_conv_b, m_dt_bias, m_a_log, m_d_skip, m_ssd_norm, m_sgu_ln_g, m_sgu_ln_b, m_sgu_w, m_sgu_b, m_w_out, m_ffn2_norm, m_ffn2_w_gate, m_ffn2_w_up, m_ffn2_w_down, m_final_norm, v_ffn1_norm, v_ffn1_w_gate, v_ffn1_w_up, v_ffn1_w_down, v_mix_norm, v_w_in, v_conv_w, v_conv_b, v_dt_bias, v_a_log, v_d_skip, v_ssd_norm, v_sgu_ln_g, v_sgu_ln_b, v_sgu_w, v_sgu_b, v_w_out, v_ffn2_norm, v_ffn2_w_gate, v_ffn2_w_up, v_ffn2_w_down, v_final_norm):
    given = dict(x=x, ffn1_norm=ffn1_norm, ffn1_w_gate=ffn1_w_gate, ffn1_w_up=ffn1_w_up, ffn1_w_down=ffn1_w_down, mix_norm=mix_norm, w_in=w_in, conv_w=conv_w, conv_b=conv_b, dt_bias=dt_bias, a_log=a_log, d_skip=d_skip, ssd_norm=ssd_norm, sgu_ln_g=sgu_ln_g, sgu_ln_b=sgu_ln_b, sgu_w=sgu_w, sgu_b=sgu_b, w_out=w_out, ffn2_norm=ffn2_norm, ffn2_w_gate=ffn2_w_gate, ffn2_w_up=ffn2_w_up, ffn2_w_down=ffn2_w_down, final_norm=final_norm, loss_target=loss_target, m_ffn1_norm=m_ffn1_norm, m_ffn1_w_gate=m_ffn1_w_gate, m_ffn1_w_up=m_ffn1_w_up, m_ffn1_w_down=m_ffn1_w_down, m_mix_norm=m_mix_norm, m_w_in=m_w_in, m_conv_w=m_conv_w, m_conv_b=m_conv_b, m_dt_bias=m_dt_bias, m_a_log=m_a_log, m_d_skip=m_d_skip, m_ssd_norm=m_ssd_norm, m_sgu_ln_g=m_sgu_ln_g, m_sgu_ln_b=m_sgu_ln_b, m_sgu_w=m_sgu_w, m_sgu_b=m_sgu_b, m_w_out=m_w_out, m_ffn2_norm=m_ffn2_norm, m_ffn2_w_gate=m_ffn2_w_gate, m_ffn2_w_up=m_ffn2_w_up, m_ffn2_w_down=m_ffn2_w_down, m_final_norm=m_final_norm, v_ffn1_norm=v_ffn1_norm, v_ffn1_w_gate=v_ffn1_w_gate, v_ffn1_w_up=v_ffn1_w_up, v_ffn1_w_down=v_ffn1_w_down, v_mix_norm=v_mix_norm, v_w_in=v_w_in, v_conv_w=v_conv_w, v_conv_b=v_conv_b, v_dt_bias=v_dt_bias, v_a_log=v_a_log, v_d_skip=v_d_skip, v_ssd_norm=v_ssd_norm, v_sgu_ln_g=v_sgu_ln_g, v_sgu_ln_b=v_sgu_ln_b, v_sgu_w=v_sgu_w, v_sgu_b=v_sgu_b, v_w_out=v_w_out, v_ffn2_norm=v_ffn2_norm, v_ffn2_w_gate=v_ffn2_w_gate, v_ffn2_w_up=v_ffn2_w_up, v_ffn2_w_down=v_ffn2_w_down, v_final_norm=v_final_norm)
    T = given["x"].shape[0] * given["x"].shape[1]
    D = given["x"].shape[2]
    x0 = given["x"].reshape(T, D)
    tgt = given["loss_target"].reshape(T, D)
    c = lax.axis_index("c")

    bf = {n: given[n].astype(BF) for n in SHARDED if n not in ("w_in", "conv_w")}
    bf["w_in"] = _pack_w_in(given["w_in"]).astype(BF)
    bf["conv_w"] = given["conv_w"]
    first_key = (0, GROUPS[0][0])
    first = [bf[n][0].reshape((2, bf[n].shape[1] // 2) + bf[n].shape[2:]) for n in GROUPS[0][1]]
    gathers = {first_key: _gather_halves_start(first, "l0_" + GROUPS[0][0])}
    later = {(i, gname): [bf[n][i] for n in names] for i in range(DEPTH) for gname, names in GROUPS
             if (i, gname) != first_key}
    zones = {key: _landing_zones(arrs) for key, arrs in later.items()}

    def gathered(i, gname, after):
        if (i, gname) != first_key:
            return gathers[(i, gname)].wait(after)[3:]
        got = gathers[first_key].wait([after] + [z for zs in zones.values() for z in zs])[3:]
        got = _gather_halves_finish(got, "l0_" + gname).wait(after)
        prev = got[0]
        for key, arrs in later.items():
            gathers[key] = _gather_start(arrs, zones[key], f"l{key[0]}_{key[1]}", after=[prev])
            prev = gathers[key].token_array
        return [z.reshape((4, 2 * z.shape[2]) + z.shape[3:]) for z in got]

    def mix_params(i, got):
        win = got[0].reshape(D, W_QKV + W_SSD + W_UV)
        rep = lambda v: jnp.repeat(v, HEAD)[None]
        ssd = (got[1].transpose(1, 0, 2).reshape(4, SSD_CONV_DIM), given["conv_b"][i][None],
               rep(given["dt_bias"][i]), rep(given["a_log"][i]), rep(given["d_skip"][i]), given["ssd_norm"][i][None])
        sgu = (given["sgu_ln_g"][i][None], given["sgu_ln_b"][i][None], given["sgu_w"][i],
               jnp.repeat(given["sgu_b"][i].T, HEAD, axis=1))
        return dict(mix_norm=given["mix_norm"][i][None], w_in=win, w_out=got[2].reshape(-1, D), ssd=ssd, sgu=sgu)

    x = x0
    tape = []
    for i in range(DEPTH):
        got = gathered(i, "ffn1", x)
        token = functools.reduce(lambda a, b: a + b, [g.token for g in gathers.values()]) if i == 0 else 0.0
        P = dict(ffn1=(given["ffn1_norm"][i][None] + token, *got))
        x, s1 = _ffn_fwd(x, *P["ffn1"])
        P.update(mix_params(i, gathered(i, "mix", x)))
        x, s2 = _mix_fwd(x, P)
        P["ffn2"] = (given["ffn2_norm"][i][None], *gathered(i, "ffn2", x))
        x, s3 = _ffn_fwd(x, *P["ffn2"])
        tape.append((P, s1, s2, s3))
    loss_part, dx, dgf = _final_loss(x, given["final_norm"][None], tgt)

    me = 2 * lax.axis_index("x") + lax.axis_index("y")
    jobs = []

    flight = dict(op=None, owners=[], ticks=0)

    def tick(after, begin=None):
        parts, owners = [], []
        if flight["op"] is not None:
            got = flight["op"].wait(after)
            for job, (a0, na) in zip(flight["owners"], flight["op"].spans):
                mine, k = got[a0:a0 + na], len(job["names"])
                if job["stage"] == 1:
                    parts.append(_part_chips([_pair_add(g, l, c) for g, l in zip(mine[:k], mine[k:])]))
                elif job["stage"] == 2:
                    parts.append(_part_join([_chip_sum(p, l, me, c) for p, l in zip(mine[:k], mine[k:])]))
                else:
                    job.update(stage=4, out=dict(zip(job["names"], mine)))
                    continue
                job["stage"] += 1
                owners.append(job)
        if begin is not None:
            i, gname, gd = begin
            names = [n for n in dict(GROUPS)[gname] if n != "conv_w"]
            jobs.append(dict(key=(i, gname), names=names, stage=1))
            parts.append(_part_sibling([_halved(gd[n]) for n in names]))
            owners.append(jobs[-1])
        flight.update(op=_start_parts(parts, f"rs_tick{flight['ticks']}") if parts else None, owners=owners,
                      ticks=flight["ticks"] + 1)
        return flight["op"].token if parts else 0.0

    grads = [dict() for _ in range(DEPTH)]
    for i in reversed(range(DEPTH)):
        P, s1, s2, s3 = tape[i]
        g = grads[i]
        norm, wg, wu, wd = P["ffn2"]
        mids, (g["ffn2_w_gate"], g["ffn2_w_up"], g["ffn2_w_down"]) = _ffn_bwd_weights(dx, s3, wd)
        tok = tick(g["ffn2_w_down"], (i, "ffn2", g))
        dx, dn2 = _ffn_bwd_input(dx, s3, mids, norm + tok, wg, wu)
        mids, gm = _mix_bwd_weights(dx, s2, P)
        g.update(gm)
        tok = tick(gm["w_in"], (i, "mix", g))
        dx, dnm = _mix_bwd_input(dx, s2, mids, {**P, "mix_norm": P["mix_norm"] + tok})
        norm, wg, wu, wd = P["ffn1"]
        mids, (g["ffn1_w_gate"], g["ffn1_w_up"], g["ffn1_w_down"]) = _ffn_bwd_weights(dx, s1, wd)
        tok = tick(g["ffn1_w_down"], (i, "ffn1", g))
        dx, dn1 = _ffn_bwd_input(dx, s1, mids, norm + tok, wg, wu)
        g["ffn1_norm"], g["mix_norm"], g["ffn2_norm"] = dn1[0], dnm[0], dn2[0]
    grad_x = dx.reshape(given["x"].shape)

    order = [n for n in SMALL if n != "final_norm"] + ["final_norm"]
    small = [jnp.stack([grads[i][n] for i in range(DEPTH)]) for n in order[:-1] + ["conv_w"]]
    small = small[:-1] + [dgf[0], small[-1], loss_part[0, 0:1]]
    n_small = sum(s.size for s in small)
    rows_small = -(-n_small // (128 * 8)) * 8

    def flat(arrs):
        fill = rows_small * 128 - sum(a.size for a in arrs)
        return jnp.concatenate([a.reshape(-1) for a in arrs] + [jnp.zeros((fill,), F32)]).reshape(rows_small, 128)

    gsmall = _all_reduce_small(flat(small)).reshape(-1)

    grad_w = {}
    off = 0
    for n in order:
        size = given[n].size
        grad_w[n] = gsmall[off:off + size].reshape(given[n].shape)
        off += size
    cw = gsmall[off:off + 2 * 4 * SSD_CONV_DIM].reshape(DEPTH, 4, SSD_CONV_DIM)
    grad_w["conv_w"] = lax.dynamic_slice_in_dim(cw, me * (SSD_CONV_DIM // 4), SSD_CONV_DIM // 4, axis=2)
    loss = gsmall[off + 2 * 4 * SSD_CONV_DIM]

    delta, new_m, new_v = {}, {}, {}
    shp = given["conv_w"].shape
    d, m2, v2 = _adamw(*[a.reshape(shp[0] * shp[1], shp[2])
                         for a in (given["conv_w"], grad_w["conv_w"], given["m_conv_w"], given["v_conv_w"])])
    delta["conv_w"], new_m["conv_w"], new_v["conv_w"] = d.reshape(shp), m2.reshape(shp), v2.reshape(shp)
    packed = [flat([given[pre + n] for n in order]) for pre in ("", "m_", "v_")]
    small_out = _adamw(packed[0], gsmall.reshape(rows_small, 128), packed[1], packed[2])
    outs = [o.reshape(-1) for o in small_out]
    off = 0
    for n in order:
        size = given[n].size
        for dst, o in zip((delta, new_m, new_v), outs):
            dst[n] = o[off:off + size].reshape(given[n].shape)
        off += size

    stepped, arrived = {}, {}

    def update_arrived(dep):
        out = None
        for job in jobs:
            if job["stage"] == 4 and not job.get("seen"):
                job["seen"] = True
                for n, full in job["out"].items():
                    view = (lambda a: jnp.swapaxes(a, 1, 2)) if n in TRANSPOSED else (lambda a: a)
                    arrived.setdefault(n, {})[job["key"][0]] = full.reshape(view(given[n]).shape[1:])
                    if len(arrived[n]) == DEPTH:
                        res = _adamw_pair(view(given[n]), arrived[n][0], arrived[n][1], view(given["m_" + n]),
                                          view(given["v_" + n]), dep)
                        stepped[n] = [view(r) for r in res]
                        out = res[0]
        return out

    after = small_out[0]
    while any(j["stage"] < 4 for j in jobs):
        done = update_arrived(jnp.zeros((8, 128), F32) + tok)
        after = after if done is None else done
        tok = tick(after)
    update_arrived(jnp.zeros((8, 128), F32) + tok)
    for n, (d, m2, v2, g) in stepped.items():
        delta[n], new_m[n], new_v[n], grad_w[n] = d, m2, v2, g

    return (loss, grad_x, *[grad_w[n] for n in WEIGHTS], *[delta[n] for n in WEIGHTS],
            *[new_m[n] for n in WEIGHTS], *[new_v[n] for n in WEIGHTS])
```

```python
import functools
import math

import jax
import jax.numpy as jnp
from jax import lax
from jax.experimental import pallas as pl
from jax.experimental.pallas import tpu as pltpu

F32 = jnp.float32
BF = jnp.bfloat16

RMS_EPS = 1e-6
LN_EPS = 1e-5
SEQ = 2048
CHUNK = 128
N_CHUNK = SEQ // CHUNK
ATT_W = 384
HEAD = 64
SSD_W = 384
SSD_CONV_DIM = 896
SSD_STATE = 128
SGU_W = 256
DILATIONS = (1, 4, 16)
W_QKV = 3 * ATT_W
W_SSD = SSD_CONV_DIM + SSD_W + SSD_W
W_UV = 2 * SGU_W
ADAM_LR = 0.001
ADAM_B1 = 0.9
ADAM_B2 = 0.999
ADAM_EPS = 1e-08
ADAM_WD = 0.01
ADAM_STEP = 10
NEG = -1e30
ATTN_BWD_VMEM = 48 * 2 ** 20
FFN_VMEM = 60 * 2 ** 20


def _dot(a, b):
    return jnp.dot(a, b, preferred_element_type=F32)


def _dot_nt(a, b):
    return lax.dot_general(a, b, (((1,), (1,)), ((), ())), preferred_element_type=F32)


def _dot_tn(a, b):
    return lax.dot_general(a, b, (((0,), (0,)), ((), ())), preferred_element_type=F32)


def _sigmoid(x):
    return 1.0 / (1.0 + jnp.exp(-x))


def _call(body, *, name, grid, in_specs, out_specs, out_shape, scratch=(), sem=None, vmem=None):
    return pl.pallas_call(
        body, name=name, grid=grid, in_specs=in_specs, out_specs=out_specs, out_shape=out_shape,
        scratch_shapes=list(scratch),
        compiler_params=pltpu.CompilerParams(dimension_semantics=sem, vmem_limit_bytes=vmem),
    )


def _tile(n, want):
    t = min(n, want)
    while n % t:
        t //= 2
    return t


def _final_loss(x, g, tgt):
    T, D = x.shape
    tm = _tile(T, 512)

    def body(x_ref, g_ref, t_ref, l_ref, dx_ref, dg_ref):
        @pl.when(pl.program_id(0) == 0)
        def _():
            dg_ref[...] = jnp.zeros_like(dg_ref)
            l_ref[...] = jnp.zeros_like(l_ref)

        xf = x_ref[...]
        gg = g_ref[...]
        r = lax.rsqrt(jnp.mean(xf * xf, axis=-1, keepdims=True) + RMS_EPS)
        xn = xf * r
        e = xn * gg - t_ref[...]
        part = 0.5 * jnp.sum(jnp.mean(e * e, axis=-1, keepdims=True), axis=0, keepdims=True)
        l_ref[...] += jnp.broadcast_to(part, l_ref.shape)
        dy = e * (1.0 / D)
        u = dy * gg
        mu = jnp.mean(u * xf, axis=-1, keepdims=True)
        dx_ref[...] = r * (u - xf * (r * r * mu))
        dg_ref[...] += jnp.sum(dy * xn, axis=0, keepdims=True)

    row = pl.BlockSpec((tm, D), lambda i: (i, 0))
    vec = pl.BlockSpec((1, D), lambda i: (0, 0))
    lsp = pl.BlockSpec((1, 128), lambda i: (0, 0))
    return _call(body, name="final_loss", grid=(T // tm,), in_specs=[row, vec, row], out_specs=[lsp, row, vec],
                 out_shape=[jax.ShapeDtypeStruct((1, 128), F32), jax.ShapeDtypeStruct((T, D), F32),
                            jax.ShapeDtypeStruct((1, D), F32)],
                 sem=("arbitrary",))(x, g, tgt)


def _resident(shape):
    return pl.BlockSpec(shape, lambda *_: (0,) * len(shape), pipeline_mode=pl.Buffered(1))


def _ffn_fwd_k(x, gn, wg, wu, wd):
    T, D = x.shape
    NS, _, Fs = wg.shape
    tm = _tile(T, 1024)

    def body(x_ref, gn_ref, wg_ref, wu_ref, wd_ref, o_ref, h_ref, s1_ref, s2_ref, a_ref, hs, acc):
        j = pl.program_id(1)

        @pl.when(j == 0)
        def _():
            xf = x_ref[...]
            r = lax.rsqrt(jnp.mean(xf * xf, axis=-1, keepdims=True) + RMS_EPS)
            hs[...] = (xf * r * gn_ref[...]).astype(BF)
            h_ref[...] = hs[...]
            acc[...] = jnp.zeros_like(acc)

        h = hs[...]
        g = _dot(h, wg_ref[...])
        u = _dot(h, wu_ref[...])
        sg = _sigmoid(g)
        s1 = g * sg
        a = (s1 * u).astype(BF)
        s1_ref[...] = s1.astype(BF)
        s2_ref[...] = (u * (sg * (1.0 + g * (1.0 - sg)))).astype(BF)
        a_ref[...] = a
        acc[...] += _dot(a, wd_ref[...])

        @pl.when(j == NS - 1)
        def _():
            o_ref[...] = x_ref[...] + 0.5 * acc[...]

    row = pl.BlockSpec((tm, D), lambda i, j: (i, 0))
    act = pl.BlockSpec((None, tm, Fs), lambda i, j: (j, i, 0))
    sh = jax.ShapeDtypeStruct((NS, T, Fs), BF)
    wspec = lambda w: pl.BlockSpec((None,) + w.shape[1:], lambda i, j: (j, 0, 0))
    return _call(body, name="ffn_fwd", grid=(T // tm, NS),
                 in_specs=[row, pl.BlockSpec((1, D), lambda i, j: (0, 0)), wspec(wg), wspec(wu), wspec(wd)],
                 out_specs=[row, row, act, act, act],
                 out_shape=[jax.ShapeDtypeStruct((T, D), F32), jax.ShapeDtypeStruct((T, D), BF), sh, sh, sh],
                 scratch=[pltpu.VMEM((tm, D), BF), pltpu.VMEM((tm, D), F32)],
                 sem=("parallel", "arbitrary"), vmem=FFN_VMEM)(x, gn, wg, wu, wd)


def _ffn_bwd_act(dxo, s1, s2, wd):
    NS, T, Fs = s1.shape
    D = dxo.shape[1]
    tm = _tile(T, 1024)

    def body(dxo_ref, s1_ref, s2_ref, wd_ref, dg_ref, du_ref, dy_ref, dys):
        j = pl.program_id(1)

        @pl.when(j == 0)
        def _():
            dys[...] = (0.5 * dxo_ref[...]).astype(BF)
            dy_ref[...] = dys[...]

        da = _dot_nt(dys[...], wd_ref[j])
        dg_ref[...] = (da * s2_ref[...].astype(F32)).astype(BF)
        du_ref[...] = (da * s1_ref[...].astype(F32)).astype(BF)

    row = pl.BlockSpec((tm, D), lambda i, j: (i, 0))
    act = pl.BlockSpec((None, tm, Fs), lambda i, j: (j, i, 0))
    sh = jax.ShapeDtypeStruct((NS, T, Fs), BF)
    return _call(body, name="ffn_bwd_act", grid=(T // tm, NS), in_specs=[row, act, act, _resident(wd.shape)],
                 out_specs=[act, act, row], out_shape=[sh, sh, jax.ShapeDtypeStruct((T, D), BF)],
                 scratch=[pltpu.VMEM((tm, D), BF)], sem=("parallel", "arbitrary"))(dxo, s1, s2, wd)


def _ffn_bwd_dx(dg, du, wg, wu, x, gn, dxo):
    NS, T, Fs = dg.shape
    D = x.shape[1]
    tm = _tile(T, 1024)

    def body(dg_ref, du_ref, wg_ref, wu_ref, x_ref, gn_ref, dxo_ref, dx_ref, dgn_ref, acc):
        i, j = pl.program_id(0), pl.program_id(1)

        @pl.when((i == 0) & (j == 0))
        def _():
            dgn_ref[...] = jnp.zeros_like(dgn_ref)

        @pl.when(j == 0)
        def _():
            acc[...] = jnp.zeros_like(acc)

        acc[...] += _dot_nt(dg_ref[...], wg_ref[j]) + _dot_nt(du_ref[...], wu_ref[j])

        @pl.when(j == NS - 1)
        def _():
            xf = x_ref[...]
            r = lax.rsqrt(jnp.mean(xf * xf, axis=-1, keepdims=True) + RMS_EPS)
            dh = acc[...]
            uu = dh * gn_ref[...]
            mu = jnp.mean(uu * xf, axis=-1, keepdims=True)
            dx_ref[...] = dxo_ref[...] + r * (uu - xf * (r * r * mu))
            dgn_ref[...] += jnp.sum(dh * xf * r, axis=0, keepdims=True)

    row = pl.BlockSpec((tm, D), lambda i, j: (i, 0))
    vec = pl.BlockSpec((1, D), lambda i, j: (0, 0))
    act = pl.BlockSpec((None, tm, Fs), lambda i, j: (j, i, 0))
    return _call(body, name="ffn_bwd_dx", grid=(T // tm, NS),
                 in_specs=[act, act, _resident(wg.shape), _resident(wu.shape), row, vec, row], out_specs=[row, vec],
                 out_shape=[jax.ShapeDtypeStruct((T, D), F32), jax.ShapeDtypeStruct((1, D), F32)],
                 scratch=[pltpu.VMEM((tm, D), F32)], sem=("arbitrary", "arbitrary"), vmem=FFN_VMEM)(
        dg, du, wg, wu, x, gn, dxo)


def _ffn_bwd_k2(hb, dyb, a, dg, du):
    NS, T, Fs = a.shape
    D = hb.shape[1]
    tk = _tile(T, 1024)

    def body(h_ref, dy_ref, a_ref, dg_ref, du_ref, og_ref, ou_ref, od_ref):
        @pl.when(pl.program_id(1) == 0)
        def _():
            og_ref[...] = jnp.zeros_like(og_ref)
            ou_ref[...] = jnp.zeros_like(ou_ref)
            od_ref[...] = jnp.zeros_like(od_ref)

        h = h_ref[...]
        og_ref[...] += _dot_tn(dg_ref[...], h)
        ou_ref[...] += _dot_tn(du_ref[...], h)
        od_ref[...] += _dot_tn(a_ref[...], dy_ref[...])

    row = pl.BlockSpec((tk, D), lambda j, k: (k, 0))
    act = pl.BlockSpec((None, tk, Fs), lambda j, k: (j, k, 0))
    return _call(body, name="ffn_bwd_w", grid=(NS, T // tk), in_specs=[row, row, act, act, act],
                 out_specs=[pl.BlockSpec((None, Fs, D), lambda j, k: (j, 0, 0))] * 3,
                 out_shape=[jax.ShapeDtypeStruct((NS, Fs, D), F32)] * 3,
                 sem=("parallel", "arbitrary"))(hb, dyb, a, dg, du)


def _mm_nn(a, b, res=None, out_dtype=F32):
    T, K = a.shape
    N = b.shape[1]
    tm = _tile(T, 512)
    tn = N if N <= 2048 else _tile(N, 1024)

    def body(*refs):
        if res is None:
            a_ref, b_ref, o_ref = refs
            o_ref[...] = _dot(a_ref[...], b_ref[...]).astype(out_dtype)
        else:
            a_ref, b_ref, r_ref, o_ref = refs
            o_ref[...] = (r_ref[...] + _dot(a_ref[...], b_ref[...])).astype(out_dtype)

    o = pl.BlockSpec((tm, tn), lambda i, j: (i, j))
    ins = [pl.BlockSpec((tm, K), lambda i, j: (i, 0)), pl.BlockSpec((K, tn), lambda i, j: (0, j))]
    args = [a, b]
    if res is not None:
        ins.append(o)
        args.append(res)
    return _call(body, name="mm_nn", grid=(T // tm, N // tn), in_specs=ins, out_specs=o,
                 out_shape=jax.ShapeDtypeStruct((T, N), out_dtype), sem=("parallel", "parallel"))(*args)


def _mix_bwd_dy(dxo, w_out):
    T, D = dxo.shape
    tm = _tile(T, 512)
    cuts = (0, ATT_W, ATT_W + SSD_W, ATT_W + SSD_W + SGU_W)

    def body(dx_ref, w_ref, a_ref, s_ref, g_ref):
        d = _dot_nt(dx_ref[...].astype(BF), w_ref[...])
        for o_ref, lo, hi in zip((a_ref, s_ref, g_ref), cuts[:-1], cuts[1:]):
            o_ref[...] = d[:, lo:hi]

    row = lambda w: pl.BlockSpec((tm, w), lambda i: (i, 0))
    return _call(body, name="mix_bwd_dy", grid=(T // tm,), in_specs=[row(D), _resident(w_out.shape)],
                 out_specs=[row(ATT_W), row(SSD_W), row(SGU_W)],
                 out_shape=[jax.ShapeDtypeStruct((T, w), F32) for w in (ATT_W, SSD_W, SGU_W)],
                 sem=("parallel",))(dxo, w_out)


def _mm_tn(a, b):
    T, M = a.shape
    N = b.shape[1]
    tk = _tile(T, 1024)
    tmm = _tile(M, 512)

    def body(a_ref, b_ref, o_ref):
        @pl.when(pl.program_id(1) == 0)
        def _():
            o_ref[...] = jnp.zeros_like(o_ref)

        o_ref[...] += _dot_tn(a_ref[...].astype(BF), b_ref[...].astype(BF))

    return _call(body, name="mm_tn", grid=(M // tmm, T // tk),
                 in_specs=[pl.BlockSpec((tk, tmm), lambda i, k: (k, i)), pl.BlockSpec((tk, N), lambda i, k: (k, 0))],
                 out_specs=pl.BlockSpec((tmm, N), lambda i, k: (i, 0)),
                 out_shape=jax.ShapeDtypeStruct((M, N), F32), sem=("parallel", "arbitrary"))(a, b)


def _mix_proj(x, gn, win):
    T, D = x.shape
    tm = _tile(T, 512)
    cuts = (0, W_QKV, W_QKV + W_SSD, W_QKV + W_SSD + W_UV)

    def body(x_ref, gn_ref, w_ref, h_ref, q_ref, s_ref, u_ref):
        xf = x_ref[...]
        r = lax.rsqrt(jnp.mean(xf * xf, axis=-1, keepdims=True) + RMS_EPS)
        h = (xf * r * gn_ref[...]).astype(BF)
        h_ref[...] = h
        for o_ref, lo, hi in zip((q_ref, s_ref, u_ref), cuts[:-1], cuts[1:]):
            o_ref[...] = _dot(h, w_ref[:, lo:hi])

    row = lambda w: pl.BlockSpec((tm, w), lambda i: (i, 0))
    return _call(body, name="mix_proj", grid=(T // tm,),
                 in_specs=[row(D), pl.BlockSpec((1, D), lambda i: (0, 0)), _resident(win.shape)],
                 out_specs=[row(D), row(W_QKV), row(W_SSD), row(W_UV)],
                 out_shape=[jax.ShapeDtypeStruct((T, D), BF), jax.ShapeDtypeStruct((T, W_QKV), F32),
                            jax.ShapeDtypeStruct((T, W_SSD), F32), jax.ShapeDtypeStruct((T, W_UV), F32)],
                 sem=("parallel",))(x, gn, win)


def _mix_bwd_dx(dqkv, dsin, duv, win, x, gn, dxo):
    T, D = x.shape
    tm = _tile(T, 512)
    cuts = (0, W_QKV, W_QKV + W_SSD, W_QKV + W_SSD + W_UV)

    def body(dq_ref, ds_ref, du_ref, w_ref, x_ref, gn_ref, dxo_ref, dx_ref, dgn_ref):
        @pl.when(pl.program_id(0) == 0)
        def _():
            dgn_ref[...] = jnp.zeros_like(dgn_ref)

        dh = (_dot_nt(dq_ref[...], w_ref[:, cuts[0]:cuts[1]]) + _dot_nt(ds_ref[...], w_ref[:, cuts[1]:cuts[2]])
              + _dot_nt(du_ref[...], w_ref[:, cuts[2]:cuts[3]]))
        xf = x_ref[...]
        r = lax.rsqrt(jnp.mean(xf * xf, axis=-1, keepdims=True) + RMS_EPS)
        uu = dh * gn_ref[...]
        mu = jnp.mean(uu * xf, axis=-1, keepdims=True)
        dx_ref[...] = dxo_ref[...] + r * (uu - xf * (r * r * mu))
        dgn_ref[...] += jnp.sum(dh * xf * r, axis=0, keepdims=True)

    row = lambda w: pl.BlockSpec((tm, w), lambda i: (i, 0))
    vec = pl.BlockSpec((1, D), lambda i: (0, 0))
    return _call(body, name="mix_bwd_dx", grid=(T // tm,),
                 in_specs=[row(W_QKV), row(W_SSD), row(W_UV), _resident(win.shape), row(D), vec, row(D)],
                 out_specs=[row(D), vec],
                 out_shape=[jax.ShapeDtypeStruct((T, D), F32), jax.ShapeDtypeStruct((1, D), F32)],
                 sem=("arbitrary",))(dqkv, dsin, duv, win, x, gn, dxo)


def _lane_mask(e, width=128):
    return (lax.broadcasted_iota(jnp.int32, (1, width), 1) // HEAD) == e


def _band_mask(n):
    qi = lax.broadcasted_iota(jnp.int32, (CHUNK, 2 * CHUNK), 0)
    kj = lax.broadcasted_iota(jnp.int32, (CHUNK, 2 * CHUNK), 1)
    dist = qi + CHUNK - kj
    return (dist >= 0) & (dist <= CHUNK) & ((kj >= CHUNK) | (n > 0))


def _sub_rows(r, block, dil):
    if dil == 1:
        return pl.ds(pl.multiple_of(block * CHUNK, CHUNK), CHUNK)
    return pl.ds(r + dil * CHUNK * block, CHUNK, stride=dil)


def _attn_specs(T, dil):
    B, nb = T // SEQ, SEQ // (CHUNK * dil)
    once = dict(pipeline_mode=pl.Buffered(1))
    q_like = lambda col: pl.BlockSpec((CHUNK * dil, 128), lambda b, n, r: (b * nb + n, col), **(once if nb == 1 else {}))
    k_like = lambda col: pl.BlockSpec((SEQ, 128), lambda b, n, r: (b, col), **once)
    return B, nb, q_like, k_like


def _attn_fwd(qkv, dil):
    T = qkv.shape[0]
    B, nb, q_like, k_like = _attn_specs(T, dil)
    scale = HEAD ** -0.5

    def body(*refs):
        q_t, k_t, v_t, o_t, l_t = refs[0:3], refs[3:6], refs[6:9], refs[9:12], refs[12:15]
        n, r = pl.program_id(1), pl.program_id(2)
        mine = _sub_rows(r, 0, dil)
        cur, prv = _sub_rows(r, n, dil), _sub_rows(r, jnp.maximum(n - 1, 0), dil)
        mask = _band_mask(n)
        for t in range(3):
            qt = q_t[t][mine, :].astype(BF)
            kt = jnp.concatenate([k_t[t][prv, :], k_t[t][cur, :]], axis=0).astype(BF)
            vt = jnp.concatenate([v_t[t][prv, :], v_t[t][cur, :]], axis=0).astype(BF)
            o_pair = jnp.zeros((CHUNK, 128), F32)
            l_pair = jnp.zeros((CHUNK, 128), F32)
            for e in range(2):
                lm = _lane_mask(e)
                s = _dot_nt(jnp.where(lm, qt, jnp.zeros_like(qt)), kt) * scale
                s = jnp.where(mask, s, NEG)
                m = jnp.max(s, axis=-1, keepdims=True)
                p = jnp.exp(s - m)
                den = jnp.sum(p, axis=-1, keepdims=True)
                o = _dot(p.astype(BF), vt) / den
                o_pair = jnp.where(lm, o, o_pair)
                l_pair = jnp.where(lm, m + jnp.log(den), l_pair)
            o_t[t][mine, :] = o_pair
            l_t[t][mine, :] = l_pair

    out_spec = pl.BlockSpec((CHUNK * dil, 128), lambda b, n, r: (b * nb + n, 0))
    sh = jax.ShapeDtypeStruct((T, 128), F32)
    outs = _call(
        body, name=f"attn_fwd_d{dil}", grid=(B, nb, dil),
        in_specs=[q_like(t) for t in range(3)] + [k_like(3 + t) for t in range(3)] + [k_like(6 + t) for t in range(3)],
        out_specs=[out_spec] * 6, out_shape=[sh] * 6, sem=("parallel", "arbitrary", "arbitrary"))(*([qkv] * 9))
    return list(outs[0:3]), list(outs[3:6])


def _attn_combine(branches):
    T = branches[0][0][0].shape[0]
    tm = _tile(T, 512)

    def body(*refs):
        y_ref, l_ref = refs[-2:]
        for t in range(3):
            o = [refs[6 * i + t][...] for i in range(3)]
            a, b, c = [refs[6 * i + 3 + t][...] for i in range(3)]
            m = jnp.maximum(jnp.maximum(a, b), c)
            ea, eb, ec = jnp.exp(a - m), jnp.exp(b - m), jnp.exp(c - m)
            z = ea + eb + ec
            y_ref[:, 128 * t:128 * (t + 1)] = (ea * o[0] + eb * o[1] + ec * o[2]) / z
            l_ref[:, 128 * t:128 * (t + 1)] = m + jnp.log(z)

    tile = pl.BlockSpec((tm, 128), lambda i: (i, 0))
    row = pl.BlockSpec((tm, ATT_W), lambda i: (i, 0))
    sh = jax.ShapeDtypeStruct((T, ATT_W), F32)
    flat = [a for o_t, l_t in branches for a in (*o_t, *l_t)]
    return _call(body, name="attn_combine", grid=(T // tm,), in_specs=[tile] * 18, out_specs=[row, row],
                 out_shape=[sh, sh], sem=("parallel",))(*flat)


def _attn_bwd(qkv, do, out, lse, dil):
    T = qkv.shape[0]
    B, nb, q_like, k_like = _attn_specs(T, dil)
    scale = HEAD ** -0.5

    def body(*refs):
        q_t, k_t, v_t = refs[0:3], refs[3:6], refs[6:9]
        do_t, out_t, lse_t = refs[9:12], refs[12:15], refs[15:18]
        dq_t, dk_t, dv_t = refs[18:21], refs[21:24], refs[24:27]
        n, r = pl.program_id(1), pl.program_id(2)

        @pl.when((n == 0) & (r == 0))
        def _():
            for t in range(3):
                dk_t[t][...] = jnp.zeros_like(dk_t[t])
                dv_t[t][...] = jnp.zeros_like(dv_t[t])

        mine = _sub_rows(r, 0, dil)
        cur, prv = _sub_rows(r, n, dil), _sub_rows(r, jnp.maximum(n - 1, 0), dil)
        mask = _band_mask(n)
        for t in range(3):
            qt = q_t[t][mine, :].astype(BF)
            kt = jnp.concatenate([k_t[t][prv, :], k_t[t][cur, :]], axis=0).astype(BF)
            vt = jnp.concatenate([v_t[t][prv, :], v_t[t][cur, :]], axis=0).astype(BF)
            do_ = do_t[t][mine, :]
            dlt = do_ * out_t[t][mine, :]
            ls = lse_t[t][mine, :]
            dq_pair = jnp.zeros((CHUNK, 128), F32)
            dk_acc = jnp.zeros((2 * CHUNK, 128), F32)
            dv_acc = jnp.zeros((2 * CHUNK, 128), F32)
            for e in range(2):
                lm = _lane_mask(e)
                qm = jnp.where(lm, qt, jnp.zeros_like(qt))
                s = _dot_nt(qm, kt) * scale
                p = jnp.exp(jnp.where(mask, s - ls[:, HEAD * e:HEAD * e + 1], NEG))
                dom = jnp.where(lm, do_, 0.0).astype(BF)
                dv_acc += _dot_tn(p.astype(BF), dom)
                dp = _dot_nt(dom, vt)
                delta = jnp.sum(jnp.where(lm, dlt, 0.0), axis=-1, keepdims=True)
                ds = (p * (dp - delta) * scale).astype(BF)
                dq_pair += jnp.where(lm, _dot(ds, kt), 0.0)
                dk_acc += _dot_tn(ds, qm)
            dq_t[t][mine, :] = dq_pair
            dk_t[t][cur, :] = dk_t[t][cur, :] + dk_acc[CHUNK:]
            dk_t[t][prv, :] = dk_t[t][prv, :] + dk_acc[:CHUNK]
            dv_t[t][cur, :] = dv_t[t][cur, :] + dv_acc[CHUNK:]
            dv_t[t][prv, :] = dv_t[t][prv, :] + dv_acc[:CHUNK]

    q_out = pl.BlockSpec((CHUNK * dil, 128), lambda b, n, r: (b * nb + n, 0))
    k_out = pl.BlockSpec((SEQ, 128), lambda b, n, r: (b, 0))
    sh = jax.ShapeDtypeStruct((T, 128), F32)
    tiles = lambda: [q_like(t) for t in range(3)]
    return list(_call(
        body, name=f"attn_bwd_d{dil}", grid=(B, nb, dil),
        in_specs=tiles() + [k_like(3 + t) for t in range(3)] + [k_like(6 + t) for t in range(3)]
        + tiles() + tiles() + tiles(),
        out_specs=[q_out] * 3 + [k_out] * 6, out_shape=[sh] * 9,
        sem=("parallel", "arbitrary", "arbitrary"), vmem=ATTN_BWD_VMEM)(*([qkv] * 9 + [do] * 3 + [out] * 3 + [lse] * 3)))


def _sum_branches(parts):
    T = parts[0][0].shape[0]
    tm = _tile(T, 512)

    def body(*refs):
        o_ref = refs[-1]
        for c in range(9):
            acc = refs[c][...] + refs[9 + c][...] + refs[18 + c][...]
            o_ref[:, 128 * c:128 * (c + 1)] = acc.astype(BF)

    tile = pl.BlockSpec((tm, 128), lambda i: (i, 0))
    flat = [a for br in parts for a in br]
    return _call(body, name="attn_sum_branches", grid=(T // tm,), in_specs=[tile] * 27,
                 out_specs=pl.BlockSpec((tm, W_QKV), lambda i: (i, 0)),
                 out_shape=jax.ShapeDtypeStruct((T, W_QKV), BF), sem=("parallel",))(*flat)


def _silu(x):
    return x * _sigmoid(x)


def _dsilu(x):
    s = _sigmoid(x)
    return s * (1.0 + x * (1.0 - s))


def _log1p(u):
    return jnp.where(u < 0.01, u * (1.0 - u * (0.5 - u * (1.0 / 3.0))), jnp.log(1.0 + u))


def _softplus(x):
    return jnp.maximum(x, 0.0) + _log1p(jnp.exp(-jnp.abs(x)))


def _cumsum_rows(x, reverse=False):
    n = x.shape[0]
    rows = lax.broadcasted_iota(jnp.int32, x.shape, 0)
    k = 1
    while k < n:
        if reverse:
            x = x + jnp.where(rows < n - k, pltpu.roll(x, n - k, 0), 0.0)
        else:
            x = x + jnp.where(rows >= k, pltpu.roll(x, k, 0), 0.0)
        k *= 2
    return x


def _tri():
    r = lax.broadcasted_iota(jnp.int32, (CHUNK, CHUNK), 0)
    c = lax.broadcasted_iota(jnp.int32, (CHUNK, CHUNK), 1)
    return r >= c


def _row_mask(e):
    return (lax.broadcasted_iota(jnp.int32, (128, 1), 0) // HEAD) == e


def _first_lane(e):
    return lax.broadcasted_iota(jnp.int32, (1, 128), 1) == HEAD * e


def _ssd_pre(x_ref, halo_ref, first, cw_ref, cb_ref, dtb_ref, al_ref, ext):
    row = x_ref[...]
    z = row[:, SSD_CONV_DIM:SSD_CONV_DIM + SSD_W]
    u = row[:, SSD_CONV_DIM + SSD_W:] + dtb_ref[...]
    ext[0:8, :] = jnp.where(first, 0.0, halo_ref[:, 0:SSD_CONV_DIM])
    ext[8:8 + CHUNK, :] = row[:, 0:SSD_CONV_DIM]
    xc = cb_ref[...]
    for j in range(4):
        xc = xc + cw_ref[j:j + 1, :] * ext[pl.ds(5 + j, CHUNK), :]
    xa = _silu(xc)
    dt = _softplus(u)
    a = dt * (-jnp.exp(al_ref[...]))
    A = _cumsum_rows(a)
    return dict(z=z, u=u, xc=xc, xs=xa[:, 0:SSD_W], Bm=xa[:, SSD_W:SSD_W + 256], Cm=xa[:, SSD_W + 256:],
                dt=dt, a=a, A=A, AT=A.T, eA=jnp.exp(A), wdec=jnp.exp(A[CHUNK - 1:CHUNK, :] - A),
                dtot=jnp.exp(A[CHUNK - 1:CHUNK, :]))


def _ssd_y(p, hp_ref, dskip):
    tri = _tri()
    X = p["xs"] * p["dt"]
    Bb = [p["Bm"][:, 128 * g:128 * (g + 1)].astype(BF) for g in range(2)]
    Cb = [p["Cm"][:, 128 * g:128 * (g + 1)].astype(BF) for g in range(2)]
    CB = [_dot_nt(Cb[g], Bb[g]) for g in range(2)]
    tiles = []
    for t in range(3):
        sl = slice(128 * t, 128 * (t + 1))
        hpb = hp_ref[sl, :].astype(BF)
        acc = jnp.zeros((CHUNK, 128), F32)
        for e in range(2):
            h = 2 * t + e
            g, col = h // 3, HEAD * h
            lm = _lane_mask(e)
            L = jnp.exp(jnp.where(tri, p["A"][:, col:col + 1] - p["AT"][col:col + 1, :], NEG))
            yd = _dot((CB[g] * L).astype(BF), jnp.where(lm, X[:, sl], 0.0).astype(BF))
            yo = _dot_nt(Cb[g], hpb) * p["eA"][:, sl]
            acc = acc + jnp.where(lm, yd + yo, 0.0)
        tiles.append(acc)
    return jnp.concatenate(tiles, axis=1) + dskip * p["xs"], X, Bb, Cb, CB


def _group_stats(v):
    g0 = lax.broadcasted_iota(jnp.int32, (1, SSD_W), 1) < SSD_W // 2
    m0 = jnp.sum(jnp.where(g0, v, 0.0), axis=-1, keepdims=True) * (2.0 / SSD_W)
    m1 = jnp.sum(jnp.where(g0, 0.0, v), axis=-1, keepdims=True) * (2.0 / SSD_W)
    return jnp.where(g0, m0, m1)


def _ssd_specs(T, rev):
    B = T // SEQ

    def chunk(b, c):
        return b * N_CHUNK + (N_CHUNK - 1 - c if rev else c)

    row = pl.BlockSpec((CHUNK, W_SSD), lambda b, c: (chunk(b, c), 0))
    halo = pl.BlockSpec((8, W_SSD), lambda b, c: (jnp.maximum(chunk(b, c) * (CHUNK // 8) - 1, 0), 0))
    hp = pl.BlockSpec((None, SSD_W, SSD_STATE), lambda b, c: (chunk(b, c), 0, 0))
    y = pl.BlockSpec((CHUNK, SSD_W), lambda b, c: (chunk(b, c), 0))
    const = lambda r, w: pl.BlockSpec((r, w), lambda b, c: (0, 0))
    params = [const(4, SSD_CONV_DIM), const(1, SSD_CONV_DIM)] + [const(1, SSD_W)] * 4
    return B, row, halo, hp, y, const, params


def _ssd_fwd(sin, conv_w, conv_b, dtb, alog, dskip, norm_g):
    T = sin.shape[0]
    B, row, halo, hp, y, const, params = _ssd_specs(T, False)

    def body(x_ref, halo_ref, cw_ref, cb_ref, dtb_ref, al_ref, dk_ref, ng_ref, y_ref, hp_ref, ext, hst):
        c = pl.program_id(1)

        @pl.when(c == 0)
        def _():
            hst[...] = jnp.zeros_like(hst)

        p = _ssd_pre(x_ref, halo_ref, c == 0, cw_ref, cb_ref, dtb_ref, al_ref, ext)
        yv, X, Bb, Cb, CB = _ssd_y(p, hst, dk_ref[...])
        hp_ref[...] = hst[...]
        for t in range(3):
            sl = slice(128 * t, 128 * (t + 1))
            old = hst[sl, :]
            new = old
            for e in range(2):
                h = 2 * t + e
                g, col = h // 3, HEAD * h
                st = _dot_tn(jnp.where(_lane_mask(e), X[:, sl] * p["wdec"][:, sl], 0.0).astype(BF), Bb[g])
                new = jnp.where(_row_mask(e), old * p["dtot"][:, col:col + 1] + st, new)
            hst[sl, :] = new
        y2 = yv * _silu(p["z"])
        r = lax.rsqrt(_group_stats(y2 * y2) + RMS_EPS)
        y_ref[...] = y2 * r * ng_ref[...]

    return _call(body, name="ssd_fwd", grid=(B, N_CHUNK), in_specs=[row, halo] + params, out_specs=[y, hp],
                 out_shape=[jax.ShapeDtypeStruct((T, SSD_W), F32),
                            jax.ShapeDtypeStruct((T // CHUNK, SSD_W, SSD_STATE), F32)],
                 scratch=[pltpu.VMEM((8 + CHUNK, SSD_CONV_DIM), F32), pltpu.VMEM((SSD_W, SSD_STATE), F32)],
                 sem=("parallel", "arbitrary"))(sin, sin, conv_w, conv_b, dtb, alog, dskip, norm_g)


def _ssd_bwd(sin, hprev, dy3, conv_w, conv_b, dtb, alog, dskip, norm_g):
    T = sin.shape[0]
    B, row, halo, hp, y, const, params = _ssd_specs(T, True)

    def body(x_ref, halo_ref, hp_ref, dy_ref, cw_ref, cb_ref, dtb_ref, al_ref, dk_ref, ng_ref,
             dx_ref, dcw_ref, dcb_ref, dvec_ref, ext, ext2, dh):
        c = pl.program_id(1)

        @pl.when((pl.program_id(0) == 0) & (c == 0))
        def _():
            dcw_ref[...] = jnp.zeros_like(dcw_ref)
            dcb_ref[...] = jnp.zeros_like(dcb_ref)
            dvec_ref[...] = jnp.zeros_like(dvec_ref)

        @pl.when(c == 0)
        def _():
            dh[...] = jnp.zeros_like(dh)
            ext2[CHUNK:CHUNK + 8, :] = jnp.zeros((8, SSD_CONV_DIM), F32)

        p = _ssd_pre(x_ref, halo_ref, c == N_CHUNK - 1, cw_ref, cb_ref, dtb_ref, al_ref, ext)
        dskip_ = dk_ref[...]
        yv, X, Bb, Cb, CB = _ssd_y(p, hp_ref, dskip_)
        xs, z, A, AT = p["xs"], p["z"], p["A"], p["AT"]

        sz = _silu(z)
        y2 = yv * sz
        r = lax.rsqrt(_group_stats(y2 * y2) + RMS_EPS)
        dy3_ = dy_ref[...]
        uu = dy3_ * ng_ref[...]
        dy2 = r * (uu - y2 * (r * r * _group_stats(uu * y2)))
        dy = dy2 * sz
        dz = dy2 * yv * _dsilu(z)

        tri = _tri()
        rows = lax.broadcasted_iota(jnp.int32, (CHUNK, 1), 0)
        dG = [jnp.zeros((CHUNK, CHUNK), F32) for _ in range(2)]
        dB = [jnp.zeros((CHUNK, SSD_STATE), F32) for _ in range(2)]
        dC = [jnp.zeros((CHUNK, SSD_STATE), F32) for _ in range(2)]
        dX_t, dA_t, ddtx_t = [], [], []
        for t in range(3):
            sl = slice(128 * t, 128 * (t + 1))
            hp_t = hp_ref[sl, :]
            hpb = hp_t.astype(BF)
            dhc = dh[sl, :]
            dh_new = jnp.zeros((128, SSD_STATE), F32)
            dX = jnp.zeros((CHUNK, 128), F32)
            dA = jnp.zeros((CHUNK, 128), F32)
            ddtx = jnp.zeros((CHUNK, 128), F32)
            for e in range(2):
                h = 2 * t + e
                g, col = h // 3, HEAD * h
                lm, rm, fl = _lane_mask(e), _row_mask(e), _first_lane(e)
                L = jnp.exp(jnp.where(tri, A[:, col:col + 1] - AT[col:col + 1, :], NEG))
                Mf = CB[g] * L
                Xm = jnp.where(lm, X[:, sl], 0.0)
                Xmb = Xm.astype(BF)
                dyh = jnp.where(lm, dy[:, sl], 0.0)
                dyb = dyh.astype(BF)
                dXh = _dot_tn(Mf.astype(BF), dyb)
                dM = jnp.where(tri, _dot_nt(dyb, Xmb), 0.0)
                Wm = dM * Mf
                dAc = jnp.sum(Wm, axis=-1, keepdims=True) - jnp.sum(Wm.T, axis=-1, keepdims=True)
                dG[g] = dG[g] + dM * L
                eAt = p["eA"][:, sl]
                yo = _dot_nt(Cb[g], hpb)
                dyo = (dyh * eAt).astype(BF)
                dC[g] = dC[g] + _dot(dyo, hpb)
                dh_new = dh_new + _dot_tn(dyo, Cb[g])
                dAc = dAc + jnp.sum(dyh * yo * eAt, axis=-1, keepdims=True)
                dHn = jnp.where(rm, dhc, 0.0)
                dHnb = dHn.astype(BF)
                dec = p["dtot"][:, col:col + 1]
                dh_new = dh_new + dec * dHn
                Z = _dot_nt(Bb[g], dHnb)
                wt = p["wdec"][:, sl]
                xi = jnp.sum(Xm * Z, axis=-1, keepdims=True) * p["wdec"][:, col:col + 1]
                dXh = dXh + wt * Z
                dB[g] = dB[g] + _dot(jnp.where(lm, X[:, sl] * wt, 0.0).astype(BF), dHnb)
                dAtot = jnp.sum(xi, axis=0, keepdims=True) + dec * jnp.sum(
                    jnp.sum(dHn * hp_t, axis=-1, keepdims=True), axis=0, keepdims=True)
                dAc = dAc - xi + jnp.where(rows == CHUNK - 1, dAtot, 0.0)
                dA = dA + jnp.where(fl, dAc, 0.0)
                dX = dX + dXh
                ddtx = ddtx + jnp.where(fl, jnp.sum(dXh * xs[:, sl], axis=-1, keepdims=True), 0.0)
            dh[sl, :] = dh_new
            dX_t.append(dX)
            dA_t.append(dA)
            ddtx_t.append(ddtx)
        for g in range(2):
            dGb = dG[g].astype(BF)
            dC[g] = dC[g] + _dot(dGb, Bb[g])
            dB[g] = dB[g] + _dot_tn(dGb, Cb[g])
        dXf = jnp.concatenate(dX_t, axis=1)
        da = _cumsum_rows(jnp.concatenate(dA_t, axis=1), reverse=True)
        ddt = da * (-jnp.exp(al_ref[...])) + jnp.concatenate(ddtx_t, axis=1)
        du = ddt * _sigmoid(p["u"])
        dxs = dXf * p["dt"] + dskip_ * dy
        dxc = jnp.concatenate([dxs, dB[0], dB[1], dC[0], dC[1]], axis=1) * _dsilu(p["xc"])
        ext2[0:CHUNK, :] = dxc
        dxbc = jnp.zeros((CHUNK, SSD_CONV_DIM), F32)
        for j in range(4):
            dxbc = dxbc + cw_ref[j:j + 1, :] * ext2[pl.ds(3 - j, CHUNK), :]
            dcw_ref[j:j + 1, :] += jnp.sum(dxc * ext[pl.ds(5 + j, CHUNK), :], axis=0, keepdims=True)
        ext2[CHUNK:CHUNK + 8, :] = dxc[0:8, :]
        dcb_ref[...] += jnp.sum(dxc, axis=0, keepdims=True)
        dvec_ref[0:1, :] += jnp.sum(du, axis=0, keepdims=True)
        dvec_ref[1:2, :] += jnp.sum(da * p["a"], axis=0, keepdims=True)
        dvec_ref[2:3, :] += jnp.sum(dy * xs, axis=0, keepdims=True)
        dvec_ref[3:4, :] += jnp.sum(dy3_ * y2 * r, axis=0, keepdims=True)
        dx_ref[...] = jnp.concatenate([dxbc, dz, du], axis=1).astype(BF)

    return _call(body, name="ssd_bwd", grid=(B, N_CHUNK), in_specs=[row, halo, hp, y] + params,
                 out_specs=[row, const(4, SSD_CONV_DIM), const(1, SSD_CONV_DIM), const(8, SSD_W)],
                 out_shape=[jax.ShapeDtypeStruct((T, W_SSD), BF), jax.ShapeDtypeStruct((4, SSD_CONV_DIM), F32),
                            jax.ShapeDtypeStruct((1, SSD_CONV_DIM), F32), jax.ShapeDtypeStruct((8, SSD_W), F32)],
                 scratch=[pltpu.VMEM((8 + CHUNK, SSD_CONV_DIM), F32), pltpu.VMEM((8 + CHUNK, SSD_CONV_DIM), F32),
                          pltpu.VMEM((SSD_W, SSD_STATE), F32)],
                 sem=("arbitrary", "arbitrary"))(sin, sin, hprev, dy3, conv_w, conv_b, dtb, alog, dskip, norm_g)


def _sgu_core(uv_ref, g_ref, b_ref, w_ref, bias_ref):
    x = uv_ref[...]
    cdf = 0.5 * (1.0 + lax.erf(x * (2.0 ** -0.5)))
    ge = x * cdf
    dge = cdf + x * jnp.exp(-0.5 * x * x) * ((2.0 * math.pi) ** -0.5)
    u, v = ge[:, 0:SGU_W], ge[:, SGU_W:]
    vc = v - jnp.mean(v, axis=-1, keepdims=True)
    rstd = lax.rsqrt(jnp.mean(vc * vc, axis=-1, keepdims=True) + LN_EPS)
    vhat = vc * rstd
    vn = vhat * g_ref[...] + b_ref[...]
    tri = _tri()
    wc = [jnp.where(tri, w_ref[gi], 0.0).astype(BF) for gi in range(4)]
    vm = [jnp.where(_lane_mask(gi % 2), vn[:, 128 * (gi // 2):128 * (gi // 2 + 1)], 0.0).astype(BF) for gi in range(4)]
    mixed = jnp.concatenate([_dot(wc[2 * t], vm[2 * t]) + _dot(wc[2 * t + 1], vm[2 * t + 1]) for t in range(2)],
                            axis=1) + bias_ref[...]
    return dict(dge=dge, u=u, rstd=rstd, vhat=vhat, wc=wc, vm=vm, mixed=mixed)


def _sgu_specs():
    vec = pl.BlockSpec((1, SGU_W), lambda i: (0, 0))
    return [pl.BlockSpec((CHUNK, W_UV), lambda i: (i, 0)), vec, vec,
            pl.BlockSpec((4, CHUNK, CHUNK), lambda i: (0, 0, 0)), pl.BlockSpec((CHUNK, SGU_W), lambda i: (0, 0))]


def _sgu_fwd(uv, ln_g, ln_b, w, bias):
    T = uv.shape[0]

    def body(uv_ref, g_ref, b_ref, w_ref, bias_ref, y_ref):
        s = _sgu_core(uv_ref, g_ref, b_ref, w_ref, bias_ref)
        y_ref[...] = s["u"] * s["mixed"]

    return _call(body, name="sgu_fwd", grid=(T // CHUNK,), in_specs=_sgu_specs(),
                 out_specs=pl.BlockSpec((CHUNK, SGU_W), lambda i: (i, 0)),
                 out_shape=jax.ShapeDtypeStruct((T, SGU_W), F32), sem=("parallel",))(uv, ln_g, ln_b, w, bias)


def _sgu_bwd(uv, dy, ln_g, ln_b, w, bias):
    T = uv.shape[0]

    def body(uv_ref, dy_ref, g_ref, b_ref, w_ref, bias_ref, dx_ref, dw_ref, dbias_ref, dln_ref):
        @pl.when(pl.program_id(0) == 0)
        def _():
            dw_ref[...] = jnp.zeros_like(dw_ref)
            dbias_ref[...] = jnp.zeros_like(dbias_ref)
            dln_ref[...] = jnp.zeros_like(dln_ref)

        s = _sgu_core(uv_ref, g_ref, b_ref, w_ref, bias_ref)
        dy_ = dy_ref[...]
        du = dy_ * s["mixed"]
        dmix = dy_ * s["u"]
        dbias_ref[...] += dmix
        tri = _tri()
        dvn_t = []
        for t in range(2):
            acc = jnp.zeros((CHUNK, 128), F32)
            for e in range(2):
                gi = 2 * t + e
                dmg = jnp.where(_lane_mask(e), dmix[:, 128 * t:128 * (t + 1)], 0.0).astype(BF)
                acc = acc + _dot_tn(s["wc"][gi], dmg)
                dw_ref[gi] += jnp.where(tri, _dot_nt(dmg, s["vm"][gi]), 0.0)
            dvn_t.append(acc)
        dvn = jnp.concatenate(dvn_t, axis=1)
        dln_ref[0:1, :] += jnp.sum(dvn * s["vhat"], axis=0, keepdims=True)
        dln_ref[1:2, :] += jnp.sum(dvn, axis=0, keepdims=True)
        dvh = dvn * g_ref[...]
        dv = s["rstd"] * (dvh - jnp.mean(dvh, axis=-1, keepdims=True)
                          - s["vhat"] * jnp.mean(dvh * s["vhat"], axis=-1, keepdims=True))
        dx_ref[...] = (jnp.concatenate([du, dv], axis=1) * s["dge"]).astype(BF)

    ins = _sgu_specs()
    return _call(body, name="sgu_bwd", grid=(T // CHUNK,),
                 in_specs=[ins[0], pl.BlockSpec((CHUNK, SGU_W), lambda i: (i, 0))] + ins[1:],
                 out_specs=[pl.BlockSpec((CHUNK, W_UV), lambda i: (i, 0)),
                            pl.BlockSpec((4, CHUNK, CHUNK), lambda i: (0, 0, 0)),
                            pl.BlockSpec((CHUNK, SGU_W), lambda i: (0, 0)), pl.BlockSpec((8, SGU_W), lambda i: (0, 0))],
                 out_shape=[jax.ShapeDtypeStruct((T, W_UV), BF), jax.ShapeDtypeStruct((4, CHUNK, CHUNK), F32),
                            jax.ShapeDtypeStruct((CHUNK, SGU_W), F32), jax.ShapeDtypeStruct((8, SGU_W), F32)],
                 sem=("arbitrary",))(uv, dy, ln_g, ln_b, w, bias)


def _adamw(w, g, m, v):
    R, C = w.shape
    tr = R

    def body(w_ref, g_ref, m_ref, v_ref, d_ref, nm_ref, nv_ref):
        g_ = g_ref[...]
        m2 = ADAM_B1 * m_ref[...] + (1.0 - ADAM_B1) * g_
        v2 = ADAM_B2 * v_ref[...] + (1.0 - ADAM_B2) * (g_ * g_)
        m_hat = m2 / (1.0 - ADAM_B1 ** ADAM_STEP)
        v_hat = v2 / (1.0 - ADAM_B2 ** ADAM_STEP)
        d_ref[...] = -ADAM_LR * (m_hat / (jnp.sqrt(v_hat) + ADAM_EPS) + ADAM_WD * w_ref[...])
        nm_ref[...] = m2
        nv_ref[...] = v2

    blk = pl.BlockSpec((tr, C), lambda i: (i, 0))
    sh = jax.ShapeDtypeStruct((R, C), F32)
    return _call(body, name="adamw", grid=(R // tr,), in_specs=[blk] * 4, out_specs=[blk] * 3,
                 out_shape=[sh] * 3, sem=("parallel",))(w, g, m, v)


def _adamw_pair(w, g0, g1, m, v, dep):
    L, R, C = w.shape
    tr = max(t for t in range(8, R + 1, 8) if R % t == 0 and t * C * 4 <= 3 * 2 ** 19)

    def body(w_ref, g0_ref, g1_ref, m_ref, v_ref, dep_ref, d_ref, nm_ref, nv_ref, og_ref):
        g_ = jnp.where(pl.program_id(0) == 0, g0_ref[...], g1_ref[...])
        m2 = ADAM_B1 * m_ref[...] + (1.0 - ADAM_B1) * g_
        v2 = ADAM_B2 * v_ref[...] + (1.0 - ADAM_B2) * (g_ * g_)
        m_hat = m2 / (1.0 - ADAM_B1 ** ADAM_STEP)
        v_hat = v2 / (1.0 - ADAM_B2 ** ADAM_STEP)
        d_ref[...] = -ADAM_LR * (m_hat / (jnp.sqrt(v_hat) + ADAM_EPS) + ADAM_WD * w_ref[...])
        nm_ref[...] = m2
        nv_ref[...] = v2
        og_ref[...] = g_

    lay = pl.BlockSpec((None, tr, C), lambda l, i: (l, i, 0))
    one = lambda k: pl.BlockSpec((tr, C), lambda l, i: (jnp.where(l == k, i, 0), 0))
    return _call(body, name="adamw_pair", grid=(L, R // tr),
                 in_specs=[lay, one(0), one(1), lay, lay, pl.BlockSpec((8, 128), lambda l, i: (0, 0))],
                 out_specs=[lay] * 4,
                 out_shape=[jax.ShapeDtypeStruct((L, R, C), F32)] * 4,
                 sem=("parallel", "parallel"))(w, g0, g1, m, v, dep)


def _row_steps(rows):
    return 2 if rows % 32 == 0 else 1


def _pair_add(gbufs, rsibs, c):
    n = len(gbufs)
    steps = min(_row_steps(g.shape[2]) for g in gbufs)

    def body(c_ref, *refs):
        for a_ref, b_ref, o_ref in zip(refs[:n], refs[n:2 * n], refs[2 * n:]):
            o_ref[...] = (a_ref[...] + b_ref[...]).astype(BF)

    def specs(g):
        tr, C = g.shape[2] // steps, g.shape[3]
        return (pl.BlockSpec((None, None, tr, C), lambda j, i, c_ref: (j, c_ref[0], i, 0)),
                pl.BlockSpec((None, tr, C), lambda j, i, c_ref: (j, i, 0)))

    return list(pl.pallas_call(
        body, name="rs_pair_add",
        grid_spec=pltpu.PrefetchScalarGridSpec(
            num_scalar_prefetch=1, grid=(4, steps),
            in_specs=[specs(g)[0] for g in gbufs] + [specs(g)[1] for g in gbufs],
            out_specs=[specs(g)[1] for g in gbufs]),
        out_shape=[jax.ShapeDtypeStruct((4,) + g.shape[2:], BF) for g in gbufs],
        compiler_params=pltpu.CompilerParams(dimension_semantics=("parallel", "parallel")),
    )(jnp.reshape(c, (1,)).astype(jnp.int32), *gbufs, *rsibs))


def _chip_sum(pairs, recvs, me, c):
    n = len(pairs)
    steps = min(_row_steps(p.shape[1]) for p in pairs)

    def body(s_ref, *refs):
        for own_ref, p_ref, o_ref in zip(refs[:n], refs[n:2 * n], refs[2 * n:]):
            p = [jnp.where(s_ref[0] == j, own_ref[...], p_ref[j]).astype(F32) for j in range(4)]
            o_ref[...] = ((p[0] + p[1]) + p[2]) + p[3]

    def specs(p):
        tr, C = p.shape[1] // steps, p.shape[2]
        return (pl.BlockSpec((None, tr, C), lambda i, s: (s[0], i, 0)), pl.BlockSpec((4, tr, C), lambda i, s: (0, i, 0)),
                pl.BlockSpec((None, tr, C), lambda i, s: (s[1], i, 0)))

    return list(pl.pallas_call(
        body, name="rs_chip_sum",
        grid_spec=pltpu.PrefetchScalarGridSpec(
            num_scalar_prefetch=1, grid=(steps,),
            in_specs=[specs(p)[0] for p in pairs] + [specs(p)[1] for p in pairs],
            out_specs=[specs(p)[2] for p in pairs]),
        out_shape=[jax.ShapeDtypeStruct((2,) + p.shape[1:], F32) for p in pairs],
        compiler_params=pltpu.CompilerParams(dimension_semantics=("parallel",)),
    )(jnp.stack([me, c]).astype(jnp.int32), *pairs, *recvs))


MESH = pl.DeviceIdType.MESH
ANY = pl.BlockSpec(memory_space=pl.ANY)


def _place():
    x, y, c = lax.axis_index("x"), lax.axis_index("y"), lax.axis_index("c")
    return x, y, c, [(1 - x, y), (x, 1 - y), (1 - x, 1 - y)]


HBM = pl.BlockSpec(memory_space=pltpu.HBM)
SEM = pl.BlockSpec(memory_space=pltpu.SEMAPHORE)
EFFECT = pltpu.SideEffectType.DATAFLOW_SIDE_EFFECTING


class _Split:
    def __init__(self, tag, arrays, copies, n_copies, after=()):
        self.tag, self.copies, k = tag, copies, len(arrays)

        def body(*refs):
            sems = k + len(after)
            for cp in copies(refs[:k], refs[sems], refs[sems + 1]):
                cp.start()
            refs[-1][...] = jnp.zeros_like(refs[-1])

        out = pl.pallas_call(
            body, name=tag + "_start",
            out_shape=(pltpu.SemaphoreType.DMA((n_copies,)), pltpu.SemaphoreType.DMA((n_copies,)),
                       *[pltpu.HBM(a.shape, a.dtype) for a in arrays], jax.ShapeDtypeStruct((8, 128), F32)),
            in_specs=[HBM] * k + [ANY] * len(after),
            out_specs=(SEM, SEM, *[HBM] * k, pl.BlockSpec(memory_space=pltpu.VMEM)),
            input_output_aliases={i: 2 + i for i in range(k)},
            compiler_params=pltpu.CompilerParams(has_side_effects=EFFECT),
        )(*[pltpu.with_memory_space_constraint(a, pltpu.HBM) for a in arrays], *after)
        self.send, self.recv, self.arrays, self.token_array = out[0], out[1], list(out[2:2 + k]), out[-1]
        self.token = self.token_array[0, 0]

    def wait(self, after):
        k, copies = len(self.arrays), self.copies
        after = list(after) if isinstance(after, (list, tuple)) else [after]

        def body(*refs):
            for cp in copies(refs[:k], refs[k], refs[k + 1]):
                cp.wait_send()
                cp.wait_recv()

        return list(pl.pallas_call(
            body, name=self.tag + "_wait", out_shape=tuple(pltpu.HBM(a.shape, a.dtype) for a in self.arrays),
            in_specs=[HBM] * k + [SEM, SEM] + [ANY] * len(after), out_specs=tuple([HBM] * k),
            input_output_aliases={i: i for i in range(k)},
            compiler_params=pltpu.CompilerParams(has_side_effects=EFFECT),
        )(*self.arrays, self.send, self.recv, *after))


def _landing_zones(arrs):
    me = 2 * lax.axis_index("x") + lax.axis_index("y")
    return [lax.dynamic_update_index_in_dim(lax.empty((4,) + a.shape, a.dtype), a, me, 0) for a in arrs]


def _gather_start(arrs, lands, tag, after=()):
    n = len(arrs)

    def copies(refs, send, recv):
        x, y, c, chips = _place()
        return [pltpu.make_async_remote_copy(
            src_ref=refs[k], dst_ref=refs[n + k].at[2 * x + y], send_sem=send.at[3 * k + r],
            recv_sem=recv.at[3 * k + r], device_id=(px, py, c), device_id_type=MESH)
            for k in range(n) for r, (px, py) in enumerate(chips)]

    return _Split("gather_" + tag, list(arrs) + lands, copies, 3 * n, after)


def _gather_halves_start(arrs, tag):
    n = len(arrs)
    lands = _landing_zones(arrs)

    def copies(refs, send, recv):
        x, y, c, chips = _place()
        return [pltpu.make_async_remote_copy(
            src_ref=refs[k].at[c], dst_ref=refs[n + k].at[2 * x + y, c], send_sem=send.at[3 * k + r],
            recv_sem=recv.at[3 * k + r], device_id=(px, py, c), device_id_type=MESH)
            for k in range(n) for r, (px, py) in enumerate(chips)]

    return _Split("gather_" + tag, list(arrs) + lands, copies, 3 * n)


def _gather_halves_finish(lands, tag):
    n = len(lands)

    def copies(refs, send, recv):
        x, y, c, chips = _place()
        return [pltpu.make_async_remote_copy(
            src_ref=refs[k].at[2 * px + py, c], dst_ref=refs[k].at[2 * px + py, c], send_sem=send.at[3 * k + r],
            recv_sem=recv.at[3 * k + r], device_id=(x, y, 1 - c), device_id_type=MESH)
            for k in range(n) for r, (px, py) in enumerate(chips)]

    return _Split("gather_pass_" + tag, list(lands), copies, 3 * n)


def _part_sibling(gbufs):
    n = len(gbufs)

    def copies(refs, send, recv, off):
        x, y, c, _ = _place()
        return [pltpu.make_async_remote_copy(
            src_ref=refs[k].at[j, 1 - c], dst_ref=refs[n + k].at[j], send_sem=send.at[off + 4 * k + j],
            recv_sem=recv.at[off + 4 * k + j], device_id=(x, y, 1 - c), device_id_type=MESH)
            for k in range(n) for j in range(4)]

    return list(gbufs) + [lax.empty((4,) + g.shape[2:], g.dtype) for g in gbufs], 4 * n, copies


def _part_chips(pbufs):
    n = len(pbufs)

    def copies(refs, send, recv, off):
        x, y, c, chips = _place()
        return [pltpu.make_async_remote_copy(
            src_ref=refs[k].at[2 * px + py], dst_ref=refs[n + k].at[2 * x + y], send_sem=send.at[off + 3 * k + r],
            recv_sem=recv.at[off + 3 * k + r], device_id=(px, py, c), device_id_type=MESH)
            for k in range(n) for r, (px, py) in enumerate(chips)]

    return list(pbufs) + [lax.empty(p.shape, p.dtype) for p in pbufs], 3 * n, copies


def _part_join(fulls):
    def copies(refs, send, recv, off):
        x, y, c, _ = _place()
        return [pltpu.make_async_remote_copy(
            src_ref=refs[k].at[c], dst_ref=refs[k].at[c], send_sem=send.at[off + k], recv_sem=recv.at[off + k],
            device_id=(x, y, 1 - c), device_id_type=MESH) for k in range(len(fulls))]

    return list(fulls), len(fulls), copies


def _start_parts(parts, tag):
    arrays, spans, total = [], [], 0
    for arrs, n_copies, fn in parts:
        spans.append((len(arrays), len(arrs), total, fn))
        arrays += arrs
        total += n_copies

    def copies(refs, send, recv):
        return [cp for a0, na, off, fn in spans for cp in fn(refs[a0:a0 + na], send, recv, off)]

    op = _Split(tag, arrays, copies, total)
    op.spans = [(a0, na) for a0, na, _, _ in spans]
    return op


def _all_reduce_small(v):
    R, C = v.shape

    def body(v_ref, o_ref, g_ref, send, recv, loc):
        x, y, c, chips = _place()
        me, sibling = (x, y, c), (x, y, 1 - c)

        def rows(px, py, pc):
            return g_ref.at[4 * px + 2 * py + pc]

        def copy(k, block, to, src=None):
            return pltpu.make_async_remote_copy(
                src_ref=rows(*block) if src is None else src, dst_ref=rows(*block),
                send_sem=send.at[k], recv_sem=recv.at[k], device_id=to, device_id_type=MESH)

        mine = pltpu.make_async_copy(v_ref, rows(*me), loc)
        mine.start()
        first = [copy(0, me, sibling, src=v_ref)]
        first += [copy(1 + j, me, (*chip, c), src=v_ref) for j, chip in enumerate(chips)]
        for cp in first:
            cp.start()
        passed = [copy(4 + j, (*chip, c), sibling) for j, chip in enumerate(chips)]
        for j, chip in enumerate(chips):
            copy(1 + j, (*chip, c), me).wait_recv()
            passed[j].start()
        copy(0, sibling, me).wait_recv()
        for j, chip in enumerate(chips):
            copy(4 + j, (*chip, 1 - c), me).wait_recv()
        for cp in first + passed:
            cp.wait_send()
        mine.wait()
        acc = g_ref[0]
        for d in range(1, 8):
            acc = acc + g_ref[d]
        o_ref[...] = acc

    vm = pl.BlockSpec(memory_space=pltpu.VMEM)
    return pl.pallas_call(
        body, name="all_reduce_small", in_specs=[vm], out_specs=[vm, vm],
        out_shape=[jax.ShapeDtypeStruct((R, C), F32), jax.ShapeDtypeStruct((8, R, C), F32)],
        scratch_shapes=[pltpu.SemaphoreType.DMA((7,)), pltpu.SemaphoreType.DMA((7,)), pltpu.SemaphoreType.DMA],
    )(v)[0]


WEIGHTS = ['ffn1_norm', 'ffn1_w_gate', 'ffn1_w_up', 'ffn1_w_down', 'mix_norm', 'w_in', 'conv_w', 'conv_b', 'dt_bias',
           'a_log', 'd_skip', 'ssd_norm', 'sgu_ln_g', 'sgu_ln_b', 'sgu_w', 'sgu_b', 'w_out', 'ffn2_norm',
           'ffn2_w_gate', 'ffn2_w_up', 'ffn2_w_down', 'final_norm']
SHARDED = ['ffn1_w_gate', 'ffn1_w_up', 'ffn1_w_down', 'w_in', 'conv_w', 'w_out', 'ffn2_w_gate', 'ffn2_w_up',
           'ffn2_w_down']
SMALL = [n for n in WEIGHTS if n not in SHARDED]
GROUPS = [("ffn1", ["ffn1_w_gate", "ffn1_w_up", "ffn1_w_down"]), ("mix", ["w_in", "conv_w", "w_out"]),
          ("ffn2", ["ffn2_w_gate", "ffn2_w_up", "ffn2_w_down"])]
TRANSPOSED = ("ffn1_w_gate", "ffn1_w_up", "ffn2_w_gate", "ffn2_w_up")
DEPTH = 2


def _pack_w_in(w):
    return jnp.concatenate([w[..., 0:1152], w[..., 1536:2432], w[..., 1152:1536],
                            jnp.repeat(w[..., 2432:2438], HEAD, axis=-1), w[..., 2438:2950]], axis=-1)


def _unpack_w_in(dq, ds, du):
    return jnp.concatenate([dq, ds[:, 896:1280], ds[:, 0:896], ds[:, 1280::HEAD], du], axis=-1)


def _ffn_fwd(x, g, wg, wu, wd):
    xo, hb, S1, S2, A = _ffn_fwd_k(x, g, wg, wu, wd)
    return xo, (x, hb, S1, S2, A)


def _ffn_bwd_weights(dxo, saved, wd):
    x, hb, S1, S2, A = saved
    dG, dU, dyb = _ffn_bwd_act(dxo, S1, S2, wd)
    return (dG, dU), _ffn_bwd_k2(hb, dyb, A, dG, dU)


def _ffn_bwd_input(dxo, saved, mids, g, wg, wu):
    return _ffn_bwd_dx(mids[0], mids[1], wg, wu, saved[0], g, dxo)


def _mix_fwd(x, P):
    hb, qkv, sin, uv = _mix_proj(x, P["mix_norm"], P["w_in"])
    y_att, lse = _attn_combine([_attn_fwd(qkv, d) for d in DILATIONS])
    y_ssd, hprev = _ssd_fwd(sin, *P["ssd"])
    y_sgu = _sgu_fwd(uv, *P["sgu"])
    ycat = jnp.concatenate([y_att, y_ssd, y_sgu], axis=1).astype(BF)
    return _mm_nn(ycat, P["w_out"], res=x), (x, hb, qkv, sin, uv, y_att, lse, hprev, ycat)


def _mix_bwd_weights(dxo, saved, P):
    x, hb, qkv, sin, uv, y_att, lse, hprev, ycat = saved
    dy_att, dy_ssd, dy_sgu = _mix_bwd_dy(dxo, P["w_out"])
    dwout = _mm_tn(ycat, dxo)
    dqkv = _sum_branches([_attn_bwd(qkv, dy_att, y_att, lse, d) for d in DILATIONS])
    dsin, dcw, dcb, dvec = _ssd_bwd(sin, hprev, dy_ssd, *P["ssd"])
    duv, dsw, dsbias, dln = _sgu_bwd(uv, dy_sgu, *P["sgu"])
    dwin = _unpack_w_in(_mm_tn(hb, dqkv), _mm_tn(hb, dsin), _mm_tn(hb, duv))
    grads = dict(
        w_in=dwin, conv_w=dcw, conv_b=dcb[0], dt_bias=dvec[0, ::HEAD], a_log=dvec[1, ::HEAD],
        d_skip=jnp.sum(dvec[2].reshape(6, HEAD), axis=-1), ssd_norm=dvec[3], sgu_ln_g=dln[0], sgu_ln_b=dln[1],
        sgu_w=dsw, sgu_b=jnp.sum(dsbias.reshape(CHUNK, 4, HEAD), axis=-1).T, w_out=dwout)
    return (dqkv, dsin, duv), grads


def _mix_bwd_input(dxo, saved, mids, P):
    return _mix_bwd_dx(*mids, P["w_in"], saved[0], P["mix_norm"], dxo)


def _halved(g):
    rows = g.size // g.shape[-1]
    return g.reshape(4, 2, rows // 8, g.shape[-1])


def kernel(x, ffn1_norm, ffn1_w_gate, ffn1_w_up, ffn1_w_down, mix_norm, w_in, conv_w, conv_b, dt_bias, a_log, d_skip, ssd_norm, sgu_ln_g, sgu_ln_b, sgu_w, sgu_b, w_out, ffn2_norm, ffn2_w_gate, ffn2_w_up, ffn2_w_down, final_norm, loss_target, m_ffn1_norm, m_ffn1_w_gate, m_ffn1_w_up, m_ffn1_w_down, m_mix_norm, m_w_in, m_conv_w, m_conv_b, m_dt_bias, m_a_log, m_d_skip, m_ssd_norm, m_sgu_ln_g, m_sgu_ln_b, m_sgu_w, m_sgu_b, m_w_out, m_ffn2_norm, m_ffn2_w_gate, m_ffn2_w_up, m_ffn2_w_down, m_final_norm, v_ffn1_norm, v_ffn1_w_gate, v_ffn1_w_up, v_ffn1_w_down, v_mix_norm, v_w_in, v_conv_w, v_conv_b, v_dt_bias, v_a_log, v_d_skip, v_ssd_norm, v_sgu_ln_g, v_sgu_ln_b, v_sgu_w, v_sgu_b, v_w_out, v_ffn2_norm, v_ffn2_w_gate, v_ffn2_w_up, v_ffn2_w_down, v_final_norm):
    given = dict(x=x, ffn1_norm=ffn1_norm, ffn1_w_gate=ffn1_w_gate, ffn1_w_up=ffn1_w_up, ffn1_w_down=ffn1_w_down, mix_norm=mix_norm, w_in=w_in, conv_w=conv_w, conv_b=conv_b, dt_bias=dt_bias, a_log=a_log, d_skip=d_skip, ssd_norm=ssd_norm, sgu_ln_g=sgu_ln_g, sgu_ln_b=sgu_ln_b, sgu_w=sgu_w, sgu_b=sgu_b, w_out=w_out, ffn2_norm=ffn2_norm, ffn2_w_gate=ffn2_w_gate, ffn2_w_up=ffn2_w_up, ffn2_w_down=ffn2_w_down, final_norm=final_norm, loss_target=loss_target, m_ffn1_norm=m_ffn1_norm, m_ffn1_w_gate=m_ffn1_w_gate, m_ffn1_w_up=m_ffn1_w_up, m_ffn1_w_down=m_ffn1_w_down, m_mix_norm=m_mix_norm, m_w_in=m_w_in, m_conv_w=m_conv_w, m_conv_b=m_conv_b, m_dt_bias=m_dt_bias, m_a_log=m_a_log, m_d_skip=m_d_skip, m_ssd_norm=m_ssd_norm, m_sgu_ln_g=m_sgu_ln_g, m_sgu_ln_b=m_sgu_ln_b, m_sgu_w=m_sgu_w, m_sgu_b=m_sgu_b, m_w_out=m_w_out, m_ffn2_norm=m_ffn2_norm, m_ffn2_w_gate=m_ffn2_w_gate, m_ffn2_w_up=m_ffn2_w_up, m_ffn2_w_down=m_ffn2_w_down, m_final_norm=m_final_norm, v_ffn1_norm=v_ffn1_norm, v_ffn1_w_gate=v_ffn1_w_gate, v_ffn1_w_up=v_ffn1_w_up, v_ffn1_w_down=v_ffn1_w_down, v_mix_norm=v_mix_norm, v_w_in=v_w_in, v_conv_w=v_conv_w, v_conv_b=v_conv_b, v_dt_bias=v_dt_bias, v_a_log=v_a_log, v_d_skip=v_d_skip, v_ssd_norm=v_ssd_norm, v_sgu_ln_g=v_sgu_ln_g, v_sgu_ln_b=v_sgu_ln_b, v_sgu_w=v_sgu_w, v_sgu_b=v_sgu_b, v_w_out=v_w_out, v_ffn2_norm=v_ffn2_norm, v_ffn2_w_gate=v_ffn2_w_gate, v_ffn2_w_up=v_ffn2_w_up, v_ffn2_w_down=v_ffn2_w_down, v_final_norm=v_final_norm)
    T = given["x"].shape[0] * given["x"].shape[1]
    D = given["x"].shape[2]
    x0 = given["x"].reshape(T, D)
    tgt = given["loss_target"].reshape(T, D)
    c = lax.axis_index("c")

    bf = {n: given[n].astype(BF) for n in SHARDED if n not in ("w_in", "conv_w")}
    bf["w_in"] = _pack_w_in(given["w_in"]).astype(BF)
    bf["conv_w"] = given["conv_w"]
    first_key = (0, GROUPS[0][0])
    first = [bf[n][0].reshape((2, bf[n].shape[1] // 2) + bf[n].shape[2:]) for n in GROUPS[0][1]]
    gathers = {first_key: _gather_halves_start(first, "l0_" + GROUPS[0][0])}
    later = {(i, gname): [bf[n][i] for n in names] for i in range(DEPTH) for gname, names in GROUPS
             if (i, gname) != first_key}
    zones = {key: _landing_zones(arrs) for key, arrs in later.items()}

    def gathered(i, gname, after):
        if (i, gname) != first_key:
            return gathers[(i, gname)].wait(after)[3:]
        got = gathers[first_key].wait([after] + [z for zs in zones.values() for z in zs])[3:]
        got = _gather_halves_finish(got, "l0_" + gname).wait(after)
        prev = got[0]
        for key, arrs in later.items():
            gathers[key] = _gather_start(arrs, zones[key], f"l{key[0]}_{key[1]}", after=[prev])
            prev = gathers[key].token_array
        return [z.reshape((4, 2 * z.shape[2]) + z.shape[3:]) for z in got]

    def mix_params(i, got):
        win = got[0].reshape(D, W_QKV + W_SSD + W_UV)
        rep = lambda v: jnp.repeat(v, HEAD)[None]
        ssd = (got[1].transpose(1, 0, 2).reshape(4, SSD_CONV_DIM), given["conv_b"][i][None],
               rep(given["dt_bias"][i]), rep(given["a_log"][i]), rep(given["d_skip"][i]), given["ssd_norm"][i][None])
        sgu = (given["sgu_ln_g"][i][None], given["sgu_ln_b"][i][None], given["sgu_w"][i],
               jnp.repeat(given["sgu_b"][i].T, HEAD, axis=1))
        return dict(mix_norm=given["mix_norm"][i][None], w_in=win, w_out=got[2].reshape(-1, D), ssd=ssd, sgu=sgu)

    x = x0
    tape = []
    for i in range(DEPTH):
        got = gathered(i, "ffn1", x)
        token = functools.reduce(lambda a, b: a + b, [g.token for g in gathers.values()]) if i == 0 else 0.0
        P = dict(ffn1=(given["ffn1_norm"][i][None] + token, *got))
        x, s1 = _ffn_fwd(x, *P["ffn1"])
        P.update(mix_params(i, gathered(i, "mix", x)))
        x, s2 = _mix_fwd(x, P)
        P["ffn2"] = (given["ffn2_norm"][i][None], *gathered(i, "ffn2", x))
        x, s3 = _ffn_fwd(x, *P["ffn2"])
        tape.append((P, s1, s2, s3))
    loss_part, dx, dgf = _final_loss(x, given["final_norm"][None], tgt)

    me = 2 * lax.axis_index("x") + lax.axis_index("y")
    jobs = []

    flight = dict(op=None, owners=[], ticks=0)

    def tick(after, begin=None):
        parts, owners = [], []
        if flight["op"] is not None:
            got = flight["op"].wait(after)
            for job, (a0, na) in zip(flight["owners"], flight["op"].spans):
                mine, k = got[a0:a0 + na], len(job["names"])
                if job["stage"] == 1:
                    parts.append(_part_chips(_pair_add(mine[:k], mine[k:], c)))
                elif job["stage"] == 2:
                    parts.append(_part_join(_chip_sum(mine[:k], mine[k:], me, c)))
                else:
                    job.update(stage=4, out=dict(zip(job["names"], mine)))
                    continue
                job["stage"] += 1
                owners.append(job)
        if begin is not None:
            i, gname, gd = begin
            names = [n for n in dict(GROUPS)[gname] if n != "conv_w"]
            jobs.append(dict(key=(i, gname), names=names, stage=1))
            parts.append(_part_sibling([_halved(gd[n]) for n in names]))
            owners.append(jobs[-1])
        flight.update(op=_start_parts(parts, f"rs_tick{flight['ticks']}") if parts else None, owners=owners,
                      ticks=flight["ticks"] + 1)
        return flight["op"].token if parts else 0.0

    grads = [dict() for _ in range(DEPTH)]
    for i in reversed(range(DEPTH)):
        P, s1, s2, s3 = tape[i]
        g = grads[i]
        norm, wg, wu, wd = P["ffn2"]
        mids, (g["ffn2_w_gate"], g["ffn2_w_up"], g["ffn2_w_down"]) = _ffn_bwd_weights(dx, s3, wd)
        tok = tick(g["ffn2_w_down"], (i, "ffn2", g))
        dx, dn2 = _ffn_bwd_input(dx, s3, mids, norm + tok, wg, wu)
        mids, gm = _mix_bwd_weights(dx, s2, P)
        g.update(gm)
        tok = tick(gm["w_in"], (i, "mix", g))
        dx, dnm = _mix_bwd_input(dx, s2, mids, {**P, "mix_norm": P["mix_norm"] + tok})
        norm, wg, wu, wd = P["ffn1"]
        mids, (g["ffn1_w_gate"], g["ffn1_w_up"], g["ffn1_w_down"]) = _ffn_bwd_weights(dx, s1, wd)
        tok = tick(g["ffn1_w_down"], (i, "ffn1", g))
        dx, dn1 = _ffn_bwd_input(dx, s1, mids, norm + tok, wg, wu)
        g["ffn1_norm"], g["mix_norm"], g["ffn2_norm"] = dn1[0], dnm[0], dn2[0]
    grad_x = dx.reshape(given["x"].shape)

    order = [n for n in SMALL if n != "final_norm"] + ["final_norm"]
    small = [jnp.stack([grads[i][n] for i in range(DEPTH)]) for n in order[:-1] + ["conv_w"]]
    small = small[:-1] + [dgf[0], small[-1], loss_part[0, 0:1]]
    n_small = sum(s.size for s in small)
    rows_small = -(-n_small // (128 * 8)) * 8

    def flat(arrs):
        fill = rows_small * 128 - sum(a.size for a in arrs)
        return jnp.concatenate([a.reshape(-1) for a in arrs] + [jnp.zeros((fill,), F32)]).reshape(rows_small, 128)

    gsmall = _all_reduce_small(flat(small)).reshape(-1)

    grad_w = {}
    off = 0
    for n in order:
        size = given[n].size
        grad_w[n] = gsmall[off:off + size].reshape(given[n].shape)
        off += size
    cw = gsmall[off:off + 2 * 4 * SSD_CONV_DIM].reshape(DEPTH, 4, SSD_CONV_DIM)
    grad_w["conv_w"] = lax.dynamic_slice_in_dim(cw, me * (SSD_CONV_DIM // 4), SSD_CONV_DIM // 4, axis=2)
    loss = gsmall[off + 2 * 4 * SSD_CONV_DIM]

    delta, new_m, new_v = {}, {}, {}
    shp = given["conv_w"].shape
    d, m2, v2 = _adamw(*[a.reshape(shp[0] * shp[1], shp[2])
                         for a in (given["conv_w"], grad_w["conv_w"], given["m_conv_w"], given["v_conv_w"])])
    delta["conv_w"], new_m["conv_w"], new_v["conv_w"] = d.reshape(shp), m2.reshape(shp), v2.reshape(shp)
    packed = [flat([given[pre + n] for n in order]) for pre in ("", "m_", "v_")]
    small_out = _adamw(packed[0], gsmall.reshape(rows_small, 128), packed[1], packed[2])
    outs = [o.reshape(-1) for o in small_out]
    off = 0
    for n in order:
        size = given[n].size
        for dst, o in zip((delta, new_m, new_v), outs):
            dst[n] = o[off:off + size].reshape(given[n].shape)
        off += size

    stepped, arrived = {}, {}

    def update_arrived(dep):
        out = None
        for job in jobs:
            if job["stage"] == 4 and not job.get("seen"):
                job["seen"] = True
                for n, full in job["out"].items():
                    view = (lambda a: jnp.swapaxes(a, 1, 2)) if n in TRANSPOSED else (lambda a: a)
                    arrived.setdefault(n, {})[job["key"][0]] = full.reshape(view(given[n]).shape[1:])
                    if len(arrived[n]) == DEPTH:
                        res = _adamw_pair(view(given[n]), arrived[n][0], arrived[n][1], view(given["m_" + n]),
                                          view(given["v_" + n]), dep)
                        stepped[n] = [view(r) for r in res]
                        out = res[0]
        return out

    after = small_out[0]
    while any(j["stage"] < 4 for j in jobs):
        done = update_arrived(jnp.zeros((8, 128), F32) + tok)
        after = after if done is None else done
        tok = tick(after)
    update_arrived(jnp.zeros((8, 128), F32) + tok)
    for n, (d, m2, v2, g) in stepped.items():
        delta[n], new_m[n], new_v[n], grad_w[n] = d, m2, v2, g

    return (loss, grad_x, *[grad_w[n] for n in WEIGHTS], *[delta[n] for n in WEIGHTS],
            *[new_m[n] for n in WEIGHTS], *[new_v[n] for n in WEIGHTS])
```

```python
import functools
import math

import jax
import jax.numpy as jnp
from jax import lax
from jax.experimental import pallas as pl
from jax.experimental.pallas import tpu as pltpu

F32 = jnp.float32
BF = jnp.bfloat16

RMS_EPS = 1e-6
LN_EPS = 1e-5
SEQ = 2048
CHUNK = 128
N_CHUNK = SEQ // CHUNK
ATT_W = 384
HEAD = 64
SSD_W = 384
SSD_CONV_DIM = 896
SSD_STATE = 128
SGU_W = 256
DILATIONS = (1, 4, 16)
W_QKV = 3 * ATT_W
W_SSD = SSD_CONV_DIM + SSD_W + SSD_W
W_UV = 2 * SGU_W
ADAM_LR = 0.001
ADAM_B1 = 0.9
ADAM_B2 = 0.999
ADAM_EPS = 1e-08
ADAM_WD = 0.01
ADAM_STEP = 10
NEG = -1e30
ATTN_BWD_VMEM = 48 * 2 ** 20
ATTN_SUBSEQ_PER_STEP = 4
FFN_VMEM = 60 * 2 ** 20


def _dot(a, b):
    return jnp.dot(a, b, preferred_element_type=F32)


def _dot_nt(a, b):
    return lax.dot_general(a, b, (((1,), (1,)), ((), ())), preferred_element_type=F32)


def _dot_tn(a, b):
    return lax.dot_general(a, b, (((0,), (0,)), ((), ())), preferred_element_type=F32)


def _sigmoid(x):
    return 1.0 / (1.0 + jnp.exp(-x))


def _call(body, *, name, grid, in_specs, out_specs, out_shape, scratch=(), sem=None, vmem=None):
    return pl.pallas_call(
        body, name=name, grid=grid, in_specs=in_specs, out_specs=out_specs, out_shape=out_shape,
        scratch_shapes=list(scratch),
        compiler_params=pltpu.CompilerParams(dimension_semantics=sem, vmem_limit_bytes=vmem),
    )


def _tile(n, want):
    t = min(n, want)
    while n % t:
        t //= 2
    return t


def _final_loss(x, g, tgt):
    T, D = x.shape
    tm = _tile(T, 512)

    def body(x_ref, g_ref, t_ref, l_ref, dx_ref, dg_ref):
        @pl.when(pl.program_id(0) == 0)
        def _():
            dg_ref[...] = jnp.zeros_like(dg_ref)
            l_ref[...] = jnp.zeros_like(l_ref)

        xf = x_ref[...]
        gg = g_ref[...]
        r = lax.rsqrt(jnp.mean(xf * xf, axis=-1, keepdims=True) + RMS_EPS)
        xn = xf * r
        e = xn * gg - t_ref[...]
        part = 0.5 * jnp.sum(jnp.mean(e * e, axis=-1, keepdims=True), axis=0, keepdims=True)
        l_ref[...] += jnp.broadcast_to(part, l_ref.shape)
        dy = e * (1.0 / D)
        u = dy * gg
        mu = jnp.mean(u * xf, axis=-1, keepdims=True)
        dx_ref[...] = r * (u - xf * (r * r * mu))
        dg_ref[...] += jnp.sum(dy * xn, axis=0, keepdims=True)

    row = pl.BlockSpec((tm, D), lambda i: (i, 0))
    vec = pl.BlockSpec((1, D), lambda i: (0, 0))
    lsp = pl.BlockSpec((1, 128), lambda i: (0, 0))
    return _call(body, name="final_loss", grid=(T // tm,), in_specs=[row, vec, row], out_specs=[lsp, row, vec],
                 out_shape=[jax.ShapeDtypeStruct((1, 128), F32), jax.ShapeDtypeStruct((T, D), F32),
                            jax.ShapeDtypeStruct((1, D), F32)],
                 sem=("arbitrary",))(x, g, tgt)


def _resident(shape):
    return pl.BlockSpec(shape, lambda *_: (0,) * len(shape), pipeline_mode=pl.Buffered(1))


def _ffn_fwd_k(x, gn, wg, wu, wd):
    T, D = x.shape
    NS, _, Fs = wg.shape
    tm = _tile(T, 1024)

    def body(x_ref, gn_ref, wg_ref, wu_ref, wd_ref, o_ref, h_ref, s1_ref, s2_ref, a_ref, hs, acc):
        j = pl.program_id(1)

        @pl.when(j == 0)
        def _():
            xf = x_ref[...]
            r = lax.rsqrt(jnp.mean(xf * xf, axis=-1, keepdims=True) + RMS_EPS)
            hs[...] = (xf * r * gn_ref[...]).astype(BF)
            h_ref[...] = hs[...]
            acc[...] = jnp.zeros_like(acc)

        h = hs[...]
        g = _dot(h, wg_ref[...])
        u = _dot(h, wu_ref[...])
        sg = _sigmoid(g)
        s1 = g * sg
        a = (s1 * u).astype(BF)
        s1_ref[...] = s1.astype(BF)
        s2_ref[...] = (u * (sg * (1.0 + g * (1.0 - sg)))).astype(BF)
        a_ref[...] = a
        acc[...] += _dot(a, wd_ref[...])

        @pl.when(j == NS - 1)
        def _():
            o_ref[...] = x_ref[...] + 0.5 * acc[...]

    row = pl.BlockSpec((tm, D), lambda i, j: (i, 0))
    act = pl.BlockSpec((None, tm, Fs), lambda i, j: (j, i, 0))
    sh = jax.ShapeDtypeStruct((NS, T, Fs), BF)
    wspec = lambda w: pl.BlockSpec((None,) + w.shape[1:], lambda i, j: (j, 0, 0))
    return _call(body, name="ffn_fwd", grid=(T // tm, NS),
                 in_specs=[row, pl.BlockSpec((1, D), lambda i, j: (0, 0)), wspec(wg), wspec(wu), wspec(wd)],
                 out_specs=[row, row, act, act, act],
                 out_shape=[jax.ShapeDtypeStruct((T, D), F32), jax.ShapeDtypeStruct((T, D), BF), sh, sh, sh],
                 scratch=[pltpu.VMEM((tm, D), BF), pltpu.VMEM((tm, D), F32)],
                 sem=("parallel", "arbitrary"), vmem=FFN_VMEM)(x, gn, wg, wu, wd)


def _ffn_bwd_act(dxo, s1, s2, wd):
    NS, T, Fs = s1.shape
    D = dxo.shape[1]
    tm = _tile(T, 1024)

    def body(dxo_ref, s1_ref, s2_ref, wd_ref, dg_ref, du_ref, dy_ref, dys):
        j = pl.program_id(1)

        @pl.when(j == 0)
        def _():
            dys[...] = (0.5 * dxo_ref[...]).astype(BF)
            dy_ref[...] = dys[...]

        da = _dot_nt(dys[...], wd_ref[j])
        dg_ref[...] = (da * s2_ref[...].astype(F32)).astype(BF)
        du_ref[...] = (da * s1_ref[...].astype(F32)).astype(BF)

    row = pl.BlockSpec((tm, D), lambda i, j: (i, 0))
    act = pl.BlockSpec((None, tm, Fs), lambda i, j: (j, i, 0))
    sh = jax.ShapeDtypeStruct((NS, T, Fs), BF)
    return _call(body, name="ffn_bwd_act", grid=(T // tm, NS), in_specs=[row, act, act, _resident(wd.shape)],
                 out_specs=[act, act, row], out_shape=[sh, sh, jax.ShapeDtypeStruct((T, D), BF)],
                 scratch=[pltpu.VMEM((tm, D), BF)], sem=("parallel", "arbitrary"))(dxo, s1, s2, wd)


def _ffn_bwd_dx(dg, du, wg, wu, x, gn, dxo):
    NS, T, Fs = dg.shape
    D = x.shape[1]
    tm = _tile(T, 1024)

    def body(dg_ref, du_ref, wg_ref, wu_ref, x_ref, gn_ref, dxo_ref, dx_ref, dgn_ref, acc):
        i, j = pl.program_id(0), pl.program_id(1)

        @pl.when((i == 0) & (j == 0))
        def _():
            dgn_ref[...] = jnp.zeros_like(dgn_ref)

        @pl.when(j == 0)
        def _():
            acc[...] = jnp.zeros_like(acc)

        acc[...] += _dot_nt(dg_ref[...], wg_ref[j]) + _dot_nt(du_ref[...], wu_ref[j])

        @pl.when(j == NS - 1)
        def _():
            xf = x_ref[...]
            r = lax.rsqrt(jnp.mean(xf * xf, axis=-1, keepdims=True) + RMS_EPS)
            dh = acc[...]
            uu = dh * gn_ref[...]
            mu = jnp.mean(uu * xf, axis=-1, keepdims=True)
            dx_ref[...] = dxo_ref[...] + r * (uu - xf * (r * r * mu))
            dgn_ref[...] += jnp.sum(dh * xf * r, axis=0, keepdims=True)

    row = pl.BlockSpec((tm, D), lambda i, j: (i, 0))
    vec = pl.BlockSpec((1, D), lambda i, j: (0, 0))
    act = pl.BlockSpec((None, tm, Fs), lambda i, j: (j, i, 0))
    return _call(body, name="ffn_bwd_dx", grid=(T // tm, NS),
                 in_specs=[act, act, _resident(wg.shape), _resident(wu.shape), row, vec, row], out_specs=[row, vec],
                 out_shape=[jax.ShapeDtypeStruct((T, D), F32), jax.ShapeDtypeStruct((1, D), F32)],
                 scratch=[pltpu.VMEM((tm, D), F32)], sem=("arbitrary", "arbitrary"), vmem=FFN_VMEM)(
        dg, du, wg, wu, x, gn, dxo)


def _ffn_bwd_k2(hb, dyb, a, dg, du):
    NS, T, Fs = a.shape
    D = hb.shape[1]
    tk = _tile(T, 1024)

    def body(h_ref, dy_ref, a_ref, dg_ref, du_ref, og_ref, ou_ref, od_ref):
        @pl.when(pl.program_id(1) == 0)
        def _():
            og_ref[...] = jnp.zeros_like(og_ref)
            ou_ref[...] = jnp.zeros_like(ou_ref)
            od_ref[...] = jnp.zeros_like(od_ref)

        h = h_ref[...]
        og_ref[...] += _dot_tn(dg_ref[...], h)
        ou_ref[...] += _dot_tn(du_ref[...], h)
        od_ref[...] += _dot_tn(a_ref[...], dy_ref[...])

    row = pl.BlockSpec((tk, D), lambda j, k: (k, 0))
    act = pl.BlockSpec((None, tk, Fs), lambda j, k: (j, k, 0))
    return _call(body, name="ffn_bwd_w", grid=(NS, T // tk), in_specs=[row, row, act, act, act],
                 out_specs=[pl.BlockSpec((None, Fs, D), lambda j, k: (j, 0, 0))] * 3,
                 out_shape=[jax.ShapeDtypeStruct((NS, Fs, D), F32)] * 3,
                 sem=("parallel", "arbitrary"))(hb, dyb, a, dg, du)


def _mm_nn(a, b, res=None, out_dtype=F32):
    T, K = a.shape
    N = b.shape[1]
    tm = _tile(T, 512)
    tn = N if N <= 2048 else _tile(N, 1024)

    def body(*refs):
        if res is None:
            a_ref, b_ref, o_ref = refs
            o_ref[...] = _dot(a_ref[...], b_ref[...]).astype(out_dtype)
        else:
            a_ref, b_ref, r_ref, o_ref = refs
            o_ref[...] = (r_ref[...] + _dot(a_ref[...], b_ref[...])).astype(out_dtype)

    o = pl.BlockSpec((tm, tn), lambda i, j: (i, j))
    ins = [pl.BlockSpec((tm, K), lambda i, j: (i, 0)), pl.BlockSpec((K, tn), lambda i, j: (0, j))]
    args = [a, b]
    if res is not None:
        ins.append(o)
        args.append(res)
    return _call(body, name="mm_nn", grid=(T // tm, N // tn), in_specs=ins, out_specs=o,
                 out_shape=jax.ShapeDtypeStruct((T, N), out_dtype), sem=("parallel", "parallel"))(*args)


def _mix_bwd_dy(dxo, w_out):
    T, D = dxo.shape
    tm = _tile(T, 512)
    cuts = (0, ATT_W, ATT_W + SSD_W, ATT_W + SSD_W + SGU_W)

    def body(dx_ref, w_ref, a_ref, s_ref, g_ref):
        d = _dot_nt(dx_ref[...].astype(BF), w_ref[...])
        for o_ref, lo, hi in zip((a_ref, s_ref, g_ref), cuts[:-1], cuts[1:]):
            o_ref[...] = d[:, lo:hi]

    row = lambda w: pl.BlockSpec((tm, w), lambda i: (i, 0))
    return _call(body, name="mix_bwd_dy", grid=(T // tm,), in_specs=[row(D), _resident(w_out.shape)],
                 out_specs=[row(ATT_W), row(SSD_W), row(SGU_W)],
                 out_shape=[jax.ShapeDtypeStruct((T, w), F32) for w in (ATT_W, SSD_W, SGU_W)],
                 sem=("parallel",))(dxo, w_out)


def _mm_tn(a, b):
    T, M = a.shape
    N = b.shape[1]
    tk = _tile(T, 1024)
    tmm = _tile(M, 512)

    def body(a_ref, b_ref, o_ref):
        @pl.when(pl.program_id(1) == 0)
        def _():
            o_ref[...] = jnp.zeros_like(o_ref)

        o_ref[...] += _dot_tn(a_ref[...].astype(BF), b_ref[...].astype(BF))

    return _call(body, name="mm_tn", grid=(M // tmm, T // tk),
                 in_specs=[pl.BlockSpec((tk, tmm), lambda i, k: (k, i)), pl.BlockSpec((tk, N), lambda i, k: (k, 0))],
                 out_specs=pl.BlockSpec((tmm, N), lambda i, k: (i, 0)),
                 out_shape=jax.ShapeDtypeStruct((M, N), F32), sem=("parallel", "arbitrary"))(a, b)


def _mix_proj(x, gn, win):
    T, D = x.shape
    tm = _tile(T, 512)
    cuts = (0, W_QKV, W_QKV + W_SSD, W_QKV + W_SSD + W_UV)

    def body(x_ref, gn_ref, w_ref, h_ref, q_ref, s_ref, u_ref):
        xf = x_ref[...]
        r = lax.rsqrt(jnp.mean(xf * xf, axis=-1, keepdims=True) + RMS_EPS)
        h = (xf * r * gn_ref[...]).astype(BF)
        h_ref[...] = h
        for o_ref, lo, hi in zip((q_ref, s_ref, u_ref), cuts[:-1], cuts[1:]):
            o_ref[...] = _dot(h, w_ref[:, lo:hi])

    row = lambda w: pl.BlockSpec((tm, w), lambda i: (i, 0))
    return _call(body, name="mix_proj", grid=(T // tm,),
                 in_specs=[row(D), pl.BlockSpec((1, D), lambda i: (0, 0)), _resident(win.shape)],
                 out_specs=[row(D), row(W_QKV), row(W_SSD), row(W_UV)],
                 out_shape=[jax.ShapeDtypeStruct((T, D), BF), jax.ShapeDtypeStruct((T, W_QKV), F32),
                            jax.ShapeDtypeStruct((T, W_SSD), F32), jax.ShapeDtypeStruct((T, W_UV), F32)],
                 sem=("parallel",))(x, gn, win)


def _mix_bwd_dx(dqkv, dsin, duv, win, x, gn, dxo):
    T, D = x.shape
    tm = _tile(T, 512)
    cuts = (0, W_QKV, W_QKV + W_SSD, W_QKV + W_SSD + W_UV)

    def body(dq_ref, ds_ref, du_ref, w_ref, x_ref, gn_ref, dxo_ref, dx_ref, dgn_ref):
        @pl.when(pl.program_id(0) == 0)
        def _():
            dgn_ref[...] = jnp.zeros_like(dgn_ref)

        dh = (_dot_nt(dq_ref[...], w_ref[:, cuts[0]:cuts[1]]) + _dot_nt(ds_ref[...], w_ref[:, cuts[1]:cuts[2]])
              + _dot_nt(du_ref[...], w_ref[:, cuts[2]:cuts[3]]))
        xf = x_ref[...]
        r = lax.rsqrt(jnp.mean(xf * xf, axis=-1, keepdims=True) + RMS_EPS)
        uu = dh * gn_ref[...]
        mu = jnp.mean(uu * xf, axis=-1, keepdims=True)
        dx_ref[...] = dxo_ref[...] + r * (uu - xf * (r * r * mu))
        dgn_ref[...] += jnp.sum(dh * xf * r, axis=0, keepdims=True)

    row = lambda w: pl.BlockSpec((tm, w), lambda i: (i, 0))
    vec = pl.BlockSpec((1, D), lambda i: (0, 0))
    return _call(body, name="mix_bwd_dx", grid=(T // tm,),
                 in_specs=[row(W_QKV), row(W_SSD), row(W_UV), _resident(win.shape), row(D), vec, row(D)],
                 out_specs=[row(D), vec],
                 out_shape=[jax.ShapeDtypeStruct((T, D), F32), jax.ShapeDtypeStruct((1, D), F32)],
                 sem=("arbitrary",))(dqkv, dsin, duv, win, x, gn, dxo)


def _lane_mask(e, width=128):
    return (lax.broadcasted_iota(jnp.int32, (1, width), 1) // HEAD) == e


def _band_mask(n):
    qi = lax.broadcasted_iota(jnp.int32, (CHUNK, 2 * CHUNK), 0)
    kj = lax.broadcasted_iota(jnp.int32, (CHUNK, 2 * CHUNK), 1)
    dist = qi + CHUNK - kj
    return (dist >= 0) & (dist <= CHUNK) & ((kj >= CHUNK) | (n > 0))


def _sub_rows(r, block, dil):
    if dil == 1:
        return pl.ds(pl.multiple_of(block * CHUNK, CHUNK), CHUNK)
    return pl.ds(r + dil * CHUNK * block, CHUNK, stride=dil)


def _attn_specs(T, dil):
    B, nb = T // SEQ, SEQ // (CHUNK * dil)
    once = dict(pipeline_mode=pl.Buffered(1))
    q_like = lambda col: pl.BlockSpec((CHUNK * dil, 128), lambda b, n, r: (b * nb + n, col), **(once if nb == 1 else {}))
    k_like = lambda col: pl.BlockSpec((SEQ, 128), lambda b, n, r: (b, col), **once)
    return B, nb, q_like, k_like


def _attn_fwd(qkv, dil):
    T = qkv.shape[0]
    B, nb, q_like, k_like = _attn_specs(T, dil)
    scale = HEAD ** -0.5

    per_step = min(dil, ATTN_SUBSEQ_PER_STEP)

    def body(*refs):
        q_t, k_t, v_t, o_t, l_t = refs[0:3], refs[3:6], refs[6:9], refs[9:12], refs[12:15]
        n = pl.program_id(1)
        mask = _band_mask(n)
        for u in range(per_step):
            r = pl.program_id(2) * per_step + u
            mine = _sub_rows(r, 0, dil)
            cur, prv = _sub_rows(r, n, dil), _sub_rows(r, jnp.maximum(n - 1, 0), dil)
            for t in range(3):
                qt = q_t[t][mine, :].astype(BF)
                kt = jnp.concatenate([k_t[t][prv, :], k_t[t][cur, :]], axis=0).astype(BF)
                vt = jnp.concatenate([v_t[t][prv, :], v_t[t][cur, :]], axis=0).astype(BF)
                o_pair = jnp.zeros((CHUNK, 128), F32)
                l_pair = jnp.zeros((CHUNK, 128), F32)
                for e in range(2):
                    lm = _lane_mask(e)
                    s = _dot_nt(jnp.where(lm, qt, jnp.zeros_like(qt)), kt) * scale
                    s = jnp.where(mask, s, NEG)
                    m = jnp.max(s, axis=-1, keepdims=True)
                    p = jnp.exp(s - m)
                    den = jnp.sum(p, axis=-1, keepdims=True)
                    o = _dot(p.astype(BF), vt) / den
                    o_pair = jnp.where(lm, o, o_pair)
                    l_pair = jnp.where(lm, m + jnp.log(den), l_pair)
                o_t[t][mine, :] = o_pair
                l_t[t][mine, :] = l_pair


    out_spec = pl.BlockSpec((CHUNK * dil, 128), lambda b, n, r: (b * nb + n, 0))
    sh = jax.ShapeDtypeStruct((T, 128), F32)
    outs = _call(
        body, name=f"attn_fwd_d{dil}", grid=(B, nb, dil // per_step),
        in_specs=[q_like(t) for t in range(3)] + [k_like(3 + t) for t in range(3)] + [k_like(6 + t) for t in range(3)],
        out_specs=[out_spec] * 6, out_shape=[sh] * 6, sem=("parallel", "arbitrary", "arbitrary"))(*([qkv] * 9))
    return list(outs[0:3]), list(outs[3:6])


def _attn_combine(branches):
    T = branches[0][0][0].shape[0]
    tm = _tile(T, 512)

    def body(*refs):
        y_ref, l_ref = refs[-2:]
        for t in range(3):
            o = [refs[6 * i + t][...] for i in range(3)]
            a, b, c = [refs[6 * i + 3 + t][...] for i in range(3)]
            m = jnp.maximum(jnp.maximum(a, b), c)
            ea, eb, ec = jnp.exp(a - m), jnp.exp(b - m), jnp.exp(c - m)
            z = ea + eb + ec
            y_ref[:, 128 * t:128 * (t + 1)] = (ea * o[0] + eb * o[1] + ec * o[2]) / z
            l_ref[:, 128 * t:128 * (t + 1)] = m + jnp.log(z)

    tile = pl.BlockSpec((tm, 128), lambda i: (i, 0))
    row = pl.BlockSpec((tm, ATT_W), lambda i: (i, 0))
    sh = jax.ShapeDtypeStruct((T, ATT_W), F32)
    flat = [a for o_t, l_t in branches for a in (*o_t, *l_t)]
    return _call(body, name="attn_combine", grid=(T // tm,), in_specs=[tile] * 18, out_specs=[row, row],
                 out_shape=[sh, sh], sem=("parallel",))(*flat)


def _attn_bwd(qkv, do, out, lse, dil):
    T = qkv.shape[0]
    B, nb, q_like, k_like = _attn_specs(T, dil)
    scale = HEAD ** -0.5
    per_step = min(dil, ATTN_SUBSEQ_PER_STEP)

    def body(*refs):
        q_t, k_t, v_t = refs[0:3], refs[3:6], refs[6:9]
        do_t, out_t, lse_t = refs[9:12], refs[12:15], refs[15:18]
        dq_t, dk_t, dv_t = refs[18:21], refs[21:24], refs[24:27]
        n = pl.program_id(1)

        @pl.when((n == 0) & (pl.program_id(2) == 0))
        def _():
            for t in range(3):
                dk_t[t][...] = jnp.zeros_like(dk_t[t])
                dv_t[t][...] = jnp.zeros_like(dv_t[t])

        mask = _band_mask(n)
        for u in range(per_step):
            r = pl.program_id(2) * per_step + u
            mine = _sub_rows(r, 0, dil)
            cur, prv = _sub_rows(r, n, dil), _sub_rows(r, jnp.maximum(n - 1, 0), dil)
            for t in range(3):
                qt = q_t[t][mine, :].astype(BF)
                kt = jnp.concatenate([k_t[t][prv, :], k_t[t][cur, :]], axis=0).astype(BF)
                vt = jnp.concatenate([v_t[t][prv, :], v_t[t][cur, :]], axis=0).astype(BF)
                do_ = do_t[t][mine, :]
                dlt = do_ * out_t[t][mine, :]
                ls = lse_t[t][mine, :]
                dq_pair = jnp.zeros((CHUNK, 128), F32)
                dk_acc = jnp.zeros((2 * CHUNK, 128), F32)
                dv_acc = jnp.zeros((2 * CHUNK, 128), F32)
                for e in range(2):
                    lm = _lane_mask(e)
                    qm = jnp.where(lm, qt, jnp.zeros_like(qt))
                    s = _dot_nt(qm, kt) * scale
                    p = jnp.exp(jnp.where(mask, s - ls[:, HEAD * e:HEAD * e + 1], NEG))
                    dom = jnp.where(lm, do_, 0.0).astype(BF)
                    dv_acc += _dot_tn(p.astype(BF), dom)
                    dp = _dot_nt(dom, vt)
                    delta = jnp.sum(jnp.where(lm, dlt, 0.0), axis=-1, keepdims=True)
                    ds = (p * (dp - delta) * scale).astype(BF)
                    dq_pair += jnp.where(lm, _dot(ds, kt), 0.0)
                    dk_acc += _dot_tn(ds, qm)
                dq_t[t][mine, :] = dq_pair
                dk_t[t][cur, :] = dk_t[t][cur, :] + dk_acc[CHUNK:]
                dk_t[t][prv, :] = dk_t[t][prv, :] + dk_acc[:CHUNK]
                dv_t[t][cur, :] = dv_t[t][cur, :] + dv_acc[CHUNK:]
                dv_t[t][prv, :] = dv_t[t][prv, :] + dv_acc[:CHUNK]

    q_out = pl.BlockSpec((CHUNK * dil, 128), lambda b, n, r: (b * nb + n, 0))
    k_out = pl.BlockSpec((SEQ, 128), lambda b, n, r: (b, 0))
    sh = jax.ShapeDtypeStruct((T, 128), F32)
    tiles = lambda: [q_like(t) for t in range(3)]
    return list(_call(
        body, name=f"attn_bwd_d{dil}", grid=(B, nb, dil // per_step),
        in_specs=tiles() + [k_like(3 + t) for t in range(3)] + [k_like(6 + t) for t in range(3)]
        + tiles() + tiles() + tiles(),
        out_specs=[q_out] * 3 + [k_out] * 6, out_shape=[sh] * 9,
        sem=("parallel", "arbitrary", "arbitrary"), vmem=ATTN_BWD_VMEM)(*([qkv] * 9 + [do] * 3 + [out] * 3 + [lse] * 3)))


def _sum_branches(parts):
    T = parts[0][0].shape[0]
    tm = _tile(T, 512)

    def body(*refs):
        o_ref = refs[-1]
        for c in range(9):
            acc = refs[c][...] + refs[9 + c][...] + refs[18 + c][...]
            o_ref[:, 128 * c:128 * (c + 1)] = acc.astype(BF)

    tile = pl.BlockSpec((tm, 128), lambda i: (i, 0))
    flat = [a for br in parts for a in br]
    return _call(body, name="attn_sum_branches", grid=(T // tm,), in_specs=[tile] * 27,
                 out_specs=pl.BlockSpec((tm, W_QKV), lambda i: (i, 0)),
                 out_shape=jax.ShapeDtypeStruct((T, W_QKV), BF), sem=("parallel",))(*flat)


def _silu(x):
    return x * _sigmoid(x)


def _dsilu(x):
    s = _sigmoid(x)
    return s * (1.0 + x * (1.0 - s))


def _log1p(u):
    return jnp.where(u < 0.01, u * (1.0 - u * (0.5 - u * (1.0 / 3.0))), jnp.log(1.0 + u))


def _softplus(x):
    return jnp.maximum(x, 0.0) + _log1p(jnp.exp(-jnp.abs(x)))


def _cumsum_rows(x, reverse=False):
    n = x.shape[0]
    rows = lax.broadcasted_iota(jnp.int32, x.shape, 0)
    k = 1
    while k < n:
        if reverse:
            x = x + jnp.where(rows < n - k, pltpu.roll(x, n - k, 0), 0.0)
        else:
            x = x + jnp.where(rows >= k, pltpu.roll(x, k, 0), 0.0)
        k *= 2
    return x


def _tri():
    r = lax.broadcasted_iota(jnp.int32, (CHUNK, CHUNK), 0)
    c = lax.broadcasted_iota(jnp.int32, (CHUNK, CHUNK), 1)
    return r >= c


def _row_mask(e):
    return (lax.broadcasted_iota(jnp.int32, (128, 1), 0) // HEAD) == e


def _first_lane(e):
    return lax.broadcasted_iota(jnp.int32, (1, 128), 1) == HEAD * e


def _ssd_pre(x_ref, halo_ref, first, cw_ref, cb_ref, dtb_ref, al_ref, ext):
    row = x_ref[...]
    z = row[:, SSD_CONV_DIM:SSD_CONV_DIM + SSD_W]
    u = row[:, SSD_CONV_DIM + SSD_W:] + dtb_ref[...]
    ext[0:8, :] = jnp.where(first, 0.0, halo_ref[:, 0:SSD_CONV_DIM])
    ext[8:8 + CHUNK, :] = row[:, 0:SSD_CONV_DIM]
    xc = cb_ref[...]
    for j in range(4):
        xc = xc + cw_ref[j:j + 1, :] * ext[pl.ds(5 + j, CHUNK), :]
    xa = _silu(xc)
    dt = _softplus(u)
    a = dt * (-jnp.exp(al_ref[...]))
    A = _cumsum_rows(a)
    return dict(z=z, u=u, xc=xc, xs=xa[:, 0:SSD_W], Bm=xa[:, SSD_W:SSD_W + 256], Cm=xa[:, SSD_W + 256:],
                dt=dt, a=a, A=A, AT=A.T, eA=jnp.exp(A), wdec=jnp.exp(A[CHUNK - 1:CHUNK, :] - A),
                dtot=jnp.exp(A[CHUNK - 1:CHUNK, :]))


def _ssd_y(p, hp_ref, dskip):
    tri = _tri()
    X = p["xs"] * p["dt"]
    Bb = [p["Bm"][:, 128 * g:128 * (g + 1)].astype(BF) for g in range(2)]
    Cb = [p["Cm"][:, 128 * g:128 * (g + 1)].astype(BF) for g in range(2)]
    CB = [_dot_nt(Cb[g], Bb[g]) for g in range(2)]
    tiles = []
    for t in range(3):
        sl = slice(128 * t, 128 * (t + 1))
        hpb = hp_ref[sl, :].astype(BF)
        acc = jnp.zeros((CHUNK, 128), F32)
        for e in range(2):
            h = 2 * t + e
            g, col = h // 3, HEAD * h
            lm = _lane_mask(e)
            L = jnp.exp(jnp.where(tri, p["A"][:, col:col + 1] - p["AT"][col:col + 1, :], NEG))
            yd = _dot((CB[g] * L).astype(BF), jnp.where(lm, X[:, sl], 0.0).astype(BF))
            yo = _dot_nt(Cb[g], hpb) * p["eA"][:, sl]
            acc = acc + jnp.where(lm, yd + yo, 0.0)
        tiles.append(acc)
    return jnp.concatenate(tiles, axis=1) + dskip * p["xs"], X, Bb, Cb, CB


def _group_stats(v):
    g0 = lax.broadcasted_iota(jnp.int32, (1, SSD_W), 1) < SSD_W // 2
    m0 = jnp.sum(jnp.where(g0, v, 0.0), axis=-1, keepdims=True) * (2.0 / SSD_W)
    m1 = jnp.sum(jnp.where(g0, 0.0, v), axis=-1, keepdims=True) * (2.0 / SSD_W)
    return jnp.where(g0, m0, m1)


def _ssd_specs(T, rev):
    B = T // SEQ

    def chunk(b, c):
        return b * N_CHUNK + (N_CHUNK - 1 - c if rev else c)

    row = pl.BlockSpec((CHUNK, W_SSD), lambda b, c: (chunk(b, c), 0))
    halo = pl.BlockSpec((8, W_SSD), lambda b, c: (jnp.maximum(chunk(b, c) * (CHUNK // 8) - 1, 0), 0))
    hp = pl.BlockSpec((None, SSD_W, SSD_STATE), lambda b, c: (chunk(b, c), 0, 0))
    y = pl.BlockSpec((CHUNK, SSD_W), lambda b, c: (chunk(b, c), 0))
    const = lambda r, w: pl.BlockSpec((r, w), lambda b, c: (0, 0))
    params = [const(4, SSD_CONV_DIM), const(1, SSD_CONV_DIM)] + [const(1, SSD_W)] * 4
    return B, row, halo, hp, y, const, params


def _ssd_fwd(sin, conv_w, conv_b, dtb, alog, dskip, norm_g):
    T = sin.shape[0]
    B, row, halo, hp, y, const, params = _ssd_specs(T, False)

    def body(x_ref, halo_ref, cw_ref, cb_ref, dtb_ref, al_ref, dk_ref, ng_ref, y_ref, hp_ref, ext, hst):
        c = pl.program_id(1)

        @pl.when(c == 0)
        def _():
            hst[...] = jnp.zeros_like(hst)

        p = _ssd_pre(x_ref, halo_ref, c == 0, cw_ref, cb_ref, dtb_ref, al_ref, ext)
        yv, X, Bb, Cb, CB = _ssd_y(p, hst, dk_ref[...])
        hp_ref[...] = hst[...]
        for t in range(3):
            sl = slice(128 * t, 128 * (t + 1))
            old = hst[sl, :]
            new = old
            for e in range(2):
                h = 2 * t + e
                g, col = h // 3, HEAD * h
                st = _dot_tn(jnp.where(_lane_mask(e), X[:, sl] * p["wdec"][:, sl], 0.0).astype(BF), Bb[g])
                new = jnp.where(_row_mask(e), old * p["dtot"][:, col:col + 1] + st, new)
            hst[sl, :] = new
        y2 = yv * _silu(p["z"])
        r = lax.rsqrt(_group_stats(y2 * y2) + RMS_EPS)
        y_ref[...] = y2 * r * ng_ref[...]

    return _call(body, name="ssd_fwd", grid=(B, N_CHUNK), in_specs=[row, halo] + params, out_specs=[y, hp],
                 out_shape=[jax.ShapeDtypeStruct((T, SSD_W), F32),
                            jax.ShapeDtypeStruct((T // CHUNK, SSD_W, SSD_STATE), F32)],
                 scratch=[pltpu.VMEM((8 + CHUNK, SSD_CONV_DIM), F32), pltpu.VMEM((SSD_W, SSD_STATE), F32)],
                 sem=("parallel", "arbitrary"))(sin, sin, conv_w, conv_b, dtb, alog, dskip, norm_g)


def _ssd_bwd(sin, hprev, dy3, conv_w, conv_b, dtb, alog, dskip, norm_g):
    T = sin.shape[0]
    B, row, halo, hp, y, const, params = _ssd_specs(T, True)

    def body(x_ref, halo_ref, hp_ref, dy_ref, cw_ref, cb_ref, dtb_ref, al_ref, dk_ref, ng_ref,
             dx_ref, dcw_ref, dcb_ref, dvec_ref, ext, ext2, dh):
        c = pl.program_id(1)

        @pl.when((pl.program_id(0) == 0) & (c == 0))
        def _():
            dcw_ref[...] = jnp.zeros_like(dcw_ref)
            dcb_ref[...] = jnp.zeros_like(dcb_ref)
            dvec_ref[...] = jnp.zeros_like(dvec_ref)

        @pl.when(c == 0)
        def _():
            dh[...] = jnp.zeros_like(dh)
            ext2[CHUNK:CHUNK + 8, :] = jnp.zeros((8, SSD_CONV_DIM), F32)

        p = _ssd_pre(x_ref, halo_ref, c == N_CHUNK - 1, cw_ref, cb_ref, dtb_ref, al_ref, ext)
        dskip_ = dk_ref[...]
        yv, X, Bb, Cb, CB = _ssd_y(p, hp_ref, dskip_)
        xs, z, A, AT = p["xs"], p["z"], p["A"], p["AT"]

        sz = _silu(z)
        y2 = yv * sz
        r = lax.rsqrt(_group_stats(y2 * y2) + RMS_EPS)
        dy3_ = dy_ref[...]
        uu = dy3_ * ng_ref[...]
        dy2 = r * (uu - y2 * (r * r * _group_stats(uu * y2)))
        dy = dy2 * sz
        dz = dy2 * yv * _dsilu(z)

        tri = _tri()
        rows = lax.broadcasted_iota(jnp.int32, (CHUNK, 1), 0)
        dG = [jnp.zeros((CHUNK, CHUNK), F32) for _ in range(2)]
        dB = [jnp.zeros((CHUNK, SSD_STATE), F32) for _ in range(2)]
        dC = [jnp.zeros((CHUNK, SSD_STATE), F32) for _ in range(2)]
        dX_t, dA_t, ddtx_t = [], [], []
        for t in range(3):
            sl = slice(128 * t, 128 * (t + 1))
            hp_t = hp_ref[sl, :]
            hpb = hp_t.astype(BF)
            dhc = dh[sl, :]
            dh_new = jnp.zeros((128, SSD_STATE), F32)
            dX = jnp.zeros((CHUNK, 128), F32)
            dA = jnp.zeros((CHUNK, 128), F32)
            ddtx = jnp.zeros((CHUNK, 128), F32)
            for e in range(2):
                h = 2 * t + e
                g, col = h // 3, HEAD * h
                lm, rm, fl = _lane_mask(e), _row_mask(e), _first_lane(e)
                L = jnp.exp(jnp.where(tri, A[:, col:col + 1] - AT[col:col + 1, :], NEG))
                Mf = CB[g] * L
                Xm = jnp.where(lm, X[:, sl], 0.0)
                Xmb = Xm.astype(BF)
                dyh = jnp.where(lm, dy[:, sl], 0.0)
                dyb = dyh.astype(BF)
                dXh = _dot_tn(Mf.astype(BF), dyb)
                dM = jnp.where(tri, _dot_nt(dyb, Xmb), 0.0)
                Wm = dM * Mf
                dAc = jnp.sum(Wm, axis=-1, keepdims=True) - jnp.sum(Wm.T, axis=-1, keepdims=True)
                dG[g] = dG[g] + dM * L
                eAt = p["eA"][:, sl]
                yo = _dot_nt(Cb[g], hpb)
                dyo = (dyh * eAt).astype(BF)
                dC[g] = dC[g] + _dot(dyo, hpb)
                dh_new = dh_new + _dot_tn(dyo, Cb[g])
                dAc = dAc + jnp.sum(dyh * yo * eAt, axis=-1, keepdims=True)
                dHn = jnp.where(rm, dhc, 0.0)
                dHnb = dHn.astype(BF)
                dec = p["dtot"][:, col:col + 1]
                dh_new = dh_new + dec * dHn
                Z = _dot_nt(Bb[g], dHnb)
                wt = p["wdec"][:, sl]
                xi = jnp.sum(Xm * Z, axis=-1, keepdims=True) * p["wdec"][:, col:col + 1]
                dXh = dXh + wt * Z
                dB[g] = dB[g] + _dot(jnp.where(lm, X[:, sl] * wt, 0.0).astype(BF), dHnb)
                dAtot = jnp.sum(xi, axis=0, keepdims=True) + dec * jnp.sum(
                    jnp.sum(dHn * hp_t, axis=-1, keepdims=True), axis=0, keepdims=True)
                dAc = dAc - xi + jnp.where(rows == CHUNK - 1, dAtot, 0.0)
                dA = dA + jnp.where(fl, dAc, 0.0)
                dX = dX + dXh
                ddtx = ddtx + jnp.where(fl, jnp.sum(dXh * xs[:, sl], axis=-1, keepdims=True), 0.0)
            dh[sl, :] = dh_new
            dX_t.append(dX)
            dA_t.append(dA)
            ddtx_t.append(ddtx)
        for g in range(2):
            dGb = dG[g].astype(BF)
            dC[g] = dC[g] + _dot(dGb, Bb[g])
            dB[g] = dB[g] + _dot_tn(dGb, Cb[g])
        dXf = jnp.concatenate(dX_t, axis=1)
        da = _cumsum_rows(jnp.concatenate(dA_t, axis=1), reverse=True)
        ddt = da * (-jnp.exp(al_ref[...])) + jnp.concatenate(ddtx_t, axis=1)
        du = ddt * _sigmoid(p["u"])
        dxs = dXf * p["dt"] + dskip_ * dy
        dxc = jnp.concatenate([dxs, dB[0], dB[1], dC[0], dC[1]], axis=1) * _dsilu(p["xc"])
        ext2[0:CHUNK, :] = dxc
        dxbc = jnp.zeros((CHUNK, SSD_CONV_DIM), F32)
        for j in range(4):
            dxbc = dxbc + cw_ref[j:j + 1, :] * ext2[pl.ds(3 - j, CHUNK), :]
            dcw_ref[j:j + 1, :] += jnp.sum(dxc * ext[pl.ds(5 + j, CHUNK), :], axis=0, keepdims=True)
        ext2[CHUNK:CHUNK + 8, :] = dxc[0:8, :]
        dcb_ref[...] += jnp.sum(dxc, axis=0, keepdims=True)
        dvec_ref[0:1, :] += jnp.sum(du, axis=0, keepdims=True)
        dvec_ref[1:2, :] += jnp.sum(da * p["a"], axis=0, keepdims=True)
        dvec_ref[2:3, :] += jnp.sum(dy * xs, axis=0, keepdims=True)
        dvec_ref[3:4, :] += jnp.sum(dy3_ * y2 * r, axis=0, keepdims=True)
        dx_ref[...] = jnp.concatenate([dxbc, dz, du], axis=1).astype(BF)

    return _call(body, name="ssd_bwd", grid=(B, N_CHUNK), in_specs=[row, halo, hp, y] + params,
                 out_specs=[row, const(4, SSD_CONV_DIM), const(1, SSD_CONV_DIM), const(8, SSD_W)],
                 out_shape=[jax.ShapeDtypeStruct((T, W_SSD), BF), jax.ShapeDtypeStruct((4, SSD_CONV_DIM), F32),
                            jax.ShapeDtypeStruct((1, SSD_CONV_DIM), F32), jax.ShapeDtypeStruct((8, SSD_W), F32)],
                 scratch=[pltpu.VMEM((8 + CHUNK, SSD_CONV_DIM), F32), pltpu.VMEM((8 + CHUNK, SSD_CONV_DIM), F32),
                          pltpu.VMEM((SSD_W, SSD_STATE), F32)],
                 sem=("arbitrary", "arbitrary"))(sin, sin, hprev, dy3, conv_w, conv_b, dtb, alog, dskip, norm_g)


def _sgu_core(uv_ref, g_ref, b_ref, w_ref, bias_ref):
    x = uv_ref[...]
    cdf = 0.5 * (1.0 + lax.erf(x * (2.0 ** -0.5)))
    ge = x * cdf
    dge = cdf + x * jnp.exp(-0.5 * x * x) * ((2.0 * math.pi) ** -0.5)
    u, v = ge[:, 0:SGU_W], ge[:, SGU_W:]
    vc = v - jnp.mean(v, axis=-1, keepdims=True)
    rstd = lax.rsqrt(jnp.mean(vc * vc, axis=-1, keepdims=True) + LN_EPS)
    vhat = vc * rstd
    vn = vhat * g_ref[...] + b_ref[...]
    tri = _tri()
    wc = [jnp.where(tri, w_ref[gi], 0.0).astype(BF) for gi in range(4)]
    vm = [jnp.where(_lane_mask(gi % 2), vn[:, 128 * (gi // 2):128 * (gi // 2 + 1)], 0.0).astype(BF) for gi in range(4)]
    mixed = jnp.concatenate([_dot(wc[2 * t], vm[2 * t]) + _dot(wc[2 * t + 1], vm[2 * t + 1]) for t in range(2)],
                            axis=1) + bias_ref[...]
    return dict(dge=dge, u=u, rstd=rstd, vhat=vhat, wc=wc, vm=vm, mixed=mixed)


def _sgu_specs():
    vec = pl.BlockSpec((1, SGU_W), lambda i: (0, 0))
    return [pl.BlockSpec((CHUNK, W_UV), lambda i: (i, 0)), vec, vec,
            pl.BlockSpec((4, CHUNK, CHUNK), lambda i: (0, 0, 0)), pl.BlockSpec((CHUNK, SGU_W), lambda i: (0, 0))]


def _sgu_fwd(uv, ln_g, ln_b, w, bias):
    T = uv.shape[0]

    def body(uv_ref, g_ref, b_ref, w_ref, bias_ref, y_ref):
        s = _sgu_core(uv_ref, g_ref, b_ref, w_ref, bias_ref)
        y_ref[...] = s["u"] * s["mixed"]

    return _call(body, name="sgu_fwd", grid=(T // CHUNK,), in_specs=_sgu_specs(),
                 out_specs=pl.BlockSpec((CHUNK, SGU_W), lambda i: (i, 0)),
                 out_shape=jax.ShapeDtypeStruct((T, SGU_W), F32), sem=("parallel",))(uv, ln_g, ln_b, w, bias)


def _sgu_bwd(uv, dy, ln_g, ln_b, w, bias):
    T = uv.shape[0]

    def body(uv_ref, dy_ref, g_ref, b_ref, w_ref, bias_ref, dx_ref, dw_ref, dbias_ref, dln_ref):
        @pl.when(pl.program_id(0) == 0)
        def _():
            dw_ref[...] = jnp.zeros_like(dw_ref)
            dbias_ref[...] = jnp.zeros_like(dbias_ref)
            dln_ref[...] = jnp.zeros_like(dln_ref)

        s = _sgu_core(uv_ref, g_ref, b_ref, w_ref, bias_ref)
        dy_ = dy_ref[...]
        du = dy_ * s["mixed"]
        dmix = dy_ * s["u"]
        dbias_ref[...] += dmix
        tri = _tri()
        dvn_t = []
        for t in range(2):
            acc = jnp.zeros((CHUNK, 128), F32)
            for e in range(2):
                gi = 2 * t + e
                dmg = jnp.where(_lane_mask(e), dmix[:, 128 * t:128 * (t + 1)], 0.0).astype(BF)
                acc = acc + _dot_tn(s["wc"][gi], dmg)
                dw_ref[gi] += jnp.where(tri, _dot_nt(dmg, s["vm"][gi]), 0.0)
            dvn_t.append(acc)
        dvn = jnp.concatenate(dvn_t, axis=1)
        dln_ref[0:1, :] += jnp.sum(dvn * s["vhat"], axis=0, keepdims=True)
        dln_ref[1:2, :] += jnp.sum(dvn, axis=0, keepdims=True)
        dvh = dvn * g_ref[...]
        dv = s["rstd"] * (dvh - jnp.mean(dvh, axis=-1, keepdims=True)
                          - s["vhat"] * jnp.mean(dvh * s["vhat"], axis=-1, keepdims=True))
        dx_ref[...] = (jnp.concatenate([du, dv], axis=1) * s["dge"]).astype(BF)

    ins = _sgu_specs()
    return _call(body, name="sgu_bwd", grid=(T // CHUNK,),
                 in_specs=[ins[0], pl.BlockSpec((CHUNK, SGU_W), lambda i: (i, 0))] + ins[1:],
                 out_specs=[pl.BlockSpec((CHUNK, W_UV), lambda i: (i, 0)),
                            pl.BlockSpec((4, CHUNK, CHUNK), lambda i: (0, 0, 0)),
                            pl.BlockSpec((CHUNK, SGU_W), lambda i: (0, 0)), pl.BlockSpec((8, SGU_W), lambda i: (0, 0))],
                 out_shape=[jax.ShapeDtypeStruct((T, W_UV), BF), jax.ShapeDtypeStruct((4, CHUNK, CHUNK), F32),
                            jax.ShapeDtypeStruct((CHUNK, SGU_W), F32), jax.ShapeDtypeStruct((8, SGU_W), F32)],
                 sem=("arbitrary",))(uv, dy, ln_g, ln_b, w, bias)


def _adamw(w, g, m, v):
    R, C = w.shape
    tr = R

    def body(w_ref, g_ref, m_ref, v_ref, d_ref, nm_ref, nv_ref):
        g_ = g_ref[...]
        m2 = ADAM_B1 * m_ref[...] + (1.0 - ADAM_B1) * g_
        v2 = ADAM_B2 * v_ref[...] + (1.0 - ADAM_B2) * (g_ * g_)
        m_hat = m2 / (1.0 - ADAM_B1 ** ADAM_STEP)
        v_hat = v2 / (1.0 - ADAM_B2 ** ADAM_STEP)
        d_ref[...] = -ADAM_LR * (m_hat / (jnp.sqrt(v_hat) + ADAM_EPS) + ADAM_WD * w_ref[...])
        nm_ref[...] = m2
        nv_ref[...] = v2

    blk = pl.BlockSpec((tr, C), lambda i: (i, 0))
    sh = jax.ShapeDtypeStruct((R, C), F32)
    return _call(body, name="adamw", grid=(R // tr,), in_specs=[blk] * 4, out_specs=[blk] * 3,
                 out_shape=[sh] * 3, sem=("parallel",))(w, g, m, v)


def _adamw_pair(w, g0, g1, m, v, dep):
    L, R, C = w.shape
    tr = max(t for t in range(8, R + 1, 8) if R % t == 0 and t * C * 4 <= 3 * 2 ** 19)

    def body(w_ref, g0_ref, g1_ref, m_ref, v_ref, dep_ref, d_ref, nm_ref, nv_ref, og_ref):
        g_ = jnp.where(pl.program_id(0) == 0, g0_ref[...], g1_ref[...])
        m2 = ADAM_B1 * m_ref[...] + (1.0 - ADAM_B1) * g_
        v2 = ADAM_B2 * v_ref[...] + (1.0 - ADAM_B2) * (g_ * g_)
        m_hat = m2 / (1.0 - ADAM_B1 ** ADAM_STEP)
        v_hat = v2 / (1.0 - ADAM_B2 ** ADAM_STEP)
        d_ref[...] = -ADAM_LR * (m_hat / (jnp.sqrt(v_hat) + ADAM_EPS) + ADAM_WD * w_ref[...])
        nm_ref[...] = m2
        nv_ref[...] = v2
        og_ref[...] = g_

    lay = pl.BlockSpec((None, tr, C), lambda l, i: (l, i, 0))
    one = lambda k: pl.BlockSpec((tr, C), lambda l, i: (jnp.where(l == k, i, 0), 0))
    return _call(body, name="adamw_pair", grid=(L, R // tr),
                 in_specs=[lay, one(0), one(1), lay, lay, pl.BlockSpec((8, 128), lambda l, i: (0, 0))],
                 out_specs=[lay] * 4,
                 out_shape=[jax.ShapeDtypeStruct((L, R, C), F32)] * 4,
                 sem=("parallel", "parallel"))(w, g0, g1, m, v, dep)


def _row_steps(rows):
    return 2 if rows % 32 == 0 else 1


def _pair_add(gbufs, rsibs, c):
    n = len(gbufs)
    steps = min(_row_steps(g.shape[2]) for g in gbufs)

    def body(c_ref, *refs):
        for a_ref, b_ref, o_ref in zip(refs[:n], refs[n:2 * n], refs[2 * n:]):
            o_ref[...] = (a_ref[...] + b_ref[...]).astype(BF)

    def specs(g):
        tr, C = g.shape[2] // steps, g.shape[3]
        return (pl.BlockSpec((None, None, tr, C), lambda j, i, c_ref: (j, c_ref[0], i, 0)),
                pl.BlockSpec((None, tr, C), lambda j, i, c_ref: (j, i, 0)))

    return list(pl.pallas_call(
        body, name="rs_pair_add",
        grid_spec=pltpu.PrefetchScalarGridSpec(
            num_scalar_prefetch=1, grid=(4, steps),
            in_specs=[specs(g)[0] for g in gbufs] + [specs(g)[1] for g in gbufs],
            out_specs=[specs(g)[1] for g in gbufs]),
        out_shape=[jax.ShapeDtypeStruct((4,) + g.shape[2:], BF) for g in gbufs],
        compiler_params=pltpu.CompilerParams(dimension_semantics=("parallel", "parallel")),
    )(jnp.reshape(c, (1,)).astype(jnp.int32), *gbufs, *rsibs))


def _chip_sum(pairs, recvs, me, c):
    n = len(pairs)
    steps = min(_row_steps(p.shape[1]) for p in pairs)

    def body(s_ref, *refs):
        for own_ref, p_ref, o_ref in zip(refs[:n], refs[n:2 * n], refs[2 * n:]):
            p = [jnp.where(s_ref[0] == j, own_ref[...], p_ref[j]).astype(F32) for j in range(4)]
            o_ref[...] = ((p[0] + p[1]) + p[2]) + p[3]

    def specs(p):
        tr, C = p.shape[1] // steps, p.shape[2]
        return (pl.BlockSpec((None, tr, C), lambda i, s: (s[0], i, 0)), pl.BlockSpec((4, tr, C), lambda i, s: (0, i, 0)),
                pl.BlockSpec((None, tr, C), lambda i, s: (s[1], i, 0)))

    return list(pl.pallas_call(
        body, name="rs_chip_sum",
        grid_spec=pltpu.PrefetchScalarGridSpec(
            num_scalar_prefetch=1, grid=(steps,),
            in_specs=[specs(p)[0] for p in pairs] + [specs(p)[1] for p in pairs],
            out_specs=[specs(p)[2] for p in pairs]),
        out_shape=[jax.ShapeDtypeStruct((2,) + p.shape[1:], F32) for p in pairs],
        compiler_params=pltpu.CompilerParams(dimension_semantics=("parallel",)),
    )(jnp.stack([me, c]).astype(jnp.int32), *pairs, *recvs))


MESH = pl.DeviceIdType.MESH
ANY = pl.BlockSpec(memory_space=pl.ANY)


def _place():
    x, y, c = lax.axis_index("x"), lax.axis_index("y"), lax.axis_index("c")
    return x, y, c, [(1 - x, y), (x, 1 - y), (1 - x, 1 - y)]


HBM = pl.BlockSpec(memory_space=pltpu.HBM)
SEM = pl.BlockSpec(memory_space=pltpu.SEMAPHORE)
EFFECT = pltpu.SideEffectType.DATAFLOW_SIDE_EFFECTING


class _Split:
    def __init__(self, tag, arrays, copies, n_copies, after=()):
        self.tag, self.copies, k = tag, copies, len(arrays)

        def body(*refs):
            sems = k + len(after)
            for cp in copies(refs[:k], refs[sems], refs[sems + 1]):
                cp.start()
            refs[-1][...] = jnp.zeros_like(refs[-1])

        out = pl.pallas_call(
            body, name=tag + "_start",
            out_shape=(pltpu.SemaphoreType.DMA((n_copies,)), pltpu.SemaphoreType.DMA((n_copies,)),
                       *[pltpu.HBM(a.shape, a.dtype) for a in arrays], jax.ShapeDtypeStruct((8, 128), F32)),
            in_specs=[HBM] * k + [ANY] * len(after),
            out_specs=(SEM, SEM, *[HBM] * k, pl.BlockSpec(memory_space=pltpu.VMEM)),
            input_output_aliases={i: 2 + i for i in range(k)},
            compiler_params=pltpu.CompilerParams(has_side_effects=EFFECT),
        )(*[pltpu.with_memory_space_constraint(a, pltpu.HBM) for a in arrays], *after)
        self.send, self.recv, self.arrays, self.token_array = out[0], out[1], list(out[2:2 + k]), out[-1]
        self.token = self.token_array[0, 0]

    def wait(self, after):
        k, copies = len(self.arrays), self.copies
        after = list(after) if isinstance(after, (list, tuple)) else [after]

        def body(*refs):
            for cp in copies(refs[:k], refs[k], refs[k + 1]):
                cp.wait_send()
                cp.wait_recv()

        return list(pl.pallas_call(
            body, name=self.tag + "_wait", out_shape=tuple(pltpu.HBM(a.shape, a.dtype) for a in self.arrays),
            in_specs=[HBM] * k + [SEM, SEM] + [ANY] * len(after), out_specs=tuple([HBM] * k),
            input_output_aliases={i: i for i in range(k)},
            compiler_params=pltpu.CompilerParams(has_side_effects=EFFECT),
        )(*self.arrays, self.send, self.recv, *after))


def _landing_zones(arrs):
    me = 2 * lax.axis_index("x") + lax.axis_index("y")
    return [lax.dynamic_update_index_in_dim(lax.empty((4,) + a.shape, a.dtype), a, me, 0) for a in arrs]


def _gather_start(arrs, lands, tag, after=()):
    n = len(arrs)

    def copies(refs, send, recv):
        x, y, c, chips = _place()
        return [pltpu.make_async_remote_copy(
            src_ref=refs[k], dst_ref=refs[n + k].at[2 * x + y], send_sem=send.at[3 * k + r],
            recv_sem=recv.at[3 * k + r], device_id=(px, py, c), device_id_type=MESH)
            for k in range(n) for r, (px, py) in enumerate(chips)]

    return _Split("gather_" + tag, list(arrs) + lands, copies, 3 * n, after)


def _gather_halves_start(arrs, tag):
    n = len(arrs)
    lands = _landing_zones(arrs)

    def copies(refs, send, recv):
        x, y, c, chips = _place()
        return [pltpu.make_async_remote_copy(
            src_ref=refs[k].at[c], dst_ref=refs[n + k].at[2 * x + y, c], send_sem=send.at[3 * k + r],
            recv_sem=recv.at[3 * k + r], device_id=(px, py, c), device_id_type=MESH)
            for k in range(n) for r, (px, py) in enumerate(chips)]

    return _Split("gather_" + tag, list(arrs) + lands, copies, 3 * n)


def _gather_halves_finish(lands, tag):
    n = len(lands)

    def copies(refs, send, recv):
        x, y, c, chips = _place()
        return [pltpu.make_async_remote_copy(
            src_ref=refs[k].at[2 * px + py, c], dst_ref=refs[k].at[2 * px + py, c], send_sem=send.at[3 * k + r],
            recv_sem=recv.at[3 * k + r], device_id=(x, y, 1 - c), device_id_type=MESH)
            for k in range(n) for r, (px, py) in enumerate(chips)]

    return _Split("gather_pass_" + tag, list(lands), copies, 3 * n)


def _part_sibling(gbufs):
    n = len(gbufs)

    def copies(refs, send, recv, off):
        x, y, c, _ = _place()
        return [pltpu.make_async_remote_copy(
            src_ref=refs[k].at[j, 1 - c], dst_ref=refs[n + k].at[j], send_sem=send.at[off + 4 * k + j],
            recv_sem=recv.at[off + 4 * k + j], device_id=(x, y, 1 - c), device_id_type=MESH)
            for k in range(n) for j in range(4)]

    return list(gbufs) + [lax.empty((4,) + g.shape[2:], g.dtype) for g in gbufs], 4 * n, copies


def _part_chips(pbufs):
    n = len(pbufs)

    def copies(refs, send, recv, off):
        x, y, c, chips = _place()
        return [pltpu.make_async_remote_copy(
            src_ref=refs[k].at[2 * px + py], dst_ref=refs[n + k].at[2 * x + y], send_sem=send.at[off + 3 * k + r],
            recv_sem=recv.at[off + 3 * k + r], device_id=(px, py, c), device_id_type=MESH)
            for k in range(n) for r, (px, py) in enumerate(chips)]

    return list(pbufs) + [lax.empty(p.shape, p.dtype) for p in pbufs], 3 * n, copies


def _part_join(fulls):
    def copies(refs, send, recv, off):
        x, y, c, _ = _place()
        return [pltpu.make_async_remote_copy(
            src_ref=refs[k].at[c], dst_ref=refs[k].at[c], send_sem=send.at[off + k], recv_sem=recv.at[off + k],
            device_id=(x, y, 1 - c), device_id_type=MESH) for k in range(len(fulls))]

    return list(fulls), len(fulls), copies


def _start_parts(parts, tag):
    arrays, spans, total = [], [], 0
    for arrs, n_copies, fn in parts:
        spans.append((len(arrays), len(arrs), total, fn))
        arrays += arrs
        total += n_copies

    def copies(refs, send, recv):
        return [cp for a0, na, off, fn in spans for cp in fn(refs[a0:a0 + na], send, recv, off)]

    op = _Split(tag, arrays, copies, total)
    op.spans = [(a0, na) for a0, na, _, _ in spans]
    return op


def _all_reduce_small(v):
    R, C = v.shape

    def body(v_ref, o_ref, g_ref, send, recv, loc):
        x, y, c, chips = _place()
        me, sibling = (x, y, c), (x, y, 1 - c)

        def rows(px, py, pc):
            return g_ref.at[4 * px + 2 * py + pc]

        def copy(k, block, to, src=None):
            return pltpu.make_async_remote_copy(
                src_ref=rows(*block) if src is None else src, dst_ref=rows(*block),
                send_sem=send.at[k], recv_sem=recv.at[k], device_id=to, device_id_type=MESH)

        mine = pltpu.make_async_copy(v_ref, rows(*me), loc)
        mine.start()
        first = [copy(0, me, sibling, src=v_ref)]
        first += [copy(1 + j, me, (*chip, c), src=v_ref) for j, chip in enumerate(chips)]
        for cp in first:
            cp.start()
        passed = [copy(4 + j, (*chip, c), sibling) for j, chip in enumerate(chips)]
        for j, chip in enumerate(chips):
            copy(1 + j, (*chip, c), me).wait_recv()
            passed[j].start()
        copy(0, sibling, me).wait_recv()
        for j, chip in enumerate(chips):
            copy(4 + j, (*chip, 1 - c), me).wait_recv()
        for cp in first + passed:
            cp.wait_send()
        mine.wait()
        acc = g_ref[0]
        for d in range(1, 8):
            acc = acc + g_ref[d]
        o_ref[...] = acc

    vm = pl.BlockSpec(memory_space=pltpu.VMEM)
    return pl.pallas_call(
        body, name="all_reduce_small", in_specs=[vm], out_specs=[vm, vm],
        out_shape=[jax.ShapeDtypeStruct((R, C), F32), jax.ShapeDtypeStruct((8, R, C), F32)],
        scratch_shapes=[pltpu.SemaphoreType.DMA((7,)), pltpu.SemaphoreType.DMA((7,)), pltpu.SemaphoreType.DMA],
    )(v)[0]


WEIGHTS = ['ffn1_norm', 'ffn1_w_gate', 'ffn1_w_up', 'ffn1_w_down', 'mix_norm', 'w_in', 'conv_w', 'conv_b', 'dt_bias',
           'a_log', 'd_skip', 'ssd_norm', 'sgu_ln_g', 'sgu_ln_b', 'sgu_w', 'sgu_b', 'w_out', 'ffn2_norm',
           'ffn2_w_gate', 'ffn2_w_up', 'ffn2_w_down', 'final_norm']
SHARDED = ['ffn1_w_gate', 'ffn1_w_up', 'ffn1_w_down', 'w_in', 'conv_w', 'w_out', 'ffn2_w_gate', 'ffn2_w_up',
           'ffn2_w_down']
SMALL = [n for n in WEIGHTS if n not in SHARDED]
GROUPS = [("ffn1", ["ffn1_w_gate", "ffn1_w_up", "ffn1_w_down"]), ("mix", ["w_in", "conv_w", "w_out"]),
          ("ffn2", ["ffn2_w_gate", "ffn2_w_up", "ffn2_w_down"])]
TRANSPOSED = ("ffn1_w_gate", "ffn1_w_up", "ffn2_w_gate", "ffn2_w_up")
DEPTH = 2


def _pack_w_in(w):
    return jnp.concatenate([w[..., 0:1152], w[..., 1536:2432], w[..., 1152:1536],
                            jnp.repeat(w[..., 2432:2438], HEAD, axis=-1), w[..., 2438:2950]], axis=-1)


def _unpack_w_in(dq, ds, du):
    return jnp.concatenate([dq, ds[:, 896:1280], ds[:, 0:896], ds[:, 1280::HEAD], du], axis=-1)


def _ffn_fwd(x, g, wg, wu, wd):
    xo, hb, S1, S2, A = _ffn_fwd_k(x, g, wg, wu, wd)
    return xo, (x, hb, S1, S2, A)


def _ffn_bwd_weights(dxo, saved, wd):
    x, hb, S1, S2, A = saved
    dG, dU, dyb = _ffn_bwd_act(dxo, S1, S2, wd)
    return (dG, dU), _ffn_bwd_k2(hb, dyb, A, dG, dU)


def _ffn_bwd_input(dxo, saved, mids, g, wg, wu):
    return _ffn_bwd_dx(mids[0], mids[1], wg, wu, saved[0], g, dxo)


def _mix_fwd(x, P):
    hb, qkv, sin, uv = _mix_proj(x, P["mix_norm"], P["w_in"])
    y_att, lse = _attn_combine([_attn_fwd(qkv, d) for d in DILATIONS])
    y_ssd, hprev = _ssd_fwd(sin, *P["ssd"])
    y_sgu = _sgu_fwd(uv, *P["sgu"])
    ycat = jnp.concatenate([y_att, y_ssd, y_sgu], axis=1).astype(BF)
    return _mm_nn(ycat, P["w_out"], res=x), (x, hb, qkv, sin, uv, y_att, lse, hprev, ycat)


def _mix_bwd_weights(dxo, saved, P):
    x, hb, qkv, sin, uv, y_att, lse, hprev, ycat = saved
    dy_att, dy_ssd, dy_sgu = _mix_bwd_dy(dxo, P["w_out"])
    dwout = _mm_tn(ycat, dxo)
    dqkv = _sum_branches([_attn_bwd(qkv, dy_att, y_att, lse, d) for d in DILATIONS])
    dsin, dcw, dcb, dvec = _ssd_bwd(sin, hprev, dy_ssd, *P["ssd"])
    duv, dsw, dsbias, dln = _sgu_bwd(uv, dy_sgu, *P["sgu"])
    dwin = _unpack_w_in(_mm_tn(hb, dqkv), _mm_tn(hb, dsin), _mm_tn(hb, duv))
    grads = dict(
        w_in=dwin, conv_w=dcw, conv_b=dcb[0], dt_bias=dvec[0, ::HEAD], a_log=dvec[1, ::HEAD],
        d_skip=jnp.sum(dvec[2].reshape(6, HEAD), axis=-1), ssd_norm=dvec[3], sgu_ln_g=dln[0], sgu_ln_b=dln[1],
        sgu_w=dsw, sgu_b=jnp.sum(dsbias.reshape(CHUNK, 4, HEAD), axis=-1).T, w_out=dwout)
    return (dqkv, dsin, duv), grads


def _mix_bwd_input(dxo, saved, mids, P):
    return _mix_bwd_dx(*mids, P["w_in"], saved[0], P["mix_norm"], dxo)


def _halved(g):
    rows = g.size // g.shape[-1]
    return g.reshape(4, 2, rows // 8, g.shape[-1])


def kernel(x, ffn1_norm, ffn1_w_gate, ffn1_w_up, ffn1_w_down, mix_norm, w_in, conv_w, conv_b, dt_bias, a_log, d_skip, ssd_norm, sgu_ln_g, sgu_ln_b, sgu_w, sgu_b, w_out, ffn2_norm, ffn2_w_gate, ffn2_w_up, ffn2_w_down, final_norm, loss_target, m_ffn1_norm, m_ffn1_w_gate, m_ffn1_w_up, m_ffn1_w_down, m_mix_norm, m_w_in, m_conv_w, m_conv_b, m_dt_bias, m_a_log, m_d_skip, m_ssd_norm, m_sgu_ln_g, m_sgu_ln_b, m_sgu_w, m_sgu_b, m_w_out, m_ffn2_norm, m_ffn2_w_gate, m_ffn2_w_up, m_ffn2_w_down, m_final_norm, v_ffn1_norm, v_ffn1_w_gate, v_ffn1_w_up, v_ffn1_w_down, v_mix_norm, v_w_in, v_conv_w, v_conv_b, v_dt_bias, v_a_log, v_d_skip, v_ssd_norm, v_sgu_ln_g, v_sgu_ln_b, v_sgu_w, v_sgu_b, v_w_out, v_ffn2_norm, v_ffn2_w_gate, v_ffn2_w_up, v_ffn2_w_down, v_final_norm):
    given = dict(x=x, ffn1_norm=ffn1_norm, ffn1_w_gate=ffn1_w_gate, ffn1_w_up=ffn1_w_up, ffn1_w_down=ffn1_w_down, mix_norm=mix_norm, w_in=w_in, conv_w=conv_w, conv_b=conv_b, dt_bias=dt_bias, a_log=a_log, d_skip=d_skip, ssd_norm=ssd_norm, sgu_ln_g=sgu_ln_g, sgu_ln_b=sgu_ln_b, sgu_w=sgu_w, sgu_b=sgu_b, w_out=w_out, ffn2_norm=ffn2_norm, ffn2_w_gate=ffn2_w_gate, ffn2_w_up=ffn2_w_up, ffn2_w_down=ffn2_w_down, final_norm=final_norm, loss_target=loss_target, m_ffn1_norm=m_ffn1_norm, m_ffn1_w_gate=m_ffn1_w_gate, m_ffn1_w_up=m_ffn1_w_up, m_ffn1_w_down=m_ffn1_w_down, m_mix_norm=m_mix_norm, m_w_in=m_w_in, m_conv_w=m_conv_w, m_conv_b=m_conv_b, m_dt_bias=m_dt_bias, m_a_log=m_a_log, m_d_skip=m_d_skip, m_ssd_norm=m_ssd_norm, m_sgu_ln_g=m_sgu_ln_g, m_sgu_ln_b=m_sgu_ln_b, m_sgu_w=m_sgu_w, m_sgu_b=m_sgu_b, m_w_out=m_w_out, m_ffn2_norm=m_ffn2_norm, m_ffn2_w_gate=m_ffn2_w_gate, m_ffn2_w_up=m_ffn2_w_up, m_ffn2_w_down=m_ffn2_w_down, m_final_norm=m_final_norm, v_ffn1_norm=v_ffn1_norm, v_ffn1_w_gate=v_ffn1_w_gate, v_ffn1_w_up=v_ffn1_w_up, v_ffn1_w_down=v_ffn1_w_down, v_mix_norm=v_mix_norm, v_w_in=v_w_in, v_conv_w=v_conv_w, v_conv_b=v_conv_b, v_dt_bias=v_dt_bias, v_a_log=v_a_log, v_d_skip=v_d_skip, v_ssd_norm=v_ssd_norm, v_sgu_ln_g=v_sgu_ln_g, v_sgu_ln_b=v_sgu_ln_b, v_sgu_w=v_sgu_w, v_sgu_b=v_sgu_b, v_w_out=v_w_out, v_ffn2_norm=v_ffn2_norm, v_ffn2_w_gate=v_ffn2_w_gate, v_ffn2_w_up=v_ffn2_w_up, v_ffn2_w_down=v_ffn2_w_down, v_final_norm=v_final_norm)
    T = given["x"].shape[0] * given["x"].shape[1]
    D = given["x"].shape[2]
    x0 = given["x"].reshape(T, D)
    tgt = given["loss_target"].reshape(T, D)
    c = lax.axis_index("c")

    bf = {n: given[n].astype(BF) for n in SHARDED if n not in ("w_in", "conv_w")}
    bf["w_in"] = _pack_w_in(given["w_in"]).astype(BF)
    bf["conv_w"] = given["conv_w"]
    first_key = (0, GROUPS[0][0])
    first = [bf[n][0].reshape((2, bf[n].shape[1] // 2) + bf[n].shape[2:]) for n in GROUPS[0][1]]
    gathers = {first_key: _gather_halves_start(first, "l0_" + GROUPS[0][0])}
    later = {(i, gname): [bf[n][i] for n in names] for i in range(DEPTH) for gname, names in GROUPS
             if (i, gname) != first_key}
    zones = {key: _landing_zones(arrs) for key, arrs in later.items()}

    def gathered(i, gname, after):
        if (i, gname) != first_key:
            return gathers[(i, gname)].wait(after)[3:]
        got = gathers[first_key].wait([after] + [z for zs in zones.values() for z in zs])[3:]
        got = _gather_halves_finish(got, "l0_" + gname).wait(after)
        prev = got[0]
        for key, arrs in later.items():
            gathers[key] = _gather_start(arrs, zones[key], f"l{key[0]}_{key[1]}", after=[prev])
            prev = gathers[key].token_array
        return [z.reshape((4, 2 * z.shape[2]) + z.shape[3:]) for z in got]

    def mix_params(i, got):
        win = got[0].reshape(D, W_QKV + W_SSD + W_UV)
        rep = lambda v: jnp.repeat(v, HEAD)[None]
        ssd = (got[1].transpose(1, 0, 2).reshape(4, SSD_CONV_DIM), given["conv_b"][i][None],
               rep(given["dt_bias"][i]), rep(given["a_log"][i]), rep(given["d_skip"][i]), given["ssd_norm"][i][None])
        sgu = (given["sgu_ln_g"][i][None], given["sgu_ln_b"][i][None], given["sgu_w"][i],
               jnp.repeat(given["sgu_b"][i].T, HEAD, axis=1))
        return dict(mix_norm=given["mix_norm"][i][None], w_in=win, w_out=got[2].reshape(-1, D), ssd=ssd, sgu=sgu)

    x = x0
    tape = []
    for i in range(DEPTH):
        got = gathered(i, "ffn1", x)
        token = functools.reduce(lambda a, b: a + b, [g.token for g in gathers.values()]) if i == 0 else 0.0
        P = dict(ffn1=(given["ffn1_norm"][i][None] + token, *got))
        x, s1 = _ffn_fwd(x, *P["ffn1"])
        P.update(mix_params(i, gathered(i, "mix", x)))
        x, s2 = _mix_fwd(x, P)
        P["ffn2"] = (given["ffn2_norm"][i][None], *gathered(i, "ffn2", x))
        x, s3 = _ffn_fwd(x, *P["ffn2"])
        tape.append((P, s1, s2, s3))
    loss_part, dx, dgf = _final_loss(x, given["final_norm"][None], tgt)

    me = 2 * lax.axis_index("x") + lax.axis_index("y")
    jobs = []

    flight = dict(op=None, owners=[], ticks=0)

    def tick(after, begin=None):
        parts, owners = [], []
        if flight["op"] is not None:
            got = flight["op"].wait(after)
            for job, (a0, na) in zip(flight["owners"], flight["op"].spans):
                mine, k = got[a0:a0 + na], len(job["names"])
                if job["stage"] == 1:
                    parts.append(_part_chips(_pair_add(mine[:k], mine[k:], c)))
                elif job["stage"] == 2:
                    parts.append(_part_join(_chip_sum(mine[:k], mine[k:], me, c)))
                else:
                    job.update(stage=4, out=dict(zip(job["names"], mine)))
                    continue
                job["stage"] += 1
                owners.append(job)
        if begin is not None:
            i, gname, gd = begin
            names = [n for n in dict(GROUPS)[gname] if n != "conv_w"]
            jobs.append(dict(key=(i, gname), names=names, stage=1))
            parts.append(_part_sibling([_halved(gd[n]) for n in names]))
            owners.append(jobs[-1])
        flight.update(op=_start_parts(parts, f"rs_tick{flight['ticks']}") if parts else None, owners=owners,
                      ticks=flight["ticks"] + 1)
        return flight["op"].token if parts else 0.0

    grads = [dict() for _ in range(DEPTH)]
    for i in reversed(range(DEPTH)):
        P, s1, s2, s3 = tape[i]
        g = grads[i]
        norm, wg, wu, wd = P["ffn2"]
        mids, (g["ffn2_w_gate"], g["ffn2_w_up"], g["ffn2_w_down"]) = _ffn_bwd_weights(dx, s3, wd)
        tok = tick(g["ffn2_w_down"], (i, "ffn2", g))
        dx, dn2 = _ffn_bwd_input(dx, s3, mids, norm + tok, wg, wu)
        mids, gm = _mix_bwd_weights(dx, s2, P)
        g.update(gm)
        tok = tick(gm["w_in"], (i, "mix", g))
        dx, dnm = _mix_bwd_input(dx, s2, mids, {**P, "mix_norm": P["mix_norm"] + tok})
        norm, wg, wu, wd = P["ffn1"]
        mids, (g["ffn1_w_gate"], g["ffn1_w_up"], g["ffn1_w_down"]) = _ffn_bwd_weights(dx, s1, wd)
        tok = tick(g["ffn1_w_down"], (i, "ffn1", g))
        dx, dn1 = _ffn_bwd_input(dx, s1, mids, norm + tok, wg, wu)
        g["ffn1_norm"], g["mix_norm"], g["ffn2_norm"] = dn1[0], dnm[0], dn2[0]
    grad_x = dx.reshape(given["x"].shape)

    order = [n for n in SMALL if n != "final_norm"] + ["final_norm"]
    small = [jnp.stack([grads[i][n] for i in range(DEPTH)]) for n in order[:-1] + ["conv_w"]]
    small = small[:-1] + [dgf[0], small[-1], loss_part[0, 0:1]]
    n_small = sum(s.size for s in small)
    rows_small = -(-n_small // (128 * 8)) * 8

    def flat(arrs):
        fill = rows_small * 128 - sum(a.size for a in arrs)
        return jnp.concatenate([a.reshape(-1) for a in arrs] + [jnp.zeros((fill,), F32)]).reshape(rows_small, 128)

    gsmall = _all_reduce_small(flat(small)).reshape(-1)

    grad_w = {}
    off = 0
    for n in order:
        size = given[n].size
        grad_w[n] = gsmall[off:off + size].reshape(given[n].shape)
        off += size
    cw = gsmall[off:off + 2 * 4 * SSD_CONV_DIM].reshape(DEPTH, 4, SSD_CONV_DIM)
    grad_w["conv_w"] = lax.dynamic_slice_in_dim(cw, me * (SSD_CONV_DIM // 4), SSD_CONV_DIM // 4, axis=2)
    loss = gsmall[off + 2 * 4 * SSD_CONV_DIM]

    delta, new_m, new_v = {}, {}, {}
    shp = given["conv_w"].shape
    d, m2, v2 = _adamw(*[a.reshape(shp[0] * shp[1], shp[2])
                         for a in (given["conv_w"], grad_w["conv_w"], given["m_conv_w"], given["v_conv_w"])])
    delta["conv_w"], new_m["conv_w"], new_v["conv_w"] = d.reshape(shp), m2.reshape(shp), v2.reshape(shp)
    packed = [flat([given[pre + n] for n in order]) for pre in ("", "m_", "v_")]
    small_out = _adamw(packed[0], gsmall.reshape(rows_small, 128), packed[1], packed[2])
    outs = [o.reshape(-1) for o in small_out]
    off = 0
    for n in order:
        size = given[n].size
        for dst, o in zip((delta, new_m, new_v), outs):
            dst[n] = o[off:off + size].reshape(given[n].shape)
        off += size

    stepped, arrived = {}, {}

    def update_arrived(dep):
        out = None
        for job in jobs:
            if job["stage"] == 4 and not job.get("seen"):
                job["seen"] = True
                for n, full in job["out"].items():
                    view = (lambda a: jnp.swapaxes(a, 1, 2)) if n in TRANSPOSED else (lambda a: a)
                    arrived.setdefault(n, {})[job["key"][0]] = full.reshape(view(given[n]).shape[1:])
                    if len(arrived[n]) == DEPTH:
                        res = _adamw_pair(view(given[n]), arrived[n][0], arrived[n][1], view(given["m_" + n]),
                                          view(given["v_" + n]), dep)
                        stepped[n] = [view(r) for r in res]
                        out = res[0]
        return out

    after = small_out[0]
    while any(j["stage"] < 4 for j in jobs):
        done = update_arrived(jnp.zeros((8, 128), F32) + tok)
        after = after if done is None else done
        tok = tick(after)
    update_arrived(jnp.zeros((8, 128), F32) + tok)
    for n, (d, m2, v2, g) in stepped.items():
        delta[n], new_m[n], new_v[n], grad_w[n] = d, m2, v2, g

    return (loss, grad_x, *[grad_w[n] for n in WEIGHTS], *[delta[n] for n in WEIGHTS],
            *[new_m[n] for n in WEIGHTS], *[new_v[n] for n in WEIGHTS])
```

```python
import functools
import math

import jax
import jax.numpy as jnp
from jax import lax
from jax.experimental import pallas as pl
from jax.experimental.pallas import tpu as pltpu

F32 = jnp.float32
BF = jnp.bfloat16

RMS_EPS = 1e-6
LN_EPS = 1e-5
SEQ = 2048
CHUNK = 128
N_CHUNK = SEQ // CHUNK
ATT_W = 384
HEAD = 64
SSD_W = 384
SSD_CONV_DIM = 896
SSD_STATE = 128
SGU_W = 256
DILATIONS = (1, 4, 16)
W_QKV = 3 * ATT_W
W_SSD = SSD_CONV_DIM + SSD_W + SSD_W
W_UV = 2 * SGU_W
ADAM_LR = 0.001
ADAM_B1 = 0.9
ADAM_B2 = 0.999
ADAM_EPS = 1e-08
ADAM_WD = 0.01
ADAM_STEP = 10
NEG = -1e30
ATTN_BWD_VMEM = 48 * 2 ** 20
ATTN_SUBSEQ_PER_STEP = 4
FFN_VMEM = 60 * 2 ** 20


def _dot(a, b):
    return jnp.dot(a, b, preferred_element_type=F32)


def _dot_nt(a, b):
    return lax.dot_general(a, b, (((1,), (1,)), ((), ())), preferred_element_type=F32)


def _dot_tn(a, b):
    return lax.dot_general(a, b, (((0,), (0,)), ((), ())), preferred_element_type=F32)


def _sigmoid(x):
    return 1.0 / (1.0 + jnp.exp(-x))


def _call(body, *, name, grid, in_specs, out_specs, out_shape, scratch=(), sem=None, vmem=None):
    return pl.pallas_call(
        body, name=name, grid=grid, in_specs=in_specs, out_specs=out_specs, out_shape=out_shape,
        scratch_shapes=list(scratch),
        compiler_params=pltpu.CompilerParams(dimension_semantics=sem, vmem_limit_bytes=vmem),
    )


def _tile(n, want):
    t = min(n, want)
    while n % t:
        t //= 2
    return t


def _final_loss(x, g, tgt):
    T, D = x.shape
    tm = _tile(T, 512)

    def body(x_ref, g_ref, t_ref, l_ref, dx_ref, dg_ref):
        @pl.when(pl.program_id(0) == 0)
        def _():
            dg_ref[...] = jnp.zeros_like(dg_ref)
            l_ref[...] = jnp.zeros_like(l_ref)

        xf = x_ref[...]
        gg = g_ref[...]
        r = lax.rsqrt(jnp.mean(xf * xf, axis=-1, keepdims=True) + RMS_EPS)
        xn = xf * r
        e = xn * gg - t_ref[...]
        part = 0.5 * jnp.sum(jnp.mean(e * e, axis=-1, keepdims=True), axis=0, keepdims=True)
        l_ref[...] += jnp.broadcast_to(part, l_ref.shape)
        dy = e * (1.0 / D)
        u = dy * gg
        mu = jnp.mean(u * xf, axis=-1, keepdims=True)
        dx_ref[...] = r * (u - xf * (r * r * mu))
        dg_ref[...] += jnp.sum(dy * xn, axis=0, keepdims=True)

    row = pl.BlockSpec((tm, D), lambda i: (i, 0))
    vec = pl.BlockSpec((1, D), lambda i: (0, 0))
    lsp = pl.BlockSpec((1, 128), lambda i: (0, 0))
    return _call(body, name="final_loss", grid=(T // tm,), in_specs=[row, vec, row], out_specs=[lsp, row, vec],
                 out_shape=[jax.ShapeDtypeStruct((1, 128), F32), jax.ShapeDtypeStruct((T, D), F32),
                            jax.ShapeDtypeStruct((1, D), F32)],
                 sem=("arbitrary",))(x, g, tgt)


def _resident(shape):
    return pl.BlockSpec(shape, lambda *_: (0,) * len(shape), pipeline_mode=pl.Buffered(1))


def _ffn_fwd_k(x, gn, wg, wu, wd):
    T, D = x.shape
    NS, _, Fs = wg.shape
    tm = _tile(T, 1024)

    def body(x_ref, gn_ref, wg_ref, wu_ref, wd_ref, o_ref, h_ref, s1_ref, s2_ref, a_ref, hs, acc):
        j = pl.program_id(1)

        @pl.when(j == 0)
        def _():
            xf = x_ref[...]
            r = lax.rsqrt(jnp.mean(xf * xf, axis=-1, keepdims=True) + RMS_EPS)
            hs[...] = (xf * r * gn_ref[...]).astype(BF)
            h_ref[...] = hs[...]
            acc[...] = jnp.zeros_like(acc)

        h = hs[...]
        g = _dot(h, wg_ref[...])
        u = _dot(h, wu_ref[...])
        sg = _sigmoid(g)
        s1 = g * sg
        a = (s1 * u).astype(BF)
        s1_ref[...] = s1.astype(BF)
        s2_ref[...] = (u * (sg * (1.0 + g * (1.0 - sg)))).astype(BF)
        a_ref[...] = a
        acc[...] += _dot(a, wd_ref[...])

        @pl.when(j == NS - 1)
        def _():
            o_ref[...] = x_ref[...] + 0.5 * acc[...]

    row = pl.BlockSpec((tm, D), lambda i, j: (i, 0))
    act = pl.BlockSpec((None, tm, Fs), lambda i, j: (j, i, 0))
    sh = jax.ShapeDtypeStruct((NS, T, Fs), BF)
    wspec = lambda w: pl.BlockSpec((None,) + w.shape[1:], lambda i, j: (j, 0, 0))
    return _call(body, name="ffn_fwd", grid=(T // tm, NS),
                 in_specs=[row, pl.BlockSpec((1, D), lambda i, j: (0, 0)), wspec(wg), wspec(wu), wspec(wd)],
                 out_specs=[row, row, act, act, act],
                 out_shape=[jax.ShapeDtypeStruct((T, D), F32), jax.ShapeDtypeStruct((T, D), BF), sh, sh, sh],
                 scratch=[pltpu.VMEM((tm, D), BF), pltpu.VMEM((tm, D), F32)],
                 sem=("parallel", "arbitrary"), vmem=FFN_VMEM)(x, gn, wg, wu, wd)


def _ffn_bwd_act(dxo, s1, s2, wd):
    NS, T, Fs = s1.shape
    D = dxo.shape[1]
    tm = _tile(T, 1024)

    def body(dxo_ref, s1_ref, s2_ref, wd_ref, dg_ref, du_ref, dy_ref, dys):
        j = pl.program_id(1)

        @pl.when(j == 0)
        def _():
            dys[...] = (0.5 * dxo_ref[...]).astype(BF)
            dy_ref[...] = dys[...]

        da = _dot_nt(dys[...], wd_ref[j])
        dg_ref[...] = (da * s2_ref[...].astype(F32)).astype(BF)
        du_ref[...] = (da * s1_ref[...].astype(F32)).astype(BF)

    row = pl.BlockSpec((tm, D), lambda i, j: (i, 0))
    act = pl.BlockSpec((None, tm, Fs), lambda i, j: (j, i, 0))
    sh = jax.ShapeDtypeStruct((NS, T, Fs), BF)
    return _call(body, name="ffn_bwd_act", grid=(T // tm, NS), in_specs=[row, act, act, _resident(wd.shape)],
                 out_specs=[act, act, row], out_shape=[sh, sh, jax.ShapeDtypeStruct((T, D), BF)],
                 scratch=[pltpu.VMEM((tm, D), BF)], sem=("parallel", "arbitrary"))(dxo, s1, s2, wd)


def _ffn_bwd_dx(dg, du, wg, wu, x, gn, dxo):
    NS, T, Fs = dg.shape
    D = x.shape[1]
    tm = _tile(T, 1024)

    def body(dg_ref, du_ref, wg_ref, wu_ref, x_ref, gn_ref, dxo_ref, dx_ref, dgn_ref, acc):
        i, j = pl.program_id(0), pl.program_id(1)

        @pl.when((i == 0) & (j == 0))
        def _():
            dgn_ref[...] = jnp.zeros_like(dgn_ref)

        @pl.when(j == 0)
        def _():
            acc[...] = jnp.zeros_like(acc)

        acc[...] += _dot_nt(dg_ref[...], wg_ref[j]) + _dot_nt(du_ref[...], wu_ref[j])

        @pl.when(j == NS - 1)
        def _():
            xf = x_ref[...]
            r = lax.rsqrt(jnp.mean(xf * xf, axis=-1, keepdims=True) + RMS_EPS)
            dh = acc[...]
            uu = dh * gn_ref[...]
            mu = jnp.mean(uu * xf, axis=-1, keepdims=True)
            dx_ref[...] = dxo_ref[...] + r * (uu - xf * (r * r * mu))
            dgn_ref[...] += jnp.sum(dh * xf * r, axis=0, keepdims=True)

    row = pl.BlockSpec((tm, D), lambda i, j: (i, 0))
    vec = pl.BlockSpec((1, D), lambda i, j: (0, 0))
    act = pl.BlockSpec((None, tm, Fs), lambda i, j: (j, i, 0))
    return _call(body, name="ffn_bwd_dx", grid=(T // tm, NS),
                 in_specs=[act, act, _resident(wg.shape), _resident(wu.shape), row, vec, row], out_specs=[row, vec],
                 out_shape=[jax.ShapeDtypeStruct((T, D), F32), jax.ShapeDtypeStruct((1, D), F32)],
                 scratch=[pltpu.VMEM((tm, D), F32)], sem=("arbitrary", "arbitrary"), vmem=FFN_VMEM)(
        dg, du, wg, wu, x, gn, dxo)


def _ffn_bwd_k2(hb, dyb, a, dg, du):
    NS, T, Fs = a.shape
    D = hb.shape[1]
    tk = _tile(T, 1024)

    def body(h_ref, dy_ref, a_ref, dg_ref, du_ref, og_ref, ou_ref, od_ref):
        @pl.when(pl.program_id(1) == 0)
        def _():
            og_ref[...] = jnp.zeros_like(og_ref)
            ou_ref[...] = jnp.zeros_like(ou_ref)
            od_ref[...] = jnp.zeros_like(od_ref)

        h = h_ref[...]
        og_ref[...] += _dot_tn(dg_ref[...], h)
        ou_ref[...] += _dot_tn(du_ref[...], h)
        od_ref[...] += _dot_tn(a_ref[...], dy_ref[...])

    row = pl.BlockSpec((tk, D), lambda j, k: (k, 0))
    act = pl.BlockSpec((None, tk, Fs), lambda j, k: (j, k, 0))
    return _call(body, name="ffn_bwd_w", grid=(NS, T // tk), in_specs=[row, row, act, act, act],
                 out_specs=[pl.BlockSpec((None, Fs, D), lambda j, k: (j, 0, 0))] * 3,
                 out_shape=[jax.ShapeDtypeStruct((NS, Fs, D), F32)] * 3,
                 sem=("parallel", "arbitrary"))(hb, dyb, a, dg, du)


def _mm_nn(a, b, res=None, out_dtype=F32):
    T, K = a.shape
    N = b.shape[1]
    tm = _tile(T, 512)
    tn = N if N <= 2048 else _tile(N, 1024)

    def body(*refs):
        if res is None:
            a_ref, b_ref, o_ref = refs
            o_ref[...] = _dot(a_ref[...], b_ref[...]).astype(out_dtype)
        else:
            a_ref, b_ref, r_ref, o_ref = refs
            o_ref[...] = (r_ref[...] + _dot(a_ref[...], b_ref[...])).astype(out_dtype)

    o = pl.BlockSpec((tm, tn), lambda i, j: (i, j))
    ins = [pl.BlockSpec((tm, K), lambda i, j: (i, 0)), pl.BlockSpec((K, tn), lambda i, j: (0, j))]
    args = [a, b]
    if res is not None:
        ins.append(o)
        args.append(res)
    return _call(body, name="mm_nn", grid=(T // tm, N // tn), in_specs=ins, out_specs=o,
                 out_shape=jax.ShapeDtypeStruct((T, N), out_dtype), sem=("parallel", "parallel"))(*args)


def _mix_bwd_dy(dxo, w_out):
    T, D = dxo.shape
    tm = _tile(T, 512)
    cuts = (0, ATT_W, ATT_W + SSD_W, ATT_W + SSD_W + SGU_W)

    def body(dx_ref, w_ref, a_ref, s_ref, g_ref):
        d = _dot_nt(dx_ref[...].astype(BF), w_ref[...])
        for o_ref, lo, hi in zip((a_ref, s_ref, g_ref), cuts[:-1], cuts[1:]):
            o_ref[...] = d[:, lo:hi]

    row = lambda w: pl.BlockSpec((tm, w), lambda i: (i, 0))
    return _call(body, name="mix_bwd_dy", grid=(T // tm,), in_specs=[row(D), _resident(w_out.shape)],
                 out_specs=[row(ATT_W), row(SSD_W), row(SGU_W)],
                 out_shape=[jax.ShapeDtypeStruct((T, w), F32) for w in (ATT_W, SSD_W, SGU_W)],
                 sem=("parallel",))(dxo, w_out)


def _mm_tn(a, b):
    T, M = a.shape
    N = b.shape[1]
    tk = _tile(T, 1024)
    tmm = _tile(M, 512)

    def body(a_ref, b_ref, o_ref):
        @pl.when(pl.program_id(1) == 0)
        def _():
            o_ref[...] = jnp.zeros_like(o_ref)

        o_ref[...] += _dot_tn(a_ref[...].astype(BF), b_ref[...].astype(BF))

    return _call(body, name="mm_tn", grid=(M // tmm, T // tk),
                 in_specs=[pl.BlockSpec((tk, tmm), lambda i, k: (k, i)), pl.BlockSpec((tk, N), lambda i, k: (k, 0))],
                 out_specs=pl.BlockSpec((tmm, N), lambda i, k: (i, 0)),
                 out_shape=jax.ShapeDtypeStruct((M, N), F32), sem=("parallel", "arbitrary"))(a, b)


def _mix_proj(x, gn, win):
    T, D = x.shape
    tm = _tile(T, 512)
    cuts = (0, W_QKV, W_QKV + W_SSD, W_QKV + W_SSD + W_UV)

    def body(x_ref, gn_ref, w_ref, h_ref, q_ref, s_ref, u_ref):
        xf = x_ref[...]
        r = lax.rsqrt(jnp.mean(xf * xf, axis=-1, keepdims=True) + RMS_EPS)
        h = (xf * r * gn_ref[...]).astype(BF)
        h_ref[...] = h
        for o_ref, lo, hi in zip((q_ref, s_ref, u_ref), cuts[:-1], cuts[1:]):
            o_ref[...] = _dot(h, w_ref[:, lo:hi])

    row = lambda w: pl.BlockSpec((tm, w), lambda i: (i, 0))
    return _call(body, name="mix_proj", grid=(T // tm,),
                 in_specs=[row(D), pl.BlockSpec((1, D), lambda i: (0, 0)), _resident(win.shape)],
                 out_specs=[row(D), row(W_QKV), row(W_SSD), row(W_UV)],
                 out_shape=[jax.ShapeDtypeStruct((T, D), BF), jax.ShapeDtypeStruct((T, W_QKV), F32),
                            jax.ShapeDtypeStruct((T, W_SSD), F32), jax.ShapeDtypeStruct((T, W_UV), F32)],
                 sem=("parallel",))(x, gn, win)


def _mix_bwd_dx(dqkv, dsin, duv, win, x, gn, dxo):
    T, D = x.shape
    tm = _tile(T, 512)
    cuts = (0, W_QKV, W_QKV + W_SSD, W_QKV + W_SSD + W_UV)

    def body(dq_ref, ds_ref, du_ref, w_ref, x_ref, gn_ref, dxo_ref, dx_ref, dgn_ref):
        @pl.when(pl.program_id(0) == 0)
        def _():
            dgn_ref[...] = jnp.zeros_like(dgn_ref)

        dh = (_dot_nt(dq_ref[...], w_ref[:, cuts[0]:cuts[1]]) + _dot_nt(ds_ref[...], w_ref[:, cuts[1]:cuts[2]])
              + _dot_nt(du_ref[...], w_ref[:, cuts[2]:cuts[3]]))
        xf = x_ref[...]
        r = lax.rsqrt(jnp.mean(xf * xf, axis=-1, keepdims=True) + RMS_EPS)
        uu = dh * gn_ref[...]
        mu = jnp.mean(uu * xf, axis=-1, keepdims=True)
        dx_ref[...] = dxo_ref[...] + r * (uu - xf * (r * r * mu))
        dgn_ref[...] += jnp.sum(dh * xf * r, axis=0, keepdims=True)

    row = lambda w: pl.BlockSpec((tm, w), lambda i: (i, 0))
    vec = pl.BlockSpec((1, D), lambda i: (0, 0))
    return _call(body, name="mix_bwd_dx", grid=(T // tm,),
                 in_specs=[row(W_QKV), row(W_SSD), row(W_UV), _resident(win.shape), row(D), vec, row(D)],
                 out_specs=[row(D), vec],
                 out_shape=[jax.ShapeDtypeStruct((T, D), F32), jax.ShapeDtypeStruct((1, D), F32)],
                 sem=("arbitrary",))(dqkv, dsin, duv, win, x, gn, dxo)


def _lane_mask(e, width=128):
    return (lax.broadcasted_iota(jnp.int32, (1, width), 1) // HEAD) == e


def _band_mask(n):
    qi = lax.broadcasted_iota(jnp.int32, (CHUNK, 2 * CHUNK), 0)
    kj = lax.broadcasted_iota(jnp.int32, (CHUNK, 2 * CHUNK), 1)
    dist = qi + CHUNK - kj
    return (dist >= 0) & (dist <= CHUNK) & ((kj >= CHUNK) | (n > 0))


def _sub_rows(r, block, dil):
    if dil == 1:
        return pl.ds(pl.multiple_of(block * CHUNK, CHUNK), CHUNK)
    return pl.ds(r + dil * CHUNK * block, CHUNK, stride=dil)


def _attn_specs(T, dil):
    per_step = ATTN_SUBSEQ_PER_STEP
    qrows = CHUNK * (dil if dil > 1 else per_step)
    B, nbq = T // SEQ, SEQ // qrows
    once = dict(pipeline_mode=pl.Buffered(1))
    q_like = lambda col: pl.BlockSpec((qrows, 128), lambda b, n, r: (b * nbq + n, col), **(once if nbq == 1 else {}))
    k_like = lambda col: pl.BlockSpec((SEQ, 128), lambda b, n, r: (b, col), **once)
    return B, nbq, max(dil // per_step, 1), min(per_step, max(dil, per_step)), q_like, k_like


def _attn_step(u, dil, per_step):
    if dil > 1:
        r = pl.program_id(2) * per_step + u
        return r, pl.program_id(1), _sub_rows(r, 0, dil)
    return 0, pl.program_id(1) * per_step + u, pl.ds(CHUNK * u, CHUNK)


def _attn_fwd(qkv, dil):
    T = qkv.shape[0]
    B, nb, last, per_step, q_like, k_like = _attn_specs(T, dil)
    scale = HEAD ** -0.5

    def body(*refs):
        q_t, k_t, v_t, o_t, l_t = refs[0:3], refs[3:6], refs[6:9], refs[9:12], refs[12:15]
        for u in range(per_step):
            r, n, mine = _attn_step(u, dil, per_step)
            mask = _band_mask(n)
            cur, prv = _sub_rows(r, n, dil), _sub_rows(r, jnp.maximum(n - 1, 0), dil)
            for t in range(3):
                qt = q_t[t][mine, :].astype(BF)
                kt = jnp.concatenate([k_t[t][prv, :], k_t[t][cur, :]], axis=0).astype(BF)
                vt = jnp.concatenate([v_t[t][prv, :], v_t[t][cur, :]], axis=0).astype(BF)
                o_pair = jnp.zeros((CHUNK, 128), F32)
                l_pair = jnp.zeros((CHUNK, 128), F32)
                for e in range(2):
                    lm = _lane_mask(e)
                    s = _dot_nt(jnp.where(lm, qt, jnp.zeros_like(qt)), kt) * scale
                    s = jnp.where(mask, s, NEG)
                    m = jnp.max(s, axis=-1, keepdims=True)
                    p = jnp.exp(s - m)
                    den = jnp.sum(p, axis=-1, keepdims=True)
                    o = _dot(p.astype(BF), vt) / den
                    o_pair = jnp.where(lm, o, o_pair)
                    l_pair = jnp.where(lm, m + jnp.log(den), l_pair)
                o_t[t][mine, :] = o_pair
                l_t[t][mine, :] = l_pair


    out_spec = pl.BlockSpec(q_like(0).block_shape, lambda b, n, r: (b * nb + n, 0))
    sh = jax.ShapeDtypeStruct((T, 128), F32)
    outs = _call(
        body, name=f"attn_fwd_d{dil}", grid=(B, nb, last),
        in_specs=[q_like(t) for t in range(3)] + [k_like(3 + t) for t in range(3)] + [k_like(6 + t) for t in range(3)],
        out_specs=[out_spec] * 6, out_shape=[sh] * 6, sem=("parallel", "arbitrary", "arbitrary"))(*([qkv] * 9))
    return list(outs[0:3]), list(outs[3:6])


def _attn_combine(branches):
    T = branches[0][0][0].shape[0]
    tm = _tile(T, 512)

    def body(*refs):
        y_ref, l_ref = refs[-2:]
        for t in range(3):
            o = [refs[6 * i + t][...] for i in range(3)]
            a, b, c = [refs[6 * i + 3 + t][...] for i in range(3)]
            m = jnp.maximum(jnp.maximum(a, b), c)
            ea, eb, ec = jnp.exp(a - m), jnp.exp(b - m), jnp.exp(c - m)
            z = ea + eb + ec
            y_ref[:, 128 * t:128 * (t + 1)] = (ea * o[0] + eb * o[1] + ec * o[2]) / z
            l_ref[:, 128 * t:128 * (t + 1)] = m + jnp.log(z)

    tile = pl.BlockSpec((tm, 128), lambda i: (i, 0))
    row = pl.BlockSpec((tm, ATT_W), lambda i: (i, 0))
    sh = jax.ShapeDtypeStruct((T, ATT_W), F32)
    flat = [a for o_t, l_t in branches for a in (*o_t, *l_t)]
    return _call(body, name="attn_combine", grid=(T // tm,), in_specs=[tile] * 18, out_specs=[row, row],
                 out_shape=[sh, sh], sem=("parallel",))(*flat)


def _attn_bwd(qkv, do, out, lse, dil):
    T = qkv.shape[0]
    B, nb, last, per_step, q_like, k_like = _attn_specs(T, dil)
    scale = HEAD ** -0.5

    def body(*refs):
        q_t, k_t, v_t = refs[0:3], refs[3:6], refs[6:9]
        do_t, out_t, lse_t = refs[9:12], refs[12:15], refs[15:18]
        dq_t, dk_t, dv_t = refs[18:21], refs[21:24], refs[24:27]
        @pl.when((pl.program_id(1) == 0) & (pl.program_id(2) == 0))
        def _():
            for t in range(3):
                dk_t[t][...] = jnp.zeros_like(dk_t[t])
                dv_t[t][...] = jnp.zeros_like(dv_t[t])

        for u in range(per_step):
            r, n, mine = _attn_step(u, dil, per_step)
            mask = _band_mask(n)
            cur, prv = _sub_rows(r, n, dil), _sub_rows(r, jnp.maximum(n - 1, 0), dil)
            for t in range(3):
                qt = q_t[t][mine, :].astype(BF)
                kt = jnp.concatenate([k_t[t][prv, :], k_t[t][cur, :]], axis=0).astype(BF)
                vt = jnp.concatenate([v_t[t][prv, :], v_t[t][cur, :]], axis=0).astype(BF)
                do_ = do_t[t][mine, :]
                dlt = do_ * out_t[t][mine, :]
                ls = lse_t[t][mine, :]
                dq_pair = jnp.zeros((CHUNK, 128), F32)
                dk_acc = jnp.zeros((2 * CHUNK, 128), F32)
                dv_acc = jnp.zeros((2 * CHUNK, 128), F32)
                for e in range(2):
                    lm = _lane_mask(e)
                    qm = jnp.where(lm, qt, jnp.zeros_like(qt))
                    s = _dot_nt(qm, kt) * scale
                    p = jnp.exp(jnp.where(mask, s - ls[:, HEAD * e:HEAD * e + 1], NEG))
                    dom = jnp.where(lm, do_, 0.0).astype(BF)
                    dv_acc += _dot_tn(p.astype(BF), dom)
                    dp = _dot_nt(dom, vt)
                    delta = jnp.sum(jnp.where(lm, dlt, 0.0), axis=-1, keepdims=True)
                    ds = (p * (dp - delta) * scale).astype(BF)
                    dq_pair += jnp.where(lm, _dot(ds, kt), 0.0)
                    dk_acc += _dot_tn(ds, qm)
                dq_t[t][mine, :] = dq_pair
                dk_t[t][cur, :] = dk_t[t][cur, :] + dk_acc[CHUNK:]
                dk_t[t][prv, :] = dk_t[t][prv, :] + dk_acc[:CHUNK]
                dv_t[t][cur, :] = dv_t[t][cur, :] + dv_acc[CHUNK:]
                dv_t[t][prv, :] = dv_t[t][prv, :] + dv_acc[:CHUNK]

    q_out = pl.BlockSpec(q_like(0).block_shape, lambda b, n, r: (b * nb + n, 0))
    k_out = pl.BlockSpec((SEQ, 128), lambda b, n, r: (b, 0))
    sh = jax.ShapeDtypeStruct((T, 128), F32)
    tiles = lambda: [q_like(t) for t in range(3)]
    return list(_call(
        body, name=f"attn_bwd_d{dil}", grid=(B, nb, last),
        in_specs=tiles() + [k_like(3 + t) for t in range(3)] + [k_like(6 + t) for t in range(3)]
        + tiles() + tiles() + tiles(),
        out_specs=[q_out] * 3 + [k_out] * 6, out_shape=[sh] * 9,
        sem=("parallel", "arbitrary", "arbitrary"), vmem=ATTN_BWD_VMEM)(*([qkv] * 9 + [do] * 3 + [out] * 3 + [lse] * 3)))


def _sum_branches(parts):
    T = parts[0][0].shape[0]
    tm = _tile(T, 512)

    def body(*refs):
        o_ref = refs[-1]
        for c in range(9):
            acc = refs[c][...] + refs[9 + c][...] + refs[18 + c][...]
            o_ref[:, 128 * c:128 * (c + 1)] = acc.astype(BF)

    tile = pl.BlockSpec((tm, 128), lambda i: (i, 0))
    flat = [a for br in parts for a in br]
    return _call(body, name="attn_sum_branches", grid=(T // tm,), in_specs=[tile] * 27,
                 out_specs=pl.BlockSpec((tm, W_QKV), lambda i: (i, 0)),
                 out_shape=jax.ShapeDtypeStruct((T, W_QKV), BF), sem=("parallel",))(*flat)


def _silu(x):
    return x * _sigmoid(x)


def _dsilu(x):
    s = _sigmoid(x)
    return s * (1.0 + x * (1.0 - s))


def _log1p(u):
    return jnp.where(u < 0.01, u * (1.0 - u * (0.5 - u * (1.0 / 3.0))), jnp.log(1.0 + u))


def _softplus(x):
    return jnp.maximum(x, 0.0) + _log1p(jnp.exp(-jnp.abs(x)))


def _cumsum_rows(x, reverse=False):
    n = x.shape[0]
    rows = lax.broadcasted_iota(jnp.int32, x.shape, 0)
    k = 1
    while k < n:
        if reverse:
            x = x + jnp.where(rows < n - k, pltpu.roll(x, n - k, 0), 0.0)
        else:
            x = x + jnp.where(rows >= k, pltpu.roll(x, k, 0), 0.0)
        k *= 2
    return x


def _tri():
    r = lax.broadcasted_iota(jnp.int32, (CHUNK, CHUNK), 0)
    c = lax.broadcasted_iota(jnp.int32, (CHUNK, CHUNK), 1)
    return r >= c


def _row_mask(e):
    return (lax.broadcasted_iota(jnp.int32, (128, 1), 0) // HEAD) == e


def _first_lane(e):
    return lax.broadcasted_iota(jnp.int32, (1, 128), 1) == HEAD * e


def _ssd_pre(x_ref, halo_ref, first, cw_ref, cb_ref, dtb_ref, al_ref, ext):
    row = x_ref[...]
    z = row[:, SSD_CONV_DIM:SSD_CONV_DIM + SSD_W]
    u = row[:, SSD_CONV_DIM + SSD_W:] + dtb_ref[...]
    ext[0:8, :] = jnp.where(first, 0.0, halo_ref[:, 0:SSD_CONV_DIM])
    ext[8:8 + CHUNK, :] = row[:, 0:SSD_CONV_DIM]
    xc = cb_ref[...]
    for j in range(4):
        xc = xc + cw_ref[j:j + 1, :] * ext[pl.ds(5 + j, CHUNK), :]
    xa = _silu(xc)
    dt = _softplus(u)
    a = dt * (-jnp.exp(al_ref[...]))
    A = _cumsum_rows(a)
    return dict(z=z, u=u, xc=xc, xs=xa[:, 0:SSD_W], Bm=xa[:, SSD_W:SSD_W + 256], Cm=xa[:, SSD_W + 256:],
                dt=dt, a=a, A=A, AT=A.T, eA=jnp.exp(A), wdec=jnp.exp(A[CHUNK - 1:CHUNK, :] - A),
                dtot=jnp.exp(A[CHUNK - 1:CHUNK, :]))


def _ssd_y(p, hp_ref, dskip):
    tri = _tri()
    X = p["xs"] * p["dt"]
    Bb = [p["Bm"][:, 128 * g:128 * (g + 1)].astype(BF) for g in range(2)]
    Cb = [p["Cm"][:, 128 * g:128 * (g + 1)].astype(BF) for g in range(2)]
    CB = [_dot_nt(Cb[g], Bb[g]) for g in range(2)]
    tiles = []
    for t in range(3):
        sl = slice(128 * t, 128 * (t + 1))
        hpb = hp_ref[sl, :].astype(BF)
        acc = jnp.zeros((CHUNK, 128), F32)
        for e in range(2):
            h = 2 * t + e
            g, col = h // 3, HEAD * h
            lm = _lane_mask(e)
            L = jnp.exp(jnp.where(tri, p["A"][:, col:col + 1] - p["AT"][col:col + 1, :], NEG))
            yd = _dot((CB[g] * L).astype(BF), jnp.where(lm, X[:, sl], 0.0).astype(BF))
            yo = _dot_nt(Cb[g], hpb) * p["eA"][:, sl]
            acc = acc + jnp.where(lm, yd + yo, 0.0)
        tiles.append(acc)
    return jnp.concatenate(tiles, axis=1) + dskip * p["xs"], X, Bb, Cb, CB


def _group_stats(v):
    g0 = lax.broadcasted_iota(jnp.int32, (1, SSD_W), 1) < SSD_W // 2
    m0 = jnp.sum(jnp.where(g0, v, 0.0), axis=-1, keepdims=True) * (2.0 / SSD_W)
    m1 = jnp.sum(jnp.where(g0, 0.0, v), axis=-1, keepdims=True) * (2.0 / SSD_W)
    return jnp.where(g0, m0, m1)


def _ssd_specs(T, rev):
    B = T // SEQ
    chunk = (lambda c: N_CHUNK - 1 - c) if rev else (lambda c: c)
    row = pl.BlockSpec((B, CHUNK, W_SSD), lambda c: (0, chunk(c), 0))
    halo = pl.BlockSpec((B, 8, W_SSD), lambda c: (0, jnp.maximum(chunk(c) * (CHUNK // 8) - 1, 0), 0))
    hp = pl.BlockSpec((B, None, SSD_W, SSD_STATE), lambda c: (0, chunk(c), 0, 0))
    y = pl.BlockSpec((B, CHUNK, SSD_W), lambda c: (0, chunk(c), 0))
    const = lambda r, w: pl.BlockSpec((r, w), lambda c: (0, 0))
    params = [const(4, SSD_CONV_DIM), const(1, SSD_CONV_DIM)] + [const(1, SSD_W)] * 4
    return B, row, halo, hp, y, const, params


def _ssd_fwd(sin, conv_w, conv_b, dtb, alog, dskip, norm_g):
    T = sin.shape[0]
    B, row, halo, hp, y, const, params = _ssd_specs(T, False)

    def body(xs_ref, halos_ref, cw_ref, cb_ref, dtb_ref, al_ref, dk_ref, ng_ref, ys_ref, hps_ref, exts, hsts):
        @pl.when(pl.program_id(0) == 0)
        def _():
            hsts[...] = jnp.zeros_like(hsts)

        for b in range(B):
            one(xs_ref.at[b], halos_ref.at[b], cw_ref, cb_ref, dtb_ref, al_ref, dk_ref, ng_ref, ys_ref.at[b],
                hps_ref.at[b], exts.at[b], hsts.at[b])

    def one(x_ref, halo_ref, cw_ref, cb_ref, dtb_ref, al_ref, dk_ref, ng_ref, y_ref, hp_ref, ext, hst):
        c = pl.program_id(0)
        p = _ssd_pre(x_ref, halo_ref, c == 0, cw_ref, cb_ref, dtb_ref, al_ref, ext)
        yv, X, Bb, Cb, CB = _ssd_y(p, hst, dk_ref[...])
        hp_ref[...] = hst[...]
        for t in range(3):
            sl = slice(128 * t, 128 * (t + 1))
            old = hst[sl, :]
            new = old
            for e in range(2):
                h = 2 * t + e
                g, col = h // 3, HEAD * h
                st = _dot_tn(jnp.where(_lane_mask(e), X[:, sl] * p["wdec"][:, sl], 0.0).astype(BF), Bb[g])
                new = jnp.where(_row_mask(e), old * p["dtot"][:, col:col + 1] + st, new)
            hst[sl, :] = new
        y2 = yv * _silu(p["z"])
        r = lax.rsqrt(_group_stats(y2 * y2) + RMS_EPS)
        y_ref[...] = y2 * r * ng_ref[...]

    sin3 = sin.reshape(B, SEQ, W_SSD)
    yo, hprev = _call(
        body, name="ssd_fwd", grid=(N_CHUNK,), in_specs=[row, halo] + params, out_specs=[y, hp],
        out_shape=[jax.ShapeDtypeStruct((B, SEQ, SSD_W), F32),
                   jax.ShapeDtypeStruct((B, N_CHUNK, SSD_W, SSD_STATE), F32)],
        scratch=[pltpu.VMEM((B, 8 + CHUNK, SSD_CONV_DIM), F32), pltpu.VMEM((B, SSD_W, SSD_STATE), F32)],
        sem=("arbitrary",))(sin3, sin3, conv_w, conv_b, dtb, alog, dskip, norm_g)
    return yo.reshape(T, SSD_W), hprev


def _ssd_bwd(sin, hprev, dy3, conv_w, conv_b, dtb, alog, dskip, norm_g):
    T = sin.shape[0]
    B, row, halo, hp, y, const, params = _ssd_specs(T, True)

    def body(xs_ref, halos_ref, hps_ref, dys_ref, cw_ref, cb_ref, dtb_ref, al_ref, dk_ref, ng_ref,
             dxs_ref, dcw_ref, dcb_ref, dvec_ref, exts, ext2s, dhs):
        @pl.when(pl.program_id(0) == 0)
        def _():
            dcw_ref[...] = jnp.zeros_like(dcw_ref)
            dcb_ref[...] = jnp.zeros_like(dcb_ref)
            dvec_ref[...] = jnp.zeros_like(dvec_ref)
            dhs[...] = jnp.zeros_like(dhs)
            ext2s[:, CHUNK:CHUNK + 8, :] = jnp.zeros((B, 8, SSD_CONV_DIM), F32)

        for b in range(B):
            one(xs_ref.at[b], halos_ref.at[b], hps_ref.at[b], dys_ref.at[b], cw_ref, cb_ref, dtb_ref, al_ref, dk_ref,
                ng_ref, dxs_ref.at[b], dcw_ref, dcb_ref, dvec_ref, exts.at[b], ext2s.at[b], dhs.at[b])

    def one(x_ref, halo_ref, hp_ref, dy_ref, cw_ref, cb_ref, dtb_ref, al_ref, dk_ref, ng_ref,
            dx_ref, dcw_ref, dcb_ref, dvec_ref, ext, ext2, dh):
        c = pl.program_id(0)
        p = _ssd_pre(x_ref, halo_ref, c == N_CHUNK - 1, cw_ref, cb_ref, dtb_ref, al_ref, ext)
        dskip_ = dk_ref[...]
        yv, X, Bb, Cb, CB = _ssd_y(p, hp_ref, dskip_)
        xs, z, A, AT = p["xs"], p["z"], p["A"], p["AT"]

        sz = _silu(z)
        y2 = yv * sz
        r = lax.rsqrt(_group_stats(y2 * y2) + RMS_EPS)
        dy3_ = dy_ref[...]
        uu = dy3_ * ng_ref[...]
        dy2 = r * (uu - y2 * (r * r * _group_stats(uu * y2)))
        dy = dy2 * sz
        dz = dy2 * yv * _dsilu(z)

        tri = _tri()
        rows = lax.broadcasted_iota(jnp.int32, (CHUNK, 1), 0)
        dG = [jnp.zeros((CHUNK, CHUNK), F32) for _ in range(2)]
        dB = [jnp.zeros((CHUNK, SSD_STATE), F32) for _ in range(2)]
        dC = [jnp.zeros((CHUNK, SSD_STATE), F32) for _ in range(2)]
        dX_t, dA_t, ddtx_t = [], [], []
        for t in range(3):
            sl = slice(128 * t, 128 * (t + 1))
            hp_t = hp_ref[sl, :]
            hpb = hp_t.astype(BF)
            dhc = dh[sl, :]
            dh_new = jnp.zeros((128, SSD_STATE), F32)
            dX = jnp.zeros((CHUNK, 128), F32)
            dA = jnp.zeros((CHUNK, 128), F32)
            ddtx = jnp.zeros((CHUNK, 128), F32)
            for e in range(2):
                h = 2 * t + e
                g, col = h // 3, HEAD * h
                lm, rm, fl = _lane_mask(e), _row_mask(e), _first_lane(e)
                L = jnp.exp(jnp.where(tri, A[:, col:col + 1] - AT[col:col + 1, :], NEG))
                Mf = CB[g] * L
                Xm = jnp.where(lm, X[:, sl], 0.0)
                Xmb = Xm.astype(BF)
                dyh = jnp.where(lm, dy[:, sl], 0.0)
                dyb = dyh.astype(BF)
                dXh = _dot_tn(Mf.astype(BF), dyb)
                dM = jnp.where(tri, _dot_nt(dyb, Xmb), 0.0)
                Wm = dM * Mf
                dAc = jnp.sum(Wm, axis=-1, keepdims=True) - jnp.sum(Wm.T, axis=-1, keepdims=True)
                dG[g] = dG[g] + dM * L
                eAt = p["eA"][:, sl]
                yo = _dot_nt(Cb[g], hpb)
                dyo = (dyh * eAt).astype(BF)
                dC[g] = dC[g] + _dot(dyo, hpb)
                dh_new = dh_new + _dot_tn(dyo, Cb[g])
                dAc = dAc + jnp.sum(dyh * yo * eAt, axis=-1, keepdims=True)
                dHn = jnp.where(rm, dhc, 0.0)
                dHnb = dHn.astype(BF)
                dec = p["dtot"][:, col:col + 1]
                dh_new = dh_new + dec * dHn
                Z = _dot_nt(Bb[g], dHnb)
                wt = p["wdec"][:, sl]
                xi = jnp.sum(Xm * Z, axis=-1, keepdims=True) * p["wdec"][:, col:col + 1]
                dXh = dXh + wt * Z
                dB[g] = dB[g] + _dot(jnp.where(lm, X[:, sl] * wt, 0.0).astype(BF), dHnb)
                dAtot = jnp.sum(xi, axis=0, keepdims=True) + dec * jnp.sum(
                    jnp.sum(dHn * hp_t, axis=-1, keepdims=True), axis=0, keepdims=True)
                dAc = dAc - xi + jnp.where(rows == CHUNK - 1, dAtot, 0.0)
                dA = dA + jnp.where(fl, dAc, 0.0)
                dX = dX + dXh
                ddtx = ddtx + jnp.where(fl, jnp.sum(dXh * xs[:, sl], axis=-1, keepdims=True), 0.0)
            dh[sl, :] = dh_new
            dX_t.append(dX)
            dA_t.append(dA)
            ddtx_t.append(ddtx)
        for g in range(2):
            dGb = dG[g].astype(BF)
            dC[g] = dC[g] + _dot(dGb, Bb[g])
            dB[g] = dB[g] + _dot_tn(dGb, Cb[g])
        dXf = jnp.concatenate(dX_t, axis=1)
        da = _cumsum_rows(jnp.concatenate(dA_t, axis=1), reverse=True)
        ddt = da * (-jnp.exp(al_ref[...])) + jnp.concatenate(ddtx_t, axis=1)
        du = ddt * _sigmoid(p["u"])
        dxs = dXf * p["dt"] + dskip_ * dy
        dxc = jnp.concatenate([dxs, dB[0], dB[1], dC[0], dC[1]], axis=1) * _dsilu(p["xc"])
        ext2[0:CHUNK, :] = dxc
        dxbc = jnp.zeros((CHUNK, SSD_CONV_DIM), F32)
        for j in range(4):
            dxbc = dxbc + cw_ref[j:j + 1, :] * ext2[pl.ds(3 - j, CHUNK), :]
            dcw_ref[j:j + 1, :] += jnp.sum(dxc * ext[pl.ds(5 + j, CHUNK), :], axis=0, keepdims=True)
        ext2[CHUNK:CHUNK + 8, :] = dxc[0:8, :]
        dcb_ref[...] += jnp.sum(dxc, axis=0, keepdims=True)
        dvec_ref[0:1, :] += jnp.sum(du, axis=0, keepdims=True)
        dvec_ref[1:2, :] += jnp.sum(da * p["a"], axis=0, keepdims=True)
        dvec_ref[2:3, :] += jnp.sum(dy * xs, axis=0, keepdims=True)
        dvec_ref[3:4, :] += jnp.sum(dy3_ * y2 * r, axis=0, keepdims=True)
        dx_ref[...] = jnp.concatenate([dxbc, dz, du], axis=1).astype(BF)

    sin3 = sin.reshape(B, SEQ, W_SSD)
    out = _call(body, name="ssd_bwd", grid=(N_CHUNK,), in_specs=[row, halo, hp, y] + params,
                out_specs=[row, const(4, SSD_CONV_DIM), const(1, SSD_CONV_DIM), const(8, SSD_W)],
                out_shape=[jax.ShapeDtypeStruct((B, SEQ, W_SSD), BF), jax.ShapeDtypeStruct((4, SSD_CONV_DIM), F32),
                           jax.ShapeDtypeStruct((1, SSD_CONV_DIM), F32), jax.ShapeDtypeStruct((8, SSD_W), F32)],
                scratch=[pltpu.VMEM((B, 8 + CHUNK, SSD_CONV_DIM), F32), pltpu.VMEM((B, 8 + CHUNK, SSD_CONV_DIM), F32),
                         pltpu.VMEM((B, SSD_W, SSD_STATE), F32)],
                sem=("arbitrary",))(sin3, sin3, hprev, dy3.reshape(B, SEQ, SSD_W), conv_w, conv_b, dtb, alog, dskip,
                                    norm_g)
    return (out[0].reshape(T, W_SSD),) + tuple(out[1:])


def _sgu_core(uv_ref, g_ref, b_ref, w_ref, bias_ref):
    x = uv_ref[...]
    cdf = 0.5 * (1.0 + lax.erf(x * (2.0 ** -0.5)))
    ge = x * cdf
    dge = cdf + x * jnp.exp(-0.5 * x * x) * ((2.0 * math.pi) ** -0.5)
    u, v = ge[:, 0:SGU_W], ge[:, SGU_W:]
    vc = v - jnp.mean(v, axis=-1, keepdims=True)
    rstd = lax.rsqrt(jnp.mean(vc * vc, axis=-1, keepdims=True) + LN_EPS)
    vhat = vc * rstd
    vn = vhat * g_ref[...] + b_ref[...]
    tri = _tri()
    wc = [jnp.where(tri, w_ref[gi], 0.0).astype(BF) for gi in range(4)]
    vm = [jnp.where(_lane_mask(gi % 2), vn[:, 128 * (gi // 2):128 * (gi // 2 + 1)], 0.0).astype(BF) for gi in range(4)]
    mixed = jnp.concatenate([_dot(wc[2 * t], vm[2 * t]) + _dot(wc[2 * t + 1], vm[2 * t + 1]) for t in range(2)],
                            axis=1) + bias_ref[...]
    return dict(dge=dge, u=u, rstd=rstd, vhat=vhat, wc=wc, vm=vm, mixed=mixed)


def _sgu_specs():
    vec = pl.BlockSpec((1, SGU_W), lambda i: (0, 0))
    return [pl.BlockSpec((CHUNK, W_UV), lambda i: (i, 0)), vec, vec,
            pl.BlockSpec((4, CHUNK, CHUNK), lambda i: (0, 0, 0)), pl.BlockSpec((CHUNK, SGU_W), lambda i: (0, 0))]


def _sgu_fwd(uv, ln_g, ln_b, w, bias):
    T = uv.shape[0]

    def body(uv_ref, g_ref, b_ref, w_ref, bias_ref, y_ref):
        s = _sgu_core(uv_ref, g_ref, b_ref, w_ref, bias_ref)
        y_ref[...] = s["u"] * s["mixed"]

    return _call(body, name="sgu_fwd", grid=(T // CHUNK,), in_specs=_sgu_specs(),
                 out_specs=pl.BlockSpec((CHUNK, SGU_W), lambda i: (i, 0)),
                 out_shape=jax.ShapeDtypeStruct((T, SGU_W), F32), sem=("parallel",))(uv, ln_g, ln_b, w, bias)


def _sgu_bwd(uv, dy, ln_g, ln_b, w, bias):
    T = uv.shape[0]

    def body(uv_ref, dy_ref, g_ref, b_ref, w_ref, bias_ref, dx_ref, dw_ref, dbias_ref, dln_ref):
        @pl.when(pl.program_id(0) == 0)
        def _():
            dw_ref[...] = jnp.zeros_like(dw_ref)
            dbias_ref[...] = jnp.zeros_like(dbias_ref)
            dln_ref[...] = jnp.zeros_like(dln_ref)

        s = _sgu_core(uv_ref, g_ref, b_ref, w_ref, bias_ref)
        dy_ = dy_ref[...]
        du = dy_ * s["mixed"]
        dmix = dy_ * s["u"]
        dbias_ref[...] += dmix
        tri = _tri()
        dvn_t = []
        for t in range(2):
            acc = jnp.zeros((CHUNK, 128), F32)
            for e in range(2):
                gi = 2 * t + e
                dmg = jnp.where(_lane_mask(e), dmix[:, 128 * t:128 * (t + 1)], 0.0).astype(BF)
                acc = acc + _dot_tn(s["wc"][gi], dmg)
                dw_ref[gi] += jnp.where(tri, _dot_nt(dmg, s["vm"][gi]), 0.0)
            dvn_t.append(acc)
        dvn = jnp.concatenate(dvn_t, axis=1)
        dln_ref[0:1, :] += jnp.sum(dvn * s["vhat"], axis=0, keepdims=True)
        dln_ref[1:2, :] += jnp.sum(dvn, axis=0, keepdims=True)
        dvh = dvn * g_ref[...]
        dv = s["rstd"] * (dvh - jnp.mean(dvh, axis=-1, keepdims=True)
                          - s["vhat"] * jnp.mean(dvh * s["vhat"], axis=-1, keepdims=True))
        dx_ref[...] = (jnp.concatenate([du, dv], axis=1) * s["dge"]).astype(BF)

    ins = _sgu_specs()
    return _call(body, name="sgu_bwd", grid=(T // CHUNK,),
                 in_specs=[ins[0], pl.BlockSpec((CHUNK, SGU_W), lambda i: (i, 0))] + ins[1:],
                 out_specs=[pl.BlockSpec((CHUNK, W_UV), lambda i: (i, 0)),
                            pl.BlockSpec((4, CHUNK, CHUNK), lambda i: (0, 0, 0)),
                            pl.BlockSpec((CHUNK, SGU_W), lambda i: (0, 0)), pl.BlockSpec((8, SGU_W), lambda i: (0, 0))],
                 out_shape=[jax.ShapeDtypeStruct((T, W_UV), BF), jax.ShapeDtypeStruct((4, CHUNK, CHUNK), F32),
                            jax.ShapeDtypeStruct((CHUNK, SGU_W), F32), jax.ShapeDtypeStruct((8, SGU_W), F32)],
                 sem=("arbitrary",))(uv, dy, ln_g, ln_b, w, bias)


def _adamw(w, g, m, v):
    R, C = w.shape
    tr = R

    def body(w_ref, g_ref, m_ref, v_ref, d_ref, nm_ref, nv_ref):
        g_ = g_ref[...]
        m2 = ADAM_B1 * m_ref[...] + (1.0 - ADAM_B1) * g_
        v2 = ADAM_B2 * v_ref[...] + (1.0 - ADAM_B2) * (g_ * g_)
        m_hat = m2 / (1.0 - ADAM_B1 ** ADAM_STEP)
        v_hat = v2 / (1.0 - ADAM_B2 ** ADAM_STEP)
        d_ref[...] = -ADAM_LR * (m_hat / (jnp.sqrt(v_hat) + ADAM_EPS) + ADAM_WD * w_ref[...])
        nm_ref[...] = m2
        nv_ref[...] = v2

    blk = pl.BlockSpec((tr, C), lambda i: (i, 0))
    sh = jax.ShapeDtypeStruct((R, C), F32)
    return _call(body, name="adamw", grid=(R // tr,), in_specs=[blk] * 4, out_specs=[blk] * 3,
                 out_shape=[sh] * 3, sem=("parallel",))(w, g, m, v)


def _adamw_pair(w, g0, g1, m, v, dep):
    L, R, C = w.shape
    tr = max(t for t in range(8, R + 1, 8) if R % t == 0 and t * C * 4 <= 3 * 2 ** 19)

    def body(w_ref, g0_ref, g1_ref, m_ref, v_ref, dep_ref, d_ref, nm_ref, nv_ref, og_ref):
        g_ = jnp.where(pl.program_id(0) == 0, g0_ref[...], g1_ref[...])
        m2 = ADAM_B1 * m_ref[...] + (1.0 - ADAM_B1) * g_
        v2 = ADAM_B2 * v_ref[...] + (1.0 - ADAM_B2) * (g_ * g_)
        m_hat = m2 / (1.0 - ADAM_B1 ** ADAM_STEP)
        v_hat = v2 / (1.0 - ADAM_B2 ** ADAM_STEP)
        d_ref[...] = -ADAM_LR * (m_hat / (jnp.sqrt(v_hat) + ADAM_EPS) + ADAM_WD * w_ref[...])
        nm_ref[...] = m2
        nv_ref[...] = v2
        og_ref[...] = g_

    lay = pl.BlockSpec((None, tr, C), lambda l, i: (l, i, 0))
    one = lambda k: pl.BlockSpec((tr, C), lambda l, i: (jnp.where(l == k, i, 0), 0))
    return _call(body, name="adamw_pair", grid=(L, R // tr),
                 in_specs=[lay, one(0), one(1), lay, lay, pl.BlockSpec((8, 128), lambda l, i: (0, 0))],
                 out_specs=[lay] * 4,
                 out_shape=[jax.ShapeDtypeStruct((L, R, C), F32)] * 4,
                 sem=("parallel", "parallel"))(w, g0, g1, m, v, dep)


def _row_steps(rows):
    return 2 if rows % 32 == 0 else 1


def _pair_add(gbufs, rsibs, c):
    n = len(gbufs)
    steps = min(_row_steps(g.shape[2]) for g in gbufs)

    def body(c_ref, *refs):
        for a_ref, b_ref, o_ref in zip(refs[:n], refs[n:2 * n], refs[2 * n:]):
            o_ref[...] = (a_ref[...] + b_ref[...]).astype(BF)

    def specs(g):
        tr, C = g.shape[2] // steps, g.shape[3]
        return (pl.BlockSpec((None, None, tr, C), lambda j, i, c_ref: (j, c_ref[0], i, 0)),
                pl.BlockSpec((None, tr, C), lambda j, i, c_ref: (j, i, 0)))

    return list(pl.pallas_call(
        body, name="rs_pair_add",
        grid_spec=pltpu.PrefetchScalarGridSpec(
            num_scalar_prefetch=1, grid=(4, steps),
            in_specs=[specs(g)[0] for g in gbufs] + [specs(g)[1] for g in gbufs],
            out_specs=[specs(g)[1] for g in gbufs]),
        out_shape=[jax.ShapeDtypeStruct((4,) + g.shape[2:], BF) for g in gbufs],
        compiler_params=pltpu.CompilerParams(dimension_semantics=("parallel", "parallel")),
    )(jnp.reshape(c, (1,)).astype(jnp.int32), *gbufs, *rsibs))


def _chip_sum(pairs, recvs, me, c):
    n = len(pairs)
    steps = min(_row_steps(p.shape[1]) for p in pairs)

    def body(s_ref, *refs):
        for own_ref, p_ref, o_ref in zip(refs[:n], refs[n:2 * n], refs[2 * n:]):
            p = [jnp.where(s_ref[0] == j, own_ref[...], p_ref[j]).astype(F32) for j in range(4)]
            o_ref[...] = ((p[0] + p[1]) + p[2]) + p[3]

    def specs(p):
        tr, C = p.shape[1] // steps, p.shape[2]
        return (pl.BlockSpec((None, tr, C), lambda i, s: (s[0], i, 0)), pl.BlockSpec((4, tr, C), lambda i, s: (0, i, 0)),
                pl.BlockSpec((None, tr, C), lambda i, s: (s[1], i, 0)))

    return list(pl.pallas_call(
        body, name="rs_chip_sum",
        grid_spec=pltpu.PrefetchScalarGridSpec(
            num_scalar_prefetch=1, grid=(steps,),
            in_specs=[specs(p)[0] for p in pairs] + [specs(p)[1] for p in pairs],
            out_specs=[specs(p)[2] for p in pairs]),
        out_shape=[jax.ShapeDtypeStruct((2,) + p.shape[1:], F32) for p in pairs],
        compiler_params=pltpu.CompilerParams(dimension_semantics=("parallel",)),
    )(jnp.stack([me, c]).astype(jnp.int32), *pairs, *recvs))


MESH = pl.DeviceIdType.MESH
ANY = pl.BlockSpec(memory_space=pl.ANY)


def _place():
    x, y, c = lax.axis_index("x"), lax.axis_index("y"), lax.axis_index("c")
    return x, y, c, [(1 - x, y), (x, 1 - y), (1 - x, 1 - y)]


HBM = pl.BlockSpec(memory_space=pltpu.HBM)
SEM = pl.BlockSpec(memory_space=pltpu.SEMAPHORE)
EFFECT = pltpu.SideEffectType.DATAFLOW_SIDE_EFFECTING


class _Split:
    def __init__(self, tag, arrays, copies, n_copies, after=()):
        self.tag, self.copies, k = tag, copies, len(arrays)

        def body(*refs):
            sems = k + len(after)
            for cp in copies(refs[:k], refs[sems], refs[sems + 1]):
                cp.start()
            refs[-1][...] = jnp.zeros_like(refs[-1])

        out = pl.pallas_call(
            body, name=tag + "_start",
            out_shape=(pltpu.SemaphoreType.DMA((n_copies,)), pltpu.SemaphoreType.DMA((n_copies,)),
                       *[pltpu.HBM(a.shape, a.dtype) for a in arrays], jax.ShapeDtypeStruct((8, 128), F32)),
            in_specs=[HBM] * k + [ANY] * len(after),
            out_specs=(SEM, SEM, *[HBM] * k, pl.BlockSpec(memory_space=pltpu.VMEM)),
            input_output_aliases={i: 2 + i for i in range(k)},
            compiler_params=pltpu.CompilerParams(has_side_effects=EFFECT),
        )(*[pltpu.with_memory_space_constraint(a, pltpu.HBM) for a in arrays], *after)
        self.send, self.recv, self.arrays, self.token_array = out[0], out[1], list(out[2:2 + k]), out[-1]
        self.token = self.token_array[0, 0]

    def wait(self, after):
        k, copies = len(self.arrays), self.copies
        after = list(after) if isinstance(after, (list, tuple)) else [after]

        def body(*refs):
            for cp in copies(refs[:k], refs[k], refs[k + 1]):
                cp.wait_send()
                cp.wait_recv()

        return list(pl.pallas_call(
            body, name=self.tag + "_wait", out_shape=tuple(pltpu.HBM(a.shape, a.dtype) for a in self.arrays),
            in_specs=[HBM] * k + [SEM, SEM] + [ANY] * len(after), out_specs=tuple([HBM] * k),
            input_output_aliases={i: i for i in range(k)},
            compiler_params=pltpu.CompilerParams(has_side_effects=EFFECT),
        )(*self.arrays, self.send, self.recv, *after))


def _landing_zones(arrs):
    me = 2 * lax.axis_index("x") + lax.axis_index("y")
    return [lax.dynamic_update_index_in_dim(lax.empty((4,) + a.shape, a.dtype), a, me, 0) for a in arrs]


def _gather_start(arrs, lands, tag, after=()):
    n = len(arrs)

    def copies(refs, send, recv):
        x, y, c, chips = _place()
        return [pltpu.make_async_remote_copy(
            src_ref=refs[k], dst_ref=refs[n + k].at[2 * x + y], send_sem=send.at[3 * k + r],
            recv_sem=recv.at[3 * k + r], device_id=(px, py, c), device_id_type=MESH)
            for k in range(n) for r, (px, py) in enumerate(chips)]

    return _Split("gather_" + tag, list(arrs) + lands, copies, 3 * n, after)


def _gather_halves_start(arrs, tag):
    n = len(arrs)
    lands = _landing_zones(arrs)

    def copies(refs, send, recv):
        x, y, c, chips = _place()
        return [pltpu.make_async_remote_copy(
            src_ref=refs[k].at[c], dst_ref=refs[n + k].at[2 * x + y, c], send_sem=send.at[3 * k + r],
            recv_sem=recv.at[3 * k + r], device_id=(px, py, c), device_id_type=MESH)
            for k in range(n) for r, (px, py) in enumerate(chips)]

    return _Split("gather_" + tag, list(arrs) + lands, copies, 3 * n)


def _gather_halves_finish(lands, tag):
    n = len(lands)

    def copies(refs, send, recv):
        x, y, c, chips = _place()
        return [pltpu.make_async_remote_copy(
            src_ref=refs[k].at[2 * px + py, c], dst_ref=refs[k].at[2 * px + py, c], send_sem=send.at[3 * k + r],
            recv_sem=recv.at[3 * k + r], device_id=(x, y, 1 - c), device_id_type=MESH)
            for k in range(n) for r, (px, py) in enumerate(chips)]

    return _Split("gather_pass_" + tag, list(lands), copies, 3 * n)


def _part_sibling(gbufs):
    n = len(gbufs)

    def copies(refs, send, recv, off):
        x, y, c, _ = _place()
        return [pltpu.make_async_remote_copy(
            src_ref=refs[k].at[j, 1 - c], dst_ref=refs[n + k].at[j], send_sem=send.at[off + 4 * k + j],
            recv_sem=recv.at[off + 4 * k + j], device_id=(x, y, 1 - c), device_id_type=MESH)
            for k in range(n) for j in range(4)]

    return list(gbufs) + [lax.empty((4,) + g.shape[2:], g.dtype) for g in gbufs], 4 * n, copies


def _part_chips(pbufs):
    n = len(pbufs)

    def copies(refs, send, recv, off):
        x, y, c, chips = _place()
        return [pltpu.make_async_remote_copy(
            src_ref=refs[k].at[2 * px + py], dst_ref=refs[n + k].at[2 * x + y], send_sem=send.at[off + 3 * k + r],
            recv_sem=recv.at[off + 3 * k + r], device_id=(px, py, c), device_id_type=MESH)
            for k in range(n) for r, (px, py) in enumerate(chips)]

    return list(pbufs) + [lax.empty(p.shape, p.dtype) for p in pbufs], 3 * n, copies


def _part_join(fulls):
    def copies(refs, send, recv, off):
        x, y, c, _ = _place()
        return [pltpu.make_async_remote_copy(
            src_ref=refs[k].at[c], dst_ref=refs[k].at[c], send_sem=send.at[off + k], recv_sem=recv.at[off + k],
            device_id=(x, y, 1 - c), device_id_type=MESH) for k in range(len(fulls))]

    return list(fulls), len(fulls), copies


def _start_parts(parts, tag):
    arrays, spans, total = [], [], 0
    for arrs, n_copies, fn in parts:
        spans.append((len(arrays), len(arrs), total, fn))
        arrays += arrs
        total += n_copies

    def copies(refs, send, recv):
        return [cp for a0, na, off, fn in spans for cp in fn(refs[a0:a0 + na], send, recv, off)]

    op = _Split(tag, arrays, copies, total)
    op.spans = [(a0, na) for a0, na, _, _ in spans]
    return op


def _all_reduce_small(v):
    R, C = v.shape

    def body(v_ref, o_ref, g_ref, send, recv, loc):
        x, y, c, chips = _place()
        me, sibling = (x, y, c), (x, y, 1 - c)

        def rows(px, py, pc):
            return g_ref.at[4 * px + 2 * py + pc]

        def copy(k, block, to, src=None):
            return pltpu.make_async_remote_copy(
                src_ref=rows(*block) if src is None else src, dst_ref=rows(*block),
                send_sem=send.at[k], recv_sem=recv.at[k], device_id=to, device_id_type=MESH)

        mine = pltpu.make_async_copy(v_ref, rows(*me), loc)
        mine.start()
        first = [copy(0, me, sibling, src=v_ref)]
        first += [copy(1 + j, me, (*chip, c), src=v_ref) for j, chip in enumerate(chips)]
        for cp in first:
            cp.start()
        passed = [copy(4 + j, (*chip, c), sibling) for j, chip in enumerate(chips)]
        for j, chip in enumerate(chips):
            copy(1 + j, (*chip, c), me).wait_recv()
            passed[j].start()
        copy(0, sibling, me).wait_recv()
        for j, chip in enumerate(chips):
            copy(4 + j, (*chip, 1 - c), me).wait_recv()
        for cp in first + passed:
            cp.wait_send()
        mine.wait()
        acc = g_ref[0]
        for d in range(1, 8):
            acc = acc + g_ref[d]
        o_ref[...] = acc

    vm = pl.BlockSpec(memory_space=pltpu.VMEM)
    return pl.pallas_call(
        body, name="all_reduce_small", in_specs=[vm], out_specs=[vm, vm],
        out_shape=[jax.ShapeDtypeStruct((R, C), F32), jax.ShapeDtypeStruct((8, R, C), F32)],
        scratch_shapes=[pltpu.SemaphoreType.DMA((7,)), pltpu.SemaphoreType.DMA((7,)), pltpu.SemaphoreType.DMA],
    )(v)[0]


WEIGHTS = ['ffn1_norm', 'ffn1_w_gate', 'ffn1_w_up', 'ffn1_w_down', 'mix_norm', 'w_in', 'conv_w', 'conv_b', 'dt_bias',
           'a_log', 'd_skip', 'ssd_norm', 'sgu_ln_g', 'sgu_ln_b', 'sgu_w', 'sgu_b', 'w_out', 'ffn2_norm',
           'ffn2_w_gate', 'ffn2_w_up', 'ffn2_w_down', 'final_norm']
SHARDED = ['ffn1_w_gate', 'ffn1_w_up', 'ffn1_w_down', 'w_in', 'conv_w', 'w_out', 'ffn2_w_gate', 'ffn2_w_up',
           'ffn2_w_down']
SMALL = [n for n in WEIGHTS if n not in SHARDED]
GROUPS = [("ffn1", ["ffn1_w_gate", "ffn1_w_up", "ffn1_w_down"]), ("mix", ["w_in", "conv_w", "w_out"]),
          ("ffn2", ["ffn2_w_gate", "ffn2_w_up", "ffn2_w_down"])]
TRANSPOSED = ("ffn1_w_gate", "ffn1_w_up", "ffn2_w_gate", "ffn2_w_up")
DEPTH = 2


def _pack_w_in(w):
    return jnp.concatenate([w[..., 0:1152], w[..., 1536:2432], w[..., 1152:1536],
                            jnp.repeat(w[..., 2432:2438], HEAD, axis=-1), w[..., 2438:2950]], axis=-1)


def _unpack_w_in(dq, ds, du):
    return jnp.concatenate([dq, ds[:, 896:1280], ds[:, 0:896], ds[:, 1280::HEAD], du], axis=-1)


def _ffn_fwd(x, g, wg, wu, wd):
    xo, hb, S1, S2, A = _ffn_fwd_k(x, g, wg, wu, wd)
    return xo, (x, hb, S1, S2, A)


def _ffn_bwd_weights(dxo, saved, wd):
    x, hb, S1, S2, A = saved
    dG, dU, dyb = _ffn_bwd_act(dxo, S1, S2, wd)
    return (dG, dU), _ffn_bwd_k2(hb, dyb, A, dG, dU)


def _ffn_bwd_input(dxo, saved, mids, g, wg, wu):
    return _ffn_bwd_dx(mids[0], mids[1], wg, wu, saved[0], g, dxo)


def _mix_fwd(x, P):
    hb, qkv, sin, uv = _mix_proj(x, P["mix_norm"], P["w_in"])
    y_att, lse = _attn_combine([_attn_fwd(qkv, d) for d in DILATIONS])
    y_ssd, hprev = _ssd_fwd(sin, *P["ssd"])
    y_sgu = _sgu_fwd(uv, *P["sgu"])
    ycat = jnp.concatenate([y_att, y_ssd, y_sgu], axis=1).astype(BF)
    return _mm_nn(ycat, P["w_out"], res=x), (x, hb, qkv, sin, uv, y_att, lse, hprev, ycat)


def _mix_bwd_weights(dxo, saved, P):
    x, hb, qkv, sin, uv, y_att, lse, hprev, ycat = saved
    dy_att, dy_ssd, dy_sgu = _mix_bwd_dy(dxo, P["w_out"])
    dwout = _mm_tn(ycat, dxo)
    dqkv = _sum_branches([_attn_bwd(qkv, dy_att, y_att, lse, d) for d in DILATIONS])
    dsin, dcw, dcb, dvec = _ssd_bwd(sin, hprev, dy_ssd, *P["ssd"])
    duv, dsw, dsbias, dln = _sgu_bwd(uv, dy_sgu, *P["sgu"])
    dwin = _unpack_w_in(_mm_tn(hb, dqkv), _mm_tn(hb, dsin), _mm_tn(hb, duv))
    grads = dict(
        w_in=dwin, conv_w=dcw, conv_b=dcb[0], dt_bias=dvec[0, ::HEAD], a_log=dvec[1, ::HEAD],
        d_skip=jnp.sum(dvec[2].reshape(6, HEAD), axis=-1), ssd_norm=dvec[3], sgu_ln_g=dln[0], sgu_ln_b=dln[1],
        sgu_w=dsw, sgu_b=jnp.sum(dsbias.reshape(CHUNK, 4, HEAD), axis=-1).T, w_out=dwout)
    return (dqkv, dsin, duv), grads


def _mix_bwd_input(dxo, saved, mids, P):
    return _mix_bwd_dx(*mids, P["w_in"], saved[0], P["mix_norm"], dxo)


def _halved(g):
    rows = g.size // g.shape[-1]
    return g.reshape(4, 2, rows // 8, g.shape[-1])


def kernel(x, ffn1_norm, ffn1_w_gate, ffn1_w_up, ffn1_w_down, mix_norm, w_in, conv_w, conv_b, dt_bias, a_log, d_skip, ssd_norm, sgu_ln_g, sgu_ln_b, sgu_w, sgu_b, w_out, ffn2_norm, ffn2_w_gate, ffn2_w_up, ffn2_w_down, final_norm, loss_target, m_ffn1_norm, m_ffn1_w_gate, m_ffn1_w_up, m_ffn1_w_down, m_mix_norm, m_w_in, m_conv_w, m_conv_b, m_dt_bias, m_a_log, m_d_skip, m_ssd_norm, m_sgu_ln_g, m_sgu_ln_b, m_sgu_w, m_sgu_b, m_w_out, m_ffn2_norm, m_ffn2_w_gate, m_ffn2_w_up, m_ffn2_w_down, m_final_norm, v_ffn1_norm, v_ffn1_w_gate, v_ffn1_w_up, v_ffn1_w_down, v_mix_norm, v_w_in, v_conv_w, v_conv_b, v_dt_bias, v_a_log, v_d_skip, v_ssd_norm, v_sgu_ln_g, v_sgu_ln_b, v_sgu_w, v_sgu_b, v_w_out, v_ffn2_norm, v_ffn2_w_gate, v_ffn2_w_up, v_ffn2_w_down, v_final_norm):
    given = dict(x=x, ffn1_norm=ffn1_norm, ffn1_w_gate=ffn1_w_gate, ffn1_w_up=ffn1_w_up, ffn1_w_down=ffn1_w_down, mix_norm=mix_norm, w_in=w_in, conv_w=conv_w, conv_b=conv_b, dt_bias=dt_bias, a_log=a_log, d_skip=d_skip, ssd_norm=ssd_norm, sgu_ln_g=sgu_ln_g, sgu_ln_b=sgu_ln_b, sgu_w=sgu_w, sgu_b=sgu_b, w_out=w_out, ffn2_norm=ffn2_norm, ffn2_w_gate=ffn2_w_gate, ffn2_w_up=ffn2_w_up, ffn2_w_down=ffn2_w_down, final_norm=final_norm, loss_target=loss_target, m_ffn1_norm=m_ffn1_norm, m_ffn1_w_gate=m_ffn1_w_gate, m_ffn1_w_up=m_ffn1_w_up, m_ffn1_w_down=m_ffn1_w_down, m_mix_norm=m_mix_norm, m_w_in=m_w_in, m_conv_w=m_conv_w, m_conv_b=m_conv_b, m_dt_bias=m_dt_bias, m_a_log=m_a_log, m_d_skip=m_d_skip, m_ssd_norm=m_ssd_norm, m_sgu_ln_g=m_sgu_ln_g, m_sgu_ln_b=m_sgu_ln_b, m_sgu_w=m_sgu_w, m_sgu_b=m_sgu_b, m_w_out=m_w_out, m_ffn2_norm=m_ffn2_norm, m_ffn2_w_gate=m_ffn2_w_gate, m_ffn2_w_up=m_ffn2_w_up, m_ffn2_w_down=m_ffn2_w_down, m_final_norm=m_final_norm, v_ffn1_norm=v_ffn1_norm, v_ffn1_w_gate=v_ffn1_w_gate, v_ffn1_w_up=v_ffn1_w_up, v_ffn1_w_down=v_ffn1_w_down, v_mix_norm=v_mix_norm, v_w_in=v_w_in, v_conv_w=v_conv_w, v_conv_b=v_conv_b, v_dt_bias=v_dt_bias, v_a_log=v_a_log, v_d_skip=v_d_skip, v_ssd_norm=v_ssd_norm, v_sgu_ln_g=v_sgu_ln_g, v_sgu_ln_b=v_sgu_ln_b, v_sgu_w=v_sgu_w, v_sgu_b=v_sgu_b, v_w_out=v_w_out, v_ffn2_norm=v_ffn2_norm, v_ffn2_w_gate=v_ffn2_w_gate, v_ffn2_w_up=v_ffn2_w_up, v_ffn2_w_down=v_ffn2_w_down, v_final_norm=v_final_norm)
    T = given["x"].shape[0] * given["x"].shape[1]
    D = given["x"].shape[2]
    x0 = given["x"].reshape(T, D)
    tgt = given["loss_target"].reshape(T, D)
    c = lax.axis_index("c")

    bf = {n: given[n].astype(BF) for n in SHARDED if n not in ("w_in", "conv_w")}
    bf["w_in"] = _pack_w_in(given["w_in"]).astype(BF)
    bf["conv_w"] = given["conv_w"]
    first_key = (0, GROUPS[0][0])
    first = [bf[n][0].reshape((2, bf[n].shape[1] // 2) + bf[n].shape[2:]) for n in GROUPS[0][1]]
    gathers = {first_key: _gather_halves_start(first, "l0_" + GROUPS[0][0])}
    later = {(i, gname): [bf[n][i] for n in names] for i in range(DEPTH) for gname, names in GROUPS
             if (i, gname) != first_key}
    zones = {key: _landing_zones(arrs) for key, arrs in later.items()}

    def gathered(i, gname, after):
        if (i, gname) != first_key:
            return gathers[(i, gname)].wait(after)[3:]
        got = gathers[first_key].wait([after] + [z for zs in zones.values() for z in zs])[3:]
        got = _gather_halves_finish(got, "l0_" + gname).wait(after)
        prev = got[0]
        for key, arrs in later.items():
            gathers[key] = _gather_start(arrs, zones[key], f"l{key[0]}_{key[1]}", after=[prev])
            prev = gathers[key].token_array
        return [z.reshape((4, 2 * z.shape[2]) + z.shape[3:]) for z in got]

    def mix_params(i, got):
        win = got[0].reshape(D, W_QKV + W_SSD + W_UV)
        rep = lambda v: jnp.repeat(v, HEAD)[None]
        ssd = (got[1].transpose(1, 0, 2).reshape(4, SSD_CONV_DIM), given["conv_b"][i][None],
               rep(given["dt_bias"][i]), rep(given["a_log"][i]), rep(given["d_skip"][i]), given["ssd_norm"][i][None])
        sgu = (given["sgu_ln_g"][i][None], given["sgu_ln_b"][i][None], given["sgu_w"][i],
               jnp.repeat(given["sgu_b"][i].T, HEAD, axis=1))
        return dict(mix_norm=given["mix_norm"][i][None], w_in=win, w_out=got[2].reshape(-1, D), ssd=ssd, sgu=sgu)

    x = x0
    tape = []
    for i in range(DEPTH):
        got = gathered(i, "ffn1", x)
        token = functools.reduce(lambda a, b: a + b, [g.token for g in gathers.values()]) if i == 0 else 0.0
        P = dict(ffn1=(given["ffn1_norm"][i][None] + token, *got))
        x, s1 = _ffn_fwd(x, *P["ffn1"])
        P.update(mix_params(i, gathered(i, "mix", x)))
        x, s2 = _mix_fwd(x, P)
        P["ffn2"] = (given["ffn2_norm"][i][None], *gathered(i, "ffn2", x))
        x, s3 = _ffn_fwd(x, *P["ffn2"])
        tape.append((P, s1, s2, s3))
    loss_part, dx, dgf = _final_loss(x, given["final_norm"][None], tgt)

    me = 2 * lax.axis_index("x") + lax.axis_index("y")
    jobs = []

    flight = dict(op=None, owners=[], ticks=0)

    def tick(after, begin=None):
        parts, owners = [], []
        if flight["op"] is not None:
            got = flight["op"].wait(after)
            for job, (a0, na) in zip(flight["owners"], flight["op"].spans):
                mine, k = got[a0:a0 + na], len(job["names"])
                if job["stage"] == 1:
                    parts.append(_part_chips(_pair_add(mine[:k], mine[k:], c)))
                elif job["stage"] == 2:
                    parts.append(_part_join(_chip_sum(mine[:k], mine[k:], me, c)))
                else:
                    job.update(stage=4, out=dict(zip(job["names"], mine)))
                    continue
                job["stage"] += 1
                owners.append(job)
        if begin is not None:
            i, gname, gd = begin
            names = [n for n in dict(GROUPS)[gname] if n != "conv_w"]
            jobs.append(dict(key=(i, gname), names=names, stage=1))
            parts.append(_part_sibling([_halved(gd[n]) for n in names]))
            owners.append(jobs[-1])
        flight.update(op=_start_parts(parts, f"rs_tick{flight['ticks']}") if parts else None, owners=owners,
                      ticks=flight["ticks"] + 1)
        return flight["op"].token if parts else 0.0

    grads = [dict() for _ in range(DEPTH)]
    for i in reversed(range(DEPTH)):
        P, s1, s2, s3 = tape[i]
        g = grads[i]
        norm, wg, wu, wd = P["ffn2"]
        mids, (g["ffn2_w_gate"], g["ffn2_w_up"], g["ffn2_w_down"]) = _ffn_bwd_weights(dx, s3, wd)
        tok = tick(g["ffn2_w_down"], (i, "ffn2", g))
        dx, dn2 = _ffn_bwd_input(dx, s3, mids, norm + tok, wg, wu)
        mids, gm = _mix_bwd_weights(dx, s2, P)
        g.update(gm)
        tok = tick(gm["w_in"], (i, "mix", g))
        dx, dnm = _mix_bwd_input(dx, s2, mids, {**P, "mix_norm": P["mix_norm"] + tok})
        norm, wg, wu, wd = P["ffn1"]
        mids, (g["ffn1_w_gate"], g["ffn1_w_up"], g["ffn1_w_down"]) = _ffn_bwd_weights(dx, s1, wd)
        tok = tick(g["ffn1_w_down"], (i, "ffn1", g))
        dx, dn1 = _ffn_bwd_input(dx, s1, mids, norm + tok, wg, wu)
        g["ffn1_norm"], g["mix_norm"], g["ffn2_norm"] = dn1[0], dnm[0], dn2[0]
    grad_x = dx.reshape(given["x"].shape)

    order = [n for n in SMALL if n != "final_norm"] + ["final_norm"]
    small = [jnp.stack([grads[i][n] for i in range(DEPTH)]) for n in order[:-1] + ["conv_w"]]
    small = small[:-1] + [dgf[0], small[-1], loss_part[0, 0:1]]
    n_small = sum(s.size for s in small)
    rows_small = -(-n_small // (128 * 8)) * 8

    def flat(arrs):
        fill = rows_small * 128 - sum(a.size for a in arrs)
        return jnp.concatenate([a.reshape(-1) for a in arrs] + [jnp.zeros((fill,), F32)]).reshape(rows_small, 128)

    gsmall = _all_reduce_small(flat(small)).reshape(-1)

    grad_w = {}
    off = 0
    for n in order:
        size = given[n].size
        grad_w[n] = gsmall[off:off + size].reshape(given[n].shape)
        off += size
    cw = gsmall[off:off + 2 * 4 * SSD_CONV_DIM].reshape(DEPTH, 4, SSD_CONV_DIM)
    grad_w["conv_w"] = lax.dynamic_slice_in_dim(cw, me * (SSD_CONV_DIM // 4), SSD_CONV_DIM // 4, axis=2)
    loss = gsmall[off + 2 * 4 * SSD_CONV_DIM]

    delta, new_m, new_v = {}, {}, {}
    shp = given["conv_w"].shape
    d, m2, v2 = _adamw(*[a.reshape(shp[0] * shp[1], shp[2])
                         for a in (given["conv_w"], grad_w["conv_w"], given["m_conv_w"], given["v_conv_w"])])
    delta["conv_w"], new_m["conv_w"], new_v["conv_w"] = d.reshape(shp), m2.reshape(shp), v2.reshape(shp)
    packed = [flat([given[pre + n] for n in order]) for pre in ("", "m_", "v_")]
    small_out = _adamw(packed[0], gsmall.reshape(rows_small, 128), packed[1], packed[2])
    outs = [o.reshape(-1) for o in small_out]
    off = 0
    for n in order:
        size = given[n].size
        for dst, o in zip((delta, new_m, new_v), outs):
            dst[n] = o[off:off + size].reshape(given[n].shape)
        off += size

    stepped, arrived = {}, {}

    def update_arrived(dep):
        out = None
        for job in jobs:
            if job["stage"] == 4 and not job.get("seen"):
                job["seen"] = True
                for n, full in job["out"].items():
                    view = (lambda a: jnp.swapaxes(a, 1, 2)) if n in TRANSPOSED else (lambda a: a)
                    arrived.setdefault(n, {})[job["key"][0]] = full.reshape(view(given[n]).shape[1:])
                    if len(arrived[n]) == DEPTH:
                        res = _adamw_pair(view(given[n]), arrived[n][0], arrived[n][1], view(given["m_" + n]),
                                          view(given["v_" + n]), dep)
                        stepped[n] = [view(r) for r in res]
                        out = res[0]
        return out

    after = small_out[0]
    while any(j["stage"] < 4 for j in jobs):
        done = update_arrived(jnp.zeros((8, 128), F32) + tok)
        after = after if done is None else done
        tok = tick(after)
    update_arrived(jnp.zeros((8, 128), F32) + tok)
    for n, (d, m2, v2, g) in stepped.items():
        delta[n], new_m[n], new_v[n], grad_w[n] = d, m2, v2, g

    return (loss, grad_x, *[grad_w[n] for n in WEIGHTS], *[delta[n] for n in WEIGHTS],
            *[new_m[n] for n in WEIGHTS], *[new_v[n] for n in WEIGHTS])
```

```python
import functools
import math

import jax
import jax.numpy as jnp
from jax import lax
from jax.experimental import pallas as pl
from jax.experimental.pallas import tpu as pltpu

F32 = jnp.float32
BF = jnp.bfloat16

RMS_EPS = 1e-6
LN_EPS = 1e-5
SEQ = 2048
CHUNK = 128
N_CHUNK = SEQ // CHUNK
ATT_W = 384
HEAD = 64
SSD_W = 384
SSD_CONV_DIM = 896
SSD_STATE = 128
SGU_W = 256
DILATIONS = (1, 4, 16)
W_QKV = 3 * ATT_W
W_SSD = SSD_CONV_DIM + SSD_W + SSD_W
W_UV = 2 * SGU_W
ADAM_LR = 0.001
ADAM_B1 = 0.9
ADAM_B2 = 0.999
ADAM_EPS = 1e-08
ADAM_WD = 0.01
ADAM_STEP = 10
NEG = -1e30
ATTN_BWD_VMEM = 48 * 2 ** 20
ATTN_SUBSEQ_PER_STEP = 4
FFN_VMEM = 60 * 2 ** 20


def _dot(a, b):
    return jnp.dot(a, b, preferred_element_type=F32)


def _dot_nt(a, b):
    return lax.dot_general(a, b, (((1,), (1,)), ((), ())), preferred_element_type=F32)


def _dot_tn(a, b):
    return lax.dot_general(a, b, (((0,), (0,)), ((), ())), preferred_element_type=F32)


def _sigmoid(x):
    return 1.0 / (1.0 + jnp.exp(-x))


def _call(body, *, name, grid, in_specs, out_specs, out_shape, scratch=(), sem=None, vmem=None):
    return pl.pallas_call(
        body, name=name, grid=grid, in_specs=in_specs, out_specs=out_specs, out_shape=out_shape,
        scratch_shapes=list(scratch),
        compiler_params=pltpu.CompilerParams(dimension_semantics=sem, vmem_limit_bytes=vmem),
    )


def _tile(n, want):
    t = min(n, want)
    while n % t:
        t //= 2
    return t


def _final_loss(x, g, tgt):
    T, D = x.shape
    tm = _tile(T, 512)

    def body(x_ref, g_ref, t_ref, l_ref, dx_ref, dg_ref):
        @pl.when(pl.program_id(0) == 0)
        def _():
            dg_ref[...] = jnp.zeros_like(dg_ref)
            l_ref[...] = jnp.zeros_like(l_ref)

        xf = x_ref[...]
        gg = g_ref[...]
        r = lax.rsqrt(jnp.mean(xf * xf, axis=-1, keepdims=True) + RMS_EPS)
        xn = xf * r
        e = xn * gg - t_ref[...]
        part = 0.5 * jnp.sum(jnp.mean(e * e, axis=-1, keepdims=True), axis=0, keepdims=True)
        l_ref[...] += jnp.broadcast_to(part, l_ref.shape)
        dy = e * (1.0 / D)
        u = dy * gg
        mu = jnp.mean(u * xf, axis=-1, keepdims=True)
        dx_ref[...] = r * (u - xf * (r * r * mu))
        dg_ref[...] += jnp.sum(dy * xn, axis=0, keepdims=True)

    row = pl.BlockSpec((tm, D), lambda i: (i, 0))
    vec = pl.BlockSpec((1, D), lambda i: (0, 0))
    lsp = pl.BlockSpec((1, 128), lambda i: (0, 0))
    return _call(body, name="final_loss", grid=(T // tm,), in_specs=[row, vec, row], out_specs=[lsp, row, vec],
                 out_shape=[jax.ShapeDtypeStruct((1, 128), F32), jax.ShapeDtypeStruct((T, D), F32),
                            jax.ShapeDtypeStruct((1, D), F32)],
                 sem=("arbitrary",))(x, g, tgt)


def _resident(shape):
    return pl.BlockSpec(shape, lambda *_: (0,) * len(shape), pipeline_mode=pl.Buffered(1))


def _ffn_fwd_k(x, gn, wg, wu, wd):
    T, D = x.shape
    NS, _, Fs = wg.shape
    tm = _tile(T, 1024)

    def body(x_ref, gn_ref, wg_ref, wu_ref, wd_ref, o_ref, h_ref, s1_ref, s2_ref, a_ref, hs, acc):
        j = pl.program_id(1)

        @pl.when(j == 0)
        def _():
            xf = x_ref[...]
            r = lax.rsqrt(jnp.mean(xf * xf, axis=-1, keepdims=True) + RMS_EPS)
            hs[...] = (xf * r * gn_ref[...]).astype(BF)
            h_ref[...] = hs[...]
            acc[...] = jnp.zeros_like(acc)

        h = hs[...]
        g = _dot(h, wg_ref[...])
        u = _dot(h, wu_ref[...])
        sg = _sigmoid(g)
        s1 = g * sg
        a = (s1 * u).astype(BF)
        s1_ref[...] = s1.astype(BF)
        s2_ref[...] = (u * (sg * (1.0 + g * (1.0 - sg)))).astype(BF)
        a_ref[...] = a
        acc[...] += _dot(a, wd_ref[...])

        @pl.when(j == NS - 1)
        def _():
            o_ref[...] = x_ref[...] + 0.5 * acc[...]

    row = pl.BlockSpec((tm, D), lambda i, j: (i, 0))
    act = pl.BlockSpec((None, tm, Fs), lambda i, j: (j, i, 0))
    sh = jax.ShapeDtypeStruct((NS, T, Fs), BF)
    wspec = lambda w: pl.BlockSpec((None,) + w.shape[1:], lambda i, j: (j, 0, 0))
    return _call(body, name="ffn_fwd", grid=(T // tm, NS),
                 in_specs=[row, pl.BlockSpec((1, D), lambda i, j: (0, 0)), wspec(wg), wspec(wu), wspec(wd)],
                 out_specs=[row, row, act, act, act],
                 out_shape=[jax.ShapeDtypeStruct((T, D), F32), jax.ShapeDtypeStruct((T, D), BF), sh, sh, sh],
                 scratch=[pltpu.VMEM((tm, D), BF), pltpu.VMEM((tm, D), F32)],
                 sem=("parallel", "arbitrary"), vmem=FFN_VMEM)(x, gn, wg, wu, wd)


def _ffn_bwd_act(dxo, s1, s2, wd):
    NS, T, Fs = s1.shape
    D = dxo.shape[1]
    tm = _tile(T, 1024)

    def body(dxo_ref, s1_ref, s2_ref, wd_ref, dg_ref, du_ref, dy_ref, dys):
        j = pl.program_id(1)

        @pl.when(j == 0)
        def _():
            dys[...] = (0.5 * dxo_ref[...]).astype(BF)
            dy_ref[...] = dys[...]

        da = _dot_nt(dys[...], wd_ref[j])
        dg_ref[...] = (da * s2_ref[...].astype(F32)).astype(BF)
        du_ref[...] = (da * s1_ref[...].astype(F32)).astype(BF)

    row = pl.BlockSpec((tm, D), lambda i, j: (i, 0))
    act = pl.BlockSpec((None, tm, Fs), lambda i, j: (j, i, 0))
    sh = jax.ShapeDtypeStruct((NS, T, Fs), BF)
    return _call(body, name="ffn_bwd_act", grid=(T // tm, NS), in_specs=[row, act, act, _resident(wd.shape)],
                 out_specs=[act, act, row], out_shape=[sh, sh, jax.ShapeDtypeStruct((T, D), BF)],
                 scratch=[pltpu.VMEM((tm, D), BF)], sem=("parallel", "arbitrary"))(dxo, s1, s2, wd)


def _ffn_bwd_dx(dg, du, wg, wu, x, gn, dxo):
    NS, T, Fs = dg.shape
    D = x.shape[1]
    tm = _tile(T, 1024)

    def body(dg_ref, du_ref, wg_ref, wu_ref, x_ref, gn_ref, dxo_ref, dx_ref, dgn_ref, acc):
        i, j = pl.program_id(0), pl.program_id(1)

        @pl.when((i == 0) & (j == 0))
        def _():
            dgn_ref[...] = jnp.zeros_like(dgn_ref)

        @pl.when(j == 0)
        def _():
            acc[...] = jnp.zeros_like(acc)

        acc[...] += _dot_nt(dg_ref[...], wg_ref[j]) + _dot_nt(du_ref[...], wu_ref[j])

        @pl.when(j == NS - 1)
        def _():
            xf = x_ref[...]
            r = lax.rsqrt(jnp.mean(xf * xf, axis=-1, keepdims=True) + RMS_EPS)
            dh = acc[...]
            uu = dh * gn_ref[...]
            mu = jnp.mean(uu * xf, axis=-1, keepdims=True)
            dx_ref[...] = dxo_ref[...] + r * (uu - xf * (r * r * mu))
            dgn_ref[...] += jnp.sum(dh * xf * r, axis=0, keepdims=True)

    row = pl.BlockSpec((tm, D), lambda i, j: (i, 0))
    vec = pl.BlockSpec((1, D), lambda i, j: (0, 0))
    act = pl.BlockSpec((None, tm, Fs), lambda i, j: (j, i, 0))
    return _call(body, name="ffn_bwd_dx", grid=(T // tm, NS),
                 in_specs=[act, act, _resident(wg.shape), _resident(wu.shape), row, vec, row], out_specs=[row, vec],
                 out_shape=[jax.ShapeDtypeStruct((T, D), F32), jax.ShapeDtypeStruct((1, D), F32)],
                 scratch=[pltpu.VMEM((tm, D), F32)], sem=("arbitrary", "arbitrary"), vmem=FFN_VMEM)(
        dg, du, wg, wu, x, gn, dxo)


def _ffn_bwd_k2(hb, dyb, a, dg, du):
    NS, T, Fs = a.shape
    D = hb.shape[1]
    tk = _tile(T, 1024)

    def body(h_ref, dy_ref, a_ref, dg_ref, du_ref, og_ref, ou_ref, od_ref):
        @pl.when(pl.program_id(1) == 0)
        def _():
            og_ref[...] = jnp.zeros_like(og_ref)
            ou_ref[...] = jnp.zeros_like(ou_ref)
            od_ref[...] = jnp.zeros_like(od_ref)

        h = h_ref[...]
        og_ref[...] += _dot_tn(dg_ref[...], h)
        ou_ref[...] += _dot_tn(du_ref[...], h)
        od_ref[...] += _dot_tn(a_ref[...], dy_ref[...])

    row = pl.BlockSpec((tk, D), lambda j, k: (k, 0))
    act = pl.BlockSpec((None, tk, Fs), lambda j, k: (j, k, 0))
    return _call(body, name="ffn_bwd_w", grid=(NS, T // tk), in_specs=[row, row, act, act, act],
                 out_specs=[pl.BlockSpec((None, Fs, D), lambda j, k: (j, 0, 0))] * 3,
                 out_shape=[jax.ShapeDtypeStruct((NS, Fs, D), F32)] * 3,
                 sem=("parallel", "arbitrary"))(hb, dyb, a, dg, du)


def _mm_nn(a, b, res=None, out_dtype=F32):
    T, K = a.shape
    N = b.shape[1]
    tm = _tile(T, 512)
    tn = N if N <= 2048 else _tile(N, 1024)

    def body(*refs):
        if res is None:
            a_ref, b_ref, o_ref = refs
            o_ref[...] = _dot(a_ref[...], b_ref[...]).astype(out_dtype)
        else:
            a_ref, b_ref, r_ref, o_ref = refs
            o_ref[...] = (r_ref[...] + _dot(a_ref[...], b_ref[...])).astype(out_dtype)

    o = pl.BlockSpec((tm, tn), lambda i, j: (i, j))
    ins = [pl.BlockSpec((tm, K), lambda i, j: (i, 0)), pl.BlockSpec((K, tn), lambda i, j: (0, j))]
    args = [a, b]
    if res is not None:
        ins.append(o)
        args.append(res)
    return _call(body, name="mm_nn", grid=(T // tm, N // tn), in_specs=ins, out_specs=o,
                 out_shape=jax.ShapeDtypeStruct((T, N), out_dtype), sem=("parallel", "parallel"))(*args)


def _mix_bwd_dy(dxo, w_out):
    T, D = dxo.shape
    tm = _tile(T, 512)
    cuts = (0, ATT_W, ATT_W + SSD_W, ATT_W + SSD_W + SGU_W)

    def body(dx_ref, w_ref, a_ref, s_ref, g_ref):
        d = _dot_nt(dx_ref[...].astype(BF), w_ref[...])
        for o_ref, lo, hi in zip((a_ref, s_ref, g_ref), cuts[:-1], cuts[1:]):
            o_ref[...] = d[:, lo:hi]

    row = lambda w: pl.BlockSpec((tm, w), lambda i: (i, 0))
    return _call(body, name="mix_bwd_dy", grid=(T // tm,), in_specs=[row(D), _resident(w_out.shape)],
                 out_specs=[row(ATT_W), row(SSD_W), row(SGU_W)],
                 out_shape=[jax.ShapeDtypeStruct((T, w), F32) for w in (ATT_W, SSD_W, SGU_W)],
                 sem=("parallel",))(dxo, w_out)


def _mm_tn(a, b):
    T, M = a.shape
    N = b.shape[1]
    tk = _tile(T, 1024)
    tmm = _tile(M, 512)

    def body(a_ref, b_ref, o_ref):
        @pl.when(pl.program_id(1) == 0)
        def _():
            o_ref[...] = jnp.zeros_like(o_ref)

        o_ref[...] += _dot_tn(a_ref[...].astype(BF), b_ref[...].astype(BF))

    return _call(body, name="mm_tn", grid=(M // tmm, T // tk),
                 in_specs=[pl.BlockSpec((tk, tmm), lambda i, k: (k, i)), pl.BlockSpec((tk, N), lambda i, k: (k, 0))],
                 out_specs=pl.BlockSpec((tmm, N), lambda i, k: (i, 0)),
                 out_shape=jax.ShapeDtypeStruct((M, N), F32), sem=("parallel", "arbitrary"))(a, b)


def _mix_proj(x, gn, win):
    T, D = x.shape
    tm = _tile(T, 512)
    cuts = (0, W_QKV, W_QKV + W_SSD, W_QKV + W_SSD + W_UV)

    def body(x_ref, gn_ref, w_ref, h_ref, q_ref, s_ref, u_ref):
        xf = x_ref[...]
        r = lax.rsqrt(jnp.mean(xf * xf, axis=-1, keepdims=True) + RMS_EPS)
        h = (xf * r * gn_ref[...]).astype(BF)
        h_ref[...] = h
        for o_ref, lo, hi in zip((q_ref, s_ref, u_ref), cuts[:-1], cuts[1:]):
            o_ref[...] = _dot(h, w_ref[:, lo:hi])

    row = lambda w: pl.BlockSpec((tm, w), lambda i: (i, 0))
    return _call(body, name="mix_proj", grid=(T // tm,),
                 in_specs=[row(D), pl.BlockSpec((1, D), lambda i: (0, 0)), _resident(win.shape)],
                 out_specs=[row(D), row(W_QKV), row(W_SSD), row(W_UV)],
                 out_shape=[jax.ShapeDtypeStruct((T, D), BF), jax.ShapeDtypeStruct((T, W_QKV), F32),
                            jax.ShapeDtypeStruct((T, W_SSD), F32), jax.ShapeDtypeStruct((T, W_UV), F32)],
                 sem=("parallel",))(x, gn, win)


def _mix_bwd_dx(dqkv, dsin, duv, win, x, gn, dxo):
    T, D = x.shape
    tm = _tile(T, 512)
    cuts = (0, W_QKV, W_QKV + W_SSD, W_QKV + W_SSD + W_UV)

    def body(dq_ref, ds_ref, du_ref, w_ref, x_ref, gn_ref, dxo_ref, dx_ref, dgn_ref):
        @pl.when(pl.program_id(0) == 0)
        def _():
            dgn_ref[...] = jnp.zeros_like(dgn_ref)

        dh = (_dot_nt(dq_ref[...], w_ref[:, cuts[0]:cuts[1]]) + _dot_nt(ds_ref[...], w_ref[:, cuts[1]:cuts[2]])
              + _dot_nt(du_ref[...], w_ref[:, cuts[2]:cuts[3]]))
        xf = x_ref[...]
        r = lax.rsqrt(jnp.mean(xf * xf, axis=-1, keepdims=True) + RMS_EPS)
        uu = dh * gn_ref[...]
        mu = jnp.mean(uu * xf, axis=-1, keepdims=True)
        dx_ref[...] = dxo_ref[...] + r * (uu - xf * (r * r * mu))
        dgn_ref[...] += jnp.sum(dh * xf * r, axis=0, keepdims=True)

    row = lambda w: pl.BlockSpec((tm, w), lambda i: (i, 0))
    vec = pl.BlockSpec((1, D), lambda i: (0, 0))
    return _call(body, name="mix_bwd_dx", grid=(T // tm,),
                 in_specs=[row(W_QKV), row(W_SSD), row(W_UV), _resident(win.shape), row(D), vec, row(D)],
                 out_specs=[row(D), vec],
                 out_shape=[jax.ShapeDtypeStruct((T, D), F32), jax.ShapeDtypeStruct((1, D), F32)],
                 sem=("arbitrary",))(dqkv, dsin, duv, win, x, gn, dxo)


def _lane_mask(e, width=128):
    return (lax.broadcasted_iota(jnp.int32, (1, width), 1) // HEAD) == e


def _band_mask(n):
    qi = lax.broadcasted_iota(jnp.int32, (CHUNK, 2 * CHUNK), 0)
    kj = lax.broadcasted_iota(jnp.int32, (CHUNK, 2 * CHUNK), 1)
    dist = qi + CHUNK - kj
    return (dist >= 0) & (dist <= CHUNK) & ((kj >= CHUNK) | (n > 0))


def _sub_rows(r, block, dil):
    if dil == 1:
        return pl.ds(pl.multiple_of(block * CHUNK, CHUNK), CHUNK)
    return pl.ds(r + dil * CHUNK * block, CHUNK, stride=dil)


def _attn_specs(T, dil):
    per_step = ATTN_SUBSEQ_PER_STEP if dil == 1 else min(dil, 2 * ATTN_SUBSEQ_PER_STEP)
    qrows = CHUNK * (dil if dil > 1 else per_step)
    B, nbq = T // SEQ, SEQ // qrows
    once = dict(pipeline_mode=pl.Buffered(1))
    q_like = lambda col: pl.BlockSpec((qrows, 128), lambda b, n, r: (b * nbq + n, col), **(once if nbq == 1 else {}))
    k_like = lambda col: pl.BlockSpec((SEQ, 128), lambda b, n, r: (b, col), **once)
    return B, nbq, max(dil // per_step, 1), per_step, q_like, k_like


def _attn_step(u, dil, per_step):
    if dil > 1:
        r = pl.program_id(2) * per_step + u
        return r, pl.program_id(1), _sub_rows(r, 0, dil)
    return 0, pl.program_id(1) * per_step + u, pl.ds(CHUNK * u, CHUNK)


def _attn_fwd(qkv, dil):
    T = qkv.shape[0]
    B, nb, last, per_step, q_like, k_like = _attn_specs(T, dil)
    scale = HEAD ** -0.5

    def body(*refs):
        q_t, k_t, v_t, o_t, l_t = refs[0:3], refs[3:6], refs[6:9], refs[9:12], refs[12:15]
        for u in range(per_step):
            r, n, mine = _attn_step(u, dil, per_step)
            mask = _band_mask(n)
            cur, prv = _sub_rows(r, n, dil), _sub_rows(r, jnp.maximum(n - 1, 0), dil)
            for t in range(3):
                qt = q_t[t][mine, :].astype(BF)
                kt = jnp.concatenate([k_t[t][prv, :], k_t[t][cur, :]], axis=0).astype(BF)
                vt = jnp.concatenate([v_t[t][prv, :], v_t[t][cur, :]], axis=0).astype(BF)
                o_pair = jnp.zeros((CHUNK, 128), F32)
                l_pair = jnp.zeros((CHUNK, 128), F32)
                for e in range(2):
                    lm = _lane_mask(e)
                    s = _dot_nt(jnp.where(lm, qt, jnp.zeros_like(qt)), kt) * scale
                    s = jnp.where(mask, s, NEG)
                    m = jnp.max(s, axis=-1, keepdims=True)
                    p = jnp.exp(s - m)
                    den = jnp.sum(p, axis=-1, keepdims=True)
                    o = _dot(p.astype(BF), vt) / den
                    o_pair = jnp.where(lm, o, o_pair)
                    l_pair = jnp.where(lm, m + jnp.log(den), l_pair)
                o_t[t][mine, :] = o_pair
                l_t[t][mine, :] = l_pair


    out_spec = pl.BlockSpec(q_like(0).block_shape, lambda b, n, r: (b * nb + n, 0))
    sh = jax.ShapeDtypeStruct((T, 128), F32)
    outs = _call(
        body, name=f"attn_fwd_d{dil}", grid=(B, nb, last),
        in_specs=[q_like(t) for t in range(3)] + [k_like(3 + t) for t in range(3)] + [k_like(6 + t) for t in range(3)],
        out_specs=[out_spec] * 6, out_shape=[sh] * 6, sem=("parallel", "arbitrary", "arbitrary"))(*([qkv] * 9))
    return list(outs[0:3]), list(outs[3:6])


def _attn_combine(branches):
    T = branches[0][0][0].shape[0]
    tm = _tile(T, 512)

    def body(*refs):
        y_ref, l_ref = refs[-2:]
        for t in range(3):
            o = [refs[6 * i + t][...] for i in range(3)]
            a, b, c = [refs[6 * i + 3 + t][...] for i in range(3)]
            m = jnp.maximum(jnp.maximum(a, b), c)
            ea, eb, ec = jnp.exp(a - m), jnp.exp(b - m), jnp.exp(c - m)
            z = ea + eb + ec
            y_ref[:, 128 * t:128 * (t + 1)] = (ea * o[0] + eb * o[1] + ec * o[2]) / z
            l_ref[:, 128 * t:128 * (t + 1)] = m + jnp.log(z)

    tile = pl.BlockSpec((tm, 128), lambda i: (i, 0))
    row = pl.BlockSpec((tm, ATT_W), lambda i: (i, 0))
    sh = jax.ShapeDtypeStruct((T, ATT_W), F32)
    flat = [a for o_t, l_t in branches for a in (*o_t, *l_t)]
    return _call(body, name="attn_combine", grid=(T // tm,), in_specs=[tile] * 18, out_specs=[row, row],
                 out_shape=[sh, sh], sem=("parallel",))(*flat)


def _attn_bwd(qkv, do, out, lse, dil):
    T = qkv.shape[0]
    B, nb, last, per_step, q_like, k_like = _attn_specs(T, dil)
    scale = HEAD ** -0.5

    def body(*refs):
        q_t, k_t, v_t = refs[0:3], refs[3:6], refs[6:9]
        do_t, out_t, lse_t = refs[9:12], refs[12:15], refs[15:18]
        dq_t, dk_t, dv_t = refs[18:21], refs[21:24], refs[24:27]
        @pl.when((pl.program_id(1) == 0) & (pl.program_id(2) == 0))
        def _():
            for t in range(3):
                dk_t[t][...] = jnp.zeros_like(dk_t[t])
                dv_t[t][...] = jnp.zeros_like(dv_t[t])

        for u in range(per_step):
            r, n, mine = _attn_step(u, dil, per_step)
            mask = _band_mask(n)
            cur, prv = _sub_rows(r, n, dil), _sub_rows(r, jnp.maximum(n - 1, 0), dil)
            for t in range(3):
                qt = q_t[t][mine, :].astype(BF)
                kt = jnp.concatenate([k_t[t][prv, :], k_t[t][cur, :]], axis=0).astype(BF)
                vt = jnp.concatenate([v_t[t][prv, :], v_t[t][cur, :]], axis=0).astype(BF)
                do_ = do_t[t][mine, :]
                dlt = do_ * out_t[t][mine, :]
                ls = lse_t[t][mine, :]
                dq_pair = jnp.zeros((CHUNK, 128), F32)
                dk_acc = jnp.zeros((2 * CHUNK, 128), F32)
                dv_acc = jnp.zeros((2 * CHUNK, 128), F32)
                for e in range(2):
                    lm = _lane_mask(e)
                    qm = jnp.where(lm, qt, jnp.zeros_like(qt))
                    s = _dot_nt(qm, kt) * scale
                    p = jnp.exp(jnp.where(mask, s - ls[:, HEAD * e:HEAD * e + 1], NEG))
                    dom = jnp.where(lm, do_, 0.0).astype(BF)
                    dv_acc += _dot_tn(p.astype(BF), dom)
                    dp = _dot_nt(dom, vt)
                    delta = jnp.sum(jnp.where(lm, dlt, 0.0), axis=-1, keepdims=True)
                    ds = (p * (dp - delta) * scale).astype(BF)
                    dq_pair += jnp.where(lm, _dot(ds, kt), 0.0)
                    dk_acc += _dot_tn(ds, qm)
                dq_t[t][mine, :] = dq_pair
                dk_t[t][cur, :] = dk_t[t][cur, :] + dk_acc[CHUNK:]
                dk_t[t][prv, :] = dk_t[t][prv, :] + dk_acc[:CHUNK]
                dv_t[t][cur, :] = dv_t[t][cur, :] + dv_acc[CHUNK:]
                dv_t[t][prv, :] = dv_t[t][prv, :] + dv_acc[:CHUNK]

    q_out = pl.BlockSpec(q_like(0).block_shape, lambda b, n, r: (b * nb + n, 0))
    k_out = pl.BlockSpec((SEQ, 128), lambda b, n, r: (b, 0))
    sh = jax.ShapeDtypeStruct((T, 128), F32)
    tiles = lambda: [q_like(t) for t in range(3)]
    return list(_call(
        body, name=f"attn_bwd_d{dil}", grid=(B, nb, last),
        in_specs=tiles() + [k_like(3 + t) for t in range(3)] + [k_like(6 + t) for t in range(3)]
        + tiles() + tiles() + tiles(),
        out_specs=[q_out] * 3 + [k_out] * 6, out_shape=[sh] * 9,
        sem=("parallel", "arbitrary", "arbitrary"), vmem=ATTN_BWD_VMEM)(*([qkv] * 9 + [do] * 3 + [out] * 3 + [lse] * 3)))


def _sum_branches(parts):
    T = parts[0][0].shape[0]
    tm = _tile(T, 512)

    def body(*refs):
        o_ref = refs[-1]
        for c in range(9):
            acc = refs[c][...] + refs[9 + c][...] + refs[18 + c][...]
            o_ref[:, 128 * c:128 * (c + 1)] = acc.astype(BF)

    tile = pl.BlockSpec((tm, 128), lambda i: (i, 0))
    flat = [a for br in parts for a in br]
    return _call(body, name="attn_sum_branches", grid=(T // tm,), in_specs=[tile] * 27,
                 out_specs=pl.BlockSpec((tm, W_QKV), lambda i: (i, 0)),
                 out_shape=jax.ShapeDtypeStruct((T, W_QKV), BF), sem=("parallel",))(*flat)


def _silu(x):
    return x * _sigmoid(x)


def _dsilu(x):
    s = _sigmoid(x)
    return s * (1.0 + x * (1.0 - s))


def _log1p(u):
    return jnp.where(u < 0.01, u * (1.0 - u * (0.5 - u * (1.0 / 3.0))), jnp.log(1.0 + u))


def _softplus(x):
    return jnp.maximum(x, 0.0) + _log1p(jnp.exp(-jnp.abs(x)))


def _cumsum_rows(x, reverse=False):
    n = x.shape[0]
    rows = lax.broadcasted_iota(jnp.int32, x.shape, 0)
    k = 1
    while k < n:
        if reverse:
            x = x + jnp.where(rows < n - k, pltpu.roll(x, n - k, 0), 0.0)
        else:
            x = x + jnp.where(rows >= k, pltpu.roll(x, k, 0), 0.0)
        k *= 2
    return x


def _tri():
    r = lax.broadcasted_iota(jnp.int32, (CHUNK, CHUNK), 0)
    c = lax.broadcasted_iota(jnp.int32, (CHUNK, CHUNK), 1)
    return r >= c


def _row_mask(e):
    return (lax.broadcasted_iota(jnp.int32, (128, 1), 0) // HEAD) == e


def _first_lane(e):
    return lax.broadcasted_iota(jnp.int32, (1, 128), 1) == HEAD * e


def _ssd_pre(x_ref, halo_ref, first, cw_ref, cb_ref, dtb_ref, al_ref, ext):
    row = x_ref[...]
    z = row[:, SSD_CONV_DIM:SSD_CONV_DIM + SSD_W]
    u = row[:, SSD_CONV_DIM + SSD_W:] + dtb_ref[...]
    ext[0:8, :] = jnp.where(first, 0.0, halo_ref[:, 0:SSD_CONV_DIM])
    ext[8:8 + CHUNK, :] = row[:, 0:SSD_CONV_DIM]
    xc = cb_ref[...]
    for j in range(4):
        xc = xc + cw_ref[j:j + 1, :] * ext[pl.ds(5 + j, CHUNK), :]
    xa = _silu(xc)
    dt = _softplus(u)
    a = dt * (-jnp.exp(al_ref[...]))
    A = _cumsum_rows(a)
    return dict(z=z, u=u, xc=xc, xs=xa[:, 0:SSD_W], Bm=xa[:, SSD_W:SSD_W + 256], Cm=xa[:, SSD_W + 256:],
                dt=dt, a=a, A=A, AT=A.T, eA=jnp.exp(A), wdec=jnp.exp(A[CHUNK - 1:CHUNK, :] - A),
                dtot=jnp.exp(A[CHUNK - 1:CHUNK, :]))


def _ssd_y(p, hp_ref, dskip):
    tri = _tri()
    X = p["xs"] * p["dt"]
    Bb = [p["Bm"][:, 128 * g:128 * (g + 1)].astype(BF) for g in range(2)]
    Cb = [p["Cm"][:, 128 * g:128 * (g + 1)].astype(BF) for g in range(2)]
    CB = [_dot_nt(Cb[g], Bb[g]) for g in range(2)]
    tiles = []
    for t in range(3):
        sl = slice(128 * t, 128 * (t + 1))
        hpb = hp_ref[sl, :].astype(BF)
        acc = jnp.zeros((CHUNK, 128), F32)
        for e in range(2):
            h = 2 * t + e
            g, col = h // 3, HEAD * h
            lm = _lane_mask(e)
            L = jnp.exp(jnp.where(tri, p["A"][:, col:col + 1] - p["AT"][col:col + 1, :], NEG))
            yd = _dot((CB[g] * L).astype(BF), jnp.where(lm, X[:, sl], 0.0).astype(BF))
            yo = _dot_nt(Cb[g], hpb) * p["eA"][:, sl]
            acc = acc + jnp.where(lm, yd + yo, 0.0)
        tiles.append(acc)
    return jnp.concatenate(tiles, axis=1) + dskip * p["xs"], X, Bb, Cb, CB


def _group_stats(v):
    g0 = lax.broadcasted_iota(jnp.int32, (1, SSD_W), 1) < SSD_W // 2
    m0 = jnp.sum(jnp.where(g0, v, 0.0), axis=-1, keepdims=True) * (2.0 / SSD_W)
    m1 = jnp.sum(jnp.where(g0, 0.0, v), axis=-1, keepdims=True) * (2.0 / SSD_W)
    return jnp.where(g0, m0, m1)


def _ssd_specs(T, rev):
    B = T // SEQ
    chunk = (lambda c: N_CHUNK - 1 - c) if rev else (lambda c: c)
    row = pl.BlockSpec((B, CHUNK, W_SSD), lambda c: (0, chunk(c), 0))
    halo = pl.BlockSpec((B, 8, W_SSD), lambda c: (0, jnp.maximum(chunk(c) * (CHUNK // 8) - 1, 0), 0))
    hp = pl.BlockSpec((B, None, SSD_W, SSD_STATE), lambda c: (0, chunk(c), 0, 0))
    y = pl.BlockSpec((B, CHUNK, SSD_W), lambda c: (0, chunk(c), 0))
    const = lambda r, w: pl.BlockSpec((r, w), lambda c: (0, 0))
    params = [const(4, SSD_CONV_DIM), const(1, SSD_CONV_DIM)] + [const(1, SSD_W)] * 4
    return B, row, halo, hp, y, const, params


def _ssd_fwd(sin, conv_w, conv_b, dtb, alog, dskip, norm_g):
    T = sin.shape[0]
    B, row, halo, hp, y, const, params = _ssd_specs(T, False)

    def body(xs_ref, halos_ref, cw_ref, cb_ref, dtb_ref, al_ref, dk_ref, ng_ref, ys_ref, hps_ref, exts, hsts):
        @pl.when(pl.program_id(0) == 0)
        def _():
            hsts[...] = jnp.zeros_like(hsts)

        for b in range(B):
            one(xs_ref.at[b], halos_ref.at[b], cw_ref, cb_ref, dtb_ref, al_ref, dk_ref, ng_ref, ys_ref.at[b],
                hps_ref.at[b], exts.at[b], hsts.at[b])

    def one(x_ref, halo_ref, cw_ref, cb_ref, dtb_ref, al_ref, dk_ref, ng_ref, y_ref, hp_ref, ext, hst):
        c = pl.program_id(0)
        p = _ssd_pre(x_ref, halo_ref, c == 0, cw_ref, cb_ref, dtb_ref, al_ref, ext)
        yv, X, Bb, Cb, CB = _ssd_y(p, hst, dk_ref[...])
        hp_ref[...] = hst[...]
        for t in range(3):
            sl = slice(128 * t, 128 * (t + 1))
            old = hst[sl, :]
            new = old
            for e in range(2):
                h = 2 * t + e
                g, col = h // 3, HEAD * h
                st = _dot_tn(jnp.where(_lane_mask(e), X[:, sl] * p["wdec"][:, sl], 0.0).astype(BF), Bb[g])
                new = jnp.where(_row_mask(e), old * p["dtot"][:, col:col + 1] + st, new)
            hst[sl, :] = new
        y2 = yv * _silu(p["z"])
        r = lax.rsqrt(_group_stats(y2 * y2) + RMS_EPS)
        y_ref[...] = y2 * r * ng_ref[...]

    sin3 = sin.reshape(B, SEQ, W_SSD)
    yo, hprev = _call(
        body, name="ssd_fwd", grid=(N_CHUNK,), in_specs=[row, halo] + params, out_specs=[y, hp],
        out_shape=[jax.ShapeDtypeStruct((B, SEQ, SSD_W), F32),
                   jax.ShapeDtypeStruct((B, N_CHUNK, SSD_W, SSD_STATE), F32)],
        scratch=[pltpu.VMEM((B, 8 + CHUNK, SSD_CONV_DIM), F32), pltpu.VMEM((B, SSD_W, SSD_STATE), F32)],
        sem=("arbitrary",))(sin3, sin3, conv_w, conv_b, dtb, alog, dskip, norm_g)
    return yo.reshape(T, SSD_W), hprev


def _ssd_bwd(sin, hprev, dy3, conv_w, conv_b, dtb, alog, dskip, norm_g):
    T = sin.shape[0]
    B, row, halo, hp, y, const, params = _ssd_specs(T, True)

    def body(xs_ref, halos_ref, hps_ref, dys_ref, cw_ref, cb_ref, dtb_ref, al_ref, dk_ref, ng_ref,
             dxs_ref, dcw_ref, dcb_ref, dvec_ref, exts, ext2s, dhs):
        @pl.when(pl.program_id(0) == 0)
        def _():
            dcw_ref[...] = jnp.zeros_like(dcw_ref)
            dcb_ref[...] = jnp.zeros_like(dcb_ref)
            dvec_ref[...] = jnp.zeros_like(dvec_ref)
            dhs[...] = jnp.zeros_like(dhs)
            ext2s[:, CHUNK:CHUNK + 8, :] = jnp.zeros((B, 8, SSD_CONV_DIM), F32)

        for b in range(B):
            one(xs_ref.at[b], halos_ref.at[b], hps_ref.at[b], dys_ref.at[b], cw_ref, cb_ref, dtb_ref, al_ref, dk_ref,
                ng_ref, dxs_ref.at[b], dcw_ref, dcb_ref, dvec_ref, exts.at[b], ext2s.at[b], dhs.at[b])

    def one(x_ref, halo_ref, hp_ref, dy_ref, cw_ref, cb_ref, dtb_ref, al_ref, dk_ref, ng_ref,
            dx_ref, dcw_ref, dcb_ref, dvec_ref, ext, ext2, dh):
        c = pl.program_id(0)
        p = _ssd_pre(x_ref, halo_ref, c == N_CHUNK - 1, cw_ref, cb_ref, dtb_ref, al_ref, ext)
        dskip_ = dk_ref[...]
        yv, X, Bb, Cb, CB = _ssd_y(p, hp_ref, dskip_)
        xs, z, A, AT = p["xs"], p["z"], p["A"], p["AT"]

        sz = _silu(z)
        y2 = yv * sz
        r = lax.rsqrt(_group_stats(y2 * y2) + RMS_EPS)
        dy3_ = dy_ref[...]
        uu = dy3_ * ng_ref[...]
        dy2 = r * (uu - y2 * (r * r * _group_stats(uu * y2)))
        dy = dy2 * sz
        dz = dy2 * yv * _dsilu(z)

        tri = _tri()
        rows = lax.broadcasted_iota(jnp.int32, (CHUNK, 1), 0)
        dG = [jnp.zeros((CHUNK, CHUNK), F32) for _ in range(2)]
        dB = [jnp.zeros((CHUNK, SSD_STATE), F32) for _ in range(2)]
        dC = [jnp.zeros((CHUNK, SSD_STATE), F32) for _ in range(2)]
        dX_t, dA_t, ddtx_t = [], [], []
        for t in range(3):
            sl = slice(128 * t, 128 * (t + 1))
            hp_t = hp_ref[sl, :]
            hpb = hp_t.astype(BF)
            dhc = dh[sl, :]
            dh_new = jnp.zeros((128, SSD_STATE), F32)
            dX = jnp.zeros((CHUNK, 128), F32)
            dA = jnp.zeros((CHUNK, 128), F32)
            ddtx = jnp.zeros((CHUNK, 128), F32)
            for e in range(2):
                h = 2 * t + e
                g, col = h // 3, HEAD * h
                lm, rm, fl = _lane_mask(e), _row_mask(e), _first_lane(e)
                L = jnp.exp(jnp.where(tri, A[:, col:col + 1] - AT[col:col + 1, :], NEG))
                Mf = CB[g] * L
                Xm = jnp.where(lm, X[:, sl], 0.0)
                Xmb = Xm.astype(BF)
                dyh = jnp.where(lm, dy[:, sl], 0.0)
                dyb = dyh.astype(BF)
                dXh = _dot_tn(Mf.astype(BF), dyb)
                dM = jnp.where(tri, _dot_nt(dyb, Xmb), 0.0)
                Wm = dM * Mf
                dAc = jnp.sum(Wm, axis=-1, keepdims=True) - jnp.sum(Wm.T, axis=-1, keepdims=True)
                dG[g] = dG[g] + dM * L
                eAt = p["eA"][:, sl]
                yo = _dot_nt(Cb[g], hpb)
                dyo = (dyh * eAt).astype(BF)
                dC[g] = dC[g] + _dot(dyo, hpb)
                dh_new = dh_new + _dot_tn(dyo, Cb[g])
                dAc = dAc + jnp.sum(dyh * yo * eAt, axis=-1, keepdims=True)
                dHn = jnp.where(rm, dhc, 0.0)
                dHnb = dHn.astype(BF)
                dec = p["dtot"][:, col:col + 1]
                dh_new = dh_new + dec * dHn
                Z = _dot_nt(Bb[g], dHnb)
                wt = p["wdec"][:, sl]
                xi = jnp.sum(Xm * Z, axis=-1, keepdims=True) * p["wdec"][:, col:col + 1]
                dXh = dXh + wt * Z
                dB[g] = dB[g] + _dot(jnp.where(lm, X[:, sl] * wt, 0.0).astype(BF), dHnb)
                dAtot = jnp.sum(xi, axis=0, keepdims=True) + dec * jnp.sum(
                    jnp.sum(dHn * hp_t, axis=-1, keepdims=True), axis=0, keepdims=True)
                dAc = dAc - xi + jnp.where(rows == CHUNK - 1, dAtot, 0.0)
                dA = dA + jnp.where(fl, dAc, 0.0)
                dX = dX + dXh
                ddtx = ddtx + jnp.where(fl, jnp.sum(dXh * xs[:, sl], axis=-1, keepdims=True), 0.0)
            dh[sl, :] = dh_new
            dX_t.append(dX)
            dA_t.append(dA)
            ddtx_t.append(ddtx)
        for g in range(2):
            dGb = dG[g].astype(BF)
            dC[g] = dC[g] + _dot(dGb, Bb[g])
            dB[g] = dB[g] + _dot_tn(dGb, Cb[g])
        dXf = jnp.concatenate(dX_t, axis=1)
        da = _cumsum_rows(jnp.concatenate(dA_t, axis=1), reverse=True)
        ddt = da * (-jnp.exp(al_ref[...])) + jnp.concatenate(ddtx_t, axis=1)
        du = ddt * _sigmoid(p["u"])
        dxs = dXf * p["dt"] + dskip_ * dy
        dxc = jnp.concatenate([dxs, dB[0], dB[1], dC[0], dC[1]], axis=1) * _dsilu(p["xc"])
        ext2[0:CHUNK, :] = dxc
        dxbc = jnp.zeros((CHUNK, SSD_CONV_DIM), F32)
        for j in range(4):
            dxbc = dxbc + cw_ref[j:j + 1, :] * ext2[pl.ds(3 - j, CHUNK), :]
            dcw_ref[j:j + 1, :] += jnp.sum(dxc * ext[pl.ds(5 + j, CHUNK), :], axis=0, keepdims=True)
        ext2[CHUNK:CHUNK + 8, :] = dxc[0:8, :]
        dcb_ref[...] += jnp.sum(dxc, axis=0, keepdims=True)
        dvec_ref[0:1, :] += jnp.sum(du, axis=0, keepdims=True)
        dvec_ref[1:2, :] += jnp.sum(da * p["a"], axis=0, keepdims=True)
        dvec_ref[2:3, :] += jnp.sum(dy * xs, axis=0, keepdims=True)
        dvec_ref[3:4, :] += jnp.sum(dy3_ * y2 * r, axis=0, keepdims=True)
        dx_ref[...] = jnp.concatenate([dxbc, dz, du], axis=1).astype(BF)

    sin3 = sin.reshape(B, SEQ, W_SSD)
    out = _call(body, name="ssd_bwd", grid=(N_CHUNK,), in_specs=[row, halo, hp, y] + params,
                out_specs=[row, const(4, SSD_CONV_DIM), const(1, SSD_CONV_DIM), const(8, SSD_W)],
                out_shape=[jax.ShapeDtypeStruct((B, SEQ, W_SSD), BF), jax.ShapeDtypeStruct((4, SSD_CONV_DIM), F32),
                           jax.ShapeDtypeStruct((1, SSD_CONV_DIM), F32), jax.ShapeDtypeStruct((8, SSD_W), F32)],
                scratch=[pltpu.VMEM((B, 8 + CHUNK, SSD_CONV_DIM), F32), pltpu.VMEM((B, 8 + CHUNK, SSD_CONV_DIM), F32),
                         pltpu.VMEM((B, SSD_W, SSD_STATE), F32)],
                sem=("arbitrary",))(sin3, sin3, hprev, dy3.reshape(B, SEQ, SSD_W), conv_w, conv_b, dtb, alog, dskip,
                                    norm_g)
    return (out[0].reshape(T, W_SSD),) + tuple(out[1:])


def _sgu_core(uv_ref, g_ref, b_ref, w_ref, bias_ref):
    x = uv_ref[...]
    cdf = 0.5 * (1.0 + lax.erf(x * (2.0 ** -0.5)))
    ge = x * cdf
    dge = cdf + x * jnp.exp(-0.5 * x * x) * ((2.0 * math.pi) ** -0.5)
    u, v = ge[:, 0:SGU_W], ge[:, SGU_W:]
    vc = v - jnp.mean(v, axis=-1, keepdims=True)
    rstd = lax.rsqrt(jnp.mean(vc * vc, axis=-1, keepdims=True) + LN_EPS)
    vhat = vc * rstd
    vn = vhat * g_ref[...] + b_ref[...]
    tri = _tri()
    wc = [jnp.where(tri, w_ref[gi], 0.0).astype(BF) for gi in range(4)]
    vm = [jnp.where(_lane_mask(gi % 2), vn[:, 128 * (gi // 2):128 * (gi // 2 + 1)], 0.0).astype(BF) for gi in range(4)]
    mixed = jnp.concatenate([_dot(wc[2 * t], vm[2 * t]) + _dot(wc[2 * t + 1], vm[2 * t + 1]) for t in range(2)],
                            axis=1) + bias_ref[...]
    return dict(dge=dge, u=u, rstd=rstd, vhat=vhat, wc=wc, vm=vm, mixed=mixed)


def _sgu_specs():
    vec = pl.BlockSpec((1, SGU_W), lambda i: (0, 0))
    return [pl.BlockSpec((CHUNK, W_UV), lambda i: (i, 0)), vec, vec,
            pl.BlockSpec((4, CHUNK, CHUNK), lambda i: (0, 0, 0)), pl.BlockSpec((CHUNK, SGU_W), lambda i: (0, 0))]


def _sgu_fwd(uv, ln_g, ln_b, w, bias):
    T = uv.shape[0]

    def body(uv_ref, g_ref, b_ref, w_ref, bias_ref, y_ref):
        s = _sgu_core(uv_ref, g_ref, b_ref, w_ref, bias_ref)
        y_ref[...] = s["u"] * s["mixed"]

    return _call(body, name="sgu_fwd", grid=(T // CHUNK,), in_specs=_sgu_specs(),
                 out_specs=pl.BlockSpec((CHUNK, SGU_W), lambda i: (i, 0)),
                 out_shape=jax.ShapeDtypeStruct((T, SGU_W), F32), sem=("parallel",))(uv, ln_g, ln_b, w, bias)


def _sgu_bwd(uv, dy, ln_g, ln_b, w, bias):
    T = uv.shape[0]

    def body(uv_ref, dy_ref, g_ref, b_ref, w_ref, bias_ref, dx_ref, dw_ref, dbias_ref, dln_ref):
        @pl.when(pl.program_id(0) == 0)
        def _():
            dw_ref[...] = jnp.zeros_like(dw_ref)
            dbias_ref[...] = jnp.zeros_like(dbias_ref)
            dln_ref[...] = jnp.zeros_like(dln_ref)

        s = _sgu_core(uv_ref, g_ref, b_ref, w_ref, bias_ref)
        dy_ = dy_ref[...]
        du = dy_ * s["mixed"]
        dmix = dy_ * s["u"]
        dbias_ref[...] += dmix
        tri = _tri()
        dvn_t = []
        for t in range(2):
            acc = jnp.zeros((CHUNK, 128), F32)
            for e in range(2):
                gi = 2 * t + e
                dmg = jnp.where(_lane_mask(e), dmix[:, 128 * t:128 * (t + 1)], 0.0).astype(BF)
                acc = acc + _dot_tn(s["wc"][gi], dmg)
                dw_ref[gi] += jnp.where(tri, _dot_nt(dmg, s["vm"][gi]), 0.0)
            dvn_t.append(acc)
        dvn = jnp.concatenate(dvn_t, axis=1)
        dln_ref[0:1, :] += jnp.sum(dvn * s["vhat"], axis=0, keepdims=True)
        dln_ref[1:2, :] += jnp.sum(dvn, axis=0, keepdims=True)
        dvh = dvn * g_ref[...]
        dv = s["rstd"] * (dvh - jnp.mean(dvh, axis=-1, keepdims=True)
                          - s["vhat"] * jnp.mean(dvh * s["vhat"], axis=-1, keepdims=True))
        dx_ref[...] = (jnp.concatenate([du, dv], axis=1) * s["dge"]).astype(BF)

    ins = _sgu_specs()
    return _call(body, name="sgu_bwd", grid=(T // CHUNK,),
                 in_specs=[ins[0], pl.BlockSpec((CHUNK, SGU_W), lambda i: (i, 0))] + ins[1:],
                 out_specs=[pl.BlockSpec((CHUNK, W_UV), lambda i: (i, 0)),
                            pl.BlockSpec((4, CHUNK, CHUNK), lambda i: (0, 0, 0)),
                            pl.BlockSpec((CHUNK, SGU_W), lambda i: (0, 0)), pl.BlockSpec((8, SGU_W), lambda i: (0, 0))],
                 out_shape=[jax.ShapeDtypeStruct((T, W_UV), BF), jax.ShapeDtypeStruct((4, CHUNK, CHUNK), F32),
                            jax.ShapeDtypeStruct((CHUNK, SGU_W), F32), jax.ShapeDtypeStruct((8, SGU_W), F32)],
                 sem=("arbitrary",))(uv, dy, ln_g, ln_b, w, bias)


def _adamw(w, g, m, v):
    R, C = w.shape
    tr = R

    def body(w_ref, g_ref, m_ref, v_ref, d_ref, nm_ref, nv_ref):
        g_ = g_ref[...]
        m2 = ADAM_B1 * m_ref[...] + (1.0 - ADAM_B1) * g_
        v2 = ADAM_B2 * v_ref[...] + (1.0 - ADAM_B2) * (g_ * g_)
        m_hat = m2 / (1.0 - ADAM_B1 ** ADAM_STEP)
        v_hat = v2 / (1.0 - ADAM_B2 ** ADAM_STEP)
        d_ref[...] = -ADAM_LR * (m_hat / (jnp.sqrt(v_hat) + ADAM_EPS) + ADAM_WD * w_ref[...])
        nm_ref[...] = m2
        nv_ref[...] = v2

    blk = pl.BlockSpec((tr, C), lambda i: (i, 0))
    sh = jax.ShapeDtypeStruct((R, C), F32)
    return _call(body, name="adamw", grid=(R // tr,), in_specs=[blk] * 4, out_specs=[blk] * 3,
                 out_shape=[sh] * 3, sem=("parallel",))(w, g, m, v)


def _adamw_pair(w, g0, g1, m, v, dep):
    L, R, C = w.shape
    tr = max(t for t in range(8, R + 1, 8) if R % t == 0 and t * C * 4 <= 3 * 2 ** 19)

    def body(w_ref, g0_ref, g1_ref, m_ref, v_ref, dep_ref, d_ref, nm_ref, nv_ref, og_ref):
        g_ = jnp.where(pl.program_id(0) == 0, g0_ref[...], g1_ref[...])
        m2 = ADAM_B1 * m_ref[...] + (1.0 - ADAM_B1) * g_
        v2 = ADAM_B2 * v_ref[...] + (1.0 - ADAM_B2) * (g_ * g_)
        m_hat = m2 / (1.0 - ADAM_B1 ** ADAM_STEP)
        v_hat = v2 / (1.0 - ADAM_B2 ** ADAM_STEP)
        d_ref[...] = -ADAM_LR * (m_hat / (jnp.sqrt(v_hat) + ADAM_EPS) + ADAM_WD * w_ref[...])
        nm_ref[...] = m2
        nv_ref[...] = v2
        og_ref[...] = g_

    lay = pl.BlockSpec((None, tr, C), lambda l, i: (l, i, 0))
    one = lambda k: pl.BlockSpec((tr, C), lambda l, i: (jnp.where(l == k, i, 0), 0))
    return _call(body, name="adamw_pair", grid=(L, R // tr),
                 in_specs=[lay, one(0), one(1), lay, lay, pl.BlockSpec((8, 128), lambda l, i: (0, 0))],
                 out_specs=[lay] * 4,
                 out_shape=[jax.ShapeDtypeStruct((L, R, C), F32)] * 4,
                 sem=("parallel", "parallel"))(w, g0, g1, m, v, dep)


def _row_steps(rows):
    return 2 if rows % 32 == 0 else 1


def _pair_add(gbufs, rsibs, c):
    n = len(gbufs)
    steps = min(_row_steps(g.shape[2]) for g in gbufs)

    def body(c_ref, *refs):
        for a_ref, b_ref, o_ref in zip(refs[:n], refs[n:2 * n], refs[2 * n:]):
            o_ref[...] = (a_ref[...] + b_ref[...]).astype(BF)

    def specs(g):
        tr, C = g.shape[2] // steps, g.shape[3]
        return (pl.BlockSpec((None, None, tr, C), lambda j, i, c_ref: (j, c_ref[0], i, 0)),
                pl.BlockSpec((None, tr, C), lambda j, i, c_ref: (j, i, 0)))

    return list(pl.pallas_call(
        body, name="rs_pair_add",
        grid_spec=pltpu.PrefetchScalarGridSpec(
            num_scalar_prefetch=1, grid=(4, steps),
            in_specs=[specs(g)[0] for g in gbufs] + [specs(g)[1] for g in gbufs],
            out_specs=[specs(g)[1] for g in gbufs]),
        out_shape=[jax.ShapeDtypeStruct((4,) + g.shape[2:], BF) for g in gbufs],
        compiler_params=pltpu.CompilerParams(dimension_semantics=("parallel", "parallel")),
    )(jnp.reshape(c, (1,)).astype(jnp.int32), *gbufs, *rsibs))


def _chip_sum(pairs, recvs, me, c):
    n = len(pairs)
    steps = min(_row_steps(p.shape[1]) for p in pairs)

    def body(s_ref, *refs):
        for own_ref, p_ref, o_ref in zip(refs[:n], refs[n:2 * n], refs[2 * n:]):
            p = [jnp.where(s_ref[0] == j, own_ref[...], p_ref[j]).astype(F32) for j in range(4)]
            o_ref[...] = ((p[0] + p[1]) + p[2]) + p[3]

    def specs(p):
        tr, C = p.shape[1] // steps, p.shape[2]
        return (pl.BlockSpec((None, tr, C), lambda i, s: (s[0], i, 0)), pl.BlockSpec((4, tr, C), lambda i, s: (0, i, 0)),
                pl.BlockSpec((None, tr, C), lambda i, s: (s[1], i, 0)))

    return list(pl.pallas_call(
        body, name="rs_chip_sum",
        grid_spec=pltpu.PrefetchScalarGridSpec(
            num_scalar_prefetch=1, grid=(steps,),
            in_specs=[specs(p)[0] for p in pairs] + [specs(p)[1] for p in pairs],
            out_specs=[specs(p)[2] for p in pairs]),
        out_shape=[jax.ShapeDtypeStruct((2,) + p.shape[1:], F32) for p in pairs],
        compiler_params=pltpu.CompilerParams(dimension_semantics=("parallel",)),
    )(jnp.stack([me, c]).astype(jnp.int32), *pairs, *recvs))


MESH = pl.DeviceIdType.MESH
ANY = pl.BlockSpec(memory_space=pl.ANY)


def _place():
    x, y, c = lax.axis_index("x"), lax.axis_index("y"), lax.axis_index("c")
    return x, y, c, [(1 - x, y), (x, 1 - y), (1 - x, 1 - y)]


HBM = pl.BlockSpec(memory_space=pltpu.HBM)
SEM = pl.BlockSpec(memory_space=pltpu.SEMAPHORE)
EFFECT = pltpu.SideEffectType.DATAFLOW_SIDE_EFFECTING


class _Split:
    def __init__(self, tag, arrays, copies, n_copies, after=()):
        self.tag, self.copies, k = tag, copies, len(arrays)

        def body(*refs):
            sems = k + len(after)
            for cp in copies(refs[:k], refs[sems], refs[sems + 1]):
                cp.start()
            refs[-1][...] = jnp.zeros_like(refs[-1])

        out = pl.pallas_call(
            body, name=tag + "_start",
            out_shape=(pltpu.SemaphoreType.DMA((n_copies,)), pltpu.SemaphoreType.DMA((n_copies,)),
                       *[pltpu.HBM(a.shape, a.dtype) for a in arrays], jax.ShapeDtypeStruct((8, 128), F32)),
            in_specs=[HBM] * k + [ANY] * len(after),
            out_specs=(SEM, SEM, *[HBM] * k, pl.BlockSpec(memory_space=pltpu.VMEM)),
            input_output_aliases={i: 2 + i for i in range(k)},
            compiler_params=pltpu.CompilerParams(has_side_effects=EFFECT),
        )(*[pltpu.with_memory_space_constraint(a, pltpu.HBM) for a in arrays], *after)
        self.send, self.recv, self.arrays, self.token_array = out[0], out[1], list(out[2:2 + k]), out[-1]
        self.token = self.token_array[0, 0]

    def wait(self, after):
        k, copies = len(self.arrays), self.copies
        after = list(after) if isinstance(after, (list, tuple)) else [after]

        def body(*refs):
            for cp in copies(refs[:k], refs[k], refs[k + 1]):
                cp.wait_send()
                cp.wait_recv()

        return list(pl.pallas_call(
            body, name=self.tag + "_wait", out_shape=tuple(pltpu.HBM(a.shape, a.dtype) for a in self.arrays),
            in_specs=[HBM] * k + [SEM, SEM] + [ANY] * len(after), out_specs=tuple([HBM] * k),
            input_output_aliases={i: i for i in range(k)},
            compiler_params=pltpu.CompilerParams(has_side_effects=EFFECT),
        )(*self.arrays, self.send, self.recv, *after))


def _landing_zones(arrs):
    me = 2 * lax.axis_index("x") + lax.axis_index("y")
    return [lax.dynamic_update_index_in_dim(lax.empty((4,) + a.shape, a.dtype), a, me, 0) for a in arrs]


def _gather_start(arrs, lands, tag, after=()):
    n = len(arrs)

    def copies(refs, send, recv):
        x, y, c, chips = _place()
        return [pltpu.make_async_remote_copy(
            src_ref=refs[k], dst_ref=refs[n + k].at[2 * x + y], send_sem=send.at[3 * k + r],
            recv_sem=recv.at[3 * k + r], device_id=(px, py, c), device_id_type=MESH)
            for k in range(n) for r, (px, py) in enumerate(chips)]

    return _Split("gather_" + tag, list(arrs) + lands, copies, 3 * n, after)


def _gather_halves_start(arrs, tag):
    n = len(arrs)
    lands = _landing_zones(arrs)

    def copies(refs, send, recv):
        x, y, c, chips = _place()
        return [pltpu.make_async_remote_copy(
            src_ref=refs[k].at[c], dst_ref=refs[n + k].at[2 * x + y, c], send_sem=send.at[3 * k + r],
            recv_sem=recv.at[3 * k + r], device_id=(px, py, c), device_id_type=MESH)
            for k in range(n) for r, (px, py) in enumerate(chips)]

    return _Split("gather_" + tag, list(arrs) + lands, copies, 3 * n)


def _gather_halves_finish(lands, tag):
    n = len(lands)

    def copies(refs, send, recv):
        x, y, c, chips = _place()
        return [pltpu.make_async_remote_copy(
            src_ref=refs[k].at[2 * px + py, c], dst_ref=refs[k].at[2 * px + py, c], send_sem=send.at[3 * k + r],
            recv_sem=recv.at[3 * k + r], device_id=(x, y, 1 - c), device_id_type=MESH)
            for k in range(n) for r, (px, py) in enumerate(chips)]

    return _Split("gather_pass_" + tag, list(lands), copies, 3 * n)


def _part_sibling(gbufs):
    n = len(gbufs)

    def copies(refs, send, recv, off):
        x, y, c, _ = _place()
        return [pltpu.make_async_remote_copy(
            src_ref=refs[k].at[j, 1 - c], dst_ref=refs[n + k].at[j], send_sem=send.at[off + 4 * k + j],
            recv_sem=recv.at[off + 4 * k + j], device_id=(x, y, 1 - c), device_id_type=MESH)
            for k in range(n) for j in range(4)]

    return list(gbufs) + [lax.empty((4,) + g.shape[2:], g.dtype) for g in gbufs], 4 * n, copies


def _part_chips(pbufs):
    n = len(pbufs)

    def copies(refs, send, recv, off):
        x, y, c, chips = _place()
        return [pltpu.make_async_remote_copy(
            src_ref=refs[k].at[2 * px + py], dst_ref=refs[n + k].at[2 * x + y], send_sem=send.at[off + 3 * k + r],
            recv_sem=recv.at[off + 3 * k + r], device_id=(px, py, c), device_id_type=MESH)
            for k in range(n) for r, (px, py) in enumerate(chips)]

    return list(pbufs) + [lax.empty(p.shape, p.dtype) for p in pbufs], 3 * n, copies


def _part_join(fulls):
    def copies(refs, send, recv, off):
        x, y, c, _ = _place()
        return [pltpu.make_async_remote_copy(
            src_ref=refs[k].at[c], dst_ref=refs[k].at[c], send_sem=send.at[off + k], recv_sem=recv.at[off + k],
            device_id=(x, y, 1 - c), device_id_type=MESH) for k in range(len(fulls))]

    return list(fulls), len(fulls), copies


def _start_parts(parts, tag):
    arrays, spans, total = [], [], 0
    for arrs, n_copies, fn in parts:
        spans.append((len(arrays), len(arrs), total, fn))
        arrays += arrs
        total += n_copies

    def copies(refs, send, recv):
        return [cp for a0, na, off, fn in spans for cp in fn(refs[a0:a0 + na], send, recv, off)]

    op = _Split(tag, arrays, copies, total)
    op.spans = [(a0, na) for a0, na, _, _ in spans]
    return op


def _all_reduce_small(v):
    R, C = v.shape

    def body(v_ref, o_ref, g_ref, send, recv, loc):
        x, y, c, chips = _place()
        me, sibling = (x, y, c), (x, y, 1 - c)

        def rows(px, py, pc):
            return g_ref.at[4 * px + 2 * py + pc]

        def copy(k, block, to, src=None):
            return pltpu.make_async_remote_copy(
                src_ref=rows(*block) if src is None else src, dst_ref=rows(*block),
                send_sem=send.at[k], recv_sem=recv.at[k], device_id=to, device_id_type=MESH)

        mine = pltpu.make_async_copy(v_ref, rows(*me), loc)
        mine.start()
        first = [copy(0, me, sibling, src=v_ref)]
        first += [copy(1 + j, me, (*chip, c), src=v_ref) for j, chip in enumerate(chips)]
        for cp in first:
            cp.start()
        passed = [copy(4 + j, (*chip, c), sibling) for j, chip in enumerate(chips)]
        for j, chip in enumerate(chips):
            copy(1 + j, (*chip, c), me).wait_recv()
            passed[j].start()
        copy(0, sibling, me).wait_recv()
        for j, chip in enumerate(chips):
            copy(4 + j, (*chip, 1 - c), me).wait_recv()
        for cp in first + passed:
            cp.wait_send()
        mine.wait()
        acc = g_ref[0]
        for d in range(1, 8):
            acc = acc + g_ref[d]
        o_ref[...] = acc

    vm = pl.BlockSpec(memory_space=pltpu.VMEM)
    return pl.pallas_call(
        body, name="all_reduce_small", in_specs=[vm], out_specs=[vm, vm],
        out_shape=[jax.ShapeDtypeStruct((R, C), F32), jax.ShapeDtypeStruct((8, R, C), F32)],
        scratch_shapes=[pltpu.SemaphoreType.DMA((7,)), pltpu.SemaphoreType.DMA((7,)), pltpu.SemaphoreType.DMA],
    )(v)[0]


WEIGHTS = ['ffn1_norm', 'ffn1_w_gate', 'ffn1_w_up', 'ffn1_w_down', 'mix_norm', 'w_in', 'conv_w', 'conv_b', 'dt_bias',
           'a_log', 'd_skip', 'ssd_norm', 'sgu_ln_g', 'sgu_ln_b', 'sgu_w', 'sgu_b', 'w_out', 'ffn2_norm',
           'ffn2_w_gate', 'ffn2_w_up', 'ffn2_w_down', 'final_norm']
SHARDED = ['ffn1_w_gate', 'ffn1_w_up', 'ffn1_w_down', 'w_in', 'conv_w', 'w_out', 'ffn2_w_gate', 'ffn2_w_up',
           'ffn2_w_down']
SMALL = [n for n in WEIGHTS if n not in SHARDED]
GROUPS = [("ffn1", ["ffn1_w_gate", "ffn1_w_up", "ffn1_w_down"]), ("mix", ["w_in", "conv_w", "w_out"]),
          ("ffn2", ["ffn2_w_gate", "ffn2_w_up", "ffn2_w_down"])]
TRANSPOSED = ("ffn1_w_gate", "ffn1_w_up", "ffn2_w_gate", "ffn2_w_up")
DEPTH = 2


def _pack_w_in(w):
    return jnp.concatenate([w[..., 0:1152], w[..., 1536:2432], w[..., 1152:1536],
                            jnp.repeat(w[..., 2432:2438], HEAD, axis=-1), w[..., 2438:2950]], axis=-1)


def _unpack_w_in(dq, ds, du):
    return jnp.concatenate([dq, ds[:, 896:1280], ds[:, 0:896], ds[:, 1280::HEAD], du], axis=-1)


def _ffn_fwd(x, g, wg, wu, wd):
    xo, hb, S1, S2, A = _ffn_fwd_k(x, g, wg, wu, wd)
    return xo, (x, hb, S1, S2, A)


def _ffn_bwd_weights(dxo, saved, wd):
    x, hb, S1, S2, A = saved
    dG, dU, dyb = _ffn_bwd_act(dxo, S1, S2, wd)
    return (dG, dU), _ffn_bwd_k2(hb, dyb, A, dG, dU)


def _ffn_bwd_input(dxo, saved, mids, g, wg, wu):
    return _ffn_bwd_dx(mids[0], mids[1], wg, wu, saved[0], g, dxo)


def _mix_fwd(x, P):
    hb, qkv, sin, uv = _mix_proj(x, P["mix_norm"], P["w_in"])
    y_att, lse = _attn_combine([_attn_fwd(qkv, d) for d in DILATIONS])
    y_ssd, hprev = _ssd_fwd(sin, *P["ssd"])
    y_sgu = _sgu_fwd(uv, *P["sgu"])
    ycat = jnp.concatenate([y_att, y_ssd, y_sgu], axis=1).astype(BF)
    return _mm_nn(ycat, P["w_out"], res=x), (x, hb, qkv, sin, uv, y_att, lse, hprev, ycat)


def _mix_bwd_weights(dxo, saved, P):
    x, hb, qkv, sin, uv, y_att, lse, hprev, ycat = saved
    dy_att, dy_ssd, dy_sgu = _mix_bwd_dy(dxo, P["w_out"])
    dwout = _mm_tn(ycat, dxo)
    dqkv = _sum_branches([_attn_bwd(qkv, dy_att, y_att, lse, d) for d in DILATIONS])
    dsin, dcw, dcb, dvec = _ssd_bwd(sin, hprev, dy_ssd, *P["ssd"])
    duv, dsw, dsbias, dln = _sgu_bwd(uv, dy_sgu, *P["sgu"])
    dwin = _unpack_w_in(_mm_tn(hb, dqkv), _mm_tn(hb, dsin), _mm_tn(hb, duv))
    grads = dict(
        w_in=dwin, conv_w=dcw, conv_b=dcb[0], dt_bias=dvec[0, ::HEAD], a_log=dvec[1, ::HEAD],
        d_skip=jnp.sum(dvec[2].reshape(6, HEAD), axis=-1), ssd_norm=dvec[3], sgu_ln_g=dln[0], sgu_ln_b=dln[1],
        sgu_w=dsw, sgu_b=jnp.sum(dsbias.reshape(CHUNK, 4, HEAD), axis=-1).T, w_out=dwout)
    return (dqkv, dsin, duv), grads


def _mix_bwd_input(dxo, saved, mids, P):
    return _mix_bwd_dx(*mids, P["w_in"], saved[0], P["mix_norm"], dxo)


def _halved(g):
    rows = g.size // g.shape[-1]
    return g.reshape(4, 2, rows // 8, g.shape[-1])


def kernel(x, ffn1_norm, ffn1_w_gate, ffn1_w_up, ffn1_w_down, mix_norm, w_in, conv_w, conv_b, dt_bias, a_log, d_skip, ssd_norm, sgu_ln_g, sgu_ln_b, sgu_w, sgu_b, w_out, ffn2_norm, ffn2_w_gate, ffn2_w_up, ffn2_w_down, final_norm, loss_target, m_ffn1_norm, m_ffn1_w_gate, m_ffn1_w_up, m_ffn1_w_down, m_mix_norm, m_w_in, m_conv_w, m_conv_b, m_dt_bias, m_a_log, m_d_skip, m_ssd_norm, m_sgu_ln_g, m_sgu_ln_b, m_sgu_w, m_sgu_b, m_w_out, m_ffn2_norm, m_ffn2_w_gate, m_ffn2_w_up, m_ffn2_w_down, m_final_norm, v_ffn1_norm, v_ffn1_w_gate, v_ffn1_w_up, v_ffn1_w_down, v_mix_norm, v_w_in, v_conv_w, v_conv_b, v_dt_bias, v_a_log, v_d_skip, v_ssd_norm, v_sgu_ln_g, v_sgu_ln_b, v_sgu_w, v_sgu_b, v_w_out, v_ffn2_norm, v_ffn2_w_gate, v_ffn2_w_up, v_ffn2_w_down, v_final_norm):
    given = dict(x=x, ffn1_norm=ffn1_norm, ffn1_w_gate=ffn1_w_gate, ffn1_w_up=ffn1_w_up, ffn1_w_down=ffn1_w_down, mix_norm=mix_norm, w_in=w_in, conv_w=conv_w, conv_b=conv_b, dt_bias=dt_bias, a_log=a_log, d_skip=d_skip, ssd_norm=ssd_norm, sgu_ln_g=sgu_ln_g, sgu_ln_b=sgu_ln_b, sgu_w=sgu_w, sgu_b=sgu_b, w_out=w_out, ffn2_norm=ffn2_norm, ffn2_w_gate=ffn2_w_gate, ffn2_w_up=ffn2_w_up, ffn2_w_down=ffn2_w_down, final_norm=final_norm, loss_target=loss_target, m_ffn1_norm=m_ffn1_norm, m_ffn1_w_gate=m_ffn1_w_gate, m_ffn1_w_up=m_ffn1_w_up, m_ffn1_w_down=m_ffn1_w_down, m_mix_norm=m_mix_norm, m_w_in=m_w_in, m_conv_w=m_conv_w, m_conv_b=m_conv_b, m_dt_bias=m_dt_bias, m_a_log=m_a_log, m_d_skip=m_d_skip, m_ssd_norm=m_ssd_norm, m_sgu_ln_g=m_sgu_ln_g, m_sgu_ln_b=m_sgu_ln_b, m_sgu_w=m_sgu_w, m_sgu_b=m_sgu_b, m_w_out=m_w_out, m_ffn2_norm=m_ffn2_norm, m_ffn2_w_gate=m_ffn2_w_gate, m_ffn2_w_up=m_ffn2_w_up, m_ffn2_w_down=m_ffn2_w_down, m_final_norm=m_final_norm, v_ffn1_norm=v_ffn1_norm, v_ffn1_w_gate=v_ffn1_w_gate, v_ffn1_w_up=v_ffn1_w_up, v_ffn1_w_down=v_ffn1_w_down, v_mix_norm=v_mix_norm, v_w_in=v_w_in, v_conv_w=v_conv_w, v_conv_b=v_conv_b, v_dt_bias=v_dt_bias, v_a_log=v_a_log, v_d_skip=v_d_skip, v_ssd_norm=v_ssd_norm, v_sgu_ln_g=v_sgu_ln_g, v_sgu_ln_b=v_sgu_ln_b, v_sgu_w=v_sgu_w, v_sgu_b=v_sgu_b, v_w_out=v_w_out, v_ffn2_norm=v_ffn2_norm, v_ffn2_w_gate=v_ffn2_w_gate, v_ffn2_w_up=v_ffn2_w_up, v_ffn2_w_down=v_ffn2_w_down, v_final_norm=v_final_norm)
    T = given["x"].shape[0] * given["x"].shape[1]
    D = given["x"].shape[2]
    x0 = given["x"].reshape(T, D)
    tgt = given["loss_target"].reshape(T, D)
    c = lax.axis_index("c")

    bf = {n: given[n].astype(BF) for n in SHARDED if n not in ("w_in", "conv_w")}
    bf["w_in"] = _pack_w_in(given["w_in"]).astype(BF)
    bf["conv_w"] = given["conv_w"]
    first_key = (0, GROUPS[0][0])
    first = [bf[n][0].reshape((2, bf[n].shape[1] // 2) + bf[n].shape[2:]) for n in GROUPS[0][1]]
    gathers = {first_key: _gather_halves_start(first, "l0_" + GROUPS[0][0])}
    later = {(i, gname): [bf[n][i] for n in names] for i in range(DEPTH) for gname, names in GROUPS
             if (i, gname) != first_key}
    zones = {key: _landing_zones(arrs) for key, arrs in later.items()}

    def gathered(i, gname, after):
        if (i, gname) != first_key:
            return gathers[(i, gname)].wait(after)[3:]
        got = gathers[first_key].wait([after] + [z for zs in zones.values() for z in zs])[3:]
        got = _gather_halves_finish(got, "l0_" + gname).wait(after)
        prev = got[0]
        for key, arrs in later.items():
            gathers[key] = _gather_start(arrs, zones[key], f"l{key[0]}_{key[1]}", after=[prev])
            prev = gathers[key].token_array
        return [z.reshape((4, 2 * z.shape[2]) + z.shape[3:]) for z in got]

    def mix_params(i, got):
        win = got[0].reshape(D, W_QKV + W_SSD + W_UV)
        rep = lambda v: jnp.repeat(v, HEAD)[None]
        ssd = (got[1].transpose(1, 0, 2).reshape(4, SSD_CONV_DIM), given["conv_b"][i][None],
               rep(given["dt_bias"][i]), rep(given["a_log"][i]), rep(given["d_skip"][i]), given["ssd_norm"][i][None])
        sgu = (given["sgu_ln_g"][i][None], given["sgu_ln_b"][i][None], given["sgu_w"][i],
               jnp.repeat(given["sgu_b"][i].T, HEAD, axis=1))
        return dict(mix_norm=given["mix_norm"][i][None], w_in=win, w_out=got[2].reshape(-1, D), ssd=ssd, sgu=sgu)

    x = x0
    tape = []
    for i in range(DEPTH):
        got = gathered(i, "ffn1", x)
        token = functools.reduce(lambda a, b: a + b, [g.token for g in gathers.values()]) if i == 0 else 0.0
        P = dict(ffn1=(given["ffn1_norm"][i][None] + token, *got))
        x, s1 = _ffn_fwd(x, *P["ffn1"])
        P.update(mix_params(i, gathered(i, "mix", x)))
        x, s2 = _mix_fwd(x, P)
        P["ffn2"] = (given["ffn2_norm"][i][None], *gathered(i, "ffn2", x))
        x, s3 = _ffn_fwd(x, *P["ffn2"])
        tape.append((P, s1, s2, s3))
    loss_part, dx, dgf = _final_loss(x, given["final_norm"][None], tgt)

    me = 2 * lax.axis_index("x") + lax.axis_index("y")
    jobs = []

    flight = dict(op=None, owners=[], ticks=0)

    def tick(after, begin=None):
        parts, owners = [], []
        if flight["op"] is not None:
            got = flight["op"].wait(after)
            for job, (a0, na) in zip(flight["owners"], flight["op"].spans):
                mine, k = got[a0:a0 + na], len(job["names"])
                if job["stage"] == 1:
                    parts.append(_part_chips(_pair_add(mine[:k], mine[k:], c)))
                elif job["stage"] == 2:
                    parts.append(_part_join(_chip_sum(mine[:k], mine[k:], me, c)))
                else:
                    job.update(stage=4, out=dict(zip(job["names"], mine)))
                    continue
                job["stage"] += 1
                owners.append(job)
        if begin is not None:
            i, gname, gd = begin
            names = [n for n in dict(GROUPS)[gname] if n != "conv_w"]
            jobs.append(dict(key=(i, gname), names=names, stage=1))
            parts.append(_part_sibling([_halved(gd[n]) for n in names]))
            owners.append(jobs[-1])
        flight.update(op=_start_parts(parts, f"rs_tick{flight['ticks']}") if parts else None, owners=owners,
                      ticks=flight["ticks"] + 1)
        return flight["op"].token if parts else 0.0

    grads = [dict() for _ in range(DEPTH)]
    for i in reversed(range(DEPTH)):
        P, s1, s2, s3 = tape[i]
        g = grads[i]
        norm, wg, wu, wd = P["ffn2"]
        mids, (g["ffn2_w_gate"], g["ffn2_w_up"], g["ffn2_w_down"]) = _ffn_bwd_weights(dx, s3, wd)
        tok = tick(g["ffn2_w_down"], (i, "ffn2", g))
        dx, dn2 = _ffn_bwd_input(dx, s3, mids, norm + tok, wg, wu)
        mids, gm = _mix_bwd_weights(dx, s2, P)
        g.update(gm)
        tok = tick(gm["w_in"], (i, "mix", g))
        dx, dnm = _mix_bwd_input(dx, s2, mids, {**P, "mix_norm": P["mix_norm"] + tok})
        norm, wg, wu, wd = P["ffn1"]
        mids, (g["ffn1_w_gate"], g["ffn1_w_up"], g["ffn1_w_down"]) = _ffn_bwd_weights(dx, s1, wd)
        tok = tick(g["ffn1_w_down"], (i, "ffn1", g))
        dx, dn1 = _ffn_bwd_input(dx, s1, mids, norm + tok, wg, wu)
        g["ffn1_norm"], g["mix_norm"], g["ffn2_norm"] = dn1[0], dnm[0], dn2[0]
    grad_x = dx.reshape(given["x"].shape)

    order = [n for n in SMALL if n != "final_norm"] + ["final_norm"]
    small = [jnp.stack([grads[i][n] for i in range(DEPTH)]) for n in order[:-1] + ["conv_w"]]
    small = small[:-1] + [dgf[0], small[-1], loss_part[0, 0:1]]
    n_small = sum(s.size for s in small)
    rows_small = -(-n_small // (128 * 8)) * 8

    def flat(arrs):
        fill = rows_small * 128 - sum(a.size for a in arrs)
        return jnp.concatenate([a.reshape(-1) for a in arrs] + [jnp.zeros((fill,), F32)]).reshape(rows_small, 128)

    gsmall = _all_reduce_small(flat(small)).reshape(-1)

    grad_w = {}
    off = 0
    for n in order:
        size = given[n].size
        grad_w[n] = gsmall[off:off + size].reshape(given[n].shape)
        off += size
    cw = gsmall[off:off + 2 * 4 * SSD_CONV_DIM].reshape(DEPTH, 4, SSD_CONV_DIM)
    grad_w["conv_w"] = lax.dynamic_slice_in_dim(cw, me * (SSD_CONV_DIM // 4), SSD_CONV_DIM // 4, axis=2)
    loss = gsmall[off + 2 * 4 * SSD_CONV_DIM]

    delta, new_m, new_v = {}, {}, {}
    shp = given["conv_w"].shape
    d, m2, v2 = _adamw(*[a.reshape(shp[0] * shp[1], shp[2])
                         for a in (given["conv_w"], grad_w["conv_w"], given["m_conv_w"], given["v_conv_w"])])
    delta["conv_w"], new_m["conv_w"], new_v["conv_w"] = d.reshape(shp), m2.reshape(shp), v2.reshape(shp)
    packed = [flat([given[pre + n] for n in order]) for pre in ("", "m_", "v_")]
    small_out = _adamw(packed[0], gsmall.reshape(rows_small, 128), packed[1], packed[2])
    outs = [o.reshape(-1) for o in small_out]
    off = 0
    for n in order:
        size = given[n].size
        for dst, o in zip((delta, new_m, new_v), outs):
            dst[n] = o[off:off + size].reshape(given[n].shape)
        off += size

    stepped, arrived = {}, {}

    def update_arrived(dep):
        out = None
        for job in jobs:
            if job["stage"] == 4 and not job.get("seen"):
                job["seen"] = True
                for n, full in job["out"].items():
                    view = (lambda a: jnp.swapaxes(a, 1, 2)) if n in TRANSPOSED else (lambda a: a)
                    arrived.setdefault(n, {})[job["key"][0]] = full.reshape(view(given[n]).shape[1:])
                    if len(arrived[n]) == DEPTH:
                        res = _adamw_pair(view(given[n]), arrived[n][0], arrived[n][1], view(given["m_" + n]),
                                          view(given["v_" + n]), dep)
                        stepped[n] = [view(r) for r in res]
                        out = res[0]
        return out

    after = small_out[0]
    while any(j["stage"] < 4 for j in jobs):
        done = update_arrived(jnp.zeros((8, 128), F32) + tok)
        after = after if done is None else done
        tok = tick(after)
    update_arrived(jnp.zeros((8, 128), F32) + tok)
    for n, (d, m2, v2, g) in stepped.items():
        delta[n], new_m[n], new_v[n], grad_w[n] = d, m2, v2, g

    return (loss, grad_x, *[grad_w[n] for n in WEIGHTS], *[delta[n] for n in WEIGHTS],
            *[new_m[n] for n in WEIGHTS], *[new_v[n] for n in WEIGHTS])
```

```python
import functools
import math

import jax
import jax.numpy as jnp
from jax import lax
from jax.experimental import pallas as pl
from jax.experimental.pallas import tpu as pltpu

F32 = jnp.float32
BF = jnp.bfloat16

RMS_EPS = 1e-6
LN_EPS = 1e-5
SEQ = 2048
CHUNK = 128
N_CHUNK = SEQ // CHUNK
ATT_W = 384
HEAD = 64
SSD_W = 384
SSD_CONV_DIM = 896
SSD_STATE = 128
SGU_W = 256
DILATIONS = (1, 4, 16)
W_QKV = 3 * ATT_W
W_SSD = SSD_CONV_DIM + SSD_W + SSD_W
W_UV = 2 * SGU_W
ADAM_LR = 0.001
ADAM_B1 = 0.9
ADAM_B2 = 0.999
ADAM_EPS = 1e-08
ADAM_WD = 0.01
ADAM_STEP = 10
NEG = -1e30
ATTN_BWD_VMEM = 48 * 2 ** 20
ATTN_SUBSEQ_PER_STEP = 4
SGU_CHUNKS_PER_STEP = 4
FFN_VMEM = 60 * 2 ** 20


def _dot(a, b):
    return jnp.dot(a, b, preferred_element_type=F32)


def _dot_nt(a, b):
    return lax.dot_general(a, b, (((1,), (1,)), ((), ())), preferred_element_type=F32)


def _dot_tn(a, b):
    return lax.dot_general(a, b, (((0,), (0,)), ((), ())), preferred_element_type=F32)


def _sigmoid(x):
    return 1.0 / (1.0 + jnp.exp(-x))


def _call(body, *, name, grid, in_specs, out_specs, out_shape, scratch=(), sem=None, vmem=None):
    return pl.pallas_call(
        body, name=name, grid=grid, in_specs=in_specs, out_specs=out_specs, out_shape=out_shape,
        scratch_shapes=list(scratch),
        compiler_params=pltpu.CompilerParams(dimension_semantics=sem, vmem_limit_bytes=vmem),
    )


def _tile(n, want):
    t = min(n, want)
    while n % t:
        t //= 2
    return t


def _final_loss(x, g, tgt):
    T, D = x.shape
    tm = _tile(T, 512)

    def body(x_ref, g_ref, t_ref, l_ref, dx_ref, dg_ref):
        @pl.when(pl.program_id(0) == 0)
        def _():
            dg_ref[...] = jnp.zeros_like(dg_ref)
            l_ref[...] = jnp.zeros_like(l_ref)

        xf = x_ref[...]
        gg = g_ref[...]
        r = lax.rsqrt(jnp.mean(xf * xf, axis=-1, keepdims=True) + RMS_EPS)
        xn = xf * r
        e = xn * gg - t_ref[...]
        part = 0.5 * jnp.sum(jnp.mean(e * e, axis=-1, keepdims=True), axis=0, keepdims=True)
        l_ref[...] += jnp.broadcast_to(part, l_ref.shape)
        dy = e * (1.0 / D)
        u = dy * gg
        mu = jnp.mean(u * xf, axis=-1, keepdims=True)
        dx_ref[...] = r * (u - xf * (r * r * mu))
        dg_ref[...] += jnp.sum(dy * xn, axis=0, keepdims=True)

    row = pl.BlockSpec((tm, D), lambda i: (i, 0))
    vec = pl.BlockSpec((1, D), lambda i: (0, 0))
    lsp = pl.BlockSpec((1, 128), lambda i: (0, 0))
    return _call(body, name="final_loss", grid=(T // tm,), in_specs=[row, vec, row], out_specs=[lsp, row, vec],
                 out_shape=[jax.ShapeDtypeStruct((1, 128), F32), jax.ShapeDtypeStruct((T, D), F32),
                            jax.ShapeDtypeStruct((1, D), F32)],
                 sem=("arbitrary",))(x, g, tgt)


def _resident(shape):
    return pl.BlockSpec(shape, lambda *_: (0,) * len(shape), pipeline_mode=pl.Buffered(1))


def _ffn_fwd_k(x, gn, wg, wu, wd):
    T, D = x.shape
    NS, _, Fs = wg.shape
    tm = _tile(T, 1024)

    def body(x_ref, gn_ref, wg_ref, wu_ref, wd_ref, o_ref, h_ref, s1_ref, s2_ref, a_ref, hs, acc):
        j = pl.program_id(1)

        @pl.when(j == 0)
        def _():
            xf = x_ref[...]
            r = lax.rsqrt(jnp.mean(xf * xf, axis=-1, keepdims=True) + RMS_EPS)
            hs[...] = (xf * r * gn_ref[...]).astype(BF)
            h_ref[...] = hs[...]
            acc[...] = jnp.zeros_like(acc)

        h = hs[...]
        g = _dot(h, wg_ref[...])
        u = _dot(h, wu_ref[...])
        sg = _sigmoid(g)
        s1 = g * sg
        a = (s1 * u).astype(BF)
        s1_ref[...] = s1.astype(BF)
        s2_ref[...] = (u * (sg * (1.0 + g * (1.0 - sg)))).astype(BF)
        a_ref[...] = a
        acc[...] += _dot(a, wd_ref[...])

        @pl.when(j == NS - 1)
        def _():
            o_ref[...] = x_ref[...] + 0.5 * acc[...]

    row = pl.BlockSpec((tm, D), lambda i, j: (i, 0))
    act = pl.BlockSpec((None, tm, Fs), lambda i, j: (j, i, 0))
    sh = jax.ShapeDtypeStruct((NS, T, Fs), BF)
    wspec = lambda w: pl.BlockSpec((None,) + w.shape[1:], lambda i, j: (j, 0, 0))
    return _call(body, name="ffn_fwd", grid=(T // tm, NS),
                 in_specs=[row, pl.BlockSpec((1, D), lambda i, j: (0, 0)), wspec(wg), wspec(wu), wspec(wd)],
                 out_specs=[row, row, act, act, act],
                 out_shape=[jax.ShapeDtypeStruct((T, D), F32), jax.ShapeDtypeStruct((T, D), BF), sh, sh, sh],
                 scratch=[pltpu.VMEM((tm, D), BF), pltpu.VMEM((tm, D), F32)],
                 sem=("parallel", "arbitrary"), vmem=FFN_VMEM)(x, gn, wg, wu, wd)


def _ffn_bwd_act(dxo, s1, s2, wd):
    NS, T, Fs = s1.shape
    D = dxo.shape[1]
    tm = _tile(T, 1024)

    def body(dxo_ref, s1_ref, s2_ref, wd_ref, dg_ref, du_ref, dy_ref, dys):
        j = pl.program_id(1)

        @pl.when(j == 0)
        def _():
            dys[...] = (0.5 * dxo_ref[...]).astype(BF)
            dy_ref[...] = dys[...]

        da = _dot_nt(dys[...], wd_ref[j])
        dg_ref[...] = (da * s2_ref[...].astype(F32)).astype(BF)
        du_ref[...] = (da * s1_ref[...].astype(F32)).astype(BF)

    row = pl.BlockSpec((tm, D), lambda i, j: (i, 0))
    act = pl.BlockSpec((None, tm, Fs), lambda i, j: (j, i, 0))
    sh = jax.ShapeDtypeStruct((NS, T, Fs), BF)
    return _call(body, name="ffn_bwd_act", grid=(T // tm, NS), in_specs=[row, act, act, _resident(wd.shape)],
                 out_specs=[act, act, row], out_shape=[sh, sh, jax.ShapeDtypeStruct((T, D), BF)],
                 scratch=[pltpu.VMEM((tm, D), BF)], sem=("parallel", "arbitrary"))(dxo, s1, s2, wd)


def _ffn_bwd_dx(dg, du, wg, wu, x, gn, dxo):
    NS, T, Fs = dg.shape
    D = x.shape[1]
    tm = _tile(T, 1024)

    def body(dg_ref, du_ref, wg_ref, wu_ref, x_ref, gn_ref, dxo_ref, dx_ref, dgn_ref, acc):
        i, j = pl.program_id(0), pl.program_id(1)

        @pl.when((i == 0) & (j == 0))
        def _():
            dgn_ref[...] = jnp.zeros_like(dgn_ref)

        @pl.when(j == 0)
        def _():
            acc[...] = jnp.zeros_like(acc)

        acc[...] += _dot_nt(dg_ref[...], wg_ref[j]) + _dot_nt(du_ref[...], wu_ref[j])

        @pl.when(j == NS - 1)
        def _():
            xf = x_ref[...]
            r = lax.rsqrt(jnp.mean(xf * xf, axis=-1, keepdims=True) + RMS_EPS)
            dh = acc[...]
            uu = dh * gn_ref[...]
            mu = jnp.mean(uu * xf, axis=-1, keepdims=True)
            dx_ref[...] = dxo_ref[...] + r * (uu - xf * (r * r * mu))
            dgn_ref[...] += jnp.sum(dh * xf * r, axis=0, keepdims=True)

    row = pl.BlockSpec((tm, D), lambda i, j: (i, 0))
    vec = pl.BlockSpec((1, D), lambda i, j: (0, 0))
    act = pl.BlockSpec((None, tm, Fs), lambda i, j: (j, i, 0))
    return _call(body, name="ffn_bwd_dx", grid=(T // tm, NS),
                 in_specs=[act, act, _resident(wg.shape), _resident(wu.shape), row, vec, row], out_specs=[row, vec],
                 out_shape=[jax.ShapeDtypeStruct((T, D), F32), jax.ShapeDtypeStruct((1, D), F32)],
                 scratch=[pltpu.VMEM((tm, D), F32)], sem=("arbitrary", "arbitrary"), vmem=FFN_VMEM)(
        dg, du, wg, wu, x, gn, dxo)


def _ffn_bwd_k2(hb, dyb, a, dg, du):
    NS, T, Fs = a.shape
    D = hb.shape[1]
    tk = _tile(T, 1024)

    def body(h_ref, dy_ref, a_ref, dg_ref, du_ref, og_ref, ou_ref, od_ref):
        @pl.when(pl.program_id(1) == 0)
        def _():
            og_ref[...] = jnp.zeros_like(og_ref)
            ou_ref[...] = jnp.zeros_like(ou_ref)
            od_ref[...] = jnp.zeros_like(od_ref)

        h = h_ref[...]
        og_ref[...] += _dot_tn(dg_ref[...], h)
        ou_ref[...] += _dot_tn(du_ref[...], h)
        od_ref[...] += _dot_tn(a_ref[...], dy_ref[...])

    row = pl.BlockSpec((tk, D), lambda j, k: (k, 0))
    act = pl.BlockSpec((None, tk, Fs), lambda j, k: (j, k, 0))
    return _call(body, name="ffn_bwd_w", grid=(NS, T // tk), in_specs=[row, row, act, act, act],
                 out_specs=[pl.BlockSpec((None, Fs, D), lambda j, k: (j, 0, 0))] * 3,
                 out_shape=[jax.ShapeDtypeStruct((NS, Fs, D), F32)] * 3,
                 sem=("parallel", "arbitrary"))(hb, dyb, a, dg, du)


def _mm_nn(a, b, res=None, out_dtype=F32):
    T, K = a.shape
    N = b.shape[1]
    tm = _tile(T, 512)
    tn = N if N <= 2048 else _tile(N, 1024)

    def body(*refs):
        if res is None:
            a_ref, b_ref, o_ref = refs
            o_ref[...] = _dot(a_ref[...], b_ref[...]).astype(out_dtype)
        else:
            a_ref, b_ref, r_ref, o_ref = refs
            o_ref[...] = (r_ref[...] + _dot(a_ref[...], b_ref[...])).astype(out_dtype)

    o = pl.BlockSpec((tm, tn), lambda i, j: (i, j))
    ins = [pl.BlockSpec((tm, K), lambda i, j: (i, 0)), pl.BlockSpec((K, tn), lambda i, j: (0, j))]
    args = [a, b]
    if res is not None:
        ins.append(o)
        args.append(res)
    return _call(body, name="mm_nn", grid=(T // tm, N // tn), in_specs=ins, out_specs=o,
                 out_shape=jax.ShapeDtypeStruct((T, N), out_dtype), sem=("parallel", "parallel"))(*args)


def _mix_bwd_dy(dxo, w_out):
    T, D = dxo.shape
    tm = _tile(T, 512)
    cuts = (0, ATT_W, ATT_W + SSD_W, ATT_W + SSD_W + SGU_W)

    def body(dx_ref, w_ref, a_ref, s_ref, g_ref):
        d = _dot_nt(dx_ref[...].astype(BF), w_ref[...])
        for o_ref, lo, hi in zip((a_ref, s_ref, g_ref), cuts[:-1], cuts[1:]):
            o_ref[...] = d[:, lo:hi]

    row = lambda w: pl.BlockSpec((tm, w), lambda i: (i, 0))
    return _call(body, name="mix_bwd_dy", grid=(T // tm,), in_specs=[row(D), _resident(w_out.shape)],
                 out_specs=[row(ATT_W), row(SSD_W), row(SGU_W)],
                 out_shape=[jax.ShapeDtypeStruct((T, w), F32) for w in (ATT_W, SSD_W, SGU_W)],
                 sem=("parallel",))(dxo, w_out)


def _mm_tn(a, b):
    T, M = a.shape
    N = b.shape[1]
    tk = _tile(T, 1024)
    tmm = _tile(M, 512)

    def body(a_ref, b_ref, o_ref):
        @pl.when(pl.program_id(1) == 0)
        def _():
            o_ref[...] = jnp.zeros_like(o_ref)

        o_ref[...] += _dot_tn(a_ref[...].astype(BF), b_ref[...].astype(BF))

    return _call(body, name="mm_tn", grid=(M // tmm, T // tk),
                 in_specs=[pl.BlockSpec((tk, tmm), lambda i, k: (k, i)), pl.BlockSpec((tk, N), lambda i, k: (k, 0))],
                 out_specs=pl.BlockSpec((tmm, N), lambda i, k: (i, 0)),
                 out_shape=jax.ShapeDtypeStruct((M, N), F32), sem=("parallel", "arbitrary"))(a, b)


def _mix_proj(x, gn, win):
    T, D = x.shape
    tm = _tile(T, 512)
    cuts = (0, W_QKV, W_QKV + W_SSD, W_QKV + W_SSD + W_UV)

    def body(x_ref, gn_ref, w_ref, h_ref, q_ref, s_ref, u_ref):
        xf = x_ref[...]
        r = lax.rsqrt(jnp.mean(xf * xf, axis=-1, keepdims=True) + RMS_EPS)
        h = (xf * r * gn_ref[...]).astype(BF)
        h_ref[...] = h
        for o_ref, lo, hi in zip((q_ref, s_ref, u_ref), cuts[:-1], cuts[1:]):
            o_ref[...] = _dot(h, w_ref[:, lo:hi])

    row = lambda w: pl.BlockSpec((tm, w), lambda i: (i, 0))
    return _call(body, name="mix_proj", grid=(T // tm,),
                 in_specs=[row(D), pl.BlockSpec((1, D), lambda i: (0, 0)), _resident(win.shape)],
                 out_specs=[row(D), row(W_QKV), row(W_SSD), row(W_UV)],
                 out_shape=[jax.ShapeDtypeStruct((T, D), BF), jax.ShapeDtypeStruct((T, W_QKV), F32),
                            jax.ShapeDtypeStruct((T, W_SSD), F32), jax.ShapeDtypeStruct((T, W_UV), F32)],
                 sem=("parallel",))(x, gn, win)


def _mix_bwd_dx(dqkv, dsin, duv, win, x, gn, dxo):
    T, D = x.shape
    tm = _tile(T, 512)
    cuts = (0, W_QKV, W_QKV + W_SSD, W_QKV + W_SSD + W_UV)

    def body(dq_ref, ds_ref, du_ref, w_ref, x_ref, gn_ref, dxo_ref, dx_ref, dgn_ref):
        @pl.when(pl.program_id(0) == 0)
        def _():
            dgn_ref[...] = jnp.zeros_like(dgn_ref)

        dh = (_dot_nt(dq_ref[...], w_ref[:, cuts[0]:cuts[1]]) + _dot_nt(ds_ref[...], w_ref[:, cuts[1]:cuts[2]])
              + _dot_nt(du_ref[...], w_ref[:, cuts[2]:cuts[3]]))
        xf = x_ref[...]
        r = lax.rsqrt(jnp.mean(xf * xf, axis=-1, keepdims=True) + RMS_EPS)
        uu = dh * gn_ref[...]
        mu = jnp.mean(uu * xf, axis=-1, keepdims=True)
        dx_ref[...] = dxo_ref[...] + r * (uu - xf * (r * r * mu))
        dgn_ref[...] += jnp.sum(dh * xf * r, axis=0, keepdims=True)

    row = lambda w: pl.BlockSpec((tm, w), lambda i: (i, 0))
    vec = pl.BlockSpec((1, D), lambda i: (0, 0))
    return _call(body, name="mix_bwd_dx", grid=(T // tm,),
                 in_specs=[row(W_QKV), row(W_SSD), row(W_UV), _resident(win.shape), row(D), vec, row(D)],
                 out_specs=[row(D), vec],
                 out_shape=[jax.ShapeDtypeStruct((T, D), F32), jax.ShapeDtypeStruct((1, D), F32)],
                 sem=("arbitrary",))(dqkv, dsin, duv, win, x, gn, dxo)


def _lane_mask(e, width=128):
    return (lax.broadcasted_iota(jnp.int32, (1, width), 1) // HEAD) == e


def _band_mask(n):
    qi = lax.broadcasted_iota(jnp.int32, (CHUNK, 2 * CHUNK), 0)
    kj = lax.broadcasted_iota(jnp.int32, (CHUNK, 2 * CHUNK), 1)
    dist = qi + CHUNK - kj
    return (dist >= 0) & (dist <= CHUNK) & ((kj >= CHUNK) | (n > 0))


def _sub_rows(r, block, dil):
    if dil == 1:
        return pl.ds(pl.multiple_of(block * CHUNK, CHUNK), CHUNK)
    return pl.ds(r + dil * CHUNK * block, CHUNK, stride=dil)


def _attn_specs(T, dil):
    per_step = ATTN_SUBSEQ_PER_STEP if dil == 1 else min(dil, 2 * ATTN_SUBSEQ_PER_STEP)
    qrows = CHUNK * (dil if dil > 1 else per_step)
    B, nbq = T // SEQ, SEQ // qrows
    once = dict(pipeline_mode=pl.Buffered(1))
    q_like = lambda col: pl.BlockSpec((qrows, 128), lambda b, n, r: (b * nbq + n, col), **(once if nbq == 1 else {}))
    k_like = lambda col: pl.BlockSpec((SEQ, 128), lambda b, n, r: (b, col), **once)
    return B, nbq, max(dil // per_step, 1), per_step, q_like, k_like


def _attn_step(u, dil, per_step):
    if dil > 1:
        r = pl.program_id(2) * per_step + u
        return r, pl.program_id(1), _sub_rows(r, 0, dil)
    return 0, pl.program_id(1) * per_step + u, pl.ds(CHUNK * u, CHUNK)


def _attn_fwd(qkv, dil):
    T = qkv.shape[0]
    B, nb, last, per_step, q_like, k_like = _attn_specs(T, dil)
    scale = HEAD ** -0.5

    def body(*refs):
        q_t, k_t, v_t, o_t, l_t = refs[0:3], refs[3:6], refs[6:9], refs[9:12], refs[12:15]
        for u in range(per_step):
            r, n, mine = _attn_step(u, dil, per_step)
            mask = _band_mask(n)
            cur, prv = _sub_rows(r, n, dil), _sub_rows(r, jnp.maximum(n - 1, 0), dil)
            for t in range(3):
                qt = q_t[t][mine, :].astype(BF)
                kt = jnp.concatenate([k_t[t][prv, :], k_t[t][cur, :]], axis=0).astype(BF)
                vt = jnp.concatenate([v_t[t][prv, :], v_t[t][cur, :]], axis=0).astype(BF)
                o_pair = jnp.zeros((CHUNK, 128), F32)
                l_pair = jnp.zeros((CHUNK, 128), F32)
                for e in range(2):
                    lm = _lane_mask(e)
                    s = _dot_nt(jnp.where(lm, qt, jnp.zeros_like(qt)), kt) * scale
                    s = jnp.where(mask, s, NEG)
                    m = jnp.max(s, axis=-1, keepdims=True)
                    p = jnp.exp(s - m)
                    den = jnp.sum(p, axis=-1, keepdims=True)
                    o = _dot(p.astype(BF), vt) / den
                    o_pair = jnp.where(lm, o, o_pair)
                    l_pair = jnp.where(lm, m + jnp.log(den), l_pair)
                o_t[t][mine, :] = o_pair
                l_t[t][mine, :] = l_pair


    out_spec = pl.BlockSpec(q_like(0).block_shape, lambda b, n, r: (b * nb + n, 0))
    sh = jax.ShapeDtypeStruct((T, 128), F32)
    outs = _call(
        body, name=f"attn_fwd_d{dil}", grid=(B, nb, last),
        in_specs=[q_like(t) for t in range(3)] + [k_like(3 + t) for t in range(3)] + [k_like(6 + t) for t in range(3)],
        out_specs=[out_spec] * 6, out_shape=[sh] * 6, sem=("parallel", "arbitrary", "arbitrary"))(*([qkv] * 9))
    return list(outs[0:3]), list(outs[3:6])


def _attn_combine(branches):
    T = branches[0][0][0].shape[0]
    tm = _tile(T, 512)

    def body(*refs):
        y_ref, l_ref = refs[-2:]
        for t in range(3):
            o = [refs[6 * i + t][...] for i in range(3)]
            a, b, c = [refs[6 * i + 3 + t][...] for i in range(3)]
            m = jnp.maximum(jnp.maximum(a, b), c)
            ea, eb, ec = jnp.exp(a - m), jnp.exp(b - m), jnp.exp(c - m)
            z = ea + eb + ec
            y_ref[:, 128 * t:128 * (t + 1)] = (ea * o[0] + eb * o[1] + ec * o[2]) / z
            l_ref[:, 128 * t:128 * (t + 1)] = m + jnp.log(z)

    tile = pl.BlockSpec((tm, 128), lambda i: (i, 0))
    row = pl.BlockSpec((tm, ATT_W), lambda i: (i, 0))
    sh = jax.ShapeDtypeStruct((T, ATT_W), F32)
    flat = [a for o_t, l_t in branches for a in (*o_t, *l_t)]
    return _call(body, name="attn_combine", grid=(T // tm,), in_specs=[tile] * 18, out_specs=[row, row],
                 out_shape=[sh, sh], sem=("parallel",))(*flat)


def _attn_bwd(qkv, do, out, lse, dil):
    T = qkv.shape[0]
    B, nb, last, per_step, q_like, k_like = _attn_specs(T, dil)
    scale = HEAD ** -0.5

    def body(*refs):
        q_t, k_t, v_t = refs[0:3], refs[3:6], refs[6:9]
        do_t, out_t, lse_t = refs[9:12], refs[12:15], refs[15:18]
        dq_t, dk_t, dv_t = refs[18:21], refs[21:24], refs[24:27]
        @pl.when((pl.program_id(1) == 0) & (pl.program_id(2) == 0))
        def _():
            for t in range(3):
                dk_t[t][...] = jnp.zeros_like(dk_t[t])
                dv_t[t][...] = jnp.zeros_like(dv_t[t])

        for u in range(per_step):
            r, n, mine = _attn_step(u, dil, per_step)
            mask = _band_mask(n)
            cur, prv = _sub_rows(r, n, dil), _sub_rows(r, jnp.maximum(n - 1, 0), dil)
            for t in range(3):
                qt = q_t[t][mine, :].astype(BF)
                kt = jnp.concatenate([k_t[t][prv, :], k_t[t][cur, :]], axis=0).astype(BF)
                vt = jnp.concatenate([v_t[t][prv, :], v_t[t][cur, :]], axis=0).astype(BF)
                do_ = do_t[t][mine, :]
                dlt = do_ * out_t[t][mine, :]
                ls = lse_t[t][mine, :]
                dq_pair = jnp.zeros((CHUNK, 128), F32)
                dk_acc = jnp.zeros((2 * CHUNK, 128), F32)
                dv_acc = jnp.zeros((2 * CHUNK, 128), F32)
                for e in range(2):
                    lm = _lane_mask(e)
                    qm = jnp.where(lm, qt, jnp.zeros_like(qt))
                    s = _dot_nt(qm, kt) * scale
                    p = jnp.exp(jnp.where(mask, s - ls[:, HEAD * e:HEAD * e + 1], NEG))
                    dom = jnp.where(lm, do_, 0.0).astype(BF)
                    dv_acc += _dot_tn(p.astype(BF), dom)
                    dp = _dot_nt(dom, vt)
                    delta = jnp.sum(jnp.where(lm, dlt, 0.0), axis=-1, keepdims=True)
                    ds = (p * (dp - delta) * scale).astype(BF)
                    dq_pair += jnp.where(lm, _dot(ds, kt), 0.0)
                    dk_acc += _dot_tn(ds, qm)
                dq_t[t][mine, :] = dq_pair
                dk_t[t][cur, :] = dk_t[t][cur, :] + dk_acc[CHUNK:]
                dk_t[t][prv, :] = dk_t[t][prv, :] + dk_acc[:CHUNK]
                dv_t[t][cur, :] = dv_t[t][cur, :] + dv_acc[CHUNK:]
                dv_t[t][prv, :] = dv_t[t][prv, :] + dv_acc[:CHUNK]

    q_out = pl.BlockSpec(q_like(0).block_shape, lambda b, n, r: (b * nb + n, 0))
    k_out = pl.BlockSpec((SEQ, 128), lambda b, n, r: (b, 0))
    sh = jax.ShapeDtypeStruct((T, 128), F32)
    tiles = lambda: [q_like(t) for t in range(3)]
    return list(_call(
        body, name=f"attn_bwd_d{dil}", grid=(B, nb, last),
        in_specs=tiles() + [k_like(3 + t) for t in range(3)] + [k_like(6 + t) for t in range(3)]
        + tiles() + tiles() + tiles(),
        out_specs=[q_out] * 3 + [k_out] * 6, out_shape=[sh] * 9,
        sem=("parallel", "arbitrary", "arbitrary"), vmem=ATTN_BWD_VMEM)(*([qkv] * 9 + [do] * 3 + [out] * 3 + [lse] * 3)))


def _sum_branches(parts):
    T = parts[0][0].shape[0]
    tm = _tile(T, 512)

    def body(*refs):
        o_ref = refs[-1]
        for c in range(9):
            acc = refs[c][...] + refs[9 + c][...] + refs[18 + c][...]
            o_ref[:, 128 * c:128 * (c + 1)] = acc.astype(BF)

    tile = pl.BlockSpec((tm, 128), lambda i: (i, 0))
    flat = [a for br in parts for a in br]
    return _call(body, name="attn_sum_branches", grid=(T // tm,), in_specs=[tile] * 27,
                 out_specs=pl.BlockSpec((tm, W_QKV), lambda i: (i, 0)),
                 out_shape=jax.ShapeDtypeStruct((T, W_QKV), BF), sem=("parallel",))(*flat)


def _silu(x):
    return x * _sigmoid(x)


def _dsilu(x):
    s = _sigmoid(x)
    return s * (1.0 + x * (1.0 - s))


def _log1p(u):
    return jnp.where(u < 0.01, u * (1.0 - u * (0.5 - u * (1.0 / 3.0))), jnp.log(1.0 + u))


def _softplus(x):
    return jnp.maximum(x, 0.0) + _log1p(jnp.exp(-jnp.abs(x)))


def _cumsum_rows(x, reverse=False):
    n = x.shape[0]
    rows = lax.broadcasted_iota(jnp.int32, x.shape, 0)
    k = 1
    while k < n:
        if reverse:
            x = x + jnp.where(rows < n - k, pltpu.roll(x, n - k, 0), 0.0)
        else:
            x = x + jnp.where(rows >= k, pltpu.roll(x, k, 0), 0.0)
        k *= 2
    return x


def _tri():
    r = lax.broadcasted_iota(jnp.int32, (CHUNK, CHUNK), 0)
    c = lax.broadcasted_iota(jnp.int32, (CHUNK, CHUNK), 1)
    return r >= c


def _row_mask(e):
    return (lax.broadcasted_iota(jnp.int32, (128, 1), 0) // HEAD) == e


def _first_lane(e):
    return lax.broadcasted_iota(jnp.int32, (1, 128), 1) == HEAD * e


def _ssd_pre(x_ref, halo_ref, first, cw_ref, cb_ref, dtb_ref, al_ref, ext):
    row = x_ref[...]
    z = row[:, SSD_CONV_DIM:SSD_CONV_DIM + SSD_W]
    u = row[:, SSD_CONV_DIM + SSD_W:] + dtb_ref[...]
    ext[0:8, :] = jnp.where(first, 0.0, halo_ref[:, 0:SSD_CONV_DIM])
    ext[8:8 + CHUNK, :] = row[:, 0:SSD_CONV_DIM]
    xc = cb_ref[...]
    for j in range(4):
        xc = xc + cw_ref[j:j + 1, :] * ext[pl.ds(5 + j, CHUNK), :]
    xa = _silu(xc)
    dt = _softplus(u)
    a = dt * (-jnp.exp(al_ref[...]))
    A = _cumsum_rows(a)
    return dict(z=z, u=u, xc=xc, xs=xa[:, 0:SSD_W], Bm=xa[:, SSD_W:SSD_W + 256], Cm=xa[:, SSD_W + 256:],
                dt=dt, a=a, A=A, AT=A.T, eA=jnp.exp(A), wdec=jnp.exp(A[CHUNK - 1:CHUNK, :] - A),
                dtot=jnp.exp(A[CHUNK - 1:CHUNK, :]))


def _ssd_y(p, hp_ref, dskip):
    tri = _tri()
    X = p["xs"] * p["dt"]
    Bb = [p["Bm"][:, 128 * g:128 * (g + 1)].astype(BF) for g in range(2)]
    Cb = [p["Cm"][:, 128 * g:128 * (g + 1)].astype(BF) for g in range(2)]
    CB = [_dot_nt(Cb[g], Bb[g]) for g in range(2)]
    tiles = []
    for t in range(3):
        sl = slice(128 * t, 128 * (t + 1))
        hpb = hp_ref[sl, :].astype(BF)
        acc = jnp.zeros((CHUNK, 128), F32)
        for e in range(2):
            h = 2 * t + e
            g, col = h // 3, HEAD * h
            lm = _lane_mask(e)
            L = jnp.exp(jnp.where(tri, p["A"][:, col:col + 1] - p["AT"][col:col + 1, :], NEG))
            yd = _dot((CB[g] * L).astype(BF), jnp.where(lm, X[:, sl], 0.0).astype(BF))
            yo = _dot_nt(Cb[g], hpb) * p["eA"][:, sl]
            acc = acc + jnp.where(lm, yd + yo, 0.0)
        tiles.append(acc)
    return jnp.concatenate(tiles, axis=1) + dskip * p["xs"], X, Bb, Cb, CB


def _group_stats(v):
    g0 = lax.broadcasted_iota(jnp.int32, (1, SSD_W), 1) < SSD_W // 2
    m0 = jnp.sum(jnp.where(g0, v, 0.0), axis=-1, keepdims=True) * (2.0 / SSD_W)
    m1 = jnp.sum(jnp.where(g0, 0.0, v), axis=-1, keepdims=True) * (2.0 / SSD_W)
    return jnp.where(g0, m0, m1)


def _ssd_specs(T, rev):
    B = T // SEQ
    chunk = (lambda c: N_CHUNK - 1 - c) if rev else (lambda c: c)
    row = pl.BlockSpec((B, CHUNK, W_SSD), lambda c: (0, chunk(c), 0))
    halo = pl.BlockSpec((B, 8, W_SSD), lambda c: (0, jnp.maximum(chunk(c) * (CHUNK // 8) - 1, 0), 0))
    hp = pl.BlockSpec((B, None, SSD_W, SSD_STATE), lambda c: (0, chunk(c), 0, 0))
    y = pl.BlockSpec((B, CHUNK, SSD_W), lambda c: (0, chunk(c), 0))
    const = lambda r, w: pl.BlockSpec((r, w), lambda c: (0, 0))
    params = [const(4, SSD_CONV_DIM), const(1, SSD_CONV_DIM)] + [const(1, SSD_W)] * 4
    return B, row, halo, hp, y, const, params


def _ssd_fwd(sin, conv_w, conv_b, dtb, alog, dskip, norm_g):
    T = sin.shape[0]
    B, row, halo, hp, y, const, params = _ssd_specs(T, False)

    def body(xs_ref, halos_ref, cw_ref, cb_ref, dtb_ref, al_ref, dk_ref, ng_ref, ys_ref, hps_ref, exts, hsts):
        @pl.when(pl.program_id(0) == 0)
        def _():
            hsts[...] = jnp.zeros_like(hsts)

        for b in range(B):
            one(xs_ref.at[b], halos_ref.at[b], cw_ref, cb_ref, dtb_ref, al_ref, dk_ref, ng_ref, ys_ref.at[b],
                hps_ref.at[b], exts.at[b], hsts.at[b])

    def one(x_ref, halo_ref, cw_ref, cb_ref, dtb_ref, al_ref, dk_ref, ng_ref, y_ref, hp_ref, ext, hst):
        c = pl.program_id(0)
        p = _ssd_pre(x_ref, halo_ref, c == 0, cw_ref, cb_ref, dtb_ref, al_ref, ext)
        yv, X, Bb, Cb, CB = _ssd_y(p, hst, dk_ref[...])
        hp_ref[...] = hst[...]
        for t in range(3):
            sl = slice(128 * t, 128 * (t + 1))
            old = hst[sl, :]
            new = old
            for e in range(2):
                h = 2 * t + e
                g, col = h // 3, HEAD * h
                st = _dot_tn(jnp.where(_lane_mask(e), X[:, sl] * p["wdec"][:, sl], 0.0).astype(BF), Bb[g])
                new = jnp.where(_row_mask(e), old * p["dtot"][:, col:col + 1] + st, new)
            hst[sl, :] = new
        y2 = yv * _silu(p["z"])
        r = lax.rsqrt(_group_stats(y2 * y2) + RMS_EPS)
        y_ref[...] = y2 * r * ng_ref[...]

    sin3 = sin.reshape(B, SEQ, W_SSD)
    yo, hprev = _call(
        body, name="ssd_fwd", grid=(N_CHUNK,), in_specs=[row, halo] + params, out_specs=[y, hp],
        out_shape=[jax.ShapeDtypeStruct((B, SEQ, SSD_W), F32),
                   jax.ShapeDtypeStruct((B, N_CHUNK, SSD_W, SSD_STATE), F32)],
        scratch=[pltpu.VMEM((B, 8 + CHUNK, SSD_CONV_DIM), F32), pltpu.VMEM((B, SSD_W, SSD_STATE), F32)],
        sem=("arbitrary",))(sin3, sin3, conv_w, conv_b, dtb, alog, dskip, norm_g)
    return yo.reshape(T, SSD_W), hprev


def _ssd_bwd(sin, hprev, dy3, conv_w, conv_b, dtb, alog, dskip, norm_g):
    T = sin.shape[0]
    B, row, halo, hp, y, const, params = _ssd_specs(T, True)

    def body(xs_ref, halos_ref, hps_ref, dys_ref, cw_ref, cb_ref, dtb_ref, al_ref, dk_ref, ng_ref,
             dxs_ref, dcw_ref, dcb_ref, dvec_ref, exts, ext2s, dhs):
        @pl.when(pl.program_id(0) == 0)
        def _():
            dcw_ref[...] = jnp.zeros_like(dcw_ref)
            dcb_ref[...] = jnp.zeros_like(dcb_ref)
            dvec_ref[...] = jnp.zeros_like(dvec_ref)
            dhs[...] = jnp.zeros_like(dhs)
            ext2s[:, CHUNK:CHUNK + 8, :] = jnp.zeros((B, 8, SSD_CONV_DIM), F32)

        for b in range(B):
            one(xs_ref.at[b], halos_ref.at[b], hps_ref.at[b], dys_ref.at[b], cw_ref, cb_ref, dtb_ref, al_ref, dk_ref,
                ng_ref, dxs_ref.at[b], dcw_ref, dcb_ref, dvec_ref, exts.at[b], ext2s.at[b], dhs.at[b])

    def one(x_ref, halo_ref, hp_ref, dy_ref, cw_ref, cb_ref, dtb_ref, al_ref, dk_ref, ng_ref,
            dx_ref, dcw_ref, dcb_ref, dvec_ref, ext, ext2, dh):
        c = pl.program_id(0)
        p = _ssd_pre(x_ref, halo_ref, c == N_CHUNK - 1, cw_ref, cb_ref, dtb_ref, al_ref, ext)
        dskip_ = dk_ref[...]
        yv, X, Bb, Cb, CB = _ssd_y(p, hp_ref, dskip_)
        xs, z, A, AT = p["xs"], p["z"], p["A"], p["AT"]

        sz = _silu(z)
        y2 = yv * sz
        r = lax.rsqrt(_group_stats(y2 * y2) + RMS_EPS)
        dy3_ = dy_ref[...]
        uu = dy3_ * ng_ref[...]
        dy2 = r * (uu - y2 * (r * r * _group_stats(uu * y2)))
        dy = dy2 * sz
        dz = dy2 * yv * _dsilu(z)

        tri = _tri()
        rows = lax.broadcasted_iota(jnp.int32, (CHUNK, 1), 0)
        dG = [jnp.zeros((CHUNK, CHUNK), F32) for _ in range(2)]
        dB = [jnp.zeros((CHUNK, SSD_STATE), F32) for _ in range(2)]
        dC = [jnp.zeros((CHUNK, SSD_STATE), F32) for _ in range(2)]
        dX_t, dA_t, ddtx_t = [], [], []
        for t in range(3):
            sl = slice(128 * t, 128 * (t + 1))
            hp_t = hp_ref[sl, :]
            hpb = hp_t.astype(BF)
            dhc = dh[sl, :]
            dh_new = jnp.zeros((128, SSD_STATE), F32)
            dX = jnp.zeros((CHUNK, 128), F32)
            dA = jnp.zeros((CHUNK, 128), F32)
            ddtx = jnp.zeros((CHUNK, 128), F32)
            for e in range(2):
                h = 2 * t + e
                g, col = h // 3, HEAD * h
                lm, rm, fl = _lane_mask(e), _row_mask(e), _first_lane(e)
                L = jnp.exp(jnp.where(tri, A[:, col:col + 1] - AT[col:col + 1, :], NEG))
                Mf = CB[g] * L
                Xm = jnp.where(lm, X[:, sl], 0.0)
                Xmb = Xm.astype(BF)
                dyh = jnp.where(lm, dy[:, sl], 0.0)
                dyb = dyh.astype(BF)
                dXh = _dot_tn(Mf.astype(BF), dyb)
                dM = jnp.where(tri, _dot_nt(dyb, Xmb), 0.0)
                Wm = dM * Mf
                dAc = jnp.sum(Wm, axis=-1, keepdims=True) - jnp.sum(Wm.T, axis=-1, keepdims=True)
                dG[g] = dG[g] + dM * L
                eAt = p["eA"][:, sl]
                yo = _dot_nt(Cb[g], hpb)
                dyo = (dyh * eAt).astype(BF)
                dC[g] = dC[g] + _dot(dyo, hpb)
                dh_new = dh_new + _dot_tn(dyo, Cb[g])
                dAc = dAc + jnp.sum(dyh * yo * eAt, axis=-1, keepdims=True)
                dHn = jnp.where(rm, dhc, 0.0)
                dHnb = dHn.astype(BF)
                dec = p["dtot"][:, col:col + 1]
                dh_new = dh_new + dec * dHn
                Z = _dot_nt(Bb[g], dHnb)
                wt = p["wdec"][:, sl]
                xi = jnp.sum(Xm * Z, axis=-1, keepdims=True) * p["wdec"][:, col:col + 1]
                dXh = dXh + wt * Z
                dB[g] = dB[g] + _dot(jnp.where(lm, X[:, sl] * wt, 0.0).astype(BF), dHnb)
                dAtot = jnp.sum(xi, axis=0, keepdims=True) + dec * jnp.sum(
                    jnp.sum(dHn * hp_t, axis=-1, keepdims=True), axis=0, keepdims=True)
                dAc = dAc - xi + jnp.where(rows == CHUNK - 1, dAtot, 0.0)
                dA = dA + jnp.where(fl, dAc, 0.0)
                dX = dX + dXh
                ddtx = ddtx + jnp.where(fl, jnp.sum(dXh * xs[:, sl], axis=-1, keepdims=True), 0.0)
            dh[sl, :] = dh_new
            dX_t.append(dX)
            dA_t.append(dA)
            ddtx_t.append(ddtx)
        for g in range(2):
            dGb = dG[g].astype(BF)
            dC[g] = dC[g] + _dot(dGb, Bb[g])
            dB[g] = dB[g] + _dot_tn(dGb, Cb[g])
        dXf = jnp.concatenate(dX_t, axis=1)
        da = _cumsum_rows(jnp.concatenate(dA_t, axis=1), reverse=True)
        ddt = da * (-jnp.exp(al_ref[...])) + jnp.concatenate(ddtx_t, axis=1)
        du = ddt * _sigmoid(p["u"])
        dxs = dXf * p["dt"] + dskip_ * dy
        dxc = jnp.concatenate([dxs, dB[0], dB[1], dC[0], dC[1]], axis=1) * _dsilu(p["xc"])
        ext2[0:CHUNK, :] = dxc
        dxbc = jnp.zeros((CHUNK, SSD_CONV_DIM), F32)
        for j in range(4):
            dxbc = dxbc + cw_ref[j:j + 1, :] * ext2[pl.ds(3 - j, CHUNK), :]
            dcw_ref[j:j + 1, :] += jnp.sum(dxc * ext[pl.ds(5 + j, CHUNK), :], axis=0, keepdims=True)
        ext2[CHUNK:CHUNK + 8, :] = dxc[0:8, :]
        dcb_ref[...] += jnp.sum(dxc, axis=0, keepdims=True)
        dvec_ref[0:1, :] += jnp.sum(du, axis=0, keepdims=True)
        dvec_ref[1:2, :] += jnp.sum(da * p["a"], axis=0, keepdims=True)
        dvec_ref[2:3, :] += jnp.sum(dy * xs, axis=0, keepdims=True)
        dvec_ref[3:4, :] += jnp.sum(dy3_ * y2 * r, axis=0, keepdims=True)
        dx_ref[...] = jnp.concatenate([dxbc, dz, du], axis=1).astype(BF)

    sin3 = sin.reshape(B, SEQ, W_SSD)
    out = _call(body, name="ssd_bwd", grid=(N_CHUNK,), in_specs=[row, halo, hp, y] + params,
                out_specs=[row, const(4, SSD_CONV_DIM), const(1, SSD_CONV_DIM), const(8, SSD_W)],
                out_shape=[jax.ShapeDtypeStruct((B, SEQ, W_SSD), BF), jax.ShapeDtypeStruct((4, SSD_CONV_DIM), F32),
                           jax.ShapeDtypeStruct((1, SSD_CONV_DIM), F32), jax.ShapeDtypeStruct((8, SSD_W), F32)],
                scratch=[pltpu.VMEM((B, 8 + CHUNK, SSD_CONV_DIM), F32), pltpu.VMEM((B, 8 + CHUNK, SSD_CONV_DIM), F32),
                         pltpu.VMEM((B, SSD_W, SSD_STATE), F32)],
                sem=("arbitrary",))(sin3, sin3, hprev, dy3.reshape(B, SEQ, SSD_W), conv_w, conv_b, dtb, alog, dskip,
                                    norm_g)
    return (out[0].reshape(T, W_SSD),) + tuple(out[1:])


def _sgu_weights(w_ref):
    tri = _tri()
    return [jnp.where(tri, w_ref[gi], 0.0).astype(BF) for gi in range(4)]


def _sgu_core(x, g_ref, b_ref, wc, bias_ref):
    cdf = 0.5 * (1.0 + lax.erf(x * (2.0 ** -0.5)))
    ge = x * cdf
    dge = cdf + x * jnp.exp(-0.5 * x * x) * ((2.0 * math.pi) ** -0.5)
    u, v = ge[:, 0:SGU_W], ge[:, SGU_W:]
    vc = v - jnp.mean(v, axis=-1, keepdims=True)
    rstd = lax.rsqrt(jnp.mean(vc * vc, axis=-1, keepdims=True) + LN_EPS)
    vhat = vc * rstd
    vn = vhat * g_ref[...] + b_ref[...]
    vm = [jnp.where(_lane_mask(gi % 2), vn[:, 128 * (gi // 2):128 * (gi // 2 + 1)], 0.0).astype(BF) for gi in range(4)]
    mixed = jnp.concatenate([_dot(wc[2 * t], vm[2 * t]) + _dot(wc[2 * t + 1], vm[2 * t + 1]) for t in range(2)],
                            axis=1) + bias_ref[...]
    return dict(dge=dge, u=u, rstd=rstd, vhat=vhat, vm=vm, mixed=mixed)


SGU_ROWS = SGU_CHUNKS_PER_STEP * CHUNK


def _sgu_chunk_rows():
    return [pl.ds(j * CHUNK, CHUNK) for j in range(SGU_CHUNKS_PER_STEP)]


def _sgu_specs():
    vec = pl.BlockSpec((1, SGU_W), lambda i: (0, 0))
    return [pl.BlockSpec((SGU_ROWS, W_UV), lambda i: (i, 0)), vec, vec,
            pl.BlockSpec((4, CHUNK, CHUNK), lambda i: (0, 0, 0)), pl.BlockSpec((CHUNK, SGU_W), lambda i: (0, 0))]


def _sgu_fwd(uv, ln_g, ln_b, w, bias):
    T = uv.shape[0]

    def body(uv_ref, g_ref, b_ref, w_ref, bias_ref, y_ref):
        wc = _sgu_weights(w_ref)
        for rows in _sgu_chunk_rows():
            s = _sgu_core(uv_ref[rows, :], g_ref, b_ref, wc, bias_ref)
            y_ref[rows, :] = s["u"] * s["mixed"]

    return _call(body, name="sgu_fwd", grid=(T // SGU_ROWS,), in_specs=_sgu_specs(),
                 out_specs=pl.BlockSpec((SGU_ROWS, SGU_W), lambda i: (i, 0)),
                 out_shape=jax.ShapeDtypeStruct((T, SGU_W), F32), sem=("parallel",))(uv, ln_g, ln_b, w, bias)


def _sgu_bwd(uv, dy, ln_g, ln_b, w, bias):
    T = uv.shape[0]

    def body(uv_ref, dy_ref, g_ref, b_ref, w_ref, bias_ref, dx_ref, dw_ref, dbias_ref, dln_ref):
        @pl.when(pl.program_id(0) == 0)
        def _():
            dw_ref[...] = jnp.zeros_like(dw_ref)
            dbias_ref[...] = jnp.zeros_like(dbias_ref)
            dln_ref[...] = jnp.zeros_like(dln_ref)

        tri = _tri()
        wc = _sgu_weights(w_ref)
        for rows in _sgu_chunk_rows():
            s = _sgu_core(uv_ref[rows, :], g_ref, b_ref, wc, bias_ref)
            dy_ = dy_ref[rows, :]
            du = dy_ * s["mixed"]
            dmix = dy_ * s["u"]
            dbias_ref[...] += dmix
            dvn_t = []
            for t in range(2):
                acc = jnp.zeros((CHUNK, 128), F32)
                for e in range(2):
                    gi = 2 * t + e
                    dmg = jnp.where(_lane_mask(e), dmix[:, 128 * t:128 * (t + 1)], 0.0).astype(BF)
                    acc = acc + _dot_tn(wc[gi], dmg)
                    dw_ref[gi] += jnp.where(tri, _dot_nt(dmg, s["vm"][gi]), 0.0)
                dvn_t.append(acc)
            dvn = jnp.concatenate(dvn_t, axis=1)
            dln_ref[0:1, :] += jnp.sum(dvn * s["vhat"], axis=0, keepdims=True)
            dln_ref[1:2, :] += jnp.sum(dvn, axis=0, keepdims=True)
            dvh = dvn * g_ref[...]
            dv = s["rstd"] * (dvh - jnp.mean(dvh, axis=-1, keepdims=True)
                              - s["vhat"] * jnp.mean(dvh * s["vhat"], axis=-1, keepdims=True))
            dx_ref[rows, :] = (jnp.concatenate([du, dv], axis=1) * s["dge"]).astype(BF)

    ins = _sgu_specs()
    return _call(body, name="sgu_bwd", grid=(T // SGU_ROWS,),
                 in_specs=[ins[0], pl.BlockSpec((SGU_ROWS, SGU_W), lambda i: (i, 0))] + ins[1:],
                 out_specs=[pl.BlockSpec((SGU_ROWS, W_UV), lambda i: (i, 0)),
                            pl.BlockSpec((4, CHUNK, CHUNK), lambda i: (0, 0, 0)),
                            pl.BlockSpec((CHUNK, SGU_W), lambda i: (0, 0)), pl.BlockSpec((8, SGU_W), lambda i: (0, 0))],
                 out_shape=[jax.ShapeDtypeStruct((T, W_UV), BF), jax.ShapeDtypeStruct((4, CHUNK, CHUNK), F32),
                            jax.ShapeDtypeStruct((CHUNK, SGU_W), F32), jax.ShapeDtypeStruct((8, SGU_W), F32)],
                 sem=("arbitrary",))(uv, dy, ln_g, ln_b, w, bias)


def _adamw(w, g, m, v):
    R, C = w.shape
    tr = R

    def body(w_ref, g_ref, m_ref, v_ref, d_ref, nm_ref, nv_ref):
        g_ = g_ref[...]
        m2 = ADAM_B1 * m_ref[...] + (1.0 - ADAM_B1) * g_
        v2 = ADAM_B2 * v_ref[...] + (1.0 - ADAM_B2) * (g_ * g_)
        m_hat = m2 / (1.0 - ADAM_B1 ** ADAM_STEP)
        v_hat = v2 / (1.0 - ADAM_B2 ** ADAM_STEP)
        d_ref[...] = -ADAM_LR * (m_hat / (jnp.sqrt(v_hat) + ADAM_EPS) + ADAM_WD * w_ref[...])
        nm_ref[...] = m2
        nv_ref[...] = v2

    blk = pl.BlockSpec((tr, C), lambda i: (i, 0))
    sh = jax.ShapeDtypeStruct((R, C), F32)
    return _call(body, name="adamw", grid=(R // tr,), in_specs=[blk] * 4, out_specs=[blk] * 3,
                 out_shape=[sh] * 3, sem=("parallel",))(w, g, m, v)


def _adamw_pair(w, g0, g1, m, v, dep):
    L, R, C = w.shape
    tr = max(t for t in range(8, R + 1, 8) if R % t == 0 and t * C * 4 <= 3 * 2 ** 19)

    def body(w_ref, g0_ref, g1_ref, m_ref, v_ref, dep_ref, d_ref, nm_ref, nv_ref, og_ref):
        g_ = jnp.where(pl.program_id(0) == 0, g0_ref[...], g1_ref[...])
        m2 = ADAM_B1 * m_ref[...] + (1.0 - ADAM_B1) * g_
        v2 = ADAM_B2 * v_ref[...] + (1.0 - ADAM_B2) * (g_ * g_)
        m_hat = m2 / (1.0 - ADAM_B1 ** ADAM_STEP)
        v_hat = v2 / (1.0 - ADAM_B2 ** ADAM_STEP)
        d_ref[...] = -ADAM_LR * (m_hat / (jnp.sqrt(v_hat) + ADAM_EPS) + ADAM_WD * w_ref[...])
        nm_ref[...] = m2
        nv_ref[...] = v2
        og_ref[...] = g_

    lay = pl.BlockSpec((None, tr, C), lambda l, i: (l, i, 0))
    one = lambda k: pl.BlockSpec((tr, C), lambda l, i: (jnp.where(l == k, i, 0), 0))
    return _call(body, name="adamw_pair", grid=(L, R // tr),
                 in_specs=[lay, one(0), one(1), lay, lay, pl.BlockSpec((8, 128), lambda l, i: (0, 0))],
                 out_specs=[lay] * 4,
                 out_shape=[jax.ShapeDtypeStruct((L, R, C), F32)] * 4,
                 sem=("parallel", "parallel"))(w, g0, g1, m, v, dep)


def _row_steps(rows):
    return 2 if rows % 32 == 0 else 1


def _pair_add(gbufs, rsibs, c):
    n = len(gbufs)
    steps = min(_row_steps(g.shape[2]) for g in gbufs)

    def body(c_ref, *refs):
        for a_ref, b_ref, o_ref in zip(refs[:n], refs[n:2 * n], refs[2 * n:]):
            o_ref[...] = (a_ref[...] + b_ref[...]).astype(BF)

    def specs(g):
        tr, C = g.shape[2] // steps, g.shape[3]
        return (pl.BlockSpec((None, None, tr, C), lambda j, i, c_ref: (j, c_ref[0], i, 0)),
                pl.BlockSpec((None, tr, C), lambda j, i, c_ref: (j, i, 0)))

    return list(pl.pallas_call(
        body, name="rs_pair_add",
        grid_spec=pltpu.PrefetchScalarGridSpec(
            num_scalar_prefetch=1, grid=(4, steps),
            in_specs=[specs(g)[0] for g in gbufs] + [specs(g)[1] for g in gbufs],
            out_specs=[specs(g)[1] for g in gbufs]),
        out_shape=[jax.ShapeDtypeStruct((4,) + g.shape[2:], BF) for g in gbufs],
        compiler_params=pltpu.CompilerParams(dimension_semantics=("parallel", "parallel")),
    )(jnp.reshape(c, (1,)).astype(jnp.int32), *gbufs, *rsibs))


def _chip_sum(pairs, recvs, me, c):
    n = len(pairs)
    steps = min(_row_steps(p.shape[1]) for p in pairs)

    def body(s_ref, *refs):
        for own_ref, p_ref, o_ref in zip(refs[:n], refs[n:2 * n], refs[2 * n:]):
            p = [jnp.where(s_ref[0] == j, own_ref[...], p_ref[j]).astype(F32) for j in range(4)]
            o_ref[...] = ((p[0] + p[1]) + p[2]) + p[3]

    def specs(p):
        tr, C = p.shape[1] // steps, p.shape[2]
        return (pl.BlockSpec((None, tr, C), lambda i, s: (s[0], i, 0)), pl.BlockSpec((4, tr, C), lambda i, s: (0, i, 0)),
                pl.BlockSpec((None, tr, C), lambda i, s: (s[1], i, 0)))

    return list(pl.pallas_call(
        body, name="rs_chip_sum",
        grid_spec=pltpu.PrefetchScalarGridSpec(
            num_scalar_prefetch=1, grid=(steps,),
            in_specs=[specs(p)[0] for p in pairs] + [specs(p)[1] for p in pairs],
            out_specs=[specs(p)[2] for p in pairs]),
        out_shape=[jax.ShapeDtypeStruct((2,) + p.shape[1:], F32) for p in pairs],
        compiler_params=pltpu.CompilerParams(dimension_semantics=("parallel",)),
    )(jnp.stack([me, c]).astype(jnp.int32), *pairs, *recvs))


MESH = pl.DeviceIdType.MESH
ANY = pl.BlockSpec(memory_space=pl.ANY)


def _place():
    x, y, c = lax.axis_index("x"), lax.axis_index("y"), lax.axis_index("c")
    return x, y, c, [(1 - x, y), (x, 1 - y), (1 - x, 1 - y)]


HBM = pl.BlockSpec(memory_space=pltpu.HBM)
SEM = pl.BlockSpec(memory_space=pltpu.SEMAPHORE)
EFFECT = pltpu.SideEffectType.DATAFLOW_SIDE_EFFECTING


class _Split:
    def __init__(self, tag, arrays, copies, n_copies, after=()):
        self.tag, self.copies, k = tag, copies, len(arrays)

        def body(*refs):
            sems = k + len(after)
            for cp in copies(refs[:k], refs[sems], refs[sems + 1]):
                cp.start()
            refs[-1][...] = jnp.zeros_like(refs[-1])

        out = pl.pallas_call(
            body, name=tag + "_start",
            out_shape=(pltpu.SemaphoreType.DMA((n_copies,)), pltpu.SemaphoreType.DMA((n_copies,)),
                       *[pltpu.HBM(a.shape, a.dtype) for a in arrays], jax.ShapeDtypeStruct((8, 128), F32)),
            in_specs=[HBM] * k + [ANY] * len(after),
            out_specs=(SEM, SEM, *[HBM] * k, pl.BlockSpec(memory_space=pltpu.VMEM)),
            input_output_aliases={i: 2 + i for i in range(k)},
            compiler_params=pltpu.CompilerParams(has_side_effects=EFFECT),
        )(*[pltpu.with_memory_space_constraint(a, pltpu.HBM) for a in arrays], *after)
        self.send, self.recv, self.arrays, self.token_array = out[0], out[1], list(out[2:2 + k]), out[-1]
        self.token = self.token_array[0, 0]

    def wait(self, after):
        k, copies = len(self.arrays), self.copies
        after = list(after) if isinstance(after, (list, tuple)) else [after]

        def body(*refs):
            for cp in copies(refs[:k], refs[k], refs[k + 1]):
                cp.wait_send()
                cp.wait_recv()

        return list(pl.pallas_call(
            body, name=self.tag + "_wait", out_shape=tuple(pltpu.HBM(a.shape, a.dtype) for a in self.arrays),
            in_specs=[HBM] * k + [SEM, SEM] + [ANY] * len(after), out_specs=tuple([HBM] * k),
            input_output_aliases={i: i for i in range(k)},
            compiler_params=pltpu.CompilerParams(has_side_effects=EFFECT),
        )(*self.arrays, self.send, self.recv, *after))


def _landing_zones(arrs):
    me = 2 * lax.axis_index("x") + lax.axis_index("y")
    return [lax.dynamic_update_index_in_dim(lax.empty((4,) + a.shape, a.dtype), a, me, 0) for a in arrs]


def _gather_start(arrs, lands, tag, after=()):
    n = len(arrs)

    def copies(refs, send, recv):
        x, y, c, chips = _place()
        return [pltpu.make_async_remote_copy(
            src_ref=refs[k], dst_ref=refs[n + k].at[2 * x + y], send_sem=send.at[3 * k + r],
            recv_sem=recv.at[3 * k + r], device_id=(px, py, c), device_id_type=MESH)
            for k in range(n) for r, (px, py) in enumerate(chips)]

    return _Split("gather_" + tag, list(arrs) + lands, copies, 3 * n, after)


def _gather_halves_start(arrs, tag):
    n = len(arrs)
    lands = _landing_zones(arrs)

    def copies(refs, send, recv):
        x, y, c, chips = _place()
        return [pltpu.make_async_remote_copy(
            src_ref=refs[k].at[c], dst_ref=refs[n + k].at[2 * x + y, c], send_sem=send.at[3 * k + r],
            recv_sem=recv.at[3 * k + r], device_id=(px, py, c), device_id_type=MESH)
            for k in range(n) for r, (px, py) in enumerate(chips)]

    return _Split("gather_" + tag, list(arrs) + lands, copies, 3 * n)


def _gather_halves_finish(lands, tag):
    n = len(lands)

    def copies(refs, send, recv):
        x, y, c, chips = _place()
        return [pltpu.make_async_remote_copy(
            src_ref=refs[k].at[2 * px + py, c], dst_ref=refs[k].at[2 * px + py, c], send_sem=send.at[3 * k + r],
            recv_sem=recv.at[3 * k + r], device_id=(x, y, 1 - c), device_id_type=MESH)
            for k in range(n) for r, (px, py) in enumerate(chips)]

    return _Split("gather_pass_" + tag, list(lands), copies, 3 * n)


def _part_sibling(gbufs):
    n = len(gbufs)

    def copies(refs, send, recv, off):
        x, y, c, _ = _place()
        return [pltpu.make_async_remote_copy(
            src_ref=refs[k].at[j, 1 - c], dst_ref=refs[n + k].at[j], send_sem=send.at[off + 4 * k + j],
            recv_sem=recv.at[off + 4 * k + j], device_id=(x, y, 1 - c), device_id_type=MESH)
            for k in range(n) for j in range(4)]

    return list(gbufs) + [lax.empty((4,) + g.shape[2:], g.dtype) for g in gbufs], 4 * n, copies


def _part_chips(pbufs):
    n = len(pbufs)

    def copies(refs, send, recv, off):
        x, y, c, chips = _place()
        return [pltpu.make_async_remote_copy(
            src_ref=refs[k].at[2 * px + py], dst_ref=refs[n + k].at[2 * x + y], send_sem=send.at[off + 3 * k + r],
            recv_sem=recv.at[off + 3 * k + r], device_id=(px, py, c), device_id_type=MESH)
            for k in range(n) for r, (px, py) in enumerate(chips)]

    return list(pbufs) + [lax.empty(p.shape, p.dtype) for p in pbufs], 3 * n, copies


def _part_join(fulls):
    def copies(refs, send, recv, off):
        x, y, c, _ = _place()
        return [pltpu.make_async_remote_copy(
            src_ref=refs[k].at[c], dst_ref=refs[k].at[c], send_sem=send.at[off + k], recv_sem=recv.at[off + k],
            device_id=(x, y, 1 - c), device_id_type=MESH) for k in range(len(fulls))]

    return list(fulls), len(fulls), copies


def _start_parts(parts, tag):
    arrays, spans, total = [], [], 0
    for arrs, n_copies, fn in parts:
        spans.append((len(arrays), len(arrs), total, fn))
        arrays += arrs
        total += n_copies

    def copies(refs, send, recv):
        return [cp for a0, na, off, fn in spans for cp in fn(refs[a0:a0 + na], send, recv, off)]

    op = _Split(tag, arrays, copies, total)
    op.spans = [(a0, na) for a0, na, _, _ in spans]
    return op


def _all_reduce_small(v):
    R, C = v.shape

    def body(v_ref, o_ref, g_ref, send, recv, loc):
        x, y, c, chips = _place()
        me, sibling = (x, y, c), (x, y, 1 - c)

        def rows(px, py, pc):
            return g_ref.at[4 * px + 2 * py + pc]

        def copy(k, block, to, src=None):
            return pltpu.make_async_remote_copy(
                src_ref=rows(*block) if src is None else src, dst_ref=rows(*block),
                send_sem=send.at[k], recv_sem=recv.at[k], device_id=to, device_id_type=MESH)

        mine = pltpu.make_async_copy(v_ref, rows(*me), loc)
        mine.start()
        first = [copy(0, me, sibling, src=v_ref)]
        first += [copy(1 + j, me, (*chip, c), src=v_ref) for j, chip in enumerate(chips)]
        for cp in first:
            cp.start()
        passed = [copy(4 + j, (*chip, c), sibling) for j, chip in enumerate(chips)]
        for j, chip in enumerate(chips):
            copy(1 + j, (*chip, c), me).wait_recv()
            passed[j].start()
        copy(0, sibling, me).wait_recv()
        for j, chip in enumerate(chips):
            copy(4 + j, (*chip, 1 - c), me).wait_recv()
        for cp in first + passed:
            cp.wait_send()
        mine.wait()
        acc = g_ref[0]
        for d in range(1, 8):
            acc = acc + g_ref[d]
        o_ref[...] = acc

    vm = pl.BlockSpec(memory_space=pltpu.VMEM)
    return pl.pallas_call(
        body, name="all_reduce_small", in_specs=[vm], out_specs=[vm, vm],
        out_shape=[jax.ShapeDtypeStruct((R, C), F32), jax.ShapeDtypeStruct((8, R, C), F32)],
        scratch_shapes=[pltpu.SemaphoreType.DMA((7,)), pltpu.SemaphoreType.DMA((7,)), pltpu.SemaphoreType.DMA],
    )(v)[0]


WEIGHTS = ['ffn1_norm', 'ffn1_w_gate', 'ffn1_w_up', 'ffn1_w_down', 'mix_norm', 'w_in', 'conv_w', 'conv_b', 'dt_bias',
           'a_log', 'd_skip', 'ssd_norm', 'sgu_ln_g', 'sgu_ln_b', 'sgu_w', 'sgu_b', 'w_out', 'ffn2_norm',
           'ffn2_w_gate', 'ffn2_w_up', 'ffn2_w_down', 'final_norm']
SHARDED = ['ffn1_w_gate', 'ffn1_w_up', 'ffn1_w_down', 'w_in', 'conv_w', 'w_out', 'ffn2_w_gate', 'ffn2_w_up',
           'ffn2_w_down']
SMALL = [n for n in WEIGHTS if n not in SHARDED]
GROUPS = [("ffn1", ["ffn1_w_gate", "ffn1_w_up", "ffn1_w_down"]), ("mix", ["w_in", "conv_w", "w_out"]),
          ("ffn2", ["ffn2_w_gate", "ffn2_w_up", "ffn2_w_down"])]
TRANSPOSED = ("ffn1_w_gate", "ffn1_w_up", "ffn2_w_gate", "ffn2_w_up")
DEPTH = 2


def _pack_w_in(w):
    return jnp.concatenate([w[..., 0:1152], w[..., 1536:2432], w[..., 1152:1536],
                            jnp.repeat(w[..., 2432:2438], HEAD, axis=-1), w[..., 2438:2950]], axis=-1)


def _unpack_w_in(dq, ds, du):
    return jnp.concatenate([dq, ds[:, 896:1280], ds[:, 0:896], ds[:, 1280::HEAD], du], axis=-1)


def _ffn_fwd(x, g, wg, wu, wd):
    xo, hb, S1, S2, A = _ffn_fwd_k(x, g, wg, wu, wd)
    return xo, (x, hb, S1, S2, A)


def _ffn_bwd_weights(dxo, saved, wd):
    x, hb, S1, S2, A = saved
    dG, dU, dyb = _ffn_bwd_act(dxo, S1, S2, wd)
    return (dG, dU), _ffn_bwd_k2(hb, dyb, A, dG, dU)


def _ffn_bwd_input(dxo, saved, mids, g, wg, wu):
    return _ffn_bwd_dx(mids[0], mids[1], wg, wu, saved[0], g, dxo)


def _mix_fwd(x, P):
    hb, qkv, sin, uv = _mix_proj(x, P["mix_norm"], P["w_in"])
    y_att, lse = _attn_combine([_attn_fwd(qkv, d) for d in DILATIONS])
    y_ssd, hprev = _ssd_fwd(sin, *P["ssd"])
    y_sgu = _sgu_fwd(uv, *P["sgu"])
    ycat = jnp.concatenate([y_att, y_ssd, y_sgu], axis=1).astype(BF)
    return _mm_nn(ycat, P["w_out"], res=x), (x, hb, qkv, sin, uv, y_att, lse, hprev, ycat)


def _mix_bwd_weights(dxo, saved, P):
    x, hb, qkv, sin, uv, y_att, lse, hprev, ycat = saved
    dy_att, dy_ssd, dy_sgu = _mix_bwd_dy(dxo, P["w_out"])
    dwout = _mm_tn(ycat, dxo)
    dqkv = _sum_branches([_attn_bwd(qkv, dy_att, y_att, lse, d) for d in DILATIONS])
    dsin, dcw, dcb, dvec = _ssd_bwd(sin, hprev, dy_ssd, *P["ssd"])
    duv, dsw, dsbias, dln = _sgu_bwd(uv, dy_sgu, *P["sgu"])
    dwin = _unpack_w_in(_mm_tn(hb, dqkv), _mm_tn(hb, dsin), _mm_tn(hb, duv))
    grads = dict(
        w_in=dwin, conv_w=dcw, conv_b=dcb[0], dt_bias=dvec[0, ::HEAD], a_log=dvec[1, ::HEAD],
        d_skip=jnp.sum(dvec[2].reshape(6, HEAD), axis=-1), ssd_norm=dvec[3], sgu_ln_g=dln[0], sgu_ln_b=dln[1],
        sgu_w=dsw, sgu_b=jnp.sum(dsbias.reshape(CHUNK, 4, HEAD), axis=-1).T, w_out=dwout)
    return (dqkv, dsin, duv), grads


def _mix_bwd_input(dxo, saved, mids, P):
    return _mix_bwd_dx(*mids, P["w_in"], saved[0], P["mix_norm"], dxo)


def _halved(g):
    rows = g.size // g.shape[-1]
    return g.reshape(4, 2, rows // 8, g.shape[-1])


def kernel(x, ffn1_norm, ffn1_w_gate, ffn1_w_up, ffn1_w_down, mix_norm, w_in, conv_w, conv_b, dt_bias, a_log, d_skip, ssd_norm, sgu_ln_g, sgu_ln_b, sgu_w, sgu_b, w_out, ffn2_norm, ffn2_w_gate, ffn2_w_up, ffn2_w_down, final_norm, loss_target, m_ffn1_norm, m_ffn1_w_gate, m_ffn1_w_up, m_ffn1_w_down, m_mix_norm, m_w_in, m_conv_w, m_conv_b, m_dt_bias, m_a_log, m_d_skip, m_ssd_norm, m_sgu_ln_g, m_sgu_ln_b, m_sgu_w, m_sgu_b, m_w_out, m_ffn2_norm, m_ffn2_w_gate, m_ffn2_w_up, m_ffn2_w_down, m_final_norm, v_ffn1_norm, v_ffn1_w_gate, v_ffn1_w_up, v_ffn1_w_down, v_mix_norm, v_w_in, v_conv_w, v_conv_b, v_dt_bias, v_a_log, v_d_skip, v_ssd_norm, v_sgu_ln_g, v_sgu_ln_b, v_sgu_w, v_sgu_b, v_w_out, v_ffn2_norm, v_ffn2_w_gate, v_ffn2_w_up, v_ffn2_w_down, v_final_norm):
    given = dict(x=x, ffn1_norm=ffn1_norm, ffn1_w_gate=ffn1_w_gate, ffn1_w_up=ffn1_w_up, ffn1_w_down=ffn1_w_down, mix_norm=mix_norm, w_in=w_in, conv_w=conv_w, conv_b=conv_b, dt_bias=dt_bias, a_log=a_log, d_skip=d_skip, ssd_norm=ssd_norm, sgu_ln_g=sgu_ln_g, sgu_ln_b=sgu_ln_b, sgu_w=sgu_w, sgu_b=sgu_b, w_out=w_out, ffn2_norm=ffn2_norm, ffn2_w_gate=ffn2_w_gate, ffn2_w_up=ffn2_w_up, ffn2_w_down=ffn2_w_down, final_norm=final_norm, loss_target=loss_target, m_ffn1_norm=m_ffn1_norm, m_ffn1_w_gate=m_ffn1_w_gate, m_ffn1_w_up=m_ffn1_w_up, m_ffn1_w_down=m_ffn1_w_down, m_mix_norm=m_mix_norm, m_w_in=m_w_in, m_conv_w=m_conv_w, m_conv_b=m_conv_b, m_dt_bias=m_dt_bias, m_a_log=m_a_log, m_d_skip=m_d_skip, m_ssd_norm=m_ssd_norm, m_sgu_ln_g=m_sgu_ln_g, m_sgu_ln_b=m_sgu_ln_b, m_sgu_w=m_sgu_w, m_sgu_b=m_sgu_b, m_w_out=m_w_out, m_ffn2_norm=m_ffn2_norm, m_ffn2_w_gate=m_ffn2_w_gate, m_ffn2_w_up=m_ffn2_w_up, m_ffn2_w_down=m_ffn2_w_down, m_final_norm=m_final_norm, v_ffn1_norm=v_ffn1_norm, v_ffn1_w_gate=v_ffn1_w_gate, v_ffn1_w_up=v_ffn1_w_up, v_ffn1_w_down=v_ffn1_w_down, v_mix_norm=v_mix_norm, v_w_in=v_w_in, v_conv_w=v_conv_w, v_conv_b=v_conv_b, v_dt_bias=v_dt_bias, v_a_log=v_a_log, v_d_skip=v_d_skip, v_ssd_norm=v_ssd_norm, v_sgu_ln_g=v_sgu_ln_g, v_sgu_ln_b=v_sgu_ln_b, v_sgu_w=v_sgu_w, v_sgu_b=v_sgu_b, v_w_out=v_w_out, v_ffn2_norm=v_ffn2_norm, v_ffn2_w_gate=v_ffn2_w_gate, v_ffn2_w_up=v_ffn2_w_up, v_ffn2_w_down=v_ffn2_w_down, v_final_norm=v_final_norm)
    T = given["x"].shape[0] * given["x"].shape[1]
    D = given["x"].shape[2]
    x0 = given["x"].reshape(T, D)
    tgt = given["loss_target"].reshape(T, D)
    c = lax.axis_index("c")

    bf = {n: given[n].astype(BF) for n in SHARDED if n not in ("w_in", "conv_w")}
    bf["w_in"] = _pack_w_in(given["w_in"]).astype(BF)
    bf["conv_w"] = given["conv_w"]
    first_key = (0, GROUPS[0][0])
    first = [bf[n][0].reshape((2, bf[n].shape[1] // 2) + bf[n].shape[2:]) for n in GROUPS[0][1]]
    gathers = {first_key: _gather_halves_start(first, "l0_" + GROUPS[0][0])}
    later = {(i, gname): [bf[n][i] for n in names] for i in range(DEPTH) for gname, names in GROUPS
             if (i, gname) != first_key}
    zones = {key: _landing_zones(arrs) for key, arrs in later.items()}

    def gathered(i, gname, after):
        if (i, gname) != first_key:
            return gathers[(i, gname)].wait(after)[3:]
        got = gathers[first_key].wait([after] + [z for zs in zones.values() for z in zs])[3:]
        got = _gather_halves_finish(got, "l0_" + gname).wait(after)
        prev = got[0]
        for key, arrs in later.items():
            gathers[key] = _gather_start(arrs, zones[key], f"l{key[0]}_{key[1]}", after=[prev])
            prev = gathers[key].token_array
        return [z.reshape((4, 2 * z.shape[2]) + z.shape[3:]) for z in got]

    def mix_params(i, got):
        win = got[0].reshape(D, W_QKV + W_SSD + W_UV)
        rep = lambda v: jnp.repeat(v, HEAD)[None]
        ssd = (got[1].transpose(1, 0, 2).reshape(4, SSD_CONV_DIM), given["conv_b"][i][None],
               rep(given["dt_bias"][i]), rep(given["a_log"][i]), rep(given["d_skip"][i]), given["ssd_norm"][i][None])
        sgu = (given["sgu_ln_g"][i][None], given["sgu_ln_b"][i][None], given["sgu_w"][i],
               jnp.repeat(given["sgu_b"][i].T, HEAD, axis=1))
        return dict(mix_norm=given["mix_norm"][i][None], w_in=win, w_out=got[2].reshape(-1, D), ssd=ssd, sgu=sgu)

    x = x0
    tape = []
    for i in range(DEPTH):
        got = gathered(i, "ffn1", x)
        token = functools.reduce(lambda a, b: a + b, [g.token for g in gathers.values()]) if i == 0 else 0.0
        P = dict(ffn1=(given["ffn1_norm"][i][None] + token, *got))
        x, s1 = _ffn_fwd(x, *P["ffn1"])
        P.update(mix_params(i, gathered(i, "mix", x)))
        x, s2 = _mix_fwd(x, P)
        P["ffn2"] = (given["ffn2_norm"][i][None], *gathered(i, "ffn2", x))
        x, s3 = _ffn_fwd(x, *P["ffn2"])
        tape.append((P, s1, s2, s3))
    loss_part, dx, dgf = _final_loss(x, given["final_norm"][None], tgt)

    me = 2 * lax.axis_index("x") + lax.axis_index("y")
    jobs = []

    flight = dict(op=None, owners=[], ticks=0)

    def tick(after, begin=None):
        parts, owners = [], []
        if flight["op"] is not None:
            got = flight["op"].wait(after)
            for job, (a0, na) in zip(flight["owners"], flight["op"].spans):
                mine, k = got[a0:a0 + na], len(job["names"])
                if job["stage"] == 1:
                    parts.append(_part_chips(_pair_add(mine[:k], mine[k:], c)))
                elif job["stage"] == 2:
                    parts.append(_part_join(_chip_sum(mine[:k], mine[k:], me, c)))
                else:
                    job.update(stage=4, out=dict(zip(job["names"], mine)))
                    continue
                job["stage"] += 1
                owners.append(job)
        if begin is not None:
            i, gname, gd = begin
            names = [n for n in dict(GROUPS)[gname] if n != "conv_w"]
            jobs.append(dict(key=(i, gname), names=names, stage=1))
            parts.append(_part_sibling([_halved(gd[n]) for n in names]))
            owners.append(jobs[-1])
        flight.update(op=_start_parts(parts, f"rs_tick{flight['ticks']}") if parts else None, owners=owners,
                      ticks=flight["ticks"] + 1)
        return flight["op"].token if parts else 0.0

    grads = [dict() for _ in range(DEPTH)]
    for i in reversed(range(DEPTH)):
        P, s1, s2, s3 = tape[i]
        g = grads[i]
        norm, wg, wu, wd = P["ffn2"]
        mids, (g["ffn2_w_gate"], g["ffn2_w_up"], g["ffn2_w_down"]) = _ffn_bwd_weights(dx, s3, wd)
        tok = tick(g["ffn2_w_down"], (i, "ffn2", g))
        dx, dn2 = _ffn_bwd_input(dx, s3, mids, norm + tok, wg, wu)
        mids, gm = _mix_bwd_weights(dx, s2, P)
        g.update(gm)
        tok = tick(gm["w_in"], (i, "mix", g))
        dx, dnm = _mix_bwd_input(dx, s2, mids, {**P, "mix_norm": P["mix_norm"] + tok})
        norm, wg, wu, wd = P["ffn1"]
        mids, (g["ffn1_w_gate"], g["ffn1_w_up"], g["ffn1_w_down"]) = _ffn_bwd_weights(dx, s1, wd)
        tok = tick(g["ffn1_w_down"], (i, "ffn1", g))
        dx, dn1 = _ffn_bwd_input(dx, s1, mids, norm + tok, wg, wu)
        g["ffn1_norm"], g["mix_norm"], g["ffn2_norm"] = dn1[0], dnm[0], dn2[0]
    grad_x = dx.reshape(given["x"].shape)

    order = [n for n in SMALL if n != "final_norm"] + ["final_norm"]
    small = [jnp.stack([grads[i][n] for i in range(DEPTH)]) for n in order[:-1] + ["conv_w"]]
    small = small[:-1] + [dgf[0], small[-1], loss_part[0, 0:1]]
    n_small = sum(s.size for s in small)
    rows_small = -(-n_small // (128 * 8)) * 8

    def flat(arrs):
        fill = rows_small * 128 - sum(a.size for a in arrs)
        return jnp.concatenate([a.reshape(-1) for a in arrs] + [jnp.zeros((fill,), F32)]).reshape(rows_small, 128)

    gsmall = _all_reduce_small(flat(small)).reshape(-1)

    grad_w = {}
    off = 0
    for n in order:
        size = given[n].size
        grad_w[n] = gsmall[off:off + size].reshape(given[n].shape)
        off += size
    cw = gsmall[off:off + 2 * 4 * SSD_CONV_DIM].reshape(DEPTH, 4, SSD_CONV_DIM)
    grad_w["conv_w"] = lax.dynamic_slice_in_dim(cw, me * (SSD_CONV_DIM // 4), SSD_CONV_DIM // 4, axis=2)
    loss = gsmall[off + 2 * 4 * SSD_CONV_DIM]

    delta, new_m, new_v = {}, {}, {}
    shp = given["conv_w"].shape
    d, m2, v2 = _adamw(*[a.reshape(shp[0] * shp[1], shp[2])
                         for a in (given["conv_w"], grad_w["conv_w"], given["m_conv_w"], given["v_conv_w"])])
    delta["conv_w"], new_m["conv_w"], new_v["conv_w"] = d.reshape(shp), m2.reshape(shp), v2.reshape(shp)
    packed = [flat([given[pre + n] for n in order]) for pre in ("", "m_", "v_")]
    small_out = _adamw(packed[0], gsmall.reshape(rows_small, 128), packed[1], packed[2])
    outs = [o.reshape(-1) for o in small_out]
    off = 0
    for n in order:
        size = given[n].size
        for dst, o in zip((delta, new_m, new_v), outs):
            dst[n] = o[off:off + size].reshape(given[n].shape)
        off += size

    stepped, arrived = {}, {}

    def update_arrived(dep):
        out = None
        for job in jobs:
            if job["stage"] == 4 and not job.get("seen"):
                job["seen"] = True
                for n, full in job["out"].items():
                    view = (lambda a: jnp.swapaxes(a, 1, 2)) if n in TRANSPOSED else (lambda a: a)
                    arrived.setdefault(n, {})[job["key"][0]] = full.reshape(view(given[n]).shape[1:])
                    if len(arrived[n]) == DEPTH:
                        res = _adamw_pair(view(given[n]), arrived[n][0], arrived[n][1], view(given["m_" + n]),
                                          view(given["v_" + n]), dep)
                        stepped[n] = [view(r) for r in res]
                        out = res[0]
        return out

    after = small_out[0]
    while any(j["stage"] < 4 for j in jobs):
        done = update_arrived(jnp.zeros((8, 128), F32) + tok)
        after = after if done is None else done
        tok = tick(after)
    update_arrived(jnp.zeros((8, 128), F32) + tok)
    for n, (d, m2, v2, g) in stepped.items():
        delta[n], new_m[n], new_v[n], grad_w[n] = d, m2, v2, g

    return (loss, grad_x, *[grad_w[n] for n in WEIGHTS], *[delta[n] for n in WEIGHTS],
            *[new_m[n] for n in WEIGHTS], *[new_v[n] for n in WEIGHTS])
```

```python
import functools
import math

import jax
import jax.numpy as jnp
from jax import lax
from jax.experimental import pallas as pl
from jax.experimental.pallas import tpu as pltpu

F32 = jnp.float32
BF = jnp.bfloat16

RMS_EPS = 1e-6
LN_EPS = 1e-5
SEQ = 2048
CHUNK = 128
N_CHUNK = SEQ // CHUNK
ATT_W = 384
HEAD = 64
SSD_W = 384
SSD_CONV_DIM = 896
SSD_STATE = 128
SGU_W = 256
DILATIONS = (1, 4, 16)
W_QKV = 3 * ATT_W
W_SSD = SSD_CONV_DIM + SSD_W + SSD_W
W_UV = 2 * SGU_W
ADAM_LR = 0.001
ADAM_B1 = 0.9
ADAM_B2 = 0.999
ADAM_EPS = 1e-08
ADAM_WD = 0.01
ADAM_STEP = 10
NEG = -1e30
ATTN_BWD_VMEM = 48 * 2 ** 20
ATTN_SUBSEQ_PER_STEP = 4
SGU_CHUNKS_PER_STEP = 4
FFN_VMEM = 60 * 2 ** 20


def _dot(a, b):
    return jnp.dot(a, b, preferred_element_type=F32)


def _dot_nt(a, b):
    return lax.dot_general(a, b, (((1,), (1,)), ((), ())), preferred_element_type=F32)


def _dot_tn(a, b):
    return lax.dot_general(a, b, (((0,), (0,)), ((), ())), preferred_element_type=F32)


def _sigmoid(x):
    return 1.0 / (1.0 + jnp.exp(-x))


def _call(body, *, name, grid, in_specs, out_specs, out_shape, scratch=(), sem=None, vmem=None):
    return pl.pallas_call(
        body, name=name, grid=grid, in_specs=in_specs, out_specs=out_specs, out_shape=out_shape,
        scratch_shapes=list(scratch),
        compiler_params=pltpu.CompilerParams(dimension_semantics=sem, vmem_limit_bytes=vmem),
    )


def _tile(n, want):
    t = min(n, want)
    while n % t:
        t //= 2
    return t


def _final_loss(x, g, tgt):
    T, D = x.shape
    tm = _tile(T, 512)

    def body(x_ref, g_ref, t_ref, l_ref, dx_ref, dg_ref):
        @pl.when(pl.program_id(0) == 0)
        def _():
            dg_ref[...] = jnp.zeros_like(dg_ref)
            l_ref[...] = jnp.zeros_like(l_ref)

        xf = x_ref[...]
        gg = g_ref[...]
        r = lax.rsqrt(jnp.mean(xf * xf, axis=-1, keepdims=True) + RMS_EPS)
        xn = xf * r
        e = xn * gg - t_ref[...]
        part = 0.5 * jnp.sum(jnp.mean(e * e, axis=-1, keepdims=True), axis=0, keepdims=True)
        l_ref[...] += jnp.broadcast_to(part, l_ref.shape)
        dy = e * (1.0 / D)
        u = dy * gg
        mu = jnp.mean(u * xf, axis=-1, keepdims=True)
        dx_ref[...] = r * (u - xf * (r * r * mu))
        dg_ref[...] += jnp.sum(dy * xn, axis=0, keepdims=True)

    row = pl.BlockSpec((tm, D), lambda i: (i, 0))
    vec = pl.BlockSpec((1, D), lambda i: (0, 0))
    lsp = pl.BlockSpec((1, 128), lambda i: (0, 0))
    return _call(body, name="final_loss", grid=(T // tm,), in_specs=[row, vec, row], out_specs=[lsp, row, vec],
                 out_shape=[jax.ShapeDtypeStruct((1, 128), F32), jax.ShapeDtypeStruct((T, D), F32),
                            jax.ShapeDtypeStruct((1, D), F32)],
                 sem=("arbitrary",))(x, g, tgt)


def _resident(shape):
    return pl.BlockSpec(shape, lambda *_: (0,) * len(shape), pipeline_mode=pl.Buffered(1))


def _ffn_fwd_k(x, gn, wg, wu, wd):
    T, D = x.shape
    NS, _, Fs = wg.shape
    tm = _tile(T, 1024)

    def body(x_ref, gn_ref, wg_ref, wu_ref, wd_ref, o_ref, h_ref, s1_ref, s2_ref, a_ref, hs, acc):
        j = pl.program_id(1)

        @pl.when(j == 0)
        def _():
            xf = x_ref[...]
            r = lax.rsqrt(jnp.mean(xf * xf, axis=-1, keepdims=True) + RMS_EPS)
            hs[...] = (xf * r * gn_ref[...]).astype(BF)
            h_ref[...] = hs[...]
            acc[...] = jnp.zeros_like(acc)

        h = hs[...]
        g = _dot(h, wg_ref[...])
        u = _dot(h, wu_ref[...])
        sg = _sigmoid(g)
        s1 = g * sg
        a = (s1 * u).astype(BF)
        s1_ref[...] = s1.astype(BF)
        s2_ref[...] = (u * (sg * (1.0 + g * (1.0 - sg)))).astype(BF)
        a_ref[...] = a
        acc[...] += _dot(a, wd_ref[...])

        @pl.when(j == NS - 1)
        def _():
            o_ref[...] = x_ref[...] + 0.5 * acc[...]

    row = pl.BlockSpec((tm, D), lambda i, j: (i, 0))
    act = pl.BlockSpec((None, tm, Fs), lambda i, j: (j, i, 0))
    sh = jax.ShapeDtypeStruct((NS, T, Fs), BF)
    wspec = lambda w: pl.BlockSpec((None,) + w.shape[1:], lambda i, j: (j, 0, 0))
    return _call(body, name="ffn_fwd", grid=(T // tm, NS),
                 in_specs=[row, pl.BlockSpec((1, D), lambda i, j: (0, 0)), wspec(wg), wspec(wu), wspec(wd)],
                 out_specs=[row, row, act, act, act],
                 out_shape=[jax.ShapeDtypeStruct((T, D), F32), jax.ShapeDtypeStruct((T, D), BF), sh, sh, sh],
                 scratch=[pltpu.VMEM((tm, D), BF), pltpu.VMEM((tm, D), F32)],
                 sem=("parallel", "arbitrary"), vmem=FFN_VMEM)(x, gn, wg, wu, wd)


def _ffn_bwd_act(dxo, s1, s2, wd):
    NS, T, Fs = s1.shape
    D = dxo.shape[1]
    tm = _tile(T, 1024)

    def body(dxo_ref, s1_ref, s2_ref, wd_ref, dg_ref, du_ref, dy_ref, dys):
        j = pl.program_id(1)

        @pl.when(j == 0)
        def _():
            dys[...] = (0.5 * dxo_ref[...]).astype(BF)
            dy_ref[...] = dys[...]

        da = _dot_nt(dys[...], wd_ref[j])
        dg_ref[...] = (da * s2_ref[...].astype(F32)).astype(BF)
        du_ref[...] = (da * s1_ref[...].astype(F32)).astype(BF)

    row = pl.BlockSpec((tm, D), lambda i, j: (i, 0))
    act = pl.BlockSpec((None, tm, Fs), lambda i, j: (j, i, 0))
    sh = jax.ShapeDtypeStruct((NS, T, Fs), BF)
    return _call(body, name="ffn_bwd_act", grid=(T // tm, NS), in_specs=[row, act, act, _resident(wd.shape)],
                 out_specs=[act, act, row], out_shape=[sh, sh, jax.ShapeDtypeStruct((T, D), BF)],
                 scratch=[pltpu.VMEM((tm, D), BF)], sem=("parallel", "arbitrary"))(dxo, s1, s2, wd)


def _ffn_bwd_dx(dg, du, wg, wu, x, gn, dxo):
    NS, T, Fs = dg.shape
    D = x.shape[1]
    tm = _tile(T, 1024)

    def body(dg_ref, du_ref, wg_ref, wu_ref, x_ref, gn_ref, dxo_ref, dx_ref, dgn_ref, acc):
        i, j = pl.program_id(0), pl.program_id(1)

        @pl.when((i == 0) & (j == 0))
        def _():
            dgn_ref[...] = jnp.zeros_like(dgn_ref)

        @pl.when(j == 0)
        def _():
            acc[...] = jnp.zeros_like(acc)

        acc[...] += _dot_nt(dg_ref[...], wg_ref[j]) + _dot_nt(du_ref[...], wu_ref[j])

        @pl.when(j == NS - 1)
        def _():
            xf = x_ref[...]
            r = lax.rsqrt(jnp.mean(xf * xf, axis=-1, keepdims=True) + RMS_EPS)
            dh = acc[...]
            uu = dh * gn_ref[...]
            mu = jnp.mean(uu * xf, axis=-1, keepdims=True)
            dx_ref[...] = dxo_ref[...] + r * (uu - xf * (r * r * mu))
            dgn_ref[...] += jnp.sum(dh * xf * r, axis=0, keepdims=True)

    row = pl.BlockSpec((tm, D), lambda i, j: (i, 0))
    vec = pl.BlockSpec((1, D), lambda i, j: (0, 0))
    act = pl.BlockSpec((None, tm, Fs), lambda i, j: (j, i, 0))
    return _call(body, name="ffn_bwd_dx", grid=(T // tm, NS),
                 in_specs=[act, act, _resident(wg.shape), _resident(wu.shape), row, vec, row], out_specs=[row, vec],
                 out_shape=[jax.ShapeDtypeStruct((T, D), F32), jax.ShapeDtypeStruct((1, D), F32)],
                 scratch=[pltpu.VMEM((tm, D), F32)], sem=("arbitrary", "arbitrary"), vmem=FFN_VMEM)(
        dg, du, wg, wu, x, gn, dxo)


def _ffn_bwd_k2(hb, dyb, a, dg, du):
    NS, T, Fs = a.shape
    D = hb.shape[1]
    tk = _tile(T, 1024)

    def body(h_ref, dy_ref, a_ref, dg_ref, du_ref, og_ref, ou_ref, od_ref):
        @pl.when(pl.program_id(1) == 0)
        def _():
            og_ref[...] = jnp.zeros_like(og_ref)
            ou_ref[...] = jnp.zeros_like(ou_ref)
            od_ref[...] = jnp.zeros_like(od_ref)

        h = h_ref[...]
        og_ref[...] += _dot_tn(dg_ref[...], h)
        ou_ref[...] += _dot_tn(du_ref[...], h)
        od_ref[...] += _dot_tn(a_ref[...], dy_ref[...])

    row = pl.BlockSpec((tk, D), lambda j, k: (k, 0))
    act = pl.BlockSpec((None, tk, Fs), lambda j, k: (j, k, 0))
    return _call(body, name="ffn_bwd_w", grid=(NS, T // tk), in_specs=[row, row, act, act, act],
                 out_specs=[pl.BlockSpec((None, Fs, D), lambda j, k: (j, 0, 0))] * 3,
                 out_shape=[jax.ShapeDtypeStruct((NS, Fs, D), F32)] * 3,
                 sem=("parallel", "arbitrary"))(hb, dyb, a, dg, du)


def _mix_out(parts, w_out, res):
    T, D = res.shape
    tm = _tile(T, 512)

    def body(*refs):
        w_ref, r_ref, o_ref, cat_ref = refs[len(parts):]
        cat = jnp.concatenate([p_ref[...].astype(BF) for p_ref in refs[:len(parts)]], axis=1)
        cat_ref[...] = cat
        o_ref[...] = r_ref[...] + _dot(cat, w_ref[...])

    row = lambda n: pl.BlockSpec((tm, n), lambda i: (i, 0))
    return _call(body, name="mix_out", grid=(T // tm,),
                 in_specs=[row(p.shape[1]) for p in parts] + [_resident(w_out.shape), row(D)],
                 out_specs=[row(D), row(D)],
                 out_shape=[jax.ShapeDtypeStruct((T, D), F32), jax.ShapeDtypeStruct((T, D), BF)],
                 sem=("parallel",))(*parts, w_out, res)


def _mix_bwd_dy(dxo, w_out):
    T, D = dxo.shape
    tm = _tile(T, 512)
    cuts = (0, ATT_W, ATT_W + SSD_W, ATT_W + SSD_W + SGU_W)

    def body(dx_ref, w_ref, a_ref, s_ref, g_ref):
        d = _dot_nt(dx_ref[...].astype(BF), w_ref[...])
        for o_ref, lo, hi in zip((a_ref, s_ref, g_ref), cuts[:-1], cuts[1:]):
            o_ref[...] = d[:, lo:hi]

    row = lambda w: pl.BlockSpec((tm, w), lambda i: (i, 0))
    return _call(body, name="mix_bwd_dy", grid=(T // tm,), in_specs=[row(D), _resident(w_out.shape)],
                 out_specs=[row(ATT_W), row(SSD_W), row(SGU_W)],
                 out_shape=[jax.ShapeDtypeStruct((T, w), F32) for w in (ATT_W, SSD_W, SGU_W)],
                 sem=("parallel",))(dxo, w_out)


def _mm_tn(a, b):
    T, M = a.shape
    N = b.shape[1]
    tk = _tile(T, 1024)
    tmm = _tile(M, 512)

    def body(a_ref, b_ref, o_ref):
        @pl.when(pl.program_id(1) == 0)
        def _():
            o_ref[...] = jnp.zeros_like(o_ref)

        o_ref[...] += _dot_tn(a_ref[...].astype(BF), b_ref[...].astype(BF))

    return _call(body, name="mm_tn", grid=(M // tmm, T // tk),
                 in_specs=[pl.BlockSpec((tk, tmm), lambda i, k: (k, i)), pl.BlockSpec((tk, N), lambda i, k: (k, 0))],
                 out_specs=pl.BlockSpec((tmm, N), lambda i, k: (i, 0)),
                 out_shape=jax.ShapeDtypeStruct((M, N), F32), sem=("parallel", "arbitrary"))(a, b)


def _mix_proj(x, gn, win):
    T, D = x.shape
    tm = _tile(T, 512)
    cuts = (0, W_QKV, W_QKV + W_SSD, W_QKV + W_SSD + W_UV)

    def body(x_ref, gn_ref, w_ref, h_ref, q_ref, s_ref, u_ref):
        xf = x_ref[...]
        r = lax.rsqrt(jnp.mean(xf * xf, axis=-1, keepdims=True) + RMS_EPS)
        h = (xf * r * gn_ref[...]).astype(BF)
        h_ref[...] = h
        for o_ref, lo, hi in zip((q_ref, s_ref, u_ref), cuts[:-1], cuts[1:]):
            o_ref[...] = _dot(h, w_ref[:, lo:hi])

    row = lambda w: pl.BlockSpec((tm, w), lambda i: (i, 0))
    return _call(body, name="mix_proj", grid=(T // tm,),
                 in_specs=[row(D), pl.BlockSpec((1, D), lambda i: (0, 0)), _resident(win.shape)],
                 out_specs=[row(D), row(W_QKV), row(W_SSD), row(W_UV)],
                 out_shape=[jax.ShapeDtypeStruct((T, D), BF), jax.ShapeDtypeStruct((T, W_QKV), F32),
                            jax.ShapeDtypeStruct((T, W_SSD), F32), jax.ShapeDtypeStruct((T, W_UV), F32)],
                 sem=("parallel",))(x, gn, win)


def _mix_bwd_dx(dqkv, dsin, duv, win, x, gn, dxo):
    T, D = x.shape
    tm = _tile(T, 512)
    cuts = (0, W_QKV, W_QKV + W_SSD, W_QKV + W_SSD + W_UV)

    def body(dq_ref, ds_ref, du_ref, w_ref, x_ref, gn_ref, dxo_ref, dx_ref, dgn_ref):
        @pl.when(pl.program_id(0) == 0)
        def _():
            dgn_ref[...] = jnp.zeros_like(dgn_ref)

        dh = (_dot_nt(dq_ref[...], w_ref[:, cuts[0]:cuts[1]]) + _dot_nt(ds_ref[...], w_ref[:, cuts[1]:cuts[2]])
              + _dot_nt(du_ref[...], w_ref[:, cuts[2]:cuts[3]]))
        xf = x_ref[...]
        r = lax.rsqrt(jnp.mean(xf * xf, axis=-1, keepdims=True) + RMS_EPS)
        uu = dh * gn_ref[...]
        mu = jnp.mean(uu * xf, axis=-1, keepdims=True)
        dx_ref[...] = dxo_ref[...] + r * (uu - xf * (r * r * mu))
        dgn_ref[...] += jnp.sum(dh * xf * r, axis=0, keepdims=True)

    row = lambda w: pl.BlockSpec((tm, w), lambda i: (i, 0))
    vec = pl.BlockSpec((1, D), lambda i: (0, 0))
    return _call(body, name="mix_bwd_dx", grid=(T // tm,),
                 in_specs=[row(W_QKV), row(W_SSD), row(W_UV), _resident(win.shape), row(D), vec, row(D)],
                 out_specs=[row(D), vec],
                 out_shape=[jax.ShapeDtypeStruct((T, D), F32), jax.ShapeDtypeStruct((1, D), F32)],
                 sem=("arbitrary",))(dqkv, dsin, duv, win, x, gn, dxo)


def _lane_mask(e, width=128):
    return (lax.broadcasted_iota(jnp.int32, (1, width), 1) // HEAD) == e


def _band_mask(n):
    qi = lax.broadcasted_iota(jnp.int32, (CHUNK, 2 * CHUNK), 0)
    kj = lax.broadcasted_iota(jnp.int32, (CHUNK, 2 * CHUNK), 1)
    dist = qi + CHUNK - kj
    return (dist >= 0) & (dist <= CHUNK) & ((kj >= CHUNK) | (n > 0))


def _sub_rows(r, block, dil):
    if dil == 1:
        return pl.ds(pl.multiple_of(block * CHUNK, CHUNK), CHUNK)
    return pl.ds(r + dil * CHUNK * block, CHUNK, stride=dil)


def _attn_specs(T, dil):
    per_step = ATTN_SUBSEQ_PER_STEP if dil == 1 else min(dil, 2 * ATTN_SUBSEQ_PER_STEP)
    qrows = CHUNK * (dil if dil > 1 else per_step)
    B, nbq = T // SEQ, SEQ // qrows
    once = dict(pipeline_mode=pl.Buffered(1))
    q_like = lambda col: pl.BlockSpec((qrows, 128), lambda b, n, r: (b * nbq + n, col), **(once if nbq == 1 else {}))
    k_like = lambda col: pl.BlockSpec((SEQ, 128), lambda b, n, r: (b, col), **once)
    return B, nbq, max(dil // per_step, 1), per_step, q_like, k_like


def _attn_step(u, dil, per_step):
    if dil > 1:
        r = pl.program_id(2) * per_step + u
        return r, pl.program_id(1), _sub_rows(r, 0, dil)
    return 0, pl.program_id(1) * per_step + u, pl.ds(CHUNK * u, CHUNK)


def _attn_fwd(qkv, dil):
    T = qkv.shape[0]
    B, nb, last, per_step, q_like, k_like = _attn_specs(T, dil)
    scale = HEAD ** -0.5

    def body(*refs):
        q_t, k_t, v_t, o_t, l_t = refs[0:3], refs[3:6], refs[6:9], refs[9:12], refs[12:15]
        for u in range(per_step):
            r, n, mine = _attn_step(u, dil, per_step)
            mask = _band_mask(n)
            cur, prv = _sub_rows(r, n, dil), _sub_rows(r, jnp.maximum(n - 1, 0), dil)
            for t in range(3):
                qt = q_t[t][mine, :].astype(BF)
                kt = jnp.concatenate([k_t[t][prv, :], k_t[t][cur, :]], axis=0).astype(BF)
                vt = jnp.concatenate([v_t[t][prv, :], v_t[t][cur, :]], axis=0).astype(BF)
                o_pair = jnp.zeros((CHUNK, 128), F32)
                l_pair = jnp.zeros((CHUNK, 128), F32)
                for e in range(2):
                    lm = _lane_mask(e)
                    s = _dot_nt(jnp.where(lm, qt, jnp.zeros_like(qt)), kt) * scale
                    s = jnp.where(mask, s, NEG)
                    m = jnp.max(s, axis=-1, keepdims=True)
                    p = jnp.exp(s - m)
                    den = jnp.sum(p, axis=-1, keepdims=True)
                    o = _dot(p.astype(BF), vt) / den
                    o_pair = jnp.where(lm, o, o_pair)
                    l_pair = jnp.where(lm, m + jnp.log(den), l_pair)
                o_t[t][mine, :] = o_pair
                l_t[t][mine, :] = l_pair


    out_spec = pl.BlockSpec(q_like(0).block_shape, lambda b, n, r: (b * nb + n, 0))
    sh = jax.ShapeDtypeStruct((T, 128), F32)
    outs = _call(
        body, name=f"attn_fwd_d{dil}", grid=(B, nb, last),
        in_specs=[q_like(t) for t in range(3)] + [k_like(3 + t) for t in range(3)] + [k_like(6 + t) for t in range(3)],
        out_specs=[out_spec] * 6, out_shape=[sh] * 6, sem=("parallel", "arbitrary", "arbitrary"))(*([qkv] * 9))
    return list(outs[0:3]), list(outs[3:6])


def _attn_combine(branches):
    T = branches[0][0][0].shape[0]
    tm = _tile(T, 512)

    def body(*refs):
        y_ref, l_ref = refs[-2:]
        for t in range(3):
            o = [refs[6 * i + t][...] for i in range(3)]
            a, b, c = [refs[6 * i + 3 + t][...] for i in range(3)]
            m = jnp.maximum(jnp.maximum(a, b), c)
            ea, eb, ec = jnp.exp(a - m), jnp.exp(b - m), jnp.exp(c - m)
            z = ea + eb + ec
            y_ref[:, 128 * t:128 * (t + 1)] = (ea * o[0] + eb * o[1] + ec * o[2]) / z
            l_ref[:, 128 * t:128 * (t + 1)] = m + jnp.log(z)

    tile = pl.BlockSpec((tm, 128), lambda i: (i, 0))
    row = pl.BlockSpec((tm, ATT_W), lambda i: (i, 0))
    sh = jax.ShapeDtypeStruct((T, ATT_W), F32)
    flat = [a for o_t, l_t in branches for a in (*o_t, *l_t)]
    return _call(body, name="attn_combine", grid=(T // tm,), in_specs=[tile] * 18, out_specs=[row, row],
                 out_shape=[sh, sh], sem=("parallel",))(*flat)


def _attn_bwd(qkv, do, out, lse, dil):
    T = qkv.shape[0]
    B, nb, last, per_step, q_like, k_like = _attn_specs(T, dil)
    scale = HEAD ** -0.5

    def body(*refs):
        q_t, k_t, v_t = refs[0:3], refs[3:6], refs[6:9]
        do_t, out_t, lse_t = refs[9:12], refs[12:15], refs[15:18]
        dq_t, dk_t, dv_t = refs[18:21], refs[21:24], refs[24:27]
        @pl.when((pl.program_id(1) == 0) & (pl.program_id(2) == 0))
        def _():
            for t in range(3):
                dk_t[t][...] = jnp.zeros_like(dk_t[t])
                dv_t[t][...] = jnp.zeros_like(dv_t[t])

        for u in range(per_step):
            r, n, mine = _attn_step(u, dil, per_step)
            mask = _band_mask(n)
            cur, prv = _sub_rows(r, n, dil), _sub_rows(r, jnp.maximum(n - 1, 0), dil)
            for t in range(3):
                qt = q_t[t][mine, :].astype(BF)
                kt = jnp.concatenate([k_t[t][prv, :], k_t[t][cur, :]], axis=0).astype(BF)
                vt = jnp.concatenate([v_t[t][prv, :], v_t[t][cur, :]], axis=0).astype(BF)
                do_ = do_t[t][mine, :]
                dlt = do_ * out_t[t][mine, :]
                ls = lse_t[t][mine, :]
                dq_pair = jnp.zeros((CHUNK, 128), F32)
                dk_acc = jnp.zeros((2 * CHUNK, 128), F32)
                dv_acc = jnp.zeros((2 * CHUNK, 128), F32)
                for e in range(2):
                    lm = _lane_mask(e)
                    qm = jnp.where(lm, qt, jnp.zeros_like(qt))
                    s = _dot_nt(qm, kt) * scale
                    p = jnp.exp(jnp.where(mask, s - ls[:, HEAD * e:HEAD * e + 1], NEG))
                    dom = jnp.where(lm, do_, 0.0).astype(BF)
                    dv_acc += _dot_tn(p.astype(BF), dom)
                    dp = _dot_nt(dom, vt)
                    delta = jnp.sum(jnp.where(lm, dlt, 0.0), axis=-1, keepdims=True)
                    ds = (p * (dp - delta) * scale).astype(BF)
                    dq_pair += jnp.where(lm, _dot(ds, kt), 0.0)
                    dk_acc += _dot_tn(ds, qm)
                dq_t[t][mine, :] = dq_pair
                dk_t[t][cur, :] = dk_t[t][cur, :] + dk_acc[CHUNK:]
                dk_t[t][prv, :] = dk_t[t][prv, :] + dk_acc[:CHUNK]
                dv_t[t][cur, :] = dv_t[t][cur, :] + dv_acc[CHUNK:]
                dv_t[t][prv, :] = dv_t[t][prv, :] + dv_acc[:CHUNK]

    q_out = pl.BlockSpec(q_like(0).block_shape, lambda b, n, r: (b * nb + n, 0))
    k_out = pl.BlockSpec((SEQ, 128), lambda b, n, r: (b, 0))
    sh = jax.ShapeDtypeStruct((T, 128), F32)
    tiles = lambda: [q_like(t) for t in range(3)]
    return list(_call(
        body, name=f"attn_bwd_d{dil}", grid=(B, nb, last),
        in_specs=tiles() + [k_like(3 + t) for t in range(3)] + [k_like(6 + t) for t in range(3)]
        + tiles() + tiles() + tiles(),
        out_specs=[q_out] * 3 + [k_out] * 6, out_shape=[sh] * 9,
        sem=("parallel", "arbitrary", "arbitrary"), vmem=ATTN_BWD_VMEM)(*([qkv] * 9 + [do] * 3 + [out] * 3 + [lse] * 3)))


def _sum_branches(parts):
    T = parts[0][0].shape[0]
    tm = _tile(T, 512)

    def body(*refs):
        o_ref = refs[-1]
        for c in range(9):
            acc = refs[c][...] + refs[9 + c][...] + refs[18 + c][...]
            o_ref[:, 128 * c:128 * (c + 1)] = acc.astype(BF)

    tile = pl.BlockSpec((tm, 128), lambda i: (i, 0))
    flat = [a for br in parts for a in br]
    return _call(body, name="attn_sum_branches", grid=(T // tm,), in_specs=[tile] * 27,
                 out_specs=pl.BlockSpec((tm, W_QKV), lambda i: (i, 0)),
                 out_shape=jax.ShapeDtypeStruct((T, W_QKV), BF), sem=("parallel",))(*flat)


def _silu(x):
    return x * _sigmoid(x)


def _dsilu(x):
    s = _sigmoid(x)
    return s * (1.0 + x * (1.0 - s))


def _log1p(u):
    return jnp.where(u < 0.01, u * (1.0 - u * (0.5 - u * (1.0 / 3.0))), jnp.log(1.0 + u))


def _softplus(x):
    return jnp.maximum(x, 0.0) + _log1p(jnp.exp(-jnp.abs(x)))


def _cumsum_rows(x, reverse=False):
    n = x.shape[0]
    rows = lax.broadcasted_iota(jnp.int32, x.shape, 0)
    k = 1
    while k < n:
        if reverse:
            x = x + jnp.where(rows < n - k, pltpu.roll(x, n - k, 0), 0.0)
        else:
            x = x + jnp.where(rows >= k, pltpu.roll(x, k, 0), 0.0)
        k *= 2
    return x


def _tri():
    r = lax.broadcasted_iota(jnp.int32, (CHUNK, CHUNK), 0)
    c = lax.broadcasted_iota(jnp.int32, (CHUNK, CHUNK), 1)
    return r >= c


def _row_mask(e):
    return (lax.broadcasted_iota(jnp.int32, (128, 1), 0) // HEAD) == e


def _first_lane(e):
    return lax.broadcasted_iota(jnp.int32, (1, 128), 1) == HEAD * e


def _ssd_pre(x_ref, halo_ref, first, cw_ref, cb_ref, dtb_ref, al_ref, ext):
    row = x_ref[...]
    z = row[:, SSD_CONV_DIM:SSD_CONV_DIM + SSD_W]
    u = row[:, SSD_CONV_DIM + SSD_W:] + dtb_ref[...]
    ext[0:8, :] = jnp.where(first, 0.0, halo_ref[:, 0:SSD_CONV_DIM])
    ext[8:8 + CHUNK, :] = row[:, 0:SSD_CONV_DIM]
    xc = cb_ref[...]
    for j in range(4):
        xc = xc + cw_ref[j:j + 1, :] * ext[pl.ds(5 + j, CHUNK), :]
    xa = _silu(xc)
    dt = _softplus(u)
    a = dt * (-jnp.exp(al_ref[...]))
    A = _cumsum_rows(a)
    return dict(z=z, u=u, xc=xc, xs=xa[:, 0:SSD_W], Bm=xa[:, SSD_W:SSD_W + 256], Cm=xa[:, SSD_W + 256:],
                dt=dt, a=a, A=A, AT=A.T, eA=jnp.exp(A), wdec=jnp.exp(A[CHUNK - 1:CHUNK, :] - A),
                dtot=jnp.exp(A[CHUNK - 1:CHUNK, :]))


def _ssd_y(p, hp_ref, dskip):
    tri = _tri()
    X = p["xs"] * p["dt"]
    Bb = [p["Bm"][:, 128 * g:128 * (g + 1)].astype(BF) for g in range(2)]
    Cb = [p["Cm"][:, 128 * g:128 * (g + 1)].astype(BF) for g in range(2)]
    CB = [_dot_nt(Cb[g], Bb[g]) for g in range(2)]
    tiles = []
    for t in range(3):
        sl = slice(128 * t, 128 * (t + 1))
        hpb = hp_ref[sl, :].astype(BF)
        acc = jnp.zeros((CHUNK, 128), F32)
        for e in range(2):
            h = 2 * t + e
            g, col = h // 3, HEAD * h
            lm = _lane_mask(e)
            L = jnp.exp(jnp.where(tri, p["A"][:, col:col + 1] - p["AT"][col:col + 1, :], NEG))
            yd = _dot((CB[g] * L).astype(BF), jnp.where(lm, X[:, sl], 0.0).astype(BF))
            yo = _dot_nt(Cb[g], hpb) * p["eA"][:, sl]
            acc = acc + jnp.where(lm, yd + yo, 0.0)
        tiles.append(acc)
    return jnp.concatenate(tiles, axis=1) + dskip * p["xs"], X, Bb, Cb, CB


def _group_stats(v):
    g0 = lax.broadcasted_iota(jnp.int32, (1, SSD_W), 1) < SSD_W // 2
    m0 = jnp.sum(jnp.where(g0, v, 0.0), axis=-1, keepdims=True) * (2.0 / SSD_W)
    m1 = jnp.sum(jnp.where(g0, 0.0, v), axis=-1, keepdims=True) * (2.0 / SSD_W)
    return jnp.where(g0, m0, m1)


def _ssd_specs(T, rev):
    B = T // SEQ
    chunk = (lambda c: N_CHUNK - 1 - c) if rev else (lambda c: c)
    row = pl.BlockSpec((B, CHUNK, W_SSD), lambda c: (0, chunk(c), 0))
    halo = pl.BlockSpec((B, 8, W_SSD), lambda c: (0, jnp.maximum(chunk(c) * (CHUNK // 8) - 1, 0), 0))
    hp = pl.BlockSpec((B, None, SSD_W, SSD_STATE), lambda c: (0, chunk(c), 0, 0))
    y = pl.BlockSpec((B, CHUNK, SSD_W), lambda c: (0, chunk(c), 0))
    const = lambda r, w: pl.BlockSpec((r, w), lambda c: (0, 0))
    params = [const(4, SSD_CONV_DIM), const(1, SSD_CONV_DIM)] + [const(1, SSD_W)] * 4
    return B, row, halo, hp, y, const, params


def _ssd_fwd(sin, conv_w, conv_b, dtb, alog, dskip, norm_g):
    T = sin.shape[0]
    B, row, halo, hp, y, const, params = _ssd_specs(T, False)

    def body(xs_ref, halos_ref, cw_ref, cb_ref, dtb_ref, al_ref, dk_ref, ng_ref, ys_ref, hps_ref, exts, hsts):
        @pl.when(pl.program_id(0) == 0)
        def _():
            hsts[...] = jnp.zeros_like(hsts)

        for b in range(B):
            one(xs_ref.at[b], halos_ref.at[b], cw_ref, cb_ref, dtb_ref, al_ref, dk_ref, ng_ref, ys_ref.at[b],
                hps_ref.at[b], exts.at[b], hsts.at[b])

    def one(x_ref, halo_ref, cw_ref, cb_ref, dtb_ref, al_ref, dk_ref, ng_ref, y_ref, hp_ref, ext, hst):
        c = pl.program_id(0)
        p = _ssd_pre(x_ref, halo_ref, c == 0, cw_ref, cb_ref, dtb_ref, al_ref, ext)
        yv, X, Bb, Cb, CB = _ssd_y(p, hst, dk_ref[...])
        hp_ref[...] = hst[...]
        for t in range(3):
            sl = slice(128 * t, 128 * (t + 1))
            old = hst[sl, :]
            new = old
            for e in range(2):
                h = 2 * t + e
                g, col = h // 3, HEAD * h
                st = _dot_tn(jnp.where(_lane_mask(e), X[:, sl] * p["wdec"][:, sl], 0.0).astype(BF), Bb[g])
                new = jnp.where(_row_mask(e), old * p["dtot"][:, col:col + 1] + st, new)
            hst[sl, :] = new
        y2 = yv * _silu(p["z"])
        r = lax.rsqrt(_group_stats(y2 * y2) + RMS_EPS)
        y_ref[...] = y2 * r * ng_ref[...]

    sin3 = sin.reshape(B, SEQ, W_SSD)
    yo, hprev = _call(
        body, name="ssd_fwd", grid=(N_CHUNK,), in_specs=[row, halo] + params, out_specs=[y, hp],
        out_shape=[jax.ShapeDtypeStruct((B, SEQ, SSD_W), F32),
                   jax.ShapeDtypeStruct((B, N_CHUNK, SSD_W, SSD_STATE), F32)],
        scratch=[pltpu.VMEM((B, 8 + CHUNK, SSD_CONV_DIM), F32), pltpu.VMEM((B, SSD_W, SSD_STATE), F32)],
        sem=("arbitrary",))(sin3, sin3, conv_w, conv_b, dtb, alog, dskip, norm_g)
    return yo.reshape(T, SSD_W), hprev


def _ssd_bwd(sin, hprev, dy3, conv_w, conv_b, dtb, alog, dskip, norm_g):
    T = sin.shape[0]
    B, row, halo, hp, y, const, params = _ssd_specs(T, True)

    def body(xs_ref, halos_ref, hps_ref, dys_ref, cw_ref, cb_ref, dtb_ref, al_ref, dk_ref, ng_ref,
             dxs_ref, dcw_ref, dcb_ref, dvec_ref, exts, ext2s, dhs):
        @pl.when(pl.program_id(0) == 0)
        def _():
            dcw_ref[...] = jnp.zeros_like(dcw_ref)
            dcb_ref[...] = jnp.zeros_like(dcb_ref)
            dvec_ref[...] = jnp.zeros_like(dvec_ref)
            dhs[...] = jnp.zeros_like(dhs)
            ext2s[:, CHUNK:CHUNK + 8, :] = jnp.zeros((B, 8, SSD_CONV_DIM), F32)

        for b in range(B):
            one(xs_ref.at[b], halos_ref.at[b], hps_ref.at[b], dys_ref.at[b], cw_ref, cb_ref, dtb_ref, al_ref, dk_ref,
                ng_ref, dxs_ref.at[b], dcw_ref, dcb_ref, dvec_ref, exts.at[b], ext2s.at[b], dhs.at[b])

    def one(x_ref, halo_ref, hp_ref, dy_ref, cw_ref, cb_ref, dtb_ref, al_ref, dk_ref, ng_ref,
            dx_ref, dcw_ref, dcb_ref, dvec_ref, ext, ext2, dh):
        c = pl.program_id(0)
        p = _ssd_pre(x_ref, halo_ref, c == N_CHUNK - 1, cw_ref, cb_ref, dtb_ref, al_ref, ext)
        dskip_ = dk_ref[...]
        yv, X, Bb, Cb, CB = _ssd_y(p, hp_ref, dskip_)
        xs, z, A, AT = p["xs"], p["z"], p["A"], p["AT"]

        sz = _silu(z)
        y2 = yv * sz
        r = lax.rsqrt(_group_stats(y2 * y2) + RMS_EPS)
        dy3_ = dy_ref[...]
        uu = dy3_ * ng_ref[...]
        dy2 = r * (uu - y2 * (r * r * _group_stats(uu * y2)))
        dy = dy2 * sz
        dz = dy2 * yv * _dsilu(z)

        tri = _tri()
        rows = lax.broadcasted_iota(jnp.int32, (CHUNK, 1), 0)
        dG = [jnp.zeros((CHUNK, CHUNK), F32) for _ in range(2)]
        dB = [jnp.zeros((CHUNK, SSD_STATE), F32) for _ in range(2)]
        dC = [jnp.zeros((CHUNK, SSD_STATE), F32) for _ in range(2)]
        dX_t, dA_t, ddtx_t = [], [], []
        for t in range(3):
            sl = slice(128 * t, 128 * (t + 1))
            hp_t = hp_ref[sl, :]
            hpb = hp_t.astype(BF)
            dhc = dh[sl, :]
            dh_new = jnp.zeros((128, SSD_STATE), F32)
            dX = jnp.zeros((CHUNK, 128), F32)
            dA = jnp.zeros((CHUNK, 128), F32)
            ddtx = jnp.zeros((CHUNK, 128), F32)
            for e in range(2):
                h = 2 * t + e
                g, col = h // 3, HEAD * h
                lm, rm, fl = _lane_mask(e), _row_mask(e), _first_lane(e)
                L = jnp.exp(jnp.where(tri, A[:, col:col + 1] - AT[col:col + 1, :], NEG))
                Mf = CB[g] * L
                Xm = jnp.where(lm, X[:, sl], 0.0)
                Xmb = Xm.astype(BF)
                dyh = jnp.where(lm, dy[:, sl], 0.0)
                dyb = dyh.astype(BF)
                dXh = _dot_tn(Mf.astype(BF), dyb)
                dM = jnp.where(tri, _dot_nt(dyb, Xmb), 0.0)
                Wm = dM * Mf
                dAc = jnp.sum(Wm, axis=-1, keepdims=True) - jnp.sum(Wm.T, axis=-1, keepdims=True)
                dG[g] = dG[g] + dM * L
                eAt = p["eA"][:, sl]
                yo = _dot_nt(Cb[g], hpb)
                dyo = (dyh * eAt).astype(BF)
                dC[g] = dC[g] + _dot(dyo, hpb)
                dh_new = dh_new + _dot_tn(dyo, Cb[g])
                dAc = dAc + jnp.sum(dyh * yo * eAt, axis=-1, keepdims=True)
                dHn = jnp.where(rm, dhc, 0.0)
                dHnb = dHn.astype(BF)
                dec = p["dtot"][:, col:col + 1]
                dh_new = dh_new + dec * dHn
                Z = _dot_nt(Bb[g], dHnb)
                wt = p["wdec"][:, sl]
                xi = jnp.sum(Xm * Z, axis=-1, keepdims=True) * p["wdec"][:, col:col + 1]
                dXh = dXh + wt * Z
                dB[g] = dB[g] + _dot(jnp.where(lm, X[:, sl] * wt, 0.0).astype(BF), dHnb)
                dAtot = jnp.sum(xi, axis=0, keepdims=True) + dec * jnp.sum(
                    jnp.sum(dHn * hp_t, axis=-1, keepdims=True), axis=0, keepdims=True)
                dAc = dAc - xi + jnp.where(rows == CHUNK - 1, dAtot, 0.0)
                dA = dA + jnp.where(fl, dAc, 0.0)
                dX = dX + dXh
                ddtx = ddtx + jnp.where(fl, jnp.sum(dXh * xs[:, sl], axis=-1, keepdims=True), 0.0)
            dh[sl, :] = dh_new
            dX_t.append(dX)
            dA_t.append(dA)
            ddtx_t.append(ddtx)
        for g in range(2):
            dGb = dG[g].astype(BF)
            dC[g] = dC[g] + _dot(dGb, Bb[g])
            dB[g] = dB[g] + _dot_tn(dGb, Cb[g])
        dXf = jnp.concatenate(dX_t, axis=1)
        da = _cumsum_rows(jnp.concatenate(dA_t, axis=1), reverse=True)
        ddt = da * (-jnp.exp(al_ref[...])) + jnp.concatenate(ddtx_t, axis=1)
        du = ddt * _sigmoid(p["u"])
        dxs = dXf * p["dt"] + dskip_ * dy
        dxc = jnp.concatenate([dxs, dB[0], dB[1], dC[0], dC[1]], axis=1) * _dsilu(p["xc"])
        ext2[0:CHUNK, :] = dxc
        dxbc = jnp.zeros((CHUNK, SSD_CONV_DIM), F32)
        for j in range(4):
            dxbc = dxbc + cw_ref[j:j + 1, :] * ext2[pl.ds(3 - j, CHUNK), :]
            dcw_ref[j:j + 1, :] += jnp.sum(dxc * ext[pl.ds(5 + j, CHUNK), :], axis=0, keepdims=True)
        ext2[CHUNK:CHUNK + 8, :] = dxc[0:8, :]
        dcb_ref[...] += jnp.sum(dxc, axis=0, keepdims=True)
        dvec_ref[0:1, :] += jnp.sum(du, axis=0, keepdims=True)
        dvec_ref[1:2, :] += jnp.sum(da * p["a"], axis=0, keepdims=True)
        dvec_ref[2:3, :] += jnp.sum(dy * xs, axis=0, keepdims=True)
        dvec_ref[3:4, :] += jnp.sum(dy3_ * y2 * r, axis=0, keepdims=True)
        dx_ref[...] = jnp.concatenate([dxbc, dz, du], axis=1).astype(BF)

    sin3 = sin.reshape(B, SEQ, W_SSD)
    out = _call(body, name="ssd_bwd", grid=(N_CHUNK,), in_specs=[row, halo, hp, y] + params,
                out_specs=[row, const(4, SSD_CONV_DIM), const(1, SSD_CONV_DIM), const(8, SSD_W)],
                out_shape=[jax.ShapeDtypeStruct((B, SEQ, W_SSD), BF), jax.ShapeDtypeStruct((4, SSD_CONV_DIM), F32),
                           jax.ShapeDtypeStruct((1, SSD_CONV_DIM), F32), jax.ShapeDtypeStruct((8, SSD_W), F32)],
                scratch=[pltpu.VMEM((B, 8 + CHUNK, SSD_CONV_DIM), F32), pltpu.VMEM((B, 8 + CHUNK, SSD_CONV_DIM), F32),
                         pltpu.VMEM((B, SSD_W, SSD_STATE), F32)],
                sem=("arbitrary",))(sin3, sin3, hprev, dy3.reshape(B, SEQ, SSD_W), conv_w, conv_b, dtb, alog, dskip,
                                    norm_g)
    return (out[0].reshape(T, W_SSD),) + tuple(out[1:])


def _sgu_weights(w_ref):
    tri = _tri()
    return [jnp.where(tri, w_ref[gi], 0.0).astype(BF) for gi in range(4)]


def _sgu_core(x, g_ref, b_ref, wc, bias_ref):
    cdf = 0.5 * (1.0 + lax.erf(x * (2.0 ** -0.5)))
    ge = x * cdf
    dge = cdf + x * jnp.exp(-0.5 * x * x) * ((2.0 * math.pi) ** -0.5)
    u, v = ge[:, 0:SGU_W], ge[:, SGU_W:]
    vc = v - jnp.mean(v, axis=-1, keepdims=True)
    rstd = lax.rsqrt(jnp.mean(vc * vc, axis=-1, keepdims=True) + LN_EPS)
    vhat = vc * rstd
    vn = vhat * g_ref[...] + b_ref[...]
    vm = [jnp.where(_lane_mask(gi % 2), vn[:, 128 * (gi // 2):128 * (gi // 2 + 1)], 0.0).astype(BF) for gi in range(4)]
    mixed = jnp.concatenate([_dot(wc[2 * t], vm[2 * t]) + _dot(wc[2 * t + 1], vm[2 * t + 1]) for t in range(2)],
                            axis=1) + bias_ref[...]
    return dict(dge=dge, u=u, rstd=rstd, vhat=vhat, vm=vm, mixed=mixed)


SGU_ROWS = SGU_CHUNKS_PER_STEP * CHUNK


def _sgu_chunk_rows():
    return [pl.ds(j * CHUNK, CHUNK) for j in range(SGU_CHUNKS_PER_STEP)]


def _sgu_specs():
    vec = pl.BlockSpec((1, SGU_W), lambda i: (0, 0))
    return [pl.BlockSpec((SGU_ROWS, W_UV), lambda i: (i, 0)), vec, vec,
            pl.BlockSpec((4, CHUNK, CHUNK), lambda i: (0, 0, 0)), pl.BlockSpec((CHUNK, SGU_W), lambda i: (0, 0))]


def _sgu_fwd(uv, ln_g, ln_b, w, bias):
    T = uv.shape[0]

    def body(uv_ref, g_ref, b_ref, w_ref, bias_ref, y_ref):
        wc = _sgu_weights(w_ref)
        for rows in _sgu_chunk_rows():
            s = _sgu_core(uv_ref[rows, :], g_ref, b_ref, wc, bias_ref)
            y_ref[rows, :] = s["u"] * s["mixed"]

    return _call(body, name="sgu_fwd", grid=(T // SGU_ROWS,), in_specs=_sgu_specs(),
                 out_specs=pl.BlockSpec((SGU_ROWS, SGU_W), lambda i: (i, 0)),
                 out_shape=jax.ShapeDtypeStruct((T, SGU_W), F32), sem=("parallel",))(uv, ln_g, ln_b, w, bias)


def _sgu_bwd(uv, dy, ln_g, ln_b, w, bias):
    T = uv.shape[0]

    def body(uv_ref, dy_ref, g_ref, b_ref, w_ref, bias_ref, dx_ref, dw_ref, dbias_ref, dln_ref):
        @pl.when(pl.program_id(0) == 0)
        def _():
            dw_ref[...] = jnp.zeros_like(dw_ref)
            dbias_ref[...] = jnp.zeros_like(dbias_ref)
            dln_ref[...] = jnp.zeros_like(dln_ref)

        tri = _tri()
        wc = _sgu_weights(w_ref)
        for rows in _sgu_chunk_rows():
            s = _sgu_core(uv_ref[rows, :], g_ref, b_ref, wc, bias_ref)
            dy_ = dy_ref[rows, :]
            du = dy_ * s["mixed"]
            dmix = dy_ * s["u"]
            dbias_ref[...] += dmix
            dvn_t = []
            for t in range(2):
                acc = jnp.zeros((CHUNK, 128), F32)
                for e in range(2):
                    gi = 2 * t + e
                    dmg = jnp.where(_lane_mask(e), dmix[:, 128 * t:128 * (t + 1)], 0.0).astype(BF)
                    acc = acc + _dot_tn(wc[gi], dmg)
                    dw_ref[gi] += jnp.where(tri, _dot_nt(dmg, s["vm"][gi]), 0.0)
                dvn_t.append(acc)
            dvn = jnp.concatenate(dvn_t, axis=1)
            dln_ref[0:1, :] += jnp.sum(dvn * s["vhat"], axis=0, keepdims=True)
            dln_ref[1:2, :] += jnp.sum(dvn, axis=0, keepdims=True)
            dvh = dvn * g_ref[...]
            dv = s["rstd"] * (dvh - jnp.mean(dvh, axis=-1, keepdims=True)
                              - s["vhat"] * jnp.mean(dvh * s["vhat"], axis=-1, keepdims=True))
            dx_ref[rows, :] = (jnp.concatenate([du, dv], axis=1) * s["dge"]).astype(BF)

    ins = _sgu_specs()
    return _call(body, name="sgu_bwd", grid=(T // SGU_ROWS,),
                 in_specs=[ins[0], pl.BlockSpec((SGU_ROWS, SGU_W), lambda i: (i, 0))] + ins[1:],
                 out_specs=[pl.BlockSpec((SGU_ROWS, W_UV), lambda i: (i, 0)),
                            pl.BlockSpec((4, CHUNK, CHUNK), lambda i: (0, 0, 0)),
                            pl.BlockSpec((CHUNK, SGU_W), lambda i: (0, 0)), pl.BlockSpec((8, SGU_W), lambda i: (0, 0))],
                 out_shape=[jax.ShapeDtypeStruct((T, W_UV), BF), jax.ShapeDtypeStruct((4, CHUNK, CHUNK), F32),
                            jax.ShapeDtypeStruct((CHUNK, SGU_W), F32), jax.ShapeDtypeStruct((8, SGU_W), F32)],
                 sem=("arbitrary",))(uv, dy, ln_g, ln_b, w, bias)


def _adamw(w, g, m, v):
    R, C = w.shape
    tr = R

    def body(w_ref, g_ref, m_ref, v_ref, d_ref, nm_ref, nv_ref):
        g_ = g_ref[...]
        m2 = ADAM_B1 * m_ref[...] + (1.0 - ADAM_B1) * g_
        v2 = ADAM_B2 * v_ref[...] + (1.0 - ADAM_B2) * (g_ * g_)
        m_hat = m2 / (1.0 - ADAM_B1 ** ADAM_STEP)
        v_hat = v2 / (1.0 - ADAM_B2 ** ADAM_STEP)
        d_ref[...] = -ADAM_LR * (m_hat / (jnp.sqrt(v_hat) + ADAM_EPS) + ADAM_WD * w_ref[...])
        nm_ref[...] = m2
        nv_ref[...] = v2

    blk = pl.BlockSpec((tr, C), lambda i: (i, 0))
    sh = jax.ShapeDtypeStruct((R, C), F32)
    return _call(body, name="adamw", grid=(R // tr,), in_specs=[blk] * 4, out_specs=[blk] * 3,
                 out_shape=[sh] * 3, sem=("parallel",))(w, g, m, v)


def _adamw_pair(w, g0, g1, m, v, dep):
    L, R, C = w.shape
    tr = max(t for t in range(8, R + 1, 8) if R % t == 0 and t * C * 4 <= 3 * 2 ** 19)

    def body(w_ref, g0_ref, g1_ref, m_ref, v_ref, dep_ref, d_ref, nm_ref, nv_ref, og_ref):
        g_ = jnp.where(pl.program_id(0) == 0, g0_ref[...], g1_ref[...])
        m2 = ADAM_B1 * m_ref[...] + (1.0 - ADAM_B1) * g_
        v2 = ADAM_B2 * v_ref[...] + (1.0 - ADAM_B2) * (g_ * g_)
        m_hat = m2 / (1.0 - ADAM_B1 ** ADAM_STEP)
        v_hat = v2 / (1.0 - ADAM_B2 ** ADAM_STEP)
        d_ref[...] = -ADAM_LR * (m_hat / (jnp.sqrt(v_hat) + ADAM_EPS) + ADAM_WD * w_ref[...])
        nm_ref[...] = m2
        nv_ref[...] = v2
        og_ref[...] = g_

    lay = pl.BlockSpec((None, tr, C), lambda l, i: (l, i, 0))
    one = lambda k: pl.BlockSpec((tr, C), lambda l, i: (jnp.where(l == k, i, 0), 0))
    return _call(body, name="adamw_pair", grid=(L, R // tr),
                 in_specs=[lay, one(0), one(1), lay, lay, pl.BlockSpec((8, 128), lambda l, i: (0, 0))],
                 out_specs=[lay] * 4,
                 out_shape=[jax.ShapeDtypeStruct((L, R, C), F32)] * 4,
                 sem=("parallel", "parallel"))(w, g0, g1, m, v, dep)


def _row_steps(rows):
    return 2 if rows % 32 == 0 else 1


def _pair_add(gbufs, rsibs, c):
    n = len(gbufs)
    steps = min(_row_steps(g.shape[2]) for g in gbufs)

    def body(c_ref, *refs):
        for a_ref, b_ref, o_ref in zip(refs[:n], refs[n:2 * n], refs[2 * n:]):
            o_ref[...] = (a_ref[...] + b_ref[...]).astype(BF)

    def specs(g):
        tr, C = g.shape[2] // steps, g.shape[3]
        return (pl.BlockSpec((None, None, tr, C), lambda j, i, c_ref: (j, c_ref[0], i, 0)),
                pl.BlockSpec((None, tr, C), lambda j, i, c_ref: (j, i, 0)))

    return list(pl.pallas_call(
        body, name="rs_pair_add",
        grid_spec=pltpu.PrefetchScalarGridSpec(
            num_scalar_prefetch=1, grid=(4, steps),
            in_specs=[specs(g)[0] for g in gbufs] + [specs(g)[1] for g in gbufs],
            out_specs=[specs(g)[1] for g in gbufs]),
        out_shape=[jax.ShapeDtypeStruct((4,) + g.shape[2:], BF) for g in gbufs],
        compiler_params=pltpu.CompilerParams(dimension_semantics=("parallel", "parallel")),
    )(jnp.reshape(c, (1,)).astype(jnp.int32), *gbufs, *rsibs))


def _chip_sum(pairs, recvs, me, c):
    n = len(pairs)
    steps = min(_row_steps(p.shape[1]) for p in pairs)

    def body(s_ref, *refs):
        for own_ref, p_ref, o_ref in zip(refs[:n], refs[n:2 * n], refs[2 * n:]):
            p = [jnp.where(s_ref[0] == j, own_ref[...], p_ref[j]).astype(F32) for j in range(4)]
            o_ref[...] = ((p[0] + p[1]) + p[2]) + p[3]

    def specs(p):
        tr, C = p.shape[1] // steps, p.shape[2]
        return (pl.BlockSpec((None, tr, C), lambda i, s: (s[0], i, 0)), pl.BlockSpec((4, tr, C), lambda i, s: (0, i, 0)),
                pl.BlockSpec((None, tr, C), lambda i, s: (s[1], i, 0)))

    return list(pl.pallas_call(
        body, name="rs_chip_sum",
        grid_spec=pltpu.PrefetchScalarGridSpec(
            num_scalar_prefetch=1, grid=(steps,),
            in_specs=[specs(p)[0] for p in pairs] + [specs(p)[1] for p in pairs],
            out_specs=[specs(p)[2] for p in pairs]),
        out_shape=[jax.ShapeDtypeStruct((2,) + p.shape[1:], F32) for p in pairs],
        compiler_params=pltpu.CompilerParams(dimension_semantics=("parallel",)),
    )(jnp.stack([me, c]).astype(jnp.int32), *pairs, *recvs))


MESH = pl.DeviceIdType.MESH
ANY = pl.BlockSpec(memory_space=pl.ANY)


def _place():
    x, y, c = lax.axis_index("x"), lax.axis_index("y"), lax.axis_index("c")
    return x, y, c, [(1 - x, y), (x, 1 - y), (1 - x, 1 - y)]


HBM = pl.BlockSpec(memory_space=pltpu.HBM)
SEM = pl.BlockSpec(memory_space=pltpu.SEMAPHORE)
EFFECT = pltpu.SideEffectType.DATAFLOW_SIDE_EFFECTING


class _Split:
    def __init__(self, tag, arrays, copies, n_copies, after=()):
        self.tag, self.copies, k = tag, copies, len(arrays)

        def body(*refs):
            sems = k + len(after)
            for cp in copies(refs[:k], refs[sems], refs[sems + 1]):
                cp.start()
            refs[-1][...] = jnp.zeros_like(refs[-1])

        out = pl.pallas_call(
            body, name=tag + "_start",
            out_shape=(pltpu.SemaphoreType.DMA((n_copies,)), pltpu.SemaphoreType.DMA((n_copies,)),
                       *[pltpu.HBM(a.shape, a.dtype) for a in arrays], jax.ShapeDtypeStruct((8, 128), F32)),
            in_specs=[HBM] * k + [ANY] * len(after),
            out_specs=(SEM, SEM, *[HBM] * k, pl.BlockSpec(memory_space=pltpu.VMEM)),
            input_output_aliases={i: 2 + i for i in range(k)},
            compiler_params=pltpu.CompilerParams(has_side_effects=EFFECT),
        )(*[pltpu.with_memory_space_constraint(a, pltpu.HBM) for a in arrays], *after)
        self.send, self.recv, self.arrays, self.token_array = out[0], out[1], list(out[2:2 + k]), out[-1]
        self.token = self.token_array[0, 0]

    def wait(self, after):
        k, copies = len(self.arrays), self.copies
        after = list(after) if isinstance(after, (list, tuple)) else [after]

        def body(*refs):
            for cp in copies(refs[:k], refs[k], refs[k + 1]):
                cp.wait_send()
                cp.wait_recv()

        return list(pl.pallas_call(
            body, name=self.tag + "_wait", out_shape=tuple(pltpu.HBM(a.shape, a.dtype) for a in self.arrays),
            in_specs=[HBM] * k + [SEM, SEM] + [ANY] * len(after), out_specs=tuple([HBM] * k),
            input_output_aliases={i: i for i in range(k)},
            compiler_params=pltpu.CompilerParams(has_side_effects=EFFECT),
        )(*self.arrays, self.send, self.recv, *after))


def _landing_zones(arrs):
    me = 2 * lax.axis_index("x") + lax.axis_index("y")
    return [lax.dynamic_update_index_in_dim(lax.empty((4,) + a.shape, a.dtype), a, me, 0) for a in arrs]


def _gather_start(arrs, lands, tag, after=()):
    n = len(arrs)

    def copies(refs, send, recv):
        x, y, c, chips = _place()
        return [pltpu.make_async_remote_copy(
            src_ref=refs[k], dst_ref=refs[n + k].at[2 * x + y], send_sem=send.at[3 * k + r],
            recv_sem=recv.at[3 * k + r], device_id=(px, py, c), device_id_type=MESH)
            for k in range(n) for r, (px, py) in enumerate(chips)]

    return _Split("gather_" + tag, list(arrs) + lands, copies, 3 * n, after)


def _gather_halves_start(arrs, tag):
    n = len(arrs)
    lands = _landing_zones(arrs)

    def copies(refs, send, recv):
        x, y, c, chips = _place()
        return [pltpu.make_async_remote_copy(
            src_ref=refs[k].at[c], dst_ref=refs[n + k].at[2 * x + y, c], send_sem=send.at[3 * k + r],
            recv_sem=recv.at[3 * k + r], device_id=(px, py, c), device_id_type=MESH)
            for k in range(n) for r, (px, py) in enumerate(chips)]

    return _Split("gather_" + tag, list(arrs) + lands, copies, 3 * n)


def _gather_halves_finish(lands, tag):
    n = len(lands)

    def copies(refs, send, recv):
        x, y, c, chips = _place()
        return [pltpu.make_async_remote_copy(
            src_ref=refs[k].at[2 * px + py, c], dst_ref=refs[k].at[2 * px + py, c], send_sem=send.at[3 * k + r],
            recv_sem=recv.at[3 * k + r], device_id=(x, y, 1 - c), device_id_type=MESH)
            for k in range(n) for r, (px, py) in enumerate(chips)]

    return _Split("gather_pass_" + tag, list(lands), copies, 3 * n)


def _part_sibling(gbufs):
    n = len(gbufs)

    def copies(refs, send, recv, off):
        x, y, c, _ = _place()
        return [pltpu.make_async_remote_copy(
            src_ref=refs[k].at[j, 1 - c], dst_ref=refs[n + k].at[j], send_sem=send.at[off + 4 * k + j],
            recv_sem=recv.at[off + 4 * k + j], device_id=(x, y, 1 - c), device_id_type=MESH)
            for k in range(n) for j in range(4)]

    return list(gbufs) + [lax.empty((4,) + g.shape[2:], g.dtype) for g in gbufs], 4 * n, copies


def _part_chips(pbufs):
    n = len(pbufs)

    def copies(refs, send, recv, off):
        x, y, c, chips = _place()
        return [pltpu.make_async_remote_copy(
            src_ref=refs[k].at[2 * px + py], dst_ref=refs[n + k].at[2 * x + y], send_sem=send.at[off + 3 * k + r],
            recv_sem=recv.at[off + 3 * k + r], device_id=(px, py, c), device_id_type=MESH)
            for k in range(n) for r, (px, py) in enumerate(chips)]

    return list(pbufs) + [lax.empty(p.shape, p.dtype) for p in pbufs], 3 * n, copies


def _part_join(fulls):
    def copies(refs, send, recv, off):
        x, y, c, _ = _place()
        return [pltpu.make_async_remote_copy(
            src_ref=refs[k].at[c], dst_ref=refs[k].at[c], send_sem=send.at[off + k], recv_sem=recv.at[off + k],
            device_id=(x, y, 1 - c), device_id_type=MESH) for k in range(len(fulls))]

    return list(fulls), len(fulls), copies


def _start_parts(parts, tag):
    arrays, spans, total = [], [], 0
    for arrs, n_copies, fn in parts:
        spans.append((len(arrays), len(arrs), total, fn))
        arrays += arrs
        total += n_copies

    def copies(refs, send, recv):
        return [cp for a0, na, off, fn in spans for cp in fn(refs[a0:a0 + na], send, recv, off)]

    op = _Split(tag, arrays, copies, total)
    op.spans = [(a0, na) for a0, na, _, _ in spans]
    return op


def _all_reduce_small(v):
    R, C = v.shape

    def body(v_ref, o_ref, g_ref, send, recv, loc):
        x, y, c, chips = _place()
        me, sibling = (x, y, c), (x, y, 1 - c)

        def rows(px, py, pc):
            return g_ref.at[4 * px + 2 * py + pc]

        def copy(k, block, to, src=None):
            return pltpu.make_async_remote_copy(
                src_ref=rows(*block) if src is None else src, dst_ref=rows(*block),
                send_sem=send.at[k], recv_sem=recv.at[k], device_id=to, device_id_type=MESH)

        mine = pltpu.make_async_copy(v_ref, rows(*me), loc)
        mine.start()
        first = [copy(0, me, sibling, src=v_ref)]
        first += [copy(1 + j, me, (*chip, c), src=v_ref) for j, chip in enumerate(chips)]
        for cp in first:
            cp.start()
        passed = [copy(4 + j, (*chip, c), sibling) for j, chip in enumerate(chips)]
        for j, chip in enumerate(chips):
            copy(1 + j, (*chip, c), me).wait_recv()
            passed[j].start()
        copy(0, sibling, me).wait_recv()
        for j, chip in enumerate(chips):
            copy(4 + j, (*chip, 1 - c), me).wait_recv()
        for cp in first + passed:
            cp.wait_send()
        mine.wait()
        acc = g_ref[0]
        for d in range(1, 8):
            acc = acc + g_ref[d]
        o_ref[...] = acc

    vm = pl.BlockSpec(memory_space=pltpu.VMEM)
    return pl.pallas_call(
        body, name="all_reduce_small", in_specs=[vm], out_specs=[vm, vm],
        out_shape=[jax.ShapeDtypeStruct((R, C), F32), jax.ShapeDtypeStruct((8, R, C), F32)],
        scratch_shapes=[pltpu.SemaphoreType.DMA((7,)), pltpu.SemaphoreType.DMA((7,)), pltpu.SemaphoreType.DMA],
    )(v)[0]


WEIGHTS = ['ffn1_norm', 'ffn1_w_gate', 'ffn1_w_up', 'ffn1_w_down', 'mix_norm', 'w_in', 'conv_w', 'conv_b', 'dt_bias',
           'a_log', 'd_skip', 'ssd_norm', 'sgu_ln_g', 'sgu_ln_b', 'sgu_w', 'sgu_b', 'w_out', 'ffn2_norm',
           'ffn2_w_gate', 'ffn2_w_up', 'ffn2_w_down', 'final_norm']
SHARDED = ['ffn1_w_gate', 'ffn1_w_up', 'ffn1_w_down', 'w_in', 'conv_w', 'w_out', 'ffn2_w_gate', 'ffn2_w_up',
           'ffn2_w_down']
SMALL = [n for n in WEIGHTS if n not in SHARDED]
GROUPS = [("ffn1", ["ffn1_w_gate", "ffn1_w_up", "ffn1_w_down"]), ("mix", ["w_in", "conv_w", "w_out"]),
          ("ffn2", ["ffn2_w_gate", "ffn2_w_up", "ffn2_w_down"])]
TRANSPOSED = ("ffn1_w_gate", "ffn1_w_up", "ffn2_w_gate", "ffn2_w_up")
DEPTH = 2


def _pack_w_in(w):
    return jnp.concatenate([w[..., 0:1152], w[..., 1536:2432], w[..., 1152:1536],
                            jnp.repeat(w[..., 2432:2438], HEAD, axis=-1), w[..., 2438:2950]], axis=-1)


def _unpack_w_in(dq, ds, du):
    return jnp.concatenate([dq, ds[:, 896:1280], ds[:, 0:896], ds[:, 1280::HEAD], du], axis=-1)


def _ffn_fwd(x, g, wg, wu, wd):
    xo, hb, S1, S2, A = _ffn_fwd_k(x, g, wg, wu, wd)
    return xo, (x, hb, S1, S2, A)


def _ffn_bwd_weights(dxo, saved, wd):
    x, hb, S1, S2, A = saved
    dG, dU, dyb = _ffn_bwd_act(dxo, S1, S2, wd)
    return (dG, dU), _ffn_bwd_k2(hb, dyb, A, dG, dU)


def _ffn_bwd_input(dxo, saved, mids, g, wg, wu):
    return _ffn_bwd_dx(mids[0], mids[1], wg, wu, saved[0], g, dxo)


def _mix_fwd(x, P):
    hb, qkv, sin, uv = _mix_proj(x, P["mix_norm"], P["w_in"])
    y_att, lse = _attn_combine([_attn_fwd(qkv, d) for d in DILATIONS])
    y_ssd, hprev = _ssd_fwd(sin, *P["ssd"])
    y_sgu = _sgu_fwd(uv, *P["sgu"])
    xo, ycat = _mix_out([y_att, y_ssd, y_sgu], P["w_out"], x)
    return xo, (x, hb, qkv, sin, uv, y_att, lse, hprev, ycat)


def _mix_bwd_weights(dxo, saved, P):
    x, hb, qkv, sin, uv, y_att, lse, hprev, ycat = saved
    dy_att, dy_ssd, dy_sgu = _mix_bwd_dy(dxo, P["w_out"])
    dwout = _mm_tn(ycat, dxo)
    dqkv = _sum_branches([_attn_bwd(qkv, dy_att, y_att, lse, d) for d in DILATIONS])
    dsin, dcw, dcb, dvec = _ssd_bwd(sin, hprev, dy_ssd, *P["ssd"])
    duv, dsw, dsbias, dln = _sgu_bwd(uv, dy_sgu, *P["sgu"])
    dwin = _unpack_w_in(_mm_tn(hb, dqkv), _mm_tn(hb, dsin), _mm_tn(hb, duv))
    grads = dict(
        w_in=dwin, conv_w=dcw, conv_b=dcb[0], dt_bias=dvec[0, ::HEAD], a_log=dvec[1, ::HEAD],
        d_skip=jnp.sum(dvec[2].reshape(6, HEAD), axis=-1), ssd_norm=dvec[3], sgu_ln_g=dln[0], sgu_ln_b=dln[1],
        sgu_w=dsw, sgu_b=jnp.sum(dsbias.reshape(CHUNK, 4, HEAD), axis=-1).T, w_out=dwout)
    return (dqkv, dsin, duv), grads


def _mix_bwd_input(dxo, saved, mids, P):
    return _mix_bwd_dx(*mids, P["w_in"], saved[0], P["mix_norm"], dxo)


def _halved(g):
    rows = g.size // g.shape[-1]
    return g.reshape(4, 2, rows // 8, g.shape[-1])


def kernel(x, ffn1_norm, ffn1_w_gate, ffn1_w_up, ffn1_w_down, mix_norm, w_in, conv_w, conv_b, dt_bias, a_log, d_skip, ssd_norm, sgu_ln_g, sgu_ln_b, sgu_w, sgu_b, w_out, ffn2_norm, ffn2_w_gate, ffn2_w_up, ffn2_w_down, final_norm, loss_target, m_ffn1_norm, m_ffn1_w_gate, m_ffn1_w_up, m_ffn1_w_down, m_mix_norm, m_w_in, m_conv_w, m_conv_b, m_dt_bias, m_a_log, m_d_skip, m_ssd_norm, m_sgu_ln_g, m_sgu_ln_b, m_sgu_w, m_sgu_b, m_w_out, m_ffn2_norm, m_ffn2_w_gate, m_ffn2_w_up, m_ffn2_w_down, m_final_norm, v_ffn1_norm, v_ffn1_w_gate, v_ffn1_w_up, v_ffn1_w_down, v_mix_norm, v_w_in, v_conv_w, v_conv_b, v_dt_bias, v_a_log, v_d_skip, v_ssd_norm, v_sgu_ln_g, v_sgu_ln_b, v_sgu_w, v_sgu_b, v_w_out, v_ffn2_norm, v_ffn2_w_gate, v_ffn2_w_up, v_ffn2_w_down, v_final_norm):
    given = dict(x=x, ffn1_norm=ffn1_norm, ffn1_w_gate=ffn1_w_gate, ffn1_w_up=ffn1_w_up, ffn1_w_down=ffn1_w_down, mix_norm=mix_norm, w_in=w_in, conv_w=conv_w, conv_b=conv_b, dt_bias=dt_bias, a_log=a_log, d_skip=d_skip, ssd_norm=ssd_norm, sgu_ln_g=sgu_ln_g, sgu_ln_b=sgu_ln_b, sgu_w=sgu_w, sgu_b=sgu_b, w_out=w_out, ffn2_norm=ffn2_norm, ffn2_w_gate=ffn2_w_gate, ffn2_w_up=ffn2_w_up, ffn2_w_down=ffn2_w_down, final_norm=final_norm, loss_target=loss_target, m_ffn1_norm=m_ffn1_norm, m_ffn1_w_gate=m_ffn1_w_gate, m_ffn1_w_up=m_ffn1_w_up, m_ffn1_w_down=m_ffn1_w_down, m_mix_norm=m_mix_norm, m_w_in=m_w_in, m_conv_w=m_conv_w, m_conv_b=m_conv_b, m_dt_bias=m_dt_bias, m_a_log=m_a_log, m_d_skip=m_d_skip, m_ssd_norm=m_ssd_norm, m_sgu_ln_g=m_sgu_ln_g, m_sgu_ln_b=m_sgu_ln_b, m_sgu_w=m_sgu_w, m_sgu_b=m_sgu_b, m_w_out=m_w_out, m_ffn2_norm=m_ffn2_norm, m_ffn2_w_gate=m_ffn2_w_gate, m_ffn2_w_up=m_ffn2_w_up, m_ffn2_w_down=m_ffn2_w_down, m_final_norm=m_final_norm, v_ffn1_norm=v_ffn1_norm, v_ffn1_w_gate=v_ffn1_w_gate, v_ffn1_w_up=v_ffn1_w_up, v_ffn1_w_down=v_ffn1_w_down, v_mix_norm=v_mix_norm, v_w_in=v_w_in, v_conv_w=v_conv_w, v_conv_b=v_conv_b, v_dt_bias=v_dt_bias, v_a_log=v_a_log, v_d_skip=v_d_skip, v_ssd_norm=v_ssd_norm, v_sgu_ln_g=v_sgu_ln_g, v_sgu_ln_b=v_sgu_ln_b, v_sgu_w=v_sgu_w, v_sgu_b=v_sgu_b, v_w_out=v_w_out, v_ffn2_norm=v_ffn2_norm, v_ffn2_w_gate=v_ffn2_w_gate, v_ffn2_w_up=v_ffn2_w_up, v_ffn2_w_down=v_ffn2_w_down, v_final_norm=v_final_norm)
    T = given["x"].shape[0] * given["x"].shape[1]
    D = given["x"].shape[2]
    x0 = given["x"].reshape(T, D)
    tgt = given["loss_target"].reshape(T, D)
    c = lax.axis_index("c")

    bf = {n: given[n].astype(BF) for n in SHARDED if n not in ("w_in", "conv_w")}
    bf["w_in"] = _pack_w_in(given["w_in"]).astype(BF)
    bf["conv_w"] = given["conv_w"]
    first_key = (0, GROUPS[0][0])
    first = [bf[n][0].reshape((2, bf[n].shape[1] // 2) + bf[n].shape[2:]) for n in GROUPS[0][1]]
    gathers = {first_key: _gather_halves_start(first, "l0_" + GROUPS[0][0])}
    later = {(i, gname): [bf[n][i] for n in names] for i in range(DEPTH) for gname, names in GROUPS
             if (i, gname) != first_key}
    zones = {key: _landing_zones(arrs) for key, arrs in later.items()}

    def gathered(i, gname, after):
        if (i, gname) != first_key:
            return gathers[(i, gname)].wait(after)[3:]
        got = gathers[first_key].wait([after] + [z for zs in zones.values() for z in zs])[3:]
        got = _gather_halves_finish(got, "l0_" + gname).wait(after)
        prev = got[0]
        for key, arrs in later.items():
            gathers[key] = _gather_start(arrs, zones[key], f"l{key[0]}_{key[1]}", after=[prev])
            prev = gathers[key].token_array
        return [z.reshape((4, 2 * z.shape[2]) + z.shape[3:]) for z in got]

    def mix_params(i, got):
        win = got[0].reshape(D, W_QKV + W_SSD + W_UV)
        rep = lambda v: jnp.repeat(v, HEAD)[None]
        ssd = (got[1].transpose(1, 0, 2).reshape(4, SSD_CONV_DIM), given["conv_b"][i][None],
               rep(given["dt_bias"][i]), rep(given["a_log"][i]), rep(given["d_skip"][i]), given["ssd_norm"][i][None])
        sgu = (given["sgu_ln_g"][i][None], given["sgu_ln_b"][i][None], given["sgu_w"][i],
               jnp.repeat(given["sgu_b"][i].T, HEAD, axis=1))
        return dict(mix_norm=given["mix_norm"][i][None], w_in=win, w_out=got[2].reshape(-1, D), ssd=ssd, sgu=sgu)

    x = x0
    tape = []
    for i in range(DEPTH):
        got = gathered(i, "ffn1", x)
        token = functools.reduce(lambda a, b: a + b, [g.token for g in gathers.values()]) if i == 0 else 0.0
        P = dict(ffn1=(given["ffn1_norm"][i][None] + token, *got))
        x, s1 = _ffn_fwd(x, *P["ffn1"])
        P.update(mix_params(i, gathered(i, "mix", x)))
        x, s2 = _mix_fwd(x, P)
        P["ffn2"] = (given["ffn2_norm"][i][None], *gathered(i, "ffn2", x))
        x, s3 = _ffn_fwd(x, *P["ffn2"])
        tape.append((P, s1, s2, s3))
    loss_part, dx, dgf = _final_loss(x, given["final_norm"][None], tgt)

    me = 2 * lax.axis_index("x") + lax.axis_index("y")
    jobs = []

    flight = dict(op=None, owners=[], ticks=0)

    def tick(after, begin=None):
        parts, owners = [], []
        if flight["op"] is not None:
            got = flight["op"].wait(after)
            for job, (a0, na) in zip(flight["owners"], flight["op"].spans):
                mine, k = got[a0:a0 + na], len(job["names"])
                if job["stage"] == 1:
                    parts.append(_part_chips(_pair_add(mine[:k], mine[k:], c)))
                elif job["stage"] == 2:
                    parts.append(_part_join(_chip_sum(mine[:k], mine[k:], me, c)))
                else:
                    job.update(stage=4, out=dict(zip(job["names"], mine)))
                    continue
                job["stage"] += 1
                owners.append(job)
        if begin is not None:
            i, gname, gd = begin
            names = [n for n in dict(GROUPS)[gname] if n != "conv_w"]
            jobs.append(dict(key=(i, gname), names=names, stage=1))
            parts.append(_part_sibling([_halved(gd[n]) for n in names]))
            owners.append(jobs[-1])
        flight.update(op=_start_parts(parts, f"rs_tick{flight['ticks']}") if parts else None, owners=owners,
                      ticks=flight["ticks"] + 1)
        return flight["op"].token if parts else 0.0

    grads = [dict() for _ in range(DEPTH)]
    for i in reversed(range(DEPTH)):
        P, s1, s2, s3 = tape[i]
        g = grads[i]
        norm, wg, wu, wd = P["ffn2"]
        mids, (g["ffn2_w_gate"], g["ffn2_w_up"], g["ffn2_w_down"]) = _ffn_bwd_weights(dx, s3, wd)
        tok = tick(g["ffn2_w_down"], (i, "ffn2", g))
        dx, dn2 = _ffn_bwd_input(dx, s3, mids, norm + tok, wg, wu)
        mids, gm = _mix_bwd_weights(dx, s2, P)
        g.update(gm)
        tok = tick(gm["w_in"], (i, "mix", g))
        dx, dnm = _mix_bwd_input(dx, s2, mids, {**P, "mix_norm": P["mix_norm"] + tok})
        norm, wg, wu, wd = P["ffn1"]
        mids, (g["ffn1_w_gate"], g["ffn1_w_up"], g["ffn1_w_down"]) = _ffn_bwd_weights(dx, s1, wd)
        tok = tick(g["ffn1_w_down"], (i, "ffn1", g))
        dx, dn1 = _ffn_bwd_input(dx, s1, mids, norm + tok, wg, wu)
        g["ffn1_norm"], g["mix_norm"], g["ffn2_norm"] = dn1[0], dnm[0], dn2[0]
    grad_x = dx.reshape(given["x"].shape)

    order = [n for n in SMALL if n != "final_norm"] + ["final_norm"]
    small = [jnp.stack([grads[i][n] for i in range(DEPTH)]) for n in order[:-1] + ["conv_w"]]
    small = small[:-1] + [dgf[0], small[-1], loss_part[0, 0:1]]
    n_small = sum(s.size for s in small)
    rows_small = -(-n_small // (128 * 8)) * 8

    def flat(arrs):
        fill = rows_small * 128 - sum(a.size for a in arrs)
        return jnp.concatenate([a.reshape(-1) for a in arrs] + [jnp.zeros((fill,), F32)]).reshape(rows_small, 128)

    gsmall = _all_reduce_small(flat(small)).reshape(-1)

    grad_w = {}
    off = 0
    for n in order:
        size = given[n].size
        grad_w[n] = gsmall[off:off + size].reshape(given[n].shape)
        off += size
    cw = gsmall[off:off + 2 * 4 * SSD_CONV_DIM].reshape(DEPTH, 4, SSD_CONV_DIM)
    grad_w["conv_w"] = lax.dynamic_slice_in_dim(cw, me * (SSD_CONV_DIM // 4), SSD_CONV_DIM // 4, axis=2)
    loss = gsmall[off + 2 * 4 * SSD_CONV_DIM]

    delta, new_m, new_v = {}, {}, {}
    shp = given["conv_w"].shape
    d, m2, v2 = _adamw(*[a.reshape(shp[0] * shp[1], shp[2])
                         for a in (given["conv_w"], grad_w["conv_w"], given["m_conv_w"], given["v_conv_w"])])
    delta["conv_w"], new_m["conv_w"], new_v["conv_w"] = d.reshape(shp), m2.reshape(shp), v2.reshape(shp)
    packed = [flat([given[pre + n] for n in order]) for pre in ("", "m_", "v_")]
    small_out = _adamw(packed[0], gsmall.reshape(rows_small, 128), packed[1], packed[2])
    outs = [o.reshape(-1) for o in small_out]
    off = 0
    for n in order:
        size = given[n].size
        for dst, o in zip((delta, new_m, new_v), outs):
            dst[n] = o[off:off + size].reshape(given[n].shape)
        off += size

    stepped, arrived = {}, {}

    def update_arrived(dep):
        out = None
        for job in jobs:
            if job["stage"] == 4 and not job.get("seen"):
                job["seen"] = True
                for n, full in job["out"].items():
                    view = (lambda a: jnp.swapaxes(a, 1, 2)) if n in TRANSPOSED else (lambda a: a)
                    arrived.setdefault(n, {})[job["key"][0]] = full.reshape(view(given[n]).shape[1:])
                    if len(arrived[n]) == DEPTH:
                        res = _adamw_pair(view(given[n]), arrived[n][0], arrived[n][1], view(given["m_" + n]),
                                          view(given["v_" + n]), dep)
                        stepped[n] = [view(r) for r in res]
                        out = res[0]
        return out

    after = small_out[0]
    while any(j["stage"] < 4 for j in jobs):
        done = update_arrived(jnp.zeros((8, 128), F32) + tok)
        after = after if done is None else done
        tok = tick(after)
    update_arrived(jnp.zeros((8, 128), F32) + tok)
    for n, (d, m2, v2, g) in stepped.items():
        delta[n], new_m[n], new_v[n], grad_w[n] = d, m2, v2, g

    return (loss, grad_x, *[grad_w[n] for n in WEIGHTS], *[delta[n] for n in WEIGHTS],
            *[new_m[n] for n in WEIGHTS], *[new_v[n] for n in WEIGHTS])
```

```python
import functools
import math

import jax
import jax.numpy as jnp
from jax import lax
from jax.experimental import pallas as pl
from jax.experimental.pallas import tpu as pltpu

F32 = jnp.float32
BF = jnp.bfloat16

RMS_EPS = 1e-6
LN_EPS = 1e-5
SEQ = 2048
CHUNK = 128
N_CHUNK = SEQ // CHUNK
ATT_W = 384
HEAD = 64
SSD_W = 384
SSD_CONV_DIM = 896
SSD_STATE = 128
SGU_W = 256
DILATIONS = (1, 4, 16)
W_QKV = 3 * ATT_W
W_SSD = SSD_CONV_DIM + SSD_W + SSD_W
W_UV = 2 * SGU_W
ADAM_LR = 0.001
ADAM_B1 = 0.9
ADAM_B2 = 0.999
ADAM_EPS = 1e-08
ADAM_WD = 0.01
ADAM_STEP = 10
NEG = -1e30
ATTN_BWD_VMEM = 48 * 2 ** 20
ATTN_SUBSEQ_PER_STEP = 4
SGU_CHUNKS_PER_STEP = 4
FFN_VMEM = 60 * 2 ** 20


def _dot(a, b):
    return jnp.dot(a, b, preferred_element_type=F32)


def _dot_nt(a, b):
    return lax.dot_general(a, b, (((1,), (1,)), ((), ())), preferred_element_type=F32)


def _dot_tn(a, b):
    return lax.dot_general(a, b, (((0,), (0,)), ((), ())), preferred_element_type=F32)


def _sigmoid(x):
    return 1.0 / (1.0 + jnp.exp(-x))


def _call(body, *, name, grid, in_specs, out_specs, out_shape, scratch=(), sem=None, vmem=None):
    return pl.pallas_call(
        body, name=name, grid=grid, in_specs=in_specs, out_specs=out_specs, out_shape=out_shape,
        scratch_shapes=list(scratch),
        compiler_params=pltpu.CompilerParams(dimension_semantics=sem, vmem_limit_bytes=vmem),
    )


def _tile(n, want):
    t = min(n, want)
    while n % t:
        t //= 2
    return t


def _final_loss(x, g, tgt):
    T, D = x.shape
    tm = _tile(T, 512)

    def body(x_ref, g_ref, t_ref, l_ref, dx_ref, dg_ref):
        @pl.when(pl.program_id(0) == 0)
        def _():
            dg_ref[...] = jnp.zeros_like(dg_ref)
            l_ref[...] = jnp.zeros_like(l_ref)

        xf = x_ref[...]
        gg = g_ref[...]
        r = lax.rsqrt(jnp.mean(xf * xf, axis=-1, keepdims=True) + RMS_EPS)
        xn = xf * r
        e = xn * gg - t_ref[...]
        part = 0.5 * jnp.sum(jnp.mean(e * e, axis=-1, keepdims=True), axis=0, keepdims=True)
        l_ref[...] += jnp.broadcast_to(part, l_ref.shape)
        dy = e * (1.0 / D)
        u = dy * gg
        mu = jnp.mean(u * xf, axis=-1, keepdims=True)
        dx_ref[...] = r * (u - xf * (r * r * mu))
        dg_ref[...] += jnp.sum(dy * xn, axis=0, keepdims=True)

    row = pl.BlockSpec((tm, D), lambda i: (i, 0))
    vec = pl.BlockSpec((1, D), lambda i: (0, 0))
    lsp = pl.BlockSpec((1, 128), lambda i: (0, 0))
    return _call(body, name="final_loss", grid=(T // tm,), in_specs=[row, vec, row], out_specs=[lsp, row, vec],
                 out_shape=[jax.ShapeDtypeStruct((1, 128), F32), jax.ShapeDtypeStruct((T, D), F32),
                            jax.ShapeDtypeStruct((1, D), F32)],
                 sem=("arbitrary",))(x, g, tgt)


def _resident(shape):
    return pl.BlockSpec(shape, lambda *_: (0,) * len(shape), pipeline_mode=pl.Buffered(1))


def _ffn_fwd_k(x, gn, wg, wu, wd):
    T, D = x.shape
    NS, _, Fs = wg.shape
    tm = _tile(T, 1024)

    def body(x_ref, gn_ref, wg_ref, wu_ref, wd_ref, o_ref, h_ref, s1_ref, s2_ref, a_ref, hs, acc):
        j = pl.program_id(1)

        @pl.when(j == 0)
        def _():
            xf = x_ref[...]
            r = lax.rsqrt(jnp.mean(xf * xf, axis=-1, keepdims=True) + RMS_EPS)
            hs[...] = (xf * r * gn_ref[...]).astype(BF)
            h_ref[...] = hs[...]
            acc[...] = jnp.zeros_like(acc)

        h = hs[...]
        g = _dot(h, wg_ref[...])
        u = _dot(h, wu_ref[...])
        sg = _sigmoid(g)
        s1 = g * sg
        a = (s1 * u).astype(BF)
        s1_ref[...] = s1.astype(BF)
        s2_ref[...] = (u * (sg * (1.0 + g * (1.0 - sg)))).astype(BF)
        a_ref[...] = a
        acc[...] += _dot(a, wd_ref[...])

        @pl.when(j == NS - 1)
        def _():
            o_ref[...] = x_ref[...] + 0.5 * acc[...]

    row = pl.BlockSpec((tm, D), lambda i, j: (i, 0))
    act = pl.BlockSpec((None, tm, Fs), lambda i, j: (j, i, 0))
    sh = jax.ShapeDtypeStruct((NS, T, Fs), BF)
    wspec = lambda w: pl.BlockSpec((None,) + w.shape[1:], lambda i, j: (j, 0, 0))
    return _call(body, name="ffn_fwd", grid=(T // tm, NS),
                 in_specs=[row, pl.BlockSpec((1, D), lambda i, j: (0, 0)), wspec(wg), wspec(wu), wspec(wd)],
                 out_specs=[row, row, act, act, act],
                 out_shape=[jax.ShapeDtypeStruct((T, D), F32), jax.ShapeDtypeStruct((T, D), BF), sh, sh, sh],
                 scratch=[pltpu.VMEM((tm, D), BF), pltpu.VMEM((tm, D), F32)],
                 sem=("parallel", "arbitrary"), vmem=FFN_VMEM)(x, gn, wg, wu, wd)


def _ffn_bwd_act(dxo, s1, s2, wd):
    NS, T, Fs = s1.shape
    D = dxo.shape[1]
    tm = _tile(T, 1024)

    def body(dxo_ref, s1_ref, s2_ref, wd_ref, dg_ref, du_ref, dy_ref, dys):
        j = pl.program_id(1)

        @pl.when(j == 0)
        def _():
            dys[...] = (0.5 * dxo_ref[...]).astype(BF)
            dy_ref[...] = dys[...]

        da = _dot_nt(dys[...], wd_ref[j])
        dg_ref[...] = (da * s2_ref[...].astype(F32)).astype(BF)
        du_ref[...] = (da * s1_ref[...].astype(F32)).astype(BF)

    row = pl.BlockSpec((tm, D), lambda i, j: (i, 0))
    act = pl.BlockSpec((None, tm, Fs), lambda i, j: (j, i, 0))
    sh = jax.ShapeDtypeStruct((NS, T, Fs), BF)
    return _call(body, name="ffn_bwd_act", grid=(T // tm, NS), in_specs=[row, act, act, _resident(wd.shape)],
                 out_specs=[act, act, row], out_shape=[sh, sh, jax.ShapeDtypeStruct((T, D), BF)],
                 scratch=[pltpu.VMEM((tm, D), BF)], sem=("parallel", "arbitrary"))(dxo, s1, s2, wd)


def _ffn_bwd_dx(dg, du, wg, wu, x, gn, dxo):
    NS, T, Fs = dg.shape
    D = x.shape[1]
    tm = _tile(T, 1024)

    def body(dg_ref, du_ref, wg_ref, wu_ref, x_ref, gn_ref, dxo_ref, dx_ref, dgn_ref, acc):
        i, j = pl.program_id(0), pl.program_id(1)

        @pl.when((i == 0) & (j == 0))
        def _():
            dgn_ref[...] = jnp.zeros_like(dgn_ref)

        @pl.when(j == 0)
        def _():
            acc[...] = jnp.zeros_like(acc)

        acc[...] += _dot_nt(dg_ref[...], wg_ref[j]) + _dot_nt(du_ref[...], wu_ref[j])

        @pl.when(j == NS - 1)
        def _():
            xf = x_ref[...]
            r = lax.rsqrt(jnp.mean(xf * xf, axis=-1, keepdims=True) + RMS_EPS)
            dh = acc[...]
            uu = dh * gn_ref[...]
            mu = jnp.mean(uu * xf, axis=-1, keepdims=True)
            dx_ref[...] = dxo_ref[...] + r * (uu - xf * (r * r * mu))
            dgn_ref[...] += jnp.sum(dh * xf * r, axis=0, keepdims=True)

    row = pl.BlockSpec((tm, D), lambda i, j: (i, 0))
    vec = pl.BlockSpec((1, D), lambda i, j: (0, 0))
    act = pl.BlockSpec((None, tm, Fs), lambda i, j: (j, i, 0))
    return _call(body, name="ffn_bwd_dx", grid=(T // tm, NS),
                 in_specs=[act, act, _resident(wg.shape), _resident(wu.shape), row, vec, row], out_specs=[row, vec],
                 out_shape=[jax.ShapeDtypeStruct((T, D), F32), jax.ShapeDtypeStruct((1, D), F32)],
                 scratch=[pltpu.VMEM((tm, D), F32)], sem=("arbitrary", "arbitrary"), vmem=FFN_VMEM)(
        dg, du, wg, wu, x, gn, dxo)


def _ffn_bwd_k2(hb, dyb, a, dg, du):
    NS, T, Fs = a.shape
    D = hb.shape[1]
    tk = _tile(T, 1024)

    def body(h_ref, dy_ref, a_ref, dg_ref, du_ref, og_ref, ou_ref, od_ref):
        @pl.when(pl.program_id(1) == 0)
        def _():
            og_ref[...] = jnp.zeros_like(og_ref)
            ou_ref[...] = jnp.zeros_like(ou_ref)
            od_ref[...] = jnp.zeros_like(od_ref)

        h = h_ref[...]
        og_ref[...] += _dot_tn(dg_ref[...], h)
        ou_ref[...] += _dot_tn(du_ref[...], h)
        od_ref[...] += _dot_tn(a_ref[...], dy_ref[...])

    row = pl.BlockSpec((tk, D), lambda j, k: (k, 0))
    act = pl.BlockSpec((None, tk, Fs), lambda j, k: (j, k, 0))
    return _call(body, name="ffn_bwd_w", grid=(NS, T // tk), in_specs=[row, row, act, act, act],
                 out_specs=[pl.BlockSpec((None, Fs, D), lambda j, k: (j, 0, 0))] * 3,
                 out_shape=[jax.ShapeDtypeStruct((NS, Fs, D), F32)] * 3,
                 sem=("parallel", "arbitrary"))(hb, dyb, a, dg, du)


def _mm_nn(a, b, res=None, out_dtype=F32):
    T, K = a.shape
    N = b.shape[1]
    tm = _tile(T, 512)
    tn = N if N <= 2048 else _tile(N, 1024)

    def body(*refs):
        if res is None:
            a_ref, b_ref, o_ref = refs
            o_ref[...] = _dot(a_ref[...], b_ref[...]).astype(out_dtype)
        else:
            a_ref, b_ref, r_ref, o_ref = refs
            o_ref[...] = (r_ref[...] + _dot(a_ref[...], b_ref[...])).astype(out_dtype)

    o = pl.BlockSpec((tm, tn), lambda i, j: (i, j))
    ins = [pl.BlockSpec((tm, K), lambda i, j: (i, 0)), pl.BlockSpec((K, tn), lambda i, j: (0, j))]
    args = [a, b]
    if res is not None:
        ins.append(o)
        args.append(res)
    return _call(body, name="mm_nn", grid=(T // tm, N // tn), in_specs=ins, out_specs=o,
                 out_shape=jax.ShapeDtypeStruct((T, N), out_dtype), sem=("parallel", "parallel"))(*args)


def _mix_bwd_dy(dxo, w_out):
    T, D = dxo.shape
    tm = _tile(T, 512)
    cuts = (0, ATT_W, ATT_W + SSD_W, ATT_W + SSD_W + SGU_W)

    def body(dx_ref, w_ref, a_ref, s_ref, g_ref):
        d = _dot_nt(dx_ref[...].astype(BF), w_ref[...])
        for o_ref, lo, hi in zip((a_ref, s_ref, g_ref), cuts[:-1], cuts[1:]):
            o_ref[...] = d[:, lo:hi]

    row = lambda w: pl.BlockSpec((tm, w), lambda i: (i, 0))
    return _call(body, name="mix_bwd_dy", grid=(T // tm,), in_specs=[row(D), _resident(w_out.shape)],
                 out_specs=[row(ATT_W), row(SSD_W), row(SGU_W)],
                 out_shape=[jax.ShapeDtypeStruct((T, w), F32) for w in (ATT_W, SSD_W, SGU_W)],
                 sem=("parallel",))(dxo, w_out)


def _mm_tn(a, b):
    T, M = a.shape
    N = b.shape[1]
    tk = _tile(T, 512)
    tmm = _tile(M, 1024)

    def body(a_ref, b_ref, o_ref):
        @pl.when(pl.program_id(1) == 0)
        def _():
            o_ref[...] = jnp.zeros_like(o_ref)

        o_ref[...] += _dot_tn(a_ref[...].astype(BF), b_ref[...].astype(BF))

    return _call(body, name="mm_tn", grid=(M // tmm, T // tk),
                 in_specs=[pl.BlockSpec((tk, tmm), lambda i, k: (k, i)), pl.BlockSpec((tk, N), lambda i, k: (k, 0))],
                 out_specs=pl.BlockSpec((tmm, N), lambda i, k: (i, 0)),
                 out_shape=jax.ShapeDtypeStruct((M, N), F32), sem=("parallel", "arbitrary"), vmem=FFN_VMEM)(a, b)


def _mix_proj(x, gn, win):
    T, D = x.shape
    tm = _tile(T, 512)
    cuts = (0, W_QKV, W_QKV + W_SSD, W_QKV + W_SSD + W_UV)

    def body(x_ref, gn_ref, w_ref, h_ref, q_ref, s_ref, u_ref):
        xf = x_ref[...]
        r = lax.rsqrt(jnp.mean(xf * xf, axis=-1, keepdims=True) + RMS_EPS)
        h = (xf * r * gn_ref[...]).astype(BF)
        h_ref[...] = h
        for o_ref, lo, hi in zip((q_ref, s_ref, u_ref), cuts[:-1], cuts[1:]):
            o_ref[...] = _dot(h, w_ref[:, lo:hi])

    row = lambda w: pl.BlockSpec((tm, w), lambda i: (i, 0))
    return _call(body, name="mix_proj", grid=(T // tm,),
                 in_specs=[row(D), pl.BlockSpec((1, D), lambda i: (0, 0)), _resident(win.shape)],
                 out_specs=[row(D), row(W_QKV), row(W_SSD), row(W_UV)],
                 out_shape=[jax.ShapeDtypeStruct((T, D), BF), jax.ShapeDtypeStruct((T, W_QKV), F32),
                            jax.ShapeDtypeStruct((T, W_SSD), F32), jax.ShapeDtypeStruct((T, W_UV), F32)],
                 sem=("parallel",))(x, gn, win)


def _mix_bwd_dx(dqkv, dsin, duv, win, x, gn, dxo):
    T, D = x.shape
    tm = _tile(T, 512)
    cuts = (0, W_QKV, W_QKV + W_SSD, W_QKV + W_SSD + W_UV)

    def body(dq_ref, ds_ref, du_ref, w_ref, x_ref, gn_ref, dxo_ref, dx_ref, dgn_ref):
        @pl.when(pl.program_id(0) == 0)
        def _():
            dgn_ref[...] = jnp.zeros_like(dgn_ref)

        dh = (_dot_nt(dq_ref[...], w_ref[:, cuts[0]:cuts[1]]) + _dot_nt(ds_ref[...], w_ref[:, cuts[1]:cuts[2]])
              + _dot_nt(du_ref[...], w_ref[:, cuts[2]:cuts[3]]))
        xf = x_ref[...]
        r = lax.rsqrt(jnp.mean(xf * xf, axis=-1, keepdims=True) + RMS_EPS)
        uu = dh * gn_ref[...]
        mu = jnp.mean(uu * xf, axis=-1, keepdims=True)
        dx_ref[...] = dxo_ref[...] + r * (uu - xf * (r * r * mu))
        dgn_ref[...] += jnp.sum(dh * xf * r, axis=0, keepdims=True)

    row = lambda w: pl.BlockSpec((tm, w), lambda i: (i, 0))
    vec = pl.BlockSpec((1, D), lambda i: (0, 0))
    return _call(body, name="mix_bwd_dx", grid=(T // tm,),
                 in_specs=[row(W_QKV), row(W_SSD), row(W_UV), _resident(win.shape), row(D), vec, row(D)],
                 out_specs=[row(D), vec],
                 out_shape=[jax.ShapeDtypeStruct((T, D), F32), jax.ShapeDtypeStruct((1, D), F32)],
                 sem=("arbitrary",))(dqkv, dsin, duv, win, x, gn, dxo)


def _lane_mask(e, width=128):
    return (lax.broadcasted_iota(jnp.int32, (1, width), 1) // HEAD) == e


def _band_mask(n):
    qi = lax.broadcasted_iota(jnp.int32, (CHUNK, 2 * CHUNK), 0)
    kj = lax.broadcasted_iota(jnp.int32, (CHUNK, 2 * CHUNK), 1)
    dist = qi + CHUNK - kj
    return (dist >= 0) & (dist <= CHUNK) & ((kj >= CHUNK) | (n > 0))


def _sub_rows(r, block, dil):
    if dil == 1:
        return pl.ds(pl.multiple_of(block * CHUNK, CHUNK), CHUNK)
    return pl.ds(r + dil * CHUNK * block, CHUNK, stride=dil)


def _attn_specs(T, dil):
    per_step = ATTN_SUBSEQ_PER_STEP if dil == 1 else min(dil, 2 * ATTN_SUBSEQ_PER_STEP)
    qrows = CHUNK * (dil if dil > 1 else per_step)
    B, nbq = T // SEQ, SEQ // qrows
    once = dict(pipeline_mode=pl.Buffered(1))
    q_like = lambda col: pl.BlockSpec((qrows, 128), lambda b, n, r: (b * nbq + n, col), **(once if nbq == 1 else {}))
    k_like = lambda col: pl.BlockSpec((SEQ, 128), lambda b, n, r: (b, col), **once)
    return B, nbq, max(dil // per_step, 1), per_step, q_like, k_like


def _attn_step(u, dil, per_step):
    if dil > 1:
        r = pl.program_id(2) * per_step + u
        return r, pl.program_id(1), _sub_rows(r, 0, dil)
    return 0, pl.program_id(1) * per_step + u, pl.ds(CHUNK * u, CHUNK)


def _attn_fwd(qkv, dil):
    T = qkv.shape[0]
    B, nb, last, per_step, q_like, k_like = _attn_specs(T, dil)
    scale = HEAD ** -0.5

    def body(*refs):
        q_t, k_t, v_t, o_t, l_t = refs[0:3], refs[3:6], refs[6:9], refs[9:12], refs[12:15]
        for u in range(per_step):
            r, n, mine = _attn_step(u, dil, per_step)
            mask = _band_mask(n)
            cur, prv = _sub_rows(r, n, dil), _sub_rows(r, jnp.maximum(n - 1, 0), dil)
            for t in range(3):
                qt = q_t[t][mine, :].astype(BF)
                kt = jnp.concatenate([k_t[t][prv, :], k_t[t][cur, :]], axis=0).astype(BF)
                vt = jnp.concatenate([v_t[t][prv, :], v_t[t][cur, :]], axis=0).astype(BF)
                o_pair = jnp.zeros((CHUNK, 128), F32)
                l_pair = jnp.zeros((CHUNK, 128), F32)
                for e in range(2):
                    lm = _lane_mask(e)
                    s = _dot_nt(jnp.where(lm, qt, jnp.zeros_like(qt)), kt) * scale
                    s = jnp.where(mask, s, NEG)
                    m = jnp.max(s, axis=-1, keepdims=True)
                    p = jnp.exp(s - m)
                    den = jnp.sum(p, axis=-1, keepdims=True)
                    o = _dot(p.astype(BF), vt) / den
                    o_pair = jnp.where(lm, o, o_pair)
                    l_pair = jnp.where(lm, m + jnp.log(den), l_pair)
                o_t[t][mine, :] = o_pair
                l_t[t][mine, :] = l_pair


    out_spec = pl.BlockSpec(q_like(0).block_shape, lambda b, n, r: (b * nb + n, 0))
    sh = jax.ShapeDtypeStruct((T, 128), F32)
    outs = _call(
        body, name=f"attn_fwd_d{dil}", grid=(B, nb, last),
        in_specs=[q_like(t) for t in range(3)] + [k_like(3 + t) for t in range(3)] + [k_like(6 + t) for t in range(3)],
        out_specs=[out_spec] * 6, out_shape=[sh] * 6, sem=("parallel", "arbitrary", "arbitrary"))(*([qkv] * 9))
    return list(outs[0:3]), list(outs[3:6])


def _attn_combine(branches):
    T = branches[0][0][0].shape[0]
    tm = _tile(T, 512)

    def body(*refs):
        y_ref, l_ref = refs[-2:]
        for t in range(3):
            o = [refs[6 * i + t][...] for i in range(3)]
            a, b, c = [refs[6 * i + 3 + t][...] for i in range(3)]
            m = jnp.maximum(jnp.maximum(a, b), c)
            ea, eb, ec = jnp.exp(a - m), jnp.exp(b - m), jnp.exp(c - m)
            z = ea + eb + ec
            y_ref[:, 128 * t:128 * (t + 1)] = (ea * o[0] + eb * o[1] + ec * o[2]) / z
            l_ref[:, 128 * t:128 * (t + 1)] = m + jnp.log(z)

    tile = pl.BlockSpec((tm, 128), lambda i: (i, 0))
    row = pl.BlockSpec((tm, ATT_W), lambda i: (i, 0))
    sh = jax.ShapeDtypeStruct((T, ATT_W), F32)
    flat = [a for o_t, l_t in branches for a in (*o_t, *l_t)]
    return _call(body, name="attn_combine", grid=(T // tm,), in_specs=[tile] * 18, out_specs=[row, row],
                 out_shape=[sh, sh], sem=("parallel",))(*flat)


def _attn_bwd(qkv, do, out, lse, dil):
    T = qkv.shape[0]
    B, nb, last, per_step, q_like, k_like = _attn_specs(T, dil)
    scale = HEAD ** -0.5

    def body(*refs):
        q_t, k_t, v_t = refs[0:3], refs[3:6], refs[6:9]
        do_t, out_t, lse_t = refs[9:12], refs[12:15], refs[15:18]
        dq_t, dk_t, dv_t = refs[18:21], refs[21:24], refs[24:27]
        @pl.when((pl.program_id(1) == 0) & (pl.program_id(2) == 0))
        def _():
            for t in range(3):
                dk_t[t][...] = jnp.zeros_like(dk_t[t])
                dv_t[t][...] = jnp.zeros_like(dv_t[t])

        for u in range(per_step):
            r, n, mine = _attn_step(u, dil, per_step)
            mask = _band_mask(n)
            cur, prv = _sub_rows(r, n, dil), _sub_rows(r, jnp.maximum(n - 1, 0), dil)
            for t in range(3):
                qt = q_t[t][mine, :].astype(BF)
                kt = jnp.concatenate([k_t[t][prv, :], k_t[t][cur, :]], axis=0).astype(BF)
                vt = jnp.concatenate([v_t[t][prv, :], v_t[t][cur, :]], axis=0).astype(BF)
                do_ = do_t[t][mine, :]
                dlt = do_ * out_t[t][mine, :]
                ls = lse_t[t][mine, :]
                dq_pair = jnp.zeros((CHUNK, 128), F32)
                dk_acc = jnp.zeros((2 * CHUNK, 128), F32)
                dv_acc = jnp.zeros((2 * CHUNK, 128), F32)
                for e in range(2):
                    lm = _lane_mask(e)
                    qm = jnp.where(lm, qt, jnp.zeros_like(qt))
                    s = _dot_nt(qm, kt) * scale
                    p = jnp.exp(jnp.where(mask, s - ls[:, HEAD * e:HEAD * e + 1], NEG))
                    dom = jnp.where(lm, do_, 0.0).astype(BF)
                    dv_acc += _dot_tn(p.astype(BF), dom)
                    dp = _dot_nt(dom, vt)
                    delta = jnp.sum(jnp.where(lm, dlt, 0.0), axis=-1, keepdims=True)
                    ds = (p * (dp - delta) * scale).astype(BF)
                    dq_pair += jnp.where(lm, _dot(ds, kt), 0.0)
                    dk_acc += _dot_tn(ds, qm)
                dq_t[t][mine, :] = dq_pair
                dk_t[t][cur, :] = dk_t[t][cur, :] + dk_acc[CHUNK:]
                dk_t[t][prv, :] = dk_t[t][prv, :] + dk_acc[:CHUNK]
                dv_t[t][cur, :] = dv_t[t][cur, :] + dv_acc[CHUNK:]
                dv_t[t][prv, :] = dv_t[t][prv, :] + dv_acc[:CHUNK]

    q_out = pl.BlockSpec(q_like(0).block_shape, lambda b, n, r: (b * nb + n, 0))
    k_out = pl.BlockSpec((SEQ, 128), lambda b, n, r: (b, 0))
    sh = jax.ShapeDtypeStruct((T, 128), F32)
    tiles = lambda: [q_like(t) for t in range(3)]
    return list(_call(
        body, name=f"attn_bwd_d{dil}", grid=(B, nb, last),
        in_specs=tiles() + [k_like(3 + t) for t in range(3)] + [k_like(6 + t) for t in range(3)]
        + tiles() + tiles() + tiles(),
        out_specs=[q_out] * 3 + [k_out] * 6, out_shape=[sh] * 9,
        sem=("parallel", "arbitrary", "arbitrary"), vmem=ATTN_BWD_VMEM)(*([qkv] * 9 + [do] * 3 + [out] * 3 + [lse] * 3)))


def _sum_branches(parts):
    T = parts[0][0].shape[0]
    tm = _tile(T, 512)

    def body(*refs):
        o_ref = refs[-1]
        for c in range(9):
            acc = refs[c][...] + refs[9 + c][...] + refs[18 + c][...]
            o_ref[:, 128 * c:128 * (c + 1)] = acc.astype(BF)

    tile = pl.BlockSpec((tm, 128), lambda i: (i, 0))
    flat = [a for br in parts for a in br]
    return _call(body, name="attn_sum_branches", grid=(T // tm,), in_specs=[tile] * 27,
                 out_specs=pl.BlockSpec((tm, W_QKV), lambda i: (i, 0)),
                 out_shape=jax.ShapeDtypeStruct((T, W_QKV), BF), sem=("parallel",))(*flat)


def _silu(x):
    return x * _sigmoid(x)


def _dsilu(x):
    s = _sigmoid(x)
    return s * (1.0 + x * (1.0 - s))


def _log1p(u):
    return jnp.where(u < 0.01, u * (1.0 - u * (0.5 - u * (1.0 / 3.0))), jnp.log(1.0 + u))


def _softplus(x):
    return jnp.maximum(x, 0.0) + _log1p(jnp.exp(-jnp.abs(x)))


def _cumsum_rows(x, reverse=False):
    n = x.shape[0]
    rows = lax.broadcasted_iota(jnp.int32, x.shape, 0)
    k = 1
    while k < n:
        if reverse:
            x = x + jnp.where(rows < n - k, pltpu.roll(x, n - k, 0), 0.0)
        else:
            x = x + jnp.where(rows >= k, pltpu.roll(x, k, 0), 0.0)
        k *= 2
    return x


def _tri():
    r = lax.broadcasted_iota(jnp.int32, (CHUNK, CHUNK), 0)
    c = lax.broadcasted_iota(jnp.int32, (CHUNK, CHUNK), 1)
    return r >= c


def _row_mask(e):
    return (lax.broadcasted_iota(jnp.int32, (128, 1), 0) // HEAD) == e


def _first_lane(e):
    return lax.broadcasted_iota(jnp.int32, (1, 128), 1) == HEAD * e


def _ssd_pre(x_ref, halo_ref, first, cw_ref, cb_ref, dtb_ref, al_ref, ext):
    row = x_ref[...]
    z = row[:, SSD_CONV_DIM:SSD_CONV_DIM + SSD_W]
    u = row[:, SSD_CONV_DIM + SSD_W:] + dtb_ref[...]
    ext[0:8, :] = jnp.where(first, 0.0, halo_ref[:, 0:SSD_CONV_DIM])
    ext[8:8 + CHUNK, :] = row[:, 0:SSD_CONV_DIM]
    xc = cb_ref[...]
    for j in range(4):
        xc = xc + cw_ref[j:j + 1, :] * ext[pl.ds(5 + j, CHUNK), :]
    xa = _silu(xc)
    dt = _softplus(u)
    a = dt * (-jnp.exp(al_ref[...]))
    A = _cumsum_rows(a)
    return dict(z=z, u=u, xc=xc, xs=xa[:, 0:SSD_W], Bm=xa[:, SSD_W:SSD_W + 256], Cm=xa[:, SSD_W + 256:],
                dt=dt, a=a, A=A, AT=A.T, eA=jnp.exp(A), wdec=jnp.exp(A[CHUNK - 1:CHUNK, :] - A),
                dtot=jnp.exp(A[CHUNK - 1:CHUNK, :]))


def _ssd_y(p, hp_ref, dskip):
    tri = _tri()
    X = p["xs"] * p["dt"]
    Bb = [p["Bm"][:, 128 * g:128 * (g + 1)].astype(BF) for g in range(2)]
    Cb = [p["Cm"][:, 128 * g:128 * (g + 1)].astype(BF) for g in range(2)]
    CB = [_dot_nt(Cb[g], Bb[g]) for g in range(2)]
    tiles = []
    for t in range(3):
        sl = slice(128 * t, 128 * (t + 1))
        hpb = hp_ref[sl, :].astype(BF)
        acc = jnp.zeros((CHUNK, 128), F32)
        for e in range(2):
            h = 2 * t + e
            g, col = h // 3, HEAD * h
            lm = _lane_mask(e)
            L = jnp.exp(jnp.where(tri, p["A"][:, col:col + 1] - p["AT"][col:col + 1, :], NEG))
            yd = _dot((CB[g] * L).astype(BF), jnp.where(lm, X[:, sl], 0.0).astype(BF))
            yo = _dot_nt(Cb[g], hpb) * p["eA"][:, sl]
            acc = acc + jnp.where(lm, yd + yo, 0.0)
        tiles.append(acc)
    return jnp.concatenate(tiles, axis=1) + dskip * p["xs"], X, Bb, Cb, CB


def _group_stats(v):
    g0 = lax.broadcasted_iota(jnp.int32, (1, SSD_W), 1) < SSD_W // 2
    m0 = jnp.sum(jnp.where(g0, v, 0.0), axis=-1, keepdims=True) * (2.0 / SSD_W)
    m1 = jnp.sum(jnp.where(g0, 0.0, v), axis=-1, keepdims=True) * (2.0 / SSD_W)
    return jnp.where(g0, m0, m1)


def _ssd_specs(T, rev):
    B = T // SEQ
    chunk = (lambda c: N_CHUNK - 1 - c) if rev else (lambda c: c)
    row = pl.BlockSpec((B, CHUNK, W_SSD), lambda c: (0, chunk(c), 0))
    halo = pl.BlockSpec((B, 8, W_SSD), lambda c: (0, jnp.maximum(chunk(c) * (CHUNK // 8) - 1, 0), 0))
    hp = pl.BlockSpec((B, None, SSD_W, SSD_STATE), lambda c: (0, chunk(c), 0, 0))
    y = pl.BlockSpec((B, CHUNK, SSD_W), lambda c: (0, chunk(c), 0))
    const = lambda r, w: pl.BlockSpec((r, w), lambda c: (0, 0))
    params = [const(4, SSD_CONV_DIM), const(1, SSD_CONV_DIM)] + [const(1, SSD_W)] * 4
    return B, row, halo, hp, y, const, params


def _ssd_fwd(sin, conv_w, conv_b, dtb, alog, dskip, norm_g):
    T = sin.shape[0]
    B, row, halo, hp, y, const, params = _ssd_specs(T, False)

    def body(xs_ref, halos_ref, cw_ref, cb_ref, dtb_ref, al_ref, dk_ref, ng_ref, ys_ref, hps_ref, exts, hsts):
        @pl.when(pl.program_id(0) == 0)
        def _():
            hsts[...] = jnp.zeros_like(hsts)

        for b in range(B):
            one(xs_ref.at[b], halos_ref.at[b], cw_ref, cb_ref, dtb_ref, al_ref, dk_ref, ng_ref, ys_ref.at[b],
                hps_ref.at[b], exts.at[b], hsts.at[b])

    def one(x_ref, halo_ref, cw_ref, cb_ref, dtb_ref, al_ref, dk_ref, ng_ref, y_ref, hp_ref, ext, hst):
        c = pl.program_id(0)
        p = _ssd_pre(x_ref, halo_ref, c == 0, cw_ref, cb_ref, dtb_ref, al_ref, ext)
        yv, X, Bb, Cb, CB = _ssd_y(p, hst, dk_ref[...])
        hp_ref[...] = hst[...]
        for t in range(3):
            sl = slice(128 * t, 128 * (t + 1))
            old = hst[sl, :]
            new = old
            for e in range(2):
                h = 2 * t + e
                g, col = h // 3, HEAD * h
                st = _dot_tn(jnp.where(_lane_mask(e), X[:, sl] * p["wdec"][:, sl], 0.0).astype(BF), Bb[g])
                new = jnp.where(_row_mask(e), old * p["dtot"][:, col:col + 1] + st, new)
            hst[sl, :] = new
        y2 = yv * _silu(p["z"])
        r = lax.rsqrt(_group_stats(y2 * y2) + RMS_EPS)
        y_ref[...] = y2 * r * ng_ref[...]

    sin3 = sin.reshape(B, SEQ, W_SSD)
    yo, hprev = _call(
        body, name="ssd_fwd", grid=(N_CHUNK,), in_specs=[row, halo] + params, out_specs=[y, hp],
        out_shape=[jax.ShapeDtypeStruct((B, SEQ, SSD_W), F32),
                   jax.ShapeDtypeStruct((B, N_CHUNK, SSD_W, SSD_STATE), F32)],
        scratch=[pltpu.VMEM((B, 8 + CHUNK, SSD_CONV_DIM), F32), pltpu.VMEM((B, SSD_W, SSD_STATE), F32)],
        sem=("arbitrary",))(sin3, sin3, conv_w, conv_b, dtb, alog, dskip, norm_g)
    return yo.reshape(T, SSD_W), hprev


def _ssd_bwd(sin, hprev, dy3, conv_w, conv_b, dtb, alog, dskip, norm_g):
    T = sin.shape[0]
    B, row, halo, hp, y, const, params = _ssd_specs(T, True)

    def body(xs_ref, halos_ref, hps_ref, dys_ref, cw_ref, cb_ref, dtb_ref, al_ref, dk_ref, ng_ref,
             dxs_ref, dcw_ref, dcb_ref, dvec_ref, exts, ext2s, dhs):
        @pl.when(pl.program_id(0) == 0)
        def _():
            dcw_ref[...] = jnp.zeros_like(dcw_ref)
            dcb_ref[...] = jnp.zeros_like(dcb_ref)
            dvec_ref[...] = jnp.zeros_like(dvec_ref)
            dhs[...] = jnp.zeros_like(dhs)
            ext2s[:, CHUNK:CHUNK + 8, :] = jnp.zeros((B, 8, SSD_CONV_DIM), F32)

        for b in range(B):
            one(xs_ref.at[b], halos_ref.at[b], hps_ref.at[b], dys_ref.at[b], cw_ref, cb_ref, dtb_ref, al_ref, dk_ref,
                ng_ref, dxs_ref.at[b], dcw_ref, dcb_ref, dvec_ref, exts.at[b], ext2s.at[b], dhs.at[b])

    def one(x_ref, halo_ref, hp_ref, dy_ref, cw_ref, cb_ref, dtb_ref, al_ref, dk_ref, ng_ref,
            dx_ref, dcw_ref, dcb_ref, dvec_ref, ext, ext2, dh):
        c = pl.program_id(0)
        p = _ssd_pre(x_ref, halo_ref, c == N_CHUNK - 1, cw_ref, cb_ref, dtb_ref, al_ref, ext)
        dskip_ = dk_ref[...]
        yv, X, Bb, Cb, CB = _ssd_y(p, hp_ref, dskip_)
        xs, z, A, AT = p["xs"], p["z"], p["A"], p["AT"]

        sz = _silu(z)
        y2 = yv * sz
        r = lax.rsqrt(_group_stats(y2 * y2) + RMS_EPS)
        dy3_ = dy_ref[...]
        uu = dy3_ * ng_ref[...]
        dy2 = r * (uu - y2 * (r * r * _group_stats(uu * y2)))
        dy = dy2 * sz
        dz = dy2 * yv * _dsilu(z)

        tri = _tri()
        rows = lax.broadcasted_iota(jnp.int32, (CHUNK, 1), 0)
        dG = [jnp.zeros((CHUNK, CHUNK), F32) for _ in range(2)]
        dB = [jnp.zeros((CHUNK, SSD_STATE), F32) for _ in range(2)]
        dC = [jnp.zeros((CHUNK, SSD_STATE), F32) for _ in range(2)]
        dX_t, dA_t, ddtx_t = [], [], []
        for t in range(3):
            sl = slice(128 * t, 128 * (t + 1))
            hp_t = hp_ref[sl, :]
            hpb = hp_t.astype(BF)
            dhc = dh[sl, :]
            dh_new = jnp.zeros((128, SSD_STATE), F32)
            dX = jnp.zeros((CHUNK, 128), F32)
            dA = jnp.zeros((CHUNK, 128), F32)
            ddtx = jnp.zeros((CHUNK, 128), F32)
            for e in range(2):
                h = 2 * t + e
                g, col = h // 3, HEAD * h
                lm, rm, fl = _lane_mask(e), _row_mask(e), _first_lane(e)
                L = jnp.exp(jnp.where(tri, A[:, col:col + 1] - AT[col:col + 1, :], NEG))
                Mf = CB[g] * L
                Xm = jnp.where(lm, X[:, sl], 0.0)
                Xmb = Xm.astype(BF)
                dyh = jnp.where(lm, dy[:, sl], 0.0)
                dyb = dyh.astype(BF)
                dXh = _dot_tn(Mf.astype(BF), dyb)
                dM = jnp.where(tri, _dot_nt(dyb, Xmb), 0.0)
                Wm = dM * Mf
                dAc = jnp.sum(Wm, axis=-1, keepdims=True) - jnp.sum(Wm.T, axis=-1, keepdims=True)
                dG[g] = dG[g] + dM * L
                eAt = p["eA"][:, sl]
                yo = _dot_nt(Cb[g], hpb)
                dyo = (dyh * eAt).astype(BF)
                dC[g] = dC[g] + _dot(dyo, hpb)
                dh_new = dh_new + _dot_tn(dyo, Cb[g])
                dAc = dAc + jnp.sum(dyh * yo * eAt, axis=-1, keepdims=True)
                dHn = jnp.where(rm, dhc, 0.0)
                dHnb = dHn.astype(BF)
                dec = p["dtot"][:, col:col + 1]
                dh_new = dh_new + dec * dHn
                Z = _dot_nt(Bb[g], dHnb)
                wt = p["wdec"][:, sl]
                xi = jnp.sum(Xm * Z, axis=-1, keepdims=True) * p["wdec"][:, col:col + 1]
                dXh = dXh + wt * Z
                dB[g] = dB[g] + _dot(jnp.where(lm, X[:, sl] * wt, 0.0).astype(BF), dHnb)
                dAtot = jnp.sum(xi, axis=0, keepdims=True) + dec * jnp.sum(
                    jnp.sum(dHn * hp_t, axis=-1, keepdims=True), axis=0, keepdims=True)
                dAc = dAc - xi + jnp.where(rows == CHUNK - 1, dAtot, 0.0)
                dA = dA + jnp.where(fl, dAc, 0.0)
                dX = dX + dXh
                ddtx = ddtx + jnp.where(fl, jnp.sum(dXh * xs[:, sl], axis=-1, keepdims=True), 0.0)
            dh[sl, :] = dh_new
            dX_t.append(dX)
            dA_t.append(dA)
            ddtx_t.append(ddtx)
        for g in range(2):
            dGb = dG[g].astype(BF)
            dC[g] = dC[g] + _dot(dGb, Bb[g])
            dB[g] = dB[g] + _dot_tn(dGb, Cb[g])
        dXf = jnp.concatenate(dX_t, axis=1)
        da = _cumsum_rows(jnp.concatenate(dA_t, axis=1), reverse=True)
        ddt = da * (-jnp.exp(al_ref[...])) + jnp.concatenate(ddtx_t, axis=1)
        du = ddt * _sigmoid(p["u"])
        dxs = dXf * p["dt"] + dskip_ * dy
        dxc = jnp.concatenate([dxs, dB[0], dB[1], dC[0], dC[1]], axis=1) * _dsilu(p["xc"])
        ext2[0:CHUNK, :] = dxc
        dxbc = jnp.zeros((CHUNK, SSD_CONV_DIM), F32)
        for j in range(4):
            dxbc = dxbc + cw_ref[j:j + 1, :] * ext2[pl.ds(3 - j, CHUNK), :]
            dcw_ref[j:j + 1, :] += jnp.sum(dxc * ext[pl.ds(5 + j, CHUNK), :], axis=0, keepdims=True)
        ext2[CHUNK:CHUNK + 8, :] = dxc[0:8, :]
        dcb_ref[...] += jnp.sum(dxc, axis=0, keepdims=True)
        dvec_ref[0:1, :] += jnp.sum(du, axis=0, keepdims=True)
        dvec_ref[1:2, :] += jnp.sum(da * p["a"], axis=0, keepdims=True)
        dvec_ref[2:3, :] += jnp.sum(dy * xs, axis=0, keepdims=True)
        dvec_ref[3:4, :] += jnp.sum(dy3_ * y2 * r, axis=0, keepdims=True)
        dx_ref[...] = jnp.concatenate([dxbc, dz, du], axis=1).astype(BF)

    sin3 = sin.reshape(B, SEQ, W_SSD)
    out = _call(body, name="ssd_bwd", grid=(N_CHUNK,), in_specs=[row, halo, hp, y] + params,
                out_specs=[row, const(4, SSD_CONV_DIM), const(1, SSD_CONV_DIM), const(8, SSD_W)],
                out_shape=[jax.ShapeDtypeStruct((B, SEQ, W_SSD), BF), jax.ShapeDtypeStruct((4, SSD_CONV_DIM), F32),
                           jax.ShapeDtypeStruct((1, SSD_CONV_DIM), F32), jax.ShapeDtypeStruct((8, SSD_W), F32)],
                scratch=[pltpu.VMEM((B, 8 + CHUNK, SSD_CONV_DIM), F32), pltpu.VMEM((B, 8 + CHUNK, SSD_CONV_DIM), F32),
                         pltpu.VMEM((B, SSD_W, SSD_STATE), F32)],
                sem=("arbitrary",))(sin3, sin3, hprev, dy3.reshape(B, SEQ, SSD_W), conv_w, conv_b, dtb, alog, dskip,
                                    norm_g)
    return (out[0].reshape(T, W_SSD),) + tuple(out[1:])


def _sgu_weights(w_ref):
    tri = _tri()
    return [jnp.where(tri, w_ref[gi], 0.0).astype(BF) for gi in range(4)]


def _sgu_core(x, g_ref, b_ref, wc, bias_ref):
    cdf = 0.5 * (1.0 + lax.erf(x * (2.0 ** -0.5)))
    ge = x * cdf
    dge = cdf + x * jnp.exp(-0.5 * x * x) * ((2.0 * math.pi) ** -0.5)
    u, v = ge[:, 0:SGU_W], ge[:, SGU_W:]
    vc = v - jnp.mean(v, axis=-1, keepdims=True)
    rstd = lax.rsqrt(jnp.mean(vc * vc, axis=-1, keepdims=True) + LN_EPS)
    vhat = vc * rstd
    vn = vhat * g_ref[...] + b_ref[...]
    vm = [jnp.where(_lane_mask(gi % 2), vn[:, 128 * (gi // 2):128 * (gi // 2 + 1)], 0.0).astype(BF) for gi in range(4)]
    mixed = jnp.concatenate([_dot(wc[2 * t], vm[2 * t]) + _dot(wc[2 * t + 1], vm[2 * t + 1]) for t in range(2)],
                            axis=1) + bias_ref[...]
    return dict(dge=dge, u=u, rstd=rstd, vhat=vhat, vm=vm, mixed=mixed)


SGU_ROWS = SGU_CHUNKS_PER_STEP * CHUNK


def _sgu_chunk_rows():
    return [pl.ds(j * CHUNK, CHUNK) for j in range(SGU_CHUNKS_PER_STEP)]


def _sgu_specs():
    vec = pl.BlockSpec((1, SGU_W), lambda i: (0, 0))
    return [pl.BlockSpec((SGU_ROWS, W_UV), lambda i: (i, 0)), vec, vec,
            pl.BlockSpec((4, CHUNK, CHUNK), lambda i: (0, 0, 0)), pl.BlockSpec((CHUNK, SGU_W), lambda i: (0, 0))]


def _sgu_fwd(uv, ln_g, ln_b, w, bias):
    T = uv.shape[0]

    def body(uv_ref, g_ref, b_ref, w_ref, bias_ref, y_ref):
        wc = _sgu_weights(w_ref)
        for rows in _sgu_chunk_rows():
            s = _sgu_core(uv_ref[rows, :], g_ref, b_ref, wc, bias_ref)
            y_ref[rows, :] = s["u"] * s["mixed"]

    return _call(body, name="sgu_fwd", grid=(T // SGU_ROWS,), in_specs=_sgu_specs(),
                 out_specs=pl.BlockSpec((SGU_ROWS, SGU_W), lambda i: (i, 0)),
                 out_shape=jax.ShapeDtypeStruct((T, SGU_W), F32), sem=("parallel",))(uv, ln_g, ln_b, w, bias)


def _sgu_bwd(uv, dy, ln_g, ln_b, w, bias):
    T = uv.shape[0]

    def body(uv_ref, dy_ref, g_ref, b_ref, w_ref, bias_ref, dx_ref, dw_ref, dbias_ref, dln_ref):
        @pl.when(pl.program_id(0) == 0)
        def _():
            dw_ref[...] = jnp.zeros_like(dw_ref)
            dbias_ref[...] = jnp.zeros_like(dbias_ref)
            dln_ref[...] = jnp.zeros_like(dln_ref)

        tri = _tri()
        wc = _sgu_weights(w_ref)
        for rows in _sgu_chunk_rows():
            s = _sgu_core(uv_ref[rows, :], g_ref, b_ref, wc, bias_ref)
            dy_ = dy_ref[rows, :]
            du = dy_ * s["mixed"]
            dmix = dy_ * s["u"]
            dbias_ref[...] += dmix
            dvn_t = []
            for t in range(2):
                acc = jnp.zeros((CHUNK, 128), F32)
                for e in range(2):
                    gi = 2 * t + e
                    dmg = jnp.where(_lane_mask(e), dmix[:, 128 * t:128 * (t + 1)], 0.0).astype(BF)
                    acc = acc + _dot_tn(wc[gi], dmg)
                    dw_ref[gi] += jnp.where(tri, _dot_nt(dmg, s["vm"][gi]), 0.0)
                dvn_t.append(acc)
            dvn = jnp.concatenate(dvn_t, axis=1)
            dln_ref[0:1, :] += jnp.sum(dvn * s["vhat"], axis=0, keepdims=True)
            dln_ref[1:2, :] += jnp.sum(dvn, axis=0, keepdims=True)
            dvh = dvn * g_ref[...]
            dv = s["rstd"] * (dvh - jnp.mean(dvh, axis=-1, keepdims=True)
                              - s["vhat"] * jnp.mean(dvh * s["vhat"], axis=-1, keepdims=True))
            dx_ref[rows, :] = (jnp.concatenate([du, dv], axis=1) * s["dge"]).astype(BF)

    ins = _sgu_specs()
    return _call(body, name="sgu_bwd", grid=(T // SGU_ROWS,),
                 in_specs=[ins[0], pl.BlockSpec((SGU_ROWS, SGU_W), lambda i: (i, 0))] + ins[1:],
                 out_specs=[pl.BlockSpec((SGU_ROWS, W_UV), lambda i: (i, 0)),
                            pl.BlockSpec((4, CHUNK, CHUNK), lambda i: (0, 0, 0)),
                            pl.BlockSpec((CHUNK, SGU_W), lambda i: (0, 0)), pl.BlockSpec((8, SGU_W), lambda i: (0, 0))],
                 out_shape=[jax.ShapeDtypeStruct((T, W_UV), BF), jax.ShapeDtypeStruct((4, CHUNK, CHUNK), F32),
                            jax.ShapeDtypeStruct((CHUNK, SGU_W), F32), jax.ShapeDtypeStruct((8, SGU_W), F32)],
                 sem=("arbitrary",))(uv, dy, ln_g, ln_b, w, bias)


def _adamw(w, g, m, v):
    R, C = w.shape
    tr = R

    def body(w_ref, g_ref, m_ref, v_ref, d_ref, nm_ref, nv_ref):
        g_ = g_ref[...]
        m2 = ADAM_B1 * m_ref[...] + (1.0 - ADAM_B1) * g_
        v2 = ADAM_B2 * v_ref[...] + (1.0 - ADAM_B2) * (g_ * g_)
        m_hat = m2 / (1.0 - ADAM_B1 ** ADAM_STEP)
        v_hat = v2 / (1.0 - ADAM_B2 ** ADAM_STEP)
        d_ref[...] = -ADAM_LR * (m_hat / (jnp.sqrt(v_hat) + ADAM_EPS) + ADAM_WD * w_ref[...])
        nm_ref[...] = m2
        nv_ref[...] = v2

    blk = pl.BlockSpec((tr, C), lambda i: (i, 0))
    sh = jax.ShapeDtypeStruct((R, C), F32)
    return _call(body, name="adamw", grid=(R // tr,), in_specs=[blk] * 4, out_specs=[blk] * 3,
                 out_shape=[sh] * 3, sem=("parallel",))(w, g, m, v)


def _adamw_pair(w, g0, g1, m, v, dep):
    L, R, C = w.shape
    tr = max(t for t in range(8, R + 1, 8) if R % t == 0 and t * C * 4 <= 3 * 2 ** 19)

    def body(w_ref, g0_ref, g1_ref, m_ref, v_ref, dep_ref, d_ref, nm_ref, nv_ref, og_ref):
        g_ = jnp.where(pl.program_id(0) == 0, g0_ref[...], g1_ref[...])
        m2 = ADAM_B1 * m_ref[...] + (1.0 - ADAM_B1) * g_
        v2 = ADAM_B2 * v_ref[...] + (1.0 - ADAM_B2) * (g_ * g_)
        m_hat = m2 / (1.0 - ADAM_B1 ** ADAM_STEP)
        v_hat = v2 / (1.0 - ADAM_B2 ** ADAM_STEP)
        d_ref[...] = -ADAM_LR * (m_hat / (jnp.sqrt(v_hat) + ADAM_EPS) + ADAM_WD * w_ref[...])
        nm_ref[...] = m2
        nv_ref[...] = v2
        og_ref[...] = g_

    lay = pl.BlockSpec((None, tr, C), lambda l, i: (l, i, 0))
    one = lambda k: pl.BlockSpec((tr, C), lambda l, i: (jnp.where(l == k, i, 0), 0))
    return _call(body, name="adamw_pair", grid=(L, R // tr),
                 in_specs=[lay, one(0), one(1), lay, lay, pl.BlockSpec((8, 128), lambda l, i: (0, 0))],
                 out_specs=[lay] * 4,
                 out_shape=[jax.ShapeDtypeStruct((L, R, C), F32)] * 4,
                 sem=("parallel", "parallel"))(w, g0, g1, m, v, dep)


def _row_steps(rows):
    return 2 if rows % 32 == 0 else 1


def _pair_add(gbufs, rsibs, c):
    n = len(gbufs)
    steps = min(_row_steps(g.shape[2]) for g in gbufs)

    def body(c_ref, *refs):
        for a_ref, b_ref, o_ref in zip(refs[:n], refs[n:2 * n], refs[2 * n:]):
            o_ref[...] = (a_ref[...] + b_ref[...]).astype(BF)

    def specs(g):
        tr, C = g.shape[2] // steps, g.shape[3]
        return (pl.BlockSpec((None, None, tr, C), lambda j, i, c_ref: (j, c_ref[0], i, 0)),
                pl.BlockSpec((None, tr, C), lambda j, i, c_ref: (j, i, 0)))

    return list(pl.pallas_call(
        body, name="rs_pair_add",
        grid_spec=pltpu.PrefetchScalarGridSpec(
            num_scalar_prefetch=1, grid=(4, steps),
            in_specs=[specs(g)[0] for g in gbufs] + [specs(g)[1] for g in gbufs],
            out_specs=[specs(g)[1] for g in gbufs]),
        out_shape=[jax.ShapeDtypeStruct((4,) + g.shape[2:], BF) for g in gbufs],
        compiler_params=pltpu.CompilerParams(dimension_semantics=("parallel", "parallel")),
    )(jnp.reshape(c, (1,)).astype(jnp.int32), *gbufs, *rsibs))


def _chip_sum(pairs, recvs, me, c):
    n = len(pairs)
    steps = min(_row_steps(p.shape[1]) for p in pairs)

    def body(s_ref, *refs):
        for own_ref, p_ref, o_ref in zip(refs[:n], refs[n:2 * n], refs[2 * n:]):
            p = [jnp.where(s_ref[0] == j, own_ref[...], p_ref[j]).astype(F32) for j in range(4)]
            o_ref[...] = ((p[0] + p[1]) + p[2]) + p[3]

    def specs(p):
        tr, C = p.shape[1] // steps, p.shape[2]
        return (pl.BlockSpec((None, tr, C), lambda i, s: (s[0], i, 0)), pl.BlockSpec((4, tr, C), lambda i, s: (0, i, 0)),
                pl.BlockSpec((None, tr, C), lambda i, s: (s[1], i, 0)))

    return list(pl.pallas_call(
        body, name="rs_chip_sum",
        grid_spec=pltpu.PrefetchScalarGridSpec(
            num_scalar_prefetch=1, grid=(steps,),
            in_specs=[specs(p)[0] for p in pairs] + [specs(p)[1] for p in pairs],
            out_specs=[specs(p)[2] for p in pairs]),
        out_shape=[jax.ShapeDtypeStruct((2,) + p.shape[1:], F32) for p in pairs],
        compiler_params=pltpu.CompilerParams(dimension_semantics=("parallel",)),
    )(jnp.stack([me, c]).astype(jnp.int32), *pairs, *recvs))


MESH = pl.DeviceIdType.MESH
ANY = pl.BlockSpec(memory_space=pl.ANY)


def _place():
    x, y, c = lax.axis_index("x"), lax.axis_index("y"), lax.axis_index("c")
    return x, y, c, [(1 - x, y), (x, 1 - y), (1 - x, 1 - y)]


HBM = pl.BlockSpec(memory_space=pltpu.HBM)
SEM = pl.BlockSpec(memory_space=pltpu.SEMAPHORE)
EFFECT = pltpu.SideEffectType.DATAFLOW_SIDE_EFFECTING


class _Split:
    def __init__(self, tag, arrays, copies, n_copies, after=()):
        self.tag, self.copies, k = tag, copies, len(arrays)

        def body(*refs):
            sems = k + len(after)
            for cp in copies(refs[:k], refs[sems], refs[sems + 1]):
                cp.start()
            refs[-1][...] = jnp.zeros_like(refs[-1])

        out = pl.pallas_call(
            body, name=tag + "_start",
            out_shape=(pltpu.SemaphoreType.DMA((n_copies,)), pltpu.SemaphoreType.DMA((n_copies,)),
                       *[pltpu.HBM(a.shape, a.dtype) for a in arrays], jax.ShapeDtypeStruct((8, 128), F32)),
            in_specs=[HBM] * k + [ANY] * len(after),
            out_specs=(SEM, SEM, *[HBM] * k, pl.BlockSpec(memory_space=pltpu.VMEM)),
            input_output_aliases={i: 2 + i for i in range(k)},
            compiler_params=pltpu.CompilerParams(has_side_effects=EFFECT),
        )(*[pltpu.with_memory_space_constraint(a, pltpu.HBM) for a in arrays], *after)
        self.send, self.recv, self.arrays, self.token_array = out[0], out[1], list(out[2:2 + k]), out[-1]
        self.token = self.token_array[0, 0]

    def wait(self, after):
        k, copies = len(self.arrays), self.copies
        after = list(after) if isinstance(after, (list, tuple)) else [after]

        def body(*refs):
            for cp in copies(refs[:k], refs[k], refs[k + 1]):
                cp.wait_send()
                cp.wait_recv()

        return list(pl.pallas_call(
            body, name=self.tag + "_wait", out_shape=tuple(pltpu.HBM(a.shape, a.dtype) for a in self.arrays),
            in_specs=[HBM] * k + [SEM, SEM] + [ANY] * len(after), out_specs=tuple([HBM] * k),
            input_output_aliases={i: i for i in range(k)},
            compiler_params=pltpu.CompilerParams(has_side_effects=EFFECT),
        )(*self.arrays, self.send, self.recv, *after))


def _landing_zones(arrs):
    me = 2 * lax.axis_index("x") + lax.axis_index("y")
    return [lax.dynamic_update_index_in_dim(lax.empty((4,) + a.shape, a.dtype), a, me, 0) for a in arrs]


def _gather_start(arrs, lands, tag, after=()):
    n = len(arrs)

    def copies(refs, send, recv):
        x, y, c, chips = _place()
        return [pltpu.make_async_remote_copy(
            src_ref=refs[k], dst_ref=refs[n + k].at[2 * x + y], send_sem=send.at[3 * k + r],
            recv_sem=recv.at[3 * k + r], device_id=(px, py, c), device_id_type=MESH)
            for k in range(n) for r, (px, py) in enumerate(chips)]

    return _Split("gather_" + tag, list(arrs) + lands, copies, 3 * n, after)


def _gather_halves_start(arrs, tag):
    n = len(arrs)
    lands = _landing_zones(arrs)

    def copies(refs, send, recv):
        x, y, c, chips = _place()
        return [pltpu.make_async_remote_copy(
            src_ref=refs[k].at[c], dst_ref=refs[n + k].at[2 * x + y, c], send_sem=send.at[3 * k + r],
            recv_sem=recv.at[3 * k + r], device_id=(px, py, c), device_id_type=MESH)
            for k in range(n) for r, (px, py) in enumerate(chips)]

    return _Split("gather_" + tag, list(arrs) + lands, copies, 3 * n)


def _gather_halves_finish(lands, tag):
    n = len(lands)

    def copies(refs, send, recv):
        x, y, c, chips = _place()
        return [pltpu.make_async_remote_copy(
            src_ref=refs[k].at[2 * px + py, c], dst_ref=refs[k].at[2 * px + py, c], send_sem=send.at[3 * k + r],
            recv_sem=recv.at[3 * k + r], device_id=(x, y, 1 - c), device_id_type=MESH)
            for k in range(n) for r, (px, py) in enumerate(chips)]

    return _Split("gather_pass_" + tag, list(lands), copies, 3 * n)


def _part_sibling(gbufs):
    n = len(gbufs)

    def copies(refs, send, recv, off):
        x, y, c, _ = _place()
        return [pltpu.make_async_remote_copy(
            src_ref=refs[k].at[j, 1 - c], dst_ref=refs[n + k].at[j], send_sem=send.at[off + 4 * k + j],
            recv_sem=recv.at[off + 4 * k + j], device_id=(x, y, 1 - c), device_id_type=MESH)
            for k in range(n) for j in range(4)]

    return list(gbufs) + [lax.empty((4,) + g.shape[2:], g.dtype) for g in gbufs], 4 * n, copies


def _part_chips(pbufs):
    n = len(pbufs)

    def copies(refs, send, recv, off):
        x, y, c, chips = _place()
        return [pltpu.make_async_remote_copy(
            src_ref=refs[k].at[2 * px + py], dst_ref=refs[n + k].at[2 * x + y], send_sem=send.at[off + 3 * k + r],
            recv_sem=recv.at[off + 3 * k + r], device_id=(px, py, c), device_id_type=MESH)
            for k in range(n) for r, (px, py) in enumerate(chips)]

    return list(pbufs) + [lax.empty(p.shape, p.dtype) for p in pbufs], 3 * n, copies


def _part_join(fulls):
    def copies(refs, send, recv, off):
        x, y, c, _ = _place()
        return [pltpu.make_async_remote_copy(
            src_ref=refs[k].at[c], dst_ref=refs[k].at[c], send_sem=send.at[off + k], recv_sem=recv.at[off + k],
            device_id=(x, y, 1 - c), device_id_type=MESH) for k in range(len(fulls))]

    return list(fulls), len(fulls), copies


def _start_parts(parts, tag):
    arrays, spans, total = [], [], 0
    for arrs, n_copies, fn in parts:
        spans.append((len(arrays), len(arrs), total, fn))
        arrays += arrs
        total += n_copies

    def copies(refs, send, recv):
        return [cp for a0, na, off, fn in spans for cp in fn(refs[a0:a0 + na], send, recv, off)]

    op = _Split(tag, arrays, copies, total)
    op.spans = [(a0, na) for a0, na, _, _ in spans]
    return op


def _all_reduce_small(v):
    R, C = v.shape

    def body(v_ref, o_ref, g_ref, send, recv, loc):
        x, y, c, chips = _place()
        me, sibling = (x, y, c), (x, y, 1 - c)

        def rows(px, py, pc):
            return g_ref.at[4 * px + 2 * py + pc]

        def copy(k, block, to, src=None):
            return pltpu.make_async_remote_copy(
                src_ref=rows(*block) if src is None else src, dst_ref=rows(*block),
                send_sem=send.at[k], recv_sem=recv.at[k], device_id=to, device_id_type=MESH)

        mine = pltpu.make_async_copy(v_ref, rows(*me), loc)
        mine.start()
        first = [copy(0, me, sibling, src=v_ref)]
        first += [copy(1 + j, me, (*chip, c), src=v_ref) for j, chip in enumerate(chips)]
        for cp in first:
            cp.start()
        passed = [copy(4 + j, (*chip, c), sibling) for j, chip in enumerate(chips)]
        for j, chip in enumerate(chips):
            copy(1 + j, (*chip, c), me).wait_recv()
            passed[j].start()
        copy(0, sibling, me).wait_recv()
        for j, chip in enumerate(chips):
            copy(4 + j, (*chip, 1 - c), me).wait_recv()
        for cp in first + passed:
            cp.wait_send()
        mine.wait()
        acc = g_ref[0]
        for d in range(1, 8):
            acc = acc + g_ref[d]
        o_ref[...] = acc

    vm = pl.BlockSpec(memory_space=pltpu.VMEM)
    return pl.pallas_call(
        body, name="all_reduce_small", in_specs=[vm], out_specs=[vm, vm],
        out_shape=[jax.ShapeDtypeStruct((R, C), F32), jax.ShapeDtypeStruct((8, R, C), F32)],
        scratch_shapes=[pltpu.SemaphoreType.DMA((7,)), pltpu.SemaphoreType.DMA((7,)), pltpu.SemaphoreType.DMA],
    )(v)[0]


WEIGHTS = ['ffn1_norm', 'ffn1_w_gate', 'ffn1_w_up', 'ffn1_w_down', 'mix_norm', 'w_in', 'conv_w', 'conv_b', 'dt_bias',
           'a_log', 'd_skip', 'ssd_norm', 'sgu_ln_g', 'sgu_ln_b', 'sgu_w', 'sgu_b', 'w_out', 'ffn2_norm',
           'ffn2_w_gate', 'ffn2_w_up', 'ffn2_w_down', 'final_norm']
SHARDED = ['ffn1_w_gate', 'ffn1_w_up', 'ffn1_w_down', 'w_in', 'conv_w', 'w_out', 'ffn2_w_gate', 'ffn2_w_up',
           'ffn2_w_down']
SMALL = [n for n in WEIGHTS if n not in SHARDED]
GROUPS = [("ffn1", ["ffn1_w_gate", "ffn1_w_up", "ffn1_w_down"]), ("mix", ["w_in", "conv_w", "w_out"]),
          ("ffn2", ["ffn2_w_gate", "ffn2_w_up", "ffn2_w_down"])]
TRANSPOSED = ("ffn1_w_gate", "ffn1_w_up", "ffn2_w_gate", "ffn2_w_up")
DEPTH = 2


def _pack_w_in(w):
    return jnp.concatenate([w[..., 0:1152], w[..., 1536:2432], w[..., 1152:1536],
                            jnp.repeat(w[..., 2432:2438], HEAD, axis=-1), w[..., 2438:2950]], axis=-1)


def _unpack_w_in(dq, ds, du):
    return jnp.concatenate([dq, ds[:, 896:1280], ds[:, 0:896], ds[:, 1280::HEAD], du], axis=-1)


def _ffn_fwd(x, g, wg, wu, wd):
    xo, hb, S1, S2, A = _ffn_fwd_k(x, g, wg, wu, wd)
    return xo, (x, hb, S1, S2, A)


def _ffn_bwd_weights(dxo, saved, wd):
    x, hb, S1, S2, A = saved
    dG, dU, dyb = _ffn_bwd_act(dxo, S1, S2, wd)
    return (dG, dU), _ffn_bwd_k2(hb, dyb, A, dG, dU)


def _ffn_bwd_input(dxo, saved, mids, g, wg, wu):
    return _ffn_bwd_dx(mids[0], mids[1], wg, wu, saved[0], g, dxo)


def _mix_fwd(x, P):
    hb, qkv, sin, uv = _mix_proj(x, P["mix_norm"], P["w_in"])
    y_att, lse = _attn_combine([_attn_fwd(qkv, d) for d in DILATIONS])
    y_ssd, hprev = _ssd_fwd(sin, *P["ssd"])
    y_sgu = _sgu_fwd(uv, *P["sgu"])
    ycat = jnp.concatenate([y_att, y_ssd, y_sgu], axis=1).astype(BF)
    return _mm_nn(ycat, P["w_out"], res=x), (x, hb, qkv, sin, uv, y_att, lse, hprev, ycat)


def _mix_bwd_weights(dxo, saved, P):
    x, hb, qkv, sin, uv, y_att, lse, hprev, ycat = saved
    dy_att, dy_ssd, dy_sgu = _mix_bwd_dy(dxo, P["w_out"])
    dwout = _mm_tn(ycat, dxo)
    dqkv = _sum_branches([_attn_bwd(qkv, dy_att, y_att, lse, d) for d in DILATIONS])
    dsin, dcw, dcb, dvec = _ssd_bwd(sin, hprev, dy_ssd, *P["ssd"])
    duv, dsw, dsbias, dln = _sgu_bwd(uv, dy_sgu, *P["sgu"])
    dwin = _unpack_w_in(_mm_tn(hb, dqkv), _mm_tn(hb, dsin), _mm_tn(hb, duv))
    grads = dict(
        w_in=dwin, conv_w=dcw, conv_b=dcb[0], dt_bias=dvec[0, ::HEAD], a_log=dvec[1, ::HEAD],
        d_skip=jnp.sum(dvec[2].reshape(6, HEAD), axis=-1), ssd_norm=dvec[3], sgu_ln_g=dln[0], sgu_ln_b=dln[1],
        sgu_w=dsw, sgu_b=jnp.sum(dsbias.reshape(CHUNK, 4, HEAD), axis=-1).T, w_out=dwout)
    return (dqkv, dsin, duv), grads


def _mix_bwd_input(dxo, saved, mids, P):
    return _mix_bwd_dx(*mids, P["w_in"], saved[0], P["mix_norm"], dxo)


def _halved(g):
    rows = g.size // g.shape[-1]
    return g.reshape(4, 2, rows // 8, g.shape[-1])


def kernel(x, ffn1_norm, ffn1_w_gate, ffn1_w_up, ffn1_w_down, mix_norm, w_in, conv_w, conv_b, dt_bias, a_log, d_skip, ssd_norm, sgu_ln_g, sgu_ln_b, sgu_w, sgu_b, w_out, ffn2_norm, ffn2_w_gate, ffn2_w_up, ffn2_w_down, final_norm, loss_target, m_ffn1_norm, m_ffn1_w_gate, m_ffn1_w_up, m_ffn1_w_down, m_mix_norm, m_w_in, m_conv_w, m_conv_b, m_dt_bias, m_a_log, m_d_skip, m_ssd_norm, m_sgu_ln_g, m_sgu_ln_b, m_sgu_w, m_sgu_b, m_w_out, m_ffn2_norm, m_ffn2_w_gate, m_ffn2_w_up, m_ffn2_w_down, m_final_norm, v_ffn1_norm, v_ffn1_w_gate, v_ffn1_w_up, v_ffn1_w_down, v_mix_norm, v_w_in, v_conv_w, v_conv_b, v_dt_bias, v_a_log, v_d_skip, v_ssd_norm, v_sgu_ln_g, v_sgu_ln_b, v_sgu_w, v_sgu_b, v_w_out, v_ffn2_norm, v_ffn2_w_gate, v_ffn2_w_up, v_ffn2_w_down, v_final_norm):
    given = dict(x=x, ffn1_norm=ffn1_norm, ffn1_w_gate=ffn1_w_gate, ffn1_w_up=ffn1_w_up, ffn1_w_down=ffn1_w_down, mix_norm=mix_norm, w_in=w_in, conv_w=conv_w, conv_b=conv_b, dt_bias=dt_bias, a_log=a_log, d_skip=d_skip, ssd_norm=ssd_norm, sgu_ln_g=sgu_ln_g, sgu_ln_b=sgu_ln_b, sgu_w=sgu_w, sgu_b=sgu_b, w_out=w_out, ffn2_norm=ffn2_norm, ffn2_w_gate=ffn2_w_gate, ffn2_w_up=ffn2_w_up, ffn2_w_down=ffn2_w_down, final_norm=final_norm, loss_target=loss_target, m_ffn1_norm=m_ffn1_norm, m_ffn1_w_gate=m_ffn1_w_gate, m_ffn1_w_up=m_ffn1_w_up, m_ffn1_w_down=m_ffn1_w_down, m_mix_norm=m_mix_norm, m_w_in=m_w_in, m_conv_w=m_conv_w, m_conv_b=m_conv_b, m_dt_bias=m_dt_bias, m_a_log=m_a_log, m_d_skip=m_d_skip, m_ssd_norm=m_ssd_norm, m_sgu_ln_g=m_sgu_ln_g, m_sgu_ln_b=m_sgu_ln_b, m_sgu_w=m_sgu_w, m_sgu_b=m_sgu_b, m_w_out=m_w_out, m_ffn2_norm=m_ffn2_norm, m_ffn2_w_gate=m_ffn2_w_gate, m_ffn2_w_up=m_ffn2_w_up, m_ffn2_w_down=m_ffn2_w_down, m_final_norm=m_final_norm, v_ffn1_norm=v_ffn1_norm, v_ffn1_w_gate=v_ffn1_w_gate, v_ffn1_w_up=v_ffn1_w_up, v_ffn1_w_down=v_ffn1_w_down, v_mix_norm=v_mix_norm, v_w_in=v_w_in, v_conv_w=v_conv_w, v_conv_b=v_conv_b, v_dt_bias=v_dt_bias, v_a_log=v_a_log, v_d_skip=v_d_skip, v_ssd_norm=v_ssd_norm, v_sgu_ln_g=v_sgu_ln_g, v_sgu_ln_b=v_sgu_ln_b, v_sgu_w=v_sgu_w, v_sgu_b=v_sgu_b, v_w_out=v_w_out, v_ffn2_norm=v_ffn2_norm, v_ffn2_w_gate=v_ffn2_w_gate, v_ffn2_w_up=v_ffn2_w_up, v_ffn2_w_down=v_ffn2_w_down, v_final_norm=v_final_norm)
    T = given["x"].shape[0] * given["x"].shape[1]
    D = given["x"].shape[2]
    x0 = given["x"].reshape(T, D)
    tgt = given["loss_target"].reshape(T, D)
    c = lax.axis_index("c")

    bf = {n: given[n].astype(BF) for n in SHARDED if n not in ("w_in", "conv_w")}
    bf["w_in"] = _pack_w_in(given["w_in"]).astype(BF)
    bf["conv_w"] = given["conv_w"]
    first_key = (0, GROUPS[0][0])
    first = [bf[n][0].reshape((2, bf[n].shape[1] // 2) + bf[n].shape[2:]) for n in GROUPS[0][1]]
    gathers = {first_key: _gather_halves_start(first, "l0_" + GROUPS[0][0])}
    later = {(i, gname): [bf[n][i] for n in names] for i in range(DEPTH) for gname, names in GROUPS
             if (i, gname) != first_key}
    zones = {key: _landing_zones(arrs) for key, arrs in later.items()}

    def gathered(i, gname, after):
        if (i, gname) != first_key:
            return gathers[(i, gname)].wait(after)[3:]
        got = gathers[first_key].wait([after] + [z for zs in zones.values() for z in zs])[3:]
        got = _gather_halves_finish(got, "l0_" + gname).wait(after)
        prev = got[0]
        for key, arrs in later.items():
            gathers[key] = _gather_start(arrs, zones[key], f"l{key[0]}_{key[1]}", after=[prev])
            prev = gathers[key].token_array
        return [z.reshape((4, 2 * z.shape[2]) + z.shape[3:]) for z in got]

    def mix_params(i, got):
        win = got[0].reshape(D, W_QKV + W_SSD + W_UV)
        rep = lambda v: jnp.repeat(v, HEAD)[None]
        ssd = (got[1].transpose(1, 0, 2).reshape(4, SSD_CONV_DIM), given["conv_b"][i][None],
               rep(given["dt_bias"][i]), rep(given["a_log"][i]), rep(given["d_skip"][i]), given["ssd_norm"][i][None])
        sgu = (given["sgu_ln_g"][i][None], given["sgu_ln_b"][i][None], given["sgu_w"][i],
               jnp.repeat(given["sgu_b"][i].T, HEAD, axis=1))
        return dict(mix_norm=given["mix_norm"][i][None], w_in=win, w_out=got[2].reshape(-1, D), ssd=ssd, sgu=sgu)

    x = x0
    tape = []
    for i in range(DEPTH):
        got = gathered(i, "ffn1", x)
        token = functools.reduce(lambda a, b: a + b, [g.token for g in gathers.values()]) if i == 0 else 0.0
        P = dict(ffn1=(given["ffn1_norm"][i][None] + token, *got))
        x, s1 = _ffn_fwd(x, *P["ffn1"])
        P.update(mix_params(i, gathered(i, "mix", x)))
        x, s2 = _mix_fwd(x, P)
        P["ffn2"] = (given["ffn2_norm"][i][None], *gathered(i, "ffn2", x))
        x, s3 = _ffn_fwd(x, *P["ffn2"])
        tape.append((P, s1, s2, s3))
    loss_part, dx, dgf = _final_loss(x, given["final_norm"][None], tgt)

    me = 2 * lax.axis_index("x") + lax.axis_index("y")
    jobs = []

    flight = dict(op=None, owners=[], ticks=0)

    def tick(after, begin=None):
        parts, owners = [], []
        if flight["op"] is not None:
            got = flight["op"].wait(after)
            for job, (a0, na) in zip(flight["owners"], flight["op"].spans):
                mine, k = got[a0:a0 + na], len(job["names"])
                if job["stage"] == 1:
                    parts.append(_part_chips(_pair_add(mine[:k], mine[k:], c)))
                elif job["stage"] == 2:
                    parts.append(_part_join(_chip_sum(mine[:k], mine[k:], me, c)))
                else:
                    job.update(stage=4, out=dict(zip(job["names"], mine)))
                    continue
                job["stage"] += 1
                owners.append(job)
        if begin is not None:
            i, gname, gd = begin
            names = [n for n in dict(GROUPS)[gname] if n != "conv_w"]
            jobs.append(dict(key=(i, gname), names=names, stage=1))
            parts.append(_part_sibling([_halved(gd[n]) for n in names]))
            owners.append(jobs[-1])
        flight.update(op=_start_parts(parts, f"rs_tick{flight['ticks']}") if parts else None, owners=owners,
                      ticks=flight["ticks"] + 1)
        return flight["op"].token if parts else 0.0

    grads = [dict() for _ in range(DEPTH)]
    for i in reversed(range(DEPTH)):
        P, s1, s2, s3 = tape[i]
        g = grads[i]
        norm, wg, wu, wd = P["ffn2"]
        mids, (g["ffn2_w_gate"], g["ffn2_w_up"], g["ffn2_w_down"]) = _ffn_bwd_weights(dx, s3, wd)
        tok = tick(g["ffn2_w_down"], (i, "ffn2", g))
        dx, dn2 = _ffn_bwd_input(dx, s3, mids, norm + tok, wg, wu)
        mids, gm = _mix_bwd_weights(dx, s2, P)
        g.update(gm)
        tok = tick(gm["w_in"], (i, "mix", g))
        dx, dnm = _mix_bwd_input(dx, s2, mids, {**P, "mix_norm": P["mix_norm"] + tok})
        norm, wg, wu, wd = P["ffn1"]
        mids, (g["ffn1_w_gate"], g["ffn1_w_up"], g["ffn1_w_down"]) = _ffn_bwd_weights(dx, s1, wd)
        tok = tick(g["ffn1_w_down"], (i, "ffn1", g))
        dx, dn1 = _ffn_bwd_input(dx, s1, mids, norm + tok, wg, wu)
        g["ffn1_norm"], g["mix_norm"], g["ffn2_norm"] = dn1[0], dnm[0], dn2[0]
    grad_x = dx.reshape(given["x"].shape)

    order = [n for n in SMALL if n != "final_norm"] + ["final_norm"]
    small = [jnp.stack([grads[i][n] for i in range(DEPTH)]) for n in order[:-1] + ["conv_w"]]
    small = small[:-1] + [dgf[0], small[-1], loss_part[0, 0:1]]
    n_small = sum(s.size for s in small)
    rows_small = -(-n_small // (128 * 8)) * 8

    def flat(arrs):
        fill = rows_small * 128 - sum(a.size for a in arrs)
        return jnp.concatenate([a.reshape(-1) for a in arrs] + [jnp.zeros((fill,), F32)]).reshape(rows_small, 128)

    gsmall = _all_reduce_small(flat(small)).reshape(-1)

    grad_w = {}
    off = 0
    for n in order:
        size = given[n].size
        grad_w[n] = gsmall[off:off + size].reshape(given[n].shape)
        off += size
    cw = gsmall[off:off + 2 * 4 * SSD_CONV_DIM].reshape(DEPTH, 4, SSD_CONV_DIM)
    grad_w["conv_w"] = lax.dynamic_slice_in_dim(cw, me * (SSD_CONV_DIM // 4), SSD_CONV_DIM // 4, axis=2)
    loss = gsmall[off + 2 * 4 * SSD_CONV_DIM]

    delta, new_m, new_v = {}, {}, {}
    shp = given["conv_w"].shape
    d, m2, v2 = _adamw(*[a.reshape(shp[0] * shp[1], shp[2])
                         for a in (given["conv_w"], grad_w["conv_w"], given["m_conv_w"], given["v_conv_w"])])
    delta["conv_w"], new_m["conv_w"], new_v["conv_w"] = d.reshape(shp), m2.reshape(shp), v2.reshape(shp)
    packed = [flat([given[pre + n] for n in order]) for pre in ("", "m_", "v_")]
    small_out = _adamw(packed[0], gsmall.reshape(rows_small, 128), packed[1], packed[2])
    outs = [o.reshape(-1) for o in small_out]
    off = 0
    for n in order:
        size = given[n].size
        for dst, o in zip((delta, new_m, new_v), outs):
            dst[n] = o[off:off + size].reshape(given[n].shape)
        off += size

    stepped, arrived = {}, {}

    def update_arrived(dep):
        out = None
        for job in jobs:
            if job["stage"] == 4 and not job.get("seen"):
                job["seen"] = True
                for n, full in job["out"].items():
                    view = (lambda a: jnp.swapaxes(a, 1, 2)) if n in TRANSPOSED else (lambda a: a)
                    arrived.setdefault(n, {})[job["key"][0]] = full.reshape(view(given[n]).shape[1:])
                    if len(arrived[n]) == DEPTH:
                        res = _adamw_pair(view(given[n]), arrived[n][0], arrived[n][1], view(given["m_" + n]),
                                          view(given["v_" + n]), dep)
                        stepped[n] = [view(r) for r in res]
                        out = res[0]
        return out

    after = small_out[0]
    while any(j["stage"] < 4 for j in jobs):
        done = update_arrived(jnp.zeros((8, 128), F32) + tok)
        after = after if done is None else done
        tok = tick(after)
    update_arrived(jnp.zeros((8, 128), F32) + tok)
    for n, (d, m2, v2, g) in stepped.items():
        delta[n], new_m[n], new_v[n], grad_w[n] = d, m2, v2, g

    return (loss, grad_x, *[grad_w[n] for n in WEIGHTS], *[delta[n] for n in WEIGHTS],
            *[new_m[n] for n in WEIGHTS], *[new_v[n] for n in WEIGHTS])
```
